```python
import jax, jax.numpy as jnp
from jax import lax
import numpy as np

D_MODEL = 1024
BATCH = 8
SEQ = 4096
DEPTH = 1

CHUNK = 64
MEM_LEN = 256
GLA_HEADS = 4
GLA_DK = D_MODEL // 2
GLA_DV = D_MODEL
GLA_HDK = GLA_DK // GLA_HEADS
GLA_HDV = GLA_DV // GLA_HEADS
GLA_GATE_RANK = 16
GLA_GATE_TEMP = 16.0
POOL_WINDOWS = (2, 4, 8, 16)
POOL_GROUPS = len(POOL_WINDOWS)
POOL_WIDTH = D_MODEL // 2
POOL_GROUP_DIM = POOL_WIDTH // POOL_GROUPS
XA_HEADS = 4
XA_HEAD_DIM = 128
XA_WIDTH = XA_HEADS * XA_HEAD_DIM
N_BRANCH = 3
D_FF = 2816
EPS = 1e-6

IN_SPLITS = (GLA_DK, GLA_DK, GLA_DV, GLA_DV, GLA_GATE_RANK, POOL_WIDTH, XA_WIDTH, N_BRANCH * D_MODEL)
IN_WIDTH = sum(IN_SPLITS)

kernel_name = "hybrid_gla_pool_memxattn_macaron_block"


def rms_norm(x, g):
    xf = x.astype(jnp.float32)
    y = xf * lax.rsqrt(jnp.mean(xf * xf, axis=-1, keepdims=True) + EPS)
    return (y * g.astype(jnp.float32)).astype(x.dtype)


def swiglu(h, w_in, w_out):
    a, b = jnp.split(h @ w_in, 2, axis=-1)
    return (jax.nn.silu(a) * b) @ w_out


def gla_chunked(q, k, v, log_a):
    B, S = q.shape[0], q.shape[1]
    nc = S // CHUNK

    def to_chunks(t):
        return t.reshape(B, nc, CHUNK, GLA_HEADS, t.shape[-1]).transpose(1, 0, 3, 2, 4)

    qc, kc, vc = to_chunks(q), to_chunks(k), to_chunks(v)
    b = jnp.cumsum(to_chunks(log_a.astype(jnp.float32)), axis=3)
    b_end = b[:, :, :, -1:, :]
    kt = (kc.astype(jnp.float32) * jnp.exp(b_end - b)).astype(v.dtype)
    decay = jnp.exp(b_end[:, :, :, 0, :]).astype(v.dtype)

    def step(state, inp):
        q_c, k_c, v_c, d_c = inp
        state = d_c[..., None] * state + jnp.einsum('bhck,bhcv->bhkv', k_c, v_c)
        o = jnp.einsum('bhck,bhkv->bhcv', q_c, state)
        return state, o

    s0 = jnp.zeros((B, GLA_HEADS, GLA_HDK, GLA_HDV), v.dtype)
    _, o = lax.scan(step, s0, (qc, kt, vc, decay))
    return o.transpose(1, 0, 3, 2, 4).reshape(B, S, GLA_DV)


def multiscale_pool(p, w_pool, pool_scale):
    B, S, _ = p.shape
    pg = p.reshape(B, S, POOL_GROUPS, POOL_GROUP_DIM).astype(jnp.float32)
    c0 = jnp.concatenate([jnp.zeros((B, 1, POOL_GROUPS, POOL_GROUP_DIM), jnp.float32),
                          jnp.cumsum(pg, axis=1)], axis=1)
    pos = jnp.arange(1, S + 1, dtype=jnp.float32)
    outs = []
    for g, w in enumerate(POOL_WINDOWS):
        cg = c0[:, :, g]
        lag = jnp.concatenate([jnp.zeros((B, w - 1, POOL_GROUP_DIM), jnp.float32), cg[:, :S + 1 - w]], axis=1)
        cnt = jnp.minimum(pos, float(w))[None, :, None]
        outs.append((cg[:, 1:] - lag) / cnt - pg[:, :, g])
    mixed = jnp.stack(outs, axis=2).astype(p.dtype)
    y = jnp.einsum('bsgc,gcd->bsgd', mixed, w_pool).reshape(B, S, POOL_WIDTH)
    return y * pool_scale


def memory_cross_attention(xq, mem_n, w_mem_kv):
    B, S, _ = xq.shape
    M = mem_n.shape[1]
    q = xq.reshape(B, S, XA_HEADS, XA_HEAD_DIM)
    k, v = jnp.split(mem_n @ w_mem_kv, 2, axis=-1)
    k = k.reshape(B, M, XA_HEADS, XA_HEAD_DIM)
    v = v.reshape(B, M, XA_HEADS, XA_HEAD_DIM)
    s = jnp.einsum('bshd,bmhd->bhsm', q, k).astype(jnp.float32) * (XA_HEAD_DIM ** -0.5)
    pr = jax.nn.softmax(s, axis=-1).astype(v.dtype)
    return jnp.einsum('bhsm,bmhd->bshd', pr, v).reshape(B, S, XA_WIDTH)


def token_mixing(h, mem, w_in, w_fu, b_f, gla_norm_g, w_pool, pool_scale, mem_norm_g, w_mem_kv,
                 w_up_gla, w_up_pool, w_up_xattn, w_o):
    B, S, _ = h.shape
    idx = np.cumsum(IN_SPLITS)[:-1].tolist()
    q, k, v, g_out, f_low, p_in, xq, gates = jnp.split(h @ w_in, idx, axis=-1)
    q = q.reshape(B, S, GLA_HEADS, GLA_HDK) * (GLA_HDK ** -0.5)
    k = k.reshape(B, S, GLA_HEADS, GLA_HDK)
    v = v.reshape(B, S, GLA_HEADS, GLA_HDV)
    f = (f_low @ w_fu + b_f).astype(jnp.float32)
    log_a = (jax.nn.log_sigmoid(f) / GLA_GATE_TEMP).reshape(B, S, GLA_HEADS, GLA_HDK)
    o = gla_chunked(q, k, v, log_a).reshape(B, S, GLA_HEADS, GLA_HDV)
    o = rms_norm(o, gla_norm_g.reshape(GLA_HEADS, GLA_HDV)).reshape(B, S, GLA_DV)
    y_a = (o * jax.nn.silu(g_out)) @ w_up_gla
    y_b = multiscale_pool(p_in, w_pool, pool_scale) @ w_up_pool
    y_c = memory_cross_attention(xq, rms_norm(mem, mem_norm_g), w_mem_kv) @ w_up_xattn
    gt = jax.nn.sigmoid(gates.reshape(B, S, N_BRANCH, D_MODEL))
    merged = gt[:, :, 0] * y_a + gt[:, :, 1] * y_b + gt[:, :, 2] * y_c
    return merged @ w_o


def _fwd_setup_inputs(seed: int = 0) -> dict:
    key = jax.random.key(seed)
    ks = jax.random.split(key, 32)
    L = DEPTH

    def dense(k, shape, fan_in):
        return jax.random.normal(k, shape, jnp.float32) * (fan_in ** -0.5)

    def gain(k, n):
        return 1.0 + 0.02 * jax.random.normal(k, (L, n), jnp.float32)

    return {
        "x": jax.random.normal(ks[0], (BATCH, SEQ, D_MODEL), jnp.float32),
        "mem": jax.random.normal(ks[1], (BATCH, MEM_LEN, D_MODEL), jnp.float32),
        "ffn1_pre_g": gain(ks[2], D_MODEL),
        "ffn1_w_in": dense(ks[3], (L, D_MODEL, 2 * D_FF), D_MODEL),
        "ffn1_w_out": dense(ks[4], (L, D_FF, D_MODEL), D_FF),
        "ffn1_post_g": gain(ks[5], D_MODEL),
        "mix_pre_g": gain(ks[6], D_MODEL),
        "w_in": dense(ks[7], (L, D_MODEL, IN_WIDTH), D_MODEL),
        "w_fu": dense(ks[8], (L, GLA_GATE_RANK, GLA_DK), GLA_GATE_RANK),
        "b_f": 0.1 * jax.random.normal(ks[9], (L, GLA_DK), jnp.float32),
        "gla_norm_g": gain(ks[10], GLA_DV),
        "w_pool": dense(ks[11], (L, POOL_GROUPS, POOL_GROUP_DIM, POOL_GROUP_DIM), POOL_GROUP_DIM),
        "pool_scale": gain(ks[12], POOL_WIDTH),
        "mem_norm_g": gain(ks[13], D_MODEL),
        "w_mem_kv": dense(ks[14], (L, D_MODEL, 2 * XA_WIDTH), D_MODEL),
        "w_up_gla": dense(ks[15], (L, GLA_DV, D_MODEL), GLA_DV),
        "w_up_pool": dense(ks[16], (L, POOL_WIDTH, D_MODEL), POOL_WIDTH),
        "w_up_xattn": dense(ks[17], (L, XA_WIDTH, D_MODEL), XA_WIDTH),
        "w_o": dense(ks[18], (L, D_MODEL, D_MODEL), D_MODEL),
        "mix_post_g": gain(ks[19], D_MODEL),
        "ffn2_pre_g": gain(ks[20], D_MODEL),
        "ffn2_w_in": dense(ks[21], (L, D_MODEL, 2 * D_FF), D_MODEL),
        "ffn2_w_out": dense(ks[22], (L, D_FF, D_MODEL), D_FF),
        "ffn2_post_g": gain(ks[23], D_MODEL),
        "final_g": gain(ks[24], D_MODEL),
    }


def _fwd_reference(x, mem, ffn1_pre_g, ffn1_w_in, ffn1_w_out, ffn1_post_g, mix_pre_g, w_in, w_fu, b_f,
              gla_norm_g, w_pool, pool_scale, mem_norm_g, w_mem_kv, w_up_gla, w_up_pool, w_up_xattn,
              w_o, mix_post_g, ffn2_pre_g, ffn2_w_in, ffn2_w_out, ffn2_post_g, final_g):
    for l in range(DEPTH):
        x = x + 0.5 * rms_norm(swiglu(rms_norm(x, ffn1_pre_g[l]), ffn1_w_in[l], ffn1_w_out[l]), ffn1_post_g[l])
        h = rms_norm(x, mix_pre_g[l])
        y = token_mixing(h, mem, w_in[l], w_fu[l], b_f[l], gla_norm_g[l], w_pool[l], pool_scale[l],
                         mem_norm_g[l], w_mem_kv[l], w_up_gla[l], w_up_pool[l], w_up_xattn[l], w_o[l])
        x = x + rms_norm(y, mix_post_g[l])
        x = x + 0.5 * rms_norm(swiglu(rms_norm(x, ffn2_pre_g[l]), ffn2_w_in[l], ffn2_w_out[l]), ffn2_post_g[l])
        x = rms_norm(x, final_g[l])
    return x


import jax as _jax
import jax.numpy as _jnp

TWIN_FORMAT = 'train_step'
FWD_PARAMS = ['x', 'mem', 'ffn1_pre_g', 'ffn1_w_in', 'ffn1_w_out', 'ffn1_post_g', 'mix_pre_g', 'w_in', 'w_fu', 'b_f', 'gla_norm_g', 'w_pool', 'pool_scale', 'mem_norm_g', 'w_mem_kv', 'w_up_gla', 'w_up_pool', 'w_up_xattn', 'w_o', 'mix_post_g', 'ffn2_pre_g', 'ffn2_w_in', 'ffn2_w_out', 'ffn2_post_g', 'final_g']
TWIN_WEIGHTS = ['ffn1_pre_g', 'ffn1_w_in', 'ffn1_w_out', 'ffn1_post_g', 'mix_pre_g', 'w_in', 'w_fu', 'b_f', 'gla_norm_g', 'w_pool', 'pool_scale', 'mem_norm_g', 'w_mem_kv', 'w_up_gla', 'w_up_pool', 'w_up_xattn', 'w_o', 'mix_post_g', 'ffn2_pre_g', 'ffn2_w_in', 'ffn2_w_out', 'ffn2_post_g', 'final_g']
TWIN_DIFF_INPUT = 'x'
TWIN_INPUTS = ['x', 'mem', 'ffn1_pre_g', 'ffn1_w_in', 'ffn1_w_out', 'ffn1_post_g', 'mix_pre_g', 'w_in', 'w_fu', 'b_f', 'gla_norm_g', 'w_pool', 'pool_scale', 'mem_norm_g', 'w_mem_kv', 'w_up_gla', 'w_up_pool', 'w_up_xattn', 'w_o', 'mix_post_g', 'ffn2_pre_g', 'ffn2_w_in', 'ffn2_w_out', 'ffn2_post_g', 'final_g', 'loss_target', 'm_ffn1_pre_g', 'm_ffn1_w_in', 'm_ffn1_w_out', 'm_ffn1_post_g', 'm_mix_pre_g', 'm_w_in', 'm_w_fu', 'm_b_f', 'm_gla_norm_g', 'm_w_pool', 'm_pool_scale', 'm_mem_norm_g', 'm_w_mem_kv', 'm_w_up_gla', 'm_w_up_pool', 'm_w_up_xattn', 'm_w_o', 'm_mix_post_g', 'm_ffn2_pre_g', 'm_ffn2_w_in', 'm_ffn2_w_out', 'm_ffn2_post_g', 'm_final_g', 'v_ffn1_pre_g', 'v_ffn1_w_in', 'v_ffn1_w_out', 'v_ffn1_post_g', 'v_mix_pre_g', 'v_w_in', 'v_w_fu', 'v_b_f', 'v_gla_norm_g', 'v_w_pool', 'v_pool_scale', 'v_mem_norm_g', 'v_w_mem_kv', 'v_w_up_gla', 'v_w_up_pool', 'v_w_up_xattn', 'v_w_o', 'v_mix_post_g', 'v_ffn2_pre_g', 'v_ffn2_w_in', 'v_ffn2_w_out', 'v_ffn2_post_g', 'v_final_g']
TWIN_OUTPUTS = ['loss', 'grad_x', 'grad_ffn1_pre_g', 'grad_ffn1_w_in', 'grad_ffn1_w_out', 'grad_ffn1_post_g', 'grad_mix_pre_g', 'grad_w_in', 'grad_w_fu', 'grad_b_f', 'grad_gla_norm_g', 'grad_w_pool', 'grad_pool_scale', 'grad_mem_norm_g', 'grad_w_mem_kv', 'grad_w_up_gla', 'grad_w_up_pool', 'grad_w_up_xattn', 'grad_w_o', 'grad_mix_post_g', 'grad_ffn2_pre_g', 'grad_ffn2_w_in', 'grad_ffn2_w_out', 'grad_ffn2_post_g', 'grad_final_g', 'delta_ffn1_pre_g', 'delta_ffn1_w_in', 'delta_ffn1_w_out', 'delta_ffn1_post_g', 'delta_mix_pre_g', 'delta_w_in', 'delta_w_fu', 'delta_b_f', 'delta_gla_norm_g', 'delta_w_pool', 'delta_pool_scale', 'delta_mem_norm_g', 'delta_w_mem_kv', 'delta_w_up_gla', 'delta_w_up_pool', 'delta_w_up_xattn', 'delta_w_o', 'delta_mix_post_g', 'delta_ffn2_pre_g', 'delta_ffn2_w_in', 'delta_ffn2_w_out', 'delta_ffn2_post_g', 'delta_final_g', 'new_m_ffn1_pre_g', 'new_m_ffn1_w_in', 'new_m_ffn1_w_out', 'new_m_ffn1_post_g', 'new_m_mix_pre_g', 'new_m_w_in', 'new_m_w_fu', 'new_m_b_f', 'new_m_gla_norm_g', 'new_m_w_pool', 'new_m_pool_scale', 'new_m_mem_norm_g', 'new_m_w_mem_kv', 'new_m_w_up_gla', 'new_m_w_up_pool', 'new_m_w_up_xattn', 'new_m_w_o', 'new_m_mix_post_g', 'new_m_ffn2_pre_g', 'new_m_ffn2_w_in', 'new_m_ffn2_w_out', 'new_m_ffn2_post_g', 'new_m_final_g', 'new_v_ffn1_pre_g', 'new_v_ffn1_w_in', 'new_v_ffn1_w_out', 'new_v_ffn1_post_g', 'new_v_mix_pre_g', 'new_v_w_in', 'new_v_w_fu', 'new_v_b_f', 'new_v_gla_norm_g', 'new_v_w_pool', 'new_v_pool_scale', 'new_v_mem_norm_g', 'new_v_w_mem_kv', 'new_v_w_up_gla', 'new_v_w_up_pool', 'new_v_w_up_xattn', 'new_v_w_o', 'new_v_mix_post_g', 'new_v_ffn2_pre_g', 'new_v_ffn2_w_in', 'new_v_ffn2_w_out', 'new_v_ffn2_post_g', 'new_v_final_g']
TWIN_LEAF_KINDS = {'loss': 'loss', 'grad_x': 'grad_x', 'grad_ffn1_pre_g': 'grad_w', 'grad_ffn1_w_in': 'grad_w', 'grad_ffn1_w_out': 'grad_w', 'grad_ffn1_post_g': 'grad_w', 'grad_mix_pre_g': 'grad_w', 'grad_w_in': 'grad_w', 'grad_w_fu': 'grad_w', 'grad_b_f': 'grad_w', 'grad_gla_norm_g': 'grad_w', 'grad_w_pool': 'grad_w', 'grad_pool_scale': 'grad_w', 'grad_mem_norm_g': 'grad_w', 'grad_w_mem_kv': 'grad_w', 'grad_w_up_gla': 'grad_w', 'grad_w_up_pool': 'grad_w', 'grad_w_up_xattn': 'grad_w', 'grad_w_o': 'grad_w', 'grad_mix_post_g': 'grad_w', 'grad_ffn2_pre_g': 'grad_w', 'grad_ffn2_w_in': 'grad_w', 'grad_ffn2_w_out': 'grad_w', 'grad_ffn2_post_g': 'grad_w', 'grad_final_g': 'grad_w', 'delta_ffn1_pre_g': 'delta_w', 'delta_ffn1_w_in': 'delta_w', 'delta_ffn1_w_out': 'delta_w', 'delta_ffn1_post_g': 'delta_w', 'delta_mix_pre_g': 'delta_w', 'delta_w_in': 'delta_w', 'delta_w_fu': 'delta_w', 'delta_b_f': 'delta_w', 'delta_gla_norm_g': 'delta_w', 'delta_w_pool': 'delta_w', 'delta_pool_scale': 'delta_w', 'delta_mem_norm_g': 'delta_w', 'delta_w_mem_kv': 'delta_w', 'delta_w_up_gla': 'delta_w', 'delta_w_up_pool': 'delta_w', 'delta_w_up_xattn': 'delta_w', 'delta_w_o': 'delta_w', 'delta_mix_post_g': 'delta_w', 'delta_ffn2_pre_g': 'delta_w', 'delta_ffn2_w_in': 'delta_w', 'delta_ffn2_w_out': 'delta_w', 'delta_ffn2_post_g': 'delta_w', 'delta_final_g': 'delta_w', 'new_m_ffn1_pre_g': 'new_m', 'new_m_ffn1_w_in': 'new_m', 'new_m_ffn1_w_out': 'new_m', 'new_m_ffn1_post_g': 'new_m', 'new_m_mix_pre_g': 'new_m', 'new_m_w_in': 'new_m', 'new_m_w_fu': 'new_m', 'new_m_b_f': 'new_m', 'new_m_gla_norm_g': 'new_m', 'new_m_w_pool': 'new_m', 'new_m_pool_scale': 'new_m', 'new_m_mem_norm_g': 'new_m', 'new_m_w_mem_kv': 'new_m', 'new_m_w_up_gla': 'new_m', 'new_m_w_up_pool': 'new_m', 'new_m_w_up_xattn': 'new_m', 'new_m_w_o': 'new_m', 'new_m_mix_post_g': 'new_m', 'new_m_ffn2_pre_g': 'new_m', 'new_m_ffn2_w_in': 'new_m', 'new_m_ffn2_w_out': 'new_m', 'new_m_ffn2_post_g': 'new_m', 'new_m_final_g': 'new_m', 'new_v_ffn1_pre_g': 'new_v', 'new_v_ffn1_w_in': 'new_v', 'new_v_ffn1_w_out': 'new_v', 'new_v_ffn1_post_g': 'new_v', 'new_v_mix_pre_g': 'new_v', 'new_v_w_in': 'new_v', 'new_v_w_fu': 'new_v', 'new_v_b_f': 'new_v', 'new_v_gla_norm_g': 'new_v', 'new_v_w_pool': 'new_v', 'new_v_pool_scale': 'new_v', 'new_v_mem_norm_g': 'new_v', 'new_v_w_mem_kv': 'new_v', 'new_v_w_up_gla': 'new_v', 'new_v_w_up_pool': 'new_v', 'new_v_w_up_xattn': 'new_v', 'new_v_w_o': 'new_v', 'new_v_mix_post_g': 'new_v', 'new_v_ffn2_pre_g': 'new_v', 'new_v_ffn2_w_in': 'new_v', 'new_v_ffn2_w_out': 'new_v', 'new_v_ffn2_post_g': 'new_v', 'new_v_final_g': 'new_v'}


def _forward(args):
    return _fwd_reference(*[args[k] for k in FWD_PARAMS])


def _output_shape():
    out = _jax.eval_shape(lambda: _forward(_fwd_setup_inputs(0)))
    return out.shape, out.dtype

N_MICROBATCH = 1
ADAM_LR = 0.001
ADAM_B1 = 0.9
ADAM_B2 = 0.999
ADAM_EPS = 1e-08
ADAM_WD = 0.01
ADAM_STEP = 10
PER_EXAMPLE_BATCH_AXIS = {'x': 0, 'mem': 0, 'loss_target': 0}
SHARED_INPUTS = []
_WEIGHT_DTYPES = {'ffn1_pre_g': _jnp.float32, 'ffn1_w_in': _jnp.float32, 'ffn1_w_out': _jnp.float32, 'ffn1_post_g': _jnp.float32, 'mix_pre_g': _jnp.float32, 'w_in': _jnp.float32, 'w_fu': _jnp.float32, 'b_f': _jnp.float32, 'gla_norm_g': _jnp.float32, 'w_pool': _jnp.float32, 'pool_scale': _jnp.float32, 'mem_norm_g': _jnp.float32, 'w_mem_kv': _jnp.float32, 'w_up_gla': _jnp.float32, 'w_up_pool': _jnp.float32, 'w_up_xattn': _jnp.float32, 'w_o': _jnp.float32, 'mix_post_g': _jnp.float32, 'ffn2_pre_g': _jnp.float32, 'ffn2_w_in': _jnp.float32, 'ffn2_w_out': _jnp.float32, 'ffn2_post_g': _jnp.float32, 'final_g': _jnp.float32}
MOMENT_SCALE = {'ffn1_pre_g': 1.491872e-01, 'ffn1_w_in': 6.289169e-02, 'ffn1_w_out': 1.028251e-01, 'ffn1_post_g': 1.246592e-01, 'mix_pre_g': 1.970347e-01, 'w_in': 6.927675e-02, 'w_fu': 1.153636e-02, 'b_f': 5.223051e-02, 'gla_norm_g': 7.014906e-02, 'w_pool': 1.458527e-01, 'pool_scale': 1.460683e-01, 'mem_norm_g': 1.847256e-02, 'w_mem_kv': 1.751976e-02, 'w_up_gla': 7.028155e-02, 'w_up_pool': 1.033809e-01, 'w_up_xattn': 1.260715e-02, 'w_o': 1.247919e-01, 'mix_post_g': 3.479560e-01, 'ffn2_pre_g': 8.203759e-02, 'ffn2_w_in': 3.435606e-02, 'ffn2_w_out': 5.597830e-02, 'ffn2_post_g': 9.724801e-02, 'final_g': 3.202101e+01}


def _to_microbatches(a, axis):
    t = _jnp.moveaxis(a, axis, 0)
    t = t.reshape((N_MICROBATCH, t.shape[0] // N_MICROBATCH) + t.shape[1:])
    return _jnp.moveaxis(t, 1, axis + 1)


def setup_inputs(seed: int = 0) -> dict:
    inp = _fwd_setup_inputs(seed)
    key = _jax.random.fold_in(_jax.random.key(seed), 7919)
    shape, _ = _output_shape()
    out = dict(inp)
    out["loss_target"] = _jax.random.normal(_jax.random.fold_in(key, 0), shape, _jnp.float32)
    for i, name in enumerate(TWIN_WEIGHTS):
        w = inp[name].astype(_jnp.float32)
        if MOMENT_SCALE is None:
            s = _jnp.sqrt(_jnp.mean(_jnp.square(w)) + 1e-30)
        else:
            s = MOMENT_SCALE[name]
        km, kv = _jax.random.split(_jax.random.fold_in(key, i + 1))
        out[name] = w
        out["m_" + name] = s * _jax.random.normal(km, w.shape, _jnp.float32)
        out["v_" + name] = (s * s) * _jax.random.uniform(kv, w.shape, _jnp.float32, 0.5, 1.5)
    if N_MICROBATCH > 1:
        for name, axis in PER_EXAMPLE_BATCH_AXIS.items():
            out[name] = _to_microbatches(out[name], axis)
    return {'x': out['x'], 'mem': out['mem'], 'ffn1_pre_g': out['ffn1_pre_g'], 'ffn1_w_in': out['ffn1_w_in'], 'ffn1_w_out': out['ffn1_w_out'], 'ffn1_post_g': out['ffn1_post_g'], 'mix_pre_g': out['mix_pre_g'], 'w_in': out['w_in'], 'w_fu': out['w_fu'], 'b_f': out['b_f'], 'gla_norm_g': out['gla_norm_g'], 'w_pool': out['w_pool'], 'pool_scale': out['pool_scale'], 'mem_norm_g': out['mem_norm_g'], 'w_mem_kv': out['w_mem_kv'], 'w_up_gla': out['w_up_gla'], 'w_up_pool': out['w_up_pool'], 'w_up_xattn': out['w_up_xattn'], 'w_o': out['w_o'], 'mix_post_g': out['mix_post_g'], 'ffn2_pre_g': out['ffn2_pre_g'], 'ffn2_w_in': out['ffn2_w_in'], 'ffn2_w_out': out['ffn2_w_out'], 'ffn2_post_g': out['ffn2_post_g'], 'final_g': out['final_g'], 'loss_target': out['loss_target'], 'm_ffn1_pre_g': out['m_ffn1_pre_g'], 'm_ffn1_w_in': out['m_ffn1_w_in'], 'm_ffn1_w_out': out['m_ffn1_w_out'], 'm_ffn1_post_g': out['m_ffn1_post_g'], 'm_mix_pre_g': out['m_mix_pre_g'], 'm_w_in': out['m_w_in'], 'm_w_fu': out['m_w_fu'], 'm_b_f': out['m_b_f'], 'm_gla_norm_g': out['m_gla_norm_g'], 'm_w_pool': out['m_w_pool'], 'm_pool_scale': out['m_pool_scale'], 'm_mem_norm_g': out['m_mem_norm_g'], 'm_w_mem_kv': out['m_w_mem_kv'], 'm_w_up_gla': out['m_w_up_gla'], 'm_w_up_pool': out['m_w_up_pool'], 'm_w_up_xattn': out['m_w_up_xattn'], 'm_w_o': out['m_w_o'], 'm_mix_post_g': out['m_mix_post_g'], 'm_ffn2_pre_g': out['m_ffn2_pre_g'], 'm_ffn2_w_in': out['m_ffn2_w_in'], 'm_ffn2_w_out': out['m_ffn2_w_out'], 'm_ffn2_post_g': out['m_ffn2_post_g'], 'm_final_g': out['m_final_g'], 'v_ffn1_pre_g': out['v_ffn1_pre_g'], 'v_ffn1_w_in': out['v_ffn1_w_in'], 'v_ffn1_w_out': out['v_ffn1_w_out'], 'v_ffn1_post_g': out['v_ffn1_post_g'], 'v_mix_pre_g': out['v_mix_pre_g'], 'v_w_in': out['v_w_in'], 'v_w_fu': out['v_w_fu'], 'v_b_f': out['v_b_f'], 'v_gla_norm_g': out['v_gla_norm_g'], 'v_w_pool': out['v_w_pool'], 'v_pool_scale': out['v_pool_scale'], 'v_mem_norm_g': out['v_mem_norm_g'], 'v_w_mem_kv': out['v_w_mem_kv'], 'v_w_up_gla': out['v_w_up_gla'], 'v_w_up_pool': out['v_w_up_pool'], 'v_w_up_xattn': out['v_w_up_xattn'], 'v_w_o': out['v_w_o'], 'v_mix_post_g': out['v_mix_post_g'], 'v_ffn2_pre_g': out['v_ffn2_pre_g'], 'v_ffn2_w_in': out['v_ffn2_w_in'], 'v_ffn2_w_out': out['v_ffn2_w_out'], 'v_ffn2_post_g': out['v_ffn2_post_g'], 'v_final_g': out['v_final_g']}


def _loss(weights, diff, rest, loss_target):
    with _jax.named_scope("forward"):
        args = {**rest, TWIN_DIFF_INPUT: diff, **{k: w.astype(_WEIGHT_DTYPES[k]) for k, w in weights.items()}}
        y = _forward(args)
    with _jax.named_scope("loss_head"):
        err = _jnp.square(y.astype(_jnp.float32) - loss_target)
        return 0.5 * _jnp.sum(_jnp.mean(err, axis=-1)) if err.ndim else 0.5 * err


def _adamw(w, g, m, v):
    m = ADAM_B1 * m + (1.0 - ADAM_B1) * g
    v = ADAM_B2 * v + (1.0 - ADAM_B2) * _jnp.square(g)
    m_hat = m / (1.0 - ADAM_B1 ** ADAM_STEP)
    v_hat = v / (1.0 - ADAM_B2 ** ADAM_STEP)
    delta = -ADAM_LR * (m_hat / (_jnp.sqrt(v_hat) + ADAM_EPS) + ADAM_WD * w)
    return delta, m, v


def reference(x, mem, ffn1_pre_g, ffn1_w_in, ffn1_w_out, ffn1_post_g, mix_pre_g, w_in, w_fu, b_f, gla_norm_g, w_pool, pool_scale, mem_norm_g, w_mem_kv, w_up_gla, w_up_pool, w_up_xattn, w_o, mix_post_g, ffn2_pre_g, ffn2_w_in, ffn2_w_out, ffn2_post_g, final_g, loss_target, m_ffn1_pre_g, m_ffn1_w_in, m_ffn1_w_out, m_ffn1_post_g, m_mix_pre_g, m_w_in, m_w_fu, m_b_f, m_gla_norm_g, m_w_pool, m_pool_scale, m_mem_norm_g, m_w_mem_kv, m_w_up_gla, m_w_up_pool, m_w_up_xattn, m_w_o, m_mix_post_g, m_ffn2_pre_g, m_ffn2_w_in, m_ffn2_w_out, m_ffn2_post_g, m_final_g, v_ffn1_pre_g, v_ffn1_w_in, v_ffn1_w_out, v_ffn1_post_g, v_mix_pre_g, v_w_in, v_w_fu, v_b_f, v_gla_norm_g, v_w_pool, v_pool_scale, v_mem_norm_g, v_w_mem_kv, v_w_up_gla, v_w_up_pool, v_w_up_xattn, v_w_o, v_mix_post_g, v_ffn2_pre_g, v_ffn2_w_in, v_ffn2_w_out, v_ffn2_post_g, v_final_g):
    given = dict(x=x, mem=mem, ffn1_pre_g=ffn1_pre_g, ffn1_w_in=ffn1_w_in, ffn1_w_out=ffn1_w_out, ffn1_post_g=ffn1_post_g, mix_pre_g=mix_pre_g, w_in=w_in, w_fu=w_fu, b_f=b_f, gla_norm_g=gla_norm_g, w_pool=w_pool, pool_scale=pool_scale, mem_norm_g=mem_norm_g, w_mem_kv=w_mem_kv, w_up_gla=w_up_gla, w_up_pool=w_up_pool, w_up_xattn=w_up_xattn, w_o=w_o, mix_post_g=mix_post_g, ffn2_pre_g=ffn2_pre_g, ffn2_w_in=ffn2_w_in, ffn2_w_out=ffn2_w_out, ffn2_post_g=ffn2_post_g, final_g=final_g, loss_target=loss_target, m_ffn1_pre_g=m_ffn1_pre_g, m_ffn1_w_in=m_ffn1_w_in, m_ffn1_w_out=m_ffn1_w_out, m_ffn1_post_g=m_ffn1_post_g, m_mix_pre_g=m_mix_pre_g, m_w_in=m_w_in, m_w_fu=m_w_fu, m_b_f=m_b_f, m_gla_norm_g=m_gla_norm_g, m_w_pool=m_w_pool, m_pool_scale=m_pool_scale, m_mem_norm_g=m_mem_norm_g, m_w_mem_kv=m_w_mem_kv, m_w_up_gla=m_w_up_gla, m_w_up_pool=m_w_up_pool, m_w_up_xattn=m_w_up_xattn, m_w_o=m_w_o, m_mix_post_g=m_mix_post_g, m_ffn2_pre_g=m_ffn2_pre_g, m_ffn2_w_in=m_ffn2_w_in, m_ffn2_w_out=m_ffn2_w_out, m_ffn2_post_g=m_ffn2_post_g, m_final_g=m_final_g, v_ffn1_pre_g=v_ffn1_pre_g, v_ffn1_w_in=v_ffn1_w_in, v_ffn1_w_out=v_ffn1_w_out, v_ffn1_post_g=v_ffn1_post_g, v_mix_pre_g=v_mix_pre_g, v_w_in=v_w_in, v_w_fu=v_w_fu, v_b_f=v_b_f, v_gla_norm_g=v_gla_norm_g, v_w_pool=v_w_pool, v_pool_scale=v_pool_scale, v_mem_norm_g=v_mem_norm_g, v_w_mem_kv=v_w_mem_kv, v_w_up_gla=v_w_up_gla, v_w_up_pool=v_w_up_pool, v_w_up_xattn=v_w_up_xattn, v_w_o=v_w_o, v_mix_post_g=v_mix_post_g, v_ffn2_pre_g=v_ffn2_pre_g, v_ffn2_w_in=v_ffn2_w_in, v_ffn2_w_out=v_ffn2_w_out, v_ffn2_post_g=v_ffn2_post_g, v_final_g=v_final_g)
    weights = {n: given[n] for n in TWIN_WEIGHTS}
    shared = {n: given[n] for n in SHARED_INPUTS}
    per_example = {n: given[n] for n in ['x', 'mem']}
    grad_fn = _jax.value_and_grad(_loss, argnums=(0, 1))

    def one_microbatch(ex, loss_target):
        ex = dict(ex)
        diff = ex.pop(TWIN_DIFF_INPUT)
        return grad_fn(weights, diff, {**shared, **ex}, loss_target)

    if N_MICROBATCH == 1:
        loss, (grad_w, grad_x) = one_microbatch(per_example, given["loss_target"])
    else:
        def body(carry, xs):
            loss_sum, grad_sum = carry
            l_k, (gw_k, gx_k) = one_microbatch(xs[0], xs[1])
            with _jax.named_scope("update"):
                return (loss_sum + l_k, _jax.tree.map(_jnp.add, grad_sum, gw_k)), gx_k

        init = (_jnp.zeros((), _jnp.float32), _jax.tree.map(_jnp.zeros_like, weights))
        (loss, grad_w), grad_x = _jax.lax.scan(body, init, (per_example, given["loss_target"]))
    with _jax.named_scope("update"):
        delta_w, new_m, new_v = {}, {}, {}
        for n in TWIN_WEIGHTS:
            delta_w[n], new_m[n], new_v[n] = _adamw(weights[n], grad_w[n], given["m_" + n], given["v_" + n])
    return (loss, grad_x, *[grad_w[n] for n in TWIN_WEIGHTS], *[delta_w[n] for n in TWIN_WEIGHTS],
            *[new_m[n] for n in TWIN_WEIGHTS], *[new_v[n] for n in TWIN_WEIGHTS])
```

```python
import functools

import jax
import jax.numpy as jnp
from jax import lax
from jax.experimental import pallas as pl
from jax.experimental.pallas import tpu as pltpu

F32 = jnp.float32
BF16 = jnp.bfloat16

N_DEV = 8
D_MODEL = 1024
D_FF = 2816
FF_BLK = 2 * D_FF // N_DEV
N_FF_BLK = D_FF // FF_BLK
CHUNK = 64
GLA_HEADS = 4
GLA_DK = 512
GLA_DV = 1024
GLA_HDK = 128
GLA_HDV = 256
GATE_RANK = 16
GATE_TEMP = 16.0
POOL_WINDOWS = (2, 4, 8, 16)
POOL_W = 512
POOL_GD = 128
POOL_HALO = 16
XA_HEADS = 4
XA_HD = 128
XA_W = 512
EPS = 1e-6
IN_SPLITS = (GLA_DK, GLA_DK, GLA_DV, GLA_DV, GATE_RANK, POOL_W, XA_W, 3 * D_MODEL)
IN_WIDTH = sum(IN_SPLITS)
IN_SHARD = IN_WIDTH // N_DEV
INT_W = 3072 + 3072 + 1024 + 128
INT_NBLK = 3
INT_BLK = INT_W // INT_NBLK
FLOW_W = 128

ADAM_LR = 0.001
ADAM_B1 = 0.9
ADAM_B2 = 0.999
ADAM_EPS = 1e-08
ADAM_WD = 0.01
ADAM_STEP = 10

VMEM_LIMIT = 56 * 1024 * 1024

_NN = (((1,), (0,)), ((), ()))
_NT = (((1,), (1,)), ((), ()))
_TN = (((0,), (0,)), ((), ()))


def _pcall(body, *, name, grid, in_specs, out_specs, out_shape, scratch=()):
    return pl.pallas_call(
        body, name=name, grid=grid, in_specs=in_specs, out_specs=out_specs, out_shape=out_shape,
        scratch_shapes=list(scratch),
        compiler_params=pltpu.CompilerParams(dimension_semantics=("arbitrary",) * len(grid),
                                             vmem_limit_bytes=VMEM_LIMIT))


def _dot(a, b, dims=_NN):
    return lax.dot_general(a.astype(BF16), b.astype(BF16), dims, preferred_element_type=F32)


def _mm(name, a, b, *, grid, a_blk, a_map, b_blk, b_map, o_shape, o_blk, o_map, dims, out_dtype=F32):
    nk = grid[2]

    def body(a_ref, b_ref, o_ref, *scr):
        p = _dot(a_ref[...], b_ref[...], dims)
        if nk == 1:
            o_ref[...] = p.astype(o_ref.dtype)
        else:
            acc = scr[0]
            k = pl.program_id(2)

            @pl.when(k == 0)
            def _():
                acc[...] = p

            @pl.when(k > 0)
            def _():
                acc[...] += p

            @pl.when(k == nk - 1)
            def _():
                o_ref[...] = acc[...].astype(o_ref.dtype)

    acc_shape = tuple(d for d in o_blk if d is not None)
    return _pcall(body, name=name, grid=grid,
                  in_specs=[pl.BlockSpec(a_blk, a_map), pl.BlockSpec(b_blk, b_map)],
                  out_specs=pl.BlockSpec(o_blk, o_map),
                  out_shape=jax.ShapeDtypeStruct(o_shape, out_dtype),
                  scratch=[pltpu.VMEM(acc_shape, F32)] if nk > 1 else [])(a, b)


def _mm_nn(name, a, b, out_dtype=F32, tm=512, tn=None):
    M, K = a.shape
    N = b.shape[1]
    tm, tn = min(tm, M), (tn or N)
    return _mm(name, a, b, grid=(N // tn, M // tm, 1), a_blk=(tm, K), a_map=lambda j, i, k: (i, 0),
               b_blk=(K, tn), b_map=lambda j, i, k: (0, j), o_shape=(M, N), o_blk=(tm, tn),
               o_map=lambda j, i, k: (i, j), dims=_NN, out_dtype=out_dtype)


def _mm_nt(name, a, b, out_dtype=F32, tm=512):
    M, K = a.shape
    N = b.shape[0]
    tm = min(tm, M)
    return _mm(name, a, b, grid=(1, M // tm, 1), a_blk=(tm, K), a_map=lambda j, i, k: (i, 0),
               b_blk=(N, K), b_map=lambda j, i, k: (0, 0), o_shape=(M, N), o_blk=(tm, N),
               o_map=lambda j, i, k: (i, 0), dims=_NT, out_dtype=out_dtype)


def _mm_tn(name, a, b, out_dtype=BF16, ts=512, tn=None):
    S, M = a.shape
    N = b.shape[1]
    ts, tn = min(ts, S), (tn or N)
    return _mm(name, a, b, grid=(N // tn, 1, S // ts), a_blk=(ts, M), a_map=lambda j, i, k: (k, 0),
               b_blk=(ts, tn), b_map=lambda j, i, k: (k, j), o_shape=(M, N), o_blk=(M, tn),
               o_map=lambda j, i, k: (0, j), dims=_TN, out_dtype=out_dtype)


class _Win:
    def __init__(self, arr, w, c):
        self.arr, self.w, self.c = arr, w, c


def _row_spec(x, tm):
    if isinstance(x, _Win):
        return x.arr, pl.BlockSpec((tm, x.w), functools.partial(lambda i, c: (i, c), c=x.c))
    if x.ndim == 3:
        return x, pl.BlockSpec((x.shape[0], tm, x.shape[2]), lambda i: (0, i, 0))
    return x, pl.BlockSpec((tm, x.shape[1]), lambda i: (i, 0))


def _rowwise(name, fn, rows, consts, outs, accs=(), tm=256):
    first = rows[0].arr if isinstance(rows[0], _Win) else rows[0]
    S = first.shape[1] if first.ndim == 3 else first.shape[0]
    tm = min(tm, S)
    n_in, n_out = len(rows) + len(consts), len(outs)
    arrays, in_specs = [], []
    for r in rows:
        arr, spec = _row_spec(r, tm)
        arrays.append(arr)
        in_specs.append(spec)
    for c in consts:
        arrays.append(c)
        in_specs.append(pl.BlockSpec(c.shape, functools.partial(lambda i, n: (0,) * n, n=c.ndim)))
    out_specs = [_row_spec(o, tm)[1] for o in outs]
    out_specs += [pl.BlockSpec(a.shape, functools.partial(lambda i, n: (0,) * n, n=len(a.shape))) for a in accs]

    def body(*refs):
        res = fn(*[r[...] for r in refs[:n_in]])
        if not isinstance(res, (tuple, list)):
            res = (res,)
        for r, v in zip(refs[n_in:n_in + n_out], res[:n_out]):
            r[...] = v.astype(r.dtype)
        i = pl.program_id(0)
        for r, v in zip(refs[n_in + n_out:], res[n_out:]):
            @pl.when(i == 0)
            def _(r=r, v=v):
                r[...] = v.astype(r.dtype)

            @pl.when(i > 0)
            def _(r=r, v=v):
                r[...] += v.astype(r.dtype)

    res = _pcall(body, name=name, grid=(S // tm,), in_specs=in_specs, out_specs=out_specs,
                 out_shape=[jax.ShapeDtypeStruct(o.shape, o.dtype) for o in list(outs) + list(accs)])(*arrays)
    return res


def _sds(shape, dtype=F32):
    return jax.ShapeDtypeStruct(shape, dtype)


def _rms(x, g):
    return x * lax.rsqrt(jnp.mean(x * x, axis=-1, keepdims=True) + EPS) * g


def _rms_bwd(x, g, dy):
    _, vjp = jax.vjp(_rms, x, g)
    return vjp(dy)


def _sigmoid(x):
    return 0.5 * jnp.tanh(0.5 * x) + 0.5


def _silu(x):
    return x * _sigmoid(x)


def _log_sigmoid(f):
    return jnp.minimum(f, 0.0) - jnp.log(1.0 + jnp.exp(-jnp.abs(f)))


def _head_rms_gate(o, g_out, gn):
    parts = [_rms(o[:, h * GLA_HDV:(h + 1) * GLA_HDV], gn[:, h * GLA_HDV:(h + 1) * GLA_HDV]) for h in range(GLA_HEADS)]
    return jnp.concatenate(parts, axis=-1) * _silu(g_out)


def _merge(gates, ya, yb, yc):
    return (_sigmoid(gates[:, :D_MODEL]) * ya + _sigmoid(gates[:, D_MODEL:2 * D_MODEL]) * yb
            + _sigmoid(gates[:, 2 * D_MODEL:]) * yc)


def _tri_dot(t, x):
    hi = x.astype(BF16)
    r1 = x - hi.astype(F32)
    mid = r1.astype(BF16)
    lo = (r1 - mid.astype(F32)).astype(BF16)
    d = functools.partial(lax.dot_general, dimension_numbers=_NN, preferred_element_type=F32)
    return d(t, hi) + d(t, mid) + d(t, lo)


def _ffn_fwd(tag, x, gpre, wg, wout, gpost):
    S = x.shape[0]
    h = _rowwise(tag + "_pre", lambda x, g: _rms(x, g), [x], [gpre], [_sds((S, D_MODEL), BF16)])[0]
    tm = min(512, S)
    u = _mm(tag + "_up", h, wg, grid=(N_DEV, S // tm, 1), a_blk=(tm, D_MODEL), a_map=lambda j, i, k: (i, 0),
            b_blk=(None, D_MODEL, FF_BLK), b_map=lambda j, i, k: (j, 0, 0), o_shape=(N_DEV, S, FF_BLK),
            o_blk=(None, tm, FF_BLK), o_map=lambda j, i, k: (j, i, 0), dims=_NN)
    act = _rowwise(tag + "_act", lambda u: _silu(u[:N_FF_BLK]) * u[N_FF_BLK:], [u], [],
                   [_sds((N_FF_BLK, S, FF_BLK), BF16)], tm=128)[0]
    f = _mm(tag + "_down", act, wout.reshape(N_FF_BLK, FF_BLK, D_MODEL), grid=(1, S // tm, N_FF_BLK),
            a_blk=(None, tm, FF_BLK), a_map=lambda j, i, k: (k, i, 0), b_blk=(None, FF_BLK, D_MODEL),
            b_map=lambda j, i, k: (k, 0, 0), o_shape=(S, D_MODEL), o_blk=(tm, D_MODEL),
            o_map=lambda j, i, k: (i, 0), dims=_NN)
    xo = _rowwise(tag + "_post", lambda x, f, g: x + 0.5 * _rms(f, g), [x, f], [gpost], [_sds((S, D_MODEL))])[0]
    return xo, (h, u, act, f)


def _ffn_bwd(tag, dxo, x, gpre, wg, wout, gpost, saved):
    h, u, act, f = saved
    S = x.shape[0]
    tm = min(512, S)

    def post_bwd(f, dxo, g):
        df, dg = _rms_bwd(f, g, 0.5 * dxo)
        return df, dg

    df, dgpost = _rowwise(tag + "_post_bwd", post_bwd, [f, dxo], [gpost], [_sds((S, D_MODEL), BF16)],
                          [_sds((1, D_MODEL))])
    ts = min(512, S)
    dwout = _mm(tag + "_dwout", act, df, grid=(N_FF_BLK, 1, S // ts), a_blk=(None, ts, FF_BLK),
                a_map=lambda j, i, k: (j, k, 0), b_blk=(ts, D_MODEL), b_map=lambda j, i, k: (k, 0),
                o_shape=(D_FF, D_MODEL), o_blk=(FF_BLK, D_MODEL), o_map=lambda j, i, k: (j, 0), dims=_TN,
                out_dtype=BF16)
    dact = _mm(tag + "_dact", df, wout.reshape(N_FF_BLK, FF_BLK, D_MODEL), grid=(N_FF_BLK, S // tm, 1),
               a_blk=(tm, D_MODEL), a_map=lambda j, i, k: (i, 0), b_blk=(None, FF_BLK, D_MODEL),
               b_map=lambda j, i, k: (j, 0, 0), o_shape=(N_FF_BLK, S, FF_BLK), o_blk=(None, tm, FF_BLK),
               o_map=lambda j, i, k: (j, i, 0), dims=_NT)

    def act_bwd(u, dact):
        a, b = u[:N_FF_BLK], u[N_FF_BLK:]
        sg = _sigmoid(a)
        da = dact * b * (sg * (1.0 + a * (1.0 - sg)))
        db = dact * (a * sg)
        return jnp.concatenate([da, db], axis=0)

    du = _rowwise(tag + "_act_bwd", act_bwd, [u, dact], [], [_sds((N_DEV, S, FF_BLK), BF16)], tm=128)[0]
    dwg = _mm(tag + "_dwin", h, du, grid=(N_DEV, 1, S // ts), a_blk=(ts, D_MODEL), a_map=lambda j, i, k: (k, 0),
              b_blk=(None, ts, FF_BLK), b_map=lambda j, i, k: (j, k, 0), o_shape=(N_DEV, D_MODEL, FF_BLK),
              o_blk=(None, D_MODEL, FF_BLK), o_map=lambda j, i, k: (j, 0, 0), dims=_TN, out_dtype=BF16)
    dh = _mm(tag + "_dh", du, wg, grid=(1, S // tm, N_DEV), a_blk=(None, tm, FF_BLK), a_map=lambda j, i, k: (k, i, 0),
             b_blk=(None, D_MODEL, FF_BLK), b_map=lambda j, i, k: (k, 0, 0), o_shape=(S, D_MODEL),
             o_blk=(tm, D_MODEL), o_map=lambda j, i, k: (i, 0), dims=_NT)

    def pre_bwd(x, dh, dxo, g):
        dx, dg = _rms_bwd(x, g, dh)
        return dx + dxo, dg

    dx, dgpre = _rowwise(tag + "_pre_bwd", pre_bwd, [x, dh, dxo], [gpre], [_sds((S, D_MODEL))], [_sds((1, D_MODEL))])
    return dx, dwg, dwout, dgpre, dgpost


def _gla_chunk_common(k, flow, wfu, bf, tri):
    f = _dot(flow, wfu) + bf
    la = _log_sigmoid(f) * (1.0 / GATE_TEMP)
    b = _tri_dot(tri, la)
    b_end = b[CHUNK - 1:CHUNK, :]
    w = jnp.exp(b_end - b)
    kt = k * w
    d = jnp.exp(b_end)
    return f, w, kt, d


def _tri_matrix(lower):
    r = lax.broadcasted_iota(jnp.int32, (CHUNK, CHUNK), 0)
    c = lax.broadcasted_iota(jnp.int32, (CHUNK, CHUNK), 1)
    return jnp.where((r >= c) if lower else (r <= c), 1.0, 0.0).astype(BF16)


def _gla_fwd(proj, wfu, bf):
    S = proj.shape[0]
    nc = S // CHUNK

    def body(q_ref, k_ref, v_ref, flow_ref, wfu_ref, bf_ref, o_ref, st_ref, state):
        c = pl.program_id(0)

        @pl.when(c == 0)
        def _():
            state[...] = jnp.zeros_like(state)

        _, _, kt, d = _gla_chunk_common(k_ref[...], flow_ref[...], wfu_ref[...], bf_ref[...], _tri_matrix(True))
        q = q_ref[...] * (GLA_HDK ** -0.5)
        v = v_ref[...]
        outs = []
        for h in range(GLA_HEADS):
            ks, vs = slice(h * GLA_HDK, (h + 1) * GLA_HDK), slice(h * GLA_HDV, (h + 1) * GLA_HDV)
            st = state[vs, :] * d[:, ks] + _dot(v[:, vs], kt[:, ks], _TN)
            state[vs, :] = st
            outs.append(_dot(q[:, ks], st, _NT))
        o_ref[...] = jnp.concatenate(outs, axis=-1)
        st_ref[...] = state[...]

    return _pcall(
        body, name="gla_fwd", grid=(nc,),
        in_specs=[pl.BlockSpec((CHUNK, GLA_DK), lambda c: (c, 0)), pl.BlockSpec((CHUNK, GLA_DK), lambda c: (c, 1)),
                  pl.BlockSpec((CHUNK, GLA_DV), lambda c: (c, 1)),
                  pl.BlockSpec((CHUNK, FLOW_W), lambda c: (c, (INT_W - FLOW_W) // FLOW_W)),
                  pl.BlockSpec(wfu.shape, lambda c: (0, 0)), pl.BlockSpec(bf.shape, lambda c: (0, 0))],
        out_specs=[pl.BlockSpec((CHUNK, GLA_DV), lambda c: (c, 0)),
                   pl.BlockSpec((None, GLA_DV, GLA_HDK), lambda c: (c, 0, 0))],
        out_shape=[_sds((S, GLA_DV)), _sds((nc, GLA_DV, GLA_HDK))],
        scratch=[pltpu.VMEM((GLA_DV, GLA_HDK), F32)])(proj, proj, proj, proj, wfu, bf)


def _gla_bwd(proj, wfu, bf, states, do):
    S = proj.shape[0]
    nc = S // CHUNK

    def body(q_ref, k_ref, v_ref, flow_ref, wfu_ref, bf_ref, st_ref, stp_ref, do_ref,
             dq_ref, dk_ref, dv_ref, dflow_ref, dwfu_ref, dbf_ref, dstate):
        step = pl.program_id(0)
        c = nc - 1 - step

        @pl.when(step == 0)
        def _():
            dstate[...] = jnp.zeros_like(dstate)
            dwfu_ref[...] = jnp.zeros_like(dwfu_ref)
            dbf_ref[...] = jnp.zeros_like(dbf_ref)

        k, flow, wfu_v = k_ref[...], flow_ref[...], wfu_ref[...]
        f, w, kt, d = _gla_chunk_common(k, flow, wfu_v, bf_ref[...], _tri_matrix(True))
        q = q_ref[...] * (GLA_HDK ** -0.5)
        v, dout = v_ref[...], do_ref[...]
        has_prev = (c > 0).astype(F32)
        dqs, dkts, dvs, dds = [], [], [], []
        for h in range(GLA_HEADS):
            ks, vs = slice(h * GLA_HDK, (h + 1) * GLA_HDK), slice(h * GLA_HDV, (h + 1) * GLA_HDV)
            st = st_ref[vs, :]
            dqs.append(_dot(dout[:, vs], st))
            dst = dstate[vs, :] + _dot(dout[:, vs], q[:, ks], _TN)
            dkts.append(_dot(v[:, vs], dst))
            dvs.append(_dot(kt[:, ks], dst, _NT))
            dds.append(jnp.sum(dst * stp_ref[vs, :], axis=0, keepdims=True) * has_prev)
            dstate[vs, :] = dst * d[:, ks]
        dq_ref[...] = (jnp.concatenate(dqs, axis=-1) * (GLA_HDK ** -0.5)).astype(dq_ref.dtype)
        dv_ref[...] = jnp.concatenate(dvs, axis=-1).astype(dv_ref.dtype)
        dkt = jnp.concatenate(dkts, axis=-1)
        dd = jnp.concatenate(dds, axis=-1)
        dk_ref[...] = (dkt * w).astype(dk_ref.dtype)
        de = dkt * kt
        db_end = jnp.sum(de, axis=0, keepdims=True) + dd * d
        dla = db_end - _tri_dot(_tri_matrix(False), de)
        df = dla * (1.0 - _sigmoid(f)) * (1.0 / GATE_TEMP)
        dflow_ref[...] = _dot(df, wfu_v, _NT).astype(dflow_ref.dtype)
        dwfu_ref[...] += _dot(flow, df, _TN)
        dbf_ref[...] += jnp.sum(df, axis=0, keepdims=True)

    rc = lambda s: nc - 1 - s
    return _pcall(
        body, name="gla_bwd", grid=(nc,),
        in_specs=[pl.BlockSpec((CHUNK, GLA_DK), lambda s: (rc(s), 0)), pl.BlockSpec((CHUNK, GLA_DK), lambda s: (rc(s), 1)),
                  pl.BlockSpec((CHUNK, GLA_DV), lambda s: (rc(s), 1)),
                  pl.BlockSpec((CHUNK, FLOW_W), lambda s: (rc(s), (INT_W - FLOW_W) // FLOW_W)),
                  pl.BlockSpec(wfu.shape, lambda s: (0, 0)), pl.BlockSpec(bf.shape, lambda s: (0, 0)),
                  pl.BlockSpec((None, GLA_DV, GLA_HDK), lambda s: (rc(s), 0, 0)),
                  pl.BlockSpec((None, GLA_DV, GLA_HDK), lambda s: (jnp.maximum(rc(s) - 1, 0), 0, 0)),
                  pl.BlockSpec((CHUNK, GLA_DV), lambda s: (rc(s), 0))],
        out_specs=[pl.BlockSpec((CHUNK, GLA_DK), lambda s: (rc(s), 0)), pl.BlockSpec((CHUNK, GLA_DK), lambda s: (rc(s), 0)),
                   pl.BlockSpec((CHUNK, GLA_DV), lambda s: (rc(s), 0)), pl.BlockSpec((CHUNK, FLOW_W), lambda s: (rc(s), 0)),
                   pl.BlockSpec(wfu.shape, lambda s: (0, 0)), pl.BlockSpec(bf.shape, lambda s: (0, 0))],
        out_shape=[_sds((S, GLA_DK), BF16), _sds((S, GLA_DK), BF16), _sds((S, GLA_DV), BF16), _sds((S, FLOW_W), BF16),
                   _sds(wfu.shape), _sds(bf.shape)],
        scratch=[pltpu.VMEM((GLA_DV, GLA_HDK), F32)])(proj, proj, proj, proj, wfu, bf, states, states, do)


def _pool_counts(tm, i):
    t = (lax.broadcasted_iota(jnp.int32, (tm, POOL_GD), 0) + i * tm + 1).astype(F32)
    return [jnp.minimum(t, float(w)) for w in POOL_WINDOWS]


def _pool_fwd(proj, w_pool, pool_scale, tm=256):
    S = proj.shape[0]
    tm = min(tm, S // 2)
    col = (3072 + 3072) // POOL_W
    hb = tm // POOL_HALO

    def body(p_ref, halo_ref, wp_ref, sc_ref, mixed_ref, out_ref):
        i = pl.program_id(0)
        p = p_ref[...]
        halo = halo_ref[...] * (i > 0).astype(F32)
        ext = jnp.concatenate([halo, p], axis=0)
        n = tm + POOL_HALO
        sums, acc, k = {}, ext, 1
        while k < POOL_WINDOWS[-1]:
            acc = acc + pltpu.roll(acc, k, axis=0)
            k *= 2
            sums[k] = acc
        cnts = _pool_counts(tm, i)
        mixed, lin = [], []
        for g, w in enumerate(POOL_WINDOWS):
            ls = slice(g * POOL_GD, (g + 1) * POOL_GD)
            m = sums[w][POOL_HALO:n, ls] / cnts[g] - p[:, ls]
            mixed.append(m)
            lin.append(_dot(m, wp_ref[g]))
        mixed_ref[...] = jnp.concatenate(mixed, axis=-1)
        out_ref[...] = (jnp.concatenate(lin, axis=-1) * sc_ref[...]).astype(out_ref.dtype)

    return _pcall(
        body, name="pool_fwd", grid=(S // tm,),
        in_specs=[pl.BlockSpec((tm, POOL_W), lambda i: (i, col)),
                  pl.BlockSpec((POOL_HALO, POOL_W), lambda i: (jnp.maximum(i * hb - 1, 0), col)),
                  pl.BlockSpec(w_pool.shape, lambda i: (0, 0, 0)), pl.BlockSpec(pool_scale.shape, lambda i: (0, 0))],
        out_specs=[pl.BlockSpec((tm, POOL_W), lambda i: (i, 0)), pl.BlockSpec((tm, POOL_W), lambda i: (i, 0))],
        out_shape=[_sds((S, POOL_W)), _sds((S, POOL_W), BF16)])(proj, proj, w_pool, pool_scale)


def _pool_lin_bwd(dout, mixed, w_pool, pool_scale):
    S = dout.shape[0]

    def fn(dout, mixed, wp, sc):
        dlin = dout * sc
        dm, dwp, lin = [], [], []
        for g in range(len(POOL_WINDOWS)):
            ls = slice(g * POOL_GD, (g + 1) * POOL_GD)
            lin.append(_dot(mixed[:, ls], wp[g]))
            dm.append(_dot(dlin[:, ls], wp[g], _NT))
            dwp.append(_dot(mixed[:, ls], dlin[:, ls], _TN))
        dsc = jnp.sum(dout * jnp.concatenate(lin, axis=-1), axis=0, keepdims=True)
        return jnp.concatenate(dm, axis=-1), jnp.concatenate(dwp, axis=0), dsc

    return _rowwise("pool_lin_bwd", fn, [dout, mixed], [w_pool, pool_scale], [_sds((S, POOL_W))],
                    [_sds((len(POOL_WINDOWS) * POOL_GD, POOL_GD)), _sds((1, POOL_W))])


def _pool_win_bwd(dmixed, tm=256):
    S = dmixed.shape[0]
    tm = min(tm, S // 2)
    nt = S // tm
    hb = tm // POOL_HALO

    def body(dm_ref, halo_ref, dp_ref):
        i = pl.program_id(0)
        dm = dm_ref[...]
        halo = halo_ref[...] * (i < nt - 1).astype(F32)
        cnts = _pool_counts(tm, i)
        cnts_h = [c[:POOL_HALO] for c in _pool_counts(tm, i + 1)]
        r = jnp.concatenate([jnp.concatenate([dm[:, g * POOL_GD:(g + 1) * POOL_GD] / cnts[g] for g in range(4)], axis=-1),
                             jnp.concatenate([halo[:, g * POOL_GD:(g + 1) * POOL_GD] / cnts_h[g] for g in range(4)], axis=-1)],
                            axis=0)
        n = tm + POOL_HALO
        sums, acc, k = {}, r, 1
        while k < POOL_WINDOWS[-1]:
            acc = acc + pltpu.roll(acc, n - k, axis=0)
            k *= 2
            sums[k] = acc
        dp = [sums[w][:tm, g * POOL_GD:(g + 1) * POOL_GD] for g, w in enumerate(POOL_WINDOWS)]
        dp_ref[...] = (jnp.concatenate(dp, axis=-1) - dm).astype(dp_ref.dtype)

    return _pcall(
        body, name="pool_win_bwd", grid=(nt,),
        in_specs=[pl.BlockSpec((tm, POOL_W), lambda i: (i, 0)),
                  pl.BlockSpec((POOL_HALO, POOL_W), lambda i: (jnp.minimum((i + 1) * hb, S // POOL_HALO - 1), 0))],
        out_specs=pl.BlockSpec((tm, POOL_W), lambda i: (i, 0)),
        out_shape=_sds((S, POOL_W), BF16))(dmixed, dmixed)


def _xattn_probs(q, kv, h):
    hs = slice(h * XA_HD, (h + 1) * XA_HD)
    s = _dot(q[:, hs], kv[:, hs], _NT) * (XA_HD ** -0.5)
    s = s - jnp.max(s, axis=-1, keepdims=True)
    e = jnp.exp(s)
    return e / jnp.sum(e, axis=-1, keepdims=True)


def _xattn_fwd(proj, kv):
    S = proj.shape[0]

    def fn(q, kv):
        outs = []
        for h in range(XA_HEADS):
            p = _xattn_probs(q, kv, h)
            outs.append(_dot(p, kv[:, XA_W + h * XA_HD:XA_W + (h + 1) * XA_HD]))
        return jnp.concatenate(outs, axis=-1)

    return _rowwise("xattn_fwd", fn, [_Win(proj, XA_W, (3072 + 3072 + POOL_W) // XA_W)], [kv], [_sds((S, XA_W), BF16)])[0]


def _xattn_bwd(proj, kv, dxa):
    S = proj.shape[0]

    def fn(q, dxa, kv):
        dqs, dks, dvs = [], [], []
        for h in range(XA_HEADS):
            hs = slice(h * XA_HD, (h + 1) * XA_HD)
            vh = kv[:, XA_W + h * XA_HD:XA_W + (h + 1) * XA_HD]
            p = _xattn_probs(q, kv, h)
            dp = _dot(dxa[:, hs], vh, _NT)
            ds = p * (dp - jnp.sum(p * dp, axis=-1, keepdims=True)) * (XA_HD ** -0.5)
            dqs.append(_dot(ds, kv[:, hs]))
            dks.append(_dot(ds, q[:, hs], _TN))
            dvs.append(_dot(p, dxa[:, hs], _TN))
        return jnp.concatenate(dqs, axis=-1), jnp.concatenate(dks + dvs, axis=-1)

    return _rowwise("xattn_bwd", fn, [_Win(proj, XA_W, (3072 + 3072 + POOL_W) // XA_W), dxa], [kv],
                    [_sds((S, XA_W), BF16)], [_sds(kv.shape)])


def _local_step(x, mem, tgt, W):
    S = x.shape[0]
    M = mem.shape[0]
    x1, sv1 = _ffn_fwd("ffn1", x, W["ffn1_pre_g"], W["ffn1_w_in"], W["ffn1_w_out"], W["ffn1_post_g"])

    h = _rowwise("mix_pre", lambda x, g: _rms(x, g), [x1], [W["mix_pre_g"]], [_sds((S, D_MODEL), BF16)])[0]
    proj = _mm_nn("mix_proj", h, W["w_int"], tn=INT_BLK)
    o_raw, states = _gla_fwd(proj, W["w_fu_pad"], W["b_f"])
    ya_in = _rowwise("gla_out", _head_rms_gate, [o_raw, _Win(proj, GLA_DV, 2)], [W["gla_norm_g"]],
                     [_sds((S, GLA_DV), BF16)])[0]
    mixed, pool_out = _pool_fwd(proj, W["w_pool"], W["pool_scale"])
    mem_n = _rowwise("mem_norm", lambda m, g: _rms(m, g), [mem], [W["mem_norm_g"]], [_sds((M, D_MODEL), BF16)])[0]
    kv = _mm_nn("mem_kv", mem_n, W["w_mem_kv"])
    xa = _xattn_fwd(proj, kv)
    ya = _mm_nn("up_gla", ya_in, W["w_up_gla"])
    yb = _mm_nn("up_pool", pool_out, W["w_up_pool"])
    yc = _mm_nn("up_xattn", xa, W["w_up_xattn"])
    merged = _rowwise("merge", _merge, [_Win(proj, 3 * D_MODEL, 1), ya, yb, yc], [], [_sds((S, D_MODEL), BF16)])[0]
    y = _mm_nn("mix_out", merged, W["w_o"])
    x2 = _rowwise("mix_post", lambda x, y, g: x + _rms(y, g), [x1, y], [W["mix_post_g"]], [_sds((S, D_MODEL))])[0]

    x3, sv2 = _ffn_fwd("ffn2", x2, W["ffn2_pre_g"], W["ffn2_w_in"], W["ffn2_w_out"], W["ffn2_post_g"])

    def head(x3, t, g):
        out, vjp = jax.vjp(_rms, x3, g)
        e = out - t
        loss = 0.5 * jnp.sum(jnp.mean(e * e, axis=-1, keepdims=True), axis=0, keepdims=True)
        dx3, dg = vjp(e * (1.0 / D_MODEL))
        return dx3, jnp.broadcast_to(loss, (1, 128)), dg

    dx3, loss_acc, d_final_g = _rowwise("head", head, [x3, tgt], [W["final_g"]], [_sds((S, D_MODEL))],
                                        [_sds((1, 128)), _sds((1, D_MODEL))])

    dx2, d_w2in, d_w2out, d_g2pre, d_g2post = _ffn_bwd("ffn2", dx3, x2, W["ffn2_pre_g"], W["ffn2_w_in"],
                                                         W["ffn2_w_out"], W["ffn2_post_g"], sv2)

    def mix_post_bwd(y, dx2, g):
        return _rms_bwd(y, g, dx2)

    dy, d_mix_post_g = _rowwise("mix_post_bwd", mix_post_bwd, [y, dx2], [W["mix_post_g"]], [_sds((S, D_MODEL), BF16)],
                                [_sds((1, D_MODEL))])
    d_w_o = _mm_tn("d_w_o", merged, dy)
    dmerged = _mm_nt("d_merged", dy, W["w_o"])

    def merge_bwd(gates, ya, yb, yc, dm):
        _, vjp = jax.vjp(_merge, gates, ya, yb, yc)
        return vjp(dm)

    dgates, dya, dyb, dyc = _rowwise("merge_bwd", merge_bwd, [_Win(proj, 3 * D_MODEL, 1), ya, yb, yc, dmerged], [],
                                     [_sds((S, 3 * D_MODEL), BF16)] + [_sds((S, D_MODEL), BF16)] * 3)
    d_w_up_gla = _mm_tn("d_w_up_gla", ya_in, dya)
    d_w_up_pool = _mm_tn("d_w_up_pool", pool_out, dyb)
    d_w_up_xattn = _mm_tn("d_w_up_xattn", xa, dyc)
    d_ya_in = _mm_nt("d_ya_in", dya, W["w_up_gla"])
    d_pool_out = _mm_nt("d_pool_out", dyb, W["w_up_pool"])
    d_xa = _mm_nt("d_xa", dyc, W["w_up_xattn"])

    def gla_out_bwd(o, g_out, d, gn):
        _, vjp = jax.vjp(_head_rms_gate, o, g_out, gn)
        return vjp(d)

    do_raw, dg_out, d_gla_norm_g = _rowwise("gla_out_bwd", gla_out_bwd, [o_raw, _Win(proj, GLA_DV, 2), d_ya_in],
                                            [W["gla_norm_g"]], [_sds((S, GLA_DV)), _sds((S, GLA_DV), BF16)],
                                            [_sds((1, GLA_DV))])
    dq, dk, dv, dflow, d_wfu_pad, d_b_f = _gla_bwd(proj, W["w_fu_pad"], W["b_f"], states, do_raw)
    dmixed, d_w_pool, d_pool_scale = _pool_lin_bwd(d_pool_out, mixed, W["w_pool"], W["pool_scale"])
    dp_in = _pool_win_bwd(dmixed)
    dxq, dkv = _xattn_bwd(proj, kv, d_xa)
    d_w_mem_kv = _mm_tn("d_w_mem_kv", mem_n, dkv)
    dmem_n = _mm_nt("d_mem_n", dkv, W["w_mem_kv"])
    d_mem_norm_g = _rowwise("mem_norm_bwd", lambda m, d, g: _rms_bwd(m, g, d)[1], [mem, dmem_n], [W["mem_norm_g"]], [],
                            [_sds((1, D_MODEL))])[0]
    dproj = jnp.concatenate([dq, dk, dv, dg_out, dgates, dp_in, dxq, dflow], axis=-1)
    d_w_int = _mm_tn("d_w_int", h, dproj, tn=INT_BLK)
    dh = _mm(
        "d_mix_h", dproj, W["w_int"], grid=(1, S // min(512, S), INT_NBLK), a_blk=(min(512, S), INT_BLK),
        a_map=lambda j, i, k: (i, k), b_blk=(D_MODEL, INT_BLK), b_map=lambda j, i, k: (0, k), o_shape=(S, D_MODEL),
        o_blk=(min(512, S), D_MODEL), o_map=lambda j, i, k: (i, 0), dims=_NT)

    def mix_pre_bwd(x, dh, dx2, g):
        dx, dg = _rms_bwd(x, g, dh)
        return dx + dx2, dg

    dx1, d_mix_pre_g = _rowwise("mix_pre_bwd", mix_pre_bwd, [x1, dh, dx2], [W["mix_pre_g"]], [_sds((S, D_MODEL))],
                                [_sds((1, D_MODEL))])

    dx, d_w1in, d_w1out, d_g1pre, d_g1post = _ffn_bwd("ffn1", dx1, x, W["ffn1_pre_g"], W["ffn1_w_in"],
                                                        W["ffn1_w_out"], W["ffn1_post_g"], sv1)
    grads = dict(
        ffn1_pre_g=d_g1pre, ffn1_w_in=d_w1in, ffn1_w_out=d_w1out, ffn1_post_g=d_g1post, mix_pre_g=d_mix_pre_g,
        w_int=d_w_int, w_fu_pad=d_wfu_pad, b_f=d_b_f, gla_norm_g=d_gla_norm_g, w_pool=d_w_pool,
        pool_scale=d_pool_scale, mem_norm_g=d_mem_norm_g, w_mem_kv=d_w_mem_kv, w_up_gla=d_w_up_gla,
        w_up_pool=d_w_up_pool, w_up_xattn=d_w_up_xattn, w_o=d_w_o, mix_post_g=d_mix_post_g, ffn2_pre_g=d_g2pre,
        ffn2_w_in=d_w2in, ffn2_w_out=d_w2out, ffn2_post_g=d_g2post, final_g=d_final_g)
    return loss_acc[0, 0], dx, grads


def _exchange(name, arrays, scatter):
    n = len(arrays)

    def body(*refs):
        ins, outs = refs[:n], refs[n:2 * n]
        send_sems, recv_sems, local_sems = refs[2 * n:]
        x, y, c = lax.axis_index("x"), lax.axis_index("y"), lax.axis_index("c")
        me = 4 * x + 2 * y + c
        copies = []
        for a in range(n):
            src_local = ins[a].at[me] if scatter else ins[a]
            cp = pltpu.make_async_copy(src_local, outs[a].at[me], local_sems.at[a])
            cp.start()
            copies.append(cp)
        for r in range(1, N_DEV):
            px = 1 - x if r & 4 else x
            py = 1 - y if r & 2 else y
            pc = 1 - c if r & 1 else c
            peer = 4 * px + 2 * py + pc
            for a in range(n):
                cp = pltpu.make_async_remote_copy(
                    src_ref=ins[a].at[peer] if scatter else ins[a], dst_ref=outs[a].at[me],
                    send_sem=send_sems.at[a * (N_DEV - 1) + r - 1], recv_sem=recv_sems.at[a * (N_DEV - 1) + r - 1],
                    device_id=(px, py, pc), device_id_type=pl.DeviceIdType.MESH)
                cp.start()
                copies.append(cp)
        for cp in copies:
            cp.wait()

    any_spec = pl.BlockSpec(memory_space=pl.ANY)
    out_shape = [_sds(a.shape if scatter else (N_DEV,) + a.shape, a.dtype) for a in arrays]
    return pl.pallas_call(
        body, name=name, in_specs=[any_spec] * n, out_specs=[any_spec] * n, out_shape=out_shape,
        scratch_shapes=[pltpu.SemaphoreType.DMA((n * (N_DEV - 1),)), pltpu.SemaphoreType.DMA((n * (N_DEV - 1),)),
                        pltpu.SemaphoreType.DMA((n,))])(*arrays)


def _adam(name, recv, w, m, v):
    shape = w.shape
    R, C = shape
    tr = R
    while N_DEV * tr * C * 4 > 6 * 1024 * 1024 and tr % 32 == 0:
        tr //= 2

    def body(recv_ref, w_ref, m_ref, v_ref, g_ref, d_ref, m2_ref, v2_ref):
        g = recv_ref[0].astype(F32)
        for j in range(1, N_DEV):
            g = g + recv_ref[j].astype(F32)
        w_, m_, v_ = w_ref[...], m_ref[...], v_ref[...]
        m2 = ADAM_B1 * m_ + (1.0 - ADAM_B1) * g
        v2 = ADAM_B2 * v_ + (1.0 - ADAM_B2) * (g * g)
        m_hat = m2 / (1.0 - ADAM_B1 ** ADAM_STEP)
        v_hat = v2 / (1.0 - ADAM_B2 ** ADAM_STEP)
        g_ref[...] = g
        d_ref[...] = -ADAM_LR * (m_hat / (jnp.sqrt(v_hat) + ADAM_EPS) + ADAM_WD * w_)
        m2_ref[...] = m2
        v2_ref[...] = v2

    blk = pl.BlockSpec((tr, C), lambda i: (i, 0))
    return _pcall(body, name=name, grid=(R // tr,),
                  in_specs=[pl.BlockSpec((N_DEV, tr, C), lambda i: (0, i, 0)), blk, blk, blk],
                  out_specs=[blk] * 4, out_shape=[_sds(shape)] * 4)(recv, w, m, v)


_NAMES = ['ffn1_pre_g', 'ffn1_w_in', 'ffn1_w_out', 'ffn1_post_g', 'mix_pre_g', 'w_in', 'w_fu', 'b_f', 'gla_norm_g',
          'w_pool', 'pool_scale', 'mem_norm_g', 'w_mem_kv', 'w_up_gla', 'w_up_pool', 'w_up_xattn', 'w_o', 'mix_post_g',
          'ffn2_pre_g', 'ffn2_w_in', 'ffn2_w_out', 'ffn2_post_g', 'final_g']
_SHARDED = ['ffn1_w_in', 'ffn1_w_out', 'w_in', 'w_fu', 'w_mem_kv', 'w_up_gla', 'w_up_pool', 'w_up_xattn', 'w_o',
            'ffn2_w_in', 'ffn2_w_out']
_COL_SHARDED = ['w_up_pool', 'w_up_xattn']
_ROW_SHARDED = ['ffn1_w_out', 'ffn2_w_out', 'w_mem_kv', 'w_up_gla', 'w_o']


def _cols_to_full(g):
    return jnp.transpose(g, (1, 0, 2)).reshape(g.shape[1], N_DEV * g.shape[2])


def _full_to_cols(f):
    R, C = f.shape
    return jnp.transpose(f.reshape(R, N_DEV, C // N_DEV), (1, 0, 2))


def _to_internal(w_in_full):
    o = 0
    parts = []
    for s in IN_SPLITS:
        parts.append(w_in_full[:, o:o + s])
        o += s
    q, k, v, g_out, f_low, p_in, xq, gates = parts
    f_low = jnp.pad(f_low, ((0, 0), (0, FLOW_W - GATE_RANK)))
    return jnp.concatenate([q, k, v, g_out, gates, p_in, xq, f_low], axis=-1)


def _from_internal(d):
    q, k, v, g_out = d[:, 0:512], d[:, 512:1024], d[:, 1024:2048], d[:, 2048:3072]
    gates, p_in, xq = d[:, 3072:6144], d[:, 6144:6656], d[:, 6656:7168]
    f_low = d[:, 7168:7168 + GATE_RANK]
    return jnp.concatenate([q, k, v, g_out, f_low, p_in, xq, gates], axis=-1)


def _step(x, mem, tgt, P, Mo, Vo):
    shards = [P[n].astype(BF16) for n in _SHARDED]
    gathered = dict(zip(_SHARDED, _exchange("gather_weights", shards, scatter=False)))
    W = {n: P[n] for n in _NAMES if n not in _SHARDED}
    W["w_pool"] = P["w_pool"]
    for n in _ROW_SHARDED:
        g = gathered[n]
        W[n] = g.reshape(N_DEV * g.shape[1], g.shape[2])
    for n in _COL_SHARDED:
        W[n] = _cols_to_full(gathered[n])
    W["ffn1_w_in"], W["ffn2_w_in"] = gathered["ffn1_w_in"], gathered["ffn2_w_in"]
    W["w_int"] = _to_internal(_cols_to_full(gathered["w_in"]))
    W["w_fu_pad"] = jnp.pad(_cols_to_full(gathered["w_fu"]), ((0, FLOW_W - GATE_RANK), (0, 0)))

    loss, dx, G = _local_step(x, mem, tgt, W)

    def chunks(n):
        if n in ("ffn1_w_in", "ffn2_w_in"):
            return G[n]
        if n == "w_in":
            return _full_to_cols(_from_internal(G["w_int"]))
        if n == "w_fu":
            return _full_to_cols(G["w_fu_pad"][:GATE_RANK].astype(BF16))
        if n in _COL_SHARDED:
            return _full_to_cols(G[n])
        return G[n].reshape(N_DEV, G[n].shape[0] // N_DEV, G[n].shape[1])

    recv_big = dict(zip(_SHARDED, _exchange("scatter_grads", [chunks(n) for n in _SHARDED], scatter=True)))
    small = [n for n in _NAMES if n not in _SHARDED]
    recv_small = dict(zip(small, _exchange("gather_small_grads", [G[n] for n in small], scatter=False)))

    grads, deltas, new_m, new_v = {}, {}, {}, {}
    for n in _NAMES:
        shp = P[n].shape
        shp2 = shp if len(shp) == 2 else (shp[0] * shp[1], shp[2])
        recv = recv_big[n] if n in _SHARDED else recv_small[n]
        g, d, m2, v2 = _adam("adam_" + n, recv.reshape((N_DEV,) + shp2), P[n].reshape(shp2), Mo[n].reshape(shp2),
                             Vo[n].reshape(shp2))
        grads[n], deltas[n], new_m[n], new_v[n] = (t.reshape((1,) + shp) for t in (g, d, m2, v2))
    return loss, dx, grads, deltas, new_m, new_v


def kernel(x, mem, ffn1_pre_g, ffn1_w_in, ffn1_w_out, ffn1_post_g, mix_pre_g, w_in, w_fu, b_f, gla_norm_g, w_pool, pool_scale, mem_norm_g, w_mem_kv, w_up_gla, w_up_pool, w_up_xattn, w_o, mix_post_g, ffn2_pre_g, ffn2_w_in, ffn2_w_out, ffn2_post_g, final_g, loss_target, m_ffn1_pre_g, m_ffn1_w_in, m_ffn1_w_out, m_ffn1_post_g, m_mix_pre_g, m_w_in, m_w_fu, m_b_f, m_gla_norm_g, m_w_pool, m_pool_scale, m_mem_norm_g, m_w_mem_kv, m_w_up_gla, m_w_up_pool, m_w_up_xattn, m_w_o, m_mix_post_g, m_ffn2_pre_g, m_ffn2_w_in, m_ffn2_w_out, m_ffn2_post_g, m_final_g, v_ffn1_pre_g, v_ffn1_w_in, v_ffn1_w_out, v_ffn1_post_g, v_mix_pre_g, v_w_in, v_w_fu, v_b_f, v_gla_norm_g, v_w_pool, v_pool_scale, v_mem_norm_g, v_w_mem_kv, v_w_up_gla, v_w_up_pool, v_w_up_xattn, v_w_o, v_mix_post_g, v_ffn2_pre_g, v_ffn2_w_in, v_ffn2_w_out, v_ffn2_post_g, v_final_g):
    params = [ffn1_pre_g, ffn1_w_in, ffn1_w_out, ffn1_post_g, mix_pre_g, w_in, w_fu, b_f, gla_norm_g, w_pool, pool_scale, mem_norm_g, w_mem_kv, w_up_gla, w_up_pool, w_up_xattn, w_o, mix_post_g, ffn2_pre_g, ffn2_w_in, ffn2_w_out, ffn2_post_g, final_g]
    moms = [m_ffn1_pre_g, m_ffn1_w_in, m_ffn1_w_out, m_ffn1_post_g, m_mix_pre_g, m_w_in, m_w_fu, m_b_f, m_gla_norm_g, m_w_pool, m_pool_scale, m_mem_norm_g, m_w_mem_kv, m_w_up_gla, m_w_up_pool, m_w_up_xattn, m_w_o, m_mix_post_g, m_ffn2_pre_g, m_ffn2_w_in, m_ffn2_w_out, m_ffn2_post_g, m_final_g]
    vars_ = [v_ffn1_pre_g, v_ffn1_w_in, v_ffn1_w_out, v_ffn1_post_g, v_mix_pre_g, v_w_in, v_w_fu, v_b_f, v_gla_norm_g, v_w_pool, v_pool_scale, v_mem_norm_g, v_w_mem_kv, v_w_up_gla, v_w_up_pool, v_w_up_xattn, v_w_o, v_mix_post_g, v_ffn2_pre_g, v_ffn2_w_in, v_ffn2_w_out, v_ffn2_post_g, v_final_g]
    P = {n: a[0] if a.ndim > 2 else a for n, a in zip(_NAMES, params)}
    Mo = {n: a[0] if a.ndim > 2 else a for n, a in zip(_NAMES, moms)}
    Vo = {n: a[0] if a.ndim > 2 else a for n, a in zip(_NAMES, vars_)}
    loss, dx, grads, deltas, new_m, new_v = _step(x[0], mem[0], loss_target[0], P, Mo, Vo)
    loss = lax.psum(loss, ("x", "y", "c"))
    out = [loss, dx[None]]
    for d in (grads, deltas, new_m, new_v):
        for n in _NAMES:
            out.append(d[n].reshape(params[_NAMES.index(n)].shape))
    return tuple(out)
```

```python
import functools

import jax
import jax.numpy as jnp
from jax import lax
from jax.experimental import pallas as pl
from jax.experimental.pallas import tpu as pltpu

F32 = jnp.float32
BF16 = jnp.bfloat16

N_DEV = 8
D_MODEL = 1024
D_FF = 2816
FF_BLK = 2 * D_FF // N_DEV
N_FF_BLK = D_FF // FF_BLK
CHUNK = 64
GLA_HEADS = 4
GLA_DK = 512
GLA_DV = 1024
GLA_HDK = 128
GLA_HDV = 256
GATE_RANK = 16
GATE_TEMP = 16.0
POOL_WINDOWS = (2, 4, 8, 16)
POOL_W = 512
POOL_GD = 128
POOL_HALO = 16
XA_HEADS = 4
XA_HD = 128
XA_W = 512
EPS = 1e-6
IN_SPLITS = (GLA_DK, GLA_DK, GLA_DV, GLA_DV, GATE_RANK, POOL_W, XA_W, 3 * D_MODEL)
IN_WIDTH = sum(IN_SPLITS)
IN_SHARD = IN_WIDTH // N_DEV
INT_W = 3072 + 3072 + 1024 + 128
INT_NBLK = 3
INT_BLK = INT_W // INT_NBLK
FLOW_W = 128

ADAM_LR = 0.001
ADAM_B1 = 0.9
ADAM_B2 = 0.999
ADAM_EPS = 1e-08
ADAM_WD = 0.01
ADAM_STEP = 10

VMEM_LIMIT = 56 * 1024 * 1024

_NN = (((1,), (0,)), ((), ()))
_NT = (((1,), (1,)), ((), ()))
_TN = (((0,), (0,)), ((), ()))


def _pcall(body, *, name, grid, in_specs, out_specs, out_shape, scratch=()):
    return pl.pallas_call(
        body, name=name, grid=grid, in_specs=in_specs, out_specs=out_specs, out_shape=out_shape,
        scratch_shapes=list(scratch),
        compiler_params=pltpu.CompilerParams(dimension_semantics=("arbitrary",) * len(grid),
                                             vmem_limit_bytes=VMEM_LIMIT))


def _dot(a, b, dims=_NN):
    return lax.dot_general(a.astype(BF16), b.astype(BF16), dims, preferred_element_type=F32)


def _mm(name, a, b, *, grid, a_blk, a_map, b_blk, b_map, o_shape, o_blk, o_map, dims, out_dtype=F32):
    nk = grid[2]

    def body(a_ref, b_ref, o_ref, *scr):
        p = _dot(a_ref[...], b_ref[...], dims)
        if nk == 1:
            o_ref[...] = p.astype(o_ref.dtype)
        else:
            acc = scr[0]
            k = pl.program_id(2)

            @pl.when(k == 0)
            def _():
                acc[...] = p

            @pl.when(k > 0)
            def _():
                acc[...] += p

            @pl.when(k == nk - 1)
            def _():
                o_ref[...] = acc[...].astype(o_ref.dtype)

    acc_shape = tuple(d for d in o_blk if d is not None)
    return _pcall(body, name=name, grid=grid,
                  in_specs=[pl.BlockSpec(a_blk, a_map), pl.BlockSpec(b_blk, b_map)],
                  out_specs=pl.BlockSpec(o_blk, o_map),
                  out_shape=jax.ShapeDtypeStruct(o_shape, out_dtype),
                  scratch=[pltpu.VMEM(acc_shape, F32)] if nk > 1 else [])(a, b)


def _mm_nn(name, a, b, out_dtype=F32, tm=512, tn=None):
    M, K = a.shape
    N = b.shape[1]
    tm, tn = min(tm, M), (tn or N)
    return _mm(name, a, b, grid=(N // tn, M // tm, 1), a_blk=(tm, K), a_map=lambda j, i, k: (i, 0),
               b_blk=(K, tn), b_map=lambda j, i, k: (0, j), o_shape=(M, N), o_blk=(tm, tn),
               o_map=lambda j, i, k: (i, j), dims=_NN, out_dtype=out_dtype)


def _mm_nt(name, a, b, out_dtype=F32, tm=512):
    M, K = a.shape
    N = b.shape[0]
    tm = min(tm, M)
    return _mm(name, a, b, grid=(1, M // tm, 1), a_blk=(tm, K), a_map=lambda j, i, k: (i, 0),
               b_blk=(N, K), b_map=lambda j, i, k: (0, 0), o_shape=(M, N), o_blk=(tm, N),
               o_map=lambda j, i, k: (i, 0), dims=_NT, out_dtype=out_dtype)


def _mm_tn(name, a, b, out_dtype=BF16, ts=512, tn=None):
    S, M = a.shape
    N = b.shape[1]
    ts, tn = min(ts, S), (tn or N)
    return _mm(name, a, b, grid=(N // tn, 1, S // ts), a_blk=(ts, M), a_map=lambda j, i, k: (k, 0),
               b_blk=(ts, tn), b_map=lambda j, i, k: (k, j), o_shape=(M, N), o_blk=(M, tn),
               o_map=lambda j, i, k: (0, j), dims=_TN, out_dtype=out_dtype)


class _Win:
    def __init__(self, arr, w, c):
        self.arr, self.w, self.c = arr, w, c


def _row_spec(x, tm):
    if isinstance(x, _Win):
        return x.arr, pl.BlockSpec((tm, x.w), functools.partial(lambda i, c: (i, c), c=x.c))
    if x.ndim == 3:
        return x, pl.BlockSpec((x.shape[0], tm, x.shape[2]), lambda i: (0, i, 0))
    return x, pl.BlockSpec((tm, x.shape[1]), lambda i: (i, 0))


def _rowwise(name, fn, rows, consts, outs, accs=(), tm=256, deps=()):
    first = rows[0].arr if isinstance(rows[0], _Win) else rows[0]
    S = first.shape[1] if first.ndim == 3 else first.shape[0]
    tm = min(tm, S)
    n_in, n_out = len(rows) + len(consts), len(outs)
    arrays, in_specs = [], []
    for r in rows:
        arr, spec = _row_spec(r, tm)
        arrays.append(arr)
        in_specs.append(spec)
    for c in consts:
        arrays.append(c)
        in_specs.append(pl.BlockSpec(c.shape, functools.partial(lambda i, n: (0,) * n, n=c.ndim)))
    for d in deps:
        arrays.append(d)
        in_specs.append(pl.BlockSpec(memory_space=pl.ANY))
    n_dep = len(deps)
    out_specs = [_row_spec(o, tm)[1] for o in outs]
    out_specs += [pl.BlockSpec(a.shape, functools.partial(lambda i, n: (0,) * n, n=len(a.shape))) for a in accs]

    def body(*refs):
        res = fn(*[r[...] for r in refs[:n_in]])
        if not isinstance(res, (tuple, list)):
            res = (res,)
        orefs = refs[n_in + n_dep:]
        for r, v in zip(orefs[:n_out], res[:n_out]):
            r[...] = v.astype(r.dtype)
        i = pl.program_id(0)
        for r, v in zip(orefs[n_out:], res[n_out:]):
            @pl.when(i == 0)
            def _(r=r, v=v):
                r[...] = v.astype(r.dtype)

            @pl.when(i > 0)
            def _(r=r, v=v):
                r[...] += v.astype(r.dtype)

    res = _pcall(body, name=name, grid=(S // tm,), in_specs=in_specs, out_specs=out_specs,
                 out_shape=[jax.ShapeDtypeStruct(o.shape, o.dtype) for o in list(outs) + list(accs)])(*arrays)
    return res


def _sds(shape, dtype=F32):
    return jax.ShapeDtypeStruct(shape, dtype)


def _rms(x, g):
    return x * lax.rsqrt(jnp.mean(x * x, axis=-1, keepdims=True) + EPS) * g


def _rms_bwd(x, g, dy):
    _, vjp = jax.vjp(_rms, x, g)
    return vjp(dy)


def _sigmoid(x):
    return 0.5 * jnp.tanh(0.5 * x) + 0.5


def _silu(x):
    return x * _sigmoid(x)


def _log_sigmoid(f):
    return jnp.minimum(f, 0.0) - jnp.log(1.0 + jnp.exp(-jnp.abs(f)))


def _head_rms_gate(o, g_out, gn):
    parts = [_rms(o[:, h * GLA_HDV:(h + 1) * GLA_HDV], gn[:, h * GLA_HDV:(h + 1) * GLA_HDV]) for h in range(GLA_HEADS)]
    return jnp.concatenate(parts, axis=-1) * _silu(g_out)


def _merge(gates, ya, yb, yc):
    return (_sigmoid(gates[:, :D_MODEL]) * ya + _sigmoid(gates[:, D_MODEL:2 * D_MODEL]) * yb
            + _sigmoid(gates[:, 2 * D_MODEL:]) * yc)


def _tri_dot(t, x):
    hi = x.astype(BF16)
    r1 = x - hi.astype(F32)
    mid = r1.astype(BF16)
    lo = (r1 - mid.astype(F32)).astype(BF16)
    d = functools.partial(lax.dot_general, dimension_numbers=_NN, preferred_element_type=F32)
    return d(t, hi) + d(t, mid) + d(t, lo)


def _ffn_fwd(tag, x, gpre, wg, wout, gpost):
    S = x.shape[0]
    h = _rowwise(tag + "_pre", lambda x, g: _rms(x, g), [x], [gpre], [_sds((S, D_MODEL), BF16)])[0]
    tm = min(512, S)
    u = _mm(tag + "_up", h, wg, grid=(N_DEV, S // tm, 1), a_blk=(tm, D_MODEL), a_map=lambda j, i, k: (i, 0),
            b_blk=(None, D_MODEL, FF_BLK), b_map=lambda j, i, k: (j, 0, 0), o_shape=(N_DEV, S, FF_BLK),
            o_blk=(None, tm, FF_BLK), o_map=lambda j, i, k: (j, i, 0), dims=_NN)
    act = _rowwise(tag + "_act", lambda u: _silu(u[:N_FF_BLK]) * u[N_FF_BLK:], [u], [],
                   [_sds((N_FF_BLK, S, FF_BLK), BF16)], tm=128)[0]
    f = _mm(tag + "_down", act, wout.reshape(N_FF_BLK, FF_BLK, D_MODEL), grid=(1, S // tm, N_FF_BLK),
            a_blk=(None, tm, FF_BLK), a_map=lambda j, i, k: (k, i, 0), b_blk=(None, FF_BLK, D_MODEL),
            b_map=lambda j, i, k: (k, 0, 0), o_shape=(S, D_MODEL), o_blk=(tm, D_MODEL),
            o_map=lambda j, i, k: (i, 0), dims=_NN)
    xo = _rowwise(tag + "_post", lambda x, f, g: x + 0.5 * _rms(f, g), [x, f], [gpost], [_sds((S, D_MODEL))])[0]
    return xo, (h, u, act, f)


def _ffn_bwd(tag, dxo, x, gpre, wg, wout, gpost, saved, deps=()):
    h, u, act, f = saved
    S = x.shape[0]
    tm = min(512, S)

    def post_bwd(f, dxo, g):
        df, dg = _rms_bwd(f, g, 0.5 * dxo)
        return df, dg

    df, dgpost = _rowwise(tag + "_post_bwd", post_bwd, [f, dxo], [gpost], [_sds((S, D_MODEL), BF16)],
                          [_sds((1, D_MODEL))], deps=deps)
    ts = min(512, S)
    dwout = _mm(tag + "_dwout", act, df, grid=(N_FF_BLK, 1, S // ts), a_blk=(None, ts, FF_BLK),
                a_map=lambda j, i, k: (j, k, 0), b_blk=(ts, D_MODEL), b_map=lambda j, i, k: (k, 0),
                o_shape=(D_FF, D_MODEL), o_blk=(FF_BLK, D_MODEL), o_map=lambda j, i, k: (j, 0), dims=_TN,
                out_dtype=BF16)
    dact = _mm(tag + "_dact", df, wout.reshape(N_FF_BLK, FF_BLK, D_MODEL), grid=(N_FF_BLK, S // tm, 1),
               a_blk=(tm, D_MODEL), a_map=lambda j, i, k: (i, 0), b_blk=(None, FF_BLK, D_MODEL),
               b_map=lambda j, i, k: (j, 0, 0), o_shape=(N_FF_BLK, S, FF_BLK), o_blk=(None, tm, FF_BLK),
               o_map=lambda j, i, k: (j, i, 0), dims=_NT)

    def act_bwd(u, dact):
        a, b = u[:N_FF_BLK], u[N_FF_BLK:]
        sg = _sigmoid(a)
        da = dact * b * (sg * (1.0 + a * (1.0 - sg)))
        db = dact * (a * sg)
        return jnp.concatenate([da, db], axis=0)

    du = _rowwise(tag + "_act_bwd", act_bwd, [u, dact], [], [_sds((N_DEV, S, FF_BLK), BF16)], tm=128)[0]
    dwg = _mm(tag + "_dwin", h, du, grid=(N_DEV, 1, S // ts), a_blk=(ts, D_MODEL), a_map=lambda j, i, k: (k, 0),
              b_blk=(None, ts, FF_BLK), b_map=lambda j, i, k: (j, k, 0), o_shape=(N_DEV, D_MODEL, FF_BLK),
              o_blk=(None, D_MODEL, FF_BLK), o_map=lambda j, i, k: (j, 0, 0), dims=_TN, out_dtype=BF16)
    dh = _mm(tag + "_dh", du, wg, grid=(1, S // tm, N_DEV), a_blk=(None, tm, FF_BLK), a_map=lambda j, i, k: (k, i, 0),
             b_blk=(None, D_MODEL, FF_BLK), b_map=lambda j, i, k: (k, 0, 0), o_shape=(S, D_MODEL),
             o_blk=(tm, D_MODEL), o_map=lambda j, i, k: (i, 0), dims=_NT)

    def pre_bwd(x, dh, dxo, g):
        dx, dg = _rms_bwd(x, g, dh)
        return dx + dxo, dg

    dx, dgpre = _rowwise(tag + "_pre_bwd", pre_bwd, [x, dh, dxo], [gpre], [_sds((S, D_MODEL))], [_sds((1, D_MODEL))])
    return dx, dwg, dwout, dgpre, dgpost


def _gla_chunk_common(k, flow, wfu, bf, tri):
    f = _dot(flow, wfu) + bf
    la = _log_sigmoid(f) * (1.0 / GATE_TEMP)
    b = _tri_dot(tri, la)
    b_end = b[CHUNK - 1:CHUNK, :]
    w = jnp.exp(b_end - b)
    kt = k * w
    d = jnp.exp(b_end)
    return f, w, kt, d


def _tri_matrix(lower):
    r = lax.broadcasted_iota(jnp.int32, (CHUNK, CHUNK), 0)
    c = lax.broadcasted_iota(jnp.int32, (CHUNK, CHUNK), 1)
    return jnp.where((r >= c) if lower else (r <= c), 1.0, 0.0).astype(BF16)


def _gla_fwd(proj, wfu, bf):
    S = proj.shape[0]
    nc = S // CHUNK

    def body(q_ref, k_ref, v_ref, flow_ref, wfu_ref, bf_ref, o_ref, st_ref, state):
        c = pl.program_id(0)

        @pl.when(c == 0)
        def _():
            state[...] = jnp.zeros_like(state)

        _, _, kt, d = _gla_chunk_common(k_ref[...], flow_ref[...], wfu_ref[...], bf_ref[...], _tri_matrix(True))
        q = q_ref[...] * (GLA_HDK ** -0.5)
        v = v_ref[...]
        outs = []
        for h in range(GLA_HEADS):
            ks, vs = slice(h * GLA_HDK, (h + 1) * GLA_HDK), slice(h * GLA_HDV, (h + 1) * GLA_HDV)
            st = state[vs, :] * d[:, ks] + _dot(v[:, vs], kt[:, ks], _TN)
            state[vs, :] = st
            outs.append(_dot(q[:, ks], st, _NT))
        o_ref[...] = jnp.concatenate(outs, axis=-1)
        st_ref[...] = state[...]

    return _pcall(
        body, name="gla_fwd", grid=(nc,),
        in_specs=[pl.BlockSpec((CHUNK, GLA_DK), lambda c: (c, 0)), pl.BlockSpec((CHUNK, GLA_DK), lambda c: (c, 1)),
                  pl.BlockSpec((CHUNK, GLA_DV), lambda c: (c, 1)),
                  pl.BlockSpec((CHUNK, FLOW_W), lambda c: (c, (INT_W - FLOW_W) // FLOW_W)),
                  pl.BlockSpec(wfu.shape, lambda c: (0, 0)), pl.BlockSpec(bf.shape, lambda c: (0, 0))],
        out_specs=[pl.BlockSpec((CHUNK, GLA_DV), lambda c: (c, 0)),
                   pl.BlockSpec((None, GLA_DV, GLA_HDK), lambda c: (c, 0, 0))],
        out_shape=[_sds((S, GLA_DV)), _sds((nc, GLA_DV, GLA_HDK))],
        scratch=[pltpu.VMEM((GLA_DV, GLA_HDK), F32)])(proj, proj, proj, proj, wfu, bf)


def _gla_bwd(proj, wfu, bf, states, do):
    S = proj.shape[0]
    nc = S // CHUNK

    def body(q_ref, k_ref, v_ref, flow_ref, wfu_ref, bf_ref, st_ref, stp_ref, do_ref,
             dq_ref, dk_ref, dv_ref, dflow_ref, dwfu_ref, dbf_ref, dstate):
        step = pl.program_id(0)
        c = nc - 1 - step

        @pl.when(step == 0)
        def _():
            dstate[...] = jnp.zeros_like(dstate)
            dwfu_ref[...] = jnp.zeros_like(dwfu_ref)
            dbf_ref[...] = jnp.zeros_like(dbf_ref)

        k, flow, wfu_v = k_ref[...], flow_ref[...], wfu_ref[...]
        f, w, kt, d = _gla_chunk_common(k, flow, wfu_v, bf_ref[...], _tri_matrix(True))
        q = q_ref[...] * (GLA_HDK ** -0.5)
        v, dout = v_ref[...], do_ref[...]
        has_prev = (c > 0).astype(F32)
        dqs, dkts, dvs, dds = [], [], [], []
        for h in range(GLA_HEADS):
            ks, vs = slice(h * GLA_HDK, (h + 1) * GLA_HDK), slice(h * GLA_HDV, (h + 1) * GLA_HDV)
            st = st_ref[vs, :]
            dqs.append(_dot(dout[:, vs], st))
            dst = dstate[vs, :] + _dot(dout[:, vs], q[:, ks], _TN)
            dkts.append(_dot(v[:, vs], dst))
            dvs.append(_dot(kt[:, ks], dst, _NT))
            dds.append(jnp.sum(dst * stp_ref[vs, :], axis=0, keepdims=True) * has_prev)
            dstate[vs, :] = dst * d[:, ks]
        dq_ref[...] = (jnp.concatenate(dqs, axis=-1) * (GLA_HDK ** -0.5)).astype(dq_ref.dtype)
        dv_ref[...] = jnp.concatenate(dvs, axis=-1).astype(dv_ref.dtype)
        dkt = jnp.concatenate(dkts, axis=-1)
        dd = jnp.concatenate(dds, axis=-1)
        dk_ref[...] = (dkt * w).astype(dk_ref.dtype)
        de = dkt * kt
        db_end = jnp.sum(de, axis=0, keepdims=True) + dd * d
        dla = db_end - _tri_dot(_tri_matrix(False), de)
        df = dla * (1.0 - _sigmoid(f)) * (1.0 / GATE_TEMP)
        dflow_ref[...] = _dot(df, wfu_v, _NT).astype(dflow_ref.dtype)
        dwfu_ref[...] += _dot(flow, df, _TN)
        dbf_ref[...] += jnp.sum(df, axis=0, keepdims=True)

    rc = lambda s: nc - 1 - s
    return _pcall(
        body, name="gla_bwd", grid=(nc,),
        in_specs=[pl.BlockSpec((CHUNK, GLA_DK), lambda s: (rc(s), 0)), pl.BlockSpec((CHUNK, GLA_DK), lambda s: (rc(s), 1)),
                  pl.BlockSpec((CHUNK, GLA_DV), lambda s: (rc(s), 1)),
                  pl.BlockSpec((CHUNK, FLOW_W), lambda s: (rc(s), (INT_W - FLOW_W) // FLOW_W)),
                  pl.BlockSpec(wfu.shape, lambda s: (0, 0)), pl.BlockSpec(bf.shape, lambda s: (0, 0)),
                  pl.BlockSpec((None, GLA_DV, GLA_HDK), lambda s: (rc(s), 0, 0)),
                  pl.BlockSpec((None, GLA_DV, GLA_HDK), lambda s: (jnp.maximum(rc(s) - 1, 0), 0, 0)),
                  pl.BlockSpec((CHUNK, GLA_DV), lambda s: (rc(s), 0))],
        out_specs=[pl.BlockSpec((CHUNK, GLA_DK), lambda s: (rc(s), 0)), pl.BlockSpec((CHUNK, GLA_DK), lambda s: (rc(s), 0)),
                   pl.BlockSpec((CHUNK, GLA_DV), lambda s: (rc(s), 0)), pl.BlockSpec((CHUNK, FLOW_W), lambda s: (rc(s), 0)),
                   pl.BlockSpec(wfu.shape, lambda s: (0, 0)), pl.BlockSpec(bf.shape, lambda s: (0, 0))],
        out_shape=[_sds((S, GLA_DK), BF16), _sds((S, GLA_DK), BF16), _sds((S, GLA_DV), BF16), _sds((S, FLOW_W), BF16),
                   _sds(wfu.shape), _sds(bf.shape)],
        scratch=[pltpu.VMEM((GLA_DV, GLA_HDK), F32)])(proj, proj, proj, proj, wfu, bf, states, states, do)


def _pool_counts(tm, i):
    t = (lax.broadcasted_iota(jnp.int32, (tm, POOL_GD), 0) + i * tm + 1).astype(F32)
    return [jnp.minimum(t, float(w)) for w in POOL_WINDOWS]


def _pool_fwd(proj, w_pool, pool_scale, tm=256):
    S = proj.shape[0]
    tm = min(tm, S // 2)
    col = (3072 + 3072) // POOL_W
    hb = tm // POOL_HALO

    def body(p_ref, halo_ref, wp_ref, sc_ref, mixed_ref, out_ref):
        i = pl.program_id(0)
        p = p_ref[...]
        halo = halo_ref[...] * (i > 0).astype(F32)
        ext = jnp.concatenate([halo, p], axis=0)
        n = tm + POOL_HALO
        sums, acc, k = {}, ext, 1
        while k < POOL_WINDOWS[-1]:
            acc = acc + pltpu.roll(acc, k, axis=0)
            k *= 2
            sums[k] = acc
        cnts = _pool_counts(tm, i)
        mixed, lin = [], []
        for g, w in enumerate(POOL_WINDOWS):
            ls = slice(g * POOL_GD, (g + 1) * POOL_GD)
            m = sums[w][POOL_HALO:n, ls] / cnts[g] - p[:, ls]
            mixed.append(m)
            lin.append(_dot(m, wp_ref[g]))
        mixed_ref[...] = jnp.concatenate(mixed, axis=-1)
        out_ref[...] = (jnp.concatenate(lin, axis=-1) * sc_ref[...]).astype(out_ref.dtype)

    return _pcall(
        body, name="pool_fwd", grid=(S // tm,),
        in_specs=[pl.BlockSpec((tm, POOL_W), lambda i: (i, col)),
                  pl.BlockSpec((POOL_HALO, POOL_W), lambda i: (jnp.maximum(i * hb - 1, 0), col)),
                  pl.BlockSpec(w_pool.shape, lambda i: (0, 0, 0)), pl.BlockSpec(pool_scale.shape, lambda i: (0, 0))],
        out_specs=[pl.BlockSpec((tm, POOL_W), lambda i: (i, 0)), pl.BlockSpec((tm, POOL_W), lambda i: (i, 0))],
        out_shape=[_sds((S, POOL_W)), _sds((S, POOL_W), BF16)])(proj, proj, w_pool, pool_scale)


def _pool_lin_bwd(dout, mixed, w_pool, pool_scale):
    S = dout.shape[0]

    def fn(dout, mixed, wp, sc):
        dlin = dout * sc
        dm, dwp, lin = [], [], []
        for g in range(len(POOL_WINDOWS)):
            ls = slice(g * POOL_GD, (g + 1) * POOL_GD)
            lin.append(_dot(mixed[:, ls], wp[g]))
            dm.append(_dot(dlin[:, ls], wp[g], _NT))
            dwp.append(_dot(mixed[:, ls], dlin[:, ls], _TN))
        dsc = jnp.sum(dout * jnp.concatenate(lin, axis=-1), axis=0, keepdims=True)
        return jnp.concatenate(dm, axis=-1), jnp.concatenate(dwp, axis=0), dsc

    return _rowwise("pool_lin_bwd", fn, [dout, mixed], [w_pool, pool_scale], [_sds((S, POOL_W))],
                    [_sds((len(POOL_WINDOWS) * POOL_GD, POOL_GD)), _sds((1, POOL_W))])


def _pool_win_bwd(dmixed, tm=256):
    S = dmixed.shape[0]
    tm = min(tm, S // 2)
    nt = S // tm
    hb = tm // POOL_HALO

    def body(dm_ref, halo_ref, dp_ref):
        i = pl.program_id(0)
        dm = dm_ref[...]
        halo = halo_ref[...] * (i < nt - 1).astype(F32)
        cnts = _pool_counts(tm, i)
        cnts_h = [c[:POOL_HALO] for c in _pool_counts(tm, i + 1)]
        r = jnp.concatenate([jnp.concatenate([dm[:, g * POOL_GD:(g + 1) * POOL_GD] / cnts[g] for g in range(4)], axis=-1),
                             jnp.concatenate([halo[:, g * POOL_GD:(g + 1) * POOL_GD] / cnts_h[g] for g in range(4)], axis=-1)],
                            axis=0)
        n = tm + POOL_HALO
        sums, acc, k = {}, r, 1
        while k < POOL_WINDOWS[-1]:
            acc = acc + pltpu.roll(acc, n - k, axis=0)
            k *= 2
            sums[k] = acc
        dp = [sums[w][:tm, g * POOL_GD:(g + 1) * POOL_GD] for g, w in enumerate(POOL_WINDOWS)]
        dp_ref[...] = (jnp.concatenate(dp, axis=-1) - dm).astype(dp_ref.dtype)

    return _pcall(
        body, name="pool_win_bwd", grid=(nt,),
        in_specs=[pl.BlockSpec((tm, POOL_W), lambda i: (i, 0)),
                  pl.BlockSpec((POOL_HALO, POOL_W), lambda i: (jnp.minimum((i + 1) * hb, S // POOL_HALO - 1), 0))],
        out_specs=pl.BlockSpec((tm, POOL_W), lambda i: (i, 0)),
        out_shape=_sds((S, POOL_W), BF16))(dmixed, dmixed)


def _xattn_probs(q, kv, h):
    hs = slice(h * XA_HD, (h + 1) * XA_HD)
    s = _dot(q[:, hs], kv[:, hs], _NT) * (XA_HD ** -0.5)
    s = s - jnp.max(s, axis=-1, keepdims=True)
    e = jnp.exp(s)
    return e / jnp.sum(e, axis=-1, keepdims=True)


def _xattn_fwd(proj, kv):
    S = proj.shape[0]

    def fn(q, kv):
        outs = []
        for h in range(XA_HEADS):
            p = _xattn_probs(q, kv, h)
            outs.append(_dot(p, kv[:, XA_W + h * XA_HD:XA_W + (h + 1) * XA_HD]))
        return jnp.concatenate(outs, axis=-1)

    return _rowwise("xattn_fwd", fn, [_Win(proj, XA_W, (3072 + 3072 + POOL_W) // XA_W)], [kv], [_sds((S, XA_W), BF16)])[0]


def _xattn_bwd(proj, kv, dxa):
    S = proj.shape[0]

    def fn(q, dxa, kv):
        dqs, dks, dvs = [], [], []
        for h in range(XA_HEADS):
            hs = slice(h * XA_HD, (h + 1) * XA_HD)
            vh = kv[:, XA_W + h * XA_HD:XA_W + (h + 1) * XA_HD]
            p = _xattn_probs(q, kv, h)
            dp = _dot(dxa[:, hs], vh, _NT)
            ds = p * (dp - jnp.sum(p * dp, axis=-1, keepdims=True)) * (XA_HD ** -0.5)
            dqs.append(_dot(ds, kv[:, hs]))
            dks.append(_dot(ds, q[:, hs], _TN))
            dvs.append(_dot(p, dxa[:, hs], _TN))
        return jnp.concatenate(dqs, axis=-1), jnp.concatenate(dks + dvs, axis=-1)

    return _rowwise("xattn_bwd", fn, [_Win(proj, XA_W, (3072 + 3072 + POOL_W) // XA_W), dxa], [kv],
                    [_sds((S, XA_W), BF16)], [_sds(kv.shape)])


def _mix_fwd(x1, mem, W):
    S = x1.shape[0]
    M = mem.shape[0]
    h = _rowwise("mix_pre", lambda x, g: _rms(x, g), [x1], [W["mix_pre_g"]], [_sds((S, D_MODEL), BF16)])[0]
    proj = _mm_nn("mix_proj", h, W["w_int"], tn=INT_BLK)
    o_raw, states = _gla_fwd(proj, W["w_fu_pad"], W["b_f"])
    ya_in = _rowwise("gla_out", _head_rms_gate, [o_raw, _Win(proj, GLA_DV, 2)], [W["gla_norm_g"]],
                     [_sds((S, GLA_DV), BF16)])[0]
    mixed, pool_out = _pool_fwd(proj, W["w_pool"], W["pool_scale"])
    mem_n = _rowwise("mem_norm", lambda m, g: _rms(m, g), [mem], [W["mem_norm_g"]], [_sds((M, D_MODEL), BF16)])[0]
    kv = _mm_nn("mem_kv", mem_n, W["w_mem_kv"])
    xa = _xattn_fwd(proj, kv)
    ya = _mm_nn("up_gla", ya_in, W["w_up_gla"])
    yb = _mm_nn("up_pool", pool_out, W["w_up_pool"])
    yc = _mm_nn("up_xattn", xa, W["w_up_xattn"])
    merged = _rowwise("merge", _merge, [_Win(proj, 3 * D_MODEL, 1), ya, yb, yc], [], [_sds((S, D_MODEL), BF16)])[0]
    y = _mm_nn("mix_out", merged, W["w_o"])
    x2 = _rowwise("mix_post", lambda x, y, g: x + _rms(y, g), [x1, y], [W["mix_post_g"]], [_sds((S, D_MODEL))])[0]
    return x2, (h, proj, o_raw, states, ya_in, mixed, pool_out, mem_n, kv, xa, ya, yb, yc, merged, y)


def _head(x3, tgt, final_g):
    S = x3.shape[0]

    def head(x3, t, g):
        out, vjp = jax.vjp(_rms, x3, g)
        e = out - t
        loss = 0.5 * jnp.sum(jnp.mean(e * e, axis=-1, keepdims=True), axis=0, keepdims=True)
        dx3, dg = vjp(e * (1.0 / D_MODEL))
        return dx3, jnp.broadcast_to(loss, (1, 128)), dg

    dx3, loss_acc, d_final_g = _rowwise("head", head, [x3, tgt], [final_g], [_sds((S, D_MODEL))],
                                        [_sds((1, 128)), _sds((1, D_MODEL))])
    return loss_acc[0, 0], dx3, d_final_g


def _mix_bwd(dx2, x1, mem, W, saved, deps=()):
    h, proj, o_raw, states, ya_in, mixed, pool_out, mem_n, kv, xa, ya, yb, yc, merged, y = saved
    S = x1.shape[0]

    def mix_post_bwd(y, dx2, g):
        return _rms_bwd(y, g, dx2)

    dy, d_mix_post_g = _rowwise("mix_post_bwd", mix_post_bwd, [y, dx2], [W["mix_post_g"]], [_sds((S, D_MODEL), BF16)],
                                [_sds((1, D_MODEL))], deps=deps)
    d_w_o = _mm_tn("d_w_o", merged, dy)
    dmerged = _mm_nt("d_merged", dy, W["w_o"])

    def merge_bwd(gates, ya, yb, yc, dm):
        _, vjp = jax.vjp(_merge, gates, ya, yb, yc)
        return vjp(dm)

    dgates, dya, dyb, dyc = _rowwise("merge_bwd", merge_bwd, [_Win(proj, 3 * D_MODEL, 1), ya, yb, yc, dmerged], [],
                                     [_sds((S, 3 * D_MODEL), BF16)] + [_sds((S, D_MODEL), BF16)] * 3)
    d_w_up_gla = _mm_tn("d_w_up_gla", ya_in, dya)
    d_w_up_pool = _mm_tn("d_w_up_pool", pool_out, dyb)
    d_w_up_xattn = _mm_tn("d_w_up_xattn", xa, dyc)
    d_ya_in = _mm_nt("d_ya_in", dya, W["w_up_gla"])
    d_pool_out = _mm_nt("d_pool_out", dyb, W["w_up_pool"])
    d_xa = _mm_nt("d_xa", dyc, W["w_up_xattn"])

    def gla_out_bwd(o, g_out, d, gn):
        _, vjp = jax.vjp(_head_rms_gate, o, g_out, gn)
        return vjp(d)

    do_raw, dg_out, d_gla_norm_g = _rowwise("gla_out_bwd", gla_out_bwd, [o_raw, _Win(proj, GLA_DV, 2), d_ya_in],
                                            [W["gla_norm_g"]], [_sds((S, GLA_DV)), _sds((S, GLA_DV), BF16)],
                                            [_sds((1, GLA_DV))])
    dq, dk, dv, dflow, d_wfu_pad, d_b_f = _gla_bwd(proj, W["w_fu_pad"], W["b_f"], states, do_raw)
    dmixed, d_w_pool, d_pool_scale = _pool_lin_bwd(d_pool_out, mixed, W["w_pool"], W["pool_scale"])
    dp_in = _pool_win_bwd(dmixed)
    dxq, dkv = _xattn_bwd(proj, kv, d_xa)
    d_w_mem_kv = _mm_tn("d_w_mem_kv", mem_n, dkv)
    dmem_n = _mm_nt("d_mem_n", dkv, W["w_mem_kv"])
    d_mem_norm_g = _rowwise("mem_norm_bwd", lambda m, d, g: _rms_bwd(m, g, d)[1], [mem, dmem_n], [W["mem_norm_g"]], [],
                            [_sds((1, D_MODEL))])[0]
    dproj = jnp.concatenate([dq, dk, dv, dg_out, dgates, dp_in, dxq, dflow], axis=-1)
    d_w_int = _mm_tn("d_w_int", h, dproj, tn=INT_BLK)
    dh = _mm(
        "d_mix_h", dproj, W["w_int"], grid=(1, S // min(512, S), INT_NBLK), a_blk=(min(512, S), INT_BLK),
        a_map=lambda j, i, k: (i, k), b_blk=(D_MODEL, INT_BLK), b_map=lambda j, i, k: (0, k), o_shape=(S, D_MODEL),
        o_blk=(min(512, S), D_MODEL), o_map=lambda j, i, k: (i, 0), dims=_NT)

    def mix_pre_bwd(x, dh, dx2, g):
        dx, dg = _rms_bwd(x, g, dh)
        return dx + dx2, dg

    dx1, d_mix_pre_g = _rowwise("mix_pre_bwd", mix_pre_bwd, [x1, dh, dx2], [W["mix_pre_g"]], [_sds((S, D_MODEL))],
                                [_sds((1, D_MODEL))])
    grads = dict(
        mix_pre_g=d_mix_pre_g, w_int=d_w_int, w_fu_pad=d_wfu_pad, b_f=d_b_f, gla_norm_g=d_gla_norm_g, w_pool=d_w_pool,
        pool_scale=d_pool_scale, mem_norm_g=d_mem_norm_g, w_mem_kv=d_w_mem_kv, w_up_gla=d_w_up_gla,
        w_up_pool=d_w_up_pool, w_up_xattn=d_w_up_xattn, w_o=d_w_o, mix_post_g=d_mix_post_g)
    return dx1, grads


def _mesh_pos():
    x, y, c = lax.axis_index("x"), lax.axis_index("y"), lax.axis_index("c")
    return x, y, c, 4 * x + 2 * y + c


def _peer(x, y, c, r):
    px = 1 - x if r & 4 else x
    py = 1 - y if r & 2 else y
    pc = 1 - c if r & 1 else c
    return (px, py, pc), 4 * px + 2 * py + pc


def _peer_copies(srcs, lands, send_sems, recv_sems, scatter):
    x, y, c, me = _mesh_pos()
    cps = []
    for r in range(1, N_DEV):
        pos, peer = _peer(x, y, c, r)
        for a in range(len(srcs)):
            k = a * (N_DEV - 1) + r - 1
            cps.append(pltpu.make_async_remote_copy(
                src_ref=srcs[a].at[peer] if scatter else srcs[a], dst_ref=lands[a].at[me],
                send_sem=send_sems.at[k], recv_sem=recv_sems.at[k], device_id=pos,
                device_id_type=pl.DeviceIdType.MESH))
    return cps


def _exchange(name, arrays, scatter):
    n = len(arrays)

    def body(*refs):
        ins, outs = refs[:n], refs[n:2 * n]
        send_sems, recv_sems, local_sems = refs[2 * n:]
        me = _mesh_pos()[3]
        copies = [pltpu.make_async_copy(ins[a].at[me] if scatter else ins[a], outs[a].at[me], local_sems.at[a])
                  for a in range(n)]
        copies += _peer_copies(ins, outs, send_sems, recv_sems, scatter)
        for cp in copies:
            cp.start()
        for cp in copies:
            cp.wait()

    any_spec = pl.BlockSpec(memory_space=pl.ANY)
    out_shape = [_sds(a.shape if scatter else (N_DEV,) + a.shape, a.dtype) for a in arrays]
    return pl.pallas_call(
        body, name=name, in_specs=[any_spec] * n, out_specs=[any_spec] * n, out_shape=out_shape,
        scratch_shapes=[pltpu.SemaphoreType.DMA((n * (N_DEV - 1),)), pltpu.SemaphoreType.DMA((n * (N_DEV - 1),)),
                        pltpu.SemaphoreType.DMA((n,))])(*arrays)


_HBM = pl.BlockSpec(memory_space=pltpu.HBM)
_SEM = pl.BlockSpec(memory_space=pltpu.SEMAPHORE)
_EFFECT = pltpu.SideEffectType.DATAFLOW_SIDE_EFFECTING


def _place_own(name, arrays, scatter):
    n = len(arrays)

    def body(*refs):
        ins, outs, sems = refs[:n], refs[n:2 * n], refs[2 * n]
        me = _mesh_pos()[3]
        copies = [pltpu.make_async_copy(ins[a].at[me] if scatter else ins[a], outs[a].at[me], sems.at[a])
                  for a in range(n)]
        for cp in copies:
            cp.start()
        for cp in copies:
            cp.wait()

    any_spec = pl.BlockSpec(memory_space=pl.ANY)
    out_shape = [_sds(a.shape if scatter else (N_DEV,) + a.shape, a.dtype) for a in arrays]
    return pl.pallas_call(body, name=name, in_specs=[any_spec] * n, out_specs=[any_spec] * n, out_shape=out_shape,
                          scratch_shapes=[pltpu.SemaphoreType.DMA((n,))])(*arrays)


def _exchange_start(name, arrays, scatter):
    n = len(arrays)
    lands = _place_own(name + "_own", arrays, scatter)
    n_sem = n * (N_DEV - 1)

    def body(*refs):
        srcs, lands_ = refs[:n], refs[n:2 * n]
        send_sems, recv_sems = refs[2 * n], refs[2 * n + 1]
        token = refs[-1]
        for cp in _peer_copies(srcs, lands_, send_sems, recv_sems, scatter):
            cp.start()
        token[...] = jnp.zeros_like(token)

    hbm = lambda a: pltpu.HBM(a.shape, a.dtype)
    res = pl.pallas_call(
        body, name=name,
        out_shape=(pltpu.SemaphoreType.DMA((n_sem,)), pltpu.SemaphoreType.DMA((n_sem,)),
                   *[hbm(a) for a in arrays], *[hbm(a) for a in lands], _sds((8, 128))),
        in_specs=[_HBM] * (2 * n), out_specs=(_SEM, _SEM, *[_HBM] * (2 * n), pl.BlockSpec(memory_space=pltpu.VMEM)),
        input_output_aliases={i: 2 + i for i in range(2 * n)},
        compiler_params=pltpu.CompilerParams(has_side_effects=_EFFECT),
    )(*[pltpu.with_memory_space_constraint(a, pltpu.HBM) for a in list(arrays) + list(lands)])
    return (res[0], res[1], res[2:2 + n], res[2 + n:2 + 2 * n], scatter), res[-1]


def _exchange_wait(name, state, after):
    send_sems, recv_sems, srcs, lands, scatter = state
    n = len(srcs)

    def body(*refs):
        srcs_, lands_ = refs[:n], refs[n:2 * n]
        for cp in _peer_copies(srcs_, lands_, refs[2 * n], refs[2 * n + 1], scatter):
            cp.wait_send()
            cp.wait_recv()

    hbm = lambda a: pltpu.HBM(a.shape, a.dtype)
    res = pl.pallas_call(
        body, name=name, out_shape=tuple(hbm(a) for a in list(srcs) + list(lands)),
        in_specs=[_HBM] * (2 * n) + [_SEM, _SEM] + [pl.BlockSpec(memory_space=pl.ANY)] * len(after),
        out_specs=tuple([_HBM] * (2 * n)), input_output_aliases={i: i for i in range(2 * n)},
        compiler_params=pltpu.CompilerParams(has_side_effects=_EFFECT),
    )(*srcs, *lands, send_sems, recv_sems, *after)
    return res[n:]


def _adam(name, recv, w, m, v):
    shape = w.shape
    R, C = shape
    tr = R
    while N_DEV * tr * C * 4 > 6 * 1024 * 1024 and tr % 32 == 0:
        tr //= 2

    def body(recv_ref, w_ref, m_ref, v_ref, g_ref, d_ref, m2_ref, v2_ref):
        g = recv_ref[0].astype(F32)
        for j in range(1, N_DEV):
            g = g + recv_ref[j].astype(F32)
        w_, m_, v_ = w_ref[...], m_ref[...], v_ref[...]
        m2 = ADAM_B1 * m_ + (1.0 - ADAM_B1) * g
        v2 = ADAM_B2 * v_ + (1.0 - ADAM_B2) * (g * g)
        m_hat = m2 / (1.0 - ADAM_B1 ** ADAM_STEP)
        v_hat = v2 / (1.0 - ADAM_B2 ** ADAM_STEP)
        g_ref[...] = g
        d_ref[...] = -ADAM_LR * (m_hat / (jnp.sqrt(v_hat) + ADAM_EPS) + ADAM_WD * w_)
        m2_ref[...] = m2
        v2_ref[...] = v2

    blk = pl.BlockSpec((tr, C), lambda i: (i, 0))
    return _pcall(body, name=name, grid=(R // tr,),
                  in_specs=[pl.BlockSpec((N_DEV, tr, C), lambda i: (0, i, 0)), blk, blk, blk],
                  out_specs=[blk] * 4, out_shape=[_sds(shape)] * 4)(recv, w, m, v)


_NAMES = ['ffn1_pre_g', 'ffn1_w_in', 'ffn1_w_out', 'ffn1_post_g', 'mix_pre_g', 'w_in', 'w_fu', 'b_f', 'gla_norm_g',
          'w_pool', 'pool_scale', 'mem_norm_g', 'w_mem_kv', 'w_up_gla', 'w_up_pool', 'w_up_xattn', 'w_o', 'mix_post_g',
          'ffn2_pre_g', 'ffn2_w_in', 'ffn2_w_out', 'ffn2_post_g', 'final_g']
_SHARDED = ['ffn1_w_in', 'ffn1_w_out', 'w_in', 'w_fu', 'w_mem_kv', 'w_up_gla', 'w_up_pool', 'w_up_xattn', 'w_o',
            'ffn2_w_in', 'ffn2_w_out']
_COL_SHARDED = ['w_up_pool', 'w_up_xattn']


def _cols_to_full(g):
    return jnp.transpose(g, (1, 0, 2)).reshape(g.shape[1], N_DEV * g.shape[2])


def _full_to_cols(f):
    R, C = f.shape
    return jnp.transpose(f.reshape(R, N_DEV, C // N_DEV), (1, 0, 2))


def _to_internal(w_in_full):
    o = 0
    parts = []
    for s in IN_SPLITS:
        parts.append(w_in_full[:, o:o + s])
        o += s
    q, k, v, g_out, f_low, p_in, xq, gates = parts
    f_low = jnp.pad(f_low, ((0, 0), (0, FLOW_W - GATE_RANK)))
    return jnp.concatenate([q, k, v, g_out, gates, p_in, xq, f_low], axis=-1)


def _from_internal(d):
    q, k, v, g_out = d[:, 0:512], d[:, 512:1024], d[:, 1024:2048], d[:, 2048:3072]
    gates, p_in, xq = d[:, 3072:6144], d[:, 6144:6656], d[:, 6656:7168]
    f_low = d[:, 7168:7168 + GATE_RANK]
    return jnp.concatenate([q, k, v, g_out, f_low, p_in, xq, gates], axis=-1)


_GROUPS = {"ffn1": ['ffn1_w_in', 'ffn1_w_out'],
           "mix": ['w_in', 'w_fu', 'w_mem_kv', 'w_up_gla', 'w_up_pool', 'w_up_xattn', 'w_o'],
           "ffn2": ['ffn2_w_in', 'ffn2_w_out']}


def _rows_to_full(g):
    return g.reshape(N_DEV * g.shape[1], g.shape[2])


def _mix_weights(gathered):
    W = {n: _rows_to_full(gathered[n]) for n in ('w_mem_kv', 'w_up_gla', 'w_o')}
    for n in _COL_SHARDED:
        W[n] = _cols_to_full(gathered[n])
    W["w_int"] = _to_internal(_cols_to_full(gathered["w_in"]))
    W["w_fu_pad"] = jnp.pad(_cols_to_full(gathered["w_fu"]), ((0, FLOW_W - GATE_RANK), (0, 0)))
    return W


def _mix_chunks(G, n):
    if n == "w_in":
        return _full_to_cols(_from_internal(G["w_int"]))
    if n == "w_fu":
        return _full_to_cols(G["w_fu_pad"][:GATE_RANK].astype(BF16))
    if n in _COL_SHARDED:
        return _full_to_cols(G[n])
    return G[n].reshape(N_DEV, G[n].shape[0] // N_DEV, G[n].shape[1])


def _step(x, mem, tgt, P, Mo, Vo):
    gather, tokens = {}, []
    for grp, names in _GROUPS.items():
        gather[grp], tok = _exchange_start("gather_" + grp, [P[n].astype(BF16) for n in names], scatter=False)
        tokens.append(tok)
    small = {n: P[n] for n in _NAMES if n not in _SHARDED}

    g1 = dict(zip(_GROUPS["ffn1"], _exchange_wait("gather_ffn1_wait", gather["ffn1"], tokens[1:])))
    w1in, w1out = g1["ffn1_w_in"], _rows_to_full(g1["ffn1_w_out"])
    x1, sv1 = _ffn_fwd("ffn1", x, small["ffn1_pre_g"], w1in, w1out, small["ffn1_post_g"])

    gm = dict(zip(_GROUPS["mix"], _exchange_wait("gather_mix_wait", gather["mix"], [x1])))
    Wm = {**small, **_mix_weights(gm)}
    x2, svm = _mix_fwd(x1, mem, Wm)

    g2 = dict(zip(_GROUPS["ffn2"], _exchange_wait("gather_ffn2_wait", gather["ffn2"], [x2])))
    w2in, w2out = g2["ffn2_w_in"], _rows_to_full(g2["ffn2_w_out"])
    x3, sv2 = _ffn_fwd("ffn2", x2, small["ffn2_pre_g"], w2in, w2out, small["ffn2_post_g"])

    loss, dx3, d_final_g = _head(x3, tgt, small["final_g"])

    G = dict(final_g=d_final_g)
    dx2, d_w2in, d_w2out, G["ffn2_pre_g"], G["ffn2_post_g"] = _ffn_bwd(
        "ffn2", dx3, x2, small["ffn2_pre_g"], w2in, w2out, small["ffn2_post_g"], sv2)
    scat2, tok2 = _exchange_start("scatter_ffn2", [d_w2in, d_w2out.reshape(N_DEV, -1, D_MODEL)], scatter=True)

    dx1, Gm = _mix_bwd(dx2, x1, mem, Wm, svm, deps=[tok2])
    G.update(Gm)
    scatm, tokm = _exchange_start("scatter_mix", [_mix_chunks(Gm, n) for n in _GROUPS["mix"]], scatter=True)

    dx, d_w1in, d_w1out, G["ffn1_pre_g"], G["ffn1_post_g"] = _ffn_bwd(
        "ffn1", dx1, x, small["ffn1_pre_g"], w1in, w1out, small["ffn1_post_g"], sv1, deps=[tokm])
    scat1, tok1 = _exchange_start("scatter_ffn1", [d_w1in, d_w1out.reshape(N_DEV, -1, D_MODEL)], scatter=True)

    recv = dict(zip(small, _exchange("gather_small_grads", [G[n] for n in small], scatter=False)))
    grads, deltas, new_m, new_v = {}, {}, {}, {}
    last = tok1
    for grp, state in (("small", None), ("ffn2", scat2), ("mix", scatm), ("ffn1", scat1)):
        names = list(small) if state is None else _GROUPS[grp]
        if state is not None:
            recv.update(zip(names, _exchange_wait("scatter_" + grp + "_wait", state, [last])))
        for n in names:
            shp = P[n].shape
            shp2 = shp if len(shp) == 2 else (shp[0] * shp[1], shp[2])
            g, d, m2, v2 = _adam("adam_" + n, recv[n].reshape((N_DEV,) + shp2), P[n].reshape(shp2),
                                 Mo[n].reshape(shp2), Vo[n].reshape(shp2))
            grads[n], deltas[n], new_m[n], new_v[n] = (t.reshape((1,) + shp) for t in (g, d, m2, v2))
            last = v2
    return loss, dx, grads, deltas, new_m, new_v


def kernel(x, mem, ffn1_pre_g, ffn1_w_in, ffn1_w_out, ffn1_post_g, mix_pre_g, w_in, w_fu, b_f, gla_norm_g, w_pool, pool_scale, mem_norm_g, w_mem_kv, w_up_gla, w_up_pool, w_up_xattn, w_o, mix_post_g, ffn2_pre_g, ffn2_w_in, ffn2_w_out, ffn2_post_g, final_g, loss_target, m_ffn1_pre_g, m_ffn1_w_in, m_ffn1_w_out, m_ffn1_post_g, m_mix_pre_g, m_w_in, m_w_fu, m_b_f, m_gla_norm_g, m_w_pool, m_pool_scale, m_mem_norm_g, m_w_mem_kv, m_w_up_gla, m_w_up_pool, m_w_up_xattn, m_w_o, m_mix_post_g, m_ffn2_pre_g, m_ffn2_w_in, m_ffn2_w_out, m_ffn2_post_g, m_final_g, v_ffn1_pre_g, v_ffn1_w_in, v_ffn1_w_out, v_ffn1_post_g, v_mix_pre_g, v_w_in, v_w_fu, v_b_f, v_gla_norm_g, v_w_pool, v_pool_scale, v_mem_norm_g, v_w_mem_kv, v_w_up_gla, v_w_up_pool, v_w_up_xattn, v_w_o, v_mix_post_g, v_ffn2_pre_g, v_ffn2_w_in, v_ffn2_w_out, v_ffn2_post_g, v_final_g):
    params = [ffn1_pre_g, ffn1_w_in, ffn1_w_out, ffn1_post_g, mix_pre_g, w_in, w_fu, b_f, gla_norm_g, w_pool, pool_scale, mem_norm_g, w_mem_kv, w_up_gla, w_up_pool, w_up_xattn, w_o, mix_post_g, ffn2_pre_g, ffn2_w_in, ffn2_w_out, ffn2_post_g, final_g]
    moms = [m_ffn1_pre_g, m_ffn1_w_in, m_ffn1_w_out, m_ffn1_post_g, m_mix_pre_g, m_w_in, m_w_fu, m_b_f, m_gla_norm_g, m_w_pool, m_pool_scale, m_mem_norm_g, m_w_mem_kv, m_w_up_gla, m_w_up_pool, m_w_up_xattn, m_w_o, m_mix_post_g, m_ffn2_pre_g, m_ffn2_w_in, m_ffn2_w_out, m_ffn2_post_g, m_final_g]
    vars_ = [v_ffn1_pre_g, v_ffn1_w_in, v_ffn1_w_out, v_ffn1_post_g, v_mix_pre_g, v_w_in, v_w_fu, v_b_f, v_gla_norm_g, v_w_pool, v_pool_scale, v_mem_norm_g, v_w_mem_kv, v_w_up_gla, v_w_up_pool, v_w_up_xattn, v_w_o, v_mix_post_g, v_ffn2_pre_g, v_ffn2_w_in, v_ffn2_w_out, v_ffn2_post_g, v_final_g]
    P = {n: a[0] if a.ndim > 2 else a for n, a in zip(_NAMES, params)}
    Mo = {n: a[0] if a.ndim > 2 else a for n, a in zip(_NAMES, moms)}
    Vo = {n: a[0] if a.ndim > 2 else a for n, a in zip(_NAMES, vars_)}
    loss, dx, grads, deltas, new_m, new_v = _step(x[0], mem[0], loss_target[0], P, Mo, Vo)
    loss = lax.psum(loss, ("x", "y", "c"))
    out = [loss, dx[None]]
    for d in (grads, deltas, new_m, new_v):
        for n in _NAMES:
            out.append(d[n].reshape(params[_NAMES.index(n)].shape))
    return tuple(out)
```

```python
import functools

import jax
import jax.numpy as jnp
from jax import lax
from jax.experimental import pallas as pl
from jax.experimental.pallas import tpu as pltpu

F32 = jnp.float32
BF16 = jnp.bfloat16

N_DEV = 8
D_MODEL = 1024
D_FF = 2816
FF_BLK = 2 * D_FF // N_DEV
N_FF_BLK = D_FF // FF_BLK
CHUNK = 64
GLA_HEADS = 4
GLA_DK = 512
GLA_DV = 1024
GLA_HDK = 128
GLA_HDV = 256
GATE_RANK = 16
GATE_TEMP = 16.0
POOL_WINDOWS = (2, 4, 8, 16)
POOL_W = 512
POOL_GD = 128
POOL_HALO = 16
XA_HEADS = 4
XA_HD = 128
XA_W = 512
EPS = 1e-6
IN_SPLITS = (GLA_DK, GLA_DK, GLA_DV, GLA_DV, GATE_RANK, POOL_W, XA_W, 3 * D_MODEL)
IN_WIDTH = sum(IN_SPLITS)
IN_SHARD = IN_WIDTH // N_DEV
INT_W = 3072 + 3072 + 1024 + 128
INT_NBLK = 3
INT_BLK = INT_W // INT_NBLK
FLOW_W = 128

ADAM_LR = 0.001
ADAM_B1 = 0.9
ADAM_B2 = 0.999
ADAM_EPS = 1e-08
ADAM_WD = 0.01
ADAM_STEP = 10

VMEM_LIMIT = 56 * 1024 * 1024

_NN = (((1,), (0,)), ((), ()))
_NT = (((1,), (1,)), ((), ()))
_TN = (((0,), (0,)), ((), ()))


def _pcall(body, *, name, grid, in_specs, out_specs, out_shape, scratch=()):
    return pl.pallas_call(
        body, name=name, grid=grid, in_specs=in_specs, out_specs=out_specs, out_shape=out_shape,
        scratch_shapes=list(scratch),
        compiler_params=pltpu.CompilerParams(dimension_semantics=("arbitrary",) * len(grid),
                                             vmem_limit_bytes=VMEM_LIMIT))


def _dot(a, b, dims=_NN):
    return lax.dot_general(a.astype(BF16), b.astype(BF16), dims, preferred_element_type=F32)


def _mm(name, a, b, *, grid, a_blk, a_map, b_blk, b_map, o_shape, o_blk, o_map, dims, out_dtype=F32, deps=()):
    nk = grid[2]

    def body(a_ref, b_ref, *rest):
        o_ref, scr = rest[len(deps)], rest[len(deps) + 1:]
        p = _dot(a_ref[...], b_ref[...], dims)
        if nk == 1:
            o_ref[...] = p.astype(o_ref.dtype)
        else:
            acc = scr[0]
            k = pl.program_id(2)

            @pl.when(k == 0)
            def _():
                acc[...] = p

            @pl.when(k > 0)
            def _():
                acc[...] += p

            @pl.when(k == nk - 1)
            def _():
                o_ref[...] = acc[...].astype(o_ref.dtype)

    acc_shape = tuple(d for d in o_blk if d is not None)
    return _pcall(body, name=name, grid=grid,
                  in_specs=[pl.BlockSpec(a_blk, a_map), pl.BlockSpec(b_blk, b_map)]
                  + [pl.BlockSpec(memory_space=pl.ANY)] * len(deps),
                  out_specs=pl.BlockSpec(o_blk, o_map),
                  out_shape=jax.ShapeDtypeStruct(o_shape, out_dtype),
                  scratch=[pltpu.VMEM(acc_shape, F32)] if nk > 1 else [])(a, b, *deps)


def _mm_nn(name, a, b, out_dtype=F32, tm=512, tn=None):
    M, K = a.shape
    N = b.shape[1]
    tm, tn = min(tm, M), (tn or N)
    return _mm(name, a, b, grid=(N // tn, M // tm, 1), a_blk=(tm, K), a_map=lambda j, i, k: (i, 0),
               b_blk=(K, tn), b_map=lambda j, i, k: (0, j), o_shape=(M, N), o_blk=(tm, tn),
               o_map=lambda j, i, k: (i, j), dims=_NN, out_dtype=out_dtype)


def _mm_nt(name, a, b, out_dtype=F32, tm=512):
    M, K = a.shape
    N = b.shape[0]
    tm = min(tm, M)
    return _mm(name, a, b, grid=(1, M // tm, 1), a_blk=(tm, K), a_map=lambda j, i, k: (i, 0),
               b_blk=(N, K), b_map=lambda j, i, k: (0, 0), o_shape=(M, N), o_blk=(tm, N),
               o_map=lambda j, i, k: (i, 0), dims=_NT, out_dtype=out_dtype)


def _mm_tn(name, a, b, out_dtype=BF16, ts=512, tn=None):
    S, M = a.shape
    N = b.shape[1]
    ts, tn = min(ts, S), (tn or N)
    return _mm(name, a, b, grid=(N // tn, 1, S // ts), a_blk=(ts, M), a_map=lambda j, i, k: (k, 0),
               b_blk=(ts, tn), b_map=lambda j, i, k: (k, j), o_shape=(M, N), o_blk=(M, tn),
               o_map=lambda j, i, k: (0, j), dims=_TN, out_dtype=out_dtype)


class _Win:
    def __init__(self, arr, w, c):
        self.arr, self.w, self.c = arr, w, c


def _row_spec(x, tm):
    if isinstance(x, _Win):
        return x.arr, pl.BlockSpec((tm, x.w), functools.partial(lambda i, c: (i, c), c=x.c))
    if x.ndim == 3:
        return x, pl.BlockSpec((x.shape[0], tm, x.shape[2]), lambda i: (0, i, 0))
    return x, pl.BlockSpec((tm, x.shape[1]), lambda i: (i, 0))


def _rowwise(name, fn, rows, consts, outs, accs=(), tm=256):
    first = rows[0].arr if isinstance(rows[0], _Win) else rows[0]
    S = first.shape[1] if first.ndim == 3 else first.shape[0]
    tm = min(tm, S)
    n_in, n_out = len(rows) + len(consts), len(outs)
    arrays, in_specs = [], []
    for r in rows:
        arr, spec = _row_spec(r, tm)
        arrays.append(arr)
        in_specs.append(spec)
    for c in consts:
        arrays.append(c)
        in_specs.append(pl.BlockSpec(c.shape, functools.partial(lambda i, n: (0,) * n, n=c.ndim)))
    out_specs = [_row_spec(o, tm)[1] for o in outs]
    out_specs += [pl.BlockSpec(a.shape, functools.partial(lambda i, n: (0,) * n, n=len(a.shape))) for a in accs]

    def body(*refs):
        res = fn(*[r[...] for r in refs[:n_in]])
        if not isinstance(res, (tuple, list)):
            res = (res,)
        orefs = refs[n_in:]
        for r, v in zip(orefs[:n_out], res[:n_out]):
            r[...] = v.astype(r.dtype)
        i = pl.program_id(0)
        for r, v in zip(orefs[n_out:], res[n_out:]):
            @pl.when(i == 0)
            def _(r=r, v=v):
                r[...] = v.astype(r.dtype)

            @pl.when(i > 0)
            def _(r=r, v=v):
                r[...] += v.astype(r.dtype)

    res = _pcall(body, name=name, grid=(S // tm,), in_specs=in_specs, out_specs=out_specs,
                 out_shape=[jax.ShapeDtypeStruct(o.shape, o.dtype) for o in list(outs) + list(accs)])(*arrays)
    return res


def _sds(shape, dtype=F32):
    return jax.ShapeDtypeStruct(shape, dtype)


def _rms(x, g):
    return x * lax.rsqrt(jnp.mean(x * x, axis=-1, keepdims=True) + EPS) * g


def _rms_bwd(x, g, dy):
    _, vjp = jax.vjp(_rms, x, g)
    return vjp(dy)


def _sigmoid(x):
    return 0.5 * jnp.tanh(0.5 * x) + 0.5


def _silu(x):
    return x * _sigmoid(x)


def _log_sigmoid(f):
    return jnp.minimum(f, 0.0) - jnp.log(1.0 + jnp.exp(-jnp.abs(f)))


def _head_rms_gate(o, g_out, gn):
    parts = [_rms(o[:, h * GLA_HDV:(h + 1) * GLA_HDV], gn[:, h * GLA_HDV:(h + 1) * GLA_HDV]) for h in range(GLA_HEADS)]
    return jnp.concatenate(parts, axis=-1) * _silu(g_out)


def _merge(gates, ya, yb, yc):
    return (_sigmoid(gates[:, :D_MODEL]) * ya + _sigmoid(gates[:, D_MODEL:2 * D_MODEL]) * yb
            + _sigmoid(gates[:, 2 * D_MODEL:]) * yc)


def _tri_dot(t, x):
    hi = x.astype(BF16)
    r1 = x - hi.astype(F32)
    mid = r1.astype(BF16)
    lo = (r1 - mid.astype(F32)).astype(BF16)
    d = functools.partial(lax.dot_general, dimension_numbers=_NN, preferred_element_type=F32)
    return d(t, hi) + d(t, mid) + d(t, lo)


def _ffn_fwd(tag, x, gpre, wg, gpost, get_wout):
    S = x.shape[0]
    h = _rowwise(tag + "_pre", lambda x, g: _rms(x, g), [x], [gpre], [_sds((S, D_MODEL), BF16)])[0]
    tm = min(512, S)

    def up_body(h_ref, w_ref, u_ref, act_ref):
        hh = h_ref[...]
        a, b = _dot(hh, w_ref[0]), _dot(hh, w_ref[1])
        u_ref[0] = a.astype(BF16)
        u_ref[1] = b.astype(BF16)
        act_ref[...] = (_silu(a) * b).astype(BF16)

    u, act = _pcall(
        up_body, name=tag + "_up", grid=(N_FF_BLK, S // tm),
        in_specs=[pl.BlockSpec((tm, D_MODEL), lambda j, i: (i, 0)),
                  pl.BlockSpec((2, None, D_MODEL, FF_BLK), lambda j, i: (0, j, 0, 0))],
        out_specs=[pl.BlockSpec((2, None, tm, FF_BLK), lambda j, i: (0, j, i, 0)),
                   pl.BlockSpec((None, tm, FF_BLK), lambda j, i: (j, i, 0))],
        out_shape=[_sds((2, N_FF_BLK, S, FF_BLK), BF16), _sds((N_FF_BLK, S, FF_BLK), BF16)],
    )(h, wg.reshape(2, N_FF_BLK, D_MODEL, FF_BLK))

    def down_body(act_ref, w_ref, x_ref, g_ref, f_ref, xo_ref):
        f = _dot(act_ref[0], w_ref[0])
        for k in range(1, N_FF_BLK):
            f = f + _dot(act_ref[k], w_ref[k])
        f_ref[...] = f
        xo_ref[...] = x_ref[...] + 0.5 * _rms(f, g_ref[...])

    wout = get_wout(act)
    row = pl.BlockSpec((tm, D_MODEL), lambda i: (i, 0))
    f, xo = _pcall(
        down_body, name=tag + "_down", grid=(S // tm,),
        in_specs=[pl.BlockSpec((N_FF_BLK, tm, FF_BLK), lambda i: (0, i, 0)),
                  pl.BlockSpec((N_FF_BLK, FF_BLK, D_MODEL), lambda i: (0, 0, 0)), row,
                  pl.BlockSpec((1, D_MODEL), lambda i: (0, 0))],
        out_specs=[row, row], out_shape=[_sds((S, D_MODEL)), _sds((S, D_MODEL))],
    )(act, wout.reshape(N_FF_BLK, FF_BLK, D_MODEL), x, gpost)
    return xo, wout, (h, u, act, f)


def _ffn_bwd(tag, dxo, x, gpre, wg, wout, gpost, saved, on_grads=None):
    h, u, act, f = saved
    S = x.shape[0]
    tm = min(512, S)

    def post_bwd(f, dxo, g):
        df, dg = _rms_bwd(f, g, 0.5 * dxo)
        return df, dg

    df, dgpost = _rowwise(tag + "_post_bwd", post_bwd, [f, dxo], [gpost], [_sds((S, D_MODEL), BF16)],
                          [_sds((1, D_MODEL))])
    ts = min(2048, S)
    dwout = _mm(tag + "_dwout", act, df, grid=(N_FF_BLK, 1, S // ts), a_blk=(None, ts, FF_BLK),
                a_map=lambda j, i, k: (j, k, 0), b_blk=(ts, D_MODEL), b_map=lambda j, i, k: (k, 0),
                o_shape=(D_FF, D_MODEL), o_blk=(FF_BLK, D_MODEL), o_map=lambda j, i, k: (j, 0), dims=_TN,
                out_dtype=BF16)

    def dact_body(df_ref, w_ref, u_ref, du_ref):
        dact = _dot(df_ref[...], w_ref[...], _NT)
        a, b = u_ref[0].astype(F32), u_ref[1].astype(F32)
        sg = _sigmoid(a)
        du_ref[0] = (dact * b * (sg * (1.0 + a * (1.0 - sg)))).astype(BF16)
        du_ref[1] = (dact * (a * sg)).astype(BF16)

    u_spec = pl.BlockSpec((2, None, tm, FF_BLK), lambda j, i: (0, j, i, 0))
    du = _pcall(
        dact_body, name=tag + "_dact", grid=(N_FF_BLK, S // tm),
        in_specs=[pl.BlockSpec((tm, D_MODEL), lambda j, i: (i, 0)),
                  pl.BlockSpec((None, FF_BLK, D_MODEL), lambda j, i: (j, 0, 0)), u_spec],
        out_specs=u_spec, out_shape=_sds((2, N_FF_BLK, S, FF_BLK), BF16),
    )(df, wout.reshape(N_FF_BLK, FF_BLK, D_MODEL), u).reshape(N_DEV, S, FF_BLK)
    dwg = _mm(tag + "_dwin", h, du, grid=(N_DEV, 1, S // ts), a_blk=(ts, D_MODEL), a_map=lambda j, i, k: (k, 0),
              b_blk=(None, ts, FF_BLK), b_map=lambda j, i, k: (j, k, 0), o_shape=(N_DEV, D_MODEL, FF_BLK),
              o_blk=(None, D_MODEL, FF_BLK), o_map=lambda j, i, k: (j, 0, 0), dims=_TN, out_dtype=BF16)
    deps = [on_grads(dwg, dwout)] if on_grads is not None else []

    def dh_body(du_ref, w_ref, x_ref, dxo_ref, g_ref, *rest):
        dx_ref, dg_ref, acc = rest[len(deps):]
        i, k = pl.program_id(0), pl.program_id(1)
        p = _dot(du_ref[...], w_ref[...], _NT)

        @pl.when(k == 0)
        def _():
            acc[...] = p

        @pl.when(k > 0)
        def _():
            acc[...] += p

        @pl.when(k == N_DEV - 1)
        def _():
            dx, dg = _rms_bwd(x_ref[...], g_ref[...], acc[...])
            dx_ref[...] = dx + dxo_ref[...]

            @pl.when(i == 0)
            def _():
                dg_ref[...] = dg

            @pl.when(i > 0)
            def _():
                dg_ref[...] += dg

    row = pl.BlockSpec((tm, D_MODEL), lambda i, k: (i, 0))
    vec = pl.BlockSpec((1, D_MODEL), lambda i, k: (0, 0))
    dx, dgpre = _pcall(
        dh_body, name=tag + "_dh", grid=(S // tm, N_DEV),
        in_specs=[pl.BlockSpec((None, tm, FF_BLK), lambda i, k: (k, i, 0)),
                  pl.BlockSpec((None, D_MODEL, FF_BLK), lambda i, k: (k, 0, 0)), row, row, vec]
        + [pl.BlockSpec(memory_space=pl.ANY)] * len(deps),
        out_specs=[row, vec], out_shape=[_sds((S, D_MODEL)), _sds((1, D_MODEL))],
        scratch=[pltpu.VMEM((tm, D_MODEL), F32)])(du, wg, x, dxo, gpre, *deps)
    return dx, dwg, dwout, dgpre, dgpost


def _gla_chunk_common(k, flow, wfu, bf, tri):
    f = _dot(flow, wfu) + bf
    la = _log_sigmoid(f) * (1.0 / GATE_TEMP)
    b = _tri_dot(tri, la)
    b_end = b[CHUNK - 1:CHUNK, :]
    w = jnp.exp(b_end - b)
    kt = k * w
    d = jnp.exp(b_end)
    return f, w, kt, d


def _tri_matrix(lower):
    r = lax.broadcasted_iota(jnp.int32, (CHUNK, CHUNK), 0)
    c = lax.broadcasted_iota(jnp.int32, (CHUNK, CHUNK), 1)
    return jnp.where((r >= c) if lower else (r <= c), 1.0, 0.0).astype(BF16)


def _gla_fwd(proj, wfu, bf):
    S = proj.shape[0]
    nc = S // CHUNK

    def body(q_ref, k_ref, v_ref, flow_ref, wfu_ref, bf_ref, o_ref, st_ref, state):
        c = pl.program_id(0)

        @pl.when(c == 0)
        def _():
            state[...] = jnp.zeros_like(state)

        _, _, kt, d = _gla_chunk_common(k_ref[...], flow_ref[...], wfu_ref[...], bf_ref[...], _tri_matrix(True))
        q = q_ref[...] * (GLA_HDK ** -0.5)
        v = v_ref[...]
        outs = []
        for h in range(GLA_HEADS):
            ks, vs = slice(h * GLA_HDK, (h + 1) * GLA_HDK), slice(h * GLA_HDV, (h + 1) * GLA_HDV)
            st = state[vs, :] * d[:, ks] + _dot(v[:, vs], kt[:, ks], _TN)
            state[vs, :] = st
            outs.append(_dot(q[:, ks], st, _NT))
        o_ref[...] = jnp.concatenate(outs, axis=-1)
        st_ref[...] = state[...]

    return _pcall(
        body, name="gla_fwd", grid=(nc,),
        in_specs=[pl.BlockSpec((CHUNK, GLA_DK), lambda c: (c, 0)), pl.BlockSpec((CHUNK, GLA_DK), lambda c: (c, 1)),
                  pl.BlockSpec((CHUNK, GLA_DV), lambda c: (c, 1)),
                  pl.BlockSpec((CHUNK, FLOW_W), lambda c: (c, (INT_W - FLOW_W) // FLOW_W)),
                  pl.BlockSpec(wfu.shape, lambda c: (0, 0)), pl.BlockSpec(bf.shape, lambda c: (0, 0))],
        out_specs=[pl.BlockSpec((CHUNK, GLA_DV), lambda c: (c, 0)),
                   pl.BlockSpec((None, GLA_DV, GLA_HDK), lambda c: (c, 0, 0))],
        out_shape=[_sds((S, GLA_DV)), _sds((nc, GLA_DV, GLA_HDK))],
        scratch=[pltpu.VMEM((GLA_DV, GLA_HDK), F32)])(proj, proj, proj, proj, wfu, bf)


def _gla_bwd(proj, wfu, bf, states, do):
    S = proj.shape[0]
    nc = S // CHUNK

    def body(q_ref, k_ref, v_ref, flow_ref, wfu_ref, bf_ref, st_ref, stp_ref, do_ref,
             dq_ref, dk_ref, dv_ref, dflow_ref, dwfu_ref, dbf_ref, dstate):
        step = pl.program_id(0)
        c = nc - 1 - step

        @pl.when(step == 0)
        def _():
            dstate[...] = jnp.zeros_like(dstate)
            dwfu_ref[...] = jnp.zeros_like(dwfu_ref)
            dbf_ref[...] = jnp.zeros_like(dbf_ref)

        k, flow, wfu_v = k_ref[...], flow_ref[...], wfu_ref[...]
        f, w, kt, d = _gla_chunk_common(k, flow, wfu_v, bf_ref[...], _tri_matrix(True))
        q = q_ref[...] * (GLA_HDK ** -0.5)
        v, dout = v_ref[...], do_ref[...]
        has_prev = (c > 0).astype(F32)
        dqs, dkts, dvs, dds = [], [], [], []
        for h in range(GLA_HEADS):
            ks, vs = slice(h * GLA_HDK, (h + 1) * GLA_HDK), slice(h * GLA_HDV, (h + 1) * GLA_HDV)
            st = st_ref[vs, :]
            dqs.append(_dot(dout[:, vs], st))
            dst = dstate[vs, :] + _dot(dout[:, vs], q[:, ks], _TN)
            dkts.append(_dot(v[:, vs], dst))
            dvs.append(_dot(kt[:, ks], dst, _NT))
            dds.append(jnp.sum(dst * stp_ref[vs, :], axis=0, keepdims=True) * has_prev)
            dstate[vs, :] = dst * d[:, ks]
        dq_ref[...] = (jnp.concatenate(dqs, axis=-1) * (GLA_HDK ** -0.5)).astype(dq_ref.dtype)
        dv_ref[...] = jnp.concatenate(dvs, axis=-1).astype(dv_ref.dtype)
        dkt = jnp.concatenate(dkts, axis=-1)
        dd = jnp.concatenate(dds, axis=-1)
        dk_ref[...] = (dkt * w).astype(dk_ref.dtype)
        de = dkt * kt
        db_end = jnp.sum(de, axis=0, keepdims=True) + dd * d
        dla = db_end - _tri_dot(_tri_matrix(False), de)
        df = dla * (1.0 - _sigmoid(f)) * (1.0 / GATE_TEMP)
        dflow_ref[...] = _dot(df, wfu_v, _NT).astype(dflow_ref.dtype)
        dwfu_ref[...] += _dot(flow, df, _TN)
        dbf_ref[...] += jnp.sum(df, axis=0, keepdims=True)

    rc = lambda s: nc - 1 - s
    return _pcall(
        body, name="gla_bwd", grid=(nc,),
        in_specs=[pl.BlockSpec((CHUNK, GLA_DK), lambda s: (rc(s), 0)), pl.BlockSpec((CHUNK, GLA_DK), lambda s: (rc(s), 1)),
                  pl.BlockSpec((CHUNK, GLA_DV), lambda s: (rc(s), 1)),
                  pl.BlockSpec((CHUNK, FLOW_W), lambda s: (rc(s), (INT_W - FLOW_W) // FLOW_W)),
                  pl.BlockSpec(wfu.shape, lambda s: (0, 0)), pl.BlockSpec(bf.shape, lambda s: (0, 0)),
                  pl.BlockSpec((None, GLA_DV, GLA_HDK), lambda s: (rc(s), 0, 0)),
                  pl.BlockSpec((None, GLA_DV, GLA_HDK), lambda s: (jnp.maximum(rc(s) - 1, 0), 0, 0)),
                  pl.BlockSpec((CHUNK, GLA_DV), lambda s: (rc(s), 0))],
        out_specs=[pl.BlockSpec((CHUNK, GLA_DK), lambda s: (rc(s), 0)), pl.BlockSpec((CHUNK, GLA_DK), lambda s: (rc(s), 0)),
                   pl.BlockSpec((CHUNK, GLA_DV), lambda s: (rc(s), 0)), pl.BlockSpec((CHUNK, FLOW_W), lambda s: (rc(s), 0)),
                   pl.BlockSpec(wfu.shape, lambda s: (0, 0)), pl.BlockSpec(bf.shape, lambda s: (0, 0))],
        out_shape=[_sds((S, GLA_DK), BF16), _sds((S, GLA_DK), BF16), _sds((S, GLA_DV), BF16), _sds((S, FLOW_W), BF16),
                   _sds(wfu.shape), _sds(bf.shape)],
        scratch=[pltpu.VMEM((GLA_DV, GLA_HDK), F32)])(proj, proj, proj, proj, wfu, bf, states, states, do)


def _pool_counts(tm, i):
    t = (lax.broadcasted_iota(jnp.int32, (tm, POOL_GD), 0) + i * tm + 1).astype(F32)
    return [jnp.minimum(t, float(w)) for w in POOL_WINDOWS]


def _pool_fwd(proj, w_pool, pool_scale, tm=256):
    S = proj.shape[0]
    tm = min(tm, S // 2)
    col = (3072 + 3072) // POOL_W
    hb = tm // POOL_HALO

    def body(p_ref, halo_ref, wp_ref, sc_ref, mixed_ref, out_ref):
        i = pl.program_id(0)
        p = p_ref[...]
        halo = halo_ref[...] * (i > 0).astype(F32)
        ext = jnp.concatenate([halo, p], axis=0)
        n = tm + POOL_HALO
        sums, acc, k = {}, ext, 1
        while k < POOL_WINDOWS[-1]:
            acc = acc + pltpu.roll(acc, k, axis=0)
            k *= 2
            sums[k] = acc
        cnts = _pool_counts(tm, i)
        mixed, lin = [], []
        for g, w in enumerate(POOL_WINDOWS):
            ls = slice(g * POOL_GD, (g + 1) * POOL_GD)
            m = sums[w][POOL_HALO:n, ls] / cnts[g] - p[:, ls]
            mixed.append(m)
            lin.append(_dot(m, wp_ref[g]))
        mixed_ref[...] = jnp.concatenate(mixed, axis=-1)
        out_ref[...] = (jnp.concatenate(lin, axis=-1) * sc_ref[...]).astype(out_ref.dtype)

    return _pcall(
        body, name="pool_fwd", grid=(S // tm,),
        in_specs=[pl.BlockSpec((tm, POOL_W), lambda i: (i, col)),
                  pl.BlockSpec((POOL_HALO, POOL_W), lambda i: (jnp.maximum(i * hb - 1, 0), col)),
                  pl.BlockSpec(w_pool.shape, lambda i: (0, 0, 0)), pl.BlockSpec(pool_scale.shape, lambda i: (0, 0))],
        out_specs=[pl.BlockSpec((tm, POOL_W), lambda i: (i, 0)), pl.BlockSpec((tm, POOL_W), lambda i: (i, 0))],
        out_shape=[_sds((S, POOL_W)), _sds((S, POOL_W), BF16)])(proj, proj, w_pool, pool_scale)


def _pool_lin_bwd(dout, mixed, w_pool, pool_scale):
    S = dout.shape[0]

    def fn(dout, mixed, wp, sc):
        dlin = dout * sc
        dm, dwp, lin = [], [], []
        for g in range(len(POOL_WINDOWS)):
            ls = slice(g * POOL_GD, (g + 1) * POOL_GD)
            lin.append(_dot(mixed[:, ls], wp[g]))
            dm.append(_dot(dlin[:, ls], wp[g], _NT))
            dwp.append(_dot(mixed[:, ls], dlin[:, ls], _TN))
        dsc = jnp.sum(dout * jnp.concatenate(lin, axis=-1), axis=0, keepdims=True)
        return jnp.concatenate(dm, axis=-1), jnp.concatenate(dwp, axis=0), dsc

    return _rowwise("pool_lin_bwd", fn, [dout, mixed], [w_pool, pool_scale], [_sds((S, POOL_W))],
                    [_sds((len(POOL_WINDOWS) * POOL_GD, POOL_GD)), _sds((1, POOL_W))])


def _pool_win_bwd(dmixed, tm=256):
    S = dmixed.shape[0]
    tm = min(tm, S // 2)
    nt = S // tm
    hb = tm // POOL_HALO

    def body(dm_ref, halo_ref, dp_ref):
        i = pl.program_id(0)
        dm = dm_ref[...]
        halo = halo_ref[...] * (i < nt - 1).astype(F32)
        cnts = _pool_counts(tm, i)
        cnts_h = [c[:POOL_HALO] for c in _pool_counts(tm, i + 1)]
        r = jnp.concatenate([jnp.concatenate([dm[:, g * POOL_GD:(g + 1) * POOL_GD] / cnts[g] for g in range(4)], axis=-1),
                             jnp.concatenate([halo[:, g * POOL_GD:(g + 1) * POOL_GD] / cnts_h[g] for g in range(4)], axis=-1)],
                            axis=0)
        n = tm + POOL_HALO
        sums, acc, k = {}, r, 1
        while k < POOL_WINDOWS[-1]:
            acc = acc + pltpu.roll(acc, n - k, axis=0)
            k *= 2
            sums[k] = acc
        dp = [sums[w][:tm, g * POOL_GD:(g + 1) * POOL_GD] for g, w in enumerate(POOL_WINDOWS)]
        dp_ref[...] = (jnp.concatenate(dp, axis=-1) - dm).astype(dp_ref.dtype)

    return _pcall(
        body, name="pool_win_bwd", grid=(nt,),
        in_specs=[pl.BlockSpec((tm, POOL_W), lambda i: (i, 0)),
                  pl.BlockSpec((POOL_HALO, POOL_W), lambda i: (jnp.minimum((i + 1) * hb, S // POOL_HALO - 1), 0))],
        out_specs=pl.BlockSpec((tm, POOL_W), lambda i: (i, 0)),
        out_shape=_sds((S, POOL_W), BF16))(dmixed, dmixed)


def _xattn_probs(q, kv, h):
    hs = slice(h * XA_HD, (h + 1) * XA_HD)
    s = _dot(q[:, hs], kv[:, hs], _NT) * (XA_HD ** -0.5)
    s = s - jnp.max(s, axis=-1, keepdims=True)
    e = jnp.exp(s)
    return e / jnp.sum(e, axis=-1, keepdims=True)


def _xattn_fwd(proj, kv):
    S = proj.shape[0]

    def fn(q, kv):
        outs = []
        for h in range(XA_HEADS):
            p = _xattn_probs(q, kv, h)
            outs.append(_dot(p, kv[:, XA_W + h * XA_HD:XA_W + (h + 1) * XA_HD]))
        return jnp.concatenate(outs, axis=-1)

    return _rowwise("xattn_fwd", fn, [_Win(proj, XA_W, (3072 + 3072 + POOL_W) // XA_W)], [kv], [_sds((S, XA_W), BF16)])[0]


def _xattn_bwd(proj, kv, dxa):
    S = proj.shape[0]

    def fn(q, dxa, kv):
        dqs, dks, dvs = [], [], []
        for h in range(XA_HEADS):
            hs = slice(h * XA_HD, (h + 1) * XA_HD)
            vh = kv[:, XA_W + h * XA_HD:XA_W + (h + 1) * XA_HD]
            p = _xattn_probs(q, kv, h)
            dp = _dot(dxa[:, hs], vh, _NT)
            ds = p * (dp - jnp.sum(p * dp, axis=-1, keepdims=True)) * (XA_HD ** -0.5)
            dqs.append(_dot(ds, kv[:, hs]))
            dks.append(_dot(ds, q[:, hs], _TN))
            dvs.append(_dot(p, dxa[:, hs], _TN))
        return jnp.concatenate(dqs, axis=-1), jnp.concatenate(dks + dvs, axis=-1)

    return _rowwise("xattn_bwd", fn, [_Win(proj, XA_W, (3072 + 3072 + POOL_W) // XA_W), dxa], [kv],
                    [_sds((S, XA_W), BF16)], [_sds(kv.shape)])


def _mix_fwd(x1, mem, W):
    S = x1.shape[0]
    M = mem.shape[0]
    h = _rowwise("mix_pre", lambda x, g: _rms(x, g), [x1], [W["mix_pre_g"]], [_sds((S, D_MODEL), BF16)])[0]
    proj = _mm_nn("mix_proj", h, W["w_int"], tn=INT_BLK)
    o_raw, states = _gla_fwd(proj, W["w_fu_pad"], W["b_f"])
    ya_in = _rowwise("gla_out", _head_rms_gate, [o_raw, _Win(proj, GLA_DV, 2)], [W["gla_norm_g"]],
                     [_sds((S, GLA_DV), BF16)])[0]
    mixed, pool_out = _pool_fwd(proj, W["w_pool"], W["pool_scale"])
    mem_n = _rowwise("mem_norm", lambda m, g: _rms(m, g), [mem], [W["mem_norm_g"]], [_sds((M, D_MODEL), BF16)])[0]
    kv = _mm_nn("mem_kv", mem_n, W["w_mem_kv"])
    xa = _xattn_fwd(proj, kv)
    ya = _mm_nn("up_gla", ya_in, W["w_up_gla"])
    yb = _mm_nn("up_pool", pool_out, W["w_up_pool"])
    yc = _mm_nn("up_xattn", xa, W["w_up_xattn"])
    merged = _rowwise("merge", _merge, [_Win(proj, 3 * D_MODEL, 1), ya, yb, yc], [], [_sds((S, D_MODEL), BF16)])[0]
    y = _mm_nn("mix_out", merged, W["w_o"])
    x2 = _rowwise("mix_post", lambda x, y, g: x + _rms(y, g), [x1, y], [W["mix_post_g"]], [_sds((S, D_MODEL))])[0]
    return x2, (h, proj, o_raw, states, ya_in, mixed, pool_out, mem_n, kv, xa, ya, yb, yc, merged, y)


def _head(x3, tgt, final_g):
    S = x3.shape[0]

    def head(x3, t, g):
        out, vjp = jax.vjp(_rms, x3, g)
        e = out - t
        loss = 0.5 * jnp.sum(jnp.mean(e * e, axis=-1, keepdims=True), axis=0, keepdims=True)
        dx3, dg = vjp(e * (1.0 / D_MODEL))
        return dx3, jnp.broadcast_to(loss, (1, 128)), dg

    dx3, loss_acc, d_final_g = _rowwise("head", head, [x3, tgt], [final_g], [_sds((S, D_MODEL))],
                                        [_sds((1, 128)), _sds((1, D_MODEL))])
    return loss_acc[0, 0], dx3, d_final_g


def _mix_bwd(dx2, x1, mem, W, saved, on_grads=None):
    h, proj, o_raw, states, ya_in, mixed, pool_out, mem_n, kv, xa, ya, yb, yc, merged, y = saved
    S = x1.shape[0]

    def mix_post_bwd(y, dx2, g):
        return _rms_bwd(y, g, dx2)

    dy, d_mix_post_g = _rowwise("mix_post_bwd", mix_post_bwd, [y, dx2], [W["mix_post_g"]], [_sds((S, D_MODEL), BF16)],
                                [_sds((1, D_MODEL))])
    d_w_o = _mm_tn("d_w_o", merged, dy)
    dmerged = _mm_nt("d_merged", dy, W["w_o"])

    def merge_bwd(gates, ya, yb, yc, dm):
        _, vjp = jax.vjp(_merge, gates, ya, yb, yc)
        return vjp(dm)

    dgates, dya, dyb, dyc = _rowwise("merge_bwd", merge_bwd, [_Win(proj, 3 * D_MODEL, 1), ya, yb, yc, dmerged], [],
                                     [_sds((S, 3 * D_MODEL), BF16)] + [_sds((S, D_MODEL), BF16)] * 3)
    d_w_up_gla = _mm_tn("d_w_up_gla", ya_in, dya)
    d_w_up_pool = _mm_tn("d_w_up_pool", pool_out, dyb)
    d_w_up_xattn = _mm_tn("d_w_up_xattn", xa, dyc)
    d_ya_in = _mm_nt("d_ya_in", dya, W["w_up_gla"])
    d_pool_out = _mm_nt("d_pool_out", dyb, W["w_up_pool"])
    d_xa = _mm_nt("d_xa", dyc, W["w_up_xattn"])

    def gla_out_bwd(o, g_out, d, gn):
        _, vjp = jax.vjp(_head_rms_gate, o, g_out, gn)
        return vjp(d)

    do_raw, dg_out, d_gla_norm_g = _rowwise("gla_out_bwd", gla_out_bwd, [o_raw, _Win(proj, GLA_DV, 2), d_ya_in],
                                            [W["gla_norm_g"]], [_sds((S, GLA_DV)), _sds((S, GLA_DV), BF16)],
                                            [_sds((1, GLA_DV))])
    dq, dk, dv, dflow, d_wfu_pad, d_b_f = _gla_bwd(proj, W["w_fu_pad"], W["b_f"], states, do_raw)
    dmixed, d_w_pool, d_pool_scale = _pool_lin_bwd(d_pool_out, mixed, W["w_pool"], W["pool_scale"])
    dp_in = _pool_win_bwd(dmixed)
    dxq, dkv = _xattn_bwd(proj, kv, d_xa)
    d_w_mem_kv = _mm_tn("d_w_mem_kv", mem_n, dkv)
    dmem_n = _mm_nt("d_mem_n", dkv, W["w_mem_kv"])
    d_mem_norm_g = _rowwise("mem_norm_bwd", lambda m, d, g: _rms_bwd(m, g, d)[1], [mem, dmem_n], [W["mem_norm_g"]], [],
                            [_sds((1, D_MODEL))])[0]
    dproj = jnp.concatenate([dq, dk, dv, dg_out, dgates, dp_in, dxq, dflow], axis=-1)
    d_w_int = _mm_tn("d_w_int", h, dproj, tn=INT_BLK)
    grads = dict(
        w_int=d_w_int, w_fu_pad=d_wfu_pad, b_f=d_b_f, gla_norm_g=d_gla_norm_g, w_pool=d_w_pool,
        pool_scale=d_pool_scale, mem_norm_g=d_mem_norm_g, w_mem_kv=d_w_mem_kv, w_up_gla=d_w_up_gla,
        w_up_pool=d_w_up_pool, w_up_xattn=d_w_up_xattn, w_o=d_w_o, mix_post_g=d_mix_post_g)
    deps = [on_grads(grads)] if on_grads is not None else []
    dh = _mm(
        "d_mix_h", dproj, W["w_int"], grid=(1, S // min(512, S), INT_NBLK), a_blk=(min(512, S), INT_BLK),
        a_map=lambda j, i, k: (i, k), b_blk=(D_MODEL, INT_BLK), b_map=lambda j, i, k: (0, k), o_shape=(S, D_MODEL),
        o_blk=(min(512, S), D_MODEL), o_map=lambda j, i, k: (i, 0), dims=_NT, deps=deps)

    def mix_pre_bwd(x, dh, dx2, g):
        dx, dg = _rms_bwd(x, g, dh)
        return dx + dx2, dg

    dx1, grads["mix_pre_g"] = _rowwise("mix_pre_bwd", mix_pre_bwd, [x1, dh, dx2], [W["mix_pre_g"]],
                                       [_sds((S, D_MODEL))], [_sds((1, D_MODEL))])
    return dx1, grads


def _mesh_pos():
    x, y, c = lax.axis_index("x"), lax.axis_index("y"), lax.axis_index("c")
    return x, y, c, 4 * x + 2 * y + c


def _peer(x, y, c, r):
    px = 1 - x if r & 4 else x
    py = 1 - y if r & 2 else y
    pc = 1 - c if r & 1 else c
    return (px, py, pc), 4 * px + 2 * py + pc


def _peer_copies(srcs, lands, send_sems, recv_sems, scatter):
    x, y, c, me = _mesh_pos()
    cps = []
    for r in range(1, N_DEV):
        pos, peer = _peer(x, y, c, r)
        for a in range(len(srcs)):
            k = a * (N_DEV - 1) + r - 1
            cps.append(pltpu.make_async_remote_copy(
                src_ref=srcs[a].at[peer] if scatter else srcs[a], dst_ref=lands[a].at[me],
                send_sem=send_sems.at[k], recv_sem=recv_sems.at[k], device_id=pos,
                device_id_type=pl.DeviceIdType.MESH))
    return cps


def _exchange(name, arrays, scatter):
    n = len(arrays)

    def body(*refs):
        ins, outs = refs[:n], refs[n:2 * n]
        send_sems, recv_sems, local_sems = refs[2 * n:]
        me = _mesh_pos()[3]
        copies = [pltpu.make_async_copy(ins[a].at[me] if scatter else ins[a], outs[a].at[me], local_sems.at[a])
                  for a in range(n)]
        copies += _peer_copies(ins, outs, send_sems, recv_sems, scatter)
        for cp in copies:
            cp.start()
        for cp in copies:
            cp.wait()

    any_spec = pl.BlockSpec(memory_space=pl.ANY)
    out_shape = [_sds(a.shape if scatter else (N_DEV,) + a.shape, a.dtype) for a in arrays]
    return pl.pallas_call(
        body, name=name, in_specs=[any_spec] * n, out_specs=[any_spec] * n, out_shape=out_shape,
        scratch_shapes=[pltpu.SemaphoreType.DMA((n * (N_DEV - 1),)), pltpu.SemaphoreType.DMA((n * (N_DEV - 1),)),
                        pltpu.SemaphoreType.DMA((n,))])(*arrays)


_HBM = pl.BlockSpec(memory_space=pltpu.HBM)
_SEM = pl.BlockSpec(memory_space=pltpu.SEMAPHORE)
_EFFECT = pltpu.SideEffectType.DATAFLOW_SIDE_EFFECTING


def _own_copies(srcs, lands, own_sems, scatter):
    me = _mesh_pos()[3]
    return [pltpu.make_async_copy(srcs[a].at[me] if scatter else srcs[a], lands[a].at[me], own_sems.at[a])
            for a in range(len(srcs))]


def _exchange_start(name, arrays, scatter):
    n = len(arrays)
    lands = [lax.empty(a.shape if scatter else (N_DEV,) + a.shape, a.dtype) for a in arrays]
    n_sem = n * (N_DEV - 1)

    def body(*refs):
        srcs, lands_ = refs[:n], refs[n:2 * n]
        send_sems, recv_sems, own_sems = refs[2 * n:2 * n + 3]
        token = refs[-1]
        for cp in _peer_copies(srcs, lands_, send_sems, recv_sems, scatter) + _own_copies(srcs, lands_, own_sems, scatter):
            cp.start()
        token[...] = jnp.zeros_like(token)

    hbm = lambda a: pltpu.HBM(a.shape, a.dtype)
    res = pl.pallas_call(
        body, name=name,
        out_shape=(pltpu.SemaphoreType.DMA((n_sem,)), pltpu.SemaphoreType.DMA((n_sem,)), pltpu.SemaphoreType.DMA((n,)),
                   *[hbm(a) for a in arrays], *[hbm(a) for a in lands], _sds((8, 128))),
        in_specs=[_HBM] * (2 * n),
        out_specs=(_SEM, _SEM, _SEM, *[_HBM] * (2 * n), pl.BlockSpec(memory_space=pltpu.VMEM)),
        input_output_aliases={i: 3 + i for i in range(2 * n)},
        compiler_params=pltpu.CompilerParams(has_side_effects=_EFFECT),
    )(*[pltpu.with_memory_space_constraint(a, pltpu.HBM) for a in list(arrays) + lands])
    return (res[:3], res[3:3 + n], res[3 + n:3 + 2 * n], scatter), res[-1]


def _exchange_wait(name, state, after):
    sems, srcs, lands, scatter = state
    n = len(srcs)

    def body(*refs):
        srcs_, lands_ = refs[:n], refs[n:2 * n]
        send_sems, recv_sems, own_sems = refs[2 * n:2 * n + 3]
        for cp in _peer_copies(srcs_, lands_, send_sems, recv_sems, scatter):
            cp.wait_send()
            cp.wait_recv()
        for cp in _own_copies(srcs_, lands_, own_sems, scatter):
            cp.wait()

    hbm = lambda a: pltpu.HBM(a.shape, a.dtype)
    res = pl.pallas_call(
        body, name=name, out_shape=tuple(hbm(a) for a in list(srcs) + list(lands)),
        in_specs=[_HBM] * (2 * n) + [_SEM] * 3 + [pl.BlockSpec(memory_space=pl.ANY)] * len(after),
        out_specs=tuple([_HBM] * (2 * n)), input_output_aliases={i: i for i in range(2 * n)},
        compiler_params=pltpu.CompilerParams(has_side_effects=_EFFECT),
    )(*srcs, *lands, *sems, *after)
    return res[n:]


def _adam(name, recv, w, m, v):
    shape = w.shape
    R, C = shape
    tr = R
    while N_DEV * tr * C * 4 > 6 * 1024 * 1024 and tr % 32 == 0:
        tr //= 2

    def body(recv_ref, w_ref, m_ref, v_ref, g_ref, d_ref, m2_ref, v2_ref):
        g = recv_ref[0].astype(F32)
        for j in range(1, N_DEV):
            g = g + recv_ref[j].astype(F32)
        w_, m_, v_ = w_ref[...], m_ref[...], v_ref[...]
        m2 = ADAM_B1 * m_ + (1.0 - ADAM_B1) * g
        v2 = ADAM_B2 * v_ + (1.0 - ADAM_B2) * (g * g)
        m_hat = m2 / (1.0 - ADAM_B1 ** ADAM_STEP)
        v_hat = v2 / (1.0 - ADAM_B2 ** ADAM_STEP)
        g_ref[...] = g
        d_ref[...] = -ADAM_LR * (m_hat / (jnp.sqrt(v_hat) + ADAM_EPS) + ADAM_WD * w_)
        m2_ref[...] = m2
        v2_ref[...] = v2

    blk = pl.BlockSpec((tr, C), lambda i: (i, 0))
    return _pcall(body, name=name, grid=(R // tr,),
                  in_specs=[pl.BlockSpec((N_DEV, tr, C), lambda i: (0, i, 0)), blk, blk, blk],
                  out_specs=[blk] * 4, out_shape=[_sds(shape)] * 4)(recv, w, m, v)


_NAMES = ['ffn1_pre_g', 'ffn1_w_in', 'ffn1_w_out', 'ffn1_post_g', 'mix_pre_g', 'w_in', 'w_fu', 'b_f', 'gla_norm_g',
          'w_pool', 'pool_scale', 'mem_norm_g', 'w_mem_kv', 'w_up_gla', 'w_up_pool', 'w_up_xattn', 'w_o', 'mix_post_g',
          'ffn2_pre_g', 'ffn2_w_in', 'ffn2_w_out', 'ffn2_post_g', 'final_g']
_SHARDED = ['ffn1_w_in', 'ffn1_w_out', 'w_in', 'w_fu', 'w_mem_kv', 'w_up_gla', 'w_up_pool', 'w_up_xattn', 'w_o',
            'ffn2_w_in', 'ffn2_w_out']
_COL_SHARDED = ['w_up_pool', 'w_up_xattn']


def _cols_to_full(g):
    return jnp.transpose(g, (1, 0, 2)).reshape(g.shape[1], N_DEV * g.shape[2])


def _full_to_cols(f):
    R, C = f.shape
    return jnp.transpose(f.reshape(R, N_DEV, C // N_DEV), (1, 0, 2))


def _to_internal(w_in_full):
    o = 0
    parts = []
    for s in IN_SPLITS:
        parts.append(w_in_full[:, o:o + s])
        o += s
    q, k, v, g_out, f_low, p_in, xq, gates = parts
    f_low = jnp.pad(f_low, ((0, 0), (0, FLOW_W - GATE_RANK)))
    return jnp.concatenate([q, k, v, g_out, gates, p_in, xq, f_low], axis=-1)


def _from_internal(d):
    q, k, v, g_out = d[:, 0:512], d[:, 512:1024], d[:, 1024:2048], d[:, 2048:3072]
    gates, p_in, xq = d[:, 3072:6144], d[:, 6144:6656], d[:, 6656:7168]
    f_low = d[:, 7168:7168 + GATE_RANK]
    return jnp.concatenate([q, k, v, g_out, f_low, p_in, xq, gates], axis=-1)


_GROUPS = {"ffn1": ['ffn1_w_in', 'ffn1_w_out'],
           "mix": ['w_in', 'w_fu', 'w_mem_kv', 'w_up_gla', 'w_up_pool', 'w_up_xattn', 'w_o'],
           "ffn2": ['ffn2_w_in', 'ffn2_w_out']}


def _rows_to_full(g):
    return g.reshape(N_DEV * g.shape[1], g.shape[2])


def _mix_weights(gathered):
    W = {n: _rows_to_full(gathered[n]) for n in ('w_mem_kv', 'w_up_gla', 'w_o')}
    for n in _COL_SHARDED:
        W[n] = _cols_to_full(gathered[n])
    W["w_int"] = _to_internal(_cols_to_full(gathered["w_in"]))
    W["w_fu_pad"] = jnp.pad(_cols_to_full(gathered["w_fu"]), ((0, FLOW_W - GATE_RANK), (0, 0)))
    return W


def _mix_chunks(G, n):
    if n == "w_in":
        return _full_to_cols(_from_internal(G["w_int"]))
    if n == "w_fu":
        return _full_to_cols(G["w_fu_pad"][:GATE_RANK].astype(BF16))
    if n in _COL_SHARDED:
        return _full_to_cols(G[n])
    return G[n].reshape(N_DEV, G[n].shape[0] // N_DEV, G[n].shape[1])


def _step(x, mem, tgt, P, Mo, Vo):
    gather, tokens = {}, {}
    for grp, names in (("ffn1_in", ["ffn1_w_in"]), ("ffn1_out", ["ffn1_w_out"]), ("mix", _GROUPS["mix"]),
                       ("ffn2", _GROUPS["ffn2"])):
        gather[grp], tokens[grp] = _exchange_start("gather_" + grp, [P[n].astype(BF16) for n in names], scatter=False)
    small = {n: P[n] for n in _NAMES if n not in _SHARDED}

    w1in = _exchange_wait("gather_ffn1_in_wait", gather["ffn1_in"], list(tokens.values())[1:])[0]
    def wait_w1out(act):
        return _rows_to_full(_exchange_wait("gather_ffn1_out_wait", gather["ffn1_out"], [act])[0])

    x1, w1out, sv1 = _ffn_fwd("ffn1", x, small["ffn1_pre_g"], w1in, small["ffn1_post_g"], wait_w1out)

    gm = dict(zip(_GROUPS["mix"], _exchange_wait("gather_mix_wait", gather["mix"], [x1])))
    Wm = {**small, **_mix_weights(gm)}
    x2, svm = _mix_fwd(x1, mem, Wm)

    w2in, w2out = _exchange_wait("gather_ffn2_wait", gather["ffn2"], [x2])
    x3, w2out, sv2 = _ffn_fwd("ffn2", x2, small["ffn2_pre_g"], w2in, small["ffn2_post_g"],
                              lambda act: _rows_to_full(w2out))

    loss, dx3, d_final_g = _head(x3, tgt, small["final_g"])

    G = dict(final_g=d_final_g)
    scat = {}

    def start_ffn(grp):
        def on_grads(dwg, dwout):
            scat[grp], tok = _exchange_start("scatter_" + grp, [dwg, dwout.reshape(N_DEV, -1, D_MODEL)], scatter=True)
            return tok
        return on_grads

    def start_mix(Gm):
        scat["mix"], tok = _exchange_start("scatter_mix", [_mix_chunks(Gm, n) for n in _GROUPS["mix"]], scatter=True)
        return tok

    dx2, _, _, G["ffn2_pre_g"], G["ffn2_post_g"] = _ffn_bwd(
        "ffn2", dx3, x2, small["ffn2_pre_g"], w2in, w2out, small["ffn2_post_g"], sv2, start_ffn("ffn2"))
    dx1, Gm = _mix_bwd(dx2, x1, mem, Wm, svm, start_mix)
    G.update(Gm)
    dx, _, _, G["ffn1_pre_g"], G["ffn1_post_g"] = _ffn_bwd(
        "ffn1", dx1, x, small["ffn1_pre_g"], w1in, w1out, small["ffn1_post_g"], sv1, start_ffn("ffn1"))

    recv = dict(zip(small, _exchange("gather_small_grads", [G[n] for n in small], scatter=False)))
    grads, deltas, new_m, new_v = {}, {}, {}, {}
    last = dx
    for grp, state in (("small", None), ("ffn2", scat["ffn2"]), ("mix", scat["mix"]), ("ffn1", scat["ffn1"])):
        names = list(small) if state is None else _GROUPS[grp]
        if state is not None:
            recv.update(zip(names, _exchange_wait("scatter_" + grp + "_wait", state, [last])))
        for n in names:
            shp = P[n].shape
            shp2 = shp if len(shp) == 2 else (shp[0] * shp[1], shp[2])
            g, d, m2, v2 = _adam("adam_" + n, recv[n].reshape((N_DEV,) + shp2), P[n].reshape(shp2),
                                 Mo[n].reshape(shp2), Vo[n].reshape(shp2))
            grads[n], deltas[n], new_m[n], new_v[n] = (t.reshape((1,) + shp) for t in (g, d, m2, v2))
            last = v2
    return loss, dx, grads, deltas, new_m, new_v


def kernel(x, mem, ffn1_pre_g, ffn1_w_in, ffn1_w_out, ffn1_post_g, mix_pre_g, w_in, w_fu, b_f, gla_norm_g, w_pool, pool_scale, mem_norm_g, w_mem_kv, w_up_gla, w_up_pool, w_up_xattn, w_o, mix_post_g, ffn2_pre_g, ffn2_w_in, ffn2_w_out, ffn2_post_g, final_g, loss_target, m_ffn1_pre_g, m_ffn1_w_in, m_ffn1_w_out, m_ffn1_post_g, m_mix_pre_g, m_w_in, m_w_fu, m_b_f, m_gla_norm_g, m_w_pool, m_pool_scale, m_mem_norm_g, m_w_mem_kv, m_w_up_gla, m_w_up_pool, m_w_up_xattn, m_w_o, m_mix_post_g, m_ffn2_pre_g, m_ffn2_w_in, m_ffn2_w_out, m_ffn2_post_g, m_final_g, v_ffn1_pre_g, v_ffn1_w_in, v_ffn1_w_out, v_ffn1_post_g, v_mix_pre_g, v_w_in, v_w_fu, v_b_f, v_gla_norm_g, v_w_pool, v_pool_scale, v_mem_norm_g, v_w_mem_kv, v_w_up_gla, v_w_up_pool, v_w_up_xattn, v_w_o, v_mix_post_g, v_ffn2_pre_g, v_ffn2_w_in, v_ffn2_w_out, v_ffn2_post_g, v_final_g):
    params = [ffn1_pre_g, ffn1_w_in, ffn1_w_out, ffn1_post_g, mix_pre_g, w_in, w_fu, b_f, gla_norm_g, w_pool, pool_scale, mem_norm_g, w_mem_kv, w_up_gla, w_up_pool, w_up_xattn, w_o, mix_post_g, ffn2_pre_g, ffn2_w_in, ffn2_w_out, ffn2_post_g, final_g]
    moms = [m_ffn1_pre_g, m_ffn1_w_in, m_ffn1_w_out, m_ffn1_post_g, m_mix_pre_g, m_w_in, m_w_fu, m_b_f, m_gla_norm_g, m_w_pool, m_pool_scale, m_mem_norm_g, m_w_mem_kv, m_w_up_gla, m_w_up_pool, m_w_up_xattn, m_w_o, m_mix_post_g, m_ffn2_pre_g, m_ffn2_w_in, m_ffn2_w_out, m_ffn2_post_g, m_final_g]
    vars_ = [v_ffn1_pre_g, v_ffn1_w_in, v_ffn1_w_out, v_ffn1_post_g, v_mix_pre_g, v_w_in, v_w_fu, v_b_f, v_gla_norm_g, v_w_pool, v_pool_scale, v_mem_norm_g, v_w_mem_kv, v_w_up_gla, v_w_up_pool, v_w_up_xattn, v_w_o, v_mix_post_g, v_ffn2_pre_g, v_ffn2_w_in, v_ffn2_w_out, v_ffn2_post_g, v_final_g]
    P = {n: a[0] if a.ndim > 2 else a for n, a in zip(_NAMES, params)}
    Mo = {n: a[0] if a.ndim > 2 else a for n, a in zip(_NAMES, moms)}
    Vo = {n: a[0] if a.ndim > 2 else a for n, a in zip(_NAMES, vars_)}
    loss, dx, grads, deltas, new_m, new_v = _step(x[0], mem[0], loss_target[0], P, Mo, Vo)
    loss = lax.psum(loss, ("x", "y", "c"))
    out = [loss, dx[None]]
    for d in (grads, deltas, new_m, new_v):
        for n in _NAMES:
            out.append(d[n].reshape(params[_NAMES.index(n)].shape))
    return tuple(out)
```

```python
import functools

import jax
import jax.numpy as jnp
from jax import lax
from jax.experimental import pallas as pl
from jax.experimental.pallas import tpu as pltpu

F32 = jnp.float32
BF16 = jnp.bfloat16

N_DEV = 8
D_MODEL = 1024
D_FF = 2816
FF_BLK = 2 * D_FF // N_DEV
N_FF_BLK = D_FF // FF_BLK
CHUNK = 64
GLA_HEADS = 4
GLA_DK = 512
GLA_DV = 1024
GLA_HDK = 128
GLA_HDV = 256
GATE_RANK = 16
GATE_TEMP = 16.0
POOL_WINDOWS = (2, 4, 8, 16)
POOL_W = 512
POOL_GD = 128
POOL_HALO = 16
XA_HEADS = 4
XA_HD = 128
XA_W = 512
EPS = 1e-6
IN_SPLITS = (GLA_DK, GLA_DK, GLA_DV, GLA_DV, GATE_RANK, POOL_W, XA_W, 3 * D_MODEL)
IN_WIDTH = sum(IN_SPLITS)
IN_SHARD = IN_WIDTH // N_DEV
INT_W = 3072 + 3072 + 1024 + 128
INT_NBLK = 3
INT_BLK = INT_W // INT_NBLK
FLOW_W = 128

ADAM_LR = 0.001
ADAM_B1 = 0.9
ADAM_B2 = 0.999
ADAM_EPS = 1e-08
ADAM_WD = 0.01
ADAM_STEP = 10

VMEM_LIMIT = 56 * 1024 * 1024

_NN = (((1,), (0,)), ((), ()))
_NT = (((1,), (1,)), ((), ()))
_TN = (((0,), (0,)), ((), ()))


def _pcall(body, *, name, grid, in_specs, out_specs, out_shape, scratch=()):
    return pl.pallas_call(
        body, name=name, grid=grid, in_specs=in_specs, out_specs=out_specs, out_shape=out_shape,
        scratch_shapes=list(scratch),
        compiler_params=pltpu.CompilerParams(dimension_semantics=("arbitrary",) * len(grid),
                                             vmem_limit_bytes=VMEM_LIMIT))


def _dot(a, b, dims=_NN):
    return lax.dot_general(a.astype(BF16), b.astype(BF16), dims, preferred_element_type=F32)


def _mm(name, a, b, *, grid, a_blk, a_map, b_blk, b_map, o_shape, o_blk, o_map, dims, out_dtype=F32, deps=()):
    nk = grid[2]

    def body(a_ref, b_ref, *rest):
        o_ref, scr = rest[len(deps)], rest[len(deps) + 1:]
        p = _dot(a_ref[...], b_ref[...], dims)
        if nk == 1:
            o_ref[...] = p.astype(o_ref.dtype)
        else:
            acc = scr[0]
            k = pl.program_id(2)

            @pl.when(k == 0)
            def _():
                acc[...] = p

            @pl.when(k > 0)
            def _():
                acc[...] += p

            @pl.when(k == nk - 1)
            def _():
                o_ref[...] = acc[...].astype(o_ref.dtype)

    acc_shape = tuple(d for d in o_blk if d is not None)
    return _pcall(body, name=name, grid=grid,
                  in_specs=[pl.BlockSpec(a_blk, a_map), pl.BlockSpec(b_blk, b_map)]
                  + [pl.BlockSpec(memory_space=pl.ANY)] * len(deps),
                  out_specs=pl.BlockSpec(o_blk, o_map),
                  out_shape=jax.ShapeDtypeStruct(o_shape, out_dtype),
                  scratch=[pltpu.VMEM(acc_shape, F32)] if nk > 1 else [])(a, b, *deps)


def _mm_nn(name, a, b, out_dtype=F32, tm=512, tn=None):
    M, K = a.shape
    N = b.shape[1]
    tm, tn = min(tm, M), (tn or N)
    return _mm(name, a, b, grid=(N // tn, M // tm, 1), a_blk=(tm, K), a_map=lambda j, i, k: (i, 0),
               b_blk=(K, tn), b_map=lambda j, i, k: (0, j), o_shape=(M, N), o_blk=(tm, tn),
               o_map=lambda j, i, k: (i, j), dims=_NN, out_dtype=out_dtype)


def _mm_nt(name, a, b, out_dtype=F32, tm=512):
    M, K = a.shape
    N = b.shape[0]
    tm = min(tm, M)
    return _mm(name, a, b, grid=(1, M // tm, 1), a_blk=(tm, K), a_map=lambda j, i, k: (i, 0),
               b_blk=(N, K), b_map=lambda j, i, k: (0, 0), o_shape=(M, N), o_blk=(tm, N),
               o_map=lambda j, i, k: (i, 0), dims=_NT, out_dtype=out_dtype)


def _mm_tn(name, a, b, out_dtype=BF16, ts=512, tn=None):
    S, M = a.shape
    N = b.shape[1]
    ts, tn = min(ts, S), (tn or N)
    return _mm(name, a, b, grid=(N // tn, 1, S // ts), a_blk=(ts, M), a_map=lambda j, i, k: (k, 0),
               b_blk=(ts, tn), b_map=lambda j, i, k: (k, j), o_shape=(M, N), o_blk=(M, tn),
               o_map=lambda j, i, k: (0, j), dims=_TN, out_dtype=out_dtype)


class _Win:
    def __init__(self, arr, w, c):
        self.arr, self.w, self.c = arr, w, c


def _row_spec(x, tm):
    if isinstance(x, _Win):
        return x.arr, pl.BlockSpec((tm, x.w), functools.partial(lambda i, c: (i, c), c=x.c))
    if x.ndim == 3:
        return x, pl.BlockSpec((x.shape[0], tm, x.shape[2]), lambda i: (0, i, 0))
    return x, pl.BlockSpec((tm, x.shape[1]), lambda i: (i, 0))


def _rowwise(name, fn, rows, consts, outs, accs=(), tm=256):
    first = rows[0].arr if isinstance(rows[0], _Win) else rows[0]
    S = first.shape[1] if first.ndim == 3 else first.shape[0]
    tm = min(tm, S)
    n_in, n_out = len(rows) + len(consts), len(outs)
    arrays, in_specs = [], []
    for r in rows:
        arr, spec = _row_spec(r, tm)
        arrays.append(arr)
        in_specs.append(spec)
    for c in consts:
        arrays.append(c)
        in_specs.append(pl.BlockSpec(c.shape, functools.partial(lambda i, n: (0,) * n, n=c.ndim)))
    out_specs = [_row_spec(o, tm)[1] for o in outs]
    out_specs += [pl.BlockSpec(a.shape, functools.partial(lambda i, n: (0,) * n, n=len(a.shape))) for a in accs]

    def body(*refs):
        res = fn(*[r[...] for r in refs[:n_in]])
        if not isinstance(res, (tuple, list)):
            res = (res,)
        orefs = refs[n_in:]
        for r, v in zip(orefs[:n_out], res[:n_out]):
            r[...] = v.astype(r.dtype)
        i = pl.program_id(0)
        for r, v in zip(orefs[n_out:], res[n_out:]):
            @pl.when(i == 0)
            def _(r=r, v=v):
                r[...] = v.astype(r.dtype)

            @pl.when(i > 0)
            def _(r=r, v=v):
                r[...] += v.astype(r.dtype)

    res = _pcall(body, name=name, grid=(S // tm,), in_specs=in_specs, out_specs=out_specs,
                 out_shape=[jax.ShapeDtypeStruct(o.shape, o.dtype) for o in list(outs) + list(accs)])(*arrays)
    return res


def _sds(shape, dtype=F32):
    return jax.ShapeDtypeStruct(shape, dtype)


def _rms(x, g):
    return x * lax.rsqrt(jnp.mean(x * x, axis=-1, keepdims=True) + EPS) * g


def _rms_bwd(x, g, dy):
    _, vjp = jax.vjp(_rms, x, g)
    return vjp(dy)


def _sigmoid(x):
    return 0.5 * jnp.tanh(0.5 * x) + 0.5


def _silu(x):
    return x * _sigmoid(x)


def _log_sigmoid(f):
    return jnp.minimum(f, 0.0) - jnp.log(1.0 + jnp.exp(-jnp.abs(f)))


def _head_rms_gate(o, g_out, gn):
    parts = [_rms(o[:, h * GLA_HDV:(h + 1) * GLA_HDV], gn[:, h * GLA_HDV:(h + 1) * GLA_HDV]) for h in range(GLA_HEADS)]
    return jnp.concatenate(parts, axis=-1) * _silu(g_out)


def _merge(gates, ya, yb, yc):
    return (_sigmoid(gates[:, :D_MODEL]) * ya + _sigmoid(gates[:, D_MODEL:2 * D_MODEL]) * yb
            + _sigmoid(gates[:, 2 * D_MODEL:]) * yc)


def _tri_dot(t, x):
    hi = x.astype(BF16)
    r1 = x - hi.astype(F32)
    mid = r1.astype(BF16)
    lo = (r1 - mid.astype(F32)).astype(BF16)
    d = functools.partial(lax.dot_general, dimension_numbers=_NN, preferred_element_type=F32)
    return d(t, hi) + d(t, mid) + d(t, lo)


def _ffn_fwd(tag, x, gpre, wg, gpost, get_wout):
    S = x.shape[0]
    h = _rowwise(tag + "_pre", lambda x, g: _rms(x, g), [x], [gpre], [_sds((S, D_MODEL), BF16)])[0]
    tm = min(512, S)

    def up_body(h_ref, w_ref, u_ref, act_ref):
        hh = h_ref[...]
        a, b = _dot(hh, w_ref[0]), _dot(hh, w_ref[1])
        u_ref[0] = a.astype(BF16)
        u_ref[1] = b.astype(BF16)
        act_ref[...] = (_silu(a) * b).astype(BF16)

    u, act = _pcall(
        up_body, name=tag + "_up", grid=(N_FF_BLK, S // tm),
        in_specs=[pl.BlockSpec((tm, D_MODEL), lambda j, i: (i, 0)),
                  pl.BlockSpec((2, None, D_MODEL, FF_BLK), lambda j, i: (0, j, 0, 0))],
        out_specs=[pl.BlockSpec((2, None, tm, FF_BLK), lambda j, i: (0, j, i, 0)),
                   pl.BlockSpec((None, tm, FF_BLK), lambda j, i: (j, i, 0))],
        out_shape=[_sds((2, N_FF_BLK, S, FF_BLK), BF16), _sds((N_FF_BLK, S, FF_BLK), BF16)],
    )(h, wg.reshape(2, N_FF_BLK, D_MODEL, FF_BLK))

    def down_body(act_ref, w_ref, x_ref, g_ref, f_ref, xo_ref):
        f = _dot(act_ref[0], w_ref[0])
        for k in range(1, N_FF_BLK):
            f = f + _dot(act_ref[k], w_ref[k])
        f_ref[...] = f
        xo_ref[...] = x_ref[...] + 0.5 * _rms(f, g_ref[...])

    wout = get_wout(act)
    row = pl.BlockSpec((tm, D_MODEL), lambda i: (i, 0))
    f, xo = _pcall(
        down_body, name=tag + "_down", grid=(S // tm,),
        in_specs=[pl.BlockSpec((N_FF_BLK, tm, FF_BLK), lambda i: (0, i, 0)),
                  pl.BlockSpec((N_FF_BLK, FF_BLK, D_MODEL), lambda i: (0, 0, 0)), row,
                  pl.BlockSpec((1, D_MODEL), lambda i: (0, 0))],
        out_specs=[row, row], out_shape=[_sds((S, D_MODEL)), _sds((S, D_MODEL))],
    )(act, wout.reshape(N_FF_BLK, FF_BLK, D_MODEL), x, gpost)
    return xo, wout, (h, u, act, f)


def _ffn_bwd(tag, dxo, x, gpre, wg, wout, gpost, saved, on_grads=None):
    h, u, act, f = saved
    S = x.shape[0]
    tm = min(512, S)

    def post_bwd(f, dxo, g):
        df, dg = _rms_bwd(f, g, 0.5 * dxo)
        return df, dg

    df, dgpost = _rowwise(tag + "_post_bwd", post_bwd, [f, dxo], [gpost], [_sds((S, D_MODEL), BF16)],
                          [_sds((1, D_MODEL))])
    ts = min(2048, S)
    dwout = _mm(tag + "_dwout", act, df, grid=(N_FF_BLK, 1, S // ts), a_blk=(None, ts, FF_BLK),
                a_map=lambda j, i, k: (j, k, 0), b_blk=(ts, D_MODEL), b_map=lambda j, i, k: (k, 0),
                o_shape=(D_FF, D_MODEL), o_blk=(FF_BLK, D_MODEL), o_map=lambda j, i, k: (j, 0), dims=_TN,
                out_dtype=BF16)

    def dact_body(df_ref, w_ref, u_ref, du_ref):
        dact = _dot(df_ref[...], w_ref[...], _NT)
        a, b = u_ref[0].astype(F32), u_ref[1].astype(F32)
        sg = _sigmoid(a)
        du_ref[0] = (dact * b * (sg * (1.0 + a * (1.0 - sg)))).astype(BF16)
        du_ref[1] = (dact * (a * sg)).astype(BF16)

    u_spec = pl.BlockSpec((2, None, tm, FF_BLK), lambda j, i: (0, j, i, 0))
    du = _pcall(
        dact_body, name=tag + "_dact", grid=(N_FF_BLK, S // tm),
        in_specs=[pl.BlockSpec((tm, D_MODEL), lambda j, i: (i, 0)),
                  pl.BlockSpec((None, FF_BLK, D_MODEL), lambda j, i: (j, 0, 0)), u_spec],
        out_specs=u_spec, out_shape=_sds((2, N_FF_BLK, S, FF_BLK), BF16),
    )(df, wout.reshape(N_FF_BLK, FF_BLK, D_MODEL), u).reshape(N_DEV, S, FF_BLK)
    dwg = _mm(tag + "_dwin", h, du, grid=(N_DEV, 1, S // ts), a_blk=(ts, D_MODEL), a_map=lambda j, i, k: (k, 0),
              b_blk=(None, ts, FF_BLK), b_map=lambda j, i, k: (j, k, 0), o_shape=(N_DEV, D_MODEL, FF_BLK),
              o_blk=(None, D_MODEL, FF_BLK), o_map=lambda j, i, k: (j, 0, 0), dims=_TN, out_dtype=BF16)
    deps = [on_grads(dwg, dwout)] if on_grads is not None else []

    def dh_body(du_ref, w_ref, x_ref, dxo_ref, g_ref, *rest):
        dx_ref, dg_ref, acc = rest[len(deps):]
        i, k = pl.program_id(0), pl.program_id(1)
        p = _dot(du_ref[...], w_ref[...], _NT)

        @pl.when(k == 0)
        def _():
            acc[...] = p

        @pl.when(k > 0)
        def _():
            acc[...] += p

        @pl.when(k == N_DEV - 1)
        def _():
            dx, dg = _rms_bwd(x_ref[...], g_ref[...], acc[...])
            dx_ref[...] = dx + dxo_ref[...]

            @pl.when(i == 0)
            def _():
                dg_ref[...] = dg

            @pl.when(i > 0)
            def _():
                dg_ref[...] += dg

    row = pl.BlockSpec((tm, D_MODEL), lambda i, k: (i, 0))
    vec = pl.BlockSpec((1, D_MODEL), lambda i, k: (0, 0))
    dx, dgpre = _pcall(
        dh_body, name=tag + "_dh", grid=(S // tm, N_DEV),
        in_specs=[pl.BlockSpec((None, tm, FF_BLK), lambda i, k: (k, i, 0)),
                  pl.BlockSpec((None, D_MODEL, FF_BLK), lambda i, k: (k, 0, 0)), row, row, vec]
        + [pl.BlockSpec(memory_space=pl.ANY)] * len(deps),
        out_specs=[row, vec], out_shape=[_sds((S, D_MODEL)), _sds((1, D_MODEL))],
        scratch=[pltpu.VMEM((tm, D_MODEL), F32)])(du, wg, x, dxo, gpre, *deps)
    return dx, dwg, dwout, dgpre, dgpost


def _gla_chunk_common(k, flow, wfu, bf, tri):
    f = _dot(flow, wfu) + bf
    la = _log_sigmoid(f) * (1.0 / GATE_TEMP)
    b = _tri_dot(tri, la)
    b_end = b[CHUNK - 1:CHUNK, :]
    w = jnp.exp(b_end - b)
    kt = k * w
    d = jnp.exp(b_end)
    return f, w, kt, d


def _tri_matrix(lower):
    r = lax.broadcasted_iota(jnp.int32, (CHUNK, CHUNK), 0)
    c = lax.broadcasted_iota(jnp.int32, (CHUNK, CHUNK), 1)
    return jnp.where((r >= c) if lower else (r <= c), 1.0, 0.0).astype(BF16)


def _gla_fwd(proj, wfu, bf):
    S = proj.shape[0]
    nc = S // CHUNK

    def body(q_ref, k_ref, v_ref, flow_ref, wfu_ref, bf_ref, o_ref, st_ref, state):
        c = pl.program_id(0)

        @pl.when(c == 0)
        def _():
            state[...] = jnp.zeros_like(state)

        _, _, kt, d = _gla_chunk_common(k_ref[...], flow_ref[...], wfu_ref[...], bf_ref[...], _tri_matrix(True))
        q = q_ref[...] * (GLA_HDK ** -0.5)
        v = v_ref[...]
        outs = []
        for h in range(GLA_HEADS):
            ks, vs = slice(h * GLA_HDK, (h + 1) * GLA_HDK), slice(h * GLA_HDV, (h + 1) * GLA_HDV)
            st = state[vs, :] * d[:, ks] + _dot(v[:, vs], kt[:, ks], _TN)
            state[vs, :] = st
            outs.append(_dot(q[:, ks], st, _NT))
        o_ref[...] = jnp.concatenate(outs, axis=-1)
        st_ref[...] = state[...]

    return _pcall(
        body, name="gla_fwd", grid=(nc,),
        in_specs=[pl.BlockSpec((CHUNK, GLA_DK), lambda c: (c, 0)), pl.BlockSpec((CHUNK, GLA_DK), lambda c: (c, 1)),
                  pl.BlockSpec((CHUNK, GLA_DV), lambda c: (c, 1)),
                  pl.BlockSpec((CHUNK, FLOW_W), lambda c: (c, (INT_W - FLOW_W) // FLOW_W)),
                  pl.BlockSpec(wfu.shape, lambda c: (0, 0)), pl.BlockSpec(bf.shape, lambda c: (0, 0))],
        out_specs=[pl.BlockSpec((CHUNK, GLA_DV), lambda c: (c, 0)),
                   pl.BlockSpec((None, GLA_DV, GLA_HDK), lambda c: (c, 0, 0))],
        out_shape=[_sds((S, GLA_DV)), _sds((nc, GLA_DV, GLA_HDK))],
        scratch=[pltpu.VMEM((GLA_DV, GLA_HDK), F32)])(proj, proj, proj, proj, wfu, bf)


def _gla_bwd(proj, wfu, bf, states, do):
    S = proj.shape[0]
    nc = S // CHUNK

    def body(q_ref, k_ref, v_ref, flow_ref, wfu_ref, bf_ref, st_ref, stp_ref, do_ref,
             dq_ref, dk_ref, dv_ref, dflow_ref, dwfu_ref, dbf_ref, dstate):
        step = pl.program_id(0)
        c = nc - 1 - step

        @pl.when(step == 0)
        def _():
            dstate[...] = jnp.zeros_like(dstate)
            dwfu_ref[...] = jnp.zeros_like(dwfu_ref)
            dbf_ref[...] = jnp.zeros_like(dbf_ref)

        k, flow, wfu_v = k_ref[...], flow_ref[...], wfu_ref[...]
        f, w, kt, d = _gla_chunk_common(k, flow, wfu_v, bf_ref[...], _tri_matrix(True))
        q = q_ref[...] * (GLA_HDK ** -0.5)
        v, dout = v_ref[...], do_ref[...]
        has_prev = (c > 0).astype(F32)
        dqs, dkts, dvs, dds = [], [], [], []
        for h in range(GLA_HEADS):
            ks, vs = slice(h * GLA_HDK, (h + 1) * GLA_HDK), slice(h * GLA_HDV, (h + 1) * GLA_HDV)
            st = st_ref[vs, :]
            dqs.append(_dot(dout[:, vs], st))
            dst = dstate[vs, :] + _dot(dout[:, vs], q[:, ks], _TN)
            dkts.append(_dot(v[:, vs], dst))
            dvs.append(_dot(kt[:, ks], dst, _NT))
            dds.append(jnp.sum(dst * stp_ref[vs, :], axis=0, keepdims=True) * has_prev)
            dstate[vs, :] = dst * d[:, ks]
        dq_ref[...] = (jnp.concatenate(dqs, axis=-1) * (GLA_HDK ** -0.5)).astype(dq_ref.dtype)
        dv_ref[...] = jnp.concatenate(dvs, axis=-1).astype(dv_ref.dtype)
        dkt = jnp.concatenate(dkts, axis=-1)
        dd = jnp.concatenate(dds, axis=-1)
        dk_ref[...] = (dkt * w).astype(dk_ref.dtype)
        de = dkt * kt
        db_end = jnp.sum(de, axis=0, keepdims=True) + dd * d
        dla = db_end - _tri_dot(_tri_matrix(False), de)
        df = dla * (1.0 - _sigmoid(f)) * (1.0 / GATE_TEMP)
        dflow_ref[...] = _dot(df, wfu_v, _NT).astype(dflow_ref.dtype)
        dwfu_ref[...] += _dot(flow, df, _TN)
        dbf_ref[...] += jnp.sum(df, axis=0, keepdims=True)

    rc = lambda s: nc - 1 - s
    return _pcall(
        body, name="gla_bwd", grid=(nc,),
        in_specs=[pl.BlockSpec((CHUNK, GLA_DK), lambda s: (rc(s), 0)), pl.BlockSpec((CHUNK, GLA_DK), lambda s: (rc(s), 1)),
                  pl.BlockSpec((CHUNK, GLA_DV), lambda s: (rc(s), 1)),
                  pl.BlockSpec((CHUNK, FLOW_W), lambda s: (rc(s), (INT_W - FLOW_W) // FLOW_W)),
                  pl.BlockSpec(wfu.shape, lambda s: (0, 0)), pl.BlockSpec(bf.shape, lambda s: (0, 0)),
                  pl.BlockSpec((None, GLA_DV, GLA_HDK), lambda s: (rc(s), 0, 0)),
                  pl.BlockSpec((None, GLA_DV, GLA_HDK), lambda s: (jnp.maximum(rc(s) - 1, 0), 0, 0)),
                  pl.BlockSpec((CHUNK, GLA_DV), lambda s: (rc(s), 0))],
        out_specs=[pl.BlockSpec((CHUNK, GLA_DK), lambda s: (rc(s), 0)), pl.BlockSpec((CHUNK, GLA_DK), lambda s: (rc(s), 0)),
                   pl.BlockSpec((CHUNK, GLA_DV), lambda s: (rc(s), 0)), pl.BlockSpec((CHUNK, FLOW_W), lambda s: (rc(s), 0)),
                   pl.BlockSpec(wfu.shape, lambda s: (0, 0)), pl.BlockSpec(bf.shape, lambda s: (0, 0))],
        out_shape=[_sds((S, GLA_DK), BF16), _sds((S, GLA_DK), BF16), _sds((S, GLA_DV), BF16), _sds((S, FLOW_W), BF16),
                   _sds(wfu.shape), _sds(bf.shape)],
        scratch=[pltpu.VMEM((GLA_DV, GLA_HDK), F32)])(proj, proj, proj, proj, wfu, bf, states, states, do)


def _pool_counts(tm, i):
    t = (lax.broadcasted_iota(jnp.int32, (tm, POOL_GD), 0) + i * tm + 1).astype(F32)
    return [jnp.minimum(t, float(w)) for w in POOL_WINDOWS]


def _pool_fwd(proj, w_pool, pool_scale, tm=256):
    S = proj.shape[0]
    tm = min(tm, S // 2)
    col = (3072 + 3072) // POOL_W
    hb = tm // POOL_HALO

    def body(p_ref, halo_ref, wp_ref, sc_ref, mixed_ref, out_ref):
        i = pl.program_id(0)
        p = p_ref[...]
        halo = halo_ref[...] * (i > 0).astype(F32)
        ext = jnp.concatenate([halo, p], axis=0)
        n = tm + POOL_HALO
        sums, acc, k = {}, ext, 1
        while k < POOL_WINDOWS[-1]:
            acc = acc + pltpu.roll(acc, k, axis=0)
            k *= 2
            sums[k] = acc
        cnts = _pool_counts(tm, i)
        mixed, lin = [], []
        for g, w in enumerate(POOL_WINDOWS):
            ls = slice(g * POOL_GD, (g + 1) * POOL_GD)
            m = sums[w][POOL_HALO:n, ls] / cnts[g] - p[:, ls]
            mixed.append(m)
            lin.append(_dot(m, wp_ref[g]))
        mixed_ref[...] = jnp.concatenate(mixed, axis=-1)
        out_ref[...] = (jnp.concatenate(lin, axis=-1) * sc_ref[...]).astype(out_ref.dtype)

    return _pcall(
        body, name="pool_fwd", grid=(S // tm,),
        in_specs=[pl.BlockSpec((tm, POOL_W), lambda i: (i, col)),
                  pl.BlockSpec((POOL_HALO, POOL_W), lambda i: (jnp.maximum(i * hb - 1, 0), col)),
                  pl.BlockSpec(w_pool.shape, lambda i: (0, 0, 0)), pl.BlockSpec(pool_scale.shape, lambda i: (0, 0))],
        out_specs=[pl.BlockSpec((tm, POOL_W), lambda i: (i, 0)), pl.BlockSpec((tm, POOL_W), lambda i: (i, 0))],
        out_shape=[_sds((S, POOL_W)), _sds((S, POOL_W), BF16)])(proj, proj, w_pool, pool_scale)


def _pool_lin_bwd(dout, mixed, w_pool, pool_scale):
    S = dout.shape[0]

    def fn(dout, mixed, wp, sc):
        dlin = dout * sc
        dm, dwp, lin = [], [], []
        for g in range(len(POOL_WINDOWS)):
            ls = slice(g * POOL_GD, (g + 1) * POOL_GD)
            lin.append(_dot(mixed[:, ls], wp[g]))
            dm.append(_dot(dlin[:, ls], wp[g], _NT))
            dwp.append(_dot(mixed[:, ls], dlin[:, ls], _TN))
        dsc = jnp.sum(dout * jnp.concatenate(lin, axis=-1), axis=0, keepdims=True)
        return jnp.concatenate(dm, axis=-1), jnp.concatenate(dwp, axis=0), dsc

    return _rowwise("pool_lin_bwd", fn, [dout, mixed], [w_pool, pool_scale], [_sds((S, POOL_W))],
                    [_sds((len(POOL_WINDOWS) * POOL_GD, POOL_GD)), _sds((1, POOL_W))])


def _pool_win_bwd(dmixed, tm=256):
    S = dmixed.shape[0]
    tm = min(tm, S // 2)
    nt = S // tm
    hb = tm // POOL_HALO

    def body(dm_ref, halo_ref, dp_ref):
        i = pl.program_id(0)
        dm = dm_ref[...]
        halo = halo_ref[...] * (i < nt - 1).astype(F32)
        cnts = _pool_counts(tm, i)
        cnts_h = [c[:POOL_HALO] for c in _pool_counts(tm, i + 1)]
        r = jnp.concatenate([jnp.concatenate([dm[:, g * POOL_GD:(g + 1) * POOL_GD] / cnts[g] for g in range(4)], axis=-1),
                             jnp.concatenate([halo[:, g * POOL_GD:(g + 1) * POOL_GD] / cnts_h[g] for g in range(4)], axis=-1)],
                            axis=0)
        n = tm + POOL_HALO
        sums, acc, k = {}, r, 1
        while k < POOL_WINDOWS[-1]:
            acc = acc + pltpu.roll(acc, n - k, axis=0)
            k *= 2
            sums[k] = acc
        dp = [sums[w][:tm, g * POOL_GD:(g + 1) * POOL_GD] for g, w in enumerate(POOL_WINDOWS)]
        dp_ref[...] = (jnp.concatenate(dp, axis=-1) - dm).astype(dp_ref.dtype)

    return _pcall(
        body, name="pool_win_bwd", grid=(nt,),
        in_specs=[pl.BlockSpec((tm, POOL_W), lambda i: (i, 0)),
                  pl.BlockSpec((POOL_HALO, POOL_W), lambda i: (jnp.minimum((i + 1) * hb, S // POOL_HALO - 1), 0))],
        out_specs=pl.BlockSpec((tm, POOL_W), lambda i: (i, 0)),
        out_shape=_sds((S, POOL_W), BF16))(dmixed, dmixed)


def _xattn_probs(q, kv, h):
    hs = slice(h * XA_HD, (h + 1) * XA_HD)
    s = _dot(q[:, hs], kv[:, hs], _NT) * (XA_HD ** -0.5)
    s = s - jnp.max(s, axis=-1, keepdims=True)
    e = jnp.exp(s)
    return e / jnp.sum(e, axis=-1, keepdims=True)


def _xattn_fwd(proj, kv):
    S = proj.shape[0]

    def fn(q, kv):
        outs = []
        for h in range(XA_HEADS):
            p = _xattn_probs(q, kv, h)
            outs.append(_dot(p, kv[:, XA_W + h * XA_HD:XA_W + (h + 1) * XA_HD]))
        return jnp.concatenate(outs, axis=-1)

    return _rowwise("xattn_fwd", fn, [_Win(proj, XA_W, (3072 + 3072 + POOL_W) // XA_W)], [kv], [_sds((S, XA_W), BF16)])[0]


def _xattn_bwd(proj, kv, dxa):
    S = proj.shape[0]

    def fn(q, dxa, kv):
        dqs, dks, dvs = [], [], []
        for h in range(XA_HEADS):
            hs = slice(h * XA_HD, (h + 1) * XA_HD)
            vh = kv[:, XA_W + h * XA_HD:XA_W + (h + 1) * XA_HD]
            p = _xattn_probs(q, kv, h)
            dp = _dot(dxa[:, hs], vh, _NT)
            ds = p * (dp - jnp.sum(p * dp, axis=-1, keepdims=True)) * (XA_HD ** -0.5)
            dqs.append(_dot(ds, kv[:, hs]))
            dks.append(_dot(ds, q[:, hs], _TN))
            dvs.append(_dot(p, dxa[:, hs], _TN))
        return jnp.concatenate(dqs, axis=-1), jnp.concatenate(dks + dvs, axis=-1)

    return _rowwise("xattn_bwd", fn, [_Win(proj, XA_W, (3072 + 3072 + POOL_W) // XA_W), dxa], [kv],
                    [_sds((S, XA_W), BF16)], [_sds(kv.shape)])


def _mix_fwd(x1, mem, W):
    S = x1.shape[0]
    M = mem.shape[0]
    h = _rowwise("mix_pre", lambda x, g: _rms(x, g), [x1], [W["mix_pre_g"]], [_sds((S, D_MODEL), BF16)])[0]
    proj = _mm_nn("mix_proj", h, W["w_int"], tn=INT_BLK)
    o_raw, states = _gla_fwd(proj, W["w_fu_pad"], W["b_f"])
    ya_in = _rowwise("gla_out", _head_rms_gate, [o_raw, _Win(proj, GLA_DV, 2)], [W["gla_norm_g"]],
                     [_sds((S, GLA_DV), BF16)])[0]
    mixed, pool_out = _pool_fwd(proj, W["w_pool"], W["pool_scale"])
    mem_n = _rowwise("mem_norm", lambda m, g: _rms(m, g), [mem], [W["mem_norm_g"]], [_sds((M, D_MODEL), BF16)])[0]
    kv = _mm_nn("mem_kv", mem_n, W["w_mem_kv"])
    xa = _xattn_fwd(proj, kv)
    ya = _mm_nn("up_gla", ya_in, W["w_up_gla"])
    yb = _mm_nn("up_pool", pool_out, W["w_up_pool"])
    yc = _mm_nn("up_xattn", xa, W["w_up_xattn"])
    merged = _rowwise("merge", _merge, [_Win(proj, 3 * D_MODEL, 1), ya, yb, yc], [], [_sds((S, D_MODEL), BF16)])[0]
    y = _mm_nn("mix_out", merged, W["w_o"])
    x2 = _rowwise("mix_post", lambda x, y, g: x + _rms(y, g), [x1, y], [W["mix_post_g"]], [_sds((S, D_MODEL))])[0]
    return x2, (h, proj, o_raw, states, ya_in, mixed, pool_out, mem_n, kv, xa, ya, yb, yc, merged, y)


def _head(x3, tgt, final_g):
    S = x3.shape[0]

    def head(x3, t, g):
        out, vjp = jax.vjp(_rms, x3, g)
        e = out - t
        loss = 0.5 * jnp.sum(jnp.mean(e * e, axis=-1, keepdims=True), axis=0, keepdims=True)
        dx3, dg = vjp(e * (1.0 / D_MODEL))
        return dx3, jnp.broadcast_to(loss, (1, 128)), dg

    dx3, loss_acc, d_final_g = _rowwise("head", head, [x3, tgt], [final_g], [_sds((S, D_MODEL))],
                                        [_sds((1, 128)), _sds((1, D_MODEL))])
    return loss_acc[0, 0], dx3, d_final_g


def _mix_bwd(dx2, x1, mem, W, saved, on_grads=None):
    h, proj, o_raw, states, ya_in, mixed, pool_out, mem_n, kv, xa, ya, yb, yc, merged, y = saved
    S = x1.shape[0]

    def mix_post_bwd(y, dx2, g):
        return _rms_bwd(y, g, dx2)

    dy, d_mix_post_g = _rowwise("mix_post_bwd", mix_post_bwd, [y, dx2], [W["mix_post_g"]], [_sds((S, D_MODEL), BF16)],
                                [_sds((1, D_MODEL))])
    d_w_o = _mm_tn("d_w_o", merged, dy)
    dmerged = _mm_nt("d_merged", dy, W["w_o"])

    def merge_bwd(gates, ya, yb, yc, dm):
        _, vjp = jax.vjp(_merge, gates, ya, yb, yc)
        return vjp(dm)

    dgates, dya, dyb, dyc = _rowwise("merge_bwd", merge_bwd, [_Win(proj, 3 * D_MODEL, 1), ya, yb, yc, dmerged], [],
                                     [_sds((S, 3 * D_MODEL), BF16)] + [_sds((S, D_MODEL), BF16)] * 3)
    d_w_up_gla = _mm_tn("d_w_up_gla", ya_in, dya)
    d_w_up_pool = _mm_tn("d_w_up_pool", pool_out, dyb)
    d_w_up_xattn = _mm_tn("d_w_up_xattn", xa, dyc)
    d_ya_in = _mm_nt("d_ya_in", dya, W["w_up_gla"])
    d_pool_out = _mm_nt("d_pool_out", dyb, W["w_up_pool"])
    d_xa = _mm_nt("d_xa", dyc, W["w_up_xattn"])

    def gla_out_bwd(o, g_out, d, gn):
        _, vjp = jax.vjp(_head_rms_gate, o, g_out, gn)
        return vjp(d)

    do_raw, dg_out, d_gla_norm_g = _rowwise("gla_out_bwd", gla_out_bwd, [o_raw, _Win(proj, GLA_DV, 2), d_ya_in],
                                            [W["gla_norm_g"]], [_sds((S, GLA_DV)), _sds((S, GLA_DV), BF16)],
                                            [_sds((1, GLA_DV))])
    dq, dk, dv, dflow, d_wfu_pad, d_b_f = _gla_bwd(proj, W["w_fu_pad"], W["b_f"], states, do_raw)
    dmixed, d_w_pool, d_pool_scale = _pool_lin_bwd(d_pool_out, mixed, W["w_pool"], W["pool_scale"])
    dp_in = _pool_win_bwd(dmixed)
    dxq, dkv = _xattn_bwd(proj, kv, d_xa)
    d_w_mem_kv = _mm_tn("d_w_mem_kv", mem_n, dkv)
    dmem_n = _mm_nt("d_mem_n", dkv, W["w_mem_kv"])
    d_mem_norm_g = _rowwise("mem_norm_bwd", lambda m, d, g: _rms_bwd(m, g, d)[1], [mem, dmem_n], [W["mem_norm_g"]], [],
                            [_sds((1, D_MODEL))])[0]
    dproj = jnp.concatenate([dq, dk, dv, dg_out, dgates, dp_in, dxq, dflow], axis=-1)
    d_w_int = _mm_tn("d_w_int", h, dproj, tn=INT_BLK)
    grads = dict(
        w_int=d_w_int, w_fu_pad=d_wfu_pad, b_f=d_b_f, gla_norm_g=d_gla_norm_g, w_pool=d_w_pool,
        pool_scale=d_pool_scale, mem_norm_g=d_mem_norm_g, w_mem_kv=d_w_mem_kv, w_up_gla=d_w_up_gla,
        w_up_pool=d_w_up_pool, w_up_xattn=d_w_up_xattn, w_o=d_w_o, mix_post_g=d_mix_post_g)
    deps = [on_grads(grads)] if on_grads is not None else []
    dh = _mm(
        "d_mix_h", dproj, W["w_int"], grid=(1, S // min(512, S), INT_NBLK), a_blk=(min(512, S), INT_BLK),
        a_map=lambda j, i, k: (i, k), b_blk=(D_MODEL, INT_BLK), b_map=lambda j, i, k: (0, k), o_shape=(S, D_MODEL),
        o_blk=(min(512, S), D_MODEL), o_map=lambda j, i, k: (i, 0), dims=_NT, deps=deps)

    def mix_pre_bwd(x, dh, dx2, g):
        dx, dg = _rms_bwd(x, g, dh)
        return dx + dx2, dg

    dx1, grads["mix_pre_g"] = _rowwise("mix_pre_bwd", mix_pre_bwd, [x1, dh, dx2], [W["mix_pre_g"]],
                                       [_sds((S, D_MODEL))], [_sds((1, D_MODEL))])
    return dx1, grads


def _mesh_pos():
    x, y, c = lax.axis_index("x"), lax.axis_index("y"), lax.axis_index("c")
    return x, y, c, 4 * x + 2 * y + c


def _peer(x, y, c, r):
    px = 1 - x if r & 4 else x
    py = 1 - y if r & 2 else y
    pc = 1 - c if r & 1 else c
    return (px, py, pc), 4 * px + 2 * py + pc


_ALL_PEERS = tuple(range(1, N_DEV))
_SIBLING = 1
_SAME_CORE = (2, 4, 6)


def _peer_copies(srcs, lands, send_sems, recv_sems, scatter, peers=_ALL_PEERS):
    x, y, c, me = _mesh_pos()
    cps = []
    for r in peers:
        pos, peer = _peer(x, y, c, r)
        for a in range(len(srcs)):
            k = a * (N_DEV - 1) + r - 1
            cps.append(pltpu.make_async_remote_copy(
                src_ref=srcs[a].at[peer] if scatter else srcs[a], dst_ref=lands[a].at[me],
                send_sem=send_sems.at[k], recv_sem=recv_sems.at[k], device_id=pos,
                device_id_type=pl.DeviceIdType.MESH))
    return cps


def _exchange(name, arrays, scatter):
    n = len(arrays)

    def body(*refs):
        ins, outs = refs[:n], refs[n:2 * n]
        send_sems, recv_sems, local_sems = refs[2 * n:]
        me = _mesh_pos()[3]
        copies = [pltpu.make_async_copy(ins[a].at[me] if scatter else ins[a], outs[a].at[me], local_sems.at[a])
                  for a in range(n)]
        copies += _peer_copies(ins, outs, send_sems, recv_sems, scatter)
        for cp in copies:
            cp.start()
        for cp in copies:
            cp.wait()

    any_spec = pl.BlockSpec(memory_space=pl.ANY)
    out_shape = [_sds(a.shape if scatter else (N_DEV,) + a.shape, a.dtype) for a in arrays]
    return pl.pallas_call(
        body, name=name, in_specs=[any_spec] * n, out_specs=[any_spec] * n, out_shape=out_shape,
        scratch_shapes=[pltpu.SemaphoreType.DMA((n * (N_DEV - 1),)), pltpu.SemaphoreType.DMA((n * (N_DEV - 1),)),
                        pltpu.SemaphoreType.DMA((n,))])(*arrays)


_HBM = pl.BlockSpec(memory_space=pltpu.HBM)
_SEM = pl.BlockSpec(memory_space=pltpu.SEMAPHORE)
_EFFECT = pltpu.SideEffectType.DATAFLOW_SIDE_EFFECTING


def _own_copies(srcs, lands, own_sems, scatter):
    me = _mesh_pos()[3]
    return [pltpu.make_async_copy(srcs[a].at[me] if scatter else srcs[a], lands[a].at[me], own_sems.at[a])
            for a in range(len(srcs))]


def _exchange_start(name, arrays, scatter):
    n = len(arrays)
    lands = [lax.empty(a.shape if scatter else (N_DEV,) + a.shape, a.dtype) for a in arrays]
    n_sem = n * (N_DEV - 1)

    def body(*refs):
        srcs, lands_ = refs[:n], refs[n:2 * n]
        send_sems, recv_sems, own_sems = refs[2 * n:2 * n + 3]
        token = refs[-1]
        for cp in _peer_copies(srcs, lands_, send_sems, recv_sems, scatter) + _own_copies(srcs, lands_, own_sems, scatter):
            cp.start()
        token[...] = jnp.zeros_like(token)

    hbm = lambda a: pltpu.HBM(a.shape, a.dtype)
    res = pl.pallas_call(
        body, name=name,
        out_shape=(pltpu.SemaphoreType.DMA((n_sem,)), pltpu.SemaphoreType.DMA((n_sem,)), pltpu.SemaphoreType.DMA((n,)),
                   *[hbm(a) for a in arrays], *[hbm(a) for a in lands], _sds((8, 128))),
        in_specs=[_HBM] * (2 * n),
        out_specs=(_SEM, _SEM, _SEM, *[_HBM] * (2 * n), pl.BlockSpec(memory_space=pltpu.VMEM)),
        input_output_aliases={i: 3 + i for i in range(2 * n)},
        compiler_params=pltpu.CompilerParams(has_side_effects=_EFFECT),
    )(*[pltpu.with_memory_space_constraint(a, pltpu.HBM) for a in list(arrays) + lands])
    return (res[:3], res[3:3 + n], res[3 + n:3 + 2 * n], scatter), res[-1]


def _exchange_wait(name, state, after):
    sems, srcs, lands, scatter = state
    n = len(srcs)

    def body(*refs):
        srcs_, lands_ = refs[:n], refs[n:2 * n]
        send_sems, recv_sems, own_sems = refs[2 * n:2 * n + 3]
        for cp in _peer_copies(srcs_, lands_, send_sems, recv_sems, scatter):
            cp.wait_send()
            cp.wait_recv()
        for cp in _own_copies(srcs_, lands_, own_sems, scatter):
            cp.wait()

    hbm = lambda a: pltpu.HBM(a.shape, a.dtype)
    res = pl.pallas_call(
        body, name=name, out_shape=tuple(hbm(a) for a in list(srcs) + list(lands)),
        in_specs=[_HBM] * (2 * n) + [_SEM] * 3 + [pl.BlockSpec(memory_space=pl.ANY)] * len(after),
        out_specs=tuple([_HBM] * (2 * n)), input_output_aliases={i: i for i in range(2 * n)},
        compiler_params=pltpu.CompilerParams(has_side_effects=_EFFECT),
    )(*srcs, *lands, *sems, *after)
    return res[n:]


def _gather_start(name, arrays):
    n = len(arrays)
    lands = [lax.empty((N_DEV,) + a.shape, a.dtype) for a in arrays]
    n_sem = n * (N_DEV - 1)

    def body(*refs):
        srcs, lands_ = refs[:n], refs[n:2 * n]
        send_sems, recv_sems, own_sems = refs[2 * n:2 * n + 3]
        token = refs[-1]
        for cp in (_peer_copies(srcs, lands_, send_sems, recv_sems, False, (_SIBLING,) + _SAME_CORE)
                   + _own_copies(srcs, lands_, own_sems, False)):
            cp.start()
        token[...] = jnp.zeros_like(token)

    hbm = lambda a: pltpu.HBM(a.shape, a.dtype)
    res = pl.pallas_call(
        body, name=name,
        out_shape=(pltpu.SemaphoreType.DMA((n_sem,)), pltpu.SemaphoreType.DMA((n_sem,)), pltpu.SemaphoreType.DMA((n,)),
                   *[hbm(a) for a in arrays], *[hbm(a) for a in lands], _sds((8, 128))),
        in_specs=[_HBM] * (2 * n),
        out_specs=(_SEM, _SEM, _SEM, *[_HBM] * (2 * n), pl.BlockSpec(memory_space=pltpu.VMEM)),
        input_output_aliases={i: 3 + i for i in range(2 * n)},
        compiler_params=pltpu.CompilerParams(has_side_effects=_EFFECT),
    )(*[pltpu.with_memory_space_constraint(a, pltpu.HBM) for a in list(arrays) + lands])
    return (res[:3], res[3:3 + n], res[3 + n:3 + 2 * n]), res[-1]


def _pass_on_copies(lands, send_sems, recv_sems):
    x, y, c, _ = _mesh_pos()
    sibling = _peer(x, y, c, _SIBLING)[0]
    cps = []
    for i, r in enumerate(_SAME_CORE):
        owner = _peer(x, y, c, r)[1]
        for a in range(len(lands)):
            k = a * len(_SAME_CORE) + i
            cps.append(pltpu.make_async_remote_copy(
                src_ref=lands[a].at[owner], dst_ref=lands[a].at[owner], send_sem=send_sems.at[k],
                recv_sem=recv_sems.at[k], device_id=sibling, device_id_type=pl.DeviceIdType.MESH))
    return cps


def _gather_pass_on(name, state, after):
    sems, srcs, lands = state
    n = len(srcs)
    n_sem = n * len(_SAME_CORE)

    def body(*refs):
        srcs_, lands_ = refs[:n], refs[n:2 * n]
        send_sems, recv_sems = refs[2 * n], refs[2 * n + 1]
        on_send, on_recv = refs[2 * n + 3 + len(after)], refs[2 * n + 4 + len(after)]
        arrivals = _peer_copies(srcs_, lands_, send_sems, recv_sems, False, _SAME_CORE)
        for arrived, on in zip(arrivals, _pass_on_copies(lands_, on_send, on_recv)):
            arrived.wait_recv()
            on.start()

    hbm = lambda a: pltpu.HBM(a.shape, a.dtype)
    res = pl.pallas_call(
        body, name=name,
        out_shape=(pltpu.SemaphoreType.DMA((n_sem,)), pltpu.SemaphoreType.DMA((n_sem,)),
                   *[hbm(a) for a in list(srcs) + list(lands)]),
        in_specs=[_HBM] * (2 * n) + [_SEM] * 3 + [pl.BlockSpec(memory_space=pl.ANY)] * len(after),
        out_specs=(_SEM, _SEM, *[_HBM] * (2 * n)), input_output_aliases={i: 2 + i for i in range(2 * n)},
        compiler_params=pltpu.CompilerParams(has_side_effects=_EFFECT),
    )(*srcs, *lands, *sems, *after)
    return sems, res[:2], res[2:2 + n], res[2 + n:]


def _gather_wait(name, state, after):
    sems, on_sems, srcs, lands = state
    n = len(srcs)

    def body(*refs):
        srcs_, lands_ = refs[:n], refs[n:2 * n]
        send_sems, recv_sems, own_sems, on_send, on_recv = refs[2 * n:2 * n + 5]
        for cp in _peer_copies(srcs_, lands_, send_sems, recv_sems, False, (_SIBLING,)):
            cp.wait_recv()
        for cp in _peer_copies(srcs_, lands_, send_sems, recv_sems, False, (_SIBLING,) + _SAME_CORE):
            cp.wait_send()
        for cp in _own_copies(srcs_, lands_, own_sems, False):
            cp.wait()
        for cp in _pass_on_copies(lands_, on_send, on_recv):
            cp.wait_send()
            cp.wait_recv()

    hbm = lambda a: pltpu.HBM(a.shape, a.dtype)
    res = pl.pallas_call(
        body, name=name, out_shape=tuple(hbm(a) for a in list(srcs) + list(lands)),
        in_specs=[_HBM] * (2 * n) + [_SEM] * 5 + [pl.BlockSpec(memory_space=pl.ANY)] * len(after),
        out_specs=tuple([_HBM] * (2 * n)), input_output_aliases={i: i for i in range(2 * n)},
        compiler_params=pltpu.CompilerParams(has_side_effects=_EFFECT),
    )(*srcs, *lands, *sems, *on_sems, *after)
    return res[n:]


def _adam(name, recv, w, m, v):
    shape = w.shape
    R, C = shape
    tr = R
    while N_DEV * tr * C * 4 > 6 * 1024 * 1024 and tr % 32 == 0:
        tr //= 2

    def body(recv_ref, w_ref, m_ref, v_ref, g_ref, d_ref, m2_ref, v2_ref):
        g = recv_ref[0].astype(F32)
        for j in range(1, N_DEV):
            g = g + recv_ref[j].astype(F32)
        w_, m_, v_ = w_ref[...], m_ref[...], v_ref[...]
        m2 = ADAM_B1 * m_ + (1.0 - ADAM_B1) * g
        v2 = ADAM_B2 * v_ + (1.0 - ADAM_B2) * (g * g)
        m_hat = m2 / (1.0 - ADAM_B1 ** ADAM_STEP)
        v_hat = v2 / (1.0 - ADAM_B2 ** ADAM_STEP)
        g_ref[...] = g
        d_ref[...] = -ADAM_LR * (m_hat / (jnp.sqrt(v_hat) + ADAM_EPS) + ADAM_WD * w_)
        m2_ref[...] = m2
        v2_ref[...] = v2

    blk = pl.BlockSpec((tr, C), lambda i: (i, 0))
    return _pcall(body, name=name, grid=(R // tr,),
                  in_specs=[pl.BlockSpec((N_DEV, tr, C), lambda i: (0, i, 0)), blk, blk, blk],
                  out_specs=[blk] * 4, out_shape=[_sds(shape)] * 4)(recv, w, m, v)


_NAMES = ['ffn1_pre_g', 'ffn1_w_in', 'ffn1_w_out', 'ffn1_post_g', 'mix_pre_g', 'w_in', 'w_fu', 'b_f', 'gla_norm_g',
          'w_pool', 'pool_scale', 'mem_norm_g', 'w_mem_kv', 'w_up_gla', 'w_up_pool', 'w_up_xattn', 'w_o', 'mix_post_g',
          'ffn2_pre_g', 'ffn2_w_in', 'ffn2_w_out', 'ffn2_post_g', 'final_g']
_SHARDED = ['ffn1_w_in', 'ffn1_w_out', 'w_in', 'w_fu', 'w_mem_kv', 'w_up_gla', 'w_up_pool', 'w_up_xattn', 'w_o',
            'ffn2_w_in', 'ffn2_w_out']
_COL_SHARDED = ['w_up_pool', 'w_up_xattn']


def _cols_to_full(g):
    return jnp.transpose(g, (1, 0, 2)).reshape(g.shape[1], N_DEV * g.shape[2])


def _full_to_cols(f):
    R, C = f.shape
    return jnp.transpose(f.reshape(R, N_DEV, C // N_DEV), (1, 0, 2))


def _to_internal(w_in_full):
    o = 0
    parts = []
    for s in IN_SPLITS:
        parts.append(w_in_full[:, o:o + s])
        o += s
    q, k, v, g_out, f_low, p_in, xq, gates = parts
    f_low = jnp.pad(f_low, ((0, 0), (0, FLOW_W - GATE_RANK)))
    return jnp.concatenate([q, k, v, g_out, gates, p_in, xq, f_low], axis=-1)


def _from_internal(d):
    q, k, v, g_out = d[:, 0:512], d[:, 512:1024], d[:, 1024:2048], d[:, 2048:3072]
    gates, p_in, xq = d[:, 3072:6144], d[:, 6144:6656], d[:, 6656:7168]
    f_low = d[:, 7168:7168 + GATE_RANK]
    return jnp.concatenate([q, k, v, g_out, f_low, p_in, xq, gates], axis=-1)


_GROUPS = {"ffn1": ['ffn1_w_in', 'ffn1_w_out'],
           "mix": ['w_in', 'w_fu', 'w_mem_kv', 'w_up_gla', 'w_up_pool', 'w_up_xattn', 'w_o'],
           "ffn2": ['ffn2_w_in', 'ffn2_w_out']}


def _rows_to_full(g):
    return g.reshape(N_DEV * g.shape[1], g.shape[2])


def _mix_weights(gathered):
    W = {n: _rows_to_full(gathered[n]) for n in ('w_mem_kv', 'w_up_gla', 'w_o')}
    for n in _COL_SHARDED:
        W[n] = _cols_to_full(gathered[n])
    W["w_int"] = _to_internal(_cols_to_full(gathered["w_in"]))
    W["w_fu_pad"] = jnp.pad(_cols_to_full(gathered["w_fu"]), ((0, FLOW_W - GATE_RANK), (0, 0)))
    return W


def _mix_chunks(G, n):
    if n == "w_in":
        return _full_to_cols(_from_internal(G["w_int"]))
    if n == "w_fu":
        return _full_to_cols(G["w_fu_pad"][:GATE_RANK].astype(BF16))
    if n in _COL_SHARDED:
        return _full_to_cols(G[n])
    return G[n].reshape(N_DEV, G[n].shape[0] // N_DEV, G[n].shape[1])


def _step(x, mem, tgt, P, Mo, Vo):
    gather, tokens = {}, {}
    for grp, names in (("ffn1_in", ["ffn1_w_in"]), ("ffn1_out", ["ffn1_w_out"]), ("mix", _GROUPS["mix"]),
                       ("ffn2", _GROUPS["ffn2"])):
        gather[grp], tokens[grp] = _gather_start("gather_" + grp, [P[n].astype(BF16) for n in names])
    small = {n: P[n] for n in _NAMES if n not in _SHARDED}

    def gathered(grp, after):
        return _gather_wait("gather_" + grp + "_wait", _gather_pass_on("gather_" + grp + "_on", gather[grp], after), [])

    w1in = gathered("ffn1_in", list(tokens.values())[1:])[0]
    x1, w1out, sv1 = _ffn_fwd("ffn1", x, small["ffn1_pre_g"], w1in, small["ffn1_post_g"],
                              lambda act: _rows_to_full(gathered("ffn1_out", [act])[0]))

    gm = dict(zip(_GROUPS["mix"], gathered("mix", [x1])))
    Wm = {**small, **_mix_weights(gm)}
    x2, svm = _mix_fwd(x1, mem, Wm)

    w2in, w2out = gathered("ffn2", [x2])
    x3, w2out, sv2 = _ffn_fwd("ffn2", x2, small["ffn2_pre_g"], w2in, small["ffn2_post_g"],
                              lambda act: _rows_to_full(w2out))

    loss, dx3, d_final_g = _head(x3, tgt, small["final_g"])

    G = dict(final_g=d_final_g)
    scat = {}

    def start_ffn(grp):
        def on_grads(dwg, dwout):
            scat[grp], tok = _exchange_start("scatter_" + grp, [dwg, dwout.reshape(N_DEV, -1, D_MODEL)], scatter=True)
            return tok
        return on_grads

    def start_mix(Gm):
        scat["mix"], tok = _exchange_start("scatter_mix", [_mix_chunks(Gm, n) for n in _GROUPS["mix"]], scatter=True)
        return tok

    dx2, _, _, G["ffn2_pre_g"], G["ffn2_post_g"] = _ffn_bwd(
        "ffn2", dx3, x2, small["ffn2_pre_g"], w2in, w2out, small["ffn2_post_g"], sv2, start_ffn("ffn2"))
    dx1, Gm = _mix_bwd(dx2, x1, mem, Wm, svm, start_mix)
    G.update(Gm)
    dx, _, _, G["ffn1_pre_g"], G["ffn1_post_g"] = _ffn_bwd(
        "ffn1", dx1, x, small["ffn1_pre_g"], w1in, w1out, small["ffn1_post_g"], sv1, start_ffn("ffn1"))

    recv = dict(zip(small, _exchange("gather_small_grads", [G[n] for n in small], scatter=False)))
    grads, deltas, new_m, new_v = {}, {}, {}, {}
    last = dx
    for grp, state in (("small", None), ("ffn2", scat["ffn2"]), ("mix", scat["mix"]), ("ffn1", scat["ffn1"])):
        names = list(small) if state is None else _GROUPS[grp]
        if state is not None:
            recv.update(zip(names, _exchange_wait("scatter_" + grp + "_wait", state, [last])))
        for n in names:
            shp = P[n].shape
            shp2 = shp if len(shp) == 2 else (shp[0] * shp[1], shp[2])
            g, d, m2, v2 = _adam("adam_" + n, recv[n].reshape((N_DEV,) + shp2), P[n].reshape(shp2),
                                 Mo[n].reshape(shp2), Vo[n].reshape(shp2))
            grads[n], deltas[n], new_m[n], new_v[n] = (t.reshape((1,) + shp) for t in (g, d, m2, v2))
            last = v2
    return loss, dx, grads, deltas, new_m, new_v


def kernel(x, mem, ffn1_pre_g, ffn1_w_in, ffn1_w_out, ffn1_post_g, mix_pre_g, w_in, w_fu, b_f, gla_norm_g, w_pool, pool_scale, mem_norm_g, w_mem_kv, w_up_gla, w_up_pool, w_up_xattn, w_o, mix_post_g, ffn2_pre_g, ffn2_w_in, ffn2_w_out, ffn2_post_g, final_g, loss_target, m_ffn1_pre_g, m_ffn1_w_in, m_ffn1_w_out, m_ffn1_post_g, m_mix_pre_g, m_w_in, m_w_fu, m_b_f, m_gla_norm_g, m_w_pool, m_pool_scale, m_mem_norm_g, m_w_mem_kv, m_w_up_gla, m_w_up_pool, m_w_up_xattn, m_w_o, m_mix_post_g, m_ffn2_pre_g, m_ffn2_w_in, m_ffn2_w_out, m_ffn2_post_g, m_final_g, v_ffn1_pre_g, v_ffn1_w_in, v_ffn1_w_out, v_ffn1_post_g, v_mix_pre_g, v_w_in, v_w_fu, v_b_f, v_gla_norm_g, v_w_pool, v_pool_scale, v_mem_norm_g, v_w_mem_kv, v_w_up_gla, v_w_up_pool, v_w_up_xattn, v_w_o, v_mix_post_g, v_ffn2_pre_g, v_ffn2_w_in, v_ffn2_w_out, v_ffn2_post_g, v_final_g):
    params = [ffn1_pre_g, ffn1_w_in, ffn1_w_out, ffn1_post_g, mix_pre_g, w_in, w_fu, b_f, gla_norm_g, w_pool, pool_scale, mem_norm_g, w_mem_kv, w_up_gla, w_up_pool, w_up_xattn, w_o, mix_post_g, ffn2_pre_g, ffn2_w_in, ffn2_w_out, ffn2_post_g, final_g]
    moms = [m_ffn1_pre_g, m_ffn1_w_in, m_ffn1_w_out, m_ffn1_post_g, m_mix_pre_g, m_w_in, m_w_fu, m_b_f, m_gla_norm_g, m_w_pool, m_pool_scale, m_mem_norm_g, m_w_mem_kv, m_w_up_gla, m_w_up_pool, m_w_up_xattn, m_w_o, m_mix_post_g, m_ffn2_pre_g, m_ffn2_w_in, m_ffn2_w_out, m_ffn2_post_g, m_final_g]
    vars_ = [v_ffn1_pre_g, v_ffn1_w_in, v_ffn1_w_out, v_ffn1_post_g, v_mix_pre_g, v_w_in, v_w_fu, v_b_f, v_gla_norm_g, v_w_pool, v_pool_scale, v_mem_norm_g, v_w_mem_kv, v_w_up_gla, v_w_up_pool, v_w_up_xattn, v_w_o, v_mix_post_g, v_ffn2_pre_g, v_ffn2_w_in, v_ffn2_w_out, v_ffn2_post_g, v_final_g]
    P = {n: a[0] if a.ndim > 2 else a for n, a in zip(_NAMES, params)}
    Mo = {n: a[0] if a.ndim > 2 else a for n, a in zip(_NAMES, moms)}
    Vo = {n: a[0] if a.ndim > 2 else a for n, a in zip(_NAMES, vars_)}
    loss, dx, grads, deltas, new_m, new_v = _step(x[0], mem[0], loss_target[0], P, Mo, Vo)
    loss = lax.psum(loss, ("x", "y", "c"))
    out = [loss, dx[None]]
    for d in (grads, deltas, new_m, new_v):
        for n in _NAMES:
            out.append(d[n].reshape(params[_NAMES.index(n)].shape))
    return tuple(out)
```

```python
import functools

import jax
import jax.numpy as jnp
from jax import lax
from jax.experimental import pallas as pl
from jax.experimental.pallas import tpu as pltpu

F32 = jnp.float32
BF16 = jnp.bfloat16

N_DEV = 8
D_MODEL = 1024
D_FF = 2816
FF_BLK = 2 * D_FF // N_DEV
N_FF_BLK = D_FF // FF_BLK
FF_PAD = 768
CHUNK = 64
GLA_HEADS = 4
GLA_DK = 512
GLA_DV = 1024
GLA_HDK = 128
GLA_HDV = 256
GATE_RANK = 16
GATE_TEMP = 16.0
POOL_WINDOWS = (2, 4, 8, 16)
POOL_W = 512
POOL_GD = 128
POOL_HALO = 16
XA_HEADS = 4
XA_HD = 128
XA_W = 512
EPS = 1e-6
IN_SPLITS = (GLA_DK, GLA_DK, GLA_DV, GLA_DV, GATE_RANK, POOL_W, XA_W, 3 * D_MODEL)
IN_WIDTH = sum(IN_SPLITS)
IN_SHARD = IN_WIDTH // N_DEV
INT_W = 3072 + 3072 + 1024 + 128
INT_NBLK = 3
INT_BLK = INT_W // INT_NBLK
FLOW_W = 128

ADAM_LR = 0.001
ADAM_B1 = 0.9
ADAM_B2 = 0.999
ADAM_EPS = 1e-08
ADAM_WD = 0.01
ADAM_STEP = 10

VMEM_LIMIT = 56 * 1024 * 1024

_NN = (((1,), (0,)), ((), ()))
_NT = (((1,), (1,)), ((), ()))
_TN = (((0,), (0,)), ((), ()))


def _pcall(body, *, name, grid, in_specs, out_specs, out_shape, scratch=()):
    return pl.pallas_call(
        body, name=name, grid=grid, in_specs=in_specs, out_specs=out_specs, out_shape=out_shape,
        scratch_shapes=list(scratch),
        compiler_params=pltpu.CompilerParams(dimension_semantics=("arbitrary",) * len(grid),
                                             vmem_limit_bytes=VMEM_LIMIT))


def _dot(a, b, dims=_NN):
    return lax.dot_general(a.astype(BF16), b.astype(BF16), dims, preferred_element_type=F32)


def _mm(name, a, b, *, grid, a_blk, a_map, b_blk, b_map, o_shape, o_blk, o_map, dims, out_dtype=F32, deps=()):
    nk = grid[2]

    def body(a_ref, b_ref, *rest):
        o_ref, scr = rest[len(deps)], rest[len(deps) + 1:]
        p = _dot(a_ref[...], b_ref[...], dims)
        if nk == 1:
            o_ref[...] = p.astype(o_ref.dtype)
        else:
            acc = scr[0]
            k = pl.program_id(2)

            @pl.when(k == 0)
            def _():
                acc[...] = p

            @pl.when(k > 0)
            def _():
                acc[...] += p

            @pl.when(k == nk - 1)
            def _():
                o_ref[...] = acc[...].astype(o_ref.dtype)

    acc_shape = tuple(d for d in o_blk if d is not None)
    return _pcall(body, name=name, grid=grid,
                  in_specs=[pl.BlockSpec(a_blk, a_map), pl.BlockSpec(b_blk, b_map)]
                  + [pl.BlockSpec(memory_space=pl.ANY)] * len(deps),
                  out_specs=pl.BlockSpec(o_blk, o_map),
                  out_shape=jax.ShapeDtypeStruct(o_shape, out_dtype),
                  scratch=[pltpu.VMEM(acc_shape, F32)] if nk > 1 else [])(a, b, *deps)


def _mm_nn(name, a, b, out_dtype=F32, tm=512, tn=None):
    M, K = a.shape
    N = b.shape[1]
    tm, tn = min(tm, M), (tn or N)
    return _mm(name, a, b, grid=(N // tn, M // tm, 1), a_blk=(tm, K), a_map=lambda j, i, k: (i, 0),
               b_blk=(K, tn), b_map=lambda j, i, k: (0, j), o_shape=(M, N), o_blk=(tm, tn),
               o_map=lambda j, i, k: (i, j), dims=_NN, out_dtype=out_dtype)


def _mm_nt(name, a, b, out_dtype=F32, tm=512):
    M, K = a.shape
    N = b.shape[0]
    tm = min(tm, M)
    return _mm(name, a, b, grid=(1, M // tm, 1), a_blk=(tm, K), a_map=lambda j, i, k: (i, 0),
               b_blk=(N, K), b_map=lambda j, i, k: (0, 0), o_shape=(M, N), o_blk=(tm, N),
               o_map=lambda j, i, k: (i, 0), dims=_NT, out_dtype=out_dtype)


def _mm_tn(name, a, b, out_dtype=BF16, ts=512, tn=None, tn_a=None):
    S, M = a.shape
    N = b.shape[1]
    ts, tn, tn_a = min(ts, S), (tn or N), (tn_a or M)
    return _mm(name, a, b, grid=((N // tn) * (M // tn_a), 1, S // ts), a_blk=(ts, tn_a),
               a_map=lambda j, i, k: (k, j if tn_a < M else 0), b_blk=(ts, tn),
               b_map=lambda j, i, k: (k, j if tn < N else 0), o_shape=(M, N), o_blk=(tn_a, tn),
               o_map=lambda j, i, k: (j, 0) if tn_a < M else (0, j), dims=_TN, out_dtype=out_dtype)


class _Win:
    def __init__(self, arr, w, c):
        self.arr, self.w, self.c = arr, w, c


def _row_spec(x, tm):
    if isinstance(x, _Win):
        return x.arr, pl.BlockSpec((tm, x.w), functools.partial(lambda i, c: (i, c), c=x.c))
    if x.ndim == 3:
        return x, pl.BlockSpec((x.shape[0], tm, x.shape[2]), lambda i: (0, i, 0))
    return x, pl.BlockSpec((tm, x.shape[1]), lambda i: (i, 0))


def _rowwise(name, fn, rows, consts, outs, accs=(), tm=256):
    first = rows[0].arr if isinstance(rows[0], _Win) else rows[0]
    S = first.shape[1] if first.ndim == 3 else first.shape[0]
    tm = min(tm, S)
    n_in, n_out = len(rows) + len(consts), len(outs)
    arrays, in_specs = [], []
    for r in rows:
        arr, spec = _row_spec(r, tm)
        arrays.append(arr)
        in_specs.append(spec)
    for c in consts:
        arrays.append(c)
        in_specs.append(pl.BlockSpec(c.shape, functools.partial(lambda i, n: (0,) * n, n=c.ndim)))
    out_specs = [_row_spec(o, tm)[1] for o in outs]
    out_specs += [pl.BlockSpec(a.shape, functools.partial(lambda i, n: (0,) * n, n=len(a.shape))) for a in accs]

    def body(*refs):
        res = fn(*[r[...] for r in refs[:n_in]])
        if not isinstance(res, (tuple, list)):
            res = (res,)
        orefs = refs[n_in:]
        for r, v in zip(orefs[:n_out], res[:n_out]):
            r[...] = v.astype(r.dtype)
        i = pl.program_id(0)
        for r, v in zip(orefs[n_out:], res[n_out:]):
            @pl.when(i == 0)
            def _(r=r, v=v):
                r[...] = v.astype(r.dtype)

            @pl.when(i > 0)
            def _(r=r, v=v):
                r[...] += v.astype(r.dtype)

    res = _pcall(body, name=name, grid=(S // tm,), in_specs=in_specs, out_specs=out_specs,
                 out_shape=[jax.ShapeDtypeStruct(o.shape, o.dtype) for o in list(outs) + list(accs)])(*arrays)
    return res


def _sds(shape, dtype=F32):
    return jax.ShapeDtypeStruct(shape, dtype)


def _rms(x, g):
    return x * lax.rsqrt(jnp.mean(x * x, axis=-1, keepdims=True) + EPS) * g


def _rms_bwd(x, g, dy):
    _, vjp = jax.vjp(_rms, x, g)
    return vjp(dy)


def _sigmoid(x):
    return 0.5 * jnp.tanh(0.5 * x) + 0.5


def _silu(x):
    return x * _sigmoid(x)


def _log_sigmoid(f):
    return jnp.minimum(f, 0.0) - jnp.log(1.0 + jnp.exp(-jnp.abs(f)))


def _head_rms_gate(o, g_out, gn):
    parts = [_rms(o[:, h * GLA_HDV:(h + 1) * GLA_HDV], gn[:, h * GLA_HDV:(h + 1) * GLA_HDV]) for h in range(GLA_HEADS)]
    return jnp.concatenate(parts, axis=-1) * _silu(g_out)


def _merge(gates, ya, yb, yc):
    return (_sigmoid(gates[:, :D_MODEL]) * ya + _sigmoid(gates[:, D_MODEL:2 * D_MODEL]) * yb
            + _sigmoid(gates[:, 2 * D_MODEL:]) * yc)


def _tri_dot(t, x):
    hi = x.astype(BF16)
    r1 = x - hi.astype(F32)
    mid = r1.astype(BF16)
    lo = (r1 - mid.astype(F32)).astype(BF16)
    d = functools.partial(lax.dot_general, dimension_numbers=_NN, preferred_element_type=F32)
    return d(t, hi) + d(t, mid) + d(t, lo)


def _ffn_fwd(tag, x, gpre, wt, gpost, get_wo):
    S = x.shape[0]
    h = _rowwise(tag + "_pre", lambda x, g: _rms(x, g), [x], [gpre], [_sds((S, D_MODEL), BF16)])[0]
    tm = min(512, S)

    def up_body(h_ref, w_ref, u_ref, act_ref):
        ab = _dot(h_ref[...], w_ref[...], _NT)
        u_ref[...] = ab.astype(BF16)
        act_ref[...] = (_silu(ab[:, :FF_PAD]) * ab[:, FF_PAD:]).astype(BF16)

    u, act = _pcall(
        up_body, name=tag + "_up", grid=(N_FF_BLK, S // tm),
        in_specs=[pl.BlockSpec((tm, D_MODEL), lambda j, i: (i, 0)),
                  pl.BlockSpec((2 * FF_PAD, D_MODEL), lambda j, i: (j, 0))],
        out_specs=[pl.BlockSpec((tm, 2 * FF_PAD), lambda j, i: (i, j)),
                   pl.BlockSpec((tm, FF_PAD), lambda j, i: (i, j))],
        out_shape=[_sds((S, N_DEV * FF_PAD), BF16), _sds((S, N_FF_BLK * FF_PAD), BF16)],
    )(h, wt)

    def down_body(act_ref, w_ref, x_ref, g_ref, f_ref, xo_ref):
        f = _dot(act_ref[...], w_ref[...])
        f_ref[...] = f
        xo_ref[...] = x_ref[...] + 0.5 * _rms(f, g_ref[...])

    wo = get_wo(act)
    row = pl.BlockSpec((tm, D_MODEL), lambda i: (i, 0))
    f, xo = _pcall(
        down_body, name=tag + "_down", grid=(S // tm,),
        in_specs=[pl.BlockSpec((tm, N_FF_BLK * FF_PAD), lambda i: (i, 0)),
                  pl.BlockSpec(wo.shape, lambda i: (0, 0), pipeline_mode=pl.Buffered(1)), row,
                  pl.BlockSpec((1, D_MODEL), lambda i: (0, 0))],
        out_specs=[row, row], out_shape=[_sds((S, D_MODEL)), _sds((S, D_MODEL))],
    )(act, wo, x, gpost)
    return xo, wo, (h, u, act, f)


def _ffn_bwd(tag, dxo, x, gpre, wt, wo, gpost, saved, on_dwo=None, on_dwt=None):
    h, u, act, f = saved
    S = x.shape[0]
    tm = min(512, S)
    dep_spec = pl.BlockSpec(memory_space=pl.ANY)

    def post_bwd(f, dxo, g):
        df, dg = _rms_bwd(f, g, 0.5 * dxo)
        return df, dg

    df, dgpost = _rowwise(tag + "_post_bwd", post_bwd, [f, dxo], [gpost], [_sds((S, D_MODEL), BF16)],
                          [_sds((1, D_MODEL))])
    dwo = _mm_tn(tag + "_dwo", act, df, ts=2048, tn_a=FF_PAD)
    deps = [on_dwo(dwo)] if on_dwo is not None else []

    def dact_body(df_ref, w_ref, u_ref, *rest):
        du_ref = rest[-1]
        dact = _dot(df_ref[...], w_ref[...], _NT)
        ab = u_ref[...].astype(F32)
        a, b = ab[:, :FF_PAD], ab[:, FF_PAD:]
        sg = _sigmoid(a)
        du_ref[:, :FF_PAD] = (dact * b * (sg * (1.0 + a * (1.0 - sg)))).astype(BF16)
        du_ref[:, FF_PAD:] = (dact * (a * sg)).astype(BF16)

    u_spec = pl.BlockSpec((tm, 2 * FF_PAD), lambda j, i: (i, j))
    du = _pcall(
        dact_body, name=tag + "_dact", grid=(N_FF_BLK, S // tm),
        in_specs=[pl.BlockSpec((tm, D_MODEL), lambda j, i: (i, 0)),
                  pl.BlockSpec((FF_PAD, D_MODEL), lambda j, i: (j, 0)), u_spec] + [dep_spec] * len(deps),
        out_specs=u_spec, out_shape=_sds((S, N_DEV * FF_PAD), BF16))(df, wo, u, *deps)
    dwt = _mm_tn(tag + "_dwt", du, h, ts=2048, tn_a=FF_PAD)
    deps = [on_dwt(dwt)] if on_dwt is not None else []

    def dh_body(du_ref, w_ref, x_ref, dxo_ref, g_ref, *rest):
        dx_ref, dg_ref = rest[len(deps):]
        i = pl.program_id(0)
        dx, dg = _rms_bwd(x_ref[...], g_ref[...], _dot(du_ref[...], w_ref[...]))
        dx_ref[...] = dx + dxo_ref[...]

        @pl.when(i == 0)
        def _():
            dg_ref[...] = dg

        @pl.when(i > 0)
        def _():
            dg_ref[...] += dg

    th = min(256, S)
    row = pl.BlockSpec((th, D_MODEL), lambda i: (i, 0))
    vec = pl.BlockSpec((1, D_MODEL), lambda i: (0, 0))
    dx, dgpre = _pcall(
        dh_body, name=tag + "_dh", grid=(S // th,),
        in_specs=[pl.BlockSpec((th, N_DEV * FF_PAD), lambda i: (i, 0)),
                  pl.BlockSpec(wt.shape, lambda i: (0, 0), pipeline_mode=pl.Buffered(1)), row, row, vec]
        + [dep_spec] * len(deps),
        out_specs=[row, vec], out_shape=[_sds((S, D_MODEL)), _sds((1, D_MODEL))])(du, wt, x, dxo, gpre, *deps)
    return dx, dgpre, dgpost


def _gla_chunk_common(k, flow, wfu, bf, tri):
    f = _dot(flow, wfu) + bf
    la = _log_sigmoid(f) * (1.0 / GATE_TEMP)
    b = _tri_dot(tri, la)
    b_end = b[CHUNK - 1:CHUNK, :]
    w = jnp.exp(b_end - b)
    kt = k * w
    d = jnp.exp(b_end)
    return f, w, kt, d


def _tri_matrix(lower):
    r = lax.broadcasted_iota(jnp.int32, (CHUNK, CHUNK), 0)
    c = lax.broadcasted_iota(jnp.int32, (CHUNK, CHUNK), 1)
    return jnp.where((r >= c) if lower else (r <= c), 1.0, 0.0).astype(BF16)


def _gla_fwd(proj, wfu, bf):
    S = proj.shape[0]
    nc = S // CHUNK

    def body(q_ref, k_ref, v_ref, flow_ref, wfu_ref, bf_ref, o_ref, st_ref, state):
        c = pl.program_id(0)

        @pl.when(c == 0)
        def _():
            state[...] = jnp.zeros_like(state)

        _, _, kt, d = _gla_chunk_common(k_ref[...], flow_ref[...], wfu_ref[...], bf_ref[...], _tri_matrix(True))
        q = q_ref[...] * (GLA_HDK ** -0.5)
        v = v_ref[...]
        outs = []
        for h in range(GLA_HEADS):
            ks, vs = slice(h * GLA_HDK, (h + 1) * GLA_HDK), slice(h * GLA_HDV, (h + 1) * GLA_HDV)
            st = state[vs, :] * d[:, ks] + _dot(v[:, vs], kt[:, ks], _TN)
            state[vs, :] = st
            outs.append(_dot(q[:, ks], st, _NT))
        o_ref[...] = jnp.concatenate(outs, axis=-1)
        st_ref[...] = state[...]

    return _pcall(
        body, name="gla_fwd", grid=(nc,),
        in_specs=[pl.BlockSpec((CHUNK, GLA_DK), lambda c: (c, 0)), pl.BlockSpec((CHUNK, GLA_DK), lambda c: (c, 1)),
                  pl.BlockSpec((CHUNK, GLA_DV), lambda c: (c, 1)),
                  pl.BlockSpec((CHUNK, FLOW_W), lambda c: (c, (INT_W - FLOW_W) // FLOW_W)),
                  pl.BlockSpec(wfu.shape, lambda c: (0, 0)), pl.BlockSpec(bf.shape, lambda c: (0, 0))],
        out_specs=[pl.BlockSpec((CHUNK, GLA_DV), lambda c: (c, 0)),
                   pl.BlockSpec((None, GLA_DV, GLA_HDK), lambda c: (c, 0, 0))],
        out_shape=[_sds((S, GLA_DV)), _sds((nc, GLA_DV, GLA_HDK))],
        scratch=[pltpu.VMEM((GLA_DV, GLA_HDK), F32)])(proj, proj, proj, proj, wfu, bf)


def _gla_bwd(proj, wfu, bf, states, do):
    S = proj.shape[0]
    nc = S // CHUNK

    def body(q_ref, k_ref, v_ref, flow_ref, wfu_ref, bf_ref, st_ref, stp_ref, do_ref,
             dq_ref, dk_ref, dv_ref, dflow_ref, dwfu_ref, dbf_ref, dstate):
        step = pl.program_id(0)
        c = nc - 1 - step

        @pl.when(step == 0)
        def _():
            dstate[...] = jnp.zeros_like(dstate)
            dwfu_ref[...] = jnp.zeros_like(dwfu_ref)
            dbf_ref[...] = jnp.zeros_like(dbf_ref)

        k, flow, wfu_v = k_ref[...], flow_ref[...], wfu_ref[...]
        f, w, kt, d = _gla_chunk_common(k, flow, wfu_v, bf_ref[...], _tri_matrix(True))
        q = q_ref[...] * (GLA_HDK ** -0.5)
        v, dout = v_ref[...], do_ref[...]
        has_prev = (c > 0).astype(F32)
        dqs, dkts, dvs, dds = [], [], [], []
        for h in range(GLA_HEADS):
            ks, vs = slice(h * GLA_HDK, (h + 1) * GLA_HDK), slice(h * GLA_HDV, (h + 1) * GLA_HDV)
            st = st_ref[vs, :]
            dqs.append(_dot(dout[:, vs], st))
            dst = dstate[vs, :] + _dot(dout[:, vs], q[:, ks], _TN)
            dkts.append(_dot(v[:, vs], dst))
            dvs.append(_dot(kt[:, ks], dst, _NT))
            dds.append(jnp.sum(dst * stp_ref[vs, :], axis=0, keepdims=True) * has_prev)
            dstate[vs, :] = dst * d[:, ks]
        dq_ref[...] = (jnp.concatenate(dqs, axis=-1) * (GLA_HDK ** -0.5)).astype(dq_ref.dtype)
        dv_ref[...] = jnp.concatenate(dvs, axis=-1).astype(dv_ref.dtype)
        dkt = jnp.concatenate(dkts, axis=-1)
        dd = jnp.concatenate(dds, axis=-1)
        dk_ref[...] = (dkt * w).astype(dk_ref.dtype)
        de = dkt * kt
        db_end = jnp.sum(de, axis=0, keepdims=True) + dd * d
        dla = db_end - _tri_dot(_tri_matrix(False), de)
        df = dla * (1.0 - _sigmoid(f)) * (1.0 / GATE_TEMP)
        dflow_ref[...] = _dot(df, wfu_v, _NT).astype(dflow_ref.dtype)
        dwfu_ref[...] += _dot(flow, df, _TN)
        dbf_ref[...] += jnp.sum(df, axis=0, keepdims=True)

    rc = lambda s: nc - 1 - s
    return _pcall(
        body, name="gla_bwd", grid=(nc,),
        in_specs=[pl.BlockSpec((CHUNK, GLA_DK), lambda s: (rc(s), 0)), pl.BlockSpec((CHUNK, GLA_DK), lambda s: (rc(s), 1)),
                  pl.BlockSpec((CHUNK, GLA_DV), lambda s: (rc(s), 1)),
                  pl.BlockSpec((CHUNK, FLOW_W), lambda s: (rc(s), (INT_W - FLOW_W) // FLOW_W)),
                  pl.BlockSpec(wfu.shape, lambda s: (0, 0)), pl.BlockSpec(bf.shape, lambda s: (0, 0)),
                  pl.BlockSpec((None, GLA_DV, GLA_HDK), lambda s: (rc(s), 0, 0)),
                  pl.BlockSpec((None, GLA_DV, GLA_HDK), lambda s: (jnp.maximum(rc(s) - 1, 0), 0, 0)),
                  pl.BlockSpec((CHUNK, GLA_DV), lambda s: (rc(s), 0))],
        out_specs=[pl.BlockSpec((CHUNK, GLA_DK), lambda s: (rc(s), 0)), pl.BlockSpec((CHUNK, GLA_DK), lambda s: (rc(s), 0)),
                   pl.BlockSpec((CHUNK, GLA_DV), lambda s: (rc(s), 0)), pl.BlockSpec((CHUNK, FLOW_W), lambda s: (rc(s), 0)),
                   pl.BlockSpec(wfu.shape, lambda s: (0, 0)), pl.BlockSpec(bf.shape, lambda s: (0, 0))],
        out_shape=[_sds((S, GLA_DK), BF16), _sds((S, GLA_DK), BF16), _sds((S, GLA_DV), BF16), _sds((S, FLOW_W), BF16),
                   _sds(wfu.shape), _sds(bf.shape)],
        scratch=[pltpu.VMEM((GLA_DV, GLA_HDK), F32)])(proj, proj, proj, proj, wfu, bf, states, states, do)


def _pool_counts(tm, i):
    t = (lax.broadcasted_iota(jnp.int32, (tm, POOL_GD), 0) + i * tm + 1).astype(F32)
    return [jnp.minimum(t, float(w)) for w in POOL_WINDOWS]


def _pool_fwd(proj, w_pool, pool_scale, tm=256):
    S = proj.shape[0]
    tm = min(tm, S // 2)
    col = (3072 + 3072) // POOL_W
    hb = tm // POOL_HALO

    def body(p_ref, halo_ref, wp_ref, sc_ref, mixed_ref, out_ref):
        i = pl.program_id(0)
        p = p_ref[...]
        halo = halo_ref[...] * (i > 0).astype(F32)
        ext = jnp.concatenate([halo, p], axis=0)
        n = tm + POOL_HALO
        sums, acc, k = {}, ext, 1
        while k < POOL_WINDOWS[-1]:
            acc = acc + pltpu.roll(acc, k, axis=0)
            k *= 2
            sums[k] = acc
        cnts = _pool_counts(tm, i)
        mixed, lin = [], []
        for g, w in enumerate(POOL_WINDOWS):
            ls = slice(g * POOL_GD, (g + 1) * POOL_GD)
            m = sums[w][POOL_HALO:n, ls] / cnts[g] - p[:, ls]
            mixed.append(m)
            lin.append(_dot(m, wp_ref[g]))
        mixed_ref[...] = jnp.concatenate(mixed, axis=-1)
        out_ref[...] = (jnp.concatenate(lin, axis=-1) * sc_ref[...]).astype(out_ref.dtype)

    return _pcall(
        body, name="pool_fwd", grid=(S // tm,),
        in_specs=[pl.BlockSpec((tm, POOL_W), lambda i: (i, col)),
                  pl.BlockSpec((POOL_HALO, POOL_W), lambda i: (jnp.maximum(i * hb - 1, 0), col)),
                  pl.BlockSpec(w_pool.shape, lambda i: (0, 0, 0)), pl.BlockSpec(pool_scale.shape, lambda i: (0, 0))],
        out_specs=[pl.BlockSpec((tm, POOL_W), lambda i: (i, 0)), pl.BlockSpec((tm, POOL_W), lambda i: (i, 0))],
        out_shape=[_sds((S, POOL_W)), _sds((S, POOL_W), BF16)])(proj, proj, w_pool, pool_scale)


def _pool_lin_bwd(dout, mixed, w_pool, pool_scale):
    S = dout.shape[0]

    def fn(dout, mixed, wp, sc):
        dlin = dout * sc
        dm, dwp, lin = [], [], []
        for g in range(len(POOL_WINDOWS)):
            ls = slice(g * POOL_GD, (g + 1) * POOL_GD)
            lin.append(_dot(mixed[:, ls], wp[g]))
            dm.append(_dot(dlin[:, ls], wp[g], _NT))
            dwp.append(_dot(mixed[:, ls], dlin[:, ls], _TN))
        dsc = jnp.sum(dout * jnp.concatenate(lin, axis=-1), axis=0, keepdims=True)
        return jnp.concatenate(dm, axis=-1), jnp.concatenate(dwp, axis=0), dsc

    return _rowwise("pool_lin_bwd", fn, [dout, mixed], [w_pool, pool_scale], [_sds((S, POOL_W))],
                    [_sds((len(POOL_WINDOWS) * POOL_GD, POOL_GD)), _sds((1, POOL_W))])


def _pool_win_bwd(dmixed, tm=256):
    S = dmixed.shape[0]
    tm = min(tm, S // 2)
    nt = S // tm
    hb = tm // POOL_HALO

    def body(dm_ref, halo_ref, dp_ref):
        i = pl.program_id(0)
        dm = dm_ref[...]
        halo = halo_ref[...] * (i < nt - 1).astype(F32)
        cnts = _pool_counts(tm, i)
        cnts_h = [c[:POOL_HALO] for c in _pool_counts(tm, i + 1)]
        r = jnp.concatenate([jnp.concatenate([dm[:, g * POOL_GD:(g + 1) * POOL_GD] / cnts[g] for g in range(4)], axis=-1),
                             jnp.concatenate([halo[:, g * POOL_GD:(g + 1) * POOL_GD] / cnts_h[g] for g in range(4)], axis=-1)],
                            axis=0)
        n = tm + POOL_HALO
        sums, acc, k = {}, r, 1
        while k < POOL_WINDOWS[-1]:
            acc = acc + pltpu.roll(acc, n - k, axis=0)
            k *= 2
            sums[k] = acc
        dp = [sums[w][:tm, g * POOL_GD:(g + 1) * POOL_GD] for g, w in enumerate(POOL_WINDOWS)]
        dp_ref[...] = (jnp.concatenate(dp, axis=-1) - dm).astype(dp_ref.dtype)

    return _pcall(
        body, name="pool_win_bwd", grid=(nt,),
        in_specs=[pl.BlockSpec((tm, POOL_W), lambda i: (i, 0)),
                  pl.BlockSpec((POOL_HALO, POOL_W), lambda i: (jnp.minimum((i + 1) * hb, S // POOL_HALO - 1), 0))],
        out_specs=pl.BlockSpec((tm, POOL_W), lambda i: (i, 0)),
        out_shape=_sds((S, POOL_W), BF16))(dmixed, dmixed)


def _xattn_probs(q, kv, h):
    hs = slice(h * XA_HD, (h + 1) * XA_HD)
    s = _dot(q[:, hs], kv[:, hs], _NT) * (XA_HD ** -0.5)
    s = s - jnp.max(s, axis=-1, keepdims=True)
    e = jnp.exp(s)
    return e / jnp.sum(e, axis=-1, keepdims=True)


def _xattn_fwd(proj, kv):
    S = proj.shape[0]

    def fn(q, kv):
        outs = []
        for h in range(XA_HEADS):
            p = _xattn_probs(q, kv, h)
            outs.append(_dot(p, kv[:, XA_W + h * XA_HD:XA_W + (h + 1) * XA_HD]))
        return jnp.concatenate(outs, axis=-1)

    return _rowwise("xattn_fwd", fn, [_Win(proj, XA_W, (3072 + 3072 + POOL_W) // XA_W)], [kv], [_sds((S, XA_W), BF16)])[0]


def _xattn_bwd(proj, kv, dxa):
    S = proj.shape[0]

    def fn(q, dxa, kv):
        dqs, dks, dvs = [], [], []
        for h in range(XA_HEADS):
            hs = slice(h * XA_HD, (h + 1) * XA_HD)
            vh = kv[:, XA_W + h * XA_HD:XA_W + (h + 1) * XA_HD]
            p = _xattn_probs(q, kv, h)
            dp = _dot(dxa[:, hs], vh, _NT)
            ds = p * (dp - jnp.sum(p * dp, axis=-1, keepdims=True)) * (XA_HD ** -0.5)
            dqs.append(_dot(ds, kv[:, hs]))
            dks.append(_dot(ds, q[:, hs], _TN))
            dvs.append(_dot(p, dxa[:, hs], _TN))
        return jnp.concatenate(dqs, axis=-1), jnp.concatenate(dks + dvs, axis=-1)

    return _rowwise("xattn_bwd", fn, [_Win(proj, XA_W, (3072 + 3072 + POOL_W) // XA_W), dxa], [kv],
                    [_sds((S, XA_W), BF16)], [_sds(kv.shape)])


def _mix_fwd(x1, mem, W):
    S = x1.shape[0]
    M = mem.shape[0]
    h = _rowwise("mix_pre", lambda x, g: _rms(x, g), [x1], [W["mix_pre_g"]], [_sds((S, D_MODEL), BF16)])[0]
    proj = _mm_nn("mix_proj", h, W["w_int"], tn=INT_BLK)
    o_raw, states = _gla_fwd(proj, W["w_fu_pad"], W["b_f"])
    ya_in = _rowwise("gla_out", _head_rms_gate, [o_raw, _Win(proj, GLA_DV, 2)], [W["gla_norm_g"]],
                     [_sds((S, GLA_DV), BF16)])[0]
    mixed, pool_out = _pool_fwd(proj, W["w_pool"], W["pool_scale"])
    mem_n = _rowwise("mem_norm", lambda m, g: _rms(m, g), [mem], [W["mem_norm_g"]], [_sds((M, D_MODEL), BF16)])[0]
    kv = _mm_nn("mem_kv", mem_n, W["w_mem_kv"])
    xa = _xattn_fwd(proj, kv)
    ya = _mm_nn("up_gla", ya_in, W["w_up_gla"])
    yb = _mm_nn("up_pool", pool_out, W["w_up_pool"])
    yc = _mm_nn("up_xattn", xa, W["w_up_xattn"])
    merged = _rowwise("merge", _merge, [_Win(proj, 3 * D_MODEL, 1), ya, yb, yc], [], [_sds((S, D_MODEL), BF16)])[0]
    y = _mm_nn("mix_out", merged, W["w_o"])
    x2 = _rowwise("mix_post", lambda x, y, g: x + _rms(y, g), [x1, y], [W["mix_post_g"]], [_sds((S, D_MODEL))])[0]
    return x2, (h, proj, o_raw, states, ya_in, mixed, pool_out, mem_n, kv, xa, ya, yb, yc, merged, y)


def _head(x3, tgt, final_g):
    S = x3.shape[0]

    def head(x3, t, g):
        out, vjp = jax.vjp(_rms, x3, g)
        e = out - t
        loss = 0.5 * jnp.sum(jnp.mean(e * e, axis=-1, keepdims=True), axis=0, keepdims=True)
        dx3, dg = vjp(e * (1.0 / D_MODEL))
        return dx3, jnp.broadcast_to(loss, (1, 128)), dg

    dx3, loss_acc, d_final_g = _rowwise("head", head, [x3, tgt], [final_g], [_sds((S, D_MODEL))],
                                        [_sds((1, 128)), _sds((1, D_MODEL))])
    return loss_acc[0, 0], dx3, d_final_g


def _mix_bwd(dx2, x1, mem, W, saved, on_grads=None):
    h, proj, o_raw, states, ya_in, mixed, pool_out, mem_n, kv, xa, ya, yb, yc, merged, y = saved
    S = x1.shape[0]

    def mix_post_bwd(y, dx2, g):
        return _rms_bwd(y, g, dx2)

    dy, d_mix_post_g = _rowwise("mix_post_bwd", mix_post_bwd, [y, dx2], [W["mix_post_g"]], [_sds((S, D_MODEL), BF16)],
                                [_sds((1, D_MODEL))])
    d_w_o = _mm_tn("d_w_o", merged, dy)
    dmerged = _mm_nt("d_merged", dy, W["w_o"])

    def merge_bwd(gates, ya, yb, yc, dm):
        _, vjp = jax.vjp(_merge, gates, ya, yb, yc)
        return vjp(dm)

    dgates, dya, dyb, dyc = _rowwise("merge_bwd", merge_bwd, [_Win(proj, 3 * D_MODEL, 1), ya, yb, yc, dmerged], [],
                                     [_sds((S, 3 * D_MODEL), BF16)] + [_sds((S, D_MODEL), BF16)] * 3)
    d_w_up_gla = _mm_tn("d_w_up_gla", ya_in, dya)
    d_w_up_pool = _mm_tn("d_w_up_pool", pool_out, dyb)
    d_w_up_xattn = _mm_tn("d_w_up_xattn", xa, dyc)
    d_ya_in = _mm_nt("d_ya_in", dya, W["w_up_gla"])
    d_pool_out = _mm_nt("d_pool_out", dyb, W["w_up_pool"])
    d_xa = _mm_nt("d_xa", dyc, W["w_up_xattn"])

    def gla_out_bwd(o, g_out, d, gn):
        _, vjp = jax.vjp(_head_rms_gate, o, g_out, gn)
        return vjp(d)

    do_raw, dg_out, d_gla_norm_g = _rowwise("gla_out_bwd", gla_out_bwd, [o_raw, _Win(proj, GLA_DV, 2), d_ya_in],
                                            [W["gla_norm_g"]], [_sds((S, GLA_DV)), _sds((S, GLA_DV), BF16)],
                                            [_sds((1, GLA_DV))])
    dq, dk, dv, dflow, d_wfu_pad, d_b_f = _gla_bwd(proj, W["w_fu_pad"], W["b_f"], states, do_raw)
    dmixed, d_w_pool, d_pool_scale = _pool_lin_bwd(d_pool_out, mixed, W["w_pool"], W["pool_scale"])
    dp_in = _pool_win_bwd(dmixed)
    dxq, dkv = _xattn_bwd(proj, kv, d_xa)
    d_w_mem_kv = _mm_tn("d_w_mem_kv", mem_n, dkv)
    dmem_n = _mm_nt("d_mem_n", dkv, W["w_mem_kv"])
    d_mem_norm_g = _rowwise("mem_norm_bwd", lambda m, d, g: _rms_bwd(m, g, d)[1], [mem, dmem_n], [W["mem_norm_g"]], [],
                            [_sds((1, D_MODEL))])[0]
    dproj = jnp.concatenate([dq, dk, dv, dg_out, dgates, dp_in, dxq, dflow], axis=-1)
    d_w_int = _mm_tn("d_w_int", h, dproj, tn=INT_BLK)
    grads = dict(
        w_int=d_w_int, w_fu_pad=d_wfu_pad, b_f=d_b_f, gla_norm_g=d_gla_norm_g, w_pool=d_w_pool,
        pool_scale=d_pool_scale, mem_norm_g=d_mem_norm_g, w_mem_kv=d_w_mem_kv, w_up_gla=d_w_up_gla,
        w_up_pool=d_w_up_pool, w_up_xattn=d_w_up_xattn, w_o=d_w_o, mix_post_g=d_mix_post_g)
    deps = [on_grads(grads)] if on_grads is not None else []
    dh = _mm(
        "d_mix_h", dproj, W["w_int"], grid=(1, S // min(512, S), INT_NBLK), a_blk=(min(512, S), INT_BLK),
        a_map=lambda j, i, k: (i, k), b_blk=(D_MODEL, INT_BLK), b_map=lambda j, i, k: (0, k), o_shape=(S, D_MODEL),
        o_blk=(min(512, S), D_MODEL), o_map=lambda j, i, k: (i, 0), dims=_NT, deps=deps)

    def mix_pre_bwd(x, dh, dx2, g):
        dx, dg = _rms_bwd(x, g, dh)
        return dx + dx2, dg

    dx1, grads["mix_pre_g"] = _rowwise("mix_pre_bwd", mix_pre_bwd, [x1, dh, dx2], [W["mix_pre_g"]],
                                       [_sds((S, D_MODEL))], [_sds((1, D_MODEL))])
    return dx1, grads


def _mesh_pos():
    x, y, c = lax.axis_index("x"), lax.axis_index("y"), lax.axis_index("c")
    return x, y, c, 4 * x + 2 * y + c


def _peer(x, y, c, r):
    px = 1 - x if r & 4 else x
    py = 1 - y if r & 2 else y
    pc = 1 - c if r & 1 else c
    return (px, py, pc), 4 * px + 2 * py + pc


_ALL_PEERS = tuple(range(1, N_DEV))
_SIBLING = 1
_SAME_CORE = (2, 4, 6)


def _dev_slot(ref, dev):
    return ref.at[dev]


class _Plan:
    def __init__(self, scatter, slots=None, shapes=None):
        self.scatter, self.slots, self.shapes = scatter, slots or {}, shapes or {}

    def src(self, srcs, a, dev):
        return self.slots.get(a, _dev_slot)(srcs[a], dev) if self.scatter else srcs[a]

    def dst(self, lands, a, dev):
        return lands[a].at[dev] if self.scatter else self.slots.get(a, _dev_slot)(lands[a], dev)

    def landing_zones(self, arrays):
        lands = []
        for a, arr in enumerate(arrays):
            if self.scatter:
                lands.append(lax.empty((N_DEV,) + tuple(self.shapes.get(a, arr.shape[1:])), arr.dtype))
            elif a in self.shapes:
                lands.append(jnp.zeros(self.shapes[a], arr.dtype))
            else:
                lands.append(lax.empty((N_DEV,) + arr.shape, arr.dtype))
        return lands


_GATHER, _SCATTER = _Plan(False), _Plan(True)


def _peer_copies(srcs, lands, send_sems, recv_sems, plan, peers=_ALL_PEERS):
    x, y, c, me = _mesh_pos()
    cps = []
    for r in peers:
        pos, peer = _peer(x, y, c, r)
        for a in range(len(srcs)):
            k = a * (N_DEV - 1) + r - 1
            cps.append(pltpu.make_async_remote_copy(
                src_ref=plan.src(srcs, a, peer), dst_ref=plan.dst(lands, a, me),
                send_sem=send_sems.at[k], recv_sem=recv_sems.at[k], device_id=pos,
                device_id_type=pl.DeviceIdType.MESH))
    return cps


def _exchange(name, arrays, plan):
    n = len(arrays)

    def body(*refs):
        ins, outs = refs[:n], refs[n:2 * n]
        send_sems, recv_sems, local_sems = refs[2 * n:]
        copies = _own_copies(ins, outs, local_sems, plan) + _peer_copies(ins, outs, send_sems, recv_sems, plan)
        for cp in copies:
            cp.start()
        for cp in copies:
            cp.wait()

    any_spec = pl.BlockSpec(memory_space=pl.ANY)
    out_shape = [_sds(l.shape, l.dtype) for l in plan.landing_zones(arrays)]
    return pl.pallas_call(
        body, name=name, in_specs=[any_spec] * n, out_specs=[any_spec] * n, out_shape=out_shape,
        scratch_shapes=[pltpu.SemaphoreType.DMA((n * (N_DEV - 1),)), pltpu.SemaphoreType.DMA((n * (N_DEV - 1),)),
                        pltpu.SemaphoreType.DMA((n,))])(*arrays)


_HBM = pl.BlockSpec(memory_space=pltpu.HBM)
_SEM = pl.BlockSpec(memory_space=pltpu.SEMAPHORE)
_EFFECT = pltpu.SideEffectType.DATAFLOW_SIDE_EFFECTING


def _own_copies(srcs, lands, own_sems, plan):
    me = _mesh_pos()[3]
    return [pltpu.make_async_copy(plan.src(srcs, a, me), plan.dst(lands, a, me), own_sems.at[a])
            for a in range(len(srcs))]


def _exchange_start(name, arrays, plan):
    n = len(arrays)
    lands = plan.landing_zones(arrays)
    n_sem = n * (N_DEV - 1)

    def body(*refs):
        srcs, lands_ = refs[:n], refs[n:2 * n]
        send_sems, recv_sems, own_sems = refs[2 * n:2 * n + 3]
        token = refs[-1]
        for cp in _peer_copies(srcs, lands_, send_sems, recv_sems, plan) + _own_copies(srcs, lands_, own_sems, plan):
            cp.start()
        token[...] = jnp.zeros_like(token)

    hbm = lambda a: pltpu.HBM(a.shape, a.dtype)
    res = pl.pallas_call(
        body, name=name,
        out_shape=(pltpu.SemaphoreType.DMA((n_sem,)), pltpu.SemaphoreType.DMA((n_sem,)), pltpu.SemaphoreType.DMA((n,)),
                   *[hbm(a) for a in arrays], *[hbm(a) for a in lands], _sds((8, 128))),
        in_specs=[_HBM] * (2 * n),
        out_specs=(_SEM, _SEM, _SEM, *[_HBM] * (2 * n), pl.BlockSpec(memory_space=pltpu.VMEM)),
        input_output_aliases={i: 3 + i for i in range(2 * n)},
        compiler_params=pltpu.CompilerParams(has_side_effects=_EFFECT),
    )(*[pltpu.with_memory_space_constraint(a, pltpu.HBM) for a in list(arrays) + lands])
    return (res[:3], res[3:3 + n], res[3 + n:3 + 2 * n], plan), res[-1]


def _exchange_wait(name, state, after):
    sems, srcs, lands, plan = state
    n = len(srcs)

    def body(*refs):
        srcs_, lands_ = refs[:n], refs[n:2 * n]
        send_sems, recv_sems, own_sems = refs[2 * n:2 * n + 3]
        for cp in _peer_copies(srcs_, lands_, send_sems, recv_sems, plan):
            cp.wait_send()
            cp.wait_recv()
        for cp in _own_copies(srcs_, lands_, own_sems, plan):
            cp.wait()

    hbm = lambda a: pltpu.HBM(a.shape, a.dtype)
    res = pl.pallas_call(
        body, name=name, out_shape=tuple(hbm(a) for a in list(srcs) + list(lands)),
        in_specs=[_HBM] * (2 * n) + [_SEM] * 3 + [pl.BlockSpec(memory_space=pl.ANY)] * len(after),
        out_specs=tuple([_HBM] * (2 * n)), input_output_aliases={i: i for i in range(2 * n)},
        compiler_params=pltpu.CompilerParams(has_side_effects=_EFFECT),
    )(*srcs, *lands, *sems, *after)
    return res[n:]


def _gather_start(name, arrays, plan, after):
    n = len(arrays)
    lands = plan.landing_zones(arrays)
    n_sem = n * (N_DEV - 1)

    def body(*refs):
        srcs, lands_ = refs[:n], refs[n:2 * n]
        send_sems, recv_sems, own_sems = refs[2 * n + len(after):2 * n + len(after) + 3]
        token = refs[-1]
        for cp in (_peer_copies(srcs, lands_, send_sems, recv_sems, plan, (_SIBLING,) + _SAME_CORE)
                   + _own_copies(srcs, lands_, own_sems, plan)):
            cp.start()
        token[...] = jnp.zeros_like(token)

    hbm = lambda a: pltpu.HBM(a.shape, a.dtype)
    res = pl.pallas_call(
        body, name=name,
        out_shape=(pltpu.SemaphoreType.DMA((n_sem,)), pltpu.SemaphoreType.DMA((n_sem,)), pltpu.SemaphoreType.DMA((n,)),
                   *[hbm(a) for a in arrays], *[hbm(a) for a in lands], _sds((8, 128))),
        in_specs=[_HBM] * (2 * n) + [pl.BlockSpec(memory_space=pl.ANY)] * len(after),
        out_specs=(_SEM, _SEM, _SEM, *[_HBM] * (2 * n), pl.BlockSpec(memory_space=pltpu.VMEM)),
        input_output_aliases={i: 3 + i for i in range(2 * n)},
        compiler_params=pltpu.CompilerParams(has_side_effects=_EFFECT),
    )(*[pltpu.with_memory_space_constraint(a, pltpu.HBM) for a in list(arrays) + lands], *after)
    return (res[:3], res[3:3 + n], res[3 + n:3 + 2 * n], plan), res[-1]


def _pass_on_copies(lands, send_sems, recv_sems, plan):
    x, y, c, _ = _mesh_pos()
    sibling = _peer(x, y, c, _SIBLING)[0]
    cps = []
    for i, r in enumerate(_SAME_CORE):
        owner = _peer(x, y, c, r)[1]
        for a in range(len(lands)):
            k = a * len(_SAME_CORE) + i
            cps.append(pltpu.make_async_remote_copy(
                src_ref=plan.dst(lands, a, owner), dst_ref=plan.dst(lands, a, owner), send_sem=send_sems.at[k],
                recv_sem=recv_sems.at[k], device_id=sibling, device_id_type=pl.DeviceIdType.MESH))
    return cps


def _gather_pass_on(name, state, after):
    sems, srcs, lands, plan = state
    n = len(srcs)
    n_sem = n * len(_SAME_CORE)

    def body(*refs):
        srcs_, lands_ = refs[:n], refs[n:2 * n]
        send_sems, recv_sems = refs[2 * n], refs[2 * n + 1]
        on_send, on_recv = refs[2 * n + 3 + len(after)], refs[2 * n + 4 + len(after)]
        arrivals = _peer_copies(srcs_, lands_, send_sems, recv_sems, plan, _SAME_CORE)
        for arrived, on in zip(arrivals, _pass_on_copies(lands_, on_send, on_recv, plan)):
            arrived.wait_recv()
            on.start()

    hbm = lambda a: pltpu.HBM(a.shape, a.dtype)
    res = pl.pallas_call(
        body, name=name,
        out_shape=(pltpu.SemaphoreType.DMA((n_sem,)), pltpu.SemaphoreType.DMA((n_sem,)),
                   *[hbm(a) for a in list(srcs) + list(lands)]),
        in_specs=[_HBM] * (2 * n) + [_SEM] * 3 + [pl.BlockSpec(memory_space=pl.ANY)] * len(after),
        out_specs=(_SEM, _SEM, *[_HBM] * (2 * n)), input_output_aliases={i: 2 + i for i in range(2 * n)},
        compiler_params=pltpu.CompilerParams(has_side_effects=_EFFECT),
    )(*srcs, *lands, *sems, *after)
    return sems, res[:2], res[2:2 + n], res[2 + n:], plan


def _gather_wait(name, state, after):
    sems, on_sems, srcs, lands, plan = state
    n = len(srcs)

    def body(*refs):
        srcs_, lands_ = refs[:n], refs[n:2 * n]
        send_sems, recv_sems, own_sems, on_send, on_recv = refs[2 * n:2 * n + 5]
        for cp in _peer_copies(srcs_, lands_, send_sems, recv_sems, plan, (_SIBLING,)):
            cp.wait_recv()
        for cp in _peer_copies(srcs_, lands_, send_sems, recv_sems, plan, (_SIBLING,) + _SAME_CORE):
            cp.wait_send()
        for cp in _own_copies(srcs_, lands_, own_sems, plan):
            cp.wait()
        for cp in _pass_on_copies(lands_, on_send, on_recv, plan):
            cp.wait_send()
            cp.wait_recv()

    hbm = lambda a: pltpu.HBM(a.shape, a.dtype)
    res = pl.pallas_call(
        body, name=name, out_shape=tuple(hbm(a) for a in list(srcs) + list(lands)),
        in_specs=[_HBM] * (2 * n) + [_SEM] * 5 + [pl.BlockSpec(memory_space=pl.ANY)] * len(after),
        out_specs=tuple([_HBM] * (2 * n)), input_output_aliases={i: i for i in range(2 * n)},
        compiler_params=pltpu.CompilerParams(has_side_effects=_EFFECT),
    )(*srcs, *lands, *sems, *on_sems, *after)
    return res[n:]


def _adam(name, recv, w, m, v):
    shape = w.shape
    R, C = shape
    tr = R
    while N_DEV * tr * C * 4 > 6 * 1024 * 1024 and tr % 32 == 0:
        tr //= 2

    def body(recv_ref, w_ref, m_ref, v_ref, g_ref, d_ref, m2_ref, v2_ref):
        g = recv_ref[0].astype(F32)
        for j in range(1, N_DEV):
            g = g + recv_ref[j].astype(F32)
        w_, m_, v_ = w_ref[...], m_ref[...], v_ref[...]
        m2 = ADAM_B1 * m_ + (1.0 - ADAM_B1) * g
        v2 = ADAM_B2 * v_ + (1.0 - ADAM_B2) * (g * g)
        m_hat = m2 / (1.0 - ADAM_B1 ** ADAM_STEP)
        v_hat = v2 / (1.0 - ADAM_B2 ** ADAM_STEP)
        g_ref[...] = g
        d_ref[...] = -ADAM_LR * (m_hat / (jnp.sqrt(v_hat) + ADAM_EPS) + ADAM_WD * w_)
        m2_ref[...] = m2
        v2_ref[...] = v2

    blk = pl.BlockSpec((tr, C), lambda i: (i, 0))
    return _pcall(body, name=name, grid=(R // tr,),
                  in_specs=[pl.BlockSpec((N_DEV, tr, C), lambda i: (0, i, 0)), blk, blk, blk],
                  out_specs=[blk] * 4, out_shape=[_sds(shape)] * 4)(recv, w, m, v)


_NAMES = ['ffn1_pre_g', 'ffn1_w_in', 'ffn1_w_out', 'ffn1_post_g', 'mix_pre_g', 'w_in', 'w_fu', 'b_f', 'gla_norm_g',
          'w_pool', 'pool_scale', 'mem_norm_g', 'w_mem_kv', 'w_up_gla', 'w_up_pool', 'w_up_xattn', 'w_o', 'mix_post_g',
          'ffn2_pre_g', 'ffn2_w_in', 'ffn2_w_out', 'ffn2_post_g', 'final_g']
_SHARDED = ['ffn1_w_in', 'ffn1_w_out', 'w_in', 'w_fu', 'w_mem_kv', 'w_up_gla', 'w_up_pool', 'w_up_xattn', 'w_o',
            'ffn2_w_in', 'ffn2_w_out']
_COL_SHARDED = ['w_up_pool', 'w_up_xattn']


def _cols_to_full(g):
    return jnp.transpose(g, (1, 0, 2)).reshape(g.shape[1], N_DEV * g.shape[2])


def _full_to_cols(f):
    R, C = f.shape
    return jnp.transpose(f.reshape(R, N_DEV, C // N_DEV), (1, 0, 2))


def _to_internal(w_in_full):
    o = 0
    parts = []
    for s in IN_SPLITS:
        parts.append(w_in_full[:, o:o + s])
        o += s
    q, k, v, g_out, f_low, p_in, xq, gates = parts
    f_low = jnp.pad(f_low, ((0, 0), (0, FLOW_W - GATE_RANK)))
    return jnp.concatenate([q, k, v, g_out, gates, p_in, xq, f_low], axis=-1)


def _from_internal(d):
    q, k, v, g_out = d[:, 0:512], d[:, 512:1024], d[:, 1024:2048], d[:, 2048:3072]
    gates, p_in, xq = d[:, 3072:6144], d[:, 6144:6656], d[:, 6656:7168]
    f_low = d[:, 7168:7168 + GATE_RANK]
    return jnp.concatenate([q, k, v, g_out, f_low, p_in, xq, gates], axis=-1)


_FFN_IN = ('ffn1_w_in', 'ffn2_w_in')
_FFN_OUT = ('ffn1_w_out', 'ffn2_w_out')
_GATHERS = {"ffn1_in": ['ffn1_w_in'], "ffn1_out": ['ffn1_w_out'],
            "mix": ['w_in', 'w_fu', 'w_mem_kv', 'w_up_gla', 'w_up_pool', 'w_up_xattn', 'w_o'],
            "ffn2": ['ffn2_w_in', 'ffn2_w_out']}


def _ffn_in_slot(ref, d):
    return ref.at[2 * (d % N_FF_BLK) + d // N_FF_BLK, pl.ds(0, FF_BLK)]


def _ffn_out_slot(ref, d):
    rows = FF_BLK // 2
    return ref.at[d // 2, pl.ds(pl.multiple_of((d % 2) * rows, rows), rows)]


def _ffn_plan(names, scatter):
    slots, shapes = {}, {}
    for a, n in enumerate(names):
        if n in _FFN_IN:
            slots[a] = _ffn_in_slot
            shapes[a] = (FF_BLK, D_MODEL) if scatter else (N_DEV, FF_PAD, D_MODEL)
        elif n in _FFN_OUT:
            slots[a] = _ffn_out_slot
            shapes[a] = (FF_BLK // 2, D_MODEL) if scatter else (N_FF_BLK, FF_PAD, D_MODEL)
    return _Plan(scatter, slots, shapes)


def _rows_to_full(g):
    return g.reshape(N_DEV * g.shape[1], g.shape[2])


def _mix_weights(gathered):
    W = {n: _rows_to_full(gathered[n]) for n in ('w_mem_kv', 'w_up_gla', 'w_o')}
    for n in _COL_SHARDED:
        W[n] = _cols_to_full(gathered[n])
    W["w_int"] = _to_internal(_cols_to_full(gathered["w_in"]))
    W["w_fu_pad"] = jnp.pad(_cols_to_full(gathered["w_fu"]), ((0, FLOW_W - GATE_RANK), (0, 0)))
    return W


def _mix_chunks(G, n):
    if n == "w_in":
        return _full_to_cols(_from_internal(G["w_int"]))
    if n == "w_fu":
        return _full_to_cols(G["w_fu_pad"][:GATE_RANK].astype(BF16))
    if n in _COL_SHARDED:
        return _full_to_cols(G[n])
    return G[n].reshape(N_DEV, G[n].shape[0] // N_DEV, G[n].shape[1])


def _step(x, mem, tgt, P, Mo, Vo):
    def native(n, a):
        return jnp.swapaxes(a, 0, 1) if n in _FFN_IN else a

    P, Mo, Vo = ({n: native(n, a) for n, a in d.items()} for d in (P, Mo, Vo))
    small = {n: P[n] for n in _NAMES if n not in _SHARDED}

    gather, token = {}, []
    for grp, names in _GATHERS.items():
        gather[grp], tok = _gather_start("gather_" + grp, [P[n].astype(BF16) for n in names], _ffn_plan(names, False),
                                         token)
        token = [tok]

    def gathered(grp, after):
        return _gather_wait("gather_" + grp + "_wait", _gather_pass_on("gather_" + grp + "_on", gather[grp], after), [])

    def ffn_wo(lands):
        return lambda act: lands[0].reshape(N_FF_BLK * FF_PAD, D_MODEL)

    w1t = gathered("ffn1_in", token)[0].reshape(N_DEV * FF_PAD, D_MODEL)
    x1, w1o, sv1 = _ffn_fwd("ffn1", x, small["ffn1_pre_g"], w1t, small["ffn1_post_g"],
                            lambda act: ffn_wo(gathered("ffn1_out", [act]))(act))

    gm = dict(zip(_GATHERS["mix"], gathered("mix", [x1])))
    Wm = {**small, **_mix_weights(gm)}
    x2, svm = _mix_fwd(x1, mem, Wm)

    w2t, w2o = gathered("ffn2", [x2])
    w2t = w2t.reshape(N_DEV * FF_PAD, D_MODEL)
    x3, w2o, sv2 = _ffn_fwd("ffn2", x2, small["ffn2_pre_g"], w2t, small["ffn2_post_g"], ffn_wo([w2o]))

    loss, dx3, d_final_g = _head(x3, tgt, small["final_g"])

    G = dict(final_g=d_final_g)
    scat = {}

    def start(grp, names, arrays):
        scat[grp] = names, _exchange_start("scatter_" + grp, arrays, _ffn_plan(names, True))
        return scat[grp][1][1]

    def ffn_starts(tag):
        return (lambda dwo: start(tag + "_out", [tag + "_w_out"], [dwo.reshape(N_FF_BLK, FF_PAD, D_MODEL)]),
                lambda dwt: start(tag + "_in", [tag + "_w_in"], [dwt.reshape(N_DEV, FF_PAD, D_MODEL)]))

    dx2, G["ffn2_pre_g"], G["ffn2_post_g"] = _ffn_bwd(
        "ffn2", dx3, x2, small["ffn2_pre_g"], w2t, w2o, small["ffn2_post_g"], sv2, *ffn_starts("ffn2"))
    dx1, Gm = _mix_bwd(dx2, x1, mem, Wm, svm,
                       lambda Gm: start("mix", _GATHERS["mix"], [_mix_chunks(Gm, n) for n in _GATHERS["mix"]]))
    G.update(Gm)
    dx, G["ffn1_pre_g"], G["ffn1_post_g"] = _ffn_bwd(
        "ffn1", dx1, x, small["ffn1_pre_g"], w1t, w1o, small["ffn1_post_g"], sv1, *ffn_starts("ffn1"))

    recv = dict(zip(small, _exchange("gather_small_grads", [G[n] for n in small], _GATHER)))
    outs = {}
    last = dx
    for grp in ["small"] + list(scat):
        names = list(small)
        if grp != "small":
            names, (state, _) = scat[grp]
            recv.update(zip(names, _exchange_wait("scatter_" + grp + "_wait", state, [last])))
        for n in names:
            shp = P[n].shape
            shp2 = shp if len(shp) == 2 else (shp[0] * shp[1], shp[2])
            res = _adam("adam_" + n, recv[n].reshape((N_DEV,) + shp2), P[n].reshape(shp2), Mo[n].reshape(shp2),
                        Vo[n].reshape(shp2))
            outs[n] = [native(n, t.reshape(shp)) for t in res]
            last = res[-1]
    return loss, dx, outs


def kernel(x, mem, ffn1_pre_g, ffn1_w_in, ffn1_w_out, ffn1_post_g, mix_pre_g, w_in, w_fu, b_f, gla_norm_g, w_pool, pool_scale, mem_norm_g, w_mem_kv, w_up_gla, w_up_pool, w_up_xattn, w_o, mix_post_g, ffn2_pre_g, ffn2_w_in, ffn2_w_out, ffn2_post_g, final_g, loss_target, m_ffn1_pre_g, m_ffn1_w_in, m_ffn1_w_out, m_ffn1_post_g, m_mix_pre_g, m_w_in, m_w_fu, m_b_f, m_gla_norm_g, m_w_pool, m_pool_scale, m_mem_norm_g, m_w_mem_kv, m_w_up_gla, m_w_up_pool, m_w_up_xattn, m_w_o, m_mix_post_g, m_ffn2_pre_g, m_ffn2_w_in, m_ffn2_w_out, m_ffn2_post_g, m_final_g, v_ffn1_pre_g, v_ffn1_w_in, v_ffn1_w_out, v_ffn1_post_g, v_mix_pre_g, v_w_in, v_w_fu, v_b_f, v_gla_norm_g, v_w_pool, v_pool_scale, v_mem_norm_g, v_w_mem_kv, v_w_up_gla, v_w_up_pool, v_w_up_xattn, v_w_o, v_mix_post_g, v_ffn2_pre_g, v_ffn2_w_in, v_ffn2_w_out, v_ffn2_post_g, v_final_g):
    params = [ffn1_pre_g, ffn1_w_in, ffn1_w_out, ffn1_post_g, mix_pre_g, w_in, w_fu, b_f, gla_norm_g, w_pool, pool_scale, mem_norm_g, w_mem_kv, w_up_gla, w_up_pool, w_up_xattn, w_o, mix_post_g, ffn2_pre_g, ffn2_w_in, ffn2_w_out, ffn2_post_g, final_g]
    moms = [m_ffn1_pre_g, m_ffn1_w_in, m_ffn1_w_out, m_ffn1_post_g, m_mix_pre_g, m_w_in, m_w_fu, m_b_f, m_gla_norm_g, m_w_pool, m_pool_scale, m_mem_norm_g, m_w_mem_kv, m_w_up_gla, m_w_up_pool, m_w_up_xattn, m_w_o, m_mix_post_g, m_ffn2_pre_g, m_ffn2_w_in, m_ffn2_w_out, m_ffn2_post_g, m_final_g]
    vars_ = [v_ffn1_pre_g, v_ffn1_w_in, v_ffn1_w_out, v_ffn1_post_g, v_mix_pre_g, v_w_in, v_w_fu, v_b_f, v_gla_norm_g, v_w_pool, v_pool_scale, v_mem_norm_g, v_w_mem_kv, v_w_up_gla, v_w_up_pool, v_w_up_xattn, v_w_o, v_mix_post_g, v_ffn2_pre_g, v_ffn2_w_in, v_ffn2_w_out, v_ffn2_post_g, v_final_g]
    P = {n: a[0] if a.ndim > 2 else a for n, a in zip(_NAMES, params)}
    Mo = {n: a[0] if a.ndim > 2 else a for n, a in zip(_NAMES, moms)}
    Vo = {n: a[0] if a.ndim > 2 else a for n, a in zip(_NAMES, vars_)}
    loss, dx, outs = _step(x[0], mem[0], loss_target[0], P, Mo, Vo)
    loss = lax.psum(loss, ("x", "y", "c"))
    out = [loss, dx[None]]
    for kind in range(4):
        for n, p in zip(_NAMES, params):
            out.append(outs[n][kind].reshape(p.shape))
    return tuple(out)
```

```python
import functools

import jax
import jax.numpy as jnp
from jax import lax
from jax.experimental import pallas as pl
from jax.experimental.pallas import tpu as pltpu

F32 = jnp.float32
BF16 = jnp.bfloat16

N_DEV = 8
D_MODEL = 1024
D_FF = 2816
FF_BLK = 2 * D_FF // N_DEV
N_FF_BLK = D_FF // FF_BLK
FF_PAD = 768
CHUNK = 64
GLA_HEADS = 4
GLA_DK = 512
GLA_DV = 1024
GLA_HDK = 128
GLA_HDV = 256
GATE_RANK = 16
GATE_TEMP = 16.0
POOL_WINDOWS = (2, 4, 8, 16)
POOL_W = 512
POOL_GD = 128
POOL_HALO = 16
XA_HEADS = 4
XA_HD = 128
XA_W = 512
EPS = 1e-6
IN_SPLITS = (GLA_DK, GLA_DK, GLA_DV, GLA_DV, GATE_RANK, POOL_W, XA_W, 3 * D_MODEL)
IN_WIDTH = sum(IN_SPLITS)
IN_SHARD = IN_WIDTH // N_DEV
INT_W = 3072 + 3072 + 1024 + 128
INT_NBLK = 3
INT_BLK = INT_W // INT_NBLK
FLOW_W = 128

ADAM_LR = 0.001
ADAM_B1 = 0.9
ADAM_B2 = 0.999
ADAM_EPS = 1e-08
ADAM_WD = 0.01
ADAM_STEP = 10

VMEM_LIMIT = 56 * 1024 * 1024

_NN = (((1,), (0,)), ((), ()))
_NT = (((1,), (1,)), ((), ()))
_TN = (((0,), (0,)), ((), ()))


def _pcall(body, *, name, grid, in_specs, out_specs, out_shape, scratch=()):
    return pl.pallas_call(
        body, name=name, grid=grid, in_specs=in_specs, out_specs=out_specs, out_shape=out_shape,
        scratch_shapes=list(scratch),
        compiler_params=pltpu.CompilerParams(dimension_semantics=("arbitrary",) * len(grid),
                                             vmem_limit_bytes=VMEM_LIMIT))


def _dot(a, b, dims=_NN):
    return lax.dot_general(a.astype(BF16), b.astype(BF16), dims, preferred_element_type=F32)


def _mm(name, a, b, *, grid, a_blk, a_map, b_blk, b_map, o_shape, o_blk, o_map, dims, out_dtype=F32, deps=()):
    nk = grid[2]

    def body(a_ref, b_ref, *rest):
        o_ref, scr = rest[len(deps)], rest[len(deps) + 1:]
        p = _dot(a_ref[...], b_ref[...], dims)
        if nk == 1:
            o_ref[...] = p.astype(o_ref.dtype)
        else:
            acc = scr[0]
            k = pl.program_id(2)

            @pl.when(k == 0)
            def _():
                acc[...] = p

            @pl.when(k > 0)
            def _():
                acc[...] += p

            @pl.when(k == nk - 1)
            def _():
                o_ref[...] = acc[...].astype(o_ref.dtype)

    acc_shape = tuple(d for d in o_blk if d is not None)
    return _pcall(body, name=name, grid=grid,
                  in_specs=[pl.BlockSpec(a_blk, a_map), pl.BlockSpec(b_blk, b_map)]
                  + [pl.BlockSpec(memory_space=pl.ANY)] * len(deps),
                  out_specs=pl.BlockSpec(o_blk, o_map),
                  out_shape=jax.ShapeDtypeStruct(o_shape, out_dtype),
                  scratch=[pltpu.VMEM(acc_shape, F32)] if nk > 1 else [])(a, b, *deps)


def _mm_nn(name, a, b, out_dtype=F32, tm=512, tn=None):
    M, K = a.shape
    N = b.shape[1]
    tm, tn = min(tm, M), (tn or N)
    return _mm(name, a, b, grid=(N // tn, M // tm, 1), a_blk=(tm, K), a_map=lambda j, i, k: (i, 0),
               b_blk=(K, tn), b_map=lambda j, i, k: (0, j), o_shape=(M, N), o_blk=(tm, tn),
               o_map=lambda j, i, k: (i, j), dims=_NN, out_dtype=out_dtype)


def _mm_nt(name, a, b, out_dtype=F32, tm=512):
    M, K = a.shape
    N = b.shape[0]
    tm = min(tm, M)
    return _mm(name, a, b, grid=(1, M // tm, 1), a_blk=(tm, K), a_map=lambda j, i, k: (i, 0),
               b_blk=(N, K), b_map=lambda j, i, k: (0, 0), o_shape=(M, N), o_blk=(tm, N),
               o_map=lambda j, i, k: (i, 0), dims=_NT, out_dtype=out_dtype)


def _mm_tn(name, a, b, out_dtype=BF16, ts=512, tn=None, tn_a=None):
    S, M = a.shape
    N = b.shape[1]
    ts, tn, tn_a = min(ts, S), (tn or N), (tn_a or M)
    return _mm(name, a, b, grid=((N // tn) * (M // tn_a), 1, S // ts), a_blk=(ts, tn_a),
               a_map=lambda j, i, k: (k, j if tn_a < M else 0), b_blk=(ts, tn),
               b_map=lambda j, i, k: (k, j if tn < N else 0), o_shape=(M, N), o_blk=(tn_a, tn),
               o_map=lambda j, i, k: (j, 0) if tn_a < M else (0, j), dims=_TN, out_dtype=out_dtype)


class _Win:
    def __init__(self, arr, w, c):
        self.arr, self.w, self.c = arr, w, c


def _row_spec(x, tm):
    if isinstance(x, _Win):
        return x.arr, pl.BlockSpec((tm, x.w), functools.partial(lambda i, c: (i, c), c=x.c))
    if x.ndim == 3:
        return x, pl.BlockSpec((x.shape[0], tm, x.shape[2]), lambda i: (0, i, 0))
    return x, pl.BlockSpec((tm, x.shape[1]), lambda i: (i, 0))


def _rowwise(name, fn, rows, consts, outs, accs=(), tm=256):
    first = rows[0].arr if isinstance(rows[0], _Win) else rows[0]
    S = first.shape[1] if first.ndim == 3 else first.shape[0]
    tm = min(tm, S)
    n_in, n_out = len(rows) + len(consts), len(outs)
    arrays, in_specs = [], []
    for r in rows:
        arr, spec = _row_spec(r, tm)
        arrays.append(arr)
        in_specs.append(spec)
    for c in consts:
        arrays.append(c)
        in_specs.append(pl.BlockSpec(c.shape, functools.partial(lambda i, n: (0,) * n, n=c.ndim)))
    out_specs = [_row_spec(o, tm)[1] for o in outs]
    out_specs += [pl.BlockSpec(a.shape, functools.partial(lambda i, n: (0,) * n, n=len(a.shape))) for a in accs]

    def body(*refs):
        res = fn(*[r[...] for r in refs[:n_in]])
        if not isinstance(res, (tuple, list)):
            res = (res,)
        orefs = refs[n_in:]
        for r, v in zip(orefs[:n_out], res[:n_out]):
            r[...] = v.astype(r.dtype)
        i = pl.program_id(0)
        for r, v in zip(orefs[n_out:], res[n_out:]):
            @pl.when(i == 0)
            def _(r=r, v=v):
                r[...] = v.astype(r.dtype)

            @pl.when(i > 0)
            def _(r=r, v=v):
                r[...] += v.astype(r.dtype)

    res = _pcall(body, name=name, grid=(S // tm,), in_specs=in_specs, out_specs=out_specs,
                 out_shape=[jax.ShapeDtypeStruct(o.shape, o.dtype) for o in list(outs) + list(accs)])(*arrays)
    return res


def _sds(shape, dtype=F32):
    return jax.ShapeDtypeStruct(shape, dtype)


def _rms(x, g):
    return x * lax.rsqrt(jnp.mean(x * x, axis=-1, keepdims=True) + EPS) * g


def _rms_bwd(x, g, dy):
    _, vjp = jax.vjp(_rms, x, g)
    return vjp(dy)


def _sigmoid(x):
    return 0.5 * jnp.tanh(0.5 * x) + 0.5


def _silu(x):
    return x * _sigmoid(x)


def _log_sigmoid(f):
    return jnp.minimum(f, 0.0) - jnp.log(1.0 + jnp.exp(-jnp.abs(f)))


def _head_rms_gate(o, g_out, gn):
    parts = [_rms(o[:, h * GLA_HDV:(h + 1) * GLA_HDV], gn[:, h * GLA_HDV:(h + 1) * GLA_HDV]) for h in range(GLA_HEADS)]
    return jnp.concatenate(parts, axis=-1) * _silu(g_out.astype(F32))


def _merge(gates, ya, yb, yc):
    gates, ya, yb, yc = (t.astype(F32) for t in (gates, ya, yb, yc))
    return (_sigmoid(gates[:, :D_MODEL]) * ya + _sigmoid(gates[:, D_MODEL:2 * D_MODEL]) * yb
            + _sigmoid(gates[:, 2 * D_MODEL:]) * yc)


def _tri_dot(t, x):
    hi = x.astype(BF16)
    r1 = x - hi.astype(F32)
    mid = r1.astype(BF16)
    lo = (r1 - mid.astype(F32)).astype(BF16)
    d = functools.partial(lax.dot_general, dimension_numbers=_NN, preferred_element_type=F32)
    return d(t, hi) + d(t, mid) + d(t, lo)


def _ffn_fwd(tag, x, gpre, wt, gpost, get_wo):
    S = x.shape[0]
    h = _rowwise(tag + "_pre", lambda x, g: _rms(x, g), [x], [gpre], [_sds((S, D_MODEL), BF16)])[0]
    tm = min(512, S)

    def up_body(h_ref, w_ref, u_ref, act_ref):
        ab = _dot(h_ref[...], w_ref[...], _NT)
        u_ref[...] = ab.astype(BF16)
        act_ref[...] = (_silu(ab[:, :FF_PAD]) * ab[:, FF_PAD:]).astype(BF16)

    u, act = _pcall(
        up_body, name=tag + "_up", grid=(N_FF_BLK, S // tm),
        in_specs=[pl.BlockSpec((tm, D_MODEL), lambda j, i: (i, 0)),
                  pl.BlockSpec((2 * FF_PAD, D_MODEL), lambda j, i: (j, 0))],
        out_specs=[pl.BlockSpec((tm, 2 * FF_PAD), lambda j, i: (i, j)),
                   pl.BlockSpec((tm, FF_PAD), lambda j, i: (i, j))],
        out_shape=[_sds((S, N_DEV * FF_PAD), BF16), _sds((S, N_FF_BLK * FF_PAD), BF16)],
    )(h, wt)

    def down_body(act_ref, w_ref, x_ref, g_ref, f_ref, xo_ref):
        f = _dot(act_ref[...], w_ref[...])
        f_ref[...] = f
        xo_ref[...] = x_ref[...] + 0.5 * _rms(f, g_ref[...])

    wo = get_wo(act)
    row = pl.BlockSpec((tm, D_MODEL), lambda i: (i, 0))
    f, xo = _pcall(
        down_body, name=tag + "_down", grid=(S // tm,),
        in_specs=[pl.BlockSpec((tm, N_FF_BLK * FF_PAD), lambda i: (i, 0)),
                  pl.BlockSpec(wo.shape, lambda i: (0, 0), pipeline_mode=pl.Buffered(1)), row,
                  pl.BlockSpec((1, D_MODEL), lambda i: (0, 0))],
        out_specs=[row, row], out_shape=[_sds((S, D_MODEL)), _sds((S, D_MODEL))],
    )(act, wo, x, gpost)
    return xo, wo, (h, u, act, f)


def _ffn_bwd(tag, dxo, x, gpre, wt, wo, gpost, saved, on_dwo=None, on_dwt=None):
    h, u, act, f = saved
    S = x.shape[0]
    tm = min(512, S)
    dep_spec = pl.BlockSpec(memory_space=pl.ANY)

    def post_bwd(f, dxo, g):
        df, dg = _rms_bwd(f, g, 0.5 * dxo)
        return df, dg

    df, dgpost = _rowwise(tag + "_post_bwd", post_bwd, [f, dxo], [gpost], [_sds((S, D_MODEL), BF16)],
                          [_sds((1, D_MODEL))])
    dwo = _mm_tn(tag + "_dwo", act, df, ts=2048, tn_a=FF_PAD)
    deps = [on_dwo(dwo)] if on_dwo is not None else []

    def dact_body(df_ref, w_ref, u_ref, *rest):
        du_ref = rest[-1]
        dact = _dot(df_ref[...], w_ref[...], _NT)
        ab = u_ref[...].astype(F32)
        a, b = ab[:, :FF_PAD], ab[:, FF_PAD:]
        sg = _sigmoid(a)
        du_ref[:, :FF_PAD] = (dact * b * (sg * (1.0 + a * (1.0 - sg)))).astype(BF16)
        du_ref[:, FF_PAD:] = (dact * (a * sg)).astype(BF16)

    u_spec = pl.BlockSpec((tm, 2 * FF_PAD), lambda j, i: (i, j))
    du = _pcall(
        dact_body, name=tag + "_dact", grid=(N_FF_BLK, S // tm),
        in_specs=[pl.BlockSpec((tm, D_MODEL), lambda j, i: (i, 0)),
                  pl.BlockSpec((FF_PAD, D_MODEL), lambda j, i: (j, 0)), u_spec] + [dep_spec] * len(deps),
        out_specs=u_spec, out_shape=_sds((S, N_DEV * FF_PAD), BF16))(df, wo, u, *deps)
    dwt = _mm_tn(tag + "_dwt", du, h, ts=2048, tn_a=FF_PAD)
    dx, dgpre = _dh_pre_norm_bwd(tag + "_dh", du, wt, x, dxo, gpre, [on_dwt(dwt)] if on_dwt is not None else [])
    return dx, dgpre, dgpost


def _dh_pre_norm_bwd(name, dz, wt, x, dres, g, deps):
    S, K = dz.shape

    def body(dz_ref, w_ref, x_ref, dres_ref, g_ref, *rest):
        dx_ref, dg_ref = rest[len(deps):]
        i = pl.program_id(0)
        dx, dg = _rms_bwd(x_ref[...], g_ref[...], _dot(dz_ref[...], w_ref[...]))
        dx_ref[...] = dx + dres_ref[...]

        @pl.when(i == 0)
        def _():
            dg_ref[...] = dg

        @pl.when(i > 0)
        def _():
            dg_ref[...] += dg

    th = min(256, S)
    row = pl.BlockSpec((th, D_MODEL), lambda i: (i, 0))
    vec = pl.BlockSpec((1, D_MODEL), lambda i: (0, 0))
    return _pcall(
        body, name=name, grid=(S // th,),
        in_specs=[pl.BlockSpec((th, K), lambda i: (i, 0)),
                  pl.BlockSpec(wt.shape, lambda i: (0, 0), pipeline_mode=pl.Buffered(1)), row, row, vec]
        + [pl.BlockSpec(memory_space=pl.ANY)] * len(deps),
        out_specs=[row, vec], out_shape=[_sds((S, D_MODEL)), _sds((1, D_MODEL))])(dz, wt, x, dres, g, *deps)


def _gla_chunk_common(k, flow, wfu, bf, tri):
    f = _dot(flow, wfu) + bf
    la = _log_sigmoid(f) * (1.0 / GATE_TEMP)
    b = _tri_dot(tri, la)
    b_end = b[CHUNK - 1:CHUNK, :]
    w = jnp.exp(b_end - b)
    kt = k * w
    d = jnp.exp(b_end)
    return f, w, kt, d


def _tri_matrix(lower):
    r = lax.broadcasted_iota(jnp.int32, (CHUNK, CHUNK), 0)
    c = lax.broadcasted_iota(jnp.int32, (CHUNK, CHUNK), 1)
    return jnp.where((r >= c) if lower else (r <= c), 1.0, 0.0).astype(BF16)


def _gla_fwd(proj, wfu, bf):
    S = proj.shape[0]
    nc = S // CHUNK

    def body(q_ref, k_ref, v_ref, flow_ref, wfu_ref, bf_ref, o_ref, st_ref, state):
        c = pl.program_id(0)

        @pl.when(c == 0)
        def _():
            state[...] = jnp.zeros_like(state)

        _, _, kt, d = _gla_chunk_common(k_ref[...], flow_ref[...], wfu_ref[...], bf_ref[...], _tri_matrix(True))
        q = q_ref[...].astype(F32) * (GLA_HDK ** -0.5)
        v = v_ref[...]
        outs = []
        for h in range(GLA_HEADS):
            ks, vs = slice(h * GLA_HDK, (h + 1) * GLA_HDK), slice(h * GLA_HDV, (h + 1) * GLA_HDV)
            st = state[vs, :] * d[:, ks] + _dot(v[:, vs], kt[:, ks], _TN)
            state[vs, :] = st
            outs.append(_dot(q[:, ks], st, _NT))
        o_ref[...] = jnp.concatenate(outs, axis=-1)
        st_ref[...] = state[...]

    return _pcall(
        body, name="gla_fwd", grid=(nc,),
        in_specs=[pl.BlockSpec((CHUNK, GLA_DK), lambda c: (c, 0)), pl.BlockSpec((CHUNK, GLA_DK), lambda c: (c, 1)),
                  pl.BlockSpec((CHUNK, GLA_DV), lambda c: (c, 1)),
                  pl.BlockSpec((CHUNK, FLOW_W), lambda c: (c, (INT_W - FLOW_W) // FLOW_W)),
                  pl.BlockSpec(wfu.shape, lambda c: (0, 0)), pl.BlockSpec(bf.shape, lambda c: (0, 0))],
        out_specs=[pl.BlockSpec((CHUNK, GLA_DV), lambda c: (c, 0)),
                   pl.BlockSpec((None, GLA_DV, GLA_HDK), lambda c: (c, 0, 0))],
        out_shape=[_sds((S, GLA_DV)), _sds((nc, GLA_DV, GLA_HDK))],
        scratch=[pltpu.VMEM((GLA_DV, GLA_HDK), F32)])(proj, proj, proj, proj, wfu, bf)


def _gla_bwd(proj, wfu, bf, states, do):
    S = proj.shape[0]
    nc = S // CHUNK

    def body(q_ref, k_ref, v_ref, flow_ref, wfu_ref, bf_ref, st_ref, stp_ref, do_ref,
             dq_ref, dk_ref, dv_ref, dflow_ref, dwfu_ref, dbf_ref, dstate):
        step = pl.program_id(0)
        c = nc - 1 - step

        @pl.when(step == 0)
        def _():
            dstate[...] = jnp.zeros_like(dstate)
            dwfu_ref[...] = jnp.zeros_like(dwfu_ref)
            dbf_ref[...] = jnp.zeros_like(dbf_ref)

        k, flow, wfu_v = k_ref[...], flow_ref[...], wfu_ref[...]
        f, w, kt, d = _gla_chunk_common(k, flow, wfu_v, bf_ref[...], _tri_matrix(True))
        q = q_ref[...].astype(F32) * (GLA_HDK ** -0.5)
        v, dout = v_ref[...], do_ref[...]
        has_prev = (c > 0).astype(F32)
        dqs, dkts, dvs, dds = [], [], [], []
        for h in range(GLA_HEADS):
            ks, vs = slice(h * GLA_HDK, (h + 1) * GLA_HDK), slice(h * GLA_HDV, (h + 1) * GLA_HDV)
            st = st_ref[vs, :]
            dqs.append(_dot(dout[:, vs], st))
            dst = dstate[vs, :] + _dot(dout[:, vs], q[:, ks], _TN)
            dkts.append(_dot(v[:, vs], dst))
            dvs.append(_dot(kt[:, ks], dst, _NT))
            dds.append(jnp.sum(dst * stp_ref[vs, :], axis=0, keepdims=True) * has_prev)
            dstate[vs, :] = dst * d[:, ks]
        dq_ref[...] = (jnp.concatenate(dqs, axis=-1) * (GLA_HDK ** -0.5)).astype(dq_ref.dtype)
        dv_ref[...] = jnp.concatenate(dvs, axis=-1).astype(dv_ref.dtype)
        dkt = jnp.concatenate(dkts, axis=-1)
        dd = jnp.concatenate(dds, axis=-1)
        dk_ref[...] = (dkt * w).astype(dk_ref.dtype)
        de = dkt * kt
        db_end = jnp.sum(de, axis=0, keepdims=True) + dd * d
        dla = db_end - _tri_dot(_tri_matrix(False), de)
        df = dla * (1.0 - _sigmoid(f)) * (1.0 / GATE_TEMP)
        dflow_ref[...] = _dot(df, wfu_v, _NT).astype(dflow_ref.dtype)
        dwfu_ref[...] += _dot(flow, df, _TN)
        dbf_ref[...] += jnp.sum(df, axis=0, keepdims=True)

    rc = lambda s: nc - 1 - s
    return _pcall(
        body, name="gla_bwd", grid=(nc,),
        in_specs=[pl.BlockSpec((CHUNK, GLA_DK), lambda s: (rc(s), 0)), pl.BlockSpec((CHUNK, GLA_DK), lambda s: (rc(s), 1)),
                  pl.BlockSpec((CHUNK, GLA_DV), lambda s: (rc(s), 1)),
                  pl.BlockSpec((CHUNK, FLOW_W), lambda s: (rc(s), (INT_W - FLOW_W) // FLOW_W)),
                  pl.BlockSpec(wfu.shape, lambda s: (0, 0)), pl.BlockSpec(bf.shape, lambda s: (0, 0)),
                  pl.BlockSpec((None, GLA_DV, GLA_HDK), lambda s: (rc(s), 0, 0)),
                  pl.BlockSpec((None, GLA_DV, GLA_HDK), lambda s: (jnp.maximum(rc(s) - 1, 0), 0, 0)),
                  pl.BlockSpec((CHUNK, GLA_DV), lambda s: (rc(s), 0))],
        out_specs=[pl.BlockSpec((CHUNK, GLA_DK), lambda s: (rc(s), 0)), pl.BlockSpec((CHUNK, GLA_DK), lambda s: (rc(s), 0)),
                   pl.BlockSpec((CHUNK, GLA_DV), lambda s: (rc(s), 0)), pl.BlockSpec((CHUNK, FLOW_W), lambda s: (rc(s), 0)),
                   pl.BlockSpec(wfu.shape, lambda s: (0, 0)), pl.BlockSpec(bf.shape, lambda s: (0, 0))],
        out_shape=[_sds((S, GLA_DK), BF16), _sds((S, GLA_DK), BF16), _sds((S, GLA_DV), BF16), _sds((S, FLOW_W), BF16),
                   _sds(wfu.shape), _sds(bf.shape)],
        scratch=[pltpu.VMEM((GLA_DV, GLA_HDK), F32)])(proj, proj, proj, proj, wfu, bf, states, states, do)


def _pool_counts(tm, i):
    t = (lax.broadcasted_iota(jnp.int32, (tm, POOL_GD), 0) + i * tm + 1).astype(F32)
    return [jnp.minimum(t, float(w)) for w in POOL_WINDOWS]


def _pool_fwd(proj, w_pool, pool_scale, tm=256):
    S = proj.shape[0]
    tm = min(tm, S // 2)
    col = (3072 + 3072) // POOL_W
    hb = tm // POOL_HALO

    def body(p_ref, halo_ref, wp_ref, sc_ref, mixed_ref, out_ref):
        i = pl.program_id(0)
        p = p_ref[...].astype(F32)
        halo = halo_ref[...].astype(F32) * (i > 0).astype(F32)
        ext = jnp.concatenate([halo, p], axis=0)
        n = tm + POOL_HALO
        sums, acc, k = {}, ext, 1
        while k < POOL_WINDOWS[-1]:
            acc = acc + pltpu.roll(acc, k, axis=0)
            k *= 2
            sums[k] = acc
        cnts = _pool_counts(tm, i)
        mixed, lin = [], []
        for g, w in enumerate(POOL_WINDOWS):
            ls = slice(g * POOL_GD, (g + 1) * POOL_GD)
            m = sums[w][POOL_HALO:n, ls] / cnts[g] - p[:, ls]
            mixed.append(m)
            lin.append(_dot(m, wp_ref[g]))
        mixed_ref[...] = jnp.concatenate(mixed, axis=-1)
        out_ref[...] = (jnp.concatenate(lin, axis=-1) * sc_ref[...]).astype(out_ref.dtype)

    return _pcall(
        body, name="pool_fwd", grid=(S // tm,),
        in_specs=[pl.BlockSpec((tm, POOL_W), lambda i: (i, col)),
                  pl.BlockSpec((POOL_HALO, POOL_W), lambda i: (jnp.maximum(i * hb - 1, 0), col)),
                  pl.BlockSpec(w_pool.shape, lambda i: (0, 0, 0)), pl.BlockSpec(pool_scale.shape, lambda i: (0, 0))],
        out_specs=[pl.BlockSpec((tm, POOL_W), lambda i: (i, 0)), pl.BlockSpec((tm, POOL_W), lambda i: (i, 0))],
        out_shape=[_sds((S, POOL_W)), _sds((S, POOL_W), BF16)])(proj, proj, w_pool, pool_scale)


def _pool_lin_bwd(dout, mixed, w_pool, pool_scale):
    S = dout.shape[0]

    def fn(dout, mixed, wp, sc):
        dlin = dout * sc
        dm, dwp, lin = [], [], []
        for g in range(len(POOL_WINDOWS)):
            ls = slice(g * POOL_GD, (g + 1) * POOL_GD)
            lin.append(_dot(mixed[:, ls], wp[g]))
            dm.append(_dot(dlin[:, ls], wp[g], _NT))
            dwp.append(_dot(mixed[:, ls], dlin[:, ls], _TN))
        dsc = jnp.sum(dout * jnp.concatenate(lin, axis=-1), axis=0, keepdims=True)
        return jnp.concatenate(dm, axis=-1), jnp.concatenate(dwp, axis=0), dsc

    return _rowwise("pool_lin_bwd", fn, [dout, mixed], [w_pool, pool_scale], [_sds((S, POOL_W))],
                    [_sds((len(POOL_WINDOWS) * POOL_GD, POOL_GD)), _sds((1, POOL_W))])


def _pool_win_bwd(dmixed, tm=256):
    S = dmixed.shape[0]
    tm = min(tm, S // 2)
    nt = S // tm
    hb = tm // POOL_HALO

    def body(dm_ref, halo_ref, dp_ref):
        i = pl.program_id(0)
        dm = dm_ref[...]
        halo = halo_ref[...] * (i < nt - 1).astype(F32)
        cnts = _pool_counts(tm, i)
        cnts_h = [c[:POOL_HALO] for c in _pool_counts(tm, i + 1)]
        r = jnp.concatenate([jnp.concatenate([dm[:, g * POOL_GD:(g + 1) * POOL_GD] / cnts[g] for g in range(4)], axis=-1),
                             jnp.concatenate([halo[:, g * POOL_GD:(g + 1) * POOL_GD] / cnts_h[g] for g in range(4)], axis=-1)],
                            axis=0)
        n = tm + POOL_HALO
        sums, acc, k = {}, r, 1
        while k < POOL_WINDOWS[-1]:
            acc = acc + pltpu.roll(acc, n - k, axis=0)
            k *= 2
            sums[k] = acc
        dp = [sums[w][:tm, g * POOL_GD:(g + 1) * POOL_GD] for g, w in enumerate(POOL_WINDOWS)]
        dp_ref[...] = (jnp.concatenate(dp, axis=-1) - dm).astype(dp_ref.dtype)

    return _pcall(
        body, name="pool_win_bwd", grid=(nt,),
        in_specs=[pl.BlockSpec((tm, POOL_W), lambda i: (i, 0)),
                  pl.BlockSpec((POOL_HALO, POOL_W), lambda i: (jnp.minimum((i + 1) * hb, S // POOL_HALO - 1), 0))],
        out_specs=pl.BlockSpec((tm, POOL_W), lambda i: (i, 0)),
        out_shape=_sds((S, POOL_W), BF16))(dmixed, dmixed)


def _xattn_probs(q, kv, h):
    hs = slice(h * XA_HD, (h + 1) * XA_HD)
    s = _dot(q[:, hs], kv[:, hs], _NT) * (XA_HD ** -0.5)
    s = s - jnp.max(s, axis=-1, keepdims=True)
    e = jnp.exp(s)
    return e / jnp.sum(e, axis=-1, keepdims=True)


def _xattn_fwd(proj, kv):
    S = proj.shape[0]

    def fn(q, kv):
        outs = []
        for h in range(XA_HEADS):
            p = _xattn_probs(q, kv, h)
            outs.append(_dot(p, kv[:, XA_W + h * XA_HD:XA_W + (h + 1) * XA_HD]))
        return jnp.concatenate(outs, axis=-1)

    return _rowwise("xattn_fwd", fn, [_Win(proj, XA_W, (3072 + 3072 + POOL_W) // XA_W)], [kv], [_sds((S, XA_W), BF16)])[0]


def _xattn_bwd(proj, kv, dxa):
    S = proj.shape[0]

    def fn(q, dxa, kv):
        dqs, dks, dvs = [], [], []
        for h in range(XA_HEADS):
            hs = slice(h * XA_HD, (h + 1) * XA_HD)
            vh = kv[:, XA_W + h * XA_HD:XA_W + (h + 1) * XA_HD]
            p = _xattn_probs(q, kv, h)
            dp = _dot(dxa[:, hs], vh, _NT)
            ds = p * (dp - jnp.sum(p * dp, axis=-1, keepdims=True)) * (XA_HD ** -0.5)
            dqs.append(_dot(ds, kv[:, hs]))
            dks.append(_dot(ds, q[:, hs], _TN))
            dvs.append(_dot(p, dxa[:, hs], _TN))
        return jnp.concatenate(dqs, axis=-1), jnp.concatenate(dks + dvs, axis=-1)

    return _rowwise("xattn_bwd", fn, [_Win(proj, XA_W, (3072 + 3072 + POOL_W) // XA_W), dxa], [kv],
                    [_sds((S, XA_W), BF16)], [_sds(kv.shape)])


def _mix_fwd(x1, mem, W):
    S = x1.shape[0]
    M = mem.shape[0]
    h = _rowwise("mix_pre", lambda x, g: _rms(x, g), [x1], [W["mix_pre_g"]], [_sds((S, D_MODEL), BF16)])[0]
    tm = min(512, S)
    proj = _mm("mix_proj", h, W["w_int_t"], grid=(INT_NBLK, S // tm, 1), a_blk=(tm, D_MODEL), a_map=lambda j, i, k: (i, 0),
               b_blk=(INT_BLK, D_MODEL), b_map=lambda j, i, k: (j, 0), o_shape=(S, INT_W), o_blk=(tm, INT_BLK),
               o_map=lambda j, i, k: (i, j), dims=_NT, out_dtype=BF16)
    o_raw, states = _gla_fwd(proj, W["w_fu_pad"], W["b_f"])
    ya_in = _rowwise("gla_out", _head_rms_gate, [o_raw, _Win(proj, GLA_DV, 2)], [W["gla_norm_g"]],
                     [_sds((S, GLA_DV), BF16)])[0]
    mixed, pool_out = _pool_fwd(proj, W["w_pool"], W["pool_scale"])
    mem_n = _rowwise("mem_norm", lambda m, g: _rms(m, g), [mem], [W["mem_norm_g"]], [_sds((M, D_MODEL), BF16)])[0]
    kv = _mm_nn("mem_kv", mem_n, W["w_mem_kv"])
    xa = _xattn_fwd(proj, kv)
    ya = _mm_nn("up_gla", ya_in, W["w_up_gla"], BF16)
    yb = _mm_nn("up_pool", pool_out, W["w_up_pool"], BF16)
    yc = _mm_nn("up_xattn", xa, W["w_up_xattn"], BF16)
    merged = _rowwise("merge", _merge, [_Win(proj, 3 * D_MODEL, 1), ya, yb, yc], [], [_sds((S, D_MODEL), BF16)])[0]
    y = _mm_nn("mix_out", merged, W["w_o"])
    x2 = _rowwise("mix_post", lambda x, y, g: x + _rms(y, g), [x1, y], [W["mix_post_g"]], [_sds((S, D_MODEL))])[0]
    return x2, (h, proj, o_raw, states, ya_in, mixed, pool_out, mem_n, kv, xa, ya, yb, yc, merged, y)


def _head(x3, tgt, final_g):
    S = x3.shape[0]

    def head(x3, t, g):
        out, vjp = jax.vjp(_rms, x3, g)
        e = out - t
        loss = 0.5 * jnp.sum(jnp.mean(e * e, axis=-1, keepdims=True), axis=0, keepdims=True)
        dx3, dg = vjp(e * (1.0 / D_MODEL))
        return dx3, jnp.broadcast_to(loss, (1, 128)), dg

    dx3, loss_acc, d_final_g = _rowwise("head", head, [x3, tgt], [final_g], [_sds((S, D_MODEL))],
                                        [_sds((1, 128)), _sds((1, D_MODEL))])
    return loss_acc[0, 0], dx3, d_final_g


def _mix_bwd(dx2, x1, mem, W, saved, on_grads=None):
    h, proj, o_raw, states, ya_in, mixed, pool_out, mem_n, kv, xa, ya, yb, yc, merged, y = saved
    S = x1.shape[0]

    def mix_post_bwd(y, dx2, g):
        return _rms_bwd(y, g, dx2)

    dy, d_mix_post_g = _rowwise("mix_post_bwd", mix_post_bwd, [y, dx2], [W["mix_post_g"]], [_sds((S, D_MODEL), BF16)],
                                [_sds((1, D_MODEL))])
    d_w_o = _mm_tn("d_w_o", merged, dy)
    dmerged = _mm_nt("d_merged", dy, W["w_o"], BF16)

    def merge_bwd(gates, ya, yb, yc, dm):
        _, vjp = jax.vjp(_merge, gates, ya, yb, yc)
        return vjp(dm.astype(F32))

    dgates, dya, dyb, dyc = _rowwise("merge_bwd", merge_bwd, [_Win(proj, 3 * D_MODEL, 1), ya, yb, yc, dmerged], [],
                                     [_sds((S, 3 * D_MODEL), BF16)] + [_sds((S, D_MODEL), BF16)] * 3)
    d_w_up_gla = _mm_tn("d_w_up_gla", ya_in, dya)
    d_w_up_pool = _mm_tn("d_w_up_pool", pool_out, dyb)
    d_w_up_xattn = _mm_tn("d_w_up_xattn", xa, dyc)
    d_ya_in = _mm_nt("d_ya_in", dya, W["w_up_gla"], BF16)
    d_pool_out = _mm_nt("d_pool_out", dyb, W["w_up_pool"], BF16)
    d_xa = _mm_nt("d_xa", dyc, W["w_up_xattn"], BF16)

    def gla_out_bwd(o, g_out, d, gn):
        _, vjp = jax.vjp(_head_rms_gate, o, g_out, gn)
        return vjp(d.astype(F32))

    do_raw, dg_out, d_gla_norm_g = _rowwise("gla_out_bwd", gla_out_bwd, [o_raw, _Win(proj, GLA_DV, 2), d_ya_in],
                                            [W["gla_norm_g"]], [_sds((S, GLA_DV)), _sds((S, GLA_DV), BF16)],
                                            [_sds((1, GLA_DV))])
    dq, dk, dv, dflow, d_wfu_pad, d_b_f = _gla_bwd(proj, W["w_fu_pad"], W["b_f"], states, do_raw)
    dmixed, d_w_pool, d_pool_scale = _pool_lin_bwd(d_pool_out, mixed, W["w_pool"], W["pool_scale"])
    dp_in = _pool_win_bwd(dmixed)
    dxq, dkv = _xattn_bwd(proj, kv, d_xa)
    d_w_mem_kv = _mm_tn("d_w_mem_kv", mem_n, dkv)
    dmem_n = _mm_nt("d_mem_n", dkv, W["w_mem_kv"])
    d_mem_norm_g = _rowwise("mem_norm_bwd", lambda m, d, g: _rms_bwd(m, g, d)[1], [mem, dmem_n], [W["mem_norm_g"]], [],
                            [_sds((1, D_MODEL))])[0]
    dproj = jnp.concatenate([dq, dk, dv, dg_out, dgates, dp_in, dxq, dflow], axis=-1)
    d_w_int_t = _mm_tn("d_w_int", dproj, h, ts=1024, tn_a=INT_BLK)
    grads = dict(
        w_int_t=d_w_int_t, w_fu_pad=d_wfu_pad, b_f=d_b_f, gla_norm_g=d_gla_norm_g, w_pool=d_w_pool,
        pool_scale=d_pool_scale, mem_norm_g=d_mem_norm_g, w_mem_kv=d_w_mem_kv, w_up_gla=d_w_up_gla,
        w_up_pool=d_w_up_pool, w_up_xattn=d_w_up_xattn, w_o=d_w_o, mix_post_g=d_mix_post_g)
    deps = [on_grads(grads)] if on_grads is not None else []
    dx1, grads["mix_pre_g"] = _dh_pre_norm_bwd("d_mix_h", dproj, W["w_int_t"], x1, dx2, W["mix_pre_g"], deps)
    return dx1, grads


def _mesh_pos():
    x, y, c = lax.axis_index("x"), lax.axis_index("y"), lax.axis_index("c")
    return x, y, c, 4 * x + 2 * y + c


def _peer(x, y, c, r):
    px = 1 - x if r & 4 else x
    py = 1 - y if r & 2 else y
    pc = 1 - c if r & 1 else c
    return (px, py, pc), 4 * px + 2 * py + pc


_ALL_PEERS = tuple(range(1, N_DEV))
_SIBLING = 1
_SAME_CORE = (2, 4, 6)


def _dev_slot(ref, dev):
    return ref.at[dev]


class _Plan:
    def __init__(self, scatter, slots=None, shapes=None):
        self.scatter, self.slots, self.shapes = scatter, slots or {}, shapes or {}

    def src(self, srcs, a, dev):
        return self.slots.get(a, _dev_slot)(srcs[a], dev) if self.scatter else srcs[a]

    def dst(self, lands, a, dev):
        return lands[a].at[dev] if self.scatter else self.slots.get(a, _dev_slot)(lands[a], dev)

    def landing_zones(self, arrays):
        lands = []
        for a, arr in enumerate(arrays):
            if self.scatter:
                lands.append(lax.empty((N_DEV,) + tuple(self.shapes.get(a, arr.shape[1:])), arr.dtype))
            elif a in self.shapes:
                lands.append(jnp.zeros(self.shapes[a], arr.dtype))
            else:
                lands.append(lax.empty((N_DEV,) + arr.shape, arr.dtype))
        return lands


_GATHER, _SCATTER = _Plan(False), _Plan(True)


def _peer_copies(srcs, lands, send_sems, recv_sems, plan, peers=_ALL_PEERS):
    x, y, c, me = _mesh_pos()
    cps = []
    for r in peers:
        pos, peer = _peer(x, y, c, r)
        for a in range(len(srcs)):
            k = a * (N_DEV - 1) + r - 1
            cps.append(pltpu.make_async_remote_copy(
                src_ref=plan.src(srcs, a, peer), dst_ref=plan.dst(lands, a, me),
                send_sem=send_sems.at[k], recv_sem=recv_sems.at[k], device_id=pos,
                device_id_type=pl.DeviceIdType.MESH))
    return cps


def _exchange(name, arrays, plan, after):
    n = len(arrays)

    def body(*refs):
        ins, outs = refs[:n], refs[n + len(after):2 * n + len(after)]
        send_sems, recv_sems, local_sems = refs[2 * n + len(after):]
        copies = _own_copies(ins, outs, local_sems, plan) + _peer_copies(ins, outs, send_sems, recv_sems, plan)
        for cp in copies:
            cp.start()
        for cp in copies:
            cp.wait()

    any_spec = pl.BlockSpec(memory_space=pl.ANY)
    out_shape = [_sds(l.shape, l.dtype) for l in plan.landing_zones(arrays)]
    return pl.pallas_call(
        body, name=name, in_specs=[any_spec] * (n + len(after)), out_specs=[any_spec] * n, out_shape=out_shape,
        scratch_shapes=[pltpu.SemaphoreType.DMA((n * (N_DEV - 1),)), pltpu.SemaphoreType.DMA((n * (N_DEV - 1),)),
                        pltpu.SemaphoreType.DMA((n,))])(*arrays, *after)


_HBM = pl.BlockSpec(memory_space=pltpu.HBM)
_SEM = pl.BlockSpec(memory_space=pltpu.SEMAPHORE)
_EFFECT = pltpu.SideEffectType.DATAFLOW_SIDE_EFFECTING


def _own_copies(srcs, lands, own_sems, plan):
    me = _mesh_pos()[3]
    return [pltpu.make_async_copy(plan.src(srcs, a, me), plan.dst(lands, a, me), own_sems.at[a])
            for a in range(len(srcs))]


def _exchange_start(name, arrays, plan):
    n = len(arrays)
    lands = plan.landing_zones(arrays)
    n_sem = n * (N_DEV - 1)

    def body(*refs):
        srcs, lands_ = refs[:n], refs[n:2 * n]
        send_sems, recv_sems, own_sems = refs[2 * n:2 * n + 3]
        token = refs[-1]
        for cp in _peer_copies(srcs, lands_, send_sems, recv_sems, plan) + _own_copies(srcs, lands_, own_sems, plan):
            cp.start()
        token[...] = jnp.zeros_like(token)

    hbm = lambda a: pltpu.HBM(a.shape, a.dtype)
    res = pl.pallas_call(
        body, name=name,
        out_shape=(pltpu.SemaphoreType.DMA((n_sem,)), pltpu.SemaphoreType.DMA((n_sem,)), pltpu.SemaphoreType.DMA((n,)),
                   *[hbm(a) for a in arrays], *[hbm(a) for a in lands], _sds((8, 128))),
        in_specs=[_HBM] * (2 * n),
        out_specs=(_SEM, _SEM, _SEM, *[_HBM] * (2 * n), pl.BlockSpec(memory_space=pltpu.VMEM)),
        input_output_aliases={i: 3 + i for i in range(2 * n)},
        compiler_params=pltpu.CompilerParams(has_side_effects=_EFFECT),
    )(*[pltpu.with_memory_space_constraint(a, pltpu.HBM) for a in list(arrays) + lands])
    return (res[:3], res[3:3 + n], res[3 + n:3 + 2 * n], plan), res[-1]


def _exchange_wait(name, state, after):
    sems, srcs, lands, plan = state
    n = len(srcs)

    def body(*refs):
        srcs_, lands_ = refs[:n], refs[n:2 * n]
        send_sems, recv_sems, own_sems = refs[2 * n:2 * n + 3]
        for cp in _peer_copies(srcs_, lands_, send_sems, recv_sems, plan):
            cp.wait_send()
            cp.wait_recv()
        for cp in _own_copies(srcs_, lands_, own_sems, plan):
            cp.wait()

    hbm = lambda a: pltpu.HBM(a.shape, a.dtype)
    res = pl.pallas_call(
        body, name=name, out_shape=tuple(hbm(a) for a in list(srcs) + list(lands)),
        in_specs=[_HBM] * (2 * n) + [_SEM] * 3 + [pl.BlockSpec(memory_space=pl.ANY)] * len(after),
        out_specs=tuple([_HBM] * (2 * n)), input_output_aliases={i: i for i in range(2 * n)},
        compiler_params=pltpu.CompilerParams(has_side_effects=_EFFECT),
    )(*srcs, *lands, *sems, *after)
    return res[n:]


def _gather_start(name, arrays, plan, after):
    n = len(arrays)
    lands = plan.landing_zones(arrays)
    n_sem = n * (N_DEV - 1)

    def body(*refs):
        srcs, lands_ = refs[:n], refs[n:2 * n]
        send_sems, recv_sems, own_sems = refs[2 * n + len(after):2 * n + len(after) + 3]
        token = refs[-1]
        for cp in (_peer_copies(srcs, lands_, send_sems, recv_sems, plan, (_SIBLING,) + _SAME_CORE)
                   + _own_copies(srcs, lands_, own_sems, plan)):
            cp.start()
        token[...] = jnp.zeros_like(token)

    hbm = lambda a: pltpu.HBM(a.shape, a.dtype)
    res = pl.pallas_call(
        body, name=name,
        out_shape=(pltpu.SemaphoreType.DMA((n_sem,)), pltpu.SemaphoreType.DMA((n_sem,)), pltpu.SemaphoreType.DMA((n,)),
                   *[hbm(a) for a in arrays], *[hbm(a) for a in lands], _sds((8, 128))),
        in_specs=[_HBM] * (2 * n) + [pl.BlockSpec(memory_space=pl.ANY)] * len(after),
        out_specs=(_SEM, _SEM, _SEM, *[_HBM] * (2 * n), pl.BlockSpec(memory_space=pltpu.VMEM)),
        input_output_aliases={i: 3 + i for i in range(2 * n)},
        compiler_params=pltpu.CompilerParams(has_side_effects=_EFFECT),
    )(*[pltpu.with_memory_space_constraint(a, pltpu.HBM) for a in list(arrays) + lands], *after)
    return (res[:3], res[3:3 + n], res[3 + n:3 + 2 * n], plan), res[-1]


def _pass_on_copies(lands, send_sems, recv_sems, plan):
    x, y, c, _ = _mesh_pos()
    sibling = _peer(x, y, c, _SIBLING)[0]
    cps = []
    for i, r in enumerate(_SAME_CORE):
        owner = _peer(x, y, c, r)[1]
        for a in range(len(lands)):
            k = a * len(_SAME_CORE) + i
            cps.append(pltpu.make_async_remote_copy(
                src_ref=plan.dst(lands, a, owner), dst_ref=plan.dst(lands, a, owner), send_sem=send_sems.at[k],
                recv_sem=recv_sems.at[k], device_id=sibling, device_id_type=pl.DeviceIdType.MESH))
    return cps


def _gather_pass_on(name, state, after):
    sems, srcs, lands, plan = state
    n = len(srcs)
    n_sem = n * len(_SAME_CORE)

    def body(*refs):
        srcs_, lands_ = refs[:n], refs[n:2 * n]
        send_sems, recv_sems = refs[2 * n], refs[2 * n + 1]
        on_send, on_recv = refs[2 * n + 3 + len(after)], refs[2 * n + 4 + len(after)]
        arrivals = _peer_copies(srcs_, lands_, send_sems, recv_sems, plan, _SAME_CORE)
        for arrived, on in zip(arrivals, _pass_on_copies(lands_, on_send, on_recv, plan)):
            arrived.wait_recv()
            on.start()

    hbm = lambda a: pltpu.HBM(a.shape, a.dtype)
    res = pl.pallas_call(
        body, name=name,
        out_shape=(pltpu.SemaphoreType.DMA((n_sem,)), pltpu.SemaphoreType.DMA((n_sem,)),
                   *[hbm(a) for a in list(srcs) + list(lands)]),
        in_specs=[_HBM] * (2 * n) + [_SEM] * 3 + [pl.BlockSpec(memory_space=pl.ANY)] * len(after),
        out_specs=(_SEM, _SEM, *[_HBM] * (2 * n)), input_output_aliases={i: 2 + i for i in range(2 * n)},
        compiler_params=pltpu.CompilerParams(has_side_effects=_EFFECT),
    )(*srcs, *lands, *sems, *after)
    return sems, res[:2], res[2:2 + n], res[2 + n:], plan


def _gather_wait(name, state, after):
    sems, on_sems, srcs, lands, plan = state
    n = len(srcs)

    def body(*refs):
        srcs_, lands_ = refs[:n], refs[n:2 * n]
        send_sems, recv_sems, own_sems, on_send, on_recv = refs[2 * n:2 * n + 5]
        for cp in _peer_copies(srcs_, lands_, send_sems, recv_sems, plan, (_SIBLING,)):
            cp.wait_recv()
        for cp in _peer_copies(srcs_, lands_, send_sems, recv_sems, plan, (_SIBLING,) + _SAME_CORE):
            cp.wait_send()
        for cp in _own_copies(srcs_, lands_, own_sems, plan):
            cp.wait()
        for cp in _pass_on_copies(lands_, on_send, on_recv, plan):
            cp.wait_send()
            cp.wait_recv()

    hbm = lambda a: pltpu.HBM(a.shape, a.dtype)
    res = pl.pallas_call(
        body, name=name, out_shape=tuple(hbm(a) for a in list(srcs) + list(lands)),
        in_specs=[_HBM] * (2 * n) + [_SEM] * 5 + [pl.BlockSpec(memory_space=pl.ANY)] * len(after),
        out_specs=tuple([_HBM] * (2 * n)), input_output_aliases={i: i for i in range(2 * n)},
        compiler_params=pltpu.CompilerParams(has_side_effects=_EFFECT),
    )(*srcs, *lands, *sems, *on_sems, *after)
    return res[n:]


def _adam(name, recv, w, m, v):
    shape = w.shape
    R, C = shape
    tr = R
    while N_DEV * tr * C * 4 > 6 * 1024 * 1024 and tr % 32 == 0:
        tr //= 2

    def body(recv_ref, w_ref, m_ref, v_ref, g_ref, d_ref, m2_ref, v2_ref):
        g = recv_ref[0].astype(F32)
        for j in range(1, N_DEV):
            g = g + recv_ref[j].astype(F32)
        w_, m_, v_ = w_ref[...], m_ref[...], v_ref[...]
        m2 = ADAM_B1 * m_ + (1.0 - ADAM_B1) * g
        v2 = ADAM_B2 * v_ + (1.0 - ADAM_B2) * (g * g)
        m_hat = m2 / (1.0 - ADAM_B1 ** ADAM_STEP)
        v_hat = v2 / (1.0 - ADAM_B2 ** ADAM_STEP)
        g_ref[...] = g
        d_ref[...] = -ADAM_LR * (m_hat / (jnp.sqrt(v_hat) + ADAM_EPS) + ADAM_WD * w_)
        m2_ref[...] = m2
        v2_ref[...] = v2

    blk = pl.BlockSpec((tr, C), lambda i: (i, 0))
    return _pcall(body, name=name, grid=(R // tr,),
                  in_specs=[pl.BlockSpec((N_DEV, tr, C), lambda i: (0, i, 0)), blk, blk, blk],
                  out_specs=[blk] * 4, out_shape=[_sds(shape)] * 4)(recv, w, m, v)


_NAMES = ['ffn1_pre_g', 'ffn1_w_in', 'ffn1_w_out', 'ffn1_post_g', 'mix_pre_g', 'w_in', 'w_fu', 'b_f', 'gla_norm_g',
          'w_pool', 'pool_scale', 'mem_norm_g', 'w_mem_kv', 'w_up_gla', 'w_up_pool', 'w_up_xattn', 'w_o', 'mix_post_g',
          'ffn2_pre_g', 'ffn2_w_in', 'ffn2_w_out', 'ffn2_post_g', 'final_g']
_SHARDED = ['ffn1_w_in', 'ffn1_w_out', 'w_in', 'w_fu', 'w_mem_kv', 'w_up_gla', 'w_up_pool', 'w_up_xattn', 'w_o',
            'ffn2_w_in', 'ffn2_w_out']
_COL_SHARDED = ['w_up_pool', 'w_up_xattn']


def _cols_to_full(g):
    return jnp.transpose(g, (1, 0, 2)).reshape(g.shape[1], N_DEV * g.shape[2])


def _full_to_cols(f):
    R, C = f.shape
    return jnp.transpose(f.reshape(R, N_DEV, C // N_DEV), (1, 0, 2))


def _to_internal(w_in_t):
    o = 0
    parts = []
    for s in IN_SPLITS:
        parts.append(w_in_t[o:o + s])
        o += s
    q, k, v, g_out, f_low, p_in, xq, gates = parts
    f_low = jnp.pad(f_low, ((0, FLOW_W - GATE_RANK), (0, 0)))
    return jnp.concatenate([q, k, v, g_out, gates, p_in, xq, f_low], axis=0)


def _from_internal(d):
    q, k, v, g_out = d[0:512], d[512:1024], d[1024:2048], d[2048:3072]
    gates, p_in, xq = d[3072:6144], d[6144:6656], d[6656:7168]
    f_low = d[7168:7168 + GATE_RANK]
    return jnp.concatenate([q, k, v, g_out, f_low, p_in, xq, gates], axis=0)


_FFN_IN = ('ffn1_w_in', 'ffn2_w_in')
_FFN_OUT = ('ffn1_w_out', 'ffn2_w_out')
_GATHERS = {"ffn1_in": ['ffn1_w_in'], "ffn1_out": ['ffn1_w_out'],
            "mix": ['w_in', 'w_fu', 'w_mem_kv', 'w_up_gla', 'w_up_pool', 'w_up_xattn', 'w_o'],
            "ffn2": ['ffn2_w_in', 'ffn2_w_out']}


def _ffn_in_slot(ref, d):
    return ref.at[2 * (d % N_FF_BLK) + d // N_FF_BLK, pl.ds(0, FF_BLK)]


def _ffn_out_slot(ref, d):
    rows = FF_BLK // 2
    return ref.at[d // 2, pl.ds(pl.multiple_of((d % 2) * rows, rows), rows)]


def _ffn_plan(names, scatter):
    slots, shapes = {}, {}
    for a, n in enumerate(names):
        if n in _FFN_IN:
            slots[a] = _ffn_in_slot
            shapes[a] = (FF_BLK, D_MODEL) if scatter else (N_DEV, FF_PAD, D_MODEL)
        elif n in _FFN_OUT:
            slots[a] = _ffn_out_slot
            shapes[a] = (FF_BLK // 2, D_MODEL) if scatter else (N_FF_BLK, FF_PAD, D_MODEL)
    return _Plan(scatter, slots, shapes)


def _rows_to_full(g):
    return g.reshape(N_DEV * g.shape[1], g.shape[2])


def _mix_weights(gathered):
    W = {n: _rows_to_full(gathered[n]) for n in ('w_mem_kv', 'w_up_gla', 'w_o')}
    for n in _COL_SHARDED:
        W[n] = _cols_to_full(gathered[n])
    W["w_int_t"] = _to_internal(gathered["w_in"].reshape(IN_WIDTH, D_MODEL))
    W["w_fu_pad"] = jnp.pad(_cols_to_full(gathered["w_fu"]), ((0, FLOW_W - GATE_RANK), (0, 0)))
    return W


def _mix_chunks(G, n):
    if n == "w_in":
        return _from_internal(G["w_int_t"]).reshape(N_DEV, IN_SHARD, D_MODEL)
    if n == "w_fu":
        return _full_to_cols(G["w_fu_pad"][:GATE_RANK].astype(BF16))
    if n in _COL_SHARDED:
        return _full_to_cols(G[n])
    return G[n].reshape(N_DEV, G[n].shape[0] // N_DEV, G[n].shape[1])


def _step(x, mem, tgt, P, Mo, Vo):
    def native(n, a):
        return jnp.swapaxes(a, 0, 1) if n in _FFN_IN + ("w_in",) else a

    P, Mo, Vo = ({n: native(n, a) for n, a in d.items()} for d in (P, Mo, Vo))
    small = {n: P[n] for n in _NAMES if n not in _SHARDED}

    gather, token = {}, []
    for grp, names in _GATHERS.items():
        gather[grp], tok = _gather_start("gather_" + grp, [P[n].astype(BF16) for n in names], _ffn_plan(names, False),
                                         token)
        token = [tok]

    def gathered(grp, after):
        return _gather_wait("gather_" + grp + "_wait", _gather_pass_on("gather_" + grp + "_on", gather[grp], after), [])

    def ffn_wo(lands):
        return lambda act: lands[0].reshape(N_FF_BLK * FF_PAD, D_MODEL)

    w1t = gathered("ffn1_in", token)[0].reshape(N_DEV * FF_PAD, D_MODEL)
    x1, w1o, sv1 = _ffn_fwd("ffn1", x, small["ffn1_pre_g"], w1t, small["ffn1_post_g"],
                            lambda act: ffn_wo(gathered("ffn1_out", [act]))(act))

    gm = dict(zip(_GATHERS["mix"], gathered("mix", [x1])))
    Wm = {**small, **_mix_weights(gm)}
    x2, svm = _mix_fwd(x1, mem, Wm)

    w2t, w2o = gathered("ffn2", [x2])
    w2t = w2t.reshape(N_DEV * FF_PAD, D_MODEL)
    x3, w2o, sv2 = _ffn_fwd("ffn2", x2, small["ffn2_pre_g"], w2t, small["ffn2_post_g"], ffn_wo([w2o]))

    loss, dx3, d_final_g = _head(x3, tgt, small["final_g"])

    G = dict(final_g=d_final_g)
    scat = {}

    def start(grp, names, arrays):
        scat[grp] = names, _exchange_start("scatter_" + grp, arrays, _ffn_plan(names, True))
        return scat[grp][1][1]

    def ffn_starts(tag):
        return (lambda dwo: start(tag + "_out", [tag + "_w_out"], [dwo.reshape(N_FF_BLK, FF_PAD, D_MODEL)]),
                lambda dwt: start(tag + "_in", [tag + "_w_in"], [dwt.reshape(N_DEV, FF_PAD, D_MODEL)]))

    dx2, G["ffn2_pre_g"], G["ffn2_post_g"] = _ffn_bwd(
        "ffn2", dx3, x2, small["ffn2_pre_g"], w2t, w2o, small["ffn2_post_g"], sv2, *ffn_starts("ffn2"))
    dx1, Gm = _mix_bwd(dx2, x1, mem, Wm, svm,
                       lambda Gm: start("mix", _GATHERS["mix"], [_mix_chunks(Gm, n) for n in _GATHERS["mix"]]))
    G.update(Gm)
    dx, G["ffn1_pre_g"], G["ffn1_post_g"] = _ffn_bwd(
        "ffn1", dx1, x, small["ffn1_pre_g"], w1t, w1o, small["ffn1_post_g"], sv1, *ffn_starts("ffn1"))

    recv, outs = {}, {}
    last = dx
    for grp in ["ffn2_out", "ffn2_in", "mix", "small", "ffn1_out", "ffn1_in"]:
        if grp == "small":
            names = list(small)
            recv.update(zip(names, _exchange("gather_small_grads", [G[n] for n in small], _GATHER, [last])))
        else:
            names, (state, _) = scat[grp]
            recv.update(zip(names, _exchange_wait("scatter_" + grp + "_wait", state, [last])))
        for n in names:
            shp = P[n].shape
            shp2 = shp if len(shp) == 2 else (shp[0] * shp[1], shp[2])
            res = _adam("adam_" + n, recv[n].reshape((N_DEV,) + shp2), P[n].reshape(shp2), Mo[n].reshape(shp2),
                        Vo[n].reshape(shp2))
            outs[n] = [native(n, t.reshape(shp)) for t in res]
            last = res[-1]
    return loss, dx, outs


def kernel(x, mem, ffn1_pre_g, ffn1_w_in, ffn1_w_out, ffn1_post_g, mix_pre_g, w_in, w_fu, b_f, gla_norm_g, w_pool, pool_scale, mem_norm_g, w_mem_kv, w_up_gla, w_up_pool, w_up_xattn, w_o, mix_post_g, ffn2_pre_g, ffn2_w_in, ffn2_w_out, ffn2_post_g, final_g, loss_target, m_ffn1_pre_g, m_ffn1_w_in, m_ffn1_w_out, m_ffn1_post_g, m_mix_pre_g, m_w_in, m_w_fu, m_b_f, m_gla_norm_g, m_w_pool, m_pool_scale, m_mem_norm_g, m_w_mem_kv, m_w_up_gla, m_w_up_pool, m_w_up_xattn, m_w_o, m_mix_post_g, m_ffn2_pre_g, m_ffn2_w_in, m_ffn2_w_out, m_ffn2_post_g, m_final_g, v_ffn1_pre_g, v_ffn1_w_in, v_ffn1_w_out, v_ffn1_post_g, v_mix_pre_g, v_w_in, v_w_fu, v_b_f, v_gla_norm_g, v_w_pool, v_pool_scale, v_mem_norm_g, v_w_mem_kv, v_w_up_gla, v_w_up_pool, v_w_up_xattn, v_w_o, v_mix_post_g, v_ffn2_pre_g, v_ffn2_w_in, v_ffn2_w_out, v_ffn2_post_g, v_final_g):
    params = [ffn1_pre_g, ffn1_w_in, ffn1_w_out, ffn1_post_g, mix_pre_g, w_in, w_fu, b_f, gla_norm_g, w_pool, pool_scale, mem_norm_g, w_mem_kv, w_up_gla, w_up_pool, w_up_xattn, w_o, mix_post_g, ffn2_pre_g, ffn2_w_in, ffn2_w_out, ffn2_post_g, final_g]
    moms = [m_ffn1_pre_g, m_ffn1_w_in, m_ffn1_w_out, m_ffn1_post_g, m_mix_pre_g, m_w_in, m_w_fu, m_b_f, m_gla_norm_g, m_w_pool, m_pool_scale, m_mem_norm_g, m_w_mem_kv, m_w_up_gla, m_w_up_pool, m_w_up_xattn, m_w_o, m_mix_post_g, m_ffn2_pre_g, m_ffn2_w_in, m_ffn2_w_out, m_ffn2_post_g, m_final_g]
    vars_ = [v_ffn1_pre_g, v_ffn1_w_in, v_ffn1_w_out, v_ffn1_post_g, v_mix_pre_g, v_w_in, v_w_fu, v_b_f, v_gla_norm_g, v_w_pool, v_pool_scale, v_mem_norm_g, v_w_mem_kv, v_w_up_gla, v_w_up_pool, v_w_up_xattn, v_w_o, v_mix_post_g, v_ffn2_pre_g, v_ffn2_w_in, v_ffn2_w_out, v_ffn2_post_g, v_final_g]
    P = {n: a[0] if a.ndim > 2 else a for n, a in zip(_NAMES, params)}
    Mo = {n: a[0] if a.ndim > 2 else a for n, a in zip(_NAMES, moms)}
    Vo = {n: a[0] if a.ndim > 2 else a for n, a in zip(_NAMES, vars_)}
    loss, dx, outs = _step(x[0], mem[0], loss_target[0], P, Mo, Vo)
    loss = lax.psum(loss, ("x", "y", "c"))
    out = [loss, dx[None]]
    for kind in range(4):
        for n, p in zip(_NAMES, params):
            out.append(outs[n][kind].reshape(p.shape))
    return tuple(out)
```

```python
import functools

import jax
import jax.numpy as jnp
from jax import lax
from jax.experimental import pallas as pl
from jax.experimental.pallas import tpu as pltpu

F32 = jnp.float32
BF16 = jnp.bfloat16

N_DEV = 8
D_MODEL = 1024
D_FF = 2816
FF_BLK = 2 * D_FF // N_DEV
N_FF_BLK = D_FF // FF_BLK
FF_PAD = 768
CHUNK = 64
GLA_HEADS = 4
GLA_DK = 512
GLA_DV = 1024
GLA_HDK = 128
GLA_HDV = 256
GATE_RANK = 16
GATE_TEMP = 16.0
POOL_WINDOWS = (2, 4, 8, 16)
POOL_W = 512
POOL_GD = 128
POOL_HALO = 16
XA_HEADS = 4
XA_HD = 128
XA_W = 512
EPS = 1e-6
IN_SPLITS = (GLA_DK, GLA_DK, GLA_DV, GLA_DV, GATE_RANK, POOL_W, XA_W, 3 * D_MODEL)
IN_WIDTH = sum(IN_SPLITS)
IN_SHARD = IN_WIDTH // N_DEV
INT_W = 3072 + 3072 + 1024 + 128
INT_NBLK = 3
INT_BLK = INT_W // INT_NBLK
FLOW_W = 128

ADAM_LR = 0.001
ADAM_B1 = 0.9
ADAM_B2 = 0.999
ADAM_EPS = 1e-08
ADAM_WD = 0.01
ADAM_STEP = 10

VMEM_LIMIT = 56 * 1024 * 1024

_NN = (((1,), (0,)), ((), ()))
_NT = (((1,), (1,)), ((), ()))
_TN = (((0,), (0,)), ((), ()))


def _pcall(body, *, name, grid, in_specs, out_specs, out_shape, scratch=()):
    return pl.pallas_call(
        body, name=name, grid=grid, in_specs=in_specs, out_specs=out_specs, out_shape=out_shape,
        scratch_shapes=list(scratch),
        compiler_params=pltpu.CompilerParams(dimension_semantics=("arbitrary",) * len(grid),
                                             vmem_limit_bytes=VMEM_LIMIT))


def _dot(a, b, dims=_NN):
    return lax.dot_general(a.astype(BF16), b.astype(BF16), dims, preferred_element_type=F32)


def _mm(name, a, b, *, grid, a_blk, a_map, b_blk, b_map, o_shape, o_blk, o_map, dims, out_dtype=F32, deps=()):
    nk = grid[2]

    def body(a_ref, b_ref, *rest):
        o_ref, scr = rest[len(deps)], rest[len(deps) + 1:]
        p = _dot(a_ref[...], b_ref[...], dims)
        if nk == 1:
            o_ref[...] = p.astype(o_ref.dtype)
        else:
            acc = scr[0]
            k = pl.program_id(2)

            @pl.when(k == 0)
            def _():
                acc[...] = p

            @pl.when(k > 0)
            def _():
                acc[...] += p

            @pl.when(k == nk - 1)
            def _():
                o_ref[...] = acc[...].astype(o_ref.dtype)

    acc_shape = tuple(d for d in o_blk if d is not None)
    return _pcall(body, name=name, grid=grid,
                  in_specs=[pl.BlockSpec(a_blk, a_map), pl.BlockSpec(b_blk, b_map)]
                  + [pl.BlockSpec(memory_space=pl.ANY)] * len(deps),
                  out_specs=pl.BlockSpec(o_blk, o_map),
                  out_shape=jax.ShapeDtypeStruct(o_shape, out_dtype),
                  scratch=[pltpu.VMEM(acc_shape, F32)] if nk > 1 else [])(a, b, *deps)


def _mm_nn(name, a, b, out_dtype=F32, tm=512, tn=None):
    M, K = a.shape
    N = b.shape[1]
    tm, tn = min(tm, M), (tn or N)
    return _mm(name, a, b, grid=(N // tn, M // tm, 1), a_blk=(tm, K), a_map=lambda j, i, k: (i, 0),
               b_blk=(K, tn), b_map=lambda j, i, k: (0, j), o_shape=(M, N), o_blk=(tm, tn),
               o_map=lambda j, i, k: (i, j), dims=_NN, out_dtype=out_dtype)


def _mm_nt(name, a, b, out_dtype=F32, tm=512):
    M, K = a.shape
    N = b.shape[0]
    tm = min(tm, M)
    return _mm(name, a, b, grid=(1, M // tm, 1), a_blk=(tm, K), a_map=lambda j, i, k: (i, 0),
               b_blk=(N, K), b_map=lambda j, i, k: (0, 0), o_shape=(M, N), o_blk=(tm, N),
               o_map=lambda j, i, k: (i, 0), dims=_NT, out_dtype=out_dtype)


def _mm_tn(name, a, b, out_dtype=BF16, ts=512, tn=None, tn_a=None, deps=()):
    S, M = a.shape
    N = b.shape[1]
    ts, tn, tn_a = min(ts, S), (tn or N), (tn_a or M)
    return _mm(name, a, b, grid=((N // tn) * (M // tn_a), 1, S // ts), a_blk=(ts, tn_a),
               a_map=lambda j, i, k: (k, j if tn_a < M else 0), b_blk=(ts, tn),
               b_map=lambda j, i, k: (k, j if tn < N else 0), o_shape=(M, N), o_blk=(tn_a, tn),
               o_map=lambda j, i, k: (j, 0) if tn_a < M else (0, j), dims=_TN, out_dtype=out_dtype, deps=deps)


class _Win:
    def __init__(self, arr, w, c):
        self.arr, self.w, self.c = arr, w, c


def _row_spec(x, tm):
    if isinstance(x, _Win):
        return x.arr, pl.BlockSpec((tm, x.w), functools.partial(lambda i, c: (i, c), c=x.c))
    if x.ndim == 3:
        return x, pl.BlockSpec((x.shape[0], tm, x.shape[2]), lambda i: (0, i, 0))
    return x, pl.BlockSpec((tm, x.shape[1]), lambda i: (i, 0))


def _rowwise(name, fn, rows, consts, outs, accs=(), tm=256):
    first = rows[0].arr if isinstance(rows[0], _Win) else rows[0]
    S = first.shape[1] if first.ndim == 3 else first.shape[0]
    tm = min(tm, S)
    n_in, n_out = len(rows) + len(consts), len(outs)
    arrays, in_specs = [], []
    for r in rows:
        arr, spec = _row_spec(r, tm)
        arrays.append(arr)
        in_specs.append(spec)
    for c in consts:
        arrays.append(c)
        in_specs.append(pl.BlockSpec(c.shape, functools.partial(lambda i, n: (0,) * n, n=c.ndim)))
    out_specs = [_row_spec(o, tm)[1] for o in outs]
    out_specs += [pl.BlockSpec(a.shape, functools.partial(lambda i, n: (0,) * n, n=len(a.shape))) for a in accs]

    def body(*refs):
        res = fn(*[r[...] for r in refs[:n_in]])
        if not isinstance(res, (tuple, list)):
            res = (res,)
        orefs = refs[n_in:]
        for r, v in zip(orefs[:n_out], res[:n_out]):
            r[...] = v.astype(r.dtype)
        i = pl.program_id(0)
        for r, v in zip(orefs[n_out:], res[n_out:]):
            @pl.when(i == 0)
            def _(r=r, v=v):
                r[...] = v.astype(r.dtype)

            @pl.when(i > 0)
            def _(r=r, v=v):
                r[...] += v.astype(r.dtype)

    res = _pcall(body, name=name, grid=(S // tm,), in_specs=in_specs, out_specs=out_specs,
                 out_shape=[jax.ShapeDtypeStruct(o.shape, o.dtype) for o in list(outs) + list(accs)])(*arrays)
    return res


def _sds(shape, dtype=F32):
    return jax.ShapeDtypeStruct(shape, dtype)


def _rms(x, g):
    return x * lax.rsqrt(jnp.mean(x * x, axis=-1, keepdims=True) + EPS) * g


def _rms_bwd(x, g, dy):
    _, vjp = jax.vjp(_rms, x, g)
    return vjp(dy)


def _sigmoid(x):
    return 0.5 * jnp.tanh(0.5 * x) + 0.5


def _silu(x):
    return x * _sigmoid(x)


def _log_sigmoid(f):
    return jnp.minimum(f, 0.0) - jnp.log(1.0 + jnp.exp(-jnp.abs(f)))


def _head_rms_gate(o, g_out, gn):
    parts = [_rms(o[:, h * GLA_HDV:(h + 1) * GLA_HDV], gn[:, h * GLA_HDV:(h + 1) * GLA_HDV]) for h in range(GLA_HEADS)]
    return jnp.concatenate(parts, axis=-1) * _silu(g_out.astype(F32))


def _merge(gates, ya, yb, yc):
    gates, ya, yb, yc = (t.astype(F32) for t in (gates, ya, yb, yc))
    return (_sigmoid(gates[:, :D_MODEL]) * ya + _sigmoid(gates[:, D_MODEL:2 * D_MODEL]) * yb
            + _sigmoid(gates[:, 2 * D_MODEL:]) * yc)


def _tri_dot(t, x):
    hi = x.astype(BF16)
    r1 = x - hi.astype(F32)
    mid = r1.astype(BF16)
    lo = (r1 - mid.astype(F32)).astype(BF16)
    d = functools.partial(lax.dot_general, dimension_numbers=_NN, preferred_element_type=F32)
    return d(t, hi) + d(t, mid) + d(t, lo)


def _ffn_fwd(tag, x, gpre, wt, gpost, get_wo):
    S = x.shape[0]
    h = _rowwise(tag + "_pre", lambda x, g: _rms(x, g), [x], [gpre], [_sds((S, D_MODEL), BF16)])[0]
    tm = min(512, S)

    def up_body(h_ref, w_ref, u_ref, act_ref):
        hh = h_ref[...]
        for j in range(N_FF_BLK):
            ab = _dot(hh, w_ref[2 * FF_PAD * j:2 * FF_PAD * (j + 1), :], _NT)
            u_ref[:, 2 * FF_PAD * j:2 * FF_PAD * (j + 1)] = ab.astype(BF16)
            act_ref[:, FF_PAD * j:FF_PAD * (j + 1)] = (_silu(ab[:, :FF_PAD]) * ab[:, FF_PAD:]).astype(BF16)

    u, act = _pcall(
        up_body, name=tag + "_up", grid=(S // tm,),
        in_specs=[pl.BlockSpec((tm, D_MODEL), lambda i: (i, 0)),
                  pl.BlockSpec(wt.shape, lambda i: (0, 0), pipeline_mode=pl.Buffered(1))],
        out_specs=[pl.BlockSpec((tm, N_DEV * FF_PAD), lambda i: (i, 0)),
                   pl.BlockSpec((tm, N_FF_BLK * FF_PAD), lambda i: (i, 0))],
        out_shape=[_sds((S, N_DEV * FF_PAD), BF16), _sds((S, N_FF_BLK * FF_PAD), BF16)],
    )(h, wt)

    def down_body(act_ref, w_ref, x_ref, g_ref, f_ref, xo_ref):
        f = _dot(act_ref[...], w_ref[...])
        f_ref[...] = f
        xo_ref[...] = x_ref[...] + 0.5 * _rms(f, g_ref[...])

    wo = get_wo(act)
    row = pl.BlockSpec((tm, D_MODEL), lambda i: (i, 0))
    f, xo = _pcall(
        down_body, name=tag + "_down", grid=(S // tm,),
        in_specs=[pl.BlockSpec((tm, N_FF_BLK * FF_PAD), lambda i: (i, 0)),
                  pl.BlockSpec(wo.shape, lambda i: (0, 0), pipeline_mode=pl.Buffered(1)), row,
                  pl.BlockSpec((1, D_MODEL), lambda i: (0, 0))],
        out_specs=[row, row], out_shape=[_sds((S, D_MODEL)), _sds((S, D_MODEL))],
    )(act, wo, x, gpost)
    return xo, wo, (h, u, act, f)


def _ffn_bwd(tag, dxo, x, gpre, wt, wo, gpost, saved, on_dwo=None, on_dwt=None):
    h, u, act, f = saved
    S = x.shape[0]
    tm = min(256, S)

    def dact_body(f_ref, dxo_ref, g_ref, w_ref, u_ref, df_ref, du_ref, dg_ref):
        i = pl.program_id(0)
        df, dg = _rms_bwd(f_ref[...], g_ref[...], 0.5 * dxo_ref[...])
        df = df.astype(BF16)
        df_ref[...] = df

        @pl.when(i == 0)
        def _():
            dg_ref[...] = dg

        @pl.when(i > 0)
        def _():
            dg_ref[...] += dg

        for j in range(N_FF_BLK):
            dact = _dot(df, w_ref[FF_PAD * j:FF_PAD * (j + 1), :], _NT)
            ab = u_ref[:, 2 * FF_PAD * j:2 * FF_PAD * (j + 1)].astype(F32)
            a, b = ab[:, :FF_PAD], ab[:, FF_PAD:]
            sg = _sigmoid(a)
            du_ref[:, 2 * FF_PAD * j:2 * FF_PAD * j + FF_PAD] = (dact * b * (sg * (1.0 + a * (1.0 - sg)))).astype(BF16)
            du_ref[:, 2 * FF_PAD * j + FF_PAD:2 * FF_PAD * (j + 1)] = (dact * (a * sg)).astype(BF16)

    row = pl.BlockSpec((tm, D_MODEL), lambda i: (i, 0))
    vec = pl.BlockSpec((1, D_MODEL), lambda i: (0, 0))
    u_spec = pl.BlockSpec((tm, N_DEV * FF_PAD), lambda i: (i, 0))
    df, du, dgpost = _pcall(
        dact_body, name=tag + "_dact", grid=(S // tm,),
        in_specs=[row, row, vec, pl.BlockSpec(wo.shape, lambda i: (0, 0), pipeline_mode=pl.Buffered(1)), u_spec],
        out_specs=[row, u_spec, vec],
        out_shape=[_sds((S, D_MODEL), BF16), _sds((S, N_DEV * FF_PAD), BF16), _sds((1, D_MODEL))])(f, dxo, gpost, wo, u)
    dwo = _mm_tn(tag + "_dwo", act, df, ts=2048, tn_a=FF_PAD)
    dwt = _mm_tn(tag + "_dwt", du, h, ts=2048, tn_a=FF_PAD, deps=[on_dwo(dwo)] if on_dwo is not None else [])
    dx, dgpre = _dh_pre_norm_bwd(tag + "_dh", du, wt, x, dxo, gpre, [on_dwt(dwt)] if on_dwt is not None else [])
    return dx, dgpre, dgpost


def _dh_pre_norm_bwd(name, dz, wt, x, dres, g, deps):
    S, K = dz.shape

    def body(dz_ref, w_ref, x_ref, dres_ref, g_ref, *rest):
        dx_ref, dg_ref = rest[len(deps):]
        i = pl.program_id(0)
        dx, dg = _rms_bwd(x_ref[...], g_ref[...], _dot(dz_ref[...], w_ref[...]))
        dx_ref[...] = dx + dres_ref[...]

        @pl.when(i == 0)
        def _():
            dg_ref[...] = dg

        @pl.when(i > 0)
        def _():
            dg_ref[...] += dg

    th = min(256, S)
    row = pl.BlockSpec((th, D_MODEL), lambda i: (i, 0))
    vec = pl.BlockSpec((1, D_MODEL), lambda i: (0, 0))
    return _pcall(
        body, name=name, grid=(S // th,),
        in_specs=[pl.BlockSpec((th, K), lambda i: (i, 0)),
                  pl.BlockSpec(wt.shape, lambda i: (0, 0), pipeline_mode=pl.Buffered(1)), row, row, vec]
        + [pl.BlockSpec(memory_space=pl.ANY)] * len(deps),
        out_specs=[row, vec], out_shape=[_sds((S, D_MODEL)), _sds((1, D_MODEL))])(dz, wt, x, dres, g, *deps)


def _gla_chunk_common(k, flow, wfu, bf, tri):
    f = _dot(flow, wfu) + bf
    la = _log_sigmoid(f) * (1.0 / GATE_TEMP)
    b = _tri_dot(tri, la)
    b_end = b[CHUNK - 1:CHUNK, :]
    w = jnp.exp(b_end - b)
    kt = k * w
    d = jnp.exp(b_end)
    return f, w, kt, d


def _tri_matrix(lower):
    r = lax.broadcasted_iota(jnp.int32, (CHUNK, CHUNK), 0)
    c = lax.broadcasted_iota(jnp.int32, (CHUNK, CHUNK), 1)
    return jnp.where((r >= c) if lower else (r <= c), 1.0, 0.0).astype(BF16)


def _gla_fwd(proj, wfu, bf):
    S = proj.shape[0]
    nc = S // CHUNK

    def body(q_ref, k_ref, v_ref, flow_ref, wfu_ref, bf_ref, o_ref, st_ref, state):
        c = pl.program_id(0)

        @pl.when(c == 0)
        def _():
            state[...] = jnp.zeros_like(state)

        _, _, kt, d = _gla_chunk_common(k_ref[...], flow_ref[...], wfu_ref[...], bf_ref[...], _tri_matrix(True))
        q = q_ref[...].astype(F32) * (GLA_HDK ** -0.5)
        v = v_ref[...]
        outs = []
        for h in range(GLA_HEADS):
            ks, vs = slice(h * GLA_HDK, (h + 1) * GLA_HDK), slice(h * GLA_HDV, (h + 1) * GLA_HDV)
            st = state[vs, :] * d[:, ks] + _dot(v[:, vs], kt[:, ks], _TN)
            state[vs, :] = st
            outs.append(_dot(q[:, ks], st, _NT))
        o_ref[...] = jnp.concatenate(outs, axis=-1)
        st_ref[...] = state[...]

    return _pcall(
        body, name="gla_fwd", grid=(nc,),
        in_specs=[pl.BlockSpec((CHUNK, GLA_DK), lambda c: (c, 0)), pl.BlockSpec((CHUNK, GLA_DK), lambda c: (c, 1)),
                  pl.BlockSpec((CHUNK, GLA_DV), lambda c: (c, 1)),
                  pl.BlockSpec((CHUNK, FLOW_W), lambda c: (c, (INT_W - FLOW_W) // FLOW_W)),
                  pl.BlockSpec(wfu.shape, lambda c: (0, 0)), pl.BlockSpec(bf.shape, lambda c: (0, 0))],
        out_specs=[pl.BlockSpec((CHUNK, GLA_DV), lambda c: (c, 0)),
                   pl.BlockSpec((None, GLA_DV, GLA_HDK), lambda c: (c, 0, 0))],
        out_shape=[_sds((S, GLA_DV)), _sds((nc, GLA_DV, GLA_HDK))],
        scratch=[pltpu.VMEM((GLA_DV, GLA_HDK), F32)])(proj, proj, proj, proj, wfu, bf)


def _gla_bwd(proj, wfu, bf, states, do):
    S = proj.shape[0]
    nc = S // CHUNK

    def body(q_ref, k_ref, v_ref, flow_ref, wfu_ref, bf_ref, st_ref, stp_ref, do_ref,
             dq_ref, dk_ref, dv_ref, dflow_ref, dwfu_ref, dbf_ref, dstate):
        step = pl.program_id(0)
        c = nc - 1 - step

        @pl.when(step == 0)
        def _():
            dstate[...] = jnp.zeros_like(dstate)
            dwfu_ref[...] = jnp.zeros_like(dwfu_ref)
            dbf_ref[...] = jnp.zeros_like(dbf_ref)

        k, flow, wfu_v = k_ref[...], flow_ref[...], wfu_ref[...]
        f, w, kt, d = _gla_chunk_common(k, flow, wfu_v, bf_ref[...], _tri_matrix(True))
        q = q_ref[...].astype(F32) * (GLA_HDK ** -0.5)
        v, dout = v_ref[...], do_ref[...]
        has_prev = (c > 0).astype(F32)
        dqs, dkts, dvs, dds = [], [], [], []
        for h in range(GLA_HEADS):
            ks, vs = slice(h * GLA_HDK, (h + 1) * GLA_HDK), slice(h * GLA_HDV, (h + 1) * GLA_HDV)
            st = st_ref[vs, :]
            dqs.append(_dot(dout[:, vs], st))
            dst = dstate[vs, :] + _dot(dout[:, vs], q[:, ks], _TN)
            dkts.append(_dot(v[:, vs], dst))
            dvs.append(_dot(kt[:, ks], dst, _NT))
            dds.append(jnp.sum(dst * stp_ref[vs, :], axis=0, keepdims=True) * has_prev)
            dstate[vs, :] = dst * d[:, ks]
        dq_ref[...] = (jnp.concatenate(dqs, axis=-1) * (GLA_HDK ** -0.5)).astype(dq_ref.dtype)
        dv_ref[...] = jnp.concatenate(dvs, axis=-1).astype(dv_ref.dtype)
        dkt = jnp.concatenate(dkts, axis=-1)
        dd = jnp.concatenate(dds, axis=-1)
        dk_ref[...] = (dkt * w).astype(dk_ref.dtype)
        de = dkt * kt
        db_end = jnp.sum(de, axis=0, keepdims=True) + dd * d
        dla = db_end - _tri_dot(_tri_matrix(False), de)
        df = dla * (1.0 - _sigmoid(f)) * (1.0 / GATE_TEMP)
        dflow_ref[...] = _dot(df, wfu_v, _NT).astype(dflow_ref.dtype)
        dwfu_ref[...] += _dot(flow, df, _TN)
        dbf_ref[...] += jnp.sum(df, axis=0, keepdims=True)

    rc = lambda s: nc - 1 - s
    return _pcall(
        body, name="gla_bwd", grid=(nc,),
        in_specs=[pl.BlockSpec((CHUNK, GLA_DK), lambda s: (rc(s), 0)), pl.BlockSpec((CHUNK, GLA_DK), lambda s: (rc(s), 1)),
                  pl.BlockSpec((CHUNK, GLA_DV), lambda s: (rc(s), 1)),
                  pl.BlockSpec((CHUNK, FLOW_W), lambda s: (rc(s), (INT_W - FLOW_W) // FLOW_W)),
                  pl.BlockSpec(wfu.shape, lambda s: (0, 0)), pl.BlockSpec(bf.shape, lambda s: (0, 0)),
                  pl.BlockSpec((None, GLA_DV, GLA_HDK), lambda s: (rc(s), 0, 0)),
                  pl.BlockSpec((None, GLA_DV, GLA_HDK), lambda s: (jnp.maximum(rc(s) - 1, 0), 0, 0)),
                  pl.BlockSpec((CHUNK, GLA_DV), lambda s: (rc(s), 0))],
        out_specs=[pl.BlockSpec((CHUNK, GLA_DK), lambda s: (rc(s), 0)), pl.BlockSpec((CHUNK, GLA_DK), lambda s: (rc(s), 0)),
                   pl.BlockSpec((CHUNK, GLA_DV), lambda s: (rc(s), 0)), pl.BlockSpec((CHUNK, FLOW_W), lambda s: (rc(s), 0)),
                   pl.BlockSpec(wfu.shape, lambda s: (0, 0)), pl.BlockSpec(bf.shape, lambda s: (0, 0))],
        out_shape=[_sds((S, GLA_DK), BF16), _sds((S, GLA_DK), BF16), _sds((S, GLA_DV), BF16), _sds((S, FLOW_W), BF16),
                   _sds(wfu.shape), _sds(bf.shape)],
        scratch=[pltpu.VMEM((GLA_DV, GLA_HDK), F32)])(proj, proj, proj, proj, wfu, bf, states, states, do)


def _pool_counts(tm, i):
    t = (lax.broadcasted_iota(jnp.int32, (tm, POOL_GD), 0) + i * tm + 1).astype(F32)
    return [jnp.minimum(t, float(w)) for w in POOL_WINDOWS]


def _pool_fwd(proj, w_pool, pool_scale, tm=256):
    S = proj.shape[0]
    tm = min(tm, S // 2)
    col = (3072 + 3072) // POOL_W
    hb = tm // POOL_HALO

    def body(p_ref, halo_ref, wp_ref, sc_ref, mixed_ref, out_ref):
        i = pl.program_id(0)
        p = p_ref[...].astype(F32)
        halo = halo_ref[...].astype(F32) * (i > 0).astype(F32)
        ext = jnp.concatenate([halo, p], axis=0)
        n = tm + POOL_HALO
        sums, acc, k = {}, ext, 1
        while k < POOL_WINDOWS[-1]:
            acc = acc + pltpu.roll(acc, k, axis=0)
            k *= 2
            sums[k] = acc
        cnts = _pool_counts(tm, i)
        mixed, lin = [], []
        for g, w in enumerate(POOL_WINDOWS):
            ls = slice(g * POOL_GD, (g + 1) * POOL_GD)
            m = sums[w][POOL_HALO:n, ls] / cnts[g] - p[:, ls]
            mixed.append(m)
            lin.append(_dot(m, wp_ref[g]))
        mixed_ref[...] = jnp.concatenate(mixed, axis=-1)
        out_ref[...] = (jnp.concatenate(lin, axis=-1) * sc_ref[...]).astype(out_ref.dtype)

    return _pcall(
        body, name="pool_fwd", grid=(S // tm,),
        in_specs=[pl.BlockSpec((tm, POOL_W), lambda i: (i, col)),
                  pl.BlockSpec((POOL_HALO, POOL_W), lambda i: (jnp.maximum(i * hb - 1, 0), col)),
                  pl.BlockSpec(w_pool.shape, lambda i: (0, 0, 0)), pl.BlockSpec(pool_scale.shape, lambda i: (0, 0))],
        out_specs=[pl.BlockSpec((tm, POOL_W), lambda i: (i, 0)), pl.BlockSpec((tm, POOL_W), lambda i: (i, 0))],
        out_shape=[_sds((S, POOL_W)), _sds((S, POOL_W), BF16)])(proj, proj, w_pool, pool_scale)


def _pool_lin_bwd(dout, mixed, w_pool, pool_scale):
    S = dout.shape[0]

    def fn(dout, mixed, wp, sc):
        dlin = dout * sc
        dm, dwp, lin = [], [], []
        for g in range(len(POOL_WINDOWS)):
            ls = slice(g * POOL_GD, (g + 1) * POOL_GD)
            lin.append(_dot(mixed[:, ls], wp[g]))
            dm.append(_dot(dlin[:, ls], wp[g], _NT))
            dwp.append(_dot(mixed[:, ls], dlin[:, ls], _TN))
        dsc = jnp.sum(dout * jnp.concatenate(lin, axis=-1), axis=0, keepdims=True)
        return jnp.concatenate(dm, axis=-1), jnp.concatenate(dwp, axis=0), dsc

    return _rowwise("pool_lin_bwd", fn, [dout, mixed], [w_pool, pool_scale], [_sds((S, POOL_W))],
                    [_sds((len(POOL_WINDOWS) * POOL_GD, POOL_GD)), _sds((1, POOL_W))])


def _pool_win_bwd(dmixed, tm=256):
    S = dmixed.shape[0]
    tm = min(tm, S // 2)
    nt = S // tm
    hb = tm // POOL_HALO

    def body(dm_ref, halo_ref, dp_ref):
        i = pl.program_id(0)
        dm = dm_ref[...]
        halo = halo_ref[...] * (i < nt - 1).astype(F32)
        cnts = _pool_counts(tm, i)
        cnts_h = [c[:POOL_HALO] for c in _pool_counts(tm, i + 1)]
        r = jnp.concatenate([jnp.concatenate([dm[:, g * POOL_GD:(g + 1) * POOL_GD] / cnts[g] for g in range(4)], axis=-1),
                             jnp.concatenate([halo[:, g * POOL_GD:(g + 1) * POOL_GD] / cnts_h[g] for g in range(4)], axis=-1)],
                            axis=0)
        n = tm + POOL_HALO
        sums, acc, k = {}, r, 1
        while k < POOL_WINDOWS[-1]:
            acc = acc + pltpu.roll(acc, n - k, axis=0)
            k *= 2
            sums[k] = acc
        dp = [sums[w][:tm, g * POOL_GD:(g + 1) * POOL_GD] for g, w in enumerate(POOL_WINDOWS)]
        dp_ref[...] = (jnp.concatenate(dp, axis=-1) - dm).astype(dp_ref.dtype)

    return _pcall(
        body, name="pool_win_bwd", grid=(nt,),
        in_specs=[pl.BlockSpec((tm, POOL_W), lambda i: (i, 0)),
                  pl.BlockSpec((POOL_HALO, POOL_W), lambda i: (jnp.minimum((i + 1) * hb, S // POOL_HALO - 1), 0))],
        out_specs=pl.BlockSpec((tm, POOL_W), lambda i: (i, 0)),
        out_shape=_sds((S, POOL_W), BF16))(dmixed, dmixed)


def _xattn_probs(q, kv, h):
    hs = slice(h * XA_HD, (h + 1) * XA_HD)
    s = _dot(q[:, hs], kv[:, hs], _NT) * (XA_HD ** -0.5)
    s = s - jnp.max(s, axis=-1, keepdims=True)
    e = jnp.exp(s)
    return e / jnp.sum(e, axis=-1, keepdims=True)


def _xattn_fwd(proj, kv):
    S = proj.shape[0]

    def fn(q, kv):
        outs = []
        for h in range(XA_HEADS):
            p = _xattn_probs(q, kv, h)
            outs.append(_dot(p, kv[:, XA_W + h * XA_HD:XA_W + (h + 1) * XA_HD]))
        return jnp.concatenate(outs, axis=-1)

    return _rowwise("xattn_fwd", fn, [_Win(proj, XA_W, (3072 + 3072 + POOL_W) // XA_W)], [kv], [_sds((S, XA_W), BF16)])[0]


def _xattn_bwd(proj, kv, dxa):
    S = proj.shape[0]

    def fn(q, dxa, kv):
        dqs, dks, dvs = [], [], []
        for h in range(XA_HEADS):
            hs = slice(h * XA_HD, (h + 1) * XA_HD)
            vh = kv[:, XA_W + h * XA_HD:XA_W + (h + 1) * XA_HD]
            p = _xattn_probs(q, kv, h)
            dp = _dot(dxa[:, hs], vh, _NT)
            ds = p * (dp - jnp.sum(p * dp, axis=-1, keepdims=True)) * (XA_HD ** -0.5)
            dqs.append(_dot(ds, kv[:, hs]))
            dks.append(_dot(ds, q[:, hs], _TN))
            dvs.append(_dot(p, dxa[:, hs], _TN))
        return jnp.concatenate(dqs, axis=-1), jnp.concatenate(dks + dvs, axis=-1)

    return _rowwise("xattn_bwd", fn, [_Win(proj, XA_W, (3072 + 3072 + POOL_W) // XA_W), dxa], [kv],
                    [_sds((S, XA_W), BF16)], [_sds(kv.shape)])


def _mix_fwd(x1, mem, W):
    S = x1.shape[0]
    M = mem.shape[0]
    h = _rowwise("mix_pre", lambda x, g: _rms(x, g), [x1], [W["mix_pre_g"]], [_sds((S, D_MODEL), BF16)])[0]
    tm = min(512, S)
    proj = _mm("mix_proj", h, W["w_int_t"], grid=(INT_NBLK, S // tm, 1), a_blk=(tm, D_MODEL), a_map=lambda j, i, k: (i, 0),
               b_blk=(INT_BLK, D_MODEL), b_map=lambda j, i, k: (j, 0), o_shape=(S, INT_W), o_blk=(tm, INT_BLK),
               o_map=lambda j, i, k: (i, j), dims=_NT, out_dtype=BF16)
    o_raw, states = _gla_fwd(proj, W["w_fu_pad"], W["b_f"])
    ya_in = _rowwise("gla_out", _head_rms_gate, [o_raw, _Win(proj, GLA_DV, 2)], [W["gla_norm_g"]],
                     [_sds((S, GLA_DV), BF16)])[0]
    mixed, pool_out = _pool_fwd(proj, W["w_pool"], W["pool_scale"])
    mem_n = _rowwise("mem_norm", lambda m, g: _rms(m, g), [mem], [W["mem_norm_g"]], [_sds((M, D_MODEL), BF16)])[0]
    kv = _mm_nn("mem_kv", mem_n, W["w_mem_kv"])
    xa = _xattn_fwd(proj, kv)
    ya = _mm_nn("up_gla", ya_in, W["w_up_gla"], BF16)
    yb = _mm_nn("up_pool", pool_out, W["w_up_pool"], BF16)
    yc = _mm_nn("up_xattn", xa, W["w_up_xattn"], BF16)
    merged = _rowwise("merge", _merge, [_Win(proj, 3 * D_MODEL, 1), ya, yb, yc], [], [_sds((S, D_MODEL), BF16)])[0]
    y = _mm_nn("mix_out", merged, W["w_o"])
    x2 = _rowwise("mix_post", lambda x, y, g: x + _rms(y, g), [x1, y], [W["mix_post_g"]], [_sds((S, D_MODEL))])[0]
    return x2, (h, proj, o_raw, states, ya_in, mixed, pool_out, mem_n, kv, xa, ya, yb, yc, merged, y)


def _head(x3, tgt, final_g):
    S = x3.shape[0]

    def head(x3, t, g):
        out, vjp = jax.vjp(_rms, x3, g)
        e = out - t
        loss = 0.5 * jnp.sum(jnp.mean(e * e, axis=-1, keepdims=True), axis=0, keepdims=True)
        dx3, dg = vjp(e * (1.0 / D_MODEL))
        return dx3, jnp.broadcast_to(loss, (1, 128)), dg

    dx3, loss_acc, d_final_g = _rowwise("head", head, [x3, tgt], [final_g], [_sds((S, D_MODEL))],
                                        [_sds((1, 128)), _sds((1, D_MODEL))])
    return loss_acc[0, 0], dx3, d_final_g


def _mix_bwd(dx2, x1, mem, W, saved, on_grads=None):
    h, proj, o_raw, states, ya_in, mixed, pool_out, mem_n, kv, xa, ya, yb, yc, merged, y = saved
    S = x1.shape[0]

    def mix_post_bwd(y, dx2, g):
        return _rms_bwd(y, g, dx2)

    dy, d_mix_post_g = _rowwise("mix_post_bwd", mix_post_bwd, [y, dx2], [W["mix_post_g"]], [_sds((S, D_MODEL), BF16)],
                                [_sds((1, D_MODEL))])
    d_w_o = _mm_tn("d_w_o", merged, dy)
    dmerged = _mm_nt("d_merged", dy, W["w_o"], BF16)

    def merge_bwd(gates, ya, yb, yc, dm):
        _, vjp = jax.vjp(_merge, gates, ya, yb, yc)
        return vjp(dm.astype(F32))

    dgates, dya, dyb, dyc = _rowwise("merge_bwd", merge_bwd, [_Win(proj, 3 * D_MODEL, 1), ya, yb, yc, dmerged], [],
                                     [_sds((S, 3 * D_MODEL), BF16)] + [_sds((S, D_MODEL), BF16)] * 3)
    d_w_up_gla = _mm_tn("d_w_up_gla", ya_in, dya)
    d_w_up_pool = _mm_tn("d_w_up_pool", pool_out, dyb)
    d_w_up_xattn = _mm_tn("d_w_up_xattn", xa, dyc)
    d_ya_in = _mm_nt("d_ya_in", dya, W["w_up_gla"], BF16)
    d_pool_out = _mm_nt("d_pool_out", dyb, W["w_up_pool"], BF16)
    d_xa = _mm_nt("d_xa", dyc, W["w_up_xattn"], BF16)

    def gla_out_bwd(o, g_out, d, gn):
        _, vjp = jax.vjp(_head_rms_gate, o, g_out, gn)
        return vjp(d.astype(F32))

    do_raw, dg_out, d_gla_norm_g = _rowwise("gla_out_bwd", gla_out_bwd, [o_raw, _Win(proj, GLA_DV, 2), d_ya_in],
                                            [W["gla_norm_g"]], [_sds((S, GLA_DV)), _sds((S, GLA_DV), BF16)],
                                            [_sds((1, GLA_DV))])
    dq, dk, dv, dflow, d_wfu_pad, d_b_f = _gla_bwd(proj, W["w_fu_pad"], W["b_f"], states, do_raw)
    dmixed, d_w_pool, d_pool_scale = _pool_lin_bwd(d_pool_out, mixed, W["w_pool"], W["pool_scale"])
    dp_in = _pool_win_bwd(dmixed)
    dxq, dkv = _xattn_bwd(proj, kv, d_xa)
    d_w_mem_kv = _mm_tn("d_w_mem_kv", mem_n, dkv)
    dmem_n = _mm_nt("d_mem_n", dkv, W["w_mem_kv"])
    d_mem_norm_g = _rowwise("mem_norm_bwd", lambda m, d, g: _rms_bwd(m, g, d)[1], [mem, dmem_n], [W["mem_norm_g"]], [],
                            [_sds((1, D_MODEL))])[0]
    dproj = jnp.concatenate([dq, dk, dv, dg_out, dgates, dp_in, dxq, dflow], axis=-1)
    d_w_int_t = _mm_tn("d_w_int", dproj, h, ts=1024, tn_a=INT_BLK)
    grads = dict(
        w_int_t=d_w_int_t, w_fu_pad=d_wfu_pad, b_f=d_b_f, gla_norm_g=d_gla_norm_g, w_pool=d_w_pool,
        pool_scale=d_pool_scale, mem_norm_g=d_mem_norm_g, w_mem_kv=d_w_mem_kv, w_up_gla=d_w_up_gla,
        w_up_pool=d_w_up_pool, w_up_xattn=d_w_up_xattn, w_o=d_w_o, mix_post_g=d_mix_post_g)
    deps = [on_grads(grads)] if on_grads is not None else []
    dx1, grads["mix_pre_g"] = _dh_pre_norm_bwd("d_mix_h", dproj, W["w_int_t"], x1, dx2, W["mix_pre_g"], deps)
    return dx1, grads


def _mesh_pos():
    x, y, c = lax.axis_index("x"), lax.axis_index("y"), lax.axis_index("c")
    return x, y, c, 4 * x + 2 * y + c


def _peer(x, y, c, r):
    px = 1 - x if r & 4 else x
    py = 1 - y if r & 2 else y
    pc = 1 - c if r & 1 else c
    return (px, py, pc), 4 * px + 2 * py + pc


_ALL_PEERS = tuple(range(1, N_DEV))
_SIBLING = 1
_SAME_CORE = (2, 4, 6)


def _dev_slot(ref, dev):
    return ref.at[dev]


class _Plan:
    def __init__(self, scatter, slots=None, shapes=None):
        self.scatter, self.slots, self.shapes = scatter, slots or {}, shapes or {}

    def src(self, srcs, a, dev):
        return self.slots.get(a, _dev_slot)(srcs[a], dev) if self.scatter else srcs[a]

    def dst(self, lands, a, dev):
        return lands[a].at[dev] if self.scatter else self.slots.get(a, _dev_slot)(lands[a], dev)

    def landing_zones(self, arrays):
        lands = []
        for a, arr in enumerate(arrays):
            if self.scatter:
                lands.append(lax.empty((N_DEV,) + tuple(self.shapes.get(a, arr.shape[1:])), arr.dtype))
            elif a in self.shapes:
                lands.append(jnp.zeros(self.shapes[a], arr.dtype))
            else:
                lands.append(lax.empty((N_DEV,) + arr.shape, arr.dtype))
        return lands


_GATHER, _SCATTER = _Plan(False), _Plan(True)


def _peer_copies(srcs, lands, send_sems, recv_sems, plan, peers=_ALL_PEERS):
    x, y, c, me = _mesh_pos()
    cps = []
    for r in peers:
        pos, peer = _peer(x, y, c, r)
        for a in range(len(srcs)):
            k = a * (N_DEV - 1) + r - 1
            cps.append(pltpu.make_async_remote_copy(
                src_ref=plan.src(srcs, a, peer), dst_ref=plan.dst(lands, a, me),
                send_sem=send_sems.at[k], recv_sem=recv_sems.at[k], device_id=pos,
                device_id_type=pl.DeviceIdType.MESH))
    return cps


def _exchange(name, arrays, plan, after):
    n = len(arrays)

    def body(*refs):
        ins, outs = refs[:n], refs[n + len(after):2 * n + len(after)]
        send_sems, recv_sems, local_sems = refs[2 * n + len(after):]
        copies = _own_copies(ins, outs, local_sems, plan) + _peer_copies(ins, outs, send_sems, recv_sems, plan)
        for cp in copies:
            cp.start()
        for cp in copies:
            cp.wait()

    any_spec = pl.BlockSpec(memory_space=pl.ANY)
    out_shape = [_sds(l.shape, l.dtype) for l in plan.landing_zones(arrays)]
    return pl.pallas_call(
        body, name=name, in_specs=[any_spec] * (n + len(after)), out_specs=[any_spec] * n, out_shape=out_shape,
        scratch_shapes=[pltpu.SemaphoreType.DMA((n * (N_DEV - 1),)), pltpu.SemaphoreType.DMA((n * (N_DEV - 1),)),
                        pltpu.SemaphoreType.DMA((n,))])(*arrays, *after)


_HBM = pl.BlockSpec(memory_space=pltpu.HBM)
_SEM = pl.BlockSpec(memory_space=pltpu.SEMAPHORE)
_EFFECT = pltpu.SideEffectType.DATAFLOW_SIDE_EFFECTING


def _own_copies(srcs, lands, own_sems, plan):
    me = _mesh_pos()[3]
    return [pltpu.make_async_copy(plan.src(srcs, a, me), plan.dst(lands, a, me), own_sems.at[a])
            for a in range(len(srcs))]


def _exchange_start(name, arrays, plan):
    n = len(arrays)
    lands = plan.landing_zones(arrays)
    n_sem = n * (N_DEV - 1)

    def body(*refs):
        srcs, lands_ = refs[:n], refs[n:2 * n]
        send_sems, recv_sems, own_sems = refs[2 * n:2 * n + 3]
        token = refs[-1]
        for cp in _peer_copies(srcs, lands_, send_sems, recv_sems, plan) + _own_copies(srcs, lands_, own_sems, plan):
            cp.start()
        token[...] = jnp.zeros_like(token)

    hbm = lambda a: pltpu.HBM(a.shape, a.dtype)
    res = pl.pallas_call(
        body, name=name,
        out_shape=(pltpu.SemaphoreType.DMA((n_sem,)), pltpu.SemaphoreType.DMA((n_sem,)), pltpu.SemaphoreType.DMA((n,)),
                   *[hbm(a) for a in arrays], *[hbm(a) for a in lands], _sds((8, 128))),
        in_specs=[_HBM] * (2 * n),
        out_specs=(_SEM, _SEM, _SEM, *[_HBM] * (2 * n), pl.BlockSpec(memory_space=pltpu.VMEM)),
        input_output_aliases={i: 3 + i for i in range(2 * n)},
        compiler_params=pltpu.CompilerParams(has_side_effects=_EFFECT),
    )(*[pltpu.with_memory_space_constraint(a, pltpu.HBM) for a in list(arrays) + lands])
    return (res[:3], res[3:3 + n], res[3 + n:3 + 2 * n], plan), res[-1]


def _exchange_wait(name, state, after):
    sems, srcs, lands, plan = state
    n = len(srcs)

    def body(*refs):
        srcs_, lands_ = refs[:n], refs[n:2 * n]
        send_sems, recv_sems, own_sems = refs[2 * n:2 * n + 3]
        for cp in _peer_copies(srcs_, lands_, send_sems, recv_sems, plan):
            cp.wait_send()
            cp.wait_recv()
        for cp in _own_copies(srcs_, lands_, own_sems, plan):
            cp.wait()

    hbm = lambda a: pltpu.HBM(a.shape, a.dtype)
    res = pl.pallas_call(
        body, name=name, out_shape=tuple(hbm(a) for a in list(srcs) + list(lands)),
        in_specs=[_HBM] * (2 * n) + [_SEM] * 3 + [pl.BlockSpec(memory_space=pl.ANY)] * len(after),
        out_specs=tuple([_HBM] * (2 * n)), input_output_aliases={i: i for i in range(2 * n)},
        compiler_params=pltpu.CompilerParams(has_side_effects=_EFFECT),
    )(*srcs, *lands, *sems, *after)
    return res[n:]


def _gather_start(name, arrays, plan, after):
    n = len(arrays)
    lands = plan.landing_zones(arrays)
    n_sem = n * (N_DEV - 1)

    def body(*refs):
        srcs, lands_ = refs[:n], refs[n:2 * n]
        send_sems, recv_sems, own_sems = refs[2 * n + len(after):2 * n + len(after) + 3]
        token = refs[-1]
        for cp in (_peer_copies(srcs, lands_, send_sems, recv_sems, plan, (_SIBLING,) + _SAME_CORE)
                   + _own_copies(srcs, lands_, own_sems, plan)):
            cp.start()
        token[...] = jnp.zeros_like(token)

    hbm = lambda a: pltpu.HBM(a.shape, a.dtype)
    res = pl.pallas_call(
        body, name=name,
        out_shape=(pltpu.SemaphoreType.DMA((n_sem,)), pltpu.SemaphoreType.DMA((n_sem,)), pltpu.SemaphoreType.DMA((n,)),
                   *[hbm(a) for a in arrays], *[hbm(a) for a in lands], _sds((8, 128))),
        in_specs=[_HBM] * (2 * n) + [pl.BlockSpec(memory_space=pl.ANY)] * len(after),
        out_specs=(_SEM, _SEM, _SEM, *[_HBM] * (2 * n), pl.BlockSpec(memory_space=pltpu.VMEM)),
        input_output_aliases={i: 3 + i for i in range(2 * n)},
        compiler_params=pltpu.CompilerParams(has_side_effects=_EFFECT),
    )(*[pltpu.with_memory_space_constraint(a, pltpu.HBM) for a in list(arrays) + lands], *after)
    return (res[:3], res[3:3 + n], res[3 + n:3 + 2 * n], plan), res[-1]


def _pass_on_copies(lands, send_sems, recv_sems, plan):
    x, y, c, _ = _mesh_pos()
    sibling = _peer(x, y, c, _SIBLING)[0]
    cps = []
    for i, r in enumerate(_SAME_CORE):
        owner = _peer(x, y, c, r)[1]
        for a in range(len(lands)):
            k = a * len(_SAME_CORE) + i
            cps.append(pltpu.make_async_remote_copy(
                src_ref=plan.dst(lands, a, owner), dst_ref=plan.dst(lands, a, owner), send_sem=send_sems.at[k],
                recv_sem=recv_sems.at[k], device_id=sibling, device_id_type=pl.DeviceIdType.MESH))
    return cps


def _gather_pass_on(name, state, after):
    sems, srcs, lands, plan = state
    n = len(srcs)
    n_sem = n * len(_SAME_CORE)

    def body(*refs):
        srcs_, lands_ = refs[:n], refs[n:2 * n]
        send_sems, recv_sems = refs[2 * n], refs[2 * n + 1]
        on_send, on_recv = refs[2 * n + 3 + len(after)], refs[2 * n + 4 + len(after)]
        arrivals = _peer_copies(srcs_, lands_, send_sems, recv_sems, plan, _SAME_CORE)
        for arrived, on in zip(arrivals, _pass_on_copies(lands_, on_send, on_recv, plan)):
            arrived.wait_recv()
            on.start()

    hbm = lambda a: pltpu.HBM(a.shape, a.dtype)
    res = pl.pallas_call(
        body, name=name,
        out_shape=(pltpu.SemaphoreType.DMA((n_sem,)), pltpu.SemaphoreType.DMA((n_sem,)),
                   *[hbm(a) for a in list(srcs) + list(lands)]),
        in_specs=[_HBM] * (2 * n) + [_SEM] * 3 + [pl.BlockSpec(memory_space=pl.ANY)] * len(after),
        out_specs=(_SEM, _SEM, *[_HBM] * (2 * n)), input_output_aliases={i: 2 + i for i in range(2 * n)},
        compiler_params=pltpu.CompilerParams(has_side_effects=_EFFECT),
    )(*srcs, *lands, *sems, *after)
    return sems, res[:2], res[2:2 + n], res[2 + n:], plan


def _gather_wait(name, state, after):
    sems, on_sems, srcs, lands, plan = state
    n = len(srcs)

    def body(*refs):
        srcs_, lands_ = refs[:n], refs[n:2 * n]
        send_sems, recv_sems, own_sems, on_send, on_recv = refs[2 * n:2 * n + 5]
        for cp in _peer_copies(srcs_, lands_, send_sems, recv_sems, plan, (_SIBLING,)):
            cp.wait_recv()
        for cp in _peer_copies(srcs_, lands_, send_sems, recv_sems, plan, (_SIBLING,) + _SAME_CORE):
            cp.wait_send()
        for cp in _own_copies(srcs_, lands_, own_sems, plan):
            cp.wait()
        for cp in _pass_on_copies(lands_, on_send, on_recv, plan):
            cp.wait_send()
            cp.wait_recv()

    hbm = lambda a: pltpu.HBM(a.shape, a.dtype)
    res = pl.pallas_call(
        body, name=name, out_shape=tuple(hbm(a) for a in list(srcs) + list(lands)),
        in_specs=[_HBM] * (2 * n) + [_SEM] * 5 + [pl.BlockSpec(memory_space=pl.ANY)] * len(after),
        out_specs=tuple([_HBM] * (2 * n)), input_output_aliases={i: i for i in range(2 * n)},
        compiler_params=pltpu.CompilerParams(has_side_effects=_EFFECT),
    )(*srcs, *lands, *sems, *on_sems, *after)
    return res[n:]


def _adam(name, recv, w, m, v):
    shape = w.shape
    R, C = shape
    tr = R
    while N_DEV * tr * C * 4 > 6 * 1024 * 1024 and tr % 32 == 0:
        tr //= 2

    def body(recv_ref, w_ref, m_ref, v_ref, g_ref, d_ref, m2_ref, v2_ref):
        g = recv_ref[0].astype(F32)
        for j in range(1, N_DEV):
            g = g + recv_ref[j].astype(F32)
        w_, m_, v_ = w_ref[...], m_ref[...], v_ref[...]
        m2 = ADAM_B1 * m_ + (1.0 - ADAM_B1) * g
        v2 = ADAM_B2 * v_ + (1.0 - ADAM_B2) * (g * g)
        m_hat = m2 / (1.0 - ADAM_B1 ** ADAM_STEP)
        v_hat = v2 / (1.0 - ADAM_B2 ** ADAM_STEP)
        g_ref[...] = g
        d_ref[...] = -ADAM_LR * (m_hat / (jnp.sqrt(v_hat) + ADAM_EPS) + ADAM_WD * w_)
        m2_ref[...] = m2
        v2_ref[...] = v2

    blk = pl.BlockSpec((tr, C), lambda i: (i, 0))
    return _pcall(body, name=name, grid=(R // tr,),
                  in_specs=[pl.BlockSpec((N_DEV, tr, C), lambda i: (0, i, 0)), blk, blk, blk],
                  out_specs=[blk] * 4, out_shape=[_sds(shape)] * 4)(recv, w, m, v)


_NAMES = ['ffn1_pre_g', 'ffn1_w_in', 'ffn1_w_out', 'ffn1_post_g', 'mix_pre_g', 'w_in', 'w_fu', 'b_f', 'gla_norm_g',
          'w_pool', 'pool_scale', 'mem_norm_g', 'w_mem_kv', 'w_up_gla', 'w_up_pool', 'w_up_xattn', 'w_o', 'mix_post_g',
          'ffn2_pre_g', 'ffn2_w_in', 'ffn2_w_out', 'ffn2_post_g', 'final_g']
_SHARDED = ['ffn1_w_in', 'ffn1_w_out', 'w_in', 'w_fu', 'w_mem_kv', 'w_up_gla', 'w_up_pool', 'w_up_xattn', 'w_o',
            'ffn2_w_in', 'ffn2_w_out']
_COL_SHARDED = ['w_up_pool', 'w_up_xattn']


def _cols_to_full(g):
    return jnp.transpose(g, (1, 0, 2)).reshape(g.shape[1], N_DEV * g.shape[2])


def _full_to_cols(f):
    R, C = f.shape
    return jnp.transpose(f.reshape(R, N_DEV, C // N_DEV), (1, 0, 2))


def _to_internal(w_in_t):
    o = 0
    parts = []
    for s in IN_SPLITS:
        parts.append(w_in_t[o:o + s])
        o += s
    q, k, v, g_out, f_low, p_in, xq, gates = parts
    f_low = jnp.pad(f_low, ((0, FLOW_W - GATE_RANK), (0, 0)))
    return jnp.concatenate([q, k, v, g_out, gates, p_in, xq, f_low], axis=0)


def _from_internal(d):
    q, k, v, g_out = d[0:512], d[512:1024], d[1024:2048], d[2048:3072]
    gates, p_in, xq = d[3072:6144], d[6144:6656], d[6656:7168]
    f_low = d[7168:7168 + GATE_RANK]
    return jnp.concatenate([q, k, v, g_out, f_low, p_in, xq, gates], axis=0)


_FFN_IN = ('ffn1_w_in', 'ffn2_w_in')
_FFN_OUT = ('ffn1_w_out', 'ffn2_w_out')
_GATHERS = {"ffn1_in": ['ffn1_w_in'], "ffn1_out": ['ffn1_w_out'],
            "mix": ['w_in', 'w_fu', 'w_mem_kv', 'w_up_gla', 'w_up_pool', 'w_up_xattn', 'w_o'],
            "ffn2": ['ffn2_w_in', 'ffn2_w_out']}


def _ffn_in_slot(ref, d):
    return ref.at[2 * (d % N_FF_BLK) + d // N_FF_BLK, pl.ds(0, FF_BLK)]


def _ffn_out_slot(ref, d):
    rows = FF_BLK // 2
    return ref.at[d // 2, pl.ds(pl.multiple_of((d % 2) * rows, rows), rows)]


def _ffn_plan(names, scatter):
    slots, shapes = {}, {}
    for a, n in enumerate(names):
        if n in _FFN_IN:
            slots[a] = _ffn_in_slot
            shapes[a] = (FF_BLK, D_MODEL) if scatter else (N_DEV, FF_PAD, D_MODEL)
        elif n in _FFN_OUT:
            slots[a] = _ffn_out_slot
            shapes[a] = (FF_BLK // 2, D_MODEL) if scatter else (N_FF_BLK, FF_PAD, D_MODEL)
    return _Plan(scatter, slots, shapes)


def _rows_to_full(g):
    return g.reshape(N_DEV * g.shape[1], g.shape[2])


def _mix_weights(gathered):
    W = {n: _rows_to_full(gathered[n]) for n in ('w_mem_kv', 'w_up_gla', 'w_o')}
    for n in _COL_SHARDED:
        W[n] = _cols_to_full(gathered[n])
    W["w_int_t"] = _to_internal(gathered["w_in"].reshape(IN_WIDTH, D_MODEL))
    W["w_fu_pad"] = jnp.pad(_cols_to_full(gathered["w_fu"]), ((0, FLOW_W - GATE_RANK), (0, 0)))
    return W


def _mix_chunks(G, n):
    if n == "w_in":
        return _from_internal(G["w_int_t"]).reshape(N_DEV, IN_SHARD, D_MODEL)
    if n == "w_fu":
        return _full_to_cols(G["w_fu_pad"][:GATE_RANK].astype(BF16))
    if n in _COL_SHARDED:
        return _full_to_cols(G[n])
    return G[n].reshape(N_DEV, G[n].shape[0] // N_DEV, G[n].shape[1])


def _step(x, mem, tgt, P, Mo, Vo):
    def native(n, a):
        return jnp.swapaxes(a, 0, 1) if n in _FFN_IN + ("w_in",) else a

    P, Mo, Vo = ({n: native(n, a) for n, a in d.items()} for d in (P, Mo, Vo))
    small = {n: P[n] for n in _NAMES if n not in _SHARDED}

    gather, token = {}, []
    for grp, names in _GATHERS.items():
        gather[grp], tok = _gather_start("gather_" + grp, [P[n].astype(BF16) for n in names], _ffn_plan(names, False),
                                         token)
        token = [tok]

    def gathered(grp, after):
        return _gather_wait("gather_" + grp + "_wait", _gather_pass_on("gather_" + grp + "_on", gather[grp], after), [])

    def ffn_wo(lands):
        return lambda act: lands[0].reshape(N_FF_BLK * FF_PAD, D_MODEL)

    w1t = gathered("ffn1_in", token)[0].reshape(N_DEV * FF_PAD, D_MODEL)
    x1, w1o, sv1 = _ffn_fwd("ffn1", x, small["ffn1_pre_g"], w1t, small["ffn1_post_g"],
                            lambda act: ffn_wo(gathered("ffn1_out", [act]))(act))

    gm = dict(zip(_GATHERS["mix"], gathered("mix", [x1])))
    Wm = {**small, **_mix_weights(gm)}
    x2, svm = _mix_fwd(x1, mem, Wm)

    w2t, w2o = gathered("ffn2", [x2])
    w2t = w2t.reshape(N_DEV * FF_PAD, D_MODEL)
    x3, w2o, sv2 = _ffn_fwd("ffn2", x2, small["ffn2_pre_g"], w2t, small["ffn2_post_g"], ffn_wo([w2o]))

    loss, dx3, d_final_g = _head(x3, tgt, small["final_g"])

    G = dict(final_g=d_final_g)
    scat = {}

    def start(grp, names, arrays):
        scat[grp] = names, _exchange_start("scatter_" + grp, arrays, _ffn_plan(names, True))
        return scat[grp][1][1]

    def ffn_starts(tag):
        return (lambda dwo: start(tag + "_out", [tag + "_w_out"], [dwo.reshape(N_FF_BLK, FF_PAD, D_MODEL)]),
                lambda dwt: start(tag + "_in", [tag + "_w_in"], [dwt.reshape(N_DEV, FF_PAD, D_MODEL)]))

    dx2, G["ffn2_pre_g"], G["ffn2_post_g"] = _ffn_bwd(
        "ffn2", dx3, x2, small["ffn2_pre_g"], w2t, w2o, small["ffn2_post_g"], sv2, *ffn_starts("ffn2"))
    dx1, Gm = _mix_bwd(dx2, x1, mem, Wm, svm,
                       lambda Gm: start("mix", _GATHERS["mix"], [_mix_chunks(Gm, n) for n in _GATHERS["mix"]]))
    G.update(Gm)
    dx, G["ffn1_pre_g"], G["ffn1_post_g"] = _ffn_bwd(
        "ffn1", dx1, x, small["ffn1_pre_g"], w1t, w1o, small["ffn1_post_g"], sv1, *ffn_starts("ffn1"))

    recv, outs = {}, {}
    done = [dx]
    for grp in ["ffn2_out", "ffn2_in", "mix", "small", "ffn1_out", "ffn1_in"]:
        if grp == "small":
            names = list(small)
            recv.update(zip(names, _exchange("gather_small_grads", [G[n] for n in small], _GATHER, done)))
        else:
            names, (state, _) = scat[grp]
            recv.update(zip(names, _exchange_wait("scatter_" + grp + "_wait", state, done[-1:])))
        for n in names:
            shp = P[n].shape
            shp2 = shp if len(shp) == 2 else (shp[0] * shp[1], shp[2])
            res = _adam("adam_" + n, recv[n].reshape((N_DEV,) + shp2), P[n].reshape(shp2), Mo[n].reshape(shp2),
                        Vo[n].reshape(shp2))
            outs[n] = [native(n, t.reshape(shp)) for t in res]
            done.append(res[-1])
    return loss, dx, outs


def kernel(x, mem, ffn1_pre_g, ffn1_w_in, ffn1_w_out, ffn1_post_g, mix_pre_g, w_in, w_fu, b_f, gla_norm_g, w_pool, pool_scale, mem_norm_g, w_mem_kv, w_up_gla, w_up_pool, w_up_xattn, w_o, mix_post_g, ffn2_pre_g, ffn2_w_in, ffn2_w_out, ffn2_post_g, final_g, loss_target, m_ffn1_pre_g, m_ffn1_w_in, m_ffn1_w_out, m_ffn1_post_g, m_mix_pre_g, m_w_in, m_w_fu, m_b_f, m_gla_norm_g, m_w_pool, m_pool_scale, m_mem_norm_g, m_w_mem_kv, m_w_up_gla, m_w_up_pool, m_w_up_xattn, m_w_o, m_mix_post_g, m_ffn2_pre_g, m_ffn2_w_in, m_ffn2_w_out, m_ffn2_post_g, m_final_g, v_ffn1_pre_g, v_ffn1_w_in, v_ffn1_w_out, v_ffn1_post_g, v_mix_pre_g, v_w_in, v_w_fu, v_b_f, v_gla_norm_g, v_w_pool, v_pool_scale, v_mem_norm_g, v_w_mem_kv, v_w_up_gla, v_w_up_pool, v_w_up_xattn, v_w_o, v_mix_post_g, v_ffn2_pre_g, v_ffn2_w_in, v_ffn2_w_out, v_ffn2_post_g, v_final_g):
    params = [ffn1_pre_g, ffn1_w_in, ffn1_w_out, ffn1_post_g, mix_pre_g, w_in, w_fu, b_f, gla_norm_g, w_pool, pool_scale, mem_norm_g, w_mem_kv, w_up_gla, w_up_pool, w_up_xattn, w_o, mix_post_g, ffn2_pre_g, ffn2_w_in, ffn2_w_out, ffn2_post_g, final_g]
    moms = [m_ffn1_pre_g, m_ffn1_w_in, m_ffn1_w_out, m_ffn1_post_g, m_mix_pre_g, m_w_in, m_w_fu, m_b_f, m_gla_norm_g, m_w_pool, m_pool_scale, m_mem_norm_g, m_w_mem_kv, m_w_up_gla, m_w_up_pool, m_w_up_xattn, m_w_o, m_mix_post_g, m_ffn2_pre_g, m_ffn2_w_in, m_ffn2_w_out, m_ffn2_post_g, m_final_g]
    vars_ = [v_ffn1_pre_g, v_ffn1_w_in, v_ffn1_w_out, v_ffn1_post_g, v_mix_pre_g, v_w_in, v_w_fu, v_b_f, v_gla_norm_g, v_w_pool, v_pool_scale, v_mem_norm_g, v_w_mem_kv, v_w_up_gla, v_w_up_pool, v_w_up_xattn, v_w_o, v_mix_post_g, v_ffn2_pre_g, v_ffn2_w_in, v_ffn2_w_out, v_ffn2_post_g, v_final_g]
    P = {n: a[0] if a.ndim > 2 else a for n, a in zip(_NAMES, params)}
    Mo = {n: a[0] if a.ndim > 2 else a for n, a in zip(_NAMES, moms)}
    Vo = {n: a[0] if a.ndim > 2 else a for n, a in zip(_NAMES, vars_)}
    loss, dx, outs = _step(x[0], mem[0], loss_target[0], P, Mo, Vo)
    loss = lax.psum(loss, ("x", "y", "c"))
    out = [loss, dx[None]]
    for kind in range(4):
        for n, p in zip(_NAMES, params):
            out.append(outs[n][kind].reshape(p.shape))
    return tuple(out)
```

```python
import functools

import jax
import jax.numpy as jnp
from jax import lax
from jax.experimental import pallas as pl
from jax.experimental.pallas import tpu as pltpu

F32 = jnp.float32
BF16 = jnp.bfloat16

N_DEV = 8
D_MODEL = 1024
D_FF = 2816
FF_BLK = 2 * D_FF // N_DEV
N_FF_BLK = D_FF // FF_BLK
FF_PAD = 768
CHUNK = 64
GLA_HEADS = 4
GLA_DK = 512
GLA_DV = 1024
GLA_HDK = 128
GLA_HDV = 256
GATE_RANK = 16
GATE_TEMP = 16.0
POOL_WINDOWS = (2, 4, 8, 16)
POOL_W = 512
POOL_GD = 128
POOL_HALO = 16
XA_HEADS = 4
XA_HD = 128
XA_W = 512
EPS = 1e-6
IN_SPLITS = (GLA_DK, GLA_DK, GLA_DV, GLA_DV, GATE_RANK, POOL_W, XA_W, 3 * D_MODEL)
IN_WIDTH = sum(IN_SPLITS)
IN_SHARD = IN_WIDTH // N_DEV
INT_W = 3072 + 3072 + 1024 + 128
INT_NBLK = 3
INT_BLK = INT_W // INT_NBLK
FLOW_W = 128

ADAM_LR = 0.001
ADAM_B1 = 0.9
ADAM_B2 = 0.999
ADAM_EPS = 1e-08
ADAM_WD = 0.01
ADAM_STEP = 10

VMEM_LIMIT = 56 * 1024 * 1024

_NN = (((1,), (0,)), ((), ()))
_NT = (((1,), (1,)), ((), ()))
_TN = (((0,), (0,)), ((), ()))


def _pcall(body, *, name, grid, in_specs, out_specs, out_shape, scratch=()):
    return pl.pallas_call(
        body, name=name, grid=grid, in_specs=in_specs, out_specs=out_specs, out_shape=out_shape,
        scratch_shapes=list(scratch),
        compiler_params=pltpu.CompilerParams(dimension_semantics=("arbitrary",) * len(grid),
                                             vmem_limit_bytes=VMEM_LIMIT))


def _dot(a, b, dims=_NN):
    return lax.dot_general(a.astype(BF16), b.astype(BF16), dims, preferred_element_type=F32)


def _mm(name, a, b, *, grid, a_blk, a_map, b_blk, b_map, o_shape, o_blk, o_map, dims, out_dtype=F32, deps=()):
    nk = grid[2]

    def body(a_ref, b_ref, *rest):
        o_ref, scr = rest[len(deps)], rest[len(deps) + 1:]
        p = _dot(a_ref[...], b_ref[...], dims)
        if nk == 1:
            o_ref[...] = p.astype(o_ref.dtype)
        else:
            acc = scr[0]
            k = pl.program_id(2)

            @pl.when(k == 0)
            def _():
                acc[...] = p

            @pl.when(k > 0)
            def _():
                acc[...] += p

            @pl.when(k == nk - 1)
            def _():
                o_ref[...] = acc[...].astype(o_ref.dtype)

    acc_shape = tuple(d for d in o_blk if d is not None)
    return _pcall(body, name=name, grid=grid,
                  in_specs=[pl.BlockSpec(a_blk, a_map), pl.BlockSpec(b_blk, b_map)]
                  + [pl.BlockSpec(memory_space=pl.ANY)] * len(deps),
                  out_specs=pl.BlockSpec(o_blk, o_map),
                  out_shape=jax.ShapeDtypeStruct(o_shape, out_dtype),
                  scratch=[pltpu.VMEM(acc_shape, F32)] if nk > 1 else [])(a, b, *deps)


def _mm_nn(name, a, b, out_dtype=F32, tm=512, tn=None):
    M, K = a.shape
    N = b.shape[1]
    tm, tn = min(tm, M), (tn or N)
    return _mm(name, a, b, grid=(N // tn, M // tm, 1), a_blk=(tm, K), a_map=lambda j, i, k: (i, 0),
               b_blk=(K, tn), b_map=lambda j, i, k: (0, j), o_shape=(M, N), o_blk=(tm, tn),
               o_map=lambda j, i, k: (i, j), dims=_NN, out_dtype=out_dtype)


def _mm_nt(name, a, b, out_dtype=F32, tm=512):
    M, K = a.shape
    N = b.shape[0]
    tm = min(tm, M)
    return _mm(name, a, b, grid=(1, M // tm, 1), a_blk=(tm, K), a_map=lambda j, i, k: (i, 0),
               b_blk=(N, K), b_map=lambda j, i, k: (0, 0), o_shape=(M, N), o_blk=(tm, N),
               o_map=lambda j, i, k: (i, 0), dims=_NT, out_dtype=out_dtype)


def _mm_tn(name, a, b, out_dtype=BF16, ts=512, tn=None, tn_a=None, deps=()):
    S, M = a.shape
    N = b.shape[1]
    ts, tn, tn_a = min(ts, S), (tn or N), (tn_a or M)
    return _mm(name, a, b, grid=((N // tn) * (M // tn_a), 1, S // ts), a_blk=(ts, tn_a),
               a_map=lambda j, i, k: (k, j if tn_a < M else 0), b_blk=(ts, tn),
               b_map=lambda j, i, k: (k, j if tn < N else 0), o_shape=(M, N), o_blk=(tn_a, tn),
               o_map=lambda j, i, k: (j, 0) if tn_a < M else (0, j), dims=_TN, out_dtype=out_dtype, deps=deps)


class _Win:
    def __init__(self, arr, w, c):
        self.arr, self.w, self.c = arr, w, c


def _row_spec(x, tm):
    if isinstance(x, _Win):
        return x.arr, pl.BlockSpec((tm, x.w), functools.partial(lambda i, c: (i, c), c=x.c))
    if x.ndim == 3:
        return x, pl.BlockSpec((x.shape[0], tm, x.shape[2]), lambda i: (0, i, 0))
    return x, pl.BlockSpec((tm, x.shape[1]), lambda i: (i, 0))


def _rowwise(name, fn, rows, consts, outs, accs=(), tm=256):
    first = rows[0].arr if isinstance(rows[0], _Win) else rows[0]
    S = first.shape[1] if first.ndim == 3 else first.shape[0]
    tm = min(tm, S)
    n_in, n_out = len(rows) + len(consts), len(outs)
    arrays, in_specs = [], []
    for r in rows:
        arr, spec = _row_spec(r, tm)
        arrays.append(arr)
        in_specs.append(spec)
    for c in consts:
        arrays.append(c)
        in_specs.append(pl.BlockSpec(c.shape, functools.partial(lambda i, n: (0,) * n, n=c.ndim)))
    out_specs = [_row_spec(o, tm)[1] for o in outs]
    out_specs += [pl.BlockSpec(a.shape, functools.partial(lambda i, n: (0,) * n, n=len(a.shape))) for a in accs]

    def body(*refs):
        res = fn(*[r[...] for r in refs[:n_in]])
        if not isinstance(res, (tuple, list)):
            res = (res,)
        orefs = refs[n_in:]
        for r, v in zip(orefs[:n_out], res[:n_out]):
            r[...] = v.astype(r.dtype)
        i = pl.program_id(0)
        for r, v in zip(orefs[n_out:], res[n_out:]):
            @pl.when(i == 0)
            def _(r=r, v=v):
                r[...] = v.astype(r.dtype)

            @pl.when(i > 0)
            def _(r=r, v=v):
                r[...] += v.astype(r.dtype)

    res = _pcall(body, name=name, grid=(S // tm,), in_specs=in_specs, out_specs=out_specs,
                 out_shape=[jax.ShapeDtypeStruct(o.shape, o.dtype) for o in list(outs) + list(accs)])(*arrays)
    return res


def _sds(shape, dtype=F32):
    return jax.ShapeDtypeStruct(shape, dtype)


def _rms(x, g):
    return x * lax.rsqrt(jnp.mean(x * x, axis=-1, keepdims=True) + EPS) * g


def _rms_bwd(x, g, dy):
    _, vjp = jax.vjp(_rms, x, g)
    return vjp(dy)


def _sigmoid(x):
    return 0.5 * jnp.tanh(0.5 * x) + 0.5


def _silu(x):
    return x * _sigmoid(x)


def _log_sigmoid(f):
    return jnp.minimum(f, 0.0) - jnp.log(1.0 + jnp.exp(-jnp.abs(f)))


def _head_rms_gate(o, g_out, gn):
    parts = [_rms(o[:, h * GLA_HDV:(h + 1) * GLA_HDV], gn[:, h * GLA_HDV:(h + 1) * GLA_HDV]) for h in range(GLA_HEADS)]
    return jnp.concatenate(parts, axis=-1) * _silu(g_out.astype(F32))


def _merge(gates, ya, yb, yc):
    gates, ya, yb, yc = (t.astype(F32) for t in (gates, ya, yb, yc))
    return (_sigmoid(gates[:, :D_MODEL]) * ya + _sigmoid(gates[:, D_MODEL:2 * D_MODEL]) * yb
            + _sigmoid(gates[:, 2 * D_MODEL:]) * yc)


def _tri_dot(t, x):
    hi = x.astype(BF16)
    r1 = x - hi.astype(F32)
    mid = r1.astype(BF16)
    lo = (r1 - mid.astype(F32)).astype(BF16)
    d = functools.partial(lax.dot_general, dimension_numbers=_NN, preferred_element_type=F32)
    return d(t, hi) + d(t, mid) + d(t, lo)


def _ffn_fwd(tag, x, gpre, wt, gpost, get_wo):
    S = x.shape[0]
    h = _rowwise(tag + "_pre", lambda x, g: _rms(x, g), [x], [gpre], [_sds((S, D_MODEL), BF16)])[0]
    tm = min(512, S)

    def up_body(h_ref, w_ref, u_ref, act_ref):
        hh = h_ref[...]
        for j in range(N_FF_BLK):
            ab = _dot(hh, w_ref[2 * FF_PAD * j:2 * FF_PAD * (j + 1), :], _NT)
            u_ref[:, 2 * FF_PAD * j:2 * FF_PAD * (j + 1)] = ab.astype(BF16)
            act_ref[:, FF_PAD * j:FF_PAD * (j + 1)] = (_silu(ab[:, :FF_PAD]) * ab[:, FF_PAD:]).astype(BF16)

    u, act = _pcall(
        up_body, name=tag + "_up", grid=(S // tm,),
        in_specs=[pl.BlockSpec((tm, D_MODEL), lambda i: (i, 0)),
                  pl.BlockSpec(wt.shape, lambda i: (0, 0), pipeline_mode=pl.Buffered(1))],
        out_specs=[pl.BlockSpec((tm, N_DEV * FF_PAD), lambda i: (i, 0)),
                   pl.BlockSpec((tm, N_FF_BLK * FF_PAD), lambda i: (i, 0))],
        out_shape=[_sds((S, N_DEV * FF_PAD), BF16), _sds((S, N_FF_BLK * FF_PAD), BF16)],
    )(h, wt)

    def down_body(act_ref, w_ref, x_ref, g_ref, f_ref, xo_ref):
        f = _dot(act_ref[...], w_ref[...])
        f_ref[...] = f
        xo_ref[...] = x_ref[...] + 0.5 * _rms(f, g_ref[...])

    wo = get_wo(act)
    row = pl.BlockSpec((tm, D_MODEL), lambda i: (i, 0))
    f, xo = _pcall(
        down_body, name=tag + "_down", grid=(S // tm,),
        in_specs=[pl.BlockSpec((tm, N_FF_BLK * FF_PAD), lambda i: (i, 0)),
                  pl.BlockSpec(wo.shape, lambda i: (0, 0), pipeline_mode=pl.Buffered(1)), row,
                  pl.BlockSpec((1, D_MODEL), lambda i: (0, 0))],
        out_specs=[row, row], out_shape=[_sds((S, D_MODEL)), _sds((S, D_MODEL))],
    )(act, wo, x, gpost)
    return xo, wo, (h, u, act, f)


def _ffn_bwd(tag, dxo, x, gpre, wt, wo, gpost, saved, on_dwo=None, on_dwt=None):
    h, u, act, f = saved
    S = x.shape[0]
    tm = min(256, S)

    def dact_body(f_ref, dxo_ref, g_ref, w_ref, u_ref, df_ref, du_ref, dg_ref):
        i = pl.program_id(0)
        df, dg = _rms_bwd(f_ref[...], g_ref[...], 0.5 * dxo_ref[...])
        df = df.astype(BF16)
        df_ref[...] = df

        @pl.when(i == 0)
        def _():
            dg_ref[...] = dg

        @pl.when(i > 0)
        def _():
            dg_ref[...] += dg

        for j in range(N_FF_BLK):
            dact = _dot(df, w_ref[FF_PAD * j:FF_PAD * (j + 1), :], _NT)
            ab = u_ref[:, 2 * FF_PAD * j:2 * FF_PAD * (j + 1)].astype(F32)
            a, b = ab[:, :FF_PAD], ab[:, FF_PAD:]
            sg = _sigmoid(a)
            du_ref[:, 2 * FF_PAD * j:2 * FF_PAD * j + FF_PAD] = (dact * b * (sg * (1.0 + a * (1.0 - sg)))).astype(BF16)
            du_ref[:, 2 * FF_PAD * j + FF_PAD:2 * FF_PAD * (j + 1)] = (dact * (a * sg)).astype(BF16)

    row = pl.BlockSpec((tm, D_MODEL), lambda i: (i, 0))
    vec = pl.BlockSpec((1, D_MODEL), lambda i: (0, 0))
    u_spec = pl.BlockSpec((tm, N_DEV * FF_PAD), lambda i: (i, 0))
    df, du, dgpost = _pcall(
        dact_body, name=tag + "_dact", grid=(S // tm,),
        in_specs=[row, row, vec, pl.BlockSpec(wo.shape, lambda i: (0, 0), pipeline_mode=pl.Buffered(1)), u_spec],
        out_specs=[row, u_spec, vec],
        out_shape=[_sds((S, D_MODEL), BF16), _sds((S, N_DEV * FF_PAD), BF16), _sds((1, D_MODEL))])(f, dxo, gpost, wo, u)
    dwo = _mm_tn(tag + "_dwo", act, df, ts=2048, tn_a=FF_PAD)
    dwt = _mm_tn(tag + "_dwt", du, h, ts=2048, tn_a=FF_PAD, deps=[on_dwo(dwo)] if on_dwo is not None else [])
    dx, dgpre = _dh_pre_norm_bwd(tag + "_dh", du, wt, x, dxo, gpre, [on_dwt(dwt)] if on_dwt is not None else [])
    return dx, dgpre, dgpost


def _dh_pre_norm_bwd(name, dz, wt, x, dres, g, deps):
    S, K = dz.shape

    def body(dz_ref, w_ref, x_ref, dres_ref, g_ref, *rest):
        dx_ref, dg_ref = rest[len(deps):]
        i = pl.program_id(0)
        dx, dg = _rms_bwd(x_ref[...], g_ref[...], _dot(dz_ref[...], w_ref[...]))
        dx_ref[...] = dx + dres_ref[...]

        @pl.when(i == 0)
        def _():
            dg_ref[...] = dg

        @pl.when(i > 0)
        def _():
            dg_ref[...] += dg

    th = min(256, S)
    row = pl.BlockSpec((th, D_MODEL), lambda i: (i, 0))
    vec = pl.BlockSpec((1, D_MODEL), lambda i: (0, 0))
    return _pcall(
        body, name=name, grid=(S // th,),
        in_specs=[pl.BlockSpec((th, K), lambda i: (i, 0)),
                  pl.BlockSpec(wt.shape, lambda i: (0, 0), pipeline_mode=pl.Buffered(1)), row, row, vec]
        + [pl.BlockSpec(memory_space=pl.ANY)] * len(deps),
        out_specs=[row, vec], out_shape=[_sds((S, D_MODEL)), _sds((1, D_MODEL))])(dz, wt, x, dres, g, *deps)


GLA_G = 4


def _gla_tile_common(k, flow, wfu, bf):
    f = _dot(flow, wfu) + bf
    la = _log_sigmoid(f) * (1.0 / GATE_TEMP)
    tri = _tri_matrix(True)
    ws, ds = [], []
    for g in range(k.shape[0] // CHUNK):
        b = _tri_dot(tri, la[g * CHUNK:(g + 1) * CHUNK])
        b_end = b[CHUNK - 1:CHUNK, :]
        ws.append(jnp.exp(b_end - b))
        ds.append(jnp.exp(b_end))
    w = jnp.concatenate(ws, axis=0)
    return f, w, k * w, ds


def _tri_matrix(lower):
    r = lax.broadcasted_iota(jnp.int32, (CHUNK, CHUNK), 0)
    c = lax.broadcasted_iota(jnp.int32, (CHUNK, CHUNK), 1)
    return jnp.where((r >= c) if lower else (r <= c), 1.0, 0.0).astype(BF16)


def _heads():
    return [(slice(h * GLA_HDK, (h + 1) * GLA_HDK), slice(h * GLA_HDV, (h + 1) * GLA_HDV)) for h in range(GLA_HEADS)]


def _gla_fwd(proj, wfu, bf):
    S = proj.shape[0]
    G = min(GLA_G, S // CHUNK)
    T = G * CHUNK
    nc = S // CHUNK

    def body(q_ref, k_ref, v_ref, flow_ref, wfu_ref, bf_ref, o_ref, st_ref, state):
        @pl.when(pl.program_id(0) == 0)
        def _():
            state[...] = jnp.zeros_like(state)

        _, _, kt, ds = _gla_tile_common(k_ref[...], flow_ref[...], wfu_ref[...], bf_ref[...])
        q = q_ref[...].astype(F32) * (GLA_HDK ** -0.5)
        v = v_ref[...]
        rows = [slice(g * CHUNK, (g + 1) * CHUNK) for g in range(G)]
        kv = [[_dot(v[r, vs], kt[r, ks], _TN) for ks, vs in _heads()] for r in rows]
        st = [state[vs, :] for _, vs in _heads()]
        for g, r in enumerate(rows):
            outs = []
            for h, (ks, vs) in enumerate(_heads()):
                st[h] = st[h] * ds[g][:, ks] + kv[g][h]
                st_ref[g, vs, :] = st[h]
                outs.append(_dot(q[r, ks], st[h], _NT))
            o_ref[r, :] = jnp.concatenate(outs, axis=-1)
        for h, (_, vs) in enumerate(_heads()):
            state[vs, :] = st[h]

    return _pcall(
        body, name="gla_fwd", grid=(S // T,),
        in_specs=[pl.BlockSpec((T, GLA_DK), lambda c: (c, 0)), pl.BlockSpec((T, GLA_DK), lambda c: (c, 1)),
                  pl.BlockSpec((T, GLA_DV), lambda c: (c, 1)),
                  pl.BlockSpec((T, FLOW_W), lambda c: (c, (INT_W - FLOW_W) // FLOW_W)),
                  pl.BlockSpec(wfu.shape, lambda c: (0, 0)), pl.BlockSpec(bf.shape, lambda c: (0, 0))],
        out_specs=[pl.BlockSpec((T, GLA_DV), lambda c: (c, 0)),
                   pl.BlockSpec((G, GLA_DV, GLA_HDK), lambda c: (c, 0, 0))],
        out_shape=[_sds((S, GLA_DV)), _sds((nc, GLA_DV, GLA_HDK))],
        scratch=[pltpu.VMEM((GLA_DV, GLA_HDK), F32)])(proj, proj, proj, proj, wfu, bf)


def _gla_bwd(proj, wfu, bf, states, do):
    S = proj.shape[0]
    G = min(GLA_G, S // CHUNK)
    T = G * CHUNK
    nt = S // T

    def body(q_ref, k_ref, v_ref, flow_ref, wfu_ref, bf_ref, st_ref, stp_ref, do_ref,
             dq_ref, dk_ref, dv_ref, dflow_ref, dwfu_ref, dbf_ref, dstate):
        step = pl.program_id(0)

        @pl.when(step == 0)
        def _():
            dstate[...] = jnp.zeros_like(dstate)
            dwfu_ref[...] = jnp.zeros_like(dwfu_ref)
            dbf_ref[...] = jnp.zeros_like(dbf_ref)

        flow, wfu_v = flow_ref[...], wfu_ref[...]
        f, w, kt, ds = _gla_tile_common(k_ref[...], flow, wfu_v, bf_ref[...])
        q = q_ref[...].astype(F32) * (GLA_HDK ** -0.5)
        v, dout = v_ref[...], do_ref[...]
        rows = [slice(g * CHUNK, (g + 1) * CHUNK) for g in range(G)]
        dq = [jnp.concatenate([_dot(dout[r, vs], st_ref[g, vs, :]) for _, vs in _heads()], axis=-1)
              for g, r in enumerate(rows)]
        qdo = [[_dot(dout[r, vs], q[r, ks], _TN) for ks, vs in _heads()] for r in rows]
        dq_ref[...] = (jnp.concatenate(dq, axis=0) * (GLA_HDK ** -0.5)).astype(dq_ref.dtype)
        has_prev = (step < nt - 1).astype(F32)
        carry = [dstate[vs, :] for _, vs in _heads()]
        dkt, dv, dd = [None] * G, [None] * G, [None] * G
        for g in reversed(range(G)):
            r = rows[g]
            dkts, dvs, dds = [], [], []
            for h, (ks, vs) in enumerate(_heads()):
                dst = carry[h] + qdo[g][h]
                dkts.append(_dot(v[r, vs], dst))
                dvs.append(_dot(kt[r, ks], dst, _NT))
                st_prev = st_ref[g - 1, vs, :] if g > 0 else stp_ref[vs, :] * has_prev
                dds.append(jnp.sum(dst * st_prev, axis=0, keepdims=True))
                carry[h] = dst * ds[g][:, ks]
            dkt[g], dv[g], dd[g] = (jnp.concatenate(t, axis=-1) for t in (dkts, dvs, dds))
        for h, (_, vs) in enumerate(_heads()):
            dstate[vs, :] = carry[h]
        dkt = jnp.concatenate(dkt, axis=0)
        dv_ref[...] = jnp.concatenate(dv, axis=0).astype(dv_ref.dtype)
        dk_ref[...] = (dkt * w).astype(dk_ref.dtype)
        de = dkt * kt
        tri = _tri_matrix(False)
        dla = []
        for g, r in enumerate(rows):
            db_end = jnp.sum(de[r], axis=0, keepdims=True) + dd[g] * ds[g]
            dla.append(db_end - _tri_dot(tri, de[r]))
        df = jnp.concatenate(dla, axis=0) * (1.0 - _sigmoid(f)) * (1.0 / GATE_TEMP)
        dflow_ref[...] = _dot(df, wfu_v, _NT).astype(dflow_ref.dtype)
        dwfu_ref[...] += _dot(flow, df, _TN)
        dbf_ref[...] += jnp.sum(df, axis=0, keepdims=True)

    rt = lambda s: nt - 1 - s
    return _pcall(
        body, name="gla_bwd", grid=(nt,),
        in_specs=[pl.BlockSpec((T, GLA_DK), lambda s: (rt(s), 0)), pl.BlockSpec((T, GLA_DK), lambda s: (rt(s), 1)),
                  pl.BlockSpec((T, GLA_DV), lambda s: (rt(s), 1)),
                  pl.BlockSpec((T, FLOW_W), lambda s: (rt(s), (INT_W - FLOW_W) // FLOW_W)),
                  pl.BlockSpec(wfu.shape, lambda s: (0, 0)), pl.BlockSpec(bf.shape, lambda s: (0, 0)),
                  pl.BlockSpec((G, GLA_DV, GLA_HDK), lambda s: (rt(s), 0, 0)),
                  pl.BlockSpec((None, GLA_DV, GLA_HDK), lambda s: (jnp.maximum(rt(s) * G - 1, 0), 0, 0)),
                  pl.BlockSpec((T, GLA_DV), lambda s: (rt(s), 0))],
        out_specs=[pl.BlockSpec((T, GLA_DK), lambda s: (rt(s), 0)), pl.BlockSpec((T, GLA_DK), lambda s: (rt(s), 0)),
                   pl.BlockSpec((T, GLA_DV), lambda s: (rt(s), 0)), pl.BlockSpec((T, FLOW_W), lambda s: (rt(s), 0)),
                   pl.BlockSpec(wfu.shape, lambda s: (0, 0)), pl.BlockSpec(bf.shape, lambda s: (0, 0))],
        out_shape=[_sds((S, GLA_DK), BF16), _sds((S, GLA_DK), BF16), _sds((S, GLA_DV), BF16), _sds((S, FLOW_W), BF16),
                   _sds(wfu.shape), _sds(bf.shape)],
        scratch=[pltpu.VMEM((GLA_DV, GLA_HDK), F32)])(proj, proj, proj, proj, wfu, bf, states, states, do)


def _pool_counts(tm, i):
    t = (lax.broadcasted_iota(jnp.int32, (tm, POOL_GD), 0) + i * tm + 1).astype(F32)
    return [jnp.minimum(t, float(w)) for w in POOL_WINDOWS]


def _pool_fwd(proj, w_pool, pool_scale, tm=256):
    S = proj.shape[0]
    tm = min(tm, S // 2)
    col = (3072 + 3072) // POOL_W
    hb = tm // POOL_HALO

    def body(p_ref, halo_ref, wp_ref, sc_ref, mixed_ref, out_ref):
        i = pl.program_id(0)
        p = p_ref[...].astype(F32)
        halo = halo_ref[...].astype(F32) * (i > 0).astype(F32)
        ext = jnp.concatenate([halo, p], axis=0)
        n = tm + POOL_HALO
        sums, acc, k = {}, ext, 1
        while k < POOL_WINDOWS[-1]:
            acc = acc + pltpu.roll(acc, k, axis=0)
            k *= 2
            sums[k] = acc
        cnts = _pool_counts(tm, i)
        mixed, lin = [], []
        for g, w in enumerate(POOL_WINDOWS):
            ls = slice(g * POOL_GD, (g + 1) * POOL_GD)
            m = sums[w][POOL_HALO:n, ls] / cnts[g] - p[:, ls]
            mixed.append(m)
            lin.append(_dot(m, wp_ref[g]))
        mixed_ref[...] = jnp.concatenate(mixed, axis=-1)
        out_ref[...] = (jnp.concatenate(lin, axis=-1) * sc_ref[...]).astype(out_ref.dtype)

    return _pcall(
        body, name="pool_fwd", grid=(S // tm,),
        in_specs=[pl.BlockSpec((tm, POOL_W), lambda i: (i, col)),
                  pl.BlockSpec((POOL_HALO, POOL_W), lambda i: (jnp.maximum(i * hb - 1, 0), col)),
                  pl.BlockSpec(w_pool.shape, lambda i: (0, 0, 0)), pl.BlockSpec(pool_scale.shape, lambda i: (0, 0))],
        out_specs=[pl.BlockSpec((tm, POOL_W), lambda i: (i, 0)), pl.BlockSpec((tm, POOL_W), lambda i: (i, 0))],
        out_shape=[_sds((S, POOL_W)), _sds((S, POOL_W), BF16)])(proj, proj, w_pool, pool_scale)


def _pool_lin_bwd(dout, mixed, w_pool, pool_scale):
    S = dout.shape[0]

    def fn(dout, mixed, wp, sc):
        dlin = dout * sc
        dm, dwp, lin = [], [], []
        for g in range(len(POOL_WINDOWS)):
            ls = slice(g * POOL_GD, (g + 1) * POOL_GD)
            lin.append(_dot(mixed[:, ls], wp[g]))
            dm.append(_dot(dlin[:, ls], wp[g], _NT))
            dwp.append(_dot(mixed[:, ls], dlin[:, ls], _TN))
        dsc = jnp.sum(dout * jnp.concatenate(lin, axis=-1), axis=0, keepdims=True)
        return jnp.concatenate(dm, axis=-1), jnp.concatenate(dwp, axis=0), dsc

    return _rowwise("pool_lin_bwd", fn, [dout, mixed], [w_pool, pool_scale], [_sds((S, POOL_W))],
                    [_sds((len(POOL_WINDOWS) * POOL_GD, POOL_GD)), _sds((1, POOL_W))])


def _pool_win_bwd(dmixed, tm=256):
    S = dmixed.shape[0]
    tm = min(tm, S // 2)
    nt = S // tm
    hb = tm // POOL_HALO

    def body(dm_ref, halo_ref, dp_ref):
        i = pl.program_id(0)
        dm = dm_ref[...]
        halo = halo_ref[...] * (i < nt - 1).astype(F32)
        cnts = _pool_counts(tm, i)
        cnts_h = [c[:POOL_HALO] for c in _pool_counts(tm, i + 1)]
        r = jnp.concatenate([jnp.concatenate([dm[:, g * POOL_GD:(g + 1) * POOL_GD] / cnts[g] for g in range(4)], axis=-1),
                             jnp.concatenate([halo[:, g * POOL_GD:(g + 1) * POOL_GD] / cnts_h[g] for g in range(4)], axis=-1)],
                            axis=0)
        n = tm + POOL_HALO
        sums, acc, k = {}, r, 1
        while k < POOL_WINDOWS[-1]:
            acc = acc + pltpu.roll(acc, n - k, axis=0)
            k *= 2
            sums[k] = acc
        dp = [sums[w][:tm, g * POOL_GD:(g + 1) * POOL_GD] for g, w in enumerate(POOL_WINDOWS)]
        dp_ref[...] = (jnp.concatenate(dp, axis=-1) - dm).astype(dp_ref.dtype)

    return _pcall(
        body, name="pool_win_bwd", grid=(nt,),
        in_specs=[pl.BlockSpec((tm, POOL_W), lambda i: (i, 0)),
                  pl.BlockSpec((POOL_HALO, POOL_W), lambda i: (jnp.minimum((i + 1) * hb, S // POOL_HALO - 1), 0))],
        out_specs=pl.BlockSpec((tm, POOL_W), lambda i: (i, 0)),
        out_shape=_sds((S, POOL_W), BF16))(dmixed, dmixed)


def _xattn_probs(q, kv, h):
    hs = slice(h * XA_HD, (h + 1) * XA_HD)
    s = _dot(q[:, hs], kv[:, hs], _NT) * (XA_HD ** -0.5)
    s = s - jnp.max(s, axis=-1, keepdims=True)
    e = jnp.exp(s)
    return e / jnp.sum(e, axis=-1, keepdims=True)


def _xattn_fwd(proj, kv):
    S = proj.shape[0]

    def fn(q, kv):
        outs = []
        for h in range(XA_HEADS):
            p = _xattn_probs(q, kv, h)
            outs.append(_dot(p, kv[:, XA_W + h * XA_HD:XA_W + (h + 1) * XA_HD]))
        return jnp.concatenate(outs, axis=-1)

    return _rowwise("xattn_fwd", fn, [_Win(proj, XA_W, (3072 + 3072 + POOL_W) // XA_W)], [kv], [_sds((S, XA_W), BF16)])[0]


def _xattn_bwd(proj, kv, dxa):
    S = proj.shape[0]

    def fn(q, dxa, kv):
        dqs, dks, dvs = [], [], []
        for h in range(XA_HEADS):
            hs = slice(h * XA_HD, (h + 1) * XA_HD)
            vh = kv[:, XA_W + h * XA_HD:XA_W + (h + 1) * XA_HD]
            p = _xattn_probs(q, kv, h)
            dp = _dot(dxa[:, hs], vh, _NT)
            ds = p * (dp - jnp.sum(p * dp, axis=-1, keepdims=True)) * (XA_HD ** -0.5)
            dqs.append(_dot(ds, kv[:, hs]))
            dks.append(_dot(ds, q[:, hs], _TN))
            dvs.append(_dot(p, dxa[:, hs], _TN))
        return jnp.concatenate(dqs, axis=-1), jnp.concatenate(dks + dvs, axis=-1)

    return _rowwise("xattn_bwd", fn, [_Win(proj, XA_W, (3072 + 3072 + POOL_W) // XA_W), dxa], [kv],
                    [_sds((S, XA_W), BF16)], [_sds(kv.shape)])


def _mix_fwd(x1, mem, W):
    S = x1.shape[0]
    M = mem.shape[0]
    h = _rowwise("mix_pre", lambda x, g: _rms(x, g), [x1], [W["mix_pre_g"]], [_sds((S, D_MODEL), BF16)])[0]
    tm = min(512, S)
    proj = _mm("mix_proj", h, W["w_int_t"], grid=(INT_NBLK, S // tm, 1), a_blk=(tm, D_MODEL), a_map=lambda j, i, k: (i, 0),
               b_blk=(INT_BLK, D_MODEL), b_map=lambda j, i, k: (j, 0), o_shape=(S, INT_W), o_blk=(tm, INT_BLK),
               o_map=lambda j, i, k: (i, j), dims=_NT, out_dtype=BF16)
    o_raw, states = _gla_fwd(proj, W["w_fu_pad"], W["b_f"])
    ya_in = _rowwise("gla_out", _head_rms_gate, [o_raw, _Win(proj, GLA_DV, 2)], [W["gla_norm_g"]],
                     [_sds((S, GLA_DV), BF16)])[0]
    mixed, pool_out = _pool_fwd(proj, W["w_pool"], W["pool_scale"])
    mem_n = _rowwise("mem_norm", lambda m, g: _rms(m, g), [mem], [W["mem_norm_g"]], [_sds((M, D_MODEL), BF16)])[0]
    kv = _mm_nn("mem_kv", mem_n, W["w_mem_kv"])
    xa = _xattn_fwd(proj, kv)
    ya = _mm_nn("up_gla", ya_in, W["w_up_gla"], BF16)
    yb = _mm_nn("up_pool", pool_out, W["w_up_pool"], BF16)
    yc = _mm_nn("up_xattn", xa, W["w_up_xattn"], BF16)
    merged = _rowwise("merge", _merge, [_Win(proj, 3 * D_MODEL, 1), ya, yb, yc], [], [_sds((S, D_MODEL), BF16)])[0]
    y = _mm_nn("mix_out", merged, W["w_o"])
    x2 = _rowwise("mix_post", lambda x, y, g: x + _rms(y, g), [x1, y], [W["mix_post_g"]], [_sds((S, D_MODEL))])[0]
    return x2, (h, proj, o_raw, states, ya_in, mixed, pool_out, mem_n, kv, xa, ya, yb, yc, merged, y)


def _head(x3, tgt, final_g):
    S = x3.shape[0]

    def head(x3, t, g):
        out, vjp = jax.vjp(_rms, x3, g)
        e = out - t
        loss = 0.5 * jnp.sum(jnp.mean(e * e, axis=-1, keepdims=True), axis=0, keepdims=True)
        dx3, dg = vjp(e * (1.0 / D_MODEL))
        return dx3, jnp.broadcast_to(loss, (1, 128)), dg

    dx3, loss_acc, d_final_g = _rowwise("head", head, [x3, tgt], [final_g], [_sds((S, D_MODEL))],
                                        [_sds((1, 128)), _sds((1, D_MODEL))])
    return loss_acc[0, 0], dx3, d_final_g


def _mix_bwd(dx2, x1, mem, W, saved, on_grads=None):
    h, proj, o_raw, states, ya_in, mixed, pool_out, mem_n, kv, xa, ya, yb, yc, merged, y = saved
    S = x1.shape[0]

    def mix_post_bwd(y, dx2, g):
        return _rms_bwd(y, g, dx2)

    dy, d_mix_post_g = _rowwise("mix_post_bwd", mix_post_bwd, [y, dx2], [W["mix_post_g"]], [_sds((S, D_MODEL), BF16)],
                                [_sds((1, D_MODEL))])
    d_w_o = _mm_tn("d_w_o", merged, dy)
    dmerged = _mm_nt("d_merged", dy, W["w_o"], BF16)

    def merge_bwd(gates, ya, yb, yc, dm):
        _, vjp = jax.vjp(_merge, gates, ya, yb, yc)
        return vjp(dm.astype(F32))

    dgates, dya, dyb, dyc = _rowwise("merge_bwd", merge_bwd, [_Win(proj, 3 * D_MODEL, 1), ya, yb, yc, dmerged], [],
                                     [_sds((S, 3 * D_MODEL), BF16)] + [_sds((S, D_MODEL), BF16)] * 3)
    d_w_up_gla = _mm_tn("d_w_up_gla", ya_in, dya)
    d_w_up_pool = _mm_tn("d_w_up_pool", pool_out, dyb)
    d_w_up_xattn = _mm_tn("d_w_up_xattn", xa, dyc)
    d_ya_in = _mm_nt("d_ya_in", dya, W["w_up_gla"], BF16)
    d_pool_out = _mm_nt("d_pool_out", dyb, W["w_up_pool"], BF16)
    d_xa = _mm_nt("d_xa", dyc, W["w_up_xattn"], BF16)

    def gla_out_bwd(o, g_out, d, gn):
        _, vjp = jax.vjp(_head_rms_gate, o, g_out, gn)
        return vjp(d.astype(F32))

    do_raw, dg_out, d_gla_norm_g = _rowwise("gla_out_bwd", gla_out_bwd, [o_raw, _Win(proj, GLA_DV, 2), d_ya_in],
                                            [W["gla_norm_g"]], [_sds((S, GLA_DV)), _sds((S, GLA_DV), BF16)],
                                            [_sds((1, GLA_DV))])
    dq, dk, dv, dflow, d_wfu_pad, d_b_f = _gla_bwd(proj, W["w_fu_pad"], W["b_f"], states, do_raw)
    dmixed, d_w_pool, d_pool_scale = _pool_lin_bwd(d_pool_out, mixed, W["w_pool"], W["pool_scale"])
    dp_in = _pool_win_bwd(dmixed)
    dxq, dkv = _xattn_bwd(proj, kv, d_xa)
    d_w_mem_kv = _mm_tn("d_w_mem_kv", mem_n, dkv)
    dmem_n = _mm_nt("d_mem_n", dkv, W["w_mem_kv"])
    d_mem_norm_g = _rowwise("mem_norm_bwd", lambda m, d, g: _rms_bwd(m, g, d)[1], [mem, dmem_n], [W["mem_norm_g"]], [],
                            [_sds((1, D_MODEL))])[0]
    dproj = jnp.concatenate([dq, dk, dv, dg_out, dgates, dp_in, dxq, dflow], axis=-1)
    d_w_int_t = _mm_tn("d_w_int", dproj, h, ts=1024, tn_a=INT_BLK)
    grads = dict(
        w_int_t=d_w_int_t, w_fu_pad=d_wfu_pad, b_f=d_b_f, gla_norm_g=d_gla_norm_g, w_pool=d_w_pool,
        pool_scale=d_pool_scale, mem_norm_g=d_mem_norm_g, w_mem_kv=d_w_mem_kv, w_up_gla=d_w_up_gla,
        w_up_pool=d_w_up_pool, w_up_xattn=d_w_up_xattn, w_o=d_w_o, mix_post_g=d_mix_post_g)
    deps = [on_grads(grads)] if on_grads is not None else []
    dx1, grads["mix_pre_g"] = _dh_pre_norm_bwd("d_mix_h", dproj, W["w_int_t"], x1, dx2, W["mix_pre_g"], deps)
    return dx1, grads


def _mesh_pos():
    x, y, c = lax.axis_index("x"), lax.axis_index("y"), lax.axis_index("c")
    return x, y, c, 4 * x + 2 * y + c


def _peer(x, y, c, r):
    px = 1 - x if r & 4 else x
    py = 1 - y if r & 2 else y
    pc = 1 - c if r & 1 else c
    return (px, py, pc), 4 * px + 2 * py + pc


_ALL_PEERS = tuple(range(1, N_DEV))
_SIBLING = 1
_SAME_CORE = (2, 4, 6)


def _dev_slot(ref, dev):
    return ref.at[dev]


class _Plan:
    def __init__(self, scatter, slots=None, shapes=None):
        self.scatter, self.slots, self.shapes = scatter, slots or {}, shapes or {}

    def src(self, srcs, a, dev):
        return self.slots.get(a, _dev_slot)(srcs[a], dev) if self.scatter else srcs[a]

    def dst(self, lands, a, dev):
        return lands[a].at[dev] if self.scatter else self.slots.get(a, _dev_slot)(lands[a], dev)

    def landing_zones(self, arrays):
        lands = []
        for a, arr in enumerate(arrays):
            if self.scatter:
                lands.append(lax.empty((N_DEV,) + tuple(self.shapes.get(a, arr.shape[1:])), arr.dtype))
            elif a in self.shapes:
                lands.append(jnp.zeros(self.shapes[a], arr.dtype))
            else:
                lands.append(lax.empty((N_DEV,) + arr.shape, arr.dtype))
        return lands


_GATHER, _SCATTER = _Plan(False), _Plan(True)


def _peer_copies(srcs, lands, send_sems, recv_sems, plan, peers=_ALL_PEERS):
    x, y, c, me = _mesh_pos()
    cps = []
    for r in peers:
        pos, peer = _peer(x, y, c, r)
        for a in range(len(srcs)):
            k = a * (N_DEV - 1) + r - 1
            cps.append(pltpu.make_async_remote_copy(
                src_ref=plan.src(srcs, a, peer), dst_ref=plan.dst(lands, a, me),
                send_sem=send_sems.at[k], recv_sem=recv_sems.at[k], device_id=pos,
                device_id_type=pl.DeviceIdType.MESH))
    return cps


def _exchange(name, arrays, plan, after):
    n = len(arrays)

    def body(*refs):
        ins, outs = refs[:n], refs[n + len(after):2 * n + len(after)]
        send_sems, recv_sems, local_sems = refs[2 * n + len(after):]
        copies = _own_copies(ins, outs, local_sems, plan) + _peer_copies(ins, outs, send_sems, recv_sems, plan)
        for cp in copies:
            cp.start()
        for cp in copies:
            cp.wait()

    any_spec = pl.BlockSpec(memory_space=pl.ANY)
    out_shape = [_sds(l.shape, l.dtype) for l in plan.landing_zones(arrays)]
    return pl.pallas_call(
        body, name=name, in_specs=[any_spec] * (n + len(after)), out_specs=[any_spec] * n, out_shape=out_shape,
        scratch_shapes=[pltpu.SemaphoreType.DMA((n * (N_DEV - 1),)), pltpu.SemaphoreType.DMA((n * (N_DEV - 1),)),
                        pltpu.SemaphoreType.DMA((n,))])(*arrays, *after)


_HBM = pl.BlockSpec(memory_space=pltpu.HBM)
_SEM = pl.BlockSpec(memory_space=pltpu.SEMAPHORE)
_EFFECT = pltpu.SideEffectType.DATAFLOW_SIDE_EFFECTING


def _own_copies(srcs, lands, own_sems, plan):
    me = _mesh_pos()[3]
    return [pltpu.make_async_copy(plan.src(srcs, a, me), plan.dst(lands, a, me), own_sems.at[a])
            for a in range(len(srcs))]


def _exchange_start(name, arrays, plan):
    n = len(arrays)
    lands = plan.landing_zones(arrays)
    n_sem = n * (N_DEV - 1)

    def body(*refs):
        srcs, lands_ = refs[:n], refs[n:2 * n]
        send_sems, recv_sems, own_sems = refs[2 * n:2 * n + 3]
        token = refs[-1]
        for cp in _peer_copies(srcs, lands_, send_sems, recv_sems, plan) + _own_copies(srcs, lands_, own_sems, plan):
            cp.start()
        token[...] = jnp.zeros_like(token)

    hbm = lambda a: pltpu.HBM(a.shape, a.dtype)
    res = pl.pallas_call(
        body, name=name,
        out_shape=(pltpu.SemaphoreType.DMA((n_sem,)), pltpu.SemaphoreType.DMA((n_sem,)), pltpu.SemaphoreType.DMA((n,)),
                   *[hbm(a) for a in arrays], *[hbm(a) for a in lands], _sds((8, 128))),
        in_specs=[_HBM] * (2 * n),
        out_specs=(_SEM, _SEM, _SEM, *[_HBM] * (2 * n), pl.BlockSpec(memory_space=pltpu.VMEM)),
        input_output_aliases={i: 3 + i for i in range(2 * n)},
        compiler_params=pltpu.CompilerParams(has_side_effects=_EFFECT),
    )(*[pltpu.with_memory_space_constraint(a, pltpu.HBM) for a in list(arrays) + lands])
    return (res[:3], res[3:3 + n], res[3 + n:3 + 2 * n], plan), res[-1]


def _exchange_wait(name, state, after):
    sems, srcs, lands, plan = state
    n = len(srcs)

    def body(*refs):
        srcs_, lands_ = refs[:n], refs[n:2 * n]
        send_sems, recv_sems, own_sems = refs[2 * n:2 * n + 3]
        for cp in _peer_copies(srcs_, lands_, send_sems, recv_sems, plan):
            cp.wait_send()
            cp.wait_recv()
        for cp in _own_copies(srcs_, lands_, own_sems, plan):
            cp.wait()

    hbm = lambda a: pltpu.HBM(a.shape, a.dtype)
    res = pl.pallas_call(
        body, name=name, out_shape=tuple(hbm(a) for a in list(srcs) + list(lands)),
        in_specs=[_HBM] * (2 * n) + [_SEM] * 3 + [pl.BlockSpec(memory_space=pl.ANY)] * len(after),
        out_specs=tuple([_HBM] * (2 * n)), input_output_aliases={i: i for i in range(2 * n)},
        compiler_params=pltpu.CompilerParams(has_side_effects=_EFFECT),
    )(*srcs, *lands, *sems, *after)
    return res[n:]


def _gather_start(name, arrays, plan, after):
    n = len(arrays)
    lands = plan.landing_zones(arrays)
    n_sem = n * (N_DEV - 1)

    def body(*refs):
        srcs, lands_ = refs[:n], refs[n:2 * n]
        send_sems, recv_sems, own_sems = refs[2 * n + len(after):2 * n + len(after) + 3]
        token = refs[-1]
        for cp in (_peer_copies(srcs, lands_, send_sems, recv_sems, plan, (_SIBLING,) + _SAME_CORE)
                   + _own_copies(srcs, lands_, own_sems, plan)):
            cp.start()
        token[...] = jnp.zeros_like(token)

    hbm = lambda a: pltpu.HBM(a.shape, a.dtype)
    res = pl.pallas_call(
        body, name=name,
        out_shape=(pltpu.SemaphoreType.DMA((n_sem,)), pltpu.SemaphoreType.DMA((n_sem,)), pltpu.SemaphoreType.DMA((n,)),
                   *[hbm(a) for a in arrays], *[hbm(a) for a in lands], _sds((8, 128))),
        in_specs=[_HBM] * (2 * n) + [pl.BlockSpec(memory_space=pl.ANY)] * len(after),
        out_specs=(_SEM, _SEM, _SEM, *[_HBM] * (2 * n), pl.BlockSpec(memory_space=pltpu.VMEM)),
        input_output_aliases={i: 3 + i for i in range(2 * n)},
        compiler_params=pltpu.CompilerParams(has_side_effects=_EFFECT),
    )(*[pltpu.with_memory_space_constraint(a, pltpu.HBM) for a in list(arrays) + lands], *after)
    return (res[:3], res[3:3 + n], res[3 + n:3 + 2 * n], plan), res[-1]


def _pass_on_copies(lands, send_sems, recv_sems, plan):
    x, y, c, _ = _mesh_pos()
    sibling = _peer(x, y, c, _SIBLING)[0]
    cps = []
    for i, r in enumerate(_SAME_CORE):
        owner = _peer(x, y, c, r)[1]
        for a in range(len(lands)):
            k = a * len(_SAME_CORE) + i
            cps.append(pltpu.make_async_remote_copy(
                src_ref=plan.dst(lands, a, owner), dst_ref=plan.dst(lands, a, owner), send_sem=send_sems.at[k],
                recv_sem=recv_sems.at[k], device_id=sibling, device_id_type=pl.DeviceIdType.MESH))
    return cps


def _gather_pass_on(name, state, after):
    sems, srcs, lands, plan = state
    n = len(srcs)
    n_sem = n * len(_SAME_CORE)

    def body(*refs):
        srcs_, lands_ = refs[:n], refs[n:2 * n]
        send_sems, recv_sems = refs[2 * n], refs[2 * n + 1]
        on_send, on_recv = refs[2 * n + 3 + len(after)], refs[2 * n + 4 + len(after)]
        arrivals = _peer_copies(srcs_, lands_, send_sems, recv_sems, plan, _SAME_CORE)
        for arrived, on in zip(arrivals, _pass_on_copies(lands_, on_send, on_recv, plan)):
            arrived.wait_recv()
            on.start()

    hbm = lambda a: pltpu.HBM(a.shape, a.dtype)
    res = pl.pallas_call(
        body, name=name,
        out_shape=(pltpu.SemaphoreType.DMA((n_sem,)), pltpu.SemaphoreType.DMA((n_sem,)),
                   *[hbm(a) for a in list(srcs) + list(lands)]),
        in_specs=[_HBM] * (2 * n) + [_SEM] * 3 + [pl.BlockSpec(memory_space=pl.ANY)] * len(after),
        out_specs=(_SEM, _SEM, *[_HBM] * (2 * n)), input_output_aliases={i: 2 + i for i in range(2 * n)},
        compiler_params=pltpu.CompilerParams(has_side_effects=_EFFECT),
    )(*srcs, *lands, *sems, *after)
    return sems, res[:2], res[2:2 + n], res[2 + n:], plan


def _gather_wait(name, state, after):
    sems, on_sems, srcs, lands, plan = state
    n = len(srcs)

    def body(*refs):
        srcs_, lands_ = refs[:n], refs[n:2 * n]
        send_sems, recv_sems, own_sems, on_send, on_recv = refs[2 * n:2 * n + 5]
        for cp in _peer_copies(srcs_, lands_, send_sems, recv_sems, plan, (_SIBLING,)):
            cp.wait_recv()
        for cp in _peer_copies(srcs_, lands_, send_sems, recv_sems, plan, (_SIBLING,) + _SAME_CORE):
            cp.wait_send()
        for cp in _own_copies(srcs_, lands_, own_sems, plan):
            cp.wait()
        for cp in _pass_on_copies(lands_, on_send, on_recv, plan):
            cp.wait_send()
            cp.wait_recv()

    hbm = lambda a: pltpu.HBM(a.shape, a.dtype)
    res = pl.pallas_call(
        body, name=name, out_shape=tuple(hbm(a) for a in list(srcs) + list(lands)),
        in_specs=[_HBM] * (2 * n) + [_SEM] * 5 + [pl.BlockSpec(memory_space=pl.ANY)] * len(after),
        out_specs=tuple([_HBM] * (2 * n)), input_output_aliases={i: i for i in range(2 * n)},
        compiler_params=pltpu.CompilerParams(has_side_effects=_EFFECT),
    )(*srcs, *lands, *sems, *on_sems, *after)
    return res[n:]


def _adam(name, recv, w, m, v):
    shape = w.shape
    R, C = shape
    tr = R
    while N_DEV * tr * C * 4 > 6 * 1024 * 1024 and tr % 32 == 0:
        tr //= 2

    def body(recv_ref, w_ref, m_ref, v_ref, g_ref, d_ref, m2_ref, v2_ref):
        g = recv_ref[0].astype(F32)
        for j in range(1, N_DEV):
            g = g + recv_ref[j].astype(F32)
        w_, m_, v_ = w_ref[...], m_ref[...], v_ref[...]
        m2 = ADAM_B1 * m_ + (1.0 - ADAM_B1) * g
        v2 = ADAM_B2 * v_ + (1.0 - ADAM_B2) * (g * g)
        m_hat = m2 / (1.0 - ADAM_B1 ** ADAM_STEP)
        v_hat = v2 / (1.0 - ADAM_B2 ** ADAM_STEP)
        g_ref[...] = g
        d_ref[...] = -ADAM_LR * (m_hat / (jnp.sqrt(v_hat) + ADAM_EPS) + ADAM_WD * w_)
        m2_ref[...] = m2
        v2_ref[...] = v2

    blk = pl.BlockSpec((tr, C), lambda i: (i, 0))
    return _pcall(body, name=name, grid=(R // tr,),
                  in_specs=[pl.BlockSpec((N_DEV, tr, C), lambda i: (0, i, 0)), blk, blk, blk],
                  out_specs=[blk] * 4, out_shape=[_sds(shape)] * 4)(recv, w, m, v)


_NAMES = ['ffn1_pre_g', 'ffn1_w_in', 'ffn1_w_out', 'ffn1_post_g', 'mix_pre_g', 'w_in', 'w_fu', 'b_f', 'gla_norm_g',
          'w_pool', 'pool_scale', 'mem_norm_g', 'w_mem_kv', 'w_up_gla', 'w_up_pool', 'w_up_xattn', 'w_o', 'mix_post_g',
          'ffn2_pre_g', 'ffn2_w_in', 'ffn2_w_out', 'ffn2_post_g', 'final_g']
_SHARDED = ['ffn1_w_in', 'ffn1_w_out', 'w_in', 'w_fu', 'w_mem_kv', 'w_up_gla', 'w_up_pool', 'w_up_xattn', 'w_o',
            'ffn2_w_in', 'ffn2_w_out']
_COL_SHARDED = ['w_up_pool', 'w_up_xattn']


def _cols_to_full(g):
    return jnp.transpose(g, (1, 0, 2)).reshape(g.shape[1], N_DEV * g.shape[2])


def _full_to_cols(f):
    R, C = f.shape
    return jnp.transpose(f.reshape(R, N_DEV, C // N_DEV), (1, 0, 2))


def _to_internal(w_in_t):
    o = 0
    parts = []
    for s in IN_SPLITS:
        parts.append(w_in_t[o:o + s])
        o += s
    q, k, v, g_out, f_low, p_in, xq, gates = parts
    f_low = jnp.pad(f_low, ((0, FLOW_W - GATE_RANK), (0, 0)))
    return jnp.concatenate([q, k, v, g_out, gates, p_in, xq, f_low], axis=0)


def _from_internal(d):
    q, k, v, g_out = d[0:512], d[512:1024], d[1024:2048], d[2048:3072]
    gates, p_in, xq = d[3072:6144], d[6144:6656], d[6656:7168]
    f_low = d[7168:7168 + GATE_RANK]
    return jnp.concatenate([q, k, v, g_out, f_low, p_in, xq, gates], axis=0)


_FFN_IN = ('ffn1_w_in', 'ffn2_w_in')
_FFN_OUT = ('ffn1_w_out', 'ffn2_w_out')
_GATHERS = {"ffn1_in": ['ffn1_w_in'], "ffn1_out": ['ffn1_w_out'],
            "mix": ['w_in', 'w_fu', 'w_mem_kv', 'w_up_gla', 'w_up_pool', 'w_up_xattn', 'w_o'],
            "ffn2": ['ffn2_w_in', 'ffn2_w_out']}


def _ffn_in_slot(ref, d):
    return ref.at[2 * (d % N_FF_BLK) + d // N_FF_BLK, pl.ds(0, FF_BLK)]


def _ffn_out_slot(ref, d):
    rows = FF_BLK // 2
    return ref.at[d // 2, pl.ds(pl.multiple_of((d % 2) * rows, rows), rows)]


def _ffn_plan(names, scatter):
    slots, shapes = {}, {}
    for a, n in enumerate(names):
        if n in _FFN_IN:
            slots[a] = _ffn_in_slot
            shapes[a] = (FF_BLK, D_MODEL) if scatter else (N_DEV, FF_PAD, D_MODEL)
        elif n in _FFN_OUT:
            slots[a] = _ffn_out_slot
            shapes[a] = (FF_BLK // 2, D_MODEL) if scatter else (N_FF_BLK, FF_PAD, D_MODEL)
    return _Plan(scatter, slots, shapes)


def _rows_to_full(g):
    return g.reshape(N_DEV * g.shape[1], g.shape[2])


def _mix_weights(gathered):
    W = {n: _rows_to_full(gathered[n]) for n in ('w_mem_kv', 'w_up_gla', 'w_o')}
    for n in _COL_SHARDED:
        W[n] = _cols_to_full(gathered[n])
    W["w_int_t"] = _to_internal(gathered["w_in"].reshape(IN_WIDTH, D_MODEL))
    W["w_fu_pad"] = jnp.pad(_cols_to_full(gathered["w_fu"]), ((0, FLOW_W - GATE_RANK), (0, 0)))
    return W


def _mix_chunks(G, n):
    if n == "w_in":
        return _from_internal(G["w_int_t"]).reshape(N_DEV, IN_SHARD, D_MODEL)
    if n == "w_fu":
        return _full_to_cols(G["w_fu_pad"][:GATE_RANK].astype(BF16))
    if n in _COL_SHARDED:
        return _full_to_cols(G[n])
    return G[n].reshape(N_DEV, G[n].shape[0] // N_DEV, G[n].shape[1])


def _step(x, mem, tgt, P, Mo, Vo):
    def native(n, a):
        return jnp.swapaxes(a, 0, 1) if n in _FFN_IN + ("w_in",) else a

    P, Mo, Vo = ({n: native(n, a) for n, a in d.items()} for d in (P, Mo, Vo))
    small = {n: P[n] for n in _NAMES if n not in _SHARDED}

    gather, token = {}, []
    for grp, names in _GATHERS.items():
        gather[grp], tok = _gather_start("gather_" + grp, [P[n].astype(BF16) for n in names], _ffn_plan(names, False),
                                         token)
        token = [tok]

    def gathered(grp, after):
        return _gather_wait("gather_" + grp + "_wait", _gather_pass_on("gather_" + grp + "_on", gather[grp], after), [])

    def ffn_wo(lands):
        return lambda act: lands[0].reshape(N_FF_BLK * FF_PAD, D_MODEL)

    w1t = gathered("ffn1_in", token)[0].reshape(N_DEV * FF_PAD, D_MODEL)
    x1, w1o, sv1 = _ffn_fwd("ffn1", x, small["ffn1_pre_g"], w1t, small["ffn1_post_g"],
                            lambda act: ffn_wo(gathered("ffn1_out", [act]))(act))

    gm = dict(zip(_GATHERS["mix"], gathered("mix", [x1])))
    Wm = {**small, **_mix_weights(gm)}
    x2, svm = _mix_fwd(x1, mem, Wm)

    w2t, w2o = gathered("ffn2", [x2])
    w2t = w2t.reshape(N_DEV * FF_PAD, D_MODEL)
    x3, w2o, sv2 = _ffn_fwd("ffn2", x2, small["ffn2_pre_g"], w2t, small["ffn2_post_g"], ffn_wo([w2o]))

    loss, dx3, d_final_g = _head(x3, tgt, small["final_g"])

    G = dict(final_g=d_final_g)
    scat = {}

    def start(grp, names, arrays):
        scat[grp] = names, _exchange_start("scatter_" + grp, arrays, _ffn_plan(names, True))
        return scat[grp][1][1]

    def ffn_starts(tag):
        return (lambda dwo: start(tag + "_out", [tag + "_w_out"], [dwo.reshape(N_FF_BLK, FF_PAD, D_MODEL)]),
                lambda dwt: start(tag + "_in", [tag + "_w_in"], [dwt.reshape(N_DEV, FF_PAD, D_MODEL)]))

    dx2, G["ffn2_pre_g"], G["ffn2_post_g"] = _ffn_bwd(
        "ffn2", dx3, x2, small["ffn2_pre_g"], w2t, w2o, small["ffn2_post_g"], sv2, *ffn_starts("ffn2"))
    dx1, Gm = _mix_bwd(dx2, x1, mem, Wm, svm,
                       lambda Gm: start("mix", _GATHERS["mix"], [_mix_chunks(Gm, n) for n in _GATHERS["mix"]]))
    G.update(Gm)
    dx, G["ffn1_pre_g"], G["ffn1_post_g"] = _ffn_bwd(
        "ffn1", dx1, x, small["ffn1_pre_g"], w1t, w1o, small["ffn1_post_g"], sv1, *ffn_starts("ffn1"))

    recv, outs = {}, {}
    done = [dx]
    for grp in ["ffn2_out", "ffn2_in", "mix", "small", "ffn1_out", "ffn1_in"]:
        if grp == "small":
            names = list(small)
            recv.update(zip(names, _exchange("gather_small_grads", [G[n] for n in small], _GATHER, done)))
        else:
            names, (state, _) = scat[grp]
            recv.update(zip(names, _exchange_wait("scatter_" + grp + "_wait", state, done[-1:])))
        for n in names:
            shp = P[n].shape
            shp2 = shp if len(shp) == 2 else (shp[0] * shp[1], shp[2])
            res = _adam("adam_" + n, recv[n].reshape((N_DEV,) + shp2), P[n].reshape(shp2), Mo[n].reshape(shp2),
                        Vo[n].reshape(shp2))
            outs[n] = [native(n, t.reshape(shp)) for t in res]
            done.append(res[-1])
    return loss, dx, outs


def kernel(x, mem, ffn1_pre_g, ffn1_w_in, ffn1_w_out, ffn1_post_g, mix_pre_g, w_in, w_fu, b_f, gla_norm_g, w_pool, pool_scale, mem_norm_g, w_mem_kv, w_up_gla, w_up_pool, w_up_xattn, w_o, mix_post_g, ffn2_pre_g, ffn2_w_in, ffn2_w_out, ffn2_post_g, final_g, loss_target, m_ffn1_pre_g, m_ffn1_w_in, m_ffn1_w_out, m_ffn1_post_g, m_mix_pre_g, m_w_in, m_w_fu, m_b_f, m_gla_norm_g, m_w_pool, m_pool_scale, m_mem_norm_g, m_w_mem_kv, m_w_up_gla, m_w_up_pool, m_w_up_xattn, m_w_o, m_mix_post_g, m_ffn2_pre_g, m_ffn2_w_in, m_ffn2_w_out, m_ffn2_post_g, m_final_g, v_ffn1_pre_g, v_ffn1_w_in, v_ffn1_w_out, v_ffn1_post_g, v_mix_pre_g, v_w_in, v_w_fu, v_b_f, v_gla_norm_g, v_w_pool, v_pool_scale, v_mem_norm_g, v_w_mem_kv, v_w_up_gla, v_w_up_pool, v_w_up_xattn, v_w_o, v_mix_post_g, v_ffn2_pre_g, v_ffn2_w_in, v_ffn2_w_out, v_ffn2_post_g, v_final_g):
    params = [ffn1_pre_g, ffn1_w_in, ffn1_w_out, ffn1_post_g, mix_pre_g, w_in, w_fu, b_f, gla_norm_g, w_pool, pool_scale, mem_norm_g, w_mem_kv, w_up_gla, w_up_pool, w_up_xattn, w_o, mix_post_g, ffn2_pre_g, ffn2_w_in, ffn2_w_out, ffn2_post_g, final_g]
    moms = [m_ffn1_pre_g, m_ffn1_w_in, m_ffn1_w_out, m_ffn1_post_g, m_mix_pre_g, m_w_in, m_w_fu, m_b_f, m_gla_norm_g, m_w_pool, m_pool_scale, m_mem_norm_g, m_w_mem_kv, m_w_up_gla, m_w_up_pool, m_w_up_xattn, m_w_o, m_mix_post_g, m_ffn2_pre_g, m_ffn2_w_in, m_ffn2_w_out, m_ffn2_post_g, m_final_g]
    vars_ = [v_ffn1_pre_g, v_ffn1_w_in, v_ffn1_w_out, v_ffn1_post_g, v_mix_pre_g, v_w_in, v_w_fu, v_b_f, v_gla_norm_g, v_w_pool, v_pool_scale, v_mem_norm_g, v_w_mem_kv, v_w_up_gla, v_w_up_pool, v_w_up_xattn, v_w_o, v_mix_post_g, v_ffn2_pre_g, v_ffn2_w_in, v_ffn2_w_out, v_ffn2_post_g, v_final_g]
    P = {n: a[0] if a.ndim > 2 else a for n, a in zip(_NAMES, params)}
    Mo = {n: a[0] if a.ndim > 2 else a for n, a in zip(_NAMES, moms)}
    Vo = {n: a[0] if a.ndim > 2 else a for n, a in zip(_NAMES, vars_)}
    loss, dx, outs = _step(x[0], mem[0], loss_target[0], P, Mo, Vo)
    loss = lax.psum(loss, ("x", "y", "c"))
    out = [loss, dx[None]]
    for kind in range(4):
        for n, p in zip(_NAMES, params):
            out.append(outs[n][kind].reshape(p.shape))
    return tuple(out)
```

```python
import functools

import jax
import jax.numpy as jnp
from jax import lax
from jax.experimental import pallas as pl
from jax.experimental.pallas import tpu as pltpu

F32 = jnp.float32
BF16 = jnp.bfloat16

N_DEV = 8
D_MODEL = 1024
D_FF = 2816
FF_BLK = 2 * D_FF // N_DEV
N_FF_BLK = D_FF // FF_BLK
FF_PAD = 768
CHUNK = 64
GLA_HEADS = 4
GLA_DK = 512
GLA_DV = 1024
GLA_HDK = 128
GLA_HDV = 256
GATE_RANK = 16
GATE_TEMP = 16.0
POOL_WINDOWS = (2, 4, 8, 16)
POOL_W = 512
POOL_GD = 128
POOL_HALO = 16
XA_HEADS = 4
XA_HD = 128
XA_W = 512
EPS = 1e-6
IN_SPLITS = (GLA_DK, GLA_DK, GLA_DV, GLA_DV, GATE_RANK, POOL_W, XA_W, 3 * D_MODEL)
IN_WIDTH = sum(IN_SPLITS)
IN_SHARD = IN_WIDTH // N_DEV
INT_W = 3072 + 3072 + 1024 + 128
INT_NBLK = 3
INT_BLK = INT_W // INT_NBLK
FLOW_W = 128

ADAM_LR = 0.001
ADAM_B1 = 0.9
ADAM_B2 = 0.999
ADAM_EPS = 1e-08
ADAM_WD = 0.01
ADAM_STEP = 10

VMEM_LIMIT = 56 * 1024 * 1024

_NN = (((1,), (0,)), ((), ()))
_NT = (((1,), (1,)), ((), ()))
_TN = (((0,), (0,)), ((), ()))


def _pcall(body, *, name, grid, in_specs, out_specs, out_shape, scratch=()):
    return pl.pallas_call(
        body, name=name, grid=grid, in_specs=in_specs, out_specs=out_specs, out_shape=out_shape,
        scratch_shapes=list(scratch),
        compiler_params=pltpu.CompilerParams(dimension_semantics=("arbitrary",) * len(grid),
                                             vmem_limit_bytes=VMEM_LIMIT))


def _dot(a, b, dims=_NN):
    return lax.dot_general(a.astype(BF16), b.astype(BF16), dims, preferred_element_type=F32)


def _mm(name, a, b, *, grid, a_blk, a_map, b_blk, b_map, o_shape, o_blk, o_map, dims, out_dtype=F32, deps=()):
    nk = grid[2]

    def body(a_ref, b_ref, *rest):
        o_ref, scr = rest[len(deps)], rest[len(deps) + 1:]
        p = _dot(a_ref[...], b_ref[...], dims)
        if nk == 1:
            o_ref[...] = p.astype(o_ref.dtype)
        else:
            acc = scr[0]
            k = pl.program_id(2)

            @pl.when(k == 0)
            def _():
                acc[...] = p

            @pl.when(k > 0)
            def _():
                acc[...] += p

            @pl.when(k == nk - 1)
            def _():
                o_ref[...] = acc[...].astype(o_ref.dtype)

    acc_shape = tuple(d for d in o_blk if d is not None)
    return _pcall(body, name=name, grid=grid,
                  in_specs=[pl.BlockSpec(a_blk, a_map), pl.BlockSpec(b_blk, b_map)]
                  + [pl.BlockSpec(memory_space=pl.ANY)] * len(deps),
                  out_specs=pl.BlockSpec(o_blk, o_map),
                  out_shape=jax.ShapeDtypeStruct(o_shape, out_dtype),
                  scratch=[pltpu.VMEM(acc_shape, F32)] if nk > 1 else [])(a, b, *deps)


def _mm_nn(name, a, b, out_dtype=F32, tm=512, tn=None):
    M, K = a.shape
    N = b.shape[1]
    tm, tn = min(tm, M), (tn or N)
    return _mm(name, a, b, grid=(N // tn, M // tm, 1), a_blk=(tm, K), a_map=lambda j, i, k: (i, 0),
               b_blk=(K, tn), b_map=lambda j, i, k: (0, j), o_shape=(M, N), o_blk=(tm, tn),
               o_map=lambda j, i, k: (i, j), dims=_NN, out_dtype=out_dtype)


def _mm_nt(name, a, b, out_dtype=F32, tm=512):
    M, K = a.shape
    N = b.shape[0]
    tm = min(tm, M)
    return _mm(name, a, b, grid=(1, M // tm, 1), a_blk=(tm, K), a_map=lambda j, i, k: (i, 0),
               b_blk=(N, K), b_map=lambda j, i, k: (0, 0), o_shape=(M, N), o_blk=(tm, N),
               o_map=lambda j, i, k: (i, 0), dims=_NT, out_dtype=out_dtype)


def _mm_tn(name, a, b, out_dtype=BF16, ts=512, tn=None, tn_a=None, deps=()):
    S, M = a.shape
    N = b.shape[1]
    ts, tn, tn_a = min(ts, S), (tn or N), (tn_a or M)
    return _mm(name, a, b, grid=((N // tn) * (M // tn_a), 1, S // ts), a_blk=(ts, tn_a),
               a_map=lambda j, i, k: (k, j if tn_a < M else 0), b_blk=(ts, tn),
               b_map=lambda j, i, k: (k, j if tn < N else 0), o_shape=(M, N), o_blk=(tn_a, tn),
               o_map=lambda j, i, k: (j, 0) if tn_a < M else (0, j), dims=_TN, out_dtype=out_dtype, deps=deps)


class _Win:
    def __init__(self, arr, w, c):
        self.arr, self.w, self.c = arr, w, c


def _row_spec(x, tm):
    if isinstance(x, _Win):
        return x.arr, pl.BlockSpec((tm, x.w), functools.partial(lambda i, c: (i, c), c=x.c))
    if x.ndim == 3:
        return x, pl.BlockSpec((x.shape[0], tm, x.shape[2]), lambda i: (0, i, 0))
    return x, pl.BlockSpec((tm, x.shape[1]), lambda i: (i, 0))


def _rowwise(name, fn, rows, consts, outs, accs=(), tm=256):
    first = rows[0].arr if isinstance(rows[0], _Win) else rows[0]
    S = first.shape[1] if first.ndim == 3 else first.shape[0]
    tm = min(tm, S)
    n_in, n_out = len(rows) + len(consts), len(outs)
    arrays, in_specs = [], []
    for r in rows:
        arr, spec = _row_spec(r, tm)
        arrays.append(arr)
        in_specs.append(spec)
    for c in consts:
        arrays.append(c)
        in_specs.append(pl.BlockSpec(c.shape, functools.partial(lambda i, n: (0,) * n, n=c.ndim)))
    out_specs = [_row_spec(o, tm)[1] for o in outs]
    out_specs += [pl.BlockSpec(a.shape, functools.partial(lambda i, n: (0,) * n, n=len(a.shape))) for a in accs]

    def body(*refs):
        res = fn(*[r[...] for r in refs[:n_in]])
        if not isinstance(res, (tuple, list)):
            res = (res,)
        orefs = refs[n_in:]
        for r, v in zip(orefs[:n_out], res[:n_out]):
            r[...] = v.astype(r.dtype)
        i = pl.program_id(0)
        for r, v in zip(orefs[n_out:], res[n_out:]):
            @pl.when(i == 0)
            def _(r=r, v=v):
                r[...] = v.astype(r.dtype)

            @pl.when(i > 0)
            def _(r=r, v=v):
                r[...] += v.astype(r.dtype)

    res = _pcall(body, name=name, grid=(S // tm,), in_specs=in_specs, out_specs=out_specs,
                 out_shape=[jax.ShapeDtypeStruct(o.shape, o.dtype) for o in list(outs) + list(accs)])(*arrays)
    return res


def _sds(shape, dtype=F32):
    return jax.ShapeDtypeStruct(shape, dtype)


def _rms(x, g):
    return x * lax.rsqrt(jnp.mean(x * x, axis=-1, keepdims=True) + EPS) * g


def _rms_bwd(x, g, dy):
    _, vjp = jax.vjp(_rms, x, g)
    return vjp(dy)


def _sigmoid(x):
    return 0.5 * jnp.tanh(0.5 * x) + 0.5


def _silu(x):
    return x * _sigmoid(x)


def _log_sigmoid(f):
    return jnp.minimum(f, 0.0) - jnp.log(1.0 + jnp.exp(-jnp.abs(f)))


def _head_rms_gate(o, g_out, gn):
    parts = [_rms(o[:, h * GLA_HDV:(h + 1) * GLA_HDV], gn[:, h * GLA_HDV:(h + 1) * GLA_HDV]) for h in range(GLA_HEADS)]
    return jnp.concatenate(parts, axis=-1) * _silu(g_out.astype(F32))


def _merge(gates, ya, yb, yc):
    gates, ya, yb, yc = (t.astype(F32) for t in (gates, ya, yb, yc))
    return (_sigmoid(gates[:, :D_MODEL]) * ya + _sigmoid(gates[:, D_MODEL:2 * D_MODEL]) * yb
            + _sigmoid(gates[:, 2 * D_MODEL:]) * yc)


def _tri_dot(t, x):
    hi = x.astype(BF16)
    r1 = x - hi.astype(F32)
    mid = r1.astype(BF16)
    lo = (r1 - mid.astype(F32)).astype(BF16)
    d = functools.partial(lax.dot_general, dimension_numbers=_NN, preferred_element_type=F32)
    return d(t, hi) + d(t, mid) + d(t, lo)


def _ffn_fwd(tag, x, gpre, wt, gpost, get_wo):
    S = x.shape[0]
    h = _rowwise(tag + "_pre", lambda x, g: _rms(x, g), [x], [gpre], [_sds((S, D_MODEL), BF16)])[0]
    tm = min(512, S)

    def up_body(h_ref, w_ref, u_ref, act_ref):
        hh = h_ref[...]
        for j in range(N_FF_BLK):
            ab = _dot(hh, w_ref[2 * FF_PAD * j:2 * FF_PAD * (j + 1), :], _NT)
            u_ref[:, 2 * FF_PAD * j:2 * FF_PAD * (j + 1)] = ab.astype(BF16)
            act_ref[:, FF_PAD * j:FF_PAD * (j + 1)] = (_silu(ab[:, :FF_PAD]) * ab[:, FF_PAD:]).astype(BF16)

    u, act = _pcall(
        up_body, name=tag + "_up", grid=(S // tm,),
        in_specs=[pl.BlockSpec((tm, D_MODEL), lambda i: (i, 0)),
                  pl.BlockSpec(wt.shape, lambda i: (0, 0), pipeline_mode=pl.Buffered(1))],
        out_specs=[pl.BlockSpec((tm, N_DEV * FF_PAD), lambda i: (i, 0)),
                   pl.BlockSpec((tm, N_FF_BLK * FF_PAD), lambda i: (i, 0))],
        out_shape=[_sds((S, N_DEV * FF_PAD), BF16), _sds((S, N_FF_BLK * FF_PAD), BF16)],
    )(h, wt)

    def down_body(act_ref, w_ref, x_ref, g_ref, f_ref, xo_ref):
        f = _dot(act_ref[...], w_ref[...])
        f_ref[...] = f
        xo_ref[...] = x_ref[...] + 0.5 * _rms(f, g_ref[...])

    wo = get_wo(act)
    row = pl.BlockSpec((tm, D_MODEL), lambda i: (i, 0))
    f, xo = _pcall(
        down_body, name=tag + "_down", grid=(S // tm,),
        in_specs=[pl.BlockSpec((tm, N_FF_BLK * FF_PAD), lambda i: (i, 0)),
                  pl.BlockSpec(wo.shape, lambda i: (0, 0), pipeline_mode=pl.Buffered(1)), row,
                  pl.BlockSpec((1, D_MODEL), lambda i: (0, 0))],
        out_specs=[row, row], out_shape=[_sds((S, D_MODEL)), _sds((S, D_MODEL))],
    )(act, wo, x, gpost)
    return xo, wo, (h, u, act, f)


def _ffn_bwd(tag, dxo, x, gpre, wt, wo, gpost, saved, on_dwo=None, on_dwt=None):
    h, u, act, f = saved
    S = x.shape[0]
    tm = min(256, S)

    def dact_body(f_ref, dxo_ref, g_ref, w_ref, u_ref, df_ref, du_ref, dg_ref):
        i = pl.program_id(0)
        df, dg = _rms_bwd(f_ref[...], g_ref[...], 0.5 * dxo_ref[...])
        df = df.astype(BF16)
        df_ref[...] = df

        @pl.when(i == 0)
        def _():
            dg_ref[...] = dg

        @pl.when(i > 0)
        def _():
            dg_ref[...] += dg

        for j in range(N_FF_BLK):
            dact = _dot(df, w_ref[FF_PAD * j:FF_PAD * (j + 1), :], _NT)
            ab = u_ref[:, 2 * FF_PAD * j:2 * FF_PAD * (j + 1)].astype(F32)
            a, b = ab[:, :FF_PAD], ab[:, FF_PAD:]
            sg = _sigmoid(a)
            du_ref[:, 2 * FF_PAD * j:2 * FF_PAD * j + FF_PAD] = (dact * b * (sg * (1.0 + a * (1.0 - sg)))).astype(BF16)
            du_ref[:, 2 * FF_PAD * j + FF_PAD:2 * FF_PAD * (j + 1)] = (dact * (a * sg)).astype(BF16)

    row = pl.BlockSpec((tm, D_MODEL), lambda i: (i, 0))
    vec = pl.BlockSpec((1, D_MODEL), lambda i: (0, 0))
    u_spec = pl.BlockSpec((tm, N_DEV * FF_PAD), lambda i: (i, 0))
    df, du, dgpost = _pcall(
        dact_body, name=tag + "_dact", grid=(S // tm,),
        in_specs=[row, row, vec, pl.BlockSpec(wo.shape, lambda i: (0, 0), pipeline_mode=pl.Buffered(1)), u_spec],
        out_specs=[row, u_spec, vec],
        out_shape=[_sds((S, D_MODEL), BF16), _sds((S, N_DEV * FF_PAD), BF16), _sds((1, D_MODEL))])(f, dxo, gpost, wo, u)
    half = N_FF_BLK * FF_PAD
    dwo = _mm_tn(tag + "_dwo", act, df, ts=1024, tn_a=half)
    dwt = _mm_tn(tag + "_dwt", du, h, ts=1024, tn_a=half, deps=[on_dwo(dwo)] if on_dwo is not None else [])
    dx, dgpre = _dh_pre_norm_bwd(tag + "_dh", du, wt, x, dxo, gpre, [on_dwt(dwt)] if on_dwt is not None else [])
    return dx, dgpre, dgpost


def _dh_pre_norm_bwd(name, dz, wt, x, dres, g, deps):
    S, K = dz.shape

    def body(dz_ref, w_ref, x_ref, dres_ref, g_ref, *rest):
        dx_ref, dg_ref = rest[len(deps):]
        i = pl.program_id(0)
        dx, dg = _rms_bwd(x_ref[...], g_ref[...], _dot(dz_ref[...], w_ref[...]))
        dx_ref[...] = dx + dres_ref[...]

        @pl.when(i == 0)
        def _():
            dg_ref[...] = dg

        @pl.when(i > 0)
        def _():
            dg_ref[...] += dg

    th = min(256, S)
    row = pl.BlockSpec((th, D_MODEL), lambda i: (i, 0))
    vec = pl.BlockSpec((1, D_MODEL), lambda i: (0, 0))
    return _pcall(
        body, name=name, grid=(S // th,),
        in_specs=[pl.BlockSpec((th, K), lambda i: (i, 0)),
                  pl.BlockSpec(wt.shape, lambda i: (0, 0), pipeline_mode=pl.Buffered(1)), row, row, vec]
        + [pl.BlockSpec(memory_space=pl.ANY)] * len(deps),
        out_specs=[row, vec], out_shape=[_sds((S, D_MODEL)), _sds((1, D_MODEL))])(dz, wt, x, dres, g, *deps)


GLA_G = 4


def _gla_tile_common(k, flow, wfu, bf):
    f = _dot(flow, wfu) + bf
    la = _log_sigmoid(f) * (1.0 / GATE_TEMP)
    tri = _tri_matrix(True)
    ws, ds = [], []
    for g in range(k.shape[0] // CHUNK):
        b = _tri_dot(tri, la[g * CHUNK:(g + 1) * CHUNK])
        b_end = b[CHUNK - 1:CHUNK, :]
        ws.append(jnp.exp(b_end - b))
        ds.append(jnp.exp(b_end))
    w = jnp.concatenate(ws, axis=0)
    return f, w, k * w, ds


def _tri_matrix(lower):
    r = lax.broadcasted_iota(jnp.int32, (CHUNK, CHUNK), 0)
    c = lax.broadcasted_iota(jnp.int32, (CHUNK, CHUNK), 1)
    return jnp.where((r >= c) if lower else (r <= c), 1.0, 0.0).astype(BF16)


def _heads():
    return [(slice(h * GLA_HDK, (h + 1) * GLA_HDK), slice(h * GLA_HDV, (h + 1) * GLA_HDV)) for h in range(GLA_HEADS)]


def _gla_fwd(proj, wfu, bf):
    S = proj.shape[0]
    G = min(GLA_G, S // CHUNK)
    T = G * CHUNK
    nc = S // CHUNK

    def body(q_ref, k_ref, v_ref, flow_ref, wfu_ref, bf_ref, o_ref, st_ref, state):
        @pl.when(pl.program_id(0) == 0)
        def _():
            state[...] = jnp.zeros_like(state)

        _, _, kt, ds = _gla_tile_common(k_ref[...], flow_ref[...], wfu_ref[...], bf_ref[...])
        q = q_ref[...].astype(F32) * (GLA_HDK ** -0.5)
        v = v_ref[...]
        rows = [slice(g * CHUNK, (g + 1) * CHUNK) for g in range(G)]
        kv = [[_dot(v[r, vs], kt[r, ks], _TN) for ks, vs in _heads()] for r in rows]
        st = [state[vs, :] for _, vs in _heads()]
        for g, r in enumerate(rows):
            outs = []
            for h, (ks, vs) in enumerate(_heads()):
                st[h] = st[h] * ds[g][:, ks] + kv[g][h]
                st_ref[g, vs, :] = st[h]
                outs.append(_dot(q[r, ks], st[h], _NT))
            o_ref[r, :] = jnp.concatenate(outs, axis=-1)
        for h, (_, vs) in enumerate(_heads()):
            state[vs, :] = st[h]

    return _pcall(
        body, name="gla_fwd", grid=(S // T,),
        in_specs=[pl.BlockSpec((T, GLA_DK), lambda c: (c, 0)), pl.BlockSpec((T, GLA_DK), lambda c: (c, 1)),
                  pl.BlockSpec((T, GLA_DV), lambda c: (c, 1)),
                  pl.BlockSpec((T, FLOW_W), lambda c: (c, (INT_W - FLOW_W) // FLOW_W)),
                  pl.BlockSpec(wfu.shape, lambda c: (0, 0)), pl.BlockSpec(bf.shape, lambda c: (0, 0))],
        out_specs=[pl.BlockSpec((T, GLA_DV), lambda c: (c, 0)),
                   pl.BlockSpec((G, GLA_DV, GLA_HDK), lambda c: (c, 0, 0))],
        out_shape=[_sds((S, GLA_DV)), _sds((nc, GLA_DV, GLA_HDK))],
        scratch=[pltpu.VMEM((GLA_DV, GLA_HDK), F32)])(proj, proj, proj, proj, wfu, bf)


def _gla_bwd(proj, wfu, bf, states, do):
    S = proj.shape[0]
    G = min(GLA_G, S // CHUNK)
    T = G * CHUNK
    nt = S // T

    def body(q_ref, k_ref, v_ref, flow_ref, wfu_ref, bf_ref, st_ref, stp_ref, do_ref,
             dq_ref, dk_ref, dv_ref, dflow_ref, dwfu_ref, dbf_ref, dstate):
        step = pl.program_id(0)

        @pl.when(step == 0)
        def _():
            dstate[...] = jnp.zeros_like(dstate)
            dwfu_ref[...] = jnp.zeros_like(dwfu_ref)
            dbf_ref[...] = jnp.zeros_like(dbf_ref)

        flow, wfu_v = flow_ref[...], wfu_ref[...]
        f, w, kt, ds = _gla_tile_common(k_ref[...], flow, wfu_v, bf_ref[...])
        q = q_ref[...].astype(F32) * (GLA_HDK ** -0.5)
        v, dout = v_ref[...], do_ref[...]
        rows = [slice(g * CHUNK, (g + 1) * CHUNK) for g in range(G)]
        dq = [jnp.concatenate([_dot(dout[r, vs], st_ref[g, vs, :]) for _, vs in _heads()], axis=-1)
              for g, r in enumerate(rows)]
        qdo = [[_dot(dout[r, vs], q[r, ks], _TN) for ks, vs in _heads()] for r in rows]
        dq_ref[...] = (jnp.concatenate(dq, axis=0) * (GLA_HDK ** -0.5)).astype(dq_ref.dtype)
        has_prev = (step < nt - 1).astype(F32)
        carry = [dstate[vs, :] for _, vs in _heads()]
        dkt, dv, dd = [None] * G, [None] * G, [None] * G
        for g in reversed(range(G)):
            r = rows[g]
            dkts, dvs, dds = [], [], []
            for h, (ks, vs) in enumerate(_heads()):
                dst = carry[h] + qdo[g][h]
                dkts.append(_dot(v[r, vs], dst))
                dvs.append(_dot(kt[r, ks], dst, _NT))
                st_prev = st_ref[g - 1, vs, :] if g > 0 else stp_ref[vs, :] * has_prev
                dds.append(jnp.sum(dst * st_prev, axis=0, keepdims=True))
                carry[h] = dst * ds[g][:, ks]
            dkt[g], dv[g], dd[g] = (jnp.concatenate(t, axis=-1) for t in (dkts, dvs, dds))
        for h, (_, vs) in enumerate(_heads()):
            dstate[vs, :] = carry[h]
        dkt = jnp.concatenate(dkt, axis=0)
        dv_ref[...] = jnp.concatenate(dv, axis=0).astype(dv_ref.dtype)
        dk_ref[...] = (dkt * w).astype(dk_ref.dtype)
        de = dkt * kt
        tri = _tri_matrix(False)
        dla = []
        for g, r in enumerate(rows):
            db_end = jnp.sum(de[r], axis=0, keepdims=True) + dd[g] * ds[g]
            dla.append(db_end - _tri_dot(tri, de[r]))
        df = jnp.concatenate(dla, axis=0) * (1.0 - _sigmoid(f)) * (1.0 / GATE_TEMP)
        dflow_ref[...] = _dot(df, wfu_v, _NT).astype(dflow_ref.dtype)
        dwfu_ref[...] += _dot(flow, df, _TN)
        dbf_ref[...] += jnp.sum(df, axis=0, keepdims=True)

    rt = lambda s: nt - 1 - s
    return _pcall(
        body, name="gla_bwd", grid=(nt,),
        in_specs=[pl.BlockSpec((T, GLA_DK), lambda s: (rt(s), 0)), pl.BlockSpec((T, GLA_DK), lambda s: (rt(s), 1)),
                  pl.BlockSpec((T, GLA_DV), lambda s: (rt(s), 1)),
                  pl.BlockSpec((T, FLOW_W), lambda s: (rt(s), (INT_W - FLOW_W) // FLOW_W)),
                  pl.BlockSpec(wfu.shape, lambda s: (0, 0)), pl.BlockSpec(bf.shape, lambda s: (0, 0)),
                  pl.BlockSpec((G, GLA_DV, GLA_HDK), lambda s: (rt(s), 0, 0)),
                  pl.BlockSpec((None, GLA_DV, GLA_HDK), lambda s: (jnp.maximum(rt(s) * G - 1, 0), 0, 0)),
                  pl.BlockSpec((T, GLA_DV), lambda s: (rt(s), 0))],
        out_specs=[pl.BlockSpec((T, GLA_DK), lambda s: (rt(s), 0)), pl.BlockSpec((T, GLA_DK), lambda s: (rt(s), 0)),
                   pl.BlockSpec((T, GLA_DV), lambda s: (rt(s), 0)), pl.BlockSpec((T, FLOW_W), lambda s: (rt(s), 0)),
                   pl.BlockSpec(wfu.shape, lambda s: (0, 0)), pl.BlockSpec(bf.shape, lambda s: (0, 0))],
        out_shape=[_sds((S, GLA_DK), BF16), _sds((S, GLA_DK), BF16), _sds((S, GLA_DV), BF16), _sds((S, FLOW_W), BF16),
                   _sds(wfu.shape), _sds(bf.shape)],
        scratch=[pltpu.VMEM((GLA_DV, GLA_HDK), F32)])(proj, proj, proj, proj, wfu, bf, states, states, do)


def _pool_counts(tm, i):
    t = (lax.broadcasted_iota(jnp.int32, (tm, POOL_GD), 0) + i * tm + 1).astype(F32)
    return [jnp.minimum(t, float(w)) for w in POOL_WINDOWS]


def _pool_fwd(proj, w_pool, pool_scale, tm=256):
    S = proj.shape[0]
    tm = min(tm, S // 2)
    col = (3072 + 3072) // POOL_W
    hb = tm // POOL_HALO

    def body(p_ref, halo_ref, wp_ref, sc_ref, mixed_ref, out_ref):
        i = pl.program_id(0)
        p = p_ref[...].astype(F32)
        halo = halo_ref[...].astype(F32) * (i > 0).astype(F32)
        ext = jnp.concatenate([halo, p], axis=0)
        n = tm + POOL_HALO
        sums, acc, k = {}, ext, 1
        while k < POOL_WINDOWS[-1]:
            acc = acc + pltpu.roll(acc, k, axis=0)
            k *= 2
            sums[k] = acc
        cnts = _pool_counts(tm, i)
        mixed, lin = [], []
        for g, w in enumerate(POOL_WINDOWS):
            ls = slice(g * POOL_GD, (g + 1) * POOL_GD)
            m = sums[w][POOL_HALO:n, ls] / cnts[g] - p[:, ls]
            mixed.append(m)
            lin.append(_dot(m, wp_ref[g]))
        mixed_ref[...] = jnp.concatenate(mixed, axis=-1)
        out_ref[...] = (jnp.concatenate(lin, axis=-1) * sc_ref[...]).astype(out_ref.dtype)

    return _pcall(
        body, name="pool_fwd", grid=(S // tm,),
        in_specs=[pl.BlockSpec((tm, POOL_W), lambda i: (i, col)),
                  pl.BlockSpec((POOL_HALO, POOL_W), lambda i: (jnp.maximum(i * hb - 1, 0), col)),
                  pl.BlockSpec(w_pool.shape, lambda i: (0, 0, 0)), pl.BlockSpec(pool_scale.shape, lambda i: (0, 0))],
        out_specs=[pl.BlockSpec((tm, POOL_W), lambda i: (i, 0)), pl.BlockSpec((tm, POOL_W), lambda i: (i, 0))],
        out_shape=[_sds((S, POOL_W)), _sds((S, POOL_W), BF16)])(proj, proj, w_pool, pool_scale)


def _pool_lin_bwd(dout, mixed, w_pool, pool_scale):
    S = dout.shape[0]

    def fn(dout, mixed, wp, sc):
        dlin = dout * sc
        dm, dwp, lin = [], [], []
        for g in range(len(POOL_WINDOWS)):
            ls = slice(g * POOL_GD, (g + 1) * POOL_GD)
            lin.append(_dot(mixed[:, ls], wp[g]))
            dm.append(_dot(dlin[:, ls], wp[g], _NT))
            dwp.append(_dot(mixed[:, ls], dlin[:, ls], _TN))
        dsc = jnp.sum(dout * jnp.concatenate(lin, axis=-1), axis=0, keepdims=True)
        return jnp.concatenate(dm, axis=-1), jnp.concatenate(dwp, axis=0), dsc

    return _rowwise("pool_lin_bwd", fn, [dout, mixed], [w_pool, pool_scale], [_sds((S, POOL_W))],
                    [_sds((len(POOL_WINDOWS) * POOL_GD, POOL_GD)), _sds((1, POOL_W))])


def _pool_win_bwd(dmixed, tm=256):
    S = dmixed.shape[0]
    tm = min(tm, S // 2)
    nt = S // tm
    hb = tm // POOL_HALO

    def body(dm_ref, halo_ref, dp_ref):
        i = pl.program_id(0)
        dm = dm_ref[...]
        halo = halo_ref[...] * (i < nt - 1).astype(F32)
        cnts = _pool_counts(tm, i)
        cnts_h = [c[:POOL_HALO] for c in _pool_counts(tm, i + 1)]
        r = jnp.concatenate([jnp.concatenate([dm[:, g * POOL_GD:(g + 1) * POOL_GD] / cnts[g] for g in range(4)], axis=-1),
                             jnp.concatenate([halo[:, g * POOL_GD:(g + 1) * POOL_GD] / cnts_h[g] for g in range(4)], axis=-1)],
                            axis=0)
        n = tm + POOL_HALO
        sums, acc, k = {}, r, 1
        while k < POOL_WINDOWS[-1]:
            acc = acc + pltpu.roll(acc, n - k, axis=0)
            k *= 2
            sums[k] = acc
        dp = [sums[w][:tm, g * POOL_GD:(g + 1) * POOL_GD] for g, w in enumerate(POOL_WINDOWS)]
        dp_ref[...] = (jnp.concatenate(dp, axis=-1) - dm).astype(dp_ref.dtype)

    return _pcall(
        body, name="pool_win_bwd", grid=(nt,),
        in_specs=[pl.BlockSpec((tm, POOL_W), lambda i: (i, 0)),
                  pl.BlockSpec((POOL_HALO, POOL_W), lambda i: (jnp.minimum((i + 1) * hb, S // POOL_HALO - 1), 0))],
        out_specs=pl.BlockSpec((tm, POOL_W), lambda i: (i, 0)),
        out_shape=_sds((S, POOL_W), BF16))(dmixed, dmixed)


def _xattn_probs(q, kv, h):
    hs = slice(h * XA_HD, (h + 1) * XA_HD)
    s = _dot(q[:, hs], kv[:, hs], _NT) * (XA_HD ** -0.5)
    s = s - jnp.max(s, axis=-1, keepdims=True)
    e = jnp.exp(s)
    return e / jnp.sum(e, axis=-1, keepdims=True)


def _xattn_fwd(proj, kv):
    S = proj.shape[0]

    def fn(q, kv):
        outs = []
        for h in range(XA_HEADS):
            p = _xattn_probs(q, kv, h)
            outs.append(_dot(p, kv[:, XA_W + h * XA_HD:XA_W + (h + 1) * XA_HD]))
        return jnp.concatenate(outs, axis=-1)

    return _rowwise("xattn_fwd", fn, [_Win(proj, XA_W, (3072 + 3072 + POOL_W) // XA_W)], [kv], [_sds((S, XA_W), BF16)])[0]


def _xattn_bwd(proj, kv, dxa):
    S = proj.shape[0]

    def fn(q, dxa, kv):
        dqs, dks, dvs = [], [], []
        for h in range(XA_HEADS):
            hs = slice(h * XA_HD, (h + 1) * XA_HD)
            vh = kv[:, XA_W + h * XA_HD:XA_W + (h + 1) * XA_HD]
            p = _xattn_probs(q, kv, h)
            dp = _dot(dxa[:, hs], vh, _NT)
            ds = p * (dp - jnp.sum(p * dp, axis=-1, keepdims=True)) * (XA_HD ** -0.5)
            dqs.append(_dot(ds, kv[:, hs]))
            dks.append(_dot(ds, q[:, hs], _TN))
            dvs.append(_dot(p, dxa[:, hs], _TN))
        return jnp.concatenate(dqs, axis=-1), jnp.concatenate(dks + dvs, axis=-1)

    return _rowwise("xattn_bwd", fn, [_Win(proj, XA_W, (3072 + 3072 + POOL_W) // XA_W), dxa], [kv],
                    [_sds((S, XA_W), BF16)], [_sds(kv.shape)])


def _mix_fwd(x1, mem, W):
    S = x1.shape[0]
    M = mem.shape[0]
    h = _rowwise("mix_pre", lambda x, g: _rms(x, g), [x1], [W["mix_pre_g"]], [_sds((S, D_MODEL), BF16)])[0]
    tm = min(512, S)
    def proj_body(h_ref, w_ref, o_ref):
        hh = h_ref[...]
        for j in range(INT_NBLK):
            o_ref[:, INT_BLK * j:INT_BLK * (j + 1)] = _dot(hh, w_ref[INT_BLK * j:INT_BLK * (j + 1), :], _NT).astype(BF16)

    proj = _pcall(
        proj_body, name="mix_proj", grid=(S // tm,),
        in_specs=[pl.BlockSpec((tm, D_MODEL), lambda i: (i, 0)),
                  pl.BlockSpec((INT_W, D_MODEL), lambda i: (0, 0), pipeline_mode=pl.Buffered(1))],
        out_specs=pl.BlockSpec((tm, INT_W), lambda i: (i, 0)), out_shape=_sds((S, INT_W), BF16))(h, W["w_int_t"])
    o_raw, states = _gla_fwd(proj, W["w_fu_pad"], W["b_f"])
    ya_in = _rowwise("gla_out", _head_rms_gate, [o_raw, _Win(proj, GLA_DV, 2)], [W["gla_norm_g"]],
                     [_sds((S, GLA_DV), BF16)])[0]
    mixed, pool_out = _pool_fwd(proj, W["w_pool"], W["pool_scale"])
    mem_n = _rowwise("mem_norm", lambda m, g: _rms(m, g), [mem], [W["mem_norm_g"]], [_sds((M, D_MODEL), BF16)])[0]
    kv = _mm_nn("mem_kv", mem_n, W["w_mem_kv"])
    xa = _xattn_fwd(proj, kv)
    ya = _mm_nn("up_gla", ya_in, W["w_up_gla"], BF16)
    yb = _mm_nn("up_pool", pool_out, W["w_up_pool"], BF16)
    yc = _mm_nn("up_xattn", xa, W["w_up_xattn"], BF16)
    def out_fn(gates, ya, yb, yc, x1, wo, g):
        merged = _merge(gates, ya, yb, yc).astype(BF16)
        y = _dot(merged, wo)
        return merged, y, x1 + _rms(y, g)

    merged, y, x2 = _rowwise("mix_out", out_fn, [_Win(proj, 3 * D_MODEL, 1), ya, yb, yc, x1], [W["w_o"], W["mix_post_g"]],
                             [_sds((S, D_MODEL), BF16), _sds((S, D_MODEL)), _sds((S, D_MODEL))])
    return x2, (h, proj, o_raw, states, ya_in, mixed, pool_out, mem_n, kv, xa, ya, yb, yc, merged, y)


def _head(x3, tgt, final_g):
    S = x3.shape[0]

    def head(x3, t, g):
        out, vjp = jax.vjp(_rms, x3, g)
        e = out - t
        loss = 0.5 * jnp.sum(jnp.mean(e * e, axis=-1, keepdims=True), axis=0, keepdims=True)
        dx3, dg = vjp(e * (1.0 / D_MODEL))
        return dx3, jnp.broadcast_to(loss, (1, 128)), dg

    dx3, loss_acc, d_final_g = _rowwise("head", head, [x3, tgt], [final_g], [_sds((S, D_MODEL))],
                                        [_sds((1, 128)), _sds((1, D_MODEL))])
    return loss_acc[0, 0], dx3, d_final_g


def _mix_bwd(dx2, x1, mem, W, saved, on_grads=None):
    h, proj, o_raw, states, ya_in, mixed, pool_out, mem_n, kv, xa, ya, yb, yc, merged, y = saved
    S = x1.shape[0]

    def out_bwd(y, dx2, gates, ya, yb, yc, g, wo):
        dy, dg = _rms_bwd(y, g, dx2)
        dy = dy.astype(BF16)
        _, vjp = jax.vjp(_merge, gates, ya, yb, yc)
        return (dy, *vjp(_dot(dy, wo, _NT)), dg)

    dy, dgates, dya, dyb, dyc, d_mix_post_g = _rowwise(
        "mix_out_bwd", out_bwd, [y, dx2, _Win(proj, 3 * D_MODEL, 1), ya, yb, yc], [W["mix_post_g"], W["w_o"]],
        [_sds((S, D_MODEL), BF16), _sds((S, 3 * D_MODEL), BF16)] + [_sds((S, D_MODEL), BF16)] * 3, [_sds((1, D_MODEL))])
    d_w_o = _mm_tn("d_w_o", merged, dy)
    d_w_up_gla = _mm_tn("d_w_up_gla", ya_in, dya)
    d_w_up_pool = _mm_tn("d_w_up_pool", pool_out, dyb)
    d_w_up_xattn = _mm_tn("d_w_up_xattn", xa, dyc)
    d_ya_in = _mm_nt("d_ya_in", dya, W["w_up_gla"], BF16)
    d_pool_out = _mm_nt("d_pool_out", dyb, W["w_up_pool"], BF16)
    d_xa = _mm_nt("d_xa", dyc, W["w_up_xattn"], BF16)

    def gla_out_bwd(o, g_out, d, gn):
        _, vjp = jax.vjp(_head_rms_gate, o, g_out, gn)
        return vjp(d.astype(F32))

    do_raw, dg_out, d_gla_norm_g = _rowwise("gla_out_bwd", gla_out_bwd, [o_raw, _Win(proj, GLA_DV, 2), d_ya_in],
                                            [W["gla_norm_g"]], [_sds((S, GLA_DV)), _sds((S, GLA_DV), BF16)],
                                            [_sds((1, GLA_DV))])
    dq, dk, dv, dflow, d_wfu_pad, d_b_f = _gla_bwd(proj, W["w_fu_pad"], W["b_f"], states, do_raw)
    dmixed, d_w_pool, d_pool_scale = _pool_lin_bwd(d_pool_out, mixed, W["w_pool"], W["pool_scale"])
    dp_in = _pool_win_bwd(dmixed)
    dxq, dkv = _xattn_bwd(proj, kv, d_xa)
    d_w_mem_kv = _mm_tn("d_w_mem_kv", mem_n, dkv)
    dmem_n = _mm_nt("d_mem_n", dkv, W["w_mem_kv"])
    d_mem_norm_g = _rowwise("mem_norm_bwd", lambda m, d, g: _rms_bwd(m, g, d)[1], [mem, dmem_n], [W["mem_norm_g"]], [],
                            [_sds((1, D_MODEL))])[0]
    dproj = jnp.concatenate([dq, dk, dv, dg_out, dgates, dp_in, dxq, dflow], axis=-1)
    d_w_int_t = _mm_tn("d_w_int", dproj, h, ts=1024, tn_a=INT_BLK)
    grads = dict(
        w_int_t=d_w_int_t, w_fu_pad=d_wfu_pad, b_f=d_b_f, gla_norm_g=d_gla_norm_g, w_pool=d_w_pool,
        pool_scale=d_pool_scale, mem_norm_g=d_mem_norm_g, w_mem_kv=d_w_mem_kv, w_up_gla=d_w_up_gla,
        w_up_pool=d_w_up_pool, w_up_xattn=d_w_up_xattn, w_o=d_w_o, mix_post_g=d_mix_post_g)
    deps = [on_grads(grads)] if on_grads is not None else []
    dx1, grads["mix_pre_g"] = _dh_pre_norm_bwd("d_mix_h", dproj, W["w_int_t"], x1, dx2, W["mix_pre_g"], deps)
    return dx1, grads


def _mesh_pos():
    x, y, c = lax.axis_index("x"), lax.axis_index("y"), lax.axis_index("c")
    return x, y, c, 4 * x + 2 * y + c


def _peer(x, y, c, r):
    px = 1 - x if r & 4 else x
    py = 1 - y if r & 2 else y
    pc = 1 - c if r & 1 else c
    return (px, py, pc), 4 * px + 2 * py + pc


_ALL_PEERS = tuple(range(1, N_DEV))
_SIBLING = 1
_SAME_CORE = (2, 4, 6)


def _dev_slot(ref, dev):
    return ref.at[dev]


class _Plan:
    def __init__(self, scatter, slots=None, shapes=None):
        self.scatter, self.slots, self.shapes = scatter, slots or {}, shapes or {}

    def src(self, srcs, a, dev):
        return self.slots.get(a, _dev_slot)(srcs[a], dev) if self.scatter else srcs[a]

    def dst(self, lands, a, dev):
        return lands[a].at[dev] if self.scatter else self.slots.get(a, _dev_slot)(lands[a], dev)

    def landing_zones(self, arrays):
        lands = []
        for a, arr in enumerate(arrays):
            if self.scatter:
                lands.append(lax.empty((N_DEV,) + tuple(self.shapes.get(a, arr.shape[1:])), arr.dtype))
            elif a in self.shapes:
                lands.append(jnp.zeros(self.shapes[a], arr.dtype))
            else:
                lands.append(lax.empty((N_DEV,) + arr.shape, arr.dtype))
        return lands


_GATHER, _SCATTER = _Plan(False), _Plan(True)


def _peer_copies(srcs, lands, send_sems, recv_sems, plan, peers=_ALL_PEERS):
    x, y, c, me = _mesh_pos()
    cps = []
    for r in peers:
        pos, peer = _peer(x, y, c, r)
        for a in range(len(srcs)):
            k = a * (N_DEV - 1) + r - 1
            cps.append(pltpu.make_async_remote_copy(
                src_ref=plan.src(srcs, a, peer), dst_ref=plan.dst(lands, a, me),
                send_sem=send_sems.at[k], recv_sem=recv_sems.at[k], device_id=pos,
                device_id_type=pl.DeviceIdType.MESH))
    return cps


def _exchange(name, arrays, plan, after):
    n = len(arrays)

    def body(*refs):
        ins, outs = refs[:n], refs[n + len(after):2 * n + len(after)]
        send_sems, recv_sems, local_sems = refs[2 * n + len(after):]
        copies = _own_copies(ins, outs, local_sems, plan) + _peer_copies(ins, outs, send_sems, recv_sems, plan)
        for cp in copies:
            cp.start()
        for cp in copies:
            cp.wait()

    any_spec = pl.BlockSpec(memory_space=pl.ANY)
    out_shape = [_sds(l.shape, l.dtype) for l in plan.landing_zones(arrays)]
    return pl.pallas_call(
        body, name=name, in_specs=[any_spec] * (n + len(after)), out_specs=[any_spec] * n, out_shape=out_shape,
        scratch_shapes=[pltpu.SemaphoreType.DMA((n * (N_DEV - 1),)), pltpu.SemaphoreType.DMA((n * (N_DEV - 1),)),
                        pltpu.SemaphoreType.DMA((n,))])(*arrays, *after)


_HBM = pl.BlockSpec(memory_space=pltpu.HBM)
_SEM = pl.BlockSpec(memory_space=pltpu.SEMAPHORE)
_EFFECT = pltpu.SideEffectType.DATAFLOW_SIDE_EFFECTING


def _own_copies(srcs, lands, own_sems, plan):
    me = _mesh_pos()[3]
    return [pltpu.make_async_copy(plan.src(srcs, a, me), plan.dst(lands, a, me), own_sems.at[a])
            for a in range(len(srcs))]


def _exchange_start(name, arrays, plan):
    n = len(arrays)
    lands = plan.landing_zones(arrays)
    n_sem = n * (N_DEV - 1)

    def body(*refs):
        srcs, lands_ = refs[:n], refs[n:2 * n]
        send_sems, recv_sems, own_sems = refs[2 * n:2 * n + 3]
        token = refs[-1]
        for cp in _peer_copies(srcs, lands_, send_sems, recv_sems, plan) + _own_copies(srcs, lands_, own_sems, plan):
            cp.start()
        token[...] = jnp.zeros_like(token)

    hbm = lambda a: pltpu.HBM(a.shape, a.dtype)
    res = pl.pallas_call(
        body, name=name,
        out_shape=(pltpu.SemaphoreType.DMA((n_sem,)), pltpu.SemaphoreType.DMA((n_sem,)), pltpu.SemaphoreType.DMA((n,)),
                   *[hbm(a) for a in arrays], *[hbm(a) for a in lands], _sds((8, 128))),
        in_specs=[_HBM] * (2 * n),
        out_specs=(_SEM, _SEM, _SEM, *[_HBM] * (2 * n), pl.BlockSpec(memory_space=pltpu.VMEM)),
        input_output_aliases={i: 3 + i for i in range(2 * n)},
        compiler_params=pltpu.CompilerParams(has_side_effects=_EFFECT),
    )(*[pltpu.with_memory_space_constraint(a, pltpu.HBM) for a in list(arrays) + lands])
    return (res[:3], res[3:3 + n], res[3 + n:3 + 2 * n], plan), res[-1]


def _exchange_wait(name, state, after):
    sems, srcs, lands, plan = state
    n = len(srcs)

    def body(*refs):
        srcs_, lands_ = refs[:n], refs[n:2 * n]
        send_sems, recv_sems, own_sems = refs[2 * n:2 * n + 3]
        for cp in _peer_copies(srcs_, lands_, send_sems, recv_sems, plan):
            cp.wait_send()
            cp.wait_recv()
        for cp in _own_copies(srcs_, lands_, own_sems, plan):
            cp.wait()

    hbm = lambda a: pltpu.HBM(a.shape, a.dtype)
    res = pl.pallas_call(
        body, name=name, out_shape=tuple(hbm(a) for a in list(srcs) + list(lands)),
        in_specs=[_HBM] * (2 * n) + [_SEM] * 3 + [pl.BlockSpec(memory_space=pl.ANY)] * len(after),
        out_specs=tuple([_HBM] * (2 * n)), input_output_aliases={i: i for i in range(2 * n)},
        compiler_params=pltpu.CompilerParams(has_side_effects=_EFFECT),
    )(*srcs, *lands, *sems, *after)
    return res[n:]


def _gather_start(name, arrays, plan, after):
    n = len(arrays)
    lands = plan.landing_zones(arrays)
    n_sem = n * (N_DEV - 1)

    def body(*refs):
        srcs, lands_ = refs[:n], refs[n:2 * n]
        send_sems, recv_sems, own_sems = refs[2 * n + len(after):2 * n + len(after) + 3]
        token = refs[-1]
        for cp in (_peer_copies(srcs, lands_, send_sems, recv_sems, plan, (_SIBLING,) + _SAME_CORE)
                   + _own_copies(srcs, lands_, own_sems, plan)):
            cp.start()
        token[...] = jnp.zeros_like(token)

    hbm = lambda a: pltpu.HBM(a.shape, a.dtype)
    res = pl.pallas_call(
        body, name=name,
        out_shape=(pltpu.SemaphoreType.DMA((n_sem,)), pltpu.SemaphoreType.DMA((n_sem,)), pltpu.SemaphoreType.DMA((n,)),
                   *[hbm(a) for a in arrays], *[hbm(a) for a in lands], _sds((8, 128))),
        in_specs=[_HBM] * (2 * n) + [pl.BlockSpec(memory_space=pl.ANY)] * len(after),
        out_specs=(_SEM, _SEM, _SEM, *[_HBM] * (2 * n), pl.BlockSpec(memory_space=pltpu.VMEM)),
        input_output_aliases={i: 3 + i for i in range(2 * n)},
        compiler_params=pltpu.CompilerParams(has_side_effects=_EFFECT),
    )(*[pltpu.with_memory_space_constraint(a, pltpu.HBM) for a in list(arrays) + lands], *after)
    return (res[:3], res[3:3 + n], res[3 + n:3 + 2 * n], plan), res[-1]


def _pass_on_copies(lands, send_sems, recv_sems, plan):
    x, y, c, _ = _mesh_pos()
    sibling = _peer(x, y, c, _SIBLING)[0]
    cps = []
    for i, r in enumerate(_SAME_CORE):
        owner = _peer(x, y, c, r)[1]
        for a in range(len(lands)):
            k = a * len(_SAME_CORE) + i
            cps.append(pltpu.make_async_remote_copy(
                src_ref=plan.dst(lands, a, owner), dst_ref=plan.dst(lands, a, owner), send_sem=send_sems.at[k],
                recv_sem=recv_sems.at[k], device_id=sibling, device_id_type=pl.DeviceIdType.MESH))
    return cps


def _gather_pass_on(name, state, after):
    sems, srcs, lands, plan = state
    n = len(srcs)
    n_sem = n * len(_SAME_CORE)

    def body(*refs):
        srcs_, lands_ = refs[:n], refs[n:2 * n]
        send_sems, recv_sems = refs[2 * n], refs[2 * n + 1]
        on_send, on_recv = refs[2 * n + 3 + len(after)], refs[2 * n + 4 + len(after)]
        arrivals = _peer_copies(srcs_, lands_, send_sems, recv_sems, plan, _SAME_CORE)
        for arrived, on in zip(arrivals, _pass_on_copies(lands_, on_send, on_recv, plan)):
            arrived.wait_recv()
            on.start()

    hbm = lambda a: pltpu.HBM(a.shape, a.dtype)
    res = pl.pallas_call(
        body, name=name,
        out_shape=(pltpu.SemaphoreType.DMA((n_sem,)), pltpu.SemaphoreType.DMA((n_sem,)),
                   *[hbm(a) for a in list(srcs) + list(lands)]),
        in_specs=[_HBM] * (2 * n) + [_SEM] * 3 + [pl.BlockSpec(memory_space=pl.ANY)] * len(after),
        out_specs=(_SEM, _SEM, *[_HBM] * (2 * n)), input_output_aliases={i: 2 + i for i in range(2 * n)},
        compiler_params=pltpu.CompilerParams(has_side_effects=_EFFECT),
    )(*srcs, *lands, *sems, *after)
    return sems, res[:2], res[2:2 + n], res[2 + n:], plan


def _gather_wait(name, state, after):
    sems, on_sems, srcs, lands, plan = state
    n = len(srcs)

    def body(*refs):
        srcs_, lands_ = refs[:n], refs[n:2 * n]
        send_sems, recv_sems, own_sems, on_send, on_recv = refs[2 * n:2 * n + 5]
        for cp in _peer_copies(srcs_, lands_, send_sems, recv_sems, plan, (_SIBLING,)):
            cp.wait_recv()
        for cp in _peer_copies(srcs_, lands_, send_sems, recv_sems, plan, (_SIBLING,) + _SAME_CORE):
            cp.wait_send()
        for cp in _own_copies(srcs_, lands_, own_sems, plan):
            cp.wait()
        for cp in _pass_on_copies(lands_, on_send, on_recv, plan):
            cp.wait_send()
            cp.wait_recv()

    hbm = lambda a: pltpu.HBM(a.shape, a.dtype)
    res = pl.pallas_call(
        body, name=name, out_shape=tuple(hbm(a) for a in list(srcs) + list(lands)),
        in_specs=[_HBM] * (2 * n) + [_SEM] * 5 + [pl.BlockSpec(memory_space=pl.ANY)] * len(after),
        out_specs=tuple([_HBM] * (2 * n)), input_output_aliases={i: i for i in range(2 * n)},
        compiler_params=pltpu.CompilerParams(has_side_effects=_EFFECT),
    )(*srcs, *lands, *sems, *on_sems, *after)
    return res[n:]


def _adam(name, recv, w, m, v):
    shape = w.shape
    R, C = shape
    tr = R
    while N_DEV * tr * C * 4 > 6 * 1024 * 1024 and tr % 32 == 0:
        tr //= 2

    def body(recv_ref, w_ref, m_ref, v_ref, g_ref, d_ref, m2_ref, v2_ref):
        g = recv_ref[0].astype(F32)
        for j in range(1, N_DEV):
            g = g + recv_ref[j].astype(F32)
        w_, m_, v_ = w_ref[...], m_ref[...], v_ref[...]
        m2 = ADAM_B1 * m_ + (1.0 - ADAM_B1) * g
        v2 = ADAM_B2 * v_ + (1.0 - ADAM_B2) * (g * g)
        m_hat = m2 / (1.0 - ADAM_B1 ** ADAM_STEP)
        v_hat = v2 / (1.0 - ADAM_B2 ** ADAM_STEP)
        g_ref[...] = g
        d_ref[...] = -ADAM_LR * (m_hat / (jnp.sqrt(v_hat) + ADAM_EPS) + ADAM_WD * w_)
        m2_ref[...] = m2
        v2_ref[...] = v2

    blk = pl.BlockSpec((tr, C), lambda i: (i, 0))
    return _pcall(body, name=name, grid=(R // tr,),
                  in_specs=[pl.BlockSpec((N_DEV, tr, C), lambda i: (0, i, 0)), blk, blk, blk],
                  out_specs=[blk] * 4, out_shape=[_sds(shape)] * 4)(recv, w, m, v)


_NAMES = ['ffn1_pre_g', 'ffn1_w_in', 'ffn1_w_out', 'ffn1_post_g', 'mix_pre_g', 'w_in', 'w_fu', 'b_f', 'gla_norm_g',
          'w_pool', 'pool_scale', 'mem_norm_g', 'w_mem_kv', 'w_up_gla', 'w_up_pool', 'w_up_xattn', 'w_o', 'mix_post_g',
          'ffn2_pre_g', 'ffn2_w_in', 'ffn2_w_out', 'ffn2_post_g', 'final_g']
_SHARDED = ['ffn1_w_in', 'ffn1_w_out', 'w_in', 'w_fu', 'w_mem_kv', 'w_up_gla', 'w_up_pool', 'w_up_xattn', 'w_o',
            'ffn2_w_in', 'ffn2_w_out']
_COL_SHARDED = ['w_up_pool', 'w_up_xattn']


def _cols_to_full(g):
    return jnp.transpose(g, (1, 0, 2)).reshape(g.shape[1], N_DEV * g.shape[2])


def _full_to_cols(f):
    R, C = f.shape
    return jnp.transpose(f.reshape(R, N_DEV, C // N_DEV), (1, 0, 2))


def _to_internal(w_in_t):
    o = 0
    parts = []
    for s in IN_SPLITS:
        parts.append(w_in_t[o:o + s])
        o += s
    q, k, v, g_out, f_low, p_in, xq, gates = parts
    f_low = jnp.pad(f_low, ((0, FLOW_W - GATE_RANK), (0, 0)))
    return jnp.concatenate([q, k, v, g_out, gates, p_in, xq, f_low], axis=0)


def _from_internal(d):
    q, k, v, g_out = d[0:512], d[512:1024], d[1024:2048], d[2048:3072]
    gates, p_in, xq = d[3072:6144], d[6144:6656], d[6656:7168]
    f_low = d[7168:7168 + GATE_RANK]
    return jnp.concatenate([q, k, v, g_out, f_low, p_in, xq, gates], axis=0)


_FFN_IN = ('ffn1_w_in', 'ffn2_w_in')
_FFN_OUT = ('ffn1_w_out', 'ffn2_w_out')
_GATHERS = {"ffn1_in": ['ffn1_w_in'], "ffn1_out": ['ffn1_w_out'],
            "mix": ['w_in', 'w_fu', 'w_mem_kv', 'w_up_gla', 'w_up_pool', 'w_up_xattn', 'w_o'],
            "ffn2": ['ffn2_w_in', 'ffn2_w_out']}


def _ffn_in_slot(ref, d):
    return ref.at[2 * (d % N_FF_BLK) + d // N_FF_BLK, pl.ds(0, FF_BLK)]


def _ffn_out_slot(ref, d):
    rows = FF_BLK // 2
    return ref.at[d // 2, pl.ds(pl.multiple_of((d % 2) * rows, rows), rows)]


def _ffn_plan(names, scatter):
    slots, shapes = {}, {}
    for a, n in enumerate(names):
        if n in _FFN_IN:
            slots[a] = _ffn_in_slot
            shapes[a] = (FF_BLK, D_MODEL) if scatter else (N_DEV, FF_PAD, D_MODEL)
        elif n in _FFN_OUT:
            slots[a] = _ffn_out_slot
            shapes[a] = (FF_BLK // 2, D_MODEL) if scatter else (N_FF_BLK, FF_PAD, D_MODEL)
    return _Plan(scatter, slots, shapes)


def _rows_to_full(g):
    return g.reshape(N_DEV * g.shape[1], g.shape[2])


def _mix_weights(gathered):
    W = {n: _rows_to_full(gathered[n]) for n in ('w_mem_kv', 'w_up_gla', 'w_o')}
    for n in _COL_SHARDED:
        W[n] = _cols_to_full(gathered[n])
    W["w_int_t"] = _to_internal(gathered["w_in"].reshape(IN_WIDTH, D_MODEL))
    W["w_fu_pad"] = jnp.pad(_cols_to_full(gathered["w_fu"]), ((0, FLOW_W - GATE_RANK), (0, 0)))
    return W


def _mix_chunks(G, n):
    if n == "w_in":
        return _from_internal(G["w_int_t"]).reshape(N_DEV, IN_SHARD, D_MODEL)
    if n == "w_fu":
        return _full_to_cols(G["w_fu_pad"][:GATE_RANK].astype(BF16))
    if n in _COL_SHARDED:
        return _full_to_cols(G[n])
    return G[n].reshape(N_DEV, G[n].shape[0] // N_DEV, G[n].shape[1])


def _step(x, mem, tgt, P, Mo, Vo):
    def native(n, a):
        return jnp.swapaxes(a, 0, 1) if n in _FFN_IN + ("w_in",) else a

    P, Mo, Vo = ({n: native(n, a) for n, a in d.items()} for d in (P, Mo, Vo))
    small = {n: P[n] for n in _NAMES if n not in _SHARDED}

    gather, token = {}, []
    for grp, names in _GATHERS.items():
        gather[grp], tok = _gather_start("gather_" + grp, [P[n].astype(BF16) for n in names], _ffn_plan(names, False),
                                         token)
        token = [tok]

    def gathered(grp, after):
        return _gather_wait("gather_" + grp + "_wait", _gather_pass_on("gather_" + grp + "_on", gather[grp], after), [])

    def ffn_wo(lands):
        return lambda act: lands[0].reshape(N_FF_BLK * FF_PAD, D_MODEL)

    w1t = gathered("ffn1_in", token)[0].reshape(N_DEV * FF_PAD, D_MODEL)
    x1, w1o, sv1 = _ffn_fwd("ffn1", x, small["ffn1_pre_g"], w1t, small["ffn1_post_g"],
                            lambda act: ffn_wo(gathered("ffn1_out", [act]))(act))

    gm = dict(zip(_GATHERS["mix"], gathered("mix", [x1])))
    Wm = {**small, **_mix_weights(gm)}
    x2, svm = _mix_fwd(x1, mem, Wm)

    w2t, w2o = gathered("ffn2", [x2])
    w2t = w2t.reshape(N_DEV * FF_PAD, D_MODEL)
    x3, w2o, sv2 = _ffn_fwd("ffn2", x2, small["ffn2_pre_g"], w2t, small["ffn2_post_g"], ffn_wo([w2o]))

    loss, dx3, d_final_g = _head(x3, tgt, small["final_g"])

    G = dict(final_g=d_final_g)
    scat = {}

    def start(grp, names, arrays):
        scat[grp] = names, _exchange_start("scatter_" + grp, arrays, _ffn_plan(names, True))
        return scat[grp][1][1]

    def ffn_starts(tag):
        return (lambda dwo: start(tag + "_out", [tag + "_w_out"], [dwo.reshape(N_FF_BLK, FF_PAD, D_MODEL)]),
                lambda dwt: start(tag + "_in", [tag + "_w_in"], [dwt.reshape(N_DEV, FF_PAD, D_MODEL)]))

    dx2, G["ffn2_pre_g"], G["ffn2_post_g"] = _ffn_bwd(
        "ffn2", dx3, x2, small["ffn2_pre_g"], w2t, w2o, small["ffn2_post_g"], sv2, *ffn_starts("ffn2"))
    dx1, Gm = _mix_bwd(dx2, x1, mem, Wm, svm,
                       lambda Gm: start("mix", _GATHERS["mix"], [_mix_chunks(Gm, n) for n in _GATHERS["mix"]]))
    G.update(Gm)
    dx, G["ffn1_pre_g"], G["ffn1_post_g"] = _ffn_bwd(
        "ffn1", dx1, x, small["ffn1_pre_g"], w1t, w1o, small["ffn1_post_g"], sv1, *ffn_starts("ffn1"))

    recv, outs = {}, {}
    done = [dx]
    for grp in ["ffn2_out", "ffn2_in", "mix", "small", "ffn1_out", "ffn1_in"]:
        if grp == "small":
            names = list(small)
            recv.update(zip(names, _exchange("gather_small_grads", [G[n] for n in small], _GATHER, done)))
        else:
            names, (state, _) = scat[grp]
            recv.update(zip(names, _exchange_wait("scatter_" + grp + "_wait", state, done[-1:])))
        for n in names:
            shp = P[n].shape
            shp2 = shp if len(shp) == 2 else (shp[0] * shp[1], shp[2])
            res = _adam("adam_" + n, recv[n].reshape((N_DEV,) + shp2), P[n].reshape(shp2), Mo[n].reshape(shp2),
                        Vo[n].reshape(shp2))
            outs[n] = [native(n, t.reshape(shp)) for t in res]
            done.append(res[-1])
    return loss, dx, outs


def kernel(x, mem, ffn1_pre_g, ffn1_w_in, ffn1_w_out, ffn1_post_g, mix_pre_g, w_in, w_fu, b_f, gla_norm_g, w_pool, pool_scale, mem_norm_g, w_mem_kv, w_up_gla, w_up_pool, w_up_xattn, w_o, mix_post_g, ffn2_pre_g, ffn2_w_in, ffn2_w_out, ffn2_post_g, final_g, loss_target, m_ffn1_pre_g, m_ffn1_w_in, m_ffn1_w_out, m_ffn1_post_g, m_mix_pre_g, m_w_in, m_w_fu, m_b_f, m_gla_norm_g, m_w_pool, m_pool_scale, m_mem_norm_g, m_w_mem_kv, m_w_up_gla, m_w_up_pool, m_w_up_xattn, m_w_o, m_mix_post_g, m_ffn2_pre_g, m_ffn2_w_in, m_ffn2_w_out, m_ffn2_post_g, m_final_g, v_ffn1_pre_g, v_ffn1_w_in, v_ffn1_w_out, v_ffn1_post_g, v_mix_pre_g, v_w_in, v_w_fu, v_b_f, v_gla_norm_g, v_w_pool, v_pool_scale, v_mem_norm_g, v_w_mem_kv, v_w_up_gla, v_w_up_pool, v_w_up_xattn, v_w_o, v_mix_post_g, v_ffn2_pre_g, v_ffn2_w_in, v_ffn2_w_out, v_ffn2_post_g, v_final_g):
    params = [ffn1_pre_g, ffn1_w_in, ffn1_w_out, ffn1_post_g, mix_pre_g, w_in, w_fu, b_f, gla_norm_g, w_pool, pool_scale, mem_norm_g, w_mem_kv, w_up_gla, w_up_pool, w_up_xattn, w_o, mix_post_g, ffn2_pre_g, ffn2_w_in, ffn2_w_out, ffn2_post_g, final_g]
    moms = [m_ffn1_pre_g, m_ffn1_w_in, m_ffn1_w_out, m_ffn1_post_g, m_mix_pre_g, m_w_in, m_w_fu, m_b_f, m_gla_norm_g, m_w_pool, m_pool_scale, m_mem_norm_g, m_w_mem_kv, m_w_up_gla, m_w_up_pool, m_w_up_xattn, m_w_o, m_mix_post_g, m_ffn2_pre_g, m_ffn2_w_in, m_ffn2_w_out, m_ffn2_post_g, m_final_g]
    vars_ = [v_ffn1_pre_g, v_ffn1_w_in, v_ffn1_w_out, v_ffn1_post_g, v_mix_pre_g, v_w_in, v_w_fu, v_b_f, v_gla_norm_g, v_w_pool, v_pool_scale, v_mem_norm_g, v_w_mem_kv, v_w_up_gla, v_w_up_pool, v_w_up_xattn, v_w_o, v_mix_post_g, v_ffn2_pre_g, v_ffn2_w_in, v_ffn2_w_out, v_ffn2_post_g, v_final_g]
    P = {n: a[0] if a.ndim > 2 else a for n, a in zip(_NAMES, params)}
    Mo = {n: a[0] if a.ndim > 2 else a for n, a in zip(_NAMES, moms)}
    Vo = {n: a[0] if a.ndim > 2 else a for n, a in zip(_NAMES, vars_)}
    loss, dx, outs = _step(x[0], mem[0], loss_target[0], P, Mo, Vo)
    loss = lax.psum(loss, ("x", "y", "c"))
    out = [loss, dx[None]]
    for kind in range(4):
        for n, p in zip(_NAMES, params):
            out.append(outs[n][kind].reshape(p.shape))
    return tuple(out)
```

```python
import functools

import jax
import jax.numpy as jnp
from jax import lax
from jax.experimental import pallas as pl
from jax.experimental.pallas import tpu as pltpu

F32 = jnp.float32
BF16 = jnp.bfloat16

N_DEV = 8
D_MODEL = 1024
D_FF = 2816
FF_BLK = 2 * D_FF // N_DEV
N_FF_BLK = D_FF // FF_BLK
FF_PAD = 768
CHUNK = 64
GLA_HEADS = 4
GLA_DK = 512
GLA_DV = 1024
GLA_HDK = 128
GLA_HDV = 256
GATE_RANK = 16
GATE_TEMP = 16.0
POOL_WINDOWS = (2, 4, 8, 16)
POOL_W = 512
POOL_GD = 128
POOL_HALO = 16
XA_HEADS = 4
XA_HD = 128
XA_W = 512
EPS = 1e-6
IN_SPLITS = (GLA_DK, GLA_DK, GLA_DV, GLA_DV, GATE_RANK, POOL_W, XA_W, 3 * D_MODEL)
IN_WIDTH = sum(IN_SPLITS)
IN_SHARD = IN_WIDTH // N_DEV
INT_W = 3072 + 3072 + 1024 + 128
INT_NBLK = 3
INT_BLK = INT_W // INT_NBLK
FLOW_W = 128

ADAM_LR = 0.001
ADAM_B1 = 0.9
ADAM_B2 = 0.999
ADAM_EPS = 1e-08
ADAM_WD = 0.01
ADAM_STEP = 10

VMEM_LIMIT = 56 * 1024 * 1024

_NN = (((1,), (0,)), ((), ()))
_NT = (((1,), (1,)), ((), ()))
_TN = (((0,), (0,)), ((), ()))


def _pcall(body, *, name, grid, in_specs, out_specs, out_shape, scratch=()):
    return pl.pallas_call(
        body, name=name, grid=grid, in_specs=in_specs, out_specs=out_specs, out_shape=out_shape,
        scratch_shapes=list(scratch),
        compiler_params=pltpu.CompilerParams(dimension_semantics=("arbitrary",) * len(grid),
                                             vmem_limit_bytes=VMEM_LIMIT))


def _dot(a, b, dims=_NN):
    return lax.dot_general(a.astype(BF16), b.astype(BF16), dims, preferred_element_type=F32)


def _mm(name, a, b, *, grid, a_blk, a_map, b_blk, b_map, o_shape, o_blk, o_map, dims, out_dtype=F32, deps=()):
    nk = grid[2]

    def body(a_ref, b_ref, *rest):
        o_ref, scr = rest[len(deps)], rest[len(deps) + 1:]
        p = _dot(a_ref[...], b_ref[...], dims)
        if nk == 1:
            o_ref[...] = p.astype(o_ref.dtype)
        else:
            acc = scr[0]
            k = pl.program_id(2)

            @pl.when(k == 0)
            def _():
                acc[...] = p

            @pl.when(k > 0)
            def _():
                acc[...] += p

            @pl.when(k == nk - 1)
            def _():
                o_ref[...] = acc[...].astype(o_ref.dtype)

    acc_shape = tuple(d for d in o_blk if d is not None)
    return _pcall(body, name=name, grid=grid,
                  in_specs=[pl.BlockSpec(a_blk, a_map), pl.BlockSpec(b_blk, b_map)]
                  + [pl.BlockSpec(memory_space=pl.ANY)] * len(deps),
                  out_specs=pl.BlockSpec(o_blk, o_map),
                  out_shape=jax.ShapeDtypeStruct(o_shape, out_dtype),
                  scratch=[pltpu.VMEM(acc_shape, F32)] if nk > 1 else [])(a, b, *deps)


def _mm_nn(name, a, b, out_dtype=F32, tm=1024, tn=None):
    M, K = a.shape
    N = b.shape[1]
    tm, tn = min(tm, M), (tn or N)
    return _mm(name, a, b, grid=(N // tn, M // tm, 1), a_blk=(tm, K), a_map=lambda j, i, k: (i, 0),
               b_blk=(K, tn), b_map=lambda j, i, k: (0, j), o_shape=(M, N), o_blk=(tm, tn),
               o_map=lambda j, i, k: (i, j), dims=_NN, out_dtype=out_dtype)


def _mm_nt(name, a, b, out_dtype=F32, tm=1024):
    M, K = a.shape
    N = b.shape[0]
    tm = min(tm, M)
    return _mm(name, a, b, grid=(1, M // tm, 1), a_blk=(tm, K), a_map=lambda j, i, k: (i, 0),
               b_blk=(N, K), b_map=lambda j, i, k: (0, 0), o_shape=(M, N), o_blk=(tm, N),
               o_map=lambda j, i, k: (i, 0), dims=_NT, out_dtype=out_dtype)


def _mm_tn(name, a, b, out_dtype=BF16, ts=512, tn=None, tn_a=None, deps=()):
    S, M = a.shape
    N = b.shape[1]
    ts, tn, tn_a = min(ts, S), (tn or N), (tn_a or M)
    return _mm(name, a, b, grid=((N // tn) * (M // tn_a), 1, S // ts), a_blk=(ts, tn_a),
               a_map=lambda j, i, k: (k, j if tn_a < M else 0), b_blk=(ts, tn),
               b_map=lambda j, i, k: (k, j if tn < N else 0), o_shape=(M, N), o_blk=(tn_a, tn),
               o_map=lambda j, i, k: (j, 0) if tn_a < M else (0, j), dims=_TN, out_dtype=out_dtype, deps=deps)


class _Win:
    def __init__(self, arr, w, c):
        self.arr, self.w, self.c = arr, w, c


def _row_spec(x, tm):
    if isinstance(x, _Win):
        return x.arr, pl.BlockSpec((tm, x.w), functools.partial(lambda i, c: (i, c), c=x.c))
    if x.ndim == 3:
        return x, pl.BlockSpec((x.shape[0], tm, x.shape[2]), lambda i: (0, i, 0))
    return x, pl.BlockSpec((tm, x.shape[1]), lambda i: (i, 0))


def _rowwise(name, fn, rows, consts, outs, accs=(), tm=512):
    first = rows[0].arr if isinstance(rows[0], _Win) else rows[0]
    S = first.shape[1] if first.ndim == 3 else first.shape[0]
    tm = min(tm, S)
    n_in, n_out = len(rows) + len(consts), len(outs)
    arrays, in_specs = [], []
    for r in rows:
        arr, spec = _row_spec(r, tm)
        arrays.append(arr)
        in_specs.append(spec)
    for c in consts:
        arrays.append(c)
        in_specs.append(pl.BlockSpec(c.shape, functools.partial(lambda i, n: (0,) * n, n=c.ndim)))
    out_specs = [_row_spec(o, tm)[1] for o in outs]
    out_specs += [pl.BlockSpec(a.shape, functools.partial(lambda i, n: (0,) * n, n=len(a.shape))) for a in accs]

    def body(*refs):
        res = fn(*[r[...] for r in refs[:n_in]])
        if not isinstance(res, (tuple, list)):
            res = (res,)
        orefs = refs[n_in:]
        for r, v in zip(orefs[:n_out], res[:n_out]):
            r[...] = v.astype(r.dtype)
        i = pl.program_id(0)
        for r, v in zip(orefs[n_out:], res[n_out:]):
            @pl.when(i == 0)
            def _(r=r, v=v):
                r[...] = v.astype(r.dtype)

            @pl.when(i > 0)
            def _(r=r, v=v):
                r[...] += v.astype(r.dtype)

    res = _pcall(body, name=name, grid=(S // tm,), in_specs=in_specs, out_specs=out_specs,
                 out_shape=[jax.ShapeDtypeStruct(o.shape, o.dtype) for o in list(outs) + list(accs)])(*arrays)
    return res


def _sds(shape, dtype=F32):
    return jax.ShapeDtypeStruct(shape, dtype)


def _rms(x, g):
    return x * lax.rsqrt(jnp.mean(x * x, axis=-1, keepdims=True) + EPS) * g


def _rms_bwd(x, g, dy):
    _, vjp = jax.vjp(_rms, x, g)
    return vjp(dy)


def _sigmoid(x):
    return 0.5 * jnp.tanh(0.5 * x) + 0.5


def _silu(x):
    return x * _sigmoid(x)


def _log_sigmoid(f):
    return jnp.minimum(f, 0.0) - jnp.log(1.0 + jnp.exp(-jnp.abs(f)))


def _head_rms_gate(o, g_out, gn):
    parts = [_rms(o[:, h * GLA_HDV:(h + 1) * GLA_HDV], gn[:, h * GLA_HDV:(h + 1) * GLA_HDV]) for h in range(GLA_HEADS)]
    return jnp.concatenate(parts, axis=-1) * _silu(g_out.astype(F32))


def _merge(gates, ya, yb, yc):
    gates, ya, yb, yc = (t.astype(F32) for t in (gates, ya, yb, yc))
    return (_sigmoid(gates[:, :D_MODEL]) * ya + _sigmoid(gates[:, D_MODEL:2 * D_MODEL]) * yb
            + _sigmoid(gates[:, 2 * D_MODEL:]) * yc)


def _tri_dot(t, x):
    hi = x.astype(BF16)
    r1 = x - hi.astype(F32)
    mid = r1.astype(BF16)
    lo = (r1 - mid.astype(F32)).astype(BF16)
    d = functools.partial(lax.dot_general, dimension_numbers=_NN, preferred_element_type=F32)
    return d(t, hi) + d(t, mid) + d(t, lo)


def _ffn_fwd(tag, x, gpre, wt, gpost, get_wo):
    S = x.shape[0]
    h = _rowwise(tag + "_pre", lambda x, g: _rms(x, g), [x], [gpre], [_sds((S, D_MODEL), BF16)])[0]
    tm = min(512, S)

    def up_body(h_ref, w_ref, u_ref, act_ref):
        hh = h_ref[...]
        for j in range(N_FF_BLK):
            ab = _dot(hh, w_ref[2 * FF_PAD * j:2 * FF_PAD * (j + 1), :], _NT)
            u_ref[:, 2 * FF_PAD * j:2 * FF_PAD * (j + 1)] = ab.astype(BF16)
            act_ref[:, FF_PAD * j:FF_PAD * (j + 1)] = (_silu(ab[:, :FF_PAD]) * ab[:, FF_PAD:]).astype(BF16)

    u, act = _pcall(
        up_body, name=tag + "_up", grid=(S // tm,),
        in_specs=[pl.BlockSpec((tm, D_MODEL), lambda i: (i, 0)),
                  pl.BlockSpec(wt.shape, lambda i: (0, 0), pipeline_mode=pl.Buffered(1))],
        out_specs=[pl.BlockSpec((tm, N_DEV * FF_PAD), lambda i: (i, 0)),
                   pl.BlockSpec((tm, N_FF_BLK * FF_PAD), lambda i: (i, 0))],
        out_shape=[_sds((S, N_DEV * FF_PAD), BF16), _sds((S, N_FF_BLK * FF_PAD), BF16)],
    )(h, wt)

    def down_body(act_ref, w_ref, x_ref, g_ref, f_ref, xo_ref):
        f = _dot(act_ref[...], w_ref[...])
        f_ref[...] = f
        xo_ref[...] = x_ref[...] + 0.5 * _rms(f, g_ref[...])

    wo = get_wo(act)
    row = pl.BlockSpec((tm, D_MODEL), lambda i: (i, 0))
    f, xo = _pcall(
        down_body, name=tag + "_down", grid=(S // tm,),
        in_specs=[pl.BlockSpec((tm, N_FF_BLK * FF_PAD), lambda i: (i, 0)),
                  pl.BlockSpec(wo.shape, lambda i: (0, 0), pipeline_mode=pl.Buffered(1)), row,
                  pl.BlockSpec((1, D_MODEL), lambda i: (0, 0))],
        out_specs=[row, row], out_shape=[_sds((S, D_MODEL)), _sds((S, D_MODEL))],
    )(act, wo, x, gpost)
    return xo, wo, (h, u, act, f)


def _ffn_bwd(tag, dxo, x, gpre, wt, wo, gpost, saved, on_dwo=None, on_dwt=None):
    h, u, act, f = saved
    S = x.shape[0]
    tm = min(256, S)

    def dact_body(f_ref, dxo_ref, g_ref, w_ref, u_ref, df_ref, du_ref, dg_ref):
        i = pl.program_id(0)
        df, dg = _rms_bwd(f_ref[...], g_ref[...], 0.5 * dxo_ref[...])
        df = df.astype(BF16)
        df_ref[...] = df

        @pl.when(i == 0)
        def _():
            dg_ref[...] = dg

        @pl.when(i > 0)
        def _():
            dg_ref[...] += dg

        for j in range(N_FF_BLK):
            dact = _dot(df, w_ref[FF_PAD * j:FF_PAD * (j + 1), :], _NT)
            ab = u_ref[:, 2 * FF_PAD * j:2 * FF_PAD * (j + 1)].astype(F32)
            a, b = ab[:, :FF_PAD], ab[:, FF_PAD:]
            sg = _sigmoid(a)
            du_ref[:, 2 * FF_PAD * j:2 * FF_PAD * j + FF_PAD] = (dact * b * (sg * (1.0 + a * (1.0 - sg)))).astype(BF16)
            du_ref[:, 2 * FF_PAD * j + FF_PAD:2 * FF_PAD * (j + 1)] = (dact * (a * sg)).astype(BF16)

    row = pl.BlockSpec((tm, D_MODEL), lambda i: (i, 0))
    vec = pl.BlockSpec((1, D_MODEL), lambda i: (0, 0))
    u_spec = pl.BlockSpec((tm, N_DEV * FF_PAD), lambda i: (i, 0))
    df, du, dgpost = _pcall(
        dact_body, name=tag + "_dact", grid=(S // tm,),
        in_specs=[row, row, vec, pl.BlockSpec(wo.shape, lambda i: (0, 0), pipeline_mode=pl.Buffered(1)), u_spec],
        out_specs=[row, u_spec, vec],
        out_shape=[_sds((S, D_MODEL), BF16), _sds((S, N_DEV * FF_PAD), BF16), _sds((1, D_MODEL))])(f, dxo, gpost, wo, u)
    dwo = _mm_tn(tag + "_dwo", act, df, ts=2048, tn_a=FF_PAD)
    dwt = _mm_tn(tag + "_dwt", du, h, ts=2048, tn_a=FF_PAD, deps=[on_dwo(dwo)] if on_dwo is not None else [])
    dx, dgpre = _dh_pre_norm_bwd(tag + "_dh", du, wt, x, dxo, gpre, [on_dwt(dwt)] if on_dwt is not None else [])
    return dx, dgpre, dgpost


def _dh_pre_norm_bwd(name, dz, wt, x, dres, g, deps):
    S, K = dz.shape

    def body(dz_ref, w_ref, x_ref, dres_ref, g_ref, *rest):
        dx_ref, dg_ref = rest[len(deps):]
        i = pl.program_id(0)
        dx, dg = _rms_bwd(x_ref[...], g_ref[...], _dot(dz_ref[...], w_ref[...]))
        dx_ref[...] = dx + dres_ref[...]

        @pl.when(i == 0)
        def _():
            dg_ref[...] = dg

        @pl.when(i > 0)
        def _():
            dg_ref[...] += dg

    th = min(256, S)
    row = pl.BlockSpec((th, D_MODEL), lambda i: (i, 0))
    vec = pl.BlockSpec((1, D_MODEL), lambda i: (0, 0))
    return _pcall(
        body, name=name, grid=(S // th,),
        in_specs=[pl.BlockSpec((th, K), lambda i: (i, 0)),
                  pl.BlockSpec(wt.shape, lambda i: (0, 0), pipeline_mode=pl.Buffered(1)), row, row, vec]
        + [pl.BlockSpec(memory_space=pl.ANY)] * len(deps),
        out_specs=[row, vec], out_shape=[_sds((S, D_MODEL)), _sds((1, D_MODEL))])(dz, wt, x, dres, g, *deps)


GLA_G = 4


def _gla_tile_common(k, flow, wfu, bf):
    f = _dot(flow, wfu) + bf
    la = _log_sigmoid(f) * (1.0 / GATE_TEMP)
    tri = _tri_matrix(True)
    ws, ds = [], []
    for g in range(k.shape[0] // CHUNK):
        b = _tri_dot(tri, la[g * CHUNK:(g + 1) * CHUNK])
        b_end = b[CHUNK - 1:CHUNK, :]
        ws.append(jnp.exp(b_end - b))
        ds.append(jnp.exp(b_end))
    w = jnp.concatenate(ws, axis=0)
    return f, w, k * w, ds


def _tri_matrix(lower):
    r = lax.broadcasted_iota(jnp.int32, (CHUNK, CHUNK), 0)
    c = lax.broadcasted_iota(jnp.int32, (CHUNK, CHUNK), 1)
    return jnp.where((r >= c) if lower else (r <= c), 1.0, 0.0).astype(BF16)


def _heads():
    return [(slice(h * GLA_HDK, (h + 1) * GLA_HDK), slice(h * GLA_HDV, (h + 1) * GLA_HDV)) for h in range(GLA_HEADS)]


def _gla_fwd(proj, wfu, bf):
    S = proj.shape[0]
    G = min(GLA_G, S // CHUNK)
    T = G * CHUNK
    nc = S // CHUNK

    def body(q_ref, k_ref, v_ref, flow_ref, wfu_ref, bf_ref, o_ref, st_ref, state):
        @pl.when(pl.program_id(0) == 0)
        def _():
            state[...] = jnp.zeros_like(state)

        _, _, kt, ds = _gla_tile_common(k_ref[...], flow_ref[...], wfu_ref[...], bf_ref[...])
        q = q_ref[...].astype(F32) * (GLA_HDK ** -0.5)
        v = v_ref[...]
        rows = [slice(g * CHUNK, (g + 1) * CHUNK) for g in range(G)]
        kv = [[_dot(v[r, vs], kt[r, ks], _TN) for ks, vs in _heads()] for r in rows]
        st = [state[vs, :] for _, vs in _heads()]
        for g, r in enumerate(rows):
            outs = []
            for h, (ks, vs) in enumerate(_heads()):
                st[h] = st[h] * ds[g][:, ks] + kv[g][h]
                st_ref[g, vs, :] = st[h]
                outs.append(_dot(q[r, ks], st[h], _NT))
            o_ref[r, :] = jnp.concatenate(outs, axis=-1)
        for h, (_, vs) in enumerate(_heads()):
            state[vs, :] = st[h]

    return _pcall(
        body, name="gla_fwd", grid=(S // T,),
        in_specs=[pl.BlockSpec((T, GLA_DK), lambda c: (c, 0)), pl.BlockSpec((T, GLA_DK), lambda c: (c, 1)),
                  pl.BlockSpec((T, GLA_DV), lambda c: (c, 1)),
                  pl.BlockSpec((T, FLOW_W), lambda c: (c, (INT_W - FLOW_W) // FLOW_W)),
                  pl.BlockSpec(wfu.shape, lambda c: (0, 0)), pl.BlockSpec(bf.shape, lambda c: (0, 0))],
        out_specs=[pl.BlockSpec((T, GLA_DV), lambda c: (c, 0)),
                   pl.BlockSpec((G, GLA_DV, GLA_HDK), lambda c: (c, 0, 0))],
        out_shape=[_sds((S, GLA_DV)), _sds((nc, GLA_DV, GLA_HDK))],
        scratch=[pltpu.VMEM((GLA_DV, GLA_HDK), F32)])(proj, proj, proj, proj, wfu, bf)


def _gla_bwd(proj, wfu, bf, states, do):
    S = proj.shape[0]
    G = min(GLA_G, S // CHUNK)
    T = G * CHUNK
    nt = S // T

    def body(q_ref, k_ref, v_ref, flow_ref, wfu_ref, bf_ref, st_ref, stp_ref, do_ref,
             dq_ref, dk_ref, dv_ref, dflow_ref, dwfu_ref, dbf_ref, dstate):
        step = pl.program_id(0)

        @pl.when(step == 0)
        def _():
            dstate[...] = jnp.zeros_like(dstate)
            dwfu_ref[...] = jnp.zeros_like(dwfu_ref)
            dbf_ref[...] = jnp.zeros_like(dbf_ref)

        flow, wfu_v = flow_ref[...], wfu_ref[...]
        f, w, kt, ds = _gla_tile_common(k_ref[...], flow, wfu_v, bf_ref[...])
        q = q_ref[...].astype(F32) * (GLA_HDK ** -0.5)
        v, dout = v_ref[...], do_ref[...]
        rows = [slice(g * CHUNK, (g + 1) * CHUNK) for g in range(G)]
        dq = [jnp.concatenate([_dot(dout[r, vs], st_ref[g, vs, :]) for _, vs in _heads()], axis=-1)
              for g, r in enumerate(rows)]
        qdo = [[_dot(dout[r, vs], q[r, ks], _TN) for ks, vs in _heads()] for r in rows]
        dq_ref[...] = (jnp.concatenate(dq, axis=0) * (GLA_HDK ** -0.5)).astype(dq_ref.dtype)
        has_prev = (step < nt - 1).astype(F32)
        carry = [dstate[vs, :] for _, vs in _heads()]
        dkt, dv, dd = [None] * G, [None] * G, [None] * G
        for g in reversed(range(G)):
            r = rows[g]
            dkts, dvs, dds = [], [], []
            for h, (ks, vs) in enumerate(_heads()):
                dst = carry[h] + qdo[g][h]
                dkts.append(_dot(v[r, vs], dst))
                dvs.append(_dot(kt[r, ks], dst, _NT))
                st_prev = st_ref[g - 1, vs, :] if g > 0 else stp_ref[vs, :] * has_prev
                dds.append(jnp.sum(dst * st_prev, axis=0, keepdims=True))
                carry[h] = dst * ds[g][:, ks]
            dkt[g], dv[g], dd[g] = (jnp.concatenate(t, axis=-1) for t in (dkts, dvs, dds))
        for h, (_, vs) in enumerate(_heads()):
            dstate[vs, :] = carry[h]
        dkt = jnp.concatenate(dkt, axis=0)
        dv_ref[...] = jnp.concatenate(dv, axis=0).astype(dv_ref.dtype)
        dk_ref[...] = (dkt * w).astype(dk_ref.dtype)
        de = dkt * kt
        tri = _tri_matrix(False)
        dla = []
        for g, r in enumerate(rows):
            db_end = jnp.sum(de[r], axis=0, keepdims=True) + dd[g] * ds[g]
            dla.append(db_end - _tri_dot(tri, de[r]))
        df = jnp.concatenate(dla, axis=0) * (1.0 - _sigmoid(f)) * (1.0 / GATE_TEMP)
        dflow_ref[...] = _dot(df, wfu_v, _NT).astype(dflow_ref.dtype)
        dwfu_ref[...] += _dot(flow, df, _TN)
        dbf_ref[...] += jnp.sum(df, axis=0, keepdims=True)

    rt = lambda s: nt - 1 - s
    return _pcall(
        body, name="gla_bwd", grid=(nt,),
        in_specs=[pl.BlockSpec((T, GLA_DK), lambda s: (rt(s), 0)), pl.BlockSpec((T, GLA_DK), lambda s: (rt(s), 1)),
                  pl.BlockSpec((T, GLA_DV), lambda s: (rt(s), 1)),
                  pl.BlockSpec((T, FLOW_W), lambda s: (rt(s), (INT_W - FLOW_W) // FLOW_W)),
                  pl.BlockSpec(wfu.shape, lambda s: (0, 0)), pl.BlockSpec(bf.shape, lambda s: (0, 0)),
                  pl.BlockSpec((G, GLA_DV, GLA_HDK), lambda s: (rt(s), 0, 0)),
                  pl.BlockSpec((None, GLA_DV, GLA_HDK), lambda s: (jnp.maximum(rt(s) * G - 1, 0), 0, 0)),
                  pl.BlockSpec((T, GLA_DV), lambda s: (rt(s), 0))],
        out_specs=[pl.BlockSpec((T, GLA_DK), lambda s: (rt(s), 0)), pl.BlockSpec((T, GLA_DK), lambda s: (rt(s), 0)),
                   pl.BlockSpec((T, GLA_DV), lambda s: (rt(s), 0)), pl.BlockSpec((T, FLOW_W), lambda s: (rt(s), 0)),
                   pl.BlockSpec(wfu.shape, lambda s: (0, 0)), pl.BlockSpec(bf.shape, lambda s: (0, 0))],
        out_shape=[_sds((S, GLA_DK), BF16), _sds((S, GLA_DK), BF16), _sds((S, GLA_DV), BF16), _sds((S, FLOW_W), BF16),
                   _sds(wfu.shape), _sds(bf.shape)],
        scratch=[pltpu.VMEM((GLA_DV, GLA_HDK), F32)])(proj, proj, proj, proj, wfu, bf, states, states, do)


def _pool_counts(tm, i):
    t = (lax.broadcasted_iota(jnp.int32, (tm, POOL_GD), 0) + i * tm + 1).astype(F32)
    return [jnp.minimum(t, float(w)) for w in POOL_WINDOWS]


def _pool_fwd(proj, w_pool, pool_scale, tm=512):
    S = proj.shape[0]
    tm = min(tm, S // 2)
    col = (3072 + 3072) // POOL_W
    hb = tm // POOL_HALO

    def body(p_ref, halo_ref, wp_ref, sc_ref, mixed_ref, out_ref):
        i = pl.program_id(0)
        p = p_ref[...].astype(F32)
        halo = halo_ref[...].astype(F32) * (i > 0).astype(F32)
        ext = jnp.concatenate([halo, p], axis=0)
        n = tm + POOL_HALO
        sums, acc, k = {}, ext, 1
        while k < POOL_WINDOWS[-1]:
            acc = acc + pltpu.roll(acc, k, axis=0)
            k *= 2
            sums[k] = acc
        cnts = _pool_counts(tm, i)
        mixed, lin = [], []
        for g, w in enumerate(POOL_WINDOWS):
            ls = slice(g * POOL_GD, (g + 1) * POOL_GD)
            m = sums[w][POOL_HALO:n, ls] / cnts[g] - p[:, ls]
            mixed.append(m)
            lin.append(_dot(m, wp_ref[g]))
        mixed_ref[...] = jnp.concatenate(mixed, axis=-1)
        out_ref[...] = (jnp.concatenate(lin, axis=-1) * sc_ref[...]).astype(out_ref.dtype)

    return _pcall(
        body, name="pool_fwd", grid=(S // tm,),
        in_specs=[pl.BlockSpec((tm, POOL_W), lambda i: (i, col)),
                  pl.BlockSpec((POOL_HALO, POOL_W), lambda i: (jnp.maximum(i * hb - 1, 0), col)),
                  pl.BlockSpec(w_pool.shape, lambda i: (0, 0, 0)), pl.BlockSpec(pool_scale.shape, lambda i: (0, 0))],
        out_specs=[pl.BlockSpec((tm, POOL_W), lambda i: (i, 0)), pl.BlockSpec((tm, POOL_W), lambda i: (i, 0))],
        out_shape=[_sds((S, POOL_W)), _sds((S, POOL_W), BF16)])(proj, proj, w_pool, pool_scale)


def _pool_lin_bwd(dout, mixed, w_pool, pool_scale):
    S = dout.shape[0]

    def fn(dout, mixed, wp, sc):
        dlin = dout * sc
        dm, dwp, lin = [], [], []
        for g in range(len(POOL_WINDOWS)):
            ls = slice(g * POOL_GD, (g + 1) * POOL_GD)
            lin.append(_dot(mixed[:, ls], wp[g]))
            dm.append(_dot(dlin[:, ls], wp[g], _NT))
            dwp.append(_dot(mixed[:, ls], dlin[:, ls], _TN))
        dsc = jnp.sum(dout * jnp.concatenate(lin, axis=-1), axis=0, keepdims=True)
        return jnp.concatenate(dm, axis=-1), jnp.concatenate(dwp, axis=0), dsc

    return _rowwise("pool_lin_bwd", fn, [dout, mixed], [w_pool, pool_scale], [_sds((S, POOL_W))],
                    [_sds((len(POOL_WINDOWS) * POOL_GD, POOL_GD)), _sds((1, POOL_W))])


def _pool_win_bwd(dmixed, tm=512):
    S = dmixed.shape[0]
    tm = min(tm, S // 2)
    nt = S // tm
    hb = tm // POOL_HALO

    def body(dm_ref, halo_ref, dp_ref):
        i = pl.program_id(0)
        dm = dm_ref[...]
        halo = halo_ref[...] * (i < nt - 1).astype(F32)
        cnts = _pool_counts(tm, i)
        cnts_h = [c[:POOL_HALO] for c in _pool_counts(tm, i + 1)]
        r = jnp.concatenate([jnp.concatenate([dm[:, g * POOL_GD:(g + 1) * POOL_GD] / cnts[g] for g in range(4)], axis=-1),
                             jnp.concatenate([halo[:, g * POOL_GD:(g + 1) * POOL_GD] / cnts_h[g] for g in range(4)], axis=-1)],
                            axis=0)
        n = tm + POOL_HALO
        sums, acc, k = {}, r, 1
        while k < POOL_WINDOWS[-1]:
            acc = acc + pltpu.roll(acc, n - k, axis=0)
            k *= 2
            sums[k] = acc
        dp = [sums[w][:tm, g * POOL_GD:(g + 1) * POOL_GD] for g, w in enumerate(POOL_WINDOWS)]
        dp_ref[...] = (jnp.concatenate(dp, axis=-1) - dm).astype(dp_ref.dtype)

    return _pcall(
        body, name="pool_win_bwd", grid=(nt,),
        in_specs=[pl.BlockSpec((tm, POOL_W), lambda i: (i, 0)),
                  pl.BlockSpec((POOL_HALO, POOL_W), lambda i: (jnp.minimum((i + 1) * hb, S // POOL_HALO - 1), 0))],
        out_specs=pl.BlockSpec((tm, POOL_W), lambda i: (i, 0)),
        out_shape=_sds((S, POOL_W), BF16))(dmixed, dmixed)


def _xattn_probs(q, kv, h):
    hs = slice(h * XA_HD, (h + 1) * XA_HD)
    s = _dot(q[:, hs], kv[:, hs], _NT) * (XA_HD ** -0.5)
    s = s - jnp.max(s, axis=-1, keepdims=True)
    e = jnp.exp(s)
    return e / jnp.sum(e, axis=-1, keepdims=True)


def _xattn_fwd(proj, kv):
    S = proj.shape[0]

    def fn(q, kv):
        outs = []
        for h in range(XA_HEADS):
            p = _xattn_probs(q, kv, h)
            outs.append(_dot(p, kv[:, XA_W + h * XA_HD:XA_W + (h + 1) * XA_HD]))
        return jnp.concatenate(outs, axis=-1)

    return _rowwise("xattn_fwd", fn, [_Win(proj, XA_W, (3072 + 3072 + POOL_W) // XA_W)], [kv], [_sds((S, XA_W), BF16)])[0]


def _xattn_bwd(proj, kv, dxa):
    S = proj.shape[0]

    def fn(q, dxa, kv):
        dqs, dks, dvs = [], [], []
        for h in range(XA_HEADS):
            hs = slice(h * XA_HD, (h + 1) * XA_HD)
            vh = kv[:, XA_W + h * XA_HD:XA_W + (h + 1) * XA_HD]
            p = _xattn_probs(q, kv, h)
            dp = _dot(dxa[:, hs], vh, _NT)
            ds = p * (dp - jnp.sum(p * dp, axis=-1, keepdims=True)) * (XA_HD ** -0.5)
            dqs.append(_dot(ds, kv[:, hs]))
            dks.append(_dot(ds, q[:, hs], _TN))
            dvs.append(_dot(p, dxa[:, hs], _TN))
        return jnp.concatenate(dqs, axis=-1), jnp.concatenate(dks + dvs, axis=-1)

    return _rowwise("xattn_bwd", fn, [_Win(proj, XA_W, (3072 + 3072 + POOL_W) // XA_W), dxa], [kv],
                    [_sds((S, XA_W), BF16)], [_sds(kv.shape)])


def _mix_fwd(x1, mem, W):
    S = x1.shape[0]
    M = mem.shape[0]
    h = _rowwise("mix_pre", lambda x, g: _rms(x, g), [x1], [W["mix_pre_g"]], [_sds((S, D_MODEL), BF16)])[0]
    tm = min(512, S)
    def proj_body(h_ref, w_ref, o_ref):
        hh = h_ref[...]
        for j in range(INT_NBLK):
            o_ref[:, INT_BLK * j:INT_BLK * (j + 1)] = _dot(hh, w_ref[INT_BLK * j:INT_BLK * (j + 1), :], _NT).astype(BF16)

    proj = _pcall(
        proj_body, name="mix_proj", grid=(S // tm,),
        in_specs=[pl.BlockSpec((tm, D_MODEL), lambda i: (i, 0)),
                  pl.BlockSpec((INT_W, D_MODEL), lambda i: (0, 0), pipeline_mode=pl.Buffered(1))],
        out_specs=pl.BlockSpec((tm, INT_W), lambda i: (i, 0)), out_shape=_sds((S, INT_W), BF16))(h, W["w_int_t"])
    o_raw, states = _gla_fwd(proj, W["w_fu_pad"], W["b_f"])
    ya_in = _rowwise("gla_out", _head_rms_gate, [o_raw, _Win(proj, GLA_DV, 2)], [W["gla_norm_g"]],
                     [_sds((S, GLA_DV), BF16)])[0]
    mixed, pool_out = _pool_fwd(proj, W["w_pool"], W["pool_scale"])
    mem_n = _rowwise("mem_norm", lambda m, g: _rms(m, g), [mem], [W["mem_norm_g"]], [_sds((M, D_MODEL), BF16)])[0]
    kv = _mm_nn("mem_kv", mem_n, W["w_mem_kv"])
    xa = _xattn_fwd(proj, kv)
    ya = _mm_nn("up_gla", ya_in, W["w_up_gla"], BF16)
    yb = _mm_nn("up_pool", pool_out, W["w_up_pool"], BF16)
    yc = _mm_nn("up_xattn", xa, W["w_up_xattn"], BF16)
    def out_fn(gates, ya, yb, yc, x1, wo, g):
        merged = _merge(gates, ya, yb, yc).astype(BF16)
        y = _dot(merged, wo)
        return merged, y, x1 + _rms(y, g)

    merged, y, x2 = _rowwise("mix_out", out_fn, [_Win(proj, 3 * D_MODEL, 1), ya, yb, yc, x1], [W["w_o"], W["mix_post_g"]],
                             [_sds((S, D_MODEL), BF16), _sds((S, D_MODEL)), _sds((S, D_MODEL))], tm=256)
    return x2, (h, proj, o_raw, states, ya_in, mixed, pool_out, mem_n, kv, xa, ya, yb, yc, merged, y)


def _head(x3, tgt, final_g):
    S = x3.shape[0]

    def head(x3, t, g):
        out, vjp = jax.vjp(_rms, x3, g)
        e = out - t
        loss = 0.5 * jnp.sum(jnp.mean(e * e, axis=-1, keepdims=True), axis=0, keepdims=True)
        dx3, dg = vjp(e * (1.0 / D_MODEL))
        return dx3, jnp.broadcast_to(loss, (1, 128)), dg

    dx3, loss_acc, d_final_g = _rowwise("head", head, [x3, tgt], [final_g], [_sds((S, D_MODEL))],
                                        [_sds((1, 128)), _sds((1, D_MODEL))])
    return loss_acc[0, 0], dx3, d_final_g


def _mix_bwd(dx2, x1, mem, W, saved, on_grads=None):
    h, proj, o_raw, states, ya_in, mixed, pool_out, mem_n, kv, xa, ya, yb, yc, merged, y = saved
    S = x1.shape[0]

    def out_bwd(y, dx2, gates, ya, yb, yc, g, wo):
        dy, dg = _rms_bwd(y, g, dx2)
        dy = dy.astype(BF16)
        _, vjp = jax.vjp(_merge, gates, ya, yb, yc)
        return (dy, *vjp(_dot(dy, wo, _NT)), dg)

    dy, dgates, dya, dyb, dyc, d_mix_post_g = _rowwise(
        "mix_out_bwd", out_bwd, [y, dx2, _Win(proj, 3 * D_MODEL, 1), ya, yb, yc], [W["mix_post_g"], W["w_o"]],
        [_sds((S, D_MODEL), BF16), _sds((S, 3 * D_MODEL), BF16)] + [_sds((S, D_MODEL), BF16)] * 3, [_sds((1, D_MODEL))],
        tm=256)
    d_w_o = _mm_tn("d_w_o", merged, dy)
    d_w_up_gla = _mm_tn("d_w_up_gla", ya_in, dya)
    d_w_up_pool = _mm_tn("d_w_up_pool", pool_out, dyb)
    d_w_up_xattn = _mm_tn("d_w_up_xattn", xa, dyc)
    d_ya_in = _mm_nt("d_ya_in", dya, W["w_up_gla"], BF16)
    d_pool_out = _mm_nt("d_pool_out", dyb, W["w_up_pool"], BF16)
    d_xa = _mm_nt("d_xa", dyc, W["w_up_xattn"], BF16)

    def gla_out_bwd(o, g_out, d, gn):
        _, vjp = jax.vjp(_head_rms_gate, o, g_out, gn)
        return vjp(d.astype(F32))

    do_raw, dg_out, d_gla_norm_g = _rowwise("gla_out_bwd", gla_out_bwd, [o_raw, _Win(proj, GLA_DV, 2), d_ya_in],
                                            [W["gla_norm_g"]], [_sds((S, GLA_DV)), _sds((S, GLA_DV), BF16)],
                                            [_sds((1, GLA_DV))])
    dq, dk, dv, dflow, d_wfu_pad, d_b_f = _gla_bwd(proj, W["w_fu_pad"], W["b_f"], states, do_raw)
    dmixed, d_w_pool, d_pool_scale = _pool_lin_bwd(d_pool_out, mixed, W["w_pool"], W["pool_scale"])
    dp_in = _pool_win_bwd(dmixed)
    dxq, dkv = _xattn_bwd(proj, kv, d_xa)
    d_w_mem_kv = _mm_tn("d_w_mem_kv", mem_n, dkv)
    dmem_n = _mm_nt("d_mem_n", dkv, W["w_mem_kv"])
    d_mem_norm_g = _rowwise("mem_norm_bwd", lambda m, d, g: _rms_bwd(m, g, d)[1], [mem, dmem_n], [W["mem_norm_g"]], [],
                            [_sds((1, D_MODEL))])[0]
    dproj = jnp.concatenate([dq, dk, dv, dg_out, dgates, dp_in, dxq, dflow], axis=-1)
    d_w_int_t = _mm_tn("d_w_int", dproj, h, ts=1024, tn_a=INT_BLK)
    grads = dict(
        w_int_t=d_w_int_t, w_fu_pad=d_wfu_pad, b_f=d_b_f, gla_norm_g=d_gla_norm_g, w_pool=d_w_pool,
        pool_scale=d_pool_scale, mem_norm_g=d_mem_norm_g, w_mem_kv=d_w_mem_kv, w_up_gla=d_w_up_gla,
        w_up_pool=d_w_up_pool, w_up_xattn=d_w_up_xattn, w_o=d_w_o, mix_post_g=d_mix_post_g)
    deps = [on_grads(grads)] if on_grads is not None else []
    dx1, grads["mix_pre_g"] = _dh_pre_norm_bwd("d_mix_h", dproj, W["w_int_t"], x1, dx2, W["mix_pre_g"], deps)
    return dx1, grads


def _mesh_pos():
    x, y, c = lax.axis_index("x"), lax.axis_index("y"), lax.axis_index("c")
    return x, y, c, 4 * x + 2 * y + c


def _peer(x, y, c, r):
    px = 1 - x if r & 4 else x
    py = 1 - y if r & 2 else y
    pc = 1 - c if r & 1 else c
    return (px, py, pc), 4 * px + 2 * py + pc


_ALL_PEERS = tuple(range(1, N_DEV))
_SIBLING = 1
_SAME_CORE = (2, 4, 6)


def _dev_slot(ref, dev):
    return ref.at[dev]


class _Plan:
    def __init__(self, scatter, slots=None, shapes=None):
        self.scatter, self.slots, self.shapes = scatter, slots or {}, shapes or {}

    def src(self, srcs, a, dev):
        return self.slots.get(a, _dev_slot)(srcs[a], dev) if self.scatter else srcs[a]

    def dst(self, lands, a, dev):
        return lands[a].at[dev] if self.scatter else self.slots.get(a, _dev_slot)(lands[a], dev)

    def landing_zones(self, arrays):
        lands = []
        for a, arr in enumerate(arrays):
            if self.scatter:
                lands.append(lax.empty((N_DEV,) + tuple(self.shapes.get(a, arr.shape[1:])), arr.dtype))
            elif a in self.shapes:
                lands.append(jnp.zeros(self.shapes[a], arr.dtype))
            else:
                lands.append(lax.empty((N_DEV,) + arr.shape, arr.dtype))
        return lands


_GATHER, _SCATTER = _Plan(False), _Plan(True)


def _peer_copies(srcs, lands, send_sems, recv_sems, plan, peers=_ALL_PEERS):
    x, y, c, me = _mesh_pos()
    cps = []
    for r in peers:
        pos, peer = _peer(x, y, c, r)
        for a in range(len(srcs)):
            k = a * (N_DEV - 1) + r - 1
            cps.append(pltpu.make_async_remote_copy(
                src_ref=plan.src(srcs, a, peer), dst_ref=plan.dst(lands, a, me),
                send_sem=send_sems.at[k], recv_sem=recv_sems.at[k], device_id=pos,
                device_id_type=pl.DeviceIdType.MESH))
    return cps


def _exchange(name, arrays, plan, after):
    n = len(arrays)

    def body(*refs):
        ins, outs = refs[:n], refs[n + len(after):2 * n + len(after)]
        send_sems, recv_sems, local_sems = refs[2 * n + len(after):]
        copies = _own_copies(ins, outs, local_sems, plan) + _peer_copies(ins, outs, send_sems, recv_sems, plan)
        for cp in copies:
            cp.start()
        for cp in copies:
            cp.wait()

    any_spec = pl.BlockSpec(memory_space=pl.ANY)
    out_shape = [_sds(l.shape, l.dtype) for l in plan.landing_zones(arrays)]
    return pl.pallas_call(
        body, name=name, in_specs=[any_spec] * (n + len(after)), out_specs=[any_spec] * n, out_shape=out_shape,
        scratch_shapes=[pltpu.SemaphoreType.DMA((n * (N_DEV - 1),)), pltpu.SemaphoreType.DMA((n * (N_DEV - 1),)),
                        pltpu.SemaphoreType.DMA((n,))])(*arrays, *after)


_HBM = pl.BlockSpec(memory_space=pltpu.HBM)
_SEM = pl.BlockSpec(memory_space=pltpu.SEMAPHORE)
_EFFECT = pltpu.SideEffectType.DATAFLOW_SIDE_EFFECTING


def _own_copies(srcs, lands, own_sems, plan):
    me = _mesh_pos()[3]
    return [pltpu.make_async_copy(plan.src(srcs, a, me), plan.dst(lands, a, me), own_sems.at[a])
            for a in range(len(srcs))]


def _exchange_start(name, arrays, plan):
    n = len(arrays)
    lands = plan.landing_zones(arrays)
    n_sem = n * (N_DEV - 1)

    def body(*refs):
        srcs, lands_ = refs[:n], refs[n:2 * n]
        send_sems, recv_sems, own_sems = refs[2 * n:2 * n + 3]
        token = refs[-1]
        for cp in _peer_copies(srcs, lands_, send_sems, recv_sems, plan) + _own_copies(srcs, lands_, own_sems, plan):
            cp.start()
        token[...] = jnp.zeros_like(token)

    hbm = lambda a: pltpu.HBM(a.shape, a.dtype)
    res = pl.pallas_call(
        body, name=name,
        out_shape=(pltpu.SemaphoreType.DMA((n_sem,)), pltpu.SemaphoreType.DMA((n_sem,)), pltpu.SemaphoreType.DMA((n,)),
                   *[hbm(a) for a in arrays], *[hbm(a) for a in lands], _sds((8, 128))),
        in_specs=[_HBM] * (2 * n),
        out_specs=(_SEM, _SEM, _SEM, *[_HBM] * (2 * n), pl.BlockSpec(memory_space=pltpu.VMEM)),
        input_output_aliases={i: 3 + i for i in range(2 * n)},
        compiler_params=pltpu.CompilerParams(has_side_effects=_EFFECT),
    )(*[pltpu.with_memory_space_constraint(a, pltpu.HBM) for a in list(arrays) + lands])
    return (res[:3], res[3:3 + n], res[3 + n:3 + 2 * n], plan), res[-1]


def _exchange_wait(name, state, after):
    sems, srcs, lands, plan = state
    n = len(srcs)

    def body(*refs):
        srcs_, lands_ = refs[:n], refs[n:2 * n]
        send_sems, recv_sems, own_sems = refs[2 * n:2 * n + 3]
        for cp in _peer_copies(srcs_, lands_, send_sems, recv_sems, plan):
            cp.wait_send()
            cp.wait_recv()
        for cp in _own_copies(srcs_, lands_, own_sems, plan):
            cp.wait()

    hbm = lambda a: pltpu.HBM(a.shape, a.dtype)
    res = pl.pallas_call(
        body, name=name, out_shape=tuple(hbm(a) for a in list(srcs) + list(lands)),
        in_specs=[_HBM] * (2 * n) + [_SEM] * 3 + [pl.BlockSpec(memory_space=pl.ANY)] * len(after),
        out_specs=tuple([_HBM] * (2 * n)), input_output_aliases={i: i for i in range(2 * n)},
        compiler_params=pltpu.CompilerParams(has_side_effects=_EFFECT),
    )(*srcs, *lands, *sems, *after)
    return res[n:]


def _gather_start(name, arrays, plan, after):
    n = len(arrays)
    lands = plan.landing_zones(arrays)
    n_sem = n * (N_DEV - 1)

    def body(*refs):
        srcs, lands_ = refs[:n], refs[n:2 * n]
        send_sems, recv_sems, own_sems = refs[2 * n + len(after):2 * n + len(after) + 3]
        token = refs[-1]
        for cp in (_peer_copies(srcs, lands_, send_sems, recv_sems, plan, (_SIBLING,) + _SAME_CORE)
                   + _own_copies(srcs, lands_, own_sems, plan)):
            cp.start()
        token[...] = jnp.zeros_like(token)

    hbm = lambda a: pltpu.HBM(a.shape, a.dtype)
    res = pl.pallas_call(
        body, name=name,
        out_shape=(pltpu.SemaphoreType.DMA((n_sem,)), pltpu.SemaphoreType.DMA((n_sem,)), pltpu.SemaphoreType.DMA((n,)),
                   *[hbm(a) for a in arrays], *[hbm(a) for a in lands], _sds((8, 128))),
        in_specs=[_HBM] * (2 * n) + [pl.BlockSpec(memory_space=pl.ANY)] * len(after),
        out_specs=(_SEM, _SEM, _SEM, *[_HBM] * (2 * n), pl.BlockSpec(memory_space=pltpu.VMEM)),
        input_output_aliases={i: 3 + i for i in range(2 * n)},
        compiler_params=pltpu.CompilerParams(has_side_effects=_EFFECT),
    )(*[pltpu.with_memory_space_constraint(a, pltpu.HBM) for a in list(arrays) + lands], *after)
    return (res[:3], res[3:3 + n], res[3 + n:3 + 2 * n], plan), res[-1]


def _pass_on_copies(lands, send_sems, recv_sems, plan):
    x, y, c, _ = _mesh_pos()
    sibling = _peer(x, y, c, _SIBLING)[0]
    cps = []
    for i, r in enumerate(_SAME_CORE):
        owner = _peer(x, y, c, r)[1]
        for a in range(len(lands)):
            k = a * len(_SAME_CORE) + i
            cps.append(pltpu.make_async_remote_copy(
                src_ref=plan.dst(lands, a, owner), dst_ref=plan.dst(lands, a, owner), send_sem=send_sems.at[k],
                recv_sem=recv_sems.at[k], device_id=sibling, device_id_type=pl.DeviceIdType.MESH))
    return cps


def _gather_pass_on(name, state, after):
    sems, srcs, lands, plan = state
    n = len(srcs)
    n_sem = n * len(_SAME_CORE)

    def body(*refs):
        srcs_, lands_ = refs[:n], refs[n:2 * n]
        send_sems, recv_sems = refs[2 * n], refs[2 * n + 1]
        on_send, on_recv = refs[2 * n + 3 + len(after)], refs[2 * n + 4 + len(after)]
        arrivals = _peer_copies(srcs_, lands_, send_sems, recv_sems, plan, _SAME_CORE)
        for arrived, on in zip(arrivals, _pass_on_copies(lands_, on_send, on_recv, plan)):
            arrived.wait_recv()
            on.start()

    hbm = lambda a: pltpu.HBM(a.shape, a.dtype)
    res = pl.pallas_call(
        body, name=name,
        out_shape=(pltpu.SemaphoreType.DMA((n_sem,)), pltpu.SemaphoreType.DMA((n_sem,)),
                   *[hbm(a) for a in list(srcs) + list(lands)]),
        in_specs=[_HBM] * (2 * n) + [_SEM] * 3 + [pl.BlockSpec(memory_space=pl.ANY)] * len(after),
        out_specs=(_SEM, _SEM, *[_HBM] * (2 * n)), input_output_aliases={i: 2 + i for i in range(2 * n)},
        compiler_params=pltpu.CompilerParams(has_side_effects=_EFFECT),
    )(*srcs, *lands, *sems, *after)
    return sems, res[:2], res[2:2 + n], res[2 + n:], plan


def _gather_wait(name, state, after):
    sems, on_sems, srcs, lands, plan = state
    n = len(srcs)

    def body(*refs):
        srcs_, lands_ = refs[:n], refs[n:2 * n]
        send_sems, recv_sems, own_sems, on_send, on_recv = refs[2 * n:2 * n + 5]
        for cp in _peer_copies(srcs_, lands_, send_sems, recv_sems, plan, (_SIBLING,)):
            cp.wait_recv()
        for cp in _peer_copies(srcs_, lands_, send_sems, recv_sems, plan, (_SIBLING,) + _SAME_CORE):
            cp.wait_send()
        for cp in _own_copies(srcs_, lands_, own_sems, plan):
            cp.wait()
        for cp in _pass_on_copies(lands_, on_send, on_recv, plan):
            cp.wait_send()
            cp.wait_recv()

    hbm = lambda a: pltpu.HBM(a.shape, a.dtype)
    res = pl.pallas_call(
        body, name=name, out_shape=tuple(hbm(a) for a in list(srcs) + list(lands)),
        in_specs=[_HBM] * (2 * n) + [_SEM] * 5 + [pl.BlockSpec(memory_space=pl.ANY)] * len(after),
        out_specs=tuple([_HBM] * (2 * n)), input_output_aliases={i: i for i in range(2 * n)},
        compiler_params=pltpu.CompilerParams(has_side_effects=_EFFECT),
    )(*srcs, *lands, *sems, *on_sems, *after)
    return res[n:]


def _adam(name, recv, w, m, v):
    shape = w.shape
    R, C = shape
    tr = R
    while N_DEV * tr * C * 4 > 6 * 1024 * 1024 and tr % 32 == 0:
        tr //= 2

    def body(recv_ref, w_ref, m_ref, v_ref, g_ref, d_ref, m2_ref, v2_ref):
        g = recv_ref[0].astype(F32)
        for j in range(1, N_DEV):
            g = g + recv_ref[j].astype(F32)
        w_, m_, v_ = w_ref[...], m_ref[...], v_ref[...]
        m2 = ADAM_B1 * m_ + (1.0 - ADAM_B1) * g
        v2 = ADAM_B2 * v_ + (1.0 - ADAM_B2) * (g * g)
        m_hat = m2 / (1.0 - ADAM_B1 ** ADAM_STEP)
        v_hat = v2 / (1.0 - ADAM_B2 ** ADAM_STEP)
        g_ref[...] = g
        d_ref[...] = -ADAM_LR * (m_hat / (jnp.sqrt(v_hat) + ADAM_EPS) + ADAM_WD * w_)
        m2_ref[...] = m2
        v2_ref[...] = v2

    blk = pl.BlockSpec((tr, C), lambda i: (i, 0))
    return _pcall(body, name=name, grid=(R // tr,),
                  in_specs=[pl.BlockSpec((N_DEV, tr, C), lambda i: (0, i, 0)), blk, blk, blk],
                  out_specs=[blk] * 4, out_shape=[_sds(shape)] * 4)(recv, w, m, v)


_NAMES = ['ffn1_pre_g', 'ffn1_w_in', 'ffn1_w_out', 'ffn1_post_g', 'mix_pre_g', 'w_in', 'w_fu', 'b_f', 'gla_norm_g',
          'w_pool', 'pool_scale', 'mem_norm_g', 'w_mem_kv', 'w_up_gla', 'w_up_pool', 'w_up_xattn', 'w_o', 'mix_post_g',
          'ffn2_pre_g', 'ffn2_w_in', 'ffn2_w_out', 'ffn2_post_g', 'final_g']
_SHARDED = ['ffn1_w_in', 'ffn1_w_out', 'w_in', 'w_fu', 'w_mem_kv', 'w_up_gla', 'w_up_pool', 'w_up_xattn', 'w_o',
            'ffn2_w_in', 'ffn2_w_out']
_COL_SHARDED = ['w_up_pool', 'w_up_xattn']


def _cols_to_full(g):
    return jnp.transpose(g, (1, 0, 2)).reshape(g.shape[1], N_DEV * g.shape[2])


def _full_to_cols(f):
    R, C = f.shape
    return jnp.transpose(f.reshape(R, N_DEV, C // N_DEV), (1, 0, 2))


def _to_internal(w_in_t):
    o = 0
    parts = []
    for s in IN_SPLITS:
        parts.append(w_in_t[o:o + s])
        o += s
    q, k, v, g_out, f_low, p_in, xq, gates = parts
    f_low = jnp.pad(f_low, ((0, FLOW_W - GATE_RANK), (0, 0)))
    return jnp.concatenate([q, k, v, g_out, gates, p_in, xq, f_low], axis=0)


def _from_internal(d):
    q, k, v, g_out = d[0:512], d[512:1024], d[1024:2048], d[2048:3072]
    gates, p_in, xq = d[3072:6144], d[6144:6656], d[6656:7168]
    f_low = d[7168:7168 + GATE_RANK]
    return jnp.concatenate([q, k, v, g_out, f_low, p_in, xq, gates], axis=0)


_FFN_IN = ('ffn1_w_in', 'ffn2_w_in')
_FFN_OUT = ('ffn1_w_out', 'ffn2_w_out')
_GATHERS = {"ffn1_in": ['ffn1_w_in'], "ffn1_out": ['ffn1_w_out'],
            "mix": ['w_in', 'w_fu', 'w_mem_kv', 'w_up_gla', 'w_up_pool', 'w_up_xattn', 'w_o'],
            "ffn2": ['ffn2_w_in', 'ffn2_w_out']}


def _ffn_in_slot(ref, d):
    return ref.at[2 * (d % N_FF_BLK) + d // N_FF_BLK, pl.ds(0, FF_BLK)]


def _ffn_out_slot(ref, d):
    rows = FF_BLK // 2
    return ref.at[d // 2, pl.ds(pl.multiple_of((d % 2) * rows, rows), rows)]


def _ffn_plan(names, scatter):
    slots, shapes = {}, {}
    for a, n in enumerate(names):
        if n in _FFN_IN:
            slots[a] = _ffn_in_slot
            shapes[a] = (FF_BLK, D_MODEL) if scatter else (N_DEV, FF_PAD, D_MODEL)
        elif n in _FFN_OUT:
            slots[a] = _ffn_out_slot
            shapes[a] = (FF_BLK // 2, D_MODEL) if scatter else (N_FF_BLK, FF_PAD, D_MODEL)
    return _Plan(scatter, slots, shapes)


def _rows_to_full(g):
    return g.reshape(N_DEV * g.shape[1], g.shape[2])


def _mix_weights(gathered):
    W = {n: _rows_to_full(gathered[n]) for n in ('w_mem_kv', 'w_up_gla', 'w_o')}
    for n in _COL_SHARDED:
        W[n] = _cols_to_full(gathered[n])
    W["w_int_t"] = _to_internal(gathered["w_in"].reshape(IN_WIDTH, D_MODEL))
    W["w_fu_pad"] = jnp.pad(_cols_to_full(gathered["w_fu"]), ((0, FLOW_W - GATE_RANK), (0, 0)))
    return W


def _mix_chunks(G, n):
    if n == "w_in":
        return _from_internal(G["w_int_t"]).reshape(N_DEV, IN_SHARD, D_MODEL)
    if n == "w_fu":
        return _full_to_cols(G["w_fu_pad"][:GATE_RANK].astype(BF16))
    if n in _COL_SHARDED:
        return _full_to_cols(G[n])
    return G[n].reshape(N_DEV, G[n].shape[0] // N_DEV, G[n].shape[1])


def _step(x, mem, tgt, P, Mo, Vo):
    def native(n, a):
        return jnp.swapaxes(a, 0, 1) if n in _FFN_IN + ("w_in",) else a

    P, Mo, Vo = ({n: native(n, a) for n, a in d.items()} for d in (P, Mo, Vo))
    small = {n: P[n] for n in _NAMES if n not in _SHARDED}

    gather, token = {}, []
    for grp, names in _GATHERS.items():
        gather[grp], tok = _gather_start("gather_" + grp, [P[n].astype(BF16) for n in names], _ffn_plan(names, False),
                                         token)
        token = [tok]

    def gathered(grp, after):
        return _gather_wait("gather_" + grp + "_wait", _gather_pass_on("gather_" + grp + "_on", gather[grp], after), [])

    def ffn_wo(lands):
        return lambda act: lands[0].reshape(N_FF_BLK * FF_PAD, D_MODEL)

    w1t = gathered("ffn1_in", token)[0].reshape(N_DEV * FF_PAD, D_MODEL)
    x1, w1o, sv1 = _ffn_fwd("ffn1", x, small["ffn1_pre_g"], w1t, small["ffn1_post_g"],
                            lambda act: ffn_wo(gathered("ffn1_out", [act]))(act))

    gm = dict(zip(_GATHERS["mix"], gathered("mix", [x1])))
    Wm = {**small, **_mix_weights(gm)}
    x2, svm = _mix_fwd(x1, mem, Wm)

    w2t, w2o = gathered("ffn2", [x2])
    w2t = w2t.reshape(N_DEV * FF_PAD, D_MODEL)
    x3, w2o, sv2 = _ffn_fwd("ffn2", x2, small["ffn2_pre_g"], w2t, small["ffn2_post_g"], ffn_wo([w2o]))

    loss, dx3, d_final_g = _head(x3, tgt, small["final_g"])

    G = dict(final_g=d_final_g)
    scat = {}

    def start(grp, names, arrays):
        scat[grp] = names, _exchange_start("scatter_" + grp, arrays, _ffn_plan(names, True))
        return scat[grp][1][1]

    def ffn_starts(tag):
        return (lambda dwo: start(tag + "_out", [tag + "_w_out"], [dwo.reshape(N_FF_BLK, FF_PAD, D_MODEL)]),
                lambda dwt: start(tag + "_in", [tag + "_w_in"], [dwt.reshape(N_DEV, FF_PAD, D_MODEL)]))

    dx2, G["ffn2_pre_g"], G["ffn2_post_g"] = _ffn_bwd(
        "ffn2", dx3, x2, small["ffn2_pre_g"], w2t, w2o, small["ffn2_post_g"], sv2, *ffn_starts("ffn2"))
    dx1, Gm = _mix_bwd(dx2, x1, mem, Wm, svm,
                       lambda Gm: start("mix", _GATHERS["mix"], [_mix_chunks(Gm, n) for n in _GATHERS["mix"]]))
    G.update(Gm)
    dx, G["ffn1_pre_g"], G["ffn1_post_g"] = _ffn_bwd(
        "ffn1", dx1, x, small["ffn1_pre_g"], w1t, w1o, small["ffn1_post_g"], sv1, *ffn_starts("ffn1"))

    recv, outs = {}, {}
    done = [dx]
    for grp in ["ffn2_out", "ffn2_in", "mix", "small", "ffn1_out", "ffn1_in"]:
        if grp == "small":
            names = list(small)
            recv.update(zip(names, _exchange("gather_small_grads", [G[n] for n in small], _GATHER, done)))
        else:
            names, (state, _) = scat[grp]
            recv.update(zip(names, _exchange_wait("scatter_" + grp + "_wait", state, done[-1:])))
        for n in names:
            shp = P[n].shape
            shp2 = shp if len(shp) == 2 else (shp[0] * shp[1], shp[2])
            res = _adam("adam_" + n, recv[n].reshape((N_DEV,) + shp2), P[n].reshape(shp2), Mo[n].reshape(shp2),
                        Vo[n].reshape(shp2))
            outs[n] = [native(n, t.reshape(shp)) for t in res]
            done.append(res[-1])
    return loss, dx, outs


def kernel(x, mem, ffn1_pre_g, ffn1_w_in, ffn1_w_out, ffn1_post_g, mix_pre_g, w_in, w_fu, b_f, gla_norm_g, w_pool, pool_scale, mem_norm_g, w_mem_kv, w_up_gla, w_up_pool, w_up_xattn, w_o, mix_post_g, ffn2_pre_g, ffn2_w_in, ffn2_w_out, ffn2_post_g, final_g, loss_target, m_ffn1_pre_g, m_ffn1_w_in, m_ffn1_w_out, m_ffn1_post_g, m_mix_pre_g, m_w_in, m_w_fu, m_b_f, m_gla_norm_g, m_w_pool, m_pool_scale, m_mem_norm_g, m_w_mem_kv, m_w_up_gla, m_w_up_pool, m_w_up_xattn, m_w_o, m_mix_post_g, m_ffn2_pre_g, m_ffn2_w_in, m_ffn2_w_out, m_ffn2_post_g, m_final_g, v_ffn1_pre_g, v_ffn1_w_in, v_ffn1_w_out, v_ffn1_post_g, v_mix_pre_g, v_w_in, v_w_fu, v_b_f, v_gla_norm_g, v_w_pool, v_pool_scale, v_mem_norm_g, v_w_mem_kv, v_w_up_gla, v_w_up_pool, v_w_up_xattn, v_w_o, v_mix_post_g, v_ffn2_pre_g, v_ffn2_w_in, v_ffn2_w_out, v_ffn2_post_g, v_final_g):
    params = [ffn1_pre_g, ffn1_w_in, ffn1_w_out, ffn1_post_g, mix_pre_g, w_in, w_fu, b_f, gla_norm_g, w_pool, pool_scale, mem_norm_g, w_mem_kv, w_up_gla, w_up_pool, w_up_xattn, w_o, mix_post_g, ffn2_pre_g, ffn2_w_in, ffn2_w_out, ffn2_post_g, final_g]
    moms = [m_ffn1_pre_g, m_ffn1_w_in, m_ffn1_w_out, m_ffn1_post_g, m_mix_pre_g, m_w_in, m_w_fu, m_b_f, m_gla_norm_g, m_w_pool, m_pool_scale, m_mem_norm_g, m_w_mem_kv, m_w_up_gla, m_w_up_pool, m_w_up_xattn, m_w_o, m_mix_post_g, m_ffn2_pre_g, m_ffn2_w_in, m_ffn2_w_out, m_ffn2_post_g, m_final_g]
    vars_ = [v_ffn1_pre_g, v_ffn1_w_in, v_ffn1_w_out, v_ffn1_post_g, v_mix_pre_g, v_w_in, v_w_fu, v_b_f, v_gla_norm_g, v_w_pool, v_pool_scale, v_mem_norm_g, v_w_mem_kv, v_w_up_gla, v_w_up_pool, v_w_up_xattn, v_w_o, v_mix_post_g, v_ffn2_pre_g, v_ffn2_w_in, v_ffn2_w_out, v_ffn2_post_g, v_final_g]
    P = {n: a[0] if a.ndim > 2 else a for n, a in zip(_NAMES, params)}
    Mo = {n: a[0] if a.ndim > 2 else a for n, a in zip(_NAMES, moms)}
    Vo = {n: a[0] if a.ndim > 2 else a for n, a in zip(_NAMES, vars_)}
    loss, dx, outs = _step(x[0], mem[0], loss_target[0], P, Mo, Vo)
    loss = lax.psum(loss, ("x", "y", "c"))
    out = [loss, dx[None]]
    for kind in range(4):
        for n, p in zip(_NAMES, params):
            out.append(outs[n][kind].reshape(p.shape))
    return tuple(out)
```

```python
import functools

import jax
import jax.numpy as jnp
from jax import lax
from jax.experimental import pallas as pl
from jax.experimental.pallas import tpu as pltpu

F32 = jnp.float32
BF16 = jnp.bfloat16

N_DEV = 8
D_MODEL = 1024
D_FF = 2816
FF_BLK = 2 * D_FF // N_DEV
N_FF_BLK = D_FF // FF_BLK
FF_PAD = 768
CHUNK = 64
GLA_HEADS = 4
GLA_DK = 512
GLA_DV = 1024
GLA_HDK = 128
GLA_HDV = 256
GATE_RANK = 16
GATE_TEMP = 16.0
POOL_WINDOWS = (2, 4, 8, 16)
POOL_W = 512
POOL_GD = 128
POOL_HALO = 16
XA_HEADS = 4
XA_HD = 128
XA_W = 512
EPS = 1e-6
IN_SPLITS = (GLA_DK, GLA_DK, GLA_DV, GLA_DV, GATE_RANK, POOL_W, XA_W, 3 * D_MODEL)
IN_WIDTH = sum(IN_SPLITS)
IN_SHARD = IN_WIDTH // N_DEV
INT_W = 3072 + 3072 + 1024 + 128
INT_NBLK = 3
INT_BLK = INT_W // INT_NBLK
FLOW_W = 128

ADAM_LR = 0.001
ADAM_B1 = 0.9
ADAM_B2 = 0.999
ADAM_EPS = 1e-08
ADAM_WD = 0.01
ADAM_STEP = 10

VMEM_LIMIT = 56 * 1024 * 1024

_NN = (((1,), (0,)), ((), ()))
_NT = (((1,), (1,)), ((), ()))
_TN = (((0,), (0,)), ((), ()))


def _pcall(body, *, name, grid, in_specs, out_specs, out_shape, scratch=()):
    return pl.pallas_call(
        body, name=name, grid=grid, in_specs=in_specs, out_specs=out_specs, out_shape=out_shape,
        scratch_shapes=list(scratch),
        compiler_params=pltpu.CompilerParams(dimension_semantics=("arbitrary",) * len(grid),
                                             vmem_limit_bytes=VMEM_LIMIT))


def _dot(a, b, dims=_NN):
    return lax.dot_general(a.astype(BF16), b.astype(BF16), dims, preferred_element_type=F32)


def _mm(name, a, b, *, grid, a_blk, a_map, b_blk, b_map, o_shape, o_blk, o_map, dims, out_dtype=F32, deps=()):
    nk = grid[2]

    def body(a_ref, b_ref, *rest):
        o_ref, scr = rest[len(deps)], rest[len(deps) + 1:]
        p = _dot(a_ref[...], b_ref[...], dims)
        if nk == 1:
            o_ref[...] = p.astype(o_ref.dtype)
        else:
            acc = scr[0]
            k = pl.program_id(2)

            @pl.when(k == 0)
            def _():
                acc[...] = p

            @pl.when(k > 0)
            def _():
                acc[...] += p

            @pl.when(k == nk - 1)
            def _():
                o_ref[...] = acc[...].astype(o_ref.dtype)

    acc_shape = tuple(d for d in o_blk if d is not None)
    return _pcall(body, name=name, grid=grid,
                  in_specs=[pl.BlockSpec(a_blk, a_map), pl.BlockSpec(b_blk, b_map)]
                  + [pl.BlockSpec(memory_space=pl.ANY)] * len(deps),
                  out_specs=pl.BlockSpec(o_blk, o_map),
                  out_shape=jax.ShapeDtypeStruct(o_shape, out_dtype),
                  scratch=[pltpu.VMEM(acc_shape, F32)] if nk > 1 else [])(a, b, *deps)


def _mm_nn(name, a, b, out_dtype=F32, tm=1024, tn=None):
    M, K = a.shape
    N = b.shape[1]
    tm, tn = min(tm, M), (tn or N)
    return _mm(name, a, b, grid=(N // tn, M // tm, 1), a_blk=(tm, K), a_map=lambda j, i, k: (i, 0),
               b_blk=(K, tn), b_map=lambda j, i, k: (0, j), o_shape=(M, N), o_blk=(tm, tn),
               o_map=lambda j, i, k: (i, j), dims=_NN, out_dtype=out_dtype)


def _mm_nt(name, a, b, out_dtype=F32, tm=1024):
    M, K = a.shape
    N = b.shape[0]
    tm = min(tm, M)
    return _mm(name, a, b, grid=(1, M // tm, 1), a_blk=(tm, K), a_map=lambda j, i, k: (i, 0),
               b_blk=(N, K), b_map=lambda j, i, k: (0, 0), o_shape=(M, N), o_blk=(tm, N),
               o_map=lambda j, i, k: (i, 0), dims=_NT, out_dtype=out_dtype)


def _mm_tn(name, a, b, out_dtype=BF16, ts=512, tn=None, tn_a=None, deps=()):
    S, M = a.shape
    N = b.shape[1]
    ts, tn, tn_a = min(ts, S), (tn or N), (tn_a or M)
    return _mm(name, a, b, grid=((N // tn) * (M // tn_a), 1, S // ts), a_blk=(ts, tn_a),
               a_map=lambda j, i, k: (k, j if tn_a < M else 0), b_blk=(ts, tn),
               b_map=lambda j, i, k: (k, j if tn < N else 0), o_shape=(M, N), o_blk=(tn_a, tn),
               o_map=lambda j, i, k: (j, 0) if tn_a < M else (0, j), dims=_TN, out_dtype=out_dtype, deps=deps)


class _Win:
    def __init__(self, arr, w, c):
        self.arr, self.w, self.c = arr, w, c


def _row_spec(x, tm):
    if isinstance(x, _Win):
        return x.arr, pl.BlockSpec((tm, x.w), functools.partial(lambda i, c: (i, c), c=x.c))
    if x.ndim == 3:
        return x, pl.BlockSpec((x.shape[0], tm, x.shape[2]), lambda i: (0, i, 0))
    return x, pl.BlockSpec((tm, x.shape[1]), lambda i: (i, 0))


def _rowwise(name, fn, rows, consts, outs, accs=(), tm=512):
    first = rows[0].arr if isinstance(rows[0], _Win) else rows[0]
    S = first.shape[1] if first.ndim == 3 else first.shape[0]
    tm = min(tm, S)
    n_in, n_out = len(rows) + len(consts), len(outs)
    arrays, in_specs = [], []
    for r in rows:
        arr, spec = _row_spec(r, tm)
        arrays.append(arr)
        in_specs.append(spec)
    for c in consts:
        arrays.append(c)
        in_specs.append(pl.BlockSpec(c.shape, functools.partial(lambda i, n: (0,) * n, n=c.ndim)))
    out_specs = [_row_spec(o, tm)[1] for o in outs]
    out_specs += [pl.BlockSpec(a.shape, functools.partial(lambda i, n: (0,) * n, n=len(a.shape))) for a in accs]

    def body(*refs):
        res = fn(*[r[...] for r in refs[:n_in]])
        if not isinstance(res, (tuple, list)):
            res = (res,)
        orefs = refs[n_in:]
        for r, v in zip(orefs[:n_out], res[:n_out]):
            r[...] = v.astype(r.dtype)
        i = pl.program_id(0)
        for r, v in zip(orefs[n_out:], res[n_out:]):
            @pl.when(i == 0)
            def _(r=r, v=v):
                r[...] = v.astype(r.dtype)

            @pl.when(i > 0)
            def _(r=r, v=v):
                r[...] += v.astype(r.dtype)

    res = _pcall(body, name=name, grid=(S // tm,), in_specs=in_specs, out_specs=out_specs,
                 out_shape=[jax.ShapeDtypeStruct(o.shape, o.dtype) for o in list(outs) + list(accs)])(*arrays)
    return res


def _sds(shape, dtype=F32):
    return jax.ShapeDtypeStruct(shape, dtype)


def _rms(x, g):
    return x * lax.rsqrt(jnp.mean(x * x, axis=-1, keepdims=True) + EPS) * g


def _rms_bwd(x, g, dy):
    _, vjp = jax.vjp(_rms, x, g)
    return vjp(dy)


def _sigmoid(x):
    return 0.5 * jnp.tanh(0.5 * x) + 0.5


def _silu(x):
    return x * _sigmoid(x)


def _log_sigmoid(f):
    return jnp.minimum(f, 0.0) - jnp.log(1.0 + jnp.exp(-jnp.abs(f)))


def _head_rms_gate(o, g_out, gn):
    parts = [_rms(o[:, h * GLA_HDV:(h + 1) * GLA_HDV], gn[:, h * GLA_HDV:(h + 1) * GLA_HDV]) for h in range(GLA_HEADS)]
    return jnp.concatenate(parts, axis=-1) * _silu(g_out.astype(F32))


def _merge(gates, ya, yb, yc):
    gates, ya, yb, yc = (t.astype(F32) for t in (gates, ya, yb, yc))
    return (_sigmoid(gates[:, :D_MODEL]) * ya + _sigmoid(gates[:, D_MODEL:2 * D_MODEL]) * yb
            + _sigmoid(gates[:, 2 * D_MODEL:]) * yc)


def _tri_dot(t, x):
    hi = x.astype(BF16)
    r1 = x - hi.astype(F32)
    mid = r1.astype(BF16)
    lo = (r1 - mid.astype(F32)).astype(BF16)
    d = functools.partial(lax.dot_general, dimension_numbers=_NN, preferred_element_type=F32)
    return d(t, hi) + d(t, mid) + d(t, lo)


def _pre_norm(name, x, g):
    return _rowwise(name, lambda x, g: _rms(x, g), [x], [g], [_sds(x.shape, BF16)])[0]


def _ffn_fwd(tag, x, h, wt, gpost, get_wo, g_next, tgt=None):
    S = x.shape[0]
    tm = min(512, S)

    def up_body(h_ref, w_ref, u_ref, act_ref):
        hh = h_ref[...]
        for j in range(N_FF_BLK):
            ab = _dot(hh, w_ref[2 * FF_PAD * j:2 * FF_PAD * (j + 1), :], _NT)
            u_ref[:, 2 * FF_PAD * j:2 * FF_PAD * (j + 1)] = ab.astype(BF16)
            act_ref[:, FF_PAD * j:FF_PAD * (j + 1)] = (_silu(ab[:, :FF_PAD]) * ab[:, FF_PAD:]).astype(BF16)

    u, act = _pcall(
        up_body, name=tag + "_up", grid=(S // tm,),
        in_specs=[pl.BlockSpec((tm, D_MODEL), lambda i: (i, 0)),
                  pl.BlockSpec(wt.shape, lambda i: (0, 0), pipeline_mode=pl.Buffered(1))],
        out_specs=[pl.BlockSpec((tm, N_DEV * FF_PAD), lambda i: (i, 0)),
                   pl.BlockSpec((tm, N_FF_BLK * FF_PAD), lambda i: (i, 0))],
        out_shape=[_sds((S, N_DEV * FF_PAD), BF16), _sds((S, N_FF_BLK * FF_PAD), BF16)],
    )(h, wt)

    def down_body(act_ref, w_ref, x_ref, g_ref, gn_ref, f_ref, xo_ref, hn_ref):
        f = _dot(act_ref[...], w_ref[...])
        f_ref[...] = f
        xo = x_ref[...] + 0.5 * _rms(f, g_ref[...])
        xo_ref[...] = xo
        hn_ref[...] = _rms(xo, gn_ref[...]).astype(BF16)

    def down_head_body(act_ref, w_ref, x_ref, g_ref, gn_ref, t_ref, f_ref, dxo_ref, loss_ref, dgn_ref):
        i = pl.program_id(0)
        f = _dot(act_ref[...], w_ref[...])
        f_ref[...] = f
        xo = x_ref[...] + 0.5 * _rms(f, g_ref[...])
        out, vjp = jax.vjp(_rms, xo, gn_ref[...])
        e = out - t_ref[...]
        loss = 0.5 * jnp.sum(jnp.mean(e * e, axis=-1, keepdims=True), axis=0, keepdims=True)
        dxo_ref[...], dgn = vjp(e * (1.0 / D_MODEL))

        @pl.when(i == 0)
        def _():
            loss_ref[...] = jnp.broadcast_to(loss, loss_ref.shape)
            dgn_ref[...] = dgn

        @pl.when(i > 0)
        def _():
            loss_ref[...] += jnp.broadcast_to(loss, loss_ref.shape)
            dgn_ref[...] += dgn

    wo = get_wo(act)
    row = pl.BlockSpec((tm, D_MODEL), lambda i: (i, 0))
    vec = pl.BlockSpec((1, D_MODEL), lambda i: (0, 0))
    if tgt is not None:
        f, dxo, loss, dg_next = _pcall(
            down_head_body, name=tag + "_down", grid=(S // tm,),
            in_specs=[pl.BlockSpec((tm, N_FF_BLK * FF_PAD), lambda i: (i, 0)),
                      pl.BlockSpec(wo.shape, lambda i: (0, 0), pipeline_mode=pl.Buffered(1)), row, vec, vec, row],
            out_specs=[row, row, pl.BlockSpec((1, 128), lambda i: (0, 0)), vec],
            out_shape=[_sds((S, D_MODEL)), _sds((S, D_MODEL)), _sds((1, 128)), _sds((1, D_MODEL))],
        )(act, wo, x, gpost, g_next, tgt)
        return (loss[0, 0], dxo, dg_next), wo, (h, u, act, f)
    f, xo, h_next = _pcall(
        down_body, name=tag + "_down", grid=(S // tm,),
        in_specs=[pl.BlockSpec((tm, N_FF_BLK * FF_PAD), lambda i: (i, 0)),
                  pl.BlockSpec(wo.shape, lambda i: (0, 0), pipeline_mode=pl.Buffered(1)), row, vec, vec],
        out_specs=[row, row, row], out_shape=[_sds((S, D_MODEL)), _sds((S, D_MODEL)), _sds((S, D_MODEL), BF16)],
    )(act, wo, x, gpost, g_next)
    return (xo, h_next), wo, (h, u, act, f)


def _ffn_bwd(tag, dxo, x, gpre, wt, wo, gpost, saved, on_dwo=None, on_dwt=None):
    h, u, act, f = saved
    S = x.shape[0]
    tm = min(256, S)

    def dact_body(f_ref, dxo_ref, g_ref, w_ref, u_ref, df_ref, du_ref, dg_ref):
        i = pl.program_id(0)
        df, dg = _rms_bwd(f_ref[...], g_ref[...], 0.5 * dxo_ref[...])
        df = df.astype(BF16)
        df_ref[...] = df

        @pl.when(i == 0)
        def _():
            dg_ref[...] = dg

        @pl.when(i > 0)
        def _():
            dg_ref[...] += dg

        for j in range(N_FF_BLK):
            dact = _dot(df, w_ref[FF_PAD * j:FF_PAD * (j + 1), :], _NT)
            ab = u_ref[:, 2 * FF_PAD * j:2 * FF_PAD * (j + 1)].astype(F32)
            a, b = ab[:, :FF_PAD], ab[:, FF_PAD:]
            sg = _sigmoid(a)
            du_ref[:, 2 * FF_PAD * j:2 * FF_PAD * j + FF_PAD] = (dact * b * (sg * (1.0 + a * (1.0 - sg)))).astype(BF16)
            du_ref[:, 2 * FF_PAD * j + FF_PAD:2 * FF_PAD * (j + 1)] = (dact * (a * sg)).astype(BF16)

    row = pl.BlockSpec((tm, D_MODEL), lambda i: (i, 0))
    vec = pl.BlockSpec((1, D_MODEL), lambda i: (0, 0))
    u_spec = pl.BlockSpec((tm, N_DEV * FF_PAD), lambda i: (i, 0))
    df, du, dgpost = _pcall(
        dact_body, name=tag + "_dact", grid=(S // tm,),
        in_specs=[row, row, vec, pl.BlockSpec(wo.shape, lambda i: (0, 0), pipeline_mode=pl.Buffered(1)), u_spec],
        out_specs=[row, u_spec, vec],
        out_shape=[_sds((S, D_MODEL), BF16), _sds((S, N_DEV * FF_PAD), BF16), _sds((1, D_MODEL))])(f, dxo, gpost, wo, u)
    dwo = _mm_tn(tag + "_dwo", act, df, ts=2048, tn_a=FF_PAD)
    dwt = _mm_tn(tag + "_dwt", du, h, ts=2048, tn_a=FF_PAD, deps=[on_dwo(dwo)] if on_dwo is not None else [])
    dx, dgpre = _dh_pre_norm_bwd(tag + "_dh", du, wt, x, dxo, gpre, [on_dwt(dwt)] if on_dwt is not None else [])
    return dx, dgpre, dgpost


def _dh_pre_norm_bwd(name, dz, wt, x, dres, g, deps):
    S, K = dz.shape

    def body(dz_ref, w_ref, x_ref, dres_ref, g_ref, *rest):
        dx_ref, dg_ref = rest[len(deps):]
        i = pl.program_id(0)
        dx, dg = _rms_bwd(x_ref[...], g_ref[...], _dot(dz_ref[...], w_ref[...]))
        dx_ref[...] = dx + dres_ref[...]

        @pl.when(i == 0)
        def _():
            dg_ref[...] = dg

        @pl.when(i > 0)
        def _():
            dg_ref[...] += dg

    th = min(256, S)
    row = pl.BlockSpec((th, D_MODEL), lambda i: (i, 0))
    vec = pl.BlockSpec((1, D_MODEL), lambda i: (0, 0))
    return _pcall(
        body, name=name, grid=(S // th,),
        in_specs=[pl.BlockSpec((th, K), lambda i: (i, 0)),
                  pl.BlockSpec(wt.shape, lambda i: (0, 0), pipeline_mode=pl.Buffered(1)), row, row, vec]
        + [pl.BlockSpec(memory_space=pl.ANY)] * len(deps),
        out_specs=[row, vec], out_shape=[_sds((S, D_MODEL)), _sds((1, D_MODEL))])(dz, wt, x, dres, g, *deps)


GLA_G = 8


def _gla_tile_common(k, flow, wfu, bf):
    f = _dot(flow, wfu) + bf
    la = _log_sigmoid(f) * (1.0 / GATE_TEMP)
    tri = _tri_matrix(True)
    ws, ds = [], []
    for g in range(k.shape[0] // CHUNK):
        b = _tri_dot(tri, la[g * CHUNK:(g + 1) * CHUNK])
        b_end = b[CHUNK - 1:CHUNK, :]
        ws.append(jnp.exp(b_end - b))
        ds.append(jnp.exp(b_end))
    w = jnp.concatenate(ws, axis=0)
    return f, w, k * w, ds


def _tri_matrix(lower):
    r = lax.broadcasted_iota(jnp.int32, (CHUNK, CHUNK), 0)
    c = lax.broadcasted_iota(jnp.int32, (CHUNK, CHUNK), 1)
    return jnp.where((r >= c) if lower else (r <= c), 1.0, 0.0).astype(BF16)


def _heads():
    return [(slice(h * GLA_HDK, (h + 1) * GLA_HDK), slice(h * GLA_HDV, (h + 1) * GLA_HDV)) for h in range(GLA_HEADS)]


def _gla_fwd(proj, wfu, bf):
    S = proj.shape[0]
    G = min(GLA_G, S // CHUNK)
    T = G * CHUNK
    nc = S // CHUNK

    def body(q_ref, k_ref, v_ref, flow_ref, wfu_ref, bf_ref, o_ref, st_ref, state):
        @pl.when(pl.program_id(0) == 0)
        def _():
            state[...] = jnp.zeros_like(state)

        _, _, kt, ds = _gla_tile_common(k_ref[...], flow_ref[...], wfu_ref[...], bf_ref[...])
        q = q_ref[...].astype(F32) * (GLA_HDK ** -0.5)
        v = v_ref[...]
        rows = [slice(g * CHUNK, (g + 1) * CHUNK) for g in range(G)]
        kv = [[_dot(v[r, vs], kt[r, ks], _TN) for ks, vs in _heads()] for r in rows]
        st = [state[vs, :] for _, vs in _heads()]
        for g, r in enumerate(rows):
            outs = []
            for h, (ks, vs) in enumerate(_heads()):
                st[h] = st[h] * ds[g][:, ks] + kv[g][h]
                st_ref[g, vs, :] = st[h]
                outs.append(_dot(q[r, ks], st[h], _NT))
            o_ref[r, :] = jnp.concatenate(outs, axis=-1)
        for h, (_, vs) in enumerate(_heads()):
            state[vs, :] = st[h]

    return _pcall(
        body, name="gla_fwd", grid=(S // T,),
        in_specs=[pl.BlockSpec((T, GLA_DK), lambda c: (c, 0)), pl.BlockSpec((T, GLA_DK), lambda c: (c, 1)),
                  pl.BlockSpec((T, GLA_DV), lambda c: (c, 1)),
                  pl.BlockSpec((T, FLOW_W), lambda c: (c, (INT_W - FLOW_W) // FLOW_W)),
                  pl.BlockSpec(wfu.shape, lambda c: (0, 0)), pl.BlockSpec(bf.shape, lambda c: (0, 0))],
        out_specs=[pl.BlockSpec((T, GLA_DV), lambda c: (c, 0)),
                   pl.BlockSpec((G, GLA_DV, GLA_HDK), lambda c: (c, 0, 0))],
        out_shape=[_sds((S, GLA_DV)), _sds((nc, GLA_DV, GLA_HDK))],
        scratch=[pltpu.VMEM((GLA_DV, GLA_HDK), F32)])(proj, proj, proj, proj, wfu, bf)


def _gla_bwd(proj, wfu, bf, states, do):
    S = proj.shape[0]
    G = min(GLA_G, S // CHUNK)
    T = G * CHUNK
    nt = S // T

    def body(q_ref, k_ref, v_ref, flow_ref, wfu_ref, bf_ref, st_ref, stp_ref, do_ref,
             dq_ref, dk_ref, dv_ref, dflow_ref, dwfu_ref, dbf_ref, dstate):
        step = pl.program_id(0)

        @pl.when(step == 0)
        def _():
            dstate[...] = jnp.zeros_like(dstate)
            dwfu_ref[...] = jnp.zeros_like(dwfu_ref)
            dbf_ref[...] = jnp.zeros_like(dbf_ref)

        flow, wfu_v = flow_ref[...], wfu_ref[...]
        f, w, kt, ds = _gla_tile_common(k_ref[...], flow, wfu_v, bf_ref[...])
        q = q_ref[...].astype(F32) * (GLA_HDK ** -0.5)
        v, dout = v_ref[...], do_ref[...]
        rows = [slice(g * CHUNK, (g + 1) * CHUNK) for g in range(G)]
        dq = [jnp.concatenate([_dot(dout[r, vs], st_ref[g, vs, :]) for _, vs in _heads()], axis=-1)
              for g, r in enumerate(rows)]
        qdo = [[_dot(dout[r, vs], q[r, ks], _TN) for ks, vs in _heads()] for r in rows]
        dq_ref[...] = (jnp.concatenate(dq, axis=0) * (GLA_HDK ** -0.5)).astype(dq_ref.dtype)
        has_prev = (step < nt - 1).astype(F32)
        carry = [dstate[vs, :] for _, vs in _heads()]
        dkt, dv, dd = [None] * G, [None] * G, [None] * G
        for g in reversed(range(G)):
            r = rows[g]
            dkts, dvs, dds = [], [], []
            for h, (ks, vs) in enumerate(_heads()):
                dst = carry[h] + qdo[g][h]
                dkts.append(_dot(v[r, vs], dst))
                dvs.append(_dot(kt[r, ks], dst, _NT))
                st_prev = st_ref[g - 1, vs, :] if g > 0 else stp_ref[vs, :] * has_prev
                dds.append(jnp.sum(dst * st_prev, axis=0, keepdims=True))
                carry[h] = dst * ds[g][:, ks]
            dkt[g], dv[g], dd[g] = (jnp.concatenate(t, axis=-1) for t in (dkts, dvs, dds))
        for h, (_, vs) in enumerate(_heads()):
            dstate[vs, :] = carry[h]
        dkt = jnp.concatenate(dkt, axis=0)
        dv_ref[...] = jnp.concatenate(dv, axis=0).astype(dv_ref.dtype)
        dk_ref[...] = (dkt * w).astype(dk_ref.dtype)
        de = dkt * kt
        tri = _tri_matrix(False)
        dla = []
        for g, r in enumerate(rows):
            db_end = jnp.sum(de[r], axis=0, keepdims=True) + dd[g] * ds[g]
            dla.append(db_end - _tri_dot(tri, de[r]))
        df = jnp.concatenate(dla, axis=0) * (1.0 - _sigmoid(f)) * (1.0 / GATE_TEMP)
        dflow_ref[...] = _dot(df, wfu_v, _NT).astype(dflow_ref.dtype)
        dwfu_ref[...] += _dot(flow, df, _TN)
        dbf_ref[...] += jnp.sum(df, axis=0, keepdims=True)

    rt = lambda s: nt - 1 - s
    return _pcall(
        body, name="gla_bwd", grid=(nt,),
        in_specs=[pl.BlockSpec((T, GLA_DK), lambda s: (rt(s), 0)), pl.BlockSpec((T, GLA_DK), lambda s: (rt(s), 1)),
                  pl.BlockSpec((T, GLA_DV), lambda s: (rt(s), 1)),
                  pl.BlockSpec((T, FLOW_W), lambda s: (rt(s), (INT_W - FLOW_W) // FLOW_W)),
                  pl.BlockSpec(wfu.shape, lambda s: (0, 0)), pl.BlockSpec(bf.shape, lambda s: (0, 0)),
                  pl.BlockSpec((G, GLA_DV, GLA_HDK), lambda s: (rt(s), 0, 0)),
                  pl.BlockSpec((None, GLA_DV, GLA_HDK), lambda s: (jnp.maximum(rt(s) * G - 1, 0), 0, 0)),
                  pl.BlockSpec((T, GLA_DV), lambda s: (rt(s), 0))],
        out_specs=[pl.BlockSpec((T, GLA_DK), lambda s: (rt(s), 0)), pl.BlockSpec((T, GLA_DK), lambda s: (rt(s), 0)),
                   pl.BlockSpec((T, GLA_DV), lambda s: (rt(s), 0)), pl.BlockSpec((T, FLOW_W), lambda s: (rt(s), 0)),
                   pl.BlockSpec(wfu.shape, lambda s: (0, 0)), pl.BlockSpec(bf.shape, lambda s: (0, 0))],
        out_shape=[_sds((S, GLA_DK), BF16), _sds((S, GLA_DK), BF16), _sds((S, GLA_DV), BF16), _sds((S, FLOW_W), BF16),
                   _sds(wfu.shape), _sds(bf.shape)],
        scratch=[pltpu.VMEM((GLA_DV, GLA_HDK), F32)])(proj, proj, proj, proj, wfu, bf, states, states, do)


def _pool_counts(tm, i):
    t = (lax.broadcasted_iota(jnp.int32, (tm, POOL_GD), 0) + i * tm + 1).astype(F32)
    return [jnp.minimum(t, float(w)) for w in POOL_WINDOWS]


def _pool_fwd(proj, w_pool, pool_scale, tm=512):
    S = proj.shape[0]
    tm = min(tm, S // 2)
    col = (3072 + 3072) // POOL_W
    hb = tm // POOL_HALO

    def body(p_ref, halo_ref, wp_ref, sc_ref, mixed_ref, out_ref):
        i = pl.program_id(0)
        p = p_ref[...].astype(F32)
        halo = halo_ref[...].astype(F32) * (i > 0).astype(F32)
        ext = jnp.concatenate([halo, p], axis=0)
        n = tm + POOL_HALO
        sums, acc, k = {}, ext, 1
        while k < POOL_WINDOWS[-1]:
            acc = acc + pltpu.roll(acc, k, axis=0)
            k *= 2
            sums[k] = acc
        cnts = _pool_counts(tm, i)
        mixed, lin = [], []
        for g, w in enumerate(POOL_WINDOWS):
            ls = slice(g * POOL_GD, (g + 1) * POOL_GD)
            m = sums[w][POOL_HALO:n, ls] / cnts[g] - p[:, ls]
            mixed.append(m)
            lin.append(_dot(m, wp_ref[g]))
        mixed_ref[...] = jnp.concatenate(mixed, axis=-1)
        out_ref[...] = (jnp.concatenate(lin, axis=-1) * sc_ref[...]).astype(out_ref.dtype)

    return _pcall(
        body, name="pool_fwd", grid=(S // tm,),
        in_specs=[pl.BlockSpec((tm, POOL_W), lambda i: (i, col)),
                  pl.BlockSpec((POOL_HALO, POOL_W), lambda i: (jnp.maximum(i * hb - 1, 0), col)),
                  pl.BlockSpec(w_pool.shape, lambda i: (0, 0, 0)), pl.BlockSpec(pool_scale.shape, lambda i: (0, 0))],
        out_specs=[pl.BlockSpec((tm, POOL_W), lambda i: (i, 0)), pl.BlockSpec((tm, POOL_W), lambda i: (i, 0))],
        out_shape=[_sds((S, POOL_W)), _sds((S, POOL_W), BF16)])(proj, proj, w_pool, pool_scale)


def _pool_lin_bwd(dout, mixed, w_pool, pool_scale):
    S = dout.shape[0]

    def fn(dout, mixed, wp, sc):
        dlin = dout * sc
        dm, dwp, lin = [], [], []
        for g in range(len(POOL_WINDOWS)):
            ls = slice(g * POOL_GD, (g + 1) * POOL_GD)
            lin.append(_dot(mixed[:, ls], wp[g]))
            dm.append(_dot(dlin[:, ls], wp[g], _NT))
            dwp.append(_dot(mixed[:, ls], dlin[:, ls], _TN))
        dsc = jnp.sum(dout * jnp.concatenate(lin, axis=-1), axis=0, keepdims=True)
        return jnp.concatenate(dm, axis=-1), jnp.concatenate(dwp, axis=0), dsc

    return _rowwise("pool_lin_bwd", fn, [dout, mixed], [w_pool, pool_scale], [_sds((S, POOL_W))],
                    [_sds((len(POOL_WINDOWS) * POOL_GD, POOL_GD)), _sds((1, POOL_W))])


def _pool_win_bwd(dmixed, tm=512):
    S = dmixed.shape[0]
    tm = min(tm, S // 2)
    nt = S // tm
    hb = tm // POOL_HALO

    def body(dm_ref, halo_ref, dp_ref):
        i = pl.program_id(0)
        dm = dm_ref[...]
        halo = halo_ref[...] * (i < nt - 1).astype(F32)
        cnts = _pool_counts(tm, i)
        cnts_h = [c[:POOL_HALO] for c in _pool_counts(tm, i + 1)]
        r = jnp.concatenate([jnp.concatenate([dm[:, g * POOL_GD:(g + 1) * POOL_GD] / cnts[g] for g in range(4)], axis=-1),
                             jnp.concatenate([halo[:, g * POOL_GD:(g + 1) * POOL_GD] / cnts_h[g] for g in range(4)], axis=-1)],
                            axis=0)
        n = tm + POOL_HALO
        sums, acc, k = {}, r, 1
        while k < POOL_WINDOWS[-1]:
            acc = acc + pltpu.roll(acc, n - k, axis=0)
            k *= 2
            sums[k] = acc
        dp = [sums[w][:tm, g * POOL_GD:(g + 1) * POOL_GD] for g, w in enumerate(POOL_WINDOWS)]
        dp_ref[...] = (jnp.concatenate(dp, axis=-1) - dm).astype(dp_ref.dtype)

    return _pcall(
        body, name="pool_win_bwd", grid=(nt,),
        in_specs=[pl.BlockSpec((tm, POOL_W), lambda i: (i, 0)),
                  pl.BlockSpec((POOL_HALO, POOL_W), lambda i: (jnp.minimum((i + 1) * hb, S // POOL_HALO - 1), 0))],
        out_specs=pl.BlockSpec((tm, POOL_W), lambda i: (i, 0)),
        out_shape=_sds((S, POOL_W), BF16))(dmixed, dmixed)


def _xattn_probs(q, kv, h):
    hs = slice(h * XA_HD, (h + 1) * XA_HD)
    s = _dot(q[:, hs], kv[:, hs], _NT) * (XA_HD ** -0.5)
    s = s - jnp.max(s, axis=-1, keepdims=True)
    e = jnp.exp(s)
    return e / jnp.sum(e, axis=-1, keepdims=True)


def _xattn_fwd(proj, kv):
    S = proj.shape[0]

    def fn(q, kv):
        outs = []
        for h in range(XA_HEADS):
            p = _xattn_probs(q, kv, h)
            outs.append(_dot(p, kv[:, XA_W + h * XA_HD:XA_W + (h + 1) * XA_HD]))
        return jnp.concatenate(outs, axis=-1)

    return _rowwise("xattn_fwd", fn, [_Win(proj, XA_W, (3072 + 3072 + POOL_W) // XA_W)], [kv], [_sds((S, XA_W), BF16)])[0]


def _xattn_bwd(proj, kv, dxa):
    S = proj.shape[0]

    def fn(q, dxa, kv):
        dqs, dks, dvs = [], [], []
        for h in range(XA_HEADS):
            hs = slice(h * XA_HD, (h + 1) * XA_HD)
            vh = kv[:, XA_W + h * XA_HD:XA_W + (h + 1) * XA_HD]
            p = _xattn_probs(q, kv, h)
            dp = _dot(dxa[:, hs], vh, _NT)
            ds = p * (dp - jnp.sum(p * dp, axis=-1, keepdims=True)) * (XA_HD ** -0.5)
            dqs.append(_dot(ds, kv[:, hs]))
            dks.append(_dot(ds, q[:, hs], _TN))
            dvs.append(_dot(p, dxa[:, hs], _TN))
        return jnp.concatenate(dqs, axis=-1), jnp.concatenate(dks + dvs, axis=-1)

    return _rowwise("xattn_bwd", fn, [_Win(proj, XA_W, (3072 + 3072 + POOL_W) // XA_W), dxa], [kv],
                    [_sds((S, XA_W), BF16)], [_sds(kv.shape)])


def _mix_fwd(x1, h, mem, W, g_next):
    S = x1.shape[0]
    M = mem.shape[0]
    tm = min(512, S)
    def proj_body(h_ref, w_ref, o_ref):
        hh = h_ref[...]
        for j in range(INT_NBLK):
            o_ref[:, INT_BLK * j:INT_BLK * (j + 1)] = _dot(hh, w_ref[INT_BLK * j:INT_BLK * (j + 1), :], _NT).astype(BF16)

    proj = _pcall(
        proj_body, name="mix_proj", grid=(S // tm,),
        in_specs=[pl.BlockSpec((tm, D_MODEL), lambda i: (i, 0)),
                  pl.BlockSpec((INT_W, D_MODEL), lambda i: (0, 0), pipeline_mode=pl.Buffered(1))],
        out_specs=pl.BlockSpec((tm, INT_W), lambda i: (i, 0)), out_shape=_sds((S, INT_W), BF16))(h, W["w_int_t"])
    o_raw, states = _gla_fwd(proj, W["w_fu_pad"], W["b_f"])
    ya_in = _rowwise("gla_out", _head_rms_gate, [o_raw, _Win(proj, GLA_DV, 2)], [W["gla_norm_g"]],
                     [_sds((S, GLA_DV), BF16)])[0]
    mixed, pool_out = _pool_fwd(proj, W["w_pool"], W["pool_scale"])
    mem_n = _rowwise("mem_norm", lambda m, g: _rms(m, g), [mem], [W["mem_norm_g"]], [_sds((M, D_MODEL), BF16)])[0]
    kv = _mm_nn("mem_kv", mem_n, W["w_mem_kv"])
    xa = _xattn_fwd(proj, kv)
    ya = _mm_nn("up_gla", ya_in, W["w_up_gla"], BF16)
    yb = _mm_nn("up_pool", pool_out, W["w_up_pool"], BF16)
    yc = _mm_nn("up_xattn", xa, W["w_up_xattn"], BF16)
    def out_fn(gates, ya, yb, yc, x1, wo, g, gn):
        merged = _merge(gates, ya, yb, yc).astype(BF16)
        y = _dot(merged, wo)
        x2 = x1 + _rms(y, g)
        return merged, y, x2, _rms(x2, gn)

    merged, y, x2, h_next = _rowwise(
        "mix_out", out_fn, [_Win(proj, 3 * D_MODEL, 1), ya, yb, yc, x1], [W["w_o"], W["mix_post_g"], g_next],
        [_sds((S, D_MODEL), BF16), _sds((S, D_MODEL)), _sds((S, D_MODEL)), _sds((S, D_MODEL), BF16)], tm=256)
    return x2, h_next, (h, proj, o_raw, states, ya_in, mixed, pool_out, mem_n, kv, xa, ya, yb, yc, merged, y)


def _mix_bwd(dx2, x1, mem, W, saved, on_grads=None):
    h, proj, o_raw, states, ya_in, mixed, pool_out, mem_n, kv, xa, ya, yb, yc, merged, y = saved
    S = x1.shape[0]

    def out_bwd(y, dx2, gates, ya, yb, yc, g, wo):
        dy, dg = _rms_bwd(y, g, dx2)
        dy = dy.astype(BF16)
        _, vjp = jax.vjp(_merge, gates, ya, yb, yc)
        return (dy, *vjp(_dot(dy, wo, _NT)), dg)

    dy, dgates, dya, dyb, dyc, d_mix_post_g = _rowwise(
        "mix_out_bwd", out_bwd, [y, dx2, _Win(proj, 3 * D_MODEL, 1), ya, yb, yc], [W["mix_post_g"], W["w_o"]],
        [_sds((S, D_MODEL), BF16), _sds((S, 3 * D_MODEL), BF16)] + [_sds((S, D_MODEL), BF16)] * 3, [_sds((1, D_MODEL))],
        tm=256)
    d_w_o = _mm_tn("d_w_o", merged, dy)
    d_w_up_gla = _mm_tn("d_w_up_gla", ya_in, dya)
    d_w_up_pool = _mm_tn("d_w_up_pool", pool_out, dyb)
    d_w_up_xattn = _mm_tn("d_w_up_xattn", xa, dyc)
    d_ya_in = _mm_nt("d_ya_in", dya, W["w_up_gla"], BF16)
    d_pool_out = _mm_nt("d_pool_out", dyb, W["w_up_pool"], BF16)
    d_xa = _mm_nt("d_xa", dyc, W["w_up_xattn"], BF16)

    def gla_out_bwd(o, g_out, d, gn):
        _, vjp = jax.vjp(_head_rms_gate, o, g_out, gn)
        return vjp(d.astype(F32))

    do_raw, dg_out, d_gla_norm_g = _rowwise("gla_out_bwd", gla_out_bwd, [o_raw, _Win(proj, GLA_DV, 2), d_ya_in],
                                            [W["gla_norm_g"]], [_sds((S, GLA_DV)), _sds((S, GLA_DV), BF16)],
                                            [_sds((1, GLA_DV))])
    dq, dk, dv, dflow, d_wfu_pad, d_b_f = _gla_bwd(proj, W["w_fu_pad"], W["b_f"], states, do_raw)
    dmixed, d_w_pool, d_pool_scale = _pool_lin_bwd(d_pool_out, mixed, W["w_pool"], W["pool_scale"])
    dp_in = _pool_win_bwd(dmixed)
    dxq, dkv = _xattn_bwd(proj, kv, d_xa)
    d_w_mem_kv = _mm_tn("d_w_mem_kv", mem_n, dkv)
    dmem_n = _mm_nt("d_mem_n", dkv, W["w_mem_kv"])
    d_mem_norm_g = _rowwise("mem_norm_bwd", lambda m, d, g: _rms_bwd(m, g, d)[1], [mem, dmem_n], [W["mem_norm_g"]], [],
                            [_sds((1, D_MODEL))])[0]
    dproj = jnp.concatenate([dq, dk, dv, dg_out, dgates, dp_in, dxq, dflow], axis=-1)
    d_w_int_t = _mm_tn("d_w_int", dproj, h, ts=1024, tn_a=INT_BLK)
    grads = dict(
        w_int_t=d_w_int_t, w_fu_pad=d_wfu_pad, b_f=d_b_f, gla_norm_g=d_gla_norm_g, w_pool=d_w_pool,
        pool_scale=d_pool_scale, mem_norm_g=d_mem_norm_g, w_mem_kv=d_w_mem_kv, w_up_gla=d_w_up_gla,
        w_up_pool=d_w_up_pool, w_up_xattn=d_w_up_xattn, w_o=d_w_o, mix_post_g=d_mix_post_g)
    deps = [on_grads(grads)] if on_grads is not None else []
    dx1, grads["mix_pre_g"] = _dh_pre_norm_bwd("d_mix_h", dproj, W["w_int_t"], x1, dx2, W["mix_pre_g"], deps)
    return dx1, grads


def _mesh_pos():
    x, y, c = lax.axis_index("x"), lax.axis_index("y"), lax.axis_index("c")
    return x, y, c, 4 * x + 2 * y + c


def _peer(x, y, c, r):
    px = 1 - x if r & 4 else x
    py = 1 - y if r & 2 else y
    pc = 1 - c if r & 1 else c
    return (px, py, pc), 4 * px + 2 * py + pc


_ALL_PEERS = tuple(range(1, N_DEV))
_SIBLING = 1
_SAME_CORE = (2, 4, 6)


def _dev_slot(ref, dev):
    return ref.at[dev]


class _Plan:
    def __init__(self, scatter, slots=None, shapes=None):
        self.scatter, self.slots, self.shapes = scatter, slots or {}, shapes or {}

    def src(self, srcs, a, dev):
        return self.slots.get(a, _dev_slot)(srcs[a], dev) if self.scatter else srcs[a]

    def dst(self, lands, a, dev):
        return lands[a].at[dev] if self.scatter else self.slots.get(a, _dev_slot)(lands[a], dev)

    def landing_zones(self, arrays):
        lands = []
        for a, arr in enumerate(arrays):
            if self.scatter:
                lands.append(lax.empty((N_DEV,) + tuple(self.shapes.get(a, arr.shape[1:])), arr.dtype))
            elif a in self.shapes:
                lands.append(jnp.zeros(self.shapes[a], arr.dtype))
            else:
                lands.append(lax.empty((N_DEV,) + arr.shape, arr.dtype))
        return lands


_GATHER, _SCATTER = _Plan(False), _Plan(True)


def _peer_copies(srcs, lands, send_sems, recv_sems, plan, peers=_ALL_PEERS):
    x, y, c, me = _mesh_pos()
    cps = []
    for r in peers:
        pos, peer = _peer(x, y, c, r)
        for a in range(len(srcs)):
            k = a * (N_DEV - 1) + r - 1
            cps.append(pltpu.make_async_remote_copy(
                src_ref=plan.src(srcs, a, peer), dst_ref=plan.dst(lands, a, me),
                send_sem=send_sems.at[k], recv_sem=recv_sems.at[k], device_id=pos,
                device_id_type=pl.DeviceIdType.MESH))
    return cps


def _exchange(name, arrays, plan, after):
    n = len(arrays)

    def body(*refs):
        ins, outs = refs[:n], refs[n + len(after):2 * n + len(after)]
        send_sems, recv_sems, local_sems = refs[2 * n + len(after):]
        copies = _own_copies(ins, outs, local_sems, plan) + _peer_copies(ins, outs, send_sems, recv_sems, plan)
        for cp in copies:
            cp.start()
        for cp in copies:
            cp.wait()

    any_spec = pl.BlockSpec(memory_space=pl.ANY)
    out_shape = [_sds(l.shape, l.dtype) for l in plan.landing_zones(arrays)]
    return pl.pallas_call(
        body, name=name, in_specs=[any_spec] * (n + len(after)), out_specs=[any_spec] * n, out_shape=out_shape,
        scratch_shapes=[pltpu.SemaphoreType.DMA((n * (N_DEV - 1),)), pltpu.SemaphoreType.DMA((n * (N_DEV - 1),)),
                        pltpu.SemaphoreType.DMA((n,))])(*arrays, *after)


_HBM = pl.BlockSpec(memory_space=pltpu.HBM)
_SEM = pl.BlockSpec(memory_space=pltpu.SEMAPHORE)
_EFFECT = pltpu.SideEffectType.DATAFLOW_SIDE_EFFECTING


def _own_copies(srcs, lands, own_sems, plan):
    me = _mesh_pos()[3]
    return [pltpu.make_async_copy(plan.src(srcs, a, me), plan.dst(lands, a, me), own_sems.at[a])
            for a in range(len(srcs))]


def _exchange_start(name, arrays, plan):
    n = len(arrays)
    lands = plan.landing_zones(arrays)
    n_sem = n * (N_DEV - 1)

    def body(*refs):
        srcs, lands_ = refs[:n], refs[n:2 * n]
        send_sems, recv_sems, own_sems = refs[2 * n:2 * n + 3]
        token = refs[-1]
        for cp in _peer_copies(srcs, lands_, send_sems, recv_sems, plan) + _own_copies(srcs, lands_, own_sems, plan):
            cp.start()
        token[...] = jnp.zeros_like(token)

    hbm = lambda a: pltpu.HBM(a.shape, a.dtype)
    res = pl.pallas_call(
        body, name=name,
        out_shape=(pltpu.SemaphoreType.DMA((n_sem,)), pltpu.SemaphoreType.DMA((n_sem,)), pltpu.SemaphoreType.DMA((n,)),
                   *[hbm(a) for a in arrays], *[hbm(a) for a in lands], _sds((8, 128))),
        in_specs=[_HBM] * (2 * n),
        out_specs=(_SEM, _SEM, _SEM, *[_HBM] * (2 * n), pl.BlockSpec(memory_space=pltpu.VMEM)),
        input_output_aliases={i: 3 + i for i in range(2 * n)},
        compiler_params=pltpu.CompilerParams(has_side_effects=_EFFECT),
    )(*[pltpu.with_memory_space_constraint(a, pltpu.HBM) for a in list(arrays) + lands])
    return (res[:3], res[3:3 + n], res[3 + n:3 + 2 * n], plan), res[-1]


def _exchange_wait(name, state, after):
    sems, srcs, lands, plan = state
    n = len(srcs)

    def body(*refs):
        srcs_, lands_ = refs[:n], refs[n:2 * n]
        send_sems, recv_sems, own_sems = refs[2 * n:2 * n + 3]
        for cp in _peer_copies(srcs_, lands_, send_sems, recv_sems, plan):
            cp.wait_send()
            cp.wait_recv()
        for cp in _own_copies(srcs_, lands_, own_sems, plan):
            cp.wait()

    hbm = lambda a: pltpu.HBM(a.shape, a.dtype)
    res = pl.pallas_call(
        body, name=name, out_shape=tuple(hbm(a) for a in list(srcs) + list(lands)),
        in_specs=[_HBM] * (2 * n) + [_SEM] * 3 + [pl.BlockSpec(memory_space=pl.ANY)] * len(after),
        out_specs=tuple([_HBM] * (2 * n)), input_output_aliases={i: i for i in range(2 * n)},
        compiler_params=pltpu.CompilerParams(has_side_effects=_EFFECT),
    )(*srcs, *lands, *sems, *after)
    return res[n:]


def _gather_start(name, arrays, plan, after):
    n = len(arrays)
    lands = plan.landing_zones(arrays)
    n_sem = n * (N_DEV - 1)

    def body(*refs):
        srcs, lands_ = refs[:n], refs[n:2 * n]
        send_sems, recv_sems, own_sems = refs[2 * n + len(after):2 * n + len(after) + 3]
        token = refs[-1]
        for cp in (_peer_copies(srcs, lands_, send_sems, recv_sems, plan, (_SIBLING,) + _SAME_CORE)
                   + _own_copies(srcs, lands_, own_sems, plan)):
            cp.start()
        token[...] = jnp.zeros_like(token)

    hbm = lambda a: pltpu.HBM(a.shape, a.dtype)
    res = pl.pallas_call(
        body, name=name,
        out_shape=(pltpu.SemaphoreType.DMA((n_sem,)), pltpu.SemaphoreType.DMA((n_sem,)), pltpu.SemaphoreType.DMA((n,)),
                   *[hbm(a) for a in arrays], *[hbm(a) for a in lands], _sds((8, 128))),
        in_specs=[_HBM] * (2 * n) + [pl.BlockSpec(memory_space=pl.ANY)] * len(after),
        out_specs=(_SEM, _SEM, _SEM, *[_HBM] * (2 * n), pl.BlockSpec(memory_space=pltpu.VMEM)),
        input_output_aliases={i: 3 + i for i in range(2 * n)},
        compiler_params=pltpu.CompilerParams(has_side_effects=_EFFECT),
    )(*[pltpu.with_memory_space_constraint(a, pltpu.HBM) for a in list(arrays) + lands], *after)
    return (res[:3], res[3:3 + n], res[3 + n:3 + 2 * n], plan), res[-1]


def _pass_on_copies(lands, send_sems, recv_sems, plan):
    x, y, c, _ = _mesh_pos()
    sibling = _peer(x, y, c, _SIBLING)[0]
    cps = []
    for i, r in enumerate(_SAME_CORE):
        owner = _peer(x, y, c, r)[1]
        for a in range(len(lands)):
            k = a * len(_SAME_CORE) + i
            cps.append(pltpu.make_async_remote_copy(
                src_ref=plan.dst(lands, a, owner), dst_ref=plan.dst(lands, a, owner), send_sem=send_sems.at[k],
                recv_sem=recv_sems.at[k], device_id=sibling, device_id_type=pl.DeviceIdType.MESH))
    return cps


def _gather_pass_on(name, state, after):
    sems, srcs, lands, plan = state
    n = len(srcs)
    n_sem = n * len(_SAME_CORE)

    def body(*refs):
        srcs_, lands_ = refs[:n], refs[n:2 * n]
        send_sems, recv_sems = refs[2 * n], refs[2 * n + 1]
        on_send, on_recv = refs[2 * n + 3 + len(after)], refs[2 * n + 4 + len(after)]
        arrivals = _peer_copies(srcs_, lands_, send_sems, recv_sems, plan, _SAME_CORE)
        for arrived, on in zip(arrivals, _pass_on_copies(lands_, on_send, on_recv, plan)):
            arrived.wait_recv()
            on.start()

    hbm = lambda a: pltpu.HBM(a.shape, a.dtype)
    res = pl.pallas_call(
        body, name=name,
        out_shape=(pltpu.SemaphoreType.DMA((n_sem,)), pltpu.SemaphoreType.DMA((n_sem,)),
                   *[hbm(a) for a in list(srcs) + list(lands)]),
        in_specs=[_HBM] * (2 * n) + [_SEM] * 3 + [pl.BlockSpec(memory_space=pl.ANY)] * len(after),
        out_specs=(_SEM, _SEM, *[_HBM] * (2 * n)), input_output_aliases={i: 2 + i for i in range(2 * n)},
        compiler_params=pltpu.CompilerParams(has_side_effects=_EFFECT),
    )(*srcs, *lands, *sems, *after)
    return sems, res[:2], res[2:2 + n], res[2 + n:], plan


def _gather_wait(name, state, after):
    sems, on_sems, srcs, lands, plan = state
    n = len(srcs)

    def body(*refs):
        srcs_, lands_ = refs[:n], refs[n:2 * n]
        send_sems, recv_sems, own_sems, on_send, on_recv = refs[2 * n:2 * n + 5]
        for cp in _peer_copies(srcs_, lands_, send_sems, recv_sems, plan, (_SIBLING,)):
            cp.wait_recv()
        for cp in _peer_copies(srcs_, lands_, send_sems, recv_sems, plan, (_SIBLING,) + _SAME_CORE):
            cp.wait_send()
        for cp in _own_copies(srcs_, lands_, own_sems, plan):
            cp.wait()
        for cp in _pass_on_copies(lands_, on_send, on_recv, plan):
            cp.wait_send()
            cp.wait_recv()

    hbm = lambda a: pltpu.HBM(a.shape, a.dtype)
    res = pl.pallas_call(
        body, name=name, out_shape=tuple(hbm(a) for a in list(srcs) + list(lands)),
        in_specs=[_HBM] * (2 * n) + [_SEM] * 5 + [pl.BlockSpec(memory_space=pl.ANY)] * len(after),
        out_specs=tuple([_HBM] * (2 * n)), input_output_aliases={i: i for i in range(2 * n)},
        compiler_params=pltpu.CompilerParams(has_side_effects=_EFFECT),
    )(*srcs, *lands, *sems, *on_sems, *after)
    return res[n:]


def _adam(name, recv, w, m, v):
    shape = w.shape
    R, C = shape
    tr = R
    while N_DEV * tr * C * 4 > 6 * 1024 * 1024 and tr % 32 == 0:
        tr //= 2

    def body(recv_ref, w_ref, m_ref, v_ref, g_ref, d_ref, m2_ref, v2_ref):
        g = recv_ref[0].astype(F32)
        for j in range(1, N_DEV):
            g = g + recv_ref[j].astype(F32)
        w_, m_, v_ = w_ref[...], m_ref[...], v_ref[...]
        m2 = ADAM_B1 * m_ + (1.0 - ADAM_B1) * g
        v2 = ADAM_B2 * v_ + (1.0 - ADAM_B2) * (g * g)
        m_hat = m2 / (1.0 - ADAM_B1 ** ADAM_STEP)
        v_hat = v2 / (1.0 - ADAM_B2 ** ADAM_STEP)
        g_ref[...] = g
        d_ref[...] = -ADAM_LR * (m_hat / (jnp.sqrt(v_hat) + ADAM_EPS) + ADAM_WD * w_)
        m2_ref[...] = m2
        v2_ref[...] = v2

    blk = pl.BlockSpec((tr, C), lambda i: (i, 0))
    return _pcall(body, name=name, grid=(R // tr,),
                  in_specs=[pl.BlockSpec((N_DEV, tr, C), lambda i: (0, i, 0)), blk, blk, blk],
                  out_specs=[blk] * 4, out_shape=[_sds(shape)] * 4)(recv, w, m, v)


_NAMES = ['ffn1_pre_g', 'ffn1_w_in', 'ffn1_w_out', 'ffn1_post_g', 'mix_pre_g', 'w_in', 'w_fu', 'b_f', 'gla_norm_g',
          'w_pool', 'pool_scale', 'mem_norm_g', 'w_mem_kv', 'w_up_gla', 'w_up_pool', 'w_up_xattn', 'w_o', 'mix_post_g',
          'ffn2_pre_g', 'ffn2_w_in', 'ffn2_w_out', 'ffn2_post_g', 'final_g']
_SHARDED = ['ffn1_w_in', 'ffn1_w_out', 'w_in', 'w_fu', 'w_mem_kv', 'w_up_gla', 'w_up_pool', 'w_up_xattn', 'w_o',
            'ffn2_w_in', 'ffn2_w_out']
_COL_SHARDED = ['w_up_pool', 'w_up_xattn']


def _cols_to_full(g):
    return jnp.transpose(g, (1, 0, 2)).reshape(g.shape[1], N_DEV * g.shape[2])


def _full_to_cols(f):
    R, C = f.shape
    return jnp.transpose(f.reshape(R, N_DEV, C // N_DEV), (1, 0, 2))


def _to_internal(w_in_t):
    o = 0
    parts = []
    for s in IN_SPLITS:
        parts.append(w_in_t[o:o + s])
        o += s
    q, k, v, g_out, f_low, p_in, xq, gates = parts
    f_low = jnp.pad(f_low, ((0, FLOW_W - GATE_RANK), (0, 0)))
    return jnp.concatenate([q, k, v, g_out, gates, p_in, xq, f_low], axis=0)


def _from_internal(d):
    q, k, v, g_out = d[0:512], d[512:1024], d[1024:2048], d[2048:3072]
    gates, p_in, xq = d[3072:6144], d[6144:6656], d[6656:7168]
    f_low = d[7168:7168 + GATE_RANK]
    return jnp.concatenate([q, k, v, g_out, f_low, p_in, xq, gates], axis=0)


_FFN_IN = ('ffn1_w_in', 'ffn2_w_in')
_FFN_OUT = ('ffn1_w_out', 'ffn2_w_out')
_GATHERS = {"ffn1_in": ['ffn1_w_in'], "ffn1_out": ['ffn1_w_out'],
            "mix": ['w_in', 'w_fu', 'w_mem_kv', 'w_up_gla', 'w_up_pool', 'w_up_xattn', 'w_o'],
            "ffn2": ['ffn2_w_in', 'ffn2_w_out']}


def _ffn_in_slot(ref, d):
    return ref.at[2 * (d % N_FF_BLK) + d // N_FF_BLK, pl.ds(0, FF_BLK)]


def _ffn_out_slot(ref, d):
    rows = FF_BLK // 2
    return ref.at[d // 2, pl.ds(pl.multiple_of((d % 2) * rows, rows), rows)]


def _ffn_plan(names, scatter):
    slots, shapes = {}, {}
    for a, n in enumerate(names):
        if n in _FFN_IN:
            slots[a] = _ffn_in_slot
            shapes[a] = (FF_BLK, D_MODEL) if scatter else (N_DEV, FF_PAD, D_MODEL)
        elif n in _FFN_OUT:
            slots[a] = _ffn_out_slot
            shapes[a] = (FF_BLK // 2, D_MODEL) if scatter else (N_FF_BLK, FF_PAD, D_MODEL)
    return _Plan(scatter, slots, shapes)


def _rows_to_full(g):
    return g.reshape(N_DEV * g.shape[1], g.shape[2])


def _mix_weights(gathered):
    W = {n: _rows_to_full(gathered[n]) for n in ('w_mem_kv', 'w_up_gla', 'w_o')}
    for n in _COL_SHARDED:
        W[n] = _cols_to_full(gathered[n])
    W["w_int_t"] = _to_internal(gathered["w_in"].reshape(IN_WIDTH, D_MODEL))
    W["w_fu_pad"] = jnp.pad(_cols_to_full(gathered["w_fu"]), ((0, FLOW_W - GATE_RANK), (0, 0)))
    return W


def _mix_chunks(G, n):
    if n == "w_in":
        return _from_internal(G["w_int_t"]).reshape(N_DEV, IN_SHARD, D_MODEL)
    if n == "w_fu":
        return _full_to_cols(G["w_fu_pad"][:GATE_RANK].astype(BF16))
    if n in _COL_SHARDED:
        return _full_to_cols(G[n])
    return G[n].reshape(N_DEV, G[n].shape[0] // N_DEV, G[n].shape[1])


def _step(x, mem, tgt, P, Mo, Vo):
    def native(n, a):
        return jnp.swapaxes(a, 0, 1) if n in _FFN_IN + ("w_in",) else a

    P, Mo, Vo = ({n: native(n, a) for n, a in d.items()} for d in (P, Mo, Vo))
    small = {n: P[n] for n in _NAMES if n not in _SHARDED}

    gather, token = {}, []
    for grp, names in _GATHERS.items():
        gather[grp], tok = _gather_start("gather_" + grp, [P[n].astype(BF16) for n in names], _ffn_plan(names, False),
                                         token)
        token = [tok]

    def gathered(grp, after):
        return _gather_wait("gather_" + grp + "_wait", _gather_pass_on("gather_" + grp + "_on", gather[grp], after), [])

    def ffn_wo(lands):
        return lambda act: lands[0].reshape(N_FF_BLK * FF_PAD, D_MODEL)

    h1 = _pre_norm("ffn1_pre", x, small["ffn1_pre_g"])
    w1t = gathered("ffn1_in", token + [h1])[0].reshape(N_DEV * FF_PAD, D_MODEL)
    (x1, hm), w1o, sv1 = _ffn_fwd("ffn1", x, h1, w1t, small["ffn1_post_g"],
                                  lambda act: ffn_wo(gathered("ffn1_out", [act]))(act), small["mix_pre_g"])

    gm = dict(zip(_GATHERS["mix"], gathered("mix", [x1])))
    Wm = {**small, **_mix_weights(gm)}
    x2, h2, svm = _mix_fwd(x1, hm, mem, Wm, small["ffn2_pre_g"])

    w2t, w2o = gathered("ffn2", [x2])
    w2t = w2t.reshape(N_DEV * FF_PAD, D_MODEL)
    (loss, dx3, d_final_g), w2o, sv2 = _ffn_fwd("ffn2", x2, h2, w2t, small["ffn2_post_g"], ffn_wo([w2o]),
                                                small["final_g"], tgt)

    G = dict(final_g=d_final_g)
    scat = {}

    def start(grp, names, arrays):
        scat[grp] = names, _exchange_start("scatter_" + grp, arrays, _ffn_plan(names, True))
        return scat[grp][1][1]

    def ffn_starts(tag):
        return (lambda dwo: start(tag + "_out", [tag + "_w_out"], [dwo.reshape(N_FF_BLK, FF_PAD, D_MODEL)]),
                lambda dwt: start(tag + "_in", [tag + "_w_in"], [dwt.reshape(N_DEV, FF_PAD, D_MODEL)]))

    dx2, G["ffn2_pre_g"], G["ffn2_post_g"] = _ffn_bwd(
        "ffn2", dx3, x2, small["ffn2_pre_g"], w2t, w2o, small["ffn2_post_g"], sv2, *ffn_starts("ffn2"))
    dx1, Gm = _mix_bwd(dx2, x1, mem, Wm, svm,
                       lambda Gm: start("mix", _GATHERS["mix"], [_mix_chunks(Gm, n) for n in _GATHERS["mix"]]))
    G.update(Gm)
    dx, G["ffn1_pre_g"], G["ffn1_post_g"] = _ffn_bwd(
        "ffn1", dx1, x, small["ffn1_pre_g"], w1t, w1o, small["ffn1_post_g"], sv1, *ffn_starts("ffn1"))

    recv, outs = {}, {}
    done = [dx]
    for grp in ["ffn2_out", "ffn2_in", "mix", "small", "ffn1_out", "ffn1_in"]:
        if grp == "small":
            names = list(small)
            recv.update(zip(names, _exchange("gather_small_grads", [G[n] for n in small], _GATHER, done)))
        else:
            names, (state, _) = scat[grp]
            recv.update(zip(names, _exchange_wait("scatter_" + grp + "_wait", state, done[-1:])))
        for n in names:
            shp = P[n].shape
            shp2 = shp if len(shp) == 2 else (shp[0] * shp[1], shp[2])
            res = _adam("adam_" + n, recv[n].reshape((N_DEV,) + shp2), P[n].reshape(shp2), Mo[n].reshape(shp2),
                        Vo[n].reshape(shp2))
            outs[n] = [native(n, t.reshape(shp)) for t in res]
            done.append(res[-1])
    return loss, dx, outs


def kernel(x, mem, ffn1_pre_g, ffn1_w_in, ffn1_w_out, ffn1_post_g, mix_pre_g, w_in, w_fu, b_f, gla_norm_g, w_pool, pool_scale, mem_norm_g, w_mem_kv, w_up_gla, w_up_pool, w_up_xattn, w_o, mix_post_g, ffn2_pre_g, ffn2_w_in, ffn2_w_out, ffn2_post_g, final_g, loss_target, m_ffn1_pre_g, m_ffn1_w_in, m_ffn1_w_out, m_ffn1_post_g, m_mix_pre_g, m_w_in, m_w_fu, m_b_f, m_gla_norm_g, m_w_pool, m_pool_scale, m_mem_norm_g, m_w_mem_kv, m_w_up_gla, m_w_up_pool, m_w_up_xattn, m_w_o, m_mix_post_g, m_ffn2_pre_g, m_ffn2_w_in, m_ffn2_w_out, m_ffn2_post_g, m_final_g, v_ffn1_pre_g, v_ffn1_w_in, v_ffn1_w_out, v_ffn1_post_g, v_mix_pre_g, v_w_in, v_w_fu, v_b_f, v_gla_norm_g, v_w_pool, v_pool_scale, v_mem_norm_g, v_w_mem_kv, v_w_up_gla, v_w_up_pool, v_w_up_xattn, v_w_o, v_mix_post_g, v_ffn2_pre_g, v_ffn2_w_in, v_ffn2_w_out, v_ffn2_post_g, v_final_g):
    params = [ffn1_pre_g, ffn1_w_in, ffn1_w_out, ffn1_post_g, mix_pre_g, w_in, w_fu, b_f, gla_norm_g, w_pool, pool_scale, mem_norm_g, w_mem_kv, w_up_gla, w_up_pool, w_up_xattn, w_o, mix_post_g, ffn2_pre_g, ffn2_w_in, ffn2_w_out, ffn2_post_g, final_g]
    moms = [m_ffn1_pre_g, m_ffn1_w_in, m_ffn1_w_out, m_ffn1_post_g, m_mix_pre_g, m_w_in, m_w_fu, m_b_f, m_gla_norm_g, m_w_pool, m_pool_scale, m_mem_norm_g, m_w_mem_kv, m_w_up_gla, m_w_up_pool, m_w_up_xattn, m_w_o, m_mix_post_g, m_ffn2_pre_g, m_ffn2_w_in, m_ffn2_w_out, m_ffn2_post_g, m_final_g]
    vars_ = [v_ffn1_pre_g, v_ffn1_w_in, v_ffn1_w_out, v_ffn1_post_g, v_mix_pre_g, v_w_in, v_w_fu, v_b_f, v_gla_norm_g, v_w_pool, v_pool_scale, v_mem_norm_g, v_w_mem_kv, v_w_up_gla, v_w_up_pool, v_w_up_xattn, v_w_o, v_mix_post_g, v_ffn2_pre_g, v_ffn2_w_in, v_ffn2_w_out, v_ffn2_post_g, v_final_g]
    P = {n: a[0] if a.ndim > 2 else a for n, a in zip(_NAMES, params)}
    Mo = {n: a[0] if a.ndim > 2 else a for n, a in zip(_NAMES, moms)}
    Vo = {n: a[0] if a.ndim > 2 else a for n, a in zip(_NAMES, vars_)}
    loss, dx, outs = _step(x[0], mem[0], loss_target[0], P, Mo, Vo)
    loss = lax.psum(loss, ("x", "y", "c"))
    out = [loss, dx[None]]
    for kind in range(4):
        for n, p in zip(_NAMES, params):
            out.append(outs[n][kind].reshape(p.shape))
    return tuple(out)
```

```python
import functools

import jax
import jax.numpy as jnp
from jax import lax
from jax.experimental import pallas as pl
from jax.experimental.pallas import tpu as pltpu

F32 = jnp.float32
BF16 = jnp.bfloat16

N_DEV = 8
D_MODEL = 1024
D_FF = 2816
FF_BLK = 2 * D_FF // N_DEV
N_FF_BLK = D_FF // FF_BLK
FF_PAD = 768
CHUNK = 64
GLA_HEADS = 4
GLA_DK = 512
GLA_DV = 1024
GLA_HDK = 128
GLA_HDV = 256
GATE_RANK = 16
GATE_TEMP = 16.0
POOL_WINDOWS = (2, 4, 8, 16)
POOL_W = 512
POOL_GD = 128
POOL_HALO = 16
XA_HEADS = 4
XA_HD = 128
XA_W = 512
EPS = 1e-6
IN_SPLITS = (GLA_DK, GLA_DK, GLA_DV, GLA_DV, GATE_RANK, POOL_W, XA_W, 3 * D_MODEL)
IN_WIDTH = sum(IN_SPLITS)
IN_SHARD = IN_WIDTH // N_DEV
INT_W = 3072 + 3072 + 1024 + 128
INT_NBLK = 3
INT_BLK = INT_W // INT_NBLK
FLOW_W = 128
INT_MAIN = INT_W - FLOW_W
_POOL_COL = (3072 + 3072) // POOL_W
_XQ_COL = (3072 + 3072 + POOL_W) // XA_W

ADAM_LR = 0.001
ADAM_B1 = 0.9
ADAM_B2 = 0.999
ADAM_EPS = 1e-08
ADAM_WD = 0.01
ADAM_STEP = 10

VMEM_LIMIT = 56 * 1024 * 1024

_NN = (((1,), (0,)), ((), ()))
_NT = (((1,), (1,)), ((), ()))
_TN = (((0,), (0,)), ((), ()))


def _pcall(body, *, name, grid, in_specs, out_specs, out_shape, scratch=(), aliases=None):
    return pl.pallas_call(
        body, name=name, grid=grid, in_specs=in_specs, out_specs=out_specs, out_shape=out_shape,
        scratch_shapes=list(scratch), input_output_aliases=aliases or {},
        compiler_params=pltpu.CompilerParams(dimension_semantics=("arbitrary",) * len(grid),
                                             vmem_limit_bytes=VMEM_LIMIT))


def _dot(a, b, dims=_NN):
    return lax.dot_general(a.astype(BF16), b.astype(BF16), dims, preferred_element_type=F32)


def _mm(name, a, b, *, grid, a_blk, a_map, b_blk, b_map, o_shape, o_blk, o_map, dims, out_dtype=F32, deps=()):
    nk = grid[2]

    def body(a_ref, b_ref, *rest):
        o_ref, scr = rest[len(deps)], rest[len(deps) + 1:]
        p = _dot(a_ref[...], b_ref[...], dims)
        if nk == 1:
            o_ref[...] = p.astype(o_ref.dtype)
        else:
            acc = scr[0]
            k = pl.program_id(2)

            @pl.when(k == 0)
            def _():
                acc[...] = p

            @pl.when(k > 0)
            def _():
                acc[...] += p

            @pl.when(k == nk - 1)
            def _():
                o_ref[...] = acc[...].astype(o_ref.dtype)

    acc_shape = tuple(d for d in o_blk if d is not None)
    return _pcall(body, name=name, grid=grid,
                  in_specs=[pl.BlockSpec(a_blk, a_map), pl.BlockSpec(b_blk, b_map)]
                  + [pl.BlockSpec(memory_space=pl.ANY)] * len(deps),
                  out_specs=pl.BlockSpec(o_blk, o_map),
                  out_shape=jax.ShapeDtypeStruct(o_shape, out_dtype),
                  scratch=[pltpu.VMEM(acc_shape, F32)] if nk > 1 else [])(a, b, *deps)


def _mm_nn(name, a, b, out_dtype=F32, tm=1024, tn=None):
    M, K = a.shape
    N = b.shape[1]
    tm, tn = min(tm, M), (tn or N)
    return _mm(name, a, b, grid=(N // tn, M // tm, 1), a_blk=(tm, K), a_map=lambda j, i, k: (i, 0),
               b_blk=(K, tn), b_map=lambda j, i, k: (0, j), o_shape=(M, N), o_blk=(tm, tn),
               o_map=lambda j, i, k: (i, j), dims=_NN, out_dtype=out_dtype)


def _mm_nt(name, a, b, out_dtype=F32, tm=1024):
    M, K = a.shape
    N = b.shape[0]
    tm = min(tm, M)
    return _mm(name, a, b, grid=(1, M // tm, 1), a_blk=(tm, K), a_map=lambda j, i, k: (i, 0),
               b_blk=(N, K), b_map=lambda j, i, k: (0, 0), o_shape=(M, N), o_blk=(tm, N),
               o_map=lambda j, i, k: (i, 0), dims=_NT, out_dtype=out_dtype)


def _mm_tn(name, a, b, out_dtype=BF16, ts=512, tn=None, tn_a=None, deps=()):
    S, M = a.shape
    N = b.shape[1]
    ts, tn, tn_a = min(ts, S), (tn or N), (tn_a or M)
    return _mm(name, a, b, grid=((N // tn) * (M // tn_a), 1, S // ts), a_blk=(ts, tn_a),
               a_map=lambda j, i, k: (k, j if tn_a < M else 0), b_blk=(ts, tn),
               b_map=lambda j, i, k: (k, j if tn < N else 0), o_shape=(M, N), o_blk=(tn_a, tn),
               o_map=lambda j, i, k: (j, 0) if tn_a < M else (0, j), dims=_TN, out_dtype=out_dtype, deps=deps)


class _Win:
    def __init__(self, arr, w, c):
        self.arr, self.w, self.c = arr, w, c


def _row_spec(x, tm):
    if isinstance(x, _Win):
        return x.arr, pl.BlockSpec((tm, x.w), functools.partial(lambda i, c: (i, c), c=x.c))
    if x.ndim == 3:
        return x, pl.BlockSpec((x.shape[0], tm, x.shape[2]), lambda i: (0, i, 0))
    return x, pl.BlockSpec((tm, x.shape[1]), lambda i: (i, 0))


def _rowwise(name, fn, rows, consts, outs, accs=(), tm=512):
    first = rows[0].arr if isinstance(rows[0], _Win) else rows[0]
    S = first.shape[1] if first.ndim == 3 else first.shape[0]
    tm = min(tm, S)
    n_in, n_out = len(rows) + len(consts), len(outs)
    arrays, in_specs = [], []
    for r in rows:
        arr, spec = _row_spec(r, tm)
        arrays.append(arr)
        in_specs.append(spec)
    for c in consts:
        arrays.append(c)
        in_specs.append(pl.BlockSpec(c.shape, functools.partial(lambda i, n: (0,) * n, n=c.ndim)))
    aliases = {}
    for k, o in enumerate(outs):
        if isinstance(o, _Win):
            aliases[len(arrays)] = k
            arrays.append(o.arr)
            in_specs.append(pl.BlockSpec(memory_space=pl.ANY))
    n_thru = len(aliases)
    out_specs = [_row_spec(o, tm)[1] for o in outs]
    out_specs += [pl.BlockSpec(a.shape, functools.partial(lambda i, n: (0,) * n, n=len(a.shape))) for a in accs]
    out_shape = [_sds(o.arr.shape, o.arr.dtype) if isinstance(o, _Win) else _sds(o.shape, o.dtype) for o in outs]

    def body(*refs):
        res = fn(*[r[...] for r in refs[:n_in]])
        if not isinstance(res, (tuple, list)):
            res = (res,)
        orefs = refs[n_in + n_thru:]
        for r, v in zip(orefs[:n_out], res[:n_out]):
            r[...] = v.astype(r.dtype)
        i = pl.program_id(0)
        for r, v in zip(orefs[n_out:], res[n_out:]):
            @pl.when(i == 0)
            def _(r=r, v=v):
                r[...] = v.astype(r.dtype)

            @pl.when(i > 0)
            def _(r=r, v=v):
                r[...] += v.astype(r.dtype)

    return _pcall(body, name=name, grid=(S // tm,), in_specs=in_specs, out_specs=out_specs,
                  out_shape=out_shape + [_sds(a.shape, a.dtype) for a in accs], aliases=aliases)(*arrays)


def _sds(shape, dtype=F32):
    return jax.ShapeDtypeStruct(shape, dtype)


def _rms(x, g):
    return x * lax.rsqrt(jnp.mean(x * x, axis=-1, keepdims=True) + EPS) * g


def _rms_bwd(x, g, dy):
    _, vjp = jax.vjp(_rms, x, g)
    return vjp(dy)


def _sigmoid(x):
    return 0.5 * jnp.tanh(0.5 * x) + 0.5


def _silu(x):
    return x * _sigmoid(x)


def _log_sigmoid(f):
    return jnp.minimum(f, 0.0) - jnp.log(1.0 + jnp.exp(-jnp.abs(f)))


def _head_rms_gate(o, g_out, gn):
    parts = [_rms(o[:, h * GLA_HDV:(h + 1) * GLA_HDV], gn[:, h * GLA_HDV:(h + 1) * GLA_HDV]) for h in range(GLA_HEADS)]
    return jnp.concatenate(parts, axis=-1) * _silu(g_out.astype(F32))


def _merge(gates, ya, yb, yc):
    gates, ya, yb, yc = (t.astype(F32) for t in (gates, ya, yb, yc))
    return (_sigmoid(gates[:, :D_MODEL]) * ya + _sigmoid(gates[:, D_MODEL:2 * D_MODEL]) * yb
            + _sigmoid(gates[:, 2 * D_MODEL:]) * yc)


def _tri_dot(t, x):
    hi = x.astype(BF16)
    r1 = x - hi.astype(F32)
    mid = r1.astype(BF16)
    lo = (r1 - mid.astype(F32)).astype(BF16)
    d = functools.partial(lax.dot_general, dimension_numbers=_NN, preferred_element_type=F32)
    return d(t, hi) + d(t, mid) + d(t, lo)


def _pre_norm(name, x, g):
    return _rowwise(name, lambda x, g: _rms(x, g), [x], [g], [_sds(x.shape, BF16)])[0]


def _ffn_fwd(tag, x, h, wt, gpost, get_wo, g_next, tgt=None):
    S = x.shape[0]
    tm = min(512, S)

    def up_body(h_ref, w_ref, u_ref, act_ref):
        hh = h_ref[...]
        for j in range(N_FF_BLK):
            ab = _dot(hh, w_ref[2 * FF_PAD * j:2 * FF_PAD * (j + 1), :], _NT)
            u_ref[:, 2 * FF_PAD * j:2 * FF_PAD * (j + 1)] = ab.astype(BF16)
            act_ref[:, FF_PAD * j:FF_PAD * (j + 1)] = (_silu(ab[:, :FF_PAD]) * ab[:, FF_PAD:]).astype(BF16)

    u, act = _pcall(
        up_body, name=tag + "_up", grid=(S // tm,),
        in_specs=[pl.BlockSpec((tm, D_MODEL), lambda i: (i, 0)),
                  pl.BlockSpec(wt.shape, lambda i: (0, 0), pipeline_mode=pl.Buffered(1))],
        out_specs=[pl.BlockSpec((tm, N_DEV * FF_PAD), lambda i: (i, 0)),
                   pl.BlockSpec((tm, N_FF_BLK * FF_PAD), lambda i: (i, 0))],
        out_shape=[_sds((S, N_DEV * FF_PAD), BF16), _sds((S, N_FF_BLK * FF_PAD), BF16)],
    )(h, wt)

    def down_body(act_ref, w_ref, x_ref, g_ref, gn_ref, f_ref, xo_ref, hn_ref):
        f = _dot(act_ref[...], w_ref[...])
        f_ref[...] = f
        xo = x_ref[...] + 0.5 * _rms(f, g_ref[...])
        xo_ref[...] = xo
        hn_ref[...] = _rms(xo, gn_ref[...]).astype(BF16)

    def down_head_body(act_ref, w_ref, x_ref, g_ref, gn_ref, t_ref, f_ref, dxo_ref, loss_ref, dgn_ref):
        i = pl.program_id(0)
        f = _dot(act_ref[...], w_ref[...])
        f_ref[...] = f
        xo = x_ref[...] + 0.5 * _rms(f, g_ref[...])
        out, vjp = jax.vjp(_rms, xo, gn_ref[...])
        e = out - t_ref[...]
        loss = 0.5 * jnp.sum(jnp.mean(e * e, axis=-1, keepdims=True), axis=0, keepdims=True)
        dxo_ref[...], dgn = vjp(e * (1.0 / D_MODEL))

        @pl.when(i == 0)
        def _():
            loss_ref[...] = jnp.broadcast_to(loss, loss_ref.shape)
            dgn_ref[...] = dgn

        @pl.when(i > 0)
        def _():
            loss_ref[...] += jnp.broadcast_to(loss, loss_ref.shape)
            dgn_ref[...] += dgn

    wo = get_wo(act)
    row = pl.BlockSpec((tm, D_MODEL), lambda i: (i, 0))
    vec = pl.BlockSpec((1, D_MODEL), lambda i: (0, 0))
    if tgt is not None:
        f, dxo, loss, dg_next = _pcall(
            down_head_body, name=tag + "_down", grid=(S // tm,),
            in_specs=[pl.BlockSpec((tm, N_FF_BLK * FF_PAD), lambda i: (i, 0)),
                      pl.BlockSpec(wo.shape, lambda i: (0, 0), pipeline_mode=pl.Buffered(1)), row, vec, vec, row],
            out_specs=[row, row, pl.BlockSpec((1, 128), lambda i: (0, 0)), vec],
            out_shape=[_sds((S, D_MODEL)), _sds((S, D_MODEL)), _sds((1, 128)), _sds((1, D_MODEL))],
        )(act, wo, x, gpost, g_next, tgt)
        return (loss[0, 0], dxo, dg_next), wo, (h, u, act, f)
    f, xo, h_next = _pcall(
        down_body, name=tag + "_down", grid=(S // tm,),
        in_specs=[pl.BlockSpec((tm, N_FF_BLK * FF_PAD), lambda i: (i, 0)),
                  pl.BlockSpec(wo.shape, lambda i: (0, 0), pipeline_mode=pl.Buffered(1)), row, vec, vec],
        out_specs=[row, row, row], out_shape=[_sds((S, D_MODEL)), _sds((S, D_MODEL)), _sds((S, D_MODEL), BF16)],
    )(act, wo, x, gpost, g_next)
    return (xo, h_next), wo, (h, u, act, f)


def _ffn_bwd(tag, dxo, x, gpre, wt, wo, gpost, saved, on_dwo=None, on_dwt=None):
    h, u, act, f = saved
    S = x.shape[0]
    tm = min(256, S)

    def dact_body(f_ref, dxo_ref, g_ref, w_ref, u_ref, df_ref, du_ref, dg_ref):
        i = pl.program_id(0)
        df, dg = _rms_bwd(f_ref[...], g_ref[...], 0.5 * dxo_ref[...])
        df = df.astype(BF16)
        df_ref[...] = df

        @pl.when(i == 0)
        def _():
            dg_ref[...] = dg

        @pl.when(i > 0)
        def _():
            dg_ref[...] += dg

        for j in range(N_FF_BLK):
            dact = _dot(df, w_ref[FF_PAD * j:FF_PAD * (j + 1), :], _NT)
            ab = u_ref[:, 2 * FF_PAD * j:2 * FF_PAD * (j + 1)].astype(F32)
            a, b = ab[:, :FF_PAD], ab[:, FF_PAD:]
            sg = _sigmoid(a)
            du_ref[:, 2 * FF_PAD * j:2 * FF_PAD * j + FF_PAD] = (dact * b * (sg * (1.0 + a * (1.0 - sg)))).astype(BF16)
            du_ref[:, 2 * FF_PAD * j + FF_PAD:2 * FF_PAD * (j + 1)] = (dact * (a * sg)).astype(BF16)

    row = pl.BlockSpec((tm, D_MODEL), lambda i: (i, 0))
    vec = pl.BlockSpec((1, D_MODEL), lambda i: (0, 0))
    u_spec = pl.BlockSpec((tm, N_DEV * FF_PAD), lambda i: (i, 0))
    df, du, dgpost = _pcall(
        dact_body, name=tag + "_dact", grid=(S // tm,),
        in_specs=[row, row, vec, pl.BlockSpec(wo.shape, lambda i: (0, 0), pipeline_mode=pl.Buffered(1)), u_spec],
        out_specs=[row, u_spec, vec],
        out_shape=[_sds((S, D_MODEL), BF16), _sds((S, N_DEV * FF_PAD), BF16), _sds((1, D_MODEL))])(f, dxo, gpost, wo, u)
    dwo = _mm_tn(tag + "_dwo", act, df, ts=2048, tn_a=FF_PAD)
    dwt = _mm_tn(tag + "_dwt", du, h, ts=2048, tn_a=FF_PAD, deps=[on_dwo(dwo)] if on_dwo is not None else [])
    dx, dgpre = _dh_pre_norm_bwd(tag + "_dh", [du], wt, x, dxo, gpre, [on_dwt(dwt)] if on_dwt is not None else [])
    return dx, dgpre, dgpost


def _dh_pre_norm_bwd(name, dzs, wt, x, dres, g, deps):
    S = x.shape[0]
    ks = [dz.shape[1] for dz in dzs]

    def body(*refs):
        dz_refs = refs[:len(dzs)]
        w_ref, x_ref, dres_ref, g_ref = refs[len(dzs):len(dzs) + 4]
        dx_ref, dg_ref = refs[len(dzs) + 4 + len(deps):]
        i = pl.program_id(0)
        dh, o = None, 0
        for dz_ref, k in zip(dz_refs, ks):
            p = _dot(dz_ref[...], w_ref[o:o + k, :])
            dh, o = (p if dh is None else dh + p), o + k
        dx, dg = _rms_bwd(x_ref[...], g_ref[...], dh)
        dx_ref[...] = dx + dres_ref[...]

        @pl.when(i == 0)
        def _():
            dg_ref[...] = dg

        @pl.when(i > 0)
        def _():
            dg_ref[...] += dg

    th = min(256, S)
    row = pl.BlockSpec((th, D_MODEL), lambda i: (i, 0))
    vec = pl.BlockSpec((1, D_MODEL), lambda i: (0, 0))
    return _pcall(
        body, name=name, grid=(S // th,),
        in_specs=[pl.BlockSpec((th, k), lambda i: (i, 0)) for k in ks]
        + [pl.BlockSpec(wt.shape, lambda i: (0, 0), pipeline_mode=pl.Buffered(1)), row, row, vec]
        + [pl.BlockSpec(memory_space=pl.ANY)] * len(deps),
        out_specs=[row, vec], out_shape=[_sds((S, D_MODEL)), _sds((1, D_MODEL))])(*dzs, wt, x, dres, g, *deps)


GLA_G = 8


def _gla_tile_common(k, flow, wfu, bf):
    f = _dot(flow, wfu) + bf
    la = _log_sigmoid(f) * (1.0 / GATE_TEMP)
    tri = _tri_matrix(True)
    ws, ds = [], []
    for g in range(k.shape[0] // CHUNK):
        b = _tri_dot(tri, la[g * CHUNK:(g + 1) * CHUNK])
        b_end = b[CHUNK - 1:CHUNK, :]
        ws.append(jnp.exp(b_end - b))
        ds.append(jnp.exp(b_end))
    w = jnp.concatenate(ws, axis=0)
    return f, w, k * w, ds


def _tri_matrix(lower):
    r = lax.broadcasted_iota(jnp.int32, (CHUNK, CHUNK), 0)
    c = lax.broadcasted_iota(jnp.int32, (CHUNK, CHUNK), 1)
    return jnp.where((r >= c) if lower else (r <= c), 1.0, 0.0).astype(BF16)


def _heads():
    return [(slice(h * GLA_HDK, (h + 1) * GLA_HDK), slice(h * GLA_HDV, (h + 1) * GLA_HDV)) for h in range(GLA_HEADS)]


def _gla_fwd(proj, wfu, bf):
    S = proj.shape[0]
    G = min(GLA_G, S // CHUNK)
    T = G * CHUNK
    nc = S // CHUNK

    def body(q_ref, k_ref, v_ref, flow_ref, wfu_ref, bf_ref, o_ref, st_ref, state):
        @pl.when(pl.program_id(0) == 0)
        def _():
            state[...] = jnp.zeros_like(state)

        _, _, kt, ds = _gla_tile_common(k_ref[...], flow_ref[...], wfu_ref[...], bf_ref[...])
        q = q_ref[...].astype(F32) * (GLA_HDK ** -0.5)
        v = v_ref[...]
        rows = [slice(g * CHUNK, (g + 1) * CHUNK) for g in range(G)]
        kv = [[_dot(v[r, vs], kt[r, ks], _TN) for ks, vs in _heads()] for r in rows]
        st = [state[vs, :] for _, vs in _heads()]
        for g, r in enumerate(rows):
            outs = []
            for h, (ks, vs) in enumerate(_heads()):
                st[h] = st[h] * ds[g][:, ks] + kv[g][h]
                st_ref[g, vs, :] = st[h]
                outs.append(_dot(q[r, ks], st[h], _NT))
            o_ref[r, :] = jnp.concatenate(outs, axis=-1)
        for h, (_, vs) in enumerate(_heads()):
            state[vs, :] = st[h]

    return _pcall(
        body, name="gla_fwd", grid=(S // T,),
        in_specs=[pl.BlockSpec((T, GLA_DK), lambda c: (c, 0)), pl.BlockSpec((T, GLA_DK), lambda c: (c, 1)),
                  pl.BlockSpec((T, GLA_DV), lambda c: (c, 1)),
                  pl.BlockSpec((T, FLOW_W), lambda c: (c, (INT_W - FLOW_W) // FLOW_W)),
                  pl.BlockSpec(wfu.shape, lambda c: (0, 0)), pl.BlockSpec(bf.shape, lambda c: (0, 0))],
        out_specs=[pl.BlockSpec((T, GLA_DV), lambda c: (c, 0)),
                   pl.BlockSpec((G, GLA_DV, GLA_HDK), lambda c: (c, 0, 0))],
        out_shape=[_sds((S, GLA_DV)), _sds((nc, GLA_DV, GLA_HDK))],
        scratch=[pltpu.VMEM((GLA_DV, GLA_HDK), F32)])(proj, proj, proj, proj, wfu, bf)


def _gla_bwd(proj, wfu, bf, states, do, dmain):
    S = proj.shape[0]
    G = min(GLA_G, S // CHUNK)
    T = G * CHUNK
    nt = S // T

    def body(q_ref, k_ref, v_ref, flow_ref, wfu_ref, bf_ref, st_ref, stp_ref, do_ref, dmain_in,
             dqkv_ref, dflow_ref, dwfu_ref, dbf_ref, dstate):
        dq_ref = dqkv_ref.at[:, 0:GLA_DK]
        dk_ref = dqkv_ref.at[:, GLA_DK:2 * GLA_DK]
        dv_ref = dqkv_ref.at[:, 2 * GLA_DK:]
        step = pl.program_id(0)

        @pl.when(step == 0)
        def _():
            dstate[...] = jnp.zeros_like(dstate)
            dwfu_ref[...] = jnp.zeros_like(dwfu_ref)
            dbf_ref[...] = jnp.zeros_like(dbf_ref)

        flow, wfu_v = flow_ref[...], wfu_ref[...]
        f, w, kt, ds = _gla_tile_common(k_ref[...], flow, wfu_v, bf_ref[...])
        q = q_ref[...].astype(F32) * (GLA_HDK ** -0.5)
        v, dout = v_ref[...], do_ref[...]
        rows = [slice(g * CHUNK, (g + 1) * CHUNK) for g in range(G)]
        dq = [jnp.concatenate([_dot(dout[r, vs], st_ref[g, vs, :]) for _, vs in _heads()], axis=-1)
              for g, r in enumerate(rows)]
        qdo = [[_dot(dout[r, vs], q[r, ks], _TN) for ks, vs in _heads()] for r in rows]
        dq_ref[...] = (jnp.concatenate(dq, axis=0) * (GLA_HDK ** -0.5)).astype(dq_ref.dtype)
        has_prev = (step < nt - 1).astype(F32)
        carry = [dstate[vs, :] for _, vs in _heads()]
        dkt, dv, dd = [None] * G, [None] * G, [None] * G
        for g in reversed(range(G)):
            r = rows[g]
            dkts, dvs, dds = [], [], []
            for h, (ks, vs) in enumerate(_heads()):
                dst = carry[h] + qdo[g][h]
                dkts.append(_dot(v[r, vs], dst))
                dvs.append(_dot(kt[r, ks], dst, _NT))
                st_prev = st_ref[g - 1, vs, :] if g > 0 else stp_ref[vs, :] * has_prev
                dds.append(jnp.sum(dst * st_prev, axis=0, keepdims=True))
                carry[h] = dst * ds[g][:, ks]
            dkt[g], dv[g], dd[g] = (jnp.concatenate(t, axis=-1) for t in (dkts, dvs, dds))
        for h, (_, vs) in enumerate(_heads()):
            dstate[vs, :] = carry[h]
        dkt = jnp.concatenate(dkt, axis=0)
        dv_ref[...] = jnp.concatenate(dv, axis=0).astype(dv_ref.dtype)
        dk_ref[...] = (dkt * w).astype(dk_ref.dtype)
        de = dkt * kt
        tri = _tri_matrix(False)
        dla = []
        for g, r in enumerate(rows):
            db_end = jnp.sum(de[r], axis=0, keepdims=True) + dd[g] * ds[g]
            dla.append(db_end - _tri_dot(tri, de[r]))
        df = jnp.concatenate(dla, axis=0) * (1.0 - _sigmoid(f)) * (1.0 / GATE_TEMP)
        dflow_ref[...] = _dot(df, wfu_v, _NT).astype(dflow_ref.dtype)
        dwfu_ref[...] += _dot(flow, df, _TN)
        dbf_ref[...] += jnp.sum(df, axis=0, keepdims=True)

    rt = lambda s: nt - 1 - s
    return _pcall(
        body, name="gla_bwd", grid=(nt,),
        in_specs=[pl.BlockSpec((T, GLA_DK), lambda s: (rt(s), 0)), pl.BlockSpec((T, GLA_DK), lambda s: (rt(s), 1)),
                  pl.BlockSpec((T, GLA_DV), lambda s: (rt(s), 1)),
                  pl.BlockSpec((T, FLOW_W), lambda s: (rt(s), (INT_W - FLOW_W) // FLOW_W)),
                  pl.BlockSpec(wfu.shape, lambda s: (0, 0)), pl.BlockSpec(bf.shape, lambda s: (0, 0)),
                  pl.BlockSpec((G, GLA_DV, GLA_HDK), lambda s: (rt(s), 0, 0)),
                  pl.BlockSpec((None, GLA_DV, GLA_HDK), lambda s: (jnp.maximum(rt(s) * G - 1, 0), 0, 0)),
                  pl.BlockSpec((T, GLA_DV), lambda s: (rt(s), 0)), pl.BlockSpec(memory_space=pl.ANY)],
        out_specs=[pl.BlockSpec((T, 2 * GLA_DK + GLA_DV), lambda s: (rt(s), 0)),
                   pl.BlockSpec((T, FLOW_W), lambda s: (rt(s), 0)),
                   pl.BlockSpec(wfu.shape, lambda s: (0, 0)), pl.BlockSpec(bf.shape, lambda s: (0, 0))],
        out_shape=[_sds(dmain.shape, dmain.dtype), _sds((S, FLOW_W), BF16), _sds(wfu.shape), _sds(bf.shape)],
        scratch=[pltpu.VMEM((GLA_DV, GLA_HDK), F32)], aliases={9: 0},
    )(proj, proj, proj, proj, wfu, bf, states, states, do, dmain)


def _pool_counts(tm, i):
    t = (lax.broadcasted_iota(jnp.int32, (tm, POOL_GD), 0) + i * tm + 1).astype(F32)
    return [jnp.minimum(t, float(w)) for w in POOL_WINDOWS]


def _pool_fwd(proj, w_pool, pool_scale, tm=512):
    S = proj.shape[0]
    tm = min(tm, S // 2)
    col = _POOL_COL
    hb = tm // POOL_HALO

    def body(p_ref, halo_ref, wp_ref, sc_ref, mixed_ref, out_ref):
        i = pl.program_id(0)
        p = p_ref[...].astype(F32)
        halo = halo_ref[...].astype(F32) * (i > 0).astype(F32)
        ext = jnp.concatenate([halo, p], axis=0)
        n = tm + POOL_HALO
        sums, acc, k = {}, ext, 1
        while k < POOL_WINDOWS[-1]:
            acc = acc + pltpu.roll(acc, k, axis=0)
            k *= 2
            sums[k] = acc
        cnts = _pool_counts(tm, i)
        mixed, lin = [], []
        for g, w in enumerate(POOL_WINDOWS):
            ls = slice(g * POOL_GD, (g + 1) * POOL_GD)
            m = sums[w][POOL_HALO:n, ls] / cnts[g] - p[:, ls]
            mixed.append(m)
            lin.append(_dot(m, wp_ref[g]))
        mixed_ref[...] = jnp.concatenate(mixed, axis=-1)
        out_ref[...] = (jnp.concatenate(lin, axis=-1) * sc_ref[...]).astype(out_ref.dtype)

    return _pcall(
        body, name="pool_fwd", grid=(S // tm,),
        in_specs=[pl.BlockSpec((tm, POOL_W), lambda i: (i, col)),
                  pl.BlockSpec((POOL_HALO, POOL_W), lambda i: (jnp.maximum(i * hb - 1, 0), col)),
                  pl.BlockSpec(w_pool.shape, lambda i: (0, 0, 0)), pl.BlockSpec(pool_scale.shape, lambda i: (0, 0))],
        out_specs=[pl.BlockSpec((tm, POOL_W), lambda i: (i, 0)), pl.BlockSpec((tm, POOL_W), lambda i: (i, 0))],
        out_shape=[_sds((S, POOL_W)), _sds((S, POOL_W), BF16)])(proj, proj, w_pool, pool_scale)


def _pool_lin_bwd(dout, mixed, w_pool, pool_scale):
    S = dout.shape[0]

    def fn(dout, mixed, wp, sc):
        dlin = dout * sc
        dm, dwp, lin = [], [], []
        for g in range(len(POOL_WINDOWS)):
            ls = slice(g * POOL_GD, (g + 1) * POOL_GD)
            lin.append(_dot(mixed[:, ls], wp[g]))
            dm.append(_dot(dlin[:, ls], wp[g], _NT))
            dwp.append(_dot(mixed[:, ls], dlin[:, ls], _TN))
        dsc = jnp.sum(dout * jnp.concatenate(lin, axis=-1), axis=0, keepdims=True)
        return jnp.concatenate(dm, axis=-1), jnp.concatenate(dwp, axis=0), dsc

    return _rowwise("pool_lin_bwd", fn, [dout, mixed], [w_pool, pool_scale], [_sds((S, POOL_W))],
                    [_sds((len(POOL_WINDOWS) * POOL_GD, POOL_GD)), _sds((1, POOL_W))])


def _pool_win_bwd(dmixed, dmain, tm=512):
    S = dmixed.shape[0]
    tm = min(tm, S // 2)
    nt = S // tm
    hb = tm // POOL_HALO

    def body(dm_ref, halo_ref, dmain_in, dp_ref):
        i = pl.program_id(0)
        dm = dm_ref[...]
        halo = halo_ref[...] * (i < nt - 1).astype(F32)
        cnts = _pool_counts(tm, i)
        cnts_h = [c[:POOL_HALO] for c in _pool_counts(tm, i + 1)]
        r = jnp.concatenate([jnp.concatenate([dm[:, g * POOL_GD:(g + 1) * POOL_GD] / cnts[g] for g in range(4)], axis=-1),
                             jnp.concatenate([halo[:, g * POOL_GD:(g + 1) * POOL_GD] / cnts_h[g] for g in range(4)], axis=-1)],
                            axis=0)
        n = tm + POOL_HALO
        sums, acc, k = {}, r, 1
        while k < POOL_WINDOWS[-1]:
            acc = acc + pltpu.roll(acc, n - k, axis=0)
            k *= 2
            sums[k] = acc
        dp = [sums[w][:tm, g * POOL_GD:(g + 1) * POOL_GD] for g, w in enumerate(POOL_WINDOWS)]
        dp_ref[...] = (jnp.concatenate(dp, axis=-1) - dm).astype(dp_ref.dtype)

    return _pcall(
        body, name="pool_win_bwd", grid=(nt,),
        in_specs=[pl.BlockSpec((tm, POOL_W), lambda i: (i, 0)),
                  pl.BlockSpec((POOL_HALO, POOL_W), lambda i: (jnp.minimum((i + 1) * hb, S // POOL_HALO - 1), 0)),
                  pl.BlockSpec(memory_space=pl.ANY)],
        out_specs=pl.BlockSpec((tm, POOL_W), lambda i: (i, _POOL_COL)),
        out_shape=_sds(dmain.shape, dmain.dtype), aliases={2: 0})(dmixed, dmixed, dmain)


def _xattn_probs(q, kv, h):
    hs = slice(h * XA_HD, (h + 1) * XA_HD)
    s = _dot(q[:, hs], kv[:, hs], _NT) * (XA_HD ** -0.5)
    s = s - jnp.max(s, axis=-1, keepdims=True)
    e = jnp.exp(s)
    return e / jnp.sum(e, axis=-1, keepdims=True)


def _xattn_fwd(proj, kv):
    S = proj.shape[0]

    def fn(q, kv):
        outs = []
        for h in range(XA_HEADS):
            p = _xattn_probs(q, kv, h)
            outs.append(_dot(p, kv[:, XA_W + h * XA_HD:XA_W + (h + 1) * XA_HD]))
        return jnp.concatenate(outs, axis=-1)

    return _rowwise("xattn_fwd", fn, [_Win(proj, XA_W, _XQ_COL)], [kv], [_sds((S, XA_W), BF16)])[0]


def _xattn_bwd(proj, kv, dxa, dmain):
    def fn(q, dxa, kv):
        dqs, dks, dvs = [], [], []
        for h in range(XA_HEADS):
            hs = slice(h * XA_HD, (h + 1) * XA_HD)
            vh = kv[:, XA_W + h * XA_HD:XA_W + (h + 1) * XA_HD]
            p = _xattn_probs(q, kv, h)
            dp = _dot(dxa[:, hs], vh, _NT)
            ds = p * (dp - jnp.sum(p * dp, axis=-1, keepdims=True)) * (XA_HD ** -0.5)
            dqs.append(_dot(ds, kv[:, hs]))
            dks.append(_dot(ds, q[:, hs], _TN))
            dvs.append(_dot(p, dxa[:, hs], _TN))
        return jnp.concatenate(dqs, axis=-1), jnp.concatenate(dks + dvs, axis=-1)

    return _rowwise("xattn_bwd", fn, [_Win(proj, XA_W, _XQ_COL), dxa], [kv], [_Win(dmain, XA_W, _XQ_COL)],
                    [_sds(kv.shape)])


def _mix_fwd(x1, h, mem, W, g_next):
    S = x1.shape[0]
    M = mem.shape[0]
    tm = min(512, S)
    def proj_body(h_ref, w_ref, o_ref):
        hh = h_ref[...]
        for j in range(INT_NBLK):
            o_ref[:, INT_BLK * j:INT_BLK * (j + 1)] = _dot(hh, w_ref[INT_BLK * j:INT_BLK * (j + 1), :], _NT).astype(BF16)

    proj = _pcall(
        proj_body, name="mix_proj", grid=(S // tm,),
        in_specs=[pl.BlockSpec((tm, D_MODEL), lambda i: (i, 0)),
                  pl.BlockSpec((INT_W, D_MODEL), lambda i: (0, 0), pipeline_mode=pl.Buffered(1))],
        out_specs=pl.BlockSpec((tm, INT_W), lambda i: (i, 0)), out_shape=_sds((S, INT_W), BF16))(h, W["w_int_t"])
    o_raw, states = _gla_fwd(proj, W["w_fu_pad"], W["b_f"])
    ya_in = _rowwise("gla_out", _head_rms_gate, [o_raw, _Win(proj, GLA_DV, 2)], [W["gla_norm_g"]],
                     [_sds((S, GLA_DV), BF16)])[0]
    mixed, pool_out = _pool_fwd(proj, W["w_pool"], W["pool_scale"])
    mem_n = _rowwise("mem_norm", lambda m, g: _rms(m, g), [mem], [W["mem_norm_g"]], [_sds((M, D_MODEL), BF16)])[0]
    kv = _mm_nn("mem_kv", mem_n, W["w_mem_kv"])
    xa = _xattn_fwd(proj, kv)
    ya = _mm_nn("up_gla", ya_in, W["w_up_gla"], BF16)
    yb = _mm_nn("up_pool", pool_out, W["w_up_pool"], BF16)
    yc = _mm_nn("up_xattn", xa, W["w_up_xattn"], BF16)
    def out_fn(gates, ya, yb, yc, x1, wo, g, gn):
        merged = _merge(gates, ya, yb, yc).astype(BF16)
        y = _dot(merged, wo)
        x2 = x1 + _rms(y, g)
        return merged, y, x2, _rms(x2, gn)

    merged, y, x2, h_next = _rowwise(
        "mix_out", out_fn, [_Win(proj, 3 * D_MODEL, 1), ya, yb, yc, x1], [W["w_o"], W["mix_post_g"], g_next],
        [_sds((S, D_MODEL), BF16), _sds((S, D_MODEL)), _sds((S, D_MODEL)), _sds((S, D_MODEL), BF16)], tm=256)
    return x2, h_next, (h, proj, o_raw, states, ya_in, mixed, pool_out, mem_n, kv, xa, ya, yb, yc, merged, y)


def _mix_bwd(dx2, x1, mem, W, saved, on_grads=None):
    h, proj, o_raw, states, ya_in, mixed, pool_out, mem_n, kv, xa, ya, yb, yc, merged, y = saved
    S = x1.shape[0]

    def out_bwd(y, dx2, gates, ya, yb, yc, g, wo):
        dy, dg = _rms_bwd(y, g, dx2)
        dy = dy.astype(BF16)
        _, vjp = jax.vjp(_merge, gates, ya, yb, yc)
        return (dy, *vjp(_dot(dy, wo, _NT)), dg)

    dmain = lax.empty((S, INT_MAIN), BF16)
    dy, dmain, dya, dyb, dyc, d_mix_post_g = _rowwise(
        "mix_out_bwd", out_bwd, [y, dx2, _Win(proj, 3 * D_MODEL, 1), ya, yb, yc], [W["mix_post_g"], W["w_o"]],
        [_sds((S, D_MODEL), BF16), _Win(dmain, 3 * D_MODEL, 1)] + [_sds((S, D_MODEL), BF16)] * 3, [_sds((1, D_MODEL))],
        tm=256)
    d_w_o = _mm_tn("d_w_o", merged, dy)
    d_w_up_gla = _mm_tn("d_w_up_gla", ya_in, dya)
    d_w_up_pool = _mm_tn("d_w_up_pool", pool_out, dyb)
    d_w_up_xattn = _mm_tn("d_w_up_xattn", xa, dyc)
    d_ya_in = _mm_nt("d_ya_in", dya, W["w_up_gla"], BF16)
    d_pool_out = _mm_nt("d_pool_out", dyb, W["w_up_pool"], BF16)
    d_xa = _mm_nt("d_xa", dyc, W["w_up_xattn"], BF16)

    def gla_out_bwd(o, g_out, d, gn):
        _, vjp = jax.vjp(_head_rms_gate, o, g_out, gn)
        return vjp(d.astype(F32))

    do_raw, dmain, d_gla_norm_g = _rowwise("gla_out_bwd", gla_out_bwd, [o_raw, _Win(proj, GLA_DV, 2), d_ya_in],
                                           [W["gla_norm_g"]], [_sds((S, GLA_DV)), _Win(dmain, GLA_DV, 2)],
                                           [_sds((1, GLA_DV))])
    dmain, dflow, d_wfu_pad, d_b_f = _gla_bwd(proj, W["w_fu_pad"], W["b_f"], states, do_raw, dmain)
    dmixed, d_w_pool, d_pool_scale = _pool_lin_bwd(d_pool_out, mixed, W["w_pool"], W["pool_scale"])
    dmain = _pool_win_bwd(dmixed, dmain)
    dmain, dkv = _xattn_bwd(proj, kv, d_xa, dmain)
    d_w_mem_kv = _mm_tn("d_w_mem_kv", mem_n, dkv)
    dmem_n = _mm_nt("d_mem_n", dkv, W["w_mem_kv"])
    d_mem_norm_g = _rowwise("mem_norm_bwd", lambda m, d, g: _rms_bwd(m, g, d)[1], [mem, dmem_n], [W["mem_norm_g"]], [],
                            [_sds((1, D_MODEL))])[0]
    d_w_int_t = (_mm_tn("d_w_int", dmain, h, ts=1024, tn_a=INT_MAIN // 4), _mm_tn("d_w_flow", dflow, h, ts=2048))
    grads = dict(
        w_int_t=d_w_int_t, w_fu_pad=d_wfu_pad, b_f=d_b_f, gla_norm_g=d_gla_norm_g, w_pool=d_w_pool,
        pool_scale=d_pool_scale, mem_norm_g=d_mem_norm_g, w_mem_kv=d_w_mem_kv, w_up_gla=d_w_up_gla,
        w_up_pool=d_w_up_pool, w_up_xattn=d_w_up_xattn, w_o=d_w_o, mix_post_g=d_mix_post_g)
    deps = [on_grads(grads)] if on_grads is not None else []
    dx1, grads["mix_pre_g"] = _dh_pre_norm_bwd("d_mix_h", [dmain, dflow], W["w_int_t"], x1, dx2, W["mix_pre_g"], deps)
    return dx1, grads


def _mesh_pos():
    x, y, c = lax.axis_index("x"), lax.axis_index("y"), lax.axis_index("c")
    return x, y, c, 4 * x + 2 * y + c


def _peer(x, y, c, r):
    px = 1 - x if r & 4 else x
    py = 1 - y if r & 2 else y
    pc = 1 - c if r & 1 else c
    return (px, py, pc), 4 * px + 2 * py + pc


_ALL_PEERS = tuple(range(1, N_DEV))
_SIBLING = 1
_SAME_CORE = (2, 4, 6)


def _dev_slot(ref, dev):
    return ref.at[dev]


class _Plan:
    def __init__(self, scatter, slots=None, shapes=None):
        self.scatter, self.slots, self.shapes = scatter, slots or {}, shapes or {}

    def src(self, srcs, a, dev):
        return self.slots.get(a, _dev_slot)(srcs[a], dev) if self.scatter else srcs[a]

    def dst(self, lands, a, dev):
        return lands[a].at[dev] if self.scatter else self.slots.get(a, _dev_slot)(lands[a], dev)

    def landing_zones(self, arrays):
        lands = []
        for a, arr in enumerate(arrays):
            if self.scatter:
                lands.append(lax.empty((N_DEV,) + tuple(self.shapes.get(a, arr.shape[1:])), arr.dtype))
            elif a in self.shapes:
                lands.append(jnp.zeros(self.shapes[a], arr.dtype))
            else:
                lands.append(lax.empty((N_DEV,) + arr.shape, arr.dtype))
        return lands


_GATHER, _SCATTER = _Plan(False), _Plan(True)


def _peer_copies(srcs, lands, send_sems, recv_sems, plan, peers=_ALL_PEERS):
    x, y, c, me = _mesh_pos()
    cps = []
    for r in peers:
        pos, peer = _peer(x, y, c, r)
        for a in range(len(srcs)):
            k = a * (N_DEV - 1) + r - 1
            cps.append(pltpu.make_async_remote_copy(
                src_ref=plan.src(srcs, a, peer), dst_ref=plan.dst(lands, a, me),
                send_sem=send_sems.at[k], recv_sem=recv_sems.at[k], device_id=pos,
                device_id_type=pl.DeviceIdType.MESH))
    return cps


def _exchange(name, arrays, plan, after):
    n = len(arrays)

    def body(*refs):
        ins, outs = refs[:n], refs[n + len(after):2 * n + len(after)]
        send_sems, recv_sems, local_sems = refs[2 * n + len(after):]
        copies = _own_copies(ins, outs, local_sems, plan) + _peer_copies(ins, outs, send_sems, recv_sems, plan)
        for cp in copies:
            cp.start()
        for cp in copies:
            cp.wait()

    any_spec = pl.BlockSpec(memory_space=pl.ANY)
    out_shape = [_sds(l.shape, l.dtype) for l in plan.landing_zones(arrays)]
    return pl.pallas_call(
        body, name=name, in_specs=[any_spec] * (n + len(after)), out_specs=[any_spec] * n, out_shape=out_shape,
        scratch_shapes=[pltpu.SemaphoreType.DMA((n * (N_DEV - 1),)), pltpu.SemaphoreType.DMA((n * (N_DEV - 1),)),
                        pltpu.SemaphoreType.DMA((n,))])(*arrays, *after)


_HBM = pl.BlockSpec(memory_space=pltpu.HBM)
_SEM = pl.BlockSpec(memory_space=pltpu.SEMAPHORE)
_EFFECT = pltpu.SideEffectType.DATAFLOW_SIDE_EFFECTING


def _own_copies(srcs, lands, own_sems, plan):
    me = _mesh_pos()[3]
    return [pltpu.make_async_copy(plan.src(srcs, a, me), plan.dst(lands, a, me), own_sems.at[a])
            for a in range(len(srcs))]


def _exchange_start(name, arrays, plan):
    n = len(arrays)
    lands = plan.landing_zones(arrays)
    n_sem = n * (N_DEV - 1)

    def body(*refs):
        srcs, lands_ = refs[:n], refs[n:2 * n]
        send_sems, recv_sems, own_sems = refs[2 * n:2 * n + 3]
        token = refs[-1]
        for cp in _peer_copies(srcs, lands_, send_sems, recv_sems, plan) + _own_copies(srcs, lands_, own_sems, plan):
            cp.start()
        token[...] = jnp.zeros_like(token)

    hbm = lambda a: pltpu.HBM(a.shape, a.dtype)
    res = pl.pallas_call(
        body, name=name,
        out_shape=(pltpu.SemaphoreType.DMA((n_sem,)), pltpu.SemaphoreType.DMA((n_sem,)), pltpu.SemaphoreType.DMA((n,)),
                   *[hbm(a) for a in arrays], *[hbm(a) for a in lands], _sds((8, 128))),
        in_specs=[_HBM] * (2 * n),
        out_specs=(_SEM, _SEM, _SEM, *[_HBM] * (2 * n), pl.BlockSpec(memory_space=pltpu.VMEM)),
        input_output_aliases={i: 3 + i for i in range(2 * n)},
        compiler_params=pltpu.CompilerParams(has_side_effects=_EFFECT),
    )(*[pltpu.with_memory_space_constraint(a, pltpu.HBM) for a in list(arrays) + lands])
    return (res[:3], res[3:3 + n], res[3 + n:3 + 2 * n], plan), res[-1]


def _exchange_wait(name, state, after):
    sems, srcs, lands, plan = state
    n = len(srcs)

    def body(*refs):
        srcs_, lands_ = refs[:n], refs[n:2 * n]
        send_sems, recv_sems, own_sems = refs[2 * n:2 * n + 3]
        for cp in _peer_copies(srcs_, lands_, send_sems, recv_sems, plan):
            cp.wait_send()
            cp.wait_recv()
        for cp in _own_copies(srcs_, lands_, own_sems, plan):
            cp.wait()

    hbm = lambda a: pltpu.HBM(a.shape, a.dtype)
    res = pl.pallas_call(
        body, name=name, out_shape=tuple(hbm(a) for a in list(srcs) + list(lands)),
        in_specs=[_HBM] * (2 * n) + [_SEM] * 3 + [pl.BlockSpec(memory_space=pl.ANY)] * len(after),
        out_specs=tuple([_HBM] * (2 * n)), input_output_aliases={i: i for i in range(2 * n)},
        compiler_params=pltpu.CompilerParams(has_side_effects=_EFFECT),
    )(*srcs, *lands, *sems, *after)
    return res[n:]


def _gather_start(name, arrays, plan, after):
    n = len(arrays)
    lands = plan.landing_zones(arrays)
    n_sem = n * (N_DEV - 1)

    def body(*refs):
        srcs, lands_ = refs[:n], refs[n:2 * n]
        send_sems, recv_sems, own_sems = refs[2 * n + len(after):2 * n + len(after) + 3]
        token = refs[-1]
        for cp in (_peer_copies(srcs, lands_, send_sems, recv_sems, plan, (_SIBLING,) + _SAME_CORE)
                   + _own_copies(srcs, lands_, own_sems, plan)):
            cp.start()
        token[...] = jnp.zeros_like(token)

    hbm = lambda a: pltpu.HBM(a.shape, a.dtype)
    res = pl.pallas_call(
        body, name=name,
        out_shape=(pltpu.SemaphoreType.DMA((n_sem,)), pltpu.SemaphoreType.DMA((n_sem,)), pltpu.SemaphoreType.DMA((n,)),
                   *[hbm(a) for a in arrays], *[hbm(a) for a in lands], _sds((8, 128))),
        in_specs=[_HBM] * (2 * n) + [pl.BlockSpec(memory_space=pl.ANY)] * len(after),
        out_specs=(_SEM, _SEM, _SEM, *[_HBM] * (2 * n), pl.BlockSpec(memory_space=pltpu.VMEM)),
        input_output_aliases={i: 3 + i for i in range(2 * n)},
        compiler_params=pltpu.CompilerParams(has_side_effects=_EFFECT),
    )(*[pltpu.with_memory_space_constraint(a, pltpu.HBM) for a in list(arrays) + lands], *after)
    return (res[:3], res[3:3 + n], res[3 + n:3 + 2 * n], plan), res[-1]


def _pass_on_copies(lands, send_sems, recv_sems, plan):
    x, y, c, _ = _mesh_pos()
    sibling = _peer(x, y, c, _SIBLING)[0]
    cps = []
    for i, r in enumerate(_SAME_CORE):
        owner = _peer(x, y, c, r)[1]
        for a in range(len(lands)):
            k = a * len(_SAME_CORE) + i
            cps.append(pltpu.make_async_remote_copy(
                src_ref=plan.dst(lands, a, owner), dst_ref=plan.dst(lands, a, owner), send_sem=send_sems.at[k],
                recv_sem=recv_sems.at[k], device_id=sibling, device_id_type=pl.DeviceIdType.MESH))
    return cps


def _gather_pass_on(name, state, after):
    sems, srcs, lands, plan = state
    n = len(srcs)
    n_sem = n * len(_SAME_CORE)

    def body(*refs):
        srcs_, lands_ = refs[:n], refs[n:2 * n]
        send_sems, recv_sems = refs[2 * n], refs[2 * n + 1]
        on_send, on_recv = refs[2 * n + 3 + len(after)], refs[2 * n + 4 + len(after)]
        arrivals = _peer_copies(srcs_, lands_, send_sems, recv_sems, plan, _SAME_CORE)
        for arrived, on in zip(arrivals, _pass_on_copies(lands_, on_send, on_recv, plan)):
            arrived.wait_recv()
            on.start()

    hbm = lambda a: pltpu.HBM(a.shape, a.dtype)
    res = pl.pallas_call(
        body, name=name,
        out_shape=(pltpu.SemaphoreType.DMA((n_sem,)), pltpu.SemaphoreType.DMA((n_sem,)),
                   *[hbm(a) for a in list(srcs) + list(lands)]),
        in_specs=[_HBM] * (2 * n) + [_SEM] * 3 + [pl.BlockSpec(memory_space=pl.ANY)] * len(after),
        out_specs=(_SEM, _SEM, *[_HBM] * (2 * n)), input_output_aliases={i: 2 + i for i in range(2 * n)},
        compiler_params=pltpu.CompilerParams(has_side_effects=_EFFECT),
    )(*srcs, *lands, *sems, *after)
    return sems, res[:2], res[2:2 + n], res[2 + n:], plan


def _gather_wait(name, state, after):
    sems, on_sems, srcs, lands, plan = state
    n = len(srcs)

    def body(*refs):
        srcs_, lands_ = refs[:n], refs[n:2 * n]
        send_sems, recv_sems, own_sems, on_send, on_recv = refs[2 * n:2 * n + 5]
        for cp in _peer_copies(srcs_, lands_, send_sems, recv_sems, plan, (_SIBLING,)):
            cp.wait_recv()
        for cp in _peer_copies(srcs_, lands_, send_sems, recv_sems, plan, (_SIBLING,) + _SAME_CORE):
            cp.wait_send()
        for cp in _own_copies(srcs_, lands_, own_sems, plan):
            cp.wait()
        for cp in _pass_on_copies(lands_, on_send, on_recv, plan):
            cp.wait_send()
            cp.wait_recv()

    hbm = lambda a: pltpu.HBM(a.shape, a.dtype)
    res = pl.pallas_call(
        body, name=name, out_shape=tuple(hbm(a) for a in list(srcs) + list(lands)),
        in_specs=[_HBM] * (2 * n) + [_SEM] * 5 + [pl.BlockSpec(memory_space=pl.ANY)] * len(after),
        out_specs=tuple([_HBM] * (2 * n)), input_output_aliases={i: i for i in range(2 * n)},
        compiler_params=pltpu.CompilerParams(has_side_effects=_EFFECT),
    )(*srcs, *lands, *sems, *on_sems, *after)
    return res[n:]


def _adam(name, recv, w, m, v):
    shape = w.shape
    R, C = shape
    tr = R
    while N_DEV * tr * C * 4 > 6 * 1024 * 1024 and tr % 32 == 0:
        tr //= 2

    def body(recv_ref, w_ref, m_ref, v_ref, g_ref, d_ref, m2_ref, v2_ref):
        g = recv_ref[0].astype(F32)
        for j in range(1, N_DEV):
            g = g + recv_ref[j].astype(F32)
        w_, m_, v_ = w_ref[...], m_ref[...], v_ref[...]
        m2 = ADAM_B1 * m_ + (1.0 - ADAM_B1) * g
        v2 = ADAM_B2 * v_ + (1.0 - ADAM_B2) * (g * g)
        m_hat = m2 / (1.0 - ADAM_B1 ** ADAM_STEP)
        v_hat = v2 / (1.0 - ADAM_B2 ** ADAM_STEP)
        g_ref[...] = g
        d_ref[...] = -ADAM_LR * (m_hat / (jnp.sqrt(v_hat) + ADAM_EPS) + ADAM_WD * w_)
        m2_ref[...] = m2
        v2_ref[...] = v2

    blk = pl.BlockSpec((tr, C), lambda i: (i, 0))
    return _pcall(body, name=name, grid=(R // tr,),
                  in_specs=[pl.BlockSpec((N_DEV, tr, C), lambda i: (0, i, 0)), blk, blk, blk],
                  out_specs=[blk] * 4, out_shape=[_sds(shape)] * 4)(recv, w, m, v)


_NAMES = ['ffn1_pre_g', 'ffn1_w_in', 'ffn1_w_out', 'ffn1_post_g', 'mix_pre_g', 'w_in', 'w_fu', 'b_f', 'gla_norm_g',
          'w_pool', 'pool_scale', 'mem_norm_g', 'w_mem_kv', 'w_up_gla', 'w_up_pool', 'w_up_xattn', 'w_o', 'mix_post_g',
          'ffn2_pre_g', 'ffn2_w_in', 'ffn2_w_out', 'ffn2_post_g', 'final_g']
_SHARDED = ['ffn1_w_in', 'ffn1_w_out', 'w_in', 'w_fu', 'w_mem_kv', 'w_up_gla', 'w_up_pool', 'w_up_xattn', 'w_o',
            'ffn2_w_in', 'ffn2_w_out']
_COL_SHARDED = ['w_up_pool', 'w_up_xattn']


def _cols_to_full(g):
    return jnp.transpose(g, (1, 0, 2)).reshape(g.shape[1], N_DEV * g.shape[2])


def _full_to_cols(f):
    R, C = f.shape
    return jnp.transpose(f.reshape(R, N_DEV, C // N_DEV), (1, 0, 2))


def _to_internal(w_in_t):
    o = 0
    parts = []
    for s in IN_SPLITS:
        parts.append(w_in_t[o:o + s])
        o += s
    q, k, v, g_out, f_low, p_in, xq, gates = parts
    f_low = jnp.pad(f_low, ((0, FLOW_W - GATE_RANK), (0, 0)))
    return jnp.concatenate([q, k, v, g_out, gates, p_in, xq, f_low], axis=0)


def _from_internal(d, d_flow):
    q, k, v, g_out = d[0:512], d[512:1024], d[1024:2048], d[2048:3072]
    gates, p_in, xq = d[3072:6144], d[6144:6656], d[6656:7168]
    return jnp.concatenate([q, k, v, g_out, d_flow[:GATE_RANK], p_in, xq, gates], axis=0)


_FFN_IN = ('ffn1_w_in', 'ffn2_w_in')
_FFN_OUT = ('ffn1_w_out', 'ffn2_w_out')
_GATHERS = {"ffn1_in": ['ffn1_w_in'], "ffn1_out": ['ffn1_w_out'],
            "mix": ['w_in', 'w_fu', 'w_mem_kv', 'w_up_gla', 'w_up_pool', 'w_up_xattn', 'w_o'],
            "ffn2": ['ffn2_w_in', 'ffn2_w_out']}


def _ffn_in_slot(ref, d):
    return ref.at[2 * (d % N_FF_BLK) + d // N_FF_BLK, pl.ds(0, FF_BLK)]


def _ffn_out_slot(ref, d):
    rows = FF_BLK // 2
    return ref.at[d // 2, pl.ds(pl.multiple_of((d % 2) * rows, rows), rows)]


def _ffn_plan(names, scatter):
    slots, shapes = {}, {}
    for a, n in enumerate(names):
        if n in _FFN_IN:
            slots[a] = _ffn_in_slot
            shapes[a] = (FF_BLK, D_MODEL) if scatter else (N_DEV, FF_PAD, D_MODEL)
        elif n in _FFN_OUT:
            slots[a] = _ffn_out_slot
            shapes[a] = (FF_BLK // 2, D_MODEL) if scatter else (N_FF_BLK, FF_PAD, D_MODEL)
    return _Plan(scatter, slots, shapes)


def _rows_to_full(g):
    return g.reshape(N_DEV * g.shape[1], g.shape[2])


def _mix_weights(gathered):
    W = {n: _rows_to_full(gathered[n]) for n in ('w_mem_kv', 'w_up_gla', 'w_o')}
    for n in _COL_SHARDED:
        W[n] = _cols_to_full(gathered[n])
    W["w_int_t"] = _to_internal(gathered["w_in"].reshape(IN_WIDTH, D_MODEL))
    W["w_fu_pad"] = jnp.pad(_cols_to_full(gathered["w_fu"]), ((0, FLOW_W - GATE_RANK), (0, 0)))
    return W


def _mix_chunks(G, n):
    if n == "w_in":
        return _from_internal(*G["w_int_t"]).reshape(N_DEV, IN_SHARD, D_MODEL)
    if n == "w_fu":
        return _full_to_cols(G["w_fu_pad"][:GATE_RANK].astype(BF16))
    if n in _COL_SHARDED:
        return _full_to_cols(G[n])
    return G[n].reshape(N_DEV, G[n].shape[0] // N_DEV, G[n].shape[1])


def _step(x, mem, tgt, P, Mo, Vo):
    def native(n, a):
        return jnp.swapaxes(a, 0, 1) if n in _FFN_IN + ("w_in",) else a

    P, Mo, Vo = ({n: native(n, a) for n, a in d.items()} for d in (P, Mo, Vo))
    small = {n: P[n] for n in _NAMES if n not in _SHARDED}

    gather, token = {}, []
    for grp, names in _GATHERS.items():
        gather[grp], tok = _gather_start("gather_" + grp, [P[n].astype(BF16) for n in names], _ffn_plan(names, False),
                                         token)
        token = [tok]

    def gathered(grp, after):
        return _gather_wait("gather_" + grp + "_wait", _gather_pass_on("gather_" + grp + "_on", gather[grp], after), [])

    def ffn_wo(lands):
        return lambda act: lands[0].reshape(N_FF_BLK * FF_PAD, D_MODEL)

    h1 = _pre_norm("ffn1_pre", x, small["ffn1_pre_g"])
    w1t = gathered("ffn1_in", token + [h1])[0].reshape(N_DEV * FF_PAD, D_MODEL)
    (x1, hm), w1o, sv1 = _ffn_fwd("ffn1", x, h1, w1t, small["ffn1_post_g"],
                                  lambda act: ffn_wo(gathered("ffn1_out", [act]))(act), small["mix_pre_g"])

    gm = dict(zip(_GATHERS["mix"], gathered("mix", [x1])))
    Wm = {**small, **_mix_weights(gm)}
    x2, h2, svm = _mix_fwd(x1, hm, mem, Wm, small["ffn2_pre_g"])

    w2t, w2o = gathered("ffn2", [x2])
    w2t = w2t.reshape(N_DEV * FF_PAD, D_MODEL)
    (loss, dx3, d_final_g), w2o, sv2 = _ffn_fwd("ffn2", x2, h2, w2t, small["ffn2_post_g"], ffn_wo([w2o]),
                                                small["final_g"], tgt)

    G = dict(final_g=d_final_g)
    scat = {}

    def start(grp, names, arrays):
        scat[grp] = names, _exchange_start("scatter_" + grp, arrays, _ffn_plan(names, True))
        return scat[grp][1][1]

    def ffn_starts(tag):
        return (lambda dwo: start(tag + "_out", [tag + "_w_out"], [dwo.reshape(N_FF_BLK, FF_PAD, D_MODEL)]),
                lambda dwt: start(tag + "_in", [tag + "_w_in"], [dwt.reshape(N_DEV, FF_PAD, D_MODEL)]))

    dx2, G["ffn2_pre_g"], G["ffn2_post_g"] = _ffn_bwd(
        "ffn2", dx3, x2, small["ffn2_pre_g"], w2t, w2o, small["ffn2_post_g"], sv2, *ffn_starts("ffn2"))
    dx1, Gm = _mix_bwd(dx2, x1, mem, Wm, svm,
                       lambda Gm: start("mix", _GATHERS["mix"], [_mix_chunks(Gm, n) for n in _GATHERS["mix"]]))
    G.update(Gm)
    dx, G["ffn1_pre_g"], G["ffn1_post_g"] = _ffn_bwd(
        "ffn1", dx1, x, small["ffn1_pre_g"], w1t, w1o, small["ffn1_post_g"], sv1, *ffn_starts("ffn1"))

    recv, outs = {}, {}
    done = [dx]
    for grp in ["ffn2_out", "ffn2_in", "mix", "small", "ffn1_out", "ffn1_in"]:
        if grp == "small":
            names = list(small)
            recv.update(zip(names, _exchange("gather_small_grads", [G[n] for n in small], _GATHER, done)))
        else:
            names, (state, _) = scat[grp]
            recv.update(zip(names, _exchange_wait("scatter_" + grp + "_wait", state, done[-1:])))
        for n in names:
            shp = P[n].shape
            shp2 = shp if len(shp) == 2 else (shp[0] * shp[1], shp[2])
            res = _adam("adam_" + n, recv[n].reshape((N_DEV,) + shp2), P[n].reshape(shp2), Mo[n].reshape(shp2),
                        Vo[n].reshape(shp2))
            outs[n] = [native(n, t.reshape(shp)) for t in res]
            done.append(res[-1])
    return loss, dx, outs


def kernel(x, mem, ffn1_pre_g, ffn1_w_in, ffn1_w_out, ffn1_post_g, mix_pre_g, w_in, w_fu, b_f, gla_norm_g, w_pool, pool_scale, mem_norm_g, w_mem_kv, w_up_gla, w_up_pool, w_up_xattn, w_o, mix_post_g, ffn2_pre_g, ffn2_w_in, ffn2_w_out, ffn2_post_g, final_g, loss_target, m_ffn1_pre_g, m_ffn1_w_in, m_ffn1_w_out, m_ffn1_post_g, m_mix_pre_g, m_w_in, m_w_fu, m_b_f, m_gla_norm_g, m_w_pool, m_pool_scale, m_mem_norm_g, m_w_mem_kv, m_w_up_gla, m_w_up_pool, m_w_up_xattn, m_w_o, m_mix_post_g, m_ffn2_pre_g, m_ffn2_w_in, m_ffn2_w_out, m_ffn2_post_g, m_final_g, v_ffn1_pre_g, v_ffn1_w_in, v_ffn1_w_out, v_ffn1_post_g, v_mix_pre_g, v_w_in, v_w_fu, v_b_f, v_gla_norm_g, v_w_pool, v_pool_scale, v_mem_norm_g, v_w_mem_kv, v_w_up_gla, v_w_up_pool, v_w_up_xattn, v_w_o, v_mix_post_g, v_ffn2_pre_g, v_ffn2_w_in, v_ffn2_w_out, v_ffn2_post_g, v_final_g):
    params = [ffn1_pre_g, ffn1_w_in, ffn1_w_out, ffn1_post_g, mix_pre_g, w_in, w_fu, b_f, gla_norm_g, w_pool, pool_scale, mem_norm_g, w_mem_kv, w_up_gla, w_up_pool, w_up_xattn, w_o, mix_post_g, ffn2_pre_g, ffn2_w_in, ffn2_w_out, ffn2_post_g, final_g]
    moms = [m_ffn1_pre_g, m_ffn1_w_in, m_ffn1_w_out, m_ffn1_post_g, m_mix_pre_g, m_w_in, m_w_fu, m_b_f, m_gla_norm_g, m_w_pool, m_pool_scale, m_mem_norm_g, m_w_mem_kv, m_w_up_gla, m_w_up_pool, m_w_up_xattn, m_w_o, m_mix_post_g, m_ffn2_pre_g, m_ffn2_w_in, m_ffn2_w_out, m_ffn2_post_g, m_final_g]
    vars_ = [v_ffn1_pre_g, v_ffn1_w_in, v_ffn1_w_out, v_ffn1_post_g, v_mix_pre_g, v_w_in, v_w_fu, v_b_f, v_gla_norm_g, v_w_pool, v_pool_scale, v_mem_norm_g, v_w_mem_kv, v_w_up_gla, v_w_up_pool, v_w_up_xattn, v_w_o, v_mix_post_g, v_ffn2_pre_g, v_ffn2_w_in, v_ffn2_w_out, v_ffn2_post_g, v_final_g]
    P = {n: a[0] if a.ndim > 2 else a for n, a in zip(_NAMES, params)}
    Mo = {n: a[0] if a.ndim > 2 else a for n, a in zip(_NAMES, moms)}
    Vo = {n: a[0] if a.ndim > 2 else a for n, a in zip(_NAMES, vars_)}
    loss, dx, outs = _step(x[0], mem[0], loss_target[0], P, Mo, Vo)
    loss = lax.psum(loss, ("x", "y", "c"))
    out = [loss, dx[None]]
    for kind in range(4):
        for n, p in zip(_NAMES, params):
            out.append(outs[n][kind].reshape(p.shape))
    return tuple(out)
```

```python
import functools

import jax
import jax.numpy as jnp
from jax import lax
from jax.experimental import pallas as pl
from jax.experimental.pallas import tpu as pltpu

F32 = jnp.float32
BF16 = jnp.bfloat16

N_DEV = 8
D_MODEL = 1024
D_FF = 2816
FF_BLK = 2 * D_FF // N_DEV
N_FF_BLK = D_FF // FF_BLK
FF_PAD = 768
CHUNK = 64
GLA_HEADS = 4
GLA_DK = 512
GLA_DV = 1024
GLA_HDK = 128
GLA_HDV = 256
GATE_RANK = 16
GATE_TEMP = 16.0
POOL_WINDOWS = (2, 4, 8, 16)
POOL_W = 512
POOL_GD = 128
POOL_HALO = 16
XA_HEADS = 4
XA_HD = 128
XA_W = 512
EPS = 1e-6
IN_SPLITS = (GLA_DK, GLA_DK, GLA_DV, GLA_DV, GATE_RANK, POOL_W, XA_W, 3 * D_MODEL)
IN_WIDTH = sum(IN_SPLITS)
IN_SHARD = IN_WIDTH // N_DEV
INT_W = 3072 + 3072 + 1024 + 128
INT_NBLK = 3
INT_BLK = INT_W // INT_NBLK
FLOW_W = 128
INT_MAIN = INT_W - FLOW_W
_POOL_COL = (3072 + 3072) // POOL_W
_XQ_COL = (3072 + 3072 + POOL_W) // XA_W

ADAM_LR = 0.001
ADAM_B1 = 0.9
ADAM_B2 = 0.999
ADAM_EPS = 1e-08
ADAM_WD = 0.01
ADAM_STEP = 10

VMEM_LIMIT = 56 * 1024 * 1024

_NN = (((1,), (0,)), ((), ()))
_NT = (((1,), (1,)), ((), ()))
_TN = (((0,), (0,)), ((), ()))


def _pcall(body, *, name, grid, in_specs, out_specs, out_shape, scratch=(), aliases=None):
    return pl.pallas_call(
        body, name=name, grid=grid, in_specs=in_specs, out_specs=out_specs, out_shape=out_shape,
        scratch_shapes=list(scratch), input_output_aliases=aliases or {},
        compiler_params=pltpu.CompilerParams(dimension_semantics=("arbitrary",) * len(grid),
                                             vmem_limit_bytes=VMEM_LIMIT))


def _dot(a, b, dims=_NN):
    return lax.dot_general(a.astype(BF16), b.astype(BF16), dims, preferred_element_type=F32)


def _mm(name, a, b, *, grid, a_blk, a_map, b_blk, b_map, o_shape, o_blk, o_map, dims, out_dtype=F32, deps=()):
    nk = grid[2]

    def body(a_ref, b_ref, *rest):
        o_ref, scr = rest[len(deps)], rest[len(deps) + 1:]
        p = _dot(a_ref[...], b_ref[...], dims)
        if nk == 1:
            o_ref[...] = p.astype(o_ref.dtype)
        else:
            acc = scr[0]
            k = pl.program_id(2)

            @pl.when(k == 0)
            def _():
                acc[...] = p

            @pl.when(k > 0)
            def _():
                acc[...] += p

            @pl.when(k == nk - 1)
            def _():
                o_ref[...] = acc[...].astype(o_ref.dtype)

    acc_shape = tuple(d for d in o_blk if d is not None)
    return _pcall(body, name=name, grid=grid,
                  in_specs=[pl.BlockSpec(a_blk, a_map), pl.BlockSpec(b_blk, b_map)]
                  + [pl.BlockSpec(memory_space=pl.ANY)] * len(deps),
                  out_specs=pl.BlockSpec(o_blk, o_map),
                  out_shape=jax.ShapeDtypeStruct(o_shape, out_dtype),
                  scratch=[pltpu.VMEM(acc_shape, F32)] if nk > 1 else [])(a, b, *deps)


def _mm_nn(name, a, b, out_dtype=F32, tm=1024, tn=None):
    M, K = a.shape
    N = b.shape[1]
    tm, tn = min(tm, M), (tn or N)
    return _mm(name, a, b, grid=(N // tn, M // tm, 1), a_blk=(tm, K), a_map=lambda j, i, k: (i, 0),
               b_blk=(K, tn), b_map=lambda j, i, k: (0, j), o_shape=(M, N), o_blk=(tm, tn),
               o_map=lambda j, i, k: (i, j), dims=_NN, out_dtype=out_dtype)


def _mm_nt(name, a, b, out_dtype=F32, tm=1024):
    M, K = a.shape
    N = b.shape[0]
    tm = min(tm, M)
    return _mm(name, a, b, grid=(1, M // tm, 1), a_blk=(tm, K), a_map=lambda j, i, k: (i, 0),
               b_blk=(N, K), b_map=lambda j, i, k: (0, 0), o_shape=(M, N), o_blk=(tm, N),
               o_map=lambda j, i, k: (i, 0), dims=_NT, out_dtype=out_dtype)


def _mm_tn(name, a, b, out_dtype=BF16, ts=512, tn=None, tn_a=None, deps=()):
    S, M = a.shape
    N = b.shape[1]
    ts, tn, tn_a = min(ts, S), (tn or N), (tn_a or M)
    return _mm(name, a, b, grid=((N // tn) * (M // tn_a), 1, S // ts), a_blk=(ts, tn_a),
               a_map=lambda j, i, k: (k, j if tn_a < M else 0), b_blk=(ts, tn),
               b_map=lambda j, i, k: (k, j if tn < N else 0), o_shape=(M, N), o_blk=(tn_a, tn),
               o_map=lambda j, i, k: (j, 0) if tn_a < M else (0, j), dims=_TN, out_dtype=out_dtype, deps=deps)


class _Win:
    def __init__(self, arr, w, c):
        self.arr, self.w, self.c = arr, w, c


def _row_spec(x, tm):
    if isinstance(x, _Win):
        return x.arr, pl.BlockSpec((tm, x.w), functools.partial(lambda i, c: (i, c), c=x.c))
    if x.ndim == 3:
        return x, pl.BlockSpec((x.shape[0], tm, x.shape[2]), lambda i: (0, i, 0))
    return x, pl.BlockSpec((tm, x.shape[1]), lambda i: (i, 0))


def _rowwise(name, fn, rows, consts, outs, accs=(), tm=512):
    first = rows[0].arr if isinstance(rows[0], _Win) else rows[0]
    S = first.shape[1] if first.ndim == 3 else first.shape[0]
    tm = min(tm, S)
    n_in, n_out = len(rows) + len(consts), len(outs)
    arrays, in_specs = [], []
    for r in rows:
        arr, spec = _row_spec(r, tm)
        arrays.append(arr)
        in_specs.append(spec)
    for c in consts:
        arrays.append(c)
        in_specs.append(pl.BlockSpec(c.shape, functools.partial(lambda i, n: (0,) * n, n=c.ndim)))
    aliases = {}
    for k, o in enumerate(outs):
        if isinstance(o, _Win):
            aliases[len(arrays)] = k
            arrays.append(o.arr)
            in_specs.append(pl.BlockSpec(memory_space=pl.ANY))
    n_thru = len(aliases)
    out_specs = [_row_spec(o, tm)[1] for o in outs]
    out_specs += [pl.BlockSpec(a.shape, functools.partial(lambda i, n: (0,) * n, n=len(a.shape))) for a in accs]
    out_shape = [_sds(o.arr.shape, o.arr.dtype) if isinstance(o, _Win) else _sds(o.shape, o.dtype) for o in outs]

    def body(*refs):
        res = fn(*[r[...] for r in refs[:n_in]])
        if not isinstance(res, (tuple, list)):
            res = (res,)
        orefs = refs[n_in + n_thru:]
        for r, v in zip(orefs[:n_out], res[:n_out]):
            r[...] = v.astype(r.dtype)
        i = pl.program_id(0)
        for r, v in zip(orefs[n_out:], res[n_out:]):
            @pl.when(i == 0)
            def _(r=r, v=v):
                r[...] = v.astype(r.dtype)

            @pl.when(i > 0)
            def _(r=r, v=v):
                r[...] += v.astype(r.dtype)

    return _pcall(body, name=name, grid=(S // tm,), in_specs=in_specs, out_specs=out_specs,
                  out_shape=out_shape + [_sds(a.shape, a.dtype) for a in accs], aliases=aliases)(*arrays)


def _sds(shape, dtype=F32):
    return jax.ShapeDtypeStruct(shape, dtype)


def _rms(x, g):
    return x * lax.rsqrt(jnp.mean(x * x, axis=-1, keepdims=True) + EPS) * g


def _rms_bwd(x, g, dy):
    _, vjp = jax.vjp(_rms, x, g)
    return vjp(dy)


def _sigmoid(x):
    return 0.5 * jnp.tanh(0.5 * x) + 0.5


def _silu(x):
    return x * _sigmoid(x)


def _log_sigmoid(f):
    return jnp.minimum(f, 0.0) - jnp.log(1.0 + jnp.exp(-jnp.abs(f)))


def _head_rms_gate(o, g_out, gn):
    parts = [_rms(o[:, h * GLA_HDV:(h + 1) * GLA_HDV], gn[:, h * GLA_HDV:(h + 1) * GLA_HDV]) for h in range(GLA_HEADS)]
    return jnp.concatenate(parts, axis=-1) * _silu(g_out.astype(F32))


def _merge(gates, ya, yb, yc):
    gates, ya, yb, yc = (t.astype(F32) for t in (gates, ya, yb, yc))
    return (_sigmoid(gates[:, :D_MODEL]) * ya + _sigmoid(gates[:, D_MODEL:2 * D_MODEL]) * yb
            + _sigmoid(gates[:, 2 * D_MODEL:]) * yc)


def _tri_dot(t, x):
    hi = x.astype(BF16)
    r1 = x - hi.astype(F32)
    mid = r1.astype(BF16)
    lo = (r1 - mid.astype(F32)).astype(BF16)
    d = functools.partial(lax.dot_general, dimension_numbers=_NN, preferred_element_type=F32)
    return d(t, hi) + d(t, mid) + d(t, lo)


def _pre_norm(name, x, g):
    return _rowwise(name, lambda x, g: _rms(x, g), [x], [g], [_sds(x.shape, BF16)])[0]


def _ffn_fwd(tag, x, h, wt, gpost, get_wo, g_next, tgt=None):
    S = x.shape[0]
    tm = min(512, S)

    def up_body(h_ref, w_ref, u_ref, act_ref):
        hh = h_ref[...]
        for j in range(N_FF_BLK):
            ab = _dot(hh, w_ref[2 * FF_PAD * j:2 * FF_PAD * (j + 1), :], _NT)
            u_ref[:, 2 * FF_PAD * j:2 * FF_PAD * (j + 1)] = ab.astype(BF16)
            act_ref[:, FF_PAD * j:FF_PAD * (j + 1)] = (_silu(ab[:, :FF_PAD]) * ab[:, FF_PAD:]).astype(BF16)

    u, act = _pcall(
        up_body, name=tag + "_up", grid=(S // tm,),
        in_specs=[pl.BlockSpec((tm, D_MODEL), lambda i: (i, 0)),
                  pl.BlockSpec(wt.shape, lambda i: (0, 0), pipeline_mode=pl.Buffered(1))],
        out_specs=[pl.BlockSpec((tm, N_DEV * FF_PAD), lambda i: (i, 0)),
                   pl.BlockSpec((tm, N_FF_BLK * FF_PAD), lambda i: (i, 0))],
        out_shape=[_sds((S, N_DEV * FF_PAD), BF16), _sds((S, N_FF_BLK * FF_PAD), BF16)],
    )(h, wt)

    def down_body(act_ref, w_ref, x_ref, g_ref, gn_ref, f_ref, xo_ref, hn_ref):
        f = _dot(act_ref[...], w_ref[...])
        f_ref[...] = f
        xo = x_ref[...] + 0.5 * _rms(f, g_ref[...])
        xo_ref[...] = xo
        hn_ref[...] = _rms(xo, gn_ref[...]).astype(BF16)

    def down_head_body(act_ref, w_ref, x_ref, g_ref, gn_ref, t_ref, f_ref, dxo_ref, loss_ref, dgn_ref):
        i = pl.program_id(0)
        f = _dot(act_ref[...], w_ref[...])
        f_ref[...] = f
        xo = x_ref[...] + 0.5 * _rms(f, g_ref[...])
        out, vjp = jax.vjp(_rms, xo, gn_ref[...])
        e = out - t_ref[...]
        loss = 0.5 * jnp.sum(jnp.mean(e * e, axis=-1, keepdims=True), axis=0, keepdims=True)
        dxo_ref[...], dgn = vjp(e * (1.0 / D_MODEL))

        @pl.when(i == 0)
        def _():
            loss_ref[...] = jnp.broadcast_to(loss, loss_ref.shape)
            dgn_ref[...] = dgn

        @pl.when(i > 0)
        def _():
            loss_ref[...] += jnp.broadcast_to(loss, loss_ref.shape)
            dgn_ref[...] += dgn

    wo = get_wo(act)
    row = pl.BlockSpec((tm, D_MODEL), lambda i: (i, 0))
    vec = pl.BlockSpec((1, D_MODEL), lambda i: (0, 0))
    if tgt is not None:
        f, dxo, loss, dg_next = _pcall(
            down_head_body, name=tag + "_down", grid=(S // tm,),
            in_specs=[pl.BlockSpec((tm, N_FF_BLK * FF_PAD), lambda i: (i, 0)),
                      pl.BlockSpec(wo.shape, lambda i: (0, 0), pipeline_mode=pl.Buffered(1)), row, vec, vec, row],
            out_specs=[row, row, pl.BlockSpec((1, 128), lambda i: (0, 0)), vec],
            out_shape=[_sds((S, D_MODEL)), _sds((S, D_MODEL)), _sds((1, 128)), _sds((1, D_MODEL))],
        )(act, wo, x, gpost, g_next, tgt)
        return (loss[0, 0], dxo, dg_next), wo, (h, u, act, f)
    f, xo, h_next = _pcall(
        down_body, name=tag + "_down", grid=(S // tm,),
        in_specs=[pl.BlockSpec((tm, N_FF_BLK * FF_PAD), lambda i: (i, 0)),
                  pl.BlockSpec(wo.shape, lambda i: (0, 0), pipeline_mode=pl.Buffered(1)), row, vec, vec],
        out_specs=[row, row, row], out_shape=[_sds((S, D_MODEL)), _sds((S, D_MODEL)), _sds((S, D_MODEL), BF16)],
    )(act, wo, x, gpost, g_next)
    return (xo, h_next), wo, (h, u, act, f)


def _ffn_bwd(tag, dxo, x, gpre, wt, wo, gpost, saved, on_dwo=None, on_dwt=None):
    h, u, act, f = saved
    S = x.shape[0]
    tm = min(256, S)

    def dact_body(f_ref, dxo_ref, g_ref, w_ref, u_ref, df_ref, du_ref, dg_ref):
        i = pl.program_id(0)
        df, dg = _rms_bwd(f_ref[...], g_ref[...], 0.5 * dxo_ref[...])
        df = df.astype(BF16)
        df_ref[...] = df

        @pl.when(i == 0)
        def _():
            dg_ref[...] = dg

        @pl.when(i > 0)
        def _():
            dg_ref[...] += dg

        for j in range(N_FF_BLK):
            dact = _dot(df, w_ref[FF_PAD * j:FF_PAD * (j + 1), :], _NT)
            ab = u_ref[:, 2 * FF_PAD * j:2 * FF_PAD * (j + 1)].astype(F32)
            a, b = ab[:, :FF_PAD], ab[:, FF_PAD:]
            sg = _sigmoid(a)
            du_ref[:, 2 * FF_PAD * j:2 * FF_PAD * j + FF_PAD] = (dact * b * (sg * (1.0 + a * (1.0 - sg)))).astype(BF16)
            du_ref[:, 2 * FF_PAD * j + FF_PAD:2 * FF_PAD * (j + 1)] = (dact * (a * sg)).astype(BF16)

    row = pl.BlockSpec((tm, D_MODEL), lambda i: (i, 0))
    vec = pl.BlockSpec((1, D_MODEL), lambda i: (0, 0))
    u_spec = pl.BlockSpec((tm, N_DEV * FF_PAD), lambda i: (i, 0))
    df, du, dgpost = _pcall(
        dact_body, name=tag + "_dact", grid=(S // tm,),
        in_specs=[row, row, vec, pl.BlockSpec(wo.shape, lambda i: (0, 0), pipeline_mode=pl.Buffered(1)), u_spec],
        out_specs=[row, u_spec, vec],
        out_shape=[_sds((S, D_MODEL), BF16), _sds((S, N_DEV * FF_PAD), BF16), _sds((1, D_MODEL))])(f, dxo, gpost, wo, u)
    dwo = _mm_tn(tag + "_dwo", act, df, ts=2048, tn_a=FF_PAD)
    dwt = _mm_tn(tag + "_dwt", du, h, ts=2048, tn_a=FF_PAD, deps=[on_dwo(dwo)] if on_dwo is not None else [])
    dx, dgpre = _dh_pre_norm_bwd(tag + "_dh", [du], wt, x, dxo, gpre, [on_dwt(dwt)] if on_dwt is not None else [])
    return dx, dgpre, dgpost


def _dh_pre_norm_bwd(name, dzs, wt, x, dres, g, deps):
    S = x.shape[0]
    ks = [dz.shape[1] for dz in dzs]

    def body(*refs):
        dz_refs = refs[:len(dzs)]
        w_ref, x_ref, dres_ref, g_ref = refs[len(dzs):len(dzs) + 4]
        dx_ref, dg_ref = refs[len(dzs) + 4 + len(deps):]
        i = pl.program_id(0)
        dh, o = None, 0
        for dz_ref, k in zip(dz_refs, ks):
            p = _dot(dz_ref[...], w_ref[o:o + k, :])
            dh, o = (p if dh is None else dh + p), o + k
        dx, dg = _rms_bwd(x_ref[...], g_ref[...], dh)
        dx_ref[...] = dx + dres_ref[...]

        @pl.when(i == 0)
        def _():
            dg_ref[...] = dg

        @pl.when(i > 0)
        def _():
            dg_ref[...] += dg

    th = min(256, S)
    row = pl.BlockSpec((th, D_MODEL), lambda i: (i, 0))
    vec = pl.BlockSpec((1, D_MODEL), lambda i: (0, 0))
    return _pcall(
        body, name=name, grid=(S // th,),
        in_specs=[pl.BlockSpec((th, k), lambda i: (i, 0)) for k in ks]
        + [pl.BlockSpec(wt.shape, lambda i: (0, 0), pipeline_mode=pl.Buffered(1)), row, row, vec]
        + [pl.BlockSpec(memory_space=pl.ANY)] * len(deps),
        out_specs=[row, vec], out_shape=[_sds((S, D_MODEL)), _sds((1, D_MODEL))])(*dzs, wt, x, dres, g, *deps)


GLA_G = 8


def _gla_tile_common(k, flow, wfu, bf):
    f = _dot(flow, wfu) + bf
    la = _log_sigmoid(f) * (1.0 / GATE_TEMP)
    tri = _tri_matrix(True)
    ws, ds = [], []
    for g in range(k.shape[0] // CHUNK):
        b = _tri_dot(tri, la[g * CHUNK:(g + 1) * CHUNK])
        b_end = b[CHUNK - 1:CHUNK, :]
        ws.append(jnp.exp(b_end - b))
        ds.append(jnp.exp(b_end))
    w = jnp.concatenate(ws, axis=0)
    return f, w, k * w, ds


def _tri_matrix(lower):
    r = lax.broadcasted_iota(jnp.int32, (CHUNK, CHUNK), 0)
    c = lax.broadcasted_iota(jnp.int32, (CHUNK, CHUNK), 1)
    return jnp.where((r >= c) if lower else (r <= c), 1.0, 0.0).astype(BF16)


def _heads():
    return [(slice(h * GLA_HDK, (h + 1) * GLA_HDK), slice(h * GLA_HDV, (h + 1) * GLA_HDV)) for h in range(GLA_HEADS)]


def _gla_fwd(proj, wfu, bf):
    S = proj.shape[0]
    G = min(GLA_G, S // CHUNK)
    T = G * CHUNK
    nc = S // CHUNK

    def body(q_ref, k_ref, v_ref, flow_ref, wfu_ref, bf_ref, o_ref, st_ref, state):
        @pl.when(pl.program_id(0) == 0)
        def _():
            state[...] = jnp.zeros_like(state)

        _, _, kt, ds = _gla_tile_common(k_ref[...], flow_ref[...], wfu_ref[...], bf_ref[...])
        q = q_ref[...].astype(F32) * (GLA_HDK ** -0.5)
        v = v_ref[...]
        rows = [slice(g * CHUNK, (g + 1) * CHUNK) for g in range(G)]
        kv = [[_dot(v[r, vs], kt[r, ks], _TN) for ks, vs in _heads()] for r in rows]
        st = [state[vs, :] for _, vs in _heads()]
        for g, r in enumerate(rows):
            outs = []
            for h, (ks, vs) in enumerate(_heads()):
                st[h] = st[h] * ds[g][:, ks] + kv[g][h]
                st_ref[g, vs, :] = st[h]
                outs.append(_dot(q[r, ks], st[h], _NT))
            o_ref[r, :] = jnp.concatenate(outs, axis=-1)
        for h, (_, vs) in enumerate(_heads()):
            state[vs, :] = st[h]

    return _pcall(
        body, name="gla_fwd", grid=(S // T,),
        in_specs=[pl.BlockSpec((T, GLA_DK), lambda c: (c, 0)), pl.BlockSpec((T, GLA_DK), lambda c: (c, 1)),
                  pl.BlockSpec((T, GLA_DV), lambda c: (c, 1)),
                  pl.BlockSpec((T, FLOW_W), lambda c: (c, (INT_W - FLOW_W) // FLOW_W)),
                  pl.BlockSpec(wfu.shape, lambda c: (0, 0)), pl.BlockSpec(bf.shape, lambda c: (0, 0))],
        out_specs=[pl.BlockSpec((T, GLA_DV), lambda c: (c, 0)),
                   pl.BlockSpec((G, GLA_DV, GLA_HDK), lambda c: (c, 0, 0))],
        out_shape=[_sds((S, GLA_DV)), _sds((nc, GLA_DV, GLA_HDK))],
        scratch=[pltpu.VMEM((GLA_DV, GLA_HDK), F32)])(proj, proj, proj, proj, wfu, bf)


def _gla_bwd(proj, wfu, bf, states, do, dmain):
    S = proj.shape[0]
    G = min(GLA_G, S // CHUNK)
    T = G * CHUNK
    nt = S // T

    def body(q_ref, k_ref, v_ref, flow_ref, wfu_ref, bf_ref, st_ref, stp_ref, do_ref, dmain_in,
             dqkv_ref, dflow_ref, dwfu_ref, dbf_ref, dstate):
        dq_ref = dqkv_ref.at[:, 0:GLA_DK]
        dk_ref = dqkv_ref.at[:, GLA_DK:2 * GLA_DK]
        dv_ref = dqkv_ref.at[:, 2 * GLA_DK:]
        step = pl.program_id(0)

        @pl.when(step == 0)
        def _():
            dstate[...] = jnp.zeros_like(dstate)
            dwfu_ref[...] = jnp.zeros_like(dwfu_ref)
            dbf_ref[...] = jnp.zeros_like(dbf_ref)

        flow, wfu_v = flow_ref[...], wfu_ref[...]
        f, w, kt, ds = _gla_tile_common(k_ref[...], flow, wfu_v, bf_ref[...])
        q = q_ref[...].astype(F32) * (GLA_HDK ** -0.5)
        v, dout = v_ref[...], do_ref[...]
        rows = [slice(g * CHUNK, (g + 1) * CHUNK) for g in range(G)]
        dq = [jnp.concatenate([_dot(dout[r, vs], st_ref[g, vs, :]) for _, vs in _heads()], axis=-1)
              for g, r in enumerate(rows)]
        qdo = [[_dot(dout[r, vs], q[r, ks], _TN) for ks, vs in _heads()] for r in rows]
        dq_ref[...] = (jnp.concatenate(dq, axis=0) * (GLA_HDK ** -0.5)).astype(dq_ref.dtype)
        has_prev = (step < nt - 1).astype(F32)
        carry = [dstate[vs, :] for _, vs in _heads()]
        dkt, dv, dd = [None] * G, [None] * G, [None] * G
        for g in reversed(range(G)):
            r = rows[g]
            dkts, dvs, dds = [], [], []
            for h, (ks, vs) in enumerate(_heads()):
                dst = carry[h] + qdo[g][h]
                dkts.append(_dot(v[r, vs], dst))
                dvs.append(_dot(kt[r, ks], dst, _NT))
                st_prev = st_ref[g - 1, vs, :] if g > 0 else stp_ref[vs, :] * has_prev
                dds.append(jnp.sum(dst * st_prev, axis=0, keepdims=True))
                carry[h] = dst * ds[g][:, ks]
            dkt[g], dv[g], dd[g] = (jnp.concatenate(t, axis=-1) for t in (dkts, dvs, dds))
        for h, (_, vs) in enumerate(_heads()):
            dstate[vs, :] = carry[h]
        dkt = jnp.concatenate(dkt, axis=0)
        dv_ref[...] = jnp.concatenate(dv, axis=0).astype(dv_ref.dtype)
        dk_ref[...] = (dkt * w).astype(dk_ref.dtype)
        de = dkt * kt
        tri = _tri_matrix(False)
        dla = []
        for g, r in enumerate(rows):
            db_end = jnp.sum(de[r], axis=0, keepdims=True) + dd[g] * ds[g]
            dla.append(db_end - _tri_dot(tri, de[r]))
        df = jnp.concatenate(dla, axis=0) * (1.0 - _sigmoid(f)) * (1.0 / GATE_TEMP)
        dflow_ref[...] = _dot(df, wfu_v, _NT).astype(dflow_ref.dtype)
        dwfu_ref[...] += _dot(flow, df, _TN)
        dbf_ref[...] += jnp.sum(df, axis=0, keepdims=True)

    rt = lambda s: nt - 1 - s
    return _pcall(
        body, name="gla_bwd", grid=(nt,),
        in_specs=[pl.BlockSpec((T, GLA_DK), lambda s: (rt(s), 0)), pl.BlockSpec((T, GLA_DK), lambda s: (rt(s), 1)),
                  pl.BlockSpec((T, GLA_DV), lambda s: (rt(s), 1)),
                  pl.BlockSpec((T, FLOW_W), lambda s: (rt(s), (INT_W - FLOW_W) // FLOW_W)),
                  pl.BlockSpec(wfu.shape, lambda s: (0, 0)), pl.BlockSpec(bf.shape, lambda s: (0, 0)),
                  pl.BlockSpec((G, GLA_DV, GLA_HDK), lambda s: (rt(s), 0, 0)),
                  pl.BlockSpec((None, GLA_DV, GLA_HDK), lambda s: (jnp.maximum(rt(s) * G - 1, 0), 0, 0)),
                  pl.BlockSpec((T, GLA_DV), lambda s: (rt(s), 0)), pl.BlockSpec(memory_space=pl.ANY)],
        out_specs=[pl.BlockSpec((T, 2 * GLA_DK + GLA_DV), lambda s: (rt(s), 0)),
                   pl.BlockSpec((T, FLOW_W), lambda s: (rt(s), 0)),
                   pl.BlockSpec(wfu.shape, lambda s: (0, 0)), pl.BlockSpec(bf.shape, lambda s: (0, 0))],
        out_shape=[_sds(dmain.shape, dmain.dtype), _sds((S, FLOW_W), BF16), _sds(wfu.shape), _sds(bf.shape)],
        scratch=[pltpu.VMEM((GLA_DV, GLA_HDK), F32)], aliases={9: 0},
    )(proj, proj, proj, proj, wfu, bf, states, states, do, dmain)


def _pool_counts(tm, i):
    t = (lax.broadcasted_iota(jnp.int32, (tm, POOL_GD), 0) + i * tm + 1).astype(F32)
    return [jnp.minimum(t, float(w)) for w in POOL_WINDOWS]


def _pool_fwd(proj, w_pool, pool_scale, tm=512):
    S = proj.shape[0]
    tm = min(tm, S // 2)
    col = _POOL_COL
    hb = tm // POOL_HALO

    def body(p_ref, halo_ref, wp_ref, sc_ref, mixed_ref, out_ref):
        i = pl.program_id(0)
        p = p_ref[...].astype(F32)
        halo = halo_ref[...].astype(F32) * (i > 0).astype(F32)
        ext = jnp.concatenate([halo, p], axis=0)
        n = tm + POOL_HALO
        sums, acc, k = {}, ext, 1
        while k < POOL_WINDOWS[-1]:
            acc = acc + pltpu.roll(acc, k, axis=0)
            k *= 2
            sums[k] = acc
        cnts = _pool_counts(tm, i)
        mixed, lin = [], []
        for g, w in enumerate(POOL_WINDOWS):
            ls = slice(g * POOL_GD, (g + 1) * POOL_GD)
            m = sums[w][POOL_HALO:n, ls] / cnts[g] - p[:, ls]
            mixed.append(m)
            lin.append(_dot(m, wp_ref[g]))
        mixed_ref[...] = jnp.concatenate(mixed, axis=-1)
        out_ref[...] = (jnp.concatenate(lin, axis=-1) * sc_ref[...]).astype(out_ref.dtype)

    return _pcall(
        body, name="pool_fwd", grid=(S // tm,),
        in_specs=[pl.BlockSpec((tm, POOL_W), lambda i: (i, col)),
                  pl.BlockSpec((POOL_HALO, POOL_W), lambda i: (jnp.maximum(i * hb - 1, 0), col)),
                  pl.BlockSpec(w_pool.shape, lambda i: (0, 0, 0)), pl.BlockSpec(pool_scale.shape, lambda i: (0, 0))],
        out_specs=[pl.BlockSpec((tm, POOL_W), lambda i: (i, 0)), pl.BlockSpec((tm, POOL_W), lambda i: (i, 0))],
        out_shape=[_sds((S, POOL_W)), _sds((S, POOL_W), BF16)])(proj, proj, w_pool, pool_scale)


def _pool_lin_bwd(dout, mixed, w_pool, pool_scale):
    S = dout.shape[0]

    def fn(dout, mixed, wp, sc):
        dlin = dout * sc
        dm, dwp, lin = [], [], []
        for g in range(len(POOL_WINDOWS)):
            ls = slice(g * POOL_GD, (g + 1) * POOL_GD)
            lin.append(_dot(mixed[:, ls], wp[g]))
            dm.append(_dot(dlin[:, ls], wp[g], _NT))
            dwp.append(_dot(mixed[:, ls], dlin[:, ls], _TN))
        dsc = jnp.sum(dout * jnp.concatenate(lin, axis=-1), axis=0, keepdims=True)
        return jnp.concatenate(dm, axis=-1), jnp.concatenate(dwp, axis=0), dsc

    return _rowwise("pool_lin_bwd", fn, [dout, mixed], [w_pool, pool_scale], [_sds((S, POOL_W))],
                    [_sds((len(POOL_WINDOWS) * POOL_GD, POOL_GD)), _sds((1, POOL_W))])


def _pool_win_bwd(dmixed, dmain, tm=512):
    S = dmixed.shape[0]
    tm = min(tm, S // 2)
    nt = S // tm
    hb = tm // POOL_HALO

    def body(dm_ref, halo_ref, dmain_in, dp_ref):
        i = pl.program_id(0)
        dm = dm_ref[...]
        halo = halo_ref[...] * (i < nt - 1).astype(F32)
        cnts = _pool_counts(tm, i)
        cnts_h = [c[:POOL_HALO] for c in _pool_counts(tm, i + 1)]
        r = jnp.concatenate([jnp.concatenate([dm[:, g * POOL_GD:(g + 1) * POOL_GD] / cnts[g] for g in range(4)], axis=-1),
                             jnp.concatenate([halo[:, g * POOL_GD:(g + 1) * POOL_GD] / cnts_h[g] for g in range(4)], axis=-1)],
                            axis=0)
        n = tm + POOL_HALO
        sums, acc, k = {}, r, 1
        while k < POOL_WINDOWS[-1]:
            acc = acc + pltpu.roll(acc, n - k, axis=0)
            k *= 2
            sums[k] = acc
        dp = [sums[w][:tm, g * POOL_GD:(g + 1) * POOL_GD] for g, w in enumerate(POOL_WINDOWS)]
        dp_ref[...] = (jnp.concatenate(dp, axis=-1) - dm).astype(dp_ref.dtype)

    return _pcall(
        body, name="pool_win_bwd", grid=(nt,),
        in_specs=[pl.BlockSpec((tm, POOL_W), lambda i: (i, 0)),
                  pl.BlockSpec((POOL_HALO, POOL_W), lambda i: (jnp.minimum((i + 1) * hb, S // POOL_HALO - 1), 0)),
                  pl.BlockSpec(memory_space=pl.ANY)],
        out_specs=pl.BlockSpec((tm, POOL_W), lambda i: (i, _POOL_COL)),
        out_shape=_sds(dmain.shape, dmain.dtype), aliases={2: 0})(dmixed, dmixed, dmain)


def _xattn_probs(q, kv, h):
    hs = slice(h * XA_HD, (h + 1) * XA_HD)
    s = _dot(q[:, hs], kv[:, hs], _NT) * (XA_HD ** -0.5)
    s = s - jnp.max(s, axis=-1, keepdims=True)
    e = jnp.exp(s)
    return e / jnp.sum(e, axis=-1, keepdims=True)


def _xattn_fwd(proj, kv):
    S = proj.shape[0]

    def fn(q, kv):
        outs = []
        for h in range(XA_HEADS):
            p = _xattn_probs(q, kv, h)
            outs.append(_dot(p, kv[:, XA_W + h * XA_HD:XA_W + (h + 1) * XA_HD]))
        return jnp.concatenate(outs, axis=-1)

    return _rowwise("xattn_fwd", fn, [_Win(proj, XA_W, _XQ_COL)], [kv], [_sds((S, XA_W), BF16)])[0]


def _xattn_bwd(proj, kv, dxa, dmain):
    def fn(q, dxa, kv):
        dqs, dks, dvs = [], [], []
        for h in range(XA_HEADS):
            hs = slice(h * XA_HD, (h + 1) * XA_HD)
            vh = kv[:, XA_W + h * XA_HD:XA_W + (h + 1) * XA_HD]
            p = _xattn_probs(q, kv, h)
            dp = _dot(dxa[:, hs], vh, _NT)
            ds = p * (dp - jnp.sum(p * dp, axis=-1, keepdims=True)) * (XA_HD ** -0.5)
            dqs.append(_dot(ds, kv[:, hs]))
            dks.append(_dot(ds, q[:, hs], _TN))
            dvs.append(_dot(p, dxa[:, hs], _TN))
        return jnp.concatenate(dqs, axis=-1), jnp.concatenate(dks + dvs, axis=-1)

    return _rowwise("xattn_bwd", fn, [_Win(proj, XA_W, _XQ_COL), dxa], [kv], [_Win(dmain, XA_W, _XQ_COL)],
                    [_sds(kv.shape)])


def _mix_fwd(x1, h, mem, W, g_next):
    S = x1.shape[0]
    M = mem.shape[0]
    tm = min(512, S)
    def proj_body(h_ref, w_ref, o_ref):
        hh = h_ref[...]
        for j in range(INT_NBLK):
            o_ref[:, INT_BLK * j:INT_BLK * (j + 1)] = _dot(hh, w_ref[INT_BLK * j:INT_BLK * (j + 1), :], _NT).astype(BF16)

    proj = _pcall(
        proj_body, name="mix_proj", grid=(S // tm,),
        in_specs=[pl.BlockSpec((tm, D_MODEL), lambda i: (i, 0)),
                  pl.BlockSpec((INT_W, D_MODEL), lambda i: (0, 0), pipeline_mode=pl.Buffered(1))],
        out_specs=pl.BlockSpec((tm, INT_W), lambda i: (i, 0)), out_shape=_sds((S, INT_W), BF16))(h, W["w_int_t"])
    o_raw, states = _gla_fwd(proj, W["w_fu_pad"], W["b_f"])
    ya_in = _rowwise("gla_out", _head_rms_gate, [o_raw, _Win(proj, GLA_DV, 2)], [W["gla_norm_g"]],
                     [_sds((S, GLA_DV), BF16)])[0]
    mixed, pool_out = _pool_fwd(proj, W["w_pool"], W["pool_scale"])
    mem_n = _rowwise("mem_norm", lambda m, g: _rms(m, g), [mem], [W["mem_norm_g"]], [_sds((M, D_MODEL), BF16)])[0]
    kv = _mm_nn("mem_kv", mem_n, W["w_mem_kv"])
    xa = _xattn_fwd(proj, kv)
    def out_fn(gates, ya_in, pool_out, xa, x1, wa, wb, wc, wo, g, gn):
        ya, yb, yc = (_dot(t, w).astype(BF16) for t, w in ((ya_in, wa), (pool_out, wb), (xa, wc)))
        merged = _merge(gates, ya, yb, yc).astype(BF16)
        y = _dot(merged, wo)
        x2 = x1 + _rms(y, g)
        return ya, yb, yc, merged, y, x2, _rms(x2, gn)

    ya, yb, yc, merged, y, x2, h_next = _rowwise(
        "mix_out", out_fn, [_Win(proj, 3 * D_MODEL, 1), ya_in, pool_out, xa, x1],
        [W["w_up_gla"], W["w_up_pool"], W["w_up_xattn"], W["w_o"], W["mix_post_g"], g_next],
        [_sds((S, D_MODEL), BF16)] * 4 + [_sds((S, D_MODEL)), _sds((S, D_MODEL)), _sds((S, D_MODEL), BF16)], tm=256)
    return x2, h_next, (h, proj, o_raw, states, ya_in, mixed, pool_out, mem_n, kv, xa, ya, yb, yc, merged, y)


def _mix_bwd(dx2, x1, mem, W, saved, on_grads=None):
    h, proj, o_raw, states, ya_in, mixed, pool_out, mem_n, kv, xa, ya, yb, yc, merged, y = saved
    S = x1.shape[0]

    def out_bwd(y, dx2, gates, ya, yb, yc, g, wo, wa, wb, wc):
        dy, dg = _rms_bwd(y, g, dx2)
        dy = dy.astype(BF16)
        _, vjp = jax.vjp(_merge, gates, ya, yb, yc)
        dgates, dya, dyb, dyc = vjp(_dot(dy, wo, _NT))
        return (dy, dgates, dya, dyb, dyc, _dot(dya, wa, _NT), _dot(dyb, wb, _NT), _dot(dyc, wc, _NT), dg)

    dmain = lax.empty((S, INT_MAIN), BF16)
    dy, dmain, dya, dyb, dyc, d_ya_in, d_pool_out, d_xa, d_mix_post_g = _rowwise(
        "mix_out_bwd", out_bwd, [y, dx2, _Win(proj, 3 * D_MODEL, 1), ya, yb, yc],
        [W["mix_post_g"], W["w_o"], W["w_up_gla"], W["w_up_pool"], W["w_up_xattn"]],
        [_sds((S, D_MODEL), BF16), _Win(dmain, 3 * D_MODEL, 1)] + [_sds((S, D_MODEL), BF16)] * 4
        + [_sds((S, POOL_W), BF16), _sds((S, XA_W), BF16)], [_sds((1, D_MODEL))], tm=256)
    d_w_o = _mm_tn("d_w_o", merged, dy)
    d_w_up_gla = _mm_tn("d_w_up_gla", ya_in, dya)
    d_w_up_pool = _mm_tn("d_w_up_pool", pool_out, dyb)
    d_w_up_xattn = _mm_tn("d_w_up_xattn", xa, dyc)

    def gla_out_bwd(o, g_out, d, gn):
        _, vjp = jax.vjp(_head_rms_gate, o, g_out, gn)
        return vjp(d.astype(F32))

    do_raw, dmain, d_gla_norm_g = _rowwise("gla_out_bwd", gla_out_bwd, [o_raw, _Win(proj, GLA_DV, 2), d_ya_in],
                                           [W["gla_norm_g"]], [_sds((S, GLA_DV)), _Win(dmain, GLA_DV, 2)],
                                           [_sds((1, GLA_DV))])
    dmain, dflow, d_wfu_pad, d_b_f = _gla_bwd(proj, W["w_fu_pad"], W["b_f"], states, do_raw, dmain)
    dmixed, d_w_pool, d_pool_scale = _pool_lin_bwd(d_pool_out, mixed, W["w_pool"], W["pool_scale"])
    dmain = _pool_win_bwd(dmixed, dmain)
    dmain, dkv = _xattn_bwd(proj, kv, d_xa, dmain)
    d_w_mem_kv = _mm_tn("d_w_mem_kv", mem_n, dkv)
    dmem_n = _mm_nt("d_mem_n", dkv, W["w_mem_kv"])
    d_mem_norm_g = _rowwise("mem_norm_bwd", lambda m, d, g: _rms_bwd(m, g, d)[1], [mem, dmem_n], [W["mem_norm_g"]], [],
                            [_sds((1, D_MODEL))])[0]
    d_w_int_t = (_mm_tn("d_w_int", dmain, h, ts=1024, tn_a=INT_MAIN // 4), _mm_tn("d_w_flow", dflow, h, ts=2048))
    grads = dict(
        w_int_t=d_w_int_t, w_fu_pad=d_wfu_pad, b_f=d_b_f, gla_norm_g=d_gla_norm_g, w_pool=d_w_pool,
        pool_scale=d_pool_scale, mem_norm_g=d_mem_norm_g, w_mem_kv=d_w_mem_kv, w_up_gla=d_w_up_gla,
        w_up_pool=d_w_up_pool, w_up_xattn=d_w_up_xattn, w_o=d_w_o, mix_post_g=d_mix_post_g)
    deps = [on_grads(grads)] if on_grads is not None else []
    dx1, grads["mix_pre_g"] = _dh_pre_norm_bwd("d_mix_h", [dmain, dflow], W["w_int_t"], x1, dx2, W["mix_pre_g"], deps)
    return dx1, grads


def _mesh_pos():
    x, y, c = lax.axis_index("x"), lax.axis_index("y"), lax.axis_index("c")
    return x, y, c, 4 * x + 2 * y + c


def _peer(x, y, c, r):
    px = 1 - x if r & 4 else x
    py = 1 - y if r & 2 else y
    pc = 1 - c if r & 1 else c
    return (px, py, pc), 4 * px + 2 * py + pc


_ALL_PEERS = tuple(range(1, N_DEV))
_SIBLING = 1
_SAME_CORE = (2, 4, 6)


def _dev_slot(ref, dev):
    return ref.at[dev]


class _Plan:
    def __init__(self, scatter, slots=None, shapes=None):
        self.scatter, self.slots, self.shapes = scatter, slots or {}, shapes or {}

    def src(self, srcs, a, dev):
        return self.slots.get(a, _dev_slot)(srcs[a], dev) if self.scatter else srcs[a]

    def dst(self, lands, a, dev):
        return lands[a].at[dev] if self.scatter else self.slots.get(a, _dev_slot)(lands[a], dev)

    def landing_zones(self, arrays):
        lands = []
        for a, arr in enumerate(arrays):
            if self.scatter:
                lands.append(lax.empty((N_DEV,) + tuple(self.shapes.get(a, arr.shape[1:])), arr.dtype))
            elif a in self.shapes:
                lands.append(jnp.zeros(self.shapes[a], arr.dtype))
            else:
                lands.append(lax.empty((N_DEV,) + arr.shape, arr.dtype))
        return lands


_GATHER, _SCATTER = _Plan(False), _Plan(True)


def _peer_copies(srcs, lands, send_sems, recv_sems, plan, peers=_ALL_PEERS):
    x, y, c, me = _mesh_pos()
    cps = []
    for r in peers:
        pos, peer = _peer(x, y, c, r)
        for a in range(len(srcs)):
            k = a * (N_DEV - 1) + r - 1
            cps.append(pltpu.make_async_remote_copy(
                src_ref=plan.src(srcs, a, peer), dst_ref=plan.dst(lands, a, me),
                send_sem=send_sems.at[k], recv_sem=recv_sems.at[k], device_id=pos,
                device_id_type=pl.DeviceIdType.MESH))
    return cps


_HBM = pl.BlockSpec(memory_space=pltpu.HBM)
_SEM = pl.BlockSpec(memory_space=pltpu.SEMAPHORE)
_EFFECT = pltpu.SideEffectType.DATAFLOW_SIDE_EFFECTING


def _own_copies(srcs, lands, own_sems, plan):
    me = _mesh_pos()[3]
    return [pltpu.make_async_copy(plan.src(srcs, a, me), plan.dst(lands, a, me), own_sems.at[a])
            for a in range(len(srcs))]


def _exchange_start(name, arrays, plan):
    n = len(arrays)
    lands = plan.landing_zones(arrays)
    n_sem = n * (N_DEV - 1)

    def body(*refs):
        srcs, lands_ = refs[:n], refs[n:2 * n]
        send_sems, recv_sems, own_sems = refs[2 * n:2 * n + 3]
        token = refs[-1]
        for cp in _peer_copies(srcs, lands_, send_sems, recv_sems, plan) + _own_copies(srcs, lands_, own_sems, plan):
            cp.start()
        token[...] = jnp.zeros_like(token)

    hbm = lambda a: pltpu.HBM(a.shape, a.dtype)
    res = pl.pallas_call(
        body, name=name,
        out_shape=(pltpu.SemaphoreType.DMA((n_sem,)), pltpu.SemaphoreType.DMA((n_sem,)), pltpu.SemaphoreType.DMA((n,)),
                   *[hbm(a) for a in arrays], *[hbm(a) for a in lands], _sds((8, 128))),
        in_specs=[_HBM] * (2 * n),
        out_specs=(_SEM, _SEM, _SEM, *[_HBM] * (2 * n), pl.BlockSpec(memory_space=pltpu.VMEM)),
        input_output_aliases={i: 3 + i for i in range(2 * n)},
        compiler_params=pltpu.CompilerParams(has_side_effects=_EFFECT),
    )(*[pltpu.with_memory_space_constraint(a, pltpu.HBM) for a in list(arrays) + lands])
    return (res[:3], res[3:3 + n], res[3 + n:3 + 2 * n], plan), res[-1]


def _exchange_wait(name, state, after):
    sems, srcs, lands, plan = state
    n = len(srcs)

    def body(*refs):
        srcs_, lands_ = refs[:n], refs[n:2 * n]
        send_sems, recv_sems, own_sems = refs[2 * n:2 * n + 3]
        for cp in _peer_copies(srcs_, lands_, send_sems, recv_sems, plan):
            cp.wait_send()
            cp.wait_recv()
        for cp in _own_copies(srcs_, lands_, own_sems, plan):
            cp.wait()

    hbm = lambda a: pltpu.HBM(a.shape, a.dtype)
    res = pl.pallas_call(
        body, name=name, out_shape=tuple(hbm(a) for a in list(srcs) + list(lands)),
        in_specs=[_HBM] * (2 * n) + [_SEM] * 3 + [pl.BlockSpec(memory_space=pl.ANY)] * len(after),
        out_specs=tuple([_HBM] * (2 * n)), input_output_aliases={i: i for i in range(2 * n)},
        compiler_params=pltpu.CompilerParams(has_side_effects=_EFFECT),
    )(*srcs, *lands, *sems, *after)
    return res[n:]


def _gather_start(name, arrays, plan, after):
    n = len(arrays)
    lands = plan.landing_zones(arrays)
    n_sem = n * (N_DEV - 1)

    def body(*refs):
        srcs, lands_ = refs[:n], refs[n:2 * n]
        send_sems, recv_sems, own_sems = refs[2 * n + len(after):2 * n + len(after) + 3]
        token = refs[-1]
        for cp in (_peer_copies(srcs, lands_, send_sems, recv_sems, plan, (_SIBLING,) + _SAME_CORE)
                   + _own_copies(srcs, lands_, own_sems, plan)):
            cp.start()
        token[...] = jnp.zeros_like(token)

    hbm = lambda a: pltpu.HBM(a.shape, a.dtype)
    res = pl.pallas_call(
        body, name=name,
        out_shape=(pltpu.SemaphoreType.DMA((n_sem,)), pltpu.SemaphoreType.DMA((n_sem,)), pltpu.SemaphoreType.DMA((n,)),
                   *[hbm(a) for a in arrays], *[hbm(a) for a in lands], _sds((8, 128))),
        in_specs=[_HBM] * (2 * n) + [pl.BlockSpec(memory_space=pl.ANY)] * len(after),
        out_specs=(_SEM, _SEM, _SEM, *[_HBM] * (2 * n), pl.BlockSpec(memory_space=pltpu.VMEM)),
        input_output_aliases={i: 3 + i for i in range(2 * n)},
        compiler_params=pltpu.CompilerParams(has_side_effects=_EFFECT),
    )(*[pltpu.with_memory_space_constraint(a, pltpu.HBM) for a in list(arrays) + lands], *after)
    return (res[:3], res[3:3 + n], res[3 + n:3 + 2 * n], plan), res[-1]


def _pass_on_copies(lands, send_sems, recv_sems, plan):
    x, y, c, _ = _mesh_pos()
    sibling = _peer(x, y, c, _SIBLING)[0]
    cps = []
    for i, r in enumerate(_SAME_CORE):
        owner = _peer(x, y, c, r)[1]
        for a in range(len(lands)):
            k = a * len(_SAME_CORE) + i
            cps.append(pltpu.make_async_remote_copy(
                src_ref=plan.dst(lands, a, owner), dst_ref=plan.dst(lands, a, owner), send_sem=send_sems.at[k],
                recv_sem=recv_sems.at[k], device_id=sibling, device_id_type=pl.DeviceIdType.MESH))
    return cps


def _gather_pass_on(name, state, after):
    sems, srcs, lands, plan = state
    n = len(srcs)
    n_sem = n * len(_SAME_CORE)

    def body(*refs):
        srcs_, lands_ = refs[:n], refs[n:2 * n]
        send_sems, recv_sems = refs[2 * n], refs[2 * n + 1]
        on_send, on_recv = refs[2 * n + 3 + len(after)], refs[2 * n + 4 + len(after)]
        arrivals = _peer_copies(srcs_, lands_, send_sems, recv_sems, plan, _SAME_CORE)
        for arrived, on in zip(arrivals, _pass_on_copies(lands_, on_send, on_recv, plan)):
            arrived.wait_recv()
            on.start()

    hbm = lambda a: pltpu.HBM(a.shape, a.dtype)
    res = pl.pallas_call(
        body, name=name,
        out_shape=(pltpu.SemaphoreType.DMA((n_sem,)), pltpu.SemaphoreType.DMA((n_sem,)),
                   *[hbm(a) for a in list(srcs) + list(lands)]),
        in_specs=[_HBM] * (2 * n) + [_SEM] * 3 + [pl.BlockSpec(memory_space=pl.ANY)] * len(after),
        out_specs=(_SEM, _SEM, *[_HBM] * (2 * n)), input_output_aliases={i: 2 + i for i in range(2 * n)},
        compiler_params=pltpu.CompilerParams(has_side_effects=_EFFECT),
    )(*srcs, *lands, *sems, *after)
    return sems, res[:2], res[2:2 + n], res[2 + n:], plan


def _gather_wait(name, state, after):
    sems, on_sems, srcs, lands, plan = state
    n = len(srcs)

    def body(*refs):
        srcs_, lands_ = refs[:n], refs[n:2 * n]
        send_sems, recv_sems, own_sems, on_send, on_recv = refs[2 * n:2 * n + 5]
        for cp in _peer_copies(srcs_, lands_, send_sems, recv_sems, plan, (_SIBLING,)):
            cp.wait_recv()
        for cp in _peer_copies(srcs_, lands_, send_sems, recv_sems, plan, (_SIBLING,) + _SAME_CORE):
            cp.wait_send()
        for cp in _own_copies(srcs_, lands_, own_sems, plan):
            cp.wait()
        for cp in _pass_on_copies(lands_, on_send, on_recv, plan):
            cp.wait_send()
            cp.wait_recv()

    hbm = lambda a: pltpu.HBM(a.shape, a.dtype)
    res = pl.pallas_call(
        body, name=name, out_shape=tuple(hbm(a) for a in list(srcs) + list(lands)),
        in_specs=[_HBM] * (2 * n) + [_SEM] * 5 + [pl.BlockSpec(memory_space=pl.ANY)] * len(after),
        out_specs=tuple([_HBM] * (2 * n)), input_output_aliases={i: i for i in range(2 * n)},
        compiler_params=pltpu.CompilerParams(has_side_effects=_EFFECT),
    )(*srcs, *lands, *sems, *on_sems, *after)
    return res[n:]


def _adam(name, recv, w, m, v):
    shape = w.shape
    R, C = shape
    tr = R
    while N_DEV * tr * C * 4 > 6 * 1024 * 1024 and tr % 32 == 0:
        tr //= 2

    def body(recv_ref, w_ref, m_ref, v_ref, g_ref, d_ref, m2_ref, v2_ref):
        g = recv_ref[0].astype(F32)
        for j in range(1, N_DEV):
            g = g + recv_ref[j].astype(F32)
        w_, m_, v_ = w_ref[...], m_ref[...], v_ref[...]
        m2 = ADAM_B1 * m_ + (1.0 - ADAM_B1) * g
        v2 = ADAM_B2 * v_ + (1.0 - ADAM_B2) * (g * g)
        m_hat = m2 / (1.0 - ADAM_B1 ** ADAM_STEP)
        v_hat = v2 / (1.0 - ADAM_B2 ** ADAM_STEP)
        g_ref[...] = g
        d_ref[...] = -ADAM_LR * (m_hat / (jnp.sqrt(v_hat) + ADAM_EPS) + ADAM_WD * w_)
        m2_ref[...] = m2
        v2_ref[...] = v2

    blk = pl.BlockSpec((tr, C), lambda i: (i, 0))
    return _pcall(body, name=name, grid=(R // tr,),
                  in_specs=[pl.BlockSpec((N_DEV, tr, C), lambda i: (0, i, 0)), blk, blk, blk],
                  out_specs=[blk] * 4, out_shape=[_sds(shape)] * 4)(recv, w, m, v)


_NAMES = ['ffn1_pre_g', 'ffn1_w_in', 'ffn1_w_out', 'ffn1_post_g', 'mix_pre_g', 'w_in', 'w_fu', 'b_f', 'gla_norm_g',
          'w_pool', 'pool_scale', 'mem_norm_g', 'w_mem_kv', 'w_up_gla', 'w_up_pool', 'w_up_xattn', 'w_o', 'mix_post_g',
          'ffn2_pre_g', 'ffn2_w_in', 'ffn2_w_out', 'ffn2_post_g', 'final_g']
_SHARDED = ['ffn1_w_in', 'ffn1_w_out', 'w_in', 'w_fu', 'w_mem_kv', 'w_up_gla', 'w_up_pool', 'w_up_xattn', 'w_o',
            'ffn2_w_in', 'ffn2_w_out']
_COL_SHARDED = ['w_up_pool', 'w_up_xattn']


def _cols_to_full(g):
    return jnp.transpose(g, (1, 0, 2)).reshape(g.shape[1], N_DEV * g.shape[2])


def _full_to_cols(f):
    R, C = f.shape
    return jnp.transpose(f.reshape(R, N_DEV, C // N_DEV), (1, 0, 2))


def _to_internal(w_in_t):
    o = 0
    parts = []
    for s in IN_SPLITS:
        parts.append(w_in_t[o:o + s])
        o += s
    q, k, v, g_out, f_low, p_in, xq, gates = parts
    f_low = jnp.pad(f_low, ((0, FLOW_W - GATE_RANK), (0, 0)))
    return jnp.concatenate([q, k, v, g_out, gates, p_in, xq, f_low], axis=0)


def _from_internal(d, d_flow):
    q, k, v, g_out = d[0:512], d[512:1024], d[1024:2048], d[2048:3072]
    gates, p_in, xq = d[3072:6144], d[6144:6656], d[6656:7168]
    return jnp.concatenate([q, k, v, g_out, d_flow[:GATE_RANK], p_in, xq, gates], axis=0)


_FFN_IN = ('ffn1_w_in', 'ffn2_w_in')
_FFN_OUT = ('ffn1_w_out', 'ffn2_w_out')
_GATHERS = {"ffn1_in": ['ffn1_w_in'], "ffn1_out": ['ffn1_w_out'],
            "mix": ['w_in', 'w_fu', 'w_mem_kv', 'w_up_gla', 'w_up_pool', 'w_up_xattn', 'w_o'],
            "ffn2": ['ffn2_w_in', 'ffn2_w_out']}


def _ffn_in_slot(ref, d):
    return ref.at[2 * (d % N_FF_BLK) + d // N_FF_BLK, pl.ds(0, FF_BLK)]


def _ffn_out_slot(ref, d):
    rows = FF_BLK // 2
    return ref.at[d // 2, pl.ds(pl.multiple_of((d % 2) * rows, rows), rows)]


def _ffn_plan(names, scatter):
    slots, shapes = {}, {}
    for a, n in enumerate(names):
        if n in _FFN_IN:
            slots[a] = _ffn_in_slot
            shapes[a] = (FF_BLK, D_MODEL) if scatter else (N_DEV, FF_PAD, D_MODEL)
        elif n in _FFN_OUT:
            slots[a] = _ffn_out_slot
            shapes[a] = (FF_BLK // 2, D_MODEL) if scatter else (N_FF_BLK, FF_PAD, D_MODEL)
    return _Plan(scatter, slots, shapes)


def _rows_to_full(g):
    return g.reshape(N_DEV * g.shape[1], g.shape[2])


def _mix_weights(gathered):
    W = {n: _rows_to_full(gathered[n]) for n in ('w_mem_kv', 'w_up_gla', 'w_o')}
    for n in _COL_SHARDED:
        W[n] = _cols_to_full(gathered[n])
    W["w_int_t"] = _to_internal(gathered["w_in"].reshape(IN_WIDTH, D_MODEL))
    W["w_fu_pad"] = jnp.pad(_cols_to_full(gathered["w_fu"]), ((0, FLOW_W - GATE_RANK), (0, 0)))
    return W


def _mix_chunks(G, n):
    if n == "w_in":
        return _from_internal(*G["w_int_t"]).reshape(N_DEV, IN_SHARD, D_MODEL)
    if n == "w_fu":
        return _full_to_cols(G["w_fu_pad"][:GATE_RANK].astype(BF16))
    if n in _COL_SHARDED:
        return _full_to_cols(G[n])
    return G[n].reshape(N_DEV, G[n].shape[0] // N_DEV, G[n].shape[1])


def _step(x, mem, tgt, P, Mo, Vo):
    def native(n, a):
        return jnp.swapaxes(a, 0, 1) if n in _FFN_IN + ("w_in",) else a

    P, Mo, Vo = ({n: native(n, a) for n, a in d.items()} for d in (P, Mo, Vo))
    small = {n: P[n] for n in _NAMES if n not in _SHARDED}

    gather, token = {}, []
    for grp, names in _GATHERS.items():
        gather[grp], tok = _gather_start("gather_" + grp, [P[n].astype(BF16) for n in names], _ffn_plan(names, False),
                                         token)
        token = [tok]

    def gathered(grp, after):
        return _gather_wait("gather_" + grp + "_wait", _gather_pass_on("gather_" + grp + "_on", gather[grp], after), [])

    def ffn_wo(lands):
        return lambda act: lands[0].reshape(N_FF_BLK * FF_PAD, D_MODEL)

    h1 = _pre_norm("ffn1_pre", x, small["ffn1_pre_g"])
    w1t = gathered("ffn1_in", token + [h1])[0].reshape(N_DEV * FF_PAD, D_MODEL)
    (x1, hm), w1o, sv1 = _ffn_fwd("ffn1", x, h1, w1t, small["ffn1_post_g"],
                                  lambda act: ffn_wo(gathered("ffn1_out", [act]))(act), small["mix_pre_g"])

    gm = dict(zip(_GATHERS["mix"], gathered("mix", [x1])))
    Wm = {**small, **_mix_weights(gm)}
    x2, h2, svm = _mix_fwd(x1, hm, mem, Wm, small["ffn2_pre_g"])

    w2t, w2o = gathered("ffn2", [x2])
    w2t = w2t.reshape(N_DEV * FF_PAD, D_MODEL)
    (loss, dx3, d_final_g), w2o, sv2 = _ffn_fwd("ffn2", x2, h2, w2t, small["ffn2_post_g"], ffn_wo([w2o]),
                                                small["final_g"], tgt)

    G = dict(final_g=d_final_g)
    scat = {}

    def start(grp, names, arrays):
        scat[grp] = names, _exchange_start("scatter_" + grp, arrays, _ffn_plan(names, True))
        return scat[grp][1][1]

    def ffn_starts(tag):
        return (lambda dwo: start(tag + "_out", [tag + "_w_out"], [dwo.reshape(N_FF_BLK, FF_PAD, D_MODEL)]),
                lambda dwt: start(tag + "_in", [tag + "_w_in"], [dwt.reshape(N_DEV, FF_PAD, D_MODEL)]))

    dx2, G["ffn2_pre_g"], G["ffn2_post_g"] = _ffn_bwd(
        "ffn2", dx3, x2, small["ffn2_pre_g"], w2t, w2o, small["ffn2_post_g"], sv2, *ffn_starts("ffn2"))
    dx1, Gm = _mix_bwd(dx2, x1, mem, Wm, svm,
                       lambda Gm: start("mix", _GATHERS["mix"], [_mix_chunks(Gm, n) for n in _GATHERS["mix"]]))
    G.update(Gm)
    dx, G["ffn1_pre_g"], G["ffn1_post_g"] = _ffn_bwd(
        "ffn1", dx1, x, small["ffn1_pre_g"], w1t, w1o, small["ffn1_post_g"], sv1, *ffn_starts("ffn1"))

    scat["small"] = list(small), _exchange_start("gather_small_grads", [G[n] for n in small], _GATHER)
    recv, outs = {}, {}
    done = [dx]
    for grp in ["ffn2_out", "ffn2_in", "mix", "ffn1_out", "ffn1_in", "small"]:
        names, (state, _) = scat[grp]
        recv.update(zip(names, _exchange_wait(("gather_" if grp == "small" else "scatter_") + grp + "_wait", state,
                                              done[-1:])))
        for n in names:
            shp = P[n].shape
            shp2 = shp if len(shp) == 2 else (shp[0] * shp[1], shp[2])
            res = _adam("adam_" + n, recv[n].reshape((N_DEV,) + shp2), P[n].reshape(shp2), Mo[n].reshape(shp2),
                        Vo[n].reshape(shp2))
            outs[n] = [native(n, t.reshape(shp)) for t in res]
            done.append(res[-1])
    return loss, dx, outs


def kernel(x, mem, ffn1_pre_g, ffn1_w_in, ffn1_w_out, ffn1_post_g, mix_pre_g, w_in, w_fu, b_f, gla_norm_g, w_pool, pool_scale, mem_norm_g, w_mem_kv, w_up_gla, w_up_pool, w_up_xattn, w_o, mix_post_g, ffn2_pre_g, ffn2_w_in, ffn2_w_out, ffn2_post_g, final_g, loss_target, m_ffn1_pre_g, m_ffn1_w_in, m_ffn1_w_out, m_ffn1_post_g, m_mix_pre_g, m_w_in, m_w_fu, m_b_f, m_gla_norm_g, m_w_pool, m_pool_scale, m_mem_norm_g, m_w_mem_kv, m_w_up_gla, m_w_up_pool, m_w_up_xattn, m_w_o, m_mix_post_g, m_ffn2_pre_g, m_ffn2_w_in, m_ffn2_w_out, m_ffn2_post_g, m_final_g, v_ffn1_pre_g, v_ffn1_w_in, v_ffn1_w_out, v_ffn1_post_g, v_mix_pre_g, v_w_in, v_w_fu, v_b_f, v_gla_norm_g, v_w_pool, v_pool_scale, v_mem_norm_g, v_w_mem_kv, v_w_up_gla, v_w_up_pool, v_w_up_xattn, v_w_o, v_mix_post_g, v_ffn2_pre_g, v_ffn2_w_in, v_ffn2_w_out, v_ffn2_post_g, v_final_g):
    params = [ffn1_pre_g, ffn1_w_in, ffn1_w_out, ffn1_post_g, mix_pre_g, w_in, w_fu, b_f, gla_norm_g, w_pool, pool_scale, mem_norm_g, w_mem_kv, w_up_gla, w_up_pool, w_up_xattn, w_o, mix_post_g, ffn2_pre_g, ffn2_w_in, ffn2_w_out, ffn2_post_g, final_g]
    moms = [m_ffn1_pre_g, m_ffn1_w_in, m_ffn1_w_out, m_ffn1_post_g, m_mix_pre_g, m_w_in, m_w_fu, m_b_f, m_gla_norm_g, m_w_pool, m_pool_scale, m_mem_norm_g, m_w_mem_kv, m_w_up_gla, m_w_up_pool, m_w_up_xattn, m_w_o, m_mix_post_g, m_ffn2_pre_g, m_ffn2_w_in, m_ffn2_w_out, m_ffn2_post_g, m_final_g]
    vars_ = [v_ffn1_pre_g, v_ffn1_w_in, v_ffn1_w_out, v_ffn1_post_g, v_mix_pre_g, v_w_in, v_w_fu, v_b_f, v_gla_norm_g, v_w_pool, v_pool_scale, v_mem_norm_g, v_w_mem_kv, v_w_up_gla, v_w_up_pool, v_w_up_xattn, v_w_o, v_mix_post_g, v_ffn2_pre_g, v_ffn2_w_in, v_ffn2_w_out, v_ffn2_post_g, v_final_g]
    P = {n: a[0] if a.ndim > 2 else a for n, a in zip(_NAMES, params)}
    Mo = {n: a[0] if a.ndim > 2 else a for n, a in zip(_NAMES, moms)}
    Vo = {n: a[0] if a.ndim > 2 else a for n, a in zip(_NAMES, vars_)}
    loss, dx, outs = _step(x[0], mem[0], loss_target[0], P, Mo, Vo)
    loss = lax.psum(loss, ("x", "y", "c"))
    out = [loss, dx[None]]
    for kind in range(4):
        for n, p in zip(_NAMES, params):
            out.append(outs[n][kind].reshape(p.shape))
    return tuple(out)
```

```python
import functools

import jax
import jax.numpy as jnp
from jax import lax
from jax.experimental import pallas as pl
from jax.experimental.pallas import tpu as pltpu

F32 = jnp.float32
BF16 = jnp.bfloat16

N_DEV = 8
D_MODEL = 1024
D_FF = 2816
FF_BLK = 2 * D_FF // N_DEV
N_FF_BLK = D_FF // FF_BLK
FF_PAD = 768
CHUNK = 64
GLA_HEADS = 4
GLA_DK = 512
GLA_DV = 1024
GLA_HDK = 128
GLA_HDV = 256
GATE_RANK = 16
GATE_TEMP = 16.0
POOL_WINDOWS = (2, 4, 8, 16)
POOL_W = 512
POOL_GD = 128
POOL_HALO = 16
XA_HEADS = 4
XA_HD = 128
XA_W = 512
EPS = 1e-6
IN_SPLITS = (GLA_DK, GLA_DK, GLA_DV, GLA_DV, GATE_RANK, POOL_W, XA_W, 3 * D_MODEL)
IN_WIDTH = sum(IN_SPLITS)
IN_SHARD = IN_WIDTH // N_DEV
INT_W = 3072 + 3072 + 1024 + 128
INT_NBLK = 3
INT_BLK = INT_W // INT_NBLK
FLOW_W = 128
INT_MAIN = INT_W - FLOW_W
_POOL_COL = (3072 + 3072) // POOL_W
_XQ_COL = (3072 + 3072 + POOL_W) // XA_W

ADAM_LR = 0.001
ADAM_B1 = 0.9
ADAM_B2 = 0.999
ADAM_EPS = 1e-08
ADAM_WD = 0.01
ADAM_STEP = 10

VMEM_LIMIT = 56 * 1024 * 1024

_NN = (((1,), (0,)), ((), ()))
_NT = (((1,), (1,)), ((), ()))
_TN = (((0,), (0,)), ((), ()))


def _pcall(body, *, name, grid, in_specs, out_specs, out_shape, scratch=(), aliases=None):
    return pl.pallas_call(
        body, name=name, grid=grid, in_specs=in_specs, out_specs=out_specs, out_shape=out_shape,
        scratch_shapes=list(scratch), input_output_aliases=aliases or {},
        compiler_params=pltpu.CompilerParams(dimension_semantics=("arbitrary",) * len(grid),
                                             vmem_limit_bytes=VMEM_LIMIT))


def _dot(a, b, dims=_NN):
    return lax.dot_general(a.astype(BF16), b.astype(BF16), dims, preferred_element_type=F32)


def _mm(name, a, b, *, grid, a_blk, a_map, b_blk, b_map, o_shape, o_blk, o_map, dims, out_dtype=F32, deps=()):
    nk = grid[2]

    def body(a_ref, b_ref, *rest):
        o_ref, scr = rest[len(deps)], rest[len(deps) + 1:]
        p = _dot(a_ref[...], b_ref[...], dims)
        if nk == 1:
            o_ref[...] = p.astype(o_ref.dtype)
        else:
            acc = scr[0]
            k = pl.program_id(2)

            @pl.when(k == 0)
            def _():
                acc[...] = p

            @pl.when(k > 0)
            def _():
                acc[...] += p

            @pl.when(k == nk - 1)
            def _():
                o_ref[...] = acc[...].astype(o_ref.dtype)

    acc_shape = tuple(d for d in o_blk if d is not None)
    return _pcall(body, name=name, grid=grid,
                  in_specs=[pl.BlockSpec(a_blk, a_map), pl.BlockSpec(b_blk, b_map)]
                  + [pl.BlockSpec(memory_space=pl.ANY)] * len(deps),
                  out_specs=pl.BlockSpec(o_blk, o_map),
                  out_shape=jax.ShapeDtypeStruct(o_shape, out_dtype),
                  scratch=[pltpu.VMEM(acc_shape, F32)] if nk > 1 else [])(a, b, *deps)


def _mm_nn(name, a, b, out_dtype=F32, tm=1024, tn=None):
    M, K = a.shape
    N = b.shape[1]
    tm, tn = min(tm, M), (tn or N)
    return _mm(name, a, b, grid=(N // tn, M // tm, 1), a_blk=(tm, K), a_map=lambda j, i, k: (i, 0),
               b_blk=(K, tn), b_map=lambda j, i, k: (0, j), o_shape=(M, N), o_blk=(tm, tn),
               o_map=lambda j, i, k: (i, j), dims=_NN, out_dtype=out_dtype)


def _mm_nt(name, a, b, out_dtype=F32, tm=1024):
    M, K = a.shape
    N = b.shape[0]
    tm = min(tm, M)
    return _mm(name, a, b, grid=(1, M // tm, 1), a_blk=(tm, K), a_map=lambda j, i, k: (i, 0),
               b_blk=(N, K), b_map=lambda j, i, k: (0, 0), o_shape=(M, N), o_blk=(tm, N),
               o_map=lambda j, i, k: (i, 0), dims=_NT, out_dtype=out_dtype)


def _mm_tn(name, a, b, out_dtype=BF16, ts=512, tn=None, tn_a=None, deps=()):
    S, M = a.shape
    N = b.shape[1]
    ts, tn, tn_a = min(ts, S), (tn or N), (tn_a or M)
    return _mm(name, a, b, grid=((N // tn) * (M // tn_a), 1, S // ts), a_blk=(ts, tn_a),
               a_map=lambda j, i, k: (k, j if tn_a < M else 0), b_blk=(ts, tn),
               b_map=lambda j, i, k: (k, j if tn < N else 0), o_shape=(M, N), o_blk=(tn_a, tn),
               o_map=lambda j, i, k: (j, 0) if tn_a < M else (0, j), dims=_TN, out_dtype=out_dtype, deps=deps)


class _Win:
    def __init__(self, arr, w, c):
        self.arr, self.w, self.c = arr, w, c


def _row_spec(x, tm):
    if isinstance(x, _Win):
        return x.arr, pl.BlockSpec((tm, x.w), functools.partial(lambda i, c: (i, c), c=x.c))
    if x.ndim == 3:
        return x, pl.BlockSpec((x.shape[0], tm, x.shape[2]), lambda i: (0, i, 0))
    return x, pl.BlockSpec((tm, x.shape[1]), lambda i: (i, 0))


def _rowwise(name, fn, rows, consts, outs, accs=(), tm=512):
    first = rows[0].arr if isinstance(rows[0], _Win) else rows[0]
    S = first.shape[1] if first.ndim == 3 else first.shape[0]
    tm = min(tm, S)
    n_in, n_out = len(rows) + len(consts), len(outs)
    arrays, in_specs = [], []
    for r in rows:
        arr, spec = _row_spec(r, tm)
        arrays.append(arr)
        in_specs.append(spec)
    for c in consts:
        arrays.append(c)
        in_specs.append(pl.BlockSpec(c.shape, functools.partial(lambda i, n: (0,) * n, n=c.ndim)))
    aliases = {}
    for k, o in enumerate(outs):
        if isinstance(o, _Win):
            aliases[len(arrays)] = k
            arrays.append(o.arr)
            in_specs.append(pl.BlockSpec(memory_space=pl.ANY))
    n_thru = len(aliases)
    out_specs = [_row_spec(o, tm)[1] for o in outs]
    out_specs += [pl.BlockSpec(a.shape, functools.partial(lambda i, n: (0,) * n, n=len(a.shape))) for a in accs]
    out_shape = [_sds(o.arr.shape, o.arr.dtype) if isinstance(o, _Win) else _sds(o.shape, o.dtype) for o in outs]

    def body(*refs):
        res = fn(*[r[...] for r in refs[:n_in]])
        if not isinstance(res, (tuple, list)):
            res = (res,)
        orefs = refs[n_in + n_thru:]
        for r, v in zip(orefs[:n_out], res[:n_out]):
            r[...] = v.astype(r.dtype)
        i = pl.program_id(0)
        for r, v in zip(orefs[n_out:], res[n_out:]):
            @pl.when(i == 0)
            def _(r=r, v=v):
                r[...] = v.astype(r.dtype)

            @pl.when(i > 0)
            def _(r=r, v=v):
                r[...] += v.astype(r.dtype)

    return _pcall(body, name=name, grid=(S // tm,), in_specs=in_specs, out_specs=out_specs,
                  out_shape=out_shape + [_sds(a.shape, a.dtype) for a in accs], aliases=aliases)(*arrays)


def _sds(shape, dtype=F32):
    return jax.ShapeDtypeStruct(shape, dtype)


def _rms(x, g):
    return x * lax.rsqrt(jnp.mean(x * x, axis=-1, keepdims=True) + EPS) * g


def _rms_bwd(x, g, dy):
    _, vjp = jax.vjp(_rms, x, g)
    return vjp(dy)


def _sigmoid(x):
    return 0.5 * jnp.tanh(0.5 * x) + 0.5


def _silu(x):
    return x * _sigmoid(x)


def _log_sigmoid(f):
    return jnp.minimum(f, 0.0) - jnp.log(1.0 + jnp.exp(-jnp.abs(f)))


def _head_rms_gate(o, g_out, gn):
    parts = [_rms(o[:, h * GLA_HDV:(h + 1) * GLA_HDV], gn[:, h * GLA_HDV:(h + 1) * GLA_HDV]) for h in range(GLA_HEADS)]
    return jnp.concatenate(parts, axis=-1) * _silu(g_out.astype(F32))


def _merge(gates, ya, yb, yc):
    gates, ya, yb, yc = (t.astype(F32) for t in (gates, ya, yb, yc))
    return (_sigmoid(gates[:, :D_MODEL]) * ya + _sigmoid(gates[:, D_MODEL:2 * D_MODEL]) * yb
            + _sigmoid(gates[:, 2 * D_MODEL:]) * yc)


def _tri_dot(t, x):
    hi = x.astype(BF16)
    r1 = x - hi.astype(F32)
    mid = r1.astype(BF16)
    lo = (r1 - mid.astype(F32)).astype(BF16)
    d = functools.partial(lax.dot_general, dimension_numbers=_NN, preferred_element_type=F32)
    return d(t, hi) + d(t, mid) + d(t, lo)


def _pre_norm(name, x, g):
    return _rowwise(name, lambda x, g: _rms(x, g), [x], [g], [_sds(x.shape, BF16)])[0]


def _ffn_fwd(tag, x, h, wt, gpost, get_wo, g_next, tgt=None):
    S = x.shape[0]
    tm = min(512, S)

    def up_body(h_ref, w_ref, u_ref, act_ref):
        hh = h_ref[...]
        for j in range(N_FF_BLK):
            ab = _dot(hh, w_ref[2 * FF_PAD * j:2 * FF_PAD * (j + 1), :], _NT)
            u_ref[:, 2 * FF_PAD * j:2 * FF_PAD * (j + 1)] = ab.astype(BF16)
            act_ref[:, FF_PAD * j:FF_PAD * (j + 1)] = (_silu(ab[:, :FF_PAD]) * ab[:, FF_PAD:]).astype(BF16)

    u, act = _pcall(
        up_body, name=tag + "_up", grid=(S // tm,),
        in_specs=[pl.BlockSpec((tm, D_MODEL), lambda i: (i, 0)),
                  pl.BlockSpec(wt.shape, lambda i: (0, 0), pipeline_mode=pl.Buffered(1))],
        out_specs=[pl.BlockSpec((tm, N_DEV * FF_PAD), lambda i: (i, 0)),
                   pl.BlockSpec((tm, N_FF_BLK * FF_PAD), lambda i: (i, 0))],
        out_shape=[_sds((S, N_DEV * FF_PAD), BF16), _sds((S, N_FF_BLK * FF_PAD), BF16)],
    )(h, wt)

    def down_body(act_ref, w_ref, x_ref, g_ref, gn_ref, f_ref, xo_ref, hn_ref):
        f = _dot(act_ref[...], w_ref[...])
        f_ref[...] = f
        xo = x_ref[...] + 0.5 * _rms(f, g_ref[...])
        xo_ref[...] = xo
        hn_ref[...] = _rms(xo, gn_ref[...]).astype(BF16)

    def down_head_body(act_ref, w_ref, x_ref, g_ref, gn_ref, t_ref, f_ref, dxo_ref, loss_ref, dgn_ref):
        i = pl.program_id(0)
        f = _dot(act_ref[...], w_ref[...])
        f_ref[...] = f
        xo = x_ref[...] + 0.5 * _rms(f, g_ref[...])
        out, vjp = jax.vjp(_rms, xo, gn_ref[...])
        e = out - t_ref[...]
        loss = 0.5 * jnp.sum(jnp.mean(e * e, axis=-1, keepdims=True), axis=0, keepdims=True)
        dxo_ref[...], dgn = vjp(e * (1.0 / D_MODEL))

        @pl.when(i == 0)
        def _():
            loss_ref[...] = jnp.broadcast_to(loss, loss_ref.shape)
            dgn_ref[...] = dgn

        @pl.when(i > 0)
        def _():
            loss_ref[...] += jnp.broadcast_to(loss, loss_ref.shape)
            dgn_ref[...] += dgn

    wo = get_wo(act)
    row = pl.BlockSpec((tm, D_MODEL), lambda i: (i, 0))
    vec = pl.BlockSpec((1, D_MODEL), lambda i: (0, 0))
    if tgt is not None:
        f, dxo, loss, dg_next = _pcall(
            down_head_body, name=tag + "_down", grid=(S // tm,),
            in_specs=[pl.BlockSpec((tm, N_FF_BLK * FF_PAD), lambda i: (i, 0)),
                      pl.BlockSpec(wo.shape, lambda i: (0, 0), pipeline_mode=pl.Buffered(1)), row, vec, vec, row],
            out_specs=[row, row, pl.BlockSpec((1, 128), lambda i: (0, 0)), vec],
            out_shape=[_sds((S, D_MODEL)), _sds((S, D_MODEL)), _sds((1, 128)), _sds((1, D_MODEL))],
        )(act, wo, x, gpost, g_next, tgt)
        return (loss[0, 0], dxo, dg_next), wo, (h, u, act, f)
    f, xo, h_next = _pcall(
        down_body, name=tag + "_down", grid=(S // tm,),
        in_specs=[pl.BlockSpec((tm, N_FF_BLK * FF_PAD), lambda i: (i, 0)),
                  pl.BlockSpec(wo.shape, lambda i: (0, 0), pipeline_mode=pl.Buffered(1)), row, vec, vec],
        out_specs=[row, row, row], out_shape=[_sds((S, D_MODEL)), _sds((S, D_MODEL)), _sds((S, D_MODEL), BF16)],
    )(act, wo, x, gpost, g_next)
    return (xo, h_next), wo, (h, u, act, f)


def _ffn_bwd(tag, dxo, x, gpre, wt, wo, gpost, saved, on_dwo=None, on_dwt=None):
    h, u, act, f = saved
    S = x.shape[0]
    tm = min(256, S)

    def dact_body(f_ref, dxo_ref, g_ref, w_ref, u_ref, df_ref, du_ref, dg_ref):
        i = pl.program_id(0)
        df, dg = _rms_bwd(f_ref[...], g_ref[...], 0.5 * dxo_ref[...])
        df = df.astype(BF16)
        df_ref[...] = df

        @pl.when(i == 0)
        def _():
            dg_ref[...] = dg

        @pl.when(i > 0)
        def _():
            dg_ref[...] += dg

        for j in range(N_FF_BLK):
            dact = _dot(df, w_ref[FF_PAD * j:FF_PAD * (j + 1), :], _NT)
            ab = u_ref[:, 2 * FF_PAD * j:2 * FF_PAD * (j + 1)].astype(F32)
            a, b = ab[:, :FF_PAD], ab[:, FF_PAD:]
            sg = _sigmoid(a)
            du_ref[:, 2 * FF_PAD * j:2 * FF_PAD * j + FF_PAD] = (dact * b * (sg * (1.0 + a * (1.0 - sg)))).astype(BF16)
            du_ref[:, 2 * FF_PAD * j + FF_PAD:2 * FF_PAD * (j + 1)] = (dact * (a * sg)).astype(BF16)

    row = pl.BlockSpec((tm, D_MODEL), lambda i: (i, 0))
    vec = pl.BlockSpec((1, D_MODEL), lambda i: (0, 0))
    u_spec = pl.BlockSpec((tm, N_DEV * FF_PAD), lambda i: (i, 0))
    df, du, dgpost = _pcall(
        dact_body, name=tag + "_dact", grid=(S // tm,),
        in_specs=[row, row, vec, pl.BlockSpec(wo.shape, lambda i: (0, 0), pipeline_mode=pl.Buffered(1)), u_spec],
        out_specs=[row, u_spec, vec],
        out_shape=[_sds((S, D_MODEL), BF16), _sds((S, N_DEV * FF_PAD), BF16), _sds((1, D_MODEL))])(f, dxo, gpost, wo, u)
    dwo = _mm_tn(tag + "_dwo", act, df, ts=2048, tn_a=FF_PAD)
    dwt = _mm_tn(tag + "_dwt", du, h, ts=2048, tn_a=FF_PAD, deps=[on_dwo(dwo)] if on_dwo is not None else [])
    dx, dgpre = _dh_pre_norm_bwd(tag + "_dh", [du], wt, x, dxo, gpre, [on_dwt(dwt)] if on_dwt is not None else [])
    return dx, dgpre, dgpost


def _dh_pre_norm_bwd(name, dzs, wt, x, dres, g, deps):
    S = x.shape[0]
    ks = [dz.shape[1] for dz in dzs]

    def body(*refs):
        dz_refs = refs[:len(dzs)]
        w_ref, x_ref, dres_ref, g_ref = refs[len(dzs):len(dzs) + 4]
        dx_ref, dg_ref = refs[len(dzs) + 4 + len(deps):]
        i = pl.program_id(0)
        dh, o = None, 0
        for dz_ref, k in zip(dz_refs, ks):
            p = _dot(dz_ref[...], w_ref[o:o + k, :])
            dh, o = (p if dh is None else dh + p), o + k
        dx, dg = _rms_bwd(x_ref[...], g_ref[...], dh)
        dx_ref[...] = dx + dres_ref[...]

        @pl.when(i == 0)
        def _():
            dg_ref[...] = dg

        @pl.when(i > 0)
        def _():
            dg_ref[...] += dg

    th = min(512, S)
    row = pl.BlockSpec((th, D_MODEL), lambda i: (i, 0))
    vec = pl.BlockSpec((1, D_MODEL), lambda i: (0, 0))
    return _pcall(
        body, name=name, grid=(S // th,),
        in_specs=[pl.BlockSpec((th, k), lambda i: (i, 0)) for k in ks]
        + [pl.BlockSpec(wt.shape, lambda i: (0, 0), pipeline_mode=pl.Buffered(1)), row, row, vec]
        + [pl.BlockSpec(memory_space=pl.ANY)] * len(deps),
        out_specs=[row, vec], out_shape=[_sds((S, D_MODEL)), _sds((1, D_MODEL))])(*dzs, wt, x, dres, g, *deps)


GLA_G = 8


def _gla_tile_common(k, flow, wfu, bf):
    f = _dot(flow, wfu) + bf
    la = _log_sigmoid(f) * (1.0 / GATE_TEMP)
    tri = _tri_matrix(True)
    ws, ds = [], []
    for g in range(k.shape[0] // CHUNK):
        b = _tri_dot(tri, la[g * CHUNK:(g + 1) * CHUNK])
        b_end = b[CHUNK - 1:CHUNK, :]
        ws.append(jnp.exp(b_end - b))
        ds.append(jnp.exp(b_end))
    w = jnp.concatenate(ws, axis=0)
    return f, w, k * w, ds


def _tri_matrix(lower):
    r = lax.broadcasted_iota(jnp.int32, (CHUNK, CHUNK), 0)
    c = lax.broadcasted_iota(jnp.int32, (CHUNK, CHUNK), 1)
    return jnp.where((r >= c) if lower else (r <= c), 1.0, 0.0).astype(BF16)


def _heads():
    return [(slice(h * GLA_HDK, (h + 1) * GLA_HDK), slice(h * GLA_HDV, (h + 1) * GLA_HDV)) for h in range(GLA_HEADS)]


def _gla_fwd(proj, wfu, bf):
    S = proj.shape[0]
    G = min(GLA_G, S // CHUNK)
    T = G * CHUNK
    nc = S // CHUNK

    def body(q_ref, k_ref, v_ref, flow_ref, wfu_ref, bf_ref, o_ref, st_ref, state):
        @pl.when(pl.program_id(0) == 0)
        def _():
            state[...] = jnp.zeros_like(state)

        _, _, kt, ds = _gla_tile_common(k_ref[...], flow_ref[...], wfu_ref[...], bf_ref[...])
        q = q_ref[...].astype(F32) * (GLA_HDK ** -0.5)
        v = v_ref[...]
        rows = [slice(g * CHUNK, (g + 1) * CHUNK) for g in range(G)]
        kv = [[_dot(v[r, vs], kt[r, ks], _TN) for ks, vs in _heads()] for r in rows]
        st = [state[vs, :] for _, vs in _heads()]
        for g, r in enumerate(rows):
            outs = []
            for h, (ks, vs) in enumerate(_heads()):
                st[h] = st[h] * ds[g][:, ks] + kv[g][h]
                st_ref[g, vs, :] = st[h]
                outs.append(_dot(q[r, ks], st[h], _NT))
            o_ref[r, :] = jnp.concatenate(outs, axis=-1)
        for h, (_, vs) in enumerate(_heads()):
            state[vs, :] = st[h]

    return _pcall(
        body, name="gla_fwd", grid=(S // T,),
        in_specs=[pl.BlockSpec((T, GLA_DK), lambda c: (c, 0)), pl.BlockSpec((T, GLA_DK), lambda c: (c, 1)),
                  pl.BlockSpec((T, GLA_DV), lambda c: (c, 1)),
                  pl.BlockSpec((T, FLOW_W), lambda c: (c, (INT_W - FLOW_W) // FLOW_W)),
                  pl.BlockSpec(wfu.shape, lambda c: (0, 0)), pl.BlockSpec(bf.shape, lambda c: (0, 0))],
        out_specs=[pl.BlockSpec((T, GLA_DV), lambda c: (c, 0)),
                   pl.BlockSpec((G, GLA_DV, GLA_HDK), lambda c: (c, 0, 0))],
        out_shape=[_sds((S, GLA_DV)), _sds((nc, GLA_DV, GLA_HDK))],
        scratch=[pltpu.VMEM((GLA_DV, GLA_HDK), F32)])(proj, proj, proj, proj, wfu, bf)


def _gla_bwd(proj, wfu, bf, states, do, dmain):
    S = proj.shape[0]
    G = min(GLA_G, S // CHUNK)
    T = G * CHUNK
    nt = S // T

    def body(q_ref, k_ref, v_ref, flow_ref, wfu_ref, bf_ref, st_ref, stp_ref, do_ref, dmain_in,
             dqkv_ref, dflow_ref, dwfu_ref, dbf_ref, dstate):
        dq_ref = dqkv_ref.at[:, 0:GLA_DK]
        dk_ref = dqkv_ref.at[:, GLA_DK:2 * GLA_DK]
        dv_ref = dqkv_ref.at[:, 2 * GLA_DK:]
        step = pl.program_id(0)

        @pl.when(step == 0)
        def _():
            dstate[...] = jnp.zeros_like(dstate)
            dwfu_ref[...] = jnp.zeros_like(dwfu_ref)
            dbf_ref[...] = jnp.zeros_like(dbf_ref)

        flow, wfu_v = flow_ref[...], wfu_ref[...]
        f, w, kt, ds = _gla_tile_common(k_ref[...], flow, wfu_v, bf_ref[...])
        q = q_ref[...].astype(F32) * (GLA_HDK ** -0.5)
        v, dout = v_ref[...], do_ref[...]
        rows = [slice(g * CHUNK, (g + 1) * CHUNK) for g in range(G)]
        dq = [jnp.concatenate([_dot(dout[r, vs], st_ref[g, vs, :]) for _, vs in _heads()], axis=-1)
              for g, r in enumerate(rows)]
        qdo = [[_dot(dout[r, vs], q[r, ks], _TN) for ks, vs in _heads()] for r in rows]
        dq_ref[...] = (jnp.concatenate(dq, axis=0) * (GLA_HDK ** -0.5)).astype(dq_ref.dtype)
        has_prev = (step < nt - 1).astype(F32)
        carry = [dstate[vs, :] for _, vs in _heads()]
        dkt, dv, dd = [None] * G, [None] * G, [None] * G
        for g in reversed(range(G)):
            r = rows[g]
            dkts, dvs, dds = [], [], []
            for h, (ks, vs) in enumerate(_heads()):
                dst = carry[h] + qdo[g][h]
                dkts.append(_dot(v[r, vs], dst))
                dvs.append(_dot(kt[r, ks], dst, _NT))
                st_prev = st_ref[g - 1, vs, :] if g > 0 else stp_ref[vs, :] * has_prev
                dds.append(jnp.sum(dst * st_prev, axis=0, keepdims=True))
                carry[h] = dst * ds[g][:, ks]
            dkt[g], dv[g], dd[g] = (jnp.concatenate(t, axis=-1) for t in (dkts, dvs, dds))
        for h, (_, vs) in enumerate(_heads()):
            dstate[vs, :] = carry[h]
        dkt = jnp.concatenate(dkt, axis=0)
        dv_ref[...] = jnp.concatenate(dv, axis=0).astype(dv_ref.dtype)
        dk_ref[...] = (dkt * w).astype(dk_ref.dtype)
        de = dkt * kt
        tri = _tri_matrix(False)
        dla = []
        for g, r in enumerate(rows):
            db_end = jnp.sum(de[r], axis=0, keepdims=True) + dd[g] * ds[g]
            dla.append(db_end - _tri_dot(tri, de[r]))
        df = jnp.concatenate(dla, axis=0) * (1.0 - _sigmoid(f)) * (1.0 / GATE_TEMP)
        dflow_ref[...] = _dot(df, wfu_v, _NT).astype(dflow_ref.dtype)
        dwfu_ref[...] += _dot(flow, df, _TN)
        dbf_ref[...] += jnp.sum(df, axis=0, keepdims=True)

    rt = lambda s: nt - 1 - s
    return _pcall(
        body, name="gla_bwd", grid=(nt,),
        in_specs=[pl.BlockSpec((T, GLA_DK), lambda s: (rt(s), 0)), pl.BlockSpec((T, GLA_DK), lambda s: (rt(s), 1)),
                  pl.BlockSpec((T, GLA_DV), lambda s: (rt(s), 1)),
                  pl.BlockSpec((T, FLOW_W), lambda s: (rt(s), (INT_W - FLOW_W) // FLOW_W)),
                  pl.BlockSpec(wfu.shape, lambda s: (0, 0)), pl.BlockSpec(bf.shape, lambda s: (0, 0)),
                  pl.BlockSpec((G, GLA_DV, GLA_HDK), lambda s: (rt(s), 0, 0)),
                  pl.BlockSpec((None, GLA_DV, GLA_HDK), lambda s: (jnp.maximum(rt(s) * G - 1, 0), 0, 0)),
                  pl.BlockSpec((T, GLA_DV), lambda s: (rt(s), 0)), pl.BlockSpec(memory_space=pl.ANY)],
        out_specs=[pl.BlockSpec((T, 2 * GLA_DK + GLA_DV), lambda s: (rt(s), 0)),
                   pl.BlockSpec((T, FLOW_W), lambda s: (rt(s), 0)),
                   pl.BlockSpec(wfu.shape, lambda s: (0, 0)), pl.BlockSpec(bf.shape, lambda s: (0, 0))],
        out_shape=[_sds(dmain.shape, dmain.dtype), _sds((S, FLOW_W), BF16), _sds(wfu.shape), _sds(bf.shape)],
        scratch=[pltpu.VMEM((GLA_DV, GLA_HDK), F32)], aliases={9: 0},
    )(proj, proj, proj, proj, wfu, bf, states, states, do, dmain)


def _pool_counts(tm, i):
    t = (lax.broadcasted_iota(jnp.int32, (tm, POOL_GD), 0) + i * tm + 1).astype(F32)
    return [jnp.minimum(t, float(w)) for w in POOL_WINDOWS]


def _pool_fwd(proj, w_pool, pool_scale, tm=512):
    S = proj.shape[0]
    tm = min(tm, S // 2)
    col = _POOL_COL
    hb = tm // POOL_HALO

    def body(p_ref, halo_ref, wp_ref, sc_ref, mixed_ref, out_ref):
        i = pl.program_id(0)
        p = p_ref[...].astype(F32)
        halo = halo_ref[...].astype(F32) * (i > 0).astype(F32)
        ext = jnp.concatenate([halo, p], axis=0)
        n = tm + POOL_HALO
        sums, acc, k = {}, ext, 1
        while k < POOL_WINDOWS[-1]:
            acc = acc + pltpu.roll(acc, k, axis=0)
            k *= 2
            sums[k] = acc
        cnts = _pool_counts(tm, i)
        mixed, lin = [], []
        for g, w in enumerate(POOL_WINDOWS):
            ls = slice(g * POOL_GD, (g + 1) * POOL_GD)
            m = sums[w][POOL_HALO:n, ls] / cnts[g] - p[:, ls]
            mixed.append(m)
            lin.append(_dot(m, wp_ref[g]))
        mixed_ref[...] = jnp.concatenate(mixed, axis=-1)
        out_ref[...] = (jnp.concatenate(lin, axis=-1) * sc_ref[...]).astype(out_ref.dtype)

    return _pcall(
        body, name="pool_fwd", grid=(S // tm,),
        in_specs=[pl.BlockSpec((tm, POOL_W), lambda i: (i, col)),
                  pl.BlockSpec((POOL_HALO, POOL_W), lambda i: (jnp.maximum(i * hb - 1, 0), col)),
                  pl.BlockSpec(w_pool.shape, lambda i: (0, 0, 0)), pl.BlockSpec(pool_scale.shape, lambda i: (0, 0))],
        out_specs=[pl.BlockSpec((tm, POOL_W), lambda i: (i, 0)), pl.BlockSpec((tm, POOL_W), lambda i: (i, 0))],
        out_shape=[_sds((S, POOL_W)), _sds((S, POOL_W), BF16)])(proj, proj, w_pool, pool_scale)


def _pool_lin_bwd(dout, mixed, w_pool, pool_scale):
    S = dout.shape[0]

    def fn(dout, mixed, wp, sc):
        dlin = dout * sc
        dm, dwp, lin = [], [], []
        for g in range(len(POOL_WINDOWS)):
            ls = slice(g * POOL_GD, (g + 1) * POOL_GD)
            lin.append(_dot(mixed[:, ls], wp[g]))
            dm.append(_dot(dlin[:, ls], wp[g], _NT))
            dwp.append(_dot(mixed[:, ls], dlin[:, ls], _TN))
        dsc = jnp.sum(dout * jnp.concatenate(lin, axis=-1), axis=0, keepdims=True)
        return jnp.concatenate(dm, axis=-1), jnp.concatenate(dwp, axis=0), dsc

    return _rowwise("pool_lin_bwd", fn, [dout, mixed], [w_pool, pool_scale], [_sds((S, POOL_W))],
                    [_sds((len(POOL_WINDOWS) * POOL_GD, POOL_GD)), _sds((1, POOL_W))])


def _pool_win_bwd(dmixed, dmain, tm=512):
    S = dmixed.shape[0]
    tm = min(tm, S // 2)
    nt = S // tm
    hb = tm // POOL_HALO

    def body(dm_ref, halo_ref, dmain_in, dp_ref):
        i = pl.program_id(0)
        dm = dm_ref[...]
        halo = halo_ref[...] * (i < nt - 1).astype(F32)
        cnts = _pool_counts(tm, i)
        cnts_h = [c[:POOL_HALO] for c in _pool_counts(tm, i + 1)]
        r = jnp.concatenate([jnp.concatenate([dm[:, g * POOL_GD:(g + 1) * POOL_GD] / cnts[g] for g in range(4)], axis=-1),
                             jnp.concatenate([halo[:, g * POOL_GD:(g + 1) * POOL_GD] / cnts_h[g] for g in range(4)], axis=-1)],
                            axis=0)
        n = tm + POOL_HALO
        sums, acc, k = {}, r, 1
        while k < POOL_WINDOWS[-1]:
            acc = acc + pltpu.roll(acc, n - k, axis=0)
            k *= 2
            sums[k] = acc
        dp = [sums[w][:tm, g * POOL_GD:(g + 1) * POOL_GD] for g, w in enumerate(POOL_WINDOWS)]
        dp_ref[...] = (jnp.concatenate(dp, axis=-1) - dm).astype(dp_ref.dtype)

    return _pcall(
        body, name="pool_win_bwd", grid=(nt,),
        in_specs=[pl.BlockSpec((tm, POOL_W), lambda i: (i, 0)),
                  pl.BlockSpec((POOL_HALO, POOL_W), lambda i: (jnp.minimum((i + 1) * hb, S // POOL_HALO - 1), 0)),
                  pl.BlockSpec(memory_space=pl.ANY)],
        out_specs=pl.BlockSpec((tm, POOL_W), lambda i: (i, _POOL_COL)),
        out_shape=_sds(dmain.shape, dmain.dtype), aliases={2: 0})(dmixed, dmixed, dmain)


def _xattn_probs(q, kv, h):
    hs = slice(h * XA_HD, (h + 1) * XA_HD)
    s = _dot(q[:, hs], kv[:, hs], _NT) * (XA_HD ** -0.5)
    s = s - jnp.max(s, axis=-1, keepdims=True)
    e = jnp.exp(s)
    return e / jnp.sum(e, axis=-1, keepdims=True)


def _xattn_fwd(proj, kv):
    S = proj.shape[0]

    def fn(q, kv):
        outs = []
        for h in range(XA_HEADS):
            p = _xattn_probs(q, kv, h)
            outs.append(_dot(p, kv[:, XA_W + h * XA_HD:XA_W + (h + 1) * XA_HD]))
        return jnp.concatenate(outs, axis=-1)

    return _rowwise("xattn_fwd", fn, [_Win(proj, XA_W, _XQ_COL)], [kv], [_sds((S, XA_W), BF16)])[0]


def _xattn_bwd(proj, kv, dxa, dmain):
    def fn(q, dxa, kv):
        dqs, dks, dvs = [], [], []
        for h in range(XA_HEADS):
            hs = slice(h * XA_HD, (h + 1) * XA_HD)
            vh = kv[:, XA_W + h * XA_HD:XA_W + (h + 1) * XA_HD]
            p = _xattn_probs(q, kv, h)
            dp = _dot(dxa[:, hs], vh, _NT)
            ds = p * (dp - jnp.sum(p * dp, axis=-1, keepdims=True)) * (XA_HD ** -0.5)
            dqs.append(_dot(ds, kv[:, hs]))
            dks.append(_dot(ds, q[:, hs], _TN))
            dvs.append(_dot(p, dxa[:, hs], _TN))
        return jnp.concatenate(dqs, axis=-1), jnp.concatenate(dks + dvs, axis=-1)

    return _rowwise("xattn_bwd", fn, [_Win(proj, XA_W, _XQ_COL), dxa], [kv], [_Win(dmain, XA_W, _XQ_COL)],
                    [_sds(kv.shape)])


def _mix_fwd(x1, h, mem, W, g_next):
    S = x1.shape[0]
    M = mem.shape[0]
    tm = min(512, S)
    def proj_body(h_ref, w_ref, o_ref):
        hh = h_ref[...]
        for j in range(INT_NBLK):
            o_ref[:, INT_BLK * j:INT_BLK * (j + 1)] = _dot(hh, w_ref[INT_BLK * j:INT_BLK * (j + 1), :], _NT).astype(BF16)

    proj = _pcall(
        proj_body, name="mix_proj", grid=(S // tm,),
        in_specs=[pl.BlockSpec((tm, D_MODEL), lambda i: (i, 0)),
                  pl.BlockSpec((INT_W, D_MODEL), lambda i: (0, 0), pipeline_mode=pl.Buffered(1))],
        out_specs=pl.BlockSpec((tm, INT_W), lambda i: (i, 0)), out_shape=_sds((S, INT_W), BF16))(h, W["w_int_t"])
    o_raw, states = _gla_fwd(proj, W["w_fu_pad"], W["b_f"])
    ya_in = _rowwise("gla_out", _head_rms_gate, [o_raw, _Win(proj, GLA_DV, 2)], [W["gla_norm_g"]],
                     [_sds((S, GLA_DV), BF16)])[0]
    mixed, pool_out = _pool_fwd(proj, W["w_pool"], W["pool_scale"])
    mem_n = _rowwise("mem_norm", lambda m, g: _rms(m, g), [mem], [W["mem_norm_g"]], [_sds((M, D_MODEL), BF16)])[0]
    kv = _mm_nn("mem_kv", mem_n, W["w_mem_kv"])
    xa = _xattn_fwd(proj, kv)
    def out_fn(gates, ya_in, pool_out, xa, x1, wa, wb, wc, wo, g, gn):
        ya, yb, yc = (_dot(t, w).astype(BF16) for t, w in ((ya_in, wa), (pool_out, wb), (xa, wc)))
        merged = _merge(gates, ya, yb, yc).astype(BF16)
        y = _dot(merged, wo)
        x2 = x1 + _rms(y, g)
        return ya, yb, yc, merged, y, x2, _rms(x2, gn)

    ya, yb, yc, merged, y, x2, h_next = _rowwise(
        "mix_out", out_fn, [_Win(proj, 3 * D_MODEL, 1), ya_in, pool_out, xa, x1],
        [W["w_up_gla"], W["w_up_pool"], W["w_up_xattn"], W["w_o"], W["mix_post_g"], g_next],
        [_sds((S, D_MODEL), BF16)] * 4 + [_sds((S, D_MODEL)), _sds((S, D_MODEL)), _sds((S, D_MODEL), BF16)], tm=256)
    return x2, h_next, (h, proj, o_raw, states, ya_in, mixed, pool_out, mem_n, kv, xa, ya, yb, yc, merged, y)


def _mix_bwd(dx2, x1, mem, W, saved, on_grads=None):
    h, proj, o_raw, states, ya_in, mixed, pool_out, mem_n, kv, xa, ya, yb, yc, merged, y = saved
    S = x1.shape[0]

    def out_bwd(y, dx2, gates, ya, yb, yc, g, wo, wa, wb, wc):
        dy, dg = _rms_bwd(y, g, dx2)
        dy = dy.astype(BF16)
        _, vjp = jax.vjp(_merge, gates, ya, yb, yc)
        dgates, dya, dyb, dyc = vjp(_dot(dy, wo, _NT))
        return (dy, dgates, dya, dyb, dyc, _dot(dya, wa, _NT), _dot(dyb, wb, _NT), _dot(dyc, wc, _NT), dg)

    dmain = lax.empty((S, INT_MAIN), BF16)
    dy, dmain, dya, dyb, dyc, d_ya_in, d_pool_out, d_xa, d_mix_post_g = _rowwise(
        "mix_out_bwd", out_bwd, [y, dx2, _Win(proj, 3 * D_MODEL, 1), ya, yb, yc],
        [W["mix_post_g"], W["w_o"], W["w_up_gla"], W["w_up_pool"], W["w_up_xattn"]],
        [_sds((S, D_MODEL), BF16), _Win(dmain, 3 * D_MODEL, 1)] + [_sds((S, D_MODEL), BF16)] * 4
        + [_sds((S, POOL_W), BF16), _sds((S, XA_W), BF16)], [_sds((1, D_MODEL))], tm=256)
    d_w_o = _mm_tn("d_w_o", merged, dy)
    d_w_up_gla = _mm_tn("d_w_up_gla", ya_in, dya)
    d_w_up_pool = _mm_tn("d_w_up_pool", pool_out, dyb)
    d_w_up_xattn = _mm_tn("d_w_up_xattn", xa, dyc)

    def gla_out_bwd(o, g_out, d, gn):
        _, vjp = jax.vjp(_head_rms_gate, o, g_out, gn)
        return vjp(d.astype(F32))

    do_raw, dmain, d_gla_norm_g = _rowwise("gla_out_bwd", gla_out_bwd, [o_raw, _Win(proj, GLA_DV, 2), d_ya_in],
                                           [W["gla_norm_g"]], [_sds((S, GLA_DV)), _Win(dmain, GLA_DV, 2)],
                                           [_sds((1, GLA_DV))])
    dmain, dflow, d_wfu_pad, d_b_f = _gla_bwd(proj, W["w_fu_pad"], W["b_f"], states, do_raw, dmain)
    dmixed, d_w_pool, d_pool_scale = _pool_lin_bwd(d_pool_out, mixed, W["w_pool"], W["pool_scale"])
    dmain = _pool_win_bwd(dmixed, dmain)
    dmain, dkv = _xattn_bwd(proj, kv, d_xa, dmain)
    d_w_mem_kv = _mm_tn("d_w_mem_kv", mem_n, dkv)
    dmem_n = _mm_nt("d_mem_n", dkv, W["w_mem_kv"])
    d_mem_norm_g = _rowwise("mem_norm_bwd", lambda m, d, g: _rms_bwd(m, g, d)[1], [mem, dmem_n], [W["mem_norm_g"]], [],
                            [_sds((1, D_MODEL))])[0]
    d_w_int_t = (_mm_tn("d_w_int", dmain, h, ts=1024, tn_a=INT_MAIN // 4), _mm_tn("d_w_flow", dflow, h, ts=2048))
    grads = dict(
        w_int_t=d_w_int_t, w_fu_pad=d_wfu_pad, b_f=d_b_f, gla_norm_g=d_gla_norm_g, w_pool=d_w_pool,
        pool_scale=d_pool_scale, mem_norm_g=d_mem_norm_g, w_mem_kv=d_w_mem_kv, w_up_gla=d_w_up_gla,
        w_up_pool=d_w_up_pool, w_up_xattn=d_w_up_xattn, w_o=d_w_o, mix_post_g=d_mix_post_g)
    deps = [on_grads(grads)] if on_grads is not None else []
    dx1, grads["mix_pre_g"] = _dh_pre_norm_bwd("d_mix_h", [dmain, dflow], W["w_int_t"], x1, dx2, W["mix_pre_g"], deps)
    return dx1, grads


def _mesh_pos():
    x, y, c = lax.axis_index("x"), lax.axis_index("y"), lax.axis_index("c")
    return x, y, c, 4 * x + 2 * y + c


def _peer(x, y, c, r):
    px = 1 - x if r & 4 else x
    py = 1 - y if r & 2 else y
    pc = 1 - c if r & 1 else c
    return (px, py, pc), 4 * px + 2 * py + pc


_ALL_PEERS = tuple(range(1, N_DEV))
_SIBLING = 1
_SAME_CORE = (2, 4, 6)


def _dev_slot(ref, dev):
    return ref.at[dev]


def _zero_pad_rows(land):
    pad = FF_PAD - FF_BLK

    def body(land_in, o_ref):
        o_ref[...] = jnp.zeros_like(o_ref)

    return _pcall(body, name="zero_pad_rows", grid=(land.shape[0],), in_specs=[pl.BlockSpec(memory_space=pl.ANY)],
                  out_specs=pl.BlockSpec((None, pad, D_MODEL), lambda j: (j, FF_BLK // pad, 0)),
                  out_shape=_sds(land.shape, land.dtype), aliases={0: 0})(land)


class _Plan:
    def __init__(self, scatter, slots=None, shapes=None):
        self.scatter, self.slots, self.shapes = scatter, slots or {}, shapes or {}

    def src(self, srcs, a, dev):
        return self.slots.get(a, _dev_slot)(srcs[a], dev) if self.scatter else srcs[a]

    def dst(self, lands, a, dev):
        return lands[a].at[dev] if self.scatter else self.slots.get(a, _dev_slot)(lands[a], dev)

    def landing_zones(self, arrays):
        lands = []
        for a, arr in enumerate(arrays):
            if self.scatter:
                lands.append(lax.empty((N_DEV,) + tuple(self.shapes.get(a, arr.shape[1:])), arr.dtype))
            elif a in self.shapes:
                lands.append(_zero_pad_rows(lax.empty(self.shapes[a], arr.dtype)))
            else:
                lands.append(lax.empty((N_DEV,) + arr.shape, arr.dtype))
        return lands


_GATHER, _SCATTER = _Plan(False), _Plan(True)


def _peer_copies(srcs, lands, send_sems, recv_sems, plan, peers=_ALL_PEERS):
    x, y, c, me = _mesh_pos()
    cps = []
    for r in peers:
        pos, peer = _peer(x, y, c, r)
        for a in range(len(srcs)):
            k = a * (N_DEV - 1) + r - 1
            cps.append(pltpu.make_async_remote_copy(
                src_ref=plan.src(srcs, a, peer), dst_ref=plan.dst(lands, a, me),
                send_sem=send_sems.at[k], recv_sem=recv_sems.at[k], device_id=pos,
                device_id_type=pl.DeviceIdType.MESH))
    return cps


_HBM = pl.BlockSpec(memory_space=pltpu.HBM)
_SEM = pl.BlockSpec(memory_space=pltpu.SEMAPHORE)
_EFFECT = pltpu.SideEffectType.DATAFLOW_SIDE_EFFECTING


def _own_copies(srcs, lands, own_sems, plan):
    me = _mesh_pos()[3]
    return [pltpu.make_async_copy(plan.src(srcs, a, me), plan.dst(lands, a, me), own_sems.at[a])
            for a in range(len(srcs))]


def _exchange_start(name, arrays, plan):
    n = len(arrays)
    lands = plan.landing_zones(arrays)
    n_sem = n * (N_DEV - 1)

    def body(*refs):
        srcs, lands_ = refs[:n], refs[n:2 * n]
        send_sems, recv_sems, own_sems = refs[2 * n:2 * n + 3]
        token = refs[-1]
        for cp in _peer_copies(srcs, lands_, send_sems, recv_sems, plan) + _own_copies(srcs, lands_, own_sems, plan):
            cp.start()
        token[...] = jnp.zeros_like(token)

    hbm = lambda a: pltpu.HBM(a.shape, a.dtype)
    res = pl.pallas_call(
        body, name=name,
        out_shape=(pltpu.SemaphoreType.DMA((n_sem,)), pltpu.SemaphoreType.DMA((n_sem,)), pltpu.SemaphoreType.DMA((n,)),
                   *[hbm(a) for a in arrays], *[hbm(a) for a in lands], _sds((8, 128))),
        in_specs=[_HBM] * (2 * n),
        out_specs=(_SEM, _SEM, _SEM, *[_HBM] * (2 * n), pl.BlockSpec(memory_space=pltpu.VMEM)),
        input_output_aliases={i: 3 + i for i in range(2 * n)},
        compiler_params=pltpu.CompilerParams(has_side_effects=_EFFECT),
    )(*[pltpu.with_memory_space_constraint(a, pltpu.HBM) for a in list(arrays) + lands])
    return (res[:3], res[3:3 + n], res[3 + n:3 + 2 * n], plan), res[-1]


def _exchange_wait(name, state, after):
    sems, srcs, lands, plan = state
    n = len(srcs)

    def body(*refs):
        srcs_, lands_ = refs[:n], refs[n:2 * n]
        send_sems, recv_sems, own_sems = refs[2 * n:2 * n + 3]
        for cp in _peer_copies(srcs_, lands_, send_sems, recv_sems, plan):
            cp.wait_send()
            cp.wait_recv()
        for cp in _own_copies(srcs_, lands_, own_sems, plan):
            cp.wait()

    hbm = lambda a: pltpu.HBM(a.shape, a.dtype)
    res = pl.pallas_call(
        body, name=name, out_shape=tuple(hbm(a) for a in list(srcs) + list(lands)),
        in_specs=[_HBM] * (2 * n) + [_SEM] * 3 + [pl.BlockSpec(memory_space=pl.ANY)] * len(after),
        out_specs=tuple([_HBM] * (2 * n)), input_output_aliases={i: i for i in range(2 * n)},
        compiler_params=pltpu.CompilerParams(has_side_effects=_EFFECT),
    )(*srcs, *lands, *sems, *after)
    return res[n:]


def _gather_start(name, arrays, plan, after):
    n = len(arrays)
    lands = plan.landing_zones(arrays)
    n_sem = n * (N_DEV - 1)

    def body(*refs):
        srcs, lands_ = refs[:n], refs[n:2 * n]
        send_sems, recv_sems, own_sems = refs[2 * n + len(after):2 * n + len(after) + 3]
        token = refs[-1]
        for cp in (_peer_copies(srcs, lands_, send_sems, recv_sems, plan, (_SIBLING,) + _SAME_CORE)
                   + _own_copies(srcs, lands_, own_sems, plan)):
            cp.start()
        token[...] = jnp.zeros_like(token)

    hbm = lambda a: pltpu.HBM(a.shape, a.dtype)
    res = pl.pallas_call(
        body, name=name,
        out_shape=(pltpu.SemaphoreType.DMA((n_sem,)), pltpu.SemaphoreType.DMA((n_sem,)), pltpu.SemaphoreType.DMA((n,)),
                   *[hbm(a) for a in arrays], *[hbm(a) for a in lands], _sds((8, 128))),
        in_specs=[_HBM] * (2 * n) + [pl.BlockSpec(memory_space=pl.ANY)] * len(after),
        out_specs=(_SEM, _SEM, _SEM, *[_HBM] * (2 * n), pl.BlockSpec(memory_space=pltpu.VMEM)),
        input_output_aliases={i: 3 + i for i in range(2 * n)},
        compiler_params=pltpu.CompilerParams(has_side_effects=_EFFECT),
    )(*[pltpu.with_memory_space_constraint(a, pltpu.HBM) for a in list(arrays) + lands], *after)
    return (res[:3], res[3:3 + n], res[3 + n:3 + 2 * n], plan), res[-1]


def _pass_on_copies(lands, send_sems, recv_sems, plan):
    x, y, c, _ = _mesh_pos()
    sibling = _peer(x, y, c, _SIBLING)[0]
    cps = []
    for i, r in enumerate(_SAME_CORE):
        owner = _peer(x, y, c, r)[1]
        for a in range(len(lands)):
            k = a * len(_SAME_CORE) + i
            cps.append(pltpu.make_async_remote_copy(
                src_ref=plan.dst(lands, a, owner), dst_ref=plan.dst(lands, a, owner), send_sem=send_sems.at[k],
                recv_sem=recv_sems.at[k], device_id=sibling, device_id_type=pl.DeviceIdType.MESH))
    return cps


def _gather_pass_on(name, state, after):
    sems, srcs, lands, plan = state
    n = len(srcs)
    n_sem = n * len(_SAME_CORE)

    def body(*refs):
        srcs_, lands_ = refs[:n], refs[n:2 * n]
        send_sems, recv_sems = refs[2 * n], refs[2 * n + 1]
        on_send, on_recv = refs[2 * n + 3 + len(after)], refs[2 * n + 4 + len(after)]
        arrivals = _peer_copies(srcs_, lands_, send_sems, recv_sems, plan, _SAME_CORE)
        for arrived, on in zip(arrivals, _pass_on_copies(lands_, on_send, on_recv, plan)):
            arrived.wait_recv()
            on.start()

    hbm = lambda a: pltpu.HBM(a.shape, a.dtype)
    res = pl.pallas_call(
        body, name=name,
        out_shape=(pltpu.SemaphoreType.DMA((n_sem,)), pltpu.SemaphoreType.DMA((n_sem,)),
                   *[hbm(a) for a in list(srcs) + list(lands)]),
        in_specs=[_HBM] * (2 * n) + [_SEM] * 3 + [pl.BlockSpec(memory_space=pl.ANY)] * len(after),
        out_specs=(_SEM, _SEM, *[_HBM] * (2 * n)), input_output_aliases={i: 2 + i for i in range(2 * n)},
        compiler_params=pltpu.CompilerParams(has_side_effects=_EFFECT),
    )(*srcs, *lands, *sems, *after)
    return sems, res[:2], res[2:2 + n], res[2 + n:], plan


def _gather_wait(name, state, after):
    sems, on_sems, srcs, lands, plan = state
    n = len(srcs)

    def body(*refs):
        srcs_, lands_ = refs[:n], refs[n:2 * n]
        send_sems, recv_sems, own_sems, on_send, on_recv = refs[2 * n:2 * n + 5]
        for cp in _peer_copies(srcs_, lands_, send_sems, recv_sems, plan, (_SIBLING,)):
            cp.wait_recv()
        for cp in _peer_copies(srcs_, lands_, send_sems, recv_sems, plan, (_SIBLING,) + _SAME_CORE):
            cp.wait_send()
        for cp in _own_copies(srcs_, lands_, own_sems, plan):
            cp.wait()
        for cp in _pass_on_copies(lands_, on_send, on_recv, plan):
            cp.wait_send()
            cp.wait_recv()

    hbm = lambda a: pltpu.HBM(a.shape, a.dtype)
    res = pl.pallas_call(
        body, name=name, out_shape=tuple(hbm(a) for a in list(srcs) + list(lands)),
        in_specs=[_HBM] * (2 * n) + [_SEM] * 5 + [pl.BlockSpec(memory_space=pl.ANY)] * len(after),
        out_specs=tuple([_HBM] * (2 * n)), input_output_aliases={i: i for i in range(2 * n)},
        compiler_params=pltpu.CompilerParams(has_side_effects=_EFFECT),
    )(*srcs, *lands, *sems, *on_sems, *after)
    return res[n:]


def _adam(name, recv, w, m, v):
    shape = w.shape
    R, C = shape
    tr = R
    while N_DEV * tr * C * 4 > 6 * 1024 * 1024 and tr % 32 == 0:
        tr //= 2

    def body(recv_ref, w_ref, m_ref, v_ref, g_ref, d_ref, m2_ref, v2_ref):
        g = recv_ref[0].astype(F32)
        for j in range(1, N_DEV):
            g = g + recv_ref[j].astype(F32)
        w_, m_, v_ = w_ref[...], m_ref[...], v_ref[...]
        m2 = ADAM_B1 * m_ + (1.0 - ADAM_B1) * g
        v2 = ADAM_B2 * v_ + (1.0 - ADAM_B2) * (g * g)
        m_hat = m2 / (1.0 - ADAM_B1 ** ADAM_STEP)
        v_hat = v2 / (1.0 - ADAM_B2 ** ADAM_STEP)
        g_ref[...] = g
        d_ref[...] = -ADAM_LR * (m_hat / (jnp.sqrt(v_hat) + ADAM_EPS) + ADAM_WD * w_)
        m2_ref[...] = m2
        v2_ref[...] = v2

    blk = pl.BlockSpec((tr, C), lambda i: (i, 0))
    return _pcall(body, name=name, grid=(R // tr,),
                  in_specs=[pl.BlockSpec((N_DEV, tr, C), lambda i: (0, i, 0)), blk, blk, blk],
                  out_specs=[blk] * 4, out_shape=[_sds(shape)] * 4)(recv, w, m, v)


_NAMES = ['ffn1_pre_g', 'ffn1_w_in', 'ffn1_w_out', 'ffn1_post_g', 'mix_pre_g', 'w_in', 'w_fu', 'b_f', 'gla_norm_g',
          'w_pool', 'pool_scale', 'mem_norm_g', 'w_mem_kv', 'w_up_gla', 'w_up_pool', 'w_up_xattn', 'w_o', 'mix_post_g',
          'ffn2_pre_g', 'ffn2_w_in', 'ffn2_w_out', 'ffn2_post_g', 'final_g']
_SHARDED = ['ffn1_w_in', 'ffn1_w_out', 'w_in', 'w_fu', 'w_mem_kv', 'w_up_gla', 'w_up_pool', 'w_up_xattn', 'w_o',
            'ffn2_w_in', 'ffn2_w_out']
_COL_SHARDED = ['w_up_pool', 'w_up_xattn']


def _cols_to_full(g):
    return jnp.transpose(g, (1, 0, 2)).reshape(g.shape[1], N_DEV * g.shape[2])


def _full_to_cols(f):
    R, C = f.shape
    return jnp.transpose(f.reshape(R, N_DEV, C // N_DEV), (1, 0, 2))


def _to_internal(w_in_t):
    o = 0
    parts = []
    for s in IN_SPLITS:
        parts.append(w_in_t[o:o + s])
        o += s
    q, k, v, g_out, f_low, p_in, xq, gates = parts
    f_low = jnp.pad(f_low, ((0, FLOW_W - GATE_RANK), (0, 0)))
    return jnp.concatenate([q, k, v, g_out, gates, p_in, xq, f_low], axis=0)


def _from_internal(d, d_flow):
    q, k, v, g_out = d[0:512], d[512:1024], d[1024:2048], d[2048:3072]
    gates, p_in, xq = d[3072:6144], d[6144:6656], d[6656:7168]
    return jnp.concatenate([q, k, v, g_out, d_flow[:GATE_RANK], p_in, xq, gates], axis=0)


_FFN_IN = ('ffn1_w_in', 'ffn2_w_in')
_FFN_OUT = ('ffn1_w_out', 'ffn2_w_out')
_GATHERS = {"ffn1_in": ['ffn1_w_in'], "ffn1_out": ['ffn1_w_out'],
            "mix": ['w_in', 'w_fu', 'w_mem_kv', 'w_up_gla', 'w_up_pool', 'w_up_xattn', 'w_o'],
            "ffn2": ['ffn2_w_in', 'ffn2_w_out']}


def _ffn_in_slot(ref, d):
    return ref.at[2 * (d % N_FF_BLK) + d // N_FF_BLK, pl.ds(0, FF_BLK)]


def _ffn_out_slot(ref, d):
    rows = FF_BLK // 2
    return ref.at[d // 2, pl.ds(pl.multiple_of((d % 2) * rows, rows), rows)]


def _ffn_plan(names, scatter):
    slots, shapes = {}, {}
    for a, n in enumerate(names):
        if n in _FFN_IN:
            slots[a] = _ffn_in_slot
            shapes[a] = (FF_BLK, D_MODEL) if scatter else (N_DEV, FF_PAD, D_MODEL)
        elif n in _FFN_OUT:
            slots[a] = _ffn_out_slot
            shapes[a] = (FF_BLK // 2, D_MODEL) if scatter else (N_FF_BLK, FF_PAD, D_MODEL)
    return _Plan(scatter, slots, shapes)


def _rows_to_full(g):
    return g.reshape(N_DEV * g.shape[1], g.shape[2])


def _mix_weights(gathered):
    W = {n: _rows_to_full(gathered[n]) for n in ('w_mem_kv', 'w_up_gla', 'w_o')}
    for n in _COL_SHARDED:
        W[n] = _cols_to_full(gathered[n])
    W["w_int_t"] = _to_internal(gathered["w_in"].reshape(IN_WIDTH, D_MODEL))
    W["w_fu_pad"] = jnp.pad(_cols_to_full(gathered["w_fu"]), ((0, FLOW_W - GATE_RANK), (0, 0)))
    return W


def _mix_chunks(G, n):
    if n == "w_in":
        return _from_internal(*G["w_int_t"]).reshape(N_DEV, IN_SHARD, D_MODEL)
    if n == "w_fu":
        return _full_to_cols(G["w_fu_pad"][:GATE_RANK].astype(BF16))
    if n in _COL_SHARDED:
        return _full_to_cols(G[n])
    return G[n].reshape(N_DEV, G[n].shape[0] // N_DEV, G[n].shape[1])


def _step(x, mem, tgt, P, Mo, Vo):
    def native(n, a):
        return jnp.swapaxes(a, 0, 1) if n in _FFN_IN + ("w_in",) else a

    P, Mo, Vo = ({n: native(n, a) for n, a in d.items()} for d in (P, Mo, Vo))
    small = {n: P[n] for n in _NAMES if n not in _SHARDED}

    gather, token = {}, []
    for grp, names in _GATHERS.items():
        gather[grp], tok = _gather_start("gather_" + grp, [P[n].astype(BF16) for n in names], _ffn_plan(names, False),
                                         token)
        token = [tok]

    def gathered(grp, after):
        return _gather_wait("gather_" + grp + "_wait", _gather_pass_on("gather_" + grp + "_on", gather[grp], after), [])

    def ffn_wo(lands):
        return lambda act: lands[0].reshape(N_FF_BLK * FF_PAD, D_MODEL)

    h1 = _pre_norm("ffn1_pre", x, small["ffn1_pre_g"])
    w1t = gathered("ffn1_in", token + [h1])[0].reshape(N_DEV * FF_PAD, D_MODEL)
    (x1, hm), w1o, sv1 = _ffn_fwd("ffn1", x, h1, w1t, small["ffn1_post_g"],
                                  lambda act: ffn_wo(gathered("ffn1_out", [act]))(act), small["mix_pre_g"])

    gm = dict(zip(_GATHERS["mix"], gathered("mix", [x1])))
    Wm = {**small, **_mix_weights(gm)}
    x2, h2, svm = _mix_fwd(x1, hm, mem, Wm, small["ffn2_pre_g"])

    w2t, w2o = gathered("ffn2", [x2])
    w2t = w2t.reshape(N_DEV * FF_PAD, D_MODEL)
    (loss, dx3, d_final_g), w2o, sv2 = _ffn_fwd("ffn2", x2, h2, w2t, small["ffn2_post_g"], ffn_wo([w2o]),
                                                small["final_g"], tgt)

    G = dict(final_g=d_final_g)
    scat = {}

    def start(grp, names, arrays):
        scat[grp] = names, _exchange_start("scatter_" + grp, arrays, _ffn_plan(names, True))
        return scat[grp][1][1]

    def ffn_starts(tag):
        return (lambda dwo: start(tag + "_out", [tag + "_w_out"], [dwo.reshape(N_FF_BLK, FF_PAD, D_MODEL)]),
                lambda dwt: start(tag + "_in", [tag + "_w_in"], [dwt.reshape(N_DEV, FF_PAD, D_MODEL)]))

    dx2, G["ffn2_pre_g"], G["ffn2_post_g"] = _ffn_bwd(
        "ffn2", dx3, x2, small["ffn2_pre_g"], w2t, w2o, small["ffn2_post_g"], sv2, *ffn_starts("ffn2"))
    dx1, Gm = _mix_bwd(dx2, x1, mem, Wm, svm,
                       lambda Gm: start("mix", _GATHERS["mix"], [_mix_chunks(Gm, n) for n in _GATHERS["mix"]]))
    G.update(Gm)
    dx, G["ffn1_pre_g"], G["ffn1_post_g"] = _ffn_bwd(
        "ffn1", dx1, x, small["ffn1_pre_g"], w1t, w1o, small["ffn1_post_g"], sv1, *ffn_starts("ffn1"))

    scat["small"] = list(small), _exchange_start("gather_small_grads", [G[n] for n in small], _GATHER)
    recv, outs = {}, {}
    done = [dx]
    for grp in ["ffn2_out", "ffn2_in", "mix", "ffn1_out", "ffn1_in", "small"]:
        names, (state, _) = scat[grp]
        recv.update(zip(names, _exchange_wait(("gather_" if grp == "small" else "scatter_") + grp + "_wait", state,
                                              done)))
        for n in names:
            shp = P[n].shape
            shp2 = shp if len(shp) == 2 else (shp[0] * shp[1], shp[2])
            res = _adam("adam_" + n, recv[n].reshape((N_DEV,) + shp2), P[n].reshape(shp2), Mo[n].reshape(shp2),
                        Vo[n].reshape(shp2))
            outs[n] = [native(n, t.reshape(shp)) for t in res]
            done.append(res[-1])
    return loss, dx, outs


def kernel(x, mem, ffn1_pre_g, ffn1_w_in, ffn1_w_out, ffn1_post_g, mix_pre_g, w_in, w_fu, b_f, gla_norm_g, w_pool, pool_scale, mem_norm_g, w_mem_kv, w_up_gla, w_up_pool, w_up_xattn, w_o, mix_post_g, ffn2_pre_g, ffn2_w_in, ffn2_w_out, ffn2_post_g, final_g, loss_target, m_ffn1_pre_g, m_ffn1_w_in, m_ffn1_w_out, m_ffn1_post_g, m_mix_pre_g, m_w_in, m_w_fu, m_b_f, m_gla_norm_g, m_w_pool, m_pool_scale, m_mem_norm_g, m_w_mem_kv, m_w_up_gla, m_w_up_pool, m_w_up_xattn, m_w_o, m_mix_post_g, m_ffn2_pre_g, m_ffn2_w_in, m_ffn2_w_out, m_ffn2_post_g, m_final_g, v_ffn1_pre_g, v_ffn1_w_in, v_ffn1_w_out, v_ffn1_post_g, v_mix_pre_g, v_w_in, v_w_fu, v_b_f, v_gla_norm_g, v_w_pool, v_pool_scale, v_mem_norm_g, v_w_mem_kv, v_w_up_gla, v_w_up_pool, v_w_up_xattn, v_w_o, v_mix_post_g, v_ffn2_pre_g, v_ffn2_w_in, v_ffn2_w_out, v_ffn2_post_g, v_final_g):
    params = [ffn1_pre_g, ffn1_w_in, ffn1_w_out, ffn1_post_g, mix_pre_g, w_in, w_fu, b_f, gla_norm_g, w_pool, pool_scale, mem_norm_g, w_mem_kv, w_up_gla, w_up_pool, w_up_xattn, w_o, mix_post_g, ffn2_pre_g, ffn2_w_in, ffn2_w_out, ffn2_post_g, final_g]
    moms = [m_ffn1_pre_g, m_ffn1_w_in, m_ffn1_w_out, m_ffn1_post_g, m_mix_pre_g, m_w_in, m_w_fu, m_b_f, m_gla_norm_g, m_w_pool, m_pool_scale, m_mem_norm_g, m_w_mem_kv, m_w_up_gla, m_w_up_pool, m_w_up_xattn, m_w_o, m_mix_post_g, m_ffn2_pre_g, m_ffn2_w_in, m_ffn2_w_out, m_ffn2_post_g, m_final_g]
    vars_ = [v_ffn1_pre_g, v_ffn1_w_in, v_ffn1_w_out, v_ffn1_post_g, v_mix_pre_g, v_w_in, v_w_fu, v_b_f, v_gla_norm_g, v_w_pool, v_pool_scale, v_mem_norm_g, v_w_mem_kv, v_w_up_gla, v_w_up_pool, v_w_up_xattn, v_w_o, v_mix_post_g, v_ffn2_pre_g, v_ffn2_w_in, v_ffn2_w_out, v_ffn2_post_g, v_final_g]
    P = {n: a[0] if a.ndim > 2 else a for n, a in zip(_NAMES, params)}
    Mo = {n: a[0] if a.ndim > 2 else a for n, a in zip(_NAMES, moms)}
    Vo = {n: a[0] if a.ndim > 2 else a for n, a in zip(_NAMES, vars_)}
    loss, dx, outs = _step(x[0], mem[0], loss_target[0], P, Mo, Vo)
    loss = lax.psum(loss, ("x", "y", "c"))
    out = [loss, dx[None]]
    for kind in range(4):
        for n, p in zip(_NAMES, params):
            out.append(outs[n][kind].reshape(p.shape))
    return tuple(out)
```

```python
import functools

import jax
import jax.numpy as jnp
from jax import lax
from jax.experimental import pallas as pl
from jax.experimental.pallas import tpu as pltpu

F32 = jnp.float32
BF16 = jnp.bfloat16

N_DEV = 8
D_MODEL = 1024
D_FF = 2816
FF_BLK = 2 * D_FF // N_DEV
N_FF_BLK = D_FF // FF_BLK
FF_PAD = 768
CHUNK = 64
GLA_HEADS = 4
GLA_DK = 512
GLA_DV = 1024
GLA_HDK = 128
GLA_HDV = 256
GATE_RANK = 16
GATE_TEMP = 16.0
POOL_WINDOWS = (2, 4, 8, 16)
POOL_W = 512
POOL_GD = 128
POOL_HALO = 16
XA_HEADS = 4
XA_HD = 128
XA_W = 512
EPS = 1e-6
IN_SPLITS = (GLA_DK, GLA_DK, GLA_DV, GLA_DV, GATE_RANK, POOL_W, XA_W, 3 * D_MODEL)
IN_WIDTH = sum(IN_SPLITS)
IN_SHARD = IN_WIDTH // N_DEV
INT_W = 3072 + 3072 + 1024 + 128
INT_NBLK = 3
INT_BLK = INT_W // INT_NBLK
FLOW_W = 128
INT_MAIN = INT_W - FLOW_W
_POOL_COL = (3072 + 3072) // POOL_W
_XQ_COL = (3072 + 3072 + POOL_W) // XA_W

ADAM_LR = 0.001
ADAM_B1 = 0.9
ADAM_B2 = 0.999
ADAM_EPS = 1e-08
ADAM_WD = 0.01
ADAM_STEP = 10

VMEM_LIMIT = 56 * 1024 * 1024

_NN = (((1,), (0,)), ((), ()))
_NT = (((1,), (1,)), ((), ()))
_TN = (((0,), (0,)), ((), ()))


def _pcall(body, *, name, grid, in_specs, out_specs, out_shape, scratch=(), aliases=None):
    return pl.pallas_call(
        body, name=name, grid=grid, in_specs=in_specs, out_specs=out_specs, out_shape=out_shape,
        scratch_shapes=list(scratch), input_output_aliases=aliases or {},
        compiler_params=pltpu.CompilerParams(dimension_semantics=("arbitrary",) * len(grid),
                                             vmem_limit_bytes=VMEM_LIMIT))


def _dot(a, b, dims=_NN):
    return lax.dot_general(a.astype(BF16), b.astype(BF16), dims, preferred_element_type=F32)


def _mm(name, a, b, *, grid, a_blk, a_map, b_blk, b_map, o_shape, o_blk, o_map, dims, out_dtype=F32, deps=()):
    nk = grid[2]

    def body(a_ref, b_ref, *rest):
        o_ref, scr = rest[len(deps)], rest[len(deps) + 1:]
        p = _dot(a_ref[...], b_ref[...], dims)
        if nk == 1:
            o_ref[...] = p.astype(o_ref.dtype)
        else:
            acc = scr[0]
            k = pl.program_id(2)

            @pl.when(k == 0)
            def _():
                acc[...] = p

            @pl.when(k > 0)
            def _():
                acc[...] += p

            @pl.when(k == nk - 1)
            def _():
                o_ref[...] = acc[...].astype(o_ref.dtype)

    acc_shape = tuple(d for d in o_blk if d is not None)
    return _pcall(body, name=name, grid=grid,
                  in_specs=[pl.BlockSpec(a_blk, a_map), pl.BlockSpec(b_blk, b_map)]
                  + [pl.BlockSpec(memory_space=pl.ANY)] * len(deps),
                  out_specs=pl.BlockSpec(o_blk, o_map),
                  out_shape=jax.ShapeDtypeStruct(o_shape, out_dtype),
                  scratch=[pltpu.VMEM(acc_shape, F32)] if nk > 1 else [])(a, b, *deps)


def _mm_nn(name, a, b, out_dtype=F32, tm=1024, tn=None):
    M, K = a.shape
    N = b.shape[1]
    tm, tn = min(tm, M), (tn or N)
    return _mm(name, a, b, grid=(N // tn, M // tm, 1), a_blk=(tm, K), a_map=lambda j, i, k: (i, 0),
               b_blk=(K, tn), b_map=lambda j, i, k: (0, j), o_shape=(M, N), o_blk=(tm, tn),
               o_map=lambda j, i, k: (i, j), dims=_NN, out_dtype=out_dtype)


def _mm_nt(name, a, b, out_dtype=F32, tm=1024):
    M, K = a.shape
    N = b.shape[0]
    tm = min(tm, M)
    return _mm(name, a, b, grid=(1, M // tm, 1), a_blk=(tm, K), a_map=lambda j, i, k: (i, 0),
               b_blk=(N, K), b_map=lambda j, i, k: (0, 0), o_shape=(M, N), o_blk=(tm, N),
               o_map=lambda j, i, k: (i, 0), dims=_NT, out_dtype=out_dtype)


def _mm_tn(name, a, b, out_dtype=BF16, ts=512, tn=None, tn_a=None, deps=()):
    S, M = a.shape
    N = b.shape[1]
    ts, tn, tn_a = min(ts, S), (tn or N), (tn_a or M)
    return _mm(name, a, b, grid=((N // tn) * (M // tn_a), 1, S // ts), a_blk=(ts, tn_a),
               a_map=lambda j, i, k: (k, j if tn_a < M else 0), b_blk=(ts, tn),
               b_map=lambda j, i, k: (k, j if tn < N else 0), o_shape=(M, N), o_blk=(tn_a, tn),
               o_map=lambda j, i, k: (j, 0) if tn_a < M else (0, j), dims=_TN, out_dtype=out_dtype, deps=deps)


class _Win:
    def __init__(self, arr, w, c):
        self.arr, self.w, self.c = arr, w, c


def _row_spec(x, tm):
    if isinstance(x, _Win):
        return x.arr, pl.BlockSpec((tm, x.w), functools.partial(lambda i, c: (i, c), c=x.c))
    if x.ndim == 3:
        return x, pl.BlockSpec((x.shape[0], tm, x.shape[2]), lambda i: (0, i, 0))
    return x, pl.BlockSpec((tm, x.shape[1]), lambda i: (i, 0))


def _rowwise(name, fn, rows, consts, outs, accs=(), tm=512):
    first = rows[0].arr if isinstance(rows[0], _Win) else rows[0]
    S = first.shape[1] if first.ndim == 3 else first.shape[0]
    tm = min(tm, S)
    n_in, n_out = len(rows) + len(consts), len(outs)
    arrays, in_specs = [], []
    for r in rows:
        arr, spec = _row_spec(r, tm)
        arrays.append(arr)
        in_specs.append(spec)
    for c in consts:
        arrays.append(c)
        in_specs.append(pl.BlockSpec(c.shape, functools.partial(lambda i, n: (0,) * n, n=c.ndim)))
    aliases = {}
    for k, o in enumerate(outs):
        if isinstance(o, _Win):
            aliases[len(arrays)] = k
            arrays.append(o.arr)
            in_specs.append(pl.BlockSpec(memory_space=pl.ANY))
    n_thru = len(aliases)
    out_specs = [_row_spec(o, tm)[1] for o in outs]
    out_specs += [pl.BlockSpec(a.shape, functools.partial(lambda i, n: (0,) * n, n=len(a.shape))) for a in accs]
    out_shape = [_sds(o.arr.shape, o.arr.dtype) if isinstance(o, _Win) else _sds(o.shape, o.dtype) for o in outs]

    def body(*refs):
        res = fn(*[r[...] for r in refs[:n_in]])
        if not isinstance(res, (tuple, list)):
            res = (res,)
        orefs = refs[n_in + n_thru:]
        for r, v in zip(orefs[:n_out], res[:n_out]):
            r[...] = v.astype(r.dtype)
        i = pl.program_id(0)
        for r, v in zip(orefs[n_out:], res[n_out:]):
            @pl.when(i == 0)
            def _(r=r, v=v):
                r[...] = v.astype(r.dtype)

            @pl.when(i > 0)
            def _(r=r, v=v):
                r[...] += v.astype(r.dtype)

    return _pcall(body, name=name, grid=(S // tm,), in_specs=in_specs, out_specs=out_specs,
                  out_shape=out_shape + [_sds(a.shape, a.dtype) for a in accs], aliases=aliases)(*arrays)


def _sds(shape, dtype=F32):
    return jax.ShapeDtypeStruct(shape, dtype)


def _rms(x, g):
    return x * lax.rsqrt(jnp.mean(x * x, axis=-1, keepdims=True) + EPS) * g


def _rms_bwd(x, g, dy):
    _, vjp = jax.vjp(_rms, x, g)
    return vjp(dy)


def _sigmoid(x):
    return 0.5 * jnp.tanh(0.5 * x) + 0.5


def _silu(x):
    return x * _sigmoid(x)


def _log_sigmoid(f):
    return jnp.minimum(f, 0.0) - jnp.log(1.0 + jnp.exp(-jnp.abs(f)))


def _head_rms_gate(o, g_out, gn):
    parts = [_rms(o[:, h * GLA_HDV:(h + 1) * GLA_HDV], gn[:, h * GLA_HDV:(h + 1) * GLA_HDV]) for h in range(GLA_HEADS)]
    return jnp.concatenate(parts, axis=-1) * _silu(g_out.astype(F32))


def _merge(gates, ya, yb, yc):
    gates, ya, yb, yc = (t.astype(F32) for t in (gates, ya, yb, yc))
    return (_sigmoid(gates[:, :D_MODEL]) * ya + _sigmoid(gates[:, D_MODEL:2 * D_MODEL]) * yb
            + _sigmoid(gates[:, 2 * D_MODEL:]) * yc)


def _tri_dot(t, x):
    hi = x.astype(BF16)
    r1 = x - hi.astype(F32)
    mid = r1.astype(BF16)
    lo = (r1 - mid.astype(F32)).astype(BF16)
    d = functools.partial(lax.dot_general, dimension_numbers=_NN, preferred_element_type=F32)
    return d(t, hi) + d(t, mid) + d(t, lo)


def _pre_norm(name, x, g, after):
    S = x.shape[0]
    tm = min(512, S)

    def body(x_ref, g_ref, *rest):
        rest[-1][...] = _rms(x_ref[...], g_ref[...]).astype(BF16)

    row = pl.BlockSpec((tm, D_MODEL), lambda i: (i, 0))
    return _pcall(body, name=name, grid=(S // tm,),
                  in_specs=[row, pl.BlockSpec((1, D_MODEL), lambda i: (0, 0))] + [pl.BlockSpec(memory_space=pl.ANY)] * len(after),
                  out_specs=row, out_shape=_sds(x.shape, BF16))(x, g, *after)


def _ffn_fwd(tag, x, h, wt, gpost, get_wo, g_next, tgt=None):
    S = x.shape[0]
    tm = min(512, S)

    def up_body(h_ref, w_ref, u_ref, act_ref):
        hh = h_ref[...]
        for j in range(N_FF_BLK):
            ab = _dot(hh, w_ref[2 * FF_PAD * j:2 * FF_PAD * (j + 1), :], _NT)
            u_ref[:, 2 * FF_PAD * j:2 * FF_PAD * (j + 1)] = ab.astype(BF16)
            act_ref[:, FF_PAD * j:FF_PAD * (j + 1)] = (_silu(ab[:, :FF_PAD]) * ab[:, FF_PAD:]).astype(BF16)

    u, act = _pcall(
        up_body, name=tag + "_up", grid=(S // tm,),
        in_specs=[pl.BlockSpec((tm, D_MODEL), lambda i: (i, 0)),
                  pl.BlockSpec(wt.shape, lambda i: (0, 0), pipeline_mode=pl.Buffered(1))],
        out_specs=[pl.BlockSpec((tm, N_DEV * FF_PAD), lambda i: (i, 0)),
                   pl.BlockSpec((tm, N_FF_BLK * FF_PAD), lambda i: (i, 0))],
        out_shape=[_sds((S, N_DEV * FF_PAD), BF16), _sds((S, N_FF_BLK * FF_PAD), BF16)],
    )(h, wt)

    def down_body(act_ref, w_ref, x_ref, g_ref, gn_ref, f_ref, xo_ref, hn_ref):
        f = _dot(act_ref[...], w_ref[...])
        f_ref[...] = f
        xo = x_ref[...] + 0.5 * _rms(f, g_ref[...])
        xo_ref[...] = xo
        hn_ref[...] = _rms(xo, gn_ref[...]).astype(BF16)

    def down_head_body(act_ref, w_ref, x_ref, g_ref, gn_ref, t_ref, f_ref, dxo_ref, loss_ref, dgn_ref):
        i = pl.program_id(0)
        f = _dot(act_ref[...], w_ref[...])
        f_ref[...] = f
        xo = x_ref[...] + 0.5 * _rms(f, g_ref[...])
        out, vjp = jax.vjp(_rms, xo, gn_ref[...])
        e = out - t_ref[...]
        loss = 0.5 * jnp.sum(jnp.mean(e * e, axis=-1, keepdims=True), axis=0, keepdims=True)
        dxo_ref[...], dgn = vjp(e * (1.0 / D_MODEL))

        @pl.when(i == 0)
        def _():
            loss_ref[...] = jnp.broadcast_to(loss, loss_ref.shape)
            dgn_ref[...] = dgn

        @pl.when(i > 0)
        def _():
            loss_ref[...] += jnp.broadcast_to(loss, loss_ref.shape)
            dgn_ref[...] += dgn

    wo = get_wo(act)
    row = pl.BlockSpec((tm, D_MODEL), lambda i: (i, 0))
    vec = pl.BlockSpec((1, D_MODEL), lambda i: (0, 0))
    if tgt is not None:
        f, dxo, loss, dg_next = _pcall(
            down_head_body, name=tag + "_down", grid=(S // tm,),
            in_specs=[pl.BlockSpec((tm, N_FF_BLK * FF_PAD), lambda i: (i, 0)),
                      pl.BlockSpec(wo.shape, lambda i: (0, 0), pipeline_mode=pl.Buffered(1)), row, vec, vec, row],
            out_specs=[row, row, pl.BlockSpec((1, 128), lambda i: (0, 0)), vec],
            out_shape=[_sds((S, D_MODEL)), _sds((S, D_MODEL)), _sds((1, 128)), _sds((1, D_MODEL))],
        )(act, wo, x, gpost, g_next, tgt)
        return (loss, dxo, dg_next), wo, (h, u, act, f)
    f, xo, h_next = _pcall(
        down_body, name=tag + "_down", grid=(S // tm,),
        in_specs=[pl.BlockSpec((tm, N_FF_BLK * FF_PAD), lambda i: (i, 0)),
                  pl.BlockSpec(wo.shape, lambda i: (0, 0), pipeline_mode=pl.Buffered(1)), row, vec, vec],
        out_specs=[row, row, row], out_shape=[_sds((S, D_MODEL)), _sds((S, D_MODEL)), _sds((S, D_MODEL), BF16)],
    )(act, wo, x, gpost, g_next)
    return (xo, h_next), wo, (h, u, act, f)


def _ffn_bwd(tag, dxo, x, gpre, wt, wo, gpost, saved, on_dwo=None, on_dwt=None):
    h, u, act, f = saved
    S = x.shape[0]
    tm = min(256, S)

    def dact_body(f_ref, dxo_ref, g_ref, w_ref, u_ref, df_ref, du_ref, dg_ref):
        i = pl.program_id(0)
        df, dg = _rms_bwd(f_ref[...], g_ref[...], 0.5 * dxo_ref[...])
        df = df.astype(BF16)
        df_ref[...] = df

        @pl.when(i == 0)
        def _():
            dg_ref[...] = dg

        @pl.when(i > 0)
        def _():
            dg_ref[...] += dg

        for j in range(N_FF_BLK):
            dact = _dot(df, w_ref[FF_PAD * j:FF_PAD * (j + 1), :], _NT)
            ab = u_ref[:, 2 * FF_PAD * j:2 * FF_PAD * (j + 1)].astype(F32)
            a, b = ab[:, :FF_PAD], ab[:, FF_PAD:]
            sg = _sigmoid(a)
            du_ref[:, 2 * FF_PAD * j:2 * FF_PAD * j + FF_PAD] = (dact * b * (sg * (1.0 + a * (1.0 - sg)))).astype(BF16)
            du_ref[:, 2 * FF_PAD * j + FF_PAD:2 * FF_PAD * (j + 1)] = (dact * (a * sg)).astype(BF16)

    row = pl.BlockSpec((tm, D_MODEL), lambda i: (i, 0))
    vec = pl.BlockSpec((1, D_MODEL), lambda i: (0, 0))
    u_spec = pl.BlockSpec((tm, N_DEV * FF_PAD), lambda i: (i, 0))
    df, du, dgpost = _pcall(
        dact_body, name=tag + "_dact", grid=(S // tm,),
        in_specs=[row, row, vec, pl.BlockSpec(wo.shape, lambda i: (0, 0), pipeline_mode=pl.Buffered(1)), u_spec],
        out_specs=[row, u_spec, vec],
        out_shape=[_sds((S, D_MODEL), BF16), _sds((S, N_DEV * FF_PAD), BF16), _sds((1, D_MODEL))])(f, dxo, gpost, wo, u)
    dwo = _mm_tn(tag + "_dwo", act, df, ts=2048, tn_a=FF_PAD)
    dwt = _mm_tn(tag + "_dwt", du, h, ts=2048, tn_a=FF_PAD, deps=[on_dwo(dwo)] if on_dwo is not None else [])
    dx, dgpre = _dh_pre_norm_bwd(tag + "_dh", [du], wt, x, dxo, gpre, [on_dwt(dwt)] if on_dwt is not None else [])
    return dx, dgpre, dgpost


def _dh_pre_norm_bwd(name, dzs, wt, x, dres, g, deps):
    S = x.shape[0]
    ks = [dz.shape[1] for dz in dzs]

    def body(*refs):
        dz_refs = refs[:len(dzs)]
        w_ref, x_ref, dres_ref, g_ref = refs[len(dzs):len(dzs) + 4]
        dx_ref, dg_ref = refs[len(dzs) + 4 + len(deps):]
        i = pl.program_id(0)
        dh, o = None, 0
        for dz_ref, k in zip(dz_refs, ks):
            p = _dot(dz_ref[...], w_ref[o:o + k, :])
            dh, o = (p if dh is None else dh + p), o + k
        dx, dg = _rms_bwd(x_ref[...], g_ref[...], dh)
        dx_ref[...] = dx + dres_ref[...]

        @pl.when(i == 0)
        def _():
            dg_ref[...] = dg

        @pl.when(i > 0)
        def _():
            dg_ref[...] += dg

    th = min(512, S)
    row = pl.BlockSpec((th, D_MODEL), lambda i: (i, 0))
    vec = pl.BlockSpec((1, D_MODEL), lambda i: (0, 0))
    return _pcall(
        body, name=name, grid=(S // th,),
        in_specs=[pl.BlockSpec((th, k), lambda i: (i, 0)) for k in ks]
        + [pl.BlockSpec(wt.shape, lambda i: (0, 0), pipeline_mode=pl.Buffered(1)), row, row, vec]
        + [pl.BlockSpec(memory_space=pl.ANY)] * len(deps),
        out_specs=[row, vec], out_shape=[_sds((S, D_MODEL)), _sds((1, D_MODEL))])(*dzs, wt, x, dres, g, *deps)


GLA_G = 8


def _gla_tile_common(k, flow, wfu, bf):
    f = _dot(flow, wfu) + bf
    la = _log_sigmoid(f) * (1.0 / GATE_TEMP)
    tri = _tri_matrix(True)
    ws, ds = [], []
    for g in range(k.shape[0] // CHUNK):
        b = _tri_dot(tri, la[g * CHUNK:(g + 1) * CHUNK])
        b_end = b[CHUNK - 1:CHUNK, :]
        ws.append(jnp.exp(b_end - b))
        ds.append(jnp.exp(b_end))
    w = jnp.concatenate(ws, axis=0)
    return f, w, k * w, ds


def _tri_matrix(lower):
    r = lax.broadcasted_iota(jnp.int32, (CHUNK, CHUNK), 0)
    c = lax.broadcasted_iota(jnp.int32, (CHUNK, CHUNK), 1)
    return jnp.where((r >= c) if lower else (r <= c), 1.0, 0.0).astype(BF16)


def _heads():
    return [(slice(h * GLA_HDK, (h + 1) * GLA_HDK), slice(h * GLA_HDV, (h + 1) * GLA_HDV)) for h in range(GLA_HEADS)]


def _gla_fwd(proj, wfu, bf):
    S = proj.shape[0]
    G = min(GLA_G, S // CHUNK)
    T = G * CHUNK
    nc = S // CHUNK

    def body(q_ref, k_ref, v_ref, flow_ref, wfu_ref, bf_ref, o_ref, st_ref, state):
        @pl.when(pl.program_id(0) == 0)
        def _():
            state[...] = jnp.zeros_like(state)

        _, _, kt, ds = _gla_tile_common(k_ref[...], flow_ref[...], wfu_ref[...], bf_ref[...])
        q = q_ref[...].astype(F32) * (GLA_HDK ** -0.5)
        v = v_ref[...]
        rows = [slice(g * CHUNK, (g + 1) * CHUNK) for g in range(G)]
        kv = [[_dot(v[r, vs], kt[r, ks], _TN) for ks, vs in _heads()] for r in rows]
        st = [state[vs, :] for _, vs in _heads()]
        for g, r in enumerate(rows):
            outs = []
            for h, (ks, vs) in enumerate(_heads()):
                st[h] = st[h] * ds[g][:, ks] + kv[g][h]
                st_ref[g, vs, :] = st[h]
                outs.append(_dot(q[r, ks], st[h], _NT))
            o_ref[r, :] = jnp.concatenate(outs, axis=-1)
        for h, (_, vs) in enumerate(_heads()):
            state[vs, :] = st[h]

    return _pcall(
        body, name="gla_fwd", grid=(S // T,),
        in_specs=[pl.BlockSpec((T, GLA_DK), lambda c: (c, 0)), pl.BlockSpec((T, GLA_DK), lambda c: (c, 1)),
                  pl.BlockSpec((T, GLA_DV), lambda c: (c, 1)),
                  pl.BlockSpec((T, FLOW_W), lambda c: (c, (INT_W - FLOW_W) // FLOW_W)),
                  pl.BlockSpec(wfu.shape, lambda c: (0, 0)), pl.BlockSpec(bf.shape, lambda c: (0, 0))],
        out_specs=[pl.BlockSpec((T, GLA_DV), lambda c: (c, 0)),
                   pl.BlockSpec((G, GLA_DV, GLA_HDK), lambda c: (c, 0, 0))],
        out_shape=[_sds((S, GLA_DV)), _sds((nc, GLA_DV, GLA_HDK))],
        scratch=[pltpu.VMEM((GLA_DV, GLA_HDK), F32)])(proj, proj, proj, proj, wfu, bf)


def _gla_bwd(proj, wfu, bf, states, do, dmain):
    S = proj.shape[0]
    G = min(GLA_G, S // CHUNK)
    T = G * CHUNK
    nt = S // T

    def body(q_ref, k_ref, v_ref, flow_ref, wfu_ref, bf_ref, st_ref, stp_ref, do_ref, dmain_in,
             dqkv_ref, dflow_ref, dwfu_ref, dbf_ref, dstate):
        dq_ref = dqkv_ref.at[:, 0:GLA_DK]
        dk_ref = dqkv_ref.at[:, GLA_DK:2 * GLA_DK]
        dv_ref = dqkv_ref.at[:, 2 * GLA_DK:]
        step = pl.program_id(0)

        @pl.when(step == 0)
        def _():
            dstate[...] = jnp.zeros_like(dstate)
            dwfu_ref[...] = jnp.zeros_like(dwfu_ref)
            dbf_ref[...] = jnp.zeros_like(dbf_ref)

        flow, wfu_v = flow_ref[...], wfu_ref[...]
        f, w, kt, ds = _gla_tile_common(k_ref[...], flow, wfu_v, bf_ref[...])
        q = q_ref[...].astype(F32) * (GLA_HDK ** -0.5)
        v, dout = v_ref[...], do_ref[...]
        rows = [slice(g * CHUNK, (g + 1) * CHUNK) for g in range(G)]
        dq = [jnp.concatenate([_dot(dout[r, vs], st_ref[g, vs, :]) for _, vs in _heads()], axis=-1)
              for g, r in enumerate(rows)]
        qdo = [[_dot(dout[r, vs], q[r, ks], _TN) for ks, vs in _heads()] for r in rows]
        dq_ref[...] = (jnp.concatenate(dq, axis=0) * (GLA_HDK ** -0.5)).astype(dq_ref.dtype)
        has_prev = (step < nt - 1).astype(F32)
        carry = [dstate[vs, :] for _, vs in _heads()]
        dkt, dv, dd = [None] * G, [None] * G, [None] * G
        for g in reversed(range(G)):
            r = rows[g]
            dkts, dvs, dds = [], [], []
            for h, (ks, vs) in enumerate(_heads()):
                dst = carry[h] + qdo[g][h]
                dkts.append(_dot(v[r, vs], dst))
                dvs.append(_dot(kt[r, ks], dst, _NT))
                st_prev = st_ref[g - 1, vs, :] if g > 0 else stp_ref[vs, :] * has_prev
                dds.append(jnp.sum(dst * st_prev, axis=0, keepdims=True))
                carry[h] = dst * ds[g][:, ks]
            dkt[g], dv[g], dd[g] = (jnp.concatenate(t, axis=-1) for t in (dkts, dvs, dds))
        for h, (_, vs) in enumerate(_heads()):
            dstate[vs, :] = carry[h]
        dkt = jnp.concatenate(dkt, axis=0)
        dv_ref[...] = jnp.concatenate(dv, axis=0).astype(dv_ref.dtype)
        dk_ref[...] = (dkt * w).astype(dk_ref.dtype)
        de = dkt * kt
        tri = _tri_matrix(False)
        dla = []
        for g, r in enumerate(rows):
            db_end = jnp.sum(de[r], axis=0, keepdims=True) + dd[g] * ds[g]
            dla.append(db_end - _tri_dot(tri, de[r]))
        df = jnp.concatenate(dla, axis=0) * (1.0 - _sigmoid(f)) * (1.0 / GATE_TEMP)
        dflow_ref[...] = _dot(df, wfu_v, _NT).astype(dflow_ref.dtype)
        dwfu_ref[...] += _dot(flow, df, _TN)
        dbf_ref[...] += jnp.sum(df, axis=0, keepdims=True)

    rt = lambda s: nt - 1 - s
    return _pcall(
        body, name="gla_bwd", grid=(nt,),
        in_specs=[pl.BlockSpec((T, GLA_DK), lambda s: (rt(s), 0)), pl.BlockSpec((T, GLA_DK), lambda s: (rt(s), 1)),
                  pl.BlockSpec((T, GLA_DV), lambda s: (rt(s), 1)),
                  pl.BlockSpec((T, FLOW_W), lambda s: (rt(s), (INT_W - FLOW_W) // FLOW_W)),
                  pl.BlockSpec(wfu.shape, lambda s: (0, 0)), pl.BlockSpec(bf.shape, lambda s: (0, 0)),
                  pl.BlockSpec((G, GLA_DV, GLA_HDK), lambda s: (rt(s), 0, 0)),
                  pl.BlockSpec((None, GLA_DV, GLA_HDK), lambda s: (jnp.maximum(rt(s) * G - 1, 0), 0, 0)),
                  pl.BlockSpec((T, GLA_DV), lambda s: (rt(s), 0)), pl.BlockSpec(memory_space=pl.ANY)],
        out_specs=[pl.BlockSpec((T, 2 * GLA_DK + GLA_DV), lambda s: (rt(s), 0)),
                   pl.BlockSpec((T, FLOW_W), lambda s: (rt(s), 0)),
                   pl.BlockSpec(wfu.shape, lambda s: (0, 0)), pl.BlockSpec(bf.shape, lambda s: (0, 0))],
        out_shape=[_sds(dmain.shape, dmain.dtype), _sds((S, FLOW_W), BF16), _sds(wfu.shape), _sds(bf.shape)],
        scratch=[pltpu.VMEM((GLA_DV, GLA_HDK), F32)], aliases={9: 0},
    )(proj, proj, proj, proj, wfu, bf, states, states, do, dmain)


def _pool_counts(tm, i):
    t = (lax.broadcasted_iota(jnp.int32, (tm, POOL_GD), 0) + i * tm + 1).astype(F32)
    return [jnp.minimum(t, float(w)) for w in POOL_WINDOWS]


def _pool_fwd(proj, w_pool, pool_scale, tm=512):
    S = proj.shape[0]
    tm = min(tm, S // 2)
    col = _POOL_COL
    hb = tm // POOL_HALO

    def body(p_ref, halo_ref, wp_ref, sc_ref, mixed_ref, out_ref):
        i = pl.program_id(0)
        p = p_ref[...].astype(F32)
        halo = halo_ref[...].astype(F32) * (i > 0).astype(F32)
        ext = jnp.concatenate([halo, p], axis=0)
        n = tm + POOL_HALO
        sums, acc, k = {}, ext, 1
        while k < POOL_WINDOWS[-1]:
            acc = acc + pltpu.roll(acc, k, axis=0)
            k *= 2
            sums[k] = acc
        cnts = _pool_counts(tm, i)
        mixed, lin = [], []
        for g, w in enumerate(POOL_WINDOWS):
            ls = slice(g * POOL_GD, (g + 1) * POOL_GD)
            m = sums[w][POOL_HALO:n, ls] / cnts[g] - p[:, ls]
            mixed.append(m)
            lin.append(_dot(m, wp_ref[g]))
        mixed_ref[...] = jnp.concatenate(mixed, axis=-1)
        out_ref[...] = (jnp.concatenate(lin, axis=-1) * sc_ref[...]).astype(out_ref.dtype)

    return _pcall(
        body, name="pool_fwd", grid=(S // tm,),
        in_specs=[pl.BlockSpec((tm, POOL_W), lambda i: (i, col)),
                  pl.BlockSpec((POOL_HALO, POOL_W), lambda i: (jnp.maximum(i * hb - 1, 0), col)),
                  pl.BlockSpec(w_pool.shape, lambda i: (0, 0, 0)), pl.BlockSpec(pool_scale.shape, lambda i: (0, 0))],
        out_specs=[pl.BlockSpec((tm, POOL_W), lambda i: (i, 0)), pl.BlockSpec((tm, POOL_W), lambda i: (i, 0))],
        out_shape=[_sds((S, POOL_W)), _sds((S, POOL_W), BF16)])(proj, proj, w_pool, pool_scale)


def _pool_lin_bwd(dout, mixed, w_pool, pool_scale):
    S = dout.shape[0]

    def fn(dout, mixed, wp, sc):
        dlin = dout * sc
        dm, dwp, lin = [], [], []
        for g in range(len(POOL_WINDOWS)):
            ls = slice(g * POOL_GD, (g + 1) * POOL_GD)
            lin.append(_dot(mixed[:, ls], wp[g]))
            dm.append(_dot(dlin[:, ls], wp[g], _NT))
            dwp.append(_dot(mixed[:, ls], dlin[:, ls], _TN))
        dsc = jnp.sum(dout * jnp.concatenate(lin, axis=-1), axis=0, keepdims=True)
        return jnp.concatenate(dm, axis=-1), jnp.concatenate(dwp, axis=0), dsc

    return _rowwise("pool_lin_bwd", fn, [dout, mixed], [w_pool, pool_scale], [_sds((S, POOL_W))],
                    [_sds((len(POOL_WINDOWS) * POOL_GD, POOL_GD)), _sds((1, POOL_W))])


def _pool_win_bwd(dmixed, dmain, tm=512):
    S = dmixed.shape[0]
    tm = min(tm, S // 2)
    nt = S // tm
    hb = tm // POOL_HALO

    def body(dm_ref, halo_ref, dmain_in, dp_ref):
        i = pl.program_id(0)
        dm = dm_ref[...]
        halo = halo_ref[...] * (i < nt - 1).astype(F32)
        cnts = _pool_counts(tm, i)
        cnts_h = [c[:POOL_HALO] for c in _pool_counts(tm, i + 1)]
        r = jnp.concatenate([jnp.concatenate([dm[:, g * POOL_GD:(g + 1) * POOL_GD] / cnts[g] for g in range(4)], axis=-1),
                             jnp.concatenate([halo[:, g * POOL_GD:(g + 1) * POOL_GD] / cnts_h[g] for g in range(4)], axis=-1)],
                            axis=0)
        n = tm + POOL_HALO
        sums, acc, k = {}, r, 1
        while k < POOL_WINDOWS[-1]:
            acc = acc + pltpu.roll(acc, n - k, axis=0)
            k *= 2
            sums[k] = acc
        dp = [sums[w][:tm, g * POOL_GD:(g + 1) * POOL_GD] for g, w in enumerate(POOL_WINDOWS)]
        dp_ref[...] = (jnp.concatenate(dp, axis=-1) - dm).astype(dp_ref.dtype)

    return _pcall(
        body, name="pool_win_bwd", grid=(nt,),
        in_specs=[pl.BlockSpec((tm, POOL_W), lambda i: (i, 0)),
                  pl.BlockSpec((POOL_HALO, POOL_W), lambda i: (jnp.minimum((i + 1) * hb, S // POOL_HALO - 1), 0)),
                  pl.BlockSpec(memory_space=pl.ANY)],
        out_specs=pl.BlockSpec((tm, POOL_W), lambda i: (i, _POOL_COL)),
        out_shape=_sds(dmain.shape, dmain.dtype), aliases={2: 0})(dmixed, dmixed, dmain)


def _xattn_probs(q, kv, h):
    hs = slice(h * XA_HD, (h + 1) * XA_HD)
    s = _dot(q[:, hs], kv[:, hs], _NT) * (XA_HD ** -0.5)
    s = s - jnp.max(s, axis=-1, keepdims=True)
    e = jnp.exp(s)
    return e / jnp.sum(e, axis=-1, keepdims=True)


def _xattn_fwd(proj, kv):
    S = proj.shape[0]

    def fn(q, kv):
        outs = []
        for h in range(XA_HEADS):
            p = _xattn_probs(q, kv, h)
            outs.append(_dot(p, kv[:, XA_W + h * XA_HD:XA_W + (h + 1) * XA_HD]))
        return jnp.concatenate(outs, axis=-1)

    return _rowwise("xattn_fwd", fn, [_Win(proj, XA_W, _XQ_COL)], [kv], [_sds((S, XA_W), BF16)])[0]


def _xattn_bwd(proj, kv, dxa, dmain):
    def fn(q, dxa, kv):
        dqs, dks, dvs = [], [], []
        for h in range(XA_HEADS):
            hs = slice(h * XA_HD, (h + 1) * XA_HD)
            vh = kv[:, XA_W + h * XA_HD:XA_W + (h + 1) * XA_HD]
            p = _xattn_probs(q, kv, h)
            dp = _dot(dxa[:, hs], vh, _NT)
            ds = p * (dp - jnp.sum(p * dp, axis=-1, keepdims=True)) * (XA_HD ** -0.5)
            dqs.append(_dot(ds, kv[:, hs]))
            dks.append(_dot(ds, q[:, hs], _TN))
            dvs.append(_dot(p, dxa[:, hs], _TN))
        return jnp.concatenate(dqs, axis=-1), jnp.concatenate(dks + dvs, axis=-1)

    return _rowwise("xattn_bwd", fn, [_Win(proj, XA_W, _XQ_COL), dxa], [kv], [_Win(dmain, XA_W, _XQ_COL)],
                    [_sds(kv.shape)])


def _mix_fwd(x1, h, mem, W, g_next, on_proj=None):
    S = x1.shape[0]
    M = mem.shape[0]
    tm = min(512, S)
    def proj_body(h_ref, w_ref, o_ref):
        hh = h_ref[...]
        for j in range(INT_NBLK):
            o_ref[:, INT_BLK * j:INT_BLK * (j + 1)] = _dot(hh, w_ref[INT_BLK * j:INT_BLK * (j + 1), :], _NT).astype(BF16)

    proj = _pcall(
        proj_body, name="mix_proj", grid=(S // tm,),
        in_specs=[pl.BlockSpec((tm, D_MODEL), lambda i: (i, 0)),
                  pl.BlockSpec((INT_W, D_MODEL), lambda i: (0, 0), pipeline_mode=pl.Buffered(1))],
        out_specs=pl.BlockSpec((tm, INT_W), lambda i: (i, 0)), out_shape=_sds((S, INT_W), BF16))(h, W["w_int_t"])
    if on_proj is not None:
        on_proj(proj)
    o_raw, states = _gla_fwd(proj, W["w_fu_pad"], W["b_f"])
    ya_in = _rowwise("gla_out", _head_rms_gate, [o_raw, _Win(proj, GLA_DV, 2)], [W["gla_norm_g"]],
                     [_sds((S, GLA_DV), BF16)])[0]
    mixed, pool_out = _pool_fwd(proj, W["w_pool"], W["pool_scale"])
    mem_n = _rowwise("mem_norm", lambda m, g: _rms(m, g), [mem], [W["mem_norm_g"]], [_sds((M, D_MODEL), BF16)])[0]
    kv = _mm_nn("mem_kv", mem_n, W["w_mem_kv"])
    xa = _xattn_fwd(proj, kv)
    def out_fn(gates, ya_in, pool_out, xa, x1, wa, wb, wc, wo, g, gn):
        ya, yb, yc = (_dot(t, w).astype(BF16) for t, w in ((ya_in, wa), (pool_out, wb), (xa, wc)))
        merged = _merge(gates, ya, yb, yc).astype(BF16)
        y = _dot(merged, wo)
        x2 = x1 + _rms(y, g)
        return ya, yb, yc, merged, y, x2, _rms(x2, gn)

    ya, yb, yc, merged, y, x2, h_next = _rowwise(
        "mix_out", out_fn, [_Win(proj, 3 * D_MODEL, 1), ya_in, pool_out, xa, x1],
        [W["w_up_gla"], W["w_up_pool"], W["w_up_xattn"], W["w_o"], W["mix_post_g"], g_next],
        [_sds((S, D_MODEL), BF16)] * 4 + [_sds((S, D_MODEL)), _sds((S, D_MODEL)), _sds((S, D_MODEL), BF16)], tm=256)
    return x2, h_next, (h, proj, o_raw, states, ya_in, mixed, pool_out, mem_n, kv, xa, ya, yb, yc, merged, y)


def _mix_bwd(dx2, x1, mem, W, saved, on_grads=None):
    h, proj, o_raw, states, ya_in, mixed, pool_out, mem_n, kv, xa, ya, yb, yc, merged, y = saved
    S = x1.shape[0]

    def out_bwd(y, dx2, gates, ya, yb, yc, g, wo, wa, wb, wc):
        dy, dg = _rms_bwd(y, g, dx2)
        dy = dy.astype(BF16)
        _, vjp = jax.vjp(_merge, gates, ya, yb, yc)
        dgates, dya, dyb, dyc = vjp(_dot(dy, wo, _NT))
        return (dy, dgates, dya, dyb, dyc, _dot(dya, wa, _NT), _dot(dyb, wb, _NT), _dot(dyc, wc, _NT), dg)

    dmain = lax.empty((S, INT_MAIN), BF16)
    dy, dmain, dya, dyb, dyc, d_ya_in, d_pool_out, d_xa, d_mix_post_g = _rowwise(
        "mix_out_bwd", out_bwd, [y, dx2, _Win(proj, 3 * D_MODEL, 1), ya, yb, yc],
        [W["mix_post_g"], W["w_o"], W["w_up_gla"], W["w_up_pool"], W["w_up_xattn"]],
        [_sds((S, D_MODEL), BF16), _Win(dmain, 3 * D_MODEL, 1)] + [_sds((S, D_MODEL), BF16)] * 4
        + [_sds((S, POOL_W), BF16), _sds((S, XA_W), BF16)], [_sds((1, D_MODEL))], tm=256)
    d_w_o = _mm_tn("d_w_o", merged, dy)
    d_w_up_gla = _mm_tn("d_w_up_gla", ya_in, dya)
    d_w_up_pool = _mm_tn("d_w_up_pool", pool_out, dyb)
    d_w_up_xattn = _mm_tn("d_w_up_xattn", xa, dyc)

    def gla_out_bwd(o, g_out, d, gn):
        _, vjp = jax.vjp(_head_rms_gate, o, g_out, gn)
        return vjp(d.astype(F32))

    do_raw, dmain, d_gla_norm_g = _rowwise("gla_out_bwd", gla_out_bwd, [o_raw, _Win(proj, GLA_DV, 2), d_ya_in],
                                           [W["gla_norm_g"]], [_sds((S, GLA_DV)), _Win(dmain, GLA_DV, 2)],
                                           [_sds((1, GLA_DV))])
    dmain, dflow, d_wfu_pad, d_b_f = _gla_bwd(proj, W["w_fu_pad"], W["b_f"], states, do_raw, dmain)
    dmixed, d_w_pool, d_pool_scale = _pool_lin_bwd(d_pool_out, mixed, W["w_pool"], W["pool_scale"])
    dmain = _pool_win_bwd(dmixed, dmain)
    dmain, dkv = _xattn_bwd(proj, kv, d_xa, dmain)
    d_w_mem_kv = _mm_tn("d_w_mem_kv", mem_n, dkv)
    dmem_n = _mm_nt("d_mem_n", dkv, W["w_mem_kv"])
    d_mem_norm_g = _rowwise("mem_norm_bwd", lambda m, d, g: _rms_bwd(m, g, d)[1], [mem, dmem_n], [W["mem_norm_g"]], [],
                            [_sds((1, D_MODEL))])[0]
    d_w_int_t = (_mm_tn("d_w_int", dmain, h, ts=1024, tn_a=INT_MAIN // 4), _mm_tn("d_w_flow", dflow, h, ts=2048))
    grads = dict(
        w_int_t=d_w_int_t, w_fu_pad=d_wfu_pad, b_f=d_b_f, gla_norm_g=d_gla_norm_g, w_pool=d_w_pool,
        pool_scale=d_pool_scale, mem_norm_g=d_mem_norm_g, w_mem_kv=d_w_mem_kv, w_up_gla=d_w_up_gla,
        w_up_pool=d_w_up_pool, w_up_xattn=d_w_up_xattn, w_o=d_w_o, mix_post_g=d_mix_post_g)
    deps = [on_grads(grads)] if on_grads is not None else []
    dx1, grads["mix_pre_g"] = _dh_pre_norm_bwd("d_mix_h", [dmain, dflow], W["w_int_t"], x1, dx2, W["mix_pre_g"], deps)
    return dx1, grads


def _mesh_pos():
    x, y, c = lax.axis_index("x"), lax.axis_index("y"), lax.axis_index("c")
    return x, y, c, 4 * x + 2 * y + c


def _peer(x, y, c, r):
    px = 1 - x if r & 4 else x
    py = 1 - y if r & 2 else y
    pc = 1 - c if r & 1 else c
    return (px, py, pc), 4 * px + 2 * py + pc


_ALL_PEERS = tuple(range(1, N_DEV))
_SIBLING = 1
_SAME_CORE = (2, 4, 6)


def _dev_slot(ref, dev):
    return ref.at[dev]


def _zero_pad_rows(land):
    pad = FF_PAD - FF_BLK

    def body(land_in, o_ref):
        o_ref[...] = jnp.zeros_like(o_ref)

    return _pcall(body, name="zero_pad_rows", grid=(land.shape[0],), in_specs=[pl.BlockSpec(memory_space=pl.ANY)],
                  out_specs=pl.BlockSpec((None, pad, D_MODEL), lambda j: (j, FF_BLK // pad, 0)),
                  out_shape=_sds(land.shape, land.dtype), aliases={0: 0})(land)


class _Plan:
    def __init__(self, scatter, slots=None, shapes=None):
        self.scatter, self.slots, self.shapes = scatter, slots or {}, shapes or {}

    def src(self, srcs, a, dev):
        return self.slots.get(a, _dev_slot)(srcs[a], dev) if self.scatter else srcs[a]

    def dst(self, lands, a, dev):
        return lands[a].at[dev] if self.scatter else self.slots.get(a, _dev_slot)(lands[a], dev)

    def landing_zones(self, arrays):
        lands = []
        for a, arr in enumerate(arrays):
            if self.scatter:
                lands.append(lax.empty((N_DEV,) + tuple(self.shapes.get(a, arr.shape[1:])), arr.dtype))
            elif a in self.shapes:
                lands.append(_zero_pad_rows(lax.empty(self.shapes[a], arr.dtype)))
            else:
                lands.append(lax.empty((N_DEV,) + arr.shape, arr.dtype))
        return lands


_GATHER, _SCATTER = _Plan(False), _Plan(True)


def _peer_copies(srcs, lands, send_sems, recv_sems, plan, peers=_ALL_PEERS):
    x, y, c, me = _mesh_pos()
    cps = []
    for r in peers:
        pos, peer = _peer(x, y, c, r)
        for a in range(len(srcs)):
            k = a * (N_DEV - 1) + r - 1
            cps.append(pltpu.make_async_remote_copy(
                src_ref=plan.src(srcs, a, peer), dst_ref=plan.dst(lands, a, me),
                send_sem=send_sems.at[k], recv_sem=recv_sems.at[k], device_id=pos,
                device_id_type=pl.DeviceIdType.MESH))
    return cps


_HBM = pl.BlockSpec(memory_space=pltpu.HBM)
_SEM = pl.BlockSpec(memory_space=pltpu.SEMAPHORE)
_EFFECT = pltpu.SideEffectType.DATAFLOW_SIDE_EFFECTING


def _own_copies(srcs, lands, own_sems, plan):
    me = _mesh_pos()[3]
    return [pltpu.make_async_copy(plan.src(srcs, a, me), plan.dst(lands, a, me), own_sems.at[a])
            for a in range(len(srcs))]


def _exchange_start(name, arrays, plan):
    n = len(arrays)
    lands = plan.landing_zones(arrays)
    n_sem = n * (N_DEV - 1)

    def body(*refs):
        srcs, lands_ = refs[:n], refs[n:2 * n]
        send_sems, recv_sems, own_sems = refs[2 * n:2 * n + 3]
        token = refs[-1]
        for cp in _peer_copies(srcs, lands_, send_sems, recv_sems, plan) + _own_copies(srcs, lands_, own_sems, plan):
            cp.start()
        token[...] = jnp.zeros_like(token)

    hbm = lambda a: pltpu.HBM(a.shape, a.dtype)
    res = pl.pallas_call(
        body, name=name,
        out_shape=(pltpu.SemaphoreType.DMA((n_sem,)), pltpu.SemaphoreType.DMA((n_sem,)), pltpu.SemaphoreType.DMA((n,)),
                   *[hbm(a) for a in arrays], *[hbm(a) for a in lands], _sds((8, 128))),
        in_specs=[_HBM] * (2 * n),
        out_specs=(_SEM, _SEM, _SEM, *[_HBM] * (2 * n), pl.BlockSpec(memory_space=pltpu.VMEM)),
        input_output_aliases={i: 3 + i for i in range(2 * n)},
        compiler_params=pltpu.CompilerParams(has_side_effects=_EFFECT),
    )(*[pltpu.with_memory_space_constraint(a, pltpu.HBM) for a in list(arrays) + lands])
    return (res[:3], res[3:3 + n], res[3 + n:3 + 2 * n], plan), res[-1]


def _exchange_wait(name, state, after):
    sems, srcs, lands, plan = state
    n = len(srcs)

    def body(*refs):
        srcs_, lands_ = refs[:n], refs[n:2 * n]
        send_sems, recv_sems, own_sems = refs[2 * n:2 * n + 3]
        for cp in _peer_copies(srcs_, lands_, send_sems, recv_sems, plan):
            cp.wait_send()
            cp.wait_recv()
        for cp in _own_copies(srcs_, lands_, own_sems, plan):
            cp.wait()

    hbm = lambda a: pltpu.HBM(a.shape, a.dtype)
    res = pl.pallas_call(
        body, name=name, out_shape=tuple(hbm(a) for a in list(srcs) + list(lands)),
        in_specs=[_HBM] * (2 * n) + [_SEM] * 3 + [pl.BlockSpec(memory_space=pl.ANY)] * len(after),
        out_specs=tuple([_HBM] * (2 * n)), input_output_aliases={i: i for i in range(2 * n)},
        compiler_params=pltpu.CompilerParams(has_side_effects=_EFFECT),
    )(*srcs, *lands, *sems, *after)
    return res[n:]


def _gather_start(name, arrays, plan, after):
    n = len(arrays)
    lands = plan.landing_zones(arrays)
    n_sem = n * (N_DEV - 1)

    def body(*refs):
        srcs, lands_ = refs[:n], refs[n:2 * n]
        send_sems, recv_sems, own_sems = refs[2 * n + len(after):2 * n + len(after) + 3]
        token = refs[-1]
        for cp in (_peer_copies(srcs, lands_, send_sems, recv_sems, plan, (_SIBLING,) + _SAME_CORE)
                   + _own_copies(srcs, lands_, own_sems, plan)):
            cp.start()
        token[...] = jnp.zeros_like(token)

    hbm = lambda a: pltpu.HBM(a.shape, a.dtype)
    res = pl.pallas_call(
        body, name=name,
        out_shape=(pltpu.SemaphoreType.DMA((n_sem,)), pltpu.SemaphoreType.DMA((n_sem,)), pltpu.SemaphoreType.DMA((n,)),
                   *[hbm(a) for a in arrays], *[hbm(a) for a in lands], _sds((8, 128))),
        in_specs=[_HBM] * (2 * n) + [pl.BlockSpec(memory_space=pl.ANY)] * len(after),
        out_specs=(_SEM, _SEM, _SEM, *[_HBM] * (2 * n), pl.BlockSpec(memory_space=pltpu.VMEM)),
        input_output_aliases={i: 3 + i for i in range(2 * n)},
        compiler_params=pltpu.CompilerParams(has_side_effects=_EFFECT),
    )(*[pltpu.with_memory_space_constraint(a, pltpu.HBM) for a in list(arrays) + lands], *after)
    return (res[:3], res[3:3 + n], res[3 + n:3 + 2 * n], plan), res[-1]


def _pass_on_copies(lands, send_sems, recv_sems, plan):
    x, y, c, _ = _mesh_pos()
    sibling = _peer(x, y, c, _SIBLING)[0]
    cps = []
    for i, r in enumerate(_SAME_CORE):
        owner = _peer(x, y, c, r)[1]
        for a in range(len(lands)):
            k = a * len(_SAME_CORE) + i
            cps.append(pltpu.make_async_remote_copy(
                src_ref=plan.dst(lands, a, owner), dst_ref=plan.dst(lands, a, owner), send_sem=send_sems.at[k],
                recv_sem=recv_sems.at[k], device_id=sibling, device_id_type=pl.DeviceIdType.MESH))
    return cps


def _gather_pass_on(name, state, after):
    sems, srcs, lands, plan = state
    n = len(srcs)
    n_sem = n * len(_SAME_CORE)

    def body(*refs):
        srcs_, lands_ = refs[:n], refs[n:2 * n]
        send_sems, recv_sems = refs[2 * n], refs[2 * n + 1]
        on_send, on_recv = refs[2 * n + 3 + len(after)], refs[2 * n + 4 + len(after)]
        arrivals = _peer_copies(srcs_, lands_, send_sems, recv_sems, plan, _SAME_CORE)
        for arrived, on in zip(arrivals, _pass_on_copies(lands_, on_send, on_recv, plan)):
            arrived.wait_recv()
            on.start()

    hbm = lambda a: pltpu.HBM(a.shape, a.dtype)
    res = pl.pallas_call(
        body, name=name,
        out_shape=(pltpu.SemaphoreType.DMA((n_sem,)), pltpu.SemaphoreType.DMA((n_sem,)),
                   *[hbm(a) for a in list(srcs) + list(lands)]),
        in_specs=[_HBM] * (2 * n) + [_SEM] * 3 + [pl.BlockSpec(memory_space=pl.ANY)] * len(after),
        out_specs=(_SEM, _SEM, *[_HBM] * (2 * n)), input_output_aliases={i: 2 + i for i in range(2 * n)},
        compiler_params=pltpu.CompilerParams(has_side_effects=_EFFECT),
    )(*srcs, *lands, *sems, *after)
    return sems, res[:2], res[2:2 + n], res[2 + n:], plan


def _gather_wait(name, state, after):
    sems, on_sems, srcs, lands, plan = state
    n = len(srcs)

    def body(*refs):
        srcs_, lands_ = refs[:n], refs[n:2 * n]
        send_sems, recv_sems, own_sems, on_send, on_recv = refs[2 * n:2 * n + 5]
        for cp in _peer_copies(srcs_, lands_, send_sems, recv_sems, plan, (_SIBLING,)):
            cp.wait_recv()
        for cp in _peer_copies(srcs_, lands_, send_sems, recv_sems, plan, (_SIBLING,) + _SAME_CORE):
            cp.wait_send()
        for cp in _own_copies(srcs_, lands_, own_sems, plan):
            cp.wait()
        for cp in _pass_on_copies(lands_, on_send, on_recv, plan):
            cp.wait_send()
            cp.wait_recv()

    hbm = lambda a: pltpu.HBM(a.shape, a.dtype)
    res = pl.pallas_call(
        body, name=name, out_shape=tuple(hbm(a) for a in list(srcs) + list(lands)),
        in_specs=[_HBM] * (2 * n) + [_SEM] * 5 + [pl.BlockSpec(memory_space=pl.ANY)] * len(after),
        out_specs=tuple([_HBM] * (2 * n)), input_output_aliases={i: i for i in range(2 * n)},
        compiler_params=pltpu.CompilerParams(has_side_effects=_EFFECT),
    )(*srcs, *lands, *sems, *on_sems, *after)
    return res[n:]


def _sum_parts(name, recv):
    def body(r_ref, o_ref):
        s = r_ref[0]
        for j in range(1, N_DEV):
            s = s + r_ref[j]
        o_ref[...] = s

    return pl.pallas_call(body, name=name, out_shape=_sds(recv.shape[1:], recv.dtype))(recv)


def _adam(name, recv, w, m, v):
    shape = w.shape
    R, C = shape
    tr = R
    while N_DEV * tr * C * 4 > 6 * 1024 * 1024 and tr % 32 == 0:
        tr //= 2

    def body(recv_ref, w_ref, m_ref, v_ref, g_ref, d_ref, m2_ref, v2_ref):
        g = recv_ref[0].astype(F32)
        for j in range(1, N_DEV):
            g = g + recv_ref[j].astype(F32)
        w_, m_, v_ = w_ref[...], m_ref[...], v_ref[...]
        m2 = ADAM_B1 * m_ + (1.0 - ADAM_B1) * g
        v2 = ADAM_B2 * v_ + (1.0 - ADAM_B2) * (g * g)
        m_hat = m2 / (1.0 - ADAM_B1 ** ADAM_STEP)
        v_hat = v2 / (1.0 - ADAM_B2 ** ADAM_STEP)
        g_ref[...] = g
        d_ref[...] = -ADAM_LR * (m_hat / (jnp.sqrt(v_hat) + ADAM_EPS) + ADAM_WD * w_)
        m2_ref[...] = m2
        v2_ref[...] = v2

    blk = pl.BlockSpec((tr, C), lambda i: (i, 0))
    return _pcall(body, name=name, grid=(R // tr,),
                  in_specs=[pl.BlockSpec((N_DEV, tr, C), lambda i: (0, i, 0)), blk, blk, blk],
                  out_specs=[blk] * 4, out_shape=[_sds(shape)] * 4)(recv, w, m, v)


_NAMES = ['ffn1_pre_g', 'ffn1_w_in', 'ffn1_w_out', 'ffn1_post_g', 'mix_pre_g', 'w_in', 'w_fu', 'b_f', 'gla_norm_g',
          'w_pool', 'pool_scale', 'mem_norm_g', 'w_mem_kv', 'w_up_gla', 'w_up_pool', 'w_up_xattn', 'w_o', 'mix_post_g',
          'ffn2_pre_g', 'ffn2_w_in', 'ffn2_w_out', 'ffn2_post_g', 'final_g']
_SHARDED = ['ffn1_w_in', 'ffn1_w_out', 'w_in', 'w_fu', 'w_mem_kv', 'w_up_gla', 'w_up_pool', 'w_up_xattn', 'w_o',
            'ffn2_w_in', 'ffn2_w_out']
_COL_SHARDED = ['w_up_pool', 'w_up_xattn']


def _cols_to_full(g):
    return jnp.transpose(g, (1, 0, 2)).reshape(g.shape[1], N_DEV * g.shape[2])


def _full_to_cols(f):
    R, C = f.shape
    return jnp.transpose(f.reshape(R, N_DEV, C // N_DEV), (1, 0, 2))


def _to_internal(w_in_t):
    o = 0
    parts = []
    for s in IN_SPLITS:
        parts.append(w_in_t[o:o + s])
        o += s
    q, k, v, g_out, f_low, p_in, xq, gates = parts
    f_low = jnp.pad(f_low, ((0, FLOW_W - GATE_RANK), (0, 0)))
    return jnp.concatenate([q, k, v, g_out, gates, p_in, xq, f_low], axis=0)


def _from_internal(d, d_flow):
    q, k, v, g_out = d[0:512], d[512:1024], d[1024:2048], d[2048:3072]
    gates, p_in, xq = d[3072:6144], d[6144:6656], d[6656:7168]
    return jnp.concatenate([q, k, v, g_out, d_flow[:GATE_RANK], p_in, xq, gates], axis=0)


_FFN_IN = ('ffn1_w_in', 'ffn2_w_in')
_FFN_OUT = ('ffn1_w_out', 'ffn2_w_out')
_GATHERS = {"ffn1_in": ['ffn1_w_in'], "ffn1_out": ['ffn1_w_out'],
            "mix": ['w_in', 'w_fu', 'w_mem_kv', 'w_up_gla', 'w_up_pool', 'w_up_xattn', 'w_o'],
            "ffn2": ['ffn2_w_in', 'ffn2_w_out']}


def _ffn_in_slot(ref, d):
    return ref.at[2 * (d % N_FF_BLK) + d // N_FF_BLK, pl.ds(0, FF_BLK)]


def _ffn_out_slot(ref, d):
    rows = FF_BLK // 2
    return ref.at[d // 2, pl.ds(pl.multiple_of((d % 2) * rows, rows), rows)]


def _ffn_plan(names, scatter):
    slots, shapes = {}, {}
    for a, n in enumerate(names):
        if n in _FFN_IN:
            slots[a] = _ffn_in_slot
            shapes[a] = (FF_BLK, D_MODEL) if scatter else (N_DEV, FF_PAD, D_MODEL)
        elif n in _FFN_OUT:
            slots[a] = _ffn_out_slot
            shapes[a] = (FF_BLK // 2, D_MODEL) if scatter else (N_FF_BLK, FF_PAD, D_MODEL)
    return _Plan(scatter, slots, shapes)


def _rows_to_full(g):
    return g.reshape(N_DEV * g.shape[1], g.shape[2])


def _mix_weights(gathered):
    W = {n: _rows_to_full(gathered[n]) for n in ('w_mem_kv', 'w_up_gla', 'w_o')}
    for n in _COL_SHARDED:
        W[n] = _cols_to_full(gathered[n])
    W["w_int_t"] = _to_internal(gathered["w_in"].reshape(IN_WIDTH, D_MODEL))
    W["w_fu_pad"] = jnp.pad(_cols_to_full(gathered["w_fu"]), ((0, FLOW_W - GATE_RANK), (0, 0)))
    return W


def _mix_chunks(G, n):
    if n == "w_in":
        return _from_internal(*G["w_int_t"]).reshape(N_DEV, IN_SHARD, D_MODEL)
    if n == "w_fu":
        return _full_to_cols(G["w_fu_pad"][:GATE_RANK].astype(BF16))
    if n in _COL_SHARDED:
        return _full_to_cols(G[n])
    return G[n].reshape(N_DEV, G[n].shape[0] // N_DEV, G[n].shape[1])


def _step(x, mem, tgt, P, Mo, Vo):
    def native(n, a):
        return jnp.swapaxes(a, 0, 1) if n in _FFN_IN + ("w_in",) else a

    P, Mo, Vo = ({n: native(n, a) for n, a in d.items()} for d in (P, Mo, Vo))
    small = {n: P[n] for n in _NAMES if n not in _SHARDED}

    gather, tokens = {}, []
    for grp, names in _GATHERS.items():
        gather[grp], tok = _gather_start("gather_" + grp, [P[n].astype(BF16) for n in names], _ffn_plan(names, False),
                                         tokens[-1:])
        tokens.append(tok)

    def pass_on(grp, after):
        return _gather_pass_on("gather_" + grp + "_on", gather[grp], after)

    def gathered(grp, after):
        return _gather_wait("gather_" + grp + "_wait", pass_on(grp, after), [])

    def ffn_wo(lands):
        return lambda act: lands[0].reshape(N_FF_BLK * FF_PAD, D_MODEL)

    h1 = _pre_norm("ffn1_pre", x, small["ffn1_pre_g"], tokens[:1])
    w1t = gathered("ffn1_in", tokens[-1:] + [h1])[0].reshape(N_DEV * FF_PAD, D_MODEL)
    (x1, hm), w1o, sv1 = _ffn_fwd("ffn1", x, h1, w1t, small["ffn1_post_g"],
                                  lambda act: ffn_wo(gathered("ffn1_out", [act]))(act), small["mix_pre_g"])

    gm = dict(zip(_GATHERS["mix"], gathered("mix", [x1])))
    Wm = {**small, **_mix_weights(gm)}
    passed = {}
    x2, h2, svm = _mix_fwd(x1, hm, mem, Wm, small["ffn2_pre_g"],
                           lambda proj: passed.update(ffn2=pass_on("ffn2", [proj])))

    w2t, w2o = _gather_wait("gather_ffn2_wait", passed["ffn2"], [x2])
    w2t = w2t.reshape(N_DEV * FF_PAD, D_MODEL)
    (loss, dx3, d_final_g), w2o, sv2 = _ffn_fwd("ffn2", x2, h2, w2t, small["ffn2_post_g"], ffn_wo([w2o]),
                                                small["final_g"], tgt)

    G = dict(final_g=d_final_g)
    scat = {}

    def start(grp, names, arrays):
        scat[grp] = names, _exchange_start("scatter_" + grp, arrays, _ffn_plan(names, True))
        return scat[grp][1][1]

    def ffn_starts(tag):
        return (lambda dwo: start(tag + "_out", [tag + "_w_out"], [dwo.reshape(N_FF_BLK, FF_PAD, D_MODEL)]),
                lambda dwt: start(tag + "_in", [tag + "_w_in"], [dwt.reshape(N_DEV, FF_PAD, D_MODEL)]))

    dx2, G["ffn2_pre_g"], G["ffn2_post_g"] = _ffn_bwd(
        "ffn2", dx3, x2, small["ffn2_pre_g"], w2t, w2o, small["ffn2_post_g"], sv2, *ffn_starts("ffn2"))
    dx1, Gm = _mix_bwd(dx2, x1, mem, Wm, svm,
                       lambda Gm: start("mix", _GATHERS["mix"], [_mix_chunks(Gm, n) for n in _GATHERS["mix"]]))
    G.update(Gm)
    dx, G["ffn1_pre_g"], G["ffn1_post_g"] = _ffn_bwd(
        "ffn1", dx1, x, small["ffn1_pre_g"], w1t, w1o, small["ffn1_post_g"], sv1, *ffn_starts("ffn1"))

    scat["small"] = list(small) + ["loss"], _exchange_start("gather_small_grads", [G[n] for n in small] + [loss],
                                                            _GATHER)
    recv, outs = {}, {}
    done = [dx]
    for grp in ["ffn2_out", "ffn2_in", "mix", "ffn1_out", "ffn1_in", "small"]:
        names, (state, _) = scat[grp]
        recv.update(zip(names, _exchange_wait(("gather_" if grp == "small" else "scatter_") + grp + "_wait", state,
                                              done)))
        for n in names:
            if n == "loss":
                loss = _sum_parts("loss_sum", recv[n])[0, 0]
                continue
            shp = P[n].shape
            shp2 = shp if len(shp) == 2 else (shp[0] * shp[1], shp[2])
            res = _adam("adam_" + n, recv[n].reshape((N_DEV,) + shp2), P[n].reshape(shp2), Mo[n].reshape(shp2),
                        Vo[n].reshape(shp2))
            outs[n] = [native(n, t.reshape(shp)) for t in res]
            done.append(res[-1])
    return loss, dx, outs


def kernel(x, mem, ffn1_pre_g, ffn1_w_in, ffn1_w_out, ffn1_post_g, mix_pre_g, w_in, w_fu, b_f, gla_norm_g, w_pool, pool_scale, mem_norm_g, w_mem_kv, w_up_gla, w_up_pool, w_up_xattn, w_o, mix_post_g, ffn2_pre_g, ffn2_w_in, ffn2_w_out, ffn2_post_g, final_g, loss_target, m_ffn1_pre_g, m_ffn1_w_in, m_ffn1_w_out, m_ffn1_post_g, m_mix_pre_g, m_w_in, m_w_fu, m_b_f, m_gla_norm_g, m_w_pool, m_pool_scale, m_mem_norm_g, m_w_mem_kv, m_w_up_gla, m_w_up_pool, m_w_up_xattn, m_w_o, m_mix_post_g, m_ffn2_pre_g, m_ffn2_w_in, m_ffn2_w_out, m_ffn2_post_g, m_final_g, v_ffn1_pre_g, v_ffn1_w_in, v_ffn1_w_out, v_ffn1_post_g, v_mix_pre_g, v_w_in, v_w_fu, v_b_f, v_gla_norm_g, v_w_pool, v_pool_scale, v_mem_norm_g, v_w_mem_kv, v_w_up_gla, v_w_up_pool, v_w_up_xattn, v_w_o, v_mix_post_g, v_ffn2_pre_g, v_ffn2_w_in, v_ffn2_w_out, v_ffn2_post_g, v_final_g):
    params = [ffn1_pre_g, ffn1_w_in, ffn1_w_out, ffn1_post_g, mix_pre_g, w_in, w_fu, b_f, gla_norm_g, w_pool, pool_scale, mem_norm_g, w_mem_kv, w_up_gla, w_up_pool, w_up_xattn, w_o, mix_post_g, ffn2_pre_g, ffn2_w_in, ffn2_w_out, ffn2_post_g, final_g]
    moms = [m_ffn1_pre_g, m_ffn1_w_in, m_ffn1_w_out, m_ffn1_post_g, m_mix_pre_g, m_w_in, m_w_fu, m_b_f, m_gla_norm_g, m_w_pool, m_pool_scale, m_mem_norm_g, m_w_mem_kv, m_w_up_gla, m_w_up_pool, m_w_up_xattn, m_w_o, m_mix_post_g, m_ffn2_pre_g, m_ffn2_w_in, m_ffn2_w_out, m_ffn2_post_g, m_final_g]
    vars_ = [v_ffn1_pre_g, v_ffn1_w_in, v_ffn1_w_out, v_ffn1_post_g, v_mix_pre_g, v_w_in, v_w_fu, v_b_f, v_gla_norm_g, v_w_pool, v_pool_scale, v_mem_norm_g, v_w_mem_kv, v_w_up_gla, v_w_up_pool, v_w_up_xattn, v_w_o, v_mix_post_g, v_ffn2_pre_g, v_ffn2_w_in, v_ffn2_w_out, v_ffn2_post_g, v_final_g]
    P = {n: a[0] if a.ndim > 2 else a for n, a in zip(_NAMES, params)}
    Mo = {n: a[0] if a.ndim > 2 else a for n, a in zip(_NAMES, moms)}
    Vo = {n: a[0] if a.ndim > 2 else a for n, a in zip(_NAMES, vars_)}
    loss, dx, outs = _step(x[0], mem[0], loss_target[0], P, Mo, Vo)
    out = [loss, dx[None]]
    for kind in range(4):
        for n, p in zip(_NAMES, params):
            out.append(outs[n][kind].reshape(p.shape))
    return tuple(out)
```

```python
import functools

import jax
import jax.numpy as jnp
from jax import lax
from jax.experimental import pallas as pl
from jax.experimental.pallas import tpu as pltpu

F32 = jnp.float32
BF16 = jnp.bfloat16

N_DEV = 8
D_MODEL = 1024
D_FF = 2816
FF_BLK = 2 * D_FF // N_DEV
N_FF_BLK = D_FF // FF_BLK
FF_PAD = 768
CHUNK = 64
GLA_HEADS = 4
GLA_DK = 512
GLA_DV = 1024
GLA_HDK = 128
GLA_HDV = 256
GATE_RANK = 16
GATE_TEMP = 16.0
POOL_WINDOWS = (2, 4, 8, 16)
POOL_W = 512
POOL_GD = 128
POOL_HALO = 16
XA_HEADS = 4
XA_HD = 128
XA_W = 512
EPS = 1e-6
IN_SPLITS = (GLA_DK, GLA_DK, GLA_DV, GLA_DV, GATE_RANK, POOL_W, XA_W, 3 * D_MODEL)
IN_WIDTH = sum(IN_SPLITS)
IN_SHARD = IN_WIDTH // N_DEV
INT_W = 3072 + 3072 + 1024 + 128
INT_NBLK = 3
INT_BLK = INT_W // INT_NBLK
FLOW_W = 128
INT_MAIN = INT_W - FLOW_W
_POOL_COL = (3072 + 3072) // POOL_W
_XQ_COL = (3072 + 3072 + POOL_W) // XA_W

ADAM_LR = 0.001
ADAM_B1 = 0.9
ADAM_B2 = 0.999
ADAM_EPS = 1e-08
ADAM_WD = 0.01
ADAM_STEP = 10

VMEM_LIMIT = 56 * 1024 * 1024

_NN = (((1,), (0,)), ((), ()))
_NT = (((1,), (1,)), ((), ()))
_TN = (((0,), (0,)), ((), ()))


def _pcall(body, *, name, grid, in_specs, out_specs, out_shape, scratch=(), aliases=None):
    return pl.pallas_call(
        body, name=name, grid=grid, in_specs=in_specs, out_specs=out_specs, out_shape=out_shape,
        scratch_shapes=list(scratch), input_output_aliases=aliases or {},
        compiler_params=pltpu.CompilerParams(dimension_semantics=("arbitrary",) * len(grid),
                                             vmem_limit_bytes=VMEM_LIMIT))


def _dot(a, b, dims=_NN):
    return lax.dot_general(a.astype(BF16), b.astype(BF16), dims, preferred_element_type=F32)


def _mm(name, a, b, *, grid, a_blk, a_map, b_blk, b_map, o_shape, o_blk, o_map, dims, out_dtype=F32, deps=()):
    nk = grid[2]

    def body(a_ref, b_ref, *rest):
        o_ref, scr = rest[len(deps)], rest[len(deps) + 1:]
        p = _dot(a_ref[...], b_ref[...], dims)
        if nk == 1:
            o_ref[...] = p.astype(o_ref.dtype)
        else:
            acc = scr[0]
            k = pl.program_id(2)

            @pl.when(k == 0)
            def _():
                acc[...] = p

            @pl.when(k > 0)
            def _():
                acc[...] += p

            @pl.when(k == nk - 1)
            def _():
                o_ref[...] = acc[...].astype(o_ref.dtype)

    acc_shape = tuple(d for d in o_blk if d is not None)
    return _pcall(body, name=name, grid=grid,
                  in_specs=[pl.BlockSpec(a_blk, a_map), pl.BlockSpec(b_blk, b_map)]
                  + [pl.BlockSpec(memory_space=pl.ANY)] * len(deps),
                  out_specs=pl.BlockSpec(o_blk, o_map),
                  out_shape=jax.ShapeDtypeStruct(o_shape, out_dtype),
                  scratch=[pltpu.VMEM(acc_shape, F32)] if nk > 1 else [])(a, b, *deps)


def _mm_nn(name, a, b, out_dtype=F32, tm=1024, tn=None):
    M, K = a.shape
    N = b.shape[1]
    tm, tn = min(tm, M), (tn or N)
    return _mm(name, a, b, grid=(N // tn, M // tm, 1), a_blk=(tm, K), a_map=lambda j, i, k: (i, 0),
               b_blk=(K, tn), b_map=lambda j, i, k: (0, j), o_shape=(M, N), o_blk=(tm, tn),
               o_map=lambda j, i, k: (i, j), dims=_NN, out_dtype=out_dtype)


def _mm_nt(name, a, b, out_dtype=F32, tm=1024):
    M, K = a.shape
    N = b.shape[0]
    tm = min(tm, M)
    return _mm(name, a, b, grid=(1, M // tm, 1), a_blk=(tm, K), a_map=lambda j, i, k: (i, 0),
               b_blk=(N, K), b_map=lambda j, i, k: (0, 0), o_shape=(M, N), o_blk=(tm, N),
               o_map=lambda j, i, k: (i, 0), dims=_NT, out_dtype=out_dtype)


def _mm_tn(name, a, b, out_dtype=BF16, ts=512, tn=None, tn_a=None, deps=()):
    S, M = a.shape
    N = b.shape[1]
    ts, tn, tn_a = min(ts, S), (tn or N), (tn_a or M)
    return _mm(name, a, b, grid=((N // tn) * (M // tn_a), 1, S // ts), a_blk=(ts, tn_a),
               a_map=lambda j, i, k: (k, j if tn_a < M else 0), b_blk=(ts, tn),
               b_map=lambda j, i, k: (k, j if tn < N else 0), o_shape=(M, N), o_blk=(tn_a, tn),
               o_map=lambda j, i, k: (j, 0) if tn_a < M else (0, j), dims=_TN, out_dtype=out_dtype, deps=deps)


class _Win:
    def __init__(self, arr, w, c):
        self.arr, self.w, self.c = arr, w, c


def _row_spec(x, tm):
    if isinstance(x, _Win):
        return x.arr, pl.BlockSpec((tm, x.w), functools.partial(lambda i, c: (i, c), c=x.c))
    if x.ndim == 3:
        return x, pl.BlockSpec((x.shape[0], tm, x.shape[2]), lambda i: (0, i, 0))
    return x, pl.BlockSpec((tm, x.shape[1]), lambda i: (i, 0))


def _rowwise(name, fn, rows, consts, outs, accs=(), tm=512):
    first = rows[0].arr if isinstance(rows[0], _Win) else rows[0]
    S = first.shape[1] if first.ndim == 3 else first.shape[0]
    tm = min(tm, S)
    n_in, n_out = len(rows) + len(consts), len(outs)
    arrays, in_specs = [], []
    for r in rows:
        arr, spec = _row_spec(r, tm)
        arrays.append(arr)
        in_specs.append(spec)
    for c in consts:
        arrays.append(c)
        in_specs.append(pl.BlockSpec(c.shape, functools.partial(lambda i, n: (0,) * n, n=c.ndim)))
    aliases = {}
    for k, o in enumerate(outs):
        if isinstance(o, _Win):
            aliases[len(arrays)] = k
            arrays.append(o.arr)
            in_specs.append(pl.BlockSpec(memory_space=pl.ANY))
    n_thru = len(aliases)
    out_specs = [_row_spec(o, tm)[1] for o in outs]
    out_specs += [pl.BlockSpec(a.shape, functools.partial(lambda i, n: (0,) * n, n=len(a.shape))) for a in accs]
    out_shape = [_sds(o.arr.shape, o.arr.dtype) if isinstance(o, _Win) else _sds(o.shape, o.dtype) for o in outs]

    def body(*refs):
        res = fn(*[r[...] for r in refs[:n_in]])
        if not isinstance(res, (tuple, list)):
            res = (res,)
        orefs = refs[n_in + n_thru:]
        for r, v in zip(orefs[:n_out], res[:n_out]):
            r[...] = v.astype(r.dtype)
        i = pl.program_id(0)
        for r, v in zip(orefs[n_out:], res[n_out:]):
            @pl.when(i == 0)
            def _(r=r, v=v):
                r[...] = v.astype(r.dtype)

            @pl.when(i > 0)
            def _(r=r, v=v):
                r[...] += v.astype(r.dtype)

    return _pcall(body, name=name, grid=(S // tm,), in_specs=in_specs, out_specs=out_specs,
                  out_shape=out_shape + [_sds(a.shape, a.dtype) for a in accs], aliases=aliases)(*arrays)


def _sds(shape, dtype=F32):
    return jax.ShapeDtypeStruct(shape, dtype)


def _rms(x, g):
    return x * lax.rsqrt(jnp.mean(x * x, axis=-1, keepdims=True) + EPS) * g


def _rms_bwd(x, g, dy):
    _, vjp = jax.vjp(_rms, x, g)
    return vjp(dy)


def _sigmoid(x):
    return 0.5 * jnp.tanh(0.5 * x) + 0.5


def _silu(x):
    return x * _sigmoid(x)


def _log_sigmoid(f):
    return jnp.minimum(f, 0.0) - jnp.log(1.0 + jnp.exp(-jnp.abs(f)))


def _head_rms_gate(o, g_out, gn):
    parts = [_rms(o[:, h * GLA_HDV:(h + 1) * GLA_HDV], gn[:, h * GLA_HDV:(h + 1) * GLA_HDV]) for h in range(GLA_HEADS)]
    return jnp.concatenate(parts, axis=-1) * _silu(g_out.astype(F32))


def _merge(gates, ya, yb, yc):
    gates, ya, yb, yc = (t.astype(F32) for t in (gates, ya, yb, yc))
    return (_sigmoid(gates[:, :D_MODEL]) * ya + _sigmoid(gates[:, D_MODEL:2 * D_MODEL]) * yb
            + _sigmoid(gates[:, 2 * D_MODEL:]) * yc)


def _tri_dot(t, x):
    hi = x.astype(BF16)
    r1 = x - hi.astype(F32)
    mid = r1.astype(BF16)
    lo = (r1 - mid.astype(F32)).astype(BF16)
    d = functools.partial(lax.dot_general, dimension_numbers=_NN, preferred_element_type=F32)
    return d(t, hi) + d(t, mid) + d(t, lo)


def _pre_norm(name, x, g, after):
    S = x.shape[0]
    tm = min(512, S)

    def body(x_ref, g_ref, *rest):
        rest[-1][...] = _rms(x_ref[...], g_ref[...]).astype(BF16)

    row = pl.BlockSpec((tm, D_MODEL), lambda i: (i, 0))
    return _pcall(body, name=name, grid=(S // tm,),
                  in_specs=[row, pl.BlockSpec((1, D_MODEL), lambda i: (0, 0))] + [pl.BlockSpec(memory_space=pl.ANY)] * len(after),
                  out_specs=row, out_shape=_sds(x.shape, BF16))(x, g, *after)


def _ffn_fwd(tag, x, h, wt, gpost, get_wo, g_next, tgt=None):
    S = x.shape[0]
    tm = min(512, S)

    def up_body(h_ref, w_ref, u_ref, act_ref):
        hh = h_ref[...]
        for j in range(N_FF_BLK):
            ab = _dot(hh, w_ref[2 * FF_PAD * j:2 * FF_PAD * (j + 1), :], _NT)
            u_ref[:, 2 * FF_PAD * j:2 * FF_PAD * (j + 1)] = ab.astype(BF16)
            act_ref[:, FF_PAD * j:FF_PAD * (j + 1)] = (_silu(ab[:, :FF_PAD]) * ab[:, FF_PAD:]).astype(BF16)

    u, act = _pcall(
        up_body, name=tag + "_up", grid=(S // tm,),
        in_specs=[pl.BlockSpec((tm, D_MODEL), lambda i: (i, 0)),
                  pl.BlockSpec(wt.shape, lambda i: (0, 0), pipeline_mode=pl.Buffered(1))],
        out_specs=[pl.BlockSpec((tm, N_DEV * FF_PAD), lambda i: (i, 0)),
                   pl.BlockSpec((tm, N_FF_BLK * FF_PAD), lambda i: (i, 0))],
        out_shape=[_sds((S, N_DEV * FF_PAD), BF16), _sds((S, N_FF_BLK * FF_PAD), BF16)],
    )(h, wt)

    def down_body(act_ref, w_ref, x_ref, g_ref, gn_ref, f_ref, xo_ref, hn_ref):
        f = _dot(act_ref[...], w_ref[...])
        f_ref[...] = f.astype(BF16)
        xo = x_ref[...] + 0.5 * _rms(f, g_ref[...])
        xo_ref[...] = xo
        hn_ref[...] = _rms(xo, gn_ref[...]).astype(BF16)

    def down_head_body(act_ref, w_ref, x_ref, g_ref, gn_ref, t_ref, f_ref, dxo_ref, loss_ref, dgn_ref):
        i = pl.program_id(0)
        f = _dot(act_ref[...], w_ref[...])
        f_ref[...] = f.astype(BF16)
        xo = x_ref[...] + 0.5 * _rms(f, g_ref[...])
        out, vjp = jax.vjp(_rms, xo, gn_ref[...])
        e = out - t_ref[...]
        loss = 0.5 * jnp.sum(jnp.mean(e * e, axis=-1, keepdims=True), axis=0, keepdims=True)
        dxo_ref[...], dgn = vjp(e * (1.0 / D_MODEL))

        @pl.when(i == 0)
        def _():
            loss_ref[...] = jnp.broadcast_to(loss, loss_ref.shape)
            dgn_ref[...] = dgn

        @pl.when(i > 0)
        def _():
            loss_ref[...] += jnp.broadcast_to(loss, loss_ref.shape)
            dgn_ref[...] += dgn

    wo = get_wo(act)
    row = pl.BlockSpec((tm, D_MODEL), lambda i: (i, 0))
    vec = pl.BlockSpec((1, D_MODEL), lambda i: (0, 0))
    if tgt is not None:
        f, dxo, loss, dg_next = _pcall(
            down_head_body, name=tag + "_down", grid=(S // tm,),
            in_specs=[pl.BlockSpec((tm, N_FF_BLK * FF_PAD), lambda i: (i, 0)),
                      pl.BlockSpec(wo.shape, lambda i: (0, 0), pipeline_mode=pl.Buffered(1)), row, vec, vec, row],
            out_specs=[row, row, pl.BlockSpec((1, 128), lambda i: (0, 0)), vec],
            out_shape=[_sds((S, D_MODEL), BF16), _sds((S, D_MODEL)), _sds((1, 128)), _sds((1, D_MODEL))],
        )(act, wo, x, gpost, g_next, tgt)
        return (loss, dxo, dg_next), wo, (h, u, act, f)
    f, xo, h_next = _pcall(
        down_body, name=tag + "_down", grid=(S // tm,),
        in_specs=[pl.BlockSpec((tm, N_FF_BLK * FF_PAD), lambda i: (i, 0)),
                  pl.BlockSpec(wo.shape, lambda i: (0, 0), pipeline_mode=pl.Buffered(1)), row, vec, vec],
        out_specs=[row, row, row], out_shape=[_sds((S, D_MODEL), BF16), _sds((S, D_MODEL)), _sds((S, D_MODEL), BF16)],
    )(act, wo, x, gpost, g_next)
    return (xo, h_next), wo, (h, u, act, f)


def _ffn_bwd(tag, dxo, x, gpre, wt, wo, gpost, saved, on_dwo=None, on_dwt=None):
    h, u, act, f = saved
    S = x.shape[0]
    tm = min(256, S)

    def dact_body(f_ref, dxo_ref, g_ref, w_ref, u_ref, df_ref, du_ref, dg_ref):
        i = pl.program_id(0)
        df, dg = _rms_bwd(f_ref[...].astype(F32), g_ref[...], 0.5 * dxo_ref[...])
        df = df.astype(BF16)
        df_ref[...] = df

        @pl.when(i == 0)
        def _():
            dg_ref[...] = dg

        @pl.when(i > 0)
        def _():
            dg_ref[...] += dg

        for j in range(N_FF_BLK):
            dact = _dot(df, w_ref[FF_PAD * j:FF_PAD * (j + 1), :], _NT)
            ab = u_ref[:, 2 * FF_PAD * j:2 * FF_PAD * (j + 1)].astype(F32)
            a, b = ab[:, :FF_PAD], ab[:, FF_PAD:]
            sg = _sigmoid(a)
            du_ref[:, 2 * FF_PAD * j:2 * FF_PAD * j + FF_PAD] = (dact * b * (sg * (1.0 + a * (1.0 - sg)))).astype(BF16)
            du_ref[:, 2 * FF_PAD * j + FF_PAD:2 * FF_PAD * (j + 1)] = (dact * (a * sg)).astype(BF16)

    row = pl.BlockSpec((tm, D_MODEL), lambda i: (i, 0))
    vec = pl.BlockSpec((1, D_MODEL), lambda i: (0, 0))
    u_spec = pl.BlockSpec((tm, N_DEV * FF_PAD), lambda i: (i, 0))
    df, du, dgpost = _pcall(
        dact_body, name=tag + "_dact", grid=(S // tm,),
        in_specs=[row, row, vec, pl.BlockSpec(wo.shape, lambda i: (0, 0), pipeline_mode=pl.Buffered(1)), u_spec],
        out_specs=[row, u_spec, vec],
        out_shape=[_sds((S, D_MODEL), BF16), _sds((S, N_DEV * FF_PAD), BF16), _sds((1, D_MODEL))])(f, dxo, gpost, wo, u)
    dwo = _mm_tn(tag + "_dwo", act, df, ts=2048, tn_a=FF_PAD)
    dwt = _mm_tn(tag + "_dwt", du, h, ts=2048, tn_a=FF_PAD, deps=[on_dwo(dwo)] if on_dwo is not None else [])
    dx, dgpre = _dh_pre_norm_bwd(tag + "_dh", [du], wt, x, dxo, gpre, [on_dwt(dwt)] if on_dwt is not None else [])
    return dx, dgpre, dgpost


def _dh_pre_norm_bwd(name, dzs, wt, x, dres, g, deps):
    S = x.shape[0]
    ks = [dz.shape[1] for dz in dzs]

    def body(*refs):
        dz_refs = refs[:len(dzs)]
        w_ref, x_ref, dres_ref, g_ref = refs[len(dzs):len(dzs) + 4]
        dx_ref, dg_ref = refs[len(dzs) + 4 + len(deps):]
        i = pl.program_id(0)
        dh, o = None, 0
        for dz_ref, k in zip(dz_refs, ks):
            p = _dot(dz_ref[...], w_ref[o:o + k, :])
            dh, o = (p if dh is None else dh + p), o + k
        dx, dg = _rms_bwd(x_ref[...], g_ref[...], dh)
        dx_ref[...] = dx + dres_ref[...]

        @pl.when(i == 0)
        def _():
            dg_ref[...] = dg

        @pl.when(i > 0)
        def _():
            dg_ref[...] += dg

    th = min(512, S)
    row = pl.BlockSpec((th, D_MODEL), lambda i: (i, 0))
    vec = pl.BlockSpec((1, D_MODEL), lambda i: (0, 0))
    return _pcall(
        body, name=name, grid=(S // th,),
        in_specs=[pl.BlockSpec((th, k), lambda i: (i, 0)) for k in ks]
        + [pl.BlockSpec(wt.shape, lambda i: (0, 0), pipeline_mode=pl.Buffered(1)), row, row, vec]
        + [pl.BlockSpec(memory_space=pl.ANY)] * len(deps),
        out_specs=[row, vec], out_shape=[_sds((S, D_MODEL)), _sds((1, D_MODEL))])(*dzs, wt, x, dres, g, *deps)


GLA_G = 8


def _gla_tile_common(k, flow, wfu, bf):
    f = _dot(flow, wfu) + bf
    la = _log_sigmoid(f) * (1.0 / GATE_TEMP)
    tri = _tri_matrix(True)
    ws, ds = [], []
    for g in range(k.shape[0] // CHUNK):
        b = _tri_dot(tri, la[g * CHUNK:(g + 1) * CHUNK])
        b_end = b[CHUNK - 1:CHUNK, :]
        ws.append(jnp.exp(b_end - b))
        ds.append(jnp.exp(b_end))
    w = jnp.concatenate(ws, axis=0)
    return f, w, k * w, ds


def _tri_matrix(lower):
    r = lax.broadcasted_iota(jnp.int32, (CHUNK, CHUNK), 0)
    c = lax.broadcasted_iota(jnp.int32, (CHUNK, CHUNK), 1)
    return jnp.where((r >= c) if lower else (r <= c), 1.0, 0.0).astype(BF16)


def _heads():
    return [(slice(h * GLA_HDK, (h + 1) * GLA_HDK), slice(h * GLA_HDV, (h + 1) * GLA_HDV)) for h in range(GLA_HEADS)]


def _gla_fwd(proj, wfu, bf, deps=()):
    S = proj.shape[0]
    G = min(GLA_G, S // CHUNK)
    T = G * CHUNK
    nc = S // CHUNK

    def body(q_ref, k_ref, v_ref, flow_ref, wfu_ref, bf_ref, *rest):
        o_ref, st_ref, state = rest[len(deps):]
        @pl.when(pl.program_id(0) == 0)
        def _():
            state[...] = jnp.zeros_like(state)

        _, _, kt, ds = _gla_tile_common(k_ref[...], flow_ref[...], wfu_ref[...], bf_ref[...])
        q = q_ref[...].astype(F32) * (GLA_HDK ** -0.5)
        v = v_ref[...]
        rows = [slice(g * CHUNK, (g + 1) * CHUNK) for g in range(G)]
        kv = [[_dot(v[r, vs], kt[r, ks], _TN) for ks, vs in _heads()] for r in rows]
        st = [state[vs, :] for _, vs in _heads()]
        for g, r in enumerate(rows):
            outs = []
            for h, (ks, vs) in enumerate(_heads()):
                st[h] = st[h] * ds[g][:, ks] + kv[g][h]
                st_ref[g, vs, :] = st[h]
                outs.append(_dot(q[r, ks], st[h], _NT))
            o_ref[r, :] = jnp.concatenate(outs, axis=-1)
        for h, (_, vs) in enumerate(_heads()):
            state[vs, :] = st[h]

    return _pcall(
        body, name="gla_fwd", grid=(S // T,),
        in_specs=[pl.BlockSpec((T, GLA_DK), lambda c: (c, 0)), pl.BlockSpec((T, GLA_DK), lambda c: (c, 1)),
                  pl.BlockSpec((T, GLA_DV), lambda c: (c, 1)),
                  pl.BlockSpec((T, FLOW_W), lambda c: (c, (INT_W - FLOW_W) // FLOW_W)),
                  pl.BlockSpec(wfu.shape, lambda c: (0, 0)), pl.BlockSpec(bf.shape, lambda c: (0, 0))]
        + [pl.BlockSpec(memory_space=pl.ANY)] * len(deps),
        out_specs=[pl.BlockSpec((T, GLA_DV), lambda c: (c, 0)),
                   pl.BlockSpec((G, GLA_DV, GLA_HDK), lambda c: (c, 0, 0))],
        out_shape=[_sds((S, GLA_DV)), _sds((nc, GLA_DV, GLA_HDK))],
        scratch=[pltpu.VMEM((GLA_DV, GLA_HDK), F32)])(proj, proj, proj, proj, wfu, bf, *deps)


def _gla_bwd(proj, wfu, bf, states, do, dmain):
    S = proj.shape[0]
    G = min(GLA_G, S // CHUNK)
    T = G * CHUNK
    nt = S // T

    def body(q_ref, k_ref, v_ref, flow_ref, wfu_ref, bf_ref, st_ref, stp_ref, do_ref, dmain_in,
             dqkv_ref, dflow_ref, dwfu_ref, dbf_ref, dstate):
        dq_ref = dqkv_ref.at[:, 0:GLA_DK]
        dk_ref = dqkv_ref.at[:, GLA_DK:2 * GLA_DK]
        dv_ref = dqkv_ref.at[:, 2 * GLA_DK:]
        step = pl.program_id(0)

        @pl.when(step == 0)
        def _():
            dstate[...] = jnp.zeros_like(dstate)
            dwfu_ref[...] = jnp.zeros_like(dwfu_ref)
            dbf_ref[...] = jnp.zeros_like(dbf_ref)

        flow, wfu_v = flow_ref[...], wfu_ref[...]
        f, w, kt, ds = _gla_tile_common(k_ref[...], flow, wfu_v, bf_ref[...])
        q = q_ref[...].astype(F32) * (GLA_HDK ** -0.5)
        v, dout = v_ref[...], do_ref[...]
        rows = [slice(g * CHUNK, (g + 1) * CHUNK) for g in range(G)]
        dq = [jnp.concatenate([_dot(dout[r, vs], st_ref[g, vs, :]) for _, vs in _heads()], axis=-1)
              for g, r in enumerate(rows)]
        qdo = [[_dot(dout[r, vs], q[r, ks], _TN) for ks, vs in _heads()] for r in rows]
        dq_ref[...] = (jnp.concatenate(dq, axis=0) * (GLA_HDK ** -0.5)).astype(dq_ref.dtype)
        has_prev = (step < nt - 1).astype(F32)
        carry = [dstate[vs, :] for _, vs in _heads()]
        dkt, dv, dd = [None] * G, [None] * G, [None] * G
        for g in reversed(range(G)):
            r = rows[g]
            dkts, dvs, dds = [], [], []
            for h, (ks, vs) in enumerate(_heads()):
                dst = carry[h] + qdo[g][h]
                dkts.append(_dot(v[r, vs], dst))
                dvs.append(_dot(kt[r, ks], dst, _NT))
                st_prev = st_ref[g - 1, vs, :] if g > 0 else stp_ref[vs, :] * has_prev
                dds.append(jnp.sum(dst * st_prev, axis=0, keepdims=True))
                carry[h] = dst * ds[g][:, ks]
            dkt[g], dv[g], dd[g] = (jnp.concatenate(t, axis=-1) for t in (dkts, dvs, dds))
        for h, (_, vs) in enumerate(_heads()):
            dstate[vs, :] = carry[h]
        dkt = jnp.concatenate(dkt, axis=0)
        dv_ref[...] = jnp.concatenate(dv, axis=0).astype(dv_ref.dtype)
        dk_ref[...] = (dkt * w).astype(dk_ref.dtype)
        de = dkt * kt
        tri = _tri_matrix(False)
        dla = []
        for g, r in enumerate(rows):
            db_end = jnp.sum(de[r], axis=0, keepdims=True) + dd[g] * ds[g]
            dla.append(db_end - _tri_dot(tri, de[r]))
        df = jnp.concatenate(dla, axis=0) * (1.0 - _sigmoid(f)) * (1.0 / GATE_TEMP)
        dflow_ref[...] = _dot(df, wfu_v, _NT).astype(dflow_ref.dtype)
        dwfu_ref[...] += _dot(flow, df, _TN)
        dbf_ref[...] += jnp.sum(df, axis=0, keepdims=True)

    rt = lambda s: nt - 1 - s
    return _pcall(
        body, name="gla_bwd", grid=(nt,),
        in_specs=[pl.BlockSpec((T, GLA_DK), lambda s: (rt(s), 0)), pl.BlockSpec((T, GLA_DK), lambda s: (rt(s), 1)),
                  pl.BlockSpec((T, GLA_DV), lambda s: (rt(s), 1)),
                  pl.BlockSpec((T, FLOW_W), lambda s: (rt(s), (INT_W - FLOW_W) // FLOW_W)),
                  pl.BlockSpec(wfu.shape, lambda s: (0, 0)), pl.BlockSpec(bf.shape, lambda s: (0, 0)),
                  pl.BlockSpec((G, GLA_DV, GLA_HDK), lambda s: (rt(s), 0, 0)),
                  pl.BlockSpec((None, GLA_DV, GLA_HDK), lambda s: (jnp.maximum(rt(s) * G - 1, 0), 0, 0)),
                  pl.BlockSpec((T, GLA_DV), lambda s: (rt(s), 0)), pl.BlockSpec(memory_space=pl.ANY)],
        out_specs=[pl.BlockSpec((T, 2 * GLA_DK + GLA_DV), lambda s: (rt(s), 0)),
                   pl.BlockSpec((T, FLOW_W), lambda s: (rt(s), 0)),
                   pl.BlockSpec(wfu.shape, lambda s: (0, 0)), pl.BlockSpec(bf.shape, lambda s: (0, 0))],
        out_shape=[_sds(dmain.shape, dmain.dtype), _sds((S, FLOW_W), BF16), _sds(wfu.shape), _sds(bf.shape)],
        scratch=[pltpu.VMEM((GLA_DV, GLA_HDK), F32)], aliases={9: 0},
    )(proj, proj, proj, proj, wfu, bf, states, states, do, dmain)


def _pool_counts(tm, i):
    t = (lax.broadcasted_iota(jnp.int32, (tm, POOL_GD), 0) + i * tm + 1).astype(F32)
    return [jnp.minimum(t, float(w)) for w in POOL_WINDOWS]


def _pool_fwd(proj, w_pool, pool_scale, tm=512):
    S = proj.shape[0]
    tm = min(tm, S // 2)
    col = _POOL_COL
    hb = tm // POOL_HALO

    def body(p_ref, halo_ref, wp_ref, sc_ref, mixed_ref, out_ref):
        i = pl.program_id(0)
        p = p_ref[...].astype(F32)
        halo = halo_ref[...].astype(F32) * (i > 0).astype(F32)
        ext = jnp.concatenate([halo, p], axis=0)
        n = tm + POOL_HALO
        sums, acc, k = {}, ext, 1
        while k < POOL_WINDOWS[-1]:
            acc = acc + pltpu.roll(acc, k, axis=0)
            k *= 2
            sums[k] = acc
        cnts = _pool_counts(tm, i)
        mixed, lin = [], []
        for g, w in enumerate(POOL_WINDOWS):
            ls = slice(g * POOL_GD, (g + 1) * POOL_GD)
            m = sums[w][POOL_HALO:n, ls] / cnts[g] - p[:, ls]
            mixed.append(m)
            lin.append(_dot(m, wp_ref[g]))
        mixed_ref[...] = jnp.concatenate(mixed, axis=-1)
        out_ref[...] = (jnp.concatenate(lin, axis=-1) * sc_ref[...]).astype(out_ref.dtype)

    return _pcall(
        body, name="pool_fwd", grid=(S // tm,),
        in_specs=[pl.BlockSpec((tm, POOL_W), lambda i: (i, col)),
                  pl.BlockSpec((POOL_HALO, POOL_W), lambda i: (jnp.maximum(i * hb - 1, 0), col)),
                  pl.BlockSpec(w_pool.shape, lambda i: (0, 0, 0)), pl.BlockSpec(pool_scale.shape, lambda i: (0, 0))],
        out_specs=[pl.BlockSpec((tm, POOL_W), lambda i: (i, 0)), pl.BlockSpec((tm, POOL_W), lambda i: (i, 0))],
        out_shape=[_sds((S, POOL_W)), _sds((S, POOL_W), BF16)])(proj, proj, w_pool, pool_scale)


def _pool_lin_bwd(dout, mixed, w_pool, pool_scale):
    S = dout.shape[0]

    def fn(dout, mixed, wp, sc):
        dlin = dout * sc
        dm, dwp, lin = [], [], []
        for g in range(len(POOL_WINDOWS)):
            ls = slice(g * POOL_GD, (g + 1) * POOL_GD)
            lin.append(_dot(mixed[:, ls], wp[g]))
            dm.append(_dot(dlin[:, ls], wp[g], _NT))
            dwp.append(_dot(mixed[:, ls], dlin[:, ls], _TN))
        dsc = jnp.sum(dout * jnp.concatenate(lin, axis=-1), axis=0, keepdims=True)
        return jnp.concatenate(dm, axis=-1), jnp.concatenate(dwp, axis=0), dsc

    return _rowwise("pool_lin_bwd", fn, [dout, mixed], [w_pool, pool_scale], [_sds((S, POOL_W))],
                    [_sds((len(POOL_WINDOWS) * POOL_GD, POOL_GD)), _sds((1, POOL_W))])


def _pool_win_bwd(dmixed, dmain, tm=512):
    S = dmixed.shape[0]
    tm = min(tm, S // 2)
    nt = S // tm
    hb = tm // POOL_HALO

    def body(dm_ref, halo_ref, dmain_in, dp_ref):
        i = pl.program_id(0)
        dm = dm_ref[...]
        halo = halo_ref[...] * (i < nt - 1).astype(F32)
        cnts = _pool_counts(tm, i)
        cnts_h = [c[:POOL_HALO] for c in _pool_counts(tm, i + 1)]
        r = jnp.concatenate([jnp.concatenate([dm[:, g * POOL_GD:(g + 1) * POOL_GD] / cnts[g] for g in range(4)], axis=-1),
                             jnp.concatenate([halo[:, g * POOL_GD:(g + 1) * POOL_GD] / cnts_h[g] for g in range(4)], axis=-1)],
                            axis=0)
        n = tm + POOL_HALO
        sums, acc, k = {}, r, 1
        while k < POOL_WINDOWS[-1]:
            acc = acc + pltpu.roll(acc, n - k, axis=0)
            k *= 2
            sums[k] = acc
        dp = [sums[w][:tm, g * POOL_GD:(g + 1) * POOL_GD] for g, w in enumerate(POOL_WINDOWS)]
        dp_ref[...] = (jnp.concatenate(dp, axis=-1) - dm).astype(dp_ref.dtype)

    return _pcall(
        body, name="pool_win_bwd", grid=(nt,),
        in_specs=[pl.BlockSpec((tm, POOL_W), lambda i: (i, 0)),
                  pl.BlockSpec((POOL_HALO, POOL_W), lambda i: (jnp.minimum((i + 1) * hb, S // POOL_HALO - 1), 0)),
                  pl.BlockSpec(memory_space=pl.ANY)],
        out_specs=pl.BlockSpec((tm, POOL_W), lambda i: (i, _POOL_COL)),
        out_shape=_sds(dmain.shape, dmain.dtype), aliases={2: 0})(dmixed, dmixed, dmain)


def _xattn_probs(q, kv, h):
    hs = slice(h * XA_HD, (h + 1) * XA_HD)
    s = _dot(q[:, hs], kv[:, hs], _NT) * (XA_HD ** -0.5)
    s = s - jnp.max(s, axis=-1, keepdims=True)
    e = jnp.exp(s)
    return e / jnp.sum(e, axis=-1, keepdims=True)


def _xattn_fwd(proj, kv):
    S = proj.shape[0]

    def fn(q, kv):
        outs = []
        for h in range(XA_HEADS):
            p = _xattn_probs(q, kv, h)
            outs.append(_dot(p, kv[:, XA_W + h * XA_HD:XA_W + (h + 1) * XA_HD]))
        return jnp.concatenate(outs, axis=-1)

    return _rowwise("xattn_fwd", fn, [_Win(proj, XA_W, _XQ_COL)], [kv], [_sds((S, XA_W), BF16)])[0]


def _xattn_bwd(proj, kv, dxa, dmain):
    def fn(q, dxa, kv):
        dqs, dks, dvs = [], [], []
        for h in range(XA_HEADS):
            hs = slice(h * XA_HD, (h + 1) * XA_HD)
            vh = kv[:, XA_W + h * XA_HD:XA_W + (h + 1) * XA_HD]
            p = _xattn_probs(q, kv, h)
            dp = _dot(dxa[:, hs], vh, _NT)
            ds = p * (dp - jnp.sum(p * dp, axis=-1, keepdims=True)) * (XA_HD ** -0.5)
            dqs.append(_dot(ds, kv[:, hs]))
            dks.append(_dot(ds, q[:, hs], _TN))
            dvs.append(_dot(p, dxa[:, hs], _TN))
        return jnp.concatenate(dqs, axis=-1), jnp.concatenate(dks + dvs, axis=-1)

    return _rowwise("xattn_bwd", fn, [_Win(proj, XA_W, _XQ_COL), dxa], [kv], [_Win(dmain, XA_W, _XQ_COL)],
                    [_sds(kv.shape)])


def _mix_fwd(x1, h, mem, W, g_next, on_proj=None):
    S = x1.shape[0]
    M = mem.shape[0]
    tm = min(512, S)
    def proj_body(h_ref, w_ref, o_ref):
        hh = h_ref[...]
        for j in range(INT_NBLK):
            o_ref[:, INT_BLK * j:INT_BLK * (j + 1)] = _dot(hh, w_ref[INT_BLK * j:INT_BLK * (j + 1), :], _NT).astype(BF16)

    proj = _pcall(
        proj_body, name="mix_proj", grid=(S // tm,),
        in_specs=[pl.BlockSpec((tm, D_MODEL), lambda i: (i, 0)),
                  pl.BlockSpec((INT_W, D_MODEL), lambda i: (0, 0), pipeline_mode=pl.Buffered(1))],
        out_specs=pl.BlockSpec((tm, INT_W), lambda i: (i, 0)), out_shape=_sds((S, INT_W), BF16))(h, W["w_int_t"])
    o_raw, states = _gla_fwd(proj, W["w_fu_pad"], W["b_f"], on_proj(proj) if on_proj is not None else [])
    ya_in = _rowwise("gla_out", _head_rms_gate, [o_raw, _Win(proj, GLA_DV, 2)], [W["gla_norm_g"]],
                     [_sds((S, GLA_DV), BF16)])[0]
    mixed, pool_out = _pool_fwd(proj, W["w_pool"], W["pool_scale"])
    mem_n = _rowwise("mem_norm", lambda m, g: _rms(m, g), [mem], [W["mem_norm_g"]], [_sds((M, D_MODEL), BF16)])[0]
    kv = _mm_nn("mem_kv", mem_n, W["w_mem_kv"])
    xa = _xattn_fwd(proj, kv)
    def out_fn(gates, ya_in, pool_out, xa, x1, wa, wb, wc, wo, g, gn):
        ya, yb, yc = (_dot(t, w).astype(BF16) for t, w in ((ya_in, wa), (pool_out, wb), (xa, wc)))
        merged = _merge(gates, ya, yb, yc).astype(BF16)
        y = _dot(merged, wo)
        x2 = x1 + _rms(y, g)
        return ya, yb, yc, merged, y, x2, _rms(x2, gn)

    ya, yb, yc, merged, y, x2, h_next = _rowwise(
        "mix_out", out_fn, [_Win(proj, 3 * D_MODEL, 1), ya_in, pool_out, xa, x1],
        [W["w_up_gla"], W["w_up_pool"], W["w_up_xattn"], W["w_o"], W["mix_post_g"], g_next],
        [_sds((S, D_MODEL), BF16)] * 5 + [_sds((S, D_MODEL)), _sds((S, D_MODEL), BF16)], tm=256)
    return x2, h_next, (h, proj, o_raw, states, ya_in, mixed, pool_out, mem_n, kv, xa, ya, yb, yc, merged, y)


def _mix_bwd(dx2, x1, mem, W, saved, on_grads=None):
    h, proj, o_raw, states, ya_in, mixed, pool_out, mem_n, kv, xa, ya, yb, yc, merged, y = saved
    S = x1.shape[0]

    def out_bwd(y, dx2, gates, ya, yb, yc, g, wo, wa, wb, wc):
        dy, dg = _rms_bwd(y.astype(F32), g, dx2)
        dy = dy.astype(BF16)
        _, vjp = jax.vjp(_merge, gates, ya, yb, yc)
        dgates, dya, dyb, dyc = vjp(_dot(dy, wo, _NT))
        return (dy, dgates, dya, dyb, dyc, _dot(dya, wa, _NT), _dot(dyb, wb, _NT), _dot(dyc, wc, _NT), dg)

    dmain = lax.empty((S, INT_MAIN), BF16)
    dy, dmain, dya, dyb, dyc, d_ya_in, d_pool_out, d_xa, d_mix_post_g = _rowwise(
        "mix_out_bwd", out_bwd, [y, dx2, _Win(proj, 3 * D_MODEL, 1), ya, yb, yc],
        [W["mix_post_g"], W["w_o"], W["w_up_gla"], W["w_up_pool"], W["w_up_xattn"]],
        [_sds((S, D_MODEL), BF16), _Win(dmain, 3 * D_MODEL, 1)] + [_sds((S, D_MODEL), BF16)] * 4
        + [_sds((S, POOL_W), BF16), _sds((S, XA_W), BF16)], [_sds((1, D_MODEL))], tm=256)
    d_w_o = _mm_tn("d_w_o", merged, dy)
    d_w_up_gla = _mm_tn("d_w_up_gla", ya_in, dya)
    d_w_up_pool = _mm_tn("d_w_up_pool", pool_out, dyb)
    d_w_up_xattn = _mm_tn("d_w_up_xattn", xa, dyc)

    def gla_out_bwd(o, g_out, d, gn):
        _, vjp = jax.vjp(_head_rms_gate, o, g_out, gn)
        return vjp(d.astype(F32))

    do_raw, dmain, d_gla_norm_g = _rowwise("gla_out_bwd", gla_out_bwd, [o_raw, _Win(proj, GLA_DV, 2), d_ya_in],
                                           [W["gla_norm_g"]], [_sds((S, GLA_DV)), _Win(dmain, GLA_DV, 2)],
                                           [_sds((1, GLA_DV))])
    dmain, dflow, d_wfu_pad, d_b_f = _gla_bwd(proj, W["w_fu_pad"], W["b_f"], states, do_raw, dmain)
    dmixed, d_w_pool, d_pool_scale = _pool_lin_bwd(d_pool_out, mixed, W["w_pool"], W["pool_scale"])
    dmain = _pool_win_bwd(dmixed, dmain)
    dmain, dkv = _xattn_bwd(proj, kv, d_xa, dmain)
    d_w_mem_kv = _mm_tn("d_w_mem_kv", mem_n, dkv)
    dmem_n = _mm_nt("d_mem_n", dkv, W["w_mem_kv"])
    d_mem_norm_g = _rowwise("mem_norm_bwd", lambda m, d, g: _rms_bwd(m, g, d)[1], [mem, dmem_n], [W["mem_norm_g"]], [],
                            [_sds((1, D_MODEL))])[0]
    d_w_int_t = (_mm_tn("d_w_int", dmain, h, ts=1024, tn_a=INT_MAIN // 4), _mm_tn("d_w_flow", dflow, h, ts=2048))
    grads = dict(
        w_int_t=d_w_int_t, w_fu_pad=d_wfu_pad, b_f=d_b_f, gla_norm_g=d_gla_norm_g, w_pool=d_w_pool,
        pool_scale=d_pool_scale, mem_norm_g=d_mem_norm_g, w_mem_kv=d_w_mem_kv, w_up_gla=d_w_up_gla,
        w_up_pool=d_w_up_pool, w_up_xattn=d_w_up_xattn, w_o=d_w_o, mix_post_g=d_mix_post_g)
    deps = [on_grads(grads)] if on_grads is not None else []
    dx1, grads["mix_pre_g"] = _dh_pre_norm_bwd("d_mix_h", [dmain, dflow], W["w_int_t"], x1, dx2, W["mix_pre_g"], deps)
    return dx1, grads


def _mesh_pos():
    x, y, c = lax.axis_index("x"), lax.axis_index("y"), lax.axis_index("c")
    return x, y, c, 4 * x + 2 * y + c


def _peer(x, y, c, r):
    px = 1 - x if r & 4 else x
    py = 1 - y if r & 2 else y
    pc = 1 - c if r & 1 else c
    return (px, py, pc), 4 * px + 2 * py + pc


_ALL_PEERS = tuple(range(1, N_DEV))
_SIBLING = 1
_SAME_CORE = (2, 4, 6)


def _dev_slot(ref, dev):
    return ref.at[dev]


def _zero_pad_rows(land):
    pad = FF_PAD - FF_BLK

    def body(land_in, o_ref):
        o_ref[...] = jnp.zeros_like(o_ref)

    return _pcall(body, name="zero_pad_rows", grid=(land.shape[0],), in_specs=[pl.BlockSpec(memory_space=pl.ANY)],
                  out_specs=pl.BlockSpec((None, pad, D_MODEL), lambda j: (j, FF_BLK // pad, 0)),
                  out_shape=_sds(land.shape, land.dtype), aliases={0: 0})(land)


class _Plan:
    def __init__(self, scatter, slots=None, shapes=None):
        self.scatter, self.slots, self.shapes = scatter, slots or {}, shapes or {}

    def src(self, srcs, a, dev):
        return self.slots.get(a, _dev_slot)(srcs[a], dev) if self.scatter else srcs[a]

    def dst(self, lands, a, dev):
        return lands[a].at[dev] if self.scatter else self.slots.get(a, _dev_slot)(lands[a], dev)

    def landing_zones(self, arrays):
        lands = []
        for a, arr in enumerate(arrays):
            if self.scatter:
                lands.append(lax.empty((N_DEV,) + tuple(self.shapes.get(a, arr.shape[1:])), arr.dtype))
            elif a in self.shapes:
                lands.append(_zero_pad_rows(lax.empty(self.shapes[a], arr.dtype)))
            else:
                lands.append(lax.empty((N_DEV,) + arr.shape, arr.dtype))
        return lands


_GATHER, _SCATTER = _Plan(False), _Plan(True)


def _peer_copies(srcs, lands, send_sems, recv_sems, plan, peers=_ALL_PEERS):
    x, y, c, me = _mesh_pos()
    cps = []
    for r in peers:
        pos, peer = _peer(x, y, c, r)
        for a in range(len(srcs)):
            k = a * (N_DEV - 1) + r - 1
            cps.append(pltpu.make_async_remote_copy(
                src_ref=plan.src(srcs, a, peer), dst_ref=plan.dst(lands, a, me),
                send_sem=send_sems.at[k], recv_sem=recv_sems.at[k], device_id=pos,
                device_id_type=pl.DeviceIdType.MESH))
    return cps


_HBM = pl.BlockSpec(memory_space=pltpu.HBM)
_SEM = pl.BlockSpec(memory_space=pltpu.SEMAPHORE)
_EFFECT = pltpu.SideEffectType.DATAFLOW_SIDE_EFFECTING


def _own_copies(srcs, lands, own_sems, plan):
    me = _mesh_pos()[3]
    return [pltpu.make_async_copy(plan.src(srcs, a, me), plan.dst(lands, a, me), own_sems.at[a])
            for a in range(len(srcs))]


def _exchange_start(name, arrays, plan):
    n = len(arrays)
    lands = plan.landing_zones(arrays)
    n_sem = n * (N_DEV - 1)

    def body(*refs):
        srcs, lands_ = refs[:n], refs[n:2 * n]
        send_sems, recv_sems, own_sems = refs[2 * n:2 * n + 3]
        token = refs[-1]
        for cp in _peer_copies(srcs, lands_, send_sems, recv_sems, plan) + _own_copies(srcs, lands_, own_sems, plan):
            cp.start()
        token[...] = jnp.zeros_like(token)

    hbm = lambda a: pltpu.HBM(a.shape, a.dtype)
    res = pl.pallas_call(
        body, name=name,
        out_shape=(pltpu.SemaphoreType.DMA((n_sem,)), pltpu.SemaphoreType.DMA((n_sem,)), pltpu.SemaphoreType.DMA((n,)),
                   *[hbm(a) for a in arrays], *[hbm(a) for a in lands], _sds((8, 128))),
        in_specs=[_HBM] * (2 * n),
        out_specs=(_SEM, _SEM, _SEM, *[_HBM] * (2 * n), pl.BlockSpec(memory_space=pltpu.VMEM)),
        input_output_aliases={i: 3 + i for i in range(2 * n)},
        compiler_params=pltpu.CompilerParams(has_side_effects=_EFFECT),
    )(*[pltpu.with_memory_space_constraint(a, pltpu.HBM) for a in list(arrays) + lands])
    return (res[:3], res[3:3 + n], res[3 + n:3 + 2 * n], plan), res[-1]


def _exchange_wait(name, state, after):
    sems, srcs, lands, plan = state
    n = len(srcs)

    def body(*refs):
        srcs_, lands_ = refs[:n], refs[n:2 * n]
        send_sems, recv_sems, own_sems = refs[2 * n:2 * n + 3]
        for cp in _peer_copies(srcs_, lands_, send_sems, recv_sems, plan):
            cp.wait_send()
            cp.wait_recv()
        for cp in _own_copies(srcs_, lands_, own_sems, plan):
            cp.wait()

    hbm = lambda a: pltpu.HBM(a.shape, a.dtype)
    res = pl.pallas_call(
        body, name=name, out_shape=tuple(hbm(a) for a in list(srcs) + list(lands)),
        in_specs=[_HBM] * (2 * n) + [_SEM] * 3 + [pl.BlockSpec(memory_space=pl.ANY)] * len(after),
        out_specs=tuple([_HBM] * (2 * n)), input_output_aliases={i: i for i in range(2 * n)},
        compiler_params=pltpu.CompilerParams(has_side_effects=_EFFECT),
    )(*srcs, *lands, *sems, *after)
    return res[n:]


def _gather_start(name, arrays, plan, after):
    n = len(arrays)
    lands = plan.landing_zones(arrays)
    n_sem = n * (N_DEV - 1)

    def body(*refs):
        srcs, lands_ = refs[:n], refs[n:2 * n]
        send_sems, recv_sems, own_sems = refs[2 * n + len(after):2 * n + len(after) + 3]
        token = refs[-1]
        for cp in (_peer_copies(srcs, lands_, send_sems, recv_sems, plan, (_SIBLING,) + _SAME_CORE)
                   + _own_copies(srcs, lands_, own_sems, plan)):
            cp.start()
        token[...] = jnp.zeros_like(token)

    hbm = lambda a: pltpu.HBM(a.shape, a.dtype)
    res = pl.pallas_call(
        body, name=name,
        out_shape=(pltpu.SemaphoreType.DMA((n_sem,)), pltpu.SemaphoreType.DMA((n_sem,)), pltpu.SemaphoreType.DMA((n,)),
                   *[hbm(a) for a in arrays], *[hbm(a) for a in lands], _sds((8, 128))),
        in_specs=[_HBM] * (2 * n) + [pl.BlockSpec(memory_space=pl.ANY)] * len(after),
        out_specs=(_SEM, _SEM, _SEM, *[_HBM] * (2 * n), pl.BlockSpec(memory_space=pltpu.VMEM)),
        input_output_aliases={i: 3 + i for i in range(2 * n)},
        compiler_params=pltpu.CompilerParams(has_side_effects=_EFFECT),
    )(*[pltpu.with_memory_space_constraint(a, pltpu.HBM) for a in list(arrays) + lands], *after)
    return (res[:3], res[3:3 + n], res[3 + n:3 + 2 * n], plan), res[-1]


def _pass_on_copies(lands, send_sems, recv_sems, plan):
    x, y, c, _ = _mesh_pos()
    sibling = _peer(x, y, c, _SIBLING)[0]
    cps = []
    for i, r in enumerate(_SAME_CORE):
        owner = _peer(x, y, c, r)[1]
        for a in range(len(lands)):
            k = a * len(_SAME_CORE) + i
            cps.append(pltpu.make_async_remote_copy(
                src_ref=plan.dst(lands, a, owner), dst_ref=plan.dst(lands, a, owner), send_sem=send_sems.at[k],
                recv_sem=recv_sems.at[k], device_id=sibling, device_id_type=pl.DeviceIdType.MESH))
    return cps


def _gather_pass_on(name, state, after):
    sems, srcs, lands, plan = state
    n = len(srcs)
    n_sem = n * len(_SAME_CORE)

    def body(*refs):
        srcs_, lands_ = refs[:n], refs[n:2 * n]
        send_sems, recv_sems = refs[2 * n], refs[2 * n + 1]
        on_send, on_recv = refs[2 * n + 3 + len(after)], refs[2 * n + 4 + len(after)]
        token = refs[-1]
        arrivals = _peer_copies(srcs_, lands_, send_sems, recv_sems, plan, _SAME_CORE)
        for arrived, on in zip(arrivals, _pass_on_copies(lands_, on_send, on_recv, plan)):
            arrived.wait_recv()
            on.start()
        token[...] = jnp.zeros_like(token)

    hbm = lambda a: pltpu.HBM(a.shape, a.dtype)
    res = pl.pallas_call(
        body, name=name,
        out_shape=(pltpu.SemaphoreType.DMA((n_sem,)), pltpu.SemaphoreType.DMA((n_sem,)),
                   *[hbm(a) for a in list(srcs) + list(lands)], _sds((8, 128))),
        in_specs=[_HBM] * (2 * n) + [_SEM] * 3 + [pl.BlockSpec(memory_space=pl.ANY)] * len(after),
        out_specs=(_SEM, _SEM, *[_HBM] * (2 * n), pl.BlockSpec(memory_space=pltpu.VMEM)),
        input_output_aliases={i: 2 + i for i in range(2 * n)},
        compiler_params=pltpu.CompilerParams(has_side_effects=_EFFECT),
    )(*srcs, *lands, *sems, *after)
    return (sems, res[:2], res[2:2 + n], res[2 + n:2 + 2 * n], plan), res[-1]


def _gather_wait(name, state, after):
    sems, on_sems, srcs, lands, plan = state
    n = len(srcs)

    def body(*refs):
        srcs_, lands_ = refs[:n], refs[n:2 * n]
        send_sems, recv_sems, own_sems, on_send, on_recv = refs[2 * n:2 * n + 5]
        for cp in _peer_copies(srcs_, lands_, send_sems, recv_sems, plan, (_SIBLING,)):
            cp.wait_recv()
        for cp in _peer_copies(srcs_, lands_, send_sems, recv_sems, plan, (_SIBLING,) + _SAME_CORE):
            cp.wait_send()
        for cp in _own_copies(srcs_, lands_, own_sems, plan):
            cp.wait()
        for cp in _pass_on_copies(lands_, on_send, on_recv, plan):
            cp.wait_send()
            cp.wait_recv()

    hbm = lambda a: pltpu.HBM(a.shape, a.dtype)
    res = pl.pallas_call(
        body, name=name, out_shape=tuple(hbm(a) for a in list(srcs) + list(lands)),
        in_specs=[_HBM] * (2 * n) + [_SEM] * 5 + [pl.BlockSpec(memory_space=pl.ANY)] * len(after),
        out_specs=tuple([_HBM] * (2 * n)), input_output_aliases={i: i for i in range(2 * n)},
        compiler_params=pltpu.CompilerParams(has_side_effects=_EFFECT),
    )(*srcs, *lands, *sems, *on_sems, *after)
    return res[n:]


def _sum_parts(name, recv):
    def body(r_ref, o_ref):
        s = r_ref[0]
        for j in range(1, N_DEV):
            s = s + r_ref[j]
        o_ref[...] = s

    return pl.pallas_call(body, name=name, out_shape=_sds(recv.shape[1:], recv.dtype))(recv)


def _adam(name, recv, w, m, v):
    shape = w.shape
    R, C = shape
    tr = R
    while N_DEV * tr * C * 4 > 6 * 1024 * 1024 and tr % 32 == 0:
        tr //= 2

    def body(recv_ref, w_ref, m_ref, v_ref, g_ref, d_ref, m2_ref, v2_ref):
        g = recv_ref[0].astype(F32)
        for j in range(1, N_DEV):
            g = g + recv_ref[j].astype(F32)
        w_, m_, v_ = w_ref[...], m_ref[...], v_ref[...]
        m2 = ADAM_B1 * m_ + (1.0 - ADAM_B1) * g
        v2 = ADAM_B2 * v_ + (1.0 - ADAM_B2) * (g * g)
        m_hat = m2 / (1.0 - ADAM_B1 ** ADAM_STEP)
        v_hat = v2 / (1.0 - ADAM_B2 ** ADAM_STEP)
        g_ref[...] = g
        d_ref[...] = -ADAM_LR * (m_hat / (jnp.sqrt(v_hat) + ADAM_EPS) + ADAM_WD * w_)
        m2_ref[...] = m2
        v2_ref[...] = v2

    blk = pl.BlockSpec((tr, C), lambda i: (i, 0))
    return _pcall(body, name=name, grid=(R // tr,),
                  in_specs=[pl.BlockSpec((N_DEV, tr, C), lambda i: (0, i, 0)), blk, blk, blk],
                  out_specs=[blk] * 4, out_shape=[_sds(shape)] * 4)(recv, w, m, v)


_NAMES = ['ffn1_pre_g', 'ffn1_w_in', 'ffn1_w_out', 'ffn1_post_g', 'mix_pre_g', 'w_in', 'w_fu', 'b_f', 'gla_norm_g',
          'w_pool', 'pool_scale', 'mem_norm_g', 'w_mem_kv', 'w_up_gla', 'w_up_pool', 'w_up_xattn', 'w_o', 'mix_post_g',
          'ffn2_pre_g', 'ffn2_w_in', 'ffn2_w_out', 'ffn2_post_g', 'final_g']
_SHARDED = ['ffn1_w_in', 'ffn1_w_out', 'w_in', 'w_fu', 'w_mem_kv', 'w_up_gla', 'w_up_pool', 'w_up_xattn', 'w_o',
            'ffn2_w_in', 'ffn2_w_out']
_COL_SHARDED = ['w_up_pool', 'w_up_xattn']


def _cols_to_full(g):
    return jnp.transpose(g, (1, 0, 2)).reshape(g.shape[1], N_DEV * g.shape[2])


def _full_to_cols(f):
    R, C = f.shape
    return jnp.transpose(f.reshape(R, N_DEV, C // N_DEV), (1, 0, 2))


def _to_internal(w_in_t):
    o = 0
    parts = []
    for s in IN_SPLITS:
        parts.append(w_in_t[o:o + s])
        o += s
    q, k, v, g_out, f_low, p_in, xq, gates = parts
    f_low = jnp.pad(f_low, ((0, FLOW_W - GATE_RANK), (0, 0)))
    return jnp.concatenate([q, k, v, g_out, gates, p_in, xq, f_low], axis=0)


def _from_internal(d, d_flow):
    q, k, v, g_out = d[0:512], d[512:1024], d[1024:2048], d[2048:3072]
    gates, p_in, xq = d[3072:6144], d[6144:6656], d[6656:7168]
    return jnp.concatenate([q, k, v, g_out, d_flow[:GATE_RANK], p_in, xq, gates], axis=0)


_FFN_IN = ('ffn1_w_in', 'ffn2_w_in')
_FFN_OUT = ('ffn1_w_out', 'ffn2_w_out')
_GATHERS = {"ffn1_in": ['ffn1_w_in'], "ffn1_out": ['ffn1_w_out'],
            "mix": ['w_in', 'w_fu', 'w_mem_kv', 'w_up_gla', 'w_up_pool', 'w_up_xattn', 'w_o'],
            "ffn2": ['ffn2_w_in', 'ffn2_w_out']}


def _ffn_in_slot(ref, d):
    return ref.at[2 * (d % N_FF_BLK) + d // N_FF_BLK, pl.ds(0, FF_BLK)]


def _ffn_out_slot(ref, d):
    rows = FF_BLK // 2
    return ref.at[d // 2, pl.ds(pl.multiple_of((d % 2) * rows, rows), rows)]


def _ffn_plan(names, scatter):
    slots, shapes = {}, {}
    for a, n in enumerate(names):
        if n in _FFN_IN:
            slots[a] = _ffn_in_slot
            shapes[a] = (FF_BLK, D_MODEL) if scatter else (N_DEV, FF_PAD, D_MODEL)
        elif n in _FFN_OUT:
            slots[a] = _ffn_out_slot
            shapes[a] = (FF_BLK // 2, D_MODEL) if scatter else (N_FF_BLK, FF_PAD, D_MODEL)
    return _Plan(scatter, slots, shapes)


def _rows_to_full(g):
    return g.reshape(N_DEV * g.shape[1], g.shape[2])


def _mix_weights(gathered):
    W = {n: _rows_to_full(gathered[n]) for n in ('w_mem_kv', 'w_up_gla', 'w_o')}
    for n in _COL_SHARDED:
        W[n] = _cols_to_full(gathered[n])
    W["w_int_t"] = _to_internal(gathered["w_in"].reshape(IN_WIDTH, D_MODEL))
    W["w_fu_pad"] = jnp.pad(_cols_to_full(gathered["w_fu"]), ((0, FLOW_W - GATE_RANK), (0, 0)))
    return W


def _mix_chunks(G, n):
    if n == "w_in":
        return _from_internal(*G["w_int_t"]).reshape(N_DEV, IN_SHARD, D_MODEL)
    if n == "w_fu":
        return _full_to_cols(G["w_fu_pad"][:GATE_RANK].astype(BF16))
    if n in _COL_SHARDED:
        return _full_to_cols(G[n])
    return G[n].reshape(N_DEV, G[n].shape[0] // N_DEV, G[n].shape[1])


def _step(x, mem, tgt, P, Mo, Vo):
    def native(n, a):
        return jnp.swapaxes(a, 0, 1) if n in _FFN_IN + ("w_in",) else a

    P, Mo, Vo = ({n: native(n, a) for n, a in d.items()} for d in (P, Mo, Vo))
    small = {n: P[n] for n in _NAMES if n not in _SHARDED}

    gather, tokens = {}, []
    for grp, names in _GATHERS.items():
        gather[grp], tok = _gather_start("gather_" + grp, [P[n].astype(BF16) for n in names], _ffn_plan(names, False),
                                         tokens[-1:])
        tokens.append(tok)

    def pass_on(grp, after):
        return _gather_pass_on("gather_" + grp + "_on", gather[grp], after)

    def gathered(grp, after):
        return _gather_wait("gather_" + grp + "_wait", pass_on(grp, after)[0], [])

    def ffn_wo(lands):
        return lambda act: lands[0].reshape(N_FF_BLK * FF_PAD, D_MODEL)

    h1 = _pre_norm("ffn1_pre", x, small["ffn1_pre_g"], tokens[:1])
    w1t = gathered("ffn1_in", tokens[-1:] + [h1])[0].reshape(N_DEV * FF_PAD, D_MODEL)
    (x1, hm), w1o, sv1 = _ffn_fwd("ffn1", x, h1, w1t, small["ffn1_post_g"],
                                  lambda act: ffn_wo(gathered("ffn1_out", [act]))(act), small["mix_pre_g"])

    gm = dict(zip(_GATHERS["mix"], gathered("mix", [x1])))
    Wm = {**small, **_mix_weights(gm)}
    passed = {}

    def pass_on_ffn2(proj):
        passed["ffn2"], tok = pass_on("ffn2", [proj])
        return [tok]

    x2, h2, svm = _mix_fwd(x1, hm, mem, Wm, small["ffn2_pre_g"], pass_on_ffn2)
    w2t, w2o = _gather_wait("gather_ffn2_wait", passed["ffn2"], [x2])
    w2t = w2t.reshape(N_DEV * FF_PAD, D_MODEL)
    (loss, dx3, d_final_g), w2o, sv2 = _ffn_fwd("ffn2", x2, h2, w2t, small["ffn2_post_g"], ffn_wo([w2o]),
                                                small["final_g"], tgt)

    G = dict(final_g=d_final_g)
    scat = {}

    def start(grp, names, arrays):
        scat[grp] = names, _exchange_start("scatter_" + grp, arrays, _ffn_plan(names, True))
        return scat[grp][1][1]

    def ffn_starts(tag):
        return (lambda dwo: start(tag + "_out", [tag + "_w_out"], [dwo.reshape(N_FF_BLK, FF_PAD, D_MODEL)]),
                lambda dwt: start(tag + "_in", [tag + "_w_in"], [dwt.reshape(N_DEV, FF_PAD, D_MODEL)]))

    dx2, G["ffn2_pre_g"], G["ffn2_post_g"] = _ffn_bwd(
        "ffn2", dx3, x2, small["ffn2_pre_g"], w2t, w2o, small["ffn2_post_g"], sv2, *ffn_starts("ffn2"))
    dx1, Gm = _mix_bwd(dx2, x1, mem, Wm, svm,
                       lambda Gm: start("mix", _GATHERS["mix"], [_mix_chunks(Gm, n) for n in _GATHERS["mix"]]))
    G.update(Gm)
    dx, G["ffn1_pre_g"], G["ffn1_post_g"] = _ffn_bwd(
        "ffn1", dx1, x, small["ffn1_pre_g"], w1t, w1o, small["ffn1_post_g"], sv1, *ffn_starts("ffn1"))

    scat["small"] = list(small) + ["loss"], _exchange_start("gather_small_grads", [G[n] for n in small] + [loss],
                                                            _GATHER)
    recv, outs = {}, {}
    done = [dx]
    for grp in ["ffn2_out", "ffn2_in", "mix", "ffn1_out", "ffn1_in", "small"]:
        names, (state, _) = scat[grp]
        recv.update(zip(names, _exchange_wait(("gather_" if grp == "small" else "scatter_") + grp + "_wait", state,
                                              done)))
        for n in names:
            if n == "loss":
                loss = _sum_parts("loss_sum", recv[n])[0, 0]
                continue
            shp = P[n].shape
            shp2 = shp if len(shp) == 2 else (shp[0] * shp[1], shp[2])
            res = _adam("adam_" + n, recv[n].reshape((N_DEV,) + shp2), P[n].reshape(shp2), Mo[n].reshape(shp2),
                        Vo[n].reshape(shp2))
            outs[n] = [native(n, t.reshape(shp)) for t in res]
            done.append(res[-1])
    return loss, dx, outs


def kernel(x, mem, ffn1_pre_g, ffn1_w_in, ffn1_w_out, ffn1_post_g, mix_pre_g, w_in, w_fu, b_f, gla_norm_g, w_pool, pool_scale, mem_norm_g, w_mem_kv, w_up_gla, w_up_pool, w_up_xattn, w_o, mix_post_g, ffn2_pre_g, ffn2_w_in, ffn2_w_out, ffn2_post_g, final_g, loss_target, m_ffn1_pre_g, m_ffn1_w_in, m_ffn1_w_out, m_ffn1_post_g, m_mix_pre_g, m_w_in, m_w_fu, m_b_f, m_gla_norm_g, m_w_pool, m_pool_scale, m_mem_norm_g, m_w_mem_kv, m_w_up_gla, m_w_up_pool, m_w_up_xattn, m_w_o, m_mix_post_g, m_ffn2_pre_g, m_ffn2_w_in, m_ffn2_w_out, m_ffn2_post_g, m_final_g, v_ffn1_pre_g, v_ffn1_w_in, v_ffn1_w_out, v_ffn1_post_g, v_mix_pre_g, v_w_in, v_w_fu, v_b_f, v_gla_norm_g, v_w_pool, v_pool_scale, v_mem_norm_g, v_w_mem_kv, v_w_up_gla, v_w_up_pool, v_w_up_xattn, v_w_o, v_mix_post_g, v_ffn2_pre_g, v_ffn2_w_in, v_ffn2_w_out, v_ffn2_post_g, v_final_g):
    params = [ffn1_pre_g, ffn1_w_in, ffn1_w_out, ffn1_post_g, mix_pre_g, w_in, w_fu, b_f, gla_norm_g, w_pool, pool_scale, mem_norm_g, w_mem_kv, w_up_gla, w_up_pool, w_up_xattn, w_o, mix_post_g, ffn2_pre_g, ffn2_w_in, ffn2_w_out, ffn2_post_g, final_g]
    moms = [m_ffn1_pre_g, m_ffn1_w_in, m_ffn1_w_out, m_ffn1_post_g, m_mix_pre_g, m_w_in, m_w_fu, m_b_f, m_gla_norm_g, m_w_pool, m_pool_scale, m_mem_norm_g, m_w_mem_kv, m_w_up_gla, m_w_up_pool, m_w_up_xattn, m_w_o, m_mix_post_g, m_ffn2_pre_g, m_ffn2_w_in, m_ffn2_w_out, m_ffn2_post_g, m_final_g]
    vars_ = [v_ffn1_pre_g, v_ffn1_w_in, v_ffn1_w_out, v_ffn1_post_g, v_mix_pre_g, v_w_in, v_w_fu, v_b_f, v_gla_norm_g, v_w_pool, v_pool_scale, v_mem_norm_g, v_w_mem_kv, v_w_up_gla, v_w_up_pool, v_w_up_xattn, v_w_o, v_mix_post_g, v_ffn2_pre_g, v_ffn2_w_in, v_ffn2_w_out, v_ffn2_post_g, v_final_g]
    P = {n: a[0] if a.ndim > 2 else a for n, a in zip(_NAMES, params)}
    Mo = {n: a[0] if a.ndim > 2 else a for n, a in zip(_NAMES, moms)}
    Vo = {n: a[0] if a.ndim > 2 else a for n, a in zip(_NAMES, vars_)}
    loss, dx, outs = _step(x[0], mem[0], loss_target[0], P, Mo, Vo)
    out = [loss, dx[None]]
    for kind in range(4):
        for n, p in zip(_NAMES, params):
            out.append(outs[n][kind].reshape(p.shape))
    return tuple(out)
```

```python
import functools

import jax
import jax.numpy as jnp
from jax import lax
from jax.experimental import pallas as pl
from jax.experimental.pallas import tpu as pltpu

F32 = jnp.float32
BF16 = jnp.bfloat16

N_DEV = 8
D_MODEL = 1024
D_FF = 2816
FF_BLK = 2 * D_FF // N_DEV
N_FF_BLK = D_FF // FF_BLK
FF_PAD = 768
CHUNK = 64
GLA_HEADS = 4
GLA_DK = 512
GLA_DV = 1024
GLA_HDK = 128
GLA_HDV = 256
GATE_RANK = 16
GATE_TEMP = 16.0
POOL_WINDOWS = (2, 4, 8, 16)
POOL_W = 512
POOL_GD = 128
POOL_HALO = 16
XA_HEADS = 4
XA_HD = 128
XA_W = 512
EPS = 1e-6
IN_SPLITS = (GLA_DK, GLA_DK, GLA_DV, GLA_DV, GATE_RANK, POOL_W, XA_W, 3 * D_MODEL)
IN_WIDTH = sum(IN_SPLITS)
IN_SHARD = IN_WIDTH // N_DEV
INT_W = 3072 + 3072 + 1024 + 128
INT_NBLK = 3
INT_BLK = INT_W // INT_NBLK
FLOW_W = 128
INT_MAIN = INT_W - FLOW_W
_POOL_COL = (3072 + 3072) // POOL_W
_XQ_COL = (3072 + 3072 + POOL_W) // XA_W

ADAM_LR = 0.001
ADAM_B1 = 0.9
ADAM_B2 = 0.999
ADAM_EPS = 1e-08
ADAM_WD = 0.01
ADAM_STEP = 10

VMEM_LIMIT = 56 * 1024 * 1024

_NN = (((1,), (0,)), ((), ()))
_NT = (((1,), (1,)), ((), ()))
_TN = (((0,), (0,)), ((), ()))


def _pcall(body, *, name, grid, in_specs, out_specs, out_shape, scratch=(), aliases=None):
    return pl.pallas_call(
        body, name=name, grid=grid, in_specs=in_specs, out_specs=out_specs, out_shape=out_shape,
        scratch_shapes=list(scratch), input_output_aliases=aliases or {},
        compiler_params=pltpu.CompilerParams(dimension_semantics=("arbitrary",) * len(grid),
                                             vmem_limit_bytes=VMEM_LIMIT))


def _dot(a, b, dims=_NN):
    return lax.dot_general(a.astype(BF16), b.astype(BF16), dims, preferred_element_type=F32)


def _mm(name, a, b, *, grid, a_blk, a_map, b_blk, b_map, o_shape, o_blk, o_map, dims, out_dtype=F32, deps=()):
    nk = grid[2]

    def body(a_ref, b_ref, *rest):
        o_ref, scr = rest[len(deps)], rest[len(deps) + 1:]
        p = _dot(a_ref[...], b_ref[...], dims)
        if nk == 1:
            o_ref[...] = p.astype(o_ref.dtype)
        else:
            acc = scr[0]
            k = pl.program_id(2)

            @pl.when(k == 0)
            def _():
                acc[...] = p

            @pl.when(k > 0)
            def _():
                acc[...] += p

            @pl.when(k == nk - 1)
            def _():
                o_ref[...] = acc[...].astype(o_ref.dtype)

    acc_shape = tuple(d for d in o_blk if d is not None)
    return _pcall(body, name=name, grid=grid,
                  in_specs=[pl.BlockSpec(a_blk, a_map), pl.BlockSpec(b_blk, b_map)]
                  + [pl.BlockSpec(memory_space=pl.ANY)] * len(deps),
                  out_specs=pl.BlockSpec(o_blk, o_map),
                  out_shape=jax.ShapeDtypeStruct(o_shape, out_dtype),
                  scratch=[pltpu.VMEM(acc_shape, F32)] if nk > 1 else [])(a, b, *deps)


def _mm_nn(name, a, b, out_dtype=F32, tm=1024, tn=None):
    M, K = a.shape
    N = b.shape[1]
    tm, tn = min(tm, M), (tn or N)
    return _mm(name, a, b, grid=(N // tn, M // tm, 1), a_blk=(tm, K), a_map=lambda j, i, k: (i, 0),
               b_blk=(K, tn), b_map=lambda j, i, k: (0, j), o_shape=(M, N), o_blk=(tm, tn),
               o_map=lambda j, i, k: (i, j), dims=_NN, out_dtype=out_dtype)


def _mm_nt(name, a, b, out_dtype=F32, tm=1024):
    M, K = a.shape
    N = b.shape[0]
    tm = min(tm, M)
    return _mm(name, a, b, grid=(1, M // tm, 1), a_blk=(tm, K), a_map=lambda j, i, k: (i, 0),
               b_blk=(N, K), b_map=lambda j, i, k: (0, 0), o_shape=(M, N), o_blk=(tm, N),
               o_map=lambda j, i, k: (i, 0), dims=_NT, out_dtype=out_dtype)


def _mm_tn(name, a, b, out_dtype=BF16, ts=512, tn=None, tn_a=None, deps=()):
    S, M = a.shape
    N = b.shape[1]
    ts, tn, tn_a = min(ts, S), (tn or N), (tn_a or M)
    return _mm(name, a, b, grid=((N // tn) * (M // tn_a), 1, S // ts), a_blk=(ts, tn_a),
               a_map=lambda j, i, k: (k, j if tn_a < M else 0), b_blk=(ts, tn),
               b_map=lambda j, i, k: (k, j if tn < N else 0), o_shape=(M, N), o_blk=(tn_a, tn),
               o_map=lambda j, i, k: (j, 0) if tn_a < M else (0, j), dims=_TN, out_dtype=out_dtype, deps=deps)


class _Win:
    def __init__(self, arr, w, c):
        self.arr, self.w, self.c = arr, w, c


def _row_spec(x, tm):
    if isinstance(x, _Win):
        return x.arr, pl.BlockSpec((tm, x.w), functools.partial(lambda i, c: (i, c), c=x.c))
    if x.ndim == 3:
        return x, pl.BlockSpec((x.shape[0], tm, x.shape[2]), lambda i: (0, i, 0))
    return x, pl.BlockSpec((tm, x.shape[1]), lambda i: (i, 0))


def _rowwise(name, fn, rows, consts, outs, accs=(), tm=512):
    first = rows[0].arr if isinstance(rows[0], _Win) else rows[0]
    S = first.shape[1] if first.ndim == 3 else first.shape[0]
    tm = min(tm, S)
    n_in, n_out = len(rows) + len(consts), len(outs)
    arrays, in_specs = [], []
    for r in rows:
        arr, spec = _row_spec(r, tm)
        arrays.append(arr)
        in_specs.append(spec)
    for c in consts:
        arrays.append(c)
        in_specs.append(pl.BlockSpec(c.shape, functools.partial(lambda i, n: (0,) * n, n=c.ndim)))
    aliases = {}
    for k, o in enumerate(outs):
        if isinstance(o, _Win):
            aliases[len(arrays)] = k
            arrays.append(o.arr)
            in_specs.append(pl.BlockSpec(memory_space=pl.ANY))
    n_thru = len(aliases)
    out_specs = [_row_spec(o, tm)[1] for o in outs]
    out_specs += [pl.BlockSpec(a.shape, functools.partial(lambda i, n: (0,) * n, n=len(a.shape))) for a in accs]
    out_shape = [_sds(o.arr.shape, o.arr.dtype) if isinstance(o, _Win) else _sds(o.shape, o.dtype) for o in outs]

    def body(*refs):
        res = fn(*[r[...] for r in refs[:n_in]])
        if not isinstance(res, (tuple, list)):
            res = (res,)
        orefs = refs[n_in + n_thru:]
        for r, v in zip(orefs[:n_out], res[:n_out]):
            r[...] = v.astype(r.dtype)
        i = pl.program_id(0)
        for r, v in zip(orefs[n_out:], res[n_out:]):
            @pl.when(i == 0)
            def _(r=r, v=v):
                r[...] = v.astype(r.dtype)

            @pl.when(i > 0)
            def _(r=r, v=v):
                r[...] += v.astype(r.dtype)

    return _pcall(body, name=name, grid=(S // tm,), in_specs=in_specs, out_specs=out_specs,
                  out_shape=out_shape + [_sds(a.shape, a.dtype) for a in accs], aliases=aliases)(*arrays)


def _sds(shape, dtype=F32):
    return jax.ShapeDtypeStruct(shape, dtype)


def _rms(x, g):
    return x * lax.rsqrt(jnp.mean(x * x, axis=-1, keepdims=True) + EPS) * g


def _rms_bwd(x, g, dy):
    _, vjp = jax.vjp(_rms, x, g)
    return vjp(dy)


def _sigmoid(x):
    return 0.5 * jnp.tanh(0.5 * x) + 0.5


def _silu(x):
    return x * _sigmoid(x)


def _log_sigmoid(f):
    return jnp.minimum(f, 0.0) - jnp.log(1.0 + jnp.exp(-jnp.abs(f)))


def _head_rms_gate(o, g_out, gn):
    parts = [_rms(o[:, h * GLA_HDV:(h + 1) * GLA_HDV], gn[:, h * GLA_HDV:(h + 1) * GLA_HDV]) for h in range(GLA_HEADS)]
    return jnp.concatenate(parts, axis=-1) * _silu(g_out.astype(F32))


def _merge(gates, ya, yb, yc):
    gates, ya, yb, yc = (t.astype(F32) for t in (gates, ya, yb, yc))
    return (_sigmoid(gates[:, :D_MODEL]) * ya + _sigmoid(gates[:, D_MODEL:2 * D_MODEL]) * yb
            + _sigmoid(gates[:, 2 * D_MODEL:]) * yc)


def _tri_dot(t, x):
    hi = x.astype(BF16)
    r1 = x - hi.astype(F32)
    mid = r1.astype(BF16)
    lo = (r1 - mid.astype(F32)).astype(BF16)
    d = functools.partial(lax.dot_general, dimension_numbers=_NN, preferred_element_type=F32)
    return d(t, hi) + d(t, mid) + d(t, lo)


def _pre_norm(name, x, g, after):
    S = x.shape[0]
    tm = min(512, S)

    def body(x_ref, g_ref, *rest):
        rest[-1][...] = _rms(x_ref[...], g_ref[...]).astype(BF16)

    row = pl.BlockSpec((tm, D_MODEL), lambda i: (i, 0))
    return _pcall(body, name=name, grid=(S // tm,),
                  in_specs=[row, pl.BlockSpec((1, D_MODEL), lambda i: (0, 0))] + [pl.BlockSpec(memory_space=pl.ANY)] * len(after),
                  out_specs=row, out_shape=_sds(x.shape, BF16))(x, g, *after)


def _ffn_fwd(tag, x, h, wt, gpost, get_wo, g_next, tgt=None):
    S = x.shape[0]
    tm = min(512, S)

    def up_body(h_ref, w_ref, u_ref, act_ref):
        hh = h_ref[...]
        for j in range(N_FF_BLK):
            ab = _dot(hh, w_ref[2 * FF_PAD * j:2 * FF_PAD * (j + 1), :], _NT)
            u_ref[:, 2 * FF_PAD * j:2 * FF_PAD * (j + 1)] = ab.astype(BF16)
            act_ref[:, FF_PAD * j:FF_PAD * (j + 1)] = (_silu(ab[:, :FF_PAD]) * ab[:, FF_PAD:]).astype(BF16)

    u, act = _pcall(
        up_body, name=tag + "_up", grid=(S // tm,),
        in_specs=[pl.BlockSpec((tm, D_MODEL), lambda i: (i, 0)),
                  pl.BlockSpec(wt.shape, lambda i: (0, 0), pipeline_mode=pl.Buffered(1))],
        out_specs=[pl.BlockSpec((tm, N_DEV * FF_PAD), lambda i: (i, 0)),
                   pl.BlockSpec((tm, N_FF_BLK * FF_PAD), lambda i: (i, 0))],
        out_shape=[_sds((S, N_DEV * FF_PAD), BF16), _sds((S, N_FF_BLK * FF_PAD), BF16)],
    )(h, wt)

    def down_body(act_ref, w_ref, x_ref, g_ref, gn_ref, f_ref, xo_ref, hn_ref):
        f = _dot(act_ref[...], w_ref[...])
        f_ref[...] = f.astype(BF16)
        xo = x_ref[...] + 0.5 * _rms(f, g_ref[...])
        xo_ref[...] = xo
        hn_ref[...] = _rms(xo, gn_ref[...]).astype(BF16)

    def down_head_body(act_ref, w_ref, x_ref, g_ref, gn_ref, t_ref, f_ref, dxo_ref, loss_ref, dgn_ref):
        i = pl.program_id(0)
        f = _dot(act_ref[...], w_ref[...])
        f_ref[...] = f.astype(BF16)
        xo = x_ref[...] + 0.5 * _rms(f, g_ref[...])
        out, vjp = jax.vjp(_rms, xo, gn_ref[...])
        e = out - t_ref[...]
        loss = 0.5 * jnp.sum(jnp.mean(e * e, axis=-1, keepdims=True), axis=0, keepdims=True)
        dxo_ref[...], dgn = vjp(e * (1.0 / D_MODEL))

        @pl.when(i == 0)
        def _():
            loss_ref[...] = jnp.broadcast_to(loss, loss_ref.shape)
            dgn_ref[...] = dgn

        @pl.when(i > 0)
        def _():
            loss_ref[...] += jnp.broadcast_to(loss, loss_ref.shape)
            dgn_ref[...] += dgn

    wo = get_wo(act)
    row = pl.BlockSpec((tm, D_MODEL), lambda i: (i, 0))
    vec = pl.BlockSpec((1, D_MODEL), lambda i: (0, 0))
    if tgt is not None:
        f, dxo, loss, dg_next = _pcall(
            down_head_body, name=tag + "_down", grid=(S // tm,),
            in_specs=[pl.BlockSpec((tm, N_FF_BLK * FF_PAD), lambda i: (i, 0)),
                      pl.BlockSpec(wo.shape, lambda i: (0, 0), pipeline_mode=pl.Buffered(1)), row, vec, vec, row],
            out_specs=[row, row, pl.BlockSpec((1, 128), lambda i: (0, 0)), vec],
            out_shape=[_sds((S, D_MODEL), BF16), _sds((S, D_MODEL)), _sds((1, 128)), _sds((1, D_MODEL))],
        )(act, wo, x, gpost, g_next, tgt)
        return (loss, dxo, dg_next), wo, (h, u, act, f)
    f, xo, h_next = _pcall(
        down_body, name=tag + "_down", grid=(S // tm,),
        in_specs=[pl.BlockSpec((tm, N_FF_BLK * FF_PAD), lambda i: (i, 0)),
                  pl.BlockSpec(wo.shape, lambda i: (0, 0), pipeline_mode=pl.Buffered(1)), row, vec, vec],
        out_specs=[row, row, row], out_shape=[_sds((S, D_MODEL), BF16), _sds((S, D_MODEL)), _sds((S, D_MODEL), BF16)],
    )(act, wo, x, gpost, g_next)
    return (xo, h_next), wo, (h, u, act, f)


def _ffn_bwd(tag, dxo, x, gpre, wt, wo, gpost, saved, on_dwo=None, on_dwt=None):
    h, u, act, f = saved
    S = x.shape[0]
    tm = min(256, S)

    def dact_body(f_ref, dxo_ref, g_ref, w_ref, u_ref, df_ref, du_ref, dg_ref):
        i = pl.program_id(0)
        df, dg = _rms_bwd(f_ref[...].astype(F32), g_ref[...], 0.5 * dxo_ref[...])
        df = df.astype(BF16)
        df_ref[...] = df

        @pl.when(i == 0)
        def _():
            dg_ref[...] = dg

        @pl.when(i > 0)
        def _():
            dg_ref[...] += dg

        for j in range(N_FF_BLK):
            dact = _dot(df, w_ref[FF_PAD * j:FF_PAD * (j + 1), :], _NT)
            ab = u_ref[:, 2 * FF_PAD * j:2 * FF_PAD * (j + 1)].astype(F32)
            a, b = ab[:, :FF_PAD], ab[:, FF_PAD:]
            sg = _sigmoid(a)
            du_ref[:, 2 * FF_PAD * j:2 * FF_PAD * j + FF_PAD] = (dact * b * (sg * (1.0 + a * (1.0 - sg)))).astype(BF16)
            du_ref[:, 2 * FF_PAD * j + FF_PAD:2 * FF_PAD * (j + 1)] = (dact * (a * sg)).astype(BF16)

    row = pl.BlockSpec((tm, D_MODEL), lambda i: (i, 0))
    vec = pl.BlockSpec((1, D_MODEL), lambda i: (0, 0))
    u_spec = pl.BlockSpec((tm, N_DEV * FF_PAD), lambda i: (i, 0))
    df, du, dgpost = _pcall(
        dact_body, name=tag + "_dact", grid=(S // tm,),
        in_specs=[row, row, vec, pl.BlockSpec(wo.shape, lambda i: (0, 0), pipeline_mode=pl.Buffered(1)), u_spec],
        out_specs=[row, u_spec, vec],
        out_shape=[_sds((S, D_MODEL), BF16), _sds((S, N_DEV * FF_PAD), BF16), _sds((1, D_MODEL))])(f, dxo, gpost, wo, u)
    dwo = _mm_tn(tag + "_dwo", act, df, ts=2048, tn_a=FF_PAD)
    dwt = _mm_tn(tag + "_dwt", du, h, ts=2048, tn_a=FF_PAD, deps=[on_dwo(dwo)] if on_dwo is not None else [])
    dx, dgpre = _dh_pre_norm_bwd(tag + "_dh", [du], wt, x, dxo, gpre, [on_dwt(dwt)] if on_dwt is not None else [])
    return dx, dgpre, dgpost


def _dh_pre_norm_bwd(name, dzs, wt, x, dres, g, deps):
    S = x.shape[0]
    ks = [dz.shape[1] for dz in dzs]

    def body(*refs):
        dz_refs = refs[:len(dzs)]
        w_ref, x_ref, dres_ref, g_ref = refs[len(dzs):len(dzs) + 4]
        dx_ref, dg_ref = refs[len(dzs) + 4 + len(deps):]
        i = pl.program_id(0)
        dh, o = None, 0
        for dz_ref, k in zip(dz_refs, ks):
            p = _dot(dz_ref[...], w_ref[o:o + k, :])
            dh, o = (p if dh is None else dh + p), o + k
        dx, dg = _rms_bwd(x_ref[...], g_ref[...], dh)
        dx_ref[...] = dx + dres_ref[...]

        @pl.when(i == 0)
        def _():
            dg_ref[...] = dg

        @pl.when(i > 0)
        def _():
            dg_ref[...] += dg

    th = min(512, S)
    row = pl.BlockSpec((th, D_MODEL), lambda i: (i, 0))
    vec = pl.BlockSpec((1, D_MODEL), lambda i: (0, 0))
    return _pcall(
        body, name=name, grid=(S // th,),
        in_specs=[pl.BlockSpec((th, k), lambda i: (i, 0)) for k in ks]
        + [pl.BlockSpec(wt.shape, lambda i: (0, 0), pipeline_mode=pl.Buffered(1)), row, row, vec]
        + [pl.BlockSpec(memory_space=pl.ANY)] * len(deps),
        out_specs=[row, vec], out_shape=[_sds((S, D_MODEL)), _sds((1, D_MODEL))])(*dzs, wt, x, dres, g, *deps)


GLA_G = 8


def _gla_tile_common(k, flow, wfu, bf):
    f = _dot(flow, wfu) + bf
    la = _log_sigmoid(f) * (1.0 / GATE_TEMP)
    tri = _tri_matrix(True)
    ws, ds = [], []
    for g in range(k.shape[0] // CHUNK):
        b = _tri_dot(tri, la[g * CHUNK:(g + 1) * CHUNK])
        b_end = b[CHUNK - 1:CHUNK, :]
        ws.append(jnp.exp(b_end - b))
        ds.append(jnp.exp(b_end))
    w = jnp.concatenate(ws, axis=0)
    return f, w, k * w, ds


def _tri_matrix(lower):
    r = lax.broadcasted_iota(jnp.int32, (CHUNK, CHUNK), 0)
    c = lax.broadcasted_iota(jnp.int32, (CHUNK, CHUNK), 1)
    return jnp.where((r >= c) if lower else (r <= c), 1.0, 0.0).astype(BF16)


def _heads():
    return [(slice(h * GLA_HDK, (h + 1) * GLA_HDK), slice(h * GLA_HDV, (h + 1) * GLA_HDV)) for h in range(GLA_HEADS)]


def _gla_fwd(proj, wfu, bf, gn, deps=()):
    S = proj.shape[0]
    G = min(GLA_G, S // CHUNK)
    T = G * CHUNK
    nc = S // CHUNK

    def body(q_ref, k_ref, v_ref, go_ref, flow_ref, wfu_ref, bf_ref, gn_ref, *rest):
        ya_ref, st_ref, state = rest[len(deps):]
        @pl.when(pl.program_id(0) == 0)
        def _():
            state[...] = jnp.zeros_like(state)

        _, _, kt, ds = _gla_tile_common(k_ref[...], flow_ref[...], wfu_ref[...], bf_ref[...])
        q = q_ref[...].astype(F32) * (GLA_HDK ** -0.5)
        v = v_ref[...]
        rows = [slice(g * CHUNK, (g + 1) * CHUNK) for g in range(G)]
        kv = [[_dot(v[r, vs], kt[r, ks], _TN) for ks, vs in _heads()] for r in rows]
        st = [state[vs, :] for _, vs in _heads()]
        o = []
        for g, r in enumerate(rows):
            outs = []
            for h, (ks, vs) in enumerate(_heads()):
                st[h] = st[h] * ds[g][:, ks] + kv[g][h]
                st_ref[g, vs, :] = st[h]
                outs.append(_dot(q[r, ks], st[h], _NT))
            o.append(jnp.concatenate(outs, axis=-1))
        for h, (_, vs) in enumerate(_heads()):
            state[vs, :] = st[h]
        ya_ref[...] = _head_rms_gate(jnp.concatenate(o, axis=0), go_ref[...], gn_ref[...]).astype(BF16)

    return _pcall(
        body, name="gla_fwd", grid=(S // T,),
        in_specs=[pl.BlockSpec((T, GLA_DK), lambda c: (c, 0)), pl.BlockSpec((T, GLA_DK), lambda c: (c, 1)),
                  pl.BlockSpec((T, GLA_DV), lambda c: (c, 1)), pl.BlockSpec((T, GLA_DV), lambda c: (c, 2)),
                  pl.BlockSpec((T, FLOW_W), lambda c: (c, (INT_W - FLOW_W) // FLOW_W)),
                  pl.BlockSpec(wfu.shape, lambda c: (0, 0)), pl.BlockSpec(bf.shape, lambda c: (0, 0)),
                  pl.BlockSpec(gn.shape, lambda c: (0, 0))]
        + [pl.BlockSpec(memory_space=pl.ANY)] * len(deps),
        out_specs=[pl.BlockSpec((T, GLA_DV), lambda c: (c, 0)),
                   pl.BlockSpec((G, GLA_DV, GLA_HDK), lambda c: (c, 0, 0))],
        out_shape=[_sds((S, GLA_DV), BF16), _sds((nc, GLA_DV, GLA_HDK))],
        scratch=[pltpu.VMEM((GLA_DV, GLA_HDK), F32)])(proj, proj, proj, proj, proj, wfu, bf, gn, *deps)


def _gla_bwd(proj, wfu, bf, gn, states, d_ya_in, dmain):
    S = proj.shape[0]
    G = min(GLA_G, S // CHUNK)
    T = G * CHUNK
    nt = S // T

    def body(q_ref, k_ref, v_ref, go_ref, flow_ref, wfu_ref, bf_ref, gn_ref, st_ref, stp_ref, dya_ref, dmain_in,
             dqkvg_ref, dflow_ref, dwfu_ref, dbf_ref, dgn_ref, dstate):
        dq_ref = dqkvg_ref.at[:, 0:GLA_DK]
        dk_ref = dqkvg_ref.at[:, GLA_DK:2 * GLA_DK]
        dv_ref = dqkvg_ref.at[:, 2 * GLA_DK:2 * GLA_DK + GLA_DV]
        dgo_ref = dqkvg_ref.at[:, 2 * GLA_DK + GLA_DV:]
        step = pl.program_id(0)

        @pl.when(step == 0)
        def _():
            dstate[...] = jnp.zeros_like(dstate)
            dwfu_ref[...] = jnp.zeros_like(dwfu_ref)
            dbf_ref[...] = jnp.zeros_like(dbf_ref)
            dgn_ref[...] = jnp.zeros_like(dgn_ref)

        flow, wfu_v = flow_ref[...], wfu_ref[...]
        f, w, kt, ds = _gla_tile_common(k_ref[...], flow, wfu_v, bf_ref[...])
        q = q_ref[...].astype(F32) * (GLA_HDK ** -0.5)
        v = v_ref[...]
        rows = [slice(g * CHUNK, (g + 1) * CHUNK) for g in range(G)]
        o = jnp.concatenate([jnp.concatenate([_dot(q[r, ks], st_ref[g, vs, :], _NT) for ks, vs in _heads()], axis=-1)
                             for g, r in enumerate(rows)], axis=0)
        _, vjp = jax.vjp(_head_rms_gate, o, go_ref[...], gn_ref[...])
        dout, dgo, dgn = vjp(dya_ref[...].astype(F32))
        dgo_ref[...] = dgo.astype(dgo_ref.dtype)
        dgn_ref[...] += dgn
        dq = [jnp.concatenate([_dot(dout[r, vs], st_ref[g, vs, :]) for _, vs in _heads()], axis=-1)
              for g, r in enumerate(rows)]
        qdo = [[_dot(dout[r, vs], q[r, ks], _TN) for ks, vs in _heads()] for r in rows]
        dq_ref[...] = (jnp.concatenate(dq, axis=0) * (GLA_HDK ** -0.5)).astype(dq_ref.dtype)
        has_prev = (step < nt - 1).astype(F32)
        carry = [dstate[vs, :] for _, vs in _heads()]
        dkt, dv, dd = [None] * G, [None] * G, [None] * G
        for g in reversed(range(G)):
            r = rows[g]
            dkts, dvs, dds = [], [], []
            for h, (ks, vs) in enumerate(_heads()):
                dst = carry[h] + qdo[g][h]
                dkts.append(_dot(v[r, vs], dst))
                dvs.append(_dot(kt[r, ks], dst, _NT))
                st_prev = st_ref[g - 1, vs, :] if g > 0 else stp_ref[vs, :] * has_prev
                dds.append(jnp.sum(dst * st_prev, axis=0, keepdims=True))
                carry[h] = dst * ds[g][:, ks]
            dkt[g], dv[g], dd[g] = (jnp.concatenate(t, axis=-1) for t in (dkts, dvs, dds))
        for h, (_, vs) in enumerate(_heads()):
            dstate[vs, :] = carry[h]
        dkt = jnp.concatenate(dkt, axis=0)
        dv_ref[...] = jnp.concatenate(dv, axis=0).astype(dv_ref.dtype)
        dk_ref[...] = (dkt * w).astype(dk_ref.dtype)
        de = dkt * kt
        tri = _tri_matrix(False)
        dla = []
        for g, r in enumerate(rows):
            db_end = jnp.sum(de[r], axis=0, keepdims=True) + dd[g] * ds[g]
            dla.append(db_end - _tri_dot(tri, de[r]))
        df = jnp.concatenate(dla, axis=0) * (1.0 - _sigmoid(f)) * (1.0 / GATE_TEMP)
        dflow_ref[...] = _dot(df, wfu_v, _NT).astype(dflow_ref.dtype)
        dwfu_ref[...] += _dot(flow, df, _TN)
        dbf_ref[...] += jnp.sum(df, axis=0, keepdims=True)

    rt = lambda s: nt - 1 - s
    return _pcall(
        body, name="gla_bwd", grid=(nt,),
        in_specs=[pl.BlockSpec((T, GLA_DK), lambda s: (rt(s), 0)), pl.BlockSpec((T, GLA_DK), lambda s: (rt(s), 1)),
                  pl.BlockSpec((T, GLA_DV), lambda s: (rt(s), 1)), pl.BlockSpec((T, GLA_DV), lambda s: (rt(s), 2)),
                  pl.BlockSpec((T, FLOW_W), lambda s: (rt(s), (INT_W - FLOW_W) // FLOW_W)),
                  pl.BlockSpec(wfu.shape, lambda s: (0, 0)), pl.BlockSpec(bf.shape, lambda s: (0, 0)),
                  pl.BlockSpec(gn.shape, lambda s: (0, 0)),
                  pl.BlockSpec((G, GLA_DV, GLA_HDK), lambda s: (rt(s), 0, 0)),
                  pl.BlockSpec((None, GLA_DV, GLA_HDK), lambda s: (jnp.maximum(rt(s) * G - 1, 0), 0, 0)),
                  pl.BlockSpec((T, GLA_DV), lambda s: (rt(s), 0)), pl.BlockSpec(memory_space=pl.ANY)],
        out_specs=[pl.BlockSpec((T, 2 * GLA_DK + 2 * GLA_DV), lambda s: (rt(s), 0)),
                   pl.BlockSpec((T, FLOW_W), lambda s: (rt(s), 0)),
                   pl.BlockSpec(wfu.shape, lambda s: (0, 0)), pl.BlockSpec(bf.shape, lambda s: (0, 0)),
                   pl.BlockSpec(gn.shape, lambda s: (0, 0))],
        out_shape=[_sds(dmain.shape, dmain.dtype), _sds((S, FLOW_W), BF16), _sds(wfu.shape), _sds(bf.shape),
                   _sds(gn.shape)],
        scratch=[pltpu.VMEM((GLA_DV, GLA_HDK), F32)], aliases={11: 0},
    )(proj, proj, proj, proj, proj, wfu, bf, gn, states, states, d_ya_in, dmain)


def _pool_counts(tm, i):
    t = (lax.broadcasted_iota(jnp.int32, (tm, POOL_GD), 0) + i * tm + 1).astype(F32)
    return [jnp.minimum(t, float(w)) for w in POOL_WINDOWS]


def _pool_fwd(proj, w_pool, pool_scale, tm=512):
    S = proj.shape[0]
    tm = min(tm, S // 2)
    col = _POOL_COL
    hb = tm // POOL_HALO

    def body(p_ref, halo_ref, wp_ref, sc_ref, mixed_ref, out_ref):
        i = pl.program_id(0)
        p = p_ref[...].astype(F32)
        halo = halo_ref[...].astype(F32) * (i > 0).astype(F32)
        ext = jnp.concatenate([halo, p], axis=0)
        n = tm + POOL_HALO
        sums, acc, k = {}, ext, 1
        while k < POOL_WINDOWS[-1]:
            acc = acc + pltpu.roll(acc, k, axis=0)
            k *= 2
            sums[k] = acc
        cnts = _pool_counts(tm, i)
        mixed, lin = [], []
        for g, w in enumerate(POOL_WINDOWS):
            ls = slice(g * POOL_GD, (g + 1) * POOL_GD)
            m = sums[w][POOL_HALO:n, ls] / cnts[g] - p[:, ls]
            mixed.append(m)
            lin.append(_dot(m, wp_ref[g]))
        mixed_ref[...] = jnp.concatenate(mixed, axis=-1)
        out_ref[...] = (jnp.concatenate(lin, axis=-1) * sc_ref[...]).astype(out_ref.dtype)

    return _pcall(
        body, name="pool_fwd", grid=(S // tm,),
        in_specs=[pl.BlockSpec((tm, POOL_W), lambda i: (i, col)),
                  pl.BlockSpec((POOL_HALO, POOL_W), lambda i: (jnp.maximum(i * hb - 1, 0), col)),
                  pl.BlockSpec(w_pool.shape, lambda i: (0, 0, 0)), pl.BlockSpec(pool_scale.shape, lambda i: (0, 0))],
        out_specs=[pl.BlockSpec((tm, POOL_W), lambda i: (i, 0)), pl.BlockSpec((tm, POOL_W), lambda i: (i, 0))],
        out_shape=[_sds((S, POOL_W)), _sds((S, POOL_W), BF16)])(proj, proj, w_pool, pool_scale)


def _pool_lin_bwd(dout, mixed, w_pool, pool_scale):
    S = dout.shape[0]

    def fn(dout, mixed, wp, sc):
        dlin = dout * sc
        dm, dwp, lin = [], [], []
        for g in range(len(POOL_WINDOWS)):
            ls = slice(g * POOL_GD, (g + 1) * POOL_GD)
            lin.append(_dot(mixed[:, ls], wp[g]))
            dm.append(_dot(dlin[:, ls], wp[g], _NT))
            dwp.append(_dot(mixed[:, ls], dlin[:, ls], _TN))
        dsc = jnp.sum(dout * jnp.concatenate(lin, axis=-1), axis=0, keepdims=True)
        return jnp.concatenate(dm, axis=-1), jnp.concatenate(dwp, axis=0), dsc

    return _rowwise("pool_lin_bwd", fn, [dout, mixed], [w_pool, pool_scale], [_sds((S, POOL_W))],
                    [_sds((len(POOL_WINDOWS) * POOL_GD, POOL_GD)), _sds((1, POOL_W))])


def _pool_win_bwd(dmixed, dmain, tm=512):
    S = dmixed.shape[0]
    tm = min(tm, S // 2)
    nt = S // tm
    hb = tm // POOL_HALO

    def body(dm_ref, halo_ref, dmain_in, dp_ref):
        i = pl.program_id(0)
        dm = dm_ref[...]
        halo = halo_ref[...] * (i < nt - 1).astype(F32)
        cnts = _pool_counts(tm, i)
        cnts_h = [c[:POOL_HALO] for c in _pool_counts(tm, i + 1)]
        r = jnp.concatenate([jnp.concatenate([dm[:, g * POOL_GD:(g + 1) * POOL_GD] / cnts[g] for g in range(4)], axis=-1),
                             jnp.concatenate([halo[:, g * POOL_GD:(g + 1) * POOL_GD] / cnts_h[g] for g in range(4)], axis=-1)],
                            axis=0)
        n = tm + POOL_HALO
        sums, acc, k = {}, r, 1
        while k < POOL_WINDOWS[-1]:
            acc = acc + pltpu.roll(acc, n - k, axis=0)
            k *= 2
            sums[k] = acc
        dp = [sums[w][:tm, g * POOL_GD:(g + 1) * POOL_GD] for g, w in enumerate(POOL_WINDOWS)]
        dp_ref[...] = (jnp.concatenate(dp, axis=-1) - dm).astype(dp_ref.dtype)

    return _pcall(
        body, name="pool_win_bwd", grid=(nt,),
        in_specs=[pl.BlockSpec((tm, POOL_W), lambda i: (i, 0)),
                  pl.BlockSpec((POOL_HALO, POOL_W), lambda i: (jnp.minimum((i + 1) * hb, S // POOL_HALO - 1), 0)),
                  pl.BlockSpec(memory_space=pl.ANY)],
        out_specs=pl.BlockSpec((tm, POOL_W), lambda i: (i, _POOL_COL)),
        out_shape=_sds(dmain.shape, dmain.dtype), aliases={2: 0})(dmixed, dmixed, dmain)


def _xattn_probs(q, kv, h):
    hs = slice(h * XA_HD, (h + 1) * XA_HD)
    s = _dot(q[:, hs], kv[:, hs], _NT) * (XA_HD ** -0.5)
    s = s - jnp.max(s, axis=-1, keepdims=True)
    e = jnp.exp(s)
    return e / jnp.sum(e, axis=-1, keepdims=True)


def _xattn_fwd(proj, kv):
    S = proj.shape[0]

    def fn(q, kv):
        outs = []
        for h in range(XA_HEADS):
            p = _xattn_probs(q, kv, h)
            outs.append(_dot(p, kv[:, XA_W + h * XA_HD:XA_W + (h + 1) * XA_HD]))
        return jnp.concatenate(outs, axis=-1)

    return _rowwise("xattn_fwd", fn, [_Win(proj, XA_W, _XQ_COL)], [kv], [_sds((S, XA_W), BF16)])[0]


def _xattn_bwd(proj, kv, dxa, dmain):
    def fn(q, dxa, kv):
        dqs, dks, dvs = [], [], []
        for h in range(XA_HEADS):
            hs = slice(h * XA_HD, (h + 1) * XA_HD)
            vh = kv[:, XA_W + h * XA_HD:XA_W + (h + 1) * XA_HD]
            p = _xattn_probs(q, kv, h)
            dp = _dot(dxa[:, hs], vh, _NT)
            ds = p * (dp - jnp.sum(p * dp, axis=-1, keepdims=True)) * (XA_HD ** -0.5)
            dqs.append(_dot(ds, kv[:, hs]))
            dks.append(_dot(ds, q[:, hs], _TN))
            dvs.append(_dot(p, dxa[:, hs], _TN))
        return jnp.concatenate(dqs, axis=-1), jnp.concatenate(dks + dvs, axis=-1)

    return _rowwise("xattn_bwd", fn, [_Win(proj, XA_W, _XQ_COL), dxa], [kv], [_Win(dmain, XA_W, _XQ_COL)],
                    [_sds(kv.shape)])


def _mix_fwd(x1, h, mem, W, g_next, on_proj=None, late_weights=None):
    S = x1.shape[0]
    M = mem.shape[0]
    tm = min(512, S)
    def proj_body(h_ref, w_ref, o_ref):
        hh = h_ref[...]
        for j in range(INT_NBLK):
            o_ref[:, INT_BLK * j:INT_BLK * (j + 1)] = _dot(hh, w_ref[INT_BLK * j:INT_BLK * (j + 1), :], _NT).astype(BF16)

    proj = _pcall(
        proj_body, name="mix_proj", grid=(S // tm,),
        in_specs=[pl.BlockSpec((tm, D_MODEL), lambda i: (i, 0)),
                  pl.BlockSpec((INT_W, D_MODEL), lambda i: (0, 0), pipeline_mode=pl.Buffered(1))],
        out_specs=pl.BlockSpec((tm, INT_W), lambda i: (i, 0)), out_shape=_sds((S, INT_W), BF16))(h, W["w_int_t"])
    ya_in, states = _gla_fwd(proj, W["w_fu_pad"], W["b_f"], W["gla_norm_g"],
                             on_proj(proj) if on_proj is not None else [])
    mixed, pool_out = _pool_fwd(proj, W["w_pool"], W["pool_scale"])
    mem_n = _rowwise("mem_norm", lambda m, g: _rms(m, g), [mem], [W["mem_norm_g"]], [_sds((M, D_MODEL), BF16)])[0]
    if late_weights is not None:
        W = {**W, **late_weights([pool_out])}
    kv = _mm_nn("mem_kv", mem_n, W["w_mem_kv"])
    xa = _xattn_fwd(proj, kv)
    def out_fn(gates, ya_in, pool_out, xa, x1, wa, wb, wc, wo, g, gn):
        ya, yb, yc = (_dot(t, w).astype(BF16) for t, w in ((ya_in, wa), (pool_out, wb), (xa, wc)))
        merged = _merge(gates, ya, yb, yc).astype(BF16)
        y = _dot(merged, wo)
        x2 = x1 + _rms(y, g)
        return ya, yb, yc, merged, y, x2, _rms(x2, gn)

    ya, yb, yc, merged, y, x2, h_next = _rowwise(
        "mix_out", out_fn, [_Win(proj, 3 * D_MODEL, 1), ya_in, pool_out, xa, x1],
        [W["w_up_gla"], W["w_up_pool"], W["w_up_xattn"], W["w_o"], W["mix_post_g"], g_next],
        [_sds((S, D_MODEL), BF16)] * 5 + [_sds((S, D_MODEL)), _sds((S, D_MODEL), BF16)], tm=256)
    return x2, h_next, W, (h, proj, states, ya_in, mixed, pool_out, mem_n, kv, xa, ya, yb, yc, merged, y)


def _mix_bwd(dx2, x1, mem, W, saved, on_grads=None):
    h, proj, states, ya_in, mixed, pool_out, mem_n, kv, xa, ya, yb, yc, merged, y = saved
    S = x1.shape[0]

    def out_bwd(y, dx2, gates, ya, yb, yc, g, wo, wa, wb, wc):
        dy, dg = _rms_bwd(y.astype(F32), g, dx2)
        dy = dy.astype(BF16)
        _, vjp = jax.vjp(_merge, gates, ya, yb, yc)
        dgates, dya, dyb, dyc = vjp(_dot(dy, wo, _NT))
        return (dy, dgates, dya, dyb, dyc, _dot(dya, wa, _NT), _dot(dyb, wb, _NT), _dot(dyc, wc, _NT), dg)

    dmain = lax.empty((S, INT_MAIN), BF16)
    dy, dmain, dya, dyb, dyc, d_ya_in, d_pool_out, d_xa, d_mix_post_g = _rowwise(
        "mix_out_bwd", out_bwd, [y, dx2, _Win(proj, 3 * D_MODEL, 1), ya, yb, yc],
        [W["mix_post_g"], W["w_o"], W["w_up_gla"], W["w_up_pool"], W["w_up_xattn"]],
        [_sds((S, D_MODEL), BF16), _Win(dmain, 3 * D_MODEL, 1)] + [_sds((S, D_MODEL), BF16)] * 4
        + [_sds((S, POOL_W), BF16), _sds((S, XA_W), BF16)], [_sds((1, D_MODEL))], tm=256)
    d_w_o = _mm_tn("d_w_o", merged, dy)
    d_w_up_gla = _mm_tn("d_w_up_gla", ya_in, dya)
    d_w_up_pool = _mm_tn("d_w_up_pool", pool_out, dyb)
    d_w_up_xattn = _mm_tn("d_w_up_xattn", xa, dyc)

    dmain, dflow, d_wfu_pad, d_b_f, d_gla_norm_g = _gla_bwd(proj, W["w_fu_pad"], W["b_f"], W["gla_norm_g"], states,
                                                            d_ya_in, dmain)
    dmixed, d_w_pool, d_pool_scale = _pool_lin_bwd(d_pool_out, mixed, W["w_pool"], W["pool_scale"])
    dmain = _pool_win_bwd(dmixed, dmain)
    dmain, dkv = _xattn_bwd(proj, kv, d_xa, dmain)
    d_w_mem_kv = _mm_tn("d_w_mem_kv", mem_n, dkv)
    dmem_n = _mm_nt("d_mem_n", dkv, W["w_mem_kv"])
    d_mem_norm_g = _rowwise("mem_norm_bwd", lambda m, d, g: _rms_bwd(m, g, d)[1], [mem, dmem_n], [W["mem_norm_g"]], [],
                            [_sds((1, D_MODEL))])[0]
    d_w_int_t = (_mm_tn("d_w_int", dmain, h, ts=1024, tn_a=INT_MAIN // 4), _mm_tn("d_w_flow", dflow, h, ts=2048))
    grads = dict(
        w_int_t=d_w_int_t, w_fu_pad=d_wfu_pad, b_f=d_b_f, gla_norm_g=d_gla_norm_g, w_pool=d_w_pool,
        pool_scale=d_pool_scale, mem_norm_g=d_mem_norm_g, w_mem_kv=d_w_mem_kv, w_up_gla=d_w_up_gla,
        w_up_pool=d_w_up_pool, w_up_xattn=d_w_up_xattn, w_o=d_w_o, mix_post_g=d_mix_post_g)
    deps = [on_grads(grads)] if on_grads is not None else []
    dx1, grads["mix_pre_g"] = _dh_pre_norm_bwd("d_mix_h", [dmain, dflow], W["w_int_t"], x1, dx2, W["mix_pre_g"], deps)
    return dx1, grads


def _mesh_pos():
    x, y, c = lax.axis_index("x"), lax.axis_index("y"), lax.axis_index("c")
    return x, y, c, 4 * x + 2 * y + c


def _peer(x, y, c, r):
    px = 1 - x if r & 4 else x
    py = 1 - y if r & 2 else y
    pc = 1 - c if r & 1 else c
    return (px, py, pc), 4 * px + 2 * py + pc


_ALL_PEERS = tuple(range(1, N_DEV))
_SIBLING = 1
_SAME_CORE = (2, 4, 6)


def _dev_slot(ref, dev):
    return ref.at[dev]


def _zero_pad_rows(land):
    pad = FF_PAD - FF_BLK

    def body(land_in, o_ref):
        o_ref[...] = jnp.zeros_like(o_ref)

    return _pcall(body, name="zero_pad_rows", grid=(land.shape[0],), in_specs=[pl.BlockSpec(memory_space=pl.ANY)],
                  out_specs=pl.BlockSpec((None, pad, D_MODEL), lambda j: (j, FF_BLK // pad, 0)),
                  out_shape=_sds(land.shape, land.dtype), aliases={0: 0})(land)


class _Plan:
    def __init__(self, scatter, slots=None, shapes=None):
        self.scatter, self.slots, self.shapes = scatter, slots or {}, shapes or {}

    def src(self, srcs, a, dev):
        return self.slots.get(a, _dev_slot)(srcs[a], dev) if self.scatter else srcs[a]

    def dst(self, lands, a, dev):
        return lands[a].at[dev] if self.scatter else self.slots.get(a, _dev_slot)(lands[a], dev)

    def landing_zones(self, arrays):
        lands = []
        for a, arr in enumerate(arrays):
            if self.scatter:
                lands.append(lax.empty((N_DEV,) + tuple(self.shapes.get(a, arr.shape[1:])), arr.dtype))
            elif a in self.shapes:
                lands.append(_zero_pad_rows(lax.empty(self.shapes[a], arr.dtype)))
            else:
                lands.append(lax.empty((N_DEV,) + arr.shape, arr.dtype))
        return lands


_GATHER, _SCATTER = _Plan(False), _Plan(True)


def _peer_copies(srcs, lands, send_sems, recv_sems, plan, peers=_ALL_PEERS):
    x, y, c, me = _mesh_pos()
    cps = []
    for r in peers:
        pos, peer = _peer(x, y, c, r)
        for a in range(len(srcs)):
            k = a * (N_DEV - 1) + r - 1
            cps.append(pltpu.make_async_remote_copy(
                src_ref=plan.src(srcs, a, peer), dst_ref=plan.dst(lands, a, me),
                send_sem=send_sems.at[k], recv_sem=recv_sems.at[k], device_id=pos,
                device_id_type=pl.DeviceIdType.MESH))
    return cps


_HBM = pl.BlockSpec(memory_space=pltpu.HBM)
_SEM = pl.BlockSpec(memory_space=pltpu.SEMAPHORE)
_EFFECT = pltpu.SideEffectType.DATAFLOW_SIDE_EFFECTING


def _own_copies(srcs, lands, own_sems, plan):
    me = _mesh_pos()[3]
    return [pltpu.make_async_copy(plan.src(srcs, a, me), plan.dst(lands, a, me), own_sems.at[a])
            for a in range(len(srcs))]


def _exchange_start(name, arrays, plan):
    n = len(arrays)
    lands = plan.landing_zones(arrays)
    n_sem = n * (N_DEV - 1)

    def body(*refs):
        srcs, lands_ = refs[:n], refs[n:2 * n]
        send_sems, recv_sems, own_sems = refs[2 * n:2 * n + 3]
        token = refs[-1]
        for cp in _peer_copies(srcs, lands_, send_sems, recv_sems, plan) + _own_copies(srcs, lands_, own_sems, plan):
            cp.start()
        token[...] = jnp.zeros_like(token)

    hbm = lambda a: pltpu.HBM(a.shape, a.dtype)
    res = pl.pallas_call(
        body, name=name,
        out_shape=(pltpu.SemaphoreType.DMA((n_sem,)), pltpu.SemaphoreType.DMA((n_sem,)), pltpu.SemaphoreType.DMA((n,)),
                   *[hbm(a) for a in arrays], *[hbm(a) for a in lands], _sds((8, 128))),
        in_specs=[_HBM] * (2 * n),
        out_specs=(_SEM, _SEM, _SEM, *[_HBM] * (2 * n), pl.BlockSpec(memory_space=pltpu.VMEM)),
        input_output_aliases={i: 3 + i for i in range(2 * n)},
        compiler_params=pltpu.CompilerParams(has_side_effects=_EFFECT),
    )(*[pltpu.with_memory_space_constraint(a, pltpu.HBM) for a in list(arrays) + lands])
    return (res[:3], res[3:3 + n], res[3 + n:3 + 2 * n], plan), res[-1]


def _exchange_wait(name, state, after):
    sems, srcs, lands, plan = state
    n = len(srcs)

    def body(*refs):
        srcs_, lands_ = refs[:n], refs[n:2 * n]
        send_sems, recv_sems, own_sems = refs[2 * n:2 * n + 3]
        for cp in _peer_copies(srcs_, lands_, send_sems, recv_sems, plan):
            cp.wait_send()
            cp.wait_recv()
        for cp in _own_copies(srcs_, lands_, own_sems, plan):
            cp.wait()

    hbm = lambda a: pltpu.HBM(a.shape, a.dtype)
    res = pl.pallas_call(
        body, name=name, out_shape=tuple(hbm(a) for a in list(srcs) + list(lands)),
        in_specs=[_HBM] * (2 * n) + [_SEM] * 3 + [pl.BlockSpec(memory_space=pl.ANY)] * len(after),
        out_specs=tuple([_HBM] * (2 * n)), input_output_aliases={i: i for i in range(2 * n)},
        compiler_params=pltpu.CompilerParams(has_side_effects=_EFFECT),
    )(*srcs, *lands, *sems, *after)
    return res[n:]


def _gather_start(name, arrays, plan, after):
    n = len(arrays)
    lands = plan.landing_zones(arrays)
    n_sem = n * (N_DEV - 1)

    def body(*refs):
        srcs, lands_ = refs[:n], refs[n:2 * n]
        send_sems, recv_sems, own_sems = refs[2 * n + len(after):2 * n + len(after) + 3]
        token = refs[-1]
        for cp in (_peer_copies(srcs, lands_, send_sems, recv_sems, plan, (_SIBLING,) + _SAME_CORE)
                   + _own_copies(srcs, lands_, own_sems, plan)):
            cp.start()
        token[...] = jnp.zeros_like(token)

    hbm = lambda a: pltpu.HBM(a.shape, a.dtype)
    res = pl.pallas_call(
        body, name=name,
        out_shape=(pltpu.SemaphoreType.DMA((n_sem,)), pltpu.SemaphoreType.DMA((n_sem,)), pltpu.SemaphoreType.DMA((n,)),
                   *[hbm(a) for a in arrays], *[hbm(a) for a in lands], _sds((8, 128))),
        in_specs=[_HBM] * (2 * n) + [pl.BlockSpec(memory_space=pl.ANY)] * len(after),
        out_specs=(_SEM, _SEM, _SEM, *[_HBM] * (2 * n), pl.BlockSpec(memory_space=pltpu.VMEM)),
        input_output_aliases={i: 3 + i for i in range(2 * n)},
        compiler_params=pltpu.CompilerParams(has_side_effects=_EFFECT),
    )(*[pltpu.with_memory_space_constraint(a, pltpu.HBM) for a in list(arrays) + lands], *after)
    return (res[:3], res[3:3 + n], res[3 + n:3 + 2 * n], plan), res[-1]


def _pass_on_copies(lands, send_sems, recv_sems, plan):
    x, y, c, _ = _mesh_pos()
    sibling = _peer(x, y, c, _SIBLING)[0]
    cps = []
    for i, r in enumerate(_SAME_CORE):
        owner = _peer(x, y, c, r)[1]
        for a in range(len(lands)):
            k = a * len(_SAME_CORE) + i
            cps.append(pltpu.make_async_remote_copy(
                src_ref=plan.dst(lands, a, owner), dst_ref=plan.dst(lands, a, owner), send_sem=send_sems.at[k],
                recv_sem=recv_sems.at[k], device_id=sibling, device_id_type=pl.DeviceIdType.MESH))
    return cps


def _gather_pass_on(name, state, after):
    sems, srcs, lands, plan = state
    n = len(srcs)
    n_sem = n * len(_SAME_CORE)

    def body(*refs):
        srcs_, lands_ = refs[:n], refs[n:2 * n]
        send_sems, recv_sems = refs[2 * n], refs[2 * n + 1]
        on_send, on_recv = refs[2 * n + 3 + len(after)], refs[2 * n + 4 + len(after)]
        token = refs[-1]
        arrivals = _peer_copies(srcs_, lands_, send_sems, recv_sems, plan, _SAME_CORE)
        for arrived, on in zip(arrivals, _pass_on_copies(lands_, on_send, on_recv, plan)):
            arrived.wait_recv()
            on.start()
        token[...] = jnp.zeros_like(token)

    hbm = lambda a: pltpu.HBM(a.shape, a.dtype)
    res = pl.pallas_call(
        body, name=name,
        out_shape=(pltpu.SemaphoreType.DMA((n_sem,)), pltpu.SemaphoreType.DMA((n_sem,)),
                   *[hbm(a) for a in list(srcs) + list(lands)], _sds((8, 128))),
        in_specs=[_HBM] * (2 * n) + [_SEM] * 3 + [pl.BlockSpec(memory_space=pl.ANY)] * len(after),
        out_specs=(_SEM, _SEM, *[_HBM] * (2 * n), pl.BlockSpec(memory_space=pltpu.VMEM)),
        input_output_aliases={i: 2 + i for i in range(2 * n)},
        compiler_params=pltpu.CompilerParams(has_side_effects=_EFFECT),
    )(*srcs, *lands, *sems, *after)
    return (sems, res[:2], res[2:2 + n], res[2 + n:2 + 2 * n], plan), res[-1]


def _gather_wait(name, state, after):
    sems, on_sems, srcs, lands, plan = state
    n = len(srcs)

    def body(*refs):
        srcs_, lands_ = refs[:n], refs[n:2 * n]
        send_sems, recv_sems, own_sems, on_send, on_recv = refs[2 * n:2 * n + 5]
        for cp in _peer_copies(srcs_, lands_, send_sems, recv_sems, plan, (_SIBLING,)):
            cp.wait_recv()
        for cp in _peer_copies(srcs_, lands_, send_sems, recv_sems, plan, (_SIBLING,) + _SAME_CORE):
            cp.wait_send()
        for cp in _own_copies(srcs_, lands_, own_sems, plan):
            cp.wait()
        for cp in _pass_on_copies(lands_, on_send, on_recv, plan):
            cp.wait_send()
            cp.wait_recv()

    hbm = lambda a: pltpu.HBM(a.shape, a.dtype)
    res = pl.pallas_call(
        body, name=name, out_shape=tuple(hbm(a) for a in list(srcs) + list(lands)),
        in_specs=[_HBM] * (2 * n) + [_SEM] * 5 + [pl.BlockSpec(memory_space=pl.ANY)] * len(after),
        out_specs=tuple([_HBM] * (2 * n)), input_output_aliases={i: i for i in range(2 * n)},
        compiler_params=pltpu.CompilerParams(has_side_effects=_EFFECT),
    )(*srcs, *lands, *sems, *on_sems, *after)
    return res[n:]


def _sum_parts(name, recv):
    def body(r_ref, o_ref):
        s = r_ref[0]
        for j in range(1, N_DEV):
            s = s + r_ref[j]
        o_ref[...] = s

    return pl.pallas_call(body, name=name, out_shape=_sds(recv.shape[1:], recv.dtype))(recv)


def _adam(name, recv, w, m, v):
    shape = w.shape
    R, C = shape
    tr = R
    while N_DEV * tr * C * 4 > 6 * 1024 * 1024 and tr % 32 == 0:
        tr //= 2

    def body(recv_ref, w_ref, m_ref, v_ref, g_ref, d_ref, m2_ref, v2_ref):
        g = recv_ref[0].astype(F32)
        for j in range(1, N_DEV):
            g = g + recv_ref[j].astype(F32)
        w_, m_, v_ = w_ref[...], m_ref[...], v_ref[...]
        m2 = ADAM_B1 * m_ + (1.0 - ADAM_B1) * g
        v2 = ADAM_B2 * v_ + (1.0 - ADAM_B2) * (g * g)
        m_hat = m2 / (1.0 - ADAM_B1 ** ADAM_STEP)
        v_hat = v2 / (1.0 - ADAM_B2 ** ADAM_STEP)
        g_ref[...] = g
        d_ref[...] = -ADAM_LR * (m_hat / (jnp.sqrt(v_hat) + ADAM_EPS) + ADAM_WD * w_)
        m2_ref[...] = m2
        v2_ref[...] = v2

    blk = pl.BlockSpec((tr, C), lambda i: (i, 0))
    return _pcall(body, name=name, grid=(R // tr,),
                  in_specs=[pl.BlockSpec((N_DEV, tr, C), lambda i: (0, i, 0)), blk, blk, blk],
                  out_specs=[blk] * 4, out_shape=[_sds(shape)] * 4)(recv, w, m, v)


_NAMES = ['ffn1_pre_g', 'ffn1_w_in', 'ffn1_w_out', 'ffn1_post_g', 'mix_pre_g', 'w_in', 'w_fu', 'b_f', 'gla_norm_g',
          'w_pool', 'pool_scale', 'mem_norm_g', 'w_mem_kv', 'w_up_gla', 'w_up_pool', 'w_up_xattn', 'w_o', 'mix_post_g',
          'ffn2_pre_g', 'ffn2_w_in', 'ffn2_w_out', 'ffn2_post_g', 'final_g']
_SHARDED = ['ffn1_w_in', 'ffn1_w_out', 'w_in', 'w_fu', 'w_mem_kv', 'w_up_gla', 'w_up_pool', 'w_up_xattn', 'w_o',
            'ffn2_w_in', 'ffn2_w_out']
_COL_SHARDED = ['w_up_pool', 'w_up_xattn']


def _cols_to_full(g):
    return jnp.transpose(g, (1, 0, 2)).reshape(g.shape[1], N_DEV * g.shape[2])


def _full_to_cols(f):
    R, C = f.shape
    return jnp.transpose(f.reshape(R, N_DEV, C // N_DEV), (1, 0, 2))


def _to_internal(w_in_t):
    o = 0
    parts = []
    for s in IN_SPLITS:
        parts.append(w_in_t[o:o + s])
        o += s
    q, k, v, g_out, f_low, p_in, xq, gates = parts
    f_low = jnp.pad(f_low, ((0, FLOW_W - GATE_RANK), (0, 0)))
    return jnp.concatenate([q, k, v, g_out, gates, p_in, xq, f_low], axis=0)


def _from_internal(d, d_flow):
    q, k, v, g_out = d[0:512], d[512:1024], d[1024:2048], d[2048:3072]
    gates, p_in, xq = d[3072:6144], d[6144:6656], d[6656:7168]
    return jnp.concatenate([q, k, v, g_out, d_flow[:GATE_RANK], p_in, xq, gates], axis=0)


_FFN_IN = ('ffn1_w_in', 'ffn2_w_in')
_FFN_OUT = ('ffn1_w_out', 'ffn2_w_out')
_GATHERS = {"ffn1_in": ['ffn1_w_in'], "ffn1_out": ['ffn1_w_out'],
            "mix_in": ['w_in', 'w_fu'], "mix_rest": ['w_mem_kv', 'w_up_gla', 'w_up_pool', 'w_up_xattn', 'w_o'],
            "ffn2": ['ffn2_w_in', 'ffn2_w_out']}
_MIX = _GATHERS["mix_in"] + _GATHERS["mix_rest"]


def _ffn_in_slot(ref, d):
    return ref.at[2 * (d % N_FF_BLK) + d // N_FF_BLK, pl.ds(0, FF_BLK)]


def _ffn_out_slot(ref, d):
    rows = FF_BLK // 2
    return ref.at[d // 2, pl.ds(pl.multiple_of((d % 2) * rows, rows), rows)]


def _ffn_plan(names, scatter):
    slots, shapes = {}, {}
    for a, n in enumerate(names):
        if n in _FFN_IN:
            slots[a] = _ffn_in_slot
            shapes[a] = (FF_BLK, D_MODEL) if scatter else (N_DEV, FF_PAD, D_MODEL)
        elif n in _FFN_OUT:
            slots[a] = _ffn_out_slot
            shapes[a] = (FF_BLK // 2, D_MODEL) if scatter else (N_FF_BLK, FF_PAD, D_MODEL)
    return _Plan(scatter, slots, shapes)


def _rows_to_full(g):
    return g.reshape(N_DEV * g.shape[1], g.shape[2])


def _mix_weights(gathered):
    W = {}
    for n, g in gathered.items():
        if n == "w_in":
            W["w_int_t"] = _to_internal(g.reshape(IN_WIDTH, D_MODEL))
        elif n == "w_fu":
            W["w_fu_pad"] = jnp.pad(_cols_to_full(g), ((0, FLOW_W - GATE_RANK), (0, 0)))
        else:
            W[n] = _cols_to_full(g) if n in _COL_SHARDED else _rows_to_full(g)
    return W


def _mix_chunks(G, n):
    if n == "w_in":
        return _from_internal(*G["w_int_t"]).reshape(N_DEV, IN_SHARD, D_MODEL)
    if n == "w_fu":
        return _full_to_cols(G["w_fu_pad"][:GATE_RANK].astype(BF16))
    if n in _COL_SHARDED:
        return _full_to_cols(G[n])
    return G[n].reshape(N_DEV, G[n].shape[0] // N_DEV, G[n].shape[1])


def _step(x, mem, tgt, P, Mo, Vo):
    def native(n, a):
        return jnp.swapaxes(a, 0, 1) if n in _FFN_IN + ("w_in",) else a

    P, Mo, Vo = ({n: native(n, a) for n, a in d.items()} for d in (P, Mo, Vo))
    small = {n: P[n] for n in _NAMES if n not in _SHARDED}

    gather, tokens = {}, []
    for grp, names in _GATHERS.items():
        gather[grp], tok = _gather_start("gather_" + grp, [P[n].astype(BF16) for n in names], _ffn_plan(names, False),
                                         tokens[-1:])
        tokens.append(tok)

    def pass_on(grp, after):
        return _gather_pass_on("gather_" + grp + "_on", gather[grp], after)

    def gathered(grp, after):
        return _gather_wait("gather_" + grp + "_wait", pass_on(grp, after)[0], [])

    def ffn_wo(lands):
        return lambda act: lands[0].reshape(N_FF_BLK * FF_PAD, D_MODEL)

    h1 = _pre_norm("ffn1_pre", x, small["ffn1_pre_g"], tokens[:1])
    w1t = gathered("ffn1_in", tokens[-1:] + [h1])[0].reshape(N_DEV * FF_PAD, D_MODEL)
    (x1, hm), w1o, sv1 = _ffn_fwd("ffn1", x, h1, w1t, small["ffn1_post_g"],
                                  lambda act: ffn_wo(gathered("ffn1_out", [act]))(act), small["mix_pre_g"])

    Wm = {**small, **_mix_weights(dict(zip(_GATHERS["mix_in"], gathered("mix_in", [x1]))))}
    passed = {}

    def pass_on_later(proj):
        tokens = []
        for grp in ("mix_rest", "ffn2"):
            passed[grp], tok = pass_on(grp, [proj])
            tokens.append(tok)
        return tokens

    def mix_rest(after):
        names = _GATHERS["mix_rest"]
        return _mix_weights(dict(zip(names, _gather_wait("gather_mix_rest_wait", passed["mix_rest"], after))))

    x2, h2, Wm, svm = _mix_fwd(x1, hm, mem, Wm, small["ffn2_pre_g"], pass_on_later, mix_rest)
    w2t, w2o = _gather_wait("gather_ffn2_wait", passed["ffn2"], [x2])
    w2t = w2t.reshape(N_DEV * FF_PAD, D_MODEL)
    (loss, dx3, d_final_g), w2o, sv2 = _ffn_fwd("ffn2", x2, h2, w2t, small["ffn2_post_g"], ffn_wo([w2o]),
                                                small["final_g"], tgt)

    G = dict(final_g=d_final_g)
    scat = {}

    def start(grp, names, arrays):
        scat[grp] = names, _exchange_start("scatter_" + grp, arrays, _ffn_plan(names, True))
        return scat[grp][1][1]

    def ffn_starts(tag):
        return (lambda dwo: start(tag + "_out", [tag + "_w_out"], [dwo.reshape(N_FF_BLK, FF_PAD, D_MODEL)]),
                lambda dwt: start(tag + "_in", [tag + "_w_in"], [dwt.reshape(N_DEV, FF_PAD, D_MODEL)]))

    dx2, G["ffn2_pre_g"], G["ffn2_post_g"] = _ffn_bwd(
        "ffn2", dx3, x2, small["ffn2_pre_g"], w2t, w2o, small["ffn2_post_g"], sv2, *ffn_starts("ffn2"))
    dx1, Gm = _mix_bwd(dx2, x1, mem, Wm, svm,
                       lambda Gm: start("mix", _MIX, [_mix_chunks(Gm, n) for n in _MIX]))
    G.update(Gm)
    dx, G["ffn1_pre_g"], G["ffn1_post_g"] = _ffn_bwd(
        "ffn1", dx1, x, small["ffn1_pre_g"], w1t, w1o, small["ffn1_post_g"], sv1, *ffn_starts("ffn1"))

    scat["small"] = list(small) + ["loss"], _exchange_start("gather_small_grads", [G[n] for n in small] + [loss],
                                                            _GATHER)
    recv, outs = {}, {}
    done = [dx]
    for grp in ["ffn2_out", "ffn2_in", "mix", "ffn1_out", "ffn1_in", "small"]:
        names, (state, _) = scat[grp]
        recv.update(zip(names, _exchange_wait(("gather_" if grp == "small" else "scatter_") + grp + "_wait", state,
                                              done)))
        for n in names:
            if n == "loss":
                loss = _sum_parts("loss_sum", recv[n])[0, 0]
                continue
            shp = P[n].shape
            shp2 = shp if len(shp) == 2 else (shp[0] * shp[1], shp[2])
            res = _adam("adam_" + n, recv[n].reshape((N_DEV,) + shp2), P[n].reshape(shp2), Mo[n].reshape(shp2),
                        Vo[n].reshape(shp2))
            outs[n] = [native(n, t.reshape(shp)) for t in res]
            done.append(res[-1])
    return loss, dx, outs


def kernel(x, mem, ffn1_pre_g, ffn1_w_in, ffn1_w_out, ffn1_post_g, mix_pre_g, w_in, w_fu, b_f, gla_norm_g, w_pool, pool_scale, mem_norm_g, w_mem_kv, w_up_gla, w_up_pool, w_up_xattn, w_o, mix_post_g, ffn2_pre_g, ffn2_w_in, ffn2_w_out, ffn2_post_g, final_g, loss_target, m_ffn1_pre_g, m_ffn1_w_in, m_ffn1_w_out, m_ffn1_post_g, m_mix_pre_g, m_w_in, m_w_fu, m_b_f, m_gla_norm_g, m_w_pool, m_pool_scale, m_mem_norm_g, m_w_mem_kv, m_w_up_gla, m_w_up_pool, m_w_up_xattn, m_w_o, m_mix_post_g, m_ffn2_pre_g, m_ffn2_w_in, m_ffn2_w_out, m_ffn2_post_g, m_final_g, v_ffn1_pre_g, v_ffn1_w_in, v_ffn1_w_out, v_ffn1_post_g, v_mix_pre_g, v_w_in, v_w_fu, v_b_f, v_gla_norm_g, v_w_pool, v_pool_scale, v_mem_norm_g, v_w_mem_kv, v_w_up_gla, v_w_up_pool, v_w_up_xattn, v_w_o, v_mix_post_g, v_ffn2_pre_g, v_ffn2_w_in, v_ffn2_w_out, v_ffn2_post_g, v_final_g):
    params = [ffn1_pre_g, ffn1_w_in, ffn1_w_out, ffn1_post_g, mix_pre_g, w_in, w_fu, b_f, gla_norm_g, w_pool, pool_scale, mem_norm_g, w_mem_kv, w_up_gla, w_up_pool, w_up_xattn, w_o, mix_post_g, ffn2_pre_g, ffn2_w_in, ffn2_w_out, ffn2_post_g, final_g]
    moms = [m_ffn1_pre_g, m_ffn1_w_in, m_ffn1_w_out, m_ffn1_post_g, m_mix_pre_g, m_w_in, m_w_fu, m_b_f, m_gla_norm_g, m_w_pool, m_pool_scale, m_mem_norm_g, m_w_mem_kv, m_w_up_gla, m_w_up_pool, m_w_up_xattn, m_w_o, m_mix_post_g, m_ffn2_pre_g, m_ffn2_w_in, m_ffn2_w_out, m_ffn2_post_g, m_final_g]
    vars_ = [v_ffn1_pre_g, v_ffn1_w_in, v_ffn1_w_out, v_ffn1_post_g, v_mix_pre_g, v_w_in, v_w_fu, v_b_f, v_gla_norm_g, v_w_pool, v_pool_scale, v_mem_norm_g, v_w_mem_kv, v_w_up_gla, v_w_up_pool, v_w_up_xattn, v_w_o, v_mix_post_g, v_ffn2_pre_g, v_ffn2_w_in, v_ffn2_w_out, v_ffn2_post_g, v_final_g]
    P = {n: a[0] if a.ndim > 2 else a for n, a in zip(_NAMES, params)}
    Mo = {n: a[0] if a.ndim > 2 else a for n, a in zip(_NAMES, moms)}
    Vo = {n: a[0] if a.ndim > 2 else a for n, a in zip(_NAMES, vars_)}
    loss, dx, outs = _step(x[0], mem[0], loss_target[0], P, Mo, Vo)
    out = [loss, dx[None]]
    for kind in range(4):
        for n, p in zip(_NAMES, params):
            out.append(outs[n][kind].reshape(p.shape))
    return tuple(out)
```

```python
import functools

import jax
import jax.numpy as jnp
from jax import lax
from jax.experimental import pallas as pl
from jax.experimental.pallas import tpu as pltpu

F32 = jnp.float32
BF16 = jnp.bfloat16

N_DEV = 8
D_MODEL = 1024
D_FF = 2816
FF_BLK = 2 * D_FF // N_DEV
N_FF_BLK = D_FF // FF_BLK
FF_PAD = 768
CHUNK = 64
GLA_HEADS = 4
GLA_DK = 512
GLA_DV = 1024
GLA_HDK = 128
GLA_HDV = 256
GATE_RANK = 16
GATE_TEMP = 16.0
POOL_WINDOWS = (2, 4, 8, 16)
POOL_W = 512
POOL_GD = 128
POOL_HALO = 16
XA_HEADS = 4
XA_HD = 128
XA_W = 512
EPS = 1e-6
IN_SPLITS = (GLA_DK, GLA_DK, GLA_DV, GLA_DV, GATE_RANK, POOL_W, XA_W, 3 * D_MODEL)
IN_WIDTH = sum(IN_SPLITS)
IN_SHARD = IN_WIDTH // N_DEV
INT_W = 3072 + 3072 + 1024 + 128
INT_NBLK = 3
INT_BLK = INT_W // INT_NBLK
FLOW_W = 128
INT_MAIN = INT_W - FLOW_W
_POOL_COL = (3072 + 3072) // POOL_W
_XQ_COL = (3072 + 3072 + POOL_W) // XA_W

ADAM_LR = 0.001
ADAM_B1 = 0.9
ADAM_B2 = 0.999
ADAM_EPS = 1e-08
ADAM_WD = 0.01
ADAM_STEP = 10

VMEM_LIMIT = 56 * 1024 * 1024

_NN = (((1,), (0,)), ((), ()))
_NT = (((1,), (1,)), ((), ()))
_TN = (((0,), (0,)), ((), ()))


def _pcall(body, *, name, grid, in_specs, out_specs, out_shape, scratch=(), aliases=None):
    return pl.pallas_call(
        body, name=name, grid=grid, in_specs=in_specs, out_specs=out_specs, out_shape=out_shape,
        scratch_shapes=list(scratch), input_output_aliases=aliases or {},
        compiler_params=pltpu.CompilerParams(dimension_semantics=("arbitrary",) * len(grid),
                                             vmem_limit_bytes=VMEM_LIMIT))


def _weight_blocks(w_hbm, w_vmem, sems):
    nblk = sems.shape[0]
    rows = w_vmem.shape[0] // nblk
    first = pl.program_id(0) == 0
    copies = [pltpu.make_async_copy(w_hbm.at[pl.ds(j * rows, rows)], w_vmem.at[pl.ds(j * rows, rows)], sems.at[j])
              for j in range(nblk)]

    @pl.when(first)
    def _():
        for cp in copies:
            cp.start()

    def block(j):
        @pl.when(first)
        def _():
            copies[j].wait()

        return w_vmem[j * rows:(j + 1) * rows, :]

    return block


def _dot(a, b, dims=_NN):
    return lax.dot_general(a.astype(BF16), b.astype(BF16), dims, preferred_element_type=F32)


def _mm(name, a, b, *, grid, a_blk, a_map, b_blk, b_map, o_shape, o_blk, o_map, dims, out_dtype=F32, deps=()):
    nk = grid[2]

    def body(a_ref, b_ref, *rest):
        o_ref, scr = rest[len(deps)], rest[len(deps) + 1:]
        p = _dot(a_ref[...], b_ref[...], dims)
        if nk == 1:
            o_ref[...] = p.astype(o_ref.dtype)
        else:
            acc = scr[0]
            k = pl.program_id(2)

            @pl.when(k == 0)
            def _():
                acc[...] = p

            @pl.when(k > 0)
            def _():
                acc[...] += p

            @pl.when(k == nk - 1)
            def _():
                o_ref[...] = acc[...].astype(o_ref.dtype)

    acc_shape = tuple(d for d in o_blk if d is not None)
    return _pcall(body, name=name, grid=grid,
                  in_specs=[pl.BlockSpec(a_blk, a_map), pl.BlockSpec(b_blk, b_map)]
                  + [pl.BlockSpec(memory_space=pl.ANY)] * len(deps),
                  out_specs=pl.BlockSpec(o_blk, o_map),
                  out_shape=jax.ShapeDtypeStruct(o_shape, out_dtype),
                  scratch=[pltpu.VMEM(acc_shape, F32)] if nk > 1 else [])(a, b, *deps)


def _mm_nn(name, a, b, out_dtype=F32, tm=1024, tn=None):
    M, K = a.shape
    N = b.shape[1]
    tm, tn = min(tm, M), (tn or N)
    return _mm(name, a, b, grid=(N // tn, M // tm, 1), a_blk=(tm, K), a_map=lambda j, i, k: (i, 0),
               b_blk=(K, tn), b_map=lambda j, i, k: (0, j), o_shape=(M, N), o_blk=(tm, tn),
               o_map=lambda j, i, k: (i, j), dims=_NN, out_dtype=out_dtype)


def _mm_nt(name, a, b, out_dtype=F32, tm=1024):
    M, K = a.shape
    N = b.shape[0]
    tm = min(tm, M)
    return _mm(name, a, b, grid=(1, M // tm, 1), a_blk=(tm, K), a_map=lambda j, i, k: (i, 0),
               b_blk=(N, K), b_map=lambda j, i, k: (0, 0), o_shape=(M, N), o_blk=(tm, N),
               o_map=lambda j, i, k: (i, 0), dims=_NT, out_dtype=out_dtype)


def _mm_tn(name, a, b, out_dtype=BF16, ts=512, tn=None, tn_a=None, deps=()):
    S, M = a.shape
    N = b.shape[1]
    ts, tn, tn_a = min(ts, S), (tn or N), (tn_a or M)
    return _mm(name, a, b, grid=((N // tn) * (M // tn_a), 1, S // ts), a_blk=(ts, tn_a),
               a_map=lambda j, i, k: (k, j if tn_a < M else 0), b_blk=(ts, tn),
               b_map=lambda j, i, k: (k, j if tn < N else 0), o_shape=(M, N), o_blk=(tn_a, tn),
               o_map=lambda j, i, k: (j, 0) if tn_a < M else (0, j), dims=_TN, out_dtype=out_dtype, deps=deps)


class _Win:
    def __init__(self, arr, w, c):
        self.arr, self.w, self.c = arr, w, c


def _row_spec(x, tm):
    if isinstance(x, _Win):
        return x.arr, pl.BlockSpec((tm, x.w), functools.partial(lambda i, c: (i, c), c=x.c))
    if x.ndim == 3:
        return x, pl.BlockSpec((x.shape[0], tm, x.shape[2]), lambda i: (0, i, 0))
    return x, pl.BlockSpec((tm, x.shape[1]), lambda i: (i, 0))


def _rowwise(name, fn, rows, consts, outs, accs=(), tm=512):
    first = rows[0].arr if isinstance(rows[0], _Win) else rows[0]
    S = first.shape[1] if first.ndim == 3 else first.shape[0]
    tm = min(tm, S)
    n_in, n_out = len(rows) + len(consts), len(outs)
    arrays, in_specs = [], []
    for r in rows:
        arr, spec = _row_spec(r, tm)
        arrays.append(arr)
        in_specs.append(spec)
    for c in consts:
        arrays.append(c)
        in_specs.append(pl.BlockSpec(c.shape, functools.partial(lambda i, n: (0,) * n, n=c.ndim)))
    aliases = {}
    for k, o in enumerate(outs):
        if isinstance(o, _Win):
            aliases[len(arrays)] = k
            arrays.append(o.arr)
            in_specs.append(pl.BlockSpec(memory_space=pl.ANY))
    n_thru = len(aliases)
    out_specs = [_row_spec(o, tm)[1] for o in outs]
    out_specs += [pl.BlockSpec(a.shape, functools.partial(lambda i, n: (0,) * n, n=len(a.shape))) for a in accs]
    out_shape = [_sds(o.arr.shape, o.arr.dtype) if isinstance(o, _Win) else _sds(o.shape, o.dtype) for o in outs]

    def body(*refs):
        res = fn(*[r[...] for r in refs[:n_in]])
        if not isinstance(res, (tuple, list)):
            res = (res,)
        orefs = refs[n_in + n_thru:]
        for r, v in zip(orefs[:n_out], res[:n_out]):
            r[...] = v.astype(r.dtype)
        i = pl.program_id(0)
        for r, v in zip(orefs[n_out:], res[n_out:]):
            @pl.when(i == 0)
            def _(r=r, v=v):
                r[...] = v.astype(r.dtype)

            @pl.when(i > 0)
            def _(r=r, v=v):
                r[...] += v.astype(r.dtype)

    return _pcall(body, name=name, grid=(S // tm,), in_specs=in_specs, out_specs=out_specs,
                  out_shape=out_shape + [_sds(a.shape, a.dtype) for a in accs], aliases=aliases)(*arrays)


def _sds(shape, dtype=F32):
    return jax.ShapeDtypeStruct(shape, dtype)


def _rms(x, g):
    return x * lax.rsqrt(jnp.mean(x * x, axis=-1, keepdims=True) + EPS) * g


def _rms_bwd(x, g, dy):
    _, vjp = jax.vjp(_rms, x, g)
    return vjp(dy)


def _sigmoid(x):
    return 0.5 * jnp.tanh(0.5 * x) + 0.5


def _silu(x):
    return x * _sigmoid(x)


def _log_sigmoid(f):
    return jnp.minimum(f, 0.0) - jnp.log(1.0 + jnp.exp(-jnp.abs(f)))


def _head_rms_gate(o, g_out, gn):
    parts = [_rms(o[:, h * GLA_HDV:(h + 1) * GLA_HDV], gn[:, h * GLA_HDV:(h + 1) * GLA_HDV]) for h in range(GLA_HEADS)]
    return jnp.concatenate(parts, axis=-1) * _silu(g_out.astype(F32))


def _merge(gates, ya, yb, yc):
    gates, ya, yb, yc = (t.astype(F32) for t in (gates, ya, yb, yc))
    return (_sigmoid(gates[:, :D_MODEL]) * ya + _sigmoid(gates[:, D_MODEL:2 * D_MODEL]) * yb
            + _sigmoid(gates[:, 2 * D_MODEL:]) * yc)


def _tri_dot(t, x):
    hi = x.astype(BF16)
    r1 = x - hi.astype(F32)
    mid = r1.astype(BF16)
    lo = (r1 - mid.astype(F32)).astype(BF16)
    d = functools.partial(lax.dot_general, dimension_numbers=_NN, preferred_element_type=F32)
    return d(t, hi) + d(t, mid) + d(t, lo)


def _pre_norm(name, x, g, after):
    S = x.shape[0]
    tm = min(512, S)

    def body(x_ref, g_ref, *rest):
        rest[-1][...] = _rms(x_ref[...], g_ref[...]).astype(BF16)

    row = pl.BlockSpec((tm, D_MODEL), lambda i: (i, 0))
    return _pcall(body, name=name, grid=(S // tm,),
                  in_specs=[row, pl.BlockSpec((1, D_MODEL), lambda i: (0, 0))] + [pl.BlockSpec(memory_space=pl.ANY)] * len(after),
                  out_specs=row, out_shape=_sds(x.shape, BF16))(x, g, *after)


def _ffn_fwd(tag, x, h, wt, gpost, get_wo, g_next, tgt=None):
    S = x.shape[0]
    tm = min(512, S)

    def up_body(h_ref, w_hbm, u_ref, act_ref, w_vmem, sems):
        w_block = _weight_blocks(w_hbm, w_vmem, sems)
        hh = h_ref[...]
        for j in range(N_FF_BLK):
            ab = _dot(hh, w_block(j), _NT)
            u_ref[:, 2 * FF_PAD * j:2 * FF_PAD * (j + 1)] = ab.astype(BF16)
            act_ref[:, FF_PAD * j:FF_PAD * (j + 1)] = (_silu(ab[:, :FF_PAD]) * ab[:, FF_PAD:]).astype(BF16)

    u, act = _pcall(
        up_body, name=tag + "_up", grid=(S // tm,),
        in_specs=[pl.BlockSpec((tm, D_MODEL), lambda i: (i, 0)), pl.BlockSpec(memory_space=pl.ANY)],
        out_specs=[pl.BlockSpec((tm, N_DEV * FF_PAD), lambda i: (i, 0)),
                   pl.BlockSpec((tm, N_FF_BLK * FF_PAD), lambda i: (i, 0))],
        out_shape=[_sds((S, N_DEV * FF_PAD), BF16), _sds((S, N_FF_BLK * FF_PAD), BF16)],
        scratch=[pltpu.VMEM(wt.shape, BF16), pltpu.SemaphoreType.DMA((N_FF_BLK,))],
    )(h, wt)

    def down_body(act_ref, w_ref, x_ref, g_ref, gn_ref, f_ref, xo_ref, hn_ref):
        f = _dot(act_ref[...], w_ref[...])
        f_ref[...] = f.astype(BF16)
        xo = x_ref[...] + 0.5 * _rms(f, g_ref[...])
        xo_ref[...] = xo
        hn_ref[...] = _rms(xo, gn_ref[...]).astype(BF16)

    def down_head_body(act_ref, w_ref, x_ref, g_ref, gn_ref, t_ref, f_ref, dxo_ref, loss_ref, dgn_ref):
        i = pl.program_id(0)
        f = _dot(act_ref[...], w_ref[...])
        f_ref[...] = f.astype(BF16)
        xo = x_ref[...] + 0.5 * _rms(f, g_ref[...])
        out, vjp = jax.vjp(_rms, xo, gn_ref[...])
        e = out - t_ref[...]
        loss = 0.5 * jnp.sum(jnp.mean(e * e, axis=-1, keepdims=True), axis=0, keepdims=True)
        dxo_ref[...], dgn = vjp(e * (1.0 / D_MODEL))

        @pl.when(i == 0)
        def _():
            loss_ref[...] = jnp.broadcast_to(loss, loss_ref.shape)
            dgn_ref[...] = dgn

        @pl.when(i > 0)
        def _():
            loss_ref[...] += jnp.broadcast_to(loss, loss_ref.shape)
            dgn_ref[...] += dgn

    wo = get_wo(act)
    row = pl.BlockSpec((tm, D_MODEL), lambda i: (i, 0))
    vec = pl.BlockSpec((1, D_MODEL), lambda i: (0, 0))
    if tgt is not None:
        f, dxo, loss, dg_next = _pcall(
            down_head_body, name=tag + "_down", grid=(S // tm,),
            in_specs=[pl.BlockSpec((tm, N_FF_BLK * FF_PAD), lambda i: (i, 0)),
                      pl.BlockSpec(wo.shape, lambda i: (0, 0), pipeline_mode=pl.Buffered(1)), row, vec, vec, row],
            out_specs=[row, row, pl.BlockSpec((1, 128), lambda i: (0, 0)), vec],
            out_shape=[_sds((S, D_MODEL), BF16), _sds((S, D_MODEL)), _sds((1, 128)), _sds((1, D_MODEL))],
        )(act, wo, x, gpost, g_next, tgt)
        return (loss, dxo, dg_next), wo, (h, u, act, f)
    f, xo, h_next = _pcall(
        down_body, name=tag + "_down", grid=(S // tm,),
        in_specs=[pl.BlockSpec((tm, N_FF_BLK * FF_PAD), lambda i: (i, 0)),
                  pl.BlockSpec(wo.shape, lambda i: (0, 0), pipeline_mode=pl.Buffered(1)), row, vec, vec],
        out_specs=[row, row, row], out_shape=[_sds((S, D_MODEL), BF16), _sds((S, D_MODEL)), _sds((S, D_MODEL), BF16)],
    )(act, wo, x, gpost, g_next)
    return (xo, h_next), wo, (h, u, act, f)


def _ffn_bwd(tag, dxo, x, gpre, wt, wo, gpost, saved, on_dwo=None, on_dwt=None):
    h, u, act, f = saved
    S = x.shape[0]
    tm = min(256, S)

    def dact_body(f_ref, dxo_ref, g_ref, w_ref, u_ref, df_ref, du_ref, dg_ref):
        i = pl.program_id(0)
        df, dg = _rms_bwd(f_ref[...].astype(F32), g_ref[...], 0.5 * dxo_ref[...])
        df = df.astype(BF16)
        df_ref[...] = df

        @pl.when(i == 0)
        def _():
            dg_ref[...] = dg

        @pl.when(i > 0)
        def _():
            dg_ref[...] += dg

        for j in range(N_FF_BLK):
            dact = _dot(df, w_ref[FF_PAD * j:FF_PAD * (j + 1), :], _NT)
            ab = u_ref[:, 2 * FF_PAD * j:2 * FF_PAD * (j + 1)].astype(F32)
            a, b = ab[:, :FF_PAD], ab[:, FF_PAD:]
            sg = _sigmoid(a)
            du_ref[:, 2 * FF_PAD * j:2 * FF_PAD * j + FF_PAD] = (dact * b * (sg * (1.0 + a * (1.0 - sg)))).astype(BF16)
            du_ref[:, 2 * FF_PAD * j + FF_PAD:2 * FF_PAD * (j + 1)] = (dact * (a * sg)).astype(BF16)

    row = pl.BlockSpec((tm, D_MODEL), lambda i: (i, 0))
    vec = pl.BlockSpec((1, D_MODEL), lambda i: (0, 0))
    u_spec = pl.BlockSpec((tm, N_DEV * FF_PAD), lambda i: (i, 0))
    df, du, dgpost = _pcall(
        dact_body, name=tag + "_dact", grid=(S // tm,),
        in_specs=[row, row, vec, pl.BlockSpec(wo.shape, lambda i: (0, 0), pipeline_mode=pl.Buffered(1)), u_spec],
        out_specs=[row, u_spec, vec],
        out_shape=[_sds((S, D_MODEL), BF16), _sds((S, N_DEV * FF_PAD), BF16), _sds((1, D_MODEL))])(f, dxo, gpost, wo, u)
    dwo = _mm_tn(tag + "_dwo", act, df, ts=2048, tn_a=FF_PAD)
    dwt = _mm_tn(tag + "_dwt", du, h, ts=2048, tn_a=FF_PAD, deps=[on_dwo(dwo)] if on_dwo is not None else [])
    dx, dgpre = _dh_pre_norm_bwd(tag + "_dh", [du], wt, x, dxo, gpre, [on_dwt(dwt)] if on_dwt is not None else [])
    return dx, dgpre, dgpost


def _dh_pre_norm_bwd(name, dzs, wt, x, dres, g, deps):
    S = x.shape[0]
    ks = [dz.shape[1] for dz in dzs]

    def body(*refs):
        dz_refs = refs[:len(dzs)]
        w_ref, x_ref, dres_ref, g_ref = refs[len(dzs):len(dzs) + 4]
        dx_ref, dg_ref = refs[len(dzs) + 4 + len(deps):]
        i = pl.program_id(0)
        dh, o = None, 0
        for dz_ref, k in zip(dz_refs, ks):
            p = _dot(dz_ref[...], w_ref[o:o + k, :])
            dh, o = (p if dh is None else dh + p), o + k
        dx, dg = _rms_bwd(x_ref[...], g_ref[...], dh)
        dx_ref[...] = dx + dres_ref[...]

        @pl.when(i == 0)
        def _():
            dg_ref[...] = dg

        @pl.when(i > 0)
        def _():
            dg_ref[...] += dg

    th = min(512, S)
    row = pl.BlockSpec((th, D_MODEL), lambda i: (i, 0))
    vec = pl.BlockSpec((1, D_MODEL), lambda i: (0, 0))
    return _pcall(
        body, name=name, grid=(S // th,),
        in_specs=[pl.BlockSpec((th, k), lambda i: (i, 0)) for k in ks]
        + [pl.BlockSpec(wt.shape, lambda i: (0, 0), pipeline_mode=pl.Buffered(1)), row, row, vec]
        + [pl.BlockSpec(memory_space=pl.ANY)] * len(deps),
        out_specs=[row, vec], out_shape=[_sds((S, D_MODEL)), _sds((1, D_MODEL))])(*dzs, wt, x, dres, g, *deps)


GLA_G = 8


def _gla_tile_common(k, flow, wfu, bf):
    f = _dot(flow, wfu) + bf
    la = _log_sigmoid(f) * (1.0 / GATE_TEMP)
    tri = _tri_matrix(True)
    ws, ds = [], []
    for g in range(k.shape[0] // CHUNK):
        b = _tri_dot(tri, la[g * CHUNK:(g + 1) * CHUNK])
        b_end = b[CHUNK - 1:CHUNK, :]
        ws.append(jnp.exp(b_end - b))
        ds.append(jnp.exp(b_end))
    w = jnp.concatenate(ws, axis=0)
    return f, w, k * w, ds


def _tri_matrix(lower):
    r = lax.broadcasted_iota(jnp.int32, (CHUNK, CHUNK), 0)
    c = lax.broadcasted_iota(jnp.int32, (CHUNK, CHUNK), 1)
    return jnp.where((r >= c) if lower else (r <= c), 1.0, 0.0).astype(BF16)


def _heads():
    return [(slice(h * GLA_HDK, (h + 1) * GLA_HDK), slice(h * GLA_HDV, (h + 1) * GLA_HDV)) for h in range(GLA_HEADS)]


def _gla_fwd(proj, wfu, bf, gn, deps=()):
    S = proj.shape[0]
    G = min(GLA_G, S // CHUNK)
    T = G * CHUNK
    nc = S // CHUNK

    def body(q_ref, k_ref, v_ref, go_ref, flow_ref, wfu_ref, bf_ref, gn_ref, *rest):
        ya_ref, st_ref, state = rest[len(deps):]
        @pl.when(pl.program_id(0) == 0)
        def _():
            state[...] = jnp.zeros_like(state)

        _, _, kt, ds = _gla_tile_common(k_ref[...], flow_ref[...], wfu_ref[...], bf_ref[...])
        q = q_ref[...].astype(F32) * (GLA_HDK ** -0.5)
        v = v_ref[...]
        rows = [slice(g * CHUNK, (g + 1) * CHUNK) for g in range(G)]
        kv = [[_dot(v[r, vs], kt[r, ks], _TN) for ks, vs in _heads()] for r in rows]
        st = [state[vs, :] for _, vs in _heads()]
        o = []
        for g, r in enumerate(rows):
            outs = []
            for h, (ks, vs) in enumerate(_heads()):
                st[h] = st[h] * ds[g][:, ks] + kv[g][h]
                st_ref[g, vs, :] = st[h]
                outs.append(_dot(q[r, ks], st[h], _NT))
            o.append(jnp.concatenate(outs, axis=-1))
        for h, (_, vs) in enumerate(_heads()):
            state[vs, :] = st[h]
        ya_ref[...] = _head_rms_gate(jnp.concatenate(o, axis=0), go_ref[...], gn_ref[...]).astype(BF16)

    return _pcall(
        body, name="gla_fwd", grid=(S // T,),
        in_specs=[pl.BlockSpec((T, GLA_DK), lambda c: (c, 0)), pl.BlockSpec((T, GLA_DK), lambda c: (c, 1)),
                  pl.BlockSpec((T, GLA_DV), lambda c: (c, 1)), pl.BlockSpec((T, GLA_DV), lambda c: (c, 2)),
                  pl.BlockSpec((T, FLOW_W), lambda c: (c, (INT_W - FLOW_W) // FLOW_W)),
                  pl.BlockSpec(wfu.shape, lambda c: (0, 0)), pl.BlockSpec(bf.shape, lambda c: (0, 0)),
                  pl.BlockSpec(gn.shape, lambda c: (0, 0))]
        + [pl.BlockSpec(memory_space=pl.ANY)] * len(deps),
        out_specs=[pl.BlockSpec((T, GLA_DV), lambda c: (c, 0)),
                   pl.BlockSpec((G, GLA_DV, GLA_HDK), lambda c: (c, 0, 0))],
        out_shape=[_sds((S, GLA_DV), BF16), _sds((nc, GLA_DV, GLA_HDK))],
        scratch=[pltpu.VMEM((GLA_DV, GLA_HDK), F32)])(proj, proj, proj, proj, proj, wfu, bf, gn, *deps)


def _gla_bwd(proj, wfu, bf, gn, states, d_ya_in, dmain):
    S = proj.shape[0]
    G = min(GLA_G, S // CHUNK)
    T = G * CHUNK
    nt = S // T

    def body(q_ref, k_ref, v_ref, go_ref, flow_ref, wfu_ref, bf_ref, gn_ref, st_ref, stp_ref, dya_ref, dmain_in,
             dqkvg_ref, dflow_ref, dwfu_ref, dbf_ref, dgn_ref, dstate):
        dq_ref = dqkvg_ref.at[:, 0:GLA_DK]
        dk_ref = dqkvg_ref.at[:, GLA_DK:2 * GLA_DK]
        dv_ref = dqkvg_ref.at[:, 2 * GLA_DK:2 * GLA_DK + GLA_DV]
        dgo_ref = dqkvg_ref.at[:, 2 * GLA_DK + GLA_DV:]
        step = pl.program_id(0)

        @pl.when(step == 0)
        def _():
            dstate[...] = jnp.zeros_like(dstate)
            dwfu_ref[...] = jnp.zeros_like(dwfu_ref)
            dbf_ref[...] = jnp.zeros_like(dbf_ref)
            dgn_ref[...] = jnp.zeros_like(dgn_ref)

        flow, wfu_v = flow_ref[...], wfu_ref[...]
        f, w, kt, ds = _gla_tile_common(k_ref[...], flow, wfu_v, bf_ref[...])
        q = q_ref[...].astype(F32) * (GLA_HDK ** -0.5)
        v = v_ref[...]
        rows = [slice(g * CHUNK, (g + 1) * CHUNK) for g in range(G)]
        o = jnp.concatenate([jnp.concatenate([_dot(q[r, ks], st_ref[g, vs, :], _NT) for ks, vs in _heads()], axis=-1)
                             for g, r in enumerate(rows)], axis=0)
        _, vjp = jax.vjp(_head_rms_gate, o, go_ref[...], gn_ref[...])
        dout, dgo, dgn = vjp(dya_ref[...].astype(F32))
        dgo_ref[...] = dgo.astype(dgo_ref.dtype)
        dgn_ref[...] += dgn
        dq = [jnp.concatenate([_dot(dout[r, vs], st_ref[g, vs, :]) for _, vs in _heads()], axis=-1)
              for g, r in enumerate(rows)]
        qdo = [[_dot(dout[r, vs], q[r, ks], _TN) for ks, vs in _heads()] for r in rows]
        dq_ref[...] = (jnp.concatenate(dq, axis=0) * (GLA_HDK ** -0.5)).astype(dq_ref.dtype)
        has_prev = (step < nt - 1).astype(F32)
        carry = [dstate[vs, :] for _, vs in _heads()]
        dkt, dv, dd = [None] * G, [None] * G, [None] * G
        for g in reversed(range(G)):
            r = rows[g]
            dkts, dvs, dds = [], [], []
            for h, (ks, vs) in enumerate(_heads()):
                dst = carry[h] + qdo[g][h]
                dkts.append(_dot(v[r, vs], dst))
                dvs.append(_dot(kt[r, ks], dst, _NT))
                st_prev = st_ref[g - 1, vs, :] if g > 0 else stp_ref[vs, :] * has_prev
                dds.append(jnp.sum(dst * st_prev, axis=0, keepdims=True))
                carry[h] = dst * ds[g][:, ks]
            dkt[g], dv[g], dd[g] = (jnp.concatenate(t, axis=-1) for t in (dkts, dvs, dds))
        for h, (_, vs) in enumerate(_heads()):
            dstate[vs, :] = carry[h]
        dkt = jnp.concatenate(dkt, axis=0)
        dv_ref[...] = jnp.concatenate(dv, axis=0).astype(dv_ref.dtype)
        dk_ref[...] = (dkt * w).astype(dk_ref.dtype)
        de = dkt * kt
        tri = _tri_matrix(False)
        dla = []
        for g, r in enumerate(rows):
            db_end = jnp.sum(de[r], axis=0, keepdims=True) + dd[g] * ds[g]
            dla.append(db_end - _tri_dot(tri, de[r]))
        df = jnp.concatenate(dla, axis=0) * (1.0 - _sigmoid(f)) * (1.0 / GATE_TEMP)
        dflow_ref[...] = _dot(df, wfu_v, _NT).astype(dflow_ref.dtype)
        dwfu_ref[...] += _dot(flow, df, _TN)
        dbf_ref[...] += jnp.sum(df, axis=0, keepdims=True)

    rt = lambda s: nt - 1 - s
    return _pcall(
        body, name="gla_bwd", grid=(nt,),
        in_specs=[pl.BlockSpec((T, GLA_DK), lambda s: (rt(s), 0)), pl.BlockSpec((T, GLA_DK), lambda s: (rt(s), 1)),
                  pl.BlockSpec((T, GLA_DV), lambda s: (rt(s), 1)), pl.BlockSpec((T, GLA_DV), lambda s: (rt(s), 2)),
                  pl.BlockSpec((T, FLOW_W), lambda s: (rt(s), (INT_W - FLOW_W) // FLOW_W)),
                  pl.BlockSpec(wfu.shape, lambda s: (0, 0)), pl.BlockSpec(bf.shape, lambda s: (0, 0)),
                  pl.BlockSpec(gn.shape, lambda s: (0, 0)),
                  pl.BlockSpec((G, GLA_DV, GLA_HDK), lambda s: (rt(s), 0, 0)),
                  pl.BlockSpec((None, GLA_DV, GLA_HDK), lambda s: (jnp.maximum(rt(s) * G - 1, 0), 0, 0)),
                  pl.BlockSpec((T, GLA_DV), lambda s: (rt(s), 0)), pl.BlockSpec(memory_space=pl.ANY)],
        out_specs=[pl.BlockSpec((T, 2 * GLA_DK + 2 * GLA_DV), lambda s: (rt(s), 0)),
                   pl.BlockSpec((T, FLOW_W), lambda s: (rt(s), 0)),
                   pl.BlockSpec(wfu.shape, lambda s: (0, 0)), pl.BlockSpec(bf.shape, lambda s: (0, 0)),
                   pl.BlockSpec(gn.shape, lambda s: (0, 0))],
        out_shape=[_sds(dmain.shape, dmain.dtype), _sds((S, FLOW_W), BF16), _sds(wfu.shape), _sds(bf.shape),
                   _sds(gn.shape)],
        scratch=[pltpu.VMEM((GLA_DV, GLA_HDK), F32)], aliases={11: 0},
    )(proj, proj, proj, proj, proj, wfu, bf, gn, states, states, d_ya_in, dmain)


def _pool_counts(tm, i):
    t = (lax.broadcasted_iota(jnp.int32, (tm, POOL_GD), 0) + i * tm + 1).astype(F32)
    return [jnp.minimum(t, float(w)) for w in POOL_WINDOWS]


def _pool_fwd(proj, w_pool, pool_scale, tm=512):
    S = proj.shape[0]
    tm = min(tm, S // 2)
    col = _POOL_COL
    hb = tm // POOL_HALO

    def body(p_ref, halo_ref, wp_ref, sc_ref, mixed_ref, out_ref):
        i = pl.program_id(0)
        p = p_ref[...].astype(F32)
        halo = halo_ref[...].astype(F32) * (i > 0).astype(F32)
        ext = jnp.concatenate([halo, p], axis=0)
        n = tm + POOL_HALO
        sums, acc, k = {}, ext, 1
        while k < POOL_WINDOWS[-1]:
            acc = acc + pltpu.roll(acc, k, axis=0)
            k *= 2
            sums[k] = acc
        cnts = _pool_counts(tm, i)
        mixed, lin = [], []
        for g, w in enumerate(POOL_WINDOWS):
            ls = slice(g * POOL_GD, (g + 1) * POOL_GD)
            m = sums[w][POOL_HALO:n, ls] / cnts[g] - p[:, ls]
            mixed.append(m)
            lin.append(_dot(m, wp_ref[g]))
        mixed_ref[...] = jnp.concatenate(mixed, axis=-1)
        out_ref[...] = (jnp.concatenate(lin, axis=-1) * sc_ref[...]).astype(out_ref.dtype)

    return _pcall(
        body, name="pool_fwd", grid=(S // tm,),
        in_specs=[pl.BlockSpec((tm, POOL_W), lambda i: (i, col)),
                  pl.BlockSpec((POOL_HALO, POOL_W), lambda i: (jnp.maximum(i * hb - 1, 0), col)),
                  pl.BlockSpec(w_pool.shape, lambda i: (0, 0, 0)), pl.BlockSpec(pool_scale.shape, lambda i: (0, 0))],
        out_specs=[pl.BlockSpec((tm, POOL_W), lambda i: (i, 0)), pl.BlockSpec((tm, POOL_W), lambda i: (i, 0))],
        out_shape=[_sds((S, POOL_W)), _sds((S, POOL_W), BF16)])(proj, proj, w_pool, pool_scale)


def _pool_lin_bwd(dout, mixed, w_pool, pool_scale):
    S = dout.shape[0]

    def fn(dout, mixed, wp, sc):
        dlin = dout * sc
        dm, dwp, lin = [], [], []
        for g in range(len(POOL_WINDOWS)):
            ls = slice(g * POOL_GD, (g + 1) * POOL_GD)
            lin.append(_dot(mixed[:, ls], wp[g]))
            dm.append(_dot(dlin[:, ls], wp[g], _NT))
            dwp.append(_dot(mixed[:, ls], dlin[:, ls], _TN))
        dsc = jnp.sum(dout * jnp.concatenate(lin, axis=-1), axis=0, keepdims=True)
        return jnp.concatenate(dm, axis=-1), jnp.concatenate(dwp, axis=0), dsc

    return _rowwise("pool_lin_bwd", fn, [dout, mixed], [w_pool, pool_scale], [_sds((S, POOL_W))],
                    [_sds((len(POOL_WINDOWS) * POOL_GD, POOL_GD)), _sds((1, POOL_W))])


def _pool_win_bwd(dmixed, dmain, tm=512):
    S = dmixed.shape[0]
    tm = min(tm, S // 2)
    nt = S // tm
    hb = tm // POOL_HALO

    def body(dm_ref, halo_ref, dmain_in, dp_ref):
        i = pl.program_id(0)
        dm = dm_ref[...]
        halo = halo_ref[...] * (i < nt - 1).astype(F32)
        cnts = _pool_counts(tm, i)
        cnts_h = [c[:POOL_HALO] for c in _pool_counts(tm, i + 1)]
        r = jnp.concatenate([jnp.concatenate([dm[:, g * POOL_GD:(g + 1) * POOL_GD] / cnts[g] for g in range(4)], axis=-1),
                             jnp.concatenate([halo[:, g * POOL_GD:(g + 1) * POOL_GD] / cnts_h[g] for g in range(4)], axis=-1)],
                            axis=0)
        n = tm + POOL_HALO
        sums, acc, k = {}, r, 1
        while k < POOL_WINDOWS[-1]:
            acc = acc + pltpu.roll(acc, n - k, axis=0)
            k *= 2
            sums[k] = acc
        dp = [sums[w][:tm, g * POOL_GD:(g + 1) * POOL_GD] for g, w in enumerate(POOL_WINDOWS)]
        dp_ref[...] = (jnp.concatenate(dp, axis=-1) - dm).astype(dp_ref.dtype)

    return _pcall(
        body, name="pool_win_bwd", grid=(nt,),
        in_specs=[pl.BlockSpec((tm, POOL_W), lambda i: (i, 0)),
                  pl.BlockSpec((POOL_HALO, POOL_W), lambda i: (jnp.minimum((i + 1) * hb, S // POOL_HALO - 1), 0)),
                  pl.BlockSpec(memory_space=pl.ANY)],
        out_specs=pl.BlockSpec((tm, POOL_W), lambda i: (i, _POOL_COL)),
        out_shape=_sds(dmain.shape, dmain.dtype), aliases={2: 0})(dmixed, dmixed, dmain)


def _xattn_probs(q, kv, h):
    hs = slice(h * XA_HD, (h + 1) * XA_HD)
    s = _dot(q[:, hs], kv[:, hs], _NT) * (XA_HD ** -0.5)
    s = s - jnp.max(s, axis=-1, keepdims=True)
    e = jnp.exp(s)
    return e / jnp.sum(e, axis=-1, keepdims=True)


def _xattn_fwd(proj, kv):
    S = proj.shape[0]

    def fn(q, kv):
        outs = []
        for h in range(XA_HEADS):
            p = _xattn_probs(q, kv, h)
            outs.append(_dot(p, kv[:, XA_W + h * XA_HD:XA_W + (h + 1) * XA_HD]))
        return jnp.concatenate(outs, axis=-1)

    return _rowwise("xattn_fwd", fn, [_Win(proj, XA_W, _XQ_COL)], [kv], [_sds((S, XA_W), BF16)])[0]


def _xattn_bwd(proj, kv, dxa, dmain):
    def fn(q, dxa, kv):
        dqs, dks, dvs = [], [], []
        for h in range(XA_HEADS):
            hs = slice(h * XA_HD, (h + 1) * XA_HD)
            vh = kv[:, XA_W + h * XA_HD:XA_W + (h + 1) * XA_HD]
            p = _xattn_probs(q, kv, h)
            dp = _dot(dxa[:, hs], vh, _NT)
            ds = p * (dp - jnp.sum(p * dp, axis=-1, keepdims=True)) * (XA_HD ** -0.5)
            dqs.append(_dot(ds, kv[:, hs]))
            dks.append(_dot(ds, q[:, hs], _TN))
            dvs.append(_dot(p, dxa[:, hs], _TN))
        return jnp.concatenate(dqs, axis=-1), jnp.concatenate(dks + dvs, axis=-1)

    return _rowwise("xattn_bwd", fn, [_Win(proj, XA_W, _XQ_COL), dxa], [kv], [_Win(dmain, XA_W, _XQ_COL)],
                    [_sds(kv.shape)])


def _mix_fwd(x1, h, mem, W, g_next, on_proj=None, late_weights=None):
    S = x1.shape[0]
    M = mem.shape[0]
    tm = min(512, S)
    def proj_body(h_ref, w_hbm, o_ref, w_vmem, sems):
        w_block = _weight_blocks(w_hbm, w_vmem, sems)
        hh = h_ref[...]
        for j in range(INT_NBLK):
            o_ref[:, INT_BLK * j:INT_BLK * (j + 1)] = _dot(hh, w_block(j), _NT).astype(BF16)

    proj = _pcall(
        proj_body, name="mix_proj", grid=(S // tm,),
        in_specs=[pl.BlockSpec((tm, D_MODEL), lambda i: (i, 0)), pl.BlockSpec(memory_space=pl.ANY)],
        out_specs=pl.BlockSpec((tm, INT_W), lambda i: (i, 0)), out_shape=_sds((S, INT_W), BF16),
        scratch=[pltpu.VMEM((INT_W, D_MODEL), BF16), pltpu.SemaphoreType.DMA((INT_NBLK,))])(h, W["w_int_t"])
    ya_in, states = _gla_fwd(proj, W["w_fu_pad"], W["b_f"], W["gla_norm_g"],
                             on_proj(proj) if on_proj is not None else [])
    mixed, pool_out = _pool_fwd(proj, W["w_pool"], W["pool_scale"])
    mem_n = _rowwise("mem_norm", lambda m, g: _rms(m, g), [mem], [W["mem_norm_g"]], [_sds((M, D_MODEL), BF16)])[0]
    if late_weights is not None:
        W = {**W, **late_weights([pool_out])}
    kv = _mm_nn("mem_kv", mem_n, W["w_mem_kv"])
    xa = _xattn_fwd(proj, kv)
    def out_fn(gates, ya_in, pool_out, xa, x1, wa, wb, wc, wo, g, gn):
        ya, yb, yc = (_dot(t, w).astype(BF16) for t, w in ((ya_in, wa), (pool_out, wb), (xa, wc)))
        merged = _merge(gates, ya, yb, yc).astype(BF16)
        y = _dot(merged, wo)
        x2 = x1 + _rms(y, g)
        return ya, yb, yc, merged, y, x2, _rms(x2, gn)

    ya, yb, yc, merged, y, x2, h_next = _rowwise(
        "mix_out", out_fn, [_Win(proj, 3 * D_MODEL, 1), ya_in, pool_out, xa, x1],
        [W["w_up_gla"], W["w_up_pool"], W["w_up_xattn"], W["w_o"], W["mix_post_g"], g_next],
        [_sds((S, D_MODEL), BF16)] * 5 + [_sds((S, D_MODEL)), _sds((S, D_MODEL), BF16)], tm=256)
    return x2, h_next, W, (h, proj, states, ya_in, mixed, pool_out, mem_n, kv, xa, ya, yb, yc, merged, y)


def _mix_bwd(dx2, x1, mem, W, saved, on_grads=None):
    h, proj, states, ya_in, mixed, pool_out, mem_n, kv, xa, ya, yb, yc, merged, y = saved
    S = x1.shape[0]

    def out_bwd(y, dx2, gates, ya, yb, yc, g, wo, wa, wb, wc):
        dy, dg = _rms_bwd(y.astype(F32), g, dx2)
        dy = dy.astype(BF16)
        _, vjp = jax.vjp(_merge, gates, ya, yb, yc)
        dgates, dya, dyb, dyc = vjp(_dot(dy, wo, _NT))
        return (dy, dgates, dya, dyb, dyc, _dot(dya, wa, _NT), _dot(dyb, wb, _NT), _dot(dyc, wc, _NT), dg)

    dmain = lax.empty((S, INT_MAIN), BF16)
    dy, dmain, dya, dyb, dyc, d_ya_in, d_pool_out, d_xa, d_mix_post_g = _rowwise(
        "mix_out_bwd", out_bwd, [y, dx2, _Win(proj, 3 * D_MODEL, 1), ya, yb, yc],
        [W["mix_post_g"], W["w_o"], W["w_up_gla"], W["w_up_pool"], W["w_up_xattn"]],
        [_sds((S, D_MODEL), BF16), _Win(dmain, 3 * D_MODEL, 1)] + [_sds((S, D_MODEL), BF16)] * 4
        + [_sds((S, POOL_W), BF16), _sds((S, XA_W), BF16)], [_sds((1, D_MODEL))], tm=256)
    d_w_o = _mm_tn("d_w_o", merged, dy)
    d_w_up_gla = _mm_tn("d_w_up_gla", ya_in, dya)
    d_w_up_pool = _mm_tn("d_w_up_pool", pool_out, dyb)
    d_w_up_xattn = _mm_tn("d_w_up_xattn", xa, dyc)

    dmain, dflow, d_wfu_pad, d_b_f, d_gla_norm_g = _gla_bwd(proj, W["w_fu_pad"], W["b_f"], W["gla_norm_g"], states,
                                                            d_ya_in, dmain)
    dmixed, d_w_pool, d_pool_scale = _pool_lin_bwd(d_pool_out, mixed, W["w_pool"], W["pool_scale"])
    dmain = _pool_win_bwd(dmixed, dmain)
    dmain, dkv = _xattn_bwd(proj, kv, d_xa, dmain)
    d_w_mem_kv = _mm_tn("d_w_mem_kv", mem_n, dkv)
    dmem_n = _mm_nt("d_mem_n", dkv, W["w_mem_kv"])
    d_mem_norm_g = _rowwise("mem_norm_bwd", lambda m, d, g: _rms_bwd(m, g, d)[1], [mem, dmem_n], [W["mem_norm_g"]], [],
                            [_sds((1, D_MODEL))])[0]
    d_w_int_t = (_mm_tn("d_w_int", dmain, h, ts=2048, tn_a=INT_MAIN // 4), _mm_tn("d_w_flow", dflow, h, ts=2048))
    grads = dict(
        w_int_t=d_w_int_t, w_fu_pad=d_wfu_pad, b_f=d_b_f, gla_norm_g=d_gla_norm_g, w_pool=d_w_pool,
        pool_scale=d_pool_scale, mem_norm_g=d_mem_norm_g, w_mem_kv=d_w_mem_kv, w_up_gla=d_w_up_gla,
        w_up_pool=d_w_up_pool, w_up_xattn=d_w_up_xattn, w_o=d_w_o, mix_post_g=d_mix_post_g)
    deps = [on_grads(grads)] if on_grads is not None else []
    dx1, grads["mix_pre_g"] = _dh_pre_norm_bwd("d_mix_h", [dmain, dflow], W["w_int_t"], x1, dx2, W["mix_pre_g"], deps)
    return dx1, grads


def _mesh_pos():
    x, y, c = lax.axis_index("x"), lax.axis_index("y"), lax.axis_index("c")
    return x, y, c, 4 * x + 2 * y + c


def _peer(x, y, c, r):
    px = 1 - x if r & 4 else x
    py = 1 - y if r & 2 else y
    pc = 1 - c if r & 1 else c
    return (px, py, pc), 4 * px + 2 * py + pc


_ALL_PEERS = tuple(range(1, N_DEV))
_SIBLING = 1
_SAME_CORE = (2, 4, 6)


def _dev_slot(ref, dev):
    return ref.at[dev]


def _zero_pad_rows(land):
    pad = FF_PAD - FF_BLK

    def body(land_in, o_ref):
        o_ref[...] = jnp.zeros_like(o_ref)

    return _pcall(body, name="zero_pad_rows", grid=(land.shape[0],), in_specs=[pl.BlockSpec(memory_space=pl.ANY)],
                  out_specs=pl.BlockSpec((None, pad, D_MODEL), lambda j: (j, FF_BLK // pad, 0)),
                  out_shape=_sds(land.shape, land.dtype), aliases={0: 0})(land)


class _Plan:
    def __init__(self, scatter, slots=None, shapes=None):
        self.scatter, self.slots, self.shapes = scatter, slots or {}, shapes or {}

    def src(self, srcs, a, dev):
        return self.slots.get(a, _dev_slot)(srcs[a], dev) if self.scatter else srcs[a]

    def dst(self, lands, a, dev):
        return lands[a].at[dev] if self.scatter else self.slots.get(a, _dev_slot)(lands[a], dev)

    def landing_zones(self, arrays):
        lands = []
        for a, arr in enumerate(arrays):
            if self.scatter:
                lands.append(lax.empty((N_DEV,) + tuple(self.shapes.get(a, arr.shape[1:])), arr.dtype))
            elif a in self.shapes:
                lands.append(_zero_pad_rows(lax.empty(self.shapes[a], arr.dtype)))
            else:
                lands.append(lax.empty((N_DEV,) + arr.shape, arr.dtype))
        return lands


_GATHER, _SCATTER = _Plan(False), _Plan(True)


def _peer_copies(srcs, lands, send_sems, recv_sems, plan, peers=_ALL_PEERS):
    x, y, c, me = _mesh_pos()
    cps = []
    for r in peers:
        pos, peer = _peer(x, y, c, r)
        for a in range(len(srcs)):
            k = a * (N_DEV - 1) + r - 1
            cps.append(pltpu.make_async_remote_copy(
                src_ref=plan.src(srcs, a, peer), dst_ref=plan.dst(lands, a, me),
                send_sem=send_sems.at[k], recv_sem=recv_sems.at[k], device_id=pos,
                device_id_type=pl.DeviceIdType.MESH))
    return cps


_HBM = pl.BlockSpec(memory_space=pltpu.HBM)
_SEM = pl.BlockSpec(memory_space=pltpu.SEMAPHORE)
_EFFECT = pltpu.SideEffectType.DATAFLOW_SIDE_EFFECTING


def _own_copies(srcs, lands, own_sems, plan):
    me = _mesh_pos()[3]
    return [pltpu.make_async_copy(plan.src(srcs, a, me), plan.dst(lands, a, me), own_sems.at[a])
            for a in range(len(srcs))]


def _exchange_start(name, arrays, plan):
    n = len(arrays)
    lands = plan.landing_zones(arrays)
    n_sem = n * (N_DEV - 1)

    def body(*refs):
        srcs, lands_ = refs[:n], refs[n:2 * n]
        send_sems, recv_sems, own_sems = refs[2 * n:2 * n + 3]
        token = refs[-1]
        for cp in _peer_copies(srcs, lands_, send_sems, recv_sems, plan) + _own_copies(srcs, lands_, own_sems, plan):
            cp.start()
        token[...] = jnp.zeros_like(token)

    hbm = lambda a: pltpu.HBM(a.shape, a.dtype)
    res = pl.pallas_call(
        body, name=name,
        out_shape=(pltpu.SemaphoreType.DMA((n_sem,)), pltpu.SemaphoreType.DMA((n_sem,)), pltpu.SemaphoreType.DMA((n,)),
                   *[hbm(a) for a in arrays], *[hbm(a) for a in lands], _sds((8, 128))),
        in_specs=[_HBM] * (2 * n),
        out_specs=(_SEM, _SEM, _SEM, *[_HBM] * (2 * n), pl.BlockSpec(memory_space=pltpu.VMEM)),
        input_output_aliases={i: 3 + i for i in range(2 * n)},
        compiler_params=pltpu.CompilerParams(has_side_effects=_EFFECT),
    )(*[pltpu.with_memory_space_constraint(a, pltpu.HBM) for a in list(arrays) + lands])
    return (res[:3], res[3:3 + n], res[3 + n:3 + 2 * n], plan), res[-1]


def _exchange_wait(name, state, after):
    sems, srcs, lands, plan = state
    n = len(srcs)

    def body(*refs):
        srcs_, lands_ = refs[:n], refs[n:2 * n]
        send_sems, recv_sems, own_sems = refs[2 * n:2 * n + 3]
        for cp in _peer_copies(srcs_, lands_, send_sems, recv_sems, plan):
            cp.wait_send()
            cp.wait_recv()
        for cp in _own_copies(srcs_, lands_, own_sems, plan):
            cp.wait()

    hbm = lambda a: pltpu.HBM(a.shape, a.dtype)
    res = pl.pallas_call(
        body, name=name, out_shape=tuple(hbm(a) for a in list(srcs) + list(lands)),
        in_specs=[_HBM] * (2 * n) + [_SEM] * 3 + [pl.BlockSpec(memory_space=pl.ANY)] * len(after),
        out_specs=tuple([_HBM] * (2 * n)), input_output_aliases={i: i for i in range(2 * n)},
        compiler_params=pltpu.CompilerParams(has_side_effects=_EFFECT),
    )(*srcs, *lands, *sems, *after)
    return res[n:]


def _gather_start(name, arrays, plan, after):
    n = len(arrays)
    lands = plan.landing_zones(arrays)
    n_sem = n * (N_DEV - 1)

    def body(*refs):
        srcs, lands_ = refs[:n], refs[n:2 * n]
        send_sems, recv_sems, own_sems = refs[2 * n + len(after):2 * n + len(after) + 3]
        token = refs[-1]
        for cp in (_peer_copies(srcs, lands_, send_sems, recv_sems, plan, (_SIBLING,) + _SAME_CORE)
                   + _own_copies(srcs, lands_, own_sems, plan)):
            cp.start()
        token[...] = jnp.zeros_like(token)

    hbm = lambda a: pltpu.HBM(a.shape, a.dtype)
    res = pl.pallas_call(
        body, name=name,
        out_shape=(pltpu.SemaphoreType.DMA((n_sem,)), pltpu.SemaphoreType.DMA((n_sem,)), pltpu.SemaphoreType.DMA((n,)),
                   *[hbm(a) for a in arrays], *[hbm(a) for a in lands], _sds((8, 128))),
        in_specs=[_HBM] * (2 * n) + [pl.BlockSpec(memory_space=pl.ANY)] * len(after),
        out_specs=(_SEM, _SEM, _SEM, *[_HBM] * (2 * n), pl.BlockSpec(memory_space=pltpu.VMEM)),
        input_output_aliases={i: 3 + i for i in range(2 * n)},
        compiler_params=pltpu.CompilerParams(has_side_effects=_EFFECT),
    )(*[pltpu.with_memory_space_constraint(a, pltpu.HBM) for a in list(arrays) + lands], *after)
    return (res[:3], res[3:3 + n], res[3 + n:3 + 2 * n], plan), res[-1]


def _pass_on_copies(lands, send_sems, recv_sems, plan):
    x, y, c, _ = _mesh_pos()
    sibling = _peer(x, y, c, _SIBLING)[0]
    cps = []
    for i, r in enumerate(_SAME_CORE):
        owner = _peer(x, y, c, r)[1]
        for a in range(len(lands)):
            k = a * len(_SAME_CORE) + i
            cps.append(pltpu.make_async_remote_copy(
                src_ref=plan.dst(lands, a, owner), dst_ref=plan.dst(lands, a, owner), send_sem=send_sems.at[k],
                recv_sem=recv_sems.at[k], device_id=sibling, device_id_type=pl.DeviceIdType.MESH))
    return cps


def _gather_pass_on(name, state, after):
    sems, srcs, lands, plan = state
    n = len(srcs)
    n_sem = n * len(_SAME_CORE)

    def body(*refs):
        srcs_, lands_ = refs[:n], refs[n:2 * n]
        send_sems, recv_sems = refs[2 * n], refs[2 * n + 1]
        on_send, on_recv = refs[2 * n + 3 + len(after)], refs[2 * n + 4 + len(after)]
        token = refs[-1]
        arrivals = _peer_copies(srcs_, lands_, send_sems, recv_sems, plan, _SAME_CORE)
        for arrived, on in zip(arrivals, _pass_on_copies(lands_, on_send, on_recv, plan)):
            arrived.wait_recv()
            on.start()
        token[...] = jnp.zeros_like(token)

    hbm = lambda a: pltpu.HBM(a.shape, a.dtype)
    res = pl.pallas_call(
        body, name=name,
        out_shape=(pltpu.SemaphoreType.DMA((n_sem,)), pltpu.SemaphoreType.DMA((n_sem,)),
                   *[hbm(a) for a in list(srcs) + list(lands)], _sds((8, 128))),
        in_specs=[_HBM] * (2 * n) + [_SEM] * 3 + [pl.BlockSpec(memory_space=pl.ANY)] * len(after),
        out_specs=(_SEM, _SEM, *[_HBM] * (2 * n), pl.BlockSpec(memory_space=pltpu.VMEM)),
        input_output_aliases={i: 2 + i for i in range(2 * n)},
        compiler_params=pltpu.CompilerParams(has_side_effects=_EFFECT),
    )(*srcs, *lands, *sems, *after)
    return (sems, res[:2], res[2:2 + n], res[2 + n:2 + 2 * n], plan), res[-1]


def _gather_wait(name, state, after):
    sems, on_sems, srcs, lands, plan = state
    n = len(srcs)

    def body(*refs):
        srcs_, lands_ = refs[:n], refs[n:2 * n]
        send_sems, recv_sems, own_sems, on_send, on_recv = refs[2 * n:2 * n + 5]
        for cp in _peer_copies(srcs_, lands_, send_sems, recv_sems, plan, (_SIBLING,)):
            cp.wait_recv()
        for cp in _peer_copies(srcs_, lands_, send_sems, recv_sems, plan, (_SIBLING,) + _SAME_CORE):
            cp.wait_send()
        for cp in _own_copies(srcs_, lands_, own_sems, plan):
            cp.wait()
        for cp in _pass_on_copies(lands_, on_send, on_recv, plan):
            cp.wait_send()
            cp.wait_recv()

    hbm = lambda a: pltpu.HBM(a.shape, a.dtype)
    res = pl.pallas_call(
        body, name=name, out_shape=tuple(hbm(a) for a in list(srcs) + list(lands)),
        in_specs=[_HBM] * (2 * n) + [_SEM] * 5 + [pl.BlockSpec(memory_space=pl.ANY)] * len(after),
        out_specs=tuple([_HBM] * (2 * n)), input_output_aliases={i: i for i in range(2 * n)},
        compiler_params=pltpu.CompilerParams(has_side_effects=_EFFECT),
    )(*srcs, *lands, *sems, *on_sems, *after)
    return res[n:]


def _sum_parts(name, recv):
    def body(r_ref, o_ref):
        s = r_ref[0]
        for j in range(1, N_DEV):
            s = s + r_ref[j]
        o_ref[...] = s

    return pl.pallas_call(body, name=name, out_shape=_sds(recv.shape[1:], recv.dtype))(recv)


def _adam(name, recv, w, m, v):
    shape = w.shape
    R, C = shape
    tr = R
    while N_DEV * tr * C * 4 > 6 * 1024 * 1024 and tr % 32 == 0:
        tr //= 2

    def body(recv_ref, w_ref, m_ref, v_ref, g_ref, d_ref, m2_ref, v2_ref):
        g = recv_ref[0].astype(F32)
        for j in range(1, N_DEV):
            g = g + recv_ref[j].astype(F32)
        w_, m_, v_ = w_ref[...], m_ref[...], v_ref[...]
        m2 = ADAM_B1 * m_ + (1.0 - ADAM_B1) * g
        v2 = ADAM_B2 * v_ + (1.0 - ADAM_B2) * (g * g)
        m_hat = m2 / (1.0 - ADAM_B1 ** ADAM_STEP)
        v_hat = v2 / (1.0 - ADAM_B2 ** ADAM_STEP)
        g_ref[...] = g
        d_ref[...] = -ADAM_LR * (m_hat / (jnp.sqrt(v_hat) + ADAM_EPS) + ADAM_WD * w_)
        m2_ref[...] = m2
        v2_ref[...] = v2

    blk = pl.BlockSpec((tr, C), lambda i: (i, 0))
    return _pcall(body, name=name, grid=(R // tr,),
                  in_specs=[pl.BlockSpec((N_DEV, tr, C), lambda i: (0, i, 0)), blk, blk, blk],
                  out_specs=[blk] * 4, out_shape=[_sds(shape)] * 4)(recv, w, m, v)


_NAMES = ['ffn1_pre_g', 'ffn1_w_in', 'ffn1_w_out', 'ffn1_post_g', 'mix_pre_g', 'w_in', 'w_fu', 'b_f', 'gla_norm_g',
          'w_pool', 'pool_scale', 'mem_norm_g', 'w_mem_kv', 'w_up_gla', 'w_up_pool', 'w_up_xattn', 'w_o', 'mix_post_g',
          'ffn2_pre_g', 'ffn2_w_in', 'ffn2_w_out', 'ffn2_post_g', 'final_g']
_SHARDED = ['ffn1_w_in', 'ffn1_w_out', 'w_in', 'w_fu', 'w_mem_kv', 'w_up_gla', 'w_up_pool', 'w_up_xattn', 'w_o',
            'ffn2_w_in', 'ffn2_w_out']
_COL_SHARDED = ['w_up_pool', 'w_up_xattn']


def _cols_to_full(g):
    return jnp.transpose(g, (1, 0, 2)).reshape(g.shape[1], N_DEV * g.shape[2])


def _full_to_cols(f):
    R, C = f.shape
    return jnp.transpose(f.reshape(R, N_DEV, C // N_DEV), (1, 0, 2))


def _to_internal(w_in_t):
    o = 0
    parts = []
    for s in IN_SPLITS:
        parts.append(w_in_t[o:o + s])
        o += s
    q, k, v, g_out, f_low, p_in, xq, gates = parts
    f_low = jnp.pad(f_low, ((0, FLOW_W - GATE_RANK), (0, 0)))
    return jnp.concatenate([q, k, v, g_out, gates, p_in, xq, f_low], axis=0)


def _from_internal(d, d_flow):
    q, k, v, g_out = d[0:512], d[512:1024], d[1024:2048], d[2048:3072]
    gates, p_in, xq = d[3072:6144], d[6144:6656], d[6656:7168]
    return jnp.concatenate([q, k, v, g_out, d_flow[:GATE_RANK], p_in, xq, gates], axis=0)


_FFN_IN = ('ffn1_w_in', 'ffn2_w_in')
_FFN_OUT = ('ffn1_w_out', 'ffn2_w_out')
_GATHERS = {"ffn1_in": ['ffn1_w_in'], "ffn1_out": ['ffn1_w_out'],
            "mix_in": ['w_in', 'w_fu'], "mix_rest": ['w_mem_kv', 'w_up_gla', 'w_up_pool', 'w_up_xattn', 'w_o'],
            "ffn2": ['ffn2_w_in', 'ffn2_w_out']}
_MIX = _GATHERS["mix_in"] + _GATHERS["mix_rest"]


def _ffn_in_slot(ref, d):
    return ref.at[2 * (d % N_FF_BLK) + d // N_FF_BLK, pl.ds(0, FF_BLK)]


def _ffn_out_slot(ref, d):
    rows = FF_BLK // 2
    return ref.at[d // 2, pl.ds(pl.multiple_of((d % 2) * rows, rows), rows)]


def _ffn_plan(names, scatter):
    slots, shapes = {}, {}
    for a, n in enumerate(names):
        if n in _FFN_IN:
            slots[a] = _ffn_in_slot
            shapes[a] = (FF_BLK, D_MODEL) if scatter else (N_DEV, FF_PAD, D_MODEL)
        elif n in _FFN_OUT:
            slots[a] = _ffn_out_slot
            shapes[a] = (FF_BLK // 2, D_MODEL) if scatter else (N_FF_BLK, FF_PAD, D_MODEL)
    return _Plan(scatter, slots, shapes)


def _rows_to_full(g):
    return g.reshape(N_DEV * g.shape[1], g.shape[2])


def _mix_weights(gathered):
    W = {}
    for n, g in gathered.items():
        if n == "w_in":
            W["w_int_t"] = _to_internal(g.reshape(IN_WIDTH, D_MODEL))
        elif n == "w_fu":
            W["w_fu_pad"] = jnp.pad(_cols_to_full(g), ((0, FLOW_W - GATE_RANK), (0, 0)))
        else:
            W[n] = _cols_to_full(g) if n in _COL_SHARDED else _rows_to_full(g)
    return W


def _mix_chunks(G, n):
    if n == "w_in":
        return _from_internal(*G["w_int_t"]).reshape(N_DEV, IN_SHARD, D_MODEL)
    if n == "w_fu":
        return _full_to_cols(G["w_fu_pad"][:GATE_RANK].astype(BF16))
    if n in _COL_SHARDED:
        return _full_to_cols(G[n])
    return G[n].reshape(N_DEV, G[n].shape[0] // N_DEV, G[n].shape[1])


def _step(x, mem, tgt, P, Mo, Vo):
    def native(n, a):
        return jnp.swapaxes(a, 0, 1) if n in _FFN_IN + ("w_in",) else a

    P, Mo, Vo = ({n: native(n, a) for n, a in d.items()} for d in (P, Mo, Vo))
    small = {n: P[n] for n in _NAMES if n not in _SHARDED}

    gather, tokens = {}, []
    for grp, names in _GATHERS.items():
        gather[grp], tok = _gather_start("gather_" + grp, [P[n].astype(BF16) for n in names], _ffn_plan(names, False),
                                         tokens[-1:])
        tokens.append(tok)

    def pass_on(grp, after):
        return _gather_pass_on("gather_" + grp + "_on", gather[grp], after)

    def gathered(grp, after):
        return _gather_wait("gather_" + grp + "_wait", pass_on(grp, after)[0], [])

    def ffn_wo(lands):
        return lambda act: lands[0].reshape(N_FF_BLK * FF_PAD, D_MODEL)

    h1 = _pre_norm("ffn1_pre", x, small["ffn1_pre_g"], tokens[:1])
    w1t = gathered("ffn1_in", tokens[-1:] + [h1])[0].reshape(N_DEV * FF_PAD, D_MODEL)
    (x1, hm), w1o, sv1 = _ffn_fwd("ffn1", x, h1, w1t, small["ffn1_post_g"],
                                  lambda act: ffn_wo(gathered("ffn1_out", [act]))(act), small["mix_pre_g"])

    Wm = {**small, **_mix_weights(dict(zip(_GATHERS["mix_in"], gathered("mix_in", [x1]))))}
    passed = {}

    def pass_on_later(proj):
        tokens = []
        for grp in ("mix_rest", "ffn2"):
            passed[grp], tok = pass_on(grp, [proj])
            tokens.append(tok)
        return tokens

    def mix_rest(after):
        names = _GATHERS["mix_rest"]
        return _mix_weights(dict(zip(names, _gather_wait("gather_mix_rest_wait", passed["mix_rest"], after))))

    x2, h2, Wm, svm = _mix_fwd(x1, hm, mem, Wm, small["ffn2_pre_g"], pass_on_later, mix_rest)
    w2t, w2o = _gather_wait("gather_ffn2_wait", passed["ffn2"], [x2])
    w2t = w2t.reshape(N_DEV * FF_PAD, D_MODEL)
    (loss, dx3, d_final_g), w2o, sv2 = _ffn_fwd("ffn2", x2, h2, w2t, small["ffn2_post_g"], ffn_wo([w2o]),
                                                small["final_g"], tgt)

    G = dict(final_g=d_final_g)
    scat = {}

    def start(grp, names, arrays):
        scat[grp] = names, _exchange_start("scatter_" + grp, arrays, _ffn_plan(names, True))
        return scat[grp][1][1]

    def ffn_starts(tag):
        return (lambda dwo: start(tag + "_out", [tag + "_w_out"], [dwo.reshape(N_FF_BLK, FF_PAD, D_MODEL)]),
                lambda dwt: start(tag + "_in", [tag + "_w_in"], [dwt.reshape(N_DEV, FF_PAD, D_MODEL)]))

    dx2, G["ffn2_pre_g"], G["ffn2_post_g"] = _ffn_bwd(
        "ffn2", dx3, x2, small["ffn2_pre_g"], w2t, w2o, small["ffn2_post_g"], sv2, *ffn_starts("ffn2"))
    dx1, Gm = _mix_bwd(dx2, x1, mem, Wm, svm,
                       lambda Gm: start("mix", _MIX, [_mix_chunks(Gm, n) for n in _MIX]))
    G.update(Gm)
    dx, G["ffn1_pre_g"], G["ffn1_post_g"] = _ffn_bwd(
        "ffn1", dx1, x, small["ffn1_pre_g"], w1t, w1o, small["ffn1_post_g"], sv1, *ffn_starts("ffn1"))

    scat["small"] = list(small) + ["loss"], _exchange_start("gather_small_grads", [G[n] for n in small] + [loss],
                                                            _GATHER)
    recv, outs = {}, {}
    done = [dx]
    for grp in ["ffn2_out", "ffn2_in", "mix", "ffn1_out", "ffn1_in", "small"]:
        names, (state, _) = scat[grp]
        recv.update(zip(names, _exchange_wait(("gather_" if grp == "small" else "scatter_") + grp + "_wait", state,
                                              done)))
        for n in names:
            if n == "loss":
                loss = _sum_parts("loss_sum", recv[n])[0, 0]
                continue
            shp = P[n].shape
            shp2 = shp if len(shp) == 2 else (shp[0] * shp[1], shp[2])
            res = _adam("adam_" + n, recv[n].reshape((N_DEV,) + shp2), P[n].reshape(shp2), Mo[n].reshape(shp2),
                        Vo[n].reshape(shp2))
            outs[n] = [native(n, t.reshape(shp)) for t in res]
            done.append(res[-1])
    return loss, dx, outs


def kernel(x, mem, ffn1_pre_g, ffn1_w_in, ffn1_w_out, ffn1_post_g, mix_pre_g, w_in, w_fu, b_f, gla_norm_g, w_pool, pool_scale, mem_norm_g, w_mem_kv, w_up_gla, w_up_pool, w_up_xattn, w_o, mix_post_g, ffn2_pre_g, ffn2_w_in, ffn2_w_out, ffn2_post_g, final_g, loss_target, m_ffn1_pre_g, m_ffn1_w_in, m_ffn1_w_out, m_ffn1_post_g, m_mix_pre_g, m_w_in, m_w_fu, m_b_f, m_gla_norm_g, m_w_pool, m_pool_scale, m_mem_norm_g, m_w_mem_kv, m_w_up_gla, m_w_up_pool, m_w_up_xattn, m_w_o, m_mix_post_g, m_ffn2_pre_g, m_ffn2_w_in, m_ffn2_w_out, m_ffn2_post_g, m_final_g, v_ffn1_pre_g, v_ffn1_w_in, v_ffn1_w_out, v_ffn1_post_g, v_mix_pre_g, v_w_in, v_w_fu, v_b_f, v_gla_norm_g, v_w_pool, v_pool_scale, v_mem_norm_g, v_w_mem_kv, v_w_up_gla, v_w_up_pool, v_w_up_xattn, v_w_o, v_mix_post_g, v_ffn2_pre_g, v_ffn2_w_in, v_ffn2_w_out, v_ffn2_post_g, v_final_g):
    params = [ffn1_pre_g, ffn1_w_in, ffn1_w_out, ffn1_post_g, mix_pre_g, w_in, w_fu, b_f, gla_norm_g, w_pool, pool_scale, mem_norm_g, w_mem_kv, w_up_gla, w_up_pool, w_up_xattn, w_o, mix_post_g, ffn2_pre_g, ffn2_w_in, ffn2_w_out, ffn2_post_g, final_g]
    moms = [m_ffn1_pre_g, m_ffn1_w_in, m_ffn1_w_out, m_ffn1_post_g, m_mix_pre_g, m_w_in, m_w_fu, m_b_f, m_gla_norm_g, m_w_pool, m_pool_scale, m_mem_norm_g, m_w_mem_kv, m_w_up_gla, m_w_up_pool, m_w_up_xattn, m_w_o, m_mix_post_g, m_ffn2_pre_g, m_ffn2_w_in, m_ffn2_w_out, m_ffn2_post_g, m_final_g]
    vars_ = [v_ffn1_pre_g, v_ffn1_w_in, v_ffn1_w_out, v_ffn1_post_g, v_mix_pre_g, v_w_in, v_w_fu, v_b_f, v_gla_norm_g, v_w_pool, v_pool_scale, v_mem_norm_g, v_w_mem_kv, v_w_up_gla, v_w_up_pool, v_w_up_xattn, v_w_o, v_mix_post_g, v_ffn2_pre_g, v_ffn2_w_in, v_ffn2_w_out, v_ffn2_post_g, v_final_g]
    P = {n: a[0] if a.ndim > 2 else a for n, a in zip(_NAMES, params)}
    Mo = {n: a[0] if a.ndim > 2 else a for n, a in zip(_NAMES, moms)}
    Vo = {n: a[0] if a.ndim > 2 else a for n, a in zip(_NAMES, vars_)}
    loss, dx, outs = _step(x[0], mem[0], loss_target[0], P, Mo, Vo)
    out = [loss, dx[None]]
    for kind in range(4):
        for n, p in zip(_NAMES, params):
            out.append(outs[n][kind].reshape(p.shape))
    return tuple(out)
```

```python
import functools

import jax
import jax.numpy as jnp
from jax import lax
from jax.experimental import pallas as pl
from jax.experimental.pallas import tpu as pltpu

F32 = jnp.float32
BF16 = jnp.bfloat16

N_DEV = 8
D_MODEL = 1024
D_FF = 2816
FF_BLK = 2 * D_FF // N_DEV
N_FF_BLK = D_FF // FF_BLK
FF_PAD = 768
CHUNK = 64
GLA_HEADS = 4
GLA_DK = 512
GLA_DV = 1024
GLA_HDK = 128
GLA_HDV = 256
GATE_RANK = 16
GATE_TEMP = 16.0
POOL_WINDOWS = (2, 4, 8, 16)
POOL_W = 512
POOL_GD = 128
POOL_HALO = 16
XA_HEADS = 4
XA_HD = 128
XA_W = 512
EPS = 1e-6
IN_SPLITS = (GLA_DK, GLA_DK, GLA_DV, GLA_DV, GATE_RANK, POOL_W, XA_W, 3 * D_MODEL)
IN_WIDTH = sum(IN_SPLITS)
IN_SHARD = IN_WIDTH // N_DEV
INT_W = 3072 + 3072 + 1024 + 128
INT_NBLK = 3
INT_BLK = INT_W // INT_NBLK
FLOW_W = 128
INT_MAIN = INT_W - FLOW_W
_POOL_COL = (3072 + 3072) // POOL_W
_XQ_COL = (3072 + 3072 + POOL_W) // XA_W

ADAM_LR = 0.001
ADAM_B1 = 0.9
ADAM_B2 = 0.999
ADAM_EPS = 1e-08
ADAM_WD = 0.01
ADAM_STEP = 10

VMEM_LIMIT = 56 * 1024 * 1024

_NN = (((1,), (0,)), ((), ()))
_NT = (((1,), (1,)), ((), ()))
_TN = (((0,), (0,)), ((), ()))


def _pcall(body, *, name, grid, in_specs, out_specs, out_shape, scratch=(), aliases=None):
    return pl.pallas_call(
        body, name=name, grid=grid, in_specs=in_specs, out_specs=out_specs, out_shape=out_shape,
        scratch_shapes=list(scratch), input_output_aliases=aliases or {},
        compiler_params=pltpu.CompilerParams(dimension_semantics=("arbitrary",) * len(grid),
                                             vmem_limit_bytes=VMEM_LIMIT))


def _dot(a, b, dims=_NN):
    return lax.dot_general(a.astype(BF16), b.astype(BF16), dims, preferred_element_type=F32)


def _mm(name, a, b, *, grid, a_blk, a_map, b_blk, b_map, o_shape, o_blk, o_map, dims, out_dtype=F32, deps=()):
    nk = grid[2]

    def body(a_ref, b_ref, *rest):
        o_ref, scr = rest[len(deps)], rest[len(deps) + 1:]
        p = _dot(a_ref[...], b_ref[...], dims)
        if nk == 1:
            o_ref[...] = p.astype(o_ref.dtype)
        else:
            acc = scr[0]
            k = pl.program_id(2)

            @pl.when(k == 0)
            def _():
                acc[...] = p

            @pl.when(k > 0)
            def _():
                acc[...] += p

            @pl.when(k == nk - 1)
            def _():
                o_ref[...] = acc[...].astype(o_ref.dtype)

    acc_shape = tuple(d for d in o_blk if d is not None)
    return _pcall(body, name=name, grid=grid,
                  in_specs=[pl.BlockSpec(a_blk, a_map), pl.BlockSpec(b_blk, b_map)]
                  + [pl.BlockSpec(memory_space=pl.ANY)] * len(deps),
                  out_specs=pl.BlockSpec(o_blk, o_map),
                  out_shape=jax.ShapeDtypeStruct(o_shape, out_dtype),
                  scratch=[pltpu.VMEM(acc_shape, F32)] if nk > 1 else [])(a, b, *deps)


def _mm_nn(name, a, b, out_dtype=F32, tm=1024, tn=None):
    M, K = a.shape
    N = b.shape[1]
    tm, tn = min(tm, M), (tn or N)
    return _mm(name, a, b, grid=(N // tn, M // tm, 1), a_blk=(tm, K), a_map=lambda j, i, k: (i, 0),
               b_blk=(K, tn), b_map=lambda j, i, k: (0, j), o_shape=(M, N), o_blk=(tm, tn),
               o_map=lambda j, i, k: (i, j), dims=_NN, out_dtype=out_dtype)


def _mm_nt(name, a, b, out_dtype=F32, tm=1024):
    M, K = a.shape
    N = b.shape[0]
    tm = min(tm, M)
    return _mm(name, a, b, grid=(1, M // tm, 1), a_blk=(tm, K), a_map=lambda j, i, k: (i, 0),
               b_blk=(N, K), b_map=lambda j, i, k: (0, 0), o_shape=(M, N), o_blk=(tm, N),
               o_map=lambda j, i, k: (i, 0), dims=_NT, out_dtype=out_dtype)


def _mm_tn(name, a, b, out_dtype=BF16, ts=512, tn=None, tn_a=None, deps=()):
    S, M = a.shape
    N = b.shape[1]
    ts, tn, tn_a = min(ts, S), (tn or N), (tn_a or M)
    return _mm(name, a, b, grid=((N // tn) * (M // tn_a), 1, S // ts), a_blk=(ts, tn_a),
               a_map=lambda j, i, k: (k, j if tn_a < M else 0), b_blk=(ts, tn),
               b_map=lambda j, i, k: (k, j if tn < N else 0), o_shape=(M, N), o_blk=(tn_a, tn),
               o_map=lambda j, i, k: (j, 0) if tn_a < M else (0, j), dims=_TN, out_dtype=out_dtype, deps=deps)


class _Win:
    def __init__(self, arr, w, c):
        self.arr, self.w, self.c = arr, w, c


def _row_spec(x, tm):
    if isinstance(x, _Win):
        return x.arr, pl.BlockSpec((tm, x.w), functools.partial(lambda i, c: (i, c), c=x.c))
    if x.ndim == 3:
        return x, pl.BlockSpec((x.shape[0], tm, x.shape[2]), lambda i: (0, i, 0))
    return x, pl.BlockSpec((tm, x.shape[1]), lambda i: (i, 0))


def _rowwise(name, fn, rows, consts, outs, accs=(), tm=512):
    first = rows[0].arr if isinstance(rows[0], _Win) else rows[0]
    S = first.shape[1] if first.ndim == 3 else first.shape[0]
    tm = min(tm, S)
    n_in, n_out = len(rows) + len(consts), len(outs)
    arrays, in_specs = [], []
    for r in rows:
        arr, spec = _row_spec(r, tm)
        arrays.append(arr)
        in_specs.append(spec)
    for c in consts:
        arrays.append(c)
        in_specs.append(pl.BlockSpec(c.shape, functools.partial(lambda i, n: (0,) * n, n=c.ndim)))
    aliases = {}
    for k, o in enumerate(outs):
        if isinstance(o, _Win):
            aliases[len(arrays)] = k
            arrays.append(o.arr)
            in_specs.append(pl.BlockSpec(memory_space=pl.ANY))
    n_thru = len(aliases)
    out_specs = [_row_spec(o, tm)[1] for o in outs]
    out_specs += [pl.BlockSpec(a.shape, functools.partial(lambda i, n: (0,) * n, n=len(a.shape))) for a in accs]
    out_shape = [_sds(o.arr.shape, o.arr.dtype) if isinstance(o, _Win) else _sds(o.shape, o.dtype) for o in outs]

    def body(*refs):
        res = fn(*[r[...] for r in refs[:n_in]])
        if not isinstance(res, (tuple, list)):
            res = (res,)
        orefs = refs[n_in + n_thru:]
        for r, v in zip(orefs[:n_out], res[:n_out]):
            r[...] = v.astype(r.dtype)
        i = pl.program_id(0)
        for r, v in zip(orefs[n_out:], res[n_out:]):
            @pl.when(i == 0)
            def _(r=r, v=v):
                r[...] = v.astype(r.dtype)

            @pl.when(i > 0)
            def _(r=r, v=v):
                r[...] += v.astype(r.dtype)

    return _pcall(body, name=name, grid=(S // tm,), in_specs=in_specs, out_specs=out_specs,
                  out_shape=out_shape + [_sds(a.shape, a.dtype) for a in accs], aliases=aliases)(*arrays)


def _sds(shape, dtype=F32):
    return jax.ShapeDtypeStruct(shape, dtype)


def _rms(x, g):
    return x * lax.rsqrt(jnp.mean(x * x, axis=-1, keepdims=True) + EPS) * g


def _rms_bwd(x, g, dy):
    _, vjp = jax.vjp(_rms, x, g)
    return vjp(dy)


def _sigmoid(x):
    return 0.5 * jnp.tanh(0.5 * x) + 0.5


def _silu(x):
    return x * _sigmoid(x)


def _log_sigmoid(f):
    return jnp.minimum(f, 0.0) - jnp.log(1.0 + jnp.exp(-jnp.abs(f)))


def _head_rms_gate(o, g_out, gn):
    parts = [_rms(o[:, h * GLA_HDV:(h + 1) * GLA_HDV], gn[:, h * GLA_HDV:(h + 1) * GLA_HDV]) for h in range(GLA_HEADS)]
    return jnp.concatenate(parts, axis=-1) * _silu(g_out.astype(F32))


def _merge(gates, ya, yb, yc):
    gates, ya, yb, yc = (t.astype(F32) for t in (gates, ya, yb, yc))
    return (_sigmoid(gates[:, :D_MODEL]) * ya + _sigmoid(gates[:, D_MODEL:2 * D_MODEL]) * yb
            + _sigmoid(gates[:, 2 * D_MODEL:]) * yc)


def _tri_dot(t, x):
    hi = x.astype(BF16)
    r1 = x - hi.astype(F32)
    mid = r1.astype(BF16)
    lo = (r1 - mid.astype(F32)).astype(BF16)
    d = functools.partial(lax.dot_general, dimension_numbers=_NN, preferred_element_type=F32)
    return d(t, hi) + d(t, mid) + d(t, lo)


def _pre_norm(name, x, g, after):
    S = x.shape[0]
    tm = min(512, S)

    def body(x_ref, g_ref, *rest):
        rest[-1][...] = _rms(x_ref[...], g_ref[...]).astype(BF16)

    row = pl.BlockSpec((tm, D_MODEL), lambda i: (i, 0))
    return _pcall(body, name=name, grid=(S // tm,),
                  in_specs=[row, pl.BlockSpec((1, D_MODEL), lambda i: (0, 0))] + [pl.BlockSpec(memory_space=pl.ANY)] * len(after),
                  out_specs=row, out_shape=_sds(x.shape, BF16))(x, g, *after)


def _ffn_fwd(tag, x, h, wt, gpost, get_wo, g_next, tgt=None):
    S = x.shape[0]
    tm = min(512, S)

    def up_body(h_ref, w_ref, u_ref, act_ref):
        hh = h_ref[...]
        for j in range(N_FF_BLK):
            ab = _dot(hh, w_ref[2 * FF_PAD * j:2 * FF_PAD * (j + 1), :], _NT)
            u_ref[:, 2 * FF_PAD * j:2 * FF_PAD * (j + 1)] = ab.astype(BF16)
            act_ref[:, FF_PAD * j:FF_PAD * (j + 1)] = (_silu(ab[:, :FF_PAD]) * ab[:, FF_PAD:]).astype(BF16)

    u, act = _pcall(
        up_body, name=tag + "_up", grid=(S // tm,),
        in_specs=[pl.BlockSpec((tm, D_MODEL), lambda i: (i, 0)),
                  pl.BlockSpec(wt.shape, lambda i: (0, 0), pipeline_mode=pl.Buffered(1))],
        out_specs=[pl.BlockSpec((tm, N_DEV * FF_PAD), lambda i: (i, 0)),
                   pl.BlockSpec((tm, N_FF_BLK * FF_PAD), lambda i: (i, 0))],
        out_shape=[_sds((S, N_DEV * FF_PAD), BF16), _sds((S, N_FF_BLK * FF_PAD), BF16)],
    )(h, wt)

    def down_body(act_ref, w_ref, x_ref, g_ref, gn_ref, f_ref, xo_ref, hn_ref):
        f = _dot(act_ref[...], w_ref[...])
        f_ref[...] = f.astype(BF16)
        xo = x_ref[...] + 0.5 * _rms(f, g_ref[...])
        xo_ref[...] = xo
        hn_ref[...] = _rms(xo, gn_ref[...]).astype(BF16)

    def down_head_body(act_ref, w_ref, x_ref, g_ref, gn_ref, t_ref, f_ref, dxo_ref, loss_ref, dgn_ref):
        i = pl.program_id(0)
        f = _dot(act_ref[...], w_ref[...])
        f_ref[...] = f.astype(BF16)
        xo = x_ref[...] + 0.5 * _rms(f, g_ref[...])
        out, vjp = jax.vjp(_rms, xo, gn_ref[...])
        e = out - t_ref[...]
        loss = 0.5 * jnp.sum(jnp.mean(e * e, axis=-1, keepdims=True), axis=0, keepdims=True)
        dxo_ref[...], dgn = vjp(e * (1.0 / D_MODEL))

        @pl.when(i == 0)
        def _():
            loss_ref[...] = jnp.broadcast_to(loss, loss_ref.shape)
            dgn_ref[...] = dgn

        @pl.when(i > 0)
        def _():
            loss_ref[...] += jnp.broadcast_to(loss, loss_ref.shape)
            dgn_ref[...] += dgn

    wo = get_wo(act)
    row = pl.BlockSpec((tm, D_MODEL), lambda i: (i, 0))
    vec = pl.BlockSpec((1, D_MODEL), lambda i: (0, 0))
    if tgt is not None:
        f, dxo, loss, dg_next = _pcall(
            down_head_body, name=tag + "_down", grid=(S // tm,),
            in_specs=[pl.BlockSpec((tm, N_FF_BLK * FF_PAD), lambda i: (i, 0)),
                      pl.BlockSpec(wo.shape, lambda i: (0, 0), pipeline_mode=pl.Buffered(1)), row, vec, vec, row],
            out_specs=[row, row, pl.BlockSpec((1, 128), lambda i: (0, 0)), vec],
            out_shape=[_sds((S, D_MODEL), BF16), _sds((S, D_MODEL)), _sds((1, 128)), _sds((1, D_MODEL))],
        )(act, wo, x, gpost, g_next, tgt)
        return (loss, dxo, dg_next), wo, (h, u, act, f)
    f, xo, h_next = _pcall(
        down_body, name=tag + "_down", grid=(S // tm,),
        in_specs=[pl.BlockSpec((tm, N_FF_BLK * FF_PAD), lambda i: (i, 0)),
                  pl.BlockSpec(wo.shape, lambda i: (0, 0), pipeline_mode=pl.Buffered(1)), row, vec, vec],
        out_specs=[row, row, row], out_shape=[_sds((S, D_MODEL), BF16), _sds((S, D_MODEL)), _sds((S, D_MODEL), BF16)],
    )(act, wo, x, gpost, g_next)
    return (xo, h_next), wo, (h, u, act, f)


def _ffn_bwd(tag, dxo, x, gpre, wt, wo, gpost, saved, on_dwo=None, on_dwt=None):
    h, u, act, f = saved
    S = x.shape[0]
    tm = min(256, S)

    def dact_body(f_ref, dxo_ref, g_ref, w_ref, u_ref, df_ref, du_ref, dg_ref):
        i = pl.program_id(0)
        df, dg = _rms_bwd(f_ref[...].astype(F32), g_ref[...], 0.5 * dxo_ref[...])
        df = df.astype(BF16)
        df_ref[...] = df

        @pl.when(i == 0)
        def _():
            dg_ref[...] = dg

        @pl.when(i > 0)
        def _():
            dg_ref[...] += dg

        for j in range(N_FF_BLK):
            dact = _dot(df, w_ref[FF_PAD * j:FF_PAD * (j + 1), :], _NT)
            ab = u_ref[:, 2 * FF_PAD * j:2 * FF_PAD * (j + 1)].astype(F32)
            a, b = ab[:, :FF_PAD], ab[:, FF_PAD:]
            sg = _sigmoid(a)
            du_ref[:, 2 * FF_PAD * j:2 * FF_PAD * j + FF_PAD] = (dact * b * (sg * (1.0 + a * (1.0 - sg)))).astype(BF16)
            du_ref[:, 2 * FF_PAD * j + FF_PAD:2 * FF_PAD * (j + 1)] = (dact * (a * sg)).astype(BF16)

    row = pl.BlockSpec((tm, D_MODEL), lambda i: (i, 0))
    vec = pl.BlockSpec((1, D_MODEL), lambda i: (0, 0))
    u_spec = pl.BlockSpec((tm, N_DEV * FF_PAD), lambda i: (i, 0))
    df, du, dgpost = _pcall(
        dact_body, name=tag + "_dact", grid=(S // tm,),
        in_specs=[row, row, vec, pl.BlockSpec(wo.shape, lambda i: (0, 0), pipeline_mode=pl.Buffered(1)), u_spec],
        out_specs=[row, u_spec, vec],
        out_shape=[_sds((S, D_MODEL), BF16), _sds((S, N_DEV * FF_PAD), BF16), _sds((1, D_MODEL))])(f, dxo, gpost, wo, u)
    dwo = _mm_tn(tag + "_dwo", act, df, ts=S, tn_a=FF_PAD)
    dwt = _mm_tn(tag + "_dwt", du, h, ts=S, tn_a=FF_PAD, deps=[on_dwo(dwo)] if on_dwo is not None else [])
    dx, dgpre = _dh_pre_norm_bwd(tag + "_dh", [du], wt, x, dxo, gpre, [on_dwt(dwt)] if on_dwt is not None else [])
    return dx, dgpre, dgpost


def _dh_pre_norm_bwd(name, dzs, wt, x, dres, g, deps):
    S = x.shape[0]
    ks = [dz.shape[1] for dz in dzs]

    def body(*refs):
        dz_refs = refs[:len(dzs)]
        w_ref, x_ref, dres_ref, g_ref = refs[len(dzs):len(dzs) + 4]
        dx_ref, dg_ref = refs[len(dzs) + 4 + len(deps):]
        i = pl.program_id(0)
        dh, o = None, 0
        for dz_ref, k in zip(dz_refs, ks):
            p = _dot(dz_ref[...], w_ref[o:o + k, :])
            dh, o = (p if dh is None else dh + p), o + k
        dx, dg = _rms_bwd(x_ref[...], g_ref[...], dh)
        dx_ref[...] = dx + dres_ref[...]

        @pl.when(i == 0)
        def _():
            dg_ref[...] = dg

        @pl.when(i > 0)
        def _():
            dg_ref[...] += dg

    th = min(512, S)
    row = pl.BlockSpec((th, D_MODEL), lambda i: (i, 0))
    vec = pl.BlockSpec((1, D_MODEL), lambda i: (0, 0))
    return _pcall(
        body, name=name, grid=(S // th,),
        in_specs=[pl.BlockSpec((th, k), lambda i: (i, 0)) for k in ks]
        + [pl.BlockSpec(wt.shape, lambda i: (0, 0), pipeline_mode=pl.Buffered(1)), row, row, vec]
        + [pl.BlockSpec(memory_space=pl.ANY)] * len(deps),
        out_specs=[row, vec], out_shape=[_sds((S, D_MODEL)), _sds((1, D_MODEL))])(*dzs, wt, x, dres, g, *deps)


GLA_G = 8


def _gla_tile_common(k, flow, wfu, bf):
    f = _dot(flow, wfu) + bf
    la = _log_sigmoid(f) * (1.0 / GATE_TEMP)
    tri = _tri_matrix(True)
    ws, ds = [], []
    for g in range(k.shape[0] // CHUNK):
        b = _tri_dot(tri, la[g * CHUNK:(g + 1) * CHUNK])
        b_end = b[CHUNK - 1:CHUNK, :]
        ws.append(jnp.exp(b_end - b))
        ds.append(jnp.exp(b_end))
    w = jnp.concatenate(ws, axis=0)
    return f, w, k * w, ds


def _tri_matrix(lower):
    r = lax.broadcasted_iota(jnp.int32, (CHUNK, CHUNK), 0)
    c = lax.broadcasted_iota(jnp.int32, (CHUNK, CHUNK), 1)
    return jnp.where((r >= c) if lower else (r <= c), 1.0, 0.0).astype(BF16)


def _heads():
    return [(slice(h * GLA_HDK, (h + 1) * GLA_HDK), slice(h * GLA_HDV, (h + 1) * GLA_HDV)) for h in range(GLA_HEADS)]


def _gla_fwd(proj, wfu, bf, gn, deps=()):
    S = proj.shape[0]
    G = min(GLA_G, S // CHUNK)
    T = G * CHUNK
    nc = S // CHUNK

    def body(q_ref, k_ref, v_ref, go_ref, flow_ref, wfu_ref, bf_ref, gn_ref, *rest):
        ya_ref, st_ref, state = rest[len(deps):]
        @pl.when(pl.program_id(0) == 0)
        def _():
            state[...] = jnp.zeros_like(state)

        _, _, kt, ds = _gla_tile_common(k_ref[...], flow_ref[...], wfu_ref[...], bf_ref[...])
        q = q_ref[...].astype(F32) * (GLA_HDK ** -0.5)
        v = v_ref[...]
        rows = [slice(g * CHUNK, (g + 1) * CHUNK) for g in range(G)]
        kv = [[_dot(v[r, vs], kt[r, ks], _TN) for ks, vs in _heads()] for r in rows]
        st = [state[vs, :] for _, vs in _heads()]
        o = []
        for g, r in enumerate(rows):
            outs = []
            for h, (ks, vs) in enumerate(_heads()):
                st[h] = st[h] * ds[g][:, ks] + kv[g][h]
                st_ref[g, vs, :] = st[h]
                outs.append(_dot(q[r, ks], st[h], _NT))
            o.append(jnp.concatenate(outs, axis=-1))
        for h, (_, vs) in enumerate(_heads()):
            state[vs, :] = st[h]
        ya_ref[...] = _head_rms_gate(jnp.concatenate(o, axis=0), go_ref[...], gn_ref[...]).astype(BF16)

    return _pcall(
        body, name="gla_fwd", grid=(S // T,),
        in_specs=[pl.BlockSpec((T, GLA_DK), lambda c: (c, 0)), pl.BlockSpec((T, GLA_DK), lambda c: (c, 1)),
                  pl.BlockSpec((T, GLA_DV), lambda c: (c, 1)), pl.BlockSpec((T, GLA_DV), lambda c: (c, 2)),
                  pl.BlockSpec((T, FLOW_W), lambda c: (c, (INT_W - FLOW_W) // FLOW_W)),
                  pl.BlockSpec(wfu.shape, lambda c: (0, 0)), pl.BlockSpec(bf.shape, lambda c: (0, 0)),
                  pl.BlockSpec(gn.shape, lambda c: (0, 0))]
        + [pl.BlockSpec(memory_space=pl.ANY)] * len(deps),
        out_specs=[pl.BlockSpec((T, GLA_DV), lambda c: (c, 0)),
                   pl.BlockSpec((G, GLA_DV, GLA_HDK), lambda c: (c, 0, 0))],
        out_shape=[_sds((S, GLA_DV), BF16), _sds((nc, GLA_DV, GLA_HDK))],
        scratch=[pltpu.VMEM((GLA_DV, GLA_HDK), F32)])(proj, proj, proj, proj, proj, wfu, bf, gn, *deps)


def _gla_bwd(proj, wfu, bf, gn, states, d_ya_in, dmain):
    S = proj.shape[0]
    G = min(GLA_G, S // CHUNK)
    T = G * CHUNK
    nt = S // T

    def body(q_ref, k_ref, v_ref, go_ref, flow_ref, wfu_ref, bf_ref, gn_ref, st_ref, stp_ref, dya_ref, dmain_in,
             dqkvg_ref, dflow_ref, dwfu_ref, dbf_ref, dgn_ref, dstate):
        dq_ref = dqkvg_ref.at[:, 0:GLA_DK]
        dk_ref = dqkvg_ref.at[:, GLA_DK:2 * GLA_DK]
        dv_ref = dqkvg_ref.at[:, 2 * GLA_DK:2 * GLA_DK + GLA_DV]
        dgo_ref = dqkvg_ref.at[:, 2 * GLA_DK + GLA_DV:]
        step = pl.program_id(0)

        @pl.when(step == 0)
        def _():
            dstate[...] = jnp.zeros_like(dstate)
            dwfu_ref[...] = jnp.zeros_like(dwfu_ref)
            dbf_ref[...] = jnp.zeros_like(dbf_ref)
            dgn_ref[...] = jnp.zeros_like(dgn_ref)

        flow, wfu_v = flow_ref[...], wfu_ref[...]
        f, w, kt, ds = _gla_tile_common(k_ref[...], flow, wfu_v, bf_ref[...])
        q = q_ref[...].astype(F32) * (GLA_HDK ** -0.5)
        v = v_ref[...]
        rows = [slice(g * CHUNK, (g + 1) * CHUNK) for g in range(G)]
        o = jnp.concatenate([jnp.concatenate([_dot(q[r, ks], st_ref[g, vs, :], _NT) for ks, vs in _heads()], axis=-1)
                             for g, r in enumerate(rows)], axis=0)
        _, vjp = jax.vjp(_head_rms_gate, o, go_ref[...], gn_ref[...])
        dout, dgo, dgn = vjp(dya_ref[...].astype(F32))
        dgo_ref[...] = dgo.astype(dgo_ref.dtype)
        dgn_ref[...] += dgn
        dq = [jnp.concatenate([_dot(dout[r, vs], st_ref[g, vs, :]) for _, vs in _heads()], axis=-1)
              for g, r in enumerate(rows)]
        qdo = [[_dot(dout[r, vs], q[r, ks], _TN) for ks, vs in _heads()] for r in rows]
        dq_ref[...] = (jnp.concatenate(dq, axis=0) * (GLA_HDK ** -0.5)).astype(dq_ref.dtype)
        has_prev = (step < nt - 1).astype(F32)
        carry = [dstate[vs, :] for _, vs in _heads()]
        dkt, dv, dd = [None] * G, [None] * G, [None] * G
        for g in reversed(range(G)):
            r = rows[g]
            dkts, dvs, dds = [], [], []
            for h, (ks, vs) in enumerate(_heads()):
                dst = carry[h] + qdo[g][h]
                dkts.append(_dot(v[r, vs], dst))
                dvs.append(_dot(kt[r, ks], dst, _NT))
                st_prev = st_ref[g - 1, vs, :] if g > 0 else stp_ref[vs, :] * has_prev
                dds.append(jnp.sum(dst * st_prev, axis=0, keepdims=True))
                carry[h] = dst * ds[g][:, ks]
            dkt[g], dv[g], dd[g] = (jnp.concatenate(t, axis=-1) for t in (dkts, dvs, dds))
        for h, (_, vs) in enumerate(_heads()):
            dstate[vs, :] = carry[h]
        dkt = jnp.concatenate(dkt, axis=0)
        dv_ref[...] = jnp.concatenate(dv, axis=0).astype(dv_ref.dtype)
        dk_ref[...] = (dkt * w).astype(dk_ref.dtype)
        de = dkt * kt
        tri = _tri_matrix(False)
        dla = []
        for g, r in enumerate(rows):
            db_end = jnp.sum(de[r], axis=0, keepdims=True) + dd[g] * ds[g]
            dla.append(db_end - _tri_dot(tri, de[r]))
        df = jnp.concatenate(dla, axis=0) * (1.0 - _sigmoid(f)) * (1.0 / GATE_TEMP)
        dflow_ref[...] = _dot(df, wfu_v, _NT).astype(dflow_ref.dtype)
        dwfu_ref[...] += _dot(flow, df, _TN)
        dbf_ref[...] += jnp.sum(df, axis=0, keepdims=True)

    rt = lambda s: nt - 1 - s
    return _pcall(
        body, name="gla_bwd", grid=(nt,),
        in_specs=[pl.BlockSpec((T, GLA_DK), lambda s: (rt(s), 0)), pl.BlockSpec((T, GLA_DK), lambda s: (rt(s), 1)),
                  pl.BlockSpec((T, GLA_DV), lambda s: (rt(s), 1)), pl.BlockSpec((T, GLA_DV), lambda s: (rt(s), 2)),
                  pl.BlockSpec((T, FLOW_W), lambda s: (rt(s), (INT_W - FLOW_W) // FLOW_W)),
                  pl.BlockSpec(wfu.shape, lambda s: (0, 0)), pl.BlockSpec(bf.shape, lambda s: (0, 0)),
                  pl.BlockSpec(gn.shape, lambda s: (0, 0)),
                  pl.BlockSpec((G, GLA_DV, GLA_HDK), lambda s: (rt(s), 0, 0)),
                  pl.BlockSpec((None, GLA_DV, GLA_HDK), lambda s: (jnp.maximum(rt(s) * G - 1, 0), 0, 0)),
                  pl.BlockSpec((T, GLA_DV), lambda s: (rt(s), 0)), pl.BlockSpec(memory_space=pl.ANY)],
        out_specs=[pl.BlockSpec((T, 2 * GLA_DK + 2 * GLA_DV), lambda s: (rt(s), 0)),
                   pl.BlockSpec((T, FLOW_W), lambda s: (rt(s), 0)),
                   pl.BlockSpec(wfu.shape, lambda s: (0, 0)), pl.BlockSpec(bf.shape, lambda s: (0, 0)),
                   pl.BlockSpec(gn.shape, lambda s: (0, 0))],
        out_shape=[_sds(dmain.shape, dmain.dtype), _sds((S, FLOW_W), BF16), _sds(wfu.shape), _sds(bf.shape),
                   _sds(gn.shape)],
        scratch=[pltpu.VMEM((GLA_DV, GLA_HDK), F32)], aliases={11: 0},
    )(proj, proj, proj, proj, proj, wfu, bf, gn, states, states, d_ya_in, dmain)


def _pool_counts(tm, i):
    t = (lax.broadcasted_iota(jnp.int32, (tm, POOL_GD), 0) + i * tm + 1).astype(F32)
    return [jnp.minimum(t, float(w)) for w in POOL_WINDOWS]


def _pool_fwd(proj, w_pool, pool_scale, tm=512):
    S = proj.shape[0]
    tm = min(tm, S // 2)
    col = _POOL_COL
    hb = tm // POOL_HALO

    def body(p_ref, halo_ref, wp_ref, sc_ref, mixed_ref, out_ref):
        i = pl.program_id(0)
        p = p_ref[...].astype(F32)
        halo = halo_ref[...].astype(F32) * (i > 0).astype(F32)
        ext = jnp.concatenate([halo, p], axis=0)
        n = tm + POOL_HALO
        sums, acc, k = {}, ext, 1
        while k < POOL_WINDOWS[-1]:
            acc = acc + pltpu.roll(acc, k, axis=0)
            k *= 2
            sums[k] = acc
        cnts = _pool_counts(tm, i)
        mixed, lin = [], []
        for g, w in enumerate(POOL_WINDOWS):
            ls = slice(g * POOL_GD, (g + 1) * POOL_GD)
            m = sums[w][POOL_HALO:n, ls] / cnts[g] - p[:, ls]
            mixed.append(m)
            lin.append(_dot(m, wp_ref[g]))
        mixed_ref[...] = jnp.concatenate(mixed, axis=-1)
        out_ref[...] = (jnp.concatenate(lin, axis=-1) * sc_ref[...]).astype(out_ref.dtype)

    return _pcall(
        body, name="pool_fwd", grid=(S // tm,),
        in_specs=[pl.BlockSpec((tm, POOL_W), lambda i: (i, col)),
                  pl.BlockSpec((POOL_HALO, POOL_W), lambda i: (jnp.maximum(i * hb - 1, 0), col)),
                  pl.BlockSpec(w_pool.shape, lambda i: (0, 0, 0)), pl.BlockSpec(pool_scale.shape, lambda i: (0, 0))],
        out_specs=[pl.BlockSpec((tm, POOL_W), lambda i: (i, 0)), pl.BlockSpec((tm, POOL_W), lambda i: (i, 0))],
        out_shape=[_sds((S, POOL_W)), _sds((S, POOL_W), BF16)])(proj, proj, w_pool, pool_scale)


def _pool_lin_bwd(dout, mixed, w_pool, pool_scale):
    S = dout.shape[0]

    def fn(dout, mixed, wp, sc):
        dlin = dout * sc
        dm, dwp, lin = [], [], []
        for g in range(len(POOL_WINDOWS)):
            ls = slice(g * POOL_GD, (g + 1) * POOL_GD)
            lin.append(_dot(mixed[:, ls], wp[g]))
            dm.append(_dot(dlin[:, ls], wp[g], _NT))
            dwp.append(_dot(mixed[:, ls], dlin[:, ls], _TN))
        dsc = jnp.sum(dout * jnp.concatenate(lin, axis=-1), axis=0, keepdims=True)
        return jnp.concatenate(dm, axis=-1), jnp.concatenate(dwp, axis=0), dsc

    return _rowwise("pool_lin_bwd", fn, [dout, mixed], [w_pool, pool_scale], [_sds((S, POOL_W))],
                    [_sds((len(POOL_WINDOWS) * POOL_GD, POOL_GD)), _sds((1, POOL_W))])


def _pool_win_bwd(dmixed, dmain, tm=512):
    S = dmixed.shape[0]
    tm = min(tm, S // 2)
    nt = S // tm
    hb = tm // POOL_HALO

    def body(dm_ref, halo_ref, dmain_in, dp_ref):
        i = pl.program_id(0)
        dm = dm_ref[...]
        halo = halo_ref[...] * (i < nt - 1).astype(F32)
        cnts = _pool_counts(tm, i)
        cnts_h = [c[:POOL_HALO] for c in _pool_counts(tm, i + 1)]
        r = jnp.concatenate([jnp.concatenate([dm[:, g * POOL_GD:(g + 1) * POOL_GD] / cnts[g] for g in range(4)], axis=-1),
                             jnp.concatenate([halo[:, g * POOL_GD:(g + 1) * POOL_GD] / cnts_h[g] for g in range(4)], axis=-1)],
                            axis=0)
        n = tm + POOL_HALO
        sums, acc, k = {}, r, 1
        while k < POOL_WINDOWS[-1]:
            acc = acc + pltpu.roll(acc, n - k, axis=0)
            k *= 2
            sums[k] = acc
        dp = [sums[w][:tm, g * POOL_GD:(g + 1) * POOL_GD] for g, w in enumerate(POOL_WINDOWS)]
        dp_ref[...] = (jnp.concatenate(dp, axis=-1) - dm).astype(dp_ref.dtype)

    return _pcall(
        body, name="pool_win_bwd", grid=(nt,),
        in_specs=[pl.BlockSpec((tm, POOL_W), lambda i: (i, 0)),
                  pl.BlockSpec((POOL_HALO, POOL_W), lambda i: (jnp.minimum((i + 1) * hb, S // POOL_HALO - 1), 0)),
                  pl.BlockSpec(memory_space=pl.ANY)],
        out_specs=pl.BlockSpec((tm, POOL_W), lambda i: (i, _POOL_COL)),
        out_shape=_sds(dmain.shape, dmain.dtype), aliases={2: 0})(dmixed, dmixed, dmain)


def _xattn_probs(q, kv, h):
    hs = slice(h * XA_HD, (h + 1) * XA_HD)
    s = _dot(q[:, hs], kv[:, hs], _NT) * (XA_HD ** -0.5)
    s = s - jnp.max(s, axis=-1, keepdims=True)
    e = jnp.exp(s)
    return e / jnp.sum(e, axis=-1, keepdims=True)


def _xattn_fwd(proj, kv):
    S = proj.shape[0]

    def fn(q, kv):
        outs = []
        for h in range(XA_HEADS):
            p = _xattn_probs(q, kv, h)
            outs.append(_dot(p, kv[:, XA_W + h * XA_HD:XA_W + (h + 1) * XA_HD]))
        return jnp.concatenate(outs, axis=-1)

    return _rowwise("xattn_fwd", fn, [_Win(proj, XA_W, _XQ_COL)], [kv], [_sds((S, XA_W), BF16)])[0]


def _xattn_bwd(proj, kv, dxa, dmain):
    def fn(q, dxa, kv):
        dqs, dks, dvs = [], [], []
        for h in range(XA_HEADS):
            hs = slice(h * XA_HD, (h + 1) * XA_HD)
            vh = kv[:, XA_W + h * XA_HD:XA_W + (h + 1) * XA_HD]
            p = _xattn_probs(q, kv, h)
            dp = _dot(dxa[:, hs], vh, _NT)
            ds = p * (dp - jnp.sum(p * dp, axis=-1, keepdims=True)) * (XA_HD ** -0.5)
            dqs.append(_dot(ds, kv[:, hs]))
            dks.append(_dot(ds, q[:, hs], _TN))
            dvs.append(_dot(p, dxa[:, hs], _TN))
        return jnp.concatenate(dqs, axis=-1), jnp.concatenate(dks + dvs, axis=-1)

    return _rowwise("xattn_bwd", fn, [_Win(proj, XA_W, _XQ_COL), dxa], [kv], [_Win(dmain, XA_W, _XQ_COL)],
                    [_sds(kv.shape)])


def _mix_fwd(x1, h, mem, W, g_next, on_proj=None, late_weights=None):
    S = x1.shape[0]
    M = mem.shape[0]
    tm = min(512, S)
    def proj_body(h_ref, w_ref, o_ref):
        hh = h_ref[...]
        for j in range(INT_NBLK):
            o_ref[:, INT_BLK * j:INT_BLK * (j + 1)] = _dot(hh, w_ref[INT_BLK * j:INT_BLK * (j + 1), :], _NT).astype(BF16)

    proj = _pcall(
        proj_body, name="mix_proj", grid=(S // tm,),
        in_specs=[pl.BlockSpec((tm, D_MODEL), lambda i: (i, 0)),
                  pl.BlockSpec((INT_W, D_MODEL), lambda i: (0, 0), pipeline_mode=pl.Buffered(1))],
        out_specs=pl.BlockSpec((tm, INT_W), lambda i: (i, 0)), out_shape=_sds((S, INT_W), BF16))(h, W["w_int_t"])
    ya_in, states = _gla_fwd(proj, W["w_fu_pad"], W["b_f"], W["gla_norm_g"],
                             on_proj(proj) if on_proj is not None else [])
    mixed, pool_out = _pool_fwd(proj, W["w_pool"], W["pool_scale"])
    mem_n = _rowwise("mem_norm", lambda m, g: _rms(m, g), [mem], [W["mem_norm_g"]], [_sds((M, D_MODEL), BF16)])[0]
    if late_weights is not None:
        W = {**W, **late_weights([pool_out])}
    kv = _mm_nn("mem_kv", mem_n, W["w_mem_kv"])
    xa = _xattn_fwd(proj, kv)
    def out_fn(gates, ya_in, pool_out, xa, x1, wa, wb, wc, wo, g, gn):
        ya, yb, yc = (_dot(t, w).astype(BF16) for t, w in ((ya_in, wa), (pool_out, wb), (xa, wc)))
        merged = _merge(gates, ya, yb, yc).astype(BF16)
        y = _dot(merged, wo)
        x2 = x1 + _rms(y, g)
        return ya, yb, yc, merged, y, x2, _rms(x2, gn)

    ya, yb, yc, merged, y, x2, h_next = _rowwise(
        "mix_out", out_fn, [_Win(proj, 3 * D_MODEL, 1), ya_in, pool_out, xa, x1],
        [W["w_up_gla"], W["w_up_pool"], W["w_up_xattn"], W["w_o"], W["mix_post_g"], g_next],
        [_sds((S, D_MODEL), BF16)] * 5 + [_sds((S, D_MODEL)), _sds((S, D_MODEL), BF16)], tm=256)
    return x2, h_next, W, (h, proj, states, ya_in, mixed, pool_out, mem_n, kv, xa, ya, yb, yc, merged, y)


def _mix_bwd(dx2, x1, mem, W, saved, on_grads=None):
    h, proj, states, ya_in, mixed, pool_out, mem_n, kv, xa, ya, yb, yc, merged, y = saved
    S = x1.shape[0]

    def out_bwd(y, dx2, gates, ya, yb, yc, g, wo, wa, wb, wc):
        dy, dg = _rms_bwd(y.astype(F32), g, dx2)
        dy = dy.astype(BF16)
        _, vjp = jax.vjp(_merge, gates, ya, yb, yc)
        dgates, dya, dyb, dyc = vjp(_dot(dy, wo, _NT))
        return (dy, dgates, dya, dyb, dyc, _dot(dya, wa, _NT), _dot(dyb, wb, _NT), _dot(dyc, wc, _NT), dg)

    dmain = lax.empty((S, INT_MAIN), BF16)
    dy, dmain, dya, dyb, dyc, d_ya_in, d_pool_out, d_xa, d_mix_post_g = _rowwise(
        "mix_out_bwd", out_bwd, [y, dx2, _Win(proj, 3 * D_MODEL, 1), ya, yb, yc],
        [W["mix_post_g"], W["w_o"], W["w_up_gla"], W["w_up_pool"], W["w_up_xattn"]],
        [_sds((S, D_MODEL), BF16), _Win(dmain, 3 * D_MODEL, 1)] + [_sds((S, D_MODEL), BF16)] * 4
        + [_sds((S, POOL_W), BF16), _sds((S, XA_W), BF16)], [_sds((1, D_MODEL))], tm=256)
    d_w_o = _mm_tn("d_w_o", merged, dy, ts=S, tn_a=256)
    d_w_up_gla = _mm_tn("d_w_up_gla", ya_in, dya, ts=S, tn_a=256)
    d_w_up_pool = _mm_tn("d_w_up_pool", pool_out, dyb, ts=S, tn_a=128)
    d_w_up_xattn = _mm_tn("d_w_up_xattn", xa, dyc, ts=S, tn_a=128)

    dmain, dflow, d_wfu_pad, d_b_f, d_gla_norm_g = _gla_bwd(proj, W["w_fu_pad"], W["b_f"], W["gla_norm_g"], states,
                                                            d_ya_in, dmain)
    dmixed, d_w_pool, d_pool_scale = _pool_lin_bwd(d_pool_out, mixed, W["w_pool"], W["pool_scale"])
    dmain = _pool_win_bwd(dmixed, dmain)
    dmain, dkv = _xattn_bwd(proj, kv, d_xa, dmain)
    d_w_mem_kv = _mm_tn("d_w_mem_kv", mem_n, dkv)
    dmem_n = _mm_nt("d_mem_n", dkv, W["w_mem_kv"])
    d_mem_norm_g = _rowwise("mem_norm_bwd", lambda m, d, g: _rms_bwd(m, g, d)[1], [mem, dmem_n], [W["mem_norm_g"]], [],
                            [_sds((1, D_MODEL))])[0]
    d_w_int_t = (_mm_tn("d_w_int", dmain, h, ts=S, tn_a=INT_MAIN // 8), _mm_tn("d_w_flow", dflow, h, ts=S))
    grads = dict(
        w_int_t=d_w_int_t, w_fu_pad=d_wfu_pad, b_f=d_b_f, gla_norm_g=d_gla_norm_g, w_pool=d_w_pool,
        pool_scale=d_pool_scale, mem_norm_g=d_mem_norm_g, w_mem_kv=d_w_mem_kv, w_up_gla=d_w_up_gla,
        w_up_pool=d_w_up_pool, w_up_xattn=d_w_up_xattn, w_o=d_w_o, mix_post_g=d_mix_post_g)
    deps = [on_grads(grads)] if on_grads is not None else []
    dx1, grads["mix_pre_g"] = _dh_pre_norm_bwd("d_mix_h", [dmain, dflow], W["w_int_t"], x1, dx2, W["mix_pre_g"], deps)
    return dx1, grads


def _mesh_pos():
    x, y, c = lax.axis_index("x"), lax.axis_index("y"), lax.axis_index("c")
    return x, y, c, 4 * x + 2 * y + c


def _peer(x, y, c, r):
    px = 1 - x if r & 4 else x
    py = 1 - y if r & 2 else y
    pc = 1 - c if r & 1 else c
    return (px, py, pc), 4 * px + 2 * py + pc


_ALL_PEERS = tuple(range(1, N_DEV))
_SIBLING = 1
_SAME_CORE = (2, 4, 6)


def _dev_slot(ref, dev):
    return ref.at[dev]


def _zero_pad_rows(land):
    pad = FF_PAD - FF_BLK

    def body(land_in, o_ref):
        o_ref[...] = jnp.zeros_like(o_ref)

    return _pcall(body, name="zero_pad_rows", grid=(land.shape[0],), in_specs=[pl.BlockSpec(memory_space=pl.ANY)],
                  out_specs=pl.BlockSpec((None, pad, D_MODEL), lambda j: (j, FF_BLK // pad, 0)),
                  out_shape=_sds(land.shape, land.dtype), aliases={0: 0})(land)


class _Plan:
    def __init__(self, scatter, slots=None, shapes=None):
        self.scatter, self.slots, self.shapes = scatter, slots or {}, shapes or {}

    def src(self, srcs, a, dev):
        return self.slots.get(a, _dev_slot)(srcs[a], dev) if self.scatter else srcs[a]

    def dst(self, lands, a, dev):
        return lands[a].at[dev] if self.scatter else self.slots.get(a, _dev_slot)(lands[a], dev)

    def landing_zones(self, arrays):
        lands = []
        for a, arr in enumerate(arrays):
            if self.scatter:
                lands.append(lax.empty((N_DEV,) + tuple(self.shapes.get(a, arr.shape[1:])), arr.dtype))
            elif a in self.shapes:
                lands.append(_zero_pad_rows(lax.empty(self.shapes[a], arr.dtype)))
            else:
                lands.append(lax.empty((N_DEV,) + arr.shape, arr.dtype))
        return lands


_GATHER, _SCATTER = _Plan(False), _Plan(True)


def _peer_copies(srcs, lands, send_sems, recv_sems, plan, peers=_ALL_PEERS):
    x, y, c, me = _mesh_pos()
    cps = []
    for r in peers:
        pos, peer = _peer(x, y, c, r)
        for a in range(len(srcs)):
            k = a * (N_DEV - 1) + r - 1
            cps.append(pltpu.make_async_remote_copy(
                src_ref=plan.src(srcs, a, peer), dst_ref=plan.dst(lands, a, me),
                send_sem=send_sems.at[k], recv_sem=recv_sems.at[k], device_id=pos,
                device_id_type=pl.DeviceIdType.MESH))
    return cps


_HBM = pl.BlockSpec(memory_space=pltpu.HBM)
_SEM = pl.BlockSpec(memory_space=pltpu.SEMAPHORE)
_EFFECT = pltpu.SideEffectType.DATAFLOW_SIDE_EFFECTING


def _own_copies(srcs, lands, own_sems, plan):
    me = _mesh_pos()[3]
    return [pltpu.make_async_copy(plan.src(srcs, a, me), plan.dst(lands, a, me), own_sems.at[a])
            for a in range(len(srcs))]


def _exchange_start(name, arrays, plan):
    n = len(arrays)
    lands = plan.landing_zones(arrays)
    n_sem = n * (N_DEV - 1)

    def body(*refs):
        srcs, lands_ = refs[:n], refs[n:2 * n]
        send_sems, recv_sems, own_sems = refs[2 * n:2 * n + 3]
        token = refs[-1]
        for cp in _peer_copies(srcs, lands_, send_sems, recv_sems, plan) + _own_copies(srcs, lands_, own_sems, plan):
            cp.start()
        token[...] = jnp.zeros_like(token)

    hbm = lambda a: pltpu.HBM(a.shape, a.dtype)
    res = pl.pallas_call(
        body, name=name,
        out_shape=(pltpu.SemaphoreType.DMA((n_sem,)), pltpu.SemaphoreType.DMA((n_sem,)), pltpu.SemaphoreType.DMA((n,)),
                   *[hbm(a) for a in arrays], *[hbm(a) for a in lands], _sds((8, 128))),
        in_specs=[_HBM] * (2 * n),
        out_specs=(_SEM, _SEM, _SEM, *[_HBM] * (2 * n), pl.BlockSpec(memory_space=pltpu.VMEM)),
        input_output_aliases={i: 3 + i for i in range(2 * n)},
        compiler_params=pltpu.CompilerParams(has_side_effects=_EFFECT),
    )(*[pltpu.with_memory_space_constraint(a, pltpu.HBM) for a in list(arrays) + lands])
    return (res[:3], res[3:3 + n], res[3 + n:3 + 2 * n], plan), res[-1]


def _exchange_wait(name, state, after):
    sems, srcs, lands, plan = state
    n = len(srcs)

    def body(*refs):
        srcs_, lands_ = refs[:n], refs[n:2 * n]
        send_sems, recv_sems, own_sems = refs[2 * n:2 * n + 3]
        for cp in _peer_copies(srcs_, lands_, send_sems, recv_sems, plan):
            cp.wait_send()
            cp.wait_recv()
        for cp in _own_copies(srcs_, lands_, own_sems, plan):
            cp.wait()

    hbm = lambda a: pltpu.HBM(a.shape, a.dtype)
    res = pl.pallas_call(
        body, name=name, out_shape=tuple(hbm(a) for a in list(srcs) + list(lands)),
        in_specs=[_HBM] * (2 * n) + [_SEM] * 3 + [pl.BlockSpec(memory_space=pl.ANY)] * len(after),
        out_specs=tuple([_HBM] * (2 * n)), input_output_aliases={i: i for i in range(2 * n)},
        compiler_params=pltpu.CompilerParams(has_side_effects=_EFFECT),
    )(*srcs, *lands, *sems, *after)
    return res[n:]


def _gather_start(name, arrays, plan, after):
    n = len(arrays)
    lands = plan.landing_zones(arrays)
    n_sem = n * (N_DEV - 1)

    def body(*refs):
        srcs, lands_ = refs[:n], refs[n:2 * n]
        send_sems, recv_sems, own_sems = refs[2 * n + len(after):2 * n + len(after) + 3]
        token = refs[-1]
        for cp in (_peer_copies(srcs, lands_, send_sems, recv_sems, plan, (_SIBLING,) + _SAME_CORE)
                   + _own_copies(srcs, lands_, own_sems, plan)):
            cp.start()
        token[...] = jnp.zeros_like(token)

    hbm = lambda a: pltpu.HBM(a.shape, a.dtype)
    res = pl.pallas_call(
        body, name=name,
        out_shape=(pltpu.SemaphoreType.DMA((n_sem,)), pltpu.SemaphoreType.DMA((n_sem,)), pltpu.SemaphoreType.DMA((n,)),
                   *[hbm(a) for a in arrays], *[hbm(a) for a in lands], _sds((8, 128))),
        in_specs=[_HBM] * (2 * n) + [pl.BlockSpec(memory_space=pl.ANY)] * len(after),
        out_specs=(_SEM, _SEM, _SEM, *[_HBM] * (2 * n), pl.BlockSpec(memory_space=pltpu.VMEM)),
        input_output_aliases={i: 3 + i for i in range(2 * n)},
        compiler_params=pltpu.CompilerParams(has_side_effects=_EFFECT),
    )(*[pltpu.with_memory_space_constraint(a, pltpu.HBM) for a in list(arrays) + lands], *after)
    return (res[:3], res[3:3 + n], res[3 + n:3 + 2 * n], plan), res[-1]


def _pass_on_copies(lands, send_sems, recv_sems, plan):
    x, y, c, _ = _mesh_pos()
    sibling = _peer(x, y, c, _SIBLING)[0]
    cps = []
    for i, r in enumerate(_SAME_CORE):
        owner = _peer(x, y, c, r)[1]
        for a in range(len(lands)):
            k = a * len(_SAME_CORE) + i
            cps.append(pltpu.make_async_remote_copy(
                src_ref=plan.dst(lands, a, owner), dst_ref=plan.dst(lands, a, owner), send_sem=send_sems.at[k],
                recv_sem=recv_sems.at[k], device_id=sibling, device_id_type=pl.DeviceIdType.MESH))
    return cps


def _gather_pass_on(name, state, after):
    sems, srcs, lands, plan = state
    n = len(srcs)
    n_sem = n * len(_SAME_CORE)

    def body(*refs):
        srcs_, lands_ = refs[:n], refs[n:2 * n]
        send_sems, recv_sems = refs[2 * n], refs[2 * n + 1]
        on_send, on_recv = refs[2 * n + 3 + len(after)], refs[2 * n + 4 + len(after)]
        token = refs[-1]
        arrivals = _peer_copies(srcs_, lands_, send_sems, recv_sems, plan, _SAME_CORE)
        for arrived, on in zip(arrivals, _pass_on_copies(lands_, on_send, on_recv, plan)):
            arrived.wait_recv()
            on.start()
        token[...] = jnp.zeros_like(token)

    hbm = lambda a: pltpu.HBM(a.shape, a.dtype)
    res = pl.pallas_call(
        body, name=name,
        out_shape=(pltpu.SemaphoreType.DMA((n_sem,)), pltpu.SemaphoreType.DMA((n_sem,)),
                   *[hbm(a) for a in list(srcs) + list(lands)], _sds((8, 128))),
        in_specs=[_HBM] * (2 * n) + [_SEM] * 3 + [pl.BlockSpec(memory_space=pl.ANY)] * len(after),
        out_specs=(_SEM, _SEM, *[_HBM] * (2 * n), pl.BlockSpec(memory_space=pltpu.VMEM)),
        input_output_aliases={i: 2 + i for i in range(2 * n)},
        compiler_params=pltpu.CompilerParams(has_side_effects=_EFFECT),
    )(*srcs, *lands, *sems, *after)
    return (sems, res[:2], res[2:2 + n], res[2 + n:2 + 2 * n], plan), res[-1]


def _gather_wait(name, state, after):
    sems, on_sems, srcs, lands, plan = state
    n = len(srcs)

    def body(*refs):
        srcs_, lands_ = refs[:n], refs[n:2 * n]
        send_sems, recv_sems, own_sems, on_send, on_recv = refs[2 * n:2 * n + 5]
        for cp in _peer_copies(srcs_, lands_, send_sems, recv_sems, plan, (_SIBLING,)):
            cp.wait_recv()
        for cp in _peer_copies(srcs_, lands_, send_sems, recv_sems, plan, (_SIBLING,) + _SAME_CORE):
            cp.wait_send()
        for cp in _own_copies(srcs_, lands_, own_sems, plan):
            cp.wait()
        for cp in _pass_on_copies(lands_, on_send, on_recv, plan):
            cp.wait_send()
            cp.wait_recv()

    hbm = lambda a: pltpu.HBM(a.shape, a.dtype)
    res = pl.pallas_call(
        body, name=name, out_shape=tuple(hbm(a) for a in list(srcs) + list(lands)),
        in_specs=[_HBM] * (2 * n) + [_SEM] * 5 + [pl.BlockSpec(memory_space=pl.ANY)] * len(after),
        out_specs=tuple([_HBM] * (2 * n)), input_output_aliases={i: i for i in range(2 * n)},
        compiler_params=pltpu.CompilerParams(has_side_effects=_EFFECT),
    )(*srcs, *lands, *sems, *on_sems, *after)
    return res[n:]


def _sum_parts(name, recv):
    def body(r_ref, o_ref):
        s = r_ref[0]
        for j in range(1, N_DEV):
            s = s + r_ref[j]
        o_ref[...] = s

    return pl.pallas_call(body, name=name, out_shape=_sds(recv.shape[1:], recv.dtype))(recv)


def _adam(name, recv, w, m, v):
    shape = w.shape
    R, C = shape
    tr = R
    while N_DEV * tr * C * 4 > 6 * 1024 * 1024 and tr % 32 == 0:
        tr //= 2

    def body(recv_ref, w_ref, m_ref, v_ref, g_ref, d_ref, m2_ref, v2_ref):
        g = recv_ref[0].astype(F32)
        for j in range(1, N_DEV):
            g = g + recv_ref[j].astype(F32)
        w_, m_, v_ = w_ref[...], m_ref[...], v_ref[...]
        m2 = ADAM_B1 * m_ + (1.0 - ADAM_B1) * g
        v2 = ADAM_B2 * v_ + (1.0 - ADAM_B2) * (g * g)
        m_hat = m2 / (1.0 - ADAM_B1 ** ADAM_STEP)
        v_hat = v2 / (1.0 - ADAM_B2 ** ADAM_STEP)
        g_ref[...] = g
        d_ref[...] = -ADAM_LR * (m_hat / (jnp.sqrt(v_hat) + ADAM_EPS) + ADAM_WD * w_)
        m2_ref[...] = m2
        v2_ref[...] = v2

    blk = pl.BlockSpec((tr, C), lambda i: (i, 0))
    return _pcall(body, name=name, grid=(R // tr,),
                  in_specs=[pl.BlockSpec((N_DEV, tr, C), lambda i: (0, i, 0)), blk, blk, blk],
                  out_specs=[blk] * 4, out_shape=[_sds(shape)] * 4)(recv, w, m, v)


_NAMES = ['ffn1_pre_g', 'ffn1_w_in', 'ffn1_w_out', 'ffn1_post_g', 'mix_pre_g', 'w_in', 'w_fu', 'b_f', 'gla_norm_g',
          'w_pool', 'pool_scale', 'mem_norm_g', 'w_mem_kv', 'w_up_gla', 'w_up_pool', 'w_up_xattn', 'w_o', 'mix_post_g',
          'ffn2_pre_g', 'ffn2_w_in', 'ffn2_w_out', 'ffn2_post_g', 'final_g']
_SHARDED = ['ffn1_w_in', 'ffn1_w_out', 'w_in', 'w_fu', 'w_mem_kv', 'w_up_gla', 'w_up_pool', 'w_up_xattn', 'w_o',
            'ffn2_w_in', 'ffn2_w_out']
_COL_SHARDED = ['w_up_pool', 'w_up_xattn']


def _cols_to_full(g):
    return jnp.transpose(g, (1, 0, 2)).reshape(g.shape[1], N_DEV * g.shape[2])


def _full_to_cols(f):
    R, C = f.shape
    return jnp.transpose(f.reshape(R, N_DEV, C // N_DEV), (1, 0, 2))


def _to_internal(w_in_t):
    o = 0
    parts = []
    for s in IN_SPLITS:
        parts.append(w_in_t[o:o + s])
        o += s
    q, k, v, g_out, f_low, p_in, xq, gates = parts
    f_low = jnp.pad(f_low, ((0, FLOW_W - GATE_RANK), (0, 0)))
    return jnp.concatenate([q, k, v, g_out, gates, p_in, xq, f_low], axis=0)


def _from_internal(d, d_flow):
    q, k, v, g_out = d[0:512], d[512:1024], d[1024:2048], d[2048:3072]
    gates, p_in, xq = d[3072:6144], d[6144:6656], d[6656:7168]
    return jnp.concatenate([q, k, v, g_out, d_flow[:GATE_RANK], p_in, xq, gates], axis=0)


_FFN_IN = ('ffn1_w_in', 'ffn2_w_in')
_FFN_OUT = ('ffn1_w_out', 'ffn2_w_out')
_GATHERS = {"ffn1_in": ['ffn1_w_in'], "ffn1_out": ['ffn1_w_out'],
            "mix_in": ['w_in', 'w_fu'], "mix_rest": ['w_mem_kv', 'w_up_gla', 'w_up_pool', 'w_up_xattn', 'w_o'],
            "ffn2": ['ffn2_w_in', 'ffn2_w_out']}
_MIX = _GATHERS["mix_in"] + _GATHERS["mix_rest"]


def _ffn_in_slot(ref, d):
    return ref.at[2 * (d % N_FF_BLK) + d // N_FF_BLK, pl.ds(0, FF_BLK)]


def _ffn_out_slot(ref, d):
    rows = FF_BLK // 2
    return ref.at[d // 2, pl.ds(pl.multiple_of((d % 2) * rows, rows), rows)]


def _ffn_plan(names, scatter):
    slots, shapes = {}, {}
    for a, n in enumerate(names):
        if n in _FFN_IN:
            slots[a] = _ffn_in_slot
            shapes[a] = (FF_BLK, D_MODEL) if scatter else (N_DEV, FF_PAD, D_MODEL)
        elif n in _FFN_OUT:
            slots[a] = _ffn_out_slot
            shapes[a] = (FF_BLK // 2, D_MODEL) if scatter else (N_FF_BLK, FF_PAD, D_MODEL)
    return _Plan(scatter, slots, shapes)


def _rows_to_full(g):
    return g.reshape(N_DEV * g.shape[1], g.shape[2])


def _mix_weights(gathered):
    W = {}
    for n, g in gathered.items():
        if n == "w_in":
            W["w_int_t"] = _to_internal(g.reshape(IN_WIDTH, D_MODEL))
        elif n == "w_fu":
            W["w_fu_pad"] = jnp.pad(_cols_to_full(g), ((0, FLOW_W - GATE_RANK), (0, 0)))
        else:
            W[n] = _cols_to_full(g) if n in _COL_SHARDED else _rows_to_full(g)
    return W


def _mix_chunks(G, n):
    if n == "w_in":
        return _from_internal(*G["w_int_t"]).reshape(N_DEV, IN_SHARD, D_MODEL)
    if n == "w_fu":
        return _full_to_cols(G["w_fu_pad"][:GATE_RANK].astype(BF16))
    if n in _COL_SHARDED:
        return _full_to_cols(G[n])
    return G[n].reshape(N_DEV, G[n].shape[0] // N_DEV, G[n].shape[1])


def _step(x, mem, tgt, P, Mo, Vo):
    def native(n, a):
        return jnp.swapaxes(a, 0, 1) if n in _FFN_IN + ("w_in",) else a

    P, Mo, Vo = ({n: native(n, a) for n, a in d.items()} for d in (P, Mo, Vo))
    small = {n: P[n] for n in _NAMES if n not in _SHARDED}

    gather, tokens = {}, []
    for grp, names in _GATHERS.items():
        gather[grp], tok = _gather_start("gather_" + grp, [P[n].astype(BF16) for n in names], _ffn_plan(names, False),
                                         tokens[-1:])
        tokens.append(tok)

    def pass_on(grp, after):
        return _gather_pass_on("gather_" + grp + "_on", gather[grp], after)

    def gathered(grp, after):
        return _gather_wait("gather_" + grp + "_wait", pass_on(grp, after)[0], [])

    def ffn_wo(lands):
        return lambda act: lands[0].reshape(N_FF_BLK * FF_PAD, D_MODEL)

    h1 = _pre_norm("ffn1_pre", x, small["ffn1_pre_g"], tokens[:1])
    w1t = gathered("ffn1_in", tokens[-1:] + [h1])[0].reshape(N_DEV * FF_PAD, D_MODEL)
    (x1, hm), w1o, sv1 = _ffn_fwd("ffn1", x, h1, w1t, small["ffn1_post_g"],
                                  lambda act: ffn_wo(gathered("ffn1_out", [act]))(act), small["mix_pre_g"])

    Wm = {**small, **_mix_weights(dict(zip(_GATHERS["mix_in"], gathered("mix_in", [x1]))))}
    passed = {}

    def pass_on_later(proj):
        tokens = []
        for grp in ("mix_rest", "ffn2"):
            passed[grp], tok = pass_on(grp, [proj])
            tokens.append(tok)
        return tokens

    def mix_rest(after):
        names = _GATHERS["mix_rest"]
        return _mix_weights(dict(zip(names, _gather_wait("gather_mix_rest_wait", passed["mix_rest"], after))))

    x2, h2, Wm, svm = _mix_fwd(x1, hm, mem, Wm, small["ffn2_pre_g"], pass_on_later, mix_rest)
    w2t, w2o = _gather_wait("gather_ffn2_wait", passed["ffn2"], [x2])
    w2t = w2t.reshape(N_DEV * FF_PAD, D_MODEL)
    (loss, dx3, d_final_g), w2o, sv2 = _ffn_fwd("ffn2", x2, h2, w2t, small["ffn2_post_g"], ffn_wo([w2o]),
                                                small["final_g"], tgt)

    G = dict(final_g=d_final_g)
    scat = {}

    def start(grp, names, arrays):
        scat[grp] = names, _exchange_start("scatter_" + grp, arrays, _ffn_plan(names, True))
        return scat[grp][1][1]

    def ffn_starts(tag):
        return (lambda dwo: start(tag + "_out", [tag + "_w_out"], [dwo.reshape(N_FF_BLK, FF_PAD, D_MODEL)]),
                lambda dwt: start(tag + "_in", [tag + "_w_in"], [dwt.reshape(N_DEV, FF_PAD, D_MODEL)]))

    dx2, G["ffn2_pre_g"], G["ffn2_post_g"] = _ffn_bwd(
        "ffn2", dx3, x2, small["ffn2_pre_g"], w2t, w2o, small["ffn2_post_g"], sv2, *ffn_starts("ffn2"))
    dx1, Gm = _mix_bwd(dx2, x1, mem, Wm, svm,
                       lambda Gm: start("mix", _MIX, [_mix_chunks(Gm, n) for n in _MIX]))
    G.update(Gm)
    dx, G["ffn1_pre_g"], G["ffn1_post_g"] = _ffn_bwd(
        "ffn1", dx1, x, small["ffn1_pre_g"], w1t, w1o, small["ffn1_post_g"], sv1, *ffn_starts("ffn1"))

    scat["small"] = list(small) + ["loss"], _exchange_start("gather_small_grads", [G[n] for n in small] + [loss],
                                                            _GATHER)
    recv, outs = {}, {}
    done = [dx]
    for grp in ["ffn2_out", "ffn2_in", "mix", "ffn1_out", "ffn1_in", "small"]:
        names, (state, _) = scat[grp]
        recv.update(zip(names, _exchange_wait(("gather_" if grp == "small" else "scatter_") + grp + "_wait", state,
                                              done)))
        for n in names:
            if n == "loss":
                loss = _sum_parts("loss_sum", recv[n])[0, 0]
                continue
            shp = P[n].shape
            shp2 = shp if len(shp) == 2 else (shp[0] * shp[1], shp[2])
            res = _adam("adam_" + n, recv[n].reshape((N_DEV,) + shp2), P[n].reshape(shp2), Mo[n].reshape(shp2),
                        Vo[n].reshape(shp2))
            outs[n] = [native(n, t.reshape(shp)) for t in res]
            done.append(res[-1])
    return loss, dx, outs


def kernel(x, mem, ffn1_pre_g, ffn1_w_in, ffn1_w_out, ffn1_post_g, mix_pre_g, w_in, w_fu, b_f, gla_norm_g, w_pool, pool_scale, mem_norm_g, w_mem_kv, w_up_gla, w_up_pool, w_up_xattn, w_o, mix_post_g, ffn2_pre_g, ffn2_w_in, ffn2_w_out, ffn2_post_g, final_g, loss_target, m_ffn1_pre_g, m_ffn1_w_in, m_ffn1_w_out, m_ffn1_post_g, m_mix_pre_g, m_w_in, m_w_fu, m_b_f, m_gla_norm_g, m_w_pool, m_pool_scale, m_mem_norm_g, m_w_mem_kv, m_w_up_gla, m_w_up_pool, m_w_up_xattn, m_w_o, m_mix_post_g, m_ffn2_pre_g, m_ffn2_w_in, m_ffn2_w_out, m_ffn2_post_g, m_final_g, v_ffn1_pre_g, v_ffn1_w_in, v_ffn1_w_out, v_ffn1_post_g, v_mix_pre_g, v_w_in, v_w_fu, v_b_f, v_gla_norm_g, v_w_pool, v_pool_scale, v_mem_norm_g, v_w_mem_kv, v_w_up_gla, v_w_up_pool, v_w_up_xattn, v_w_o, v_mix_post_g, v_ffn2_pre_g, v_ffn2_w_in, v_ffn2_w_out, v_ffn2_post_g, v_final_g):
    params = [ffn1_pre_g, ffn1_w_in, ffn1_w_out, ffn1_post_g, mix_pre_g, w_in, w_fu, b_f, gla_norm_g, w_pool, pool_scale, mem_norm_g, w_mem_kv, w_up_gla, w_up_pool, w_up_xattn, w_o, mix_post_g, ffn2_pre_g, ffn2_w_in, ffn2_w_out, ffn2_post_g, final_g]
    moms = [m_ffn1_pre_g, m_ffn1_w_in, m_ffn1_w_out, m_ffn1_post_g, m_mix_pre_g, m_w_in, m_w_fu, m_b_f, m_gla_norm_g, m_w_pool, m_pool_scale, m_mem_norm_g, m_w_mem_kv, m_w_up_gla, m_w_up_pool, m_w_up_xattn, m_w_o, m_mix_post_g, m_ffn2_pre_g, m_ffn2_w_in, m_ffn2_w_out, m_ffn2_post_g, m_final_g]
    vars_ = [v_ffn1_pre_g, v_ffn1_w_in, v_ffn1_w_out, v_ffn1_post_g, v_mix_pre_g, v_w_in, v_w_fu, v_b_f, v_gla_norm_g, v_w_pool, v_pool_scale, v_mem_norm_g, v_w_mem_kv, v_w_up_gla, v_w_up_pool, v_w_up_xattn, v_w_o, v_mix_post_g, v_ffn2_pre_g, v_ffn2_w_in, v_ffn2_w_out, v_ffn2_post_g, v_final_g]
    P = {n: a[0] if a.ndim > 2 else a for n, a in zip(_NAMES, params)}
    Mo = {n: a[0] if a.ndim > 2 else a for n, a in zip(_NAMES, moms)}
    Vo = {n: a[0] if a.ndim > 2 else a for n, a in zip(_NAMES, vars_)}
    loss, dx, outs = _step(x[0], mem[0], loss_target[0], P, Mo, Vo)
    out = [loss, dx[None]]
    for kind in range(4):
        for n, p in zip(_NAMES, params):
            out.append(outs[n][kind].reshape(p.shape))
    return tuple(out)
```

```python
import functools

import jax
import jax.numpy as jnp
from jax import lax
from jax.experimental import pallas as pl
from jax.experimental.pallas import tpu as pltpu

F32 = jnp.float32
BF16 = jnp.bfloat16

N_DEV = 8
D_MODEL = 1024
D_FF = 2816
FF_BLK = 2 * D_FF // N_DEV
N_FF_BLK = D_FF // FF_BLK
FF_PAD = 768
CHUNK = 64
GLA_HEADS = 4
GLA_DK = 512
GLA_DV = 1024
GLA_HDK = 128
GLA_HDV = 256
GATE_RANK = 16
GATE_TEMP = 16.0
POOL_WINDOWS = (2, 4, 8, 16)
POOL_W = 512
POOL_GD = 128
POOL_HALO = 16
XA_HEADS = 4
XA_HD = 128
XA_W = 512
EPS = 1e-6
IN_SPLITS = (GLA_DK, GLA_DK, GLA_DV, GLA_DV, GATE_RANK, POOL_W, XA_W, 3 * D_MODEL)
IN_WIDTH = sum(IN_SPLITS)
IN_SHARD = IN_WIDTH // N_DEV
INT_W = 3072 + 3072 + 1024 + 128
INT_NBLK = 3
INT_BLK = INT_W // INT_NBLK
FLOW_W = 128
INT_MAIN = INT_W - FLOW_W
_POOL_COL = (3072 + 3072) // POOL_W
_XQ_COL = (3072 + 3072 + POOL_W) // XA_W

ADAM_LR = 0.001
ADAM_B1 = 0.9
ADAM_B2 = 0.999
ADAM_EPS = 1e-08
ADAM_WD = 0.01
ADAM_STEP = 10

VMEM_LIMIT = 56 * 1024 * 1024

_NN = (((1,), (0,)), ((), ()))
_NT = (((1,), (1,)), ((), ()))
_TN = (((0,), (0,)), ((), ()))


def _pcall(body, *, name, grid, in_specs, out_specs, out_shape, scratch=(), aliases=None):
    return pl.pallas_call(
        body, name=name, grid=grid, in_specs=in_specs, out_specs=out_specs, out_shape=out_shape,
        scratch_shapes=list(scratch), input_output_aliases=aliases or {},
        compiler_params=pltpu.CompilerParams(dimension_semantics=("arbitrary",) * len(grid),
                                             vmem_limit_bytes=VMEM_LIMIT))


def _dot(a, b, dims=_NN):
    return lax.dot_general(a.astype(BF16), b.astype(BF16), dims, preferred_element_type=F32)


def _mm(name, a, b, *, grid, a_blk, a_map, b_blk, b_map, o_shape, o_blk, o_map, dims, out_dtype=F32, deps=()):
    nk = grid[2]

    def body(a_ref, b_ref, *rest):
        o_ref, scr = rest[len(deps)], rest[len(deps) + 1:]
        p = _dot(a_ref[...], b_ref[...], dims)
        if nk == 1:
            o_ref[...] = p.astype(o_ref.dtype)
        else:
            acc = scr[0]
            k = pl.program_id(2)

            @pl.when(k == 0)
            def _():
                acc[...] = p

            @pl.when(k > 0)
            def _():
                acc[...] += p

            @pl.when(k == nk - 1)
            def _():
                o_ref[...] = acc[...].astype(o_ref.dtype)

    acc_shape = tuple(d for d in o_blk if d is not None)
    return _pcall(body, name=name, grid=grid,
                  in_specs=[pl.BlockSpec(a_blk, a_map), pl.BlockSpec(b_blk, b_map)]
                  + [pl.BlockSpec(memory_space=pl.ANY)] * len(deps),
                  out_specs=pl.BlockSpec(o_blk, o_map),
                  out_shape=jax.ShapeDtypeStruct(o_shape, out_dtype),
                  scratch=[pltpu.VMEM(acc_shape, F32)] if nk > 1 else [])(a, b, *deps)


def _mm_nn(name, a, b, out_dtype=F32, tm=1024, tn=None):
    M, K = a.shape
    N = b.shape[1]
    tm, tn = min(tm, M), (tn or N)
    return _mm(name, a, b, grid=(N // tn, M // tm, 1), a_blk=(tm, K), a_map=lambda j, i, k: (i, 0),
               b_blk=(K, tn), b_map=lambda j, i, k: (0, j), o_shape=(M, N), o_blk=(tm, tn),
               o_map=lambda j, i, k: (i, j), dims=_NN, out_dtype=out_dtype)


def _mm_nt(name, a, b, out_dtype=F32, tm=1024):
    M, K = a.shape
    N = b.shape[0]
    tm = min(tm, M)
    return _mm(name, a, b, grid=(1, M // tm, 1), a_blk=(tm, K), a_map=lambda j, i, k: (i, 0),
               b_blk=(N, K), b_map=lambda j, i, k: (0, 0), o_shape=(M, N), o_blk=(tm, N),
               o_map=lambda j, i, k: (i, 0), dims=_NT, out_dtype=out_dtype)


def _mm_tn(name, a, b, out_dtype=BF16, ts=512, tn=None, tn_a=None, deps=()):
    S, M = a.shape
    N = b.shape[1]
    ts, tn, tn_a = min(ts, S), (tn or N), (tn_a or M)
    return _mm(name, a, b, grid=((N // tn) * (M // tn_a), 1, S // ts), a_blk=(ts, tn_a),
               a_map=lambda j, i, k: (k, j if tn_a < M else 0), b_blk=(ts, tn),
               b_map=lambda j, i, k: (k, j if tn < N else 0), o_shape=(M, N), o_blk=(tn_a, tn),
               o_map=lambda j, i, k: (j, 0) if tn_a < M else (0, j), dims=_TN, out_dtype=out_dtype, deps=deps)


class _Win:
    def __init__(self, arr, w, c):
        self.arr, self.w, self.c = arr, w, c


def _row_spec(x, tm):
    if isinstance(x, _Win):
        return x.arr, pl.BlockSpec((tm, x.w), functools.partial(lambda i, c: (i, c), c=x.c))
    if x.ndim == 3:
        return x, pl.BlockSpec((x.shape[0], tm, x.shape[2]), lambda i: (0, i, 0))
    return x, pl.BlockSpec((tm, x.shape[1]), lambda i: (i, 0))


def _rowwise(name, fn, rows, consts, outs, accs=(), tm=512):
    first = rows[0].arr if isinstance(rows[0], _Win) else rows[0]
    S = first.shape[1] if first.ndim == 3 else first.shape[0]
    tm = min(tm, S)
    n_in, n_out = len(rows) + len(consts), len(outs)
    arrays, in_specs = [], []
    for r in rows:
        arr, spec = _row_spec(r, tm)
        arrays.append(arr)
        in_specs.append(spec)
    for c in consts:
        arrays.append(c)
        in_specs.append(pl.BlockSpec(c.shape, functools.partial(lambda i, n: (0,) * n, n=c.ndim)))
    aliases = {}
    for k, o in enumerate(outs):
        if isinstance(o, _Win):
            aliases[len(arrays)] = k
            arrays.append(o.arr)
            in_specs.append(pl.BlockSpec(memory_space=pl.ANY))
    n_thru = len(aliases)
    out_specs = [_row_spec(o, tm)[1] for o in outs]
    out_specs += [pl.BlockSpec(a.shape, functools.partial(lambda i, n: (0,) * n, n=len(a.shape))) for a in accs]
    out_shape = [_sds(o.arr.shape, o.arr.dtype) if isinstance(o, _Win) else _sds(o.shape, o.dtype) for o in outs]

    def body(*refs):
        res = fn(*[r[...] for r in refs[:n_in]])
        if not isinstance(res, (tuple, list)):
            res = (res,)
        orefs = refs[n_in + n_thru:]
        for r, v in zip(orefs[:n_out], res[:n_out]):
            r[...] = v.astype(r.dtype)
        i = pl.program_id(0)
        for r, v in zip(orefs[n_out:], res[n_out:]):
            @pl.when(i == 0)
            def _(r=r, v=v):
                r[...] = v.astype(r.dtype)

            @pl.when(i > 0)
            def _(r=r, v=v):
                r[...] += v.astype(r.dtype)

    return _pcall(body, name=name, grid=(S // tm,), in_specs=in_specs, out_specs=out_specs,
                  out_shape=out_shape + [_sds(a.shape, a.dtype) for a in accs], aliases=aliases)(*arrays)


def _sds(shape, dtype=F32):
    return jax.ShapeDtypeStruct(shape, dtype)


def _rms(x, g):
    return x * lax.rsqrt(jnp.mean(x * x, axis=-1, keepdims=True) + EPS) * g


def _rms_bwd(x, g, dy):
    _, vjp = jax.vjp(_rms, x, g)
    return vjp(dy)


def _sigmoid(x):
    return 0.5 * jnp.tanh(0.5 * x) + 0.5


def _silu(x):
    return x * _sigmoid(x)


def _log_sigmoid(f):
    return jnp.minimum(f, 0.0) - jnp.log(1.0 + jnp.exp(-jnp.abs(f)))


def _head_rms_gate(o, g_out, gn):
    parts = [_rms(o[:, h * GLA_HDV:(h + 1) * GLA_HDV], gn[:, h * GLA_HDV:(h + 1) * GLA_HDV]) for h in range(GLA_HEADS)]
    return jnp.concatenate(parts, axis=-1) * _silu(g_out.astype(F32))


def _merge(gates, ya, yb, yc):
    gates, ya, yb, yc = (t.astype(F32) for t in (gates, ya, yb, yc))
    return (_sigmoid(gates[:, :D_MODEL]) * ya + _sigmoid(gates[:, D_MODEL:2 * D_MODEL]) * yb
            + _sigmoid(gates[:, 2 * D_MODEL:]) * yc)


def _tri_dot(t, x):
    hi = x.astype(BF16)
    r1 = x - hi.astype(F32)
    mid = r1.astype(BF16)
    lo = (r1 - mid.astype(F32)).astype(BF16)
    d = functools.partial(lax.dot_general, dimension_numbers=_NN, preferred_element_type=F32)
    return d(t, hi) + d(t, mid) + d(t, lo)


def _pre_norm(name, x, g, after):
    S = x.shape[0]
    tm = min(512, S)

    def body(x_ref, g_ref, *rest):
        rest[-1][...] = _rms(x_ref[...], g_ref[...]).astype(BF16)

    row = pl.BlockSpec((tm, D_MODEL), lambda i: (i, 0))
    return _pcall(body, name=name, grid=(S // tm,),
                  in_specs=[row, pl.BlockSpec((1, D_MODEL), lambda i: (0, 0))] + [pl.BlockSpec(memory_space=pl.ANY)] * len(after),
                  out_specs=row, out_shape=_sds(x.shape, BF16))(x, g, *after)


def _ffn_fwd(tag, x, h, wt, gpost, get_wo, g_next, tgt=None):
    S = x.shape[0]
    tm = min(512, S)

    def up_body(h_ref, w_ref, u_ref, act_ref):
        hh = h_ref[...]
        for j in range(N_FF_BLK):
            ab = _dot(hh, w_ref[2 * FF_PAD * j:2 * FF_PAD * (j + 1), :], _NT)
            u_ref[:, 2 * FF_PAD * j:2 * FF_PAD * (j + 1)] = ab.astype(BF16)
            act_ref[:, FF_PAD * j:FF_PAD * (j + 1)] = (_silu(ab[:, :FF_PAD]) * ab[:, FF_PAD:]).astype(BF16)

    u, act = _pcall(
        up_body, name=tag + "_up", grid=(S // tm,),
        in_specs=[pl.BlockSpec((tm, D_MODEL), lambda i: (i, 0)),
                  pl.BlockSpec(wt.shape, lambda i: (0, 0), pipeline_mode=pl.Buffered(1))],
        out_specs=[pl.BlockSpec((tm, N_DEV * FF_PAD), lambda i: (i, 0)),
                   pl.BlockSpec((tm, N_FF_BLK * FF_PAD), lambda i: (i, 0))],
        out_shape=[_sds((S, N_DEV * FF_PAD), BF16), _sds((S, N_FF_BLK * FF_PAD), BF16)],
    )(h, wt)

    def down_body(act_ref, w_ref, x_ref, g_ref, gn_ref, f_ref, xo_ref, hn_ref):
        f = _dot(act_ref[...], w_ref[...])
        f_ref[...] = f.astype(BF16)
        xo = x_ref[...] + 0.5 * _rms(f, g_ref[...])
        xo_ref[...] = xo
        hn_ref[...] = _rms(xo, gn_ref[...]).astype(BF16)

    def down_head_body(act_ref, w_ref, x_ref, g_ref, gn_ref, t_ref, f_ref, dxo_ref, loss_ref, dgn_ref):
        i = pl.program_id(0)
        f = _dot(act_ref[...], w_ref[...])
        f_ref[...] = f.astype(BF16)
        xo = x_ref[...] + 0.5 * _rms(f, g_ref[...])
        out, vjp = jax.vjp(_rms, xo, gn_ref[...])
        e = out - t_ref[...]
        loss = 0.5 * jnp.sum(jnp.mean(e * e, axis=-1, keepdims=True), axis=0, keepdims=True)
        dxo_ref[...], dgn = vjp(e * (1.0 / D_MODEL))

        @pl.when(i == 0)
        def _():
            loss_ref[...] = jnp.broadcast_to(loss, loss_ref.shape)
            dgn_ref[...] = dgn

        @pl.when(i > 0)
        def _():
            loss_ref[...] += jnp.broadcast_to(loss, loss_ref.shape)
            dgn_ref[...] += dgn

    wo = get_wo(act)
    row = pl.BlockSpec((tm, D_MODEL), lambda i: (i, 0))
    vec = pl.BlockSpec((1, D_MODEL), lambda i: (0, 0))
    if tgt is not None:
        f, dxo, loss, dg_next = _pcall(
            down_head_body, name=tag + "_down", grid=(S // tm,),
            in_specs=[pl.BlockSpec((tm, N_FF_BLK * FF_PAD), lambda i: (i, 0)),
                      pl.BlockSpec(wo.shape, lambda i: (0, 0), pipeline_mode=pl.Buffered(1)), row, vec, vec, row],
            out_specs=[row, row, pl.BlockSpec((1, 128), lambda i: (0, 0)), vec],
            out_shape=[_sds((S, D_MODEL), BF16), _sds((S, D_MODEL)), _sds((1, 128)), _sds((1, D_MODEL))],
        )(act, wo, x, gpost, g_next, tgt)
        return (loss, dxo, dg_next), wo, (h, u, act, f)
    f, xo, h_next = _pcall(
        down_body, name=tag + "_down", grid=(S // tm,),
        in_specs=[pl.BlockSpec((tm, N_FF_BLK * FF_PAD), lambda i: (i, 0)),
                  pl.BlockSpec(wo.shape, lambda i: (0, 0), pipeline_mode=pl.Buffered(1)), row, vec, vec],
        out_specs=[row, row, row], out_shape=[_sds((S, D_MODEL), BF16), _sds((S, D_MODEL)), _sds((S, D_MODEL), BF16)],
    )(act, wo, x, gpost, g_next)
    return (xo, h_next), wo, (h, u, act, f)


def _ffn_bwd(tag, dxo, x, gpre, wt, wo, gpost, saved, on_dwo=None, on_dwt=None):
    h, u, act, f = saved
    S = x.shape[0]
    tm = min(256, S)

    def dact_body(f_ref, dxo_ref, g_ref, w_ref, u_ref, df_ref, du_ref, dg_ref):
        i = pl.program_id(0)
        df, dg = _rms_bwd(f_ref[...].astype(F32), g_ref[...], 0.5 * dxo_ref[...])
        df = df.astype(BF16)
        df_ref[...] = df

        @pl.when(i == 0)
        def _():
            dg_ref[...] = dg

        @pl.when(i > 0)
        def _():
            dg_ref[...] += dg

        for j in range(N_FF_BLK):
            dact = _dot(df, w_ref[FF_PAD * j:FF_PAD * (j + 1), :], _NT)
            ab = u_ref[:, 2 * FF_PAD * j:2 * FF_PAD * (j + 1)].astype(F32)
            a, b = ab[:, :FF_PAD], ab[:, FF_PAD:]
            sg = _sigmoid(a)
            du_ref[:, 2 * FF_PAD * j:2 * FF_PAD * j + FF_PAD] = (dact * b * (sg * (1.0 + a * (1.0 - sg)))).astype(BF16)
            du_ref[:, 2 * FF_PAD * j + FF_PAD:2 * FF_PAD * (j + 1)] = (dact * (a * sg)).astype(BF16)

    row = pl.BlockSpec((tm, D_MODEL), lambda i: (i, 0))
    vec = pl.BlockSpec((1, D_MODEL), lambda i: (0, 0))
    u_spec = pl.BlockSpec((tm, N_DEV * FF_PAD), lambda i: (i, 0))
    df, du, dgpost = _pcall(
        dact_body, name=tag + "_dact", grid=(S // tm,),
        in_specs=[row, row, vec, pl.BlockSpec(wo.shape, lambda i: (0, 0), pipeline_mode=pl.Buffered(1)), u_spec],
        out_specs=[row, u_spec, vec],
        out_shape=[_sds((S, D_MODEL), BF16), _sds((S, N_DEV * FF_PAD), BF16), _sds((1, D_MODEL))])(f, dxo, gpost, wo, u)
    dwo = _mm_tn(tag + "_dwo", act, df, ts=S, tn_a=FF_PAD)
    dwt = _mm_tn(tag + "_dwt", du, h, ts=S, tn_a=FF_PAD, deps=[on_dwo(dwo)] if on_dwo is not None else [])
    dx, dgpre = _dh_pre_norm_bwd(tag + "_dh", [du], wt, x, dxo, gpre, [on_dwt(dwt)] if on_dwt is not None else [])
    return dx, dgpre, dgpost


def _dh_pre_norm_bwd(name, dzs, wt, x, dres, g, deps):
    S = x.shape[0]
    ks = [dz.shape[1] for dz in dzs]

    def body(*refs):
        dz_refs = refs[:len(dzs)]
        w_ref, x_ref, dres_ref, g_ref = refs[len(dzs):len(dzs) + 4]
        dx_ref, dg_ref = refs[len(dzs) + 4 + len(deps):]
        i = pl.program_id(0)
        dh, o = None, 0
        for dz_ref, k in zip(dz_refs, ks):
            p = _dot(dz_ref[...], w_ref[o:o + k, :])
            dh, o = (p if dh is None else dh + p), o + k
        dx, dg = _rms_bwd(x_ref[...], g_ref[...], dh)
        dx_ref[...] = dx + dres_ref[...]

        @pl.when(i == 0)
        def _():
            dg_ref[...] = dg

        @pl.when(i > 0)
        def _():
            dg_ref[...] += dg

    th = min(512, S)
    row = pl.BlockSpec((th, D_MODEL), lambda i: (i, 0))
    vec = pl.BlockSpec((1, D_MODEL), lambda i: (0, 0))
    return _pcall(
        body, name=name, grid=(S // th,),
        in_specs=[pl.BlockSpec((th, k), lambda i: (i, 0)) for k in ks]
        + [pl.BlockSpec(wt.shape, lambda i: (0, 0), pipeline_mode=pl.Buffered(1)), row, row, vec]
        + [pl.BlockSpec(memory_space=pl.ANY)] * len(deps),
        out_specs=[row, vec], out_shape=[_sds((S, D_MODEL)), _sds((1, D_MODEL))])(*dzs, wt, x, dres, g, *deps)


GLA_G = 8


def _gla_tile_common(k, flow, wfu, bf):
    f = _dot(flow, wfu) + bf
    la = _log_sigmoid(f) * (1.0 / GATE_TEMP)
    tri = _tri_matrix(True)
    ws, ds = [], []
    for g in range(k.shape[0] // CHUNK):
        b = _tri_dot(tri, la[g * CHUNK:(g + 1) * CHUNK])
        b_end = b[CHUNK - 1:CHUNK, :]
        ws.append(jnp.exp(b_end - b))
        ds.append(jnp.exp(b_end))
    w = jnp.concatenate(ws, axis=0)
    return f, w, k * w, ds


def _tri_matrix(lower):
    r = lax.broadcasted_iota(jnp.int32, (CHUNK, CHUNK), 0)
    c = lax.broadcasted_iota(jnp.int32, (CHUNK, CHUNK), 1)
    return jnp.where((r >= c) if lower else (r <= c), 1.0, 0.0).astype(BF16)


def _heads():
    return [(slice(h * GLA_HDK, (h + 1) * GLA_HDK), slice(h * GLA_HDV, (h + 1) * GLA_HDV)) for h in range(GLA_HEADS)]


def _gla_fwd(proj, wfu, bf, gn, deps=()):
    S = proj.shape[0]
    G = min(GLA_G, S // CHUNK)
    T = G * CHUNK
    nc = S // CHUNK

    def body(q_ref, k_ref, v_ref, go_ref, flow_ref, wfu_ref, bf_ref, gn_ref, *rest):
        ya_ref, st_ref, state = rest[len(deps):]
        @pl.when(pl.program_id(0) == 0)
        def _():
            state[...] = jnp.zeros_like(state)

        _, _, kt, ds = _gla_tile_common(k_ref[...], flow_ref[...], wfu_ref[...], bf_ref[...])
        q = q_ref[...].astype(F32) * (GLA_HDK ** -0.5)
        v = v_ref[...]
        rows = [slice(g * CHUNK, (g + 1) * CHUNK) for g in range(G)]
        kv = [[_dot(v[r, vs], kt[r, ks], _TN) for ks, vs in _heads()] for r in rows]
        st = [state[vs, :] for _, vs in _heads()]
        o = []
        for g, r in enumerate(rows):
            outs = []
            for h, (ks, vs) in enumerate(_heads()):
                st[h] = st[h] * ds[g][:, ks] + kv[g][h]
                st_ref[g, vs, :] = st[h]
                outs.append(_dot(q[r, ks], st[h], _NT))
            o.append(jnp.concatenate(outs, axis=-1))
        for h, (_, vs) in enumerate(_heads()):
            state[vs, :] = st[h]
        ya_ref[...] = _head_rms_gate(jnp.concatenate(o, axis=0), go_ref[...], gn_ref[...]).astype(BF16)

    return _pcall(
        body, name="gla_fwd", grid=(S // T,),
        in_specs=[pl.BlockSpec((T, GLA_DK), lambda c: (c, 0)), pl.BlockSpec((T, GLA_DK), lambda c: (c, 1)),
                  pl.BlockSpec((T, GLA_DV), lambda c: (c, 1)), pl.BlockSpec((T, GLA_DV), lambda c: (c, 2)),
                  pl.BlockSpec((T, FLOW_W), lambda c: (c, (INT_W - FLOW_W) // FLOW_W)),
                  pl.BlockSpec(wfu.shape, lambda c: (0, 0)), pl.BlockSpec(bf.shape, lambda c: (0, 0)),
                  pl.BlockSpec(gn.shape, lambda c: (0, 0))]
        + [pl.BlockSpec(memory_space=pl.ANY)] * len(deps),
        out_specs=[pl.BlockSpec((T, GLA_DV), lambda c: (c, 0)),
                   pl.BlockSpec((G, GLA_DV, GLA_HDK), lambda c: (c, 0, 0))],
        out_shape=[_sds((S, GLA_DV), BF16), _sds((nc, GLA_DV, GLA_HDK))],
        scratch=[pltpu.VMEM((GLA_DV, GLA_HDK), F32)])(proj, proj, proj, proj, proj, wfu, bf, gn, *deps)


def _gla_bwd(proj, wfu, bf, gn, states, d_ya_in, dmain):
    S = proj.shape[0]
    G = min(GLA_G, S // CHUNK)
    T = G * CHUNK
    nt = S // T

    def body(q_ref, k_ref, v_ref, go_ref, flow_ref, wfu_ref, bf_ref, gn_ref, st_ref, stp_ref, dya_ref, dmain_in,
             dqkvg_ref, dflow_ref, dwfu_ref, dbf_ref, dgn_ref, dstate):
        dq_ref = dqkvg_ref.at[:, 0:GLA_DK]
        dk_ref = dqkvg_ref.at[:, GLA_DK:2 * GLA_DK]
        dv_ref = dqkvg_ref.at[:, 2 * GLA_DK:2 * GLA_DK + GLA_DV]
        dgo_ref = dqkvg_ref.at[:, 2 * GLA_DK + GLA_DV:]
        step = pl.program_id(0)

        @pl.when(step == 0)
        def _():
            dstate[...] = jnp.zeros_like(dstate)
            dwfu_ref[...] = jnp.zeros_like(dwfu_ref)
            dbf_ref[...] = jnp.zeros_like(dbf_ref)
            dgn_ref[...] = jnp.zeros_like(dgn_ref)

        flow, wfu_v = flow_ref[...], wfu_ref[...]
        f, w, kt, ds = _gla_tile_common(k_ref[...], flow, wfu_v, bf_ref[...])
        q = q_ref[...].astype(F32) * (GLA_HDK ** -0.5)
        v = v_ref[...]
        rows = [slice(g * CHUNK, (g + 1) * CHUNK) for g in range(G)]
        o = jnp.concatenate([jnp.concatenate([_dot(q[r, ks], st_ref[g, vs, :], _NT) for ks, vs in _heads()], axis=-1)
                             for g, r in enumerate(rows)], axis=0)
        _, vjp = jax.vjp(_head_rms_gate, o, go_ref[...], gn_ref[...])
        dout, dgo, dgn = vjp(dya_ref[...].astype(F32))
        dgo_ref[...] = dgo.astype(dgo_ref.dtype)
        dgn_ref[...] += dgn
        dq = [jnp.concatenate([_dot(dout[r, vs], st_ref[g, vs, :]) for _, vs in _heads()], axis=-1)
              for g, r in enumerate(rows)]
        qdo = [[_dot(dout[r, vs], q[r, ks], _TN) for ks, vs in _heads()] for r in rows]
        dq_ref[...] = (jnp.concatenate(dq, axis=0) * (GLA_HDK ** -0.5)).astype(dq_ref.dtype)
        has_prev = (step < nt - 1).astype(F32)
        carry = [dstate[vs, :] for _, vs in _heads()]
        dkt, dv, dd = [None] * G, [None] * G, [None] * G
        for g in reversed(range(G)):
            r = rows[g]
            dkts, dvs, dds = [], [], []
            for h, (ks, vs) in enumerate(_heads()):
                dst = carry[h] + qdo[g][h]
                dkts.append(_dot(v[r, vs], dst))
                dvs.append(_dot(kt[r, ks], dst, _NT))
                st_prev = st_ref[g - 1, vs, :] if g > 0 else stp_ref[vs, :] * has_prev
                dds.append(jnp.sum(dst * st_prev, axis=0, keepdims=True))
                carry[h] = dst * ds[g][:, ks]
            dkt[g], dv[g], dd[g] = (jnp.concatenate(t, axis=-1) for t in (dkts, dvs, dds))
        for h, (_, vs) in enumerate(_heads()):
            dstate[vs, :] = carry[h]
        dkt = jnp.concatenate(dkt, axis=0)
        dv_ref[...] = jnp.concatenate(dv, axis=0).astype(dv_ref.dtype)
        dk_ref[...] = (dkt * w).astype(dk_ref.dtype)
        de = dkt * kt
        tri = _tri_matrix(False)
        dla = []
        for g, r in enumerate(rows):
            db_end = jnp.sum(de[r], axis=0, keepdims=True) + dd[g] * ds[g]
            dla.append(db_end - _tri_dot(tri, de[r]))
        df = jnp.concatenate(dla, axis=0) * (1.0 - _sigmoid(f)) * (1.0 / GATE_TEMP)
        dflow_ref[...] = _dot(df, wfu_v, _NT).astype(dflow_ref.dtype)
        dwfu_ref[...] += _dot(flow, df, _TN)
        dbf_ref[...] += jnp.sum(df, axis=0, keepdims=True)

    rt = lambda s: nt - 1 - s
    return _pcall(
        body, name="gla_bwd", grid=(nt,),
        in_specs=[pl.BlockSpec((T, GLA_DK), lambda s: (rt(s), 0)), pl.BlockSpec((T, GLA_DK), lambda s: (rt(s), 1)),
                  pl.BlockSpec((T, GLA_DV), lambda s: (rt(s), 1)), pl.BlockSpec((T, GLA_DV), lambda s: (rt(s), 2)),
                  pl.BlockSpec((T, FLOW_W), lambda s: (rt(s), (INT_W - FLOW_W) // FLOW_W)),
                  pl.BlockSpec(wfu.shape, lambda s: (0, 0)), pl.BlockSpec(bf.shape, lambda s: (0, 0)),
                  pl.BlockSpec(gn.shape, lambda s: (0, 0)),
                  pl.BlockSpec((G, GLA_DV, GLA_HDK), lambda s: (rt(s), 0, 0)),
                  pl.BlockSpec((None, GLA_DV, GLA_HDK), lambda s: (jnp.maximum(rt(s) * G - 1, 0), 0, 0)),
                  pl.BlockSpec((T, GLA_DV), lambda s: (rt(s), 0)), pl.BlockSpec(memory_space=pl.ANY)],
        out_specs=[pl.BlockSpec((T, 2 * GLA_DK + 2 * GLA_DV), lambda s: (rt(s), 0)),
                   pl.BlockSpec((T, FLOW_W), lambda s: (rt(s), 0)),
                   pl.BlockSpec(wfu.shape, lambda s: (0, 0)), pl.BlockSpec(bf.shape, lambda s: (0, 0)),
                   pl.BlockSpec(gn.shape, lambda s: (0, 0))],
        out_shape=[_sds(dmain.shape, dmain.dtype), _sds((S, FLOW_W), BF16), _sds(wfu.shape), _sds(bf.shape),
                   _sds(gn.shape)],
        scratch=[pltpu.VMEM((GLA_DV, GLA_HDK), F32)], aliases={11: 0},
    )(proj, proj, proj, proj, proj, wfu, bf, gn, states, states, d_ya_in, dmain)


def _pool_counts(tm, i):
    t = (lax.broadcasted_iota(jnp.int32, (tm, POOL_GD), 0) + i * tm + 1).astype(F32)
    return [jnp.minimum(t, float(w)) for w in POOL_WINDOWS]


def _pool_fwd(proj, w_pool, pool_scale, tm=512):
    S = proj.shape[0]
    tm = min(tm, S // 2)
    col = _POOL_COL
    hb = tm // POOL_HALO

    def body(p_ref, halo_ref, wp_ref, sc_ref, mixed_ref, out_ref):
        i = pl.program_id(0)
        p = p_ref[...].astype(F32)
        halo = halo_ref[...].astype(F32) * (i > 0).astype(F32)
        ext = jnp.concatenate([halo, p], axis=0)
        n = tm + POOL_HALO
        sums, acc, k = {}, ext, 1
        while k < POOL_WINDOWS[-1]:
            acc = acc + pltpu.roll(acc, k, axis=0)
            k *= 2
            sums[k] = acc
        cnts = _pool_counts(tm, i)
        mixed, lin = [], []
        for g, w in enumerate(POOL_WINDOWS):
            ls = slice(g * POOL_GD, (g + 1) * POOL_GD)
            m = sums[w][POOL_HALO:n, ls] / cnts[g] - p[:, ls]
            mixed.append(m)
            lin.append(_dot(m, wp_ref[g]))
        mixed_ref[...] = jnp.concatenate(mixed, axis=-1)
        out_ref[...] = (jnp.concatenate(lin, axis=-1) * sc_ref[...]).astype(out_ref.dtype)

    return _pcall(
        body, name="pool_fwd", grid=(S // tm,),
        in_specs=[pl.BlockSpec((tm, POOL_W), lambda i: (i, col)),
                  pl.BlockSpec((POOL_HALO, POOL_W), lambda i: (jnp.maximum(i * hb - 1, 0), col)),
                  pl.BlockSpec(w_pool.shape, lambda i: (0, 0, 0)), pl.BlockSpec(pool_scale.shape, lambda i: (0, 0))],
        out_specs=[pl.BlockSpec((tm, POOL_W), lambda i: (i, 0)), pl.BlockSpec((tm, POOL_W), lambda i: (i, 0))],
        out_shape=[_sds((S, POOL_W)), _sds((S, POOL_W), BF16)])(proj, proj, w_pool, pool_scale)


def _pool_lin_bwd(dout, mixed, w_pool, pool_scale):
    S = dout.shape[0]

    def fn(dout, mixed, wp, sc):
        dlin = dout * sc
        dm, dwp, lin = [], [], []
        for g in range(len(POOL_WINDOWS)):
            ls = slice(g * POOL_GD, (g + 1) * POOL_GD)
            lin.append(_dot(mixed[:, ls], wp[g]))
            dm.append(_dot(dlin[:, ls], wp[g], _NT))
            dwp.append(_dot(mixed[:, ls], dlin[:, ls], _TN))
        dsc = jnp.sum(dout * jnp.concatenate(lin, axis=-1), axis=0, keepdims=True)
        return jnp.concatenate(dm, axis=-1), jnp.concatenate(dwp, axis=0), dsc

    return _rowwise("pool_lin_bwd", fn, [dout, mixed], [w_pool, pool_scale], [_sds((S, POOL_W))],
                    [_sds((len(POOL_WINDOWS) * POOL_GD, POOL_GD)), _sds((1, POOL_W))])


def _pool_win_bwd(dmixed, dmain, tm=512):
    S = dmixed.shape[0]
    tm = min(tm, S // 2)
    nt = S // tm
    hb = tm // POOL_HALO

    def body(dm_ref, halo_ref, dmain_in, dp_ref):
        i = pl.program_id(0)
        dm = dm_ref[...]
        halo = halo_ref[...] * (i < nt - 1).astype(F32)
        cnts = _pool_counts(tm, i)
        cnts_h = [c[:POOL_HALO] for c in _pool_counts(tm, i + 1)]
        r = jnp.concatenate([jnp.concatenate([dm[:, g * POOL_GD:(g + 1) * POOL_GD] / cnts[g] for g in range(4)], axis=-1),
                             jnp.concatenate([halo[:, g * POOL_GD:(g + 1) * POOL_GD] / cnts_h[g] for g in range(4)], axis=-1)],
                            axis=0)
        n = tm + POOL_HALO
        sums, acc, k = {}, r, 1
        while k < POOL_WINDOWS[-1]:
            acc = acc + pltpu.roll(acc, n - k, axis=0)
            k *= 2
            sums[k] = acc
        dp = [sums[w][:tm, g * POOL_GD:(g + 1) * POOL_GD] for g, w in enumerate(POOL_WINDOWS)]
        dp_ref[...] = (jnp.concatenate(dp, axis=-1) - dm).astype(dp_ref.dtype)

    return _pcall(
        body, name="pool_win_bwd", grid=(nt,),
        in_specs=[pl.BlockSpec((tm, POOL_W), lambda i: (i, 0)),
                  pl.BlockSpec((POOL_HALO, POOL_W), lambda i: (jnp.minimum((i + 1) * hb, S // POOL_HALO - 1), 0)),
                  pl.BlockSpec(memory_space=pl.ANY)],
        out_specs=pl.BlockSpec((tm, POOL_W), lambda i: (i, _POOL_COL)),
        out_shape=_sds(dmain.shape, dmain.dtype), aliases={2: 0})(dmixed, dmixed, dmain)


def _xattn_probs(q, kv, h):
    hs = slice(h * XA_HD, (h + 1) * XA_HD)
    s = _dot(q[:, hs], kv[:, hs], _NT) * (XA_HD ** -0.5)
    s = s - jnp.max(s, axis=-1, keepdims=True)
    e = jnp.exp(s)
    return e / jnp.sum(e, axis=-1, keepdims=True)


def _xattn_fwd(proj, kv):
    S = proj.shape[0]

    def fn(q, kv):
        outs = []
        for h in range(XA_HEADS):
            p = _xattn_probs(q, kv, h)
            outs.append(_dot(p, kv[:, XA_W + h * XA_HD:XA_W + (h + 1) * XA_HD]))
        return jnp.concatenate(outs, axis=-1)

    return _rowwise("xattn_fwd", fn, [_Win(proj, XA_W, _XQ_COL)], [kv], [_sds((S, XA_W), BF16)])[0]


def _xattn_bwd(proj, kv, dxa, dmain):
    def fn(q, dxa, kv):
        dqs, dks, dvs = [], [], []
        for h in range(XA_HEADS):
            hs = slice(h * XA_HD, (h + 1) * XA_HD)
            vh = kv[:, XA_W + h * XA_HD:XA_W + (h + 1) * XA_HD]
            p = _xattn_probs(q, kv, h)
            dp = _dot(dxa[:, hs], vh, _NT)
            ds = p * (dp - jnp.sum(p * dp, axis=-1, keepdims=True)) * (XA_HD ** -0.5)
            dqs.append(_dot(ds, kv[:, hs]))
            dks.append(_dot(ds, q[:, hs], _TN))
            dvs.append(_dot(p, dxa[:, hs], _TN))
        return jnp.concatenate(dqs, axis=-1), jnp.concatenate(dks + dvs, axis=-1)

    return _rowwise("xattn_bwd", fn, [_Win(proj, XA_W, _XQ_COL), dxa], [kv], [_Win(dmain, XA_W, _XQ_COL)],
                    [_sds(kv.shape)])


def _mix_fwd(x1, h, mem, W, g_next, on_proj=None, late_weights=None):
    S = x1.shape[0]
    M = mem.shape[0]
    tm = min(512, S)
    def proj_body(h_ref, w_ref, o_ref):
        hh = h_ref[...]
        for j in range(INT_NBLK):
            o_ref[:, INT_BLK * j:INT_BLK * (j + 1)] = _dot(hh, w_ref[INT_BLK * j:INT_BLK * (j + 1), :], _NT).astype(BF16)

    proj = _pcall(
        proj_body, name="mix_proj", grid=(S // tm,),
        in_specs=[pl.BlockSpec((tm, D_MODEL), lambda i: (i, 0)),
                  pl.BlockSpec((INT_W, D_MODEL), lambda i: (0, 0), pipeline_mode=pl.Buffered(1))],
        out_specs=pl.BlockSpec((tm, INT_W), lambda i: (i, 0)), out_shape=_sds((S, INT_W), BF16))(h, W["w_int_t"])
    ya_in, states = _gla_fwd(proj, W["w_fu_pad"], W["b_f"], W["gla_norm_g"],
                             on_proj(proj) if on_proj is not None else [])
    mixed, pool_out = _pool_fwd(proj, W["w_pool"], W["pool_scale"])
    mem_n = _rowwise("mem_norm", lambda m, g: _rms(m, g), [mem], [W["mem_norm_g"]], [_sds((M, D_MODEL), BF16)])[0]
    if late_weights is not None:
        W = {**W, **late_weights([pool_out])}
    kv = _mm_nn("mem_kv", mem_n, W["w_mem_kv"])
    xa = _xattn_fwd(proj, kv)
    def out_fn(gates, ya_in, pool_out, xa, x1, wa, wb, wc, wo, g, gn):
        ya, yb, yc = (_dot(t, w).astype(BF16) for t, w in ((ya_in, wa), (pool_out, wb), (xa, wc)))
        merged = _merge(gates, ya, yb, yc).astype(BF16)
        y = _dot(merged, wo)
        x2 = x1 + _rms(y, g)
        return ya, yb, yc, merged, y, x2, _rms(x2, gn)

    ya, yb, yc, merged, y, x2, h_next = _rowwise(
        "mix_out", out_fn, [_Win(proj, 3 * D_MODEL, 1), ya_in, pool_out, xa, x1],
        [W["w_up_gla"], W["w_up_pool"], W["w_up_xattn"], W["w_o"], W["mix_post_g"], g_next],
        [_sds((S, D_MODEL), BF16)] * 5 + [_sds((S, D_MODEL)), _sds((S, D_MODEL), BF16)], tm=256)
    return x2, h_next, W, (h, proj, states, ya_in, mixed, pool_out, mem_n, kv, xa, ya, yb, yc, merged, y)


def _mix_bwd(dx2, x1, mem, W, saved, on_grads=None):
    h, proj, states, ya_in, mixed, pool_out, mem_n, kv, xa, ya, yb, yc, merged, y = saved
    S = x1.shape[0]

    def out_bwd(y, dx2, gates, ya, yb, yc, g, wo, wa, wb, wc):
        dy, dg = _rms_bwd(y.astype(F32), g, dx2)
        dy = dy.astype(BF16)
        _, vjp = jax.vjp(_merge, gates, ya, yb, yc)
        dgates, dya, dyb, dyc = vjp(_dot(dy, wo, _NT))
        return (dy, dgates, dya, dyb, dyc, _dot(dya, wa, _NT), _dot(dyb, wb, _NT), _dot(dyc, wc, _NT), dg)

    dmain = lax.empty((S, INT_MAIN), BF16)
    dy, dmain, dya, dyb, dyc, d_ya_in, d_pool_out, d_xa, d_mix_post_g = _rowwise(
        "mix_out_bwd", out_bwd, [y, dx2, _Win(proj, 3 * D_MODEL, 1), ya, yb, yc],
        [W["mix_post_g"], W["w_o"], W["w_up_gla"], W["w_up_pool"], W["w_up_xattn"]],
        [_sds((S, D_MODEL), BF16), _Win(dmain, 3 * D_MODEL, 1)] + [_sds((S, D_MODEL), BF16)] * 4
        + [_sds((S, POOL_W), BF16), _sds((S, XA_W), BF16)], [_sds((1, D_MODEL))], tm=256)
    d_w_o = _mm_tn("d_w_o", merged, dy, ts=S, tn_a=256)
    d_w_up_gla = _mm_tn("d_w_up_gla", ya_in, dya, ts=S, tn_a=256)
    d_w_up_pool = _mm_tn("d_w_up_pool", pool_out, dyb, ts=S, tn_a=128)
    d_w_up_xattn = _mm_tn("d_w_up_xattn", xa, dyc, ts=S, tn_a=128)

    dmain, dflow, d_wfu_pad, d_b_f, d_gla_norm_g = _gla_bwd(proj, W["w_fu_pad"], W["b_f"], W["gla_norm_g"], states,
                                                            d_ya_in, dmain)
    dmixed, d_w_pool, d_pool_scale = _pool_lin_bwd(d_pool_out, mixed, W["w_pool"], W["pool_scale"])
    dmain = _pool_win_bwd(dmixed, dmain)
    dmain, dkv = _xattn_bwd(proj, kv, d_xa, dmain)
    d_w_mem_kv = _mm_tn("d_w_mem_kv", mem_n, dkv)
    dmem_n = _mm_nt("d_mem_n", dkv, W["w_mem_kv"])
    d_mem_norm_g = _rowwise("mem_norm_bwd", lambda m, d, g: _rms_bwd(m, g, d)[1], [mem, dmem_n], [W["mem_norm_g"]], [],
                            [_sds((1, D_MODEL))])[0]
    d_w_int_t = (_mm_tn("d_w_int", dmain, h, ts=S, tn_a=INT_MAIN // 8), _mm_tn("d_w_flow", dflow, h, ts=S))
    grads = dict(
        w_int_t=d_w_int_t, w_fu_pad=d_wfu_pad, b_f=d_b_f, gla_norm_g=d_gla_norm_g, w_pool=d_w_pool,
        pool_scale=d_pool_scale, mem_norm_g=d_mem_norm_g, w_mem_kv=d_w_mem_kv, w_up_gla=d_w_up_gla,
        w_up_pool=d_w_up_pool, w_up_xattn=d_w_up_xattn, w_o=d_w_o, mix_post_g=d_mix_post_g)
    deps = [on_grads(grads)] if on_grads is not None else []
    dx1, grads["mix_pre_g"] = _dh_pre_norm_bwd("d_mix_h", [dmain, dflow], W["w_int_t"], x1, dx2, W["mix_pre_g"], deps)
    return dx1, grads


def _mesh_pos():
    x, y, c = lax.axis_index("x"), lax.axis_index("y"), lax.axis_index("c")
    return x, y, c, 4 * x + 2 * y + c


def _peer(x, y, c, r):
    px = 1 - x if r & 4 else x
    py = 1 - y if r & 2 else y
    pc = 1 - c if r & 1 else c
    return (px, py, pc), 4 * px + 2 * py + pc


_ALL_PEERS = tuple(range(1, N_DEV))
_SIBLING = 1
_SAME_CORE = (2, 4, 6)


def _dev_slot(ref, dev):
    return ref.at[dev]


def _zero_pad_rows(land):
    pad = FF_PAD - FF_BLK

    def body(land_in, o_ref):
        o_ref[...] = jnp.zeros_like(o_ref)

    return _pcall(body, name="zero_pad_rows", grid=(land.shape[0],), in_specs=[pl.BlockSpec(memory_space=pl.ANY)],
                  out_specs=pl.BlockSpec((None, pad, D_MODEL), lambda j: (j, FF_BLK // pad, 0)),
                  out_shape=_sds(land.shape, land.dtype), aliases={0: 0})(land)


class _Plan:
    def __init__(self, scatter, slots=None, shapes=None):
        self.scatter, self.slots, self.shapes = scatter, slots or {}, shapes or {}

    def src(self, srcs, a, dev):
        return self.slots.get(a, _dev_slot)(srcs[a], dev) if self.scatter else srcs[a]

    def dst(self, lands, a, dev):
        return lands[a].at[dev] if self.scatter else self.slots.get(a, _dev_slot)(lands[a], dev)

    def landing_zones(self, arrays):
        lands = []
        for a, arr in enumerate(arrays):
            if self.scatter:
                lands.append(lax.empty((N_DEV,) + tuple(self.shapes.get(a, arr.shape[1:])), arr.dtype))
            elif a in self.shapes:
                lands.append(_zero_pad_rows(lax.empty(self.shapes[a], arr.dtype)))
            else:
                lands.append(lax.empty((N_DEV,) + arr.shape, arr.dtype))
        return lands


_GATHER, _SCATTER = _Plan(False), _Plan(True)


def _peer_copies(srcs, lands, send_sems, recv_sems, plan, peers=_ALL_PEERS):
    x, y, c, me = _mesh_pos()
    cps = []
    for r in peers:
        pos, peer = _peer(x, y, c, r)
        for a in range(len(srcs)):
            k = a * (N_DEV - 1) + r - 1
            cps.append(pltpu.make_async_remote_copy(
                src_ref=plan.src(srcs, a, peer), dst_ref=plan.dst(lands, a, me),
                send_sem=send_sems.at[k], recv_sem=recv_sems.at[k], device_id=pos,
                device_id_type=pl.DeviceIdType.MESH))
    return cps


_HBM = pl.BlockSpec(memory_space=pltpu.HBM)
_SEM = pl.BlockSpec(memory_space=pltpu.SEMAPHORE)
_EFFECT = pltpu.SideEffectType.DATAFLOW_SIDE_EFFECTING


def _own_copies(srcs, lands, own_sems, plan):
    me = _mesh_pos()[3]
    return [pltpu.make_async_copy(plan.src(srcs, a, me), plan.dst(lands, a, me), own_sems.at[a])
            for a in range(len(srcs))]


def _exchange_start(name, arrays, plan):
    n = len(arrays)
    lands = plan.landing_zones(arrays)
    n_sem = n * (N_DEV - 1)

    def body(*refs):
        srcs, lands_ = refs[:n], refs[n:2 * n]
        send_sems, recv_sems, own_sems = refs[2 * n:2 * n + 3]
        token = refs[-1]
        for cp in _peer_copies(srcs, lands_, send_sems, recv_sems, plan) + _own_copies(srcs, lands_, own_sems, plan):
            cp.start()
        token[...] = jnp.zeros_like(token)

    hbm = lambda a: pltpu.HBM(a.shape, a.dtype)
    res = pl.pallas_call(
        body, name=name,
        out_shape=(pltpu.SemaphoreType.DMA((n_sem,)), pltpu.SemaphoreType.DMA((n_sem,)), pltpu.SemaphoreType.DMA((n,)),
                   *[hbm(a) for a in arrays], *[hbm(a) for a in lands], _sds((8, 128))),
        in_specs=[_HBM] * (2 * n),
        out_specs=(_SEM, _SEM, _SEM, *[_HBM] * (2 * n), pl.BlockSpec(memory_space=pltpu.VMEM)),
        input_output_aliases={i: 3 + i for i in range(2 * n)},
        compiler_params=pltpu.CompilerParams(has_side_effects=_EFFECT),
    )(*[pltpu.with_memory_space_constraint(a, pltpu.HBM) for a in list(arrays) + lands])
    return (res[:3], res[3:3 + n], res[3 + n:3 + 2 * n], plan), res[-1]


def _exchange_wait(name, state, after):
    sems, srcs, lands, plan = state
    n = len(srcs)

    def body(*refs):
        srcs_, lands_ = refs[:n], refs[n:2 * n]
        send_sems, recv_sems, own_sems = refs[2 * n:2 * n + 3]
        for cp in _peer_copies(srcs_, lands_, send_sems, recv_sems, plan):
            cp.wait_send()
            cp.wait_recv()
        for cp in _own_copies(srcs_, lands_, own_sems, plan):
            cp.wait()

    hbm = lambda a: pltpu.HBM(a.shape, a.dtype)
    res = pl.pallas_call(
        body, name=name, out_shape=tuple(hbm(a) for a in list(srcs) + list(lands)),
        in_specs=[_HBM] * (2 * n) + [_SEM] * 3 + [pl.BlockSpec(memory_space=pl.ANY)] * len(after),
        out_specs=tuple([_HBM] * (2 * n)), input_output_aliases={i: i for i in range(2 * n)},
        compiler_params=pltpu.CompilerParams(has_side_effects=_EFFECT),
    )(*srcs, *lands, *sems, *after)
    return res[n:]


def _gather_start(name, arrays, plan, after):
    n = len(arrays)
    lands = plan.landing_zones(arrays)
    n_sem = n * (N_DEV - 1)

    def body(*refs):
        srcs, lands_ = refs[:n], refs[n:2 * n]
        send_sems, recv_sems, own_sems = refs[2 * n + len(after):2 * n + len(after) + 3]
        token = refs[-1]
        for cp in (_peer_copies(srcs, lands_, send_sems, recv_sems, plan, (_SIBLING,) + _SAME_CORE)
                   + _own_copies(srcs, lands_, own_sems, plan)):
            cp.start()
        token[...] = jnp.zeros_like(token)

    hbm = lambda a: pltpu.HBM(a.shape, a.dtype)
    res = pl.pallas_call(
        body, name=name,
        out_shape=(pltpu.SemaphoreType.DMA((n_sem,)), pltpu.SemaphoreType.DMA((n_sem,)), pltpu.SemaphoreType.DMA((n,)),
                   *[hbm(a) for a in arrays], *[hbm(a) for a in lands], _sds((8, 128))),
        in_specs=[_HBM] * (2 * n) + [pl.BlockSpec(memory_space=pl.ANY)] * len(after),
        out_specs=(_SEM, _SEM, _SEM, *[_HBM] * (2 * n), pl.BlockSpec(memory_space=pltpu.VMEM)),
        input_output_aliases={i: 3 + i for i in range(2 * n)},
        compiler_params=pltpu.CompilerParams(has_side_effects=_EFFECT),
    )(*[pltpu.with_memory_space_constraint(a, pltpu.HBM) for a in list(arrays) + lands], *after)
    return (res[:3], res[3:3 + n], res[3 + n:3 + 2 * n], plan), res[-1]


def _pass_on_copies(lands, send_sems, recv_sems, plan):
    x, y, c, _ = _mesh_pos()
    sibling = _peer(x, y, c, _SIBLING)[0]
    cps = []
    for i, r in enumerate(_SAME_CORE):
        owner = _peer(x, y, c, r)[1]
        for a in range(len(lands)):
            k = a * len(_SAME_CORE) + i
            cps.append(pltpu.make_async_remote_copy(
                src_ref=plan.dst(lands, a, owner), dst_ref=plan.dst(lands, a, owner), send_sem=send_sems.at[k],
                recv_sem=recv_sems.at[k], device_id=sibling, device_id_type=pl.DeviceIdType.MESH))
    return cps


def _gather_pass_on(name, state, after):
    sems, srcs, lands, plan = state
    n = len(srcs)
    n_sem = n * len(_SAME_CORE)

    def body(*refs):
        srcs_, lands_ = refs[:n], refs[n:2 * n]
        send_sems, recv_sems = refs[2 * n], refs[2 * n + 1]
        on_send, on_recv = refs[2 * n + 3 + len(after)], refs[2 * n + 4 + len(after)]
        token = refs[-1]
        arrivals = _peer_copies(srcs_, lands_, send_sems, recv_sems, plan, _SAME_CORE)
        for arrived, on in zip(arrivals, _pass_on_copies(lands_, on_send, on_recv, plan)):
            arrived.wait_recv()
            on.start()
        token[...] = jnp.zeros_like(token)

    hbm = lambda a: pltpu.HBM(a.shape, a.dtype)
    res = pl.pallas_call(
        body, name=name,
        out_shape=(pltpu.SemaphoreType.DMA((n_sem,)), pltpu.SemaphoreType.DMA((n_sem,)),
                   *[hbm(a) for a in list(srcs) + list(lands)], _sds((8, 128))),
        in_specs=[_HBM] * (2 * n) + [_SEM] * 3 + [pl.BlockSpec(memory_space=pl.ANY)] * len(after),
        out_specs=(_SEM, _SEM, *[_HBM] * (2 * n), pl.BlockSpec(memory_space=pltpu.VMEM)),
        input_output_aliases={i: 2 + i for i in range(2 * n)},
        compiler_params=pltpu.CompilerParams(has_side_effects=_EFFECT),
    )(*srcs, *lands, *sems, *after)
    return (sems, res[:2], res[2:2 + n], res[2 + n:2 + 2 * n], plan), res[-1]


def _gather_wait(name, state, after):
    sems, on_sems, srcs, lands, plan = state
    n = len(srcs)

    def body(*refs):
        srcs_, lands_ = refs[:n], refs[n:2 * n]
        send_sems, recv_sems, own_sems, on_send, on_recv = refs[2 * n:2 * n + 5]
        for cp in _peer_copies(srcs_, lands_, send_sems, recv_sems, plan, (_SIBLING,)):
            cp.wait_recv()
        for cp in _peer_copies(srcs_, lands_, send_sems, recv_sems, plan, (_SIBLING,) + _SAME_CORE):
            cp.wait_send()
        for cp in _own_copies(srcs_, lands_, own_sems, plan):
            cp.wait()
        for cp in _pass_on_copies(lands_, on_send, on_recv, plan):
            cp.wait_send()
            cp.wait_recv()

    hbm = lambda a: pltpu.HBM(a.shape, a.dtype)
    res = pl.pallas_call(
        body, name=name, out_shape=tuple(hbm(a) for a in list(srcs) + list(lands)),
        in_specs=[_HBM] * (2 * n) + [_SEM] * 5 + [pl.BlockSpec(memory_space=pl.ANY)] * len(after),
        out_specs=tuple([_HBM] * (2 * n)), input_output_aliases={i: i for i in range(2 * n)},
        compiler_params=pltpu.CompilerParams(has_side_effects=_EFFECT),
    )(*srcs, *lands, *sems, *on_sems, *after)
    return res[n:]


def _sum_parts(name, recv):
    def body(r_ref, o_ref):
        s = r_ref[0]
        for j in range(1, N_DEV):
            s = s + r_ref[j]
        o_ref[...] = s

    return pl.pallas_call(body, name=name, out_shape=_sds(recv.shape[1:], recv.dtype))(recv)


def _adam(name, recv, w, m, v, col=0):
    shape = w.shape
    R, C = shape
    tr = R
    while N_DEV * tr * C * 4 > 6 * 1024 * 1024 and tr % 32 == 0:
        tr //= 2

    def body(recv_ref, w_ref, m_ref, v_ref, g_ref, d_ref, m2_ref, v2_ref):
        g = recv_ref[0].astype(F32)
        for j in range(1, N_DEV):
            g = g + recv_ref[j].astype(F32)
        w_, m_, v_ = w_ref[...], m_ref[...], v_ref[...]
        m2 = ADAM_B1 * m_ + (1.0 - ADAM_B1) * g
        v2 = ADAM_B2 * v_ + (1.0 - ADAM_B2) * (g * g)
        m_hat = m2 / (1.0 - ADAM_B1 ** ADAM_STEP)
        v_hat = v2 / (1.0 - ADAM_B2 ** ADAM_STEP)
        g_ref[...] = g
        d_ref[...] = -ADAM_LR * (m_hat / (jnp.sqrt(v_hat) + ADAM_EPS) + ADAM_WD * w_)
        m2_ref[...] = m2
        v2_ref[...] = v2

    blk = pl.BlockSpec((tr, C), lambda i: (i, 0))
    return _pcall(body, name=name, grid=(R // tr,),
                  in_specs=[pl.BlockSpec((N_DEV, tr, C), lambda i: (0, i, col)), blk, blk, blk],
                  out_specs=[blk] * 4, out_shape=[_sds(shape)] * 4)(recv, w, m, v)


_NAMES = ['ffn1_pre_g', 'ffn1_w_in', 'ffn1_w_out', 'ffn1_post_g', 'mix_pre_g', 'w_in', 'w_fu', 'b_f', 'gla_norm_g',
          'w_pool', 'pool_scale', 'mem_norm_g', 'w_mem_kv', 'w_up_gla', 'w_up_pool', 'w_up_xattn', 'w_o', 'mix_post_g',
          'ffn2_pre_g', 'ffn2_w_in', 'ffn2_w_out', 'ffn2_post_g', 'final_g']
_SHARDED = ['ffn1_w_in', 'ffn1_w_out', 'w_in', 'w_fu', 'w_mem_kv', 'w_up_gla', 'w_up_pool', 'w_up_xattn', 'w_o',
            'ffn2_w_in', 'ffn2_w_out']
_COL_SHARDED = ['w_up_pool', 'w_up_xattn']


def _cols_to_full(g):
    return jnp.transpose(g, (1, 0, 2)).reshape(g.shape[1], N_DEV * g.shape[2])


def _full_to_cols(f):
    R, C = f.shape
    return jnp.transpose(f.reshape(R, N_DEV, C // N_DEV), (1, 0, 2))


def _to_internal(w_in_t):
    o = 0
    parts = []
    for s in IN_SPLITS:
        parts.append(w_in_t[o:o + s])
        o += s
    q, k, v, g_out, f_low, p_in, xq, gates = parts
    f_low = jnp.pad(f_low, ((0, FLOW_W - GATE_RANK), (0, 0)))
    return jnp.concatenate([q, k, v, g_out, gates, p_in, xq, f_low], axis=0)


def _from_internal(d, d_flow):
    q, k, v, g_out = d[0:512], d[512:1024], d[1024:2048], d[2048:3072]
    gates, p_in, xq = d[3072:6144], d[6144:6656], d[6656:7168]
    return jnp.concatenate([q, k, v, g_out, d_flow[:GATE_RANK], p_in, xq, gates], axis=0)


_FFN_IN = ('ffn1_w_in', 'ffn2_w_in')
_FFN_OUT = ('ffn1_w_out', 'ffn2_w_out')
_GATHERS = {"ffn1_in": ['ffn1_w_in'], "ffn1_out": ['ffn1_w_out'],
            "mix_in": ['w_in', 'w_fu'], "mix_rest": ['w_mem_kv', 'w_up_gla', 'w_up_pool', 'w_up_xattn', 'w_o'],
            "ffn2": ['ffn2_w_in', 'ffn2_w_out']}
_MIX = _GATHERS["mix_in"] + _GATHERS["mix_rest"]


def _ffn_in_slot(ref, d):
    return ref.at[2 * (d % N_FF_BLK) + d // N_FF_BLK, pl.ds(0, FF_BLK)]


def _ffn_out_slot(ref, d):
    rows = FF_BLK // 2
    return ref.at[d // 2, pl.ds(pl.multiple_of((d % 2) * rows, rows), rows)]


def _ffn_plan(names, scatter):
    slots, shapes = {}, {}
    for a, n in enumerate(names):
        if n in _FFN_IN:
            slots[a] = _ffn_in_slot
            shapes[a] = (FF_BLK, D_MODEL) if scatter else (N_DEV, FF_PAD, D_MODEL)
        elif n in _FFN_OUT:
            slots[a] = _ffn_out_slot
            shapes[a] = (FF_BLK // 2, D_MODEL) if scatter else (N_FF_BLK, FF_PAD, D_MODEL)
    return _Plan(scatter, slots, shapes)


def _rows_to_full(g):
    return g.reshape(N_DEV * g.shape[1], g.shape[2])


def _mix_weights(gathered):
    W = {}
    for n, g in gathered.items():
        if n == "w_in":
            W["w_int_t"] = _to_internal(g.reshape(IN_WIDTH, D_MODEL))
        elif n == "w_fu":
            W["w_fu_pad"] = jnp.pad(_cols_to_full(g), ((0, FLOW_W - GATE_RANK), (0, 0)))
        else:
            W[n] = _cols_to_full(g) if n in _COL_SHARDED else _rows_to_full(g)
    return W


def _mix_chunks(G, n):
    if n == "w_in":
        return _from_internal(*G["w_int_t"]).reshape(N_DEV, IN_SHARD, D_MODEL)
    if n == "w_fu":
        return _full_to_cols(G["w_fu_pad"][:GATE_RANK].astype(BF16))
    if n in _COL_SHARDED:
        return _full_to_cols(G[n])
    return G[n].reshape(N_DEV, G[n].shape[0] // N_DEV, G[n].shape[1])


def _step(x, mem, tgt, P, Mo, Vo):
    def native(n, a):
        return jnp.swapaxes(a, 0, 1) if n in _FFN_IN + ("w_in",) else a

    P, Mo, Vo = ({n: native(n, a) for n, a in d.items()} for d in (P, Mo, Vo))
    small = {n: P[n] for n in _NAMES if n not in _SHARDED}

    gather, tokens = {}, []
    for grp, names in _GATHERS.items():
        gather[grp], tok = _gather_start("gather_" + grp, [P[n].astype(BF16) for n in names], _ffn_plan(names, False),
                                         tokens[-1:])
        tokens.append(tok)

    def pass_on(grp, after):
        return _gather_pass_on("gather_" + grp + "_on", gather[grp], after)

    def gathered(grp, after):
        return _gather_wait("gather_" + grp + "_wait", pass_on(grp, after)[0], [])

    def ffn_wo(lands):
        return lambda act: lands[0].reshape(N_FF_BLK * FF_PAD, D_MODEL)

    h1 = _pre_norm("ffn1_pre", x, small["ffn1_pre_g"], tokens[:1])
    w1t = gathered("ffn1_in", tokens[-1:] + [h1])[0].reshape(N_DEV * FF_PAD, D_MODEL)
    (x1, hm), w1o, sv1 = _ffn_fwd("ffn1", x, h1, w1t, small["ffn1_post_g"],
                                  lambda act: ffn_wo(gathered("ffn1_out", [act]))(act), small["mix_pre_g"])

    Wm = {**small, **_mix_weights(dict(zip(_GATHERS["mix_in"], gathered("mix_in", [x1]))))}
    passed = {}

    def pass_on_later(proj):
        tokens = []
        for grp in ("mix_rest", "ffn2"):
            passed[grp], tok = pass_on(grp, [proj])
            tokens.append(tok)
        return tokens

    def mix_rest(after):
        names = _GATHERS["mix_rest"]
        return _mix_weights(dict(zip(names, _gather_wait("gather_mix_rest_wait", passed["mix_rest"], after))))

    x2, h2, Wm, svm = _mix_fwd(x1, hm, mem, Wm, small["ffn2_pre_g"], pass_on_later, mix_rest)
    w2t, w2o = _gather_wait("gather_ffn2_wait", passed["ffn2"], [x2])
    w2t = w2t.reshape(N_DEV * FF_PAD, D_MODEL)
    (loss, dx3, d_final_g), w2o, sv2 = _ffn_fwd("ffn2", x2, h2, w2t, small["ffn2_post_g"], ffn_wo([w2o]),
                                                small["final_g"], tgt)

    G = dict(final_g=d_final_g)
    scat = {}

    def start(grp, names, arrays):
        scat[grp] = names, _exchange_start("scatter_" + grp, arrays, _ffn_plan(names, True))
        return scat[grp][1][1]

    def ffn_starts(tag):
        return (lambda dwo: start(tag + "_out", [tag + "_w_out"], [dwo.reshape(N_FF_BLK, FF_PAD, D_MODEL)]),
                lambda dwt: start(tag + "_in", [tag + "_w_in"], [dwt.reshape(N_DEV, FF_PAD, D_MODEL)]))

    dx2, G["ffn2_pre_g"], G["ffn2_post_g"] = _ffn_bwd(
        "ffn2", dx3, x2, small["ffn2_pre_g"], w2t, w2o, small["ffn2_post_g"], sv2, *ffn_starts("ffn2"))
    dx1, Gm = _mix_bwd(dx2, x1, mem, Wm, svm,
                       lambda Gm: start("mix", _MIX, [_mix_chunks(Gm, n) for n in _MIX]))
    G.update(Gm)
    dx, G["ffn1_pre_g"], G["ffn1_post_g"] = _ffn_bwd(
        "ffn1", dx1, x, small["ffn1_pre_g"], w1t, w1o, small["ffn1_post_g"], sv1, *ffn_starts("ffn1"))

    vecs = sorted((n for n in small if P[n].ndim == 2), key=lambda n: -P[n].shape[1])
    packed = jnp.concatenate([G[n] for n in vecs] + [loss], axis=1)
    place, o = {}, 0
    for n in vecs:
        place[n] = o // P[n].shape[1]
        o += P[n].shape[1]
    scat["small"] = ["packed", "w_pool"], _exchange_start("gather_small_grads", [packed, G["w_pool"]], _GATHER)
    recv, outs = {}, {}
    done = [dx]
    for grp in ["ffn2_out", "ffn2_in", "mix", "ffn1_out", "ffn1_in", "small"]:
        names, (state, _) = scat[grp]
        recv.update(zip(names, _exchange_wait(("gather_" if grp == "small" else "scatter_") + grp + "_wait", state,
                                              done)))
        if grp == "small":
            loss = _sum_parts("loss_sum", recv["packed"][:, :, o:])[0, 0]
            names = list(small)
        for n in names:
            shp = P[n].shape
            shp2 = shp if len(shp) == 2 else (shp[0] * shp[1], shp[2])
            r = recv["packed"] if n in place else recv[n].reshape((N_DEV,) + shp2)
            res = _adam("adam_" + n, r, P[n].reshape(shp2), Mo[n].reshape(shp2), Vo[n].reshape(shp2), place.get(n, 0))
            outs[n] = [native(n, t.reshape(shp)) for t in res]
            done.append(res[-1])
    return loss, dx, outs


def kernel(x, mem, ffn1_pre_g, ffn1_w_in, ffn1_w_out, ffn1_post_g, mix_pre_g, w_in, w_fu, b_f, gla_norm_g, w_pool, pool_scale, mem_norm_g, w_mem_kv, w_up_gla, w_up_pool, w_up_xattn, w_o, mix_post_g, ffn2_pre_g, ffn2_w_in, ffn2_w_out, ffn2_post_g, final_g, loss_target, m_ffn1_pre_g, m_ffn1_w_in, m_ffn1_w_out, m_ffn1_post_g, m_mix_pre_g, m_w_in, m_w_fu, m_b_f, m_gla_norm_g, m_w_pool, m_pool_scale, m_mem_norm_g, m_w_mem_kv, m_w_up_gla, m_w_up_pool, m_w_up_xattn, m_w_o, m_mix_post_g, m_ffn2_pre_g, m_ffn2_w_in, m_ffn2_w_out, m_ffn2_post_g, m_final_g, v_ffn1_pre_g, v_ffn1_w_in, v_ffn1_w_out, v_ffn1_post_g, v_mix_pre_g, v_w_in, v_w_fu, v_b_f, v_gla_norm_g, v_w_pool, v_pool_scale, v_mem_norm_g, v_w_mem_kv, v_w_up_gla, v_w_up_pool, v_w_up_xattn, v_w_o, v_mix_post_g, v_ffn2_pre_g, v_ffn2_w_in, v_ffn2_w_out, v_ffn2_post_g, v_final_g):
    params = [ffn1_pre_g, ffn1_w_in, ffn1_w_out, ffn1_post_g, mix_pre_g, w_in, w_fu, b_f, gla_norm_g, w_pool, pool_scale, mem_norm_g, w_mem_kv, w_up_gla, w_up_pool, w_up_xattn, w_o, mix_post_g, ffn2_pre_g, ffn2_w_in, ffn2_w_out, ffn2_post_g, final_g]
    moms = [m_ffn1_pre_g, m_ffn1_w_in, m_ffn1_w_out, m_ffn1_post_g, m_mix_pre_g, m_w_in, m_w_fu, m_b_f, m_gla_norm_g, m_w_pool, m_pool_scale, m_mem_norm_g, m_w_mem_kv, m_w_up_gla, m_w_up_pool, m_w_up_xattn, m_w_o, m_mix_post_g, m_ffn2_pre_g, m_ffn2_w_in, m_ffn2_w_out, m_ffn2_post_g, m_final_g]
    vars_ = [v_ffn1_pre_g, v_ffn1_w_in, v_ffn1_w_out, v_ffn1_post_g, v_mix_pre_g, v_w_in, v_w_fu, v_b_f, v_gla_norm_g, v_w_pool, v_pool_scale, v_mem_norm_g, v_w_mem_kv, v_w_up_gla, v_w_up_pool, v_w_up_xattn, v_w_o, v_mix_post_g, v_ffn2_pre_g, v_ffn2_w_in, v_ffn2_w_out, v_ffn2_post_g, v_final_g]
    P = {n: a[0] if a.ndim > 2 else a for n, a in zip(_NAMES, params)}
    Mo = {n: a[0] if a.ndim > 2 else a for n, a in zip(_NAMES, moms)}
    Vo = {n: a[0] if a.ndim > 2 else a for n, a in zip(_NAMES, vars_)}
    loss, dx, outs = _step(x[0], mem[0], loss_target[0], P, Mo, Vo)
    out = [loss, dx[None]]
    for kind in range(4):
        for n, p in zip(_NAMES, params):
            out.append(outs[n][kind].reshape(p.shape))
    return tuple(out)
```

```python
import functools

import jax
import jax.numpy as jnp
from jax import lax
from jax.experimental import pallas as pl
from jax.experimental.pallas import tpu as pltpu

F32 = jnp.float32
BF16 = jnp.bfloat16

N_DEV = 8
D_MODEL = 1024
D_FF = 2816
FF_BLK = 2 * D_FF // N_DEV
N_FF_BLK = D_FF // FF_BLK
FF_PAD = 768
CHUNK = 64
GLA_HEADS = 4
GLA_DK = 512
GLA_DV = 1024
GLA_HDK = 128
GLA_HDV = 256
GATE_RANK = 16
GATE_TEMP = 16.0
POOL_WINDOWS = (2, 4, 8, 16)
POOL_W = 512
POOL_GD = 128
POOL_HALO = 16
XA_HEADS = 4
XA_HD = 128
XA_W = 512
EPS = 1e-6
IN_SPLITS = (GLA_DK, GLA_DK, GLA_DV, GLA_DV, GATE_RANK, POOL_W, XA_W, 3 * D_MODEL)
IN_WIDTH = sum(IN_SPLITS)
IN_SHARD = IN_WIDTH // N_DEV
INT_W = 3072 + 3072 + 1024 + 128
INT_NBLK = 3
INT_BLK = INT_W // INT_NBLK
FLOW_W = 128
INT_MAIN = INT_W - FLOW_W
_POOL_COL = (3072 + 3072) // POOL_W
_XQ_COL = (3072 + 3072 + POOL_W) // XA_W

ADAM_LR = 0.001
ADAM_B1 = 0.9
ADAM_B2 = 0.999
ADAM_EPS = 1e-08
ADAM_WD = 0.01
ADAM_STEP = 10

VMEM_LIMIT = 56 * 1024 * 1024

_NN = (((1,), (0,)), ((), ()))
_NT = (((1,), (1,)), ((), ()))
_TN = (((0,), (0,)), ((), ()))


def _pcall(body, *, name, grid, in_specs, out_specs, out_shape, scratch=(), aliases=None):
    return pl.pallas_call(
        body, name=name, grid=grid, in_specs=in_specs, out_specs=out_specs, out_shape=out_shape,
        scratch_shapes=list(scratch), input_output_aliases=aliases or {},
        compiler_params=pltpu.CompilerParams(dimension_semantics=("arbitrary",) * len(grid),
                                             vmem_limit_bytes=VMEM_LIMIT))


def _dot(a, b, dims=_NN):
    return lax.dot_general(a.astype(BF16), b.astype(BF16), dims, preferred_element_type=F32)


def _mm(name, a, b, *, grid, a_blk, a_map, b_blk, b_map, o_shape, o_blk, o_map, dims, out_dtype=F32, deps=()):
    nk = grid[2]

    def body(a_ref, b_ref, *rest):
        o_ref, scr = rest[len(deps)], rest[len(deps) + 1:]
        p = _dot(a_ref[...], b_ref[...], dims)
        if nk == 1:
            o_ref[...] = p.astype(o_ref.dtype)
        else:
            acc = scr[0]
            k = pl.program_id(2)

            @pl.when(k == 0)
            def _():
                acc[...] = p

            @pl.when(k > 0)
            def _():
                acc[...] += p

            @pl.when(k == nk - 1)
            def _():
                o_ref[...] = acc[...].astype(o_ref.dtype)

    acc_shape = tuple(d for d in o_blk if d is not None)
    return _pcall(body, name=name, grid=grid,
                  in_specs=[pl.BlockSpec(a_blk, a_map), pl.BlockSpec(b_blk, b_map)]
                  + [pl.BlockSpec(memory_space=pl.ANY)] * len(deps),
                  out_specs=pl.BlockSpec(o_blk, o_map),
                  out_shape=jax.ShapeDtypeStruct(o_shape, out_dtype),
                  scratch=[pltpu.VMEM(acc_shape, F32)] if nk > 1 else [])(a, b, *deps)


def _mm_nn(name, a, b, out_dtype=F32, tm=1024, tn=None):
    M, K = a.shape
    N = b.shape[1]
    tm, tn = min(tm, M), (tn or N)
    return _mm(name, a, b, grid=(N // tn, M // tm, 1), a_blk=(tm, K), a_map=lambda j, i, k: (i, 0),
               b_blk=(K, tn), b_map=lambda j, i, k: (0, j), o_shape=(M, N), o_blk=(tm, tn),
               o_map=lambda j, i, k: (i, j), dims=_NN, out_dtype=out_dtype)


def _mm_nt(name, a, b, out_dtype=F32, tm=1024):
    M, K = a.shape
    N = b.shape[0]
    tm = min(tm, M)
    return _mm(name, a, b, grid=(1, M // tm, 1), a_blk=(tm, K), a_map=lambda j, i, k: (i, 0),
               b_blk=(N, K), b_map=lambda j, i, k: (0, 0), o_shape=(M, N), o_blk=(tm, N),
               o_map=lambda j, i, k: (i, 0), dims=_NT, out_dtype=out_dtype)


def _mm_tn(name, a, b, out_dtype=BF16, ts=512, tn=None, tn_a=None, deps=()):
    S, M = a.shape
    N = b.shape[1]
    ts, tn, tn_a = min(ts, S), (tn or N), (tn_a or M)
    return _mm(name, a, b, grid=((N // tn) * (M // tn_a), 1, S // ts), a_blk=(ts, tn_a),
               a_map=lambda j, i, k: (k, j if tn_a < M else 0), b_blk=(ts, tn),
               b_map=lambda j, i, k: (k, j if tn < N else 0), o_shape=(M, N), o_blk=(tn_a, tn),
               o_map=lambda j, i, k: (j, 0) if tn_a < M else (0, j), dims=_TN, out_dtype=out_dtype, deps=deps)


class _Win:
    def __init__(self, arr, w, c):
        self.arr, self.w, self.c = arr, w, c


def _row_spec(x, tm):
    if isinstance(x, _Win):
        return x.arr, pl.BlockSpec((tm, x.w), functools.partial(lambda i, c: (i, c), c=x.c))
    if x.ndim == 3:
        return x, pl.BlockSpec((x.shape[0], tm, x.shape[2]), lambda i: (0, i, 0))
    return x, pl.BlockSpec((tm, x.shape[1]), lambda i: (i, 0))


def _rowwise(name, fn, rows, consts, outs, accs=(), tm=512):
    first = rows[0].arr if isinstance(rows[0], _Win) else rows[0]
    S = first.shape[1] if first.ndim == 3 else first.shape[0]
    tm = min(tm, S)
    n_in, n_out = len(rows) + len(consts), len(outs)
    arrays, in_specs = [], []
    for r in rows:
        arr, spec = _row_spec(r, tm)
        arrays.append(arr)
        in_specs.append(spec)
    for c in consts:
        arrays.append(c)
        in_specs.append(pl.BlockSpec(c.shape, functools.partial(lambda i, n: (0,) * n, n=c.ndim)))
    aliases = {}
    for k, o in enumerate(outs):
        if isinstance(o, _Win):
            aliases[len(arrays)] = k
            arrays.append(o.arr)
            in_specs.append(pl.BlockSpec(memory_space=pl.ANY))
    n_thru = len(aliases)
    out_specs = [_row_spec(o, tm)[1] for o in outs]
    out_specs += [pl.BlockSpec(a.shape, functools.partial(lambda i, n: (0,) * n, n=len(a.shape))) for a in accs]
    out_shape = [_sds(o.arr.shape, o.arr.dtype) if isinstance(o, _Win) else _sds(o.shape, o.dtype) for o in outs]

    def body(*refs):
        res = fn(*[r[...] for r in refs[:n_in]])
        if not isinstance(res, (tuple, list)):
            res = (res,)
        orefs = refs[n_in + n_thru:]
        for r, v in zip(orefs[:n_out], res[:n_out]):
            r[...] = v.astype(r.dtype)
        i = pl.program_id(0)
        for r, v in zip(orefs[n_out:], res[n_out:]):
            @pl.when(i == 0)
            def _(r=r, v=v):
                r[...] = v.astype(r.dtype)

            @pl.when(i > 0)
            def _(r=r, v=v):
                r[...] += v.astype(r.dtype)

    return _pcall(body, name=name, grid=(S // tm,), in_specs=in_specs, out_specs=out_specs,
                  out_shape=out_shape + [_sds(a.shape, a.dtype) for a in accs], aliases=aliases)(*arrays)


def _sds(shape, dtype=F32):
    return jax.ShapeDtypeStruct(shape, dtype)


def _rms(x, g):
    return x * lax.rsqrt(jnp.mean(x * x, axis=-1, keepdims=True) + EPS) * g


def _rms_bwd(x, g, dy):
    _, vjp = jax.vjp(_rms, x, g)
    return vjp(dy)


def _sigmoid(x):
    return 0.5 * jnp.tanh(0.5 * x) + 0.5


def _silu(x):
    return x * _sigmoid(x)


def _log_sigmoid(f):
    return jnp.minimum(f, 0.0) - jnp.log(1.0 + jnp.exp(-jnp.abs(f)))


def _head_rms_gate(o, g_out, gn):
    parts = [_rms(o[:, h * GLA_HDV:(h + 1) * GLA_HDV], gn[:, h * GLA_HDV:(h + 1) * GLA_HDV]) for h in range(GLA_HEADS)]
    return jnp.concatenate(parts, axis=-1) * _silu(g_out.astype(F32))


def _merge(gates, ya, yb, yc):
    gates, ya, yb, yc = (t.astype(F32) for t in (gates, ya, yb, yc))
    return (_sigmoid(gates[:, :D_MODEL]) * ya + _sigmoid(gates[:, D_MODEL:2 * D_MODEL]) * yb
            + _sigmoid(gates[:, 2 * D_MODEL:]) * yc)


def _tri_dot(t, x):
    hi = x.astype(BF16)
    r1 = x - hi.astype(F32)
    mid = r1.astype(BF16)
    lo = (r1 - mid.astype(F32)).astype(BF16)
    d = functools.partial(lax.dot_general, dimension_numbers=_NN, preferred_element_type=F32)
    return d(t, hi) + d(t, mid) + d(t, lo)


def _pre_norm(name, x, g, after):
    S = x.shape[0]
    tm = min(512, S)

    def body(x_ref, g_ref, *rest):
        rest[-1][...] = _rms(x_ref[...], g_ref[...]).astype(BF16)

    row = pl.BlockSpec((tm, D_MODEL), lambda i: (i, 0))
    return _pcall(body, name=name, grid=(S // tm,),
                  in_specs=[row, pl.BlockSpec((1, D_MODEL), lambda i: (0, 0))] + [pl.BlockSpec(memory_space=pl.ANY)] * len(after),
                  out_specs=row, out_shape=_sds(x.shape, BF16))(x, g, *after)


def _ffn_fwd(tag, x, h, wt, gpost, get_wo, g_next, tgt=None):
    S = x.shape[0]
    tm = min(512, S)

    def up_body(h_ref, w_ref, u_ref, act_ref):
        hh = h_ref[...]
        for j in range(N_FF_BLK):
            ab = _dot(hh, w_ref[2 * FF_PAD * j:2 * FF_PAD * (j + 1), :], _NT)
            u_ref[:, 2 * FF_PAD * j:2 * FF_PAD * (j + 1)] = ab.astype(BF16)
            act_ref[:, FF_PAD * j:FF_PAD * (j + 1)] = (_silu(ab[:, :FF_PAD]) * ab[:, FF_PAD:]).astype(BF16)

    u, act = _pcall(
        up_body, name=tag + "_up", grid=(S // tm,),
        in_specs=[pl.BlockSpec((tm, D_MODEL), lambda i: (i, 0)),
                  pl.BlockSpec(wt.shape, lambda i: (0, 0), pipeline_mode=pl.Buffered(1))],
        out_specs=[pl.BlockSpec((tm, N_DEV * FF_PAD), lambda i: (i, 0)),
                   pl.BlockSpec((tm, N_FF_BLK * FF_PAD), lambda i: (i, 0))],
        out_shape=[_sds((S, N_DEV * FF_PAD), BF16), _sds((S, N_FF_BLK * FF_PAD), BF16)],
    )(h, wt)

    def down_body(act_ref, w_ref, x_ref, g_ref, gn_ref, f_ref, xo_ref, hn_ref):
        f = _dot(act_ref[...], w_ref[...])
        f_ref[...] = f.astype(BF16)
        xo = x_ref[...] + 0.5 * _rms(f, g_ref[...])
        xo_ref[...] = xo
        hn_ref[...] = _rms(xo, gn_ref[...]).astype(BF16)

    def down_head_body(act_ref, w_ref, x_ref, g_ref, gn_ref, t_ref, f_ref, dxo_ref, loss_ref, dgn_ref):
        i = pl.program_id(0)
        f = _dot(act_ref[...], w_ref[...])
        f_ref[...] = f.astype(BF16)
        xo = x_ref[...] + 0.5 * _rms(f, g_ref[...])
        out, vjp = jax.vjp(_rms, xo, gn_ref[...])
        e = out - t_ref[...]
        loss = 0.5 * jnp.sum(jnp.mean(e * e, axis=-1, keepdims=True), axis=0, keepdims=True)
        dxo_ref[...], dgn = vjp(e * (1.0 / D_MODEL))

        @pl.when(i == 0)
        def _():
            loss_ref[...] = jnp.broadcast_to(loss, loss_ref.shape)
            dgn_ref[...] = dgn

        @pl.when(i > 0)
        def _():
            loss_ref[...] += jnp.broadcast_to(loss, loss_ref.shape)
            dgn_ref[...] += dgn

    wo = get_wo(act)
    row = pl.BlockSpec((tm, D_MODEL), lambda i: (i, 0))
    vec = pl.BlockSpec((1, D_MODEL), lambda i: (0, 0))
    if tgt is not None:
        f, dxo, loss, dg_next = _pcall(
            down_head_body, name=tag + "_down", grid=(S // tm,),
            in_specs=[pl.BlockSpec((tm, N_FF_BLK * FF_PAD), lambda i: (i, 0)),
                      pl.BlockSpec(wo.shape, lambda i: (0, 0), pipeline_mode=pl.Buffered(1)), row, vec, vec, row],
            out_specs=[row, row, pl.BlockSpec((1, 128), lambda i: (0, 0)), vec],
            out_shape=[_sds((S, D_MODEL), BF16), _sds((S, D_MODEL)), _sds((1, 128)), _sds((1, D_MODEL))],
        )(act, wo, x, gpost, g_next, tgt)
        return (loss, dxo, dg_next), wo, (h, u, act, f)
    f, xo, h_next = _pcall(
        down_body, name=tag + "_down", grid=(S // tm,),
        in_specs=[pl.BlockSpec((tm, N_FF_BLK * FF_PAD), lambda i: (i, 0)),
                  pl.BlockSpec(wo.shape, lambda i: (0, 0), pipeline_mode=pl.Buffered(1)), row, vec, vec],
        out_specs=[row, row, row], out_shape=[_sds((S, D_MODEL), BF16), _sds((S, D_MODEL)), _sds((S, D_MODEL), BF16)],
    )(act, wo, x, gpost, g_next)
    return (xo, h_next), wo, (h, u, act, f)


def _ffn_bwd(tag, dxo, x, gpre, wt, wo, gpost, saved, on_dwo=None, on_dwt=None):
    h, u, act, f = saved
    S = x.shape[0]
    tm = min(512, S)

    def dact_body(f_ref, dxo_ref, g_ref, w_ref, u_ref, df_ref, du_ref, dg_ref):
        i = pl.program_id(0)
        df, dg = _rms_bwd(f_ref[...].astype(F32), g_ref[...], 0.5 * dxo_ref[...])
        df = df.astype(BF16)
        df_ref[...] = df

        @pl.when(i == 0)
        def _():
            dg_ref[...] = dg

        @pl.when(i > 0)
        def _():
            dg_ref[...] += dg

        for j in range(N_FF_BLK):
            dact = _dot(df, w_ref[FF_PAD * j:FF_PAD * (j + 1), :], _NT)
            ab = u_ref[:, 2 * FF_PAD * j:2 * FF_PAD * (j + 1)].astype(F32)
            a, b = ab[:, :FF_PAD], ab[:, FF_PAD:]
            sg = _sigmoid(a)
            du_ref[:, 2 * FF_PAD * j:2 * FF_PAD * j + FF_PAD] = (dact * b * (sg * (1.0 + a * (1.0 - sg)))).astype(BF16)
            du_ref[:, 2 * FF_PAD * j + FF_PAD:2 * FF_PAD * (j + 1)] = (dact * (a * sg)).astype(BF16)

    row = pl.BlockSpec((tm, D_MODEL), lambda i: (i, 0))
    vec = pl.BlockSpec((1, D_MODEL), lambda i: (0, 0))
    u_spec = pl.BlockSpec((tm, N_DEV * FF_PAD), lambda i: (i, 0))
    df, du, dgpost = _pcall(
        dact_body, name=tag + "_dact", grid=(S // tm,),
        in_specs=[row, row, vec, pl.BlockSpec(wo.shape, lambda i: (0, 0), pipeline_mode=pl.Buffered(1)), u_spec],
        out_specs=[row, u_spec, vec],
        out_shape=[_sds((S, D_MODEL), BF16), _sds((S, N_DEV * FF_PAD), BF16), _sds((1, D_MODEL))])(f, dxo, gpost, wo, u)
    dwo = _mm_tn(tag + "_dwo", act, df, ts=S, tn_a=FF_PAD)
    dwt = _mm_tn(tag + "_dwt", du, h, ts=S, tn_a=FF_PAD, deps=[on_dwo(dwo)] if on_dwo is not None else [])
    dx, dgpre = _dh_pre_norm_bwd(tag + "_dh", [du], wt, x, dxo, gpre, [on_dwt(dwt)] if on_dwt is not None else [])
    return dx, dgpre, dgpost


def _dh_pre_norm_bwd(name, dzs, wt, x, dres, g, deps):
    S = x.shape[0]
    ks = [dz.shape[1] for dz in dzs]

    def body(*refs):
        dz_refs = refs[:len(dzs)]
        w_ref, x_ref, dres_ref, g_ref = refs[len(dzs):len(dzs) + 4]
        dx_ref, dg_ref = refs[len(dzs) + 4 + len(deps):]
        i = pl.program_id(0)
        dh, o = None, 0
        for dz_ref, k in zip(dz_refs, ks):
            p = _dot(dz_ref[...], w_ref[o:o + k, :])
            dh, o = (p if dh is None else dh + p), o + k
        dx, dg = _rms_bwd(x_ref[...], g_ref[...], dh)
        dx_ref[...] = dx + dres_ref[...]

        @pl.when(i == 0)
        def _():
            dg_ref[...] = dg

        @pl.when(i > 0)
        def _():
            dg_ref[...] += dg

    th = min(512, S)
    row = pl.BlockSpec((th, D_MODEL), lambda i: (i, 0))
    vec = pl.BlockSpec((1, D_MODEL), lambda i: (0, 0))
    return _pcall(
        body, name=name, grid=(S // th,),
        in_specs=[pl.BlockSpec((th, k), lambda i: (i, 0)) for k in ks]
        + [pl.BlockSpec(wt.shape, lambda i: (0, 0), pipeline_mode=pl.Buffered(1)), row, row, vec]
        + [pl.BlockSpec(memory_space=pl.ANY)] * len(deps),
        out_specs=[row, vec], out_shape=[_sds((S, D_MODEL)), _sds((1, D_MODEL))])(*dzs, wt, x, dres, g, *deps)


GLA_G = 8


def _gla_tile_common(k, flow, wfu, bf):
    f = _dot(flow, wfu) + bf
    la = _log_sigmoid(f) * (1.0 / GATE_TEMP)
    tri = _tri_matrix(True)
    ws, ds = [], []
    for g in range(k.shape[0] // CHUNK):
        b = _tri_dot(tri, la[g * CHUNK:(g + 1) * CHUNK])
        b_end = b[CHUNK - 1:CHUNK, :]
        ws.append(jnp.exp(b_end - b))
        ds.append(jnp.exp(b_end))
    w = jnp.concatenate(ws, axis=0)
    return f, w, k * w, ds


def _tri_matrix(lower):
    r = lax.broadcasted_iota(jnp.int32, (CHUNK, CHUNK), 0)
    c = lax.broadcasted_iota(jnp.int32, (CHUNK, CHUNK), 1)
    return jnp.where((r >= c) if lower else (r <= c), 1.0, 0.0).astype(BF16)


def _heads():
    return [(slice(h * GLA_HDK, (h + 1) * GLA_HDK), slice(h * GLA_HDV, (h + 1) * GLA_HDV)) for h in range(GLA_HEADS)]


def _gla_fwd(proj, wfu, bf, gn, deps=()):
    S = proj.shape[0]
    G = min(GLA_G, S // CHUNK)
    T = G * CHUNK
    nc = S // CHUNK

    def body(q_ref, k_ref, v_ref, go_ref, flow_ref, wfu_ref, bf_ref, gn_ref, *rest):
        ya_ref, st_ref, state = rest[len(deps):]
        @pl.when(pl.program_id(0) == 0)
        def _():
            state[...] = jnp.zeros_like(state)

        _, _, kt, ds = _gla_tile_common(k_ref[...], flow_ref[...], wfu_ref[...], bf_ref[...])
        q = q_ref[...].astype(F32) * (GLA_HDK ** -0.5)
        v = v_ref[...]
        rows = [slice(g * CHUNK, (g + 1) * CHUNK) for g in range(G)]
        kv = [[_dot(v[r, vs], kt[r, ks], _TN) for ks, vs in _heads()] for r in rows]
        st = [state[vs, :] for _, vs in _heads()]
        o = []
        for g, r in enumerate(rows):
            outs = []
            for h, (ks, vs) in enumerate(_heads()):
                st[h] = st[h] * ds[g][:, ks] + kv[g][h]
                st_ref[g, vs, :] = st[h]
                outs.append(_dot(q[r, ks], st[h], _NT))
            o.append(jnp.concatenate(outs, axis=-1))
        for h, (_, vs) in enumerate(_heads()):
            state[vs, :] = st[h]
        ya_ref[...] = _head_rms_gate(jnp.concatenate(o, axis=0), go_ref[...], gn_ref[...]).astype(BF16)

    return _pcall(
        body, name="gla_fwd", grid=(S // T,),
        in_specs=[pl.BlockSpec((T, GLA_DK), lambda c: (c, 0)), pl.BlockSpec((T, GLA_DK), lambda c: (c, 1)),
                  pl.BlockSpec((T, GLA_DV), lambda c: (c, 1)), pl.BlockSpec((T, GLA_DV), lambda c: (c, 2)),
                  pl.BlockSpec((T, FLOW_W), lambda c: (c, (INT_W - FLOW_W) // FLOW_W)),
                  pl.BlockSpec(wfu.shape, lambda c: (0, 0)), pl.BlockSpec(bf.shape, lambda c: (0, 0)),
                  pl.BlockSpec(gn.shape, lambda c: (0, 0))]
        + [pl.BlockSpec(memory_space=pl.ANY)] * len(deps),
        out_specs=[pl.BlockSpec((T, GLA_DV), lambda c: (c, 0)),
                   pl.BlockSpec((G, GLA_DV, GLA_HDK), lambda c: (c, 0, 0))],
        out_shape=[_sds((S, GLA_DV), BF16), _sds((nc, GLA_DV, GLA_HDK))],
        scratch=[pltpu.VMEM((GLA_DV, GLA_HDK), F32)])(proj, proj, proj, proj, proj, wfu, bf, gn, *deps)


def _gla_bwd(proj, wfu, bf, gn, states, d_ya_in, dmain):
    S = proj.shape[0]
    G = min(GLA_G, S // CHUNK)
    T = G * CHUNK
    nt = S // T

    def body(q_ref, k_ref, v_ref, go_ref, flow_ref, wfu_ref, bf_ref, gn_ref, st_ref, stp_ref, dya_ref, dmain_in,
             dqkvg_ref, dflow_ref, dwfu_ref, dbf_ref, dgn_ref, dstate):
        dq_ref = dqkvg_ref.at[:, 0:GLA_DK]
        dk_ref = dqkvg_ref.at[:, GLA_DK:2 * GLA_DK]
        dv_ref = dqkvg_ref.at[:, 2 * GLA_DK:2 * GLA_DK + GLA_DV]
        dgo_ref = dqkvg_ref.at[:, 2 * GLA_DK + GLA_DV:]
        step = pl.program_id(0)

        @pl.when(step == 0)
        def _():
            dstate[...] = jnp.zeros_like(dstate)
            dwfu_ref[...] = jnp.zeros_like(dwfu_ref)
            dbf_ref[...] = jnp.zeros_like(dbf_ref)
            dgn_ref[...] = jnp.zeros_like(dgn_ref)

        flow, wfu_v = flow_ref[...], wfu_ref[...]
        f, w, kt, ds = _gla_tile_common(k_ref[...], flow, wfu_v, bf_ref[...])
        q = q_ref[...].astype(F32) * (GLA_HDK ** -0.5)
        v = v_ref[...]
        rows = [slice(g * CHUNK, (g + 1) * CHUNK) for g in range(G)]
        o = jnp.concatenate([jnp.concatenate([_dot(q[r, ks], st_ref[g, vs, :], _NT) for ks, vs in _heads()], axis=-1)
                             for g, r in enumerate(rows)], axis=0)
        _, vjp = jax.vjp(_head_rms_gate, o, go_ref[...], gn_ref[...])
        dout, dgo, dgn = vjp(dya_ref[...].astype(F32))
        dgo_ref[...] = dgo.astype(dgo_ref.dtype)
        dgn_ref[...] += dgn
        dq = [jnp.concatenate([_dot(dout[r, vs], st_ref[g, vs, :]) for _, vs in _heads()], axis=-1)
              for g, r in enumerate(rows)]
        qdo = [[_dot(dout[r, vs], q[r, ks], _TN) for ks, vs in _heads()] for r in rows]
        dq_ref[...] = (jnp.concatenate(dq, axis=0) * (GLA_HDK ** -0.5)).astype(dq_ref.dtype)
        has_prev = (step < nt - 1).astype(F32)
        carry = [dstate[vs, :] for _, vs in _heads()]
        dkt, dv, dd = [None] * G, [None] * G, [None] * G
        for g in reversed(range(G)):
            r = rows[g]
            dkts, dvs, dds = [], [], []
            for h, (ks, vs) in enumerate(_heads()):
                dst = carry[h] + qdo[g][h]
                dkts.append(_dot(v[r, vs], dst))
                dvs.append(_dot(kt[r, ks], dst, _NT))
                st_prev = st_ref[g - 1, vs, :] if g > 0 else stp_ref[vs, :] * has_prev
                dds.append(jnp.sum(dst * st_prev, axis=0, keepdims=True))
                carry[h] = dst * ds[g][:, ks]
            dkt[g], dv[g], dd[g] = (jnp.concatenate(t, axis=-1) for t in (dkts, dvs, dds))
        for h, (_, vs) in enumerate(_heads()):
            dstate[vs, :] = carry[h]
        dkt = jnp.concatenate(dkt, axis=0)
        dv_ref[...] = jnp.concatenate(dv, axis=0).astype(dv_ref.dtype)
        dk_ref[...] = (dkt * w).astype(dk_ref.dtype)
        de = dkt * kt
        tri = _tri_matrix(False)
        dla = []
        for g, r in enumerate(rows):
            db_end = jnp.sum(de[r], axis=0, keepdims=True) + dd[g] * ds[g]
            dla.append(db_end - _tri_dot(tri, de[r]))
        df = jnp.concatenate(dla, axis=0) * (1.0 - _sigmoid(f)) * (1.0 / GATE_TEMP)
        dflow_ref[...] = _dot(df, wfu_v, _NT).astype(dflow_ref.dtype)
        dwfu_ref[...] += _dot(flow, df, _TN)
        dbf_ref[...] += jnp.sum(df, axis=0, keepdims=True)

    rt = lambda s: nt - 1 - s
    return _pcall(
        body, name="gla_bwd", grid=(nt,),
        in_specs=[pl.BlockSpec((T, GLA_DK), lambda s: (rt(s), 0)), pl.BlockSpec((T, GLA_DK), lambda s: (rt(s), 1)),
                  pl.BlockSpec((T, GLA_DV), lambda s: (rt(s), 1)), pl.BlockSpec((T, GLA_DV), lambda s: (rt(s), 2)),
                  pl.BlockSpec((T, FLOW_W), lambda s: (rt(s), (INT_W - FLOW_W) // FLOW_W)),
                  pl.BlockSpec(wfu.shape, lambda s: (0, 0)), pl.BlockSpec(bf.shape, lambda s: (0, 0)),
                  pl.BlockSpec(gn.shape, lambda s: (0, 0)),
                  pl.BlockSpec((G, GLA_DV, GLA_HDK), lambda s: (rt(s), 0, 0)),
                  pl.BlockSpec((None, GLA_DV, GLA_HDK), lambda s: (jnp.maximum(rt(s) * G - 1, 0), 0, 0)),
                  pl.BlockSpec((T, GLA_DV), lambda s: (rt(s), 0)), pl.BlockSpec(memory_space=pl.ANY)],
        out_specs=[pl.BlockSpec((T, 2 * GLA_DK + 2 * GLA_DV), lambda s: (rt(s), 0)),
                   pl.BlockSpec((T, FLOW_W), lambda s: (rt(s), 0)),
                   pl.BlockSpec(wfu.shape, lambda s: (0, 0)), pl.BlockSpec(bf.shape, lambda s: (0, 0)),
                   pl.BlockSpec(gn.shape, lambda s: (0, 0))],
        out_shape=[_sds(dmain.shape, dmain.dtype), _sds((S, FLOW_W), BF16), _sds(wfu.shape), _sds(bf.shape),
                   _sds(gn.shape)],
        scratch=[pltpu.VMEM((GLA_DV, GLA_HDK), F32)], aliases={11: 0},
    )(proj, proj, proj, proj, proj, wfu, bf, gn, states, states, d_ya_in, dmain)


def _pool_counts(tm, i):
    t = (lax.broadcasted_iota(jnp.int32, (tm, POOL_GD), 0) + i * tm + 1).astype(F32)
    return [jnp.minimum(t, float(w)) for w in POOL_WINDOWS]


def _pool_fwd(proj, w_pool, pool_scale, tm=512):
    S = proj.shape[0]
    tm = min(tm, S // 2)
    col = _POOL_COL
    hb = tm // POOL_HALO

    def body(p_ref, halo_ref, wp_ref, sc_ref, mixed_ref, out_ref):
        i = pl.program_id(0)
        p = p_ref[...].astype(F32)
        halo = halo_ref[...].astype(F32) * (i > 0).astype(F32)
        ext = jnp.concatenate([halo, p], axis=0)
        n = tm + POOL_HALO
        sums, acc, k = {}, ext, 1
        while k < POOL_WINDOWS[-1]:
            acc = acc + pltpu.roll(acc, k, axis=0)
            k *= 2
            sums[k] = acc
        cnts = _pool_counts(tm, i)
        mixed, lin = [], []
        for g, w in enumerate(POOL_WINDOWS):
            ls = slice(g * POOL_GD, (g + 1) * POOL_GD)
            m = sums[w][POOL_HALO:n, ls] / cnts[g] - p[:, ls]
            mixed.append(m)
            lin.append(_dot(m, wp_ref[g]))
        mixed_ref[...] = jnp.concatenate(mixed, axis=-1)
        out_ref[...] = (jnp.concatenate(lin, axis=-1) * sc_ref[...]).astype(out_ref.dtype)

    return _pcall(
        body, name="pool_fwd", grid=(S // tm,),
        in_specs=[pl.BlockSpec((tm, POOL_W), lambda i: (i, col)),
                  pl.BlockSpec((POOL_HALO, POOL_W), lambda i: (jnp.maximum(i * hb - 1, 0), col)),
                  pl.BlockSpec(w_pool.shape, lambda i: (0, 0, 0)), pl.BlockSpec(pool_scale.shape, lambda i: (0, 0))],
        out_specs=[pl.BlockSpec((tm, POOL_W), lambda i: (i, 0)), pl.BlockSpec((tm, POOL_W), lambda i: (i, 0))],
        out_shape=[_sds((S, POOL_W)), _sds((S, POOL_W), BF16)])(proj, proj, w_pool, pool_scale)


def _pool_lin_bwd(dout, mixed, w_pool, pool_scale):
    S = dout.shape[0]

    def fn(dout, mixed, wp, sc):
        dlin = dout * sc
        dm, dwp, lin = [], [], []
        for g in range(len(POOL_WINDOWS)):
            ls = slice(g * POOL_GD, (g + 1) * POOL_GD)
            lin.append(_dot(mixed[:, ls], wp[g]))
            dm.append(_dot(dlin[:, ls], wp[g], _NT))
            dwp.append(_dot(mixed[:, ls], dlin[:, ls], _TN))
        dsc = jnp.sum(dout * jnp.concatenate(lin, axis=-1), axis=0, keepdims=True)
        return jnp.concatenate(dm, axis=-1), jnp.concatenate(dwp, axis=0), dsc

    return _rowwise("pool_lin_bwd", fn, [dout, mixed], [w_pool, pool_scale], [_sds((S, POOL_W))],
                    [_sds((len(POOL_WINDOWS) * POOL_GD, POOL_GD)), _sds((1, POOL_W))])


def _pool_win_bwd(dmixed, dmain, tm=512):
    S = dmixed.shape[0]
    tm = min(tm, S // 2)
    nt = S // tm
    hb = tm // POOL_HALO

    def body(dm_ref, halo_ref, dmain_in, dp_ref):
        i = pl.program_id(0)
        dm = dm_ref[...]
        halo = halo_ref[...] * (i < nt - 1).astype(F32)
        cnts = _pool_counts(tm, i)
        cnts_h = [c[:POOL_HALO] for c in _pool_counts(tm, i + 1)]
        r = jnp.concatenate([jnp.concatenate([dm[:, g * POOL_GD:(g + 1) * POOL_GD] / cnts[g] for g in range(4)], axis=-1),
                             jnp.concatenate([halo[:, g * POOL_GD:(g + 1) * POOL_GD] / cnts_h[g] for g in range(4)], axis=-1)],
                            axis=0)
        n = tm + POOL_HALO
        sums, acc, k = {}, r, 1
        while k < POOL_WINDOWS[-1]:
            acc = acc + pltpu.roll(acc, n - k, axis=0)
            k *= 2
            sums[k] = acc
        dp = [sums[w][:tm, g * POOL_GD:(g + 1) * POOL_GD] for g, w in enumerate(POOL_WINDOWS)]
        dp_ref[...] = (jnp.concatenate(dp, axis=-1) - dm).astype(dp_ref.dtype)

    return _pcall(
        body, name="pool_win_bwd", grid=(nt,),
        in_specs=[pl.BlockSpec((tm, POOL_W), lambda i: (i, 0)),
                  pl.BlockSpec((POOL_HALO, POOL_W), lambda i: (jnp.minimum((i + 1) * hb, S // POOL_HALO - 1), 0)),
                  pl.BlockSpec(memory_space=pl.ANY)],
        out_specs=pl.BlockSpec((tm, POOL_W), lambda i: (i, _POOL_COL)),
        out_shape=_sds(dmain.shape, dmain.dtype), aliases={2: 0})(dmixed, dmixed, dmain)


def _xattn_probs(q, kv, h):
    hs = slice(h * XA_HD, (h + 1) * XA_HD)
    s = _dot(q[:, hs], kv[:, hs], _NT) * (XA_HD ** -0.5)
    s = s - jnp.max(s, axis=-1, keepdims=True)
    e = jnp.exp(s)
    return e / jnp.sum(e, axis=-1, keepdims=True)


def _xattn_fwd(proj, kv):
    S = proj.shape[0]

    def fn(q, kv):
        outs = []
        for h in range(XA_HEADS):
            p = _xattn_probs(q, kv, h)
            outs.append(_dot(p, kv[:, XA_W + h * XA_HD:XA_W + (h + 1) * XA_HD]))
        return jnp.concatenate(outs, axis=-1)

    return _rowwise("xattn_fwd", fn, [_Win(proj, XA_W, _XQ_COL)], [kv], [_sds((S, XA_W), BF16)])[0]


def _xattn_bwd(proj, kv, dxa, dmain):
    def fn(q, dxa, kv):
        dqs, dks, dvs = [], [], []
        for h in range(XA_HEADS):
            hs = slice(h * XA_HD, (h + 1) * XA_HD)
            vh = kv[:, XA_W + h * XA_HD:XA_W + (h + 1) * XA_HD]
            p = _xattn_probs(q, kv, h)
            dp = _dot(dxa[:, hs], vh, _NT)
            ds = p * (dp - jnp.sum(p * dp, axis=-1, keepdims=True)) * (XA_HD ** -0.5)
            dqs.append(_dot(ds, kv[:, hs]))
            dks.append(_dot(ds, q[:, hs], _TN))
            dvs.append(_dot(p, dxa[:, hs], _TN))
        return jnp.concatenate(dqs, axis=-1), jnp.concatenate(dks + dvs, axis=-1)

    return _rowwise("xattn_bwd", fn, [_Win(proj, XA_W, _XQ_COL), dxa], [kv], [_Win(dmain, XA_W, _XQ_COL)],
                    [_sds(kv.shape)])


def _mix_fwd(x1, h, mem, W, g_next, on_proj=None, late_weights=None):
    S = x1.shape[0]
    M = mem.shape[0]
    tm = min(512, S)
    def proj_body(h_ref, w_ref, o_ref):
        hh = h_ref[...]
        for j in range(INT_NBLK):
            o_ref[:, INT_BLK * j:INT_BLK * (j + 1)] = _dot(hh, w_ref[INT_BLK * j:INT_BLK * (j + 1), :], _NT).astype(BF16)

    proj = _pcall(
        proj_body, name="mix_proj", grid=(S // tm,),
        in_specs=[pl.BlockSpec((tm, D_MODEL), lambda i: (i, 0)),
                  pl.BlockSpec((INT_W, D_MODEL), lambda i: (0, 0), pipeline_mode=pl.Buffered(1))],
        out_specs=pl.BlockSpec((tm, INT_W), lambda i: (i, 0)), out_shape=_sds((S, INT_W), BF16))(h, W["w_int_t"])
    ya_in, states = _gla_fwd(proj, W["w_fu_pad"], W["b_f"], W["gla_norm_g"],
                             on_proj(proj) if on_proj is not None else [])
    mixed, pool_out = _pool_fwd(proj, W["w_pool"], W["pool_scale"])
    mem_n = _rowwise("mem_norm", lambda m, g: _rms(m, g), [mem], [W["mem_norm_g"]], [_sds((M, D_MODEL), BF16)])[0]
    if late_weights is not None:
        W = {**W, **late_weights([pool_out])}
    kv = _mm_nn("mem_kv", mem_n, W["w_mem_kv"])
    xa = _xattn_fwd(proj, kv)
    def out_fn(gates, ya_in, pool_out, xa, x1, wa, wb, wc, wo, g, gn):
        ya, yb, yc = (_dot(t, w).astype(BF16) for t, w in ((ya_in, wa), (pool_out, wb), (xa, wc)))
        merged = _merge(gates, ya, yb, yc).astype(BF16)
        y = _dot(merged, wo)
        x2 = x1 + _rms(y, g)
        return ya, yb, yc, merged, y, x2, _rms(x2, gn)

    ya, yb, yc, merged, y, x2, h_next = _rowwise(
        "mix_out", out_fn, [_Win(proj, 3 * D_MODEL, 1), ya_in, pool_out, xa, x1],
        [W["w_up_gla"], W["w_up_pool"], W["w_up_xattn"], W["w_o"], W["mix_post_g"], g_next],
        [_sds((S, D_MODEL), BF16)] * 5 + [_sds((S, D_MODEL)), _sds((S, D_MODEL), BF16)], tm=256)
    return x2, h_next, W, (h, proj, states, ya_in, mixed, pool_out, mem_n, kv, xa, ya, yb, yc, merged, y)


def _mix_bwd(dx2, x1, mem, W, saved, on_grads=None):
    h, proj, states, ya_in, mixed, pool_out, mem_n, kv, xa, ya, yb, yc, merged, y = saved
    S = x1.shape[0]

    def out_bwd(y, dx2, gates, ya, yb, yc, g, wo, wa, wb, wc):
        dy, dg = _rms_bwd(y.astype(F32), g, dx2)
        dy = dy.astype(BF16)
        _, vjp = jax.vjp(_merge, gates, ya, yb, yc)
        dgates, dya, dyb, dyc = vjp(_dot(dy, wo, _NT))
        return (dy, dgates, dya, dyb, dyc, _dot(dya, wa, _NT), _dot(dyb, wb, _NT), _dot(dyc, wc, _NT), dg)

    dmain = lax.empty((S, INT_MAIN), BF16)
    dy, dmain, dya, dyb, dyc, d_ya_in, d_pool_out, d_xa, d_mix_post_g = _rowwise(
        "mix_out_bwd", out_bwd, [y, dx2, _Win(proj, 3 * D_MODEL, 1), ya, yb, yc],
        [W["mix_post_g"], W["w_o"], W["w_up_gla"], W["w_up_pool"], W["w_up_xattn"]],
        [_sds((S, D_MODEL), BF16), _Win(dmain, 3 * D_MODEL, 1)] + [_sds((S, D_MODEL), BF16)] * 4
        + [_sds((S, POOL_W), BF16), _sds((S, XA_W), BF16)], [_sds((1, D_MODEL))], tm=256)
    d_w_o = _mm_tn("d_w_o", merged, dy, ts=S, tn_a=256)
    d_w_up_gla = _mm_tn("d_w_up_gla", ya_in, dya, ts=S, tn_a=256)
    d_w_up_pool = _mm_tn("d_w_up_pool", pool_out, dyb, ts=S, tn_a=128)
    d_w_up_xattn = _mm_tn("d_w_up_xattn", xa, dyc, ts=S, tn_a=128)

    dmain, dflow, d_wfu_pad, d_b_f, d_gla_norm_g = _gla_bwd(proj, W["w_fu_pad"], W["b_f"], W["gla_norm_g"], states,
                                                            d_ya_in, dmain)
    dmixed, d_w_pool, d_pool_scale = _pool_lin_bwd(d_pool_out, mixed, W["w_pool"], W["pool_scale"])
    dmain = _pool_win_bwd(dmixed, dmain)
    dmain, dkv = _xattn_bwd(proj, kv, d_xa, dmain)
    d_w_mem_kv = _mm_tn("d_w_mem_kv", mem_n, dkv)
    dmem_n = _mm_nt("d_mem_n", dkv, W["w_mem_kv"])
    d_mem_norm_g = _rowwise("mem_norm_bwd", lambda m, d, g: _rms_bwd(m, g, d)[1], [mem, dmem_n], [W["mem_norm_g"]], [],
                            [_sds((1, D_MODEL))])[0]
    d_w_int_t = (_mm_tn("d_w_int", dmain, h, ts=S, tn_a=INT_MAIN // 8), _mm_tn("d_w_flow", dflow, h, ts=S))
    grads = dict(
        w_int_t=d_w_int_t, w_fu_pad=d_wfu_pad, b_f=d_b_f, gla_norm_g=d_gla_norm_g, w_pool=d_w_pool,
        pool_scale=d_pool_scale, mem_norm_g=d_mem_norm_g, w_mem_kv=d_w_mem_kv, w_up_gla=d_w_up_gla,
        w_up_pool=d_w_up_pool, w_up_xattn=d_w_up_xattn, w_o=d_w_o, mix_post_g=d_mix_post_g)
    deps = [on_grads(grads)] if on_grads is not None else []
    dx1, grads["mix_pre_g"] = _dh_pre_norm_bwd("d_mix_h", [dmain, dflow], W["w_int_t"], x1, dx2, W["mix_pre_g"], deps)
    return dx1, grads


def _mesh_pos():
    x, y, c = lax.axis_index("x"), lax.axis_index("y"), lax.axis_index("c")
    return x, y, c, 4 * x + 2 * y + c


def _peer(x, y, c, r):
    px = 1 - x if r & 4 else x
    py = 1 - y if r & 2 else y
    pc = 1 - c if r & 1 else c
    return (px, py, pc), 4 * px + 2 * py + pc


_ALL_PEERS = tuple(range(1, N_DEV))
_SIBLING = 1
_SAME_CORE = (2, 4, 6)


def _dev_slot(ref, dev):
    return ref.at[dev]


def _zero_pad_rows(land):
    pad = FF_PAD - FF_BLK

    def body(land_in, o_ref):
        o_ref[...] = jnp.zeros_like(o_ref)

    return _pcall(body, name="zero_pad_rows", grid=(land.shape[0],), in_specs=[pl.BlockSpec(memory_space=pl.ANY)],
                  out_specs=pl.BlockSpec((None, pad, D_MODEL), lambda j: (j, FF_BLK // pad, 0)),
                  out_shape=_sds(land.shape, land.dtype), aliases={0: 0})(land)


class _Plan:
    def __init__(self, scatter, slots=None, shapes=None):
        self.scatter, self.slots, self.shapes = scatter, slots or {}, shapes or {}

    def src(self, srcs, a, dev):
        return self.slots.get(a, _dev_slot)(srcs[a], dev) if self.scatter else srcs[a]

    def dst(self, lands, a, dev):
        return lands[a].at[dev] if self.scatter else self.slots.get(a, _dev_slot)(lands[a], dev)

    def landing_zones(self, arrays):
        lands = []
        for a, arr in enumerate(arrays):
            if self.scatter:
                lands.append(lax.empty((N_DEV,) + tuple(self.shapes.get(a, arr.shape[1:])), arr.dtype))
            elif a in self.shapes:
                lands.append(_zero_pad_rows(lax.empty(self.shapes[a], arr.dtype)))
            else:
                lands.append(lax.empty((N_DEV,) + arr.shape, arr.dtype))
        return lands


_GATHER = _Plan(False)


def _peer_copies(srcs, lands, send_sems, recv_sems, plan, peers=_ALL_PEERS):
    x, y, c, me = _mesh_pos()
    cps = []
    for r in peers:
        pos, peer = _peer(x, y, c, r)
        for a in range(len(srcs)):
            k = a * (N_DEV - 1) + r - 1
            cps.append(pltpu.make_async_remote_copy(
                src_ref=plan.src(srcs, a, peer), dst_ref=plan.dst(lands, a, me),
                send_sem=send_sems.at[k], recv_sem=recv_sems.at[k], device_id=pos,
                device_id_type=pl.DeviceIdType.MESH))
    return cps


_HBM = pl.BlockSpec(memory_space=pltpu.HBM)
_SEM = pl.BlockSpec(memory_space=pltpu.SEMAPHORE)
_EFFECT = pltpu.SideEffectType.DATAFLOW_SIDE_EFFECTING


def _own_copies(srcs, lands, own_sems, plan):
    me = _mesh_pos()[3]
    return [pltpu.make_async_copy(plan.src(srcs, a, me), plan.dst(lands, a, me), own_sems.at[a])
            for a in range(len(srcs))]


def _exchange_start(name, arrays, plan):
    n = len(arrays)
    lands = plan.landing_zones(arrays)
    n_sem = n * (N_DEV - 1)

    def body(*refs):
        srcs, lands_ = refs[:n], refs[n:2 * n]
        send_sems, recv_sems, own_sems = refs[2 * n:2 * n + 3]
        token = refs[-1]
        for cp in _peer_copies(srcs, lands_, send_sems, recv_sems, plan) + _own_copies(srcs, lands_, own_sems, plan):
            cp.start()
        token[...] = jnp.zeros_like(token)

    hbm = lambda a: pltpu.HBM(a.shape, a.dtype)
    res = pl.pallas_call(
        body, name=name,
        out_shape=(pltpu.SemaphoreType.DMA((n_sem,)), pltpu.SemaphoreType.DMA((n_sem,)), pltpu.SemaphoreType.DMA((n,)),
                   *[hbm(a) for a in arrays], *[hbm(a) for a in lands], _sds((8, 128))),
        in_specs=[_HBM] * (2 * n),
        out_specs=(_SEM, _SEM, _SEM, *[_HBM] * (2 * n), pl.BlockSpec(memory_space=pltpu.VMEM)),
        input_output_aliases={i: 3 + i for i in range(2 * n)},
        compiler_params=pltpu.CompilerParams(has_side_effects=_EFFECT),
    )(*[pltpu.with_memory_space_constraint(a, pltpu.HBM) for a in list(arrays) + lands])
    return (res[:3], res[3:3 + n], res[3 + n:3 + 2 * n], plan), res[-1]


def _exchange_wait(name, state, after):
    sems, srcs, lands, plan = state
    n = len(srcs)

    def body(*refs):
        srcs_, lands_ = refs[:n], refs[n:2 * n]
        send_sems, recv_sems, own_sems = refs[2 * n:2 * n + 3]
        for cp in _peer_copies(srcs_, lands_, send_sems, recv_sems, plan):
            cp.wait_send()
            cp.wait_recv()
        for cp in _own_copies(srcs_, lands_, own_sems, plan):
            cp.wait()

    hbm = lambda a: pltpu.HBM(a.shape, a.dtype)
    res = pl.pallas_call(
        body, name=name, out_shape=tuple(hbm(a) for a in list(srcs) + list(lands)),
        in_specs=[_HBM] * (2 * n) + [_SEM] * 3 + [pl.BlockSpec(memory_space=pl.ANY)] * len(after),
        out_specs=tuple([_HBM] * (2 * n)), input_output_aliases={i: i for i in range(2 * n)},
        compiler_params=pltpu.CompilerParams(has_side_effects=_EFFECT),
    )(*srcs, *lands, *sems, *after)
    return res[n:]


def _gather_start(name, arrays, plan, after):
    n = len(arrays)
    lands = plan.landing_zones(arrays)
    n_sem = n * (N_DEV - 1)

    def body(*refs):
        srcs, lands_ = refs[:n], refs[n:2 * n]
        send_sems, recv_sems, own_sems = refs[2 * n + len(after):2 * n + len(after) + 3]
        token = refs[-1]
        for cp in (_peer_copies(srcs, lands_, send_sems, recv_sems, plan, (_SIBLING,) + _SAME_CORE)
                   + _own_copies(srcs, lands_, own_sems, plan)):
            cp.start()
        token[...] = jnp.zeros_like(token)

    hbm = lambda a: pltpu.HBM(a.shape, a.dtype)
    res = pl.pallas_call(
        body, name=name,
        out_shape=(pltpu.SemaphoreType.DMA((n_sem,)), pltpu.SemaphoreType.DMA((n_sem,)), pltpu.SemaphoreType.DMA((n,)),
                   *[hbm(a) for a in arrays], *[hbm(a) for a in lands], _sds((8, 128))),
        in_specs=[_HBM] * (2 * n) + [pl.BlockSpec(memory_space=pl.ANY)] * len(after),
        out_specs=(_SEM, _SEM, _SEM, *[_HBM] * (2 * n), pl.BlockSpec(memory_space=pltpu.VMEM)),
        input_output_aliases={i: 3 + i for i in range(2 * n)},
        compiler_params=pltpu.CompilerParams(has_side_effects=_EFFECT),
    )(*[pltpu.with_memory_space_constraint(a, pltpu.HBM) for a in list(arrays) + lands], *after)
    return (res[:3], res[3:3 + n], res[3 + n:3 + 2 * n], plan), res[-1]


def _pass_on_copies(lands, send_sems, recv_sems, plan):
    x, y, c, _ = _mesh_pos()
    sibling = _peer(x, y, c, _SIBLING)[0]
    cps = []
    for i, r in enumerate(_SAME_CORE):
        owner = _peer(x, y, c, r)[1]
        for a in range(len(lands)):
            k = a * len(_SAME_CORE) + i
            cps.append(pltpu.make_async_remote_copy(
                src_ref=plan.dst(lands, a, owner), dst_ref=plan.dst(lands, a, owner), send_sem=send_sems.at[k],
                recv_sem=recv_sems.at[k], device_id=sibling, device_id_type=pl.DeviceIdType.MESH))
    return cps


def _gather_pass_on(name, state, after):
    sems, srcs, lands, plan = state
    n = len(srcs)
    n_sem = n * len(_SAME_CORE)

    def body(*refs):
        srcs_, lands_ = refs[:n], refs[n:2 * n]
        send_sems, recv_sems = refs[2 * n], refs[2 * n + 1]
        on_send, on_recv = refs[2 * n + 3 + len(after)], refs[2 * n + 4 + len(after)]
        token = refs[-1]
        arrivals = _peer_copies(srcs_, lands_, send_sems, recv_sems, plan, _SAME_CORE)
        for arrived, on in zip(arrivals, _pass_on_copies(lands_, on_send, on_recv, plan)):
            arrived.wait_recv()
            on.start()
        token[...] = jnp.zeros_like(token)

    hbm = lambda a: pltpu.HBM(a.shape, a.dtype)
    res = pl.pallas_call(
        body, name=name,
        out_shape=(pltpu.SemaphoreType.DMA((n_sem,)), pltpu.SemaphoreType.DMA((n_sem,)),
                   *[hbm(a) for a in list(srcs) + list(lands)], _sds((8, 128))),
        in_specs=[_HBM] * (2 * n) + [_SEM] * 3 + [pl.BlockSpec(memory_space=pl.ANY)] * len(after),
        out_specs=(_SEM, _SEM, *[_HBM] * (2 * n), pl.BlockSpec(memory_space=pltpu.VMEM)),
        input_output_aliases={i: 2 + i for i in range(2 * n)},
        compiler_params=pltpu.CompilerParams(has_side_effects=_EFFECT),
    )(*srcs, *lands, *sems, *after)
    return (sems, res[:2], res[2:2 + n], res[2 + n:2 + 2 * n], plan), res[-1]


def _gather_wait(name, state, after):
    sems, on_sems, srcs, lands, plan = state
    n = len(srcs)

    def body(*refs):
        srcs_, lands_ = refs[:n], refs[n:2 * n]
        send_sems, recv_sems, own_sems, on_send, on_recv = refs[2 * n:2 * n + 5]
        for cp in _peer_copies(srcs_, lands_, send_sems, recv_sems, plan, (_SIBLING,)):
            cp.wait_recv()
        for cp in _peer_copies(srcs_, lands_, send_sems, recv_sems, plan, (_SIBLING,) + _SAME_CORE):
            cp.wait_send()
        for cp in _own_copies(srcs_, lands_, own_sems, plan):
            cp.wait()
        for cp in _pass_on_copies(lands_, on_send, on_recv, plan):
            cp.wait_send()
            cp.wait_recv()

    hbm = lambda a: pltpu.HBM(a.shape, a.dtype)
    res = pl.pallas_call(
        body, name=name, out_shape=tuple(hbm(a) for a in list(srcs) + list(lands)),
        in_specs=[_HBM] * (2 * n) + [_SEM] * 5 + [pl.BlockSpec(memory_space=pl.ANY)] * len(after),
        out_specs=tuple([_HBM] * (2 * n)), input_output_aliases={i: i for i in range(2 * n)},
        compiler_params=pltpu.CompilerParams(has_side_effects=_EFFECT),
    )(*srcs, *lands, *sems, *on_sems, *after)
    return res[n:]


def _sum_parts(name, recv):
    def body(r_ref, o_ref):
        s = r_ref[0]
        for j in range(1, N_DEV):
            s = s + r_ref[j]
        o_ref[...] = s

    return pl.pallas_call(body, name=name, out_shape=_sds(recv.shape[1:], recv.dtype))(recv)


def _adam(name, recv, w, m, v, col=0):
    shape = w.shape
    R, C = shape
    tr = R
    while N_DEV * tr * C * 4 > 6 * 1024 * 1024 and tr % 32 == 0:
        tr //= 2

    def body(recv_ref, w_ref, m_ref, v_ref, g_ref, d_ref, m2_ref, v2_ref):
        g = recv_ref[0].astype(F32)
        for j in range(1, N_DEV):
            g = g + recv_ref[j].astype(F32)
        w_, m_, v_ = w_ref[...], m_ref[...], v_ref[...]
        m2 = ADAM_B1 * m_ + (1.0 - ADAM_B1) * g
        v2 = ADAM_B2 * v_ + (1.0 - ADAM_B2) * (g * g)
        m_hat = m2 / (1.0 - ADAM_B1 ** ADAM_STEP)
        v_hat = v2 / (1.0 - ADAM_B2 ** ADAM_STEP)
        g_ref[...] = g
        d_ref[...] = -ADAM_LR * (m_hat / (jnp.sqrt(v_hat) + ADAM_EPS) + ADAM_WD * w_)
        m2_ref[...] = m2
        v2_ref[...] = v2

    blk = pl.BlockSpec((tr, C), lambda i: (i, 0))
    return _pcall(body, name=name, grid=(R // tr,),
                  in_specs=[pl.BlockSpec((N_DEV, tr, C), lambda i: (0, i, col)), blk, blk, blk],
                  out_specs=[blk] * 4, out_shape=[_sds(shape)] * 4)(recv, w, m, v)


_NAMES = ['ffn1_pre_g', 'ffn1_w_in', 'ffn1_w_out', 'ffn1_post_g', 'mix_pre_g', 'w_in', 'w_fu', 'b_f', 'gla_norm_g',
          'w_pool', 'pool_scale', 'mem_norm_g', 'w_mem_kv', 'w_up_gla', 'w_up_pool', 'w_up_xattn', 'w_o', 'mix_post_g',
          'ffn2_pre_g', 'ffn2_w_in', 'ffn2_w_out', 'ffn2_post_g', 'final_g']
_SHARDED = ['ffn1_w_in', 'ffn1_w_out', 'w_in', 'w_fu', 'w_mem_kv', 'w_up_gla', 'w_up_pool', 'w_up_xattn', 'w_o',
            'ffn2_w_in', 'ffn2_w_out']
_COL_SHARDED = ['w_up_pool', 'w_up_xattn']


def _cols_to_full(g):
    return jnp.transpose(g, (1, 0, 2)).reshape(g.shape[1], N_DEV * g.shape[2])


def _full_to_cols(f):
    R, C = f.shape
    return jnp.transpose(f.reshape(R, N_DEV, C // N_DEV), (1, 0, 2))


def _to_internal(w_in_t):
    o = 0
    parts = []
    for s in IN_SPLITS:
        parts.append(w_in_t[o:o + s])
        o += s
    q, k, v, g_out, f_low, p_in, xq, gates = parts
    f_low = jnp.pad(f_low, ((0, FLOW_W - GATE_RANK), (0, 0)))
    return jnp.concatenate([q, k, v, g_out, gates, p_in, xq, f_low], axis=0)


def _from_internal(d, d_flow):
    q, k, v, g_out = d[0:512], d[512:1024], d[1024:2048], d[2048:3072]
    gates, p_in, xq = d[3072:6144], d[6144:6656], d[6656:7168]
    return jnp.concatenate([q, k, v, g_out, d_flow[:GATE_RANK], p_in, xq, gates], axis=0)


_FFN_IN = ('ffn1_w_in', 'ffn2_w_in')
_FFN_OUT = ('ffn1_w_out', 'ffn2_w_out')
_GATHERS = {"ffn1_in": ['ffn1_w_in'], "ffn1_out": ['ffn1_w_out'],
            "mix_in": ['w_in', 'w_fu'], "mix_rest": ['w_mem_kv', 'w_up_gla', 'w_up_pool', 'w_up_xattn', 'w_o'],
            "ffn2": ['ffn2_w_in', 'ffn2_w_out']}
_MIX = _GATHERS["mix_in"] + _GATHERS["mix_rest"]


def _ffn_in_slot(ref, d):
    return ref.at[2 * (d % N_FF_BLK) + d // N_FF_BLK, pl.ds(0, FF_BLK)]


def _ffn_out_slot(ref, d):
    rows = FF_BLK // 2
    return ref.at[d // 2, pl.ds(pl.multiple_of((d % 2) * rows, rows), rows)]


def _ffn_plan(names, scatter):
    slots, shapes = {}, {}
    for a, n in enumerate(names):
        if n in _FFN_IN:
            slots[a] = _ffn_in_slot
            shapes[a] = (FF_BLK, D_MODEL) if scatter else (N_DEV, FF_PAD, D_MODEL)
        elif n in _FFN_OUT:
            slots[a] = _ffn_out_slot
            shapes[a] = (FF_BLK // 2, D_MODEL) if scatter else (N_FF_BLK, FF_PAD, D_MODEL)
    return _Plan(scatter, slots, shapes)


def _rows_to_full(g):
    return g.reshape(N_DEV * g.shape[1], g.shape[2])


def _mix_weights(gathered):
    W = {}
    for n, g in gathered.items():
        if n == "w_in":
            W["w_int_t"] = _to_internal(g.reshape(IN_WIDTH, D_MODEL))
        elif n == "w_fu":
            W["w_fu_pad"] = jnp.pad(_cols_to_full(g), ((0, FLOW_W - GATE_RANK), (0, 0)))
        else:
            W[n] = _cols_to_full(g) if n in _COL_SHARDED else _rows_to_full(g)
    return W


def _mix_chunks(G, n):
    if n == "w_in":
        return _from_internal(*G["w_int_t"]).reshape(N_DEV, IN_SHARD, D_MODEL)
    if n == "w_fu":
        return _full_to_cols(G["w_fu_pad"][:GATE_RANK].astype(BF16))
    if n in _COL_SHARDED:
        return _full_to_cols(G[n])
    return G[n].reshape(N_DEV, G[n].shape[0] // N_DEV, G[n].shape[1])


def _step(x, mem, tgt, P, Mo, Vo):
    def native(n, a):
        return jnp.swapaxes(a, 0, 1) if n in _FFN_IN + ("w_in",) else a

    P, Mo, Vo = ({n: native(n, a) for n, a in d.items()} for d in (P, Mo, Vo))
    small = {n: P[n] for n in _NAMES if n not in _SHARDED}

    gather, tokens = {}, []
    for grp, names in _GATHERS.items():
        gather[grp], tok = _gather_start("gather_" + grp, [P[n].astype(BF16) for n in names], _ffn_plan(names, False),
                                         tokens[-1:])
        tokens.append(tok)

    def pass_on(grp, after):
        return _gather_pass_on("gather_" + grp + "_on", gather[grp], after)

    def gathered(grp, after):
        return _gather_wait("gather_" + grp + "_wait", pass_on(grp, after)[0], [])

    def ffn_wo(lands):
        return lambda act: lands[0].reshape(N_FF_BLK * FF_PAD, D_MODEL)

    h1 = _pre_norm("ffn1_pre", x, small["ffn1_pre_g"], tokens[:1])
    w1t = gathered("ffn1_in", tokens[-1:] + [h1])[0].reshape(N_DEV * FF_PAD, D_MODEL)
    (x1, hm), w1o, sv1 = _ffn_fwd("ffn1", x, h1, w1t, small["ffn1_post_g"],
                                  lambda act: ffn_wo(gathered("ffn1_out", [act]))(act), small["mix_pre_g"])

    Wm = {**small, **_mix_weights(dict(zip(_GATHERS["mix_in"], gathered("mix_in", [x1]))))}
    passed = {}

    def pass_on_later(proj):
        tokens = []
        for grp in ("mix_rest", "ffn2"):
            passed[grp], tok = pass_on(grp, [proj])
            tokens.append(tok)
        return tokens

    def mix_rest(after):
        names = _GATHERS["mix_rest"]
        return _mix_weights(dict(zip(names, _gather_wait("gather_mix_rest_wait", passed["mix_rest"], after))))

    x2, h2, Wm, svm = _mix_fwd(x1, hm, mem, Wm, small["ffn2_pre_g"], pass_on_later, mix_rest)
    w2t, w2o = _gather_wait("gather_ffn2_wait", passed["ffn2"], [x2])
    w2t = w2t.reshape(N_DEV * FF_PAD, D_MODEL)
    (loss, dx3, d_final_g), w2o, sv2 = _ffn_fwd("ffn2", x2, h2, w2t, small["ffn2_post_g"], ffn_wo([w2o]),
                                                small["final_g"], tgt)

    G = dict(final_g=d_final_g)
    scat = {}

    def start(grp, names, arrays):
        scat[grp] = names, _exchange_start("scatter_" + grp, arrays, _ffn_plan(names, True))
        return scat[grp][1][1]

    def ffn_starts(tag):
        return (lambda dwo: start(tag + "_out", [tag + "_w_out"], [dwo.reshape(N_FF_BLK, FF_PAD, D_MODEL)]),
                lambda dwt: start(tag + "_in", [tag + "_w_in"], [dwt.reshape(N_DEV, FF_PAD, D_MODEL)]))

    dx2, G["ffn2_pre_g"], G["ffn2_post_g"] = _ffn_bwd(
        "ffn2", dx3, x2, small["ffn2_pre_g"], w2t, w2o, small["ffn2_post_g"], sv2, *ffn_starts("ffn2"))
    dx1, Gm = _mix_bwd(dx2, x1, mem, Wm, svm,
                       lambda Gm: start("mix", _MIX, [_mix_chunks(Gm, n) for n in _MIX]))
    G.update(Gm)
    dx, G["ffn1_pre_g"], G["ffn1_post_g"] = _ffn_bwd(
        "ffn1", dx1, x, small["ffn1_pre_g"], w1t, w1o, small["ffn1_post_g"], sv1, *ffn_starts("ffn1"))

    vecs = sorted((n for n in small if P[n].ndim == 2), key=lambda n: -P[n].shape[1])
    packed = jnp.concatenate([G[n] for n in vecs] + [loss], axis=1)
    place, o = {}, 0
    for n in vecs:
        place[n] = o // P[n].shape[1]
        o += P[n].shape[1]
    scat["small"] = ["packed", "w_pool"], _exchange_start("gather_small_grads", [packed, G["w_pool"]], _GATHER)
    recv, outs = {}, {}
    done = [dx]
    for grp in ["ffn2_out", "ffn2_in", "mix", "ffn1_out", "ffn1_in", "small"]:
        names, (state, _) = scat[grp]
        recv.update(zip(names, _exchange_wait(("gather_" if grp == "small" else "scatter_") + grp + "_wait", state,
                                              done)))
        if grp == "small":
            loss = _sum_parts("loss_sum", recv["packed"][:, :, o:])[0, 0]
            names = list(small)
        for n in names:
            shp = P[n].shape
            shp2 = shp if len(shp) == 2 else (shp[0] * shp[1], shp[2])
            r = recv["packed"] if n in place else recv[n].reshape((N_DEV,) + shp2)
            res = _adam("adam_" + n, r, P[n].reshape(shp2), Mo[n].reshape(shp2), Vo[n].reshape(shp2), place.get(n, 0))
            outs[n] = [native(n, t.reshape(shp)) for t in res]
            done.append(res[-1])
    return loss, dx, outs


def kernel(x, mem, ffn1_pre_g, ffn1_w_in, ffn1_w_out, ffn1_post_g, mix_pre_g, w_in, w_fu, b_f, gla_norm_g, w_pool, pool_scale, mem_norm_g, w_mem_kv, w_up_gla, w_up_pool, w_up_xattn, w_o, mix_post_g, ffn2_pre_g, ffn2_w_in, ffn2_w_out, ffn2_post_g, final_g, loss_target, m_ffn1_pre_g, m_ffn1_w_in, m_ffn1_w_out, m_ffn1_post_g, m_mix_pre_g, m_w_in, m_w_fu, m_b_f, m_gla_norm_g, m_w_pool, m_pool_scale, m_mem_norm_g, m_w_mem_kv, m_w_up_gla, m_w_up_pool, m_w_up_xattn, m_w_o, m_mix_post_g, m_ffn2_pre_g, m_ffn2_w_in, m_ffn2_w_out, m_ffn2_post_g, m_final_g, v_ffn1_pre_g, v_ffn1_w_in, v_ffn1_w_out, v_ffn1_post_g, v_mix_pre_g, v_w_in, v_w_fu, v_b_f, v_gla_norm_g, v_w_pool, v_pool_scale, v_mem_norm_g, v_w_mem_kv, v_w_up_gla, v_w_up_pool, v_w_up_xattn, v_w_o, v_mix_post_g, v_ffn2_pre_g, v_ffn2_w_in, v_ffn2_w_out, v_ffn2_post_g, v_final_g):
    params = [ffn1_pre_g, ffn1_w_in, ffn1_w_out, ffn1_post_g, mix_pre_g, w_in, w_fu, b_f, gla_norm_g, w_pool, pool_scale, mem_norm_g, w_mem_kv, w_up_gla, w_up_pool, w_up_xattn, w_o, mix_post_g, ffn2_pre_g, ffn2_w_in, ffn2_w_out, ffn2_post_g, final_g]
    moms = [m_ffn1_pre_g, m_ffn1_w_in, m_ffn1_w_out, m_ffn1_post_g, m_mix_pre_g, m_w_in, m_w_fu, m_b_f, m_gla_norm_g, m_w_pool, m_pool_scale, m_mem_norm_g, m_w_mem_kv, m_w_up_gla, m_w_up_pool, m_w_up_xattn, m_w_o, m_mix_post_g, m_ffn2_pre_g, m_ffn2_w_in, m_ffn2_w_out, m_ffn2_post_g, m_final_g]
    vars_ = [v_ffn1_pre_g, v_ffn1_w_in, v_ffn1_w_out, v_ffn1_post_g, v_mix_pre_g, v_w_in, v_w_fu, v_b_f, v_gla_norm_g, v_w_pool, v_pool_scale, v_mem_norm_g, v_w_mem_kv, v_w_up_gla, v_w_up_pool, v_w_up_xattn, v_w_o, v_mix_post_g, v_ffn2_pre_g, v_ffn2_w_in, v_ffn2_w_out, v_ffn2_post_g, v_final_g]
    P = {n: a[0] if a.ndim > 2 else a for n, a in zip(_NAMES, params)}
    Mo = {n: a[0] if a.ndim > 2 else a for n, a in zip(_NAMES, moms)}
    Vo = {n: a[0] if a.ndim > 2 else a for n, a in zip(_NAMES, vars_)}
    loss, dx, outs = _step(x[0], mem[0], loss_target[0], P, Mo, Vo)
    out = [loss, dx[None]]
    for kind in range(4):
        for n, p in zip(_NAMES, params):
            out.append(outs[n][kind].reshape(p.shape))
    return tuple(out)
```

```python
import functools

import jax
import jax.numpy as jnp
from jax import lax
from jax.experimental import pallas as pl
from jax.experimental.pallas import tpu as pltpu

F32 = jnp.float32
BF16 = jnp.bfloat16

N_DEV = 8
D_MODEL = 1024
D_FF = 2816
FF_BLK = 2 * D_FF // N_DEV
N_FF_BLK = D_FF // FF_BLK
FF_PAD = 768
CHUNK = 64
GLA_HEADS = 4
GLA_DK = 512
GLA_DV = 1024
GLA_HDK = 128
GLA_HDV = 256
GATE_RANK = 16
GATE_TEMP = 16.0
POOL_WINDOWS = (2, 4, 8, 16)
POOL_W = 512
POOL_GD = 128
POOL_HALO = 16
XA_HEADS = 4
XA_HD = 128
XA_W = 512
EPS = 1e-6
IN_SPLITS = (GLA_DK, GLA_DK, GLA_DV, GLA_DV, GATE_RANK, POOL_W, XA_W, 3 * D_MODEL)
IN_WIDTH = sum(IN_SPLITS)
IN_SHARD = IN_WIDTH // N_DEV
INT_W = 3072 + 3072 + 1024 + 128
INT_NBLK = 3
INT_BLK = INT_W // INT_NBLK
FLOW_W = 128
INT_MAIN = INT_W - FLOW_W
_POOL_COL = (3072 + 3072) // POOL_W
_XQ_COL = (3072 + 3072 + POOL_W) // XA_W

ADAM_LR = 0.001
ADAM_B1 = 0.9
ADAM_B2 = 0.999
ADAM_EPS = 1e-08
ADAM_WD = 0.01
ADAM_STEP = 10

VMEM_LIMIT = 56 * 1024 * 1024

_NN = (((1,), (0,)), ((), ()))
_NT = (((1,), (1,)), ((), ()))
_TN = (((0,), (0,)), ((), ()))


def _pcall(body, *, name, grid, in_specs, out_specs, out_shape, scratch=(), aliases=None):
    return pl.pallas_call(
        body, name=name, grid=grid, in_specs=in_specs, out_specs=out_specs, out_shape=out_shape,
        scratch_shapes=list(scratch), input_output_aliases=aliases or {},
        compiler_params=pltpu.CompilerParams(dimension_semantics=("arbitrary",) * len(grid),
                                             vmem_limit_bytes=VMEM_LIMIT))


def _dot(a, b, dims=_NN):
    return lax.dot_general(a.astype(BF16), b.astype(BF16), dims, preferred_element_type=F32)


def _mm(name, a, b, *, grid, a_blk, a_map, b_blk, b_map, o_shape, o_blk, o_map, dims, out_dtype=F32, deps=()):
    nk = grid[2]

    def body(a_ref, b_ref, *rest):
        o_ref, scr = rest[len(deps)], rest[len(deps) + 1:]
        p = _dot(a_ref[...], b_ref[...], dims)
        if nk == 1:
            o_ref[...] = p.astype(o_ref.dtype)
        else:
            acc = scr[0]
            k = pl.program_id(2)

            @pl.when(k == 0)
            def _():
                acc[...] = p

            @pl.when(k > 0)
            def _():
                acc[...] += p

            @pl.when(k == nk - 1)
            def _():
                o_ref[...] = acc[...].astype(o_ref.dtype)

    acc_shape = tuple(d for d in o_blk if d is not None)
    return _pcall(body, name=name, grid=grid,
                  in_specs=[pl.BlockSpec(a_blk, a_map), pl.BlockSpec(b_blk, b_map)]
                  + [pl.BlockSpec(memory_space=pl.ANY)] * len(deps),
                  out_specs=pl.BlockSpec(o_blk, o_map),
                  out_shape=jax.ShapeDtypeStruct(o_shape, out_dtype),
                  scratch=[pltpu.VMEM(acc_shape, F32)] if nk > 1 else [])(a, b, *deps)


def _mm_nn(name, a, b, out_dtype=F32, tm=1024, tn=None):
    M, K = a.shape
    N = b.shape[1]
    tm, tn = min(tm, M), (tn or N)
    return _mm(name, a, b, grid=(N // tn, M // tm, 1), a_blk=(tm, K), a_map=lambda j, i, k: (i, 0),
               b_blk=(K, tn), b_map=lambda j, i, k: (0, j), o_shape=(M, N), o_blk=(tm, tn),
               o_map=lambda j, i, k: (i, j), dims=_NN, out_dtype=out_dtype)


def _mm_nt(name, a, b, out_dtype=F32, tm=1024):
    M, K = a.shape
    N = b.shape[0]
    tm = min(tm, M)
    return _mm(name, a, b, grid=(1, M // tm, 1), a_blk=(tm, K), a_map=lambda j, i, k: (i, 0),
               b_blk=(N, K), b_map=lambda j, i, k: (0, 0), o_shape=(M, N), o_blk=(tm, N),
               o_map=lambda j, i, k: (i, 0), dims=_NT, out_dtype=out_dtype)


def _mm_tn(name, a, b, out_dtype=BF16, ts=512, tn=None, tn_a=None, deps=()):
    S, M = a.shape
    N = b.shape[1]
    ts, tn, tn_a = min(ts, S), (tn or N), (tn_a or M)
    return _mm(name, a, b, grid=((N // tn) * (M // tn_a), 1, S // ts), a_blk=(ts, tn_a),
               a_map=lambda j, i, k: (k, j if tn_a < M else 0), b_blk=(ts, tn),
               b_map=lambda j, i, k: (k, j if tn < N else 0), o_shape=(M, N), o_blk=(tn_a, tn),
               o_map=lambda j, i, k: (j, 0) if tn_a < M else (0, j), dims=_TN, out_dtype=out_dtype, deps=deps)


class _Win:
    def __init__(self, arr, w, c):
        self.arr, self.w, self.c = arr, w, c


def _row_spec(x, tm):
    if isinstance(x, _Win):
        return x.arr, pl.BlockSpec((tm, x.w), functools.partial(lambda i, c: (i, c), c=x.c))
    if x.ndim == 3:
        return x, pl.BlockSpec((x.shape[0], tm, x.shape[2]), lambda i: (0, i, 0))
    return x, pl.BlockSpec((tm, x.shape[1]), lambda i: (i, 0))


def _rowwise(name, fn, rows, consts, outs, accs=(), tm=512):
    first = rows[0].arr if isinstance(rows[0], _Win) else rows[0]
    S = first.shape[1] if first.ndim == 3 else first.shape[0]
    tm = min(tm, S)
    n_in, n_out = len(rows) + len(consts), len(outs)
    arrays, in_specs = [], []
    for r in rows:
        arr, spec = _row_spec(r, tm)
        arrays.append(arr)
        in_specs.append(spec)
    for c in consts:
        arrays.append(c)
        in_specs.append(pl.BlockSpec(c.shape, functools.partial(lambda i, n: (0,) * n, n=c.ndim)))
    aliases = {}
    for k, o in enumerate(outs):
        if isinstance(o, _Win):
            aliases[len(arrays)] = k
            arrays.append(o.arr)
            in_specs.append(pl.BlockSpec(memory_space=pl.ANY))
    n_thru = len(aliases)
    out_specs = [_row_spec(o, tm)[1] for o in outs]
    out_specs += [pl.BlockSpec(a.shape, functools.partial(lambda i, n: (0,) * n, n=len(a.shape))) for a in accs]
    out_shape = [_sds(o.arr.shape, o.arr.dtype) if isinstance(o, _Win) else _sds(o.shape, o.dtype) for o in outs]

    def body(*refs):
        res = fn(*[r[...] for r in refs[:n_in]])
        if not isinstance(res, (tuple, list)):
            res = (res,)
        orefs = refs[n_in + n_thru:]
        for r, v in zip(orefs[:n_out], res[:n_out]):
            r[...] = v.astype(r.dtype)
        i = pl.program_id(0)
        for r, v in zip(orefs[n_out:], res[n_out:]):
            @pl.when(i == 0)
            def _(r=r, v=v):
                r[...] = v.astype(r.dtype)

            @pl.when(i > 0)
            def _(r=r, v=v):
                r[...] += v.astype(r.dtype)

    return _pcall(body, name=name, grid=(S // tm,), in_specs=in_specs, out_specs=out_specs,
                  out_shape=out_shape + [_sds(a.shape, a.dtype) for a in accs], aliases=aliases)(*arrays)


def _sds(shape, dtype=F32):
    return jax.ShapeDtypeStruct(shape, dtype)


def _rms(x, g):
    return x * lax.rsqrt(jnp.mean(x * x, axis=-1, keepdims=True) + EPS) * g


def _rms_bwd(x, g, dy):
    _, vjp = jax.vjp(_rms, x, g)
    return vjp(dy)


def _sigmoid(x):
    return 0.5 * jnp.tanh(0.5 * x) + 0.5


def _silu(x):
    return x * _sigmoid(x)


def _log_sigmoid(f):
    return jnp.minimum(f, 0.0) - jnp.log(1.0 + jnp.exp(-jnp.abs(f)))


def _head_rms_gate(o, g_out, gn):
    parts = [_rms(o[:, h * GLA_HDV:(h + 1) * GLA_HDV], gn[:, h * GLA_HDV:(h + 1) * GLA_HDV]) for h in range(GLA_HEADS)]
    return jnp.concatenate(parts, axis=-1) * _silu(g_out.astype(F32))


def _merge(gates, ya, yb, yc):
    gates, ya, yb, yc = (t.astype(F32) for t in (gates, ya, yb, yc))
    return (_sigmoid(gates[:, :D_MODEL]) * ya + _sigmoid(gates[:, D_MODEL:2 * D_MODEL]) * yb
            + _sigmoid(gates[:, 2 * D_MODEL:]) * yc)


def _tri_dot(t, x):
    hi = x.astype(BF16)
    r1 = x - hi.astype(F32)
    mid = r1.astype(BF16)
    lo = (r1 - mid.astype(F32)).astype(BF16)
    d = functools.partial(lax.dot_general, dimension_numbers=_NN, preferred_element_type=F32)
    return d(t, hi) + d(t, mid) + d(t, lo)


def _pre_norm(name, x, g, after):
    S = x.shape[0]
    tm = min(512, S)

    def body(x_ref, g_ref, *rest):
        rest[-1][...] = _rms(x_ref[...], g_ref[...]).astype(BF16)

    row = pl.BlockSpec((tm, D_MODEL), lambda i: (i, 0))
    return _pcall(body, name=name, grid=(S // tm,),
                  in_specs=[row, pl.BlockSpec((1, D_MODEL), lambda i: (0, 0))] + [pl.BlockSpec(memory_space=pl.ANY)] * len(after),
                  out_specs=row, out_shape=_sds(x.shape, BF16))(x, g, *after)


def _ffn_fwd(tag, x, h, wt, gpost, get_wo, g_next, tgt=None):
    S = x.shape[0]
    tm = min(512, S)

    def up_body(h_ref, w_ref, u_ref, act_ref):
        hh = h_ref[...]
        for j in range(N_FF_BLK):
            ab = _dot(hh, w_ref[2 * FF_PAD * j:2 * FF_PAD * (j + 1), :], _NT)
            u_ref[:, 2 * FF_PAD * j:2 * FF_PAD * (j + 1)] = ab.astype(BF16)
            act_ref[:, FF_PAD * j:FF_PAD * (j + 1)] = (_silu(ab[:, :FF_PAD]) * ab[:, FF_PAD:]).astype(BF16)

    u, act = _pcall(
        up_body, name=tag + "_up", grid=(S // tm,),
        in_specs=[pl.BlockSpec((tm, D_MODEL), lambda i: (i, 0)),
                  pl.BlockSpec(wt.shape, lambda i: (0, 0), pipeline_mode=pl.Buffered(1))],
        out_specs=[pl.BlockSpec((tm, N_DEV * FF_PAD), lambda i: (i, 0)),
                   pl.BlockSpec((tm, N_FF_BLK * FF_PAD), lambda i: (i, 0))],
        out_shape=[_sds((S, N_DEV * FF_PAD), BF16), _sds((S, N_FF_BLK * FF_PAD), BF16)],
    )(h, wt)

    def down_body(act_ref, w_ref, x_ref, g_ref, gn_ref, f_ref, xo_ref, hn_ref):
        f = _dot(act_ref[...], w_ref[...])
        f_ref[...] = f.astype(BF16)
        xo = x_ref[...] + 0.5 * _rms(f, g_ref[...])
        xo_ref[...] = xo
        hn_ref[...] = _rms(xo, gn_ref[...]).astype(BF16)

    def down_head_body(act_ref, w_ref, x_ref, g_ref, gn_ref, t_ref, f_ref, dxo_ref, loss_ref, dgn_ref):
        i = pl.program_id(0)
        f = _dot(act_ref[...], w_ref[...])
        f_ref[...] = f.astype(BF16)
        xo = x_ref[...] + 0.5 * _rms(f, g_ref[...])
        out, vjp = jax.vjp(_rms, xo, gn_ref[...])
        e = out - t_ref[...]
        loss = 0.5 * jnp.sum(jnp.mean(e * e, axis=-1, keepdims=True), axis=0, keepdims=True)
        dxo, dgn = vjp(e * (1.0 / D_MODEL))
        dxo_ref[...] = dxo.astype(dxo_ref.dtype)

        @pl.when(i == 0)
        def _():
            loss_ref[...] = jnp.broadcast_to(loss, loss_ref.shape)
            dgn_ref[...] = dgn

        @pl.when(i > 0)
        def _():
            loss_ref[...] += jnp.broadcast_to(loss, loss_ref.shape)
            dgn_ref[...] += dgn

    wo = get_wo(act)
    row = pl.BlockSpec((tm, D_MODEL), lambda i: (i, 0))
    vec = pl.BlockSpec((1, D_MODEL), lambda i: (0, 0))
    if tgt is not None:
        f, dxo, loss, dg_next = _pcall(
            down_head_body, name=tag + "_down", grid=(S // tm,),
            in_specs=[pl.BlockSpec((tm, N_FF_BLK * FF_PAD), lambda i: (i, 0)),
                      pl.BlockSpec(wo.shape, lambda i: (0, 0), pipeline_mode=pl.Buffered(1)), row, vec, vec, row],
            out_specs=[row, row, pl.BlockSpec((1, 128), lambda i: (0, 0)), vec],
            out_shape=[_sds((S, D_MODEL), BF16), _sds((S, D_MODEL), BF16), _sds((1, 128)), _sds((1, D_MODEL))],
        )(act, wo, x, gpost, g_next, tgt)
        return (loss, dxo, dg_next), wo, (h, u, act, f)
    f, xo, h_next = _pcall(
        down_body, name=tag + "_down", grid=(S // tm,),
        in_specs=[pl.BlockSpec((tm, N_FF_BLK * FF_PAD), lambda i: (i, 0)),
                  pl.BlockSpec(wo.shape, lambda i: (0, 0), pipeline_mode=pl.Buffered(1)), row, vec, vec],
        out_specs=[row, row, row], out_shape=[_sds((S, D_MODEL), BF16), _sds((S, D_MODEL)), _sds((S, D_MODEL), BF16)],
    )(act, wo, x, gpost, g_next)
    return (xo, h_next), wo, (h, u, act, f)


def _ffn_bwd(tag, dxo, x, gpre, wt, wo, gpost, saved, dx_dtype, on_dwo=None, on_dwt=None):
    h, u, act, f = saved
    S = x.shape[0]
    tm = min(512, S)

    def dact_body(f_ref, dxo_ref, g_ref, w_ref, u_ref, df_ref, du_ref, dg_ref):
        i = pl.program_id(0)
        df, dg = _rms_bwd(f_ref[...].astype(F32), g_ref[...], 0.5 * dxo_ref[...].astype(F32))
        df = df.astype(BF16)
        df_ref[...] = df

        @pl.when(i == 0)
        def _():
            dg_ref[...] = dg

        @pl.when(i > 0)
        def _():
            dg_ref[...] += dg

        for j in range(N_FF_BLK):
            dact = _dot(df, w_ref[FF_PAD * j:FF_PAD * (j + 1), :], _NT)
            ab = u_ref[:, 2 * FF_PAD * j:2 * FF_PAD * (j + 1)].astype(F32)
            a, b = ab[:, :FF_PAD], ab[:, FF_PAD:]
            sg = _sigmoid(a)
            du_ref[:, 2 * FF_PAD * j:2 * FF_PAD * j + FF_PAD] = (dact * b * (sg * (1.0 + a * (1.0 - sg)))).astype(BF16)
            du_ref[:, 2 * FF_PAD * j + FF_PAD:2 * FF_PAD * (j + 1)] = (dact * (a * sg)).astype(BF16)

    row = pl.BlockSpec((tm, D_MODEL), lambda i: (i, 0))
    vec = pl.BlockSpec((1, D_MODEL), lambda i: (0, 0))
    u_spec = pl.BlockSpec((tm, N_DEV * FF_PAD), lambda i: (i, 0))
    df, du, dgpost = _pcall(
        dact_body, name=tag + "_dact", grid=(S // tm,),
        in_specs=[row, row, vec, pl.BlockSpec(wo.shape, lambda i: (0, 0), pipeline_mode=pl.Buffered(1)), u_spec],
        out_specs=[row, u_spec, vec],
        out_shape=[_sds((S, D_MODEL), BF16), _sds((S, N_DEV * FF_PAD), BF16), _sds((1, D_MODEL))])(f, dxo, gpost, wo, u)
    dwo = _mm_tn(tag + "_dwo", act, df, ts=S, tn_a=FF_PAD)
    dwt = _mm_tn(tag + "_dwt", du, h, ts=S, tn_a=FF_PAD, deps=[on_dwo(dwo)] if on_dwo is not None else [])
    dx, dgpre = _dh_pre_norm_bwd(tag + "_dh", [du], wt, x, dxo, gpre, [on_dwt(dwt)] if on_dwt is not None else [],
                                 dx_dtype)
    return dx, dgpre, dgpost


def _dh_pre_norm_bwd(name, dzs, wt, x, dres, g, deps, dx_dtype):
    S = x.shape[0]
    ks = [dz.shape[1] for dz in dzs]

    def body(*refs):
        dz_refs = refs[:len(dzs)]
        w_ref, x_ref, dres_ref, g_ref = refs[len(dzs):len(dzs) + 4]
        dx_ref, dg_ref = refs[len(dzs) + 4 + len(deps):]
        i = pl.program_id(0)
        dh, o = None, 0
        for dz_ref, k in zip(dz_refs, ks):
            p = _dot(dz_ref[...], w_ref[o:o + k, :])
            dh, o = (p if dh is None else dh + p), o + k
        dx, dg = _rms_bwd(x_ref[...], g_ref[...], dh)
        dx_ref[...] = (dx + dres_ref[...]).astype(dx_ref.dtype)

        @pl.when(i == 0)
        def _():
            dg_ref[...] = dg

        @pl.when(i > 0)
        def _():
            dg_ref[...] += dg

    th = min(512, S)
    row = pl.BlockSpec((th, D_MODEL), lambda i: (i, 0))
    vec = pl.BlockSpec((1, D_MODEL), lambda i: (0, 0))
    return _pcall(
        body, name=name, grid=(S // th,),
        in_specs=[pl.BlockSpec((th, k), lambda i: (i, 0)) for k in ks]
        + [pl.BlockSpec(wt.shape, lambda i: (0, 0), pipeline_mode=pl.Buffered(1)), row, row, vec]
        + [pl.BlockSpec(memory_space=pl.ANY)] * len(deps),
        out_specs=[row, vec], out_shape=[_sds((S, D_MODEL), dx_dtype), _sds((1, D_MODEL))])(*dzs, wt, x, dres, g, *deps)


GLA_G = 8


def _gla_tile_common(k, flow, wfu, bf):
    f = _dot(flow, wfu) + bf
    la = _log_sigmoid(f) * (1.0 / GATE_TEMP)
    tri = _tri_matrix(True)
    ws, ds = [], []
    for g in range(k.shape[0] // CHUNK):
        b = _tri_dot(tri, la[g * CHUNK:(g + 1) * CHUNK])
        b_end = b[CHUNK - 1:CHUNK, :]
        ws.append(jnp.exp(b_end - b))
        ds.append(jnp.exp(b_end))
    w = jnp.concatenate(ws, axis=0)
    return f, w, k * w, ds


def _tri_matrix(lower):
    r = lax.broadcasted_iota(jnp.int32, (CHUNK, CHUNK), 0)
    c = lax.broadcasted_iota(jnp.int32, (CHUNK, CHUNK), 1)
    return jnp.where((r >= c) if lower else (r <= c), 1.0, 0.0).astype(BF16)


def _heads():
    return [(slice(h * GLA_HDK, (h + 1) * GLA_HDK), slice(h * GLA_HDV, (h + 1) * GLA_HDV)) for h in range(GLA_HEADS)]


def _gla_fwd(proj, wfu, bf, gn, deps=()):
    S = proj.shape[0]
    G = min(GLA_G, S // CHUNK)
    T = G * CHUNK
    nc = S // CHUNK

    def body(q_ref, k_ref, v_ref, go_ref, flow_ref, wfu_ref, bf_ref, gn_ref, *rest):
        ya_ref, st_ref, state = rest[len(deps):]
        @pl.when(pl.program_id(0) == 0)
        def _():
            state[...] = jnp.zeros_like(state)

        _, _, kt, ds = _gla_tile_common(k_ref[...], flow_ref[...], wfu_ref[...], bf_ref[...])
        q = q_ref[...].astype(F32) * (GLA_HDK ** -0.5)
        v = v_ref[...]
        rows = [slice(g * CHUNK, (g + 1) * CHUNK) for g in range(G)]
        kv = [[_dot(v[r, vs], kt[r, ks], _TN) for ks, vs in _heads()] for r in rows]
        st = [state[vs, :] for _, vs in _heads()]
        o = []
        for g, r in enumerate(rows):
            outs = []
            for h, (ks, vs) in enumerate(_heads()):
                st[h] = st[h] * ds[g][:, ks] + kv[g][h]
                st_ref[g, vs, :] = st[h]
                outs.append(_dot(q[r, ks], st[h], _NT))
            o.append(jnp.concatenate(outs, axis=-1))
        for h, (_, vs) in enumerate(_heads()):
            state[vs, :] = st[h]
        ya_ref[...] = _head_rms_gate(jnp.concatenate(o, axis=0), go_ref[...], gn_ref[...]).astype(BF16)

    return _pcall(
        body, name="gla_fwd", grid=(S // T,),
        in_specs=[pl.BlockSpec((T, GLA_DK), lambda c: (c, 0)), pl.BlockSpec((T, GLA_DK), lambda c: (c, 1)),
                  pl.BlockSpec((T, GLA_DV), lambda c: (c, 1)), pl.BlockSpec((T, GLA_DV), lambda c: (c, 2)),
                  pl.BlockSpec((T, FLOW_W), lambda c: (c, (INT_W - FLOW_W) // FLOW_W)),
                  pl.BlockSpec(wfu.shape, lambda c: (0, 0)), pl.BlockSpec(bf.shape, lambda c: (0, 0)),
                  pl.BlockSpec(gn.shape, lambda c: (0, 0))]
        + [pl.BlockSpec(memory_space=pl.ANY)] * len(deps),
        out_specs=[pl.BlockSpec((T, GLA_DV), lambda c: (c, 0)),
                   pl.BlockSpec((G, GLA_DV, GLA_HDK), lambda c: (c, 0, 0))],
        out_shape=[_sds((S, GLA_DV), BF16), _sds((nc, GLA_DV, GLA_HDK))],
        scratch=[pltpu.VMEM((GLA_DV, GLA_HDK), F32)])(proj, proj, proj, proj, proj, wfu, bf, gn, *deps)


def _gla_bwd(proj, wfu, bf, gn, states, d_ya_in, dmain):
    S = proj.shape[0]
    G = min(GLA_G, S // CHUNK)
    T = G * CHUNK
    nt = S // T

    def body(q_ref, k_ref, v_ref, go_ref, flow_ref, wfu_ref, bf_ref, gn_ref, st_ref, stp_ref, dya_ref, dmain_in,
             dqkvg_ref, dflow_ref, dwfu_ref, dbf_ref, dgn_ref, dstate):
        dq_ref = dqkvg_ref.at[:, 0:GLA_DK]
        dk_ref = dqkvg_ref.at[:, GLA_DK:2 * GLA_DK]
        dv_ref = dqkvg_ref.at[:, 2 * GLA_DK:2 * GLA_DK + GLA_DV]
        dgo_ref = dqkvg_ref.at[:, 2 * GLA_DK + GLA_DV:]
        step = pl.program_id(0)

        @pl.when(step == 0)
        def _():
            dstate[...] = jnp.zeros_like(dstate)
            dwfu_ref[...] = jnp.zeros_like(dwfu_ref)
            dbf_ref[...] = jnp.zeros_like(dbf_ref)
            dgn_ref[...] = jnp.zeros_like(dgn_ref)

        flow, wfu_v = flow_ref[...], wfu_ref[...]
        f, w, kt, ds = _gla_tile_common(k_ref[...], flow, wfu_v, bf_ref[...])
        q = q_ref[...].astype(F32) * (GLA_HDK ** -0.5)
        v = v_ref[...]
        rows = [slice(g * CHUNK, (g + 1) * CHUNK) for g in range(G)]
        o = jnp.concatenate([jnp.concatenate([_dot(q[r, ks], st_ref[g, vs, :], _NT) for ks, vs in _heads()], axis=-1)
                             for g, r in enumerate(rows)], axis=0)
        _, vjp = jax.vjp(_head_rms_gate, o, go_ref[...], gn_ref[...])
        dout, dgo, dgn = vjp(dya_ref[...].astype(F32))
        dgo_ref[...] = dgo.astype(dgo_ref.dtype)
        dgn_ref[...] += dgn
        dq = [jnp.concatenate([_dot(dout[r, vs], st_ref[g, vs, :]) for _, vs in _heads()], axis=-1)
              for g, r in enumerate(rows)]
        qdo = [[_dot(dout[r, vs], q[r, ks], _TN) for ks, vs in _heads()] for r in rows]
        dq_ref[...] = (jnp.concatenate(dq, axis=0) * (GLA_HDK ** -0.5)).astype(dq_ref.dtype)
        has_prev = (step < nt - 1).astype(F32)
        carry = [dstate[vs, :] for _, vs in _heads()]
        dkt, dv, dd = [None] * G, [None] * G, [None] * G
        for g in reversed(range(G)):
            r = rows[g]
            dkts, dvs, dds = [], [], []
            for h, (ks, vs) in enumerate(_heads()):
                dst = carry[h] + qdo[g][h]
                dkts.append(_dot(v[r, vs], dst))
                dvs.append(_dot(kt[r, ks], dst, _NT))
                st_prev = st_ref[g - 1, vs, :] if g > 0 else stp_ref[vs, :] * has_prev
                dds.append(jnp.sum(dst * st_prev, axis=0, keepdims=True))
                carry[h] = dst * ds[g][:, ks]
            dkt[g], dv[g], dd[g] = (jnp.concatenate(t, axis=-1) for t in (dkts, dvs, dds))
        for h, (_, vs) in enumerate(_heads()):
            dstate[vs, :] = carry[h]
        dkt = jnp.concatenate(dkt, axis=0)
        dv_ref[...] = jnp.concatenate(dv, axis=0).astype(dv_ref.dtype)
        dk_ref[...] = (dkt * w).astype(dk_ref.dtype)
        de = dkt * kt
        tri = _tri_matrix(False)
        dla = []
        for g, r in enumerate(rows):
            db_end = jnp.sum(de[r], axis=0, keepdims=True) + dd[g] * ds[g]
            dla.append(db_end - _tri_dot(tri, de[r]))
        df = jnp.concatenate(dla, axis=0) * (1.0 - _sigmoid(f)) * (1.0 / GATE_TEMP)
        dflow_ref[...] = _dot(df, wfu_v, _NT).astype(dflow_ref.dtype)
        dwfu_ref[...] += _dot(flow, df, _TN)
        dbf_ref[...] += jnp.sum(df, axis=0, keepdims=True)

    rt = lambda s: nt - 1 - s
    return _pcall(
        body, name="gla_bwd", grid=(nt,),
        in_specs=[pl.BlockSpec((T, GLA_DK), lambda s: (rt(s), 0)), pl.BlockSpec((T, GLA_DK), lambda s: (rt(s), 1)),
                  pl.BlockSpec((T, GLA_DV), lambda s: (rt(s), 1)), pl.BlockSpec((T, GLA_DV), lambda s: (rt(s), 2)),
                  pl.BlockSpec((T, FLOW_W), lambda s: (rt(s), (INT_W - FLOW_W) // FLOW_W)),
                  pl.BlockSpec(wfu.shape, lambda s: (0, 0)), pl.BlockSpec(bf.shape, lambda s: (0, 0)),
                  pl.BlockSpec(gn.shape, lambda s: (0, 0)),
                  pl.BlockSpec((G, GLA_DV, GLA_HDK), lambda s: (rt(s), 0, 0)),
                  pl.BlockSpec((None, GLA_DV, GLA_HDK), lambda s: (jnp.maximum(rt(s) * G - 1, 0), 0, 0)),
                  pl.BlockSpec((T, GLA_DV), lambda s: (rt(s), 0)), pl.BlockSpec(memory_space=pl.ANY)],
        out_specs=[pl.BlockSpec((T, 2 * GLA_DK + 2 * GLA_DV), lambda s: (rt(s), 0)),
                   pl.BlockSpec((T, FLOW_W), lambda s: (rt(s), 0)),
                   pl.BlockSpec(wfu.shape, lambda s: (0, 0)), pl.BlockSpec(bf.shape, lambda s: (0, 0)),
                   pl.BlockSpec(gn.shape, lambda s: (0, 0))],
        out_shape=[_sds(dmain.shape, dmain.dtype), _sds((S, FLOW_W), BF16), _sds(wfu.shape), _sds(bf.shape),
                   _sds(gn.shape)],
        scratch=[pltpu.VMEM((GLA_DV, GLA_HDK), F32)], aliases={11: 0},
    )(proj, proj, proj, proj, proj, wfu, bf, gn, states, states, d_ya_in, dmain)


def _pool_counts(tm, i):
    t = (lax.broadcasted_iota(jnp.int32, (tm, POOL_GD), 0) + i * tm + 1).astype(F32)
    return [jnp.minimum(t, float(w)) for w in POOL_WINDOWS]


def _pool_fwd(proj, w_pool, pool_scale, tm=512):
    S = proj.shape[0]
    tm = min(tm, S // 2)
    col = _POOL_COL
    hb = tm // POOL_HALO

    def body(p_ref, halo_ref, wp_ref, sc_ref, mixed_ref, out_ref):
        i = pl.program_id(0)
        p = p_ref[...].astype(F32)
        halo = halo_ref[...].astype(F32) * (i > 0).astype(F32)
        ext = jnp.concatenate([halo, p], axis=0)
        n = tm + POOL_HALO
        sums, acc, k = {}, ext, 1
        while k < POOL_WINDOWS[-1]:
            acc = acc + pltpu.roll(acc, k, axis=0)
            k *= 2
            sums[k] = acc
        cnts = _pool_counts(tm, i)
        mixed, lin = [], []
        for g, w in enumerate(POOL_WINDOWS):
            ls = slice(g * POOL_GD, (g + 1) * POOL_GD)
            m = sums[w][POOL_HALO:n, ls] / cnts[g] - p[:, ls]
            mixed.append(m)
            lin.append(_dot(m, wp_ref[g]))
        mixed_ref[...] = jnp.concatenate(mixed, axis=-1)
        out_ref[...] = (jnp.concatenate(lin, axis=-1) * sc_ref[...]).astype(out_ref.dtype)

    return _pcall(
        body, name="pool_fwd", grid=(S // tm,),
        in_specs=[pl.BlockSpec((tm, POOL_W), lambda i: (i, col)),
                  pl.BlockSpec((POOL_HALO, POOL_W), lambda i: (jnp.maximum(i * hb - 1, 0), col)),
                  pl.BlockSpec(w_pool.shape, lambda i: (0, 0, 0)), pl.BlockSpec(pool_scale.shape, lambda i: (0, 0))],
        out_specs=[pl.BlockSpec((tm, POOL_W), lambda i: (i, 0)), pl.BlockSpec((tm, POOL_W), lambda i: (i, 0))],
        out_shape=[_sds((S, POOL_W)), _sds((S, POOL_W), BF16)])(proj, proj, w_pool, pool_scale)


def _pool_lin_bwd(dout, mixed, w_pool, pool_scale):
    S = dout.shape[0]

    def fn(dout, mixed, wp, sc):
        dlin = dout * sc
        dm, dwp, lin = [], [], []
        for g in range(len(POOL_WINDOWS)):
            ls = slice(g * POOL_GD, (g + 1) * POOL_GD)
            lin.append(_dot(mixed[:, ls], wp[g]))
            dm.append(_dot(dlin[:, ls], wp[g], _NT))
            dwp.append(_dot(mixed[:, ls], dlin[:, ls], _TN))
        dsc = jnp.sum(dout * jnp.concatenate(lin, axis=-1), axis=0, keepdims=True)
        return jnp.concatenate(dm, axis=-1), jnp.concatenate(dwp, axis=0), dsc

    return _rowwise("pool_lin_bwd", fn, [dout, mixed], [w_pool, pool_scale], [_sds((S, POOL_W))],
                    [_sds((len(POOL_WINDOWS) * POOL_GD, POOL_GD)), _sds((1, POOL_W))])


def _pool_win_bwd(dmixed, dmain, tm=512):
    S = dmixed.shape[0]
    tm = min(tm, S // 2)
    nt = S // tm
    hb = tm // POOL_HALO

    def body(dm_ref, halo_ref, dmain_in, dp_ref):
        i = pl.program_id(0)
        dm = dm_ref[...]
        halo = halo_ref[...] * (i < nt - 1).astype(F32)
        cnts = _pool_counts(tm, i)
        cnts_h = [c[:POOL_HALO] for c in _pool_counts(tm, i + 1)]
        r = jnp.concatenate([jnp.concatenate([dm[:, g * POOL_GD:(g + 1) * POOL_GD] / cnts[g] for g in range(4)], axis=-1),
                             jnp.concatenate([halo[:, g * POOL_GD:(g + 1) * POOL_GD] / cnts_h[g] for g in range(4)], axis=-1)],
                            axis=0)
        n = tm + POOL_HALO
        sums, acc, k = {}, r, 1
        while k < POOL_WINDOWS[-1]:
            acc = acc + pltpu.roll(acc, n - k, axis=0)
            k *= 2
            sums[k] = acc
        dp = [sums[w][:tm, g * POOL_GD:(g + 1) * POOL_GD] for g, w in enumerate(POOL_WINDOWS)]
        dp_ref[...] = (jnp.concatenate(dp, axis=-1) - dm).astype(dp_ref.dtype)

    return _pcall(
        body, name="pool_win_bwd", grid=(nt,),
        in_specs=[pl.BlockSpec((tm, POOL_W), lambda i: (i, 0)),
                  pl.BlockSpec((POOL_HALO, POOL_W), lambda i: (jnp.minimum((i + 1) * hb, S // POOL_HALO - 1), 0)),
                  pl.BlockSpec(memory_space=pl.ANY)],
        out_specs=pl.BlockSpec((tm, POOL_W), lambda i: (i, _POOL_COL)),
        out_shape=_sds(dmain.shape, dmain.dtype), aliases={2: 0})(dmixed, dmixed, dmain)


def _xattn_probs(q, kv, h):
    hs = slice(h * XA_HD, (h + 1) * XA_HD)
    s = _dot(q[:, hs], kv[:, hs], _NT) * (XA_HD ** -0.5)
    s = s - jnp.max(s, axis=-1, keepdims=True)
    e = jnp.exp(s)
    return e / jnp.sum(e, axis=-1, keepdims=True)


def _xattn_fwd(proj, kv):
    S = proj.shape[0]

    def fn(q, kv):
        outs = []
        for h in range(XA_HEADS):
            p = _xattn_probs(q, kv, h)
            outs.append(_dot(p, kv[:, XA_W + h * XA_HD:XA_W + (h + 1) * XA_HD]))
        return jnp.concatenate(outs, axis=-1)

    return _rowwise("xattn_fwd", fn, [_Win(proj, XA_W, _XQ_COL)], [kv], [_sds((S, XA_W), BF16)])[0]


def _xattn_bwd(proj, kv, dxa, dmain):
    def fn(q, dxa, kv):
        dqs, dks, dvs = [], [], []
        for h in range(XA_HEADS):
            hs = slice(h * XA_HD, (h + 1) * XA_HD)
            vh = kv[:, XA_W + h * XA_HD:XA_W + (h + 1) * XA_HD]
            p = _xattn_probs(q, kv, h)
            dp = _dot(dxa[:, hs], vh, _NT)
            ds = p * (dp - jnp.sum(p * dp, axis=-1, keepdims=True)) * (XA_HD ** -0.5)
            dqs.append(_dot(ds, kv[:, hs]))
            dks.append(_dot(ds, q[:, hs], _TN))
            dvs.append(_dot(p, dxa[:, hs], _TN))
        return jnp.concatenate(dqs, axis=-1), jnp.concatenate(dks + dvs, axis=-1)

    return _rowwise("xattn_bwd", fn, [_Win(proj, XA_W, _XQ_COL), dxa], [kv], [_Win(dmain, XA_W, _XQ_COL)],
                    [_sds(kv.shape)])


def _mix_fwd(x1, h, mem, W, g_next, on_proj=None, late_weights=None):
    S = x1.shape[0]
    M = mem.shape[0]
    tm = min(512, S)
    def proj_body(h_ref, w_ref, o_ref):
        hh = h_ref[...]
        for j in range(INT_NBLK):
            o_ref[:, INT_BLK * j:INT_BLK * (j + 1)] = _dot(hh, w_ref[INT_BLK * j:INT_BLK * (j + 1), :], _NT).astype(BF16)

    proj = _pcall(
        proj_body, name="mix_proj", grid=(S // tm,),
        in_specs=[pl.BlockSpec((tm, D_MODEL), lambda i: (i, 0)),
                  pl.BlockSpec((INT_W, D_MODEL), lambda i: (0, 0), pipeline_mode=pl.Buffered(1))],
        out_specs=pl.BlockSpec((tm, INT_W), lambda i: (i, 0)), out_shape=_sds((S, INT_W), BF16))(h, W["w_int_t"])
    ya_in, states = _gla_fwd(proj, W["w_fu_pad"], W["b_f"], W["gla_norm_g"],
                             on_proj(proj) if on_proj is not None else [])
    mixed, pool_out = _pool_fwd(proj, W["w_pool"], W["pool_scale"])
    mem_n = _rowwise("mem_norm", lambda m, g: _rms(m, g), [mem], [W["mem_norm_g"]], [_sds((M, D_MODEL), BF16)])[0]
    if late_weights is not None:
        W = {**W, **late_weights([pool_out])}
    kv = _mm_nn("mem_kv", mem_n, W["w_mem_kv"])
    xa = _xattn_fwd(proj, kv)
    def out_fn(gates, ya_in, pool_out, xa, x1, wa, wb, wc, wo, g, gn):
        ya, yb, yc = (_dot(t, w).astype(BF16) for t, w in ((ya_in, wa), (pool_out, wb), (xa, wc)))
        merged = _merge(gates, ya, yb, yc).astype(BF16)
        y = _dot(merged, wo)
        x2 = x1 + _rms(y, g)
        return ya, yb, yc, merged, y, x2, _rms(x2, gn)

    ya, yb, yc, merged, y, x2, h_next = _rowwise(
        "mix_out", out_fn, [_Win(proj, 3 * D_MODEL, 1), ya_in, pool_out, xa, x1],
        [W["w_up_gla"], W["w_up_pool"], W["w_up_xattn"], W["w_o"], W["mix_post_g"], g_next],
        [_sds((S, D_MODEL), BF16)] * 5 + [_sds((S, D_MODEL)), _sds((S, D_MODEL), BF16)], tm=256)
    return x2, h_next, W, (h, proj, states, ya_in, mixed, pool_out, mem_n, kv, xa, ya, yb, yc, merged, y)


def _mix_bwd(dx2, x1, mem, W, saved, on_grads=None):
    h, proj, states, ya_in, mixed, pool_out, mem_n, kv, xa, ya, yb, yc, merged, y = saved
    S = x1.shape[0]

    def out_bwd(y, dx2, gates, ya, yb, yc, g, wo, wa, wb, wc):
        dy, dg = _rms_bwd(y.astype(F32), g, dx2.astype(F32))
        dy = dy.astype(BF16)
        _, vjp = jax.vjp(_merge, gates, ya, yb, yc)
        dgates, dya, dyb, dyc = vjp(_dot(dy, wo, _NT))
        return (dy, dgates, dya, dyb, dyc, _dot(dya, wa, _NT), _dot(dyb, wb, _NT), _dot(dyc, wc, _NT), dg)

    dmain = lax.empty((S, INT_MAIN), BF16)
    dy, dmain, dya, dyb, dyc, d_ya_in, d_pool_out, d_xa, d_mix_post_g = _rowwise(
        "mix_out_bwd", out_bwd, [y, dx2, _Win(proj, 3 * D_MODEL, 1), ya, yb, yc],
        [W["mix_post_g"], W["w_o"], W["w_up_gla"], W["w_up_pool"], W["w_up_xattn"]],
        [_sds((S, D_MODEL), BF16), _Win(dmain, 3 * D_MODEL, 1)] + [_sds((S, D_MODEL), BF16)] * 4
        + [_sds((S, POOL_W), BF16), _sds((S, XA_W), BF16)], [_sds((1, D_MODEL))], tm=256)
    d_w_o = _mm_tn("d_w_o", merged, dy, ts=S, tn_a=256)
    d_w_up_gla = _mm_tn("d_w_up_gla", ya_in, dya, ts=S, tn_a=256)
    d_w_up_pool = _mm_tn("d_w_up_pool", pool_out, dyb, ts=S, tn_a=128)
    d_w_up_xattn = _mm_tn("d_w_up_xattn", xa, dyc, ts=S, tn_a=128)

    dmain, dflow, d_wfu_pad, d_b_f, d_gla_norm_g = _gla_bwd(proj, W["w_fu_pad"], W["b_f"], W["gla_norm_g"], states,
                                                            d_ya_in, dmain)
    dmixed, d_w_pool, d_pool_scale = _pool_lin_bwd(d_pool_out, mixed, W["w_pool"], W["pool_scale"])
    dmain = _pool_win_bwd(dmixed, dmain)
    dmain, dkv = _xattn_bwd(proj, kv, d_xa, dmain)
    d_w_mem_kv = _mm_tn("d_w_mem_kv", mem_n, dkv)
    dmem_n = _mm_nt("d_mem_n", dkv, W["w_mem_kv"])
    d_mem_norm_g = _rowwise("mem_norm_bwd", lambda m, d, g: _rms_bwd(m, g, d)[1], [mem, dmem_n], [W["mem_norm_g"]], [],
                            [_sds((1, D_MODEL))])[0]
    d_w_int_t = (_mm_tn("d_w_int", dmain, h, ts=S, tn_a=INT_MAIN // 8), _mm_tn("d_w_flow", dflow, h, ts=S))
    grads = dict(
        w_int_t=d_w_int_t, w_fu_pad=d_wfu_pad, b_f=d_b_f, gla_norm_g=d_gla_norm_g, w_pool=d_w_pool,
        pool_scale=d_pool_scale, mem_norm_g=d_mem_norm_g, w_mem_kv=d_w_mem_kv, w_up_gla=d_w_up_gla,
        w_up_pool=d_w_up_pool, w_up_xattn=d_w_up_xattn, w_o=d_w_o, mix_post_g=d_mix_post_g)
    deps = [on_grads(grads)] if on_grads is not None else []
    dx1, grads["mix_pre_g"] = _dh_pre_norm_bwd("d_mix_h", [dmain, dflow], W["w_int_t"], x1, dx2, W["mix_pre_g"], deps,
                                               BF16)
    return dx1, grads


def _mesh_pos():
    x, y, c = lax.axis_index("x"), lax.axis_index("y"), lax.axis_index("c")
    return x, y, c, 4 * x + 2 * y + c


def _peer(x, y, c, r):
    px = 1 - x if r & 4 else x
    py = 1 - y if r & 2 else y
    pc = 1 - c if r & 1 else c
    return (px, py, pc), 4 * px + 2 * py + pc


_ALL_PEERS = tuple(range(1, N_DEV))
_SIBLING = 1
_SAME_CORE = (2, 4, 6)


def _dev_slot(ref, dev):
    return ref.at[dev]


def _zero_pad_rows(land):
    pad = FF_PAD - FF_BLK

    def body(land_in, o_ref):
        o_ref[...] = jnp.zeros_like(o_ref)

    return _pcall(body, name="zero_pad_rows", grid=(land.shape[0],), in_specs=[pl.BlockSpec(memory_space=pl.ANY)],
                  out_specs=pl.BlockSpec((None, pad, D_MODEL), lambda j: (j, FF_BLK // pad, 0)),
                  out_shape=_sds(land.shape, land.dtype), aliases={0: 0})(land)


class _Plan:
    def __init__(self, scatter, slots=None, shapes=None):
        self.scatter, self.slots, self.shapes = scatter, slots or {}, shapes or {}

    def src(self, srcs, a, dev):
        return self.slots.get(a, _dev_slot)(srcs[a], dev) if self.scatter else srcs[a]

    def dst(self, lands, a, dev):
        return lands[a].at[dev] if self.scatter else self.slots.get(a, _dev_slot)(lands[a], dev)

    def landing_zones(self, arrays):
        lands = []
        for a, arr in enumerate(arrays):
            if self.scatter:
                lands.append(lax.empty((N_DEV,) + tuple(self.shapes.get(a, arr.shape[1:])), arr.dtype))
            elif a in self.shapes:
                lands.append(_zero_pad_rows(lax.empty(self.shapes[a], arr.dtype)))
            else:
                lands.append(lax.empty((N_DEV,) + arr.shape, arr.dtype))
        return lands


_GATHER = _Plan(False)


def _peer_copies(srcs, lands, send_sems, recv_sems, plan, peers=_ALL_PEERS):
    x, y, c, me = _mesh_pos()
    cps = []
    for r in peers:
        pos, peer = _peer(x, y, c, r)
        for a in range(len(srcs)):
            k = a * (N_DEV - 1) + r - 1
            cps.append(pltpu.make_async_remote_copy(
                src_ref=plan.src(srcs, a, peer), dst_ref=plan.dst(lands, a, me),
                send_sem=send_sems.at[k], recv_sem=recv_sems.at[k], device_id=pos,
                device_id_type=pl.DeviceIdType.MESH))
    return cps


_HBM = pl.BlockSpec(memory_space=pltpu.HBM)
_SEM = pl.BlockSpec(memory_space=pltpu.SEMAPHORE)
_EFFECT = pltpu.SideEffectType.DATAFLOW_SIDE_EFFECTING


def _own_copies(srcs, lands, own_sems, plan):
    me = _mesh_pos()[3]
    return [pltpu.make_async_copy(plan.src(srcs, a, me), plan.dst(lands, a, me), own_sems.at[a])
            for a in range(len(srcs))]


def _exchange_start(name, arrays, plan):
    n = len(arrays)
    lands = plan.landing_zones(arrays)
    n_sem = n * (N_DEV - 1)

    def body(*refs):
        srcs, lands_ = refs[:n], refs[n:2 * n]
        send_sems, recv_sems, own_sems = refs[2 * n:2 * n + 3]
        token = refs[-1]
        for cp in _peer_copies(srcs, lands_, send_sems, recv_sems, plan) + _own_copies(srcs, lands_, own_sems, plan):
            cp.start()
        token[...] = jnp.zeros_like(token)

    hbm = lambda a: pltpu.HBM(a.shape, a.dtype)
    res = pl.pallas_call(
        body, name=name,
        out_shape=(pltpu.SemaphoreType.DMA((n_sem,)), pltpu.SemaphoreType.DMA((n_sem,)), pltpu.SemaphoreType.DMA((n,)),
                   *[hbm(a) for a in arrays], *[hbm(a) for a in lands], _sds((8, 128))),
        in_specs=[_HBM] * (2 * n),
        out_specs=(_SEM, _SEM, _SEM, *[_HBM] * (2 * n), pl.BlockSpec(memory_space=pltpu.VMEM)),
        input_output_aliases={i: 3 + i for i in range(2 * n)},
        compiler_params=pltpu.CompilerParams(has_side_effects=_EFFECT),
    )(*[pltpu.with_memory_space_constraint(a, pltpu.HBM) for a in list(arrays) + lands])
    return (res[:3], res[3:3 + n], res[3 + n:3 + 2 * n], plan), res[-1]


def _exchange_wait(name, state, after):
    sems, srcs, lands, plan = state
    n = len(srcs)

    def body(*refs):
        srcs_, lands_ = refs[:n], refs[n:2 * n]
        send_sems, recv_sems, own_sems = refs[2 * n:2 * n + 3]
        for cp in _peer_copies(srcs_, lands_, send_sems, recv_sems, plan):
            cp.wait_send()
            cp.wait_recv()
        for cp in _own_copies(srcs_, lands_, own_sems, plan):
            cp.wait()

    hbm = lambda a: pltpu.HBM(a.shape, a.dtype)
    res = pl.pallas_call(
        body, name=name, out_shape=tuple(hbm(a) for a in list(srcs) + list(lands)),
        in_specs=[_HBM] * (2 * n) + [_SEM] * 3 + [pl.BlockSpec(memory_space=pl.ANY)] * len(after),
        out_specs=tuple([_HBM] * (2 * n)), input_output_aliases={i: i for i in range(2 * n)},
        compiler_params=pltpu.CompilerParams(has_side_effects=_EFFECT),
    )(*srcs, *lands, *sems, *after)
    return res[n:]


def _gather_start(name, arrays, plan, after):
    n = len(arrays)
    lands = plan.landing_zones(arrays)
    n_sem = n * (N_DEV - 1)

    def body(*refs):
        srcs, lands_ = refs[:n], refs[n:2 * n]
        send_sems, recv_sems, own_sems = refs[2 * n + len(after):2 * n + len(after) + 3]
        token = refs[-1]
        for cp in (_peer_copies(srcs, lands_, send_sems, recv_sems, plan, (_SIBLING,) + _SAME_CORE)
                   + _own_copies(srcs, lands_, own_sems, plan)):
            cp.start()
        token[...] = jnp.zeros_like(token)

    hbm = lambda a: pltpu.HBM(a.shape, a.dtype)
    res = pl.pallas_call(
        body, name=name,
        out_shape=(pltpu.SemaphoreType.DMA((n_sem,)), pltpu.SemaphoreType.DMA((n_sem,)), pltpu.SemaphoreType.DMA((n,)),
                   *[hbm(a) for a in arrays], *[hbm(a) for a in lands], _sds((8, 128))),
        in_specs=[_HBM] * (2 * n) + [pl.BlockSpec(memory_space=pl.ANY)] * len(after),
        out_specs=(_SEM, _SEM, _SEM, *[_HBM] * (2 * n), pl.BlockSpec(memory_space=pltpu.VMEM)),
        input_output_aliases={i: 3 + i for i in range(2 * n)},
        compiler_params=pltpu.CompilerParams(has_side_effects=_EFFECT),
    )(*[pltpu.with_memory_space_constraint(a, pltpu.HBM) for a in list(arrays) + lands], *after)
    return (res[:3], res[3:3 + n], res[3 + n:3 + 2 * n], plan), res[-1]


def _pass_on_copies(lands, send_sems, recv_sems, plan):
    x, y, c, _ = _mesh_pos()
    sibling = _peer(x, y, c, _SIBLING)[0]
    cps = []
    for i, r in enumerate(_SAME_CORE):
        owner = _peer(x, y, c, r)[1]
        for a in range(len(lands)):
            k = a * len(_SAME_CORE) + i
            cps.append(pltpu.make_async_remote_copy(
                src_ref=plan.dst(lands, a, owner), dst_ref=plan.dst(lands, a, owner), send_sem=send_sems.at[k],
                recv_sem=recv_sems.at[k], device_id=sibling, device_id_type=pl.DeviceIdType.MESH))
    return cps


def _gather_pass_on(name, state, after):
    sems, srcs, lands, plan = state
    n = len(srcs)
    n_sem = n * len(_SAME_CORE)

    def body(*refs):
        srcs_, lands_ = refs[:n], refs[n:2 * n]
        send_sems, recv_sems = refs[2 * n], refs[2 * n + 1]
        on_send, on_recv = refs[2 * n + 3 + len(after)], refs[2 * n + 4 + len(after)]
        token = refs[-1]
        arrivals = _peer_copies(srcs_, lands_, send_sems, recv_sems, plan, _SAME_CORE)
        for arrived, on in zip(arrivals, _pass_on_copies(lands_, on_send, on_recv, plan)):
            arrived.wait_recv()
            on.start()
        token[...] = jnp.zeros_like(token)

    hbm = lambda a: pltpu.HBM(a.shape, a.dtype)
    res = pl.pallas_call(
        body, name=name,
        out_shape=(pltpu.SemaphoreType.DMA((n_sem,)), pltpu.SemaphoreType.DMA((n_sem,)),
                   *[hbm(a) for a in list(srcs) + list(lands)], _sds((8, 128))),
        in_specs=[_HBM] * (2 * n) + [_SEM] * 3 + [pl.BlockSpec(memory_space=pl.ANY)] * len(after),
        out_specs=(_SEM, _SEM, *[_HBM] * (2 * n), pl.BlockSpec(memory_space=pltpu.VMEM)),
        input_output_aliases={i: 2 + i for i in range(2 * n)},
        compiler_params=pltpu.CompilerParams(has_side_effects=_EFFECT),
    )(*srcs, *lands, *sems, *after)
    return (sems, res[:2], res[2:2 + n], res[2 + n:2 + 2 * n], plan), res[-1]


def _gather_wait(name, state, after):
    sems, on_sems, srcs, lands, plan = state
    n = len(srcs)

    def body(*refs):
        srcs_, lands_ = refs[:n], refs[n:2 * n]
        send_sems, recv_sems, own_sems, on_send, on_recv = refs[2 * n:2 * n + 5]
        for cp in _peer_copies(srcs_, lands_, send_sems, recv_sems, plan, (_SIBLING,)):
            cp.wait_recv()
        for cp in _peer_copies(srcs_, lands_, send_sems, recv_sems, plan, (_SIBLING,) + _SAME_CORE):
            cp.wait_send()
        for cp in _own_copies(srcs_, lands_, own_sems, plan):
            cp.wait()
        for cp in _pass_on_copies(lands_, on_send, on_recv, plan):
            cp.wait_send()
            cp.wait_recv()

    hbm = lambda a: pltpu.HBM(a.shape, a.dtype)
    res = pl.pallas_call(
        body, name=name, out_shape=tuple(hbm(a) for a in list(srcs) + list(lands)),
        in_specs=[_HBM] * (2 * n) + [_SEM] * 5 + [pl.BlockSpec(memory_space=pl.ANY)] * len(after),
        out_specs=tuple([_HBM] * (2 * n)), input_output_aliases={i: i for i in range(2 * n)},
        compiler_params=pltpu.CompilerParams(has_side_effects=_EFFECT),
    )(*srcs, *lands, *sems, *on_sems, *after)
    return res[n:]


def _sum_parts(name, recv):
    def body(r_ref, o_ref):
        s = r_ref[0]
        for j in range(1, N_DEV):
            s = s + r_ref[j]
        o_ref[...] = s

    return pl.pallas_call(body, name=name, out_shape=_sds(recv.shape[1:], recv.dtype))(recv)


def _adam(name, recv, w, m, v, col=0):
    shape = w.shape
    R, C = shape
    tr = R
    while N_DEV * tr * C * 4 > 6 * 1024 * 1024 and tr % 32 == 0:
        tr //= 2

    def body(recv_ref, w_ref, m_ref, v_ref, g_ref, d_ref, m2_ref, v2_ref):
        g = recv_ref[0].astype(F32)
        for j in range(1, N_DEV):
            g = g + recv_ref[j].astype(F32)
        w_, m_, v_ = w_ref[...], m_ref[...], v_ref[...]
        m2 = ADAM_B1 * m_ + (1.0 - ADAM_B1) * g
        v2 = ADAM_B2 * v_ + (1.0 - ADAM_B2) * (g * g)
        m_hat = m2 / (1.0 - ADAM_B1 ** ADAM_STEP)
        v_hat = v2 / (1.0 - ADAM_B2 ** ADAM_STEP)
        g_ref[...] = g
        d_ref[...] = -ADAM_LR * (m_hat / (jnp.sqrt(v_hat) + ADAM_EPS) + ADAM_WD * w_)
        m2_ref[...] = m2
        v2_ref[...] = v2

    blk = pl.BlockSpec((tr, C), lambda i: (i, 0))
    return _pcall(body, name=name, grid=(R // tr,),
                  in_specs=[pl.BlockSpec((N_DEV, tr, C), lambda i: (0, i, col)), blk, blk, blk],
                  out_specs=[blk] * 4, out_shape=[_sds(shape)] * 4)(recv, w, m, v)


_NAMES = ['ffn1_pre_g', 'ffn1_w_in', 'ffn1_w_out', 'ffn1_post_g', 'mix_pre_g', 'w_in', 'w_fu', 'b_f', 'gla_norm_g',
          'w_pool', 'pool_scale', 'mem_norm_g', 'w_mem_kv', 'w_up_gla', 'w_up_pool', 'w_up_xattn', 'w_o', 'mix_post_g',
          'ffn2_pre_g', 'ffn2_w_in', 'ffn2_w_out', 'ffn2_post_g', 'final_g']
_SHARDED = ['ffn1_w_in', 'ffn1_w_out', 'w_in', 'w_fu', 'w_mem_kv', 'w_up_gla', 'w_up_pool', 'w_up_xattn', 'w_o',
            'ffn2_w_in', 'ffn2_w_out']
_COL_SHARDED = ['w_up_pool', 'w_up_xattn']


def _cols_to_full(g):
    return jnp.transpose(g, (1, 0, 2)).reshape(g.shape[1], N_DEV * g.shape[2])


def _full_to_cols(f):
    R, C = f.shape
    return jnp.transpose(f.reshape(R, N_DEV, C // N_DEV), (1, 0, 2))


def _to_internal(w_in_t):
    o = 0
    parts = []
    for s in IN_SPLITS:
        parts.append(w_in_t[o:o + s])
        o += s
    q, k, v, g_out, f_low, p_in, xq, gates = parts
    f_low = jnp.pad(f_low, ((0, FLOW_W - GATE_RANK), (0, 0)))
    return jnp.concatenate([q, k, v, g_out, gates, p_in, xq, f_low], axis=0)


def _from_internal(d, d_flow):
    q, k, v, g_out = d[0:512], d[512:1024], d[1024:2048], d[2048:3072]
    gates, p_in, xq = d[3072:6144], d[6144:6656], d[6656:7168]
    return jnp.concatenate([q, k, v, g_out, d_flow[:GATE_RANK], p_in, xq, gates], axis=0)


_FFN_IN = ('ffn1_w_in', 'ffn2_w_in')
_FFN_OUT = ('ffn1_w_out', 'ffn2_w_out')
_GATHERS = {"ffn1_in": ['ffn1_w_in'], "ffn1_out": ['ffn1_w_out'],
            "mix_in": ['w_in', 'w_fu'], "mix_rest": ['w_mem_kv', 'w_up_gla', 'w_up_pool', 'w_up_xattn', 'w_o'],
            "ffn2": ['ffn2_w_in', 'ffn2_w_out']}
_MIX = _GATHERS["mix_in"] + _GATHERS["mix_rest"]


def _ffn_in_slot(ref, d):
    return ref.at[2 * (d % N_FF_BLK) + d // N_FF_BLK, pl.ds(0, FF_BLK)]


def _ffn_out_slot(ref, d):
    rows = FF_BLK // 2
    return ref.at[d // 2, pl.ds(pl.multiple_of((d % 2) * rows, rows), rows)]


def _ffn_plan(names, scatter):
    slots, shapes = {}, {}
    for a, n in enumerate(names):
        if n in _FFN_IN:
            slots[a] = _ffn_in_slot
            shapes[a] = (FF_BLK, D_MODEL) if scatter else (N_DEV, FF_PAD, D_MODEL)
        elif n in _FFN_OUT:
            slots[a] = _ffn_out_slot
            shapes[a] = (FF_BLK // 2, D_MODEL) if scatter else (N_FF_BLK, FF_PAD, D_MODEL)
    return _Plan(scatter, slots, shapes)


def _rows_to_full(g):
    return g.reshape(N_DEV * g.shape[1], g.shape[2])


def _mix_weights(gathered):
    W = {}
    for n, g in gathered.items():
        if n == "w_in":
            W["w_int_t"] = _to_internal(g.reshape(IN_WIDTH, D_MODEL))
        elif n == "w_fu":
            W["w_fu_pad"] = jnp.pad(_cols_to_full(g), ((0, FLOW_W - GATE_RANK), (0, 0)))
        else:
            W[n] = _cols_to_full(g) if n in _COL_SHARDED else _rows_to_full(g)
    return W


def _mix_chunks(G, n):
    if n == "w_in":
        return _from_internal(*G["w_int_t"]).reshape(N_DEV, IN_SHARD, D_MODEL)
    if n == "w_fu":
        return _full_to_cols(G["w_fu_pad"][:GATE_RANK].astype(BF16))
    if n in _COL_SHARDED:
        return _full_to_cols(G[n])
    return G[n].reshape(N_DEV, G[n].shape[0] // N_DEV, G[n].shape[1])


def _step(x, mem, tgt, P, Mo, Vo):
    def native(n, a):
        return jnp.swapaxes(a, 0, 1) if n in _FFN_IN + ("w_in",) else a

    P, Mo, Vo = ({n: native(n, a) for n, a in d.items()} for d in (P, Mo, Vo))
    small = {n: P[n] for n in _NAMES if n not in _SHARDED}

    gather, tokens = {}, []
    for grp, names in _GATHERS.items():
        gather[grp], tok = _gather_start("gather_" + grp, [P[n].astype(BF16) for n in names], _ffn_plan(names, False),
                                         tokens[-1:])
        tokens.append(tok)

    def pass_on(grp, after):
        return _gather_pass_on("gather_" + grp + "_on", gather[grp], after)

    def gathered(grp, after):
        return _gather_wait("gather_" + grp + "_wait", pass_on(grp, after)[0], [])

    def ffn_wo(lands):
        return lambda act: lands[0].reshape(N_FF_BLK * FF_PAD, D_MODEL)

    h1 = _pre_norm("ffn1_pre", x, small["ffn1_pre_g"], tokens[:1])
    w1t = gathered("ffn1_in", tokens[-1:] + [h1])[0].reshape(N_DEV * FF_PAD, D_MODEL)
    (x1, hm), w1o, sv1 = _ffn_fwd("ffn1", x, h1, w1t, small["ffn1_post_g"],
                                  lambda act: ffn_wo(gathered("ffn1_out", [act]))(act), small["mix_pre_g"])

    Wm = {**small, **_mix_weights(dict(zip(_GATHERS["mix_in"], gathered("mix_in", [x1]))))}
    passed = {}

    def pass_on_later(proj):
        tokens = []
        for grp in ("mix_rest", "ffn2"):
            passed[grp], tok = pass_on(grp, [proj])
            tokens.append(tok)
        return tokens

    def mix_rest(after):
        names = _GATHERS["mix_rest"]
        return _mix_weights(dict(zip(names, _gather_wait("gather_mix_rest_wait", passed["mix_rest"], after))))

    x2, h2, Wm, svm = _mix_fwd(x1, hm, mem, Wm, small["ffn2_pre_g"], pass_on_later, mix_rest)
    w2t, w2o = _gather_wait("gather_ffn2_wait", passed["ffn2"], [x2])
    w2t = w2t.reshape(N_DEV * FF_PAD, D_MODEL)
    (loss, dx3, d_final_g), w2o, sv2 = _ffn_fwd("ffn2", x2, h2, w2t, small["ffn2_post_g"], ffn_wo([w2o]),
                                                small["final_g"], tgt)

    G = dict(final_g=d_final_g)
    scat = {}

    def start(grp, names, arrays):
        scat[grp] = names, _exchange_start("scatter_" + grp, arrays, _ffn_plan(names, True))
        return scat[grp][1][1]

    def ffn_starts(tag):
        return (lambda dwo: start(tag + "_out", [tag + "_w_out"], [dwo.reshape(N_FF_BLK, FF_PAD, D_MODEL)]),
                lambda dwt: start(tag + "_in", [tag + "_w_in"], [dwt.reshape(N_DEV, FF_PAD, D_MODEL)]))

    dx2, G["ffn2_pre_g"], G["ffn2_post_g"] = _ffn_bwd(
        "ffn2", dx3, x2, small["ffn2_pre_g"], w2t, w2o, small["ffn2_post_g"], sv2, BF16, *ffn_starts("ffn2"))
    dx1, Gm = _mix_bwd(dx2, x1, mem, Wm, svm,
                       lambda Gm: start("mix", _MIX, [_mix_chunks(Gm, n) for n in _MIX]))
    G.update(Gm)
    dx, G["ffn1_pre_g"], G["ffn1_post_g"] = _ffn_bwd(
        "ffn1", dx1, x, small["ffn1_pre_g"], w1t, w1o, small["ffn1_post_g"], sv1, F32, *ffn_starts("ffn1"))

    vecs = sorted((n for n in small if P[n].ndim == 2), key=lambda n: -P[n].shape[1])
    packed = jnp.concatenate([G[n] for n in vecs] + [loss], axis=1)
    place, o = {}, 0
    for n in vecs:
        place[n] = o // P[n].shape[1]
        o += P[n].shape[1]
    scat["small"] = ["packed", "w_pool"], _exchange_start("gather_small_grads", [packed, G["w_pool"]], _GATHER)
    recv, outs = {}, {}
    done = [dx]
    for grp in ["ffn2_out", "ffn2_in", "mix", "ffn1_out", "ffn1_in", "small"]:
        names, (state, _) = scat[grp]
        recv.update(zip(names, _exchange_wait(("gather_" if grp == "small" else "scatter_") + grp + "_wait", state,
                                              done)))
        if grp == "small":
            loss = _sum_parts("loss_sum", recv["packed"][:, :, o:])[0, 0]
            names = list(small)
        for n in names:
            shp = P[n].shape
            shp2 = shp if len(shp) == 2 else (shp[0] * shp[1], shp[2])
            r = recv["packed"] if n in place else recv[n].reshape((N_DEV,) + shp2)
            res = _adam("adam_" + n, r, P[n].reshape(shp2), Mo[n].reshape(shp2), Vo[n].reshape(shp2), place.get(n, 0))
            outs[n] = [native(n, t.reshape(shp)) for t in res]
            done.append(res[-1])
    return loss, dx, outs


def kernel(x, mem, ffn1_pre_g, ffn1_w_in, ffn1_w_out, ffn1_post_g, mix_pre_g, w_in, w_fu, b_f, gla_norm_g, w_pool, pool_scale, mem_norm_g, w_mem_kv, w_up_gla, w_up_pool, w_up_xattn, w_o, mix_post_g, ffn2_pre_g, ffn2_w_in, ffn2_w_out, ffn2_post_g, final_g, loss_target, m_ffn1_pre_g, m_ffn1_w_in, m_ffn1_w_out, m_ffn1_post_g, m_mix_pre_g, m_w_in, m_w_fu, m_b_f, m_gla_norm_g, m_w_pool, m_pool_scale, m_mem_norm_g, m_w_mem_kv, m_w_up_gla, m_w_up_pool, m_w_up_xattn, m_w_o, m_mix_post_g, m_ffn2_pre_g, m_ffn2_w_in, m_ffn2_w_out, m_ffn2_post_g, m_final_g, v_ffn1_pre_g, v_ffn1_w_in, v_ffn1_w_out, v_ffn1_post_g, v_mix_pre_g, v_w_in, v_w_fu, v_b_f, v_gla_norm_g, v_w_pool, v_pool_scale, v_mem_norm_g, v_w_mem_kv, v_w_up_gla, v_w_up_pool, v_w_up_xattn, v_w_o, v_mix_post_g, v_ffn2_pre_g, v_ffn2_w_in, v_ffn2_w_out, v_ffn2_post_g, v_final_g):
    params = [ffn1_pre_g, ffn1_w_in, ffn1_w_out, ffn1_post_g, mix_pre_g, w_in, w_fu, b_f, gla_norm_g, w_pool, pool_scale, mem_norm_g, w_mem_kv, w_up_gla, w_up_pool, w_up_xattn, w_o, mix_post_g, ffn2_pre_g, ffn2_w_in, ffn2_w_out, ffn2_post_g, final_g]
    moms = [m_ffn1_pre_g, m_ffn1_w_in, m_ffn1_w_out, m_ffn1_post_g, m_mix_pre_g, m_w_in, m_w_fu, m_b_f, m_gla_norm_g, m_w_pool, m_pool_scale, m_mem_norm_g, m_w_mem_kv, m_w_up_gla, m_w_up_pool, m_w_up_xattn, m_w_o, m_mix_post_g, m_ffn2_pre_g, m_ffn2_w_in, m_ffn2_w_out, m_ffn2_post_g, m_final_g]
    vars_ = [v_ffn1_pre_g, v_ffn1_w_in, v_ffn1_w_out, v_ffn1_post_g, v_mix_pre_g, v_w_in, v_w_fu, v_b_f, v_gla_norm_g, v_w_pool, v_pool_scale, v_mem_norm_g, v_w_mem_kv, v_w_up_gla, v_w_up_pool, v_w_up_xattn, v_w_o, v_mix_post_g, v_ffn2_pre_g, v_ffn2_w_in, v_ffn2_w_out, v_ffn2_post_g, v_final_g]
    P = {n: a[0] if a.ndim > 2 else a for n, a in zip(_NAMES, params)}
    Mo = {n: a[0] if a.ndim > 2 else a for n, a in zip(_NAMES, moms)}
    Vo = {n: a[0] if a.ndim > 2 else a for n, a in zip(_NAMES, vars_)}
    loss, dx, outs = _step(x[0], mem[0], loss_target[0], P, Mo, Vo)
    out = [loss, dx[None]]
    for kind in range(4):
        for n, p in zip(_NAMES, params):
            out.append(outs[n][kind].reshape(p.shape))
    return tuple(out)
```

```python
import functools

import jax
import jax.numpy as jnp
from jax import lax
from jax.experimental import pallas as pl
from jax.experimental.pallas import tpu as pltpu

F32 = jnp.float32
BF16 = jnp.bfloat16

N_DEV = 8
D_MODEL = 1024
D_FF = 2816
FF_BLK = 2 * D_FF // N_DEV
N_FF_BLK = D_FF // FF_BLK
FF_PAD = 768
CHUNK = 64
GLA_HEADS = 4
GLA_DK = 512
GLA_DV = 1024
GLA_HDK = 128
GLA_HDV = 256
GATE_RANK = 16
GATE_TEMP = 16.0
POOL_WINDOWS = (2, 4, 8, 16)
POOL_W = 512
POOL_GD = 128
POOL_HALO = 16
XA_HEADS = 4
XA_HD = 128
XA_W = 512
EPS = 1e-6
IN_SPLITS = (GLA_DK, GLA_DK, GLA_DV, GLA_DV, GATE_RANK, POOL_W, XA_W, 3 * D_MODEL)
IN_WIDTH = sum(IN_SPLITS)
IN_SHARD = IN_WIDTH // N_DEV
INT_W = 3072 + 3072 + 1024 + 128
INT_NBLK = 3
INT_BLK = INT_W // INT_NBLK
FLOW_W = 128
INT_MAIN = INT_W - FLOW_W
_POOL_COL = (3072 + 3072) // POOL_W
_XQ_COL = (3072 + 3072 + POOL_W) // XA_W

ADAM_LR = 0.001
ADAM_B1 = 0.9
ADAM_B2 = 0.999
ADAM_EPS = 1e-08
ADAM_WD = 0.01
ADAM_STEP = 10

VMEM_LIMIT = 56 * 1024 * 1024

_NN = (((1,), (0,)), ((), ()))
_NT = (((1,), (1,)), ((), ()))
_TN = (((0,), (0,)), ((), ()))


def _pcall(body, *, name, grid, in_specs, out_specs, out_shape, scratch=(), aliases=None):
    return pl.pallas_call(
        body, name=name, grid=grid, in_specs=in_specs, out_specs=out_specs, out_shape=out_shape,
        scratch_shapes=list(scratch), input_output_aliases=aliases or {},
        compiler_params=pltpu.CompilerParams(dimension_semantics=("arbitrary",) * len(grid),
                                             vmem_limit_bytes=VMEM_LIMIT))


def _dot(a, b, dims=_NN):
    return lax.dot_general(a.astype(BF16), b.astype(BF16), dims, preferred_element_type=F32)


def _mm(name, a, b, *, grid, a_blk, a_map, b_blk, b_map, o_shape, o_blk, o_map, dims, out_dtype=F32, deps=()):
    nk = grid[2]

    def body(a_ref, b_ref, *rest):
        o_ref, scr = rest[len(deps)], rest[len(deps) + 1:]
        p = _dot(a_ref[...], b_ref[...], dims)
        if nk == 1:
            o_ref[...] = p.astype(o_ref.dtype)
        else:
            acc = scr[0]
            k = pl.program_id(2)

            @pl.when(k == 0)
            def _():
                acc[...] = p

            @pl.when(k > 0)
            def _():
                acc[...] += p

            @pl.when(k == nk - 1)
            def _():
                o_ref[...] = acc[...].astype(o_ref.dtype)

    acc_shape = tuple(d for d in o_blk if d is not None)
    return _pcall(body, name=name, grid=grid,
                  in_specs=[pl.BlockSpec(a_blk, a_map), pl.BlockSpec(b_blk, b_map)]
                  + [pl.BlockSpec(memory_space=pl.ANY)] * len(deps),
                  out_specs=pl.BlockSpec(o_blk, o_map),
                  out_shape=jax.ShapeDtypeStruct(o_shape, out_dtype),
                  scratch=[pltpu.VMEM(acc_shape, F32)] if nk > 1 else [])(a, b, *deps)


def _mm_nn(name, a, b, out_dtype=F32, tm=1024, tn=None):
    M, K = a.shape
    N = b.shape[1]
    tm, tn = min(tm, M), (tn or N)
    return _mm(name, a, b, grid=(N // tn, M // tm, 1), a_blk=(tm, K), a_map=lambda j, i, k: (i, 0),
               b_blk=(K, tn), b_map=lambda j, i, k: (0, j), o_shape=(M, N), o_blk=(tm, tn),
               o_map=lambda j, i, k: (i, j), dims=_NN, out_dtype=out_dtype)


def _mm_nt(name, a, b, out_dtype=F32, tm=1024):
    M, K = a.shape
    N = b.shape[0]
    tm = min(tm, M)
    return _mm(name, a, b, grid=(1, M // tm, 1), a_blk=(tm, K), a_map=lambda j, i, k: (i, 0),
               b_blk=(N, K), b_map=lambda j, i, k: (0, 0), o_shape=(M, N), o_blk=(tm, N),
               o_map=lambda j, i, k: (i, 0), dims=_NT, out_dtype=out_dtype)


def _mm_tn(name, a, b, out_dtype=BF16, ts=512, tn=None, tn_a=None, deps=()):
    S, M = a.shape
    N = b.shape[1]
    ts, tn, tn_a = min(ts, S), (tn or N), (tn_a or M)
    return _mm(name, a, b, grid=((N // tn) * (M // tn_a), 1, S // ts), a_blk=(ts, tn_a),
               a_map=lambda j, i, k: (k, j if tn_a < M else 0), b_blk=(ts, tn),
               b_map=lambda j, i, k: (k, j if tn < N else 0), o_shape=(M, N), o_blk=(tn_a, tn),
               o_map=lambda j, i, k: (j, 0) if tn_a < M else (0, j), dims=_TN, out_dtype=out_dtype, deps=deps)


class _Win:
    def __init__(self, arr, w, c):
        self.arr, self.w, self.c = arr, w, c


def _row_spec(x, tm):
    if isinstance(x, _Win):
        return x.arr, pl.BlockSpec((tm, x.w), functools.partial(lambda i, c: (i, c), c=x.c))
    if x.ndim == 3:
        return x, pl.BlockSpec((x.shape[0], tm, x.shape[2]), lambda i: (0, i, 0))
    return x, pl.BlockSpec((tm, x.shape[1]), lambda i: (i, 0))


def _rowwise(name, fn, rows, consts, outs, accs=(), tm=512):
    first = rows[0].arr if isinstance(rows[0], _Win) else rows[0]
    S = first.shape[1] if first.ndim == 3 else first.shape[0]
    tm = min(tm, S)
    n_in, n_out = len(rows) + len(consts), len(outs)
    arrays, in_specs = [], []
    for r in rows:
        arr, spec = _row_spec(r, tm)
        arrays.append(arr)
        in_specs.append(spec)
    for c in consts:
        arrays.append(c)
        in_specs.append(pl.BlockSpec(c.shape, functools.partial(lambda i, n: (0,) * n, n=c.ndim)))
    aliases = {}
    for k, o in enumerate(outs):
        if isinstance(o, _Win):
            aliases[len(arrays)] = k
            arrays.append(o.arr)
            in_specs.append(pl.BlockSpec(memory_space=pl.ANY))
    n_thru = len(aliases)
    out_specs = [_row_spec(o, tm)[1] for o in outs]
    out_specs += [pl.BlockSpec(a.shape, functools.partial(lambda i, n: (0,) * n, n=len(a.shape))) for a in accs]
    out_shape = [_sds(o.arr.shape, o.arr.dtype) if isinstance(o, _Win) else _sds(o.shape, o.dtype) for o in outs]

    def body(*refs):
        res = fn(*[r[...] for r in refs[:n_in]])
        if not isinstance(res, (tuple, list)):
            res = (res,)
        orefs = refs[n_in + n_thru:]
        for r, v in zip(orefs[:n_out], res[:n_out]):
            r[...] = v.astype(r.dtype)
        i = pl.program_id(0)
        for r, v in zip(orefs[n_out:], res[n_out:]):
            @pl.when(i == 0)
            def _(r=r, v=v):
                r[...] = v.astype(r.dtype)

            @pl.when(i > 0)
            def _(r=r, v=v):
                r[...] += v.astype(r.dtype)

    return _pcall(body, name=name, grid=(S // tm,), in_specs=in_specs, out_specs=out_specs,
                  out_shape=out_shape + [_sds(a.shape, a.dtype) for a in accs], aliases=aliases)(*arrays)


def _sds(shape, dtype=F32):
    return jax.ShapeDtypeStruct(shape, dtype)


def _rms(x, g):
    return x * lax.rsqrt(jnp.mean(x * x, axis=-1, keepdims=True) + EPS) * g


def _rms_bwd(x, g, dy):
    _, vjp = jax.vjp(_rms, x, g)
    return vjp(dy)


def _sigmoid(x):
    return 0.5 * jnp.tanh(0.5 * x) + 0.5


def _silu(x):
    return x * _sigmoid(x)


def _log_sigmoid(f):
    return jnp.minimum(f, 0.0) - jnp.log(1.0 + jnp.exp(-jnp.abs(f)))


def _head_rms_gate(o, g_out, gn):
    parts = [_rms(o[:, h * GLA_HDV:(h + 1) * GLA_HDV], gn[:, h * GLA_HDV:(h + 1) * GLA_HDV]) for h in range(GLA_HEADS)]
    return jnp.concatenate(parts, axis=-1) * _silu(g_out.astype(F32))


def _merge(gates, ya, yb, yc):
    gates, ya, yb, yc = (t.astype(F32) for t in (gates, ya, yb, yc))
    return (_sigmoid(gates[:, :D_MODEL]) * ya + _sigmoid(gates[:, D_MODEL:2 * D_MODEL]) * yb
            + _sigmoid(gates[:, 2 * D_MODEL:]) * yc)


def _tri_dot(t, x):
    hi = x.astype(BF16)
    r1 = x - hi.astype(F32)
    mid = r1.astype(BF16)
    lo = (r1 - mid.astype(F32)).astype(BF16)
    d = functools.partial(lax.dot_general, dimension_numbers=_NN, preferred_element_type=F32)
    return d(t, hi) + d(t, mid) + d(t, lo)


def _pre_norm(name, x, g, after):
    S = x.shape[0]
    tm = min(512, S)

    def body(x_ref, g_ref, *rest):
        rest[-1][...] = _rms(x_ref[...], g_ref[...]).astype(BF16)

    row = pl.BlockSpec((tm, D_MODEL), lambda i: (i, 0))
    return _pcall(body, name=name, grid=(S // tm,),
                  in_specs=[row, pl.BlockSpec((1, D_MODEL), lambda i: (0, 0))] + [pl.BlockSpec(memory_space=pl.ANY)] * len(after),
                  out_specs=row, out_shape=_sds(x.shape, BF16))(x, g, *after)


def _ffn_fwd(tag, x, h, wt, gpost, get_wo, g_next, tgt=None):
    S = x.shape[0]
    tm = min(512, S)

    def up_body(h_ref, w_ref, u_ref, act_ref):
        hh = h_ref[...]
        for j in range(N_FF_BLK):
            ab = _dot(hh, w_ref[2 * FF_PAD * j:2 * FF_PAD * (j + 1), :], _NT)
            u_ref[:, 2 * FF_PAD * j:2 * FF_PAD * (j + 1)] = ab.astype(BF16)
            act_ref[:, FF_PAD * j:FF_PAD * (j + 1)] = (_silu(ab[:, :FF_PAD]) * ab[:, FF_PAD:]).astype(BF16)

    def down_head_body(act_ref, w_ref, x_ref, g_ref, gn_ref, t_ref, f_ref, dxo_ref, loss_ref, dgn_ref):
        i = pl.program_id(0)
        f = _dot(act_ref[...], w_ref[...])
        f_ref[...] = f.astype(BF16)
        xo = x_ref[...] + 0.5 * _rms(f, g_ref[...])
        out, vjp = jax.vjp(_rms, xo, gn_ref[...])
        e = out - t_ref[...]
        loss = 0.5 * jnp.sum(jnp.mean(e * e, axis=-1, keepdims=True), axis=0, keepdims=True)
        dxo_ref[...], dgn = vjp(e * (1.0 / D_MODEL))

        @pl.when(i == 0)
        def _():
            loss_ref[...] = jnp.broadcast_to(loss, loss_ref.shape)
            dgn_ref[...] = dgn

        @pl.when(i > 0)
        def _():
            loss_ref[...] += jnp.broadcast_to(loss, loss_ref.shape)
            dgn_ref[...] += dgn

    if tgt is not None:
        def last_body(h_ref, wt_ref, wo_ref, x_ref, g_ref, gn_ref, t_ref, u_ref, act_ref, f_ref, dxo_ref, loss_ref,
                      dgn_ref):
            up_body(h_ref, wt_ref, u_ref, act_ref)
            down_head_body(act_ref, wo_ref, x_ref, g_ref, gn_ref, t_ref, f_ref, dxo_ref, loss_ref, dgn_ref)

        wo = get_wo(None)
        tl = min(256, S)
        row = pl.BlockSpec((tl, D_MODEL), lambda i: (i, 0))
        vec = pl.BlockSpec((1, D_MODEL), lambda i: (0, 0))
        u, act, f, dxo, loss, dg_next = _pcall(
            last_body, name=tag + "_fwd", grid=(S // tl,),
            in_specs=[row, pl.BlockSpec(wt.shape, lambda i: (0, 0), pipeline_mode=pl.Buffered(1)),
                      pl.BlockSpec(wo.shape, lambda i: (0, 0), pipeline_mode=pl.Buffered(1)), row, vec, vec, row],
            out_specs=[pl.BlockSpec((tl, N_DEV * FF_PAD), lambda i: (i, 0)),
                       pl.BlockSpec((tl, N_FF_BLK * FF_PAD), lambda i: (i, 0)), row, row,
                       pl.BlockSpec((1, 128), lambda i: (0, 0)), vec],
            out_shape=[_sds((S, N_DEV * FF_PAD), BF16), _sds((S, N_FF_BLK * FF_PAD), BF16), _sds((S, D_MODEL), BF16),
                       _sds((S, D_MODEL)), _sds((1, 128)), _sds((1, D_MODEL))],
        )(h, wt, wo, x, gpost, g_next, tgt)
        return (loss, dxo, dg_next), wo, (h, u, act, f)

    u, act = _pcall(
        up_body, name=tag + "_up", grid=(S // tm,),
        in_specs=[pl.BlockSpec((tm, D_MODEL), lambda i: (i, 0)),
                  pl.BlockSpec(wt.shape, lambda i: (0, 0), pipeline_mode=pl.Buffered(1))],
        out_specs=[pl.BlockSpec((tm, N_DEV * FF_PAD), lambda i: (i, 0)),
                   pl.BlockSpec((tm, N_FF_BLK * FF_PAD), lambda i: (i, 0))],
        out_shape=[_sds((S, N_DEV * FF_PAD), BF16), _sds((S, N_FF_BLK * FF_PAD), BF16)],
    )(h, wt)

    def down_body(act_ref, w_ref, x_ref, g_ref, gn_ref, f_ref, xo_ref, hn_ref):
        f = _dot(act_ref[...], w_ref[...])
        f_ref[...] = f.astype(BF16)
        xo = x_ref[...] + 0.5 * _rms(f, g_ref[...])
        xo_ref[...] = xo
        hn_ref[...] = _rms(xo, gn_ref[...]).astype(BF16)

    wo = get_wo(act)
    row = pl.BlockSpec((tm, D_MODEL), lambda i: (i, 0))
    vec = pl.BlockSpec((1, D_MODEL), lambda i: (0, 0))
    f, xo, h_next = _pcall(
        down_body, name=tag + "_down", grid=(S // tm,),
        in_specs=[pl.BlockSpec((tm, N_FF_BLK * FF_PAD), lambda i: (i, 0)),
                  pl.BlockSpec(wo.shape, lambda i: (0, 0), pipeline_mode=pl.Buffered(1)), row, vec, vec],
        out_specs=[row, row, row], out_shape=[_sds((S, D_MODEL), BF16), _sds((S, D_MODEL)), _sds((S, D_MODEL), BF16)],
    )(act, wo, x, gpost, g_next)
    return (xo, h_next), wo, (h, u, act, f)


def _ffn_bwd(tag, dxo, x, gpre, wt, wo, gpost, saved, on_dwo=None, on_dwt=None):
    h, u, act, f = saved
    S = x.shape[0]
    tm = min(512, S)

    def dact_body(f_ref, dxo_ref, g_ref, w_ref, u_ref, df_ref, du_ref, dg_ref):
        i = pl.program_id(0)
        df, dg = _rms_bwd(f_ref[...].astype(F32), g_ref[...], 0.5 * dxo_ref[...])
        df = df.astype(BF16)
        df_ref[...] = df

        @pl.when(i == 0)
        def _():
            dg_ref[...] = dg

        @pl.when(i > 0)
        def _():
            dg_ref[...] += dg

        for j in range(N_FF_BLK):
            dact = _dot(df, w_ref[FF_PAD * j:FF_PAD * (j + 1), :], _NT)
            ab = u_ref[:, 2 * FF_PAD * j:2 * FF_PAD * (j + 1)].astype(F32)
            a, b = ab[:, :FF_PAD], ab[:, FF_PAD:]
            sg = _sigmoid(a)
            du_ref[:, 2 * FF_PAD * j:2 * FF_PAD * j + FF_PAD] = (dact * b * (sg * (1.0 + a * (1.0 - sg)))).astype(BF16)
            du_ref[:, 2 * FF_PAD * j + FF_PAD:2 * FF_PAD * (j + 1)] = (dact * (a * sg)).astype(BF16)

    row = pl.BlockSpec((tm, D_MODEL), lambda i: (i, 0))
    vec = pl.BlockSpec((1, D_MODEL), lambda i: (0, 0))
    u_spec = pl.BlockSpec((tm, N_DEV * FF_PAD), lambda i: (i, 0))
    df, du, dgpost = _pcall(
        dact_body, name=tag + "_dact", grid=(S // tm,),
        in_specs=[row, row, vec, pl.BlockSpec(wo.shape, lambda i: (0, 0), pipeline_mode=pl.Buffered(1)), u_spec],
        out_specs=[row, u_spec, vec],
        out_shape=[_sds((S, D_MODEL), BF16), _sds((S, N_DEV * FF_PAD), BF16), _sds((1, D_MODEL))])(f, dxo, gpost, wo, u)
    dwo = _mm_tn(tag + "_dwo", act, df, ts=S, tn_a=FF_PAD)
    dwt = _mm_tn(tag + "_dwt", du, h, ts=S, tn_a=FF_PAD, deps=[on_dwo(dwo)] if on_dwo is not None else [])
    dx, dgpre = _dh_pre_norm_bwd(tag + "_dh", [du], wt, x, dxo, gpre, [on_dwt(dwt)] if on_dwt is not None else [])
    return dx, dgpre, dgpost


def _dh_pre_norm_bwd(name, dzs, wt, x, dres, g, deps):
    S = x.shape[0]
    ks = [dz.shape[1] for dz in dzs]

    def body(*refs):
        dz_refs = refs[:len(dzs)]
        w_ref, x_ref, dres_ref, g_ref = refs[len(dzs):len(dzs) + 4]
        dx_ref, dg_ref = refs[len(dzs) + 4 + len(deps):]
        i = pl.program_id(0)
        dh, o = None, 0
        for dz_ref, k in zip(dz_refs, ks):
            p = _dot(dz_ref[...], w_ref[o:o + k, :])
            dh, o = (p if dh is None else dh + p), o + k
        dx, dg = _rms_bwd(x_ref[...], g_ref[...], dh)
        dx_ref[...] = dx + dres_ref[...]

        @pl.when(i == 0)
        def _():
            dg_ref[...] = dg

        @pl.when(i > 0)
        def _():
            dg_ref[...] += dg

    th = min(512, S)
    row = pl.BlockSpec((th, D_MODEL), lambda i: (i, 0))
    vec = pl.BlockSpec((1, D_MODEL), lambda i: (0, 0))
    return _pcall(
        body, name=name, grid=(S // th,),
        in_specs=[pl.BlockSpec((th, k), lambda i: (i, 0)) for k in ks]
        + [pl.BlockSpec(wt.shape, lambda i: (0, 0), pipeline_mode=pl.Buffered(1)), row, row, vec]
        + [pl.BlockSpec(memory_space=pl.ANY)] * len(deps),
        out_specs=[row, vec], out_shape=[_sds((S, D_MODEL)), _sds((1, D_MODEL))])(*dzs, wt, x, dres, g, *deps)


GLA_G = 8


def _gla_tile_common(k, flow, wfu, bf):
    f = _dot(flow, wfu) + bf
    la = _log_sigmoid(f) * (1.0 / GATE_TEMP)
    tri = _tri_matrix(True)
    ws, ds = [], []
    for g in range(k.shape[0] // CHUNK):
        b = _tri_dot(tri, la[g * CHUNK:(g + 1) * CHUNK])
        b_end = b[CHUNK - 1:CHUNK, :]
        ws.append(jnp.exp(b_end - b))
        ds.append(jnp.exp(b_end))
    w = jnp.concatenate(ws, axis=0)
    return f, w, k * w, ds


def _tri_matrix(lower):
    r = lax.broadcasted_iota(jnp.int32, (CHUNK, CHUNK), 0)
    c = lax.broadcasted_iota(jnp.int32, (CHUNK, CHUNK), 1)
    return jnp.where((r >= c) if lower else (r <= c), 1.0, 0.0).astype(BF16)


def _heads():
    return [(slice(h * GLA_HDK, (h + 1) * GLA_HDK), slice(h * GLA_HDV, (h + 1) * GLA_HDV)) for h in range(GLA_HEADS)]


def _gla_fwd(proj, wfu, bf, gn, deps=()):
    S = proj.shape[0]
    G = min(GLA_G, S // CHUNK)
    T = G * CHUNK
    nc = S // CHUNK

    def body(q_ref, k_ref, v_ref, go_ref, flow_ref, wfu_ref, bf_ref, gn_ref, *rest):
        ya_ref, st_ref, state = rest[len(deps):]
        @pl.when(pl.program_id(0) == 0)
        def _():
            state[...] = jnp.zeros_like(state)

        _, _, kt, ds = _gla_tile_common(k_ref[...], flow_ref[...], wfu_ref[...], bf_ref[...])
        q = q_ref[...].astype(F32) * (GLA_HDK ** -0.5)
        v = v_ref[...]
        rows = [slice(g * CHUNK, (g + 1) * CHUNK) for g in range(G)]
        kv = [[_dot(v[r, vs], kt[r, ks], _TN) for ks, vs in _heads()] for r in rows]
        st = [state[vs, :] for _, vs in _heads()]
        o = []
        for g, r in enumerate(rows):
            outs = []
            for h, (ks, vs) in enumerate(_heads()):
                st[h] = st[h] * ds[g][:, ks] + kv[g][h]
                st_ref[g, vs, :] = st[h]
                outs.append(_dot(q[r, ks], st[h], _NT))
            o.append(jnp.concatenate(outs, axis=-1))
        for h, (_, vs) in enumerate(_heads()):
            state[vs, :] = st[h]
        ya_ref[...] = _head_rms_gate(jnp.concatenate(o, axis=0), go_ref[...], gn_ref[...]).astype(BF16)

    return _pcall(
        body, name="gla_fwd", grid=(S // T,),
        in_specs=[pl.BlockSpec((T, GLA_DK), lambda c: (c, 0)), pl.BlockSpec((T, GLA_DK), lambda c: (c, 1)),
                  pl.BlockSpec((T, GLA_DV), lambda c: (c, 1)), pl.BlockSpec((T, GLA_DV), lambda c: (c, 2)),
                  pl.BlockSpec((T, FLOW_W), lambda c: (c, (INT_W - FLOW_W) // FLOW_W)),
                  pl.BlockSpec(wfu.shape, lambda c: (0, 0)), pl.BlockSpec(bf.shape, lambda c: (0, 0)),
                  pl.BlockSpec(gn.shape, lambda c: (0, 0))]
        + [pl.BlockSpec(memory_space=pl.ANY)] * len(deps),
        out_specs=[pl.BlockSpec((T, GLA_DV), lambda c: (c, 0)),
                   pl.BlockSpec((G, GLA_DV, GLA_HDK), lambda c: (c, 0, 0))],
        out_shape=[_sds((S, GLA_DV), BF16), _sds((nc, GLA_DV, GLA_HDK))],
        scratch=[pltpu.VMEM((GLA_DV, GLA_HDK), F32)])(proj, proj, proj, proj, proj, wfu, bf, gn, *deps)


def _gla_bwd(proj, wfu, bf, gn, states, d_ya_in, dmain):
    S = proj.shape[0]
    G = min(GLA_G, S // CHUNK)
    T = G * CHUNK
    nt = S // T

    def body(q_ref, k_ref, v_ref, go_ref, flow_ref, wfu_ref, bf_ref, gn_ref, st_ref, stp_ref, dya_ref, dmain_in,
             dqkvg_ref, dflow_ref, dwfu_ref, dbf_ref, dgn_ref, dstate):
        dq_ref = dqkvg_ref.at[:, 0:GLA_DK]
        dk_ref = dqkvg_ref.at[:, GLA_DK:2 * GLA_DK]
        dv_ref = dqkvg_ref.at[:, 2 * GLA_DK:2 * GLA_DK + GLA_DV]
        dgo_ref = dqkvg_ref.at[:, 2 * GLA_DK + GLA_DV:]
        step = pl.program_id(0)

        @pl.when(step == 0)
        def _():
            dstate[...] = jnp.zeros_like(dstate)
            dwfu_ref[...] = jnp.zeros_like(dwfu_ref)
            dbf_ref[...] = jnp.zeros_like(dbf_ref)
            dgn_ref[...] = jnp.zeros_like(dgn_ref)

        flow, wfu_v = flow_ref[...], wfu_ref[...]
        f, w, kt, ds = _gla_tile_common(k_ref[...], flow, wfu_v, bf_ref[...])
        q = q_ref[...].astype(F32) * (GLA_HDK ** -0.5)
        v = v_ref[...]
        rows = [slice(g * CHUNK, (g + 1) * CHUNK) for g in range(G)]
        o = jnp.concatenate([jnp.concatenate([_dot(q[r, ks], st_ref[g, vs, :], _NT) for ks, vs in _heads()], axis=-1)
                             for g, r in enumerate(rows)], axis=0)
        _, vjp = jax.vjp(_head_rms_gate, o, go_ref[...], gn_ref[...])
        dout, dgo, dgn = vjp(dya_ref[...].astype(F32))
        dgo_ref[...] = dgo.astype(dgo_ref.dtype)
        dgn_ref[...] += dgn
        dq = [jnp.concatenate([_dot(dout[r, vs], st_ref[g, vs, :]) for _, vs in _heads()], axis=-1)
              for g, r in enumerate(rows)]
        qdo = [[_dot(dout[r, vs], q[r, ks], _TN) for ks, vs in _heads()] for r in rows]
        dq_ref[...] = (jnp.concatenate(dq, axis=0) * (GLA_HDK ** -0.5)).astype(dq_ref.dtype)
        has_prev = (step < nt - 1).astype(F32)
        carry = [dstate[vs, :] for _, vs in _heads()]
        dkt, dv, dd = [None] * G, [None] * G, [None] * G
        for g in reversed(range(G)):
            r = rows[g]
            dkts, dvs, dds = [], [], []
            for h, (ks, vs) in enumerate(_heads()):
                dst = carry[h] + qdo[g][h]
                dkts.append(_dot(v[r, vs], dst))
                dvs.append(_dot(kt[r, ks], dst, _NT))
                st_prev = st_ref[g - 1, vs, :] if g > 0 else stp_ref[vs, :] * has_prev
                dds.append(jnp.sum(dst * st_prev, axis=0, keepdims=True))
                carry[h] = dst * ds[g][:, ks]
            dkt[g], dv[g], dd[g] = (jnp.concatenate(t, axis=-1) for t in (dkts, dvs, dds))
        for h, (_, vs) in enumerate(_heads()):
            dstate[vs, :] = carry[h]
        dkt = jnp.concatenate(dkt, axis=0)
        dv_ref[...] = jnp.concatenate(dv, axis=0).astype(dv_ref.dtype)
        dk_ref[...] = (dkt * w).astype(dk_ref.dtype)
        de = dkt * kt
        tri = _tri_matrix(False)
        dla = []
        for g, r in enumerate(rows):
            db_end = jnp.sum(de[r], axis=0, keepdims=True) + dd[g] * ds[g]
            dla.append(db_end - _tri_dot(tri, de[r]))
        df = jnp.concatenate(dla, axis=0) * (1.0 - _sigmoid(f)) * (1.0 / GATE_TEMP)
        dflow_ref[...] = _dot(df, wfu_v, _NT).astype(dflow_ref.dtype)
        dwfu_ref[...] += _dot(flow, df, _TN)
        dbf_ref[...] += jnp.sum(df, axis=0, keepdims=True)

    rt = lambda s: nt - 1 - s
    return _pcall(
        body, name="gla_bwd", grid=(nt,),
        in_specs=[pl.BlockSpec((T, GLA_DK), lambda s: (rt(s), 0)), pl.BlockSpec((T, GLA_DK), lambda s: (rt(s), 1)),
                  pl.BlockSpec((T, GLA_DV), lambda s: (rt(s), 1)), pl.BlockSpec((T, GLA_DV), lambda s: (rt(s), 2)),
                  pl.BlockSpec((T, FLOW_W), lambda s: (rt(s), (INT_W - FLOW_W) // FLOW_W)),
                  pl.BlockSpec(wfu.shape, lambda s: (0, 0)), pl.BlockSpec(bf.shape, lambda s: (0, 0)),
                  pl.BlockSpec(gn.shape, lambda s: (0, 0)),
                  pl.BlockSpec((G, GLA_DV, GLA_HDK), lambda s: (rt(s), 0, 0)),
                  pl.BlockSpec((None, GLA_DV, GLA_HDK), lambda s: (jnp.maximum(rt(s) * G - 1, 0), 0, 0)),
                  pl.BlockSpec((T, GLA_DV), lambda s: (rt(s), 0)), pl.BlockSpec(memory_space=pl.ANY)],
        out_specs=[pl.BlockSpec((T, 2 * GLA_DK + 2 * GLA_DV), lambda s: (rt(s), 0)),
                   pl.BlockSpec((T, FLOW_W), lambda s: (rt(s), 0)),
                   pl.BlockSpec(wfu.shape, lambda s: (0, 0)), pl.BlockSpec(bf.shape, lambda s: (0, 0)),
                   pl.BlockSpec(gn.shape, lambda s: (0, 0))],
        out_shape=[_sds(dmain.shape, dmain.dtype), _sds((S, FLOW_W), BF16), _sds(wfu.shape), _sds(bf.shape),
                   _sds(gn.shape)],
        scratch=[pltpu.VMEM((GLA_DV, GLA_HDK), F32)], aliases={11: 0},
    )(proj, proj, proj, proj, proj, wfu, bf, gn, states, states, d_ya_in, dmain)


def _pool_counts(tm, i):
    t = (lax.broadcasted_iota(jnp.int32, (tm, POOL_GD), 0) + i * tm + 1).astype(F32)
    return [jnp.minimum(t, float(w)) for w in POOL_WINDOWS]


def _pool_fwd(proj, w_pool, pool_scale, tm=512):
    S = proj.shape[0]
    tm = min(tm, S // 2)
    col = _POOL_COL
    hb = tm // POOL_HALO

    def body(p_ref, halo_ref, wp_ref, sc_ref, mixed_ref, out_ref):
        i = pl.program_id(0)
        p = p_ref[...].astype(F32)
        halo = halo_ref[...].astype(F32) * (i > 0).astype(F32)
        ext = jnp.concatenate([halo, p], axis=0)
        n = tm + POOL_HALO
        sums, acc, k = {}, ext, 1
        while k < POOL_WINDOWS[-1]:
            acc = acc + pltpu.roll(acc, k, axis=0)
            k *= 2
            sums[k] = acc
        cnts = _pool_counts(tm, i)
        mixed, lin = [], []
        for g, w in enumerate(POOL_WINDOWS):
            ls = slice(g * POOL_GD, (g + 1) * POOL_GD)
            m = sums[w][POOL_HALO:n, ls] / cnts[g] - p[:, ls]
            mixed.append(m)
            lin.append(_dot(m, wp_ref[g]))
        mixed_ref[...] = jnp.concatenate(mixed, axis=-1)
        out_ref[...] = (jnp.concatenate(lin, axis=-1) * sc_ref[...]).astype(out_ref.dtype)

    return _pcall(
        body, name="pool_fwd", grid=(S // tm,),
        in_specs=[pl.BlockSpec((tm, POOL_W), lambda i: (i, col)),
                  pl.BlockSpec((POOL_HALO, POOL_W), lambda i: (jnp.maximum(i * hb - 1, 0), col)),
                  pl.BlockSpec(w_pool.shape, lambda i: (0, 0, 0)), pl.BlockSpec(pool_scale.shape, lambda i: (0, 0))],
        out_specs=[pl.BlockSpec((tm, POOL_W), lambda i: (i, 0)), pl.BlockSpec((tm, POOL_W), lambda i: (i, 0))],
        out_shape=[_sds((S, POOL_W)), _sds((S, POOL_W), BF16)])(proj, proj, w_pool, pool_scale)


def _pool_lin_bwd(dout, mixed, w_pool, pool_scale):
    S = dout.shape[0]

    def fn(dout, mixed, wp, sc):
        dlin = dout * sc
        dm, dwp, lin = [], [], []
        for g in range(len(POOL_WINDOWS)):
            ls = slice(g * POOL_GD, (g + 1) * POOL_GD)
            lin.append(_dot(mixed[:, ls], wp[g]))
            dm.append(_dot(dlin[:, ls], wp[g], _NT))
            dwp.append(_dot(mixed[:, ls], dlin[:, ls], _TN))
        dsc = jnp.sum(dout * jnp.concatenate(lin, axis=-1), axis=0, keepdims=True)
        return jnp.concatenate(dm, axis=-1), jnp.concatenate(dwp, axis=0), dsc

    return _rowwise("pool_lin_bwd", fn, [dout, mixed], [w_pool, pool_scale], [_sds((S, POOL_W))],
                    [_sds((len(POOL_WINDOWS) * POOL_GD, POOL_GD)), _sds((1, POOL_W))])


def _pool_win_bwd(dmixed, dmain, tm=512):
    S = dmixed.shape[0]
    tm = min(tm, S // 2)
    nt = S // tm
    hb = tm // POOL_HALO

    def body(dm_ref, halo_ref, dmain_in, dp_ref):
        i = pl.program_id(0)
        dm = dm_ref[...]
        halo = halo_ref[...] * (i < nt - 1).astype(F32)
        cnts = _pool_counts(tm, i)
        cnts_h = [c[:POOL_HALO] for c in _pool_counts(tm, i + 1)]
        r = jnp.concatenate([jnp.concatenate([dm[:, g * POOL_GD:(g + 1) * POOL_GD] / cnts[g] for g in range(4)], axis=-1),
                             jnp.concatenate([halo[:, g * POOL_GD:(g + 1) * POOL_GD] / cnts_h[g] for g in range(4)], axis=-1)],
                            axis=0)
        n = tm + POOL_HALO
        sums, acc, k = {}, r, 1
        while k < POOL_WINDOWS[-1]:
            acc = acc + pltpu.roll(acc, n - k, axis=0)
            k *= 2
            sums[k] = acc
        dp = [sums[w][:tm, g * POOL_GD:(g + 1) * POOL_GD] for g, w in enumerate(POOL_WINDOWS)]
        dp_ref[...] = (jnp.concatenate(dp, axis=-1) - dm).astype(dp_ref.dtype)

    return _pcall(
        body, name="pool_win_bwd", grid=(nt,),
        in_specs=[pl.BlockSpec((tm, POOL_W), lambda i: (i, 0)),
                  pl.BlockSpec((POOL_HALO, POOL_W), lambda i: (jnp.minimum((i + 1) * hb, S // POOL_HALO - 1), 0)),
                  pl.BlockSpec(memory_space=pl.ANY)],
        out_specs=pl.BlockSpec((tm, POOL_W), lambda i: (i, _POOL_COL)),
        out_shape=_sds(dmain.shape, dmain.dtype), aliases={2: 0})(dmixed, dmixed, dmain)


def _xattn_probs(q, kv, h):
    hs = slice(h * XA_HD, (h + 1) * XA_HD)
    s = _dot(q[:, hs], kv[:, hs], _NT) * (XA_HD ** -0.5)
    s = s - jnp.max(s, axis=-1, keepdims=True)
    e = jnp.exp(s)
    return e / jnp.sum(e, axis=-1, keepdims=True)


def _xattn_fwd(proj, kv):
    S = proj.shape[0]

    def fn(q, kv):
        outs = []
        for h in range(XA_HEADS):
            p = _xattn_probs(q, kv, h)
            outs.append(_dot(p, kv[:, XA_W + h * XA_HD:XA_W + (h + 1) * XA_HD]))
        return jnp.concatenate(outs, axis=-1)

    return _rowwise("xattn_fwd", fn, [_Win(proj, XA_W, _XQ_COL)], [kv], [_sds((S, XA_W), BF16)])[0]


def _xattn_bwd(proj, kv, dxa, dmain):
    def fn(q, dxa, kv):
        dqs, dks, dvs = [], [], []
        for h in range(XA_HEADS):
            hs = slice(h * XA_HD, (h + 1) * XA_HD)
            vh = kv[:, XA_W + h * XA_HD:XA_W + (h + 1) * XA_HD]
            p = _xattn_probs(q, kv, h)
            dp = _dot(dxa[:, hs], vh, _NT)
            ds = p * (dp - jnp.sum(p * dp, axis=-1, keepdims=True)) * (XA_HD ** -0.5)
            dqs.append(_dot(ds, kv[:, hs]))
            dks.append(_dot(ds, q[:, hs], _TN))
            dvs.append(_dot(p, dxa[:, hs], _TN))
        return jnp.concatenate(dqs, axis=-1), jnp.concatenate(dks + dvs, axis=-1)

    return _rowwise("xattn_bwd", fn, [_Win(proj, XA_W, _XQ_COL), dxa], [kv], [_Win(dmain, XA_W, _XQ_COL)],
                    [_sds(kv.shape)])


def _mix_fwd(x1, h, mem, W, g_next, on_proj=None, late_weights=None):
    S = x1.shape[0]
    M = mem.shape[0]
    tm = min(512, S)
    def proj_body(h_ref, w_ref, o_ref):
        hh = h_ref[...]
        for j in range(INT_NBLK):
            o_ref[:, INT_BLK * j:INT_BLK * (j + 1)] = _dot(hh, w_ref[INT_BLK * j:INT_BLK * (j + 1), :], _NT).astype(BF16)

    proj = _pcall(
        proj_body, name="mix_proj", grid=(S // tm,),
        in_specs=[pl.BlockSpec((tm, D_MODEL), lambda i: (i, 0)),
                  pl.BlockSpec((INT_W, D_MODEL), lambda i: (0, 0), pipeline_mode=pl.Buffered(1))],
        out_specs=pl.BlockSpec((tm, INT_W), lambda i: (i, 0)), out_shape=_sds((S, INT_W), BF16))(h, W["w_int_t"])
    ya_in, states = _gla_fwd(proj, W["w_fu_pad"], W["b_f"], W["gla_norm_g"],
                             on_proj(proj) if on_proj is not None else [])
    mixed, pool_out = _pool_fwd(proj, W["w_pool"], W["pool_scale"])
    mem_n = _rowwise("mem_norm", lambda m, g: _rms(m, g), [mem], [W["mem_norm_g"]], [_sds((M, D_MODEL), BF16)])[0]
    if late_weights is not None:
        W = {**W, **late_weights([pool_out])}
    kv = _mm_nn("mem_kv", mem_n, W["w_mem_kv"])
    xa = _xattn_fwd(proj, kv)
    def out_fn(gates, ya_in, pool_out, xa, x1, wa, wb, wc, wo, g, gn):
        ya, yb, yc = (_dot(t, w).astype(BF16) for t, w in ((ya_in, wa), (pool_out, wb), (xa, wc)))
        merged = _merge(gates, ya, yb, yc).astype(BF16)
        y = _dot(merged, wo)
        x2 = x1 + _rms(y, g)
        return ya, yb, yc, merged, y, x2, _rms(x2, gn)

    ya, yb, yc, merged, y, x2, h_next = _rowwise(
        "mix_out", out_fn, [_Win(proj, 3 * D_MODEL, 1), ya_in, pool_out, xa, x1],
        [W["w_up_gla"], W["w_up_pool"], W["w_up_xattn"], W["w_o"], W["mix_post_g"], g_next],
        [_sds((S, D_MODEL), BF16)] * 5 + [_sds((S, D_MODEL)), _sds((S, D_MODEL), BF16)], tm=256)
    return x2, h_next, W, (h, proj, states, ya_in, mixed, pool_out, mem_n, kv, xa, ya, yb, yc, merged, y)


def _mix_bwd(dx2, x1, mem, W, saved, on_grads=None):
    h, proj, states, ya_in, mixed, pool_out, mem_n, kv, xa, ya, yb, yc, merged, y = saved
    S = x1.shape[0]

    def out_bwd(y, dx2, gates, ya, yb, yc, g, wo, wa, wb, wc):
        dy, dg = _rms_bwd(y.astype(F32), g, dx2)
        dy = dy.astype(BF16)
        _, vjp = jax.vjp(_merge, gates, ya, yb, yc)
        dgates, dya, dyb, dyc = vjp(_dot(dy, wo, _NT))
        return (dy, dgates, dya, dyb, dyc, _dot(dya, wa, _NT), _dot(dyb, wb, _NT), _dot(dyc, wc, _NT), dg)

    dmain = lax.empty((S, INT_MAIN), BF16)
    dy, dmain, dya, dyb, dyc, d_ya_in, d_pool_out, d_xa, d_mix_post_g = _rowwise(
        "mix_out_bwd", out_bwd, [y, dx2, _Win(proj, 3 * D_MODEL, 1), ya, yb, yc],
        [W["mix_post_g"], W["w_o"], W["w_up_gla"], W["w_up_pool"], W["w_up_xattn"]],
        [_sds((S, D_MODEL), BF16), _Win(dmain, 3 * D_MODEL, 1)] + [_sds((S, D_MODEL), BF16)] * 4
        + [_sds((S, POOL_W), BF16), _sds((S, XA_W), BF16)], [_sds((1, D_MODEL))], tm=256)
    d_w_o = _mm_tn("d_w_o", merged, dy, ts=S, tn_a=256)
    d_w_up_gla = _mm_tn("d_w_up_gla", ya_in, dya, ts=S, tn_a=256)
    d_w_up_pool = _mm_tn("d_w_up_pool", pool_out, dyb, ts=S, tn_a=128)
    d_w_up_xattn = _mm_tn("d_w_up_xattn", xa, dyc, ts=S, tn_a=128)

    dmain, dflow, d_wfu_pad, d_b_f, d_gla_norm_g = _gla_bwd(proj, W["w_fu_pad"], W["b_f"], W["gla_norm_g"], states,
                                                            d_ya_in, dmain)
    dmixed, d_w_pool, d_pool_scale = _pool_lin_bwd(d_pool_out, mixed, W["w_pool"], W["pool_scale"])
    dmain = _pool_win_bwd(dmixed, dmain)
    dmain, dkv = _xattn_bwd(proj, kv, d_xa, dmain)
    d_w_mem_kv = _mm_tn("d_w_mem_kv", mem_n, dkv)
    dmem_n = _mm_nt("d_mem_n", dkv, W["w_mem_kv"])
    d_mem_norm_g = _rowwise("mem_norm_bwd", lambda m, d, g: _rms_bwd(m, g, d)[1], [mem, dmem_n], [W["mem_norm_g"]], [],
                            [_sds((1, D_MODEL))])[0]
    d_w_int_t = (_mm_tn("d_w_int", dmain, h, ts=S, tn_a=INT_MAIN // 8), _mm_tn("d_w_flow", dflow, h, ts=S))
    grads = dict(
        w_int_t=d_w_int_t, w_fu_pad=d_wfu_pad, b_f=d_b_f, gla_norm_g=d_gla_norm_g, w_pool=d_w_pool,
        pool_scale=d_pool_scale, mem_norm_g=d_mem_norm_g, w_mem_kv=d_w_mem_kv, w_up_gla=d_w_up_gla,
        w_up_pool=d_w_up_pool, w_up_xattn=d_w_up_xattn, w_o=d_w_o, mix_post_g=d_mix_post_g)
    deps = [on_grads(grads)] if on_grads is not None else []
    dx1, grads["mix_pre_g"] = _dh_pre_norm_bwd("d_mix_h", [dmain, dflow], W["w_int_t"], x1, dx2, W["mix_pre_g"], deps)
    return dx1, grads


def _mesh_pos():
    x, y, c = lax.axis_index("x"), lax.axis_index("y"), lax.axis_index("c")
    return x, y, c, 4 * x + 2 * y + c


def _peer(x, y, c, r):
    px = 1 - x if r & 4 else x
    py = 1 - y if r & 2 else y
    pc = 1 - c if r & 1 else c
    return (px, py, pc), 4 * px + 2 * py + pc


_ALL_PEERS = tuple(range(1, N_DEV))
_SIBLING = 1
_SAME_CORE = (2, 4, 6)


def _dev_slot(ref, dev):
    return ref.at[dev]


def _zero_pad_rows(land):
    pad = FF_PAD - FF_BLK

    def body(land_in, o_ref):
        o_ref[...] = jnp.zeros_like(o_ref)

    return _pcall(body, name="zero_pad_rows", grid=(land.shape[0],), in_specs=[pl.BlockSpec(memory_space=pl.ANY)],
                  out_specs=pl.BlockSpec((None, pad, D_MODEL), lambda j: (j, FF_BLK // pad, 0)),
                  out_shape=_sds(land.shape, land.dtype), aliases={0: 0})(land)


class _Plan:
    def __init__(self, scatter, slots=None, shapes=None):
        self.scatter, self.slots, self.shapes = scatter, slots or {}, shapes or {}

    def src(self, srcs, a, dev):
        return self.slots.get(a, _dev_slot)(srcs[a], dev) if self.scatter else srcs[a]

    def dst(self, lands, a, dev):
        return lands[a].at[dev] if self.scatter else self.slots.get(a, _dev_slot)(lands[a], dev)

    def landing_zones(self, arrays):
        lands = []
        for a, arr in enumerate(arrays):
            if self.scatter:
                lands.append(lax.empty((N_DEV,) + tuple(self.shapes.get(a, arr.shape[1:])), arr.dtype))
            elif a in self.shapes:
                lands.append(_zero_pad_rows(lax.empty(self.shapes[a], arr.dtype)))
            else:
                lands.append(lax.empty((N_DEV,) + arr.shape, arr.dtype))
        return lands


_GATHER = _Plan(False)


def _peer_copies(srcs, lands, send_sems, recv_sems, plan, peers=_ALL_PEERS):
    x, y, c, me = _mesh_pos()
    cps = []
    for r in peers:
        pos, peer = _peer(x, y, c, r)
        for a in range(len(srcs)):
            k = a * (N_DEV - 1) + r - 1
            cps.append(pltpu.make_async_remote_copy(
                src_ref=plan.src(srcs, a, peer), dst_ref=plan.dst(lands, a, me),
                send_sem=send_sems.at[k], recv_sem=recv_sems.at[k], device_id=pos,
                device_id_type=pl.DeviceIdType.MESH))
    return cps


_HBM = pl.BlockSpec(memory_space=pltpu.HBM)
_SEM = pl.BlockSpec(memory_space=pltpu.SEMAPHORE)
_EFFECT = pltpu.SideEffectType.DATAFLOW_SIDE_EFFECTING


def _own_copies(srcs, lands, own_sems, plan):
    me = _mesh_pos()[3]
    return [pltpu.make_async_copy(plan.src(srcs, a, me), plan.dst(lands, a, me), own_sems.at[a])
            for a in range(len(srcs))]


def _exchange_start(name, arrays, plan):
    n = len(arrays)
    lands = plan.landing_zones(arrays)
    n_sem = n * (N_DEV - 1)

    def body(*refs):
        srcs, lands_ = refs[:n], refs[n:2 * n]
        send_sems, recv_sems, own_sems = refs[2 * n:2 * n + 3]
        token = refs[-1]
        for cp in _peer_copies(srcs, lands_, send_sems, recv_sems, plan) + _own_copies(srcs, lands_, own_sems, plan):
            cp.start()
        token[...] = jnp.zeros_like(token)

    hbm = lambda a: pltpu.HBM(a.shape, a.dtype)
    res = pl.pallas_call(
        body, name=name,
        out_shape=(pltpu.SemaphoreType.DMA((n_sem,)), pltpu.SemaphoreType.DMA((n_sem,)), pltpu.SemaphoreType.DMA((n,)),
                   *[hbm(a) for a in arrays], *[hbm(a) for a in lands], _sds((8, 128))),
        in_specs=[_HBM] * (2 * n),
        out_specs=(_SEM, _SEM, _SEM, *[_HBM] * (2 * n), pl.BlockSpec(memory_space=pltpu.VMEM)),
        input_output_aliases={i: 3 + i for i in range(2 * n)},
        compiler_params=pltpu.CompilerParams(has_side_effects=_EFFECT),
    )(*[pltpu.with_memory_space_constraint(a, pltpu.HBM) for a in list(arrays) + lands])
    return (res[:3], res[3:3 + n], res[3 + n:3 + 2 * n], plan), res[-1]


def _exchange_wait(name, state, after):
    sems, srcs, lands, plan = state
    n = len(srcs)

    def body(*refs):
        srcs_, lands_ = refs[:n], refs[n:2 * n]
        send_sems, recv_sems, own_sems = refs[2 * n:2 * n + 3]
        for cp in _peer_copies(srcs_, lands_, send_sems, recv_sems, plan):
            cp.wait_send()
            cp.wait_recv()
        for cp in _own_copies(srcs_, lands_, own_sems, plan):
            cp.wait()

    hbm = lambda a: pltpu.HBM(a.shape, a.dtype)
    res = pl.pallas_call(
        body, name=name, out_shape=tuple(hbm(a) for a in list(srcs) + list(lands)),
        in_specs=[_HBM] * (2 * n) + [_SEM] * 3 + [pl.BlockSpec(memory_space=pl.ANY)] * len(after),
        out_specs=tuple([_HBM] * (2 * n)), input_output_aliases={i: i for i in range(2 * n)},
        compiler_params=pltpu.CompilerParams(has_side_effects=_EFFECT),
    )(*srcs, *lands, *sems, *after)
    return res[n:]


def _gather_start(name, arrays, plan, after):
    n = len(arrays)
    lands = plan.landing_zones(arrays)
    n_sem = n * (N_DEV - 1)

    def body(*refs):
        srcs, lands_ = refs[:n], refs[n:2 * n]
        send_sems, recv_sems, own_sems = refs[2 * n + len(after):2 * n + len(after) + 3]
        token = refs[-1]
        for cp in (_peer_copies(srcs, lands_, send_sems, recv_sems, plan, (_SIBLING,) + _SAME_CORE)
                   + _own_copies(srcs, lands_, own_sems, plan)):
            cp.start()
        token[...] = jnp.zeros_like(token)

    hbm = lambda a: pltpu.HBM(a.shape, a.dtype)
    res = pl.pallas_call(
        body, name=name,
        out_shape=(pltpu.SemaphoreType.DMA((n_sem,)), pltpu.SemaphoreType.DMA((n_sem,)), pltpu.SemaphoreType.DMA((n,)),
                   *[hbm(a) for a in arrays], *[hbm(a) for a in lands], _sds((8, 128))),
        in_specs=[_HBM] * (2 * n) + [pl.BlockSpec(memory_space=pl.ANY)] * len(after),
        out_specs=(_SEM, _SEM, _SEM, *[_HBM] * (2 * n), pl.BlockSpec(memory_space=pltpu.VMEM)),
        input_output_aliases={i: 3 + i for i in range(2 * n)},
        compiler_params=pltpu.CompilerParams(has_side_effects=_EFFECT),
    )(*[pltpu.with_memory_space_constraint(a, pltpu.HBM) for a in list(arrays) + lands], *after)
    return (res[:3], res[3:3 + n], res[3 + n:3 + 2 * n], plan), res[-1]


def _pass_on_copies(lands, send_sems, recv_sems, plan):
    x, y, c, _ = _mesh_pos()
    sibling = _peer(x, y, c, _SIBLING)[0]
    cps = []
    for i, r in enumerate(_SAME_CORE):
        owner = _peer(x, y, c, r)[1]
        for a in range(len(lands)):
            k = a * len(_SAME_CORE) + i
            cps.append(pltpu.make_async_remote_copy(
                src_ref=plan.dst(lands, a, owner), dst_ref=plan.dst(lands, a, owner), send_sem=send_sems.at[k],
                recv_sem=recv_sems.at[k], device_id=sibling, device_id_type=pl.DeviceIdType.MESH))
    return cps


def _gather_pass_on(name, state, after):
    sems, srcs, lands, plan = state
    n = len(srcs)
    n_sem = n * len(_SAME_CORE)

    def body(*refs):
        srcs_, lands_ = refs[:n], refs[n:2 * n]
        send_sems, recv_sems = refs[2 * n], refs[2 * n + 1]
        on_send, on_recv = refs[2 * n + 3 + len(after)], refs[2 * n + 4 + len(after)]
        token = refs[-1]
        arrivals = _peer_copies(srcs_, lands_, send_sems, recv_sems, plan, _SAME_CORE)
        for arrived, on in zip(arrivals, _pass_on_copies(lands_, on_send, on_recv, plan)):
            arrived.wait_recv()
            on.start()
        token[...] = jnp.zeros_like(token)

    hbm = lambda a: pltpu.HBM(a.shape, a.dtype)
    res = pl.pallas_call(
        body, name=name,
        out_shape=(pltpu.SemaphoreType.DMA((n_sem,)), pltpu.SemaphoreType.DMA((n_sem,)),
                   *[hbm(a) for a in list(srcs) + list(lands)], _sds((8, 128))),
        in_specs=[_HBM] * (2 * n) + [_SEM] * 3 + [pl.BlockSpec(memory_space=pl.ANY)] * len(after),
        out_specs=(_SEM, _SEM, *[_HBM] * (2 * n), pl.BlockSpec(memory_space=pltpu.VMEM)),
        input_output_aliases={i: 2 + i for i in range(2 * n)},
        compiler_params=pltpu.CompilerParams(has_side_effects=_EFFECT),
    )(*srcs, *lands, *sems, *after)
    return (sems, res[:2], res[2:2 + n], res[2 + n:2 + 2 * n], plan), res[-1]


def _gather_wait(name, state, after):
    sems, on_sems, srcs, lands, plan = state
    n = len(srcs)

    def body(*refs):
        srcs_, lands_ = refs[:n], refs[n:2 * n]
        send_sems, recv_sems, own_sems, on_send, on_recv = refs[2 * n:2 * n + 5]
        for cp in _peer_copies(srcs_, lands_, send_sems, recv_sems, plan, (_SIBLING,)):
            cp.wait_recv()
        for cp in _peer_copies(srcs_, lands_, send_sems, recv_sems, plan, (_SIBLING,) + _SAME_CORE):
            cp.wait_send()
        for cp in _own_copies(srcs_, lands_, own_sems, plan):
            cp.wait()
        for cp in _pass_on_copies(lands_, on_send, on_recv, plan):
            cp.wait_send()
            cp.wait_recv()

    hbm = lambda a: pltpu.HBM(a.shape, a.dtype)
    res = pl.pallas_call(
        body, name=name, out_shape=tuple(hbm(a) for a in list(srcs) + list(lands)),
        in_specs=[_HBM] * (2 * n) + [_SEM] * 5 + [pl.BlockSpec(memory_space=pl.ANY)] * len(after),
        out_specs=tuple([_HBM] * (2 * n)), input_output_aliases={i: i for i in range(2 * n)},
        compiler_params=pltpu.CompilerParams(has_side_effects=_EFFECT),
    )(*srcs, *lands, *sems, *on_sems, *after)
    return res[n:]


def _sum_parts(name, recv):
    def body(r_ref, o_ref):
        s = r_ref[0]
        for j in range(1, N_DEV):
            s = s + r_ref[j]
        o_ref[...] = s

    return pl.pallas_call(body, name=name, out_shape=_sds(recv.shape[1:], recv.dtype))(recv)


def _adam(name, recv, w, m, v, col=0):
    shape = w.shape
    R, C = shape
    tr = R
    while N_DEV * tr * C * 4 > 6 * 1024 * 1024 and tr % 32 == 0:
        tr //= 2

    def body(recv_ref, w_ref, m_ref, v_ref, g_ref, d_ref, m2_ref, v2_ref):
        g = recv_ref[0].astype(F32)
        for j in range(1, N_DEV):
            g = g + recv_ref[j].astype(F32)
        w_, m_, v_ = w_ref[...], m_ref[...], v_ref[...]
        m2 = ADAM_B1 * m_ + (1.0 - ADAM_B1) * g
        v2 = ADAM_B2 * v_ + (1.0 - ADAM_B2) * (g * g)
        m_hat = m2 / (1.0 - ADAM_B1 ** ADAM_STEP)
        v_hat = v2 / (1.0 - ADAM_B2 ** ADAM_STEP)
        g_ref[...] = g
        d_ref[...] = -ADAM_LR * (m_hat / (jnp.sqrt(v_hat) + ADAM_EPS) + ADAM_WD * w_)
        m2_ref[...] = m2
        v2_ref[...] = v2

    blk = pl.BlockSpec((tr, C), lambda i: (i, 0))
    return _pcall(body, name=name, grid=(R // tr,),
                  in_specs=[pl.BlockSpec((N_DEV, tr, C), lambda i: (0, i, col)), blk, blk, blk],
                  out_specs=[blk] * 4, out_shape=[_sds(shape)] * 4)(recv, w, m, v)


_NAMES = ['ffn1_pre_g', 'ffn1_w_in', 'ffn1_w_out', 'ffn1_post_g', 'mix_pre_g', 'w_in', 'w_fu', 'b_f', 'gla_norm_g',
          'w_pool', 'pool_scale', 'mem_norm_g', 'w_mem_kv', 'w_up_gla', 'w_up_pool', 'w_up_xattn', 'w_o', 'mix_post_g',
          'ffn2_pre_g', 'ffn2_w_in', 'ffn2_w_out', 'ffn2_post_g', 'final_g']
_SHARDED = ['ffn1_w_in', 'ffn1_w_out', 'w_in', 'w_fu', 'w_mem_kv', 'w_up_gla', 'w_up_pool', 'w_up_xattn', 'w_o',
            'ffn2_w_in', 'ffn2_w_out']
_COL_SHARDED = ['w_up_pool', 'w_up_xattn']


def _cols_to_full(g):
    return jnp.transpose(g, (1, 0, 2)).reshape(g.shape[1], N_DEV * g.shape[2])


def _full_to_cols(f):
    R, C = f.shape
    return jnp.transpose(f.reshape(R, N_DEV, C // N_DEV), (1, 0, 2))


def _to_internal(w_in_t):
    o = 0
    parts = []
    for s in IN_SPLITS:
        parts.append(w_in_t[o:o + s])
        o += s
    q, k, v, g_out, f_low, p_in, xq, gates = parts
    f_low = jnp.pad(f_low, ((0, FLOW_W - GATE_RANK), (0, 0)))
    return jnp.concatenate([q, k, v, g_out, gates, p_in, xq, f_low], axis=0)


def _from_internal(d, d_flow):
    q, k, v, g_out = d[0:512], d[512:1024], d[1024:2048], d[2048:3072]
    gates, p_in, xq = d[3072:6144], d[6144:6656], d[6656:7168]
    return jnp.concatenate([q, k, v, g_out, d_flow[:GATE_RANK], p_in, xq, gates], axis=0)


_FFN_IN = ('ffn1_w_in', 'ffn2_w_in')
_FFN_OUT = ('ffn1_w_out', 'ffn2_w_out')
_GATHERS = {"ffn1_in": ['ffn1_w_in'], "ffn1_out": ['ffn1_w_out'],
            "mix_in": ['w_in', 'w_fu'], "mix_rest": ['w_mem_kv', 'w_up_gla', 'w_up_pool', 'w_up_xattn', 'w_o'],
            "ffn2": ['ffn2_w_in', 'ffn2_w_out']}
_MIX = _GATHERS["mix_in"] + _GATHERS["mix_rest"]


def _ffn_in_slot(ref, d):
    return ref.at[2 * (d % N_FF_BLK) + d // N_FF_BLK, pl.ds(0, FF_BLK)]


def _ffn_out_slot(ref, d):
    rows = FF_BLK // 2
    return ref.at[d // 2, pl.ds(pl.multiple_of((d % 2) * rows, rows), rows)]


def _ffn_plan(names, scatter):
    slots, shapes = {}, {}
    for a, n in enumerate(names):
        if n in _FFN_IN:
            slots[a] = _ffn_in_slot
            shapes[a] = (FF_BLK, D_MODEL) if scatter else (N_DEV, FF_PAD, D_MODEL)
        elif n in _FFN_OUT:
            slots[a] = _ffn_out_slot
            shapes[a] = (FF_BLK // 2, D_MODEL) if scatter else (N_FF_BLK, FF_PAD, D_MODEL)
    return _Plan(scatter, slots, shapes)


def _rows_to_full(g):
    return g.reshape(N_DEV * g.shape[1], g.shape[2])


def _mix_weights(gathered):
    W = {}
    for n, g in gathered.items():
        if n == "w_in":
            W["w_int_t"] = _to_internal(g.reshape(IN_WIDTH, D_MODEL))
        elif n == "w_fu":
            W["w_fu_pad"] = jnp.pad(_cols_to_full(g), ((0, FLOW_W - GATE_RANK), (0, 0)))
        else:
            W[n] = _cols_to_full(g) if n in _COL_SHARDED else _rows_to_full(g)
    return W


def _mix_chunks(G, n):
    if n == "w_in":
        return _from_internal(*G["w_int_t"]).reshape(N_DEV, IN_SHARD, D_MODEL)
    if n == "w_fu":
        return _full_to_cols(G["w_fu_pad"][:GATE_RANK].astype(BF16))
    if n in _COL_SHARDED:
        return _full_to_cols(G[n])
    return G[n].reshape(N_DEV, G[n].shape[0] // N_DEV, G[n].shape[1])


def _step(x, mem, tgt, P, Mo, Vo):
    def native(n, a):
        return jnp.swapaxes(a, 0, 1) if n in _FFN_IN + ("w_in",) else a

    P, Mo, Vo = ({n: native(n, a) for n, a in d.items()} for d in (P, Mo, Vo))
    small = {n: P[n] for n in _NAMES if n not in _SHARDED}

    gather, tokens = {}, []
    for grp, names in _GATHERS.items():
        gather[grp], tok = _gather_start("gather_" + grp, [P[n].astype(BF16) for n in names], _ffn_plan(names, False),
                                         tokens[-1:])
        tokens.append(tok)

    def pass_on(grp, after):
        return _gather_pass_on("gather_" + grp + "_on", gather[grp], after)

    def gathered(grp, after):
        return _gather_wait("gather_" + grp + "_wait", pass_on(grp, after)[0], [])

    def ffn_wo(lands):
        return lambda act: lands[0].reshape(N_FF_BLK * FF_PAD, D_MODEL)

    h1 = _pre_norm("ffn1_pre", x, small["ffn1_pre_g"], tokens[:1])
    w1t = gathered("ffn1_in", tokens[-1:] + [h1])[0].reshape(N_DEV * FF_PAD, D_MODEL)
    (x1, hm), w1o, sv1 = _ffn_fwd("ffn1", x, h1, w1t, small["ffn1_post_g"],
                                  lambda act: ffn_wo(gathered("ffn1_out", [act]))(act), small["mix_pre_g"])

    Wm = {**small, **_mix_weights(dict(zip(_GATHERS["mix_in"], gathered("mix_in", [x1]))))}
    passed = {}

    def pass_on_later(proj):
        tokens = []
        for grp in ("mix_rest", "ffn2"):
            passed[grp], tok = pass_on(grp, [proj])
            tokens.append(tok)
        return tokens

    def mix_rest(after):
        names = _GATHERS["mix_rest"]
        return _mix_weights(dict(zip(names, _gather_wait("gather_mix_rest_wait", passed["mix_rest"], after))))

    x2, h2, Wm, svm = _mix_fwd(x1, hm, mem, Wm, small["ffn2_pre_g"], pass_on_later, mix_rest)
    w2t, w2o = _gather_wait("gather_ffn2_wait", passed["ffn2"], [x2])
    w2t = w2t.reshape(N_DEV * FF_PAD, D_MODEL)
    (loss, dx3, d_final_g), w2o, sv2 = _ffn_fwd("ffn2", x2, h2, w2t, small["ffn2_post_g"], ffn_wo([w2o]),
                                                small["final_g"], tgt)

    G = dict(final_g=d_final_g)
    scat = {}

    def start(grp, names, arrays):
        scat[grp] = names, _exchange_start("scatter_" + grp, arrays, _ffn_plan(names, True))
        return scat[grp][1][1]

    def ffn_starts(tag):
        return (lambda dwo: start(tag + "_out", [tag + "_w_out"], [dwo.reshape(N_FF_BLK, FF_PAD, D_MODEL)]),
                lambda dwt: start(tag + "_in", [tag + "_w_in"], [dwt.reshape(N_DEV, FF_PAD, D_MODEL)]))

    dx2, G["ffn2_pre_g"], G["ffn2_post_g"] = _ffn_bwd(
        "ffn2", dx3, x2, small["ffn2_pre_g"], w2t, w2o, small["ffn2_post_g"], sv2, *ffn_starts("ffn2"))
    dx1, Gm = _mix_bwd(dx2, x1, mem, Wm, svm,
                       lambda Gm: start("mix", _MIX, [_mix_chunks(Gm, n) for n in _MIX]))
    G.update(Gm)
    dx, G["ffn1_pre_g"], G["ffn1_post_g"] = _ffn_bwd(
        "ffn1", dx1, x, small["ffn1_pre_g"], w1t, w1o, small["ffn1_post_g"], sv1, *ffn_starts("ffn1"))

    vecs = sorted((n for n in small if P[n].ndim == 2), key=lambda n: -P[n].shape[1])
    packed = jnp.concatenate([G[n] for n in vecs] + [loss], axis=1)
    place, o = {}, 0
    for n in vecs:
        place[n] = o // P[n].shape[1]
        o += P[n].shape[1]
    scat["small"] = ["packed", "w_pool"], _exchange_start("gather_small_grads", [packed, G["w_pool"]], _GATHER)
    recv, outs = {}, {}
    done = [dx]
    for grp in ["ffn2_out", "ffn2_in", "mix", "ffn1_out", "ffn1_in", "small"]:
        names, (state, _) = scat[grp]
        recv.update(zip(names, _exchange_wait(("gather_" if grp == "small" else "scatter_") + grp + "_wait", state,
                                              done)))
        if grp == "small":
            loss = _sum_parts("loss_sum", recv["packed"][:, :, o:])[0, 0]
            names = list(small)
        for n in names:
            shp = P[n].shape
            shp2 = shp if len(shp) == 2 else (shp[0] * shp[1], shp[2])
            r = recv["packed"] if n in place else recv[n].reshape((N_DEV,) + shp2)
            res = _adam("adam_" + n, r, P[n].reshape(shp2), Mo[n].reshape(shp2), Vo[n].reshape(shp2), place.get(n, 0))
            outs[n] = [native(n, t.reshape(shp)) for t in res]
            done.append(res[-1])
    return loss, dx, outs


def kernel(x, mem, ffn1_pre_g, ffn1_w_in, ffn1_w_out, ffn1_post_g, mix_pre_g, w_in, w_fu, b_f, gla_norm_g, w_pool, pool_scale, mem_norm_g, w_mem_kv, w_up_gla, w_up_pool, w_up_xattn, w_o, mix_post_g, ffn2_pre_g, ffn2_w_in, ffn2_w_out, ffn2_post_g, final_g, loss_target, m_ffn1_pre_g, m_ffn1_w_in, m_ffn1_w_out, m_ffn1_post_g, m_mix_pre_g, m_w_in, m_w_fu, m_b_f, m_gla_norm_g, m_w_pool, m_pool_scale, m_mem_norm_g, m_w_mem_kv, m_w_up_gla, m_w_up_pool, m_w_up_xattn, m_w_o, m_mix_post_g, m_ffn2_pre_g, m_ffn2_w_in, m_ffn2_w_out, m_ffn2_post_g, m_final_g, v_ffn1_pre_g, v_ffn1_w_in, v_ffn1_w_out, v_ffn1_post_g, v_mix_pre_g, v_w_in, v_w_fu, v_b_f, v_gla_norm_g, v_w_pool, v_pool_scale, v_mem_norm_g, v_w_mem_kv, v_w_up_gla, v_w_up_pool, v_w_up_xattn, v_w_o, v_mix_post_g, v_ffn2_pre_g, v_ffn2_w_in, v_ffn2_w_out, v_ffn2_post_g, v_final_g):
    params = [ffn1_pre_g, ffn1_w_in, ffn1_w_out, ffn1_post_g, mix_pre_g, w_in, w_fu, b_f, gla_norm_g, w_pool, pool_scale, mem_norm_g, w_mem_kv, w_up_gla, w_up_pool, w_up_xattn, w_o, mix_post_g, ffn2_pre_g, ffn2_w_in, ffn2_w_out, ffn2_post_g, final_g]
    moms = [m_ffn1_pre_g, m_ffn1_w_in, m_ffn1_w_out, m_ffn1_post_g, m_mix_pre_g, m_w_in, m_w_fu, m_b_f, m_gla_norm_g, m_w_pool, m_pool_scale, m_mem_norm_g, m_w_mem_kv, m_w_up_gla, m_w_up_pool, m_w_up_xattn, m_w_o, m_mix_post_g, m_ffn2_pre_g, m_ffn2_w_in, m_ffn2_w_out, m_ffn2_post_g, m_final_g]
    vars_ = [v_ffn1_pre_g, v_ffn1_w_in, v_ffn1_w_out, v_ffn1_post_g, v_mix_pre_g, v_w_in, v_w_fu, v_b_f, v_gla_norm_g, v_w_pool, v_pool_scale, v_mem_norm_g, v_w_mem_kv, v_w_up_gla, v_w_up_pool, v_w_up_xattn, v_w_o, v_mix_post_g, v_ffn2_pre_g, v_ffn2_w_in, v_ffn2_w_out, v_ffn2_post_g, v_final_g]
    P = {n: a[0] if a.ndim > 2 else a for n, a in zip(_NAMES, params)}
    Mo = {n: a[0] if a.ndim > 2 else a for n, a in zip(_NAMES, moms)}
    Vo = {n: a[0] if a.ndim > 2 else a for n, a in zip(_NAMES, vars_)}
    loss, dx, outs = _step(x[0], mem[0], loss_target[0], P, Mo, Vo)
    out = [loss, dx[None]]
    for kind in range(4):
        for n, p in zip(_NAMES, params):
            out.append(outs[n][kind].reshape(p.shape))
    return tuple(out)
```

```python
import functools

import jax
import jax.numpy as jnp
from jax import lax
from jax.experimental import pallas as pl
from jax.experimental.pallas import tpu as pltpu

F32 = jnp.float32
BF16 = jnp.bfloat16

N_DEV = 8
D_MODEL = 1024
D_FF = 2816
FF_BLK = 2 * D_FF // N_DEV
N_FF_BLK = D_FF // FF_BLK
FF_PAD = 768
CHUNK = 64
GLA_HEADS = 4
GLA_DK = 512
GLA_DV = 1024
GLA_HDK = 128
GLA_HDV = 256
GATE_RANK = 16
GATE_TEMP = 16.0
POOL_WINDOWS = (2, 4, 8, 16)
POOL_W = 512
POOL_GD = 128
POOL_HALO = 16
XA_HEADS = 4
XA_HD = 128
XA_W = 512
EPS = 1e-6
IN_SPLITS = (GLA_DK, GLA_DK, GLA_DV, GLA_DV, GATE_RANK, POOL_W, XA_W, 3 * D_MODEL)
IN_WIDTH = sum(IN_SPLITS)
IN_SHARD = IN_WIDTH // N_DEV
INT_W = 3072 + 3072 + 1024 + 128
INT_NBLK = 3
INT_BLK = INT_W // INT_NBLK
FLOW_W = 128
INT_MAIN = INT_W - FLOW_W
_POOL_COL = (3072 + 3072) // POOL_W
_XQ_COL = (3072 + 3072 + POOL_W) // XA_W

ADAM_LR = 0.001
ADAM_B1 = 0.9
ADAM_B2 = 0.999
ADAM_EPS = 1e-08
ADAM_WD = 0.01
ADAM_STEP = 10

VMEM_LIMIT = 56 * 1024 * 1024

_NN = (((1,), (0,)), ((), ()))
_NT = (((1,), (1,)), ((), ()))
_TN = (((0,), (0,)), ((), ()))


def _pcall(body, *, name, grid, in_specs, out_specs, out_shape, scratch=(), aliases=None):
    return pl.pallas_call(
        body, name=name, grid=grid, in_specs=in_specs, out_specs=out_specs, out_shape=out_shape,
        scratch_shapes=list(scratch), input_output_aliases=aliases or {},
        compiler_params=pltpu.CompilerParams(dimension_semantics=("arbitrary",) * len(grid),
                                             vmem_limit_bytes=VMEM_LIMIT))


def _dot(a, b, dims=_NN):
    return lax.dot_general(a.astype(BF16), b.astype(BF16), dims, preferred_element_type=F32)


def _mm(name, a, b, *, grid, a_blk, a_map, b_blk, b_map, o_shape, o_blk, o_map, dims, out_dtype=F32, deps=()):
    nk = grid[2]

    def body(a_ref, b_ref, *rest):
        o_ref, scr = rest[len(deps)], rest[len(deps) + 1:]
        p = _dot(a_ref[...], b_ref[...], dims)
        if nk == 1:
            o_ref[...] = p.astype(o_ref.dtype)
        else:
            acc = scr[0]
            k = pl.program_id(2)

            @pl.when(k == 0)
            def _():
                acc[...] = p

            @pl.when(k > 0)
            def _():
                acc[...] += p

            @pl.when(k == nk - 1)
            def _():
                o_ref[...] = acc[...].astype(o_ref.dtype)

    acc_shape = tuple(d for d in o_blk if d is not None)
    return _pcall(body, name=name, grid=grid,
                  in_specs=[pl.BlockSpec(a_blk, a_map), pl.BlockSpec(b_blk, b_map)]
                  + [pl.BlockSpec(memory_space=pl.ANY)] * len(deps),
                  out_specs=pl.BlockSpec(o_blk, o_map),
                  out_shape=jax.ShapeDtypeStruct(o_shape, out_dtype),
                  scratch=[pltpu.VMEM(acc_shape, F32)] if nk > 1 else [])(a, b, *deps)


def _mm_nn(name, a, b, out_dtype=F32, tm=1024, tn=None):
    M, K = a.shape
    N = b.shape[1]
    tm, tn = min(tm, M), (tn or N)
    return _mm(name, a, b, grid=(N // tn, M // tm, 1), a_blk=(tm, K), a_map=lambda j, i, k: (i, 0),
               b_blk=(K, tn), b_map=lambda j, i, k: (0, j), o_shape=(M, N), o_blk=(tm, tn),
               o_map=lambda j, i, k: (i, j), dims=_NN, out_dtype=out_dtype)


def _mm_nt(name, a, b, out_dtype=F32, tm=1024):
    M, K = a.shape
    N = b.shape[0]
    tm = min(tm, M)
    return _mm(name, a, b, grid=(1, M // tm, 1), a_blk=(tm, K), a_map=lambda j, i, k: (i, 0),
               b_blk=(N, K), b_map=lambda j, i, k: (0, 0), o_shape=(M, N), o_blk=(tm, N),
               o_map=lambda j, i, k: (i, 0), dims=_NT, out_dtype=out_dtype)


def _mm_tn(name, a, b, out_dtype=BF16, ts=512, tn=None, tn_a=None, deps=()):
    S, M = a.shape
    N = b.shape[1]
    ts, tn, tn_a = min(ts, S), (tn or N), (tn_a or M)
    return _mm(name, a, b, grid=((N // tn) * (M // tn_a), 1, S // ts), a_blk=(ts, tn_a),
               a_map=lambda j, i, k: (k, j if tn_a < M else 0), b_blk=(ts, tn),
               b_map=lambda j, i, k: (k, j if tn < N else 0), o_shape=(M, N), o_blk=(tn_a, tn),
               o_map=lambda j, i, k: (j, 0) if tn_a < M else (0, j), dims=_TN, out_dtype=out_dtype, deps=deps)


class _Win:
    def __init__(self, arr, w, c):
        self.arr, self.w, self.c = arr, w, c


def _row_spec(x, tm):
    if isinstance(x, _Win):
        return x.arr, pl.BlockSpec((tm, x.w), functools.partial(lambda i, c: (i, c), c=x.c))
    if x.ndim == 3:
        return x, pl.BlockSpec((x.shape[0], tm, x.shape[2]), lambda i: (0, i, 0))
    return x, pl.BlockSpec((tm, x.shape[1]), lambda i: (i, 0))


def _rowwise(name, fn, rows, consts, outs, accs=(), tm=512):
    first = rows[0].arr if isinstance(rows[0], _Win) else rows[0]
    S = first.shape[1] if first.ndim == 3 else first.shape[0]
    tm = min(tm, S)
    n_in, n_out = len(rows) + len(consts), len(outs)
    arrays, in_specs = [], []
    for r in rows:
        arr, spec = _row_spec(r, tm)
        arrays.append(arr)
        in_specs.append(spec)
    for c in consts:
        arrays.append(c)
        in_specs.append(pl.BlockSpec(c.shape, functools.partial(lambda i, n: (0,) * n, n=c.ndim)))
    aliases = {}
    for k, o in enumerate(outs):
        if isinstance(o, _Win):
            aliases[len(arrays)] = k
            arrays.append(o.arr)
            in_specs.append(pl.BlockSpec(memory_space=pl.ANY))
    n_thru = len(aliases)
    out_specs = [_row_spec(o, tm)[1] for o in outs]
    out_specs += [pl.BlockSpec(a.shape, functools.partial(lambda i, n: (0,) * n, n=len(a.shape))) for a in accs]
    out_shape = [_sds(o.arr.shape, o.arr.dtype) if isinstance(o, _Win) else _sds(o.shape, o.dtype) for o in outs]

    def body(*refs):
        res = fn(*[r[...] for r in refs[:n_in]])
        if not isinstance(res, (tuple, list)):
            res = (res,)
        orefs = refs[n_in + n_thru:]
        for r, v in zip(orefs[:n_out], res[:n_out]):
            r[...] = v.astype(r.dtype)
        i = pl.program_id(0)
        for r, v in zip(orefs[n_out:], res[n_out:]):
            @pl.when(i == 0)
            def _(r=r, v=v):
                r[...] = v.astype(r.dtype)

            @pl.when(i > 0)
            def _(r=r, v=v):
                r[...] += v.astype(r.dtype)

    return _pcall(body, name=name, grid=(S // tm,), in_specs=in_specs, out_specs=out_specs,
                  out_shape=out_shape + [_sds(a.shape, a.dtype) for a in accs], aliases=aliases)(*arrays)


def _sds(shape, dtype=F32):
    return jax.ShapeDtypeStruct(shape, dtype)


def _rms(x, g):
    return x * lax.rsqrt(jnp.mean(x * x, axis=-1, keepdims=True) + EPS) * g


def _rms_bwd(x, g, dy):
    _, vjp = jax.vjp(_rms, x, g)
    return vjp(dy)


def _sigmoid(x):
    return 0.5 * jnp.tanh(0.5 * x) + 0.5


def _silu(x):
    return x * _sigmoid(x)


def _log_sigmoid(f):
    return jnp.minimum(f, 0.0) - jnp.log(1.0 + jnp.exp(-jnp.abs(f)))


def _head_rms_gate(o, g_out, gn):
    parts = [_rms(o[:, h * GLA_HDV:(h + 1) * GLA_HDV], gn[:, h * GLA_HDV:(h + 1) * GLA_HDV]) for h in range(GLA_HEADS)]
    return jnp.concatenate(parts, axis=-1) * _silu(g_out.astype(F32))


def _merge(gates, ya, yb, yc):
    gates, ya, yb, yc = (t.astype(F32) for t in (gates, ya, yb, yc))
    return (_sigmoid(gates[:, :D_MODEL]) * ya + _sigmoid(gates[:, D_MODEL:2 * D_MODEL]) * yb
            + _sigmoid(gates[:, 2 * D_MODEL:]) * yc)


def _tri_dot(t, x):
    hi = x.astype(BF16)
    r1 = x - hi.astype(F32)
    mid = r1.astype(BF16)
    lo = (r1 - mid.astype(F32)).astype(BF16)
    d = functools.partial(lax.dot_general, dimension_numbers=_NN, preferred_element_type=F32)
    return d(t, hi) + d(t, mid) + d(t, lo)


def _pre_norm(name, x, g, after):
    S = x.shape[0]
    tm = min(512, S)

    def body(x_ref, g_ref, *rest):
        rest[-1][...] = _rms(x_ref[...], g_ref[...]).astype(BF16)

    row = pl.BlockSpec((tm, D_MODEL), lambda i: (i, 0))
    return _pcall(body, name=name, grid=(S // tm,),
                  in_specs=[row, pl.BlockSpec((1, D_MODEL), lambda i: (0, 0))] + [pl.BlockSpec(memory_space=pl.ANY)] * len(after),
                  out_specs=row, out_shape=_sds(x.shape, BF16))(x, g, *after)


def _ffn_fwd(tag, x, h, wt, gpost, get_wo, g_next, tgt=None):
    S = x.shape[0]
    tm = min(512, S)

    def up_body(h_ref, w_ref, u_ref, act_ref):
        hh = h_ref[...]
        for j in range(N_FF_BLK):
            ab = _dot(hh, w_ref[2 * FF_PAD * j:2 * FF_PAD * (j + 1), :], _NT)
            u_ref[:, 2 * FF_PAD * j:2 * FF_PAD * (j + 1)] = ab.astype(BF16)
            act_ref[:, FF_PAD * j:FF_PAD * (j + 1)] = (_silu(ab[:, :FF_PAD]) * ab[:, FF_PAD:]).astype(BF16)

    def down_head_body(act_ref, w_ref, x_ref, g_ref, gn_ref, t_ref, f_ref, dxo_ref, loss_ref, dgn_ref):
        i = pl.program_id(0)
        f = _dot(act_ref[...], w_ref[...])
        f_ref[...] = f.astype(BF16)
        xo = x_ref[...] + 0.5 * _rms(f, g_ref[...])
        out, vjp = jax.vjp(_rms, xo, gn_ref[...])
        e = out - t_ref[...]
        loss = 0.5 * jnp.sum(jnp.mean(e * e, axis=-1, keepdims=True), axis=0, keepdims=True)
        dxo_ref[...], dgn = vjp(e * (1.0 / D_MODEL))

        @pl.when(i == 0)
        def _():
            loss_ref[...] = jnp.broadcast_to(loss, loss_ref.shape)
            dgn_ref[...] = dgn

        @pl.when(i > 0)
        def _():
            loss_ref[...] += jnp.broadcast_to(loss, loss_ref.shape)
            dgn_ref[...] += dgn

    if tgt is not None:
        def last_body(h_ref, wt_ref, wo_ref, x_ref, g_ref, gn_ref, t_ref, u_ref, act_ref, f_ref, dxo_ref, loss_ref,
                      dgn_ref):
            up_body(h_ref, wt_ref, u_ref, act_ref)
            down_head_body(act_ref, wo_ref, x_ref, g_ref, gn_ref, t_ref, f_ref, dxo_ref, loss_ref, dgn_ref)

        wo = get_wo(None)
        tl = min(256, S)
        row = pl.BlockSpec((tl, D_MODEL), lambda i: (i, 0))
        vec = pl.BlockSpec((1, D_MODEL), lambda i: (0, 0))
        u, act, f, dxo, loss, dg_next = _pcall(
            last_body, name=tag + "_fwd", grid=(S // tl,),
            in_specs=[row, pl.BlockSpec(wt.shape, lambda i: (0, 0), pipeline_mode=pl.Buffered(1)),
                      pl.BlockSpec(wo.shape, lambda i: (0, 0), pipeline_mode=pl.Buffered(1)), row, vec, vec, row],
            out_specs=[pl.BlockSpec((tl, N_DEV * FF_PAD), lambda i: (i, 0)),
                       pl.BlockSpec((tl, N_FF_BLK * FF_PAD), lambda i: (i, 0)), row, row,
                       pl.BlockSpec((1, 128), lambda i: (0, 0)), vec],
            out_shape=[_sds((S, N_DEV * FF_PAD), BF16), _sds((S, N_FF_BLK * FF_PAD), BF16), _sds((S, D_MODEL), BF16),
                       _sds((S, D_MODEL)), _sds((1, 128)), _sds((1, D_MODEL))],
        )(h, wt, wo, x, gpost, g_next, tgt)
        return (loss, dxo, dg_next), wo, (h, u, act, f)

    u, act = _pcall(
        up_body, name=tag + "_up", grid=(S // tm,),
        in_specs=[pl.BlockSpec((tm, D_MODEL), lambda i: (i, 0)),
                  pl.BlockSpec(wt.shape, lambda i: (0, 0), pipeline_mode=pl.Buffered(1))],
        out_specs=[pl.BlockSpec((tm, N_DEV * FF_PAD), lambda i: (i, 0)),
                   pl.BlockSpec((tm, N_FF_BLK * FF_PAD), lambda i: (i, 0))],
        out_shape=[_sds((S, N_DEV * FF_PAD), BF16), _sds((S, N_FF_BLK * FF_PAD), BF16)],
    )(h, wt)

    def down_body(act_ref, w_ref, x_ref, g_ref, gn_ref, f_ref, xo_ref, hn_ref):
        f = _dot(act_ref[...], w_ref[...])
        f_ref[...] = f.astype(BF16)
        xo = x_ref[...] + 0.5 * _rms(f, g_ref[...])
        xo_ref[...] = xo
        hn_ref[...] = _rms(xo, gn_ref[...]).astype(BF16)

    wo = get_wo(act)
    row = pl.BlockSpec((tm, D_MODEL), lambda i: (i, 0))
    vec = pl.BlockSpec((1, D_MODEL), lambda i: (0, 0))
    f, xo, h_next = _pcall(
        down_body, name=tag + "_down", grid=(S // tm,),
        in_specs=[pl.BlockSpec((tm, N_FF_BLK * FF_PAD), lambda i: (i, 0)),
                  pl.BlockSpec(wo.shape, lambda i: (0, 0), pipeline_mode=pl.Buffered(1)), row, vec, vec],
        out_specs=[row, row, row], out_shape=[_sds((S, D_MODEL), BF16), _sds((S, D_MODEL)), _sds((S, D_MODEL), BF16)],
    )(act, wo, x, gpost, g_next)
    return (xo, h_next), wo, (h, u, act, f)


def _ffn_bwd(tag, dxo, x, gpre, wt, wo, gpost, saved, on_dwo=None, on_dwt=None):
    h, u, act, f = saved
    S = x.shape[0]
    tm = min(512, S)

    def dact_body(f_ref, dxo_ref, g_ref, w_ref, u_ref, df_ref, du_ref, dg_ref):
        i = pl.program_id(0)
        df, dg = _rms_bwd(f_ref[...].astype(F32), g_ref[...], 0.5 * dxo_ref[...])
        df = df.astype(BF16)
        df_ref[...] = df

        @pl.when(i == 0)
        def _():
            dg_ref[...] = dg

        @pl.when(i > 0)
        def _():
            dg_ref[...] += dg

        for j in range(N_FF_BLK):
            dact = _dot(df, w_ref[FF_PAD * j:FF_PAD * (j + 1), :], _NT)
            ab = u_ref[:, 2 * FF_PAD * j:2 * FF_PAD * (j + 1)].astype(F32)
            a, b = ab[:, :FF_PAD], ab[:, FF_PAD:]
            sg = _sigmoid(a)
            du_ref[:, 2 * FF_PAD * j:2 * FF_PAD * j + FF_PAD] = (dact * b * (sg * (1.0 + a * (1.0 - sg)))).astype(BF16)
            du_ref[:, 2 * FF_PAD * j + FF_PAD:2 * FF_PAD * (j + 1)] = (dact * (a * sg)).astype(BF16)

    row = pl.BlockSpec((tm, D_MODEL), lambda i: (i, 0))
    vec = pl.BlockSpec((1, D_MODEL), lambda i: (0, 0))
    u_spec = pl.BlockSpec((tm, N_DEV * FF_PAD), lambda i: (i, 0))
    df, du, dgpost = _pcall(
        dact_body, name=tag + "_dact", grid=(S // tm,),
        in_specs=[row, row, vec, pl.BlockSpec(wo.shape, lambda i: (0, 0), pipeline_mode=pl.Buffered(1)), u_spec],
        out_specs=[row, u_spec, vec],
        out_shape=[_sds((S, D_MODEL), BF16), _sds((S, N_DEV * FF_PAD), BF16), _sds((1, D_MODEL))])(f, dxo, gpost, wo, u)
    dwo = _mm_tn(tag + "_dwo", act, df, ts=S, tn_a=FF_PAD)
    dwt = _mm_tn(tag + "_dwt", du, h, ts=S, tn_a=FF_PAD, deps=[on_dwo(dwo)] if on_dwo is not None else [])
    dx, dgpre = _dh_pre_norm_bwd(tag + "_dh", [du], wt, x, dxo, gpre, [on_dwt(dwt)] if on_dwt is not None else [])
    return dx, dgpre, dgpost


def _dh_pre_norm_bwd(name, dzs, wt, x, dres, g, deps):
    S = x.shape[0]
    ks = [dz.shape[1] for dz in dzs]

    def body(*refs):
        dz_refs = refs[:len(dzs)]
        w_ref, x_ref, dres_ref, g_ref = refs[len(dzs):len(dzs) + 4]
        dx_ref, dg_ref = refs[len(dzs) + 4 + len(deps):]
        i = pl.program_id(0)
        dh, o = None, 0
        for dz_ref, k in zip(dz_refs, ks):
            p = _dot(dz_ref[...], w_ref[o:o + k, :])
            dh, o = (p if dh is None else dh + p), o + k
        dx, dg = _rms_bwd(x_ref[...], g_ref[...], dh)
        dx_ref[...] = dx + dres_ref[...]

        @pl.when(i == 0)
        def _():
            dg_ref[...] = dg

        @pl.when(i > 0)
        def _():
            dg_ref[...] += dg

    th = min(512, S)
    row = pl.BlockSpec((th, D_MODEL), lambda i: (i, 0))
    vec = pl.BlockSpec((1, D_MODEL), lambda i: (0, 0))
    return _pcall(
        body, name=name, grid=(S // th,),
        in_specs=[pl.BlockSpec((th, k), lambda i: (i, 0)) for k in ks]
        + [pl.BlockSpec(wt.shape, lambda i: (0, 0), pipeline_mode=pl.Buffered(1)), row, row, vec]
        + [pl.BlockSpec(memory_space=pl.ANY)] * len(deps),
        out_specs=[row, vec], out_shape=[_sds((S, D_MODEL)), _sds((1, D_MODEL))])(*dzs, wt, x, dres, g, *deps)


GLA_G = 8


def _gla_tile_common(k, flow, wfu, bf):
    f = _dot(flow, wfu) + bf
    la = _log_sigmoid(f) * (1.0 / GATE_TEMP)
    tri = _tri_matrix(True)
    ws, ds = [], []
    for g in range(k.shape[0] // CHUNK):
        b = _tri_dot(tri, la[g * CHUNK:(g + 1) * CHUNK])
        b_end = b[CHUNK - 1:CHUNK, :]
        ws.append(jnp.exp(b_end - b))
        ds.append(jnp.exp(b_end))
    w = jnp.concatenate(ws, axis=0)
    return f, w, k * w, ds


def _tri_matrix(lower):
    r = lax.broadcasted_iota(jnp.int32, (CHUNK, CHUNK), 0)
    c = lax.broadcasted_iota(jnp.int32, (CHUNK, CHUNK), 1)
    return jnp.where((r >= c) if lower else (r <= c), 1.0, 0.0).astype(BF16)


def _heads():
    return [(slice(h * GLA_HDK, (h + 1) * GLA_HDK), slice(h * GLA_HDV, (h + 1) * GLA_HDV)) for h in range(GLA_HEADS)]


def _gla_fwd(proj, wfu, bf, gn, deps=()):
    S = proj.shape[0]
    G = min(GLA_G, S // CHUNK)
    T = G * CHUNK
    nc = S // CHUNK

    def body(q_ref, k_ref, v_ref, go_ref, flow_ref, wfu_ref, bf_ref, gn_ref, *rest):
        ya_ref, st_ref, state = rest[len(deps):]
        @pl.when(pl.program_id(0) == 0)
        def _():
            state[...] = jnp.zeros_like(state)

        _, _, kt, ds = _gla_tile_common(k_ref[...], flow_ref[...], wfu_ref[...], bf_ref[...])
        q = q_ref[...].astype(F32) * (GLA_HDK ** -0.5)
        v = v_ref[...]
        rows = [slice(g * CHUNK, (g + 1) * CHUNK) for g in range(G)]
        kv = [[_dot(v[r, vs], kt[r, ks], _TN) for ks, vs in _heads()] for r in rows]
        st = [state[vs, :] for _, vs in _heads()]
        o = []
        for g, r in enumerate(rows):
            outs = []
            for h, (ks, vs) in enumerate(_heads()):
                st[h] = st[h] * ds[g][:, ks] + kv[g][h]
                st_ref[g, vs, :] = st[h]
                outs.append(_dot(q[r, ks], st[h], _NT))
            o.append(jnp.concatenate(outs, axis=-1))
        for h, (_, vs) in enumerate(_heads()):
            state[vs, :] = st[h]
        ya_ref[...] = _head_rms_gate(jnp.concatenate(o, axis=0), go_ref[...], gn_ref[...]).astype(BF16)

    return _pcall(
        body, name="gla_fwd", grid=(S // T,),
        in_specs=[pl.BlockSpec((T, GLA_DK), lambda c: (c, 0)), pl.BlockSpec((T, GLA_DK), lambda c: (c, 1)),
                  pl.BlockSpec((T, GLA_DV), lambda c: (c, 1)), pl.BlockSpec((T, GLA_DV), lambda c: (c, 2)),
                  pl.BlockSpec((T, FLOW_W), lambda c: (c, (INT_W - FLOW_W) // FLOW_W)),
                  pl.BlockSpec(wfu.shape, lambda c: (0, 0)), pl.BlockSpec(bf.shape, lambda c: (0, 0)),
                  pl.BlockSpec(gn.shape, lambda c: (0, 0))]
        + [pl.BlockSpec(memory_space=pl.ANY)] * len(deps),
        out_specs=[pl.BlockSpec((T, GLA_DV), lambda c: (c, 0)),
                   pl.BlockSpec((G, GLA_DV, GLA_HDK), lambda c: (c, 0, 0))],
        out_shape=[_sds((S, GLA_DV), BF16), _sds((nc, GLA_DV, GLA_HDK))],
        scratch=[pltpu.VMEM((GLA_DV, GLA_HDK), F32)])(proj, proj, proj, proj, proj, wfu, bf, gn, *deps)


def _gla_bwd(proj, wfu, bf, gn, states, d_ya_in, dmain):
    S = proj.shape[0]
    G = min(GLA_G, S // CHUNK)
    T = G * CHUNK
    nt = S // T

    def body(q_ref, k_ref, v_ref, go_ref, flow_ref, wfu_ref, bf_ref, gn_ref, st_ref, stp_ref, dya_ref, dmain_in,
             dqkvg_ref, dflow_ref, dwfu_ref, dbf_ref, dgn_ref, dstate):
        dq_ref = dqkvg_ref.at[:, 0:GLA_DK]
        dk_ref = dqkvg_ref.at[:, GLA_DK:2 * GLA_DK]
        dv_ref = dqkvg_ref.at[:, 2 * GLA_DK:2 * GLA_DK + GLA_DV]
        dgo_ref = dqkvg_ref.at[:, 2 * GLA_DK + GLA_DV:]
        step = pl.program_id(0)

        @pl.when(step == 0)
        def _():
            dstate[...] = jnp.zeros_like(dstate)
            dwfu_ref[...] = jnp.zeros_like(dwfu_ref)
            dbf_ref[...] = jnp.zeros_like(dbf_ref)
            dgn_ref[...] = jnp.zeros_like(dgn_ref)

        flow, wfu_v = flow_ref[...], wfu_ref[...]
        f, w, kt, ds = _gla_tile_common(k_ref[...], flow, wfu_v, bf_ref[...])
        q = q_ref[...].astype(F32) * (GLA_HDK ** -0.5)
        v = v_ref[...]
        rows = [slice(g * CHUNK, (g + 1) * CHUNK) for g in range(G)]
        o = jnp.concatenate([jnp.concatenate([_dot(q[r, ks], st_ref[g, vs, :], _NT) for ks, vs in _heads()], axis=-1)
                             for g, r in enumerate(rows)], axis=0)
        _, vjp = jax.vjp(_head_rms_gate, o, go_ref[...], gn_ref[...])
        dout, dgo, dgn = vjp(dya_ref[...].astype(F32))
        dgo_ref[...] = dgo.astype(dgo_ref.dtype)
        dgn_ref[...] += dgn
        dq = [jnp.concatenate([_dot(dout[r, vs], st_ref[g, vs, :]) for _, vs in _heads()], axis=-1)
              for g, r in enumerate(rows)]
        qdo = [[_dot(dout[r, vs], q[r, ks], _TN) for ks, vs in _heads()] for r in rows]
        dq_ref[...] = (jnp.concatenate(dq, axis=0) * (GLA_HDK ** -0.5)).astype(dq_ref.dtype)
        has_prev = (step < nt - 1).astype(F32)
        carry = [dstate[vs, :] for _, vs in _heads()]
        dkt, dv, dd = [None] * G, [None] * G, [None] * G
        for g in reversed(range(G)):
            r = rows[g]
            dkts, dvs, dds = [], [], []
            for h, (ks, vs) in enumerate(_heads()):
                dst = carry[h] + qdo[g][h]
                dkts.append(_dot(v[r, vs], dst))
                dvs.append(_dot(kt[r, ks], dst, _NT))
                st_prev = st_ref[g - 1, vs, :] if g > 0 else stp_ref[vs, :] * has_prev
                dds.append(jnp.sum(dst * st_prev, axis=0, keepdims=True))
                carry[h] = dst * ds[g][:, ks]
            dkt[g], dv[g], dd[g] = (jnp.concatenate(t, axis=-1) for t in (dkts, dvs, dds))
        for h, (_, vs) in enumerate(_heads()):
            dstate[vs, :] = carry[h]
        dkt = jnp.concatenate(dkt, axis=0)
        dv_ref[...] = jnp.concatenate(dv, axis=0).astype(dv_ref.dtype)
        dk_ref[...] = (dkt * w).astype(dk_ref.dtype)
        de = dkt * kt
        tri = _tri_matrix(False)
        dla = []
        for g, r in enumerate(rows):
            db_end = jnp.sum(de[r], axis=0, keepdims=True) + dd[g] * ds[g]
            dla.append(db_end - _tri_dot(tri, de[r]))
        df = jnp.concatenate(dla, axis=0) * (1.0 - _sigmoid(f)) * (1.0 / GATE_TEMP)
        dflow_ref[...] = _dot(df, wfu_v, _NT).astype(dflow_ref.dtype)
        dwfu_ref[...] += _dot(flow, df, _TN)
        dbf_ref[...] += jnp.sum(df, axis=0, keepdims=True)

    rt = lambda s: nt - 1 - s
    return _pcall(
        body, name="gla_bwd", grid=(nt,),
        in_specs=[pl.BlockSpec((T, GLA_DK), lambda s: (rt(s), 0)), pl.BlockSpec((T, GLA_DK), lambda s: (rt(s), 1)),
                  pl.BlockSpec((T, GLA_DV), lambda s: (rt(s), 1)), pl.BlockSpec((T, GLA_DV), lambda s: (rt(s), 2)),
                  pl.BlockSpec((T, FLOW_W), lambda s: (rt(s), (INT_W - FLOW_W) // FLOW_W)),
                  pl.BlockSpec(wfu.shape, lambda s: (0, 0)), pl.BlockSpec(bf.shape, lambda s: (0, 0)),
                  pl.BlockSpec(gn.shape, lambda s: (0, 0)),
                  pl.BlockSpec((G, GLA_DV, GLA_HDK), lambda s: (rt(s), 0, 0)),
                  pl.BlockSpec((None, GLA_DV, GLA_HDK), lambda s: (jnp.maximum(rt(s) * G - 1, 0), 0, 0)),
                  pl.BlockSpec((T, GLA_DV), lambda s: (rt(s), 0)), pl.BlockSpec(memory_space=pl.ANY)],
        out_specs=[pl.BlockSpec((T, 2 * GLA_DK + 2 * GLA_DV), lambda s: (rt(s), 0)),
                   pl.BlockSpec((T, FLOW_W), lambda s: (rt(s), 0)),
                   pl.BlockSpec(wfu.shape, lambda s: (0, 0)), pl.BlockSpec(bf.shape, lambda s: (0, 0)),
                   pl.BlockSpec(gn.shape, lambda s: (0, 0))],
        out_shape=[_sds(dmain.shape, dmain.dtype), _sds((S, FLOW_W), BF16), _sds(wfu.shape), _sds(bf.shape),
                   _sds(gn.shape)],
        scratch=[pltpu.VMEM((GLA_DV, GLA_HDK), F32)], aliases={11: 0},
    )(proj, proj, proj, proj, proj, wfu, bf, gn, states, states, d_ya_in, dmain)


def _pool_counts(tm, i):
    t = (lax.broadcasted_iota(jnp.int32, (tm, POOL_GD), 0) + i * tm + 1).astype(F32)
    return [jnp.minimum(t, float(w)) for w in POOL_WINDOWS]


def _pool_fwd(proj, w_pool, pool_scale, tm=512):
    S = proj.shape[0]
    tm = min(tm, S // 2)
    col = _POOL_COL
    hb = tm // POOL_HALO

    def body(p_ref, halo_ref, wp_ref, sc_ref, mixed_ref, out_ref):
        i = pl.program_id(0)
        p = p_ref[...].astype(F32)
        halo = halo_ref[...].astype(F32) * (i > 0).astype(F32)
        ext = jnp.concatenate([halo, p], axis=0)
        n = tm + POOL_HALO
        sums, acc, k = {}, ext, 1
        while k < POOL_WINDOWS[-1]:
            acc = acc + pltpu.roll(acc, k, axis=0)
            k *= 2
            sums[k] = acc
        cnts = _pool_counts(tm, i)
        mixed, lin = [], []
        for g, w in enumerate(POOL_WINDOWS):
            ls = slice(g * POOL_GD, (g + 1) * POOL_GD)
            m = sums[w][POOL_HALO:n, ls] / cnts[g] - p[:, ls]
            mixed.append(m)
            lin.append(_dot(m, wp_ref[g]))
        mixed_ref[...] = jnp.concatenate(mixed, axis=-1)
        out_ref[...] = (jnp.concatenate(lin, axis=-1) * sc_ref[...]).astype(out_ref.dtype)

    return _pcall(
        body, name="pool_fwd", grid=(S // tm,),
        in_specs=[pl.BlockSpec((tm, POOL_W), lambda i: (i, col)),
                  pl.BlockSpec((POOL_HALO, POOL_W), lambda i: (jnp.maximum(i * hb - 1, 0), col)),
                  pl.BlockSpec(w_pool.shape, lambda i: (0, 0, 0)), pl.BlockSpec(pool_scale.shape, lambda i: (0, 0))],
        out_specs=[pl.BlockSpec((tm, POOL_W), lambda i: (i, 0)), pl.BlockSpec((tm, POOL_W), lambda i: (i, 0))],
        out_shape=[_sds((S, POOL_W)), _sds((S, POOL_W), BF16)])(proj, proj, w_pool, pool_scale)


def _pool_lin_bwd(dout, mixed, w_pool, pool_scale):
    S = dout.shape[0]

    def fn(dout, mixed, wp, sc):
        dlin = dout * sc
        dm, dwp, lin = [], [], []
        for g in range(len(POOL_WINDOWS)):
            ls = slice(g * POOL_GD, (g + 1) * POOL_GD)
            lin.append(_dot(mixed[:, ls], wp[g]))
            dm.append(_dot(dlin[:, ls], wp[g], _NT))
            dwp.append(_dot(mixed[:, ls], dlin[:, ls], _TN))
        dsc = jnp.sum(dout * jnp.concatenate(lin, axis=-1), axis=0, keepdims=True)
        return jnp.concatenate(dm, axis=-1), jnp.concatenate(dwp, axis=0), dsc

    return _rowwise("pool_lin_bwd", fn, [dout, mixed], [w_pool, pool_scale], [_sds((S, POOL_W))],
                    [_sds((len(POOL_WINDOWS) * POOL_GD, POOL_GD)), _sds((1, POOL_W))])


def _pool_win_bwd(dmixed, dmain, tm=512):
    S = dmixed.shape[0]
    tm = min(tm, S // 2)
    nt = S // tm
    hb = tm // POOL_HALO

    def body(dm_ref, halo_ref, dmain_in, dp_ref):
        i = pl.program_id(0)
        dm = dm_ref[...]
        halo = halo_ref[...] * (i < nt - 1).astype(F32)
        cnts = _pool_counts(tm, i)
        cnts_h = [c[:POOL_HALO] for c in _pool_counts(tm, i + 1)]
        r = jnp.concatenate([jnp.concatenate([dm[:, g * POOL_GD:(g + 1) * POOL_GD] / cnts[g] for g in range(4)], axis=-1),
                             jnp.concatenate([halo[:, g * POOL_GD:(g + 1) * POOL_GD] / cnts_h[g] for g in range(4)], axis=-1)],
                            axis=0)
        n = tm + POOL_HALO
        sums, acc, k = {}, r, 1
        while k < POOL_WINDOWS[-1]:
            acc = acc + pltpu.roll(acc, n - k, axis=0)
            k *= 2
            sums[k] = acc
        dp = [sums[w][:tm, g * POOL_GD:(g + 1) * POOL_GD] for g, w in enumerate(POOL_WINDOWS)]
        dp_ref[...] = (jnp.concatenate(dp, axis=-1) - dm).astype(dp_ref.dtype)

    return _pcall(
        body, name="pool_win_bwd", grid=(nt,),
        in_specs=[pl.BlockSpec((tm, POOL_W), lambda i: (i, 0)),
                  pl.BlockSpec((POOL_HALO, POOL_W), lambda i: (jnp.minimum((i + 1) * hb, S // POOL_HALO - 1), 0)),
                  pl.BlockSpec(memory_space=pl.ANY)],
        out_specs=pl.BlockSpec((tm, POOL_W), lambda i: (i, _POOL_COL)),
        out_shape=_sds(dmain.shape, dmain.dtype), aliases={2: 0})(dmixed, dmixed, dmain)


def _xattn_probs(q, kv, h):
    hs = slice(h * XA_HD, (h + 1) * XA_HD)
    s = _dot(q[:, hs], kv[:, hs], _NT) * (XA_HD ** -0.5)
    s = s - jnp.max(s, axis=-1, keepdims=True)
    e = jnp.exp(s)
    return e / jnp.sum(e, axis=-1, keepdims=True)


def _xattn_fwd(proj, kv):
    S = proj.shape[0]

    def fn(q, kv):
        outs = []
        for h in range(XA_HEADS):
            p = _xattn_probs(q, kv, h)
            outs.append(_dot(p, kv[:, XA_W + h * XA_HD:XA_W + (h + 1) * XA_HD]))
        return jnp.concatenate(outs, axis=-1)

    return _rowwise("xattn_fwd", fn, [_Win(proj, XA_W, _XQ_COL)], [kv], [_sds((S, XA_W), BF16)])[0]


def _xattn_bwd(proj, kv, dxa, dmain):
    def fn(q, dxa, kv):
        dqs, dks, dvs = [], [], []
        for h in range(XA_HEADS):
            hs = slice(h * XA_HD, (h + 1) * XA_HD)
            vh = kv[:, XA_W + h * XA_HD:XA_W + (h + 1) * XA_HD]
            p = _xattn_probs(q, kv, h)
            dp = _dot(dxa[:, hs], vh, _NT)
            ds = p * (dp - jnp.sum(p * dp, axis=-1, keepdims=True)) * (XA_HD ** -0.5)
            dqs.append(_dot(ds, kv[:, hs]))
            dks.append(_dot(ds, q[:, hs], _TN))
            dvs.append(_dot(p, dxa[:, hs], _TN))
        return jnp.concatenate(dqs, axis=-1), jnp.concatenate(dks + dvs, axis=-1)

    return _rowwise("xattn_bwd", fn, [_Win(proj, XA_W, _XQ_COL), dxa], [kv], [_Win(dmain, XA_W, _XQ_COL)],
                    [_sds(kv.shape)])


def _mix_fwd(x1, h, mem, W, g_next, on_proj=None, late_weights=None):
    S = x1.shape[0]
    M = mem.shape[0]
    tm = min(512, S)
    def proj_body(h_ref, w_ref, o_ref):
        hh = h_ref[...]
        for j in range(INT_NBLK):
            o_ref[:, INT_BLK * j:INT_BLK * (j + 1)] = _dot(hh, w_ref[INT_BLK * j:INT_BLK * (j + 1), :], _NT).astype(BF16)

    proj = _pcall(
        proj_body, name="mix_proj", grid=(S // tm,),
        in_specs=[pl.BlockSpec((tm, D_MODEL), lambda i: (i, 0)),
                  pl.BlockSpec((INT_W, D_MODEL), lambda i: (0, 0), pipeline_mode=pl.Buffered(1))],
        out_specs=pl.BlockSpec((tm, INT_W), lambda i: (i, 0)), out_shape=_sds((S, INT_W), BF16))(h, W["w_int_t"])
    ya_in, states = _gla_fwd(proj, W["w_fu_pad"], W["b_f"], W["gla_norm_g"],
                             on_proj(proj) if on_proj is not None else [])
    mixed, pool_out = _pool_fwd(proj, W["w_pool"], W["pool_scale"])
    mem_n = _rowwise("mem_norm", lambda m, g: _rms(m, g), [mem], [W["mem_norm_g"]], [_sds((M, D_MODEL), BF16)])[0]
    if late_weights is not None:
        W = {**W, **late_weights([pool_out])}
    kv = _mm_nn("mem_kv", mem_n, W["w_mem_kv"])
    xa = _xattn_fwd(proj, kv)
    def out_fn(gates, ya_in, pool_out, xa, x1, wa, wb, wc, wo, g, gn):
        ya, yb, yc = (_dot(t, w).astype(BF16) for t, w in ((ya_in, wa), (pool_out, wb), (xa, wc)))
        merged = _merge(gates, ya, yb, yc).astype(BF16)
        y = _dot(merged, wo)
        x2 = x1 + _rms(y, g)
        return ya, yb, yc, merged, y, x2, _rms(x2, gn)

    ya, yb, yc, merged, y, x2, h_next = _rowwise(
        "mix_out", out_fn, [_Win(proj, 3 * D_MODEL, 1), ya_in, pool_out, xa, x1],
        [W["w_up_gla"], W["w_up_pool"], W["w_up_xattn"], W["w_o"], W["mix_post_g"], g_next],
        [_sds((S, D_MODEL), BF16)] * 5 + [_sds((S, D_MODEL)), _sds((S, D_MODEL), BF16)])
    return x2, h_next, W, (h, proj, states, ya_in, mixed, pool_out, mem_n, kv, xa, ya, yb, yc, merged, y)


def _mix_bwd(dx2, x1, mem, W, saved, on_grads=None):
    h, proj, states, ya_in, mixed, pool_out, mem_n, kv, xa, ya, yb, yc, merged, y = saved
    S = x1.shape[0]

    def out_bwd(y, dx2, gates, ya, yb, yc, g, wo, wa, wb, wc):
        dy, dg = _rms_bwd(y.astype(F32), g, dx2)
        dy = dy.astype(BF16)
        _, vjp = jax.vjp(_merge, gates, ya, yb, yc)
        dgates, dya, dyb, dyc = vjp(_dot(dy, wo, _NT))
        return (dy, dgates, dya, dyb, dyc, _dot(dya, wa, _NT), _dot(dyb, wb, _NT), _dot(dyc, wc, _NT), dg)

    dmain = lax.empty((S, INT_MAIN), BF16)
    dy, dmain, dya, dyb, dyc, d_ya_in, d_pool_out, d_xa, d_mix_post_g = _rowwise(
        "mix_out_bwd", out_bwd, [y, dx2, _Win(proj, 3 * D_MODEL, 1), ya, yb, yc],
        [W["mix_post_g"], W["w_o"], W["w_up_gla"], W["w_up_pool"], W["w_up_xattn"]],
        [_sds((S, D_MODEL), BF16), _Win(dmain, 3 * D_MODEL, 1)] + [_sds((S, D_MODEL), BF16)] * 4
        + [_sds((S, POOL_W), BF16), _sds((S, XA_W), BF16)], [_sds((1, D_MODEL))])
    d_w_o = _mm_tn("d_w_o", merged, dy, ts=S, tn_a=256)
    d_w_up_gla = _mm_tn("d_w_up_gla", ya_in, dya, ts=S, tn_a=256)
    d_w_up_pool = _mm_tn("d_w_up_pool", pool_out, dyb, ts=S, tn_a=128)
    d_w_up_xattn = _mm_tn("d_w_up_xattn", xa, dyc, ts=S, tn_a=128)

    dmain, dflow, d_wfu_pad, d_b_f, d_gla_norm_g = _gla_bwd(proj, W["w_fu_pad"], W["b_f"], W["gla_norm_g"], states,
                                                            d_ya_in, dmain)
    dmixed, d_w_pool, d_pool_scale = _pool_lin_bwd(d_pool_out, mixed, W["w_pool"], W["pool_scale"])
    dmain = _pool_win_bwd(dmixed, dmain)
    dmain, dkv = _xattn_bwd(proj, kv, d_xa, dmain)
    d_w_mem_kv = _mm_tn("d_w_mem_kv", mem_n, dkv)
    dmem_n = _mm_nt("d_mem_n", dkv, W["w_mem_kv"])
    d_mem_norm_g = _rowwise("mem_norm_bwd", lambda m, d, g: _rms_bwd(m, g, d)[1], [mem, dmem_n], [W["mem_norm_g"]], [],
                            [_sds((1, D_MODEL))])[0]
    d_w_int_t = (_mm_tn("d_w_int", dmain, h, ts=S, tn_a=INT_MAIN // 8), _mm_tn("d_w_flow", dflow, h, ts=S))
    grads = dict(
        w_int_t=d_w_int_t, w_fu_pad=d_wfu_pad, b_f=d_b_f, gla_norm_g=d_gla_norm_g, w_pool=d_w_pool,
        pool_scale=d_pool_scale, mem_norm_g=d_mem_norm_g, w_mem_kv=d_w_mem_kv, w_up_gla=d_w_up_gla,
        w_up_pool=d_w_up_pool, w_up_xattn=d_w_up_xattn, w_o=d_w_o, mix_post_g=d_mix_post_g)
    deps = [on_grads(grads)] if on_grads is not None else []
    dx1, grads["mix_pre_g"] = _dh_pre_norm_bwd("d_mix_h", [dmain, dflow], W["w_int_t"], x1, dx2, W["mix_pre_g"], deps)
    return dx1, grads


def _mesh_pos():
    x, y, c = lax.axis_index("x"), lax.axis_index("y"), lax.axis_index("c")
    return x, y, c, 4 * x + 2 * y + c


def _peer(x, y, c, r):
    px = 1 - x if r & 4 else x
    py = 1 - y if r & 2 else y
    pc = 1 - c if r & 1 else c
    return (px, py, pc), 4 * px + 2 * py + pc


_ALL_PEERS = tuple(range(1, N_DEV))
_SIBLING = 1
_SAME_CORE = (2, 4, 6)


def _dev_slot(ref, dev):
    return ref.at[dev]


def _zero_pad_rows(land):
    pad = FF_PAD - FF_BLK

    def body(land_in, o_ref):
        o_ref[...] = jnp.zeros_like(o_ref)

    return _pcall(body, name="zero_pad_rows", grid=(land.shape[0],), in_specs=[pl.BlockSpec(memory_space=pl.ANY)],
                  out_specs=pl.BlockSpec((None, pad, D_MODEL), lambda j: (j, FF_BLK // pad, 0)),
                  out_shape=_sds(land.shape, land.dtype), aliases={0: 0})(land)


class _Plan:
    def __init__(self, scatter, slots=None, shapes=None):
        self.scatter, self.slots, self.shapes = scatter, slots or {}, shapes or {}

    def src(self, srcs, a, dev):
        return self.slots.get(a, _dev_slot)(srcs[a], dev) if self.scatter else srcs[a]

    def dst(self, lands, a, dev):
        return lands[a].at[dev] if self.scatter else self.slots.get(a, _dev_slot)(lands[a], dev)

    def landing_zones(self, arrays):
        lands = []
        for a, arr in enumerate(arrays):
            if self.scatter:
                lands.append(lax.empty((N_DEV,) + tuple(self.shapes.get(a, arr.shape[1:])), arr.dtype))
            elif a in self.shapes:
                lands.append(_zero_pad_rows(lax.empty(self.shapes[a], arr.dtype)))
            else:
                lands.append(lax.empty((N_DEV,) + arr.shape, arr.dtype))
        return lands


_GATHER = _Plan(False)


def _peer_copies(srcs, lands, send_sems, recv_sems, plan, peers=_ALL_PEERS):
    x, y, c, me = _mesh_pos()
    cps = []
    for r in peers:
        pos, peer = _peer(x, y, c, r)
        for a in range(len(srcs)):
            k = a * (N_DEV - 1) + r - 1
            cps.append(pltpu.make_async_remote_copy(
                src_ref=plan.src(srcs, a, peer), dst_ref=plan.dst(lands, a, me),
                send_sem=send_sems.at[k], recv_sem=recv_sems.at[k], device_id=pos,
                device_id_type=pl.DeviceIdType.MESH))
    return cps


_HBM = pl.BlockSpec(memory_space=pltpu.HBM)
_SEM = pl.BlockSpec(memory_space=pltpu.SEMAPHORE)
_EFFECT = pltpu.SideEffectType.DATAFLOW_SIDE_EFFECTING


def _own_copies(srcs, lands, own_sems, plan):
    me = _mesh_pos()[3]
    return [pltpu.make_async_copy(plan.src(srcs, a, me), plan.dst(lands, a, me), own_sems.at[a])
            for a in range(len(srcs))]


def _exchange_start(name, arrays, plan):
    n = len(arrays)
    lands = plan.landing_zones(arrays)
    n_sem = n * (N_DEV - 1)

    def body(*refs):
        srcs, lands_ = refs[:n], refs[n:2 * n]
        send_sems, recv_sems, own_sems = refs[2 * n:2 * n + 3]
        token = refs[-1]
        for cp in _peer_copies(srcs, lands_, send_sems, recv_sems, plan) + _own_copies(srcs, lands_, own_sems, plan):
            cp.start()
        token[...] = jnp.zeros_like(token)

    hbm = lambda a: pltpu.HBM(a.shape, a.dtype)
    res = pl.pallas_call(
        body, name=name,
        out_shape=(pltpu.SemaphoreType.DMA((n_sem,)), pltpu.SemaphoreType.DMA((n_sem,)), pltpu.SemaphoreType.DMA((n,)),
                   *[hbm(a) for a in arrays], *[hbm(a) for a in lands], _sds((8, 128))),
        in_specs=[_HBM] * (2 * n),
        out_specs=(_SEM, _SEM, _SEM, *[_HBM] * (2 * n), pl.BlockSpec(memory_space=pltpu.VMEM)),
        input_output_aliases={i: 3 + i for i in range(2 * n)},
        compiler_params=pltpu.CompilerParams(has_side_effects=_EFFECT),
    )(*[pltpu.with_memory_space_constraint(a, pltpu.HBM) for a in list(arrays) + lands])
    return (res[:3], res[3:3 + n], res[3 + n:3 + 2 * n], plan), res[-1]


def _exchange_wait(name, state, after):
    sems, srcs, lands, plan = state
    n = len(srcs)

    def body(*refs):
        srcs_, lands_ = refs[:n], refs[n:2 * n]
        send_sems, recv_sems, own_sems = refs[2 * n:2 * n + 3]
        for cp in _peer_copies(srcs_, lands_, send_sems, recv_sems, plan):
            cp.wait_send()
            cp.wait_recv()
        for cp in _own_copies(srcs_, lands_, own_sems, plan):
            cp.wait()

    hbm = lambda a: pltpu.HBM(a.shape, a.dtype)
    res = pl.pallas_call(
        body, name=name, out_shape=tuple(hbm(a) for a in list(srcs) + list(lands)),
        in_specs=[_HBM] * (2 * n) + [_SEM] * 3 + [pl.BlockSpec(memory_space=pl.ANY)] * len(after),
        out_specs=tuple([_HBM] * (2 * n)), input_output_aliases={i: i for i in range(2 * n)},
        compiler_params=pltpu.CompilerParams(has_side_effects=_EFFECT),
    )(*srcs, *lands, *sems, *after)
    return res[n:]


def _gather_start(name, arrays, plan, after):
    n = len(arrays)
    lands = plan.landing_zones(arrays)
    n_sem = n * (N_DEV - 1)

    def body(*refs):
        srcs, lands_ = refs[:n], refs[n:2 * n]
        send_sems, recv_sems, own_sems = refs[2 * n + len(after):2 * n + len(after) + 3]
        token = refs[-1]
        for cp in (_peer_copies(srcs, lands_, send_sems, recv_sems, plan, (_SIBLING,) + _SAME_CORE)
                   + _own_copies(srcs, lands_, own_sems, plan)):
            cp.start()
        token[...] = jnp.zeros_like(token)

    hbm = lambda a: pltpu.HBM(a.shape, a.dtype)
    res = pl.pallas_call(
        body, name=name,
        out_shape=(pltpu.SemaphoreType.DMA((n_sem,)), pltpu.SemaphoreType.DMA((n_sem,)), pltpu.SemaphoreType.DMA((n,)),
                   *[hbm(a) for a in arrays], *[hbm(a) for a in lands], _sds((8, 128))),
        in_specs=[_HBM] * (2 * n) + [pl.BlockSpec(memory_space=pl.ANY)] * len(after),
        out_specs=(_SEM, _SEM, _SEM, *[_HBM] * (2 * n), pl.BlockSpec(memory_space=pltpu.VMEM)),
        input_output_aliases={i: 3 + i for i in range(2 * n)},
        compiler_params=pltpu.CompilerParams(has_side_effects=_EFFECT),
    )(*[pltpu.with_memory_space_constraint(a, pltpu.HBM) for a in list(arrays) + lands], *after)
    return (res[:3], res[3:3 + n], res[3 + n:3 + 2 * n], plan), res[-1]


def _pass_on_copies(lands, send_sems, recv_sems, plan):
    x, y, c, _ = _mesh_pos()
    sibling = _peer(x, y, c, _SIBLING)[0]
    cps = []
    for i, r in enumerate(_SAME_CORE):
        owner = _peer(x, y, c, r)[1]
        for a in range(len(lands)):
            k = a * len(_SAME_CORE) + i
            cps.append(pltpu.make_async_remote_copy(
                src_ref=plan.dst(lands, a, owner), dst_ref=plan.dst(lands, a, owner), send_sem=send_sems.at[k],
                recv_sem=recv_sems.at[k], device_id=sibling, device_id_type=pl.DeviceIdType.MESH))
    return cps


def _gather_pass_on(name, state, after):
    sems, srcs, lands, plan = state
    n = len(srcs)
    n_sem = n * len(_SAME_CORE)

    def body(*refs):
        srcs_, lands_ = refs[:n], refs[n:2 * n]
        send_sems, recv_sems = refs[2 * n], refs[2 * n + 1]
        on_send, on_recv = refs[2 * n + 3 + len(after)], refs[2 * n + 4 + len(after)]
        token = refs[-1]
        arrivals = _peer_copies(srcs_, lands_, send_sems, recv_sems, plan, _SAME_CORE)
        for arrived, on in zip(arrivals, _pass_on_copies(lands_, on_send, on_recv, plan)):
            arrived.wait_recv()
            on.start()
        token[...] = jnp.zeros_like(token)

    hbm = lambda a: pltpu.HBM(a.shape, a.dtype)
    res = pl.pallas_call(
        body, name=name,
        out_shape=(pltpu.SemaphoreType.DMA((n_sem,)), pltpu.SemaphoreType.DMA((n_sem,)),
                   *[hbm(a) for a in list(srcs) + list(lands)], _sds((8, 128))),
        in_specs=[_HBM] * (2 * n) + [_SEM] * 3 + [pl.BlockSpec(memory_space=pl.ANY)] * len(after),
        out_specs=(_SEM, _SEM, *[_HBM] * (2 * n), pl.BlockSpec(memory_space=pltpu.VMEM)),
        input_output_aliases={i: 2 + i for i in range(2 * n)},
        compiler_params=pltpu.CompilerParams(has_side_effects=_EFFECT),
    )(*srcs, *lands, *sems, *after)
    return (sems, res[:2], res[2:2 + n], res[2 + n:2 + 2 * n], plan), res[-1]


def _gather_wait(name, state, after):
    sems, on_sems, srcs, lands, plan = state
    n = len(srcs)

    def body(*refs):
        srcs_, lands_ = refs[:n], refs[n:2 * n]
        send_sems, recv_sems, own_sems, on_send, on_recv = refs[2 * n:2 * n + 5]
        for cp in _peer_copies(srcs_, lands_, send_sems, recv_sems, plan, (_SIBLING,)):
            cp.wait_recv()
        for cp in _peer_copies(srcs_, lands_, send_sems, recv_sems, plan, (_SIBLING,) + _SAME_CORE):
            cp.wait_send()
        for cp in _own_copies(srcs_, lands_, own_sems, plan):
            cp.wait()
        for cp in _pass_on_copies(lands_, on_send, on_recv, plan):
            cp.wait_send()
            cp.wait_recv()

    hbm = lambda a: pltpu.HBM(a.shape, a.dtype)
    res = pl.pallas_call(
        body, name=name, out_shape=tuple(hbm(a) for a in list(srcs) + list(lands)),
        in_specs=[_HBM] * (2 * n) + [_SEM] * 5 + [pl.BlockSpec(memory_space=pl.ANY)] * len(after),
        out_specs=tuple([_HBM] * (2 * n)), input_output_aliases={i: i for i in range(2 * n)},
        compiler_params=pltpu.CompilerParams(has_side_effects=_EFFECT),
    )(*srcs, *lands, *sems, *on_sems, *after)
    return res[n:]


def _sum_parts(name, recv):
    def body(r_ref, o_ref):
        s = r_ref[0]
        for j in range(1, N_DEV):
            s = s + r_ref[j]
        o_ref[...] = s

    return pl.pallas_call(body, name=name, out_shape=_sds(recv.shape[1:], recv.dtype))(recv)


def _adam(name, recv, w, m, v, col=0):
    shape = w.shape
    R, C = shape
    tr = R
    while N_DEV * tr * C * 4 > 6 * 1024 * 1024 and tr % 32 == 0:
        tr //= 2

    def body(recv_ref, w_ref, m_ref, v_ref, g_ref, d_ref, m2_ref, v2_ref):
        g = recv_ref[0].astype(F32)
        for j in range(1, N_DEV):
            g = g + recv_ref[j].astype(F32)
        w_, m_, v_ = w_ref[...], m_ref[...], v_ref[...]
        m2 = ADAM_B1 * m_ + (1.0 - ADAM_B1) * g
        v2 = ADAM_B2 * v_ + (1.0 - ADAM_B2) * (g * g)
        m_hat = m2 / (1.0 - ADAM_B1 ** ADAM_STEP)
        v_hat = v2 / (1.0 - ADAM_B2 ** ADAM_STEP)
        g_ref[...] = g
        d_ref[...] = -ADAM_LR * (m_hat / (jnp.sqrt(v_hat) + ADAM_EPS) + ADAM_WD * w_)
        m2_ref[...] = m2
        v2_ref[...] = v2

    blk = pl.BlockSpec((tr, C), lambda i: (i, 0))
    return _pcall(body, name=name, grid=(R // tr,),
                  in_specs=[pl.BlockSpec((N_DEV, tr, C), lambda i: (0, i, col)), blk, blk, blk],
                  out_specs=[blk] * 4, out_shape=[_sds(shape)] * 4)(recv, w, m, v)


_NAMES = ['ffn1_pre_g', 'ffn1_w_in', 'ffn1_w_out', 'ffn1_post_g', 'mix_pre_g', 'w_in', 'w_fu', 'b_f', 'gla_norm_g',
          'w_pool', 'pool_scale', 'mem_norm_g', 'w_mem_kv', 'w_up_gla', 'w_up_pool', 'w_up_xattn', 'w_o', 'mix_post_g',
          'ffn2_pre_g', 'ffn2_w_in', 'ffn2_w_out', 'ffn2_post_g', 'final_g']
_SHARDED = ['ffn1_w_in', 'ffn1_w_out', 'w_in', 'w_fu', 'w_mem_kv', 'w_up_gla', 'w_up_pool', 'w_up_xattn', 'w_o',
            'ffn2_w_in', 'ffn2_w_out']
_COL_SHARDED = ['w_up_pool', 'w_up_xattn']


def _cols_to_full(g):
    return jnp.transpose(g, (1, 0, 2)).reshape(g.shape[1], N_DEV * g.shape[2])


def _full_to_cols(f):
    R, C = f.shape
    return jnp.transpose(f.reshape(R, N_DEV, C // N_DEV), (1, 0, 2))


def _to_internal(w_in_t):
    o = 0
    parts = []
    for s in IN_SPLITS:
        parts.append(w_in_t[o:o + s])
        o += s
    q, k, v, g_out, f_low, p_in, xq, gates = parts
    f_low = jnp.pad(f_low, ((0, FLOW_W - GATE_RANK), (0, 0)))
    return jnp.concatenate([q, k, v, g_out, gates, p_in, xq, f_low], axis=0)


def _from_internal(d, d_flow):
    q, k, v, g_out = d[0:512], d[512:1024], d[1024:2048], d[2048:3072]
    gates, p_in, xq = d[3072:6144], d[6144:6656], d[6656:7168]
    return jnp.concatenate([q, k, v, g_out, d_flow[:GATE_RANK], p_in, xq, gates], axis=0)


_FFN_IN = ('ffn1_w_in', 'ffn2_w_in')
_FFN_OUT = ('ffn1_w_out', 'ffn2_w_out')
_GATHERS = {"ffn1_in": ['ffn1_w_in'], "ffn1_out": ['ffn1_w_out'],
            "mix_in": ['w_in', 'w_fu'], "mix_rest": ['w_mem_kv', 'w_up_gla', 'w_up_pool', 'w_up_xattn', 'w_o'],
            "ffn2": ['ffn2_w_in', 'ffn2_w_out']}
_MIX = _GATHERS["mix_in"] + _GATHERS["mix_rest"]


def _ffn_in_slot(ref, d):
    return ref.at[2 * (d % N_FF_BLK) + d // N_FF_BLK, pl.ds(0, FF_BLK)]


def _ffn_out_slot(ref, d):
    rows = FF_BLK // 2
    return ref.at[d // 2, pl.ds(pl.multiple_of((d % 2) * rows, rows), rows)]


def _ffn_plan(names, scatter):
    slots, shapes = {}, {}
    for a, n in enumerate(names):
        if n in _FFN_IN:
            slots[a] = _ffn_in_slot
            shapes[a] = (FF_BLK, D_MODEL) if scatter else (N_DEV, FF_PAD, D_MODEL)
        elif n in _FFN_OUT:
            slots[a] = _ffn_out_slot
            shapes[a] = (FF_BLK // 2, D_MODEL) if scatter else (N_FF_BLK, FF_PAD, D_MODEL)
    return _Plan(scatter, slots, shapes)


def _rows_to_full(g):
    return g.reshape(N_DEV * g.shape[1], g.shape[2])


def _mix_weights(gathered):
    W = {}
    for n, g in gathered.items():
        if n == "w_in":
            W["w_int_t"] = _to_internal(g.reshape(IN_WIDTH, D_MODEL))
        elif n == "w_fu":
            W["w_fu_pad"] = jnp.pad(_cols_to_full(g), ((0, FLOW_W - GATE_RANK), (0, 0)))
        else:
            W[n] = _cols_to_full(g) if n in _COL_SHARDED else _rows_to_full(g)
    return W


def _mix_chunks(G, n):
    if n == "w_in":
        return _from_internal(*G["w_int_t"]).reshape(N_DEV, IN_SHARD, D_MODEL)
    if n == "w_fu":
        return _full_to_cols(G["w_fu_pad"][:GATE_RANK].astype(BF16))
    if n in _COL_SHARDED:
        return _full_to_cols(G[n])
    return G[n].reshape(N_DEV, G[n].shape[0] // N_DEV, G[n].shape[1])


def _step(x, mem, tgt, P, Mo, Vo):
    def native(n, a):
        return jnp.swapaxes(a, 0, 1) if n in _FFN_IN + ("w_in",) else a

    P, Mo, Vo = ({n: native(n, a) for n, a in d.items()} for d in (P, Mo, Vo))
    small = {n: P[n] for n in _NAMES if n not in _SHARDED}

    gather, tokens = {}, []
    for grp, names in _GATHERS.items():
        gather[grp], tok = _gather_start("gather_" + grp, [P[n].astype(BF16) for n in names], _ffn_plan(names, False),
                                         tokens[-1:])
        tokens.append(tok)

    def pass_on(grp, after):
        return _gather_pass_on("gather_" + grp + "_on", gather[grp], after)

    def gathered(grp, after):
        return _gather_wait("gather_" + grp + "_wait", pass_on(grp, after)[0], [])

    def ffn_wo(lands):
        return lambda act: lands[0].reshape(N_FF_BLK * FF_PAD, D_MODEL)

    h1 = _pre_norm("ffn1_pre", x, small["ffn1_pre_g"], tokens[:1])
    w1t = gathered("ffn1_in", tokens[-1:] + [h1])[0].reshape(N_DEV * FF_PAD, D_MODEL)
    (x1, hm), w1o, sv1 = _ffn_fwd("ffn1", x, h1, w1t, small["ffn1_post_g"],
                                  lambda act: ffn_wo(gathered("ffn1_out", [act]))(act), small["mix_pre_g"])

    Wm = {**small, **_mix_weights(dict(zip(_GATHERS["mix_in"], gathered("mix_in", [x1]))))}
    passed = {}

    def pass_on_later(proj):
        tokens = []
        for grp in ("mix_rest", "ffn2"):
            passed[grp], tok = pass_on(grp, [proj])
            tokens.append(tok)
        return tokens

    def mix_rest(after):
        names = _GATHERS["mix_rest"]
        return _mix_weights(dict(zip(names, _gather_wait("gather_mix_rest_wait", passed["mix_rest"], after))))

    x2, h2, Wm, svm = _mix_fwd(x1, hm, mem, Wm, small["ffn2_pre_g"], pass_on_later, mix_rest)
    w2t, w2o = _gather_wait("gather_ffn2_wait", passed["ffn2"], [x2])
    w2t = w2t.reshape(N_DEV * FF_PAD, D_MODEL)
    (loss, dx3, d_final_g), w2o, sv2 = _ffn_fwd("ffn2", x2, h2, w2t, small["ffn2_post_g"], ffn_wo([w2o]),
                                                small["final_g"], tgt)

    G = dict(final_g=d_final_g)
    scat = {}

    def start(grp, names, arrays):
        scat[grp] = names, _exchange_start("scatter_" + grp, arrays, _ffn_plan(names, True))
        return scat[grp][1][1]

    def ffn_starts(tag):
        return (lambda dwo: start(tag + "_out", [tag + "_w_out"], [dwo.reshape(N_FF_BLK, FF_PAD, D_MODEL)]),
                lambda dwt: start(tag + "_in", [tag + "_w_in"], [dwt.reshape(N_DEV, FF_PAD, D_MODEL)]))

    dx2, G["ffn2_pre_g"], G["ffn2_post_g"] = _ffn_bwd(
        "ffn2", dx3, x2, small["ffn2_pre_g"], w2t, w2o, small["ffn2_post_g"], sv2, *ffn_starts("ffn2"))
    dx1, Gm = _mix_bwd(dx2, x1, mem, Wm, svm,
                       lambda Gm: start("mix", _MIX, [_mix_chunks(Gm, n) for n in _MIX]))
    G.update(Gm)
    dx, G["ffn1_pre_g"], G["ffn1_post_g"] = _ffn_bwd(
        "ffn1", dx1, x, small["ffn1_pre_g"], w1t, w1o, small["ffn1_post_g"], sv1, *ffn_starts("ffn1"))

    vecs = sorted((n for n in small if P[n].ndim == 2), key=lambda n: -P[n].shape[1])
    packed = jnp.concatenate([G[n] for n in vecs] + [loss], axis=1)
    place, o = {}, 0
    for n in vecs:
        place[n] = o // P[n].shape[1]
        o += P[n].shape[1]
    scat["small"] = ["packed", "w_pool"], _exchange_start("gather_small_grads", [packed, G["w_pool"]], _GATHER)
    recv, outs = {}, {}
    done = [dx]
    for grp in ["ffn2_out", "ffn2_in", "mix", "ffn1_out", "ffn1_in", "small"]:
        names, (state, _) = scat[grp]
        recv.update(zip(names, _exchange_wait(("gather_" if grp == "small" else "scatter_") + grp + "_wait", state,
                                              done)))
        if grp == "small":
            loss = _sum_parts("loss_sum", recv["packed"][:, :, o:])[0, 0]
            names = list(small)
        for n in names:
            shp = P[n].shape
            shp2 = shp if len(shp) == 2 else (shp[0] * shp[1], shp[2])
            r = recv["packed"] if n in place else recv[n].reshape((N_DEV,) + shp2)
            res = _adam("adam_" + n, r, P[n].reshape(shp2), Mo[n].reshape(shp2), Vo[n].reshape(shp2), place.get(n, 0))
            outs[n] = [native(n, t.reshape(shp)) for t in res]
            done.append(res[-1])
    return loss, dx, outs


def kernel(x, mem, ffn1_pre_g, ffn1_w_in, ffn1_w_out, ffn1_post_g, mix_pre_g, w_in, w_fu, b_f, gla_norm_g, w_pool, pool_scale, mem_norm_g, w_mem_kv, w_up_gla, w_up_pool, w_up_xattn, w_o, mix_post_g, ffn2_pre_g, ffn2_w_in, ffn2_w_out, ffn2_post_g, final_g, loss_target, m_ffn1_pre_g, m_ffn1_w_in, m_ffn1_w_out, m_ffn1_post_g, m_mix_pre_g, m_w_in, m_w_fu, m_b_f, m_gla_norm_g, m_w_pool, m_pool_scale, m_mem_norm_g, m_w_mem_kv, m_w_up_gla, m_w_up_pool, m_w_up_xattn, m_w_o, m_mix_post_g, m_ffn2_pre_g, m_ffn2_w_in, m_ffn2_w_out, m_ffn2_post_g, m_final_g, v_ffn1_pre_g, v_ffn1_w_in, v_ffn1_w_out, v_ffn1_post_g, v_mix_pre_g, v_w_in, v_w_fu, v_b_f, v_gla_norm_g, v_w_pool, v_pool_scale, v_mem_norm_g, v_w_mem_kv, v_w_up_gla, v_w_up_pool, v_w_up_xattn, v_w_o, v_mix_post_g, v_ffn2_pre_g, v_ffn2_w_in, v_ffn2_w_out, v_ffn2_post_g, v_final_g):
    params = [ffn1_pre_g, ffn1_w_in, ffn1_w_out, ffn1_post_g, mix_pre_g, w_in, w_fu, b_f, gla_norm_g, w_pool, pool_scale, mem_norm_g, w_mem_kv, w_up_gla, w_up_pool, w_up_xattn, w_o, mix_post_g, ffn2_pre_g, ffn2_w_in, ffn2_w_out, ffn2_post_g, final_g]
    moms = [m_ffn1_pre_g, m_ffn1_w_in, m_ffn1_w_out, m_ffn1_post_g, m_mix_pre_g, m_w_in, m_w_fu, m_b_f, m_gla_norm_g, m_w_pool, m_pool_scale, m_mem_norm_g, m_w_mem_kv, m_w_up_gla, m_w_up_pool, m_w_up_xattn, m_w_o, m_mix_post_g, m_ffn2_pre_g, m_ffn2_w_in, m_ffn2_w_out, m_ffn2_post_g, m_final_g]
    vars_ = [v_ffn1_pre_g, v_ffn1_w_in, v_ffn1_w_out, v_ffn1_post_g, v_mix_pre_g, v_w_in, v_w_fu, v_b_f, v_gla_norm_g, v_w_pool, v_pool_scale, v_mem_norm_g, v_w_mem_kv, v_w_up_gla, v_w_up_pool, v_w_up_xattn, v_w_o, v_mix_post_g, v_ffn2_pre_g, v_ffn2_w_in, v_ffn2_w_out, v_ffn2_post_g, v_final_g]
    P = {n: a[0] if a.ndim > 2 else a for n, a in zip(_NAMES, params)}
    Mo = {n: a[0] if a.ndim > 2 else a for n, a in zip(_NAMES, moms)}
    Vo = {n: a[0] if a.ndim > 2 else a for n, a in zip(_NAMES, vars_)}
    loss, dx, outs = _step(x[0], mem[0], loss_target[0], P, Mo, Vo)
    out = [loss, dx[None]]
    for kind in range(4):
        for n, p in zip(_NAMES, params):
            out.append(outs[n][kind].reshape(p.shape))
    return tuple(out)
```

```python
import functools

import jax
import jax.numpy as jnp
from jax import lax
from jax.experimental import pallas as pl
from jax.experimental.pallas import tpu as pltpu

F32 = jnp.float32
BF16 = jnp.bfloat16

N_DEV = 8
D_MODEL = 1024
D_FF = 2816
FF_BLK = 2 * D_FF // N_DEV
N_FF_BLK = D_FF // FF_BLK
FF_PAD = 768
CHUNK = 64
GLA_HEADS = 4
GLA_DK = 512
GLA_DV = 1024
GLA_HDK = 128
GLA_HDV = 256
GATE_RANK = 16
GATE_TEMP = 16.0
POOL_WINDOWS = (2, 4, 8, 16)
POOL_W = 512
POOL_GD = 128
POOL_HALO = 16
XA_HEADS = 4
XA_HD = 128
XA_W = 512
EPS = 1e-6
IN_SPLITS = (GLA_DK, GLA_DK, GLA_DV, GLA_DV, GATE_RANK, POOL_W, XA_W, 3 * D_MODEL)
IN_WIDTH = sum(IN_SPLITS)
IN_SHARD = IN_WIDTH // N_DEV
INT_W = 3072 + 3072 + 1024 + 128
INT_NBLK = 3
INT_BLK = INT_W // INT_NBLK
FLOW_W = 128
INT_MAIN = INT_W - FLOW_W
_POOL_COL = (3072 + 3072) // POOL_W
_XQ_COL = (3072 + 3072 + POOL_W) // XA_W

ADAM_LR = 0.001
ADAM_B1 = 0.9
ADAM_B2 = 0.999
ADAM_EPS = 1e-08
ADAM_WD = 0.01
ADAM_STEP = 10

VMEM_LIMIT = 56 * 1024 * 1024

_NN = (((1,), (0,)), ((), ()))
_NT = (((1,), (1,)), ((), ()))
_TN = (((0,), (0,)), ((), ()))


def _pcall(body, *, name, grid, in_specs, out_specs, out_shape, scratch=(), aliases=None):
    return pl.pallas_call(
        body, name=name, grid=grid, in_specs=in_specs, out_specs=out_specs, out_shape=out_shape,
        scratch_shapes=list(scratch), input_output_aliases=aliases or {},
        compiler_params=pltpu.CompilerParams(dimension_semantics=("arbitrary",) * len(grid),
                                             vmem_limit_bytes=VMEM_LIMIT))


def _dot(a, b, dims=_NN):
    return lax.dot_general(a.astype(BF16), b.astype(BF16), dims, preferred_element_type=F32)


def _mm(name, a, b, *, grid, a_blk, a_map, b_blk, b_map, o_shape, o_blk, o_map, dims, out_dtype=F32, deps=()):
    nk = grid[2]

    def body(a_ref, b_ref, *rest):
        o_ref, scr = rest[len(deps)], rest[len(deps) + 1:]
        p = _dot(a_ref[...], b_ref[...], dims)
        if nk == 1:
            o_ref[...] = p.astype(o_ref.dtype)
        else:
            acc = scr[0]
            k = pl.program_id(2)

            @pl.when(k == 0)
            def _():
                acc[...] = p

            @pl.when(k > 0)
            def _():
                acc[...] += p

            @pl.when(k == nk - 1)
            def _():
                o_ref[...] = acc[...].astype(o_ref.dtype)

    acc_shape = tuple(d for d in o_blk if d is not None)
    return _pcall(body, name=name, grid=grid,
                  in_specs=[pl.BlockSpec(a_blk, a_map), pl.BlockSpec(b_blk, b_map)]
                  + [pl.BlockSpec(memory_space=pl.ANY)] * len(deps),
                  out_specs=pl.BlockSpec(o_blk, o_map),
                  out_shape=jax.ShapeDtypeStruct(o_shape, out_dtype),
                  scratch=[pltpu.VMEM(acc_shape, F32)] if nk > 1 else [])(a, b, *deps)


def _mm_nn(name, a, b, out_dtype=F32, tm=1024, tn=None):
    M, K = a.shape
    N = b.shape[1]
    tm, tn = min(tm, M), (tn or N)
    return _mm(name, a, b, grid=(N // tn, M // tm, 1), a_blk=(tm, K), a_map=lambda j, i, k: (i, 0),
               b_blk=(K, tn), b_map=lambda j, i, k: (0, j), o_shape=(M, N), o_blk=(tm, tn),
               o_map=lambda j, i, k: (i, j), dims=_NN, out_dtype=out_dtype)


def _mm_nt(name, a, b, out_dtype=F32, tm=1024):
    M, K = a.shape
    N = b.shape[0]
    tm = min(tm, M)
    return _mm(name, a, b, grid=(1, M // tm, 1), a_blk=(tm, K), a_map=lambda j, i, k: (i, 0),
               b_blk=(N, K), b_map=lambda j, i, k: (0, 0), o_shape=(M, N), o_blk=(tm, N),
               o_map=lambda j, i, k: (i, 0), dims=_NT, out_dtype=out_dtype)


def _mm_tn(name, a, b, out_dtype=BF16, ts=512, tn=None, tn_a=None, deps=()):
    S, M = a.shape
    N = b.shape[1]
    ts, tn, tn_a = min(ts, S), (tn or N), (tn_a or M)
    return _mm(name, a, b, grid=((N // tn) * (M // tn_a), 1, S // ts), a_blk=(ts, tn_a),
               a_map=lambda j, i, k: (k, j if tn_a < M else 0), b_blk=(ts, tn),
               b_map=lambda j, i, k: (k, j if tn < N else 0), o_shape=(M, N), o_blk=(tn_a, tn),
               o_map=lambda j, i, k: (j, 0) if tn_a < M else (0, j), dims=_TN, out_dtype=out_dtype, deps=deps)


class _Win:
    def __init__(self, arr, w, c):
        self.arr, self.w, self.c = arr, w, c


def _row_spec(x, tm):
    if isinstance(x, _Win):
        return x.arr, pl.BlockSpec((tm, x.w), functools.partial(lambda i, c: (i, c), c=x.c))
    if x.ndim == 3:
        return x, pl.BlockSpec((x.shape[0], tm, x.shape[2]), lambda i: (0, i, 0))
    return x, pl.BlockSpec((tm, x.shape[1]), lambda i: (i, 0))


def _rowwise(name, fn, rows, consts, outs, accs=(), tm=512):
    first = rows[0].arr if isinstance(rows[0], _Win) else rows[0]
    S = first.shape[1] if first.ndim == 3 else first.shape[0]
    tm = min(tm, S)
    n_in, n_out = len(rows) + len(consts), len(outs)
    arrays, in_specs = [], []
    for r in rows:
        arr, spec = _row_spec(r, tm)
        arrays.append(arr)
        in_specs.append(spec)
    for c in consts:
        arrays.append(c)
        in_specs.append(pl.BlockSpec(c.shape, functools.partial(lambda i, n: (0,) * n, n=c.ndim)))
    aliases = {}
    for k, o in enumerate(outs):
        if isinstance(o, _Win):
            aliases[len(arrays)] = k
            arrays.append(o.arr)
            in_specs.append(pl.BlockSpec(memory_space=pl.ANY))
    n_thru = len(aliases)
    out_specs = [_row_spec(o, tm)[1] for o in outs]
    out_specs += [pl.BlockSpec(a.shape, functools.partial(lambda i, n: (0,) * n, n=len(a.shape))) for a in accs]
    out_shape = [_sds(o.arr.shape, o.arr.dtype) if isinstance(o, _Win) else _sds(o.shape, o.dtype) for o in outs]

    def body(*refs):
        res = fn(*[r[...] for r in refs[:n_in]])
        if not isinstance(res, (tuple, list)):
            res = (res,)
        orefs = refs[n_in + n_thru:]
        for r, v in zip(orefs[:n_out], res[:n_out]):
            r[...] = v.astype(r.dtype)
        i = pl.program_id(0)
        for r, v in zip(orefs[n_out:], res[n_out:]):
            @pl.when(i == 0)
            def _(r=r, v=v):
                r[...] = v.astype(r.dtype)

            @pl.when(i > 0)
            def _(r=r, v=v):
                r[...] += v.astype(r.dtype)

    return _pcall(body, name=name, grid=(S // tm,), in_specs=in_specs, out_specs=out_specs,
                  out_shape=out_shape + [_sds(a.shape, a.dtype) for a in accs], aliases=aliases)(*arrays)


def _sds(shape, dtype=F32):
    return jax.ShapeDtypeStruct(shape, dtype)


def _rms(x, g):
    return x * lax.rsqrt(jnp.mean(x * x, axis=-1, keepdims=True) + EPS) * g


def _rms_bwd(x, g, dy):
    _, vjp = jax.vjp(_rms, x, g)
    return vjp(dy)


def _sigmoid(x):
    return 0.5 * jnp.tanh(0.5 * x) + 0.5


def _silu(x):
    return x * _sigmoid(x)


def _log_sigmoid(f):
    return jnp.minimum(f, 0.0) - jnp.log(1.0 + jnp.exp(-jnp.abs(f)))


def _head_rms_gate(o, g_out, gn):
    parts = [_rms(o[:, h * GLA_HDV:(h + 1) * GLA_HDV], gn[:, h * GLA_HDV:(h + 1) * GLA_HDV]) for h in range(GLA_HEADS)]
    return jnp.concatenate(parts, axis=-1) * _silu(g_out.astype(F32))


def _merge(gates, ya, yb, yc):
    gates, ya, yb, yc = (t.astype(F32) for t in (gates, ya, yb, yc))
    return (_sigmoid(gates[:, :D_MODEL]) * ya + _sigmoid(gates[:, D_MODEL:2 * D_MODEL]) * yb
            + _sigmoid(gates[:, 2 * D_MODEL:]) * yc)


def _tri_dot(t, x):
    hi = x.astype(BF16)
    r1 = x - hi.astype(F32)
    mid = r1.astype(BF16)
    lo = (r1 - mid.astype(F32)).astype(BF16)
    d = functools.partial(lax.dot_general, dimension_numbers=_NN, preferred_element_type=F32)
    return d(t, hi) + d(t, mid) + d(t, lo)


def _pre_norm(name, x, g, after):
    S = x.shape[0]
    tm = min(1024, S)

    def body(x_ref, g_ref, *rest):
        rest[-1][...] = _rms(x_ref[...], g_ref[...]).astype(BF16)

    row = pl.BlockSpec((tm, D_MODEL), lambda i: (i, 0))
    return _pcall(body, name=name, grid=(S // tm,),
                  in_specs=[row, pl.BlockSpec((1, D_MODEL), lambda i: (0, 0))] + [pl.BlockSpec(memory_space=pl.ANY)] * len(after),
                  out_specs=row, out_shape=_sds(x.shape, BF16))(x, g, *after)


def _ffn_fwd(tag, x, h, wt, gpost, get_wo, g_next, tgt=None):
    S = x.shape[0]
    tm = min(512, S)

    def up_body(h_ref, w_ref, u_ref, act_ref):
        hh = h_ref[...]
        for j in range(N_FF_BLK):
            ab = _dot(hh, w_ref[2 * FF_PAD * j:2 * FF_PAD * (j + 1), :], _NT)
            u_ref[:, 2 * FF_PAD * j:2 * FF_PAD * (j + 1)] = ab.astype(BF16)
            act_ref[:, FF_PAD * j:FF_PAD * (j + 1)] = (_silu(ab[:, :FF_PAD]) * ab[:, FF_PAD:]).astype(BF16)

    def down_head_body(act_ref, w_ref, x_ref, g_ref, gn_ref, t_ref, f_ref, dxo_ref, loss_ref, dgn_ref):
        i = pl.program_id(0)
        f = _dot(act_ref[...], w_ref[...])
        f_ref[...] = f.astype(BF16)
        xo = x_ref[...] + 0.5 * _rms(f, g_ref[...])
        out, vjp = jax.vjp(_rms, xo, gn_ref[...])
        e = out - t_ref[...]
        loss = 0.5 * jnp.sum(jnp.mean(e * e, axis=-1, keepdims=True), axis=0, keepdims=True)
        dxo_ref[...], dgn = vjp(e * (1.0 / D_MODEL))

        @pl.when(i == 0)
        def _():
            loss_ref[...] = jnp.broadcast_to(loss, loss_ref.shape)
            dgn_ref[...] = dgn

        @pl.when(i > 0)
        def _():
            loss_ref[...] += jnp.broadcast_to(loss, loss_ref.shape)
            dgn_ref[...] += dgn

    if tgt is not None:
        def last_body(h_ref, wt_ref, wo_ref, x_ref, g_ref, gn_ref, t_ref, u_ref, act_ref, f_ref, dxo_ref, loss_ref,
                      dgn_ref):
            up_body(h_ref, wt_ref, u_ref, act_ref)
            down_head_body(act_ref, wo_ref, x_ref, g_ref, gn_ref, t_ref, f_ref, dxo_ref, loss_ref, dgn_ref)

        wo = get_wo(None)
        tl = min(256, S)
        row = pl.BlockSpec((tl, D_MODEL), lambda i: (i, 0))
        vec = pl.BlockSpec((1, D_MODEL), lambda i: (0, 0))
        u, act, f, dxo, loss, dg_next = _pcall(
            last_body, name=tag + "_fwd", grid=(S // tl,),
            in_specs=[row, pl.BlockSpec(wt.shape, lambda i: (0, 0), pipeline_mode=pl.Buffered(1)),
                      pl.BlockSpec(wo.shape, lambda i: (0, 0), pipeline_mode=pl.Buffered(1)), row, vec, vec, row],
            out_specs=[pl.BlockSpec((tl, N_DEV * FF_PAD), lambda i: (i, 0)),
                       pl.BlockSpec((tl, N_FF_BLK * FF_PAD), lambda i: (i, 0)), row, row,
                       pl.BlockSpec((1, 128), lambda i: (0, 0)), vec],
            out_shape=[_sds((S, N_DEV * FF_PAD), BF16), _sds((S, N_FF_BLK * FF_PAD), BF16), _sds((S, D_MODEL), BF16),
                       _sds((S, D_MODEL)), _sds((1, 128)), _sds((1, D_MODEL))],
        )(h, wt, wo, x, gpost, g_next, tgt)
        return (loss, dxo, dg_next), wo, (h, u, act, f)

    u, act = _pcall(
        up_body, name=tag + "_up", grid=(S // tm,),
        in_specs=[pl.BlockSpec((tm, D_MODEL), lambda i: (i, 0)),
                  pl.BlockSpec(wt.shape, lambda i: (0, 0), pipeline_mode=pl.Buffered(1))],
        out_specs=[pl.BlockSpec((tm, N_DEV * FF_PAD), lambda i: (i, 0)),
                   pl.BlockSpec((tm, N_FF_BLK * FF_PAD), lambda i: (i, 0))],
        out_shape=[_sds((S, N_DEV * FF_PAD), BF16), _sds((S, N_FF_BLK * FF_PAD), BF16)],
    )(h, wt)

    def down_body(act_ref, w_ref, x_ref, g_ref, gn_ref, f_ref, xo_ref, hn_ref):
        f = _dot(act_ref[...], w_ref[...])
        f_ref[...] = f.astype(BF16)
        xo = x_ref[...] + 0.5 * _rms(f, g_ref[...])
        xo_ref[...] = xo
        hn_ref[...] = _rms(xo, gn_ref[...]).astype(BF16)

    wo = get_wo(act)
    row = pl.BlockSpec((tm, D_MODEL), lambda i: (i, 0))
    vec = pl.BlockSpec((1, D_MODEL), lambda i: (0, 0))
    f, xo, h_next = _pcall(
        down_body, name=tag + "_down", grid=(S // tm,),
        in_specs=[pl.BlockSpec((tm, N_FF_BLK * FF_PAD), lambda i: (i, 0)),
                  pl.BlockSpec(wo.shape, lambda i: (0, 0), pipeline_mode=pl.Buffered(1)), row, vec, vec],
        out_specs=[row, row, row], out_shape=[_sds((S, D_MODEL), BF16), _sds((S, D_MODEL)), _sds((S, D_MODEL), BF16)],
    )(act, wo, x, gpost, g_next)
    return (xo, h_next), wo, (h, u, act, f)


def _ffn_bwd(tag, dxo, x, gpre, wt, wo, gpost, saved, on_dwo=None, on_dwt=None):
    h, u, act, f = saved
    S = x.shape[0]
    tm = min(512, S)

    def dact_body(f_ref, dxo_ref, g_ref, w_ref, u_ref, df_ref, du_ref, dg_ref):
        i = pl.program_id(0)
        df, dg = _rms_bwd(f_ref[...].astype(F32), g_ref[...], 0.5 * dxo_ref[...])
        df = df.astype(BF16)
        df_ref[...] = df

        @pl.when(i == 0)
        def _():
            dg_ref[...] = dg

        @pl.when(i > 0)
        def _():
            dg_ref[...] += dg

        for j in range(N_FF_BLK):
            dact = _dot(df, w_ref[FF_PAD * j:FF_PAD * (j + 1), :], _NT)
            ab = u_ref[:, 2 * FF_PAD * j:2 * FF_PAD * (j + 1)].astype(F32)
            a, b = ab[:, :FF_PAD], ab[:, FF_PAD:]
            sg = _sigmoid(a)
            du_ref[:, 2 * FF_PAD * j:2 * FF_PAD * j + FF_PAD] = (dact * b * (sg * (1.0 + a * (1.0 - sg)))).astype(BF16)
            du_ref[:, 2 * FF_PAD * j + FF_PAD:2 * FF_PAD * (j + 1)] = (dact * (a * sg)).astype(BF16)

    row = pl.BlockSpec((tm, D_MODEL), lambda i: (i, 0))
    vec = pl.BlockSpec((1, D_MODEL), lambda i: (0, 0))
    u_spec = pl.BlockSpec((tm, N_DEV * FF_PAD), lambda i: (i, 0))
    df, du, dgpost = _pcall(
        dact_body, name=tag + "_dact", grid=(S // tm,),
        in_specs=[row, row, vec, pl.BlockSpec(wo.shape, lambda i: (0, 0), pipeline_mode=pl.Buffered(1)), u_spec],
        out_specs=[row, u_spec, vec],
        out_shape=[_sds((S, D_MODEL), BF16), _sds((S, N_DEV * FF_PAD), BF16), _sds((1, D_MODEL))])(f, dxo, gpost, wo, u)
    dwo = _mm_tn(tag + "_dwo", act, df, ts=S, tn_a=FF_PAD)
    dwt = _mm_tn(tag + "_dwt", du, h, ts=S, tn_a=FF_PAD, deps=[on_dwo(dwo)] if on_dwo is not None else [])
    dx, dgpre = _dh_pre_norm_bwd(tag + "_dh", [du], wt, x, dxo, gpre, [on_dwt(dwt)] if on_dwt is not None else [])
    return dx, dgpre, dgpost


def _dh_pre_norm_bwd(name, dzs, wt, x, dres, g, deps):
    S = x.shape[0]
    ks = [dz.shape[1] for dz in dzs]

    def body(*refs):
        dz_refs = refs[:len(dzs)]
        w_ref, x_ref, dres_ref, g_ref = refs[len(dzs):len(dzs) + 4]
        dx_ref, dg_ref = refs[len(dzs) + 4 + len(deps):]
        i = pl.program_id(0)
        dh, o = None, 0
        for dz_ref, k in zip(dz_refs, ks):
            p = _dot(dz_ref[...], w_ref[o:o + k, :])
            dh, o = (p if dh is None else dh + p), o + k
        dx, dg = _rms_bwd(x_ref[...], g_ref[...], dh)
        dx_ref[...] = dx + dres_ref[...]

        @pl.when(i == 0)
        def _():
            dg_ref[...] = dg

        @pl.when(i > 0)
        def _():
            dg_ref[...] += dg

    th = min(512, S)
    row = pl.BlockSpec((th, D_MODEL), lambda i: (i, 0))
    vec = pl.BlockSpec((1, D_MODEL), lambda i: (0, 0))
    return _pcall(
        body, name=name, grid=(S // th,),
        in_specs=[pl.BlockSpec((th, k), lambda i: (i, 0)) for k in ks]
        + [pl.BlockSpec(wt.shape, lambda i: (0, 0), pipeline_mode=pl.Buffered(1)), row, row, vec]
        + [pl.BlockSpec(memory_space=pl.ANY)] * len(deps),
        out_specs=[row, vec], out_shape=[_sds((S, D_MODEL)), _sds((1, D_MODEL))])(*dzs, wt, x, dres, g, *deps)


GLA_G = 8


def _gla_tile_common(k, flow, wfu, bf):
    f = _dot(flow, wfu) + bf
    la = _log_sigmoid(f) * (1.0 / GATE_TEMP)
    tri = _tri_matrix(True)
    ws, ds = [], []
    for g in range(k.shape[0] // CHUNK):
        b = _tri_dot(tri, la[g * CHUNK:(g + 1) * CHUNK])
        b_end = b[CHUNK - 1:CHUNK, :]
        ws.append(jnp.exp(b_end - b))
        ds.append(jnp.exp(b_end))
    w = jnp.concatenate(ws, axis=0)
    return f, w, k * w, ds


def _tri_matrix(lower):
    r = lax.broadcasted_iota(jnp.int32, (CHUNK, CHUNK), 0)
    c = lax.broadcasted_iota(jnp.int32, (CHUNK, CHUNK), 1)
    return jnp.where((r >= c) if lower else (r <= c), 1.0, 0.0).astype(BF16)


def _heads():
    return [(slice(h * GLA_HDK, (h + 1) * GLA_HDK), slice(h * GLA_HDV, (h + 1) * GLA_HDV)) for h in range(GLA_HEADS)]


def _gla_fwd(proj, wfu, bf, gn, deps=()):
    S = proj.shape[0]
    G = min(GLA_G, S // CHUNK)
    T = G * CHUNK
    nc = S // CHUNK

    def body(q_ref, k_ref, v_ref, go_ref, flow_ref, wfu_ref, bf_ref, gn_ref, *rest):
        ya_ref, st_ref, state = rest[len(deps):]
        @pl.when(pl.program_id(0) == 0)
        def _():
            state[...] = jnp.zeros_like(state)

        _, _, kt, ds = _gla_tile_common(k_ref[...], flow_ref[...], wfu_ref[...], bf_ref[...])
        q = q_ref[...].astype(F32) * (GLA_HDK ** -0.5)
        v = v_ref[...]
        rows = [slice(g * CHUNK, (g + 1) * CHUNK) for g in range(G)]
        kv = [[_dot(v[r, vs], kt[r, ks], _TN) for ks, vs in _heads()] for r in rows]
        st = [state[vs, :] for _, vs in _heads()]
        o = []
        for g, r in enumerate(rows):
            outs = []
            for h, (ks, vs) in enumerate(_heads()):
                st[h] = st[h] * ds[g][:, ks] + kv[g][h]
                st_ref[g, vs, :] = st[h]
                outs.append(_dot(q[r, ks], st[h], _NT))
            o.append(jnp.concatenate(outs, axis=-1))
        for h, (_, vs) in enumerate(_heads()):
            state[vs, :] = st[h]
        ya_ref[...] = _head_rms_gate(jnp.concatenate(o, axis=0), go_ref[...], gn_ref[...]).astype(BF16)

    return _pcall(
        body, name="gla_fwd", grid=(S // T,),
        in_specs=[pl.BlockSpec((T, GLA_DK), lambda c: (c, 0)), pl.BlockSpec((T, GLA_DK), lambda c: (c, 1)),
                  pl.BlockSpec((T, GLA_DV), lambda c: (c, 1)), pl.BlockSpec((T, GLA_DV), lambda c: (c, 2)),
                  pl.BlockSpec((T, FLOW_W), lambda c: (c, (INT_W - FLOW_W) // FLOW_W)),
                  pl.BlockSpec(wfu.shape, lambda c: (0, 0)), pl.BlockSpec(bf.shape, lambda c: (0, 0)),
                  pl.BlockSpec(gn.shape, lambda c: (0, 0))]
        + [pl.BlockSpec(memory_space=pl.ANY)] * len(deps),
        out_specs=[pl.BlockSpec((T, GLA_DV), lambda c: (c, 0)),
                   pl.BlockSpec((G, GLA_DV, GLA_HDK), lambda c: (c, 0, 0))],
        out_shape=[_sds((S, GLA_DV), BF16), _sds((nc, GLA_DV, GLA_HDK))],
        scratch=[pltpu.VMEM((GLA_DV, GLA_HDK), F32)])(proj, proj, proj, proj, proj, wfu, bf, gn, *deps)


def _gla_bwd(proj, wfu, bf, gn, states, d_ya_in, dmain):
    S = proj.shape[0]
    G = min(GLA_G, S // CHUNK)
    T = G * CHUNK
    nt = S // T

    def body(q_ref, k_ref, v_ref, go_ref, flow_ref, wfu_ref, bf_ref, gn_ref, st_ref, stp_ref, dya_ref, dmain_in,
             dqkvg_ref, dflow_ref, dwfu_ref, dbf_ref, dgn_ref, dstate):
        dq_ref = dqkvg_ref.at[:, 0:GLA_DK]
        dk_ref = dqkvg_ref.at[:, GLA_DK:2 * GLA_DK]
        dv_ref = dqkvg_ref.at[:, 2 * GLA_DK:2 * GLA_DK + GLA_DV]
        dgo_ref = dqkvg_ref.at[:, 2 * GLA_DK + GLA_DV:]
        step = pl.program_id(0)

        @pl.when(step == 0)
        def _():
            dstate[...] = jnp.zeros_like(dstate)
            dwfu_ref[...] = jnp.zeros_like(dwfu_ref)
            dbf_ref[...] = jnp.zeros_like(dbf_ref)
            dgn_ref[...] = jnp.zeros_like(dgn_ref)

        flow, wfu_v = flow_ref[...], wfu_ref[...]
        f, w, kt, ds = _gla_tile_common(k_ref[...], flow, wfu_v, bf_ref[...])
        q = q_ref[...].astype(F32) * (GLA_HDK ** -0.5)
        v = v_ref[...]
        rows = [slice(g * CHUNK, (g + 1) * CHUNK) for g in range(G)]
        o = jnp.concatenate([jnp.concatenate([_dot(q[r, ks], st_ref[g, vs, :], _NT) for ks, vs in _heads()], axis=-1)
                             for g, r in enumerate(rows)], axis=0)
        _, vjp = jax.vjp(_head_rms_gate, o, go_ref[...], gn_ref[...])
        dout, dgo, dgn = vjp(dya_ref[...].astype(F32))
        dgo_ref[...] = dgo.astype(dgo_ref.dtype)
        dgn_ref[...] += dgn
        dq = [jnp.concatenate([_dot(dout[r, vs], st_ref[g, vs, :]) for _, vs in _heads()], axis=-1)
              for g, r in enumerate(rows)]
        qdo = [[_dot(dout[r, vs], q[r, ks], _TN) for ks, vs in _heads()] for r in rows]
        dq_ref[...] = (jnp.concatenate(dq, axis=0) * (GLA_HDK ** -0.5)).astype(dq_ref.dtype)
        has_prev = (step < nt - 1).astype(F32)
        carry = [dstate[vs, :] for _, vs in _heads()]
        dkt, dv, dd = [None] * G, [None] * G, [None] * G
        for g in reversed(range(G)):
            r = rows[g]
            dkts, dvs, dds = [], [], []
            for h, (ks, vs) in enumerate(_heads()):
                dst = carry[h] + qdo[g][h]
                dkts.append(_dot(v[r, vs], dst))
                dvs.append(_dot(kt[r, ks], dst, _NT))
                st_prev = st_ref[g - 1, vs, :] if g > 0 else stp_ref[vs, :] * has_prev
                dds.append(jnp.sum(dst * st_prev, axis=0, keepdims=True))
                carry[h] = dst * ds[g][:, ks]
            dkt[g], dv[g], dd[g] = (jnp.concatenate(t, axis=-1) for t in (dkts, dvs, dds))
        for h, (_, vs) in enumerate(_heads()):
            dstate[vs, :] = carry[h]
        dkt = jnp.concatenate(dkt, axis=0)
        dv_ref[...] = jnp.concatenate(dv, axis=0).astype(dv_ref.dtype)
        dk_ref[...] = (dkt * w).astype(dk_ref.dtype)
        de = dkt * kt
        tri = _tri_matrix(False)
        dla = []
        for g, r in enumerate(rows):
            db_end = jnp.sum(de[r], axis=0, keepdims=True) + dd[g] * ds[g]
            dla.append(db_end - _tri_dot(tri, de[r]))
        df = jnp.concatenate(dla, axis=0) * (1.0 - _sigmoid(f)) * (1.0 / GATE_TEMP)
        dflow_ref[...] = _dot(df, wfu_v, _NT).astype(dflow_ref.dtype)
        dwfu_ref[...] += _dot(flow, df, _TN)
        dbf_ref[...] += jnp.sum(df, axis=0, keepdims=True)

    rt = lambda s: nt - 1 - s
    return _pcall(
        body, name="gla_bwd", grid=(nt,),
        in_specs=[pl.BlockSpec((T, GLA_DK), lambda s: (rt(s), 0)), pl.BlockSpec((T, GLA_DK), lambda s: (rt(s), 1)),
                  pl.BlockSpec((T, GLA_DV), lambda s: (rt(s), 1)), pl.BlockSpec((T, GLA_DV), lambda s: (rt(s), 2)),
                  pl.BlockSpec((T, FLOW_W), lambda s: (rt(s), (INT_W - FLOW_W) // FLOW_W)),
                  pl.BlockSpec(wfu.shape, lambda s: (0, 0)), pl.BlockSpec(bf.shape, lambda s: (0, 0)),
                  pl.BlockSpec(gn.shape, lambda s: (0, 0)),
                  pl.BlockSpec((G, GLA_DV, GLA_HDK), lambda s: (rt(s), 0, 0)),
                  pl.BlockSpec((None, GLA_DV, GLA_HDK), lambda s: (jnp.maximum(rt(s) * G - 1, 0), 0, 0)),
                  pl.BlockSpec((T, GLA_DV), lambda s: (rt(s), 0)), pl.BlockSpec(memory_space=pl.ANY)],
        out_specs=[pl.BlockSpec((T, 2 * GLA_DK + 2 * GLA_DV), lambda s: (rt(s), 0)),
                   pl.BlockSpec((T, FLOW_W), lambda s: (rt(s), 0)),
                   pl.BlockSpec(wfu.shape, lambda s: (0, 0)), pl.BlockSpec(bf.shape, lambda s: (0, 0)),
                   pl.BlockSpec(gn.shape, lambda s: (0, 0))],
        out_shape=[_sds(dmain.shape, dmain.dtype), _sds((S, FLOW_W), BF16), _sds(wfu.shape), _sds(bf.shape),
                   _sds(gn.shape)],
        scratch=[pltpu.VMEM((GLA_DV, GLA_HDK), F32)], aliases={11: 0},
    )(proj, proj, proj, proj, proj, wfu, bf, gn, states, states, d_ya_in, dmain)


def _pool_counts(tm, i):
    t = (lax.broadcasted_iota(jnp.int32, (tm, POOL_GD), 0) + i * tm + 1).astype(F32)
    return [jnp.minimum(t, float(w)) for w in POOL_WINDOWS]


def _pool_fwd(proj, w_pool, pool_scale, tm=1024):
    S = proj.shape[0]
    tm = min(tm, S // 2)
    col = _POOL_COL
    hb = tm // POOL_HALO

    def body(p_ref, halo_ref, wp_ref, sc_ref, mixed_ref, out_ref):
        i = pl.program_id(0)
        p = p_ref[...].astype(F32)
        halo = halo_ref[...].astype(F32) * (i > 0).astype(F32)
        ext = jnp.concatenate([halo, p], axis=0)
        n = tm + POOL_HALO
        sums, acc, k = {}, ext, 1
        while k < POOL_WINDOWS[-1]:
            acc = acc + pltpu.roll(acc, k, axis=0)
            k *= 2
            sums[k] = acc
        cnts = _pool_counts(tm, i)
        mixed, lin = [], []
        for g, w in enumerate(POOL_WINDOWS):
            ls = slice(g * POOL_GD, (g + 1) * POOL_GD)
            m = sums[w][POOL_HALO:n, ls] / cnts[g] - p[:, ls]
            mixed.append(m)
            lin.append(_dot(m, wp_ref[g]))
        mixed_ref[...] = jnp.concatenate(mixed, axis=-1)
        out_ref[...] = (jnp.concatenate(lin, axis=-1) * sc_ref[...]).astype(out_ref.dtype)

    return _pcall(
        body, name="pool_fwd", grid=(S // tm,),
        in_specs=[pl.BlockSpec((tm, POOL_W), lambda i: (i, col)),
                  pl.BlockSpec((POOL_HALO, POOL_W), lambda i: (jnp.maximum(i * hb - 1, 0), col)),
                  pl.BlockSpec(w_pool.shape, lambda i: (0, 0, 0)), pl.BlockSpec(pool_scale.shape, lambda i: (0, 0))],
        out_specs=[pl.BlockSpec((tm, POOL_W), lambda i: (i, 0)), pl.BlockSpec((tm, POOL_W), lambda i: (i, 0))],
        out_shape=[_sds((S, POOL_W)), _sds((S, POOL_W), BF16)])(proj, proj, w_pool, pool_scale)


def _pool_lin_bwd(dout, mixed, w_pool, pool_scale):
    S = dout.shape[0]

    def fn(dout, mixed, wp, sc):
        dlin = dout * sc
        dm, dwp, lin = [], [], []
        for g in range(len(POOL_WINDOWS)):
            ls = slice(g * POOL_GD, (g + 1) * POOL_GD)
            lin.append(_dot(mixed[:, ls], wp[g]))
            dm.append(_dot(dlin[:, ls], wp[g], _NT))
            dwp.append(_dot(mixed[:, ls], dlin[:, ls], _TN))
        dsc = jnp.sum(dout * jnp.concatenate(lin, axis=-1), axis=0, keepdims=True)
        return jnp.concatenate(dm, axis=-1), jnp.concatenate(dwp, axis=0), dsc

    return _rowwise("pool_lin_bwd", fn, [dout, mixed], [w_pool, pool_scale], [_sds((S, POOL_W))],
                    [_sds((len(POOL_WINDOWS) * POOL_GD, POOL_GD)), _sds((1, POOL_W))], tm=1024)


def _pool_win_bwd(dmixed, dmain, tm=1024):
    S = dmixed.shape[0]
    tm = min(tm, S // 2)
    nt = S // tm
    hb = tm // POOL_HALO

    def body(dm_ref, halo_ref, dmain_in, dp_ref):
        i = pl.program_id(0)
        dm = dm_ref[...]
        halo = halo_ref[...] * (i < nt - 1).astype(F32)
        cnts = _pool_counts(tm, i)
        cnts_h = [c[:POOL_HALO] for c in _pool_counts(tm, i + 1)]
        r = jnp.concatenate([jnp.concatenate([dm[:, g * POOL_GD:(g + 1) * POOL_GD] / cnts[g] for g in range(4)], axis=-1),
                             jnp.concatenate([halo[:, g * POOL_GD:(g + 1) * POOL_GD] / cnts_h[g] for g in range(4)], axis=-1)],
                            axis=0)
        n = tm + POOL_HALO
        sums, acc, k = {}, r, 1
        while k < POOL_WINDOWS[-1]:
            acc = acc + pltpu.roll(acc, n - k, axis=0)
            k *= 2
            sums[k] = acc
        dp = [sums[w][:tm, g * POOL_GD:(g + 1) * POOL_GD] for g, w in enumerate(POOL_WINDOWS)]
        dp_ref[...] = (jnp.concatenate(dp, axis=-1) - dm).astype(dp_ref.dtype)

    return _pcall(
        body, name="pool_win_bwd", grid=(nt,),
        in_specs=[pl.BlockSpec((tm, POOL_W), lambda i: (i, 0)),
                  pl.BlockSpec((POOL_HALO, POOL_W), lambda i: (jnp.minimum((i + 1) * hb, S // POOL_HALO - 1), 0)),
                  pl.BlockSpec(memory_space=pl.ANY)],
        out_specs=pl.BlockSpec((tm, POOL_W), lambda i: (i, _POOL_COL)),
        out_shape=_sds(dmain.shape, dmain.dtype), aliases={2: 0})(dmixed, dmixed, dmain)


def _xattn_probs(q, kv, h):
    hs = slice(h * XA_HD, (h + 1) * XA_HD)
    s = _dot(q[:, hs], kv[:, hs], _NT) * (XA_HD ** -0.5)
    s = s - jnp.max(s, axis=-1, keepdims=True)
    e = jnp.exp(s)
    return e / jnp.sum(e, axis=-1, keepdims=True)


def _xattn_fwd(proj, kv):
    S = proj.shape[0]

    def fn(q, kv):
        outs = []
        for h in range(XA_HEADS):
            p = _xattn_probs(q, kv, h)
            outs.append(_dot(p, kv[:, XA_W + h * XA_HD:XA_W + (h + 1) * XA_HD]))
        return jnp.concatenate(outs, axis=-1)

    return _rowwise("xattn_fwd", fn, [_Win(proj, XA_W, _XQ_COL)], [kv], [_sds((S, XA_W), BF16)])[0]


def _xattn_bwd(proj, kv, dxa, dmain):
    def fn(q, dxa, kv):
        dqs, dks, dvs = [], [], []
        for h in range(XA_HEADS):
            hs = slice(h * XA_HD, (h + 1) * XA_HD)
            vh = kv[:, XA_W + h * XA_HD:XA_W + (h + 1) * XA_HD]
            p = _xattn_probs(q, kv, h)
            dp = _dot(dxa[:, hs], vh, _NT)
            ds = p * (dp - jnp.sum(p * dp, axis=-1, keepdims=True)) * (XA_HD ** -0.5)
            dqs.append(_dot(ds, kv[:, hs]))
            dks.append(_dot(ds, q[:, hs], _TN))
            dvs.append(_dot(p, dxa[:, hs], _TN))
        return jnp.concatenate(dqs, axis=-1), jnp.concatenate(dks + dvs, axis=-1)

    return _rowwise("xattn_bwd", fn, [_Win(proj, XA_W, _XQ_COL), dxa], [kv], [_Win(dmain, XA_W, _XQ_COL)],
                    [_sds(kv.shape)])


def _mix_fwd(x1, h, mem, W, g_next, on_proj=None, late_weights=None):
    S = x1.shape[0]
    M = mem.shape[0]
    tm = min(512, S)
    def proj_body(h_ref, w_ref, o_ref):
        hh = h_ref[...]
        for j in range(INT_NBLK):
            o_ref[:, INT_BLK * j:INT_BLK * (j + 1)] = _dot(hh, w_ref[INT_BLK * j:INT_BLK * (j + 1), :], _NT).astype(BF16)

    proj = _pcall(
        proj_body, name="mix_proj", grid=(S // tm,),
        in_specs=[pl.BlockSpec((tm, D_MODEL), lambda i: (i, 0)),
                  pl.BlockSpec((INT_W, D_MODEL), lambda i: (0, 0), pipeline_mode=pl.Buffered(1))],
        out_specs=pl.BlockSpec((tm, INT_W), lambda i: (i, 0)), out_shape=_sds((S, INT_W), BF16))(h, W["w_int_t"])
    ya_in, states = _gla_fwd(proj, W["w_fu_pad"], W["b_f"], W["gla_norm_g"],
                             on_proj(proj) if on_proj is not None else [])
    mixed, pool_out = _pool_fwd(proj, W["w_pool"], W["pool_scale"])
    mem_n = _rowwise("mem_norm", lambda m, g: _rms(m, g), [mem], [W["mem_norm_g"]], [_sds((M, D_MODEL), BF16)])[0]
    if late_weights is not None:
        W = {**W, **late_weights([pool_out])}
    kv = _mm_nn("mem_kv", mem_n, W["w_mem_kv"])
    xa = _xattn_fwd(proj, kv)
    def out_fn(gates, ya_in, pool_out, xa, x1, wa, wb, wc, wo, g, gn):
        ya, yb, yc = (_dot(t, w).astype(BF16) for t, w in ((ya_in, wa), (pool_out, wb), (xa, wc)))
        merged = _merge(gates, ya, yb, yc).astype(BF16)
        y = _dot(merged, wo)
        x2 = x1 + _rms(y, g)
        return ya, yb, yc, merged, y, x2, _rms(x2, gn)

    ya, yb, yc, merged, y, x2, h_next = _rowwise(
        "mix_out", out_fn, [_Win(proj, 3 * D_MODEL, 1), ya_in, pool_out, xa, x1],
        [W["w_up_gla"], W["w_up_pool"], W["w_up_xattn"], W["w_o"], W["mix_post_g"], g_next],
        [_sds((S, D_MODEL), BF16)] * 5 + [_sds((S, D_MODEL)), _sds((S, D_MODEL), BF16)])
    return x2, h_next, W, (h, proj, states, ya_in, mixed, pool_out, mem_n, kv, xa, ya, yb, yc, merged, y)


def _mix_bwd(dx2, x1, mem, W, saved, on_grads=None):
    h, proj, states, ya_in, mixed, pool_out, mem_n, kv, xa, ya, yb, yc, merged, y = saved
    S = x1.shape[0]

    def out_bwd(y, dx2, gates, ya, yb, yc, g, wo, wa, wb, wc):
        dy, dg = _rms_bwd(y.astype(F32), g, dx2)
        dy = dy.astype(BF16)
        _, vjp = jax.vjp(_merge, gates, ya, yb, yc)
        dgates, dya, dyb, dyc = vjp(_dot(dy, wo, _NT))
        return (dy, dgates, dya, dyb, dyc, _dot(dya, wa, _NT), _dot(dyb, wb, _NT), _dot(dyc, wc, _NT), dg)

    dmain = lax.empty((S, INT_MAIN), BF16)
    dy, dmain, dya, dyb, dyc, d_ya_in, d_pool_out, d_xa, d_mix_post_g = _rowwise(
        "mix_out_bwd", out_bwd, [y, dx2, _Win(proj, 3 * D_MODEL, 1), ya, yb, yc],
        [W["mix_post_g"], W["w_o"], W["w_up_gla"], W["w_up_pool"], W["w_up_xattn"]],
        [_sds((S, D_MODEL), BF16), _Win(dmain, 3 * D_MODEL, 1)] + [_sds((S, D_MODEL), BF16)] * 4
        + [_sds((S, POOL_W), BF16), _sds((S, XA_W), BF16)], [_sds((1, D_MODEL))])
    d_w_o = _mm_tn("d_w_o", merged, dy, ts=S, tn_a=256)
    d_w_up_gla = _mm_tn("d_w_up_gla", ya_in, dya, ts=S, tn_a=256)
    d_w_up_pool = _mm_tn("d_w_up_pool", pool_out, dyb, ts=S, tn_a=128)
    d_w_up_xattn = _mm_tn("d_w_up_xattn", xa, dyc, ts=S, tn_a=128)

    dmain, dflow, d_wfu_pad, d_b_f, d_gla_norm_g = _gla_bwd(proj, W["w_fu_pad"], W["b_f"], W["gla_norm_g"], states,
                                                            d_ya_in, dmain)
    dmixed, d_w_pool, d_pool_scale = _pool_lin_bwd(d_pool_out, mixed, W["w_pool"], W["pool_scale"])
    dmain = _pool_win_bwd(dmixed, dmain)
    dmain, dkv = _xattn_bwd(proj, kv, d_xa, dmain)
    d_w_mem_kv = _mm_tn("d_w_mem_kv", mem_n, dkv)
    dmem_n = _mm_nt("d_mem_n", dkv, W["w_mem_kv"])
    d_mem_norm_g = _rowwise("mem_norm_bwd", lambda m, d, g: _rms_bwd(m, g, d)[1], [mem, dmem_n], [W["mem_norm_g"]], [],
                            [_sds((1, D_MODEL))])[0]
    d_w_int_t = (_mm_tn("d_w_int", dmain, h, ts=S, tn_a=INT_MAIN // 8), _mm_tn("d_w_flow", dflow, h, ts=S))
    grads = dict(
        w_int_t=d_w_int_t, w_fu_pad=d_wfu_pad, b_f=d_b_f, gla_norm_g=d_gla_norm_g, w_pool=d_w_pool,
        pool_scale=d_pool_scale, mem_norm_g=d_mem_norm_g, w_mem_kv=d_w_mem_kv, w_up_gla=d_w_up_gla,
        w_up_pool=d_w_up_pool, w_up_xattn=d_w_up_xattn, w_o=d_w_o, mix_post_g=d_mix_post_g)
    deps = [on_grads(grads)] if on_grads is not None else []
    dx1, grads["mix_pre_g"] = _dh_pre_norm_bwd("d_mix_h", [dmain, dflow], W["w_int_t"], x1, dx2, W["mix_pre_g"], deps)
    return dx1, grads


def _mesh_pos():
    x, y, c = lax.axis_index("x"), lax.axis_index("y"), lax.axis_index("c")
    return x, y, c, 4 * x + 2 * y + c


def _peer(x, y, c, r):
    px = 1 - x if r & 4 else x
    py = 1 - y if r & 2 else y
    pc = 1 - c if r & 1 else c
    return (px, py, pc), 4 * px + 2 * py + pc


_ALL_PEERS = tuple(range(1, N_DEV))
_SIBLING = 1
_SAME_CORE = (2, 4, 6)


def _dev_slot(ref, dev):
    return ref.at[dev]


def _zero_pad_rows(land):
    pad = FF_PAD - FF_BLK

    def body(land_in, o_ref):
        o_ref[...] = jnp.zeros_like(o_ref)

    return _pcall(body, name="zero_pad_rows", grid=(land.shape[0],), in_specs=[pl.BlockSpec(memory_space=pl.ANY)],
                  out_specs=pl.BlockSpec((None, pad, D_MODEL), lambda j: (j, FF_BLK // pad, 0)),
                  out_shape=_sds(land.shape, land.dtype), aliases={0: 0})(land)


class _Plan:
    def __init__(self, scatter, slots=None, shapes=None):
        self.scatter, self.slots, self.shapes = scatter, slots or {}, shapes or {}

    def src(self, srcs, a, dev):
        return self.slots.get(a, _dev_slot)(srcs[a], dev) if self.scatter else srcs[a]

    def dst(self, lands, a, dev):
        return lands[a].at[dev] if self.scatter else self.slots.get(a, _dev_slot)(lands[a], dev)

    def landing_zones(self, arrays):
        lands = []
        for a, arr in enumerate(arrays):
            if self.scatter:
                lands.append(lax.empty((N_DEV,) + tuple(self.shapes.get(a, arr.shape[1:])), arr.dtype))
            elif a in self.shapes:
                lands.append(_zero_pad_rows(lax.empty(self.shapes[a], arr.dtype)))
            else:
                lands.append(lax.empty((N_DEV,) + arr.shape, arr.dtype))
        return lands


_GATHER = _Plan(False)


def _peer_copies(srcs, lands, send_sems, recv_sems, plan, peers=_ALL_PEERS):
    x, y, c, me = _mesh_pos()
    cps = []
    for r in peers:
        pos, peer = _peer(x, y, c, r)
        for a in range(len(srcs)):
            k = a * (N_DEV - 1) + r - 1
            cps.append(pltpu.make_async_remote_copy(
                src_ref=plan.src(srcs, a, peer), dst_ref=plan.dst(lands, a, me),
                send_sem=send_sems.at[k], recv_sem=recv_sems.at[k], device_id=pos,
                device_id_type=pl.DeviceIdType.MESH))
    return cps


_HBM = pl.BlockSpec(memory_space=pltpu.HBM)
_SEM = pl.BlockSpec(memory_space=pltpu.SEMAPHORE)
_EFFECT = pltpu.SideEffectType.DATAFLOW_SIDE_EFFECTING


def _own_copies(srcs, lands, own_sems, plan):
    me = _mesh_pos()[3]
    return [pltpu.make_async_copy(plan.src(srcs, a, me), plan.dst(lands, a, me), own_sems.at[a])
            for a in range(len(srcs))]


def _exchange_start(name, arrays, plan):
    n = len(arrays)
    lands = plan.landing_zones(arrays)
    n_sem = n * (N_DEV - 1)

    def body(*refs):
        srcs, lands_ = refs[:n], refs[n:2 * n]
        send_sems, recv_sems, own_sems = refs[2 * n:2 * n + 3]
        token = refs[-1]
        for cp in _peer_copies(srcs, lands_, send_sems, recv_sems, plan) + _own_copies(srcs, lands_, own_sems, plan):
            cp.start()
        token[...] = jnp.zeros_like(token)

    hbm = lambda a: pltpu.HBM(a.shape, a.dtype)
    res = pl.pallas_call(
        body, name=name,
        out_shape=(pltpu.SemaphoreType.DMA((n_sem,)), pltpu.SemaphoreType.DMA((n_sem,)), pltpu.SemaphoreType.DMA((n,)),
                   *[hbm(a) for a in arrays], *[hbm(a) for a in lands], _sds((8, 128))),
        in_specs=[_HBM] * (2 * n),
        out_specs=(_SEM, _SEM, _SEM, *[_HBM] * (2 * n), pl.BlockSpec(memory_space=pltpu.VMEM)),
        input_output_aliases={i: 3 + i for i in range(2 * n)},
        compiler_params=pltpu.CompilerParams(has_side_effects=_EFFECT),
    )(*[pltpu.with_memory_space_constraint(a, pltpu.HBM) for a in list(arrays) + lands])
    return (res[:3], res[3:3 + n], res[3 + n:3 + 2 * n], plan), res[-1]


def _exchange_wait(name, state, after):
    sems, srcs, lands, plan = state
    n = len(srcs)

    def body(*refs):
        srcs_, lands_ = refs[:n], refs[n:2 * n]
        send_sems, recv_sems, own_sems = refs[2 * n:2 * n + 3]
        for cp in _peer_copies(srcs_, lands_, send_sems, recv_sems, plan):
            cp.wait_send()
            cp.wait_recv()
        for cp in _own_copies(srcs_, lands_, own_sems, plan):
            cp.wait()

    hbm = lambda a: pltpu.HBM(a.shape, a.dtype)
    res = pl.pallas_call(
        body, name=name, out_shape=tuple(hbm(a) for a in list(srcs) + list(lands)),
        in_specs=[_HBM] * (2 * n) + [_SEM] * 3 + [pl.BlockSpec(memory_space=pl.ANY)] * len(after),
        out_specs=tuple([_HBM] * (2 * n)), input_output_aliases={i: i for i in range(2 * n)},
        compiler_params=pltpu.CompilerParams(has_side_effects=_EFFECT),
    )(*srcs, *lands, *sems, *after)
    return res[n:]


def _gather_start(name, arrays, plan, after):
    n = len(arrays)
    lands = plan.landing_zones(arrays)
    n_sem = n * (N_DEV - 1)

    def body(*refs):
        srcs, lands_ = refs[:n], refs[n:2 * n]
        send_sems, recv_sems, own_sems = refs[2 * n + len(after):2 * n + len(after) + 3]
        token = refs[-1]
        for cp in (_peer_copies(srcs, lands_, send_sems, recv_sems, plan, (_SIBLING,) + _SAME_CORE)
                   + _own_copies(srcs, lands_, own_sems, plan)):
            cp.start()
        token[...] = jnp.zeros_like(token)

    hbm = lambda a: pltpu.HBM(a.shape, a.dtype)
    res = pl.pallas_call(
        body, name=name,
        out_shape=(pltpu.SemaphoreType.DMA((n_sem,)), pltpu.SemaphoreType.DMA((n_sem,)), pltpu.SemaphoreType.DMA((n,)),
                   *[hbm(a) for a in arrays], *[hbm(a) for a in lands], _sds((8, 128))),
        in_specs=[_HBM] * (2 * n) + [pl.BlockSpec(memory_space=pl.ANY)] * len(after),
        out_specs=(_SEM, _SEM, _SEM, *[_HBM] * (2 * n), pl.BlockSpec(memory_space=pltpu.VMEM)),
        input_output_aliases={i: 3 + i for i in range(2 * n)},
        compiler_params=pltpu.CompilerParams(has_side_effects=_EFFECT),
    )(*[pltpu.with_memory_space_constraint(a, pltpu.HBM) for a in list(arrays) + lands], *after)
    return (res[:3], res[3:3 + n], res[3 + n:3 + 2 * n], plan), res[-1]


def _pass_on_copies(lands, send_sems, recv_sems, plan):
    x, y, c, _ = _mesh_pos()
    sibling = _peer(x, y, c, _SIBLING)[0]
    cps = []
    for i, r in enumerate(_SAME_CORE):
        owner = _peer(x, y, c, r)[1]
        for a in range(len(lands)):
            k = a * len(_SAME_CORE) + i
            cps.append(pltpu.make_async_remote_copy(
                src_ref=plan.dst(lands, a, owner), dst_ref=plan.dst(lands, a, owner), send_sem=send_sems.at[k],
                recv_sem=recv_sems.at[k], device_id=sibling, device_id_type=pl.DeviceIdType.MESH))
    return cps


def _gather_pass_on(name, state, after):
    sems, srcs, lands, plan = state
    n = len(srcs)
    n_sem = n * len(_SAME_CORE)

    def body(*refs):
        srcs_, lands_ = refs[:n], refs[n:2 * n]
        send_sems, recv_sems = refs[2 * n], refs[2 * n + 1]
        on_send, on_recv = refs[2 * n + 3 + len(after)], refs[2 * n + 4 + len(after)]
        token = refs[-1]
        arrivals = _peer_copies(srcs_, lands_, send_sems, recv_sems, plan, _SAME_CORE)
        for arrived, on in zip(arrivals, _pass_on_copies(lands_, on_send, on_recv, plan)):
            arrived.wait_recv()
            on.start()
        token[...] = jnp.zeros_like(token)

    hbm = lambda a: pltpu.HBM(a.shape, a.dtype)
    res = pl.pallas_call(
        body, name=name,
        out_shape=(pltpu.SemaphoreType.DMA((n_sem,)), pltpu.SemaphoreType.DMA((n_sem,)),
                   *[hbm(a) for a in list(srcs) + list(lands)], _sds((8, 128))),
        in_specs=[_HBM] * (2 * n) + [_SEM] * 3 + [pl.BlockSpec(memory_space=pl.ANY)] * len(after),
        out_specs=(_SEM, _SEM, *[_HBM] * (2 * n), pl.BlockSpec(memory_space=pltpu.VMEM)),
        input_output_aliases={i: 2 + i for i in range(2 * n)},
        compiler_params=pltpu.CompilerParams(has_side_effects=_EFFECT),
    )(*srcs, *lands, *sems, *after)
    return (sems, res[:2], res[2:2 + n], res[2 + n:2 + 2 * n], plan), res[-1]


def _gather_wait(name, state, after):
    sems, on_sems, srcs, lands, plan = state
    n = len(srcs)

    def body(*refs):
        srcs_, lands_ = refs[:n], refs[n:2 * n]
        send_sems, recv_sems, own_sems, on_send, on_recv = refs[2 * n:2 * n + 5]
        for cp in _peer_copies(srcs_, lands_, send_sems, recv_sems, plan, (_SIBLING,)):
            cp.wait_recv()
        for cp in _peer_copies(srcs_, lands_, send_sems, recv_sems, plan, (_SIBLING,) + _SAME_CORE):
            cp.wait_send()
        for cp in _own_copies(srcs_, lands_, own_sems, plan):
            cp.wait()
        for cp in _pass_on_copies(lands_, on_send, on_recv, plan):
            cp.wait_send()
            cp.wait_recv()

    hbm = lambda a: pltpu.HBM(a.shape, a.dtype)
    res = pl.pallas_call(
        body, name=name, out_shape=tuple(hbm(a) for a in list(srcs) + list(lands)),
        in_specs=[_HBM] * (2 * n) + [_SEM] * 5 + [pl.BlockSpec(memory_space=pl.ANY)] * len(after),
        out_specs=tuple([_HBM] * (2 * n)), input_output_aliases={i: i for i in range(2 * n)},
        compiler_params=pltpu.CompilerParams(has_side_effects=_EFFECT),
    )(*srcs, *lands, *sems, *on_sems, *after)
    return res[n:]


def _sum_parts(name, recv):
    def body(r_ref, o_ref):
        s = r_ref[0]
        for j in range(1, N_DEV):
            s = s + r_ref[j]
        o_ref[...] = s

    return pl.pallas_call(body, name=name, out_shape=_sds(recv.shape[1:], recv.dtype))(recv)


def _adam(name, recv, w, m, v, col=0):
    shape = w.shape
    R, C = shape
    tr = R
    while N_DEV * tr * C * 4 > 6 * 1024 * 1024 and tr % 32 == 0:
        tr //= 2

    def body(recv_ref, w_ref, m_ref, v_ref, g_ref, d_ref, m2_ref, v2_ref):
        g = recv_ref[0].astype(F32)
        for j in range(1, N_DEV):
            g = g + recv_ref[j].astype(F32)
        w_, m_, v_ = w_ref[...], m_ref[...], v_ref[...]
        m2 = ADAM_B1 * m_ + (1.0 - ADAM_B1) * g
        v2 = ADAM_B2 * v_ + (1.0 - ADAM_B2) * (g * g)
        m_hat = m2 / (1.0 - ADAM_B1 ** ADAM_STEP)
        v_hat = v2 / (1.0 - ADAM_B2 ** ADAM_STEP)
        g_ref[...] = g
        d_ref[...] = -ADAM_LR * (m_hat / (jnp.sqrt(v_hat) + ADAM_EPS) + ADAM_WD * w_)
        m2_ref[...] = m2
        v2_ref[...] = v2

    blk = pl.BlockSpec((tr, C), lambda i: (i, 0))
    return _pcall(body, name=name, grid=(R // tr,),
                  in_specs=[pl.BlockSpec((N_DEV, tr, C), lambda i: (0, i, col)), blk, blk, blk],
                  out_specs=[blk] * 4, out_shape=[_sds(shape)] * 4)(recv, w, m, v)


_NAMES = ['ffn1_pre_g', 'ffn1_w_in', 'ffn1_w_out', 'ffn1_post_g', 'mix_pre_g', 'w_in', 'w_fu', 'b_f', 'gla_norm_g',
          'w_pool', 'pool_scale', 'mem_norm_g', 'w_mem_kv', 'w_up_gla', 'w_up_pool', 'w_up_xattn', 'w_o', 'mix_post_g',
          'ffn2_pre_g', 'ffn2_w_in', 'ffn2_w_out', 'ffn2_post_g', 'final_g']
_SHARDED = ['ffn1_w_in', 'ffn1_w_out', 'w_in', 'w_fu', 'w_mem_kv', 'w_up_gla', 'w_up_pool', 'w_up_xattn', 'w_o',
            'ffn2_w_in', 'ffn2_w_out']
_COL_SHARDED = ['w_up_pool', 'w_up_xattn']


def _cols_to_full(g):
    return jnp.transpose(g, (1, 0, 2)).reshape(g.shape[1], N_DEV * g.shape[2])


def _full_to_cols(f):
    R, C = f.shape
    return jnp.transpose(f.reshape(R, N_DEV, C // N_DEV), (1, 0, 2))


def _to_internal(w_in_t):
    o = 0
    parts = []
    for s in IN_SPLITS:
        parts.append(w_in_t[o:o + s])
        o += s
    q, k, v, g_out, f_low, p_in, xq, gates = parts
    f_low = jnp.pad(f_low, ((0, FLOW_W - GATE_RANK), (0, 0)))
    return jnp.concatenate([q, k, v, g_out, gates, p_in, xq, f_low], axis=0)


def _from_internal(d, d_flow):
    q, k, v, g_out = d[0:512], d[512:1024], d[1024:2048], d[2048:3072]
    gates, p_in, xq = d[3072:6144], d[6144:6656], d[6656:7168]
    return jnp.concatenate([q, k, v, g_out, d_flow[:GATE_RANK], p_in, xq, gates], axis=0)


_FFN_IN = ('ffn1_w_in', 'ffn2_w_in')
_FFN_OUT = ('ffn1_w_out', 'ffn2_w_out')
_GATHERS = {"ffn1_in": ['ffn1_w_in'], "ffn1_out": ['ffn1_w_out'],
            "mix_in": ['w_in', 'w_fu'], "mix_rest": ['w_mem_kv', 'w_up_gla', 'w_up_pool', 'w_up_xattn', 'w_o'],
            "ffn2": ['ffn2_w_in', 'ffn2_w_out']}
_MIX = _GATHERS["mix_in"] + _GATHERS["mix_rest"]


def _ffn_in_slot(ref, d):
    return ref.at[2 * (d % N_FF_BLK) + d // N_FF_BLK, pl.ds(0, FF_BLK)]


def _ffn_out_slot(ref, d):
    rows = FF_BLK // 2
    return ref.at[d // 2, pl.ds(pl.multiple_of((d % 2) * rows, rows), rows)]


def _ffn_plan(names, scatter):
    slots, shapes = {}, {}
    for a, n in enumerate(names):
        if n in _FFN_IN:
            slots[a] = _ffn_in_slot
            shapes[a] = (FF_BLK, D_MODEL) if scatter else (N_DEV, FF_PAD, D_MODEL)
        elif n in _FFN_OUT:
            slots[a] = _ffn_out_slot
            shapes[a] = (FF_BLK // 2, D_MODEL) if scatter else (N_FF_BLK, FF_PAD, D_MODEL)
    return _Plan(scatter, slots, shapes)


def _rows_to_full(g):
    return g.reshape(N_DEV * g.shape[1], g.shape[2])


def _mix_weights(gathered):
    W = {}
    for n, g in gathered.items():
        if n == "w_in":
            W["w_int_t"] = _to_internal(g.reshape(IN_WIDTH, D_MODEL))
        elif n == "w_fu":
            W["w_fu_pad"] = jnp.pad(_cols_to_full(g), ((0, FLOW_W - GATE_RANK), (0, 0)))
        else:
            W[n] = _cols_to_full(g) if n in _COL_SHARDED else _rows_to_full(g)
    return W


def _mix_chunks(G, n):
    if n == "w_in":
        return _from_internal(*G["w_int_t"]).reshape(N_DEV, IN_SHARD, D_MODEL)
    if n == "w_fu":
        return _full_to_cols(G["w_fu_pad"][:GATE_RANK].astype(BF16))
    if n in _COL_SHARDED:
        return _full_to_cols(G[n])
    return G[n].reshape(N_DEV, G[n].shape[0] // N_DEV, G[n].shape[1])


def _step(x, mem, tgt, P, Mo, Vo):
    def native(n, a):
        return jnp.swapaxes(a, 0, 1) if n in _FFN_IN + ("w_in",) else a

    P, Mo, Vo = ({n: native(n, a) for n, a in d.items()} for d in (P, Mo, Vo))
    small = {n: P[n] for n in _NAMES if n not in _SHARDED}

    gather, tokens = {}, []
    for grp, names in _GATHERS.items():
        gather[grp], tok = _gather_start("gather_" + grp, [P[n].astype(BF16) for n in names], _ffn_plan(names, False),
                                         tokens[-1:])
        tokens.append(tok)

    def pass_on(grp, after):
        return _gather_pass_on("gather_" + grp + "_on", gather[grp], after)

    def gathered(grp, after):
        return _gather_wait("gather_" + grp + "_wait", pass_on(grp, after)[0], [])

    def ffn_wo(lands):
        return lambda act: lands[0].reshape(N_FF_BLK * FF_PAD, D_MODEL)

    h1 = _pre_norm("ffn1_pre", x, small["ffn1_pre_g"], tokens[:1])
    w1t = gathered("ffn1_in", tokens[-1:] + [h1])[0].reshape(N_DEV * FF_PAD, D_MODEL)
    (x1, hm), w1o, sv1 = _ffn_fwd("ffn1", x, h1, w1t, small["ffn1_post_g"],
                                  lambda act: ffn_wo(gathered("ffn1_out", [act]))(act), small["mix_pre_g"])

    Wm = {**small, **_mix_weights(dict(zip(_GATHERS["mix_in"], gathered("mix_in", [x1]))))}
    passed = {}

    def pass_on_later(proj):
        tokens = []
        for grp in ("mix_rest", "ffn2"):
            passed[grp], tok = pass_on(grp, [proj])
            tokens.append(tok)
        return tokens

    def mix_rest(after):
        names = _GATHERS["mix_rest"]
        return _mix_weights(dict(zip(names, _gather_wait("gather_mix_rest_wait", passed["mix_rest"], after))))

    x2, h2, Wm, svm = _mix_fwd(x1, hm, mem, Wm, small["ffn2_pre_g"], pass_on_later, mix_rest)
    w2t, w2o = _gather_wait("gather_ffn2_wait", passed["ffn2"], [x2])
    w2t = w2t.reshape(N_DEV * FF_PAD, D_MODEL)
    (loss, dx3, d_final_g), w2o, sv2 = _ffn_fwd("ffn2", x2, h2, w2t, small["ffn2_post_g"], ffn_wo([w2o]),
                                                small["final_g"], tgt)

    G = dict(final_g=d_final_g)
    scat = {}

    def start(grp, names, arrays):
        scat[grp] = names, _exchange_start("scatter_" + grp, arrays, _ffn_plan(names, True))
        return scat[grp][1][1]

    def ffn_starts(tag):
        return (lambda dwo: start(tag + "_out", [tag + "_w_out"], [dwo.reshape(N_FF_BLK, FF_PAD, D_MODEL)]),
                lambda dwt: start(tag + "_in", [tag + "_w_in"], [dwt.reshape(N_DEV, FF_PAD, D_MODEL)]))

    dx2, G["ffn2_pre_g"], G["ffn2_post_g"] = _ffn_bwd(
        "ffn2", dx3, x2, small["ffn2_pre_g"], w2t, w2o, small["ffn2_post_g"], sv2, *ffn_starts("ffn2"))
    dx1, Gm = _mix_bwd(dx2, x1, mem, Wm, svm,
                       lambda Gm: start("mix", _MIX, [_mix_chunks(Gm, n) for n in _MIX]))
    G.update(Gm)
    dx, G["ffn1_pre_g"], G["ffn1_post_g"] = _ffn_bwd(
        "ffn1", dx1, x, small["ffn1_pre_g"], w1t, w1o, small["ffn1_post_g"], sv1, *ffn_starts("ffn1"))

    vecs = sorted((n for n in small if P[n].ndim == 2), key=lambda n: -P[n].shape[1])
    packed = jnp.concatenate([G[n] for n in vecs] + [loss], axis=1)
    place, o = {}, 0
    for n in vecs:
        place[n] = o // P[n].shape[1]
        o += P[n].shape[1]
    scat["small"] = ["packed", "w_pool"], _exchange_start("gather_small_grads", [packed, G["w_pool"]], _GATHER)
    recv, outs = {}, {}
    done = [dx]
    for grp in ["ffn2_out", "ffn2_in", "mix", "ffn1_out", "ffn1_in", "small"]:
        names, (state, _) = scat[grp]
        recv.update(zip(names, _exchange_wait(("gather_" if grp == "small" else "scatter_") + grp + "_wait", state,
                                              done)))
        if grp == "small":
            loss = _sum_parts("loss_sum", recv["packed"][:, :, o:])[0, 0]
            names = list(small)
        for n in names:
            shp = P[n].shape
            shp2 = shp if len(shp) == 2 else (shp[0] * shp[1], shp[2])
            r = recv["packed"] if n in place else recv[n].reshape((N_DEV,) + shp2)
            res = _adam("adam_" + n, r, P[n].reshape(shp2), Mo[n].reshape(shp2), Vo[n].reshape(shp2), place.get(n, 0))
            outs[n] = [native(n, t.reshape(shp)) for t in res]
            done.append(res[-1])
    return loss, dx, outs


def kernel(x, mem, ffn1_pre_g, ffn1_w_in, ffn1_w_out, ffn1_post_g, mix_pre_g, w_in, w_fu, b_f, gla_norm_g, w_pool, pool_scale, mem_norm_g, w_mem_kv, w_up_gla, w_up_pool, w_up_xattn, w_o, mix_post_g, ffn2_pre_g, ffn2_w_in, ffn2_w_out, ffn2_post_g, final_g, loss_target, m_ffn1_pre_g, m_ffn1_w_in, m_ffn1_w_out, m_ffn1_post_g, m_mix_pre_g, m_w_in, m_w_fu, m_b_f, m_gla_norm_g, m_w_pool, m_pool_scale, m_mem_norm_g, m_w_mem_kv, m_w_up_gla, m_w_up_pool, m_w_up_xattn, m_w_o, m_mix_post_g, m_ffn2_pre_g, m_ffn2_w_in, m_ffn2_w_out, m_ffn2_post_g, m_final_g, v_ffn1_pre_g, v_ffn1_w_in, v_ffn1_w_out, v_ffn1_post_g, v_mix_pre_g, v_w_in, v_w_fu, v_b_f, v_gla_norm_g, v_w_pool, v_pool_scale, v_mem_norm_g, v_w_mem_kv, v_w_up_gla, v_w_up_pool, v_w_up_xattn, v_w_o, v_mix_post_g, v_ffn2_pre_g, v_ffn2_w_in, v_ffn2_w_out, v_ffn2_post_g, v_final_g):
    params = [ffn1_pre_g, ffn1_w_in, ffn1_w_out, ffn1_post_g, mix_pre_g, w_in, w_fu, b_f, gla_norm_g, w_pool, pool_scale, mem_norm_g, w_mem_kv, w_up_gla, w_up_pool, w_up_xattn, w_o, mix_post_g, ffn2_pre_g, ffn2_w_in, ffn2_w_out, ffn2_post_g, final_g]
    moms = [m_ffn1_pre_g, m_ffn1_w_in, m_ffn1_w_out, m_ffn1_post_g, m_mix_pre_g, m_w_in, m_w_fu, m_b_f, m_gla_norm_g, m_w_pool, m_pool_scale, m_mem_norm_g, m_w_mem_kv, m_w_up_gla, m_w_up_pool, m_w_up_xattn, m_w_o, m_mix_post_g, m_ffn2_pre_g, m_ffn2_w_in, m_ffn2_w_out, m_ffn2_post_g, m_final_g]
    vars_ = [v_ffn1_pre_g, v_ffn1_w_in, v_ffn1_w_out, v_ffn1_post_g, v_mix_pre_g, v_w_in, v_w_fu, v_b_f, v_gla_norm_g, v_w_pool, v_pool_scale, v_mem_norm_g, v_w_mem_kv, v_w_up_gla, v_w_up_pool, v_w_up_xattn, v_w_o, v_mix_post_g, v_ffn2_pre_g, v_ffn2_w_in, v_ffn2_w_out, v_ffn2_post_g, v_final_g]
    P = {n: a[0] if a.ndim > 2 else a for n, a in zip(_NAMES, params)}
    Mo = {n: a[0] if a.ndim > 2 else a for n, a in zip(_NAMES, moms)}
    Vo = {n: a[0] if a.ndim > 2 else a for n, a in zip(_NAMES, vars_)}
    loss, dx, outs = _step(x[0], mem[0], loss_target[0], P, Mo, Vo)
    out = [loss, dx[None]]
    for kind in range(4):
        for n, p in zip(_NAMES, params):
            out.append(outs[n][kind].reshape(p.shape))
    return tuple(out)
```

```python
import functools

import jax
import jax.numpy as jnp
from jax import lax
from jax.experimental import pallas as pl
from jax.experimental.pallas import tpu as pltpu

F32 = jnp.float32
BF16 = jnp.bfloat16

N_DEV = 8
D_MODEL = 1024
D_FF = 2816
FF_BLK = 2 * D_FF // N_DEV
N_FF_BLK = D_FF // FF_BLK
FF_PAD = 768
CHUNK = 64
GLA_HEADS = 4
GLA_DK = 512
GLA_DV = 1024
GLA_HDK = 128
GLA_HDV = 256
GATE_RANK = 16
GATE_TEMP = 16.0
POOL_WINDOWS = (2, 4, 8, 16)
POOL_W = 512
POOL_GD = 128
POOL_HALO = 16
XA_HEADS = 4
XA_HD = 128
XA_W = 512
EPS = 1e-6
IN_SPLITS = (GLA_DK, GLA_DK, GLA_DV, GLA_DV, GATE_RANK, POOL_W, XA_W, 3 * D_MODEL)
IN_WIDTH = sum(IN_SPLITS)
IN_SHARD = IN_WIDTH // N_DEV
INT_W = 3072 + 3072 + 1024 + 128
INT_NBLK = 3
INT_BLK = INT_W // INT_NBLK
FLOW_W = 128
INT_MAIN = INT_W - FLOW_W
_POOL_COL = (3072 + 3072) // POOL_W
_XQ_COL = (3072 + 3072 + POOL_W) // XA_W

ADAM_LR = 0.001
ADAM_B1 = 0.9
ADAM_B2 = 0.999
ADAM_EPS = 1e-08
ADAM_WD = 0.01
ADAM_STEP = 10

VMEM_LIMIT = 56 * 1024 * 1024

_NN = (((1,), (0,)), ((), ()))
_NT = (((1,), (1,)), ((), ()))
_TN = (((0,), (0,)), ((), ()))


def _pcall(body, *, name, grid, in_specs, out_specs, out_shape, scratch=(), aliases=None):
    return pl.pallas_call(
        body, name=name, grid=grid, in_specs=in_specs, out_specs=out_specs, out_shape=out_shape,
        scratch_shapes=list(scratch), input_output_aliases=aliases or {},
        compiler_params=pltpu.CompilerParams(dimension_semantics=("arbitrary",) * len(grid),
                                             vmem_limit_bytes=VMEM_LIMIT))


def _dot(a, b, dims=_NN):
    return lax.dot_general(a.astype(BF16), b.astype(BF16), dims, preferred_element_type=F32)


def _mm(name, a, b, *, grid, a_blk, a_map, b_blk, b_map, o_shape, o_blk, o_map, dims, out_dtype=F32, deps=()):
    nk = grid[2]

    def body(a_ref, b_ref, *rest):
        o_ref, scr = rest[len(deps)], rest[len(deps) + 1:]
        p = _dot(a_ref[...], b_ref[...], dims)
        if nk == 1:
            o_ref[...] = p.astype(o_ref.dtype)
        else:
            acc = scr[0]
            k = pl.program_id(2)

            @pl.when(k == 0)
            def _():
                acc[...] = p

            @pl.when(k > 0)
            def _():
                acc[...] += p

            @pl.when(k == nk - 1)
            def _():
                o_ref[...] = acc[...].astype(o_ref.dtype)

    acc_shape = tuple(d for d in o_blk if d is not None)
    return _pcall(body, name=name, grid=grid,
                  in_specs=[pl.BlockSpec(a_blk, a_map), pl.BlockSpec(b_blk, b_map)]
                  + [pl.BlockSpec(memory_space=pl.ANY)] * len(deps),
                  out_specs=pl.BlockSpec(o_blk, o_map),
                  out_shape=jax.ShapeDtypeStruct(o_shape, out_dtype),
                  scratch=[pltpu.VMEM(acc_shape, F32)] if nk > 1 else [])(a, b, *deps)


def _mm_nn(name, a, b, out_dtype=F32, tm=1024, tn=None):
    M, K = a.shape
    N = b.shape[1]
    tm, tn = min(tm, M), (tn or N)
    return _mm(name, a, b, grid=(N // tn, M // tm, 1), a_blk=(tm, K), a_map=lambda j, i, k: (i, 0),
               b_blk=(K, tn), b_map=lambda j, i, k: (0, j), o_shape=(M, N), o_blk=(tm, tn),
               o_map=lambda j, i, k: (i, j), dims=_NN, out_dtype=out_dtype)


def _mm_nt(name, a, b, out_dtype=F32, tm=1024):
    M, K = a.shape
    N = b.shape[0]
    tm = min(tm, M)
    return _mm(name, a, b, grid=(1, M // tm, 1), a_blk=(tm, K), a_map=lambda j, i, k: (i, 0),
               b_blk=(N, K), b_map=lambda j, i, k: (0, 0), o_shape=(M, N), o_blk=(tm, N),
               o_map=lambda j, i, k: (i, 0), dims=_NT, out_dtype=out_dtype)


def _mm_tn(name, a, b, out_dtype=BF16, ts=512, tn=None, tn_a=None, deps=()):
    S, M = a.shape
    N = b.shape[1]
    ts, tn, tn_a = min(ts, S), (tn or N), (tn_a or M)
    return _mm(name, a, b, grid=((N // tn) * (M // tn_a), 1, S // ts), a_blk=(ts, tn_a),
               a_map=lambda j, i, k: (k, j if tn_a < M else 0), b_blk=(ts, tn),
               b_map=lambda j, i, k: (k, j if tn < N else 0), o_shape=(M, N), o_blk=(tn_a, tn),
               o_map=lambda j, i, k: (j, 0) if tn_a < M else (0, j), dims=_TN, out_dtype=out_dtype, deps=deps)


class _Win:
    def __init__(self, arr, w, c):
        self.arr, self.w, self.c = arr, w, c


def _row_spec(x, tm):
    if isinstance(x, _Win):
        return x.arr, pl.BlockSpec((tm, x.w), functools.partial(lambda i, c: (i, c), c=x.c))
    if x.ndim == 3:
        return x, pl.BlockSpec((x.shape[0], tm, x.shape[2]), lambda i: (0, i, 0))
    return x, pl.BlockSpec((tm, x.shape[1]), lambda i: (i, 0))


def _rowwise(name, fn, rows, consts, outs, accs=(), tm=512):
    first = rows[0].arr if isinstance(rows[0], _Win) else rows[0]
    S = first.shape[1] if first.ndim == 3 else first.shape[0]
    tm = min(tm, S)
    n_in, n_out = len(rows) + len(consts), len(outs)
    arrays, in_specs = [], []
    for r in rows:
        arr, spec = _row_spec(r, tm)
        arrays.append(arr)
        in_specs.append(spec)
    for c in consts:
        arrays.append(c)
        in_specs.append(pl.BlockSpec(c.shape, functools.partial(lambda i, n: (0,) * n, n=c.ndim)))
    aliases = {}
    for k, o in enumerate(outs):
        if isinstance(o, _Win):
            aliases[len(arrays)] = k
            arrays.append(o.arr)
            in_specs.append(pl.BlockSpec(memory_space=pl.ANY))
    n_thru = len(aliases)
    out_specs = [_row_spec(o, tm)[1] for o in outs]
    out_specs += [pl.BlockSpec(a.shape, functools.partial(lambda i, n: (0,) * n, n=len(a.shape))) for a in accs]
    out_shape = [_sds(o.arr.shape, o.arr.dtype) if isinstance(o, _Win) else _sds(o.shape, o.dtype) for o in outs]

    def body(*refs):
        res = fn(*[r[...] for r in refs[:n_in]])
        if not isinstance(res, (tuple, list)):
            res = (res,)
        orefs = refs[n_in + n_thru:]
        for r, v in zip(orefs[:n_out], res[:n_out]):
            r[...] = v.astype(r.dtype)
        i = pl.program_id(0)
        for r, v in zip(orefs[n_out:], res[n_out:]):
            @pl.when(i == 0)
            def _(r=r, v=v):
                r[...] = v.astype(r.dtype)

            @pl.when(i > 0)
            def _(r=r, v=v):
                r[...] += v.astype(r.dtype)

    return _pcall(body, name=name, grid=(S // tm,), in_specs=in_specs, out_specs=out_specs,
                  out_shape=out_shape + [_sds(a.shape, a.dtype) for a in accs], aliases=aliases)(*arrays)


def _sds(shape, dtype=F32):
    return jax.ShapeDtypeStruct(shape, dtype)


def _rms(x, g):
    return x * lax.rsqrt(jnp.mean(x * x, axis=-1, keepdims=True) + EPS) * g


def _rms_bwd(x, g, dy):
    _, vjp = jax.vjp(_rms, x, g)
    return vjp(dy)


def _sigmoid(x):
    return 0.5 * jnp.tanh(0.5 * x) + 0.5


def _silu(x):
    return x * _sigmoid(x)


def _log_sigmoid(f):
    return jnp.minimum(f, 0.0) - jnp.log(1.0 + jnp.exp(-jnp.abs(f)))


def _head_rms_gate(o, g_out, gn):
    parts = [_rms(o[:, h * GLA_HDV:(h + 1) * GLA_HDV], gn[:, h * GLA_HDV:(h + 1) * GLA_HDV]) for h in range(GLA_HEADS)]
    return jnp.concatenate(parts, axis=-1) * _silu(g_out.astype(F32))


def _merge(gates, ya, yb, yc):
    gates, ya, yb, yc = (t.astype(F32) for t in (gates, ya, yb, yc))
    return (_sigmoid(gates[:, :D_MODEL]) * ya + _sigmoid(gates[:, D_MODEL:2 * D_MODEL]) * yb
            + _sigmoid(gates[:, 2 * D_MODEL:]) * yc)


def _tri_dot(t, x):
    hi = x.astype(BF16)
    r1 = x - hi.astype(F32)
    mid = r1.astype(BF16)
    lo = (r1 - mid.astype(F32)).astype(BF16)
    d = functools.partial(lax.dot_general, dimension_numbers=_NN, preferred_element_type=F32)
    return d(t, hi) + d(t, mid) + d(t, lo)


def _pre_norm(name, x, g, after):
    S = x.shape[0]
    tm = min(1024, S)

    def body(x_ref, g_ref, *rest):
        rest[-1][...] = _rms(x_ref[...], g_ref[...]).astype(BF16)

    row = pl.BlockSpec((tm, D_MODEL), lambda i: (i, 0))
    return _pcall(body, name=name, grid=(S // tm,),
                  in_specs=[row, pl.BlockSpec((1, D_MODEL), lambda i: (0, 0))] + [pl.BlockSpec(memory_space=pl.ANY)] * len(after),
                  out_specs=row, out_shape=_sds(x.shape, BF16))(x, g, *after)


def _ffn_fwd(tag, x, h, wt, gpost, get_wo, g_next, tgt=None):
    S = x.shape[0]
    tm = min(512, S)

    def up_body(h_ref, w_ref, u_ref, act_ref):
        hh = h_ref[...]
        for j in range(N_FF_BLK):
            ab = _dot(hh, w_ref[2 * FF_PAD * j:2 * FF_PAD * (j + 1), :], _NT)
            u_ref[:, 2 * FF_PAD * j:2 * FF_PAD * (j + 1)] = ab.astype(BF16)
            act_ref[:, FF_PAD * j:FF_PAD * (j + 1)] = (_silu(ab[:, :FF_PAD]) * ab[:, FF_PAD:]).astype(BF16)

    def down_head_body(act_ref, w_ref, x_ref, g_ref, gn_ref, t_ref, f_ref, dxo_ref, loss_ref, dgn_ref):
        i = pl.program_id(0)
        f = _dot(act_ref[...], w_ref[...])
        f_ref[...] = f.astype(BF16)
        xo = x_ref[...] + 0.5 * _rms(f, g_ref[...])
        out, vjp = jax.vjp(_rms, xo, gn_ref[...])
        e = out - t_ref[...]
        loss = 0.5 * jnp.sum(jnp.mean(e * e, axis=-1, keepdims=True), axis=0, keepdims=True)
        dxo_ref[...], dgn = vjp(e * (1.0 / D_MODEL))

        @pl.when(i == 0)
        def _():
            loss_ref[...] = jnp.broadcast_to(loss, loss_ref.shape)
            dgn_ref[...] = dgn

        @pl.when(i > 0)
        def _():
            loss_ref[...] += jnp.broadcast_to(loss, loss_ref.shape)
            dgn_ref[...] += dgn

    if tgt is not None:
        def last_body(h_ref, wt_ref, wo_ref, x_ref, g_ref, gn_ref, t_ref, u_ref, act_ref, f_ref, dxo_ref, loss_ref,
                      dgn_ref):
            up_body(h_ref, wt_ref, u_ref, act_ref)
            down_head_body(act_ref, wo_ref, x_ref, g_ref, gn_ref, t_ref, f_ref, dxo_ref, loss_ref, dgn_ref)

        wo = get_wo(None)
        tl = min(256, S)
        row = pl.BlockSpec((tl, D_MODEL), lambda i: (i, 0))
        vec = pl.BlockSpec((1, D_MODEL), lambda i: (0, 0))
        u, act, f, dxo, loss, dg_next = _pcall(
            last_body, name=tag + "_fwd", grid=(S // tl,),
            in_specs=[row, pl.BlockSpec(wt.shape, lambda i: (0, 0), pipeline_mode=pl.Buffered(1)),
                      pl.BlockSpec(wo.shape, lambda i: (0, 0), pipeline_mode=pl.Buffered(1)), row, vec, vec, row],
            out_specs=[pl.BlockSpec((tl, N_DEV * FF_PAD), lambda i: (i, 0)),
                       pl.BlockSpec((tl, N_FF_BLK * FF_PAD), lambda i: (i, 0)), row, row,
                       pl.BlockSpec((1, 128), lambda i: (0, 0)), vec],
            out_shape=[_sds((S, N_DEV * FF_PAD), BF16), _sds((S, N_FF_BLK * FF_PAD), BF16), _sds((S, D_MODEL), BF16),
                       _sds((S, D_MODEL)), _sds((1, 128)), _sds((1, D_MODEL))],
        )(h, wt, wo, x, gpost, g_next, tgt)
        return (loss, dxo, dg_next), wo, (h, u, act, f)

    u, act = _pcall(
        up_body, name=tag + "_up", grid=(S // tm,),
        in_specs=[pl.BlockSpec((tm, D_MODEL), lambda i: (i, 0)),
                  pl.BlockSpec(wt.shape, lambda i: (0, 0), pipeline_mode=pl.Buffered(1))],
        out_specs=[pl.BlockSpec((tm, N_DEV * FF_PAD), lambda i: (i, 0)),
                   pl.BlockSpec((tm, N_FF_BLK * FF_PAD), lambda i: (i, 0))],
        out_shape=[_sds((S, N_DEV * FF_PAD), BF16), _sds((S, N_FF_BLK * FF_PAD), BF16)],
    )(h, wt)

    def down_body(act_ref, w_ref, x_ref, g_ref, gn_ref, f_ref, xo_ref, hn_ref):
        f = _dot(act_ref[...], w_ref[...])
        f_ref[...] = f.astype(BF16)
        xo = x_ref[...] + 0.5 * _rms(f, g_ref[...])
        xo_ref[...] = xo
        hn_ref[...] = _rms(xo, gn_ref[...]).astype(BF16)

    wo = get_wo(act)
    td = min(1024, S)
    row = pl.BlockSpec((td, D_MODEL), lambda i: (i, 0))
    vec = pl.BlockSpec((1, D_MODEL), lambda i: (0, 0))
    f, xo, h_next = _pcall(
        down_body, name=tag + "_down", grid=(S // td,),
        in_specs=[pl.BlockSpec((td, N_FF_BLK * FF_PAD), lambda i: (i, 0)),
                  pl.BlockSpec(wo.shape, lambda i: (0, 0), pipeline_mode=pl.Buffered(1)), row, vec, vec],
        out_specs=[row, row, row], out_shape=[_sds((S, D_MODEL), BF16), _sds((S, D_MODEL)), _sds((S, D_MODEL), BF16)],
    )(act, wo, x, gpost, g_next)
    return (xo, h_next), wo, (h, u, act, f)


def _ffn_bwd(tag, dxo, x, gpre, wt, wo, gpost, saved, on_dwo=None, on_dwt=None):
    h, u, act, f = saved
    S = x.shape[0]
    tm = min(512, S)

    def dact_body(f_ref, dxo_ref, g_ref, w_ref, u_ref, df_ref, du_ref, dg_ref):
        i = pl.program_id(0)
        df, dg = _rms_bwd(f_ref[...].astype(F32), g_ref[...], 0.5 * dxo_ref[...])
        df = df.astype(BF16)
        df_ref[...] = df

        @pl.when(i == 0)
        def _():
            dg_ref[...] = dg

        @pl.when(i > 0)
        def _():
            dg_ref[...] += dg

        for j in range(N_FF_BLK):
            dact = _dot(df, w_ref[FF_PAD * j:FF_PAD * (j + 1), :], _NT)
            ab = u_ref[:, 2 * FF_PAD * j:2 * FF_PAD * (j + 1)].astype(F32)
            a, b = ab[:, :FF_PAD], ab[:, FF_PAD:]
            sg = _sigmoid(a)
            du_ref[:, 2 * FF_PAD * j:2 * FF_PAD * j + FF_PAD] = (dact * b * (sg * (1.0 + a * (1.0 - sg)))).astype(BF16)
            du_ref[:, 2 * FF_PAD * j + FF_PAD:2 * FF_PAD * (j + 1)] = (dact * (a * sg)).astype(BF16)

    row = pl.BlockSpec((tm, D_MODEL), lambda i: (i, 0))
    vec = pl.BlockSpec((1, D_MODEL), lambda i: (0, 0))
    u_spec = pl.BlockSpec((tm, N_DEV * FF_PAD), lambda i: (i, 0))
    df, du, dgpost = _pcall(
        dact_body, name=tag + "_dact", grid=(S // tm,),
        in_specs=[row, row, vec, pl.BlockSpec(wo.shape, lambda i: (0, 0), pipeline_mode=pl.Buffered(1)), u_spec],
        out_specs=[row, u_spec, vec],
        out_shape=[_sds((S, D_MODEL), BF16), _sds((S, N_DEV * FF_PAD), BF16), _sds((1, D_MODEL))])(f, dxo, gpost, wo, u)
    dwo = _mm_tn(tag + "_dwo", act, df, ts=S, tn_a=FF_PAD)
    dwt = _mm_tn(tag + "_dwt", du, h, ts=S, tn_a=FF_PAD, deps=[on_dwo(dwo)] if on_dwo is not None else [])
    dx, dgpre = _dh_pre_norm_bwd(tag + "_dh", [du], wt, x, dxo, gpre, [on_dwt(dwt)] if on_dwt is not None else [])
    return dx, dgpre, dgpost


def _dh_pre_norm_bwd(name, dzs, wt, x, dres, g, deps):
    S = x.shape[0]
    ks = [dz.shape[1] for dz in dzs]

    def body(*refs):
        dz_refs = refs[:len(dzs)]
        w_ref, x_ref, dres_ref, g_ref = refs[len(dzs):len(dzs) + 4]
        dx_ref, dg_ref = refs[len(dzs) + 4 + len(deps):]
        i = pl.program_id(0)
        dh, o = None, 0
        for dz_ref, k in zip(dz_refs, ks):
            p = _dot(dz_ref[...], w_ref[o:o + k, :])
            dh, o = (p if dh is None else dh + p), o + k
        dx, dg = _rms_bwd(x_ref[...], g_ref[...], dh)
        dx_ref[...] = dx + dres_ref[...]

        @pl.when(i == 0)
        def _():
            dg_ref[...] = dg

        @pl.when(i > 0)
        def _():
            dg_ref[...] += dg

    th = min(512, S)
    row = pl.BlockSpec((th, D_MODEL), lambda i: (i, 0))
    vec = pl.BlockSpec((1, D_MODEL), lambda i: (0, 0))
    return _pcall(
        body, name=name, grid=(S // th,),
        in_specs=[pl.BlockSpec((th, k), lambda i: (i, 0)) for k in ks]
        + [pl.BlockSpec(wt.shape, lambda i: (0, 0), pipeline_mode=pl.Buffered(1)), row, row, vec]
        + [pl.BlockSpec(memory_space=pl.ANY)] * len(deps),
        out_specs=[row, vec], out_shape=[_sds((S, D_MODEL)), _sds((1, D_MODEL))])(*dzs, wt, x, dres, g, *deps)


GLA_G = 8


def _gla_tile_common(k, flow, wfu, bf):
    f = _dot(flow, wfu) + bf
    la = _log_sigmoid(f) * (1.0 / GATE_TEMP)
    tri = _tri_matrix(True)
    ws, ds = [], []
    for g in range(k.shape[0] // CHUNK):
        b = _tri_dot(tri, la[g * CHUNK:(g + 1) * CHUNK])
        b_end = b[CHUNK - 1:CHUNK, :]
        ws.append(jnp.exp(b_end - b))
        ds.append(jnp.exp(b_end))
    w = jnp.concatenate(ws, axis=0)
    return f, w, k * w, ds


def _tri_matrix(lower):
    r = lax.broadcasted_iota(jnp.int32, (CHUNK, CHUNK), 0)
    c = lax.broadcasted_iota(jnp.int32, (CHUNK, CHUNK), 1)
    return jnp.where((r >= c) if lower else (r <= c), 1.0, 0.0).astype(BF16)


def _heads():
    return [(slice(h * GLA_HDK, (h + 1) * GLA_HDK), slice(h * GLA_HDV, (h + 1) * GLA_HDV)) for h in range(GLA_HEADS)]


def _gla_fwd(proj, wfu, bf, gn, deps=()):
    S = proj.shape[0]
    G = min(GLA_G, S // CHUNK)
    T = G * CHUNK
    nc = S // CHUNK

    def body(q_ref, k_ref, v_ref, go_ref, flow_ref, wfu_ref, bf_ref, gn_ref, *rest):
        ya_ref, st_ref, state = rest[len(deps):]
        @pl.when(pl.program_id(0) == 0)
        def _():
            state[...] = jnp.zeros_like(state)

        _, _, kt, ds = _gla_tile_common(k_ref[...], flow_ref[...], wfu_ref[...], bf_ref[...])
        q = q_ref[...].astype(F32) * (GLA_HDK ** -0.5)
        v = v_ref[...]
        rows = [slice(g * CHUNK, (g + 1) * CHUNK) for g in range(G)]
        kv = [[_dot(v[r, vs], kt[r, ks], _TN) for ks, vs in _heads()] for r in rows]
        st = [state[vs, :] for _, vs in _heads()]
        o = []
        for g, r in enumerate(rows):
            outs = []
            for h, (ks, vs) in enumerate(_heads()):
                st[h] = st[h] * ds[g][:, ks] + kv[g][h]
                st_ref[g, vs, :] = st[h]
                outs.append(_dot(q[r, ks], st[h], _NT))
            o.append(jnp.concatenate(outs, axis=-1))
        for h, (_, vs) in enumerate(_heads()):
            state[vs, :] = st[h]
        ya_ref[...] = _head_rms_gate(jnp.concatenate(o, axis=0), go_ref[...], gn_ref[...]).astype(BF16)

    return _pcall(
        body, name="gla_fwd", grid=(S // T,),
        in_specs=[pl.BlockSpec((T, GLA_DK), lambda c: (c, 0)), pl.BlockSpec((T, GLA_DK), lambda c: (c, 1)),
                  pl.BlockSpec((T, GLA_DV), lambda c: (c, 1)), pl.BlockSpec((T, GLA_DV), lambda c: (c, 2)),
                  pl.BlockSpec((T, FLOW_W), lambda c: (c, (INT_W - FLOW_W) // FLOW_W)),
                  pl.BlockSpec(wfu.shape, lambda c: (0, 0)), pl.BlockSpec(bf.shape, lambda c: (0, 0)),
                  pl.BlockSpec(gn.shape, lambda c: (0, 0))]
        + [pl.BlockSpec(memory_space=pl.ANY)] * len(deps),
        out_specs=[pl.BlockSpec((T, GLA_DV), lambda c: (c, 0)),
                   pl.BlockSpec((G, GLA_DV, GLA_HDK), lambda c: (c, 0, 0))],
        out_shape=[_sds((S, GLA_DV), BF16), _sds((nc, GLA_DV, GLA_HDK))],
        scratch=[pltpu.VMEM((GLA_DV, GLA_HDK), F32)])(proj, proj, proj, proj, proj, wfu, bf, gn, *deps)


def _gla_bwd(proj, wfu, bf, gn, states, d_ya_in, dmain):
    S = proj.shape[0]
    G = min(GLA_G, S // CHUNK)
    T = G * CHUNK
    nt = S // T

    def body(q_ref, k_ref, v_ref, go_ref, flow_ref, wfu_ref, bf_ref, gn_ref, st_ref, stp_ref, dya_ref, dmain_in,
             dqkvg_ref, dflow_ref, dwfu_ref, dbf_ref, dgn_ref, dstate):
        dq_ref = dqkvg_ref.at[:, 0:GLA_DK]
        dk_ref = dqkvg_ref.at[:, GLA_DK:2 * GLA_DK]
        dv_ref = dqkvg_ref.at[:, 2 * GLA_DK:2 * GLA_DK + GLA_DV]
        dgo_ref = dqkvg_ref.at[:, 2 * GLA_DK + GLA_DV:]
        step = pl.program_id(0)

        @pl.when(step == 0)
        def _():
            dstate[...] = jnp.zeros_like(dstate)
            dwfu_ref[...] = jnp.zeros_like(dwfu_ref)
            dbf_ref[...] = jnp.zeros_like(dbf_ref)
            dgn_ref[...] = jnp.zeros_like(dgn_ref)

        flow, wfu_v = flow_ref[...], wfu_ref[...]
        f, w, kt, ds = _gla_tile_common(k_ref[...], flow, wfu_v, bf_ref[...])
        q = q_ref[...].astype(F32) * (GLA_HDK ** -0.5)
        v = v_ref[...]
        rows = [slice(g * CHUNK, (g + 1) * CHUNK) for g in range(G)]
        o = jnp.concatenate([jnp.concatenate([_dot(q[r, ks], st_ref[g, vs, :], _NT) for ks, vs in _heads()], axis=-1)
                             for g, r in enumerate(rows)], axis=0)
        _, vjp = jax.vjp(_head_rms_gate, o, go_ref[...], gn_ref[...])
        dout, dgo, dgn = vjp(dya_ref[...].astype(F32))
        dgo_ref[...] = dgo.astype(dgo_ref.dtype)
        dgn_ref[...] += dgn
        dq = [jnp.concatenate([_dot(dout[r, vs], st_ref[g, vs, :]) for _, vs in _heads()], axis=-1)
              for g, r in enumerate(rows)]
        qdo = [[_dot(dout[r, vs], q[r, ks], _TN) for ks, vs in _heads()] for r in rows]
        dq_ref[...] = (jnp.concatenate(dq, axis=0) * (GLA_HDK ** -0.5)).astype(dq_ref.dtype)
        has_prev = (step < nt - 1).astype(F32)
        carry = [dstate[vs, :] for _, vs in _heads()]
        dkt, dv, dd = [None] * G, [None] * G, [None] * G
        for g in reversed(range(G)):
            r = rows[g]
            dkts, dvs, dds = [], [], []
            for h, (ks, vs) in enumerate(_heads()):
                dst = carry[h] + qdo[g][h]
                dkts.append(_dot(v[r, vs], dst))
                dvs.append(_dot(kt[r, ks], dst, _NT))
                st_prev = st_ref[g - 1, vs, :] if g > 0 else stp_ref[vs, :] * has_prev
                dds.append(jnp.sum(dst * st_prev, axis=0, keepdims=True))
                carry[h] = dst * ds[g][:, ks]
            dkt[g], dv[g], dd[g] = (jnp.concatenate(t, axis=-1) for t in (dkts, dvs, dds))
        for h, (_, vs) in enumerate(_heads()):
            dstate[vs, :] = carry[h]
        dkt = jnp.concatenate(dkt, axis=0)
        dv_ref[...] = jnp.concatenate(dv, axis=0).astype(dv_ref.dtype)
        dk_ref[...] = (dkt * w).astype(dk_ref.dtype)
        de = dkt * kt
        tri = _tri_matrix(False)
        dla = []
        for g, r in enumerate(rows):
            db_end = jnp.sum(de[r], axis=0, keepdims=True) + dd[g] * ds[g]
            dla.append(db_end - _tri_dot(tri, de[r]))
        df = jnp.concatenate(dla, axis=0) * (1.0 - _sigmoid(f)) * (1.0 / GATE_TEMP)
        dflow_ref[...] = _dot(df, wfu_v, _NT).astype(dflow_ref.dtype)
        dwfu_ref[...] += _dot(flow, df, _TN)
        dbf_ref[...] += jnp.sum(df, axis=0, keepdims=True)

    rt = lambda s: nt - 1 - s
    return _pcall(
        body, name="gla_bwd", grid=(nt,),
        in_specs=[pl.BlockSpec((T, GLA_DK), lambda s: (rt(s), 0)), pl.BlockSpec((T, GLA_DK), lambda s: (rt(s), 1)),
                  pl.BlockSpec((T, GLA_DV), lambda s: (rt(s), 1)), pl.BlockSpec((T, GLA_DV), lambda s: (rt(s), 2)),
                  pl.BlockSpec((T, FLOW_W), lambda s: (rt(s), (INT_W - FLOW_W) // FLOW_W)),
                  pl.BlockSpec(wfu.shape, lambda s: (0, 0)), pl.BlockSpec(bf.shape, lambda s: (0, 0)),
                  pl.BlockSpec(gn.shape, lambda s: (0, 0)),
                  pl.BlockSpec((G, GLA_DV, GLA_HDK), lambda s: (rt(s), 0, 0)),
                  pl.BlockSpec((None, GLA_DV, GLA_HDK), lambda s: (jnp.maximum(rt(s) * G - 1, 0), 0, 0)),
                  pl.BlockSpec((T, GLA_DV), lambda s: (rt(s), 0)), pl.BlockSpec(memory_space=pl.ANY)],
        out_specs=[pl.BlockSpec((T, 2 * GLA_DK + 2 * GLA_DV), lambda s: (rt(s), 0)),
                   pl.BlockSpec((T, FLOW_W), lambda s: (rt(s), 0)),
                   pl.BlockSpec(wfu.shape, lambda s: (0, 0)), pl.BlockSpec(bf.shape, lambda s: (0, 0)),
                   pl.BlockSpec(gn.shape, lambda s: (0, 0))],
        out_shape=[_sds(dmain.shape, dmain.dtype), _sds((S, FLOW_W), BF16), _sds(wfu.shape), _sds(bf.shape),
                   _sds(gn.shape)],
        scratch=[pltpu.VMEM((GLA_DV, GLA_HDK), F32)], aliases={11: 0},
    )(proj, proj, proj, proj, proj, wfu, bf, gn, states, states, d_ya_in, dmain)


def _pool_counts(tm, i):
    t = (lax.broadcasted_iota(jnp.int32, (tm, POOL_GD), 0) + i * tm + 1).astype(F32)
    return [jnp.minimum(t, float(w)) for w in POOL_WINDOWS]


def _pool_fwd(proj, w_pool, pool_scale, tm=1024):
    S = proj.shape[0]
    tm = min(tm, S // 2)
    col = _POOL_COL
    hb = tm // POOL_HALO

    def body(p_ref, halo_ref, wp_ref, sc_ref, mixed_ref, out_ref):
        i = pl.program_id(0)
        p = p_ref[...].astype(F32)
        halo = halo_ref[...].astype(F32) * (i > 0).astype(F32)
        ext = jnp.concatenate([halo, p], axis=0)
        n = tm + POOL_HALO
        sums, acc, k = {}, ext, 1
        while k < POOL_WINDOWS[-1]:
            acc = acc + pltpu.roll(acc, k, axis=0)
            k *= 2
            sums[k] = acc
        cnts = _pool_counts(tm, i)
        mixed, lin = [], []
        for g, w in enumerate(POOL_WINDOWS):
            ls = slice(g * POOL_GD, (g + 1) * POOL_GD)
            m = sums[w][POOL_HALO:n, ls] / cnts[g] - p[:, ls]
            mixed.append(m)
            lin.append(_dot(m, wp_ref[g]))
        mixed_ref[...] = jnp.concatenate(mixed, axis=-1)
        out_ref[...] = (jnp.concatenate(lin, axis=-1) * sc_ref[...]).astype(out_ref.dtype)

    return _pcall(
        body, name="pool_fwd", grid=(S // tm,),
        in_specs=[pl.BlockSpec((tm, POOL_W), lambda i: (i, col)),
                  pl.BlockSpec((POOL_HALO, POOL_W), lambda i: (jnp.maximum(i * hb - 1, 0), col)),
                  pl.BlockSpec(w_pool.shape, lambda i: (0, 0, 0)), pl.BlockSpec(pool_scale.shape, lambda i: (0, 0))],
        out_specs=[pl.BlockSpec((tm, POOL_W), lambda i: (i, 0)), pl.BlockSpec((tm, POOL_W), lambda i: (i, 0))],
        out_shape=[_sds((S, POOL_W)), _sds((S, POOL_W), BF16)])(proj, proj, w_pool, pool_scale)


def _pool_lin_bwd(dout, mixed, w_pool, pool_scale):
    S = dout.shape[0]

    def fn(dout, mixed, wp, sc):
        dlin = dout * sc
        dm, dwp, lin = [], [], []
        for g in range(len(POOL_WINDOWS)):
            ls = slice(g * POOL_GD, (g + 1) * POOL_GD)
            lin.append(_dot(mixed[:, ls], wp[g]))
            dm.append(_dot(dlin[:, ls], wp[g], _NT))
            dwp.append(_dot(mixed[:, ls], dlin[:, ls], _TN))
        dsc = jnp.sum(dout * jnp.concatenate(lin, axis=-1), axis=0, keepdims=True)
        return jnp.concatenate(dm, axis=-1), jnp.concatenate(dwp, axis=0), dsc

    return _rowwise("pool_lin_bwd", fn, [dout, mixed], [w_pool, pool_scale], [_sds((S, POOL_W))],
                    [_sds((len(POOL_WINDOWS) * POOL_GD, POOL_GD)), _sds((1, POOL_W))], tm=1024)


def _pool_win_bwd(dmixed, dmain, tm=1024):
    S = dmixed.shape[0]
    tm = min(tm, S // 2)
    nt = S // tm
    hb = tm // POOL_HALO

    def body(dm_ref, halo_ref, dmain_in, dp_ref):
        i = pl.program_id(0)
        dm = dm_ref[...]
        halo = halo_ref[...] * (i < nt - 1).astype(F32)
        cnts = _pool_counts(tm, i)
        cnts_h = [c[:POOL_HALO] for c in _pool_counts(tm, i + 1)]
        r = jnp.concatenate([jnp.concatenate([dm[:, g * POOL_GD:(g + 1) * POOL_GD] / cnts[g] for g in range(4)], axis=-1),
                             jnp.concatenate([halo[:, g * POOL_GD:(g + 1) * POOL_GD] / cnts_h[g] for g in range(4)], axis=-1)],
                            axis=0)
        n = tm + POOL_HALO
        sums, acc, k = {}, r, 1
        while k < POOL_WINDOWS[-1]:
            acc = acc + pltpu.roll(acc, n - k, axis=0)
            k *= 2
            sums[k] = acc
        dp = [sums[w][:tm, g * POOL_GD:(g + 1) * POOL_GD] for g, w in enumerate(POOL_WINDOWS)]
        dp_ref[...] = (jnp.concatenate(dp, axis=-1) - dm).astype(dp_ref.dtype)

    return _pcall(
        body, name="pool_win_bwd", grid=(nt,),
        in_specs=[pl.BlockSpec((tm, POOL_W), lambda i: (i, 0)),
                  pl.BlockSpec((POOL_HALO, POOL_W), lambda i: (jnp.minimum((i + 1) * hb, S // POOL_HALO - 1), 0)),
                  pl.BlockSpec(memory_space=pl.ANY)],
        out_specs=pl.BlockSpec((tm, POOL_W), lambda i: (i, _POOL_COL)),
        out_shape=_sds(dmain.shape, dmain.dtype), aliases={2: 0})(dmixed, dmixed, dmain)


def _xattn_probs(q, kv, h):
    hs = slice(h * XA_HD, (h + 1) * XA_HD)
    s = _dot(q[:, hs], kv[:, hs], _NT) * (XA_HD ** -0.5)
    s = s - jnp.max(s, axis=-1, keepdims=True)
    e = jnp.exp(s)
    return e / jnp.sum(e, axis=-1, keepdims=True)


def _xattn_fwd(proj, kv):
    S = proj.shape[0]

    def fn(q, kv):
        outs = []
        for h in range(XA_HEADS):
            p = _xattn_probs(q, kv, h)
            outs.append(_dot(p, kv[:, XA_W + h * XA_HD:XA_W + (h + 1) * XA_HD]))
        return jnp.concatenate(outs, axis=-1)

    return _rowwise("xattn_fwd", fn, [_Win(proj, XA_W, _XQ_COL)], [kv], [_sds((S, XA_W), BF16)], tm=1024)[0]


def _xattn_bwd(proj, kv, dxa, dmain):
    def fn(q, dxa, kv):
        dqs, dks, dvs = [], [], []
        for h in range(XA_HEADS):
            hs = slice(h * XA_HD, (h + 1) * XA_HD)
            vh = kv[:, XA_W + h * XA_HD:XA_W + (h + 1) * XA_HD]
            p = _xattn_probs(q, kv, h)
            dp = _dot(dxa[:, hs], vh, _NT)
            ds = p * (dp - jnp.sum(p * dp, axis=-1, keepdims=True)) * (XA_HD ** -0.5)
            dqs.append(_dot(ds, kv[:, hs]))
            dks.append(_dot(ds, q[:, hs], _TN))
            dvs.append(_dot(p, dxa[:, hs], _TN))
        return jnp.concatenate(dqs, axis=-1), jnp.concatenate(dks + dvs, axis=-1)

    return _rowwise("xattn_bwd", fn, [_Win(proj, XA_W, _XQ_COL), dxa], [kv], [_Win(dmain, XA_W, _XQ_COL)],
                    [_sds(kv.shape)], tm=1024)


def _mix_fwd(x1, h, mem, W, g_next, on_proj=None, late_weights=None):
    S = x1.shape[0]
    M = mem.shape[0]
    tm = min(512, S)
    def proj_body(h_ref, w_ref, o_ref):
        hh = h_ref[...]
        for j in range(INT_NBLK):
            o_ref[:, INT_BLK * j:INT_BLK * (j + 1)] = _dot(hh, w_ref[INT_BLK * j:INT_BLK * (j + 1), :], _NT).astype(BF16)

    proj = _pcall(
        proj_body, name="mix_proj", grid=(S // tm,),
        in_specs=[pl.BlockSpec((tm, D_MODEL), lambda i: (i, 0)),
                  pl.BlockSpec((INT_W, D_MODEL), lambda i: (0, 0), pipeline_mode=pl.Buffered(1))],
        out_specs=pl.BlockSpec((tm, INT_W), lambda i: (i, 0)), out_shape=_sds((S, INT_W), BF16))(h, W["w_int_t"])
    ya_in, states = _gla_fwd(proj, W["w_fu_pad"], W["b_f"], W["gla_norm_g"],
                             on_proj(proj) if on_proj is not None else [])
    mixed, pool_out = _pool_fwd(proj, W["w_pool"], W["pool_scale"])
    mem_n = _rowwise("mem_norm", lambda m, g: _rms(m, g), [mem], [W["mem_norm_g"]], [_sds((M, D_MODEL), BF16)])[0]
    if late_weights is not None:
        W = {**W, **late_weights([pool_out])}
    kv = _mm_nn("mem_kv", mem_n, W["w_mem_kv"])
    xa = _xattn_fwd(proj, kv)
    def out_fn(gates, ya_in, pool_out, xa, x1, wa, wb, wc, wo, g, gn):
        ya, yb, yc = (_dot(t, w).astype(BF16) for t, w in ((ya_in, wa), (pool_out, wb), (xa, wc)))
        merged = _merge(gates, ya, yb, yc).astype(BF16)
        y = _dot(merged, wo)
        x2 = x1 + _rms(y, g)
        return ya, yb, yc, merged, y, x2, _rms(x2, gn)

    ya, yb, yc, merged, y, x2, h_next = _rowwise(
        "mix_out", out_fn, [_Win(proj, 3 * D_MODEL, 1), ya_in, pool_out, xa, x1],
        [W["w_up_gla"], W["w_up_pool"], W["w_up_xattn"], W["w_o"], W["mix_post_g"], g_next],
        [_sds((S, D_MODEL), BF16)] * 5 + [_sds((S, D_MODEL)), _sds((S, D_MODEL), BF16)])
    return x2, h_next, W, (h, proj, states, ya_in, mixed, pool_out, mem_n, kv, xa, ya, yb, yc, merged, y)


def _mix_bwd(dx2, x1, mem, W, saved, on_grads=None):
    h, proj, states, ya_in, mixed, pool_out, mem_n, kv, xa, ya, yb, yc, merged, y = saved
    S = x1.shape[0]

    def out_bwd(y, dx2, gates, ya, yb, yc, g, wo, wa, wb, wc):
        dy, dg = _rms_bwd(y.astype(F32), g, dx2)
        dy = dy.astype(BF16)
        _, vjp = jax.vjp(_merge, gates, ya, yb, yc)
        dgates, dya, dyb, dyc = vjp(_dot(dy, wo, _NT))
        return (dy, dgates, dya, dyb, dyc, _dot(dya, wa, _NT), _dot(dyb, wb, _NT), _dot(dyc, wc, _NT), dg)

    dmain = lax.empty((S, INT_MAIN), BF16)
    dy, dmain, dya, dyb, dyc, d_ya_in, d_pool_out, d_xa, d_mix_post_g = _rowwise(
        "mix_out_bwd", out_bwd, [y, dx2, _Win(proj, 3 * D_MODEL, 1), ya, yb, yc],
        [W["mix_post_g"], W["w_o"], W["w_up_gla"], W["w_up_pool"], W["w_up_xattn"]],
        [_sds((S, D_MODEL), BF16), _Win(dmain, 3 * D_MODEL, 1)] + [_sds((S, D_MODEL), BF16)] * 4
        + [_sds((S, POOL_W), BF16), _sds((S, XA_W), BF16)], [_sds((1, D_MODEL))])
    d_w_o = _mm_tn("d_w_o", merged, dy, ts=S, tn_a=256)
    d_w_up_gla = _mm_tn("d_w_up_gla", ya_in, dya, ts=S, tn_a=256)
    d_w_up_pool = _mm_tn("d_w_up_pool", pool_out, dyb, ts=S, tn_a=128)
    d_w_up_xattn = _mm_tn("d_w_up_xattn", xa, dyc, ts=S, tn_a=128)

    dmain, dflow, d_wfu_pad, d_b_f, d_gla_norm_g = _gla_bwd(proj, W["w_fu_pad"], W["b_f"], W["gla_norm_g"], states,
                                                            d_ya_in, dmain)
    dmixed, d_w_pool, d_pool_scale = _pool_lin_bwd(d_pool_out, mixed, W["w_pool"], W["pool_scale"])
    dmain = _pool_win_bwd(dmixed, dmain)
    dmain, dkv = _xattn_bwd(proj, kv, d_xa, dmain)
    d_w_mem_kv = _mm_tn("d_w_mem_kv", mem_n, dkv)
    dmem_n = _mm_nt("d_mem_n", dkv, W["w_mem_kv"])
    d_mem_norm_g = _rowwise("mem_norm_bwd", lambda m, d, g: _rms_bwd(m, g, d)[1], [mem, dmem_n], [W["mem_norm_g"]], [],
                            [_sds((1, D_MODEL))])[0]
    d_w_int_t = (_mm_tn("d_w_int", dmain, h, ts=S, tn_a=INT_MAIN // 8), _mm_tn("d_w_flow", dflow, h, ts=S))
    grads = dict(
        w_int_t=d_w_int_t, w_fu_pad=d_wfu_pad, b_f=d_b_f, gla_norm_g=d_gla_norm_g, w_pool=d_w_pool,
        pool_scale=d_pool_scale, mem_norm_g=d_mem_norm_g, w_mem_kv=d_w_mem_kv, w_up_gla=d_w_up_gla,
        w_up_pool=d_w_up_pool, w_up_xattn=d_w_up_xattn, w_o=d_w_o, mix_post_g=d_mix_post_g)
    deps = [on_grads(grads)] if on_grads is not None else []
    dx1, grads["mix_pre_g"] = _dh_pre_norm_bwd("d_mix_h", [dmain, dflow], W["w_int_t"], x1, dx2, W["mix_pre_g"], deps)
    return dx1, grads


def _mesh_pos():
    x, y, c = lax.axis_index("x"), lax.axis_index("y"), lax.axis_index("c")
    return x, y, c, 4 * x + 2 * y + c


def _peer(x, y, c, r):
    px = 1 - x if r & 4 else x
    py = 1 - y if r & 2 else y
    pc = 1 - c if r & 1 else c
    return (px, py, pc), 4 * px + 2 * py + pc


_ALL_PEERS = tuple(range(1, N_DEV))
_SIBLING = 1
_SAME_CORE = (2, 4, 6)


def _dev_slot(ref, dev):
    return ref.at[dev]


def _zero_pad_rows(land):
    pad = FF_PAD - FF_BLK

    def body(land_in, o_ref):
        o_ref[...] = jnp.zeros_like(o_ref)

    return _pcall(body, name="zero_pad_rows", grid=(land.shape[0],), in_specs=[pl.BlockSpec(memory_space=pl.ANY)],
                  out_specs=pl.BlockSpec((None, pad, D_MODEL), lambda j: (j, FF_BLK // pad, 0)),
                  out_shape=_sds(land.shape, land.dtype), aliases={0: 0})(land)


class _Plan:
    def __init__(self, scatter, slots=None, shapes=None):
        self.scatter, self.slots, self.shapes = scatter, slots or {}, shapes or {}

    def src(self, srcs, a, dev):
        return self.slots.get(a, _dev_slot)(srcs[a], dev) if self.scatter else srcs[a]

    def dst(self, lands, a, dev):
        return lands[a].at[dev] if self.scatter else self.slots.get(a, _dev_slot)(lands[a], dev)

    def landing_zones(self, arrays):
        lands = []
        for a, arr in enumerate(arrays):
            if self.scatter:
                lands.append(lax.empty((N_DEV,) + tuple(self.shapes.get(a, arr.shape[1:])), arr.dtype))
            elif a in self.shapes:
                lands.append(_zero_pad_rows(lax.empty(self.shapes[a], arr.dtype)))
            else:
                lands.append(lax.empty((N_DEV,) + arr.shape, arr.dtype))
        return lands


_GATHER = _Plan(False)


def _peer_copies(srcs, lands, send_sems, recv_sems, plan, peers=_ALL_PEERS):
    x, y, c, me = _mesh_pos()
    cps = []
    for r in peers:
        pos, peer = _peer(x, y, c, r)
        for a in range(len(srcs)):
            k = a * (N_DEV - 1) + r - 1
            cps.append(pltpu.make_async_remote_copy(
                src_ref=plan.src(srcs, a, peer), dst_ref=plan.dst(lands, a, me),
                send_sem=send_sems.at[k], recv_sem=recv_sems.at[k], device_id=pos,
                device_id_type=pl.DeviceIdType.MESH))
    return cps


_HBM = pl.BlockSpec(memory_space=pltpu.HBM)
_SEM = pl.BlockSpec(memory_space=pltpu.SEMAPHORE)
_EFFECT = pltpu.SideEffectType.DATAFLOW_SIDE_EFFECTING


def _own_copies(srcs, lands, own_sems, plan):
    me = _mesh_pos()[3]
    return [pltpu.make_async_copy(plan.src(srcs, a, me), plan.dst(lands, a, me), own_sems.at[a])
            for a in range(len(srcs))]


def _exchange_start(name, arrays, plan):
    n = len(arrays)
    lands = plan.landing_zones(arrays)
    n_sem = n * (N_DEV - 1)

    def body(*refs):
        srcs, lands_ = refs[:n], refs[n:2 * n]
        send_sems, recv_sems, own_sems = refs[2 * n:2 * n + 3]
        token = refs[-1]
        for cp in _peer_copies(srcs, lands_, send_sems, recv_sems, plan) + _own_copies(srcs, lands_, own_sems, plan):
            cp.start()
        token[...] = jnp.zeros_like(token)

    hbm = lambda a: pltpu.HBM(a.shape, a.dtype)
    res = pl.pallas_call(
        body, name=name,
        out_shape=(pltpu.SemaphoreType.DMA((n_sem,)), pltpu.SemaphoreType.DMA((n_sem,)), pltpu.SemaphoreType.DMA((n,)),
                   *[hbm(a) for a in arrays], *[hbm(a) for a in lands], _sds((8, 128))),
        in_specs=[_HBM] * (2 * n),
        out_specs=(_SEM, _SEM, _SEM, *[_HBM] * (2 * n), pl.BlockSpec(memory_space=pltpu.VMEM)),
        input_output_aliases={i: 3 + i for i in range(2 * n)},
        compiler_params=pltpu.CompilerParams(has_side_effects=_EFFECT),
    )(*[pltpu.with_memory_space_constraint(a, pltpu.HBM) for a in list(arrays) + lands])
    return (res[:3], res[3:3 + n], res[3 + n:3 + 2 * n], plan), res[-1]


def _exchange_wait(name, state, after):
    sems, srcs, lands, plan = state
    n = len(srcs)

    def body(*refs):
        srcs_, lands_ = refs[:n], refs[n:2 * n]
        send_sems, recv_sems, own_sems = refs[2 * n:2 * n + 3]
        for cp in _peer_copies(srcs_, lands_, send_sems, recv_sems, plan):
            cp.wait_send()
            cp.wait_recv()
        for cp in _own_copies(srcs_, lands_, own_sems, plan):
            cp.wait()

    hbm = lambda a: pltpu.HBM(a.shape, a.dtype)
    res = pl.pallas_call(
        body, name=name, out_shape=tuple(hbm(a) for a in list(srcs) + list(lands)),
        in_specs=[_HBM] * (2 * n) + [_SEM] * 3 + [pl.BlockSpec(memory_space=pl.ANY)] * len(after),
        out_specs=tuple([_HBM] * (2 * n)), input_output_aliases={i: i for i in range(2 * n)},
        compiler_params=pltpu.CompilerParams(has_side_effects=_EFFECT),
    )(*srcs, *lands, *sems, *after)
    return res[n:]


def _gather_start(name, arrays, plan, after):
    n = len(arrays)
    lands = plan.landing_zones(arrays)
    n_sem = n * (N_DEV - 1)

    def body(*refs):
        srcs, lands_ = refs[:n], refs[n:2 * n]
        send_sems, recv_sems, own_sems = refs[2 * n + len(after):2 * n + len(after) + 3]
        token = refs[-1]
        for cp in (_peer_copies(srcs, lands_, send_sems, recv_sems, plan, (_SIBLING,) + _SAME_CORE)
                   + _own_copies(srcs, lands_, own_sems, plan)):
            cp.start()
        token[...] = jnp.zeros_like(token)

    hbm = lambda a: pltpu.HBM(a.shape, a.dtype)
    res = pl.pallas_call(
        body, name=name,
        out_shape=(pltpu.SemaphoreType.DMA((n_sem,)), pltpu.SemaphoreType.DMA((n_sem,)), pltpu.SemaphoreType.DMA((n,)),
                   *[hbm(a) for a in arrays], *[hbm(a) for a in lands], _sds((8, 128))),
        in_specs=[_HBM] * (2 * n) + [pl.BlockSpec(memory_space=pl.ANY)] * len(after),
        out_specs=(_SEM, _SEM, _SEM, *[_HBM] * (2 * n), pl.BlockSpec(memory_space=pltpu.VMEM)),
        input_output_aliases={i: 3 + i for i in range(2 * n)},
        compiler_params=pltpu.CompilerParams(has_side_effects=_EFFECT),
    )(*[pltpu.with_memory_space_constraint(a, pltpu.HBM) for a in list(arrays) + lands], *after)
    return (res[:3], res[3:3 + n], res[3 + n:3 + 2 * n], plan), res[-1]


def _pass_on_copies(lands, send_sems, recv_sems, plan):
    x, y, c, _ = _mesh_pos()
    sibling = _peer(x, y, c, _SIBLING)[0]
    cps = []
    for i, r in enumerate(_SAME_CORE):
        owner = _peer(x, y, c, r)[1]
        for a in range(len(lands)):
            k = a * len(_SAME_CORE) + i
            cps.append(pltpu.make_async_remote_copy(
                src_ref=plan.dst(lands, a, owner), dst_ref=plan.dst(lands, a, owner), send_sem=send_sems.at[k],
                recv_sem=recv_sems.at[k], device_id=sibling, device_id_type=pl.DeviceIdType.MESH))
    return cps


def _gather_pass_on(name, state, after):
    sems, srcs, lands, plan = state
    n = len(srcs)
    n_sem = n * len(_SAME_CORE)

    def body(*refs):
        srcs_, lands_ = refs[:n], refs[n:2 * n]
        send_sems, recv_sems = refs[2 * n], refs[2 * n + 1]
        on_send, on_recv = refs[2 * n + 3 + len(after)], refs[2 * n + 4 + len(after)]
        token = refs[-1]
        arrivals = _peer_copies(srcs_, lands_, send_sems, recv_sems, plan, _SAME_CORE)
        for arrived, on in zip(arrivals, _pass_on_copies(lands_, on_send, on_recv, plan)):
            arrived.wait_recv()
            on.start()
        token[...] = jnp.zeros_like(token)

    hbm = lambda a: pltpu.HBM(a.shape, a.dtype)
    res = pl.pallas_call(
        body, name=name,
        out_shape=(pltpu.SemaphoreType.DMA((n_sem,)), pltpu.SemaphoreType.DMA((n_sem,)),
                   *[hbm(a) for a in list(srcs) + list(lands)], _sds((8, 128))),
        in_specs=[_HBM] * (2 * n) + [_SEM] * 3 + [pl.BlockSpec(memory_space=pl.ANY)] * len(after),
        out_specs=(_SEM, _SEM, *[_HBM] * (2 * n), pl.BlockSpec(memory_space=pltpu.VMEM)),
        input_output_aliases={i: 2 + i for i in range(2 * n)},
        compiler_params=pltpu.CompilerParams(has_side_effects=_EFFECT),
    )(*srcs, *lands, *sems, *after)
    return (sems, res[:2], res[2:2 + n], res[2 + n:2 + 2 * n], plan), res[-1]


def _gather_wait(name, state, after):
    sems, on_sems, srcs, lands, plan = state
    n = len(srcs)

    def body(*refs):
        srcs_, lands_ = refs[:n], refs[n:2 * n]
        send_sems, recv_sems, own_sems, on_send, on_recv = refs[2 * n:2 * n + 5]
        for cp in _peer_copies(srcs_, lands_, send_sems, recv_sems, plan, (_SIBLING,)):
            cp.wait_recv()
        for cp in _peer_copies(srcs_, lands_, send_sems, recv_sems, plan, (_SIBLING,) + _SAME_CORE):
            cp.wait_send()
        for cp in _own_copies(srcs_, lands_, own_sems, plan):
            cp.wait()
        for cp in _pass_on_copies(lands_, on_send, on_recv, plan):
            cp.wait_send()
            cp.wait_recv()

    hbm = lambda a: pltpu.HBM(a.shape, a.dtype)
    res = pl.pallas_call(
        body, name=name, out_shape=tuple(hbm(a) for a in list(srcs) + list(lands)),
        in_specs=[_HBM] * (2 * n) + [_SEM] * 5 + [pl.BlockSpec(memory_space=pl.ANY)] * len(after),
        out_specs=tuple([_HBM] * (2 * n)), input_output_aliases={i: i for i in range(2 * n)},
        compiler_params=pltpu.CompilerParams(has_side_effects=_EFFECT),
    )(*srcs, *lands, *sems, *on_sems, *after)
    return res[n:]


def _sum_parts(name, recv):
    def body(r_ref, o_ref):
        s = r_ref[0]
        for j in range(1, N_DEV):
            s = s + r_ref[j]
        o_ref[...] = s

    return pl.pallas_call(body, name=name, out_shape=_sds(recv.shape[1:], recv.dtype))(recv)


def _adam(name, recv, w, m, v, col=0):
    shape = w.shape
    R, C = shape
    tr = R
    while N_DEV * tr * C * 4 > 6 * 1024 * 1024 and tr % 32 == 0:
        tr //= 2

    def body(recv_ref, w_ref, m_ref, v_ref, g_ref, d_ref, m2_ref, v2_ref):
        g = recv_ref[0].astype(F32)
        for j in range(1, N_DEV):
            g = g + recv_ref[j].astype(F32)
        w_, m_, v_ = w_ref[...], m_ref[...], v_ref[...]
        m2 = ADAM_B1 * m_ + (1.0 - ADAM_B1) * g
        v2 = ADAM_B2 * v_ + (1.0 - ADAM_B2) * (g * g)
        m_hat = m2 / (1.0 - ADAM_B1 ** ADAM_STEP)
        v_hat = v2 / (1.0 - ADAM_B2 ** ADAM_STEP)
        g_ref[...] = g
        d_ref[...] = -ADAM_LR * (m_hat / (jnp.sqrt(v_hat) + ADAM_EPS) + ADAM_WD * w_)
        m2_ref[...] = m2
        v2_ref[...] = v2

    blk = pl.BlockSpec((tr, C), lambda i: (i, 0))
    return _pcall(body, name=name, grid=(R // tr,),
                  in_specs=[pl.BlockSpec((N_DEV, tr, C), lambda i: (0, i, col)), blk, blk, blk],
                  out_specs=[blk] * 4, out_shape=[_sds(shape)] * 4)(recv, w, m, v)


_NAMES = ['ffn1_pre_g', 'ffn1_w_in', 'ffn1_w_out', 'ffn1_post_g', 'mix_pre_g', 'w_in', 'w_fu', 'b_f', 'gla_norm_g',
          'w_pool', 'pool_scale', 'mem_norm_g', 'w_mem_kv', 'w_up_gla', 'w_up_pool', 'w_up_xattn', 'w_o', 'mix_post_g',
          'ffn2_pre_g', 'ffn2_w_in', 'ffn2_w_out', 'ffn2_post_g', 'final_g']
_SHARDED = ['ffn1_w_in', 'ffn1_w_out', 'w_in', 'w_fu', 'w_mem_kv', 'w_up_gla', 'w_up_pool', 'w_up_xattn', 'w_o',
            'ffn2_w_in', 'ffn2_w_out']
_COL_SHARDED = ['w_up_pool', 'w_up_xattn']


def _cols_to_full(g):
    return jnp.transpose(g, (1, 0, 2)).reshape(g.shape[1], N_DEV * g.shape[2])


def _full_to_cols(f):
    R, C = f.shape
    return jnp.transpose(f.reshape(R, N_DEV, C // N_DEV), (1, 0, 2))


def _to_internal(w_in_t):
    o = 0
    parts = []
    for s in IN_SPLITS:
        parts.append(w_in_t[o:o + s])
        o += s
    q, k, v, g_out, f_low, p_in, xq, gates = parts
    f_low = jnp.pad(f_low, ((0, FLOW_W - GATE_RANK), (0, 0)))
    return jnp.concatenate([q, k, v, g_out, gates, p_in, xq, f_low], axis=0)


def _from_internal(d, d_flow):
    q, k, v, g_out = d[0:512], d[512:1024], d[1024:2048], d[2048:3072]
    gates, p_in, xq = d[3072:6144], d[6144:6656], d[6656:7168]
    return jnp.concatenate([q, k, v, g_out, d_flow[:GATE_RANK], p_in, xq, gates], axis=0)


_FFN_IN = ('ffn1_w_in', 'ffn2_w_in')
_FFN_OUT = ('ffn1_w_out', 'ffn2_w_out')
_GATHERS = {"ffn1_in": ['ffn1_w_in'], "ffn1_out": ['ffn1_w_out'],
            "mix_in": ['w_in', 'w_fu'], "mix_rest": ['w_mem_kv', 'w_up_gla', 'w_up_pool', 'w_up_xattn', 'w_o'],
            "ffn2": ['ffn2_w_in', 'ffn2_w_out']}
_MIX = _GATHERS["mix_in"] + _GATHERS["mix_rest"]


def _ffn_in_slot(ref, d):
    return ref.at[2 * (d % N_FF_BLK) + d // N_FF_BLK, pl.ds(0, FF_BLK)]


def _ffn_out_slot(ref, d):
    rows = FF_BLK // 2
    return ref.at[d // 2, pl.ds(pl.multiple_of((d % 2) * rows, rows), rows)]


def _ffn_plan(names, scatter):
    slots, shapes = {}, {}
    for a, n in enumerate(names):
        if n in _FFN_IN:
            slots[a] = _ffn_in_slot
            shapes[a] = (FF_BLK, D_MODEL) if scatter else (N_DEV, FF_PAD, D_MODEL)
        elif n in _FFN_OUT:
            slots[a] = _ffn_out_slot
            shapes[a] = (FF_BLK // 2, D_MODEL) if scatter else (N_FF_BLK, FF_PAD, D_MODEL)
    return _Plan(scatter, slots, shapes)


def _rows_to_full(g):
    return g.reshape(N_DEV * g.shape[1], g.shape[2])


def _mix_weights(gathered):
    W = {}
    for n, g in gathered.items():
        if n == "w_in":
            W["w_int_t"] = _to_internal(g.reshape(IN_WIDTH, D_MODEL))
        elif n == "w_fu":
            W["w_fu_pad"] = jnp.pad(_cols_to_full(g), ((0, FLOW_W - GATE_RANK), (0, 0)))
        else:
            W[n] = _cols_to_full(g) if n in _COL_SHARDED else _rows_to_full(g)
    return W


def _mix_chunks(G, n):
    if n == "w_in":
        return _from_internal(*G["w_int_t"]).reshape(N_DEV, IN_SHARD, D_MODEL)
    if n == "w_fu":
        return _full_to_cols(G["w_fu_pad"][:GATE_RANK].astype(BF16))
    if n in _COL_SHARDED:
        return _full_to_cols(G[n])
    return G[n].reshape(N_DEV, G[n].shape[0] // N_DEV, G[n].shape[1])


def _step(x, mem, tgt, P, Mo, Vo):
    def native(n, a):
        return jnp.swapaxes(a, 0, 1) if n in _FFN_IN + ("w_in",) else a

    P, Mo, Vo = ({n: native(n, a) for n, a in d.items()} for d in (P, Mo, Vo))
    small = {n: P[n] for n in _NAMES if n not in _SHARDED}

    gather, tokens = {}, []
    for grp, names in _GATHERS.items():
        gather[grp], tok = _gather_start("gather_" + grp, [P[n].astype(BF16) for n in names], _ffn_plan(names, False),
                                         tokens[-1:])
        tokens.append(tok)

    def pass_on(grp, after):
        return _gather_pass_on("gather_" + grp + "_on", gather[grp], after)

    def gathered(grp, after):
        return _gather_wait("gather_" + grp + "_wait", pass_on(grp, after)[0], [])

    def ffn_wo(lands):
        return lambda act: lands[0].reshape(N_FF_BLK * FF_PAD, D_MODEL)

    h1 = _pre_norm("ffn1_pre", x, small["ffn1_pre_g"], tokens[:1])
    w1t = gathered("ffn1_in", tokens[-1:] + [h1])[0].reshape(N_DEV * FF_PAD, D_MODEL)
    (x1, hm), w1o, sv1 = _ffn_fwd("ffn1", x, h1, w1t, small["ffn1_post_g"],
                                  lambda act: ffn_wo(gathered("ffn1_out", [act]))(act), small["mix_pre_g"])

    Wm = {**small, **_mix_weights(dict(zip(_GATHERS["mix_in"], gathered("mix_in", [x1]))))}
    passed = {}

    def pass_on_later(proj):
        tokens = []
        for grp in ("mix_rest", "ffn2"):
            passed[grp], tok = pass_on(grp, [proj])
            tokens.append(tok)
        return tokens

    def mix_rest(after):
        names = _GATHERS["mix_rest"]
        return _mix_weights(dict(zip(names, _gather_wait("gather_mix_rest_wait", passed["mix_rest"], after))))

    x2, h2, Wm, svm = _mix_fwd(x1, hm, mem, Wm, small["ffn2_pre_g"], pass_on_later, mix_rest)
    w2t, w2o = _gather_wait("gather_ffn2_wait", passed["ffn2"], [x2])
    w2t = w2t.reshape(N_DEV * FF_PAD, D_MODEL)
    (loss, dx3, d_final_g), w2o, sv2 = _ffn_fwd("ffn2", x2, h2, w2t, small["ffn2_post_g"], ffn_wo([w2o]),
                                                small["final_g"], tgt)

    G = dict(final_g=d_final_g)
    scat = {}

    def start(grp, names, arrays):
        scat[grp] = names, _exchange_start("scatter_" + grp, arrays, _ffn_plan(names, True))
        return scat[grp][1][1]

    def ffn_starts(tag):
        return (lambda dwo: start(tag + "_out", [tag + "_w_out"], [dwo.reshape(N_FF_BLK, FF_PAD, D_MODEL)]),
                lambda dwt: start(tag + "_in", [tag + "_w_in"], [dwt.reshape(N_DEV, FF_PAD, D_MODEL)]))

    dx2, G["ffn2_pre_g"], G["ffn2_post_g"] = _ffn_bwd(
        "ffn2", dx3, x2, small["ffn2_pre_g"], w2t, w2o, small["ffn2_post_g"], sv2, *ffn_starts("ffn2"))
    dx1, Gm = _mix_bwd(dx2, x1, mem, Wm, svm,
                       lambda Gm: start("mix", _MIX, [_mix_chunks(Gm, n) for n in _MIX]))
    G.update(Gm)
    dx, G["ffn1_pre_g"], G["ffn1_post_g"] = _ffn_bwd(
        "ffn1", dx1, x, small["ffn1_pre_g"], w1t, w1o, small["ffn1_post_g"], sv1, *ffn_starts("ffn1"))

    vecs = sorted((n for n in small if P[n].ndim == 2), key=lambda n: -P[n].shape[1])
    packed = jnp.concatenate([G[n] for n in vecs] + [loss], axis=1)
    place, o = {}, 0
    for n in vecs:
        place[n] = o // P[n].shape[1]
        o += P[n].shape[1]
    scat["small"] = ["packed", "w_pool"], _exchange_start("gather_small_grads", [packed, G["w_pool"]], _GATHER)
    recv, outs = {}, {}
    done = [dx]
    for grp in ["ffn2_out", "ffn2_in", "mix", "ffn1_out", "ffn1_in", "small"]:
        names, (state, _) = scat[grp]
        recv.update(zip(names, _exchange_wait(("gather_" if grp == "small" else "scatter_") + grp + "_wait", state,
                                              done)))
        if grp == "small":
            loss = _sum_parts("loss_sum", recv["packed"][:, :, o:])[0, 0]
            names = list(small)
        for n in names:
            shp = P[n].shape
            shp2 = shp if len(shp) == 2 else (shp[0] * shp[1], shp[2])
            r = recv["packed"] if n in place else recv[n].reshape((N_DEV,) + shp2)
            res = _adam("adam_" + n, r, P[n].reshape(shp2), Mo[n].reshape(shp2), Vo[n].reshape(shp2), place.get(n, 0))
            outs[n] = [native(n, t.reshape(shp)) for t in res]
            done.append(res[-1])
    return loss, dx, outs


def kernel(x, mem, ffn1_pre_g, ffn1_w_in, ffn1_w_out, ffn1_post_g, mix_pre_g, w_in, w_fu, b_f, gla_norm_g, w_pool, pool_scale, mem_norm_g, w_mem_kv, w_up_gla, w_up_pool, w_up_xattn, w_o, mix_post_g, ffn2_pre_g, ffn2_w_in, ffn2_w_out, ffn2_post_g, final_g, loss_target, m_ffn1_pre_g, m_ffn1_w_in, m_ffn1_w_out, m_ffn1_post_g, m_mix_pre_g, m_w_in, m_w_fu, m_b_f, m_gla_norm_g, m_w_pool, m_pool_scale, m_mem_norm_g, m_w_mem_kv, m_w_up_gla, m_w_up_pool, m_w_up_xattn, m_w_o, m_mix_post_g, m_ffn2_pre_g, m_ffn2_w_in, m_ffn2_w_out, m_ffn2_post_g, m_final_g, v_ffn1_pre_g, v_ffn1_w_in, v_ffn1_w_out, v_ffn1_post_g, v_mix_pre_g, v_w_in, v_w_fu, v_b_f, v_gla_norm_g, v_w_pool, v_pool_scale, v_mem_norm_g, v_w_mem_kv, v_w_up_gla, v_w_up_pool, v_w_up_xattn, v_w_o, v_mix_post_g, v_ffn2_pre_g, v_ffn2_w_in, v_ffn2_w_out, v_ffn2_post_g, v_final_g):
    params = [ffn1_pre_g, ffn1_w_in, ffn1_w_out, ffn1_post_g, mix_pre_g, w_in, w_fu, b_f, gla_norm_g, w_pool, pool_scale, mem_norm_g, w_mem_kv, w_up_gla, w_up_pool, w_up_xattn, w_o, mix_post_g, ffn2_pre_g, ffn2_w_in, ffn2_w_out, ffn2_post_g, final_g]
    moms = [m_ffn1_pre_g, m_ffn1_w_in, m_ffn1_w_out, m_ffn1_post_g, m_mix_pre_g, m_w_in, m_w_fu, m_b_f, m_gla_norm_g, m_w_pool, m_pool_scale, m_mem_norm_g, m_w_mem_kv, m_w_up_gla, m_w_up_pool, m_w_up_xattn, m_w_o, m_mix_post_g, m_ffn2_pre_g, m_ffn2_w_in, m_ffn2_w_out, m_ffn2_post_g, m_final_g]
    vars_ = [v_ffn1_pre_g, v_ffn1_w_in, v_ffn1_w_out, v_ffn1_post_g, v_mix_pre_g, v_w_in, v_w_fu, v_b_f, v_gla_norm_g, v_w_pool, v_pool_scale, v_mem_norm_g, v_w_mem_kv, v_w_up_gla, v_w_up_pool, v_w_up_xattn, v_w_o, v_mix_post_g, v_ffn2_pre_g, v_ffn2_w_in, v_ffn2_w_out, v_ffn2_post_g, v_final_g]
    P = {n: a[0] if a.ndim > 2 else a for n, a in zip(_NAMES, params)}
    Mo = {n: a[0] if a.ndim > 2 else a for n, a in zip(_NAMES, moms)}
    Vo = {n: a[0] if a.ndim > 2 else a for n, a in zip(_NAMES, vars_)}
    loss, dx, outs = _step(x[0], mem[0], loss_target[0], P, Mo, Vo)
    out = [loss, dx[None]]
    for kind in range(4):
        for n, p in zip(_NAMES, params):
            out.append(outs[n][kind].reshape(p.shape))
    return tuple(out)
```

```python
import functools

import jax
import jax.numpy as jnp
from jax import lax
from jax.experimental import pallas as pl
from jax.experimental.pallas import tpu as pltpu

F32 = jnp.float32
BF16 = jnp.bfloat16

N_DEV = 8
D_MODEL = 1024
D_FF = 2816
FF_BLK = 2 * D_FF // N_DEV
N_FF_BLK = D_FF // FF_BLK
FF_PAD = 768
CHUNK = 64
GLA_HEADS = 4
GLA_DK = 512
GLA_DV = 1024
GLA_HDK = 128
GLA_HDV = 256
GATE_RANK = 16
GATE_TEMP = 16.0
POOL_WINDOWS = (2, 4, 8, 16)
POOL_W = 512
POOL_GD = 128
POOL_HALO = 16
XA_HEADS = 4
XA_HD = 128
XA_W = 512
EPS = 1e-6
IN_SPLITS = (GLA_DK, GLA_DK, GLA_DV, GLA_DV, GATE_RANK, POOL_W, XA_W, 3 * D_MODEL)
IN_WIDTH = sum(IN_SPLITS)
IN_SHARD = IN_WIDTH // N_DEV
INT_W = 3072 + 3072 + 1024 + 128
INT_NBLK = 3
INT_BLK = INT_W // INT_NBLK
FLOW_W = 128
INT_MAIN = INT_W - FLOW_W
_POOL_COL = (3072 + 3072) // POOL_W
_XQ_COL = (3072 + 3072 + POOL_W) // XA_W

ADAM_LR = 0.001
ADAM_B1 = 0.9
ADAM_B2 = 0.999
ADAM_EPS = 1e-08
ADAM_WD = 0.01
ADAM_STEP = 10

VMEM_LIMIT = 56 * 1024 * 1024

_NN = (((1,), (0,)), ((), ()))
_NT = (((1,), (1,)), ((), ()))
_TN = (((0,), (0,)), ((), ()))


def _pcall(body, *, name, grid, in_specs, out_specs, out_shape, scratch=(), aliases=None):
    return pl.pallas_call(
        body, name=name, grid=grid, in_specs=in_specs, out_specs=out_specs, out_shape=out_shape,
        scratch_shapes=list(scratch), input_output_aliases=aliases or {},
        compiler_params=pltpu.CompilerParams(dimension_semantics=("arbitrary",) * len(grid),
                                             vmem_limit_bytes=VMEM_LIMIT))


def _dot(a, b, dims=_NN):
    return lax.dot_general(a.astype(BF16), b.astype(BF16), dims, preferred_element_type=F32)


def _mm(name, a, b, *, grid, a_blk, a_map, b_blk, b_map, o_shape, o_blk, o_map, dims, out_dtype=F32, deps=()):
    nk = grid[2]

    def body(a_ref, b_ref, *rest):
        o_ref, scr = rest[len(deps)], rest[len(deps) + 1:]
        p = _dot(a_ref[...], b_ref[...], dims)
        if nk == 1:
            o_ref[...] = p.astype(o_ref.dtype)
        else:
            acc = scr[0]
            k = pl.program_id(2)

            @pl.when(k == 0)
            def _():
                acc[...] = p

            @pl.when(k > 0)
            def _():
                acc[...] += p

            @pl.when(k == nk - 1)
            def _():
                o_ref[...] = acc[...].astype(o_ref.dtype)

    acc_shape = tuple(d for d in o_blk if d is not None)
    return _pcall(body, name=name, grid=grid,
                  in_specs=[pl.BlockSpec(a_blk, a_map), pl.BlockSpec(b_blk, b_map)]
                  + [pl.BlockSpec(memory_space=pl.ANY)] * len(deps),
                  out_specs=pl.BlockSpec(o_blk, o_map),
                  out_shape=jax.ShapeDtypeStruct(o_shape, out_dtype),
                  scratch=[pltpu.VMEM(acc_shape, F32)] if nk > 1 else [])(a, b, *deps)


def _mm_nn(name, a, b, out_dtype=F32, tm=1024, tn=None):
    M, K = a.shape
    N = b.shape[1]
    tm, tn = min(tm, M), (tn or N)
    return _mm(name, a, b, grid=(N // tn, M // tm, 1), a_blk=(tm, K), a_map=lambda j, i, k: (i, 0),
               b_blk=(K, tn), b_map=lambda j, i, k: (0, j), o_shape=(M, N), o_blk=(tm, tn),
               o_map=lambda j, i, k: (i, j), dims=_NN, out_dtype=out_dtype)


def _mm_nt(name, a, b, out_dtype=F32, tm=1024):
    M, K = a.shape
    N = b.shape[0]
    tm = min(tm, M)
    return _mm(name, a, b, grid=(1, M // tm, 1), a_blk=(tm, K), a_map=lambda j, i, k: (i, 0),
               b_blk=(N, K), b_map=lambda j, i, k: (0, 0), o_shape=(M, N), o_blk=(tm, N),
               o_map=lambda j, i, k: (i, 0), dims=_NT, out_dtype=out_dtype)


def _mm_tn(name, a, b, out_dtype=BF16, ts=512, tn=None, tn_a=None, deps=()):
    S, M = a.shape
    N = b.shape[1]
    ts, tn, tn_a = min(ts, S), (tn or N), (tn_a or M)
    return _mm(name, a, b, grid=((N // tn) * (M // tn_a), 1, S // ts), a_blk=(ts, tn_a),
               a_map=lambda j, i, k: (k, j if tn_a < M else 0), b_blk=(ts, tn),
               b_map=lambda j, i, k: (k, j if tn < N else 0), o_shape=(M, N), o_blk=(tn_a, tn),
               o_map=lambda j, i, k: (j, 0) if tn_a < M else (0, j), dims=_TN, out_dtype=out_dtype, deps=deps)


class _Win:
    def __init__(self, arr, w, c):
        self.arr, self.w, self.c = arr, w, c


def _row_spec(x, tm):
    if isinstance(x, _Win):
        return x.arr, pl.BlockSpec((tm, x.w), functools.partial(lambda i, c: (i, c), c=x.c))
    if x.ndim == 3:
        return x, pl.BlockSpec((x.shape[0], tm, x.shape[2]), lambda i: (0, i, 0))
    return x, pl.BlockSpec((tm, x.shape[1]), lambda i: (i, 0))


def _rowwise(name, fn, rows, consts, outs, accs=(), tm=512):
    first = rows[0].arr if isinstance(rows[0], _Win) else rows[0]
    S = first.shape[1] if first.ndim == 3 else first.shape[0]
    tm = min(tm, S)
    n_in, n_out = len(rows) + len(consts), len(outs)
    arrays, in_specs = [], []
    for r in rows:
        arr, spec = _row_spec(r, tm)
        arrays.append(arr)
        in_specs.append(spec)
    for c in consts:
        arrays.append(c)
        in_specs.append(pl.BlockSpec(c.shape, functools.partial(lambda i, n: (0,) * n, n=c.ndim)))
    aliases = {}
    for k, o in enumerate(outs):
        if isinstance(o, _Win):
            aliases[len(arrays)] = k
            arrays.append(o.arr)
            in_specs.append(pl.BlockSpec(memory_space=pl.ANY))
    n_thru = len(aliases)
    out_specs = [_row_spec(o, tm)[1] for o in outs]
    out_specs += [pl.BlockSpec(a.shape, functools.partial(lambda i, n: (0,) * n, n=len(a.shape))) for a in accs]
    out_shape = [_sds(o.arr.shape, o.arr.dtype) if isinstance(o, _Win) else _sds(o.shape, o.dtype) for o in outs]

    def body(*refs):
        res = fn(*[r[...] for r in refs[:n_in]])
        if not isinstance(res, (tuple, list)):
            res = (res,)
        orefs = refs[n_in + n_thru:]
        for r, v in zip(orefs[:n_out], res[:n_out]):
            r[...] = v.astype(r.dtype)
        i = pl.program_id(0)
        for r, v in zip(orefs[n_out:], res[n_out:]):
            @pl.when(i == 0)
            def _(r=r, v=v):
                r[...] = v.astype(r.dtype)

            @pl.when(i > 0)
            def _(r=r, v=v):
                r[...] += v.astype(r.dtype)

    return _pcall(body, name=name, grid=(S // tm,), in_specs=in_specs, out_specs=out_specs,
                  out_shape=out_shape + [_sds(a.shape, a.dtype) for a in accs], aliases=aliases)(*arrays)


def _sds(shape, dtype=F32):
    return jax.ShapeDtypeStruct(shape, dtype)


def _rms(x, g):
    return x * lax.rsqrt(jnp.mean(x * x, axis=-1, keepdims=True) + EPS) * g


def _rms_bwd(x, g, dy):
    _, vjp = jax.vjp(_rms, x, g)
    return vjp(dy)


def _sigmoid(x):
    return 0.5 * jnp.tanh(0.5 * x) + 0.5


def _silu(x):
    return x * _sigmoid(x)


def _log_sigmoid(f):
    return jnp.minimum(f, 0.0) - jnp.log(1.0 + jnp.exp(-jnp.abs(f)))


def _head_rms_gate(o, g_out, gn):
    parts = [_rms(o[:, h * GLA_HDV:(h + 1) * GLA_HDV], gn[:, h * GLA_HDV:(h + 1) * GLA_HDV]) for h in range(GLA_HEADS)]
    return jnp.concatenate(parts, axis=-1) * _silu(g_out.astype(F32))


def _merge(gates, ya, yb, yc):
    gates, ya, yb, yc = (t.astype(F32) for t in (gates, ya, yb, yc))
    return (_sigmoid(gates[:, :D_MODEL]) * ya + _sigmoid(gates[:, D_MODEL:2 * D_MODEL]) * yb
            + _sigmoid(gates[:, 2 * D_MODEL:]) * yc)


def _tri_dot(t, x):
    hi = x.astype(BF16)
    r1 = x - hi.astype(F32)
    mid = r1.astype(BF16)
    lo = (r1 - mid.astype(F32)).astype(BF16)
    d = functools.partial(lax.dot_general, dimension_numbers=_NN, preferred_element_type=F32)
    return d(t, hi) + d(t, mid) + d(t, lo)


def _pre_norm(name, x, g, after):
    S = x.shape[0]
    tm = min(1024, S)

    def body(x_ref, g_ref, *rest):
        rest[-1][...] = _rms(x_ref[...], g_ref[...]).astype(BF16)

    row = pl.BlockSpec((tm, D_MODEL), lambda i: (i, 0))
    return _pcall(body, name=name, grid=(S // tm,),
                  in_specs=[row, pl.BlockSpec((1, D_MODEL), lambda i: (0, 0))] + [pl.BlockSpec(memory_space=pl.ANY)] * len(after),
                  out_specs=row, out_shape=_sds(x.shape, BF16))(x, g, *after)


def _ffn_fwd(tag, x, h, wt, gpost, get_wo, g_next, tgt=None):
    S = x.shape[0]
    tm = min(512, S)

    def up_body(h_ref, w_ref, u_ref, act_ref):
        hh = h_ref[...]
        for j in range(N_FF_BLK):
            ab = _dot(hh, w_ref[2 * FF_PAD * j:2 * FF_PAD * (j + 1), :], _NT)
            u_ref[:, 2 * FF_PAD * j:2 * FF_PAD * (j + 1)] = ab.astype(BF16)
            act_ref[:, FF_PAD * j:FF_PAD * (j + 1)] = (_silu(ab[:, :FF_PAD]) * ab[:, FF_PAD:]).astype(BF16)

    def down_head_body(act_ref, w_ref, x_ref, g_ref, gn_ref, t_ref, f_ref, dxo_ref, loss_ref, dgn_ref):
        i = pl.program_id(0)
        f = _dot(act_ref[...], w_ref[...])
        f_ref[...] = f.astype(BF16)
        xo = x_ref[...] + 0.5 * _rms(f, g_ref[...])
        out, vjp = jax.vjp(_rms, xo, gn_ref[...])
        e = out - t_ref[...]
        loss = 0.5 * jnp.sum(jnp.mean(e * e, axis=-1, keepdims=True), axis=0, keepdims=True)
        dxo_ref[...], dgn = vjp(e * (1.0 / D_MODEL))

        @pl.when(i == 0)
        def _():
            loss_ref[...] = jnp.broadcast_to(loss, loss_ref.shape)
            dgn_ref[...] = dgn

        @pl.when(i > 0)
        def _():
            loss_ref[...] += jnp.broadcast_to(loss, loss_ref.shape)
            dgn_ref[...] += dgn

    if tgt is not None:
        def last_body(h_ref, wt_ref, wo_ref, x_ref, g_ref, gn_ref, t_ref, u_ref, act_ref, f_ref, dxo_ref, loss_ref,
                      dgn_ref):
            up_body(h_ref, wt_ref, u_ref, act_ref)
            down_head_body(act_ref, wo_ref, x_ref, g_ref, gn_ref, t_ref, f_ref, dxo_ref, loss_ref, dgn_ref)

        wo = get_wo(None)
        tl = min(256, S)
        row = pl.BlockSpec((tl, D_MODEL), lambda i: (i, 0))
        vec = pl.BlockSpec((1, D_MODEL), lambda i: (0, 0))
        u, act, f, dxo, loss, dg_next = _pcall(
            last_body, name=tag + "_fwd", grid=(S // tl,),
            in_specs=[row, pl.BlockSpec(wt.shape, lambda i: (0, 0), pipeline_mode=pl.Buffered(1)),
                      pl.BlockSpec(wo.shape, lambda i: (0, 0), pipeline_mode=pl.Buffered(1)), row, vec, vec, row],
            out_specs=[pl.BlockSpec((tl, N_DEV * FF_PAD), lambda i: (i, 0)),
                       pl.BlockSpec((tl, N_FF_BLK * FF_PAD), lambda i: (i, 0)), row, row,
                       pl.BlockSpec((1, 128), lambda i: (0, 0)), vec],
            out_shape=[_sds((S, N_DEV * FF_PAD), BF16), _sds((S, N_FF_BLK * FF_PAD), BF16), _sds((S, D_MODEL), BF16),
                       _sds((S, D_MODEL)), _sds((1, 128)), _sds((1, D_MODEL))],
        )(h, wt, wo, x, gpost, g_next, tgt)
        return (loss, dxo, dg_next), wo, (h, u, act, f)

    u, act = _pcall(
        up_body, name=tag + "_up", grid=(S // tm,),
        in_specs=[pl.BlockSpec((tm, D_MODEL), lambda i: (i, 0)),
                  pl.BlockSpec(wt.shape, lambda i: (0, 0), pipeline_mode=pl.Buffered(1))],
        out_specs=[pl.BlockSpec((tm, N_DEV * FF_PAD), lambda i: (i, 0)),
                   pl.BlockSpec((tm, N_FF_BLK * FF_PAD), lambda i: (i, 0))],
        out_shape=[_sds((S, N_DEV * FF_PAD), BF16), _sds((S, N_FF_BLK * FF_PAD), BF16)],
    )(h, wt)

    def down_body(act_ref, w_ref, x_ref, g_ref, gn_ref, f_ref, xo_ref, hn_ref):
        f = _dot(act_ref[...], w_ref[...])
        f_ref[...] = f.astype(BF16)
        xo = x_ref[...] + 0.5 * _rms(f, g_ref[...])
        xo_ref[...] = xo
        hn_ref[...] = _rms(xo, gn_ref[...]).astype(BF16)

    wo = get_wo(act)
    row = pl.BlockSpec((tm, D_MODEL), lambda i: (i, 0))
    vec = pl.BlockSpec((1, D_MODEL), lambda i: (0, 0))
    f, xo, h_next = _pcall(
        down_body, name=tag + "_down", grid=(S // tm,),
        in_specs=[pl.BlockSpec((tm, N_FF_BLK * FF_PAD), lambda i: (i, 0)),
                  pl.BlockSpec(wo.shape, lambda i: (0, 0), pipeline_mode=pl.Buffered(1)), row, vec, vec],
        out_specs=[row, row, row], out_shape=[_sds((S, D_MODEL), BF16), _sds((S, D_MODEL)), _sds((S, D_MODEL), BF16)],
    )(act, wo, x, gpost, g_next)
    return (xo, h_next), wo, (h, u, act, f)


def _ffn_bwd(tag, dxo, x, gpre, wt, wo, gpost, saved, on_dwo=None, on_dwt=None):
    h, u, act, f = saved
    S = x.shape[0]
    tm = min(512, S)

    def dact_body(f_ref, dxo_ref, g_ref, w_ref, u_ref, df_ref, du_ref, dg_ref):
        i = pl.program_id(0)
        df, dg = _rms_bwd(f_ref[...].astype(F32), g_ref[...], 0.5 * dxo_ref[...])
        df = df.astype(BF16)
        df_ref[...] = df

        @pl.when(i == 0)
        def _():
            dg_ref[...] = dg

        @pl.when(i > 0)
        def _():
            dg_ref[...] += dg

        for j in range(N_FF_BLK):
            dact = _dot(df, w_ref[FF_PAD * j:FF_PAD * (j + 1), :], _NT)
            ab = u_ref[:, 2 * FF_PAD * j:2 * FF_PAD * (j + 1)].astype(F32)
            a, b = ab[:, :FF_PAD], ab[:, FF_PAD:]
            sg = _sigmoid(a)
            du_ref[:, 2 * FF_PAD * j:2 * FF_PAD * j + FF_PAD] = (dact * b * (sg * (1.0 + a * (1.0 - sg)))).astype(BF16)
            du_ref[:, 2 * FF_PAD * j + FF_PAD:2 * FF_PAD * (j + 1)] = (dact * (a * sg)).astype(BF16)

    row = pl.BlockSpec((tm, D_MODEL), lambda i: (i, 0))
    vec = pl.BlockSpec((1, D_MODEL), lambda i: (0, 0))
    u_spec = pl.BlockSpec((tm, N_DEV * FF_PAD), lambda i: (i, 0))
    df, du, dgpost = _pcall(
        dact_body, name=tag + "_dact", grid=(S // tm,),
        in_specs=[row, row, vec, pl.BlockSpec(wo.shape, lambda i: (0, 0), pipeline_mode=pl.Buffered(1)), u_spec],
        out_specs=[row, u_spec, vec],
        out_shape=[_sds((S, D_MODEL), BF16), _sds((S, N_DEV * FF_PAD), BF16), _sds((1, D_MODEL))])(f, dxo, gpost, wo, u)
    dwo = _mm_tn(tag + "_dwo", act, df, ts=S, tn_a=FF_PAD)
    dwt = _mm_tn(tag + "_dwt", du, h, ts=S, tn_a=FF_PAD, deps=[on_dwo(dwo)] if on_dwo is not None else [])
    dx, dgpre = _dh_pre_norm_bwd(tag + "_dh", [du], wt, x, dxo, gpre, [on_dwt(dwt)] if on_dwt is not None else [])
    return dx, dgpre, dgpost


def _dh_pre_norm_bwd(name, dzs, wt, x, dres, g, deps):
    S = x.shape[0]
    ks = [dz.shape[1] for dz in dzs]

    def body(*refs):
        dz_refs = refs[:len(dzs)]
        w_ref, x_ref, dres_ref, g_ref = refs[len(dzs):len(dzs) + 4]
        dx_ref, dg_ref = refs[len(dzs) + 4 + len(deps):]
        i = pl.program_id(0)
        dh, o = None, 0
        for dz_ref, k in zip(dz_refs, ks):
            p = _dot(dz_ref[...], w_ref[o:o + k, :])
            dh, o = (p if dh is None else dh + p), o + k
        dx, dg = _rms_bwd(x_ref[...], g_ref[...], dh)
        dx_ref[...] = dx + dres_ref[...]

        @pl.when(i == 0)
        def _():
            dg_ref[...] = dg

        @pl.when(i > 0)
        def _():
            dg_ref[...] += dg

    th = min(512, S)
    row = pl.BlockSpec((th, D_MODEL), lambda i: (i, 0))
    vec = pl.BlockSpec((1, D_MODEL), lambda i: (0, 0))
    return _pcall(
        body, name=name, grid=(S // th,),
        in_specs=[pl.BlockSpec((th, k), lambda i: (i, 0)) for k in ks]
        + [pl.BlockSpec(wt.shape, lambda i: (0, 0), pipeline_mode=pl.Buffered(1)), row, row, vec]
        + [pl.BlockSpec(memory_space=pl.ANY)] * len(deps),
        out_specs=[row, vec], out_shape=[_sds((S, D_MODEL)), _sds((1, D_MODEL))])(*dzs, wt, x, dres, g, *deps)


GLA_G = 8


def _gla_tile_common(k, flow, wfu, bf):
    f = _dot(flow, wfu) + bf
    la = _log_sigmoid(f) * (1.0 / GATE_TEMP)
    tri = _tri_matrix(True)
    ws, ds = [], []
    for g in range(k.shape[0] // CHUNK):
        b = _tri_dot(tri, la[g * CHUNK:(g + 1) * CHUNK])
        b_end = b[CHUNK - 1:CHUNK, :]
        ws.append(jnp.exp(b_end - b))
        ds.append(jnp.exp(b_end))
    w = jnp.concatenate(ws, axis=0)
    return f, w, k * w, ds


def _tri_matrix(lower):
    r = lax.broadcasted_iota(jnp.int32, (CHUNK, CHUNK), 0)
    c = lax.broadcasted_iota(jnp.int32, (CHUNK, CHUNK), 1)
    return jnp.where((r >= c) if lower else (r <= c), 1.0, 0.0).astype(BF16)


def _heads():
    return [(slice(h * GLA_HDK, (h + 1) * GLA_HDK), slice(h * GLA_HDV, (h + 1) * GLA_HDV)) for h in range(GLA_HEADS)]


def _gla_fwd(proj, wfu, bf, gn, deps=()):
    S = proj.shape[0]
    G = min(GLA_G, S // CHUNK)
    T = G * CHUNK
    nc = S // CHUNK

    def body(q_ref, k_ref, v_ref, go_ref, flow_ref, wfu_ref, bf_ref, gn_ref, *rest):
        ya_ref, st_ref, state = rest[len(deps):]
        @pl.when(pl.program_id(0) == 0)
        def _():
            state[...] = jnp.zeros_like(state)

        _, _, kt, ds = _gla_tile_common(k_ref[...], flow_ref[...], wfu_ref[...], bf_ref[...])
        q = q_ref[...].astype(F32) * (GLA_HDK ** -0.5)
        v = v_ref[...]
        rows = [slice(g * CHUNK, (g + 1) * CHUNK) for g in range(G)]
        kv = [[_dot(v[r, vs], kt[r, ks], _TN) for ks, vs in _heads()] for r in rows]
        st = [state[vs, :] for _, vs in _heads()]
        o = []
        for g, r in enumerate(rows):
            outs = []
            for h, (ks, vs) in enumerate(_heads()):
                st[h] = st[h] * ds[g][:, ks] + kv[g][h]
                st_ref[g, vs, :] = st[h]
                outs.append(_dot(q[r, ks], st[h], _NT))
            o.append(jnp.concatenate(outs, axis=-1))
        for h, (_, vs) in enumerate(_heads()):
            state[vs, :] = st[h]
        ya_ref[...] = _head_rms_gate(jnp.concatenate(o, axis=0), go_ref[...], gn_ref[...]).astype(BF16)

    return _pcall(
        body, name="gla_fwd", grid=(S // T,),
        in_specs=[pl.BlockSpec((T, GLA_DK), lambda c: (c, 0)), pl.BlockSpec((T, GLA_DK), lambda c: (c, 1)),
                  pl.BlockSpec((T, GLA_DV), lambda c: (c, 1)), pl.BlockSpec((T, GLA_DV), lambda c: (c, 2)),
                  pl.BlockSpec((T, FLOW_W), lambda c: (c, (INT_W - FLOW_W) // FLOW_W)),
                  pl.BlockSpec(wfu.shape, lambda c: (0, 0)), pl.BlockSpec(bf.shape, lambda c: (0, 0)),
                  pl.BlockSpec(gn.shape, lambda c: (0, 0))]
        + [pl.BlockSpec(memory_space=pl.ANY)] * len(deps),
        out_specs=[pl.BlockSpec((T, GLA_DV), lambda c: (c, 0)),
                   pl.BlockSpec((G, GLA_DV, GLA_HDK), lambda c: (c, 0, 0))],
        out_shape=[_sds((S, GLA_DV), BF16), _sds((nc, GLA_DV, GLA_HDK))],
        scratch=[pltpu.VMEM((GLA_DV, GLA_HDK), F32)])(proj, proj, proj, proj, proj, wfu, bf, gn, *deps)


def _gla_bwd(proj, wfu, bf, gn, states, d_ya_in, dmain):
    S = proj.shape[0]
    G = min(GLA_G, S // CHUNK)
    T = G * CHUNK
    nt = S // T

    def body(q_ref, k_ref, v_ref, go_ref, flow_ref, wfu_ref, bf_ref, gn_ref, st_ref, stp_ref, dya_ref, dmain_in,
             dqkvg_ref, dflow_ref, dwfu_ref, dbf_ref, dgn_ref, dstate):
        dq_ref = dqkvg_ref.at[:, 0:GLA_DK]
        dk_ref = dqkvg_ref.at[:, GLA_DK:2 * GLA_DK]
        dv_ref = dqkvg_ref.at[:, 2 * GLA_DK:2 * GLA_DK + GLA_DV]
        dgo_ref = dqkvg_ref.at[:, 2 * GLA_DK + GLA_DV:]
        step = pl.program_id(0)

        @pl.when(step == 0)
        def _():
            dstate[...] = jnp.zeros_like(dstate)
            dwfu_ref[...] = jnp.zeros_like(dwfu_ref)
            dbf_ref[...] = jnp.zeros_like(dbf_ref)
            dgn_ref[...] = jnp.zeros_like(dgn_ref)

        flow, wfu_v = flow_ref[...], wfu_ref[...]
        f, w, kt, ds = _gla_tile_common(k_ref[...], flow, wfu_v, bf_ref[...])
        q = q_ref[...].astype(F32) * (GLA_HDK ** -0.5)
        v = v_ref[...]
        rows = [slice(g * CHUNK, (g + 1) * CHUNK) for g in range(G)]
        o = jnp.concatenate([jnp.concatenate([_dot(q[r, ks], st_ref[g, vs, :], _NT) for ks, vs in _heads()], axis=-1)
                             for g, r in enumerate(rows)], axis=0)
        _, vjp = jax.vjp(_head_rms_gate, o, go_ref[...], gn_ref[...])
        dout, dgo, dgn = vjp(dya_ref[...].astype(F32))
        dgo_ref[...] = dgo.astype(dgo_ref.dtype)
        dgn_ref[...] += dgn
        dq = [jnp.concatenate([_dot(dout[r, vs], st_ref[g, vs, :]) for _, vs in _heads()], axis=-1)
              for g, r in enumerate(rows)]
        qdo = [[_dot(dout[r, vs], q[r, ks], _TN) for ks, vs in _heads()] for r in rows]
        dq_ref[...] = (jnp.concatenate(dq, axis=0) * (GLA_HDK ** -0.5)).astype(dq_ref.dtype)
        has_prev = (step < nt - 1).astype(F32)
        carry = [dstate[vs, :] for _, vs in _heads()]
        dkt, dv, dd = [None] * G, [None] * G, [None] * G
        for g in reversed(range(G)):
            r = rows[g]
            dkts, dvs, dds = [], [], []
            for h, (ks, vs) in enumerate(_heads()):
                dst = carry[h] + qdo[g][h]
                dkts.append(_dot(v[r, vs], dst))
                dvs.append(_dot(kt[r, ks], dst, _NT))
                st_prev = st_ref[g - 1, vs, :] if g > 0 else stp_ref[vs, :] * has_prev
                dds.append(jnp.sum(dst * st_prev, axis=0, keepdims=True))
                carry[h] = dst * ds[g][:, ks]
            dkt[g], dv[g], dd[g] = (jnp.concatenate(t, axis=-1) for t in (dkts, dvs, dds))
        for h, (_, vs) in enumerate(_heads()):
            dstate[vs, :] = carry[h]
        dkt = jnp.concatenate(dkt, axis=0)
        dv_ref[...] = jnp.concatenate(dv, axis=0).astype(dv_ref.dtype)
        dk_ref[...] = (dkt * w).astype(dk_ref.dtype)
        de = dkt * kt
        tri = _tri_matrix(False)
        dla = []
        for g, r in enumerate(rows):
            db_end = jnp.sum(de[r], axis=0, keepdims=True) + dd[g] * ds[g]
            dla.append(db_end - _tri_dot(tri, de[r]))
        df = jnp.concatenate(dla, axis=0) * (1.0 - _sigmoid(f)) * (1.0 / GATE_TEMP)
        dflow_ref[...] = _dot(df, wfu_v, _NT).astype(dflow_ref.dtype)
        dwfu_ref[...] += _dot(flow, df, _TN)
        dbf_ref[...] += jnp.sum(df, axis=0, keepdims=True)

    rt = lambda s: nt - 1 - s
    return _pcall(
        body, name="gla_bwd", grid=(nt,),
        in_specs=[pl.BlockSpec((T, GLA_DK), lambda s: (rt(s), 0)), pl.BlockSpec((T, GLA_DK), lambda s: (rt(s), 1)),
                  pl.BlockSpec((T, GLA_DV), lambda s: (rt(s), 1)), pl.BlockSpec((T, GLA_DV), lambda s: (rt(s), 2)),
                  pl.BlockSpec((T, FLOW_W), lambda s: (rt(s), (INT_W - FLOW_W) // FLOW_W)),
                  pl.BlockSpec(wfu.shape, lambda s: (0, 0)), pl.BlockSpec(bf.shape, lambda s: (0, 0)),
                  pl.BlockSpec(gn.shape, lambda s: (0, 0)),
                  pl.BlockSpec((G, GLA_DV, GLA_HDK), lambda s: (rt(s), 0, 0)),
                  pl.BlockSpec((None, GLA_DV, GLA_HDK), lambda s: (jnp.maximum(rt(s) * G - 1, 0), 0, 0)),
                  pl.BlockSpec((T, GLA_DV), lambda s: (rt(s), 0)), pl.BlockSpec(memory_space=pl.ANY)],
        out_specs=[pl.BlockSpec((T, 2 * GLA_DK + 2 * GLA_DV), lambda s: (rt(s), 0)),
                   pl.BlockSpec((T, FLOW_W), lambda s: (rt(s), 0)),
                   pl.BlockSpec(wfu.shape, lambda s: (0, 0)), pl.BlockSpec(bf.shape, lambda s: (0, 0)),
                   pl.BlockSpec(gn.shape, lambda s: (0, 0))],
        out_shape=[_sds(dmain.shape, dmain.dtype), _sds((S, FLOW_W), BF16), _sds(wfu.shape), _sds(bf.shape),
                   _sds(gn.shape)],
        scratch=[pltpu.VMEM((GLA_DV, GLA_HDK), F32)], aliases={11: 0},
    )(proj, proj, proj, proj, proj, wfu, bf, gn, states, states, d_ya_in, dmain)


def _pool_counts(tm, i):
    t = (lax.broadcasted_iota(jnp.int32, (tm, POOL_GD), 0) + i * tm + 1).astype(F32)
    return [jnp.minimum(t, float(w)) for w in POOL_WINDOWS]


def _pool_fwd(proj, w_pool, pool_scale, tm=1024):
    S = proj.shape[0]
    tm = min(tm, S // 2)
    col = _POOL_COL
    hb = tm // POOL_HALO

    def body(p_ref, halo_ref, wp_ref, sc_ref, mixed_ref, out_ref):
        i = pl.program_id(0)
        p = p_ref[...].astype(F32)
        halo = halo_ref[...].astype(F32) * (i > 0).astype(F32)
        ext = jnp.concatenate([halo, p], axis=0)
        n = tm + POOL_HALO
        sums, acc, k = {}, ext, 1
        while k < POOL_WINDOWS[-1]:
            acc = acc + pltpu.roll(acc, k, axis=0)
            k *= 2
            sums[k] = acc
        cnts = _pool_counts(tm, i)
        mixed, lin = [], []
        for g, w in enumerate(POOL_WINDOWS):
            ls = slice(g * POOL_GD, (g + 1) * POOL_GD)
            m = sums[w][POOL_HALO:n, ls] / cnts[g] - p[:, ls]
            mixed.append(m)
            lin.append(_dot(m, wp_ref[g]))
        mixed_ref[...] = jnp.concatenate(mixed, axis=-1)
        out_ref[...] = (jnp.concatenate(lin, axis=-1) * sc_ref[...]).astype(out_ref.dtype)

    return _pcall(
        body, name="pool_fwd", grid=(S // tm,),
        in_specs=[pl.BlockSpec((tm, POOL_W), lambda i: (i, col)),
                  pl.BlockSpec((POOL_HALO, POOL_W), lambda i: (jnp.maximum(i * hb - 1, 0), col)),
                  pl.BlockSpec(w_pool.shape, lambda i: (0, 0, 0)), pl.BlockSpec(pool_scale.shape, lambda i: (0, 0))],
        out_specs=[pl.BlockSpec((tm, POOL_W), lambda i: (i, 0)), pl.BlockSpec((tm, POOL_W), lambda i: (i, 0))],
        out_shape=[_sds((S, POOL_W)), _sds((S, POOL_W), BF16)])(proj, proj, w_pool, pool_scale)


def _pool_lin_bwd(dout, mixed, w_pool, pool_scale):
    S = dout.shape[0]

    def fn(dout, mixed, wp, sc):
        dlin = dout * sc
        dm, dwp, lin = [], [], []
        for g in range(len(POOL_WINDOWS)):
            ls = slice(g * POOL_GD, (g + 1) * POOL_GD)
            lin.append(_dot(mixed[:, ls], wp[g]))
            dm.append(_dot(dlin[:, ls], wp[g], _NT))
            dwp.append(_dot(mixed[:, ls], dlin[:, ls], _TN))
        dsc = jnp.sum(dout * jnp.concatenate(lin, axis=-1), axis=0, keepdims=True)
        return jnp.concatenate(dm, axis=-1), jnp.concatenate(dwp, axis=0), dsc

    return _rowwise("pool_lin_bwd", fn, [dout, mixed], [w_pool, pool_scale], [_sds((S, POOL_W))],
                    [_sds((len(POOL_WINDOWS) * POOL_GD, POOL_GD)), _sds((1, POOL_W))], tm=1024)


def _pool_win_bwd(dmixed, dmain, tm=1024):
    S = dmixed.shape[0]
    tm = min(tm, S // 2)
    nt = S // tm
    hb = tm // POOL_HALO

    def body(dm_ref, halo_ref, dmain_in, dp_ref):
        i = pl.program_id(0)
        dm = dm_ref[...]
        halo = halo_ref[...] * (i < nt - 1).astype(F32)
        cnts = _pool_counts(tm, i)
        cnts_h = [c[:POOL_HALO] for c in _pool_counts(tm, i + 1)]
        r = jnp.concatenate([jnp.concatenate([dm[:, g * POOL_GD:(g + 1) * POOL_GD] / cnts[g] for g in range(4)], axis=-1),
                             jnp.concatenate([halo[:, g * POOL_GD:(g + 1) * POOL_GD] / cnts_h[g] for g in range(4)], axis=-1)],
                            axis=0)
        n = tm + POOL_HALO
        sums, acc, k = {}, r, 1
        while k < POOL_WINDOWS[-1]:
            acc = acc + pltpu.roll(acc, n - k, axis=0)
            k *= 2
            sums[k] = acc
        dp = [sums[w][:tm, g * POOL_GD:(g + 1) * POOL_GD] for g, w in enumerate(POOL_WINDOWS)]
        dp_ref[...] = (jnp.concatenate(dp, axis=-1) - dm).astype(dp_ref.dtype)

    return _pcall(
        body, name="pool_win_bwd", grid=(nt,),
        in_specs=[pl.BlockSpec((tm, POOL_W), lambda i: (i, 0)),
                  pl.BlockSpec((POOL_HALO, POOL_W), lambda i: (jnp.minimum((i + 1) * hb, S // POOL_HALO - 1), 0)),
                  pl.BlockSpec(memory_space=pl.ANY)],
        out_specs=pl.BlockSpec((tm, POOL_W), lambda i: (i, _POOL_COL)),
        out_shape=_sds(dmain.shape, dmain.dtype), aliases={2: 0})(dmixed, dmixed, dmain)


def _xattn_probs(q, kv, h):
    hs = slice(h * XA_HD, (h + 1) * XA_HD)
    s = _dot(q[:, hs], kv[:, hs], _NT) * (XA_HD ** -0.5)
    s = s - jnp.max(s, axis=-1, keepdims=True)
    e = jnp.exp(s)
    return e / jnp.sum(e, axis=-1, keepdims=True)


def _xattn_fwd(proj, kv):
    S = proj.shape[0]

    def fn(q, kv):
        outs = []
        for h in range(XA_HEADS):
            p = _xattn_probs(q, kv, h)
            outs.append(_dot(p, kv[:, XA_W + h * XA_HD:XA_W + (h + 1) * XA_HD]))
        return jnp.concatenate(outs, axis=-1)

    return _rowwise("xattn_fwd", fn, [_Win(proj, XA_W, _XQ_COL)], [kv], [_sds((S, XA_W), BF16)], tm=1024)[0]


def _xattn_bwd(proj, kv, dxa, dmain):
    def fn(q, dxa, kv):
        dqs, dks, dvs = [], [], []
        for h in range(XA_HEADS):
            hs = slice(h * XA_HD, (h + 1) * XA_HD)
            vh = kv[:, XA_W + h * XA_HD:XA_W + (h + 1) * XA_HD]
            p = _xattn_probs(q, kv, h)
            dp = _dot(dxa[:, hs], vh, _NT)
            ds = p * (dp - jnp.sum(p * dp, axis=-1, keepdims=True)) * (XA_HD ** -0.5)
            dqs.append(_dot(ds, kv[:, hs]))
            dks.append(_dot(ds, q[:, hs], _TN))
            dvs.append(_dot(p, dxa[:, hs], _TN))
        return jnp.concatenate(dqs, axis=-1), jnp.concatenate(dks + dvs, axis=-1)

    return _rowwise("xattn_bwd", fn, [_Win(proj, XA_W, _XQ_COL), dxa], [kv], [_Win(dmain, XA_W, _XQ_COL)],
                    [_sds(kv.shape)], tm=1024)


def _mix_fwd(x1, h, mem, W, g_next, on_proj=None, late_weights=None):
    S = x1.shape[0]
    M = mem.shape[0]
    tm = min(512, S)
    def proj_body(h_ref, w_ref, o_ref):
        hh = h_ref[...]
        for j in range(INT_NBLK):
            o_ref[:, INT_BLK * j:INT_BLK * (j + 1)] = _dot(hh, w_ref[INT_BLK * j:INT_BLK * (j + 1), :], _NT).astype(BF16)

    proj = _pcall(
        proj_body, name="mix_proj", grid=(S // tm,),
        in_specs=[pl.BlockSpec((tm, D_MODEL), lambda i: (i, 0)),
                  pl.BlockSpec((INT_W, D_MODEL), lambda i: (0, 0), pipeline_mode=pl.Buffered(1))],
        out_specs=pl.BlockSpec((tm, INT_W), lambda i: (i, 0)), out_shape=_sds((S, INT_W), BF16))(h, W["w_int_t"])
    ya_in, states = _gla_fwd(proj, W["w_fu_pad"], W["b_f"], W["gla_norm_g"],
                             on_proj(proj) if on_proj is not None else [])
    mixed, pool_out = _pool_fwd(proj, W["w_pool"], W["pool_scale"])
    mem_n = _rowwise("mem_norm", lambda m, g: _rms(m, g), [mem], [W["mem_norm_g"]], [_sds((M, D_MODEL), BF16)])[0]
    if late_weights is not None:
        W = {**W, **late_weights([pool_out])}
    kv = _mm_nn("mem_kv", mem_n, W["w_mem_kv"])
    xa = _xattn_fwd(proj, kv)
    def out_fn(gates, ya_in, pool_out, xa, x1, wa, wb, wc, wo, g, gn):
        ya, yb, yc = (_dot(t, w).astype(BF16) for t, w in ((ya_in, wa), (pool_out, wb), (xa, wc)))
        merged = _merge(gates, ya, yb, yc).astype(BF16)
        y = _dot(merged, wo)
        x2 = x1 + _rms(y, g)
        return ya, yb, yc, merged, y, x2, _rms(x2, gn)

    ya, yb, yc, merged, y, x2, h_next = _rowwise(
        "mix_out", out_fn, [_Win(proj, 3 * D_MODEL, 1), ya_in, pool_out, xa, x1],
        [W["w_up_gla"], W["w_up_pool"], W["w_up_xattn"], W["w_o"], W["mix_post_g"], g_next],
        [_sds((S, D_MODEL), BF16)] * 5 + [_sds((S, D_MODEL)), _sds((S, D_MODEL), BF16)])
    return x2, h_next, W, (h, proj, states, ya_in, mixed, pool_out, mem_n, kv, xa, ya, yb, yc, merged, y)


def _mix_bwd(dx2, x1, mem, W, saved, on_grads=None):
    h, proj, states, ya_in, mixed, pool_out, mem_n, kv, xa, ya, yb, yc, merged, y = saved
    S = x1.shape[0]

    def out_bwd(y, dx2, gates, ya, yb, yc, g, wo, wa, wb, wc):
        dy, dg = _rms_bwd(y.astype(F32), g, dx2)
        dy = dy.astype(BF16)
        _, vjp = jax.vjp(_merge, gates, ya, yb, yc)
        dgates, dya, dyb, dyc = vjp(_dot(dy, wo, _NT))
        return (dy, dgates, dya, dyb, dyc, _dot(dya, wa, _NT), _dot(dyb, wb, _NT), _dot(dyc, wc, _NT), dg)

    dmain = lax.empty((S, INT_MAIN), BF16)
    dy, dmain, dya, dyb, dyc, d_ya_in, d_pool_out, d_xa, d_mix_post_g = _rowwise(
        "mix_out_bwd", out_bwd, [y, dx2, _Win(proj, 3 * D_MODEL, 1), ya, yb, yc],
        [W["mix_post_g"], W["w_o"], W["w_up_gla"], W["w_up_pool"], W["w_up_xattn"]],
        [_sds((S, D_MODEL), BF16), _Win(dmain, 3 * D_MODEL, 1)] + [_sds((S, D_MODEL), BF16)] * 4
        + [_sds((S, POOL_W), BF16), _sds((S, XA_W), BF16)], [_sds((1, D_MODEL))])
    d_w_o = _mm_tn("d_w_o", merged, dy, ts=S, tn_a=256)
    d_w_up_gla = _mm_tn("d_w_up_gla", ya_in, dya, ts=S, tn_a=256)
    d_w_up_pool = _mm_tn("d_w_up_pool", pool_out, dyb, ts=S, tn_a=128)
    d_w_up_xattn = _mm_tn("d_w_up_xattn", xa, dyc, ts=S, tn_a=128)

    dmain, dflow, d_wfu_pad, d_b_f, d_gla_norm_g = _gla_bwd(proj, W["w_fu_pad"], W["b_f"], W["gla_norm_g"], states,
                                                            d_ya_in, dmain)
    dmixed, d_w_pool, d_pool_scale = _pool_lin_bwd(d_pool_out, mixed, W["w_pool"], W["pool_scale"])
    dmain = _pool_win_bwd(dmixed, dmain)
    dmain, dkv = _xattn_bwd(proj, kv, d_xa, dmain)
    d_w_mem_kv = _mm_tn("d_w_mem_kv", mem_n, dkv)
    dmem_n = _mm_nt("d_mem_n", dkv, W["w_mem_kv"])
    d_mem_norm_g = _rowwise("mem_norm_bwd", lambda m, d, g: _rms_bwd(m, g, d)[1], [mem, dmem_n], [W["mem_norm_g"]], [],
                            [_sds((1, D_MODEL))])[0]
    d_w_int_t = (_mm_tn("d_w_int", dmain, h, ts=S, tn_a=INT_MAIN // 8), _mm_tn("d_w_flow", dflow, h, ts=S))
    grads = dict(
        w_int_t=d_w_int_t, w_fu_pad=d_wfu_pad, b_f=d_b_f, gla_norm_g=d_gla_norm_g, w_pool=d_w_pool,
        pool_scale=d_pool_scale, mem_norm_g=d_mem_norm_g, w_mem_kv=d_w_mem_kv, w_up_gla=d_w_up_gla,
        w_up_pool=d_w_up_pool, w_up_xattn=d_w_up_xattn, w_o=d_w_o, mix_post_g=d_mix_post_g)
    deps = [on_grads(grads)] if on_grads is not None else []
    dx1, grads["mix_pre_g"] = _dh_pre_norm_bwd("d_mix_h", [dmain, dflow], W["w_int_t"], x1, dx2, W["mix_pre_g"], deps)
    return dx1, grads


def _mesh_pos():
    x, y, c = lax.axis_index("x"), lax.axis_index("y"), lax.axis_index("c")
    return x, y, c, 4 * x + 2 * y + c


def _peer(x, y, c, r):
    px = 1 - x if r & 4 else x
    py = 1 - y if r & 2 else y
    pc = 1 - c if r & 1 else c
    return (px, py, pc), 4 * px + 2 * py + pc


_ALL_PEERS = tuple(range(1, N_DEV))
_SIBLING = 1
_SAME_CORE = (2, 4, 6)


def _dev_slot(ref, dev):
    return ref.at[dev]


def _zero_pad_rows(land):
    pad = FF_PAD - FF_BLK

    def body(land_in, o_ref):
        o_ref[...] = jnp.zeros_like(o_ref)

    return _pcall(body, name="zero_pad_rows", grid=(land.shape[0],), in_specs=[pl.BlockSpec(memory_space=pl.ANY)],
                  out_specs=pl.BlockSpec((None, pad, D_MODEL), lambda j: (j, FF_BLK // pad, 0)),
                  out_shape=_sds(land.shape, land.dtype), aliases={0: 0})(land)


class _Plan:
    def __init__(self, scatter, slots=None, shapes=None):
        self.scatter, self.slots, self.shapes = scatter, slots or {}, shapes or {}

    def src(self, srcs, a, dev):
        return self.slots.get(a, _dev_slot)(srcs[a], dev) if self.scatter else srcs[a]

    def dst(self, lands, a, dev):
        return lands[a].at[dev] if self.scatter else self.slots.get(a, _dev_slot)(lands[a], dev)

    def landing_zones(self, arrays):
        lands = []
        for a, arr in enumerate(arrays):
            if self.scatter:
                lands.append(lax.empty((N_DEV,) + tuple(self.shapes.get(a, arr.shape[1:])), arr.dtype))
            elif a in self.shapes:
                lands.append(_zero_pad_rows(lax.empty(self.shapes[a], arr.dtype)))
            else:
                lands.append(lax.empty((N_DEV,) + arr.shape, arr.dtype))
        return lands


_GATHER = _Plan(False)


def _peer_copies(srcs, lands, send_sems, recv_sems, plan, peers=_ALL_PEERS):
    x, y, c, me = _mesh_pos()
    cps = []
    for r in peers:
        pos, peer = _peer(x, y, c, r)
        for a in range(len(srcs)):
            k = a * (N_DEV - 1) + r - 1
            cps.append(pltpu.make_async_remote_copy(
                src_ref=plan.src(srcs, a, peer), dst_ref=plan.dst(lands, a, me),
                send_sem=send_sems.at[k], recv_sem=recv_sems.at[k], device_id=pos,
                device_id_type=pl.DeviceIdType.MESH))
    return cps


_HBM = pl.BlockSpec(memory_space=pltpu.HBM)
_SEM = pl.BlockSpec(memory_space=pltpu.SEMAPHORE)
_EFFECT = pltpu.SideEffectType.DATAFLOW_SIDE_EFFECTING


def _own_copies(srcs, lands, own_sems, plan):
    me = _mesh_pos()[3]
    return [pltpu.make_async_copy(plan.src(srcs, a, me), plan.dst(lands, a, me), own_sems.at[a])
            for a in range(len(srcs))]


def _exchange_start(name, arrays, plan):
    n = len(arrays)
    lands = plan.landing_zones(arrays)
    n_sem = n * (N_DEV - 1)

    def body(*refs):
        srcs, lands_ = refs[:n], refs[n:2 * n]
        send_sems, recv_sems, own_sems = refs[2 * n:2 * n + 3]
        token = refs[-1]
        for cp in _peer_copies(srcs, lands_, send_sems, recv_sems, plan) + _own_copies(srcs, lands_, own_sems, plan):
            cp.start()
        token[...] = jnp.zeros_like(token)

    hbm = lambda a: pltpu.HBM(a.shape, a.dtype)
    res = pl.pallas_call(
        body, name=name,
        out_shape=(pltpu.SemaphoreType.DMA((n_sem,)), pltpu.SemaphoreType.DMA((n_sem,)), pltpu.SemaphoreType.DMA((n,)),
                   *[hbm(a) for a in arrays], *[hbm(a) for a in lands], _sds((8, 128))),
        in_specs=[_HBM] * (2 * n),
        out_specs=(_SEM, _SEM, _SEM, *[_HBM] * (2 * n), pl.BlockSpec(memory_space=pltpu.VMEM)),
        input_output_aliases={i: 3 + i for i in range(2 * n)},
        compiler_params=pltpu.CompilerParams(has_side_effects=_EFFECT),
    )(*[pltpu.with_memory_space_constraint(a, pltpu.HBM) for a in list(arrays) + lands])
    return (res[:3], res[3:3 + n], res[3 + n:3 + 2 * n], plan), res[-1]


def _exchange_wait(name, state, after):
    sems, srcs, lands, plan = state
    n = len(srcs)

    def body(*refs):
        srcs_, lands_ = refs[:n], refs[n:2 * n]
        send_sems, recv_sems, own_sems = refs[2 * n:2 * n + 3]
        for cp in _peer_copies(srcs_, lands_, send_sems, recv_sems, plan):
            cp.wait_send()
            cp.wait_recv()
        for cp in _own_copies(srcs_, lands_, own_sems, plan):
            cp.wait()

    hbm = lambda a: pltpu.HBM(a.shape, a.dtype)
    res = pl.pallas_call(
        body, name=name, out_shape=tuple(hbm(a) for a in list(srcs) + list(lands)),
        in_specs=[_HBM] * (2 * n) + [_SEM] * 3 + [pl.BlockSpec(memory_space=pl.ANY)] * len(after),
        out_specs=tuple([_HBM] * (2 * n)), input_output_aliases={i: i for i in range(2 * n)},
        compiler_params=pltpu.CompilerParams(has_side_effects=_EFFECT),
    )(*srcs, *lands, *sems, *after)
    return res[n:]


def _gather_start(name, arrays, plan, after):
    n = len(arrays)
    lands = plan.landing_zones(arrays)
    n_sem = n * (N_DEV - 1)

    def body(*refs):
        srcs, lands_ = refs[:n], refs[n:2 * n]
        send_sems, recv_sems, own_sems = refs[2 * n + len(after):2 * n + len(after) + 3]
        token = refs[-1]
        for cp in (_peer_copies(srcs, lands_, send_sems, recv_sems, plan, (_SIBLING,) + _SAME_CORE)
                   + _own_copies(srcs, lands_, own_sems, plan)):
            cp.start()
        token[...] = jnp.zeros_like(token)

    hbm = lambda a: pltpu.HBM(a.shape, a.dtype)
    res = pl.pallas_call(
        body, name=name,
        out_shape=(pltpu.SemaphoreType.DMA((n_sem,)), pltpu.SemaphoreType.DMA((n_sem,)), pltpu.SemaphoreType.DMA((n,)),
                   *[hbm(a) for a in arrays], *[hbm(a) for a in lands], _sds((8, 128))),
        in_specs=[_HBM] * (2 * n) + [pl.BlockSpec(memory_space=pl.ANY)] * len(after),
        out_specs=(_SEM, _SEM, _SEM, *[_HBM] * (2 * n), pl.BlockSpec(memory_space=pltpu.VMEM)),
        input_output_aliases={i: 3 + i for i in range(2 * n)},
        compiler_params=pltpu.CompilerParams(has_side_effects=_EFFECT),
    )(*[pltpu.with_memory_space_constraint(a, pltpu.HBM) for a in list(arrays) + lands], *after)
    return (res[:3], res[3:3 + n], res[3 + n:3 + 2 * n], plan), res[-1]


def _pass_on_copies(lands, send_sems, recv_sems, plan):
    x, y, c, _ = _mesh_pos()
    sibling = _peer(x, y, c, _SIBLING)[0]
    cps = []
    for i, r in enumerate(_SAME_CORE):
        owner = _peer(x, y, c, r)[1]
        for a in range(len(lands)):
            k = a * len(_SAME_CORE) + i
            cps.append(pltpu.make_async_remote_copy(
                src_ref=plan.dst(lands, a, owner), dst_ref=plan.dst(lands, a, owner), send_sem=send_sems.at[k],
                recv_sem=recv_sems.at[k], device_id=sibling, device_id_type=pl.DeviceIdType.MESH))
    return cps


def _gather_pass_on(name, state, after):
    sems, srcs, lands, plan = state
    n = len(srcs)
    n_sem = n * len(_SAME_CORE)

    def body(*refs):
        srcs_, lands_ = refs[:n], refs[n:2 * n]
        send_sems, recv_sems = refs[2 * n], refs[2 * n + 1]
        on_send, on_recv = refs[2 * n + 3 + len(after)], refs[2 * n + 4 + len(after)]
        token = refs[-1]
        arrivals = _peer_copies(srcs_, lands_, send_sems, recv_sems, plan, _SAME_CORE)
        for arrived, on in zip(arrivals, _pass_on_copies(lands_, on_send, on_recv, plan)):
            arrived.wait_recv()
            on.start()
        token[...] = jnp.zeros_like(token)

    hbm = lambda a: pltpu.HBM(a.shape, a.dtype)
    res = pl.pallas_call(
        body, name=name,
        out_shape=(pltpu.SemaphoreType.DMA((n_sem,)), pltpu.SemaphoreType.DMA((n_sem,)),
                   *[hbm(a) for a in list(srcs) + list(lands)], _sds((8, 128))),
        in_specs=[_HBM] * (2 * n) + [_SEM] * 3 + [pl.BlockSpec(memory_space=pl.ANY)] * len(after),
        out_specs=(_SEM, _SEM, *[_HBM] * (2 * n), pl.BlockSpec(memory_space=pltpu.VMEM)),
        input_output_aliases={i: 2 + i for i in range(2 * n)},
        compiler_params=pltpu.CompilerParams(has_side_effects=_EFFECT),
    )(*srcs, *lands, *sems, *after)
    return (sems, res[:2], res[2:2 + n], res[2 + n:2 + 2 * n], plan), res[-1]


def _gather_wait(name, state, after):
    sems, on_sems, srcs, lands, plan = state
    n = len(srcs)

    def body(*refs):
        srcs_, lands_ = refs[:n], refs[n:2 * n]
        send_sems, recv_sems, own_sems, on_send, on_recv = refs[2 * n:2 * n + 5]
        for cp in _peer_copies(srcs_, lands_, send_sems, recv_sems, plan, (_SIBLING,)):
            cp.wait_recv()
        for cp in _peer_copies(srcs_, lands_, send_sems, recv_sems, plan, (_SIBLING,) + _SAME_CORE):
            cp.wait_send()
        for cp in _own_copies(srcs_, lands_, own_sems, plan):
            cp.wait()
        for cp in _pass_on_copies(lands_, on_send, on_recv, plan):
            cp.wait_send()
            cp.wait_recv()

    hbm = lambda a: pltpu.HBM(a.shape, a.dtype)
    res = pl.pallas_call(
        body, name=name, out_shape=tuple(hbm(a) for a in list(srcs) + list(lands)),
        in_specs=[_HBM] * (2 * n) + [_SEM] * 5 + [pl.BlockSpec(memory_space=pl.ANY)] * len(after),
        out_specs=tuple([_HBM] * (2 * n)), input_output_aliases={i: i for i in range(2 * n)},
        compiler_params=pltpu.CompilerParams(has_side_effects=_EFFECT),
    )(*srcs, *lands, *sems, *on_sems, *after)
    return res[n:]


def _sum_parts(name, recv):
    def body(r_ref, o_ref):
        s = r_ref[0]
        for j in range(1, N_DEV):
            s = s + r_ref[j]
        o_ref[...] = s

    return pl.pallas_call(body, name=name, out_shape=_sds(recv.shape[1:], recv.dtype))(recv)


def _adam(name, recv, w, m, v, col=0):
    shape = w.shape
    R, C = shape
    tr = R
    while N_DEV * tr * C * 4 > 6 * 1024 * 1024 and tr % 32 == 0:
        tr //= 2

    def body(recv_ref, w_ref, m_ref, v_ref, g_ref, d_ref, m2_ref, v2_ref):
        g = recv_ref[0].astype(F32)
        for j in range(1, N_DEV):
            g = g + recv_ref[j].astype(F32)
        w_, m_, v_ = w_ref[...], m_ref[...], v_ref[...]
        m2 = ADAM_B1 * m_ + (1.0 - ADAM_B1) * g
        v2 = ADAM_B2 * v_ + (1.0 - ADAM_B2) * (g * g)
        m_hat = m2 / (1.0 - ADAM_B1 ** ADAM_STEP)
        v_hat = v2 / (1.0 - ADAM_B2 ** ADAM_STEP)
        g_ref[...] = g
        d_ref[...] = -ADAM_LR * (m_hat / (jnp.sqrt(v_hat) + ADAM_EPS) + ADAM_WD * w_)
        m2_ref[...] = m2
        v2_ref[...] = v2

    blk = pl.BlockSpec((tr, C), lambda i: (i, 0))
    return _pcall(body, name=name, grid=(R // tr,),
                  in_specs=[pl.BlockSpec((N_DEV, tr, C), lambda i: (0, i, col)), blk, blk, blk],
                  out_specs=[blk] * 4, out_shape=[_sds(shape)] * 4)(recv, w, m, v)


_NAMES = ['ffn1_pre_g', 'ffn1_w_in', 'ffn1_w_out', 'ffn1_post_g', 'mix_pre_g', 'w_in', 'w_fu', 'b_f', 'gla_norm_g',
          'w_pool', 'pool_scale', 'mem_norm_g', 'w_mem_kv', 'w_up_gla', 'w_up_pool', 'w_up_xattn', 'w_o', 'mix_post_g',
          'ffn2_pre_g', 'ffn2_w_in', 'ffn2_w_out', 'ffn2_post_g', 'final_g']
_SHARDED = ['ffn1_w_in', 'ffn1_w_out', 'w_in', 'w_fu', 'w_mem_kv', 'w_up_gla', 'w_up_pool', 'w_up_xattn', 'w_o',
            'ffn2_w_in', 'ffn2_w_out']
_COL_SHARDED = ['w_up_pool', 'w_up_xattn']


def _cols_to_full(g):
    return jnp.transpose(g, (1, 0, 2)).reshape(g.shape[1], N_DEV * g.shape[2])


def _full_to_cols(f):
    R, C = f.shape
    return jnp.transpose(f.reshape(R, N_DEV, C // N_DEV), (1, 0, 2))


def _to_internal(w_in_t):
    o = 0
    parts = []
    for s in IN_SPLITS:
        parts.append(w_in_t[o:o + s])
        o += s
    q, k, v, g_out, f_low, p_in, xq, gates = parts
    f_low = jnp.pad(f_low, ((0, FLOW_W - GATE_RANK), (0, 0)))
    return jnp.concatenate([q, k, v, g_out, gates, p_in, xq, f_low], axis=0)


def _from_internal(d, d_flow):
    q, k, v, g_out = d[0:512], d[512:1024], d[1024:2048], d[2048:3072]
    gates, p_in, xq = d[3072:6144], d[6144:6656], d[6656:7168]
    return jnp.concatenate([q, k, v, g_out, d_flow[:GATE_RANK], p_in, xq, gates], axis=0)


_FFN_IN = ('ffn1_w_in', 'ffn2_w_in')
_FFN_OUT = ('ffn1_w_out', 'ffn2_w_out')
_GATHERS = {"ffn1_in": ['ffn1_w_in'], "ffn1_out": ['ffn1_w_out'],
            "mix_in": ['w_in', 'w_fu'], "mix_rest": ['w_mem_kv', 'w_up_gla', 'w_up_pool', 'w_up_xattn', 'w_o'],
            "ffn2": ['ffn2_w_in', 'ffn2_w_out']}
_MIX = _GATHERS["mix_in"] + _GATHERS["mix_rest"]


def _ffn_in_slot(ref, d):
    return ref.at[2 * (d % N_FF_BLK) + d // N_FF_BLK, pl.ds(0, FF_BLK)]


def _ffn_out_slot(ref, d):
    rows = FF_BLK // 2
    return ref.at[d // 2, pl.ds(pl.multiple_of((d % 2) * rows, rows), rows)]


def _ffn_plan(names, scatter):
    slots, shapes = {}, {}
    for a, n in enumerate(names):
        if n in _FFN_IN:
            slots[a] = _ffn_in_slot
            shapes[a] = (FF_BLK, D_MODEL) if scatter else (N_DEV, FF_PAD, D_MODEL)
        elif n in _FFN_OUT:
            slots[a] = _ffn_out_slot
            shapes[a] = (FF_BLK // 2, D_MODEL) if scatter else (N_FF_BLK, FF_PAD, D_MODEL)
    return _Plan(scatter, slots, shapes)


def _rows_to_full(g):
    return g.reshape(N_DEV * g.shape[1], g.shape[2])


def _mix_weights(gathered):
    W = {}
    for n, g in gathered.items():
        if n == "w_in":
            W["w_int_t"] = _to_internal(g.reshape(IN_WIDTH, D_MODEL))
        elif n == "w_fu":
            W["w_fu_pad"] = jnp.pad(_cols_to_full(g), ((0, FLOW_W - GATE_RANK), (0, 0)))
        else:
            W[n] = _cols_to_full(g) if n in _COL_SHARDED else _rows_to_full(g)
    return W


def _mix_chunks(G, n):
    if n == "w_in":
        return _from_internal(*G["w_int_t"]).reshape(N_DEV, IN_SHARD, D_MODEL)
    if n == "w_fu":
        return _full_to_cols(G["w_fu_pad"][:GATE_RANK].astype(BF16))
    if n in _COL_SHARDED:
        return _full_to_cols(G[n])
    return G[n].reshape(N_DEV, G[n].shape[0] // N_DEV, G[n].shape[1])


def _step(x, mem, tgt, P, Mo, Vo):
    def native(n, a):
        return jnp.swapaxes(a, 0, 1) if n in _FFN_IN + ("w_in",) else a

    P, Mo, Vo = ({n: native(n, a) for n, a in d.items()} for d in (P, Mo, Vo))
    small = {n: P[n] for n in _NAMES if n not in _SHARDED}

    gather, tokens = {}, []
    for grp, names in _GATHERS.items():
        gather[grp], tok = _gather_start("gather_" + grp, [P[n].astype(BF16) for n in names], _ffn_plan(names, False),
                                         tokens[-1:])
        tokens.append(tok)

    def pass_on(grp, after):
        return _gather_pass_on("gather_" + grp + "_on", gather[grp], after)

    def gathered(grp, after):
        return _gather_wait("gather_" + grp + "_wait", pass_on(grp, after)[0], [])

    def ffn_wo(lands):
        return lambda act: lands[0].reshape(N_FF_BLK * FF_PAD, D_MODEL)

    h1 = _pre_norm("ffn1_pre", x, small["ffn1_pre_g"], tokens[:1])
    w1t = gathered("ffn1_in", tokens[-1:] + [h1])[0].reshape(N_DEV * FF_PAD, D_MODEL)
    (x1, hm), w1o, sv1 = _ffn_fwd("ffn1", x, h1, w1t, small["ffn1_post_g"],
                                  lambda act: ffn_wo(gathered("ffn1_out", [act]))(act), small["mix_pre_g"])

    Wm = {**small, **_mix_weights(dict(zip(_GATHERS["mix_in"], gathered("mix_in", [x1]))))}
    passed = {}

    def pass_on_later(proj):
        tokens = []
        for grp in ("mix_rest", "ffn2"):
            passed[grp], tok = pass_on(grp, [proj])
            tokens.append(tok)
        return tokens

    def mix_rest(after):
        names = _GATHERS["mix_rest"]
        return _mix_weights(dict(zip(names, _gather_wait("gather_mix_rest_wait", passed["mix_rest"], after))))

    x2, h2, Wm, svm = _mix_fwd(x1, hm, mem, Wm, small["ffn2_pre_g"], pass_on_later, mix_rest)
    w2t, w2o = _gather_wait("gather_ffn2_wait", passed["ffn2"], [x2])
    w2t = w2t.reshape(N_DEV * FF_PAD, D_MODEL)
    (loss, dx3, d_final_g), w2o, sv2 = _ffn_fwd("ffn2", x2, h2, w2t, small["ffn2_post_g"], ffn_wo([w2o]),
                                                small["final_g"], tgt)

    G = dict(final_g=d_final_g)
    scat = {}

    def start(grp, names, arrays):
        scat[grp] = names, _exchange_start("scatter_" + grp, arrays, _ffn_plan(names, True))
        return scat[grp][1][1]

    def ffn_starts(tag):
        return (lambda dwo: start(tag + "_out", [tag + "_w_out"], [dwo.reshape(N_FF_BLK, FF_PAD, D_MODEL)]),
                lambda dwt: start(tag + "_in", [tag + "_w_in"], [dwt.reshape(N_DEV, FF_PAD, D_MODEL)]))

    dx2, G["ffn2_pre_g"], G["ffn2_post_g"] = _ffn_bwd(
        "ffn2", dx3, x2, small["ffn2_pre_g"], w2t, w2o, small["ffn2_post_g"], sv2, *ffn_starts("ffn2"))
    dx1, Gm = _mix_bwd(dx2, x1, mem, Wm, svm,
                       lambda Gm: start("mix", _MIX, [_mix_chunks(Gm, n) for n in _MIX]))
    G.update(Gm)
    dx, G["ffn1_pre_g"], G["ffn1_post_g"] = _ffn_bwd(
        "ffn1", dx1, x, small["ffn1_pre_g"], w1t, w1o, small["ffn1_post_g"], sv1, *ffn_starts("ffn1"))

    vecs = sorted((n for n in small if P[n].ndim == 2), key=lambda n: -P[n].shape[1])
    packed = jnp.concatenate([G[n] for n in vecs] + [loss], axis=1)
    place, o = {}, 0
    for n in vecs:
        place[n] = o // P[n].shape[1]
        o += P[n].shape[1]
    scat["small"] = ["packed", "w_pool"], _exchange_start("gather_small_grads", [packed, G["w_pool"]], _GATHER)
    recv, outs = {}, {}
    done = [dx]
    for grp in ["ffn2_out", "ffn2_in", "mix", "ffn1_out", "ffn1_in", "small"]:
        names, (state, _) = scat[grp]
        recv.update(zip(names, _exchange_wait(("gather_" if grp == "small" else "scatter_") + grp + "_wait", state,
                                              done)))
        if grp == "small":
            loss = _sum_parts("loss_sum", recv["packed"][:, :, o:])[0, 0]
            names = list(small)
        for n in names:
            shp = P[n].shape
            shp2 = shp if len(shp) == 2 else (shp[0] * shp[1], shp[2])
            r = recv["packed"] if n in place else recv[n].reshape((N_DEV,) + shp2)
            res = _adam("adam_" + n, r, P[n].reshape(shp2), Mo[n].reshape(shp2), Vo[n].reshape(shp2), place.get(n, 0))
            outs[n] = [native(n, t.reshape(shp)) for t in res]
            done.append(res[-1])
    return loss, dx, outs


def kernel(x, mem, ffn1_pre_g, ffn1_w_in, ffn1_w_out, ffn1_post_g, mix_pre_g, w_in, w_fu, b_f, gla_norm_g, w_pool, pool_scale, mem_norm_g, w_mem_kv, w_up_gla, w_up_pool, w_up_xattn, w_o, mix_post_g, ffn2_pre_g, ffn2_w_in, ffn2_w_out, ffn2_post_g, final_g, loss_target, m_ffn1_pre_g, m_ffn1_w_in, m_ffn1_w_out, m_ffn1_post_g, m_mix_pre_g, m_w_in, m_w_fu, m_b_f, m_gla_norm_g, m_w_pool, m_pool_scale, m_mem_norm_g, m_w_mem_kv, m_w_up_gla, m_w_up_pool, m_w_up_xattn, m_w_o, m_mix_post_g, m_ffn2_pre_g, m_ffn2_w_in, m_ffn2_w_out, m_ffn2_post_g, m_final_g, v_ffn1_pre_g, v_ffn1_w_in, v_ffn1_w_out, v_ffn1_post_g, v_mix_pre_g, v_w_in, v_w_fu, v_b_f, v_gla_norm_g, v_w_pool, v_pool_scale, v_mem_norm_g, v_w_mem_kv, v_w_up_gla, v_w_up_pool, v_w_up_xattn, v_w_o, v_mix_post_g, v_ffn2_pre_g, v_ffn2_w_in, v_ffn2_w_out, v_ffn2_post_g, v_final_g):
    params = [ffn1_pre_g, ffn1_w_in, ffn1_w_out, ffn1_post_g, mix_pre_g, w_in, w_fu, b_f, gla_norm_g, w_pool, pool_scale, mem_norm_g, w_mem_kv, w_up_gla, w_up_pool, w_up_xattn, w_o, mix_post_g, ffn2_pre_g, ffn2_w_in, ffn2_w_out, ffn2_post_g, final_g]
    moms = [m_ffn1_pre_g, m_ffn1_w_in, m_ffn1_w_out, m_ffn1_post_g, m_mix_pre_g, m_w_in, m_w_fu, m_b_f, m_gla_norm_g, m_w_pool, m_pool_scale, m_mem_norm_g, m_w_mem_kv, m_w_up_gla, m_w_up_pool, m_w_up_xattn, m_w_o, m_mix_post_g, m_ffn2_pre_g, m_ffn2_w_in, m_ffn2_w_out, m_ffn2_post_g, m_final_g]
    vars_ = [v_ffn1_pre_g, v_ffn1_w_in, v_ffn1_w_out, v_ffn1_post_g, v_mix_pre_g, v_w_in, v_w_fu, v_b_f, v_gla_norm_g, v_w_pool, v_pool_scale, v_mem_norm_g, v_w_mem_kv, v_w_up_gla, v_w_up_pool, v_w_up_xattn, v_w_o, v_mix_post_g, v_ffn2_pre_g, v_ffn2_w_in, v_ffn2_w_out, v_ffn2_post_g, v_final_g]
    P = {n: a[0] if a.ndim > 2 else a for n, a in zip(_NAMES, params)}
    Mo = {n: a[0] if a.ndim > 2 else a for n, a in zip(_NAMES, moms)}
    Vo = {n: a[0] if a.ndim > 2 else a for n, a in zip(_NAMES, vars_)}
    loss, dx, outs = _step(x[0], mem[0], loss_target[0], P, Mo, Vo)
    out = [loss, dx[None]]
    for kind in range(4):
        for n, p in zip(_NAMES, params):
            out.append(outs[n][kind].reshape(p.shape))
    return tuple(out)
```

```python
import functools

import jax
import jax.numpy as jnp
from jax import lax
from jax.experimental import pallas as pl
from jax.experimental.pallas import tpu as pltpu

F32 = jnp.float32
BF16 = jnp.bfloat16

N_DEV = 8
D_MODEL = 1024
D_FF = 2816
FF_BLK = 2 * D_FF // N_DEV
N_FF_BLK = D_FF // FF_BLK
FF_PAD = 768
CHUNK = 64
GLA_HEADS = 4
GLA_DK = 512
GLA_DV = 1024
GLA_HDK = 128
GLA_HDV = 256
GATE_RANK = 16
GATE_TEMP = 16.0
POOL_WINDOWS = (2, 4, 8, 16)
POOL_W = 512
POOL_GD = 128
POOL_HALO = 16
XA_HEADS = 4
XA_HD = 128
XA_W = 512
EPS = 1e-6
IN_SPLITS = (GLA_DK, GLA_DK, GLA_DV, GLA_DV, GATE_RANK, POOL_W, XA_W, 3 * D_MODEL)
IN_WIDTH = sum(IN_SPLITS)
IN_SHARD = IN_WIDTH // N_DEV
INT_W = 3072 + 3072 + 1024 + 128
INT_NBLK = 3
INT_BLK = INT_W // INT_NBLK
FLOW_W = 128
INT_MAIN = INT_W - FLOW_W
_POOL_COL = (3072 + 3072) // POOL_W
_XQ_COL = (3072 + 3072 + POOL_W) // XA_W

ADAM_LR = 0.001
ADAM_B1 = 0.9
ADAM_B2 = 0.999
ADAM_EPS = 1e-08
ADAM_WD = 0.01
ADAM_STEP = 10

VMEM_LIMIT = 56 * 1024 * 1024

_NN = (((1,), (0,)), ((), ()))
_NT = (((1,), (1,)), ((), ()))
_TN = (((0,), (0,)), ((), ()))


def _pcall(body, *, name, grid, in_specs, out_specs, out_shape, scratch=(), aliases=None):
    return pl.pallas_call(
        body, name=name, grid=grid, in_specs=in_specs, out_specs=out_specs, out_shape=out_shape,
        scratch_shapes=list(scratch), input_output_aliases=aliases or {},
        compiler_params=pltpu.CompilerParams(dimension_semantics=("arbitrary",) * len(grid),
                                             vmem_limit_bytes=VMEM_LIMIT))


def _dot(a, b, dims=_NN):
    return lax.dot_general(a.astype(BF16), b.astype(BF16), dims, preferred_element_type=F32)


def _mm(name, a, b, *, grid, a_blk, a_map, b_blk, b_map, o_shape, o_blk, o_map, dims, out_dtype=F32, deps=()):
    nk = grid[2]

    def body(a_ref, b_ref, *rest):
        o_ref, scr = rest[len(deps)], rest[len(deps) + 1:]
        p = _dot(a_ref[...], b_ref[...], dims)
        if nk == 1:
            o_ref[...] = p.astype(o_ref.dtype)
        else:
            acc = scr[0]
            k = pl.program_id(2)

            @pl.when(k == 0)
            def _():
                acc[...] = p

            @pl.when(k > 0)
            def _():
                acc[...] += p

            @pl.when(k == nk - 1)
            def _():
                o_ref[...] = acc[...].astype(o_ref.dtype)

    acc_shape = tuple(d for d in o_blk if d is not None)
    return _pcall(body, name=name, grid=grid,
                  in_specs=[pl.BlockSpec(a_blk, a_map), pl.BlockSpec(b_blk, b_map)]
                  + [pl.BlockSpec(memory_space=pl.ANY)] * len(deps),
                  out_specs=pl.BlockSpec(o_blk, o_map),
                  out_shape=jax.ShapeDtypeStruct(o_shape, out_dtype),
                  scratch=[pltpu.VMEM(acc_shape, F32)] if nk > 1 else [])(a, b, *deps)


def _mm_nn(name, a, b, out_dtype=F32, tm=1024, tn=None):
    M, K = a.shape
    N = b.shape[1]
    tm, tn = min(tm, M), (tn or N)
    return _mm(name, a, b, grid=(N // tn, M // tm, 1), a_blk=(tm, K), a_map=lambda j, i, k: (i, 0),
               b_blk=(K, tn), b_map=lambda j, i, k: (0, j), o_shape=(M, N), o_blk=(tm, tn),
               o_map=lambda j, i, k: (i, j), dims=_NN, out_dtype=out_dtype)


def _mm_nt(name, a, b, out_dtype=F32, tm=1024):
    M, K = a.shape
    N = b.shape[0]
    tm = min(tm, M)
    return _mm(name, a, b, grid=(1, M // tm, 1), a_blk=(tm, K), a_map=lambda j, i, k: (i, 0),
               b_blk=(N, K), b_map=lambda j, i, k: (0, 0), o_shape=(M, N), o_blk=(tm, N),
               o_map=lambda j, i, k: (i, 0), dims=_NT, out_dtype=out_dtype)


def _mm_tn(name, a, b, out_dtype=BF16, ts=512, tn=None, tn_a=None, deps=()):
    S, M = a.shape
    N = b.shape[1]
    ts, tn, tn_a = min(ts, S), (tn or N), (tn_a or M)
    return _mm(name, a, b, grid=((N // tn) * (M // tn_a), 1, S // ts), a_blk=(ts, tn_a),
               a_map=lambda j, i, k: (k, j if tn_a < M else 0), b_blk=(ts, tn),
               b_map=lambda j, i, k: (k, j if tn < N else 0), o_shape=(M, N), o_blk=(tn_a, tn),
               o_map=lambda j, i, k: (j, 0) if tn_a < M else (0, j), dims=_TN, out_dtype=out_dtype, deps=deps)


class _Win:
    def __init__(self, arr, w, c):
        self.arr, self.w, self.c = arr, w, c


def _row_spec(x, tm):
    if isinstance(x, _Win):
        return x.arr, pl.BlockSpec((tm, x.w), functools.partial(lambda i, c: (i, c), c=x.c))
    if x.ndim == 3:
        return x, pl.BlockSpec((x.shape[0], tm, x.shape[2]), lambda i: (0, i, 0))
    return x, pl.BlockSpec((tm, x.shape[1]), lambda i: (i, 0))


def _rowwise(name, fn, rows, consts, outs, accs=(), tm=512):
    first = rows[0].arr if isinstance(rows[0], _Win) else rows[0]
    S = first.shape[1] if first.ndim == 3 else first.shape[0]
    tm = min(tm, S)
    n_in, n_out = len(rows) + len(consts), len(outs)
    arrays, in_specs = [], []
    for r in rows:
        arr, spec = _row_spec(r, tm)
        arrays.append(arr)
        in_specs.append(spec)
    for c in consts:
        arrays.append(c)
        in_specs.append(pl.BlockSpec(c.shape, functools.partial(lambda i, n: (0,) * n, n=c.ndim)))
    aliases = {}
    for k, o in enumerate(outs):
        if isinstance(o, _Win):
            aliases[len(arrays)] = k
            arrays.append(o.arr)
            in_specs.append(pl.BlockSpec(memory_space=pl.ANY))
    n_thru = len(aliases)
    out_specs = [_row_spec(o, tm)[1] for o in outs]
    out_specs += [pl.BlockSpec(a.shape, functools.partial(lambda i, n: (0,) * n, n=len(a.shape))) for a in accs]
    out_shape = [_sds(o.arr.shape, o.arr.dtype) if isinstance(o, _Win) else _sds(o.shape, o.dtype) for o in outs]

    def body(*refs):
        res = fn(*[r[...] for r in refs[:n_in]])
        if not isinstance(res, (tuple, list)):
            res = (res,)
        orefs = refs[n_in + n_thru:]
        for r, v in zip(orefs[:n_out], res[:n_out]):
            r[...] = v.astype(r.dtype)
        i = pl.program_id(0)
        for r, v in zip(orefs[n_out:], res[n_out:]):
            @pl.when(i == 0)
            def _(r=r, v=v):
                r[...] = v.astype(r.dtype)

            @pl.when(i > 0)
            def _(r=r, v=v):
                r[...] += v.astype(r.dtype)

    return _pcall(body, name=name, grid=(S // tm,), in_specs=in_specs, out_specs=out_specs,
                  out_shape=out_shape + [_sds(a.shape, a.dtype) for a in accs], aliases=aliases)(*arrays)


def _sds(shape, dtype=F32):
    return jax.ShapeDtypeStruct(shape, dtype)


def _rms(x, g):
    return x * lax.rsqrt(jnp.mean(x * x, axis=-1, keepdims=True) + EPS) * g


def _rms_bwd(x, g, dy):
    _, vjp = jax.vjp(_rms, x, g)
    return vjp(dy)


def _sigmoid(x):
    return 0.5 * jnp.tanh(0.5 * x) + 0.5


def _silu(x):
    return x * _sigmoid(x)


def _log_sigmoid(f):
    return jnp.minimum(f, 0.0) - jnp.log(1.0 + jnp.exp(-jnp.abs(f)))


def _head_rms_gate(o, g_out, gn):
    parts = [_rms(o[:, h * GLA_HDV:(h + 1) * GLA_HDV], gn[:, h * GLA_HDV:(h + 1) * GLA_HDV]) for h in range(GLA_HEADS)]
    return jnp.concatenate(parts, axis=-1) * _silu(g_out.astype(F32))


def _merge(gates, ya, yb, yc):
    gates, ya, yb, yc = (t.astype(F32) for t in (gates, ya, yb, yc))
    return (_sigmoid(gates[:, :D_MODEL]) * ya + _sigmoid(gates[:, D_MODEL:2 * D_MODEL]) * yb
            + _sigmoid(gates[:, 2 * D_MODEL:]) * yc)


def _tri_dot(t, x):
    hi = x.astype(BF16)
    r1 = x - hi.astype(F32)
    mid = r1.astype(BF16)
    lo = (r1 - mid.astype(F32)).astype(BF16)
    d = functools.partial(lax.dot_general, dimension_numbers=_NN, preferred_element_type=F32)
    return d(t, hi) + d(t, mid) + d(t, lo)


def _pre_norm(name, x, g, after):
    S = x.shape[0]
    tm = min(1024, S)

    def body(x_ref, g_ref, *rest):
        rest[-1][...] = _rms(x_ref[...], g_ref[...]).astype(BF16)

    row = pl.BlockSpec((tm, D_MODEL), lambda i: (i, 0))
    return _pcall(body, name=name, grid=(S // tm,),
                  in_specs=[row, pl.BlockSpec((1, D_MODEL), lambda i: (0, 0))] + [pl.BlockSpec(memory_space=pl.ANY)] * len(after),
                  out_specs=row, out_shape=_sds(x.shape, BF16))(x, g, *after)


def _ffn_fwd(tag, x, h, wt, gpost, get_wo, g_next, tgt=None):
    S = x.shape[0]
    tm = min(512, S)

    def up_body(h_ref, w_ref, u_ref, act_ref):
        hh = h_ref[...]
        for j in range(N_FF_BLK):
            ab = _dot(hh, w_ref[2 * FF_PAD * j:2 * FF_PAD * (j + 1), :], _NT)
            u_ref[:, 2 * FF_PAD * j:2 * FF_PAD * (j + 1)] = ab.astype(BF16)
            act_ref[:, FF_PAD * j:FF_PAD * (j + 1)] = (_silu(ab[:, :FF_PAD]) * ab[:, FF_PAD:]).astype(BF16)

    def down_head_body(act_ref, w_ref, x_ref, g_ref, gn_ref, t_ref, f_ref, dxo_ref, loss_ref, dgn_ref):
        i = pl.program_id(0)
        f = _dot(act_ref[...], w_ref[...])
        f_ref[...] = f.astype(BF16)
        xo = x_ref[...] + 0.5 * _rms(f, g_ref[...])
        out, vjp = jax.vjp(_rms, xo, gn_ref[...])
        e = out - t_ref[...]
        loss = 0.5 * jnp.sum(jnp.mean(e * e, axis=-1, keepdims=True), axis=0, keepdims=True)
        dxo_ref[...], dgn = vjp(e * (1.0 / D_MODEL))

        @pl.when(i == 0)
        def _():
            loss_ref[...] = jnp.broadcast_to(loss, loss_ref.shape)
            dgn_ref[...] = dgn

        @pl.when(i > 0)
        def _():
            loss_ref[...] += jnp.broadcast_to(loss, loss_ref.shape)
            dgn_ref[...] += dgn

    if tgt is not None:
        def last_body(h_ref, wt_ref, wo_ref, x_ref, g_ref, gn_ref, t_ref, u_ref, act_ref, f_ref, dxo_ref, loss_ref,
                      dgn_ref):
            up_body(h_ref, wt_ref, u_ref, act_ref)
            down_head_body(act_ref, wo_ref, x_ref, g_ref, gn_ref, t_ref, f_ref, dxo_ref, loss_ref, dgn_ref)

        wo = get_wo(None)
        tl = min(256, S)
        row = pl.BlockSpec((tl, D_MODEL), lambda i: (i, 0))
        vec = pl.BlockSpec((1, D_MODEL), lambda i: (0, 0))
        u, act, f, dxo, loss, dg_next = _pcall(
            last_body, name=tag + "_fwd", grid=(S // tl,),
            in_specs=[row, pl.BlockSpec(wt.shape, lambda i: (0, 0), pipeline_mode=pl.Buffered(1)),
                      pl.BlockSpec(wo.shape, lambda i: (0, 0), pipeline_mode=pl.Buffered(1)), row, vec, vec, row],
            out_specs=[pl.BlockSpec((tl, N_DEV * FF_PAD), lambda i: (i, 0)),
                       pl.BlockSpec((tl, N_FF_BLK * FF_PAD), lambda i: (i, 0)), row, row,
                       pl.BlockSpec((1, 128), lambda i: (0, 0)), vec],
            out_shape=[_sds((S, N_DEV * FF_PAD), BF16), _sds((S, N_FF_BLK * FF_PAD), BF16), _sds((S, D_MODEL), BF16),
                       _sds((S, D_MODEL)), _sds((1, 128)), _sds((1, D_MODEL))],
        )(h, wt, wo, x, gpost, g_next, tgt)
        return (loss, dxo, dg_next), wo, (h, u, act, f)

    u, act = _pcall(
        up_body, name=tag + "_up", grid=(S // tm,),
        in_specs=[pl.BlockSpec((tm, D_MODEL), lambda i: (i, 0)),
                  pl.BlockSpec(wt.shape, lambda i: (0, 0), pipeline_mode=pl.Buffered(1))],
        out_specs=[pl.BlockSpec((tm, N_DEV * FF_PAD), lambda i: (i, 0)),
                   pl.BlockSpec((tm, N_FF_BLK * FF_PAD), lambda i: (i, 0))],
        out_shape=[_sds((S, N_DEV * FF_PAD), BF16), _sds((S, N_FF_BLK * FF_PAD), BF16)],
    )(h, wt)

    def down_body(act_ref, w_ref, x_ref, g_ref, gn_ref, f_ref, xo_ref, hn_ref):
        f = _dot(act_ref[...], w_ref[...])
        f_ref[...] = f.astype(BF16)
        xo = x_ref[...] + 0.5 * _rms(f, g_ref[...])
        xo_ref[...] = xo
        hn_ref[...] = _rms(xo, gn_ref[...]).astype(BF16)

    wo = get_wo(act)
    row = pl.BlockSpec((tm, D_MODEL), lambda i: (i, 0))
    vec = pl.BlockSpec((1, D_MODEL), lambda i: (0, 0))
    f, xo, h_next = _pcall(
        down_body, name=tag + "_down", grid=(S // tm,),
        in_specs=[pl.BlockSpec((tm, N_FF_BLK * FF_PAD), lambda i: (i, 0)),
                  pl.BlockSpec(wo.shape, lambda i: (0, 0), pipeline_mode=pl.Buffered(1)), row, vec, vec],
        out_specs=[row, row, row], out_shape=[_sds((S, D_MODEL), BF16), _sds((S, D_MODEL)), _sds((S, D_MODEL), BF16)],
    )(act, wo, x, gpost, g_next)
    return (xo, h_next), wo, (h, u, act, f)


def _ffn_bwd(tag, dxo, x, gpre, wt, wo, gpost, saved, on_dwo=None, on_dwt=None):
    h, u, act, f = saved
    S = x.shape[0]
    tm = min(512, S)

    def dact_body(f_ref, dxo_ref, g_ref, w_ref, u_ref, df_ref, du_ref, dg_ref):
        i = pl.program_id(0)
        df, dg = _rms_bwd(f_ref[...].astype(F32), g_ref[...], 0.5 * dxo_ref[...])
        df = df.astype(BF16)
        df_ref[...] = df

        @pl.when(i == 0)
        def _():
            dg_ref[...] = dg

        @pl.when(i > 0)
        def _():
            dg_ref[...] += dg

        for j in range(N_FF_BLK):
            dact = _dot(df, w_ref[FF_PAD * j:FF_PAD * (j + 1), :], _NT)
            ab = u_ref[:, 2 * FF_PAD * j:2 * FF_PAD * (j + 1)].astype(F32)
            a, b = ab[:, :FF_PAD], ab[:, FF_PAD:]
            sg = _sigmoid(a)
            du_ref[:, 2 * FF_PAD * j:2 * FF_PAD * j + FF_PAD] = (dact * b * (sg * (1.0 + a * (1.0 - sg)))).astype(BF16)
            du_ref[:, 2 * FF_PAD * j + FF_PAD:2 * FF_PAD * (j + 1)] = (dact * (a * sg)).astype(BF16)

    row = pl.BlockSpec((tm, D_MODEL), lambda i: (i, 0))
    vec = pl.BlockSpec((1, D_MODEL), lambda i: (0, 0))
    u_spec = pl.BlockSpec((tm, N_DEV * FF_PAD), lambda i: (i, 0))
    df, du, dgpost = _pcall(
        dact_body, name=tag + "_dact", grid=(S // tm,),
        in_specs=[row, row, vec, pl.BlockSpec(wo.shape, lambda i: (0, 0), pipeline_mode=pl.Buffered(1)), u_spec],
        out_specs=[row, u_spec, vec],
        out_shape=[_sds((S, D_MODEL), BF16), _sds((S, N_DEV * FF_PAD), BF16), _sds((1, D_MODEL))])(f, dxo, gpost, wo, u)
    dwo = _mm_tn(tag + "_dwo", act, df, ts=S, tn_a=FF_PAD)
    dwt = _mm_tn(tag + "_dwt", du, h, ts=S, tn_a=FF_PAD, deps=[on_dwo(dwo)] if on_dwo is not None else [])
    dx, dgpre = _dh_pre_norm_bwd(tag + "_dh", [du], wt, x, dxo, gpre, [on_dwt(dwt)] if on_dwt is not None else [])
    return dx, dgpre, dgpost


def _dh_pre_norm_bwd(name, dzs, wt, x, dres, g, deps):
    S = x.shape[0]
    ks = [dz.shape[1] for dz in dzs]

    def body(*refs):
        dz_refs = refs[:len(dzs)]
        w_ref, x_ref, dres_ref, g_ref = refs[len(dzs):len(dzs) + 4]
        dx_ref, dg_ref = refs[len(dzs) + 4 + len(deps):]
        i = pl.program_id(0)
        dh, o = None, 0
        for dz_ref, k in zip(dz_refs, ks):
            p = _dot(dz_ref[...], w_ref[o:o + k, :])
            dh, o = (p if dh is None else dh + p), o + k
        dx, dg = _rms_bwd(x_ref[...], g_ref[...], dh)
        dx_ref[...] = dx + dres_ref[...]

        @pl.when(i == 0)
        def _():
            dg_ref[...] = dg

        @pl.when(i > 0)
        def _():
            dg_ref[...] += dg

    th = min(512, S)
    row = pl.BlockSpec((th, D_MODEL), lambda i: (i, 0))
    vec = pl.BlockSpec((1, D_MODEL), lambda i: (0, 0))
    return _pcall(
        body, name=name, grid=(S // th,),
        in_specs=[pl.BlockSpec((th, k), lambda i: (i, 0)) for k in ks]
        + [pl.BlockSpec(wt.shape, lambda i: (0, 0), pipeline_mode=pl.Buffered(1)), row, row, vec]
        + [pl.BlockSpec(memory_space=pl.ANY)] * len(deps),
        out_specs=[row, vec], out_shape=[_sds((S, D_MODEL)), _sds((1, D_MODEL))])(*dzs, wt, x, dres, g, *deps)


GLA_G = 8


def _gla_tile_common(k, flow, wfu, bf):
    f = _dot(flow, wfu) + bf
    la = _log_sigmoid(f) * (1.0 / GATE_TEMP)
    tri = _tri_matrix(True)
    ws, ds = [], []
    for g in range(k.shape[0] // CHUNK):
        b = _tri_dot(tri, la[g * CHUNK:(g + 1) * CHUNK])
        b_end = b[CHUNK - 1:CHUNK, :]
        ws.append(jnp.exp(b_end - b))
        ds.append(jnp.exp(b_end))
    w = jnp.concatenate(ws, axis=0)
    return f, w, k * w, ds


def _tri_matrix(lower):
    r = lax.broadcasted_iota(jnp.int32, (CHUNK, CHUNK), 0)
    c = lax.broadcasted_iota(jnp.int32, (CHUNK, CHUNK), 1)
    return jnp.where((r >= c) if lower else (r <= c), 1.0, 0.0).astype(BF16)


def _heads():
    return [(slice(h * GLA_HDK, (h + 1) * GLA_HDK), slice(h * GLA_HDV, (h + 1) * GLA_HDV)) for h in range(GLA_HEADS)]


def _gla_fwd(proj, wfu, bf, gn, deps=()):
    S = proj.shape[0]
    G = min(GLA_G, S // CHUNK)
    T = G * CHUNK
    nc = S // CHUNK

    def body(q_ref, k_ref, v_ref, go_ref, flow_ref, wfu_ref, bf_ref, gn_ref, *rest):
        ya_ref, st_ref, state = rest[len(deps):]
        @pl.when(pl.program_id(0) == 0)
        def _():
            state[...] = jnp.zeros_like(state)

        _, _, kt, ds = _gla_tile_common(k_ref[...], flow_ref[...], wfu_ref[...], bf_ref[...])
        q = q_ref[...].astype(F32) * (GLA_HDK ** -0.5)
        v = v_ref[...]
        rows = [slice(g * CHUNK, (g + 1) * CHUNK) for g in range(G)]
        kv = [[_dot(v[r, vs], kt[r, ks], _TN) for ks, vs in _heads()] for r in rows]
        st = [state[vs, :] for _, vs in _heads()]
        o = []
        for g, r in enumerate(rows):
            outs = []
            for h, (ks, vs) in enumerate(_heads()):
                st[h] = st[h] * ds[g][:, ks] + kv[g][h]
                st_ref[g, vs, :] = st[h]
                outs.append(_dot(q[r, ks], st[h], _NT))
            o.append(jnp.concatenate(outs, axis=-1))
        for h, (_, vs) in enumerate(_heads()):
            state[vs, :] = st[h]
        ya_ref[...] = _head_rms_gate(jnp.concatenate(o, axis=0), go_ref[...], gn_ref[...]).astype(BF16)

    return _pcall(
        body, name="gla_fwd", grid=(S // T,),
        in_specs=[pl.BlockSpec((T, GLA_DK), lambda c: (c, 0)), pl.BlockSpec((T, GLA_DK), lambda c: (c, 1)),
                  pl.BlockSpec((T, GLA_DV), lambda c: (c, 1)), pl.BlockSpec((T, GLA_DV), lambda c: (c, 2)),
                  pl.BlockSpec((T, FLOW_W), lambda c: (c, (INT_W - FLOW_W) // FLOW_W)),
                  pl.BlockSpec(wfu.shape, lambda c: (0, 0)), pl.BlockSpec(bf.shape, lambda c: (0, 0)),
                  pl.BlockSpec(gn.shape, lambda c: (0, 0))]
        + [pl.BlockSpec(memory_space=pl.ANY)] * len(deps),
        out_specs=[pl.BlockSpec((T, GLA_DV), lambda c: (c, 0)),
                   pl.BlockSpec((G, GLA_DV, GLA_HDK), lambda c: (c, 0, 0))],
        out_shape=[_sds((S, GLA_DV), BF16), _sds((nc, GLA_DV, GLA_HDK))],
        scratch=[pltpu.VMEM((GLA_DV, GLA_HDK), F32)])(proj, proj, proj, proj, proj, wfu, bf, gn, *deps)


def _gla_bwd(proj, wfu, bf, gn, states, d_ya_in, dmain):
    S = proj.shape[0]
    G = min(GLA_G, S // CHUNK)
    T = G * CHUNK
    nt = S // T

    def body(q_ref, k_ref, v_ref, go_ref, flow_ref, wfu_ref, bf_ref, gn_ref, st_ref, stp_ref, dya_ref, dmain_in,
             dqkvg_ref, dflow_ref, dwfu_ref, dbf_ref, dgn_ref, dstate):
        dq_ref = dqkvg_ref.at[:, 0:GLA_DK]
        dk_ref = dqkvg_ref.at[:, GLA_DK:2 * GLA_DK]
        dv_ref = dqkvg_ref.at[:, 2 * GLA_DK:2 * GLA_DK + GLA_DV]
        dgo_ref = dqkvg_ref.at[:, 2 * GLA_DK + GLA_DV:]
        step = pl.program_id(0)

        @pl.when(step == 0)
        def _():
            dstate[...] = jnp.zeros_like(dstate)
            dwfu_ref[...] = jnp.zeros_like(dwfu_ref)
            dbf_ref[...] = jnp.zeros_like(dbf_ref)
            dgn_ref[...] = jnp.zeros_like(dgn_ref)

        flow, wfu_v = flow_ref[...], wfu_ref[...]
        f, w, kt, ds = _gla_tile_common(k_ref[...], flow, wfu_v, bf_ref[...])
        q = q_ref[...].astype(F32) * (GLA_HDK ** -0.5)
        v = v_ref[...]
        rows = [slice(g * CHUNK, (g + 1) * CHUNK) for g in range(G)]
        o = jnp.concatenate([jnp.concatenate([_dot(q[r, ks], st_ref[g, vs, :], _NT) for ks, vs in _heads()], axis=-1)
                             for g, r in enumerate(rows)], axis=0)
        _, vjp = jax.vjp(_head_rms_gate, o, go_ref[...], gn_ref[...])
        dout, dgo, dgn = vjp(dya_ref[...].astype(F32))
        dgo_ref[...] = dgo.astype(dgo_ref.dtype)
        dgn_ref[...] += dgn
        dq = [jnp.concatenate([_dot(dout[r, vs], st_ref[g, vs, :]) for _, vs in _heads()], axis=-1)
              for g, r in enumerate(rows)]
        qdo = [[_dot(dout[r, vs], q[r, ks], _TN) for ks, vs in _heads()] for r in rows]
        dq_ref[...] = (jnp.concatenate(dq, axis=0) * (GLA_HDK ** -0.5)).astype(dq_ref.dtype)
        has_prev = (step < nt - 1).astype(F32)
        carry = [dstate[vs, :] for _, vs in _heads()]
        dkt, dv, dd = [None] * G, [None] * G, [None] * G
        for g in reversed(range(G)):
            r = rows[g]
            dkts, dvs, dds = [], [], []
            for h, (ks, vs) in enumerate(_heads()):
                dst = carry[h] + qdo[g][h]
                dkts.append(_dot(v[r, vs], dst))
                dvs.append(_dot(kt[r, ks], dst, _NT))
                st_prev = st_ref[g - 1, vs, :] if g > 0 else stp_ref[vs, :] * has_prev
                dds.append(jnp.sum(dst * st_prev, axis=0, keepdims=True))
                carry[h] = dst * ds[g][:, ks]
            dkt[g], dv[g], dd[g] = (jnp.concatenate(t, axis=-1) for t in (dkts, dvs, dds))
        for h, (_, vs) in enumerate(_heads()):
            dstate[vs, :] = carry[h]
        dkt = jnp.concatenate(dkt, axis=0)
        dv_ref[...] = jnp.concatenate(dv, axis=0).astype(dv_ref.dtype)
        dk_ref[...] = (dkt * w).astype(dk_ref.dtype)
        de = dkt * kt
        tri = _tri_matrix(False)
        dla = []
        for g, r in enumerate(rows):
            db_end = jnp.sum(de[r], axis=0, keepdims=True) + dd[g] * ds[g]
            dla.append(db_end - _tri_dot(tri, de[r]))
        df = jnp.concatenate(dla, axis=0) * (1.0 - _sigmoid(f)) * (1.0 / GATE_TEMP)
        dflow_ref[...] = _dot(df, wfu_v, _NT).astype(dflow_ref.dtype)
        dwfu_ref[...] += _dot(flow, df, _TN)
        dbf_ref[...] += jnp.sum(df, axis=0, keepdims=True)

    rt = lambda s: nt - 1 - s
    return _pcall(
        body, name="gla_bwd", grid=(nt,),
        in_specs=[pl.BlockSpec((T, GLA_DK), lambda s: (rt(s), 0)), pl.BlockSpec((T, GLA_DK), lambda s: (rt(s), 1)),
                  pl.BlockSpec((T, GLA_DV), lambda s: (rt(s), 1)), pl.BlockSpec((T, GLA_DV), lambda s: (rt(s), 2)),
                  pl.BlockSpec((T, FLOW_W), lambda s: (rt(s), (INT_W - FLOW_W) // FLOW_W)),
                  pl.BlockSpec(wfu.shape, lambda s: (0, 0)), pl.BlockSpec(bf.shape, lambda s: (0, 0)),
                  pl.BlockSpec(gn.shape, lambda s: (0, 0)),
                  pl.BlockSpec((G, GLA_DV, GLA_HDK), lambda s: (rt(s), 0, 0)),
                  pl.BlockSpec((None, GLA_DV, GLA_HDK), lambda s: (jnp.maximum(rt(s) * G - 1, 0), 0, 0)),
                  pl.BlockSpec((T, GLA_DV), lambda s: (rt(s), 0)), pl.BlockSpec(memory_space=pl.ANY)],
        out_specs=[pl.BlockSpec((T, 2 * GLA_DK + 2 * GLA_DV), lambda s: (rt(s), 0)),
                   pl.BlockSpec((T, FLOW_W), lambda s: (rt(s), 0)),
                   pl.BlockSpec(wfu.shape, lambda s: (0, 0)), pl.BlockSpec(bf.shape, lambda s: (0, 0)),
                   pl.BlockSpec(gn.shape, lambda s: (0, 0))],
        out_shape=[_sds(dmain.shape, dmain.dtype), _sds((S, FLOW_W), BF16), _sds(wfu.shape), _sds(bf.shape),
                   _sds(gn.shape)],
        scratch=[pltpu.VMEM((GLA_DV, GLA_HDK), F32)], aliases={11: 0},
    )(proj, proj, proj, proj, proj, wfu, bf, gn, states, states, d_ya_in, dmain)


def _pool_counts(tm, i):
    t = (lax.broadcasted_iota(jnp.int32, (tm, POOL_GD), 0) + i * tm + 1).astype(F32)
    return [jnp.minimum(t, float(w)) for w in POOL_WINDOWS]


def _pool_fwd(proj, w_pool, pool_scale, tm=1024):
    S = proj.shape[0]
    tm = min(tm, S // 2)
    col = _POOL_COL
    hb = tm // POOL_HALO

    def body(p_ref, halo_ref, wp_ref, sc_ref, mixed_ref, out_ref):
        i = pl.program_id(0)
        p = p_ref[...].astype(F32)
        halo = halo_ref[...].astype(F32) * (i > 0).astype(F32)
        ext = jnp.concatenate([halo, p], axis=0)
        n = tm + POOL_HALO
        sums, acc, k = {}, ext, 1
        while k < POOL_WINDOWS[-1]:
            acc = acc + pltpu.roll(acc, k, axis=0)
            k *= 2
            sums[k] = acc
        cnts = _pool_counts(tm, i)
        mixed, lin = [], []
        for g, w in enumerate(POOL_WINDOWS):
            ls = slice(g * POOL_GD, (g + 1) * POOL_GD)
            m = sums[w][POOL_HALO:n, ls] / cnts[g] - p[:, ls]
            mixed.append(m)
            lin.append(_dot(m, wp_ref[g]))
        mixed_ref[...] = jnp.concatenate(mixed, axis=-1)
        out_ref[...] = (jnp.concatenate(lin, axis=-1) * sc_ref[...]).astype(out_ref.dtype)

    return _pcall(
        body, name="pool_fwd", grid=(S // tm,),
        in_specs=[pl.BlockSpec((tm, POOL_W), lambda i: (i, col)),
                  pl.BlockSpec((POOL_HALO, POOL_W), lambda i: (jnp.maximum(i * hb - 1, 0), col)),
                  pl.BlockSpec(w_pool.shape, lambda i: (0, 0, 0)), pl.BlockSpec(pool_scale.shape, lambda i: (0, 0))],
        out_specs=[pl.BlockSpec((tm, POOL_W), lambda i: (i, 0)), pl.BlockSpec((tm, POOL_W), lambda i: (i, 0))],
        out_shape=[_sds((S, POOL_W)), _sds((S, POOL_W), BF16)])(proj, proj, w_pool, pool_scale)


def _pool_lin_bwd(dout, mixed, w_pool, pool_scale):
    S = dout.shape[0]

    def fn(dout, mixed, wp, sc):
        dlin = dout * sc
        dm, dwp, lin = [], [], []
        for g in range(len(POOL_WINDOWS)):
            ls = slice(g * POOL_GD, (g + 1) * POOL_GD)
            lin.append(_dot(mixed[:, ls], wp[g]))
            dm.append(_dot(dlin[:, ls], wp[g], _NT))
            dwp.append(_dot(mixed[:, ls], dlin[:, ls], _TN))
        dsc = jnp.sum(dout * jnp.concatenate(lin, axis=-1), axis=0, keepdims=True)
        return jnp.concatenate(dm, axis=-1), jnp.concatenate(dwp, axis=0), dsc

    return _rowwise("pool_lin_bwd", fn, [dout, mixed], [w_pool, pool_scale], [_sds((S, POOL_W))],
                    [_sds((len(POOL_WINDOWS) * POOL_GD, POOL_GD)), _sds((1, POOL_W))], tm=1024)


def _pool_win_bwd(dmixed, dmain, tm=1024):
    S = dmixed.shape[0]
    tm = min(tm, S // 2)
    nt = S // tm
    hb = tm // POOL_HALO

    def body(dm_ref, halo_ref, dmain_in, dp_ref):
        i = pl.program_id(0)
        dm = dm_ref[...]
        halo = halo_ref[...] * (i < nt - 1).astype(F32)
        cnts = _pool_counts(tm, i)
        cnts_h = [c[:POOL_HALO] for c in _pool_counts(tm, i + 1)]
        r = jnp.concatenate([jnp.concatenate([dm[:, g * POOL_GD:(g + 1) * POOL_GD] / cnts[g] for g in range(4)], axis=-1),
                             jnp.concatenate([halo[:, g * POOL_GD:(g + 1) * POOL_GD] / cnts_h[g] for g in range(4)], axis=-1)],
                            axis=0)
        n = tm + POOL_HALO
        sums, acc, k = {}, r, 1
        while k < POOL_WINDOWS[-1]:
            acc = acc + pltpu.roll(acc, n - k, axis=0)
            k *= 2
            sums[k] = acc
        dp = [sums[w][:tm, g * POOL_GD:(g + 1) * POOL_GD] for g, w in enumerate(POOL_WINDOWS)]
        dp_ref[...] = (jnp.concatenate(dp, axis=-1) - dm).astype(dp_ref.dtype)

    return _pcall(
        body, name="pool_win_bwd", grid=(nt,),
        in_specs=[pl.BlockSpec((tm, POOL_W), lambda i: (i, 0)),
                  pl.BlockSpec((POOL_HALO, POOL_W), lambda i: (jnp.minimum((i + 1) * hb, S // POOL_HALO - 1), 0)),
                  pl.BlockSpec(memory_space=pl.ANY)],
        out_specs=pl.BlockSpec((tm, POOL_W), lambda i: (i, _POOL_COL)),
        out_shape=_sds(dmain.shape, dmain.dtype), aliases={2: 0})(dmixed, dmixed, dmain)


def _xattn_probs(q, kv, h):
    hs = slice(h * XA_HD, (h + 1) * XA_HD)
    s = _dot(q[:, hs], kv[:, hs], _NT) * (XA_HD ** -0.5)
    s = s - jnp.max(s, axis=-1, keepdims=True)
    e = jnp.exp(s)
    return e / jnp.sum(e, axis=-1, keepdims=True)


def _xattn_fwd(proj, kv):
    S = proj.shape[0]

    def fn(q, kv):
        outs = []
        for h in range(XA_HEADS):
            p = _xattn_probs(q, kv, h)
            outs.append(_dot(p, kv[:, XA_W + h * XA_HD:XA_W + (h + 1) * XA_HD]))
        return jnp.concatenate(outs, axis=-1)

    return _rowwise("xattn_fwd", fn, [_Win(proj, XA_W, _XQ_COL)], [kv], [_sds((S, XA_W), BF16)], tm=1024)[0]


def _xattn_bwd(proj, kv, dxa, dmain):
    def fn(q, dxa, kv):
        dqs, dks, dvs = [], [], []
        for h in range(XA_HEADS):
            hs = slice(h * XA_HD, (h + 1) * XA_HD)
            vh = kv[:, XA_W + h * XA_HD:XA_W + (h + 1) * XA_HD]
            p = _xattn_probs(q, kv, h)
            dp = _dot(dxa[:, hs], vh, _NT)
            ds = p * (dp - jnp.sum(p * dp, axis=-1, keepdims=True)) * (XA_HD ** -0.5)
            dqs.append(_dot(ds, kv[:, hs]))
            dks.append(_dot(ds, q[:, hs], _TN))
            dvs.append(_dot(p, dxa[:, hs], _TN))
        return jnp.concatenate(dqs, axis=-1), jnp.concatenate(dks + dvs, axis=-1)

    return _rowwise("xattn_bwd", fn, [_Win(proj, XA_W, _XQ_COL), dxa], [kv], [_Win(dmain, XA_W, _XQ_COL)],
                    [_sds(kv.shape)], tm=1024)


def _mix_fwd(x1, h, mem, W, g_next, on_proj=None, late_weights=None):
    S = x1.shape[0]
    M = mem.shape[0]
    tm = min(512, S)
    def proj_body(h_ref, w_ref, o_ref):
        hh = h_ref[...]
        for j in range(INT_NBLK):
            o_ref[:, INT_BLK * j:INT_BLK * (j + 1)] = _dot(hh, w_ref[INT_BLK * j:INT_BLK * (j + 1), :], _NT).astype(BF16)

    proj = _pcall(
        proj_body, name="mix_proj", grid=(S // tm,),
        in_specs=[pl.BlockSpec((tm, D_MODEL), lambda i: (i, 0)),
                  pl.BlockSpec((INT_W, D_MODEL), lambda i: (0, 0), pipeline_mode=pl.Buffered(1))],
        out_specs=pl.BlockSpec((tm, INT_W), lambda i: (i, 0)), out_shape=_sds((S, INT_W), BF16))(h, W["w_int_t"])
    ya_in, states = _gla_fwd(proj, W["w_fu_pad"], W["b_f"], W["gla_norm_g"],
                             on_proj(proj) if on_proj is not None else [])
    mixed, pool_out = _pool_fwd(proj, W["w_pool"], W["pool_scale"])
    mem_n = _rowwise("mem_norm", lambda m, g: _rms(m, g), [mem], [W["mem_norm_g"]], [_sds((M, D_MODEL), BF16)])[0]
    if late_weights is not None:
        W = {**W, **late_weights([pool_out])}
    kv = _mm_nn("mem_kv", mem_n, W["w_mem_kv"])
    xa = _xattn_fwd(proj, kv)
    def out_fn(gates, ya_in, pool_out, xa, x1, wa, wb, wc, wo, g, gn):
        ya, yb, yc = (_dot(t, w).astype(BF16) for t, w in ((ya_in, wa), (pool_out, wb), (xa, wc)))
        merged = _merge(gates, ya, yb, yc).astype(BF16)
        y = _dot(merged, wo)
        x2 = x1 + _rms(y, g)
        return ya, yb, yc, merged, y, x2, _rms(x2, gn)

    ya, yb, yc, merged, y, x2, h_next = _rowwise(
        "mix_out", out_fn, [_Win(proj, 3 * D_MODEL, 1), ya_in, pool_out, xa, x1],
        [W["w_up_gla"], W["w_up_pool"], W["w_up_xattn"], W["w_o"], W["mix_post_g"], g_next],
        [_sds((S, D_MODEL), BF16)] * 5 + [_sds((S, D_MODEL)), _sds((S, D_MODEL), BF16)])
    return x2, h_next, W, (h, proj, states, ya_in, mixed, pool_out, mem_n, kv, xa, ya, yb, yc, merged, y)


def _mix_bwd(dx2, x1, mem, W, saved, on_grads=None, on_up_grads=None):
    h, proj, states, ya_in, mixed, pool_out, mem_n, kv, xa, ya, yb, yc, merged, y = saved
    S = x1.shape[0]

    def out_bwd(y, dx2, gates, ya, yb, yc, g, wo, wa, wb, wc):
        dy, dg = _rms_bwd(y.astype(F32), g, dx2)
        dy = dy.astype(BF16)
        _, vjp = jax.vjp(_merge, gates, ya, yb, yc)
        dgates, dya, dyb, dyc = vjp(_dot(dy, wo, _NT))
        return (dy, dgates, dya, dyb, dyc, _dot(dya, wa, _NT), _dot(dyb, wb, _NT), _dot(dyc, wc, _NT), dg)

    dmain = lax.empty((S, INT_MAIN), BF16)
    dy, dmain, dya, dyb, dyc, d_ya_in, d_pool_out, d_xa, d_mix_post_g = _rowwise(
        "mix_out_bwd", out_bwd, [y, dx2, _Win(proj, 3 * D_MODEL, 1), ya, yb, yc],
        [W["mix_post_g"], W["w_o"], W["w_up_gla"], W["w_up_pool"], W["w_up_xattn"]],
        [_sds((S, D_MODEL), BF16), _Win(dmain, 3 * D_MODEL, 1)] + [_sds((S, D_MODEL), BF16)] * 4
        + [_sds((S, POOL_W), BF16), _sds((S, XA_W), BF16)], [_sds((1, D_MODEL))])
    d_w_o = _mm_tn("d_w_o", merged, dy, ts=S, tn_a=256)
    d_w_up_gla = _mm_tn("d_w_up_gla", ya_in, dya, ts=S, tn_a=256)
    d_w_up_pool = _mm_tn("d_w_up_pool", pool_out, dyb, ts=S, tn_a=128)
    d_w_up_xattn = _mm_tn("d_w_up_xattn", xa, dyc, ts=S, tn_a=128)
    up_grads = dict(w_up_gla=d_w_up_gla, w_up_pool=d_w_up_pool, w_up_xattn=d_w_up_xattn, w_o=d_w_o)
    up_deps = [on_up_grads(up_grads)] if on_up_grads is not None else []

    dmain, dflow, d_wfu_pad, d_b_f, d_gla_norm_g = _gla_bwd(proj, W["w_fu_pad"], W["b_f"], W["gla_norm_g"], states,
                                                            d_ya_in, dmain)
    dmixed, d_w_pool, d_pool_scale = _pool_lin_bwd(d_pool_out, mixed, W["w_pool"], W["pool_scale"])
    dmain = _pool_win_bwd(dmixed, dmain)
    dmain, dkv = _xattn_bwd(proj, kv, d_xa, dmain)
    d_w_mem_kv = _mm_tn("d_w_mem_kv", mem_n, dkv)
    dmem_n = _mm_nt("d_mem_n", dkv, W["w_mem_kv"])
    d_mem_norm_g = _rowwise("mem_norm_bwd", lambda m, d, g: _rms_bwd(m, g, d)[1], [mem, dmem_n], [W["mem_norm_g"]], [],
                            [_sds((1, D_MODEL))])[0]
    d_w_int_t = (_mm_tn("d_w_int", dmain, h, ts=S, tn_a=INT_MAIN // 8, deps=up_deps),
                 _mm_tn("d_w_flow", dflow, h, ts=S))
    grads = dict(
        w_int_t=d_w_int_t, w_fu_pad=d_wfu_pad, b_f=d_b_f, gla_norm_g=d_gla_norm_g, w_pool=d_w_pool,
        pool_scale=d_pool_scale, mem_norm_g=d_mem_norm_g, w_mem_kv=d_w_mem_kv, w_up_gla=d_w_up_gla,
        w_up_pool=d_w_up_pool, w_up_xattn=d_w_up_xattn, w_o=d_w_o, mix_post_g=d_mix_post_g)
    deps = [on_grads(grads)] if on_grads is not None else []
    dx1, grads["mix_pre_g"] = _dh_pre_norm_bwd("d_mix_h", [dmain, dflow], W["w_int_t"], x1, dx2, W["mix_pre_g"], deps)
    return dx1, grads


def _mesh_pos():
    x, y, c = lax.axis_index("x"), lax.axis_index("y"), lax.axis_index("c")
    return x, y, c, 4 * x + 2 * y + c


def _peer(x, y, c, r):
    px = 1 - x if r & 4 else x
    py = 1 - y if r & 2 else y
    pc = 1 - c if r & 1 else c
    return (px, py, pc), 4 * px + 2 * py + pc


_ALL_PEERS = tuple(range(1, N_DEV))
_SIBLING = 1
_SAME_CORE = (2, 4, 6)


def _dev_slot(ref, dev):
    return ref.at[dev]


def _zero_pad_rows(land):
    pad = FF_PAD - FF_BLK

    def body(land_in, o_ref):
        o_ref[...] = jnp.zeros_like(o_ref)

    return _pcall(body, name="zero_pad_rows", grid=(land.shape[0],), in_specs=[pl.BlockSpec(memory_space=pl.ANY)],
                  out_specs=pl.BlockSpec((None, pad, D_MODEL), lambda j: (j, FF_BLK // pad, 0)),
                  out_shape=_sds(land.shape, land.dtype), aliases={0: 0})(land)


class _Plan:
    def __init__(self, scatter, slots=None, shapes=None):
        self.scatter, self.slots, self.shapes = scatter, slots or {}, shapes or {}

    def src(self, srcs, a, dev):
        return self.slots.get(a, _dev_slot)(srcs[a], dev) if self.scatter else srcs[a]

    def dst(self, lands, a, dev):
        return lands[a].at[dev] if self.scatter else self.slots.get(a, _dev_slot)(lands[a], dev)

    def landing_zones(self, arrays):
        lands = []
        for a, arr in enumerate(arrays):
            if self.scatter:
                lands.append(lax.empty((N_DEV,) + tuple(self.shapes.get(a, arr.shape[1:])), arr.dtype))
            elif a in self.shapes:
                lands.append(_zero_pad_rows(lax.empty(self.shapes[a], arr.dtype)))
            else:
                lands.append(lax.empty((N_DEV,) + arr.shape, arr.dtype))
        return lands


_GATHER = _Plan(False)


def _peer_copies(srcs, lands, send_sems, recv_sems, plan, peers=_ALL_PEERS):
    x, y, c, me = _mesh_pos()
    cps = []
    for r in peers:
        pos, peer = _peer(x, y, c, r)
        for a in range(len(srcs)):
            k = a * (N_DEV - 1) + r - 1
            cps.append(pltpu.make_async_remote_copy(
                src_ref=plan.src(srcs, a, peer), dst_ref=plan.dst(lands, a, me),
                send_sem=send_sems.at[k], recv_sem=recv_sems.at[k], device_id=pos,
                device_id_type=pl.DeviceIdType.MESH))
    return cps


_HBM = pl.BlockSpec(memory_space=pltpu.HBM)
_SEM = pl.BlockSpec(memory_space=pltpu.SEMAPHORE)
_EFFECT = pltpu.SideEffectType.DATAFLOW_SIDE_EFFECTING


def _own_copies(srcs, lands, own_sems, plan):
    me = _mesh_pos()[3]
    return [pltpu.make_async_copy(plan.src(srcs, a, me), plan.dst(lands, a, me), own_sems.at[a])
            for a in range(len(srcs))]


def _exchange_start(name, arrays, plan):
    n = len(arrays)
    lands = plan.landing_zones(arrays)
    n_sem = n * (N_DEV - 1)

    def body(*refs):
        srcs, lands_ = refs[:n], refs[n:2 * n]
        send_sems, recv_sems, own_sems = refs[2 * n:2 * n + 3]
        token = refs[-1]
        for cp in _peer_copies(srcs, lands_, send_sems, recv_sems, plan) + _own_copies(srcs, lands_, own_sems, plan):
            cp.start()
        token[...] = jnp.zeros_like(token)

    hbm = lambda a: pltpu.HBM(a.shape, a.dtype)
    res = pl.pallas_call(
        body, name=name,
        out_shape=(pltpu.SemaphoreType.DMA((n_sem,)), pltpu.SemaphoreType.DMA((n_sem,)), pltpu.SemaphoreType.DMA((n,)),
                   *[hbm(a) for a in arrays], *[hbm(a) for a in lands], _sds((8, 128))),
        in_specs=[_HBM] * (2 * n),
        out_specs=(_SEM, _SEM, _SEM, *[_HBM] * (2 * n), pl.BlockSpec(memory_space=pltpu.VMEM)),
        input_output_aliases={i: 3 + i for i in range(2 * n)},
        compiler_params=pltpu.CompilerParams(has_side_effects=_EFFECT),
    )(*[pltpu.with_memory_space_constraint(a, pltpu.HBM) for a in list(arrays) + lands])
    return (res[:3], res[3:3 + n], res[3 + n:3 + 2 * n], plan), res[-1]


def _exchange_wait(name, state, after):
    sems, srcs, lands, plan = state
    n = len(srcs)

    def body(*refs):
        srcs_, lands_ = refs[:n], refs[n:2 * n]
        send_sems, recv_sems, own_sems = refs[2 * n:2 * n + 3]
        for cp in _peer_copies(srcs_, lands_, send_sems, recv_sems, plan):
            cp.wait_send()
            cp.wait_recv()
        for cp in _own_copies(srcs_, lands_, own_sems, plan):
            cp.wait()

    hbm = lambda a: pltpu.HBM(a.shape, a.dtype)
    res = pl.pallas_call(
        body, name=name, out_shape=tuple(hbm(a) for a in list(srcs) + list(lands)),
        in_specs=[_HBM] * (2 * n) + [_SEM] * 3 + [pl.BlockSpec(memory_space=pl.ANY)] * len(after),
        out_specs=tuple([_HBM] * (2 * n)), input_output_aliases={i: i for i in range(2 * n)},
        compiler_params=pltpu.CompilerParams(has_side_effects=_EFFECT),
    )(*srcs, *lands, *sems, *after)
    return res[n:]


def _gather_start(name, arrays, plan, after):
    n = len(arrays)
    lands = plan.landing_zones(arrays)
    n_sem = n * (N_DEV - 1)

    def body(*refs):
        srcs, lands_ = refs[:n], refs[n:2 * n]
        send_sems, recv_sems, own_sems = refs[2 * n + len(after):2 * n + len(after) + 3]
        token = refs[-1]
        for cp in (_peer_copies(srcs, lands_, send_sems, recv_sems, plan, (_SIBLING,) + _SAME_CORE)
                   + _own_copies(srcs, lands_, own_sems, plan)):
            cp.start()
        token[...] = jnp.zeros_like(token)

    hbm = lambda a: pltpu.HBM(a.shape, a.dtype)
    res = pl.pallas_call(
        body, name=name,
        out_shape=(pltpu.SemaphoreType.DMA((n_sem,)), pltpu.SemaphoreType.DMA((n_sem,)), pltpu.SemaphoreType.DMA((n,)),
                   *[hbm(a) for a in arrays], *[hbm(a) for a in lands], _sds((8, 128))),
        in_specs=[_HBM] * (2 * n) + [pl.BlockSpec(memory_space=pl.ANY)] * len(after),
        out_specs=(_SEM, _SEM, _SEM, *[_HBM] * (2 * n), pl.BlockSpec(memory_space=pltpu.VMEM)),
        input_output_aliases={i: 3 + i for i in range(2 * n)},
        compiler_params=pltpu.CompilerParams(has_side_effects=_EFFECT),
    )(*[pltpu.with_memory_space_constraint(a, pltpu.HBM) for a in list(arrays) + lands], *after)
    return (res[:3], res[3:3 + n], res[3 + n:3 + 2 * n], plan), res[-1]


def _pass_on_copies(lands, send_sems, recv_sems, plan):
    x, y, c, _ = _mesh_pos()
    sibling = _peer(x, y, c, _SIBLING)[0]
    cps = []
    for i, r in enumerate(_SAME_CORE):
        owner = _peer(x, y, c, r)[1]
        for a in range(len(lands)):
            k = a * len(_SAME_CORE) + i
            cps.append(pltpu.make_async_remote_copy(
                src_ref=plan.dst(lands, a, owner), dst_ref=plan.dst(lands, a, owner), send_sem=send_sems.at[k],
                recv_sem=recv_sems.at[k], device_id=sibling, device_id_type=pl.DeviceIdType.MESH))
    return cps


def _gather_pass_on(name, state, after):
    sems, srcs, lands, plan = state
    n = len(srcs)
    n_sem = n * len(_SAME_CORE)

    def body(*refs):
        srcs_, lands_ = refs[:n], refs[n:2 * n]
        send_sems, recv_sems = refs[2 * n], refs[2 * n + 1]
        on_send, on_recv = refs[2 * n + 3 + len(after)], refs[2 * n + 4 + len(after)]
        token = refs[-1]
        arrivals = _peer_copies(srcs_, lands_, send_sems, recv_sems, plan, _SAME_CORE)
        for arrived, on in zip(arrivals, _pass_on_copies(lands_, on_send, on_recv, plan)):
            arrived.wait_recv()
            on.start()
        token[...] = jnp.zeros_like(token)

    hbm = lambda a: pltpu.HBM(a.shape, a.dtype)
    res = pl.pallas_call(
        body, name=name,
        out_shape=(pltpu.SemaphoreType.DMA((n_sem,)), pltpu.SemaphoreType.DMA((n_sem,)),
                   *[hbm(a) for a in list(srcs) + list(lands)], _sds((8, 128))),
        in_specs=[_HBM] * (2 * n) + [_SEM] * 3 + [pl.BlockSpec(memory_space=pl.ANY)] * len(after),
        out_specs=(_SEM, _SEM, *[_HBM] * (2 * n), pl.BlockSpec(memory_space=pltpu.VMEM)),
        input_output_aliases={i: 2 + i for i in range(2 * n)},
        compiler_params=pltpu.CompilerParams(has_side_effects=_EFFECT),
    )(*srcs, *lands, *sems, *after)
    return (sems, res[:2], res[2:2 + n], res[2 + n:2 + 2 * n], plan), res[-1]


def _gather_wait(name, state, after):
    sems, on_sems, srcs, lands, plan = state
    n = len(srcs)

    def body(*refs):
        srcs_, lands_ = refs[:n], refs[n:2 * n]
        send_sems, recv_sems, own_sems, on_send, on_recv = refs[2 * n:2 * n + 5]
        for cp in _peer_copies(srcs_, lands_, send_sems, recv_sems, plan, (_SIBLING,)):
            cp.wait_recv()
        for cp in _peer_copies(srcs_, lands_, send_sems, recv_sems, plan, (_SIBLING,) + _SAME_CORE):
            cp.wait_send()
        for cp in _own_copies(srcs_, lands_, own_sems, plan):
            cp.wait()
        for cp in _pass_on_copies(lands_, on_send, on_recv, plan):
            cp.wait_send()
            cp.wait_recv()

    hbm = lambda a: pltpu.HBM(a.shape, a.dtype)
    res = pl.pallas_call(
        body, name=name, out_shape=tuple(hbm(a) for a in list(srcs) + list(lands)),
        in_specs=[_HBM] * (2 * n) + [_SEM] * 5 + [pl.BlockSpec(memory_space=pl.ANY)] * len(after),
        out_specs=tuple([_HBM] * (2 * n)), input_output_aliases={i: i for i in range(2 * n)},
        compiler_params=pltpu.CompilerParams(has_side_effects=_EFFECT),
    )(*srcs, *lands, *sems, *on_sems, *after)
    return res[n:]


def _sum_parts(name, recv):
    def body(r_ref, o_ref):
        s = r_ref[0]
        for j in range(1, N_DEV):
            s = s + r_ref[j]
        o_ref[...] = s

    return pl.pallas_call(body, name=name, out_shape=_sds(recv.shape[1:], recv.dtype))(recv)


def _adam(name, recv, w, m, v, col=0):
    shape = w.shape
    R, C = shape
    tr = R
    while N_DEV * tr * C * 4 > 6 * 1024 * 1024 and tr % 32 == 0:
        tr //= 2

    def body(recv_ref, w_ref, m_ref, v_ref, g_ref, d_ref, m2_ref, v2_ref):
        g = recv_ref[0].astype(F32)
        for j in range(1, N_DEV):
            g = g + recv_ref[j].astype(F32)
        w_, m_, v_ = w_ref[...], m_ref[...], v_ref[...]
        m2 = ADAM_B1 * m_ + (1.0 - ADAM_B1) * g
        v2 = ADAM_B2 * v_ + (1.0 - ADAM_B2) * (g * g)
        m_hat = m2 / (1.0 - ADAM_B1 ** ADAM_STEP)
        v_hat = v2 / (1.0 - ADAM_B2 ** ADAM_STEP)
        g_ref[...] = g
        d_ref[...] = -ADAM_LR * (m_hat / (jnp.sqrt(v_hat) + ADAM_EPS) + ADAM_WD * w_)
        m2_ref[...] = m2
        v2_ref[...] = v2

    blk = pl.BlockSpec((tr, C), lambda i: (i, 0))
    return _pcall(body, name=name, grid=(R // tr,),
                  in_specs=[pl.BlockSpec((N_DEV, tr, C), lambda i: (0, i, col)), blk, blk, blk],
                  out_specs=[blk] * 4, out_shape=[_sds(shape)] * 4)(recv, w, m, v)


_NAMES = ['ffn1_pre_g', 'ffn1_w_in', 'ffn1_w_out', 'ffn1_post_g', 'mix_pre_g', 'w_in', 'w_fu', 'b_f', 'gla_norm_g',
          'w_pool', 'pool_scale', 'mem_norm_g', 'w_mem_kv', 'w_up_gla', 'w_up_pool', 'w_up_xattn', 'w_o', 'mix_post_g',
          'ffn2_pre_g', 'ffn2_w_in', 'ffn2_w_out', 'ffn2_post_g', 'final_g']
_SHARDED = ['ffn1_w_in', 'ffn1_w_out', 'w_in', 'w_fu', 'w_mem_kv', 'w_up_gla', 'w_up_pool', 'w_up_xattn', 'w_o',
            'ffn2_w_in', 'ffn2_w_out']
_COL_SHARDED = ['w_up_pool', 'w_up_xattn']


def _cols_to_full(g):
    return jnp.transpose(g, (1, 0, 2)).reshape(g.shape[1], N_DEV * g.shape[2])


def _full_to_cols(f):
    R, C = f.shape
    return jnp.transpose(f.reshape(R, N_DEV, C // N_DEV), (1, 0, 2))


def _to_internal(w_in_t):
    o = 0
    parts = []
    for s in IN_SPLITS:
        parts.append(w_in_t[o:o + s])
        o += s
    q, k, v, g_out, f_low, p_in, xq, gates = parts
    f_low = jnp.pad(f_low, ((0, FLOW_W - GATE_RANK), (0, 0)))
    return jnp.concatenate([q, k, v, g_out, gates, p_in, xq, f_low], axis=0)


def _from_internal(d, d_flow):
    q, k, v, g_out = d[0:512], d[512:1024], d[1024:2048], d[2048:3072]
    gates, p_in, xq = d[3072:6144], d[6144:6656], d[6656:7168]
    return jnp.concatenate([q, k, v, g_out, d_flow[:GATE_RANK], p_in, xq, gates], axis=0)


_FFN_IN = ('ffn1_w_in', 'ffn2_w_in')
_FFN_OUT = ('ffn1_w_out', 'ffn2_w_out')
_GATHERS = {"ffn1_in": ['ffn1_w_in'], "ffn1_out": ['ffn1_w_out'],
            "mix_in": ['w_in', 'w_fu'], "mix_rest": ['w_mem_kv', 'w_up_gla', 'w_up_pool', 'w_up_xattn', 'w_o'],
            "ffn2": ['ffn2_w_in', 'ffn2_w_out']}
_MIX = _GATHERS["mix_in"] + _GATHERS["mix_rest"]
_MIX_UP = ['w_up_gla', 'w_up_pool', 'w_up_xattn', 'w_o']
_MIX_LATE = [n for n in _MIX if n not in _MIX_UP]


def _ffn_in_slot(ref, d):
    return ref.at[2 * (d % N_FF_BLK) + d // N_FF_BLK, pl.ds(0, FF_BLK)]


def _ffn_out_slot(ref, d):
    rows = FF_BLK // 2
    return ref.at[d // 2, pl.ds(pl.multiple_of((d % 2) * rows, rows), rows)]


def _ffn_plan(names, scatter):
    slots, shapes = {}, {}
    for a, n in enumerate(names):
        if n in _FFN_IN:
            slots[a] = _ffn_in_slot
            shapes[a] = (FF_BLK, D_MODEL) if scatter else (N_DEV, FF_PAD, D_MODEL)
        elif n in _FFN_OUT:
            slots[a] = _ffn_out_slot
            shapes[a] = (FF_BLK // 2, D_MODEL) if scatter else (N_FF_BLK, FF_PAD, D_MODEL)
    return _Plan(scatter, slots, shapes)


def _rows_to_full(g):
    return g.reshape(N_DEV * g.shape[1], g.shape[2])


def _mix_weights(gathered):
    W = {}
    for n, g in gathered.items():
        if n == "w_in":
            W["w_int_t"] = _to_internal(g.reshape(IN_WIDTH, D_MODEL))
        elif n == "w_fu":
            W["w_fu_pad"] = jnp.pad(_cols_to_full(g), ((0, FLOW_W - GATE_RANK), (0, 0)))
        else:
            W[n] = _cols_to_full(g) if n in _COL_SHARDED else _rows_to_full(g)
    return W


def _mix_chunks(G, n):
    if n == "w_in":
        return _from_internal(*G["w_int_t"]).reshape(N_DEV, IN_SHARD, D_MODEL)
    if n == "w_fu":
        return _full_to_cols(G["w_fu_pad"][:GATE_RANK].astype(BF16))
    if n in _COL_SHARDED:
        return _full_to_cols(G[n])
    return G[n].reshape(N_DEV, G[n].shape[0] // N_DEV, G[n].shape[1])


def _step(x, mem, tgt, P, Mo, Vo):
    def native(n, a):
        return jnp.swapaxes(a, 0, 1) if n in _FFN_IN + ("w_in",) else a

    P, Mo, Vo = ({n: native(n, a) for n, a in d.items()} for d in (P, Mo, Vo))
    small = {n: P[n] for n in _NAMES if n not in _SHARDED}

    gather, tokens = {}, []
    for grp, names in _GATHERS.items():
        gather[grp], tok = _gather_start("gather_" + grp, [P[n].astype(BF16) for n in names], _ffn_plan(names, False),
                                         tokens[-1:])
        tokens.append(tok)

    def pass_on(grp, after):
        return _gather_pass_on("gather_" + grp + "_on", gather[grp], after)

    def gathered(grp, after):
        return _gather_wait("gather_" + grp + "_wait", pass_on(grp, after)[0], [])

    def ffn_wo(lands):
        return lambda act: lands[0].reshape(N_FF_BLK * FF_PAD, D_MODEL)

    h1 = _pre_norm("ffn1_pre", x, small["ffn1_pre_g"], tokens[:1])
    w1t = gathered("ffn1_in", tokens[-1:] + [h1])[0].reshape(N_DEV * FF_PAD, D_MODEL)
    (x1, hm), w1o, sv1 = _ffn_fwd("ffn1", x, h1, w1t, small["ffn1_post_g"],
                                  lambda act: ffn_wo(gathered("ffn1_out", [act]))(act), small["mix_pre_g"])

    Wm = {**small, **_mix_weights(dict(zip(_GATHERS["mix_in"], gathered("mix_in", [x1]))))}
    passed = {}

    def pass_on_later(proj):
        tokens = []
        for grp in ("mix_rest", "ffn2"):
            passed[grp], tok = pass_on(grp, [proj])
            tokens.append(tok)
        return tokens

    def mix_rest(after):
        names = _GATHERS["mix_rest"]
        return _mix_weights(dict(zip(names, _gather_wait("gather_mix_rest_wait", passed["mix_rest"], after))))

    x2, h2, Wm, svm = _mix_fwd(x1, hm, mem, Wm, small["ffn2_pre_g"], pass_on_later, mix_rest)
    w2t, w2o = _gather_wait("gather_ffn2_wait", passed["ffn2"], [x2])
    w2t = w2t.reshape(N_DEV * FF_PAD, D_MODEL)
    (loss, dx3, d_final_g), w2o, sv2 = _ffn_fwd("ffn2", x2, h2, w2t, small["ffn2_post_g"], ffn_wo([w2o]),
                                                small["final_g"], tgt)

    G = dict(final_g=d_final_g)
    scat = {}

    def start(grp, names, arrays):
        scat[grp] = names, _exchange_start("scatter_" + grp, arrays, _ffn_plan(names, True))
        return scat[grp][1][1]

    def ffn_starts(tag):
        return (lambda dwo: start(tag + "_out", [tag + "_w_out"], [dwo.reshape(N_FF_BLK, FF_PAD, D_MODEL)]),
                lambda dwt: start(tag + "_in", [tag + "_w_in"], [dwt.reshape(N_DEV, FF_PAD, D_MODEL)]))

    dx2, G["ffn2_pre_g"], G["ffn2_post_g"] = _ffn_bwd(
        "ffn2", dx3, x2, small["ffn2_pre_g"], w2t, w2o, small["ffn2_post_g"], sv2, *ffn_starts("ffn2"))
    dx1, Gm = _mix_bwd(dx2, x1, mem, Wm, svm,
                       lambda Gm: start("mix", _MIX_LATE, [_mix_chunks(Gm, n) for n in _MIX_LATE]),
                       lambda Gm: start("mix_up", _MIX_UP, [_mix_chunks(Gm, n) for n in _MIX_UP]))
    G.update(Gm)
    dx, G["ffn1_pre_g"], G["ffn1_post_g"] = _ffn_bwd(
        "ffn1", dx1, x, small["ffn1_pre_g"], w1t, w1o, small["ffn1_post_g"], sv1, *ffn_starts("ffn1"))

    vecs = sorted((n for n in small if P[n].ndim == 2), key=lambda n: -P[n].shape[1])
    packed = jnp.concatenate([G[n] for n in vecs] + [loss], axis=1)
    place, o = {}, 0
    for n in vecs:
        place[n] = o // P[n].shape[1]
        o += P[n].shape[1]
    scat["small"] = ["packed", "w_pool"], _exchange_start("gather_small_grads", [packed, G["w_pool"]], _GATHER)
    recv, outs = {}, {}
    done = [dx]
    for grp in ["ffn2_out", "ffn2_in", "mix_up", "mix", "ffn1_out", "ffn1_in", "small"]:
        names, (state, _) = scat[grp]
        recv.update(zip(names, _exchange_wait(("gather_" if grp == "small" else "scatter_") + grp + "_wait", state,
                                              done)))
        if grp == "small":
            loss = _sum_parts("loss_sum", recv["packed"][:, :, o:])[0, 0]
            names = list(small)
        for n in names:
            shp = P[n].shape
            shp2 = shp if len(shp) == 2 else (shp[0] * shp[1], shp[2])
            r = recv["packed"] if n in place else recv[n].reshape((N_DEV,) + shp2)
            res = _adam("adam_" + n, r, P[n].reshape(shp2), Mo[n].reshape(shp2), Vo[n].reshape(shp2), place.get(n, 0))
            outs[n] = [native(n, t.reshape(shp)) for t in res]
            done.append(res[-1])
    return loss, dx, outs


def kernel(x, mem, ffn1_pre_g, ffn1_w_in, ffn1_w_out, ffn1_post_g, mix_pre_g, w_in, w_fu, b_f, gla_norm_g, w_pool, pool_scale, mem_norm_g, w_mem_kv, w_up_gla, w_up_pool, w_up_xattn, w_o, mix_post_g, ffn2_pre_g, ffn2_w_in, ffn2_w_out, ffn2_post_g, final_g, loss_target, m_ffn1_pre_g, m_ffn1_w_in, m_ffn1_w_out, m_ffn1_post_g, m_mix_pre_g, m_w_in, m_w_fu, m_b_f, m_gla_norm_g, m_w_pool, m_pool_scale, m_mem_norm_g, m_w_mem_kv, m_w_up_gla, m_w_up_pool, m_w_up_xattn, m_w_o, m_mix_post_g, m_ffn2_pre_g, m_ffn2_w_in, m_ffn2_w_out, m_ffn2_post_g, m_final_g, v_ffn1_pre_g, v_ffn1_w_in, v_ffn1_w_out, v_ffn1_post_g, v_mix_pre_g, v_w_in, v_w_fu, v_b_f, v_gla_norm_g, v_w_pool, v_pool_scale, v_mem_norm_g, v_w_mem_kv, v_w_up_gla, v_w_up_pool, v_w_up_xattn, v_w_o, v_mix_post_g, v_ffn2_pre_g, v_ffn2_w_in, v_ffn2_w_out, v_ffn2_post_g, v_final_g):
    params = [ffn1_pre_g, ffn1_w_in, ffn1_w_out, ffn1_post_g, mix_pre_g, w_in, w_fu, b_f, gla_norm_g, w_pool, pool_scale, mem_norm_g, w_mem_kv, w_up_gla, w_up_pool, w_up_xattn, w_o, mix_post_g, ffn2_pre_g, ffn2_w_in, ffn2_w_out, ffn2_post_g, final_g]
    moms = [m_ffn1_pre_g, m_ffn1_w_in, m_ffn1_w_out, m_ffn1_post_g, m_mix_pre_g, m_w_in, m_w_fu, m_b_f, m_gla_norm_g, m_w_pool, m_pool_scale, m_mem_norm_g, m_w_mem_kv, m_w_up_gla, m_w_up_pool, m_w_up_xattn, m_w_o, m_mix_post_g, m_ffn2_pre_g, m_ffn2_w_in, m_ffn2_w_out, m_ffn2_post_g, m_final_g]
    vars_ = [v_ffn1_pre_g, v_ffn1_w_in, v_ffn1_w_out, v_ffn1_post_g, v_mix_pre_g, v_w_in, v_w_fu, v_b_f, v_gla_norm_g, v_w_pool, v_pool_scale, v_mem_norm_g, v_w_mem_kv, v_w_up_gla, v_w_up_pool, v_w_up_xattn, v_w_o, v_mix_post_g, v_ffn2_pre_g, v_ffn2_w_in, v_ffn2_w_out, v_ffn2_post_g, v_final_g]
    P = {n: a[0] if a.ndim > 2 else a for n, a in zip(_NAMES, params)}
    Mo = {n: a[0] if a.ndim > 2 else a for n, a in zip(_NAMES, moms)}
    Vo = {n: a[0] if a.ndim > 2 else a for n, a in zip(_NAMES, vars_)}
    loss, dx, outs = _step(x[0], mem[0], loss_target[0], P, Mo, Vo)
    out = [loss, dx[None]]
    for kind in range(4):
        for n, p in zip(_NAMES, params):
            out.append(outs[n][kind].reshape(p.shape))
    return tuple(out)
```

```python
import functools

import jax
import jax.numpy as jnp
from jax import lax
from jax.experimental import pallas as pl
from jax.experimental.pallas import tpu as pltpu

F32 = jnp.float32
BF16 = jnp.bfloat16

N_DEV = 8
D_MODEL = 1024
D_FF = 2816
FF_BLK = 2 * D_FF // N_DEV
N_FF_BLK = D_FF // FF_BLK
FF_PAD = 768
CHUNK = 64
GLA_HEADS = 4
GLA_DK = 512
GLA_DV = 1024
GLA_HDK = 128
GLA_HDV = 256
GATE_RANK = 16
GATE_TEMP = 16.0
POOL_WINDOWS = (2, 4, 8, 16)
POOL_W = 512
POOL_GD = 128
POOL_HALO = 16
XA_HEADS = 4
XA_HD = 128
XA_W = 512
EPS = 1e-6
IN_SPLITS = (GLA_DK, GLA_DK, GLA_DV, GLA_DV, GATE_RANK, POOL_W, XA_W, 3 * D_MODEL)
IN_WIDTH = sum(IN_SPLITS)
IN_SHARD = IN_WIDTH // N_DEV
INT_W = 3072 + 3072 + 1024 + 128
INT_NBLK = 3
INT_BLK = INT_W // INT_NBLK
FLOW_W = 128
INT_MAIN = INT_W - FLOW_W
_POOL_COL = (3072 + 3072) // POOL_W
_XQ_COL = (3072 + 3072 + POOL_W) // XA_W

ADAM_LR = 0.001
ADAM_B1 = 0.9
ADAM_B2 = 0.999
ADAM_EPS = 1e-08
ADAM_WD = 0.01
ADAM_STEP = 10

VMEM_LIMIT = 56 * 1024 * 1024

_NN = (((1,), (0,)), ((), ()))
_NT = (((1,), (1,)), ((), ()))
_TN = (((0,), (0,)), ((), ()))


def _pcall(body, *, name, grid, in_specs, out_specs, out_shape, scratch=(), aliases=None):
    return pl.pallas_call(
        body, name=name, grid=grid, in_specs=in_specs, out_specs=out_specs, out_shape=out_shape,
        scratch_shapes=list(scratch), input_output_aliases=aliases or {},
        compiler_params=pltpu.CompilerParams(dimension_semantics=("arbitrary",) * len(grid),
                                             vmem_limit_bytes=VMEM_LIMIT))


def _dot(a, b, dims=_NN):
    return lax.dot_general(a.astype(BF16), b.astype(BF16), dims, preferred_element_type=F32)


def _mm(name, a, b, *, grid, a_blk, a_map, b_blk, b_map, o_shape, o_blk, o_map, dims, out_dtype=F32, deps=()):
    nk = grid[2]

    def body(a_ref, b_ref, *rest):
        o_ref, scr = rest[len(deps)], rest[len(deps) + 1:]
        p = _dot(a_ref[...], b_ref[...], dims)
        if nk == 1:
            o_ref[...] = p.astype(o_ref.dtype)
        else:
            acc = scr[0]
            k = pl.program_id(2)

            @pl.when(k == 0)
            def _():
                acc[...] = p

            @pl.when(k > 0)
            def _():
                acc[...] += p

            @pl.when(k == nk - 1)
            def _():
                o_ref[...] = acc[...].astype(o_ref.dtype)

    acc_shape = tuple(d for d in o_blk if d is not None)
    return _pcall(body, name=name, grid=grid,
                  in_specs=[pl.BlockSpec(a_blk, a_map), pl.BlockSpec(b_blk, b_map)]
                  + [pl.BlockSpec(memory_space=pl.ANY)] * len(deps),
                  out_specs=pl.BlockSpec(o_blk, o_map),
                  out_shape=jax.ShapeDtypeStruct(o_shape, out_dtype),
                  scratch=[pltpu.VMEM(acc_shape, F32)] if nk > 1 else [])(a, b, *deps)


def _mm_nn(name, a, b, out_dtype=F32, tm=1024, tn=None):
    M, K = a.shape
    N = b.shape[1]
    tm, tn = min(tm, M), (tn or N)
    return _mm(name, a, b, grid=(N // tn, M // tm, 1), a_blk=(tm, K), a_map=lambda j, i, k: (i, 0),
               b_blk=(K, tn), b_map=lambda j, i, k: (0, j), o_shape=(M, N), o_blk=(tm, tn),
               o_map=lambda j, i, k: (i, j), dims=_NN, out_dtype=out_dtype)


def _mm_nt(name, a, b, out_dtype=F32, tm=1024):
    M, K = a.shape
    N = b.shape[0]
    tm = min(tm, M)
    return _mm(name, a, b, grid=(1, M // tm, 1), a_blk=(tm, K), a_map=lambda j, i, k: (i, 0),
               b_blk=(N, K), b_map=lambda j, i, k: (0, 0), o_shape=(M, N), o_blk=(tm, N),
               o_map=lambda j, i, k: (i, 0), dims=_NT, out_dtype=out_dtype)


def _mm_tn(name, a, b, out_dtype=BF16, ts=512, tn=None, tn_a=None, deps=()):
    S, M = a.shape
    N = b.shape[1]
    ts, tn, tn_a = min(ts, S), (tn or N), (tn_a or M)
    return _mm(name, a, b, grid=((N // tn) * (M // tn_a), 1, S // ts), a_blk=(ts, tn_a),
               a_map=lambda j, i, k: (k, j if tn_a < M else 0), b_blk=(ts, tn),
               b_map=lambda j, i, k: (k, j if tn < N else 0), o_shape=(M, N), o_blk=(tn_a, tn),
               o_map=lambda j, i, k: (j, 0) if tn_a < M else (0, j), dims=_TN, out_dtype=out_dtype, deps=deps)


class _Win:
    def __init__(self, arr, w, c):
        self.arr, self.w, self.c = arr, w, c


def _row_spec(x, tm):
    if isinstance(x, _Win):
        return x.arr, pl.BlockSpec((tm, x.w), functools.partial(lambda i, c: (i, c), c=x.c))
    if x.ndim == 3:
        return x, pl.BlockSpec((x.shape[0], tm, x.shape[2]), lambda i: (0, i, 0))
    return x, pl.BlockSpec((tm, x.shape[1]), lambda i: (i, 0))


def _rowwise(name, fn, rows, consts, outs, accs=(), tm=512):
    first = rows[0].arr if isinstance(rows[0], _Win) else rows[0]
    S = first.shape[1] if first.ndim == 3 else first.shape[0]
    tm = min(tm, S)
    n_in, n_out = len(rows) + len(consts), len(outs)
    arrays, in_specs = [], []
    for r in rows:
        arr, spec = _row_spec(r, tm)
        arrays.append(arr)
        in_specs.append(spec)
    for c in consts:
        arrays.append(c)
        in_specs.append(pl.BlockSpec(c.shape, functools.partial(lambda i, n: (0,) * n, n=c.ndim)))
    aliases = {}
    for k, o in enumerate(outs):
        if isinstance(o, _Win):
            aliases[len(arrays)] = k
            arrays.append(o.arr)
            in_specs.append(pl.BlockSpec(memory_space=pl.ANY))
    n_thru = len(aliases)
    out_specs = [_row_spec(o, tm)[1] for o in outs]
    out_specs += [pl.BlockSpec(a.shape, functools.partial(lambda i, n: (0,) * n, n=len(a.shape))) for a in accs]
    out_shape = [_sds(o.arr.shape, o.arr.dtype) if isinstance(o, _Win) else _sds(o.shape, o.dtype) for o in outs]

    def body(*refs):
        res = fn(*[r[...] for r in refs[:n_in]])
        if not isinstance(res, (tuple, list)):
            res = (res,)
        orefs = refs[n_in + n_thru:]
        for r, v in zip(orefs[:n_out], res[:n_out]):
            r[...] = v.astype(r.dtype)
        i = pl.program_id(0)
        for r, v in zip(orefs[n_out:], res[n_out:]):
            @pl.when(i == 0)
            def _(r=r, v=v):
                r[...] = v.astype(r.dtype)

            @pl.when(i > 0)
            def _(r=r, v=v):
                r[...] += v.astype(r.dtype)

    return _pcall(body, name=name, grid=(S // tm,), in_specs=in_specs, out_specs=out_specs,
                  out_shape=out_shape + [_sds(a.shape, a.dtype) for a in accs], aliases=aliases)(*arrays)


def _sds(shape, dtype=F32):
    return jax.ShapeDtypeStruct(shape, dtype)


def _rms(x, g):
    return x * lax.rsqrt(jnp.mean(x * x, axis=-1, keepdims=True) + EPS) * g


def _rms_bwd(x, g, dy):
    _, vjp = jax.vjp(_rms, x, g)
    return vjp(dy)


def _sigmoid(x):
    return 0.5 * jnp.tanh(0.5 * x) + 0.5


def _silu(x):
    return x * _sigmoid(x)


def _log_sigmoid(f):
    return jnp.minimum(f, 0.0) - jnp.log(1.0 + jnp.exp(-jnp.abs(f)))


def _head_rms_gate(o, g_out, gn):
    parts = [_rms(o[:, h * GLA_HDV:(h + 1) * GLA_HDV], gn[:, h * GLA_HDV:(h + 1) * GLA_HDV]) for h in range(GLA_HEADS)]
    return jnp.concatenate(parts, axis=-1) * _silu(g_out.astype(F32))


def _merge(gates, ya, yb, yc):
    gates, ya, yb, yc = (t.astype(F32) for t in (gates, ya, yb, yc))
    return (_sigmoid(gates[:, :D_MODEL]) * ya + _sigmoid(gates[:, D_MODEL:2 * D_MODEL]) * yb
            + _sigmoid(gates[:, 2 * D_MODEL:]) * yc)


def _tri_dot(t, x):
    hi = x.astype(BF16)
    r1 = x - hi.astype(F32)
    mid = r1.astype(BF16)
    lo = (r1 - mid.astype(F32)).astype(BF16)
    d = functools.partial(lax.dot_general, dimension_numbers=_NN, preferred_element_type=F32)
    return d(t, hi) + d(t, mid) + d(t, lo)


def _pre_norm(name, x, g, after):
    S = x.shape[0]
    tm = min(1024, S)

    def body(x_ref, g_ref, *rest):
        rest[-1][...] = _rms(x_ref[...], g_ref[...]).astype(BF16)

    row = pl.BlockSpec((tm, D_MODEL), lambda i: (i, 0))
    return _pcall(body, name=name, grid=(S // tm,),
                  in_specs=[row, pl.BlockSpec((1, D_MODEL), lambda i: (0, 0))] + [pl.BlockSpec(memory_space=pl.ANY)] * len(after),
                  out_specs=row, out_shape=_sds(x.shape, BF16))(x, g, *after)


def _ffn_fwd(tag, x, h, wt, gpost, get_wo, g_next, tgt=None):
    S = x.shape[0]
    tm = min(512, S)

    def up_body(h_ref, w_ref, u_ref, act_ref):
        hh = h_ref[...]
        for j in range(N_FF_BLK):
            ab = _dot(hh, w_ref[2 * FF_PAD * j:2 * FF_PAD * (j + 1), :], _NT)
            u_ref[:, 2 * FF_PAD * j:2 * FF_PAD * (j + 1)] = ab.astype(BF16)
            act_ref[:, FF_PAD * j:FF_PAD * (j + 1)] = (_silu(ab[:, :FF_PAD]) * ab[:, FF_PAD:]).astype(BF16)

    def down_head_body(act_ref, w_ref, x_ref, g_ref, gn_ref, t_ref, f_ref, dxo_ref, loss_ref, dgn_ref):
        i = pl.program_id(0)
        f = _dot(act_ref[...], w_ref[...])
        f_ref[...] = f.astype(BF16)
        xo = x_ref[...] + 0.5 * _rms(f, g_ref[...])
        out, vjp = jax.vjp(_rms, xo, gn_ref[...])
        e = out - t_ref[...]
        loss = 0.5 * jnp.sum(jnp.mean(e * e, axis=-1, keepdims=True), axis=0, keepdims=True)
        dxo_ref[...], dgn = vjp(e * (1.0 / D_MODEL))

        @pl.when(i == 0)
        def _():
            loss_ref[...] = jnp.broadcast_to(loss, loss_ref.shape)
            dgn_ref[...] = dgn

        @pl.when(i > 0)
        def _():
            loss_ref[...] += jnp.broadcast_to(loss, loss_ref.shape)
            dgn_ref[...] += dgn

    if tgt is not None:
        def last_body(h_ref, wt_ref, wo_ref, x_ref, g_ref, gn_ref, t_ref, u_ref, act_ref, f_ref, dxo_ref, loss_ref,
                      dgn_ref):
            up_body(h_ref, wt_ref, u_ref, act_ref)
            down_head_body(act_ref, wo_ref, x_ref, g_ref, gn_ref, t_ref, f_ref, dxo_ref, loss_ref, dgn_ref)

        wo = get_wo(None)
        tl = min(256, S)
        row = pl.BlockSpec((tl, D_MODEL), lambda i: (i, 0))
        vec = pl.BlockSpec((1, D_MODEL), lambda i: (0, 0))
        u, act, f, dxo, loss, dg_next = _pcall(
            last_body, name=tag + "_fwd", grid=(S // tl,),
            in_specs=[row, pl.BlockSpec(wt.shape, lambda i: (0, 0), pipeline_mode=pl.Buffered(1)),
                      pl.BlockSpec(wo.shape, lambda i: (0, 0), pipeline_mode=pl.Buffered(1)), row, vec, vec, row],
            out_specs=[pl.BlockSpec((tl, N_DEV * FF_PAD), lambda i: (i, 0)),
                       pl.BlockSpec((tl, N_FF_BLK * FF_PAD), lambda i: (i, 0)), row, row,
                       pl.BlockSpec((1, 128), lambda i: (0, 0)), vec],
            out_shape=[_sds((S, N_DEV * FF_PAD), BF16), _sds((S, N_FF_BLK * FF_PAD), BF16), _sds((S, D_MODEL), BF16),
                       _sds((S, D_MODEL)), _sds((1, 128)), _sds((1, D_MODEL))],
        )(h, wt, wo, x, gpost, g_next, tgt)
        return (loss, dxo, dg_next), wo, (h, u, act, f)

    u, act = _pcall(
        up_body, name=tag + "_up", grid=(S // tm,),
        in_specs=[pl.BlockSpec((tm, D_MODEL), lambda i: (i, 0)),
                  pl.BlockSpec(wt.shape, lambda i: (0, 0), pipeline_mode=pl.Buffered(1))],
        out_specs=[pl.BlockSpec((tm, N_DEV * FF_PAD), lambda i: (i, 0)),
                   pl.BlockSpec((tm, N_FF_BLK * FF_PAD), lambda i: (i, 0))],
        out_shape=[_sds((S, N_DEV * FF_PAD), BF16), _sds((S, N_FF_BLK * FF_PAD), BF16)],
    )(h, wt)

    def down_body(act_ref, w_ref, x_ref, g_ref, gn_ref, f_ref, xo_ref, hn_ref):
        f = _dot(act_ref[...], w_ref[...])
        f_ref[...] = f.astype(BF16)
        xo = x_ref[...] + 0.5 * _rms(f, g_ref[...])
        xo_ref[...] = xo
        hn_ref[...] = _rms(xo, gn_ref[...]).astype(BF16)

    wo = get_wo(act)
    row = pl.BlockSpec((tm, D_MODEL), lambda i: (i, 0))
    vec = pl.BlockSpec((1, D_MODEL), lambda i: (0, 0))
    f, xo, h_next = _pcall(
        down_body, name=tag + "_down", grid=(S // tm,),
        in_specs=[pl.BlockSpec((tm, N_FF_BLK * FF_PAD), lambda i: (i, 0)),
                  pl.BlockSpec(wo.shape, lambda i: (0, 0), pipeline_mode=pl.Buffered(1)), row, vec, vec],
        out_specs=[row, row, row], out_shape=[_sds((S, D_MODEL), BF16), _sds((S, D_MODEL)), _sds((S, D_MODEL), BF16)],
    )(act, wo, x, gpost, g_next)
    return (xo, h_next), wo, (h, u, act, f)


def _ffn_bwd(tag, dxo, x, gpre, wt, wo, gpost, saved, on_dwo=None, on_dwt=None):
    h, u, act, f = saved
    S = x.shape[0]
    tm = min(512, S)

    def dact_body(f_ref, dxo_ref, g_ref, w_ref, u_ref, df_ref, du_ref, dg_ref):
        i = pl.program_id(0)
        df, dg = _rms_bwd(f_ref[...].astype(F32), g_ref[...], 0.5 * dxo_ref[...])
        df = df.astype(BF16)
        df_ref[...] = df

        @pl.when(i == 0)
        def _():
            dg_ref[...] = dg

        @pl.when(i > 0)
        def _():
            dg_ref[...] += dg

        for j in range(N_FF_BLK):
            dact = _dot(df, w_ref[FF_PAD * j:FF_PAD * (j + 1), :], _NT)
            ab = u_ref[:, 2 * FF_PAD * j:2 * FF_PAD * (j + 1)].astype(F32)
            a, b = ab[:, :FF_PAD], ab[:, FF_PAD:]
            sg = _sigmoid(a)
            du_ref[:, 2 * FF_PAD * j:2 * FF_PAD * j + FF_PAD] = (dact * b * (sg * (1.0 + a * (1.0 - sg)))).astype(BF16)
            du_ref[:, 2 * FF_PAD * j + FF_PAD:2 * FF_PAD * (j + 1)] = (dact * (a * sg)).astype(BF16)

    row = pl.BlockSpec((tm, D_MODEL), lambda i: (i, 0))
    vec = pl.BlockSpec((1, D_MODEL), lambda i: (0, 0))
    u_spec = pl.BlockSpec((tm, N_DEV * FF_PAD), lambda i: (i, 0))
    df, du, dgpost = _pcall(
        dact_body, name=tag + "_dact", grid=(S // tm,),
        in_specs=[row, row, vec, pl.BlockSpec(wo.shape, lambda i: (0, 0), pipeline_mode=pl.Buffered(1)), u_spec],
        out_specs=[row, u_spec, vec],
        out_shape=[_sds((S, D_MODEL), BF16), _sds((S, N_DEV * FF_PAD), BF16), _sds((1, D_MODEL))])(f, dxo, gpost, wo, u)
    dwo = _mm_tn(tag + "_dwo", act, df, ts=S, tn_a=FF_PAD)
    dwt = _mm_tn(tag + "_dwt", du, h, ts=S, tn_a=FF_PAD, deps=[on_dwo(dwo)] if on_dwo is not None else [])
    dx, dgpre = _dh_pre_norm_bwd(tag + "_dh", [du], wt, x, dxo, gpre, [on_dwt(dwt)] if on_dwt is not None else [])
    return dx, dgpre, dgpost


def _dh_pre_norm_bwd(name, dzs, wt, x, dres, g, deps):
    S = x.shape[0]
    ks = [dz.shape[1] for dz in dzs]

    def body(*refs):
        dz_refs = refs[:len(dzs)]
        w_ref, x_ref, dres_ref, g_ref = refs[len(dzs):len(dzs) + 4]
        dx_ref, dg_ref = refs[len(dzs) + 4 + len(deps):]
        i = pl.program_id(0)
        dh, o = None, 0
        for dz_ref, k in zip(dz_refs, ks):
            p = _dot(dz_ref[...], w_ref[o:o + k, :])
            dh, o = (p if dh is None else dh + p), o + k
        dx, dg = _rms_bwd(x_ref[...], g_ref[...], dh)
        dx_ref[...] = dx + dres_ref[...]

        @pl.when(i == 0)
        def _():
            dg_ref[...] = dg

        @pl.when(i > 0)
        def _():
            dg_ref[...] += dg

    th = min(512, S)
    row = pl.BlockSpec((th, D_MODEL), lambda i: (i, 0))
    vec = pl.BlockSpec((1, D_MODEL), lambda i: (0, 0))
    return _pcall(
        body, name=name, grid=(S // th,),
        in_specs=[pl.BlockSpec((th, k), lambda i: (i, 0)) for k in ks]
        + [pl.BlockSpec(wt.shape, lambda i: (0, 0), pipeline_mode=pl.Buffered(1)), row, row, vec]
        + [pl.BlockSpec(memory_space=pl.ANY)] * len(deps),
        out_specs=[row, vec], out_shape=[_sds((S, D_MODEL)), _sds((1, D_MODEL))])(*dzs, wt, x, dres, g, *deps)


GLA_G = 8


def _gla_tile_common(k, flow, wfu, bf):
    f = _dot(flow, wfu) + bf
    la = _log_sigmoid(f) * (1.0 / GATE_TEMP)
    tri = _tri_matrix(True)
    ws, ds = [], []
    for g in range(k.shape[0] // CHUNK):
        b = _tri_dot(tri, la[g * CHUNK:(g + 1) * CHUNK])
        b_end = b[CHUNK - 1:CHUNK, :]
        ws.append(jnp.exp(b_end - b))
        ds.append(jnp.exp(b_end))
    w = jnp.concatenate(ws, axis=0)
    return f, w, k * w, ds


def _tri_matrix(lower):
    r = lax.broadcasted_iota(jnp.int32, (CHUNK, CHUNK), 0)
    c = lax.broadcasted_iota(jnp.int32, (CHUNK, CHUNK), 1)
    return jnp.where((r >= c) if lower else (r <= c), 1.0, 0.0).astype(BF16)


def _heads():
    return [(slice(h * GLA_HDK, (h + 1) * GLA_HDK), slice(h * GLA_HDV, (h + 1) * GLA_HDV)) for h in range(GLA_HEADS)]


def _gla_fwd(proj, wfu, bf, gn, deps=()):
    S = proj.shape[0]
    G = min(GLA_G, S // CHUNK)
    T = G * CHUNK
    nc = S // CHUNK

    def body(q_ref, k_ref, v_ref, go_ref, flow_ref, wfu_ref, bf_ref, gn_ref, *rest):
        ya_ref, st_ref, state = rest[len(deps):]
        @pl.when(pl.program_id(0) == 0)
        def _():
            state[...] = jnp.zeros_like(state)

        _, _, kt, ds = _gla_tile_common(k_ref[...], flow_ref[...], wfu_ref[...], bf_ref[...])
        q = q_ref[...].astype(F32) * (GLA_HDK ** -0.5)
        v = v_ref[...]
        rows = [slice(g * CHUNK, (g + 1) * CHUNK) for g in range(G)]
        kv = [[_dot(v[r, vs], kt[r, ks], _TN) for ks, vs in _heads()] for r in rows]
        st = [state[vs, :] for _, vs in _heads()]
        o = []
        for g, r in enumerate(rows):
            outs = []
            for h, (ks, vs) in enumerate(_heads()):
                st[h] = st[h] * ds[g][:, ks] + kv[g][h]
                st_ref[g, vs, :] = st[h]
                outs.append(_dot(q[r, ks], st[h], _NT))
            o.append(jnp.concatenate(outs, axis=-1))
        for h, (_, vs) in enumerate(_heads()):
            state[vs, :] = st[h]
        ya_ref[...] = _head_rms_gate(jnp.concatenate(o, axis=0), go_ref[...], gn_ref[...]).astype(BF16)

    return _pcall(
        body, name="gla_fwd", grid=(S // T,),
        in_specs=[pl.BlockSpec((T, GLA_DK), lambda c: (c, 0)), pl.BlockSpec((T, GLA_DK), lambda c: (c, 1)),
                  pl.BlockSpec((T, GLA_DV), lambda c: (c, 1)), pl.BlockSpec((T, GLA_DV), lambda c: (c, 2)),
                  pl.BlockSpec((T, FLOW_W), lambda c: (c, (INT_W - FLOW_W) // FLOW_W)),
                  pl.BlockSpec(wfu.shape, lambda c: (0, 0)), pl.BlockSpec(bf.shape, lambda c: (0, 0)),
                  pl.BlockSpec(gn.shape, lambda c: (0, 0))]
        + [pl.BlockSpec(memory_space=pl.ANY)] * len(deps),
        out_specs=[pl.BlockSpec((T, GLA_DV), lambda c: (c, 0)),
                   pl.BlockSpec((G, GLA_DV, GLA_HDK), lambda c: (c, 0, 0))],
        out_shape=[_sds((S, GLA_DV), BF16), _sds((nc, GLA_DV, GLA_HDK))],
        scratch=[pltpu.VMEM((GLA_DV, GLA_HDK), F32)])(proj, proj, proj, proj, proj, wfu, bf, gn, *deps)


def _gla_bwd(proj, wfu, bf, gn, states, d_ya_in, dmain):
    S = proj.shape[0]
    G = min(GLA_G, S // CHUNK)
    T = G * CHUNK
    nt = S // T

    def body(q_ref, k_ref, v_ref, go_ref, flow_ref, wfu_ref, bf_ref, gn_ref, st_ref, stp_ref, dya_ref, dmain_in,
             dqkvg_ref, dflow_ref, dwfu_ref, dbf_ref, dgn_ref, dstate):
        dq_ref = dqkvg_ref.at[:, 0:GLA_DK]
        dk_ref = dqkvg_ref.at[:, GLA_DK:2 * GLA_DK]
        dv_ref = dqkvg_ref.at[:, 2 * GLA_DK:2 * GLA_DK + GLA_DV]
        dgo_ref = dqkvg_ref.at[:, 2 * GLA_DK + GLA_DV:]
        step = pl.program_id(0)

        @pl.when(step == 0)
        def _():
            dstate[...] = jnp.zeros_like(dstate)
            dwfu_ref[...] = jnp.zeros_like(dwfu_ref)
            dbf_ref[...] = jnp.zeros_like(dbf_ref)
            dgn_ref[...] = jnp.zeros_like(dgn_ref)

        flow, wfu_v = flow_ref[...], wfu_ref[...]
        f, w, kt, ds = _gla_tile_common(k_ref[...], flow, wfu_v, bf_ref[...])
        q = q_ref[...].astype(F32) * (GLA_HDK ** -0.5)
        v = v_ref[...]
        rows = [slice(g * CHUNK, (g + 1) * CHUNK) for g in range(G)]
        o = jnp.concatenate([jnp.concatenate([_dot(q[r, ks], st_ref[g, vs, :], _NT) for ks, vs in _heads()], axis=-1)
                             for g, r in enumerate(rows)], axis=0)
        _, vjp = jax.vjp(_head_rms_gate, o, go_ref[...], gn_ref[...])
        dout, dgo, dgn = vjp(dya_ref[...].astype(F32))
        dgo_ref[...] = dgo.astype(dgo_ref.dtype)
        dgn_ref[...] += dgn
        dq = [jnp.concatenate([_dot(dout[r, vs], st_ref[g, vs, :]) for _, vs in _heads()], axis=-1)
              for g, r in enumerate(rows)]
        qdo = [[_dot(dout[r, vs], q[r, ks], _TN) for ks, vs in _heads()] for r in rows]
        dq_ref[...] = (jnp.concatenate(dq, axis=0) * (GLA_HDK ** -0.5)).astype(dq_ref.dtype)
        has_prev = (step < nt - 1).astype(F32)
        carry = [dstate[vs, :] for _, vs in _heads()]
        dkt, dv, dd = [None] * G, [None] * G, [None] * G
        for g in reversed(range(G)):
            r = rows[g]
            dkts, dvs, dds = [], [], []
            for h, (ks, vs) in enumerate(_heads()):
                dst = carry[h] + qdo[g][h]
                dkts.append(_dot(v[r, vs], dst))
                dvs.append(_dot(kt[r, ks], dst, _NT))
                st_prev = st_ref[g - 1, vs, :] if g > 0 else stp_ref[vs, :] * has_prev
                dds.append(jnp.sum(dst * st_prev, axis=0, keepdims=True))
                carry[h] = dst * ds[g][:, ks]
            dkt[g], dv[g], dd[g] = (jnp.concatenate(t, axis=-1) for t in (dkts, dvs, dds))
        for h, (_, vs) in enumerate(_heads()):
            dstate[vs, :] = carry[h]
        dkt = jnp.concatenate(dkt, axis=0)
        dv_ref[...] = jnp.concatenate(dv, axis=0).astype(dv_ref.dtype)
        dk_ref[...] = (dkt * w).astype(dk_ref.dtype)
        de = dkt * kt
        tri = _tri_matrix(False)
        dla = []
        for g, r in enumerate(rows):
            db_end = jnp.sum(de[r], axis=0, keepdims=True) + dd[g] * ds[g]
            dla.append(db_end - _tri_dot(tri, de[r]))
        df = jnp.concatenate(dla, axis=0) * (1.0 - _sigmoid(f)) * (1.0 / GATE_TEMP)
        dflow_ref[...] = _dot(df, wfu_v, _NT).astype(dflow_ref.dtype)
        dwfu_ref[...] += _dot(flow, df, _TN)
        dbf_ref[...] += jnp.sum(df, axis=0, keepdims=True)

    rt = lambda s: nt - 1 - s
    return _pcall(
        body, name="gla_bwd", grid=(nt,),
        in_specs=[pl.BlockSpec((T, GLA_DK), lambda s: (rt(s), 0)), pl.BlockSpec((T, GLA_DK), lambda s: (rt(s), 1)),
                  pl.BlockSpec((T, GLA_DV), lambda s: (rt(s), 1)), pl.BlockSpec((T, GLA_DV), lambda s: (rt(s), 2)),
                  pl.BlockSpec((T, FLOW_W), lambda s: (rt(s), (INT_W - FLOW_W) // FLOW_W)),
                  pl.BlockSpec(wfu.shape, lambda s: (0, 0)), pl.BlockSpec(bf.shape, lambda s: (0, 0)),
                  pl.BlockSpec(gn.shape, lambda s: (0, 0)),
                  pl.BlockSpec((G, GLA_DV, GLA_HDK), lambda s: (rt(s), 0, 0)),
                  pl.BlockSpec((None, GLA_DV, GLA_HDK), lambda s: (jnp.maximum(rt(s) * G - 1, 0), 0, 0)),
                  pl.BlockSpec((T, GLA_DV), lambda s: (rt(s), 0)), pl.BlockSpec(memory_space=pl.ANY)],
        out_specs=[pl.BlockSpec((T, 2 * GLA_DK + 2 * GLA_DV), lambda s: (rt(s), 0)),
                   pl.BlockSpec((T, FLOW_W), lambda s: (rt(s), 0)),
                   pl.BlockSpec(wfu.shape, lambda s: (0, 0)), pl.BlockSpec(bf.shape, lambda s: (0, 0)),
                   pl.BlockSpec(gn.shape, lambda s: (0, 0))],
        out_shape=[_sds(dmain.shape, dmain.dtype), _sds((S, FLOW_W), BF16), _sds(wfu.shape), _sds(bf.shape),
                   _sds(gn.shape)],
        scratch=[pltpu.VMEM((GLA_DV, GLA_HDK), F32)], aliases={11: 0},
    )(proj, proj, proj, proj, proj, wfu, bf, gn, states, states, d_ya_in, dmain)


def _pool_counts(tm, i):
    t = (lax.broadcasted_iota(jnp.int32, (tm, POOL_GD), 0) + i * tm + 1).astype(F32)
    return [jnp.minimum(t, float(w)) for w in POOL_WINDOWS]


def _pool_fwd(proj, w_pool, pool_scale, tm=1024):
    S = proj.shape[0]
    tm = min(tm, S // 2)
    col = _POOL_COL
    hb = tm // POOL_HALO

    def body(p_ref, halo_ref, wp_ref, sc_ref, mixed_ref, out_ref):
        i = pl.program_id(0)
        p = p_ref[...].astype(F32)
        halo = halo_ref[...].astype(F32) * (i > 0).astype(F32)
        ext = jnp.concatenate([halo, p], axis=0)
        n = tm + POOL_HALO
        sums, acc, k = {}, ext, 1
        while k < POOL_WINDOWS[-1]:
            acc = acc + pltpu.roll(acc, k, axis=0)
            k *= 2
            sums[k] = acc
        cnts = _pool_counts(tm, i)
        mixed, lin = [], []
        for g, w in enumerate(POOL_WINDOWS):
            ls = slice(g * POOL_GD, (g + 1) * POOL_GD)
            m = sums[w][POOL_HALO:n, ls] / cnts[g] - p[:, ls]
            mixed.append(m)
            lin.append(_dot(m, wp_ref[g]))
        mixed_ref[...] = jnp.concatenate(mixed, axis=-1)
        out_ref[...] = (jnp.concatenate(lin, axis=-1) * sc_ref[...]).astype(out_ref.dtype)

    return _pcall(
        body, name="pool_fwd", grid=(S // tm,),
        in_specs=[pl.BlockSpec((tm, POOL_W), lambda i: (i, col)),
                  pl.BlockSpec((POOL_HALO, POOL_W), lambda i: (jnp.maximum(i * hb - 1, 0), col)),
                  pl.BlockSpec(w_pool.shape, lambda i: (0, 0, 0)), pl.BlockSpec(pool_scale.shape, lambda i: (0, 0))],
        out_specs=[pl.BlockSpec((tm, POOL_W), lambda i: (i, 0)), pl.BlockSpec((tm, POOL_W), lambda i: (i, 0))],
        out_shape=[_sds((S, POOL_W)), _sds((S, POOL_W), BF16)])(proj, proj, w_pool, pool_scale)


def _pool_lin_bwd(dout, mixed, w_pool, pool_scale):
    S = dout.shape[0]

    def fn(dout, mixed, wp, sc):
        dlin = dout * sc
        dm, dwp, lin = [], [], []
        for g in range(len(POOL_WINDOWS)):
            ls = slice(g * POOL_GD, (g + 1) * POOL_GD)
            lin.append(_dot(mixed[:, ls], wp[g]))
            dm.append(_dot(dlin[:, ls], wp[g], _NT))
            dwp.append(_dot(mixed[:, ls], dlin[:, ls], _TN))
        dsc = jnp.sum(dout * jnp.concatenate(lin, axis=-1), axis=0, keepdims=True)
        return jnp.concatenate(dm, axis=-1), jnp.concatenate(dwp, axis=0), dsc

    return _rowwise("pool_lin_bwd", fn, [dout, mixed], [w_pool, pool_scale], [_sds((S, POOL_W))],
                    [_sds((len(POOL_WINDOWS) * POOL_GD, POOL_GD)), _sds((1, POOL_W))], tm=1024)


def _pool_win_bwd(dmixed, dmain, tm=1024):
    S = dmixed.shape[0]
    tm = min(tm, S // 2)
    nt = S // tm
    hb = tm // POOL_HALO

    def body(dm_ref, halo_ref, dmain_in, dp_ref):
        i = pl.program_id(0)
        dm = dm_ref[...]
        halo = halo_ref[...] * (i < nt - 1).astype(F32)
        cnts = _pool_counts(tm, i)
        cnts_h = [c[:POOL_HALO] for c in _pool_counts(tm, i + 1)]
        r = jnp.concatenate([jnp.concatenate([dm[:, g * POOL_GD:(g + 1) * POOL_GD] / cnts[g] for g in range(4)], axis=-1),
                             jnp.concatenate([halo[:, g * POOL_GD:(g + 1) * POOL_GD] / cnts_h[g] for g in range(4)], axis=-1)],
                            axis=0)
        n = tm + POOL_HALO
        sums, acc, k = {}, r, 1
        while k < POOL_WINDOWS[-1]:
            acc = acc + pltpu.roll(acc, n - k, axis=0)
            k *= 2
            sums[k] = acc
        dp = [sums[w][:tm, g * POOL_GD:(g + 1) * POOL_GD] for g, w in enumerate(POOL_WINDOWS)]
        dp_ref[...] = (jnp.concatenate(dp, axis=-1) - dm).astype(dp_ref.dtype)

    return _pcall(
        body, name="pool_win_bwd", grid=(nt,),
        in_specs=[pl.BlockSpec((tm, POOL_W), lambda i: (i, 0)),
                  pl.BlockSpec((POOL_HALO, POOL_W), lambda i: (jnp.minimum((i + 1) * hb, S // POOL_HALO - 1), 0)),
                  pl.BlockSpec(memory_space=pl.ANY)],
        out_specs=pl.BlockSpec((tm, POOL_W), lambda i: (i, _POOL_COL)),
        out_shape=_sds(dmain.shape, dmain.dtype), aliases={2: 0})(dmixed, dmixed, dmain)


def _xattn_probs(q, kv, h):
    hs = slice(h * XA_HD, (h + 1) * XA_HD)
    s = _dot(q[:, hs], kv[:, hs], _NT) * (XA_HD ** -0.5)
    s = s - jnp.max(s, axis=-1, keepdims=True)
    e = jnp.exp(s)
    return e / jnp.sum(e, axis=-1, keepdims=True)


def _xattn_fwd(proj, kv):
    S = proj.shape[0]

    def fn(q, kv):
        outs = []
        for h in range(XA_HEADS):
            p = _xattn_probs(q, kv, h)
            outs.append(_dot(p, kv[:, XA_W + h * XA_HD:XA_W + (h + 1) * XA_HD]))
        return jnp.concatenate(outs, axis=-1)

    return _rowwise("xattn_fwd", fn, [_Win(proj, XA_W, _XQ_COL)], [kv], [_sds((S, XA_W), BF16)], tm=1024)[0]


def _xattn_bwd(proj, kv, dxa, dmain):
    def fn(q, dxa, kv):
        dqs, dks, dvs = [], [], []
        for h in range(XA_HEADS):
            hs = slice(h * XA_HD, (h + 1) * XA_HD)
            vh = kv[:, XA_W + h * XA_HD:XA_W + (h + 1) * XA_HD]
            p = _xattn_probs(q, kv, h)
            dp = _dot(dxa[:, hs], vh, _NT)
            ds = p * (dp - jnp.sum(p * dp, axis=-1, keepdims=True)) * (XA_HD ** -0.5)
            dqs.append(_dot(ds, kv[:, hs]))
            dks.append(_dot(ds, q[:, hs], _TN))
            dvs.append(_dot(p, dxa[:, hs], _TN))
        return jnp.concatenate(dqs, axis=-1), jnp.concatenate(dks + dvs, axis=-1)

    return _rowwise("xattn_bwd", fn, [_Win(proj, XA_W, _XQ_COL), dxa], [kv], [_Win(dmain, XA_W, _XQ_COL)],
                    [_sds(kv.shape)], tm=1024)


def _mix_fwd(x1, h, mem, W, g_next, on_proj=None, late_weights=None):
    S = x1.shape[0]
    M = mem.shape[0]
    tm = min(512, S)
    def proj_body(h_ref, w_ref, o_ref):
        hh = h_ref[...]
        for j in range(INT_NBLK):
            o_ref[:, INT_BLK * j:INT_BLK * (j + 1)] = _dot(hh, w_ref[INT_BLK * j:INT_BLK * (j + 1), :], _NT).astype(BF16)

    proj = _pcall(
        proj_body, name="mix_proj", grid=(S // tm,),
        in_specs=[pl.BlockSpec((tm, D_MODEL), lambda i: (i, 0)),
                  pl.BlockSpec((INT_W, D_MODEL), lambda i: (0, 0), pipeline_mode=pl.Buffered(1))],
        out_specs=pl.BlockSpec((tm, INT_W), lambda i: (i, 0)), out_shape=_sds((S, INT_W), BF16))(h, W["w_int_t"])
    ya_in, states = _gla_fwd(proj, W["w_fu_pad"], W["b_f"], W["gla_norm_g"],
                             on_proj(proj) if on_proj is not None else [])
    mixed, pool_out = _pool_fwd(proj, W["w_pool"], W["pool_scale"])
    mem_n = _rowwise("mem_norm", lambda m, g: _rms(m, g), [mem], [W["mem_norm_g"]], [_sds((M, D_MODEL), BF16)])[0]
    if late_weights is not None:
        W = {**W, **late_weights([pool_out])}
    kv = _mm_nn("mem_kv", mem_n, W["w_mem_kv"])
    xa = _xattn_fwd(proj, kv)
    def out_fn(gates, ya_in, pool_out, xa, x1, wa, wb, wc, wo, g, gn):
        ya, yb, yc = (_dot(t, w).astype(BF16) for t, w in ((ya_in, wa), (pool_out, wb), (xa, wc)))
        merged = _merge(gates, ya, yb, yc).astype(BF16)
        y = _dot(merged, wo)
        x2 = x1 + _rms(y, g)
        return ya, yb, yc, merged, y, x2, _rms(x2, gn)

    ya, yb, yc, merged, y, x2, h_next = _rowwise(
        "mix_out", out_fn, [_Win(proj, 3 * D_MODEL, 1), ya_in, pool_out, xa, x1],
        [W["w_up_gla"], W["w_up_pool"], W["w_up_xattn"], W["w_o"], W["mix_post_g"], g_next],
        [_sds((S, D_MODEL), BF16)] * 5 + [_sds((S, D_MODEL)), _sds((S, D_MODEL), BF16)])
    return x2, h_next, W, (h, proj, states, ya_in, mixed, pool_out, mem_n, kv, xa, ya, yb, yc, merged, y)


def _mix_bwd(dx2, x1, mem, W, saved, on_grads=None, on_up_grads=None):
    h, proj, states, ya_in, mixed, pool_out, mem_n, kv, xa, ya, yb, yc, merged, y = saved
    S = x1.shape[0]

    def out_bwd(y, dx2, gates, ya, yb, yc, g, wo, wa, wb, wc):
        dy, dg = _rms_bwd(y.astype(F32), g, dx2)
        dy = dy.astype(BF16)
        _, vjp = jax.vjp(_merge, gates, ya, yb, yc)
        dgates, dya, dyb, dyc = vjp(_dot(dy, wo, _NT))
        return (dy, dgates, dya, dyb, dyc, _dot(dya, wa, _NT), _dot(dyb, wb, _NT), _dot(dyc, wc, _NT), dg)

    dmain = lax.empty((S, INT_MAIN), BF16)
    dy, dmain, dya, dyb, dyc, d_ya_in, d_pool_out, d_xa, d_mix_post_g = _rowwise(
        "mix_out_bwd", out_bwd, [y, dx2, _Win(proj, 3 * D_MODEL, 1), ya, yb, yc],
        [W["mix_post_g"], W["w_o"], W["w_up_gla"], W["w_up_pool"], W["w_up_xattn"]],
        [_sds((S, D_MODEL), BF16), _Win(dmain, 3 * D_MODEL, 1)] + [_sds((S, D_MODEL), BF16)] * 4
        + [_sds((S, POOL_W), BF16), _sds((S, XA_W), BF16)], [_sds((1, D_MODEL))])
    d_w_o = _mm_tn("d_w_o", merged, dy, ts=S, tn_a=256)
    d_w_up_gla = _mm_tn("d_w_up_gla", ya_in, dya, ts=S, tn_a=256)
    d_w_up_pool = _mm_tn("d_w_up_pool", pool_out, dyb, ts=S, tn_a=128)
    d_w_up_xattn = _mm_tn("d_w_up_xattn", xa, dyc, ts=S, tn_a=128)
    up_grads = dict(w_up_gla=d_w_up_gla, w_up_pool=d_w_up_pool, w_up_xattn=d_w_up_xattn, w_o=d_w_o)
    up_deps = [on_up_grads(up_grads)] if on_up_grads is not None else []

    dmain, dflow, d_wfu_pad, d_b_f, d_gla_norm_g = _gla_bwd(proj, W["w_fu_pad"], W["b_f"], W["gla_norm_g"], states,
                                                            d_ya_in, dmain)
    dmixed, d_w_pool, d_pool_scale = _pool_lin_bwd(d_pool_out, mixed, W["w_pool"], W["pool_scale"])
    dmain = _pool_win_bwd(dmixed, dmain)
    dmain, dkv = _xattn_bwd(proj, kv, d_xa, dmain)
    d_w_mem_kv = _mm_tn("d_w_mem_kv", mem_n, dkv)
    dmem_n = _mm_nt("d_mem_n", dkv, W["w_mem_kv"])
    d_mem_norm_g = _rowwise("mem_norm_bwd", lambda m, d, g: _rms_bwd(m, g, d)[1], [mem, dmem_n], [W["mem_norm_g"]], [],
                            [_sds((1, D_MODEL))])[0]
    d_w_int_t = (_mm_tn("d_w_int", dmain, h, ts=S, tn_a=INT_MAIN // 8, deps=up_deps),
                 _mm_tn("d_w_flow", dflow, h, ts=S))
    grads = dict(
        w_int_t=d_w_int_t, w_fu_pad=d_wfu_pad, b_f=d_b_f, gla_norm_g=d_gla_norm_g, w_pool=d_w_pool,
        pool_scale=d_pool_scale, mem_norm_g=d_mem_norm_g, w_mem_kv=d_w_mem_kv, w_up_gla=d_w_up_gla,
        w_up_pool=d_w_up_pool, w_up_xattn=d_w_up_xattn, w_o=d_w_o, mix_post_g=d_mix_post_g)
    deps = [on_grads(grads)] if on_grads is not None else []
    dx1, grads["mix_pre_g"] = _dh_pre_norm_bwd("d_mix_h", [dmain, dflow], W["w_int_t"], x1, dx2, W["mix_pre_g"], deps)
    return dx1, grads


def _mesh_pos():
    x, y, c = lax.axis_index("x"), lax.axis_index("y"), lax.axis_index("c")
    return x, y, c, 4 * x + 2 * y + c


def _peer(x, y, c, r):
    px = 1 - x if r & 4 else x
    py = 1 - y if r & 2 else y
    pc = 1 - c if r & 1 else c
    return (px, py, pc), 4 * px + 2 * py + pc


_ALL_PEERS = tuple(range(1, N_DEV))
_SIBLING = 1
_SAME_CORE = (2, 4, 6)


def _dev_slot(ref, dev):
    return ref.at[dev]


def _zero_pad_rows(land):
    pad = FF_PAD - FF_BLK

    def body(land_in, o_ref):
        o_ref[...] = jnp.zeros_like(o_ref)

    return _pcall(body, name="zero_pad_rows", grid=(land.shape[0],), in_specs=[pl.BlockSpec(memory_space=pl.ANY)],
                  out_specs=pl.BlockSpec((None, pad, D_MODEL), lambda j: (j, FF_BLK // pad, 0)),
                  out_shape=_sds(land.shape, land.dtype), aliases={0: 0})(land)


class _Plan:
    def __init__(self, scatter, slots=None, shapes=None):
        self.scatter, self.slots, self.shapes = scatter, slots or {}, shapes or {}

    def src(self, srcs, a, dev):
        return self.slots.get(a, _dev_slot)(srcs[a], dev) if self.scatter else srcs[a]

    def dst(self, lands, a, dev):
        return lands[a].at[dev] if self.scatter else self.slots.get(a, _dev_slot)(lands[a], dev)

    def landing_zones(self, arrays):
        lands = []
        for a, arr in enumerate(arrays):
            if self.scatter:
                lands.append(lax.empty((N_DEV,) + tuple(self.shapes.get(a, arr.shape[1:])), arr.dtype))
            elif a in self.shapes:
                lands.append(_zero_pad_rows(lax.empty(self.shapes[a], arr.dtype)))
            else:
                lands.append(lax.empty((N_DEV,) + arr.shape, arr.dtype))
        return lands


_GATHER = _Plan(False)


def _peer_copies(srcs, lands, send_sems, recv_sems, plan, peers=_ALL_PEERS):
    x, y, c, me = _mesh_pos()
    cps = []
    for r in peers:
        pos, peer = _peer(x, y, c, r)
        for a in range(len(srcs)):
            k = a * (N_DEV - 1) + r - 1
            cps.append(pltpu.make_async_remote_copy(
                src_ref=plan.src(srcs, a, peer), dst_ref=plan.dst(lands, a, me),
                send_sem=send_sems.at[k], recv_sem=recv_sems.at[k], device_id=pos,
                device_id_type=pl.DeviceIdType.MESH))
    return cps


_HBM = pl.BlockSpec(memory_space=pltpu.HBM)
_SEM = pl.BlockSpec(memory_space=pltpu.SEMAPHORE)
_EFFECT = pltpu.SideEffectType.DATAFLOW_SIDE_EFFECTING


def _own_copies(srcs, lands, own_sems, plan):
    me = _mesh_pos()[3]
    return [pltpu.make_async_copy(plan.src(srcs, a, me), plan.dst(lands, a, me), own_sems.at[a])
            for a in range(len(srcs))]


def _exchange_start(name, arrays, plan):
    n = len(arrays)
    lands = plan.landing_zones(arrays)
    n_sem = n * (N_DEV - 1)

    def body(*refs):
        srcs, lands_ = refs[:n], refs[n:2 * n]
        send_sems, recv_sems, own_sems = refs[2 * n:2 * n + 3]
        token = refs[-1]
        for cp in _peer_copies(srcs, lands_, send_sems, recv_sems, plan) + _own_copies(srcs, lands_, own_sems, plan):
            cp.start()
        token[...] = jnp.zeros_like(token)

    hbm = lambda a: pltpu.HBM(a.shape, a.dtype)
    res = pl.pallas_call(
        body, name=name,
        out_shape=(pltpu.SemaphoreType.DMA((n_sem,)), pltpu.SemaphoreType.DMA((n_sem,)), pltpu.SemaphoreType.DMA((n,)),
                   *[hbm(a) for a in arrays], *[hbm(a) for a in lands], _sds((8, 128))),
        in_specs=[_HBM] * (2 * n),
        out_specs=(_SEM, _SEM, _SEM, *[_HBM] * (2 * n), pl.BlockSpec(memory_space=pltpu.VMEM)),
        input_output_aliases={i: 3 + i for i in range(2 * n)},
        compiler_params=pltpu.CompilerParams(has_side_effects=_EFFECT),
    )(*[pltpu.with_memory_space_constraint(a, pltpu.HBM) for a in list(arrays) + lands])
    return (res[:3], res[3:3 + n], res[3 + n:3 + 2 * n], plan), res[-1]


def _exchange_wait(name, state, after):
    sems, srcs, lands, plan = state
    n = len(srcs)

    def body(*refs):
        srcs_, lands_ = refs[:n], refs[n:2 * n]
        send_sems, recv_sems, own_sems = refs[2 * n:2 * n + 3]
        for cp in _peer_copies(srcs_, lands_, send_sems, recv_sems, plan):
            cp.wait_send()
            cp.wait_recv()
        for cp in _own_copies(srcs_, lands_, own_sems, plan):
            cp.wait()

    hbm = lambda a: pltpu.HBM(a.shape, a.dtype)
    res = pl.pallas_call(
        body, name=name, out_shape=tuple(hbm(a) for a in list(srcs) + list(lands)),
        in_specs=[_HBM] * (2 * n) + [_SEM] * 3 + [pl.BlockSpec(memory_space=pl.ANY)] * len(after),
        out_specs=tuple([_HBM] * (2 * n)), input_output_aliases={i: i for i in range(2 * n)},
        compiler_params=pltpu.CompilerParams(has_side_effects=_EFFECT),
    )(*srcs, *lands, *sems, *after)
    return res[n:]


def _gather_start(name, arrays, plan, after):
    n = len(arrays)
    lands = plan.landing_zones(arrays)
    n_sem = n * (N_DEV - 1)

    def body(*refs):
        srcs, lands_ = refs[:n], refs[n:2 * n]
        send_sems, recv_sems, own_sems = refs[2 * n + len(after):2 * n + len(after) + 3]
        token = refs[-1]
        for cp in (_peer_copies(srcs, lands_, send_sems, recv_sems, plan, (_SIBLING,) + _SAME_CORE)
                   + _own_copies(srcs, lands_, own_sems, plan)):
            cp.start()
        token[...] = jnp.zeros_like(token)

    hbm = lambda a: pltpu.HBM(a.shape, a.dtype)
    res = pl.pallas_call(
        body, name=name,
        out_shape=(pltpu.SemaphoreType.DMA((n_sem,)), pltpu.SemaphoreType.DMA((n_sem,)), pltpu.SemaphoreType.DMA((n,)),
                   *[hbm(a) for a in arrays], *[hbm(a) for a in lands], _sds((8, 128))),
        in_specs=[_HBM] * (2 * n) + [pl.BlockSpec(memory_space=pl.ANY)] * len(after),
        out_specs=(_SEM, _SEM, _SEM, *[_HBM] * (2 * n), pl.BlockSpec(memory_space=pltpu.VMEM)),
        input_output_aliases={i: 3 + i for i in range(2 * n)},
        compiler_params=pltpu.CompilerParams(has_side_effects=_EFFECT),
    )(*[pltpu.with_memory_space_constraint(a, pltpu.HBM) for a in list(arrays) + lands], *after)
    return (res[:3], res[3:3 + n], res[3 + n:3 + 2 * n], plan), res[-1]


def _pass_on_copies(lands, send_sems, recv_sems, plan):
    x, y, c, _ = _mesh_pos()
    sibling = _peer(x, y, c, _SIBLING)[0]
    cps = []
    for i, r in enumerate(_SAME_CORE):
        owner = _peer(x, y, c, r)[1]
        for a in range(len(lands)):
            k = a * len(_SAME_CORE) + i
            cps.append(pltpu.make_async_remote_copy(
                src_ref=plan.dst(lands, a, owner), dst_ref=plan.dst(lands, a, owner), send_sem=send_sems.at[k],
                recv_sem=recv_sems.at[k], device_id=sibling, device_id_type=pl.DeviceIdType.MESH))
    return cps


def _gather_pass_on(name, state, after):
    sems, srcs, lands, plan = state
    n = len(srcs)
    n_sem = n * len(_SAME_CORE)

    def body(*refs):
        srcs_, lands_ = refs[:n], refs[n:2 * n]
        send_sems, recv_sems = refs[2 * n], refs[2 * n + 1]
        on_send, on_recv = refs[2 * n + 3 + len(after)], refs[2 * n + 4 + len(after)]
        token = refs[-1]
        arrivals = _peer_copies(srcs_, lands_, send_sems, recv_sems, plan, _SAME_CORE)
        for arrived, on in zip(arrivals, _pass_on_copies(lands_, on_send, on_recv, plan)):
            arrived.wait_recv()
            on.start()
        token[...] = jnp.zeros_like(token)

    hbm = lambda a: pltpu.HBM(a.shape, a.dtype)
    res = pl.pallas_call(
        body, name=name,
        out_shape=(pltpu.SemaphoreType.DMA((n_sem,)), pltpu.SemaphoreType.DMA((n_sem,)),
                   *[hbm(a) for a in list(srcs) + list(lands)], _sds((8, 128))),
        in_specs=[_HBM] * (2 * n) + [_SEM] * 3 + [pl.BlockSpec(memory_space=pl.ANY)] * len(after),
        out_specs=(_SEM, _SEM, *[_HBM] * (2 * n), pl.BlockSpec(memory_space=pltpu.VMEM)),
        input_output_aliases={i: 2 + i for i in range(2 * n)},
        compiler_params=pltpu.CompilerParams(has_side_effects=_EFFECT),
    )(*srcs, *lands, *sems, *after)
    return (sems, res[:2], res[2:2 + n], res[2 + n:2 + 2 * n], plan), res[-1]


def _gather_wait(name, state, after):
    sems, on_sems, srcs, lands, plan = state
    n = len(srcs)

    def body(*refs):
        srcs_, lands_ = refs[:n], refs[n:2 * n]
        send_sems, recv_sems, own_sems, on_send, on_recv = refs[2 * n:2 * n + 5]
        for cp in _peer_copies(srcs_, lands_, send_sems, recv_sems, plan, (_SIBLING,)):
            cp.wait_recv()
        for cp in _peer_copies(srcs_, lands_, send_sems, recv_sems, plan, (_SIBLING,) + _SAME_CORE):
            cp.wait_send()
        for cp in _own_copies(srcs_, lands_, own_sems, plan):
            cp.wait()
        for cp in _pass_on_copies(lands_, on_send, on_recv, plan):
            cp.wait_send()
            cp.wait_recv()

    hbm = lambda a: pltpu.HBM(a.shape, a.dtype)
    res = pl.pallas_call(
        body, name=name, out_shape=tuple(hbm(a) for a in list(srcs) + list(lands)),
        in_specs=[_HBM] * (2 * n) + [_SEM] * 5 + [pl.BlockSpec(memory_space=pl.ANY)] * len(after),
        out_specs=tuple([_HBM] * (2 * n)), input_output_aliases={i: i for i in range(2 * n)},
        compiler_params=pltpu.CompilerParams(has_side_effects=_EFFECT),
    )(*srcs, *lands, *sems, *on_sems, *after)
    return res[n:]


def _sum_parts(name, recv):
    def body(r_ref, o_ref):
        s = r_ref[0]
        for j in range(1, N_DEV):
            s = s + r_ref[j]
        o_ref[...] = s

    return pl.pallas_call(body, name=name, out_shape=_sds(recv.shape[1:], recv.dtype))(recv)


def _adam(name, recv, w, m, v, col=0):
    shape = w.shape
    R, C = shape
    tr = R
    while N_DEV * tr * C * 4 > 6 * 1024 * 1024 and tr % 32 == 0:
        tr //= 2

    def body(recv_ref, w_ref, m_ref, v_ref, g_ref, d_ref, m2_ref, v2_ref):
        g = recv_ref[0].astype(F32)
        for j in range(1, N_DEV):
            g = g + recv_ref[j].astype(F32)
        g_ref[...], d_ref[...], m2_ref[...], v2_ref[...] = _adam_math(g, w_ref[...], m_ref[...], v_ref[...])

    blk = pl.BlockSpec((tr, C), lambda i: (i, 0))
    return _pcall(body, name=name, grid=(R // tr,),
                  in_specs=[pl.BlockSpec((N_DEV, tr, C), lambda i: (0, i, col)), blk, blk, blk],
                  out_specs=[blk] * 4, out_shape=[_sds(shape)] * 4)(recv, w, m, v)


def _adam_math(g, w_, m_, v_):
    m2 = ADAM_B1 * m_ + (1.0 - ADAM_B1) * g
    v2 = ADAM_B2 * v_ + (1.0 - ADAM_B2) * (g * g)
    m_hat = m2 / (1.0 - ADAM_B1 ** ADAM_STEP)
    v_hat = v2 / (1.0 - ADAM_B2 ** ADAM_STEP)
    return g, -ADAM_LR * (m_hat / (jnp.sqrt(v_hat) + ADAM_EPS) + ADAM_WD * w_), m2, v2


def _adam_rows(name, recv, ws, ms, vs, offs):
    k = len(ws)

    def body(recv_ref, *refs):
        outs = refs[3 * k:]
        for a in range(k):
            cols = slice(offs[a], offs[a] + ws[a].shape[1])
            g = recv_ref[0, :, cols].astype(F32)
            for j in range(1, N_DEV):
                g = g + recv_ref[j, :, cols].astype(F32)
            res = _adam_math(g, refs[a][...], refs[k + a][...], refs[2 * k + a][...])
            for t in range(4):
                outs[4 * a + t][...] = res[t]

    def whole(a):
        return pl.BlockSpec(a.shape, lambda i: (0,) * a.ndim)

    rows = list(ws) + list(ms) + list(vs)
    return _pcall(body, name=name, grid=(1,), in_specs=[whole(recv)] + [whole(a) for a in rows],
                  out_specs=[whole(w) for w in ws for _ in range(4)],
                  out_shape=[_sds(w.shape) for w in ws for _ in range(4)])(recv, *rows)


_NAMES = ['ffn1_pre_g', 'ffn1_w_in', 'ffn1_w_out', 'ffn1_post_g', 'mix_pre_g', 'w_in', 'w_fu', 'b_f', 'gla_norm_g',
          'w_pool', 'pool_scale', 'mem_norm_g', 'w_mem_kv', 'w_up_gla', 'w_up_pool', 'w_up_xattn', 'w_o', 'mix_post_g',
          'ffn2_pre_g', 'ffn2_w_in', 'ffn2_w_out', 'ffn2_post_g', 'final_g']
_SHARDED = ['ffn1_w_in', 'ffn1_w_out', 'w_in', 'w_fu', 'w_mem_kv', 'w_up_gla', 'w_up_pool', 'w_up_xattn', 'w_o',
            'ffn2_w_in', 'ffn2_w_out']
_COL_SHARDED = ['w_up_pool', 'w_up_xattn']


def _cols_to_full(g):
    return jnp.transpose(g, (1, 0, 2)).reshape(g.shape[1], N_DEV * g.shape[2])


def _full_to_cols(f):
    R, C = f.shape
    return jnp.transpose(f.reshape(R, N_DEV, C // N_DEV), (1, 0, 2))


def _to_internal(w_in_t):
    o = 0
    parts = []
    for s in IN_SPLITS:
        parts.append(w_in_t[o:o + s])
        o += s
    q, k, v, g_out, f_low, p_in, xq, gates = parts
    f_low = jnp.pad(f_low, ((0, FLOW_W - GATE_RANK), (0, 0)))
    return jnp.concatenate([q, k, v, g_out, gates, p_in, xq, f_low], axis=0)


def _from_internal(d, d_flow):
    q, k, v, g_out = d[0:512], d[512:1024], d[1024:2048], d[2048:3072]
    gates, p_in, xq = d[3072:6144], d[6144:6656], d[6656:7168]
    return jnp.concatenate([q, k, v, g_out, d_flow[:GATE_RANK], p_in, xq, gates], axis=0)


_FFN_IN = ('ffn1_w_in', 'ffn2_w_in')
_FFN_OUT = ('ffn1_w_out', 'ffn2_w_out')
_GATHERS = {"ffn1_in": ['ffn1_w_in'], "ffn1_out": ['ffn1_w_out'],
            "mix_in": ['w_in', 'w_fu'], "mix_rest": ['w_mem_kv', 'w_up_gla', 'w_up_pool', 'w_up_xattn', 'w_o'],
            "ffn2": ['ffn2_w_in', 'ffn2_w_out']}
_MIX = _GATHERS["mix_in"] + _GATHERS["mix_rest"]
_MIX_UP = ['w_up_gla', 'w_up_pool', 'w_up_xattn', 'w_o']
_MIX_LATE = [n for n in _MIX if n not in _MIX_UP]


def _ffn_in_slot(ref, d):
    return ref.at[2 * (d % N_FF_BLK) + d // N_FF_BLK, pl.ds(0, FF_BLK)]


def _ffn_out_slot(ref, d):
    rows = FF_BLK // 2
    return ref.at[d // 2, pl.ds(pl.multiple_of((d % 2) * rows, rows), rows)]


def _ffn_plan(names, scatter):
    slots, shapes = {}, {}
    for a, n in enumerate(names):
        if n in _FFN_IN:
            slots[a] = _ffn_in_slot
            shapes[a] = (FF_BLK, D_MODEL) if scatter else (N_DEV, FF_PAD, D_MODEL)
        elif n in _FFN_OUT:
            slots[a] = _ffn_out_slot
            shapes[a] = (FF_BLK // 2, D_MODEL) if scatter else (N_FF_BLK, FF_PAD, D_MODEL)
    return _Plan(scatter, slots, shapes)


def _rows_to_full(g):
    return g.reshape(N_DEV * g.shape[1], g.shape[2])


def _mix_weights(gathered):
    W = {}
    for n, g in gathered.items():
        if n == "w_in":
            W["w_int_t"] = _to_internal(g.reshape(IN_WIDTH, D_MODEL))
        elif n == "w_fu":
            W["w_fu_pad"] = jnp.pad(_cols_to_full(g), ((0, FLOW_W - GATE_RANK), (0, 0)))
        else:
            W[n] = _cols_to_full(g) if n in _COL_SHARDED else _rows_to_full(g)
    return W


def _mix_chunks(G, n):
    if n == "w_in":
        return _from_internal(*G["w_int_t"]).reshape(N_DEV, IN_SHARD, D_MODEL)
    if n == "w_fu":
        return _full_to_cols(G["w_fu_pad"][:GATE_RANK].astype(BF16))
    if n in _COL_SHARDED:
        return _full_to_cols(G[n])
    return G[n].reshape(N_DEV, G[n].shape[0] // N_DEV, G[n].shape[1])


def _step(x, mem, tgt, P, Mo, Vo):
    def native(n, a):
        return jnp.swapaxes(a, 0, 1) if n in _FFN_IN + ("w_in",) else a

    P, Mo, Vo = ({n: native(n, a) for n, a in d.items()} for d in (P, Mo, Vo))
    small = {n: P[n] for n in _NAMES if n not in _SHARDED}

    gather, tokens = {}, []
    for grp, names in _GATHERS.items():
        gather[grp], tok = _gather_start("gather_" + grp, [P[n].astype(BF16) for n in names], _ffn_plan(names, False),
                                         tokens[-1:])
        tokens.append(tok)

    def pass_on(grp, after):
        return _gather_pass_on("gather_" + grp + "_on", gather[grp], after)

    def gathered(grp, after):
        return _gather_wait("gather_" + grp + "_wait", pass_on(grp, after)[0], [])

    def ffn_wo(lands):
        return lambda act: lands[0].reshape(N_FF_BLK * FF_PAD, D_MODEL)

    h1 = _pre_norm("ffn1_pre", x, small["ffn1_pre_g"], tokens[:1])
    w1t = gathered("ffn1_in", tokens[-1:] + [h1])[0].reshape(N_DEV * FF_PAD, D_MODEL)
    (x1, hm), w1o, sv1 = _ffn_fwd("ffn1", x, h1, w1t, small["ffn1_post_g"],
                                  lambda act: ffn_wo(gathered("ffn1_out", [act]))(act), small["mix_pre_g"])

    Wm = {**small, **_mix_weights(dict(zip(_GATHERS["mix_in"], gathered("mix_in", [x1]))))}
    passed = {}

    def pass_on_later(proj):
        tokens = []
        for grp in ("mix_rest", "ffn2"):
            passed[grp], tok = pass_on(grp, [proj])
            tokens.append(tok)
        return tokens

    def mix_rest(after):
        names = _GATHERS["mix_rest"]
        return _mix_weights(dict(zip(names, _gather_wait("gather_mix_rest_wait", passed["mix_rest"], after))))

    x2, h2, Wm, svm = _mix_fwd(x1, hm, mem, Wm, small["ffn2_pre_g"], pass_on_later, mix_rest)
    w2t, w2o = _gather_wait("gather_ffn2_wait", passed["ffn2"], [x2])
    w2t = w2t.reshape(N_DEV * FF_PAD, D_MODEL)
    (loss, dx3, d_final_g), w2o, sv2 = _ffn_fwd("ffn2", x2, h2, w2t, small["ffn2_post_g"], ffn_wo([w2o]),
                                                small["final_g"], tgt)

    G = dict(final_g=d_final_g)
    scat = {}

    def start(grp, names, arrays):
        scat[grp] = names, _exchange_start("scatter_" + grp, arrays, _ffn_plan(names, True))
        return scat[grp][1][1]

    def ffn_starts(tag):
        return (lambda dwo: start(tag + "_out", [tag + "_w_out"], [dwo.reshape(N_FF_BLK, FF_PAD, D_MODEL)]),
                lambda dwt: start(tag + "_in", [tag + "_w_in"], [dwt.reshape(N_DEV, FF_PAD, D_MODEL)]))

    dx2, G["ffn2_pre_g"], G["ffn2_post_g"] = _ffn_bwd(
        "ffn2", dx3, x2, small["ffn2_pre_g"], w2t, w2o, small["ffn2_post_g"], sv2, *ffn_starts("ffn2"))
    dx1, Gm = _mix_bwd(dx2, x1, mem, Wm, svm,
                       lambda Gm: start("mix", _MIX_LATE, [_mix_chunks(Gm, n) for n in _MIX_LATE]),
                       lambda Gm: start("mix_up", _MIX_UP, [_mix_chunks(Gm, n) for n in _MIX_UP]))
    G.update(Gm)
    dx, G["ffn1_pre_g"], G["ffn1_post_g"] = _ffn_bwd(
        "ffn1", dx1, x, small["ffn1_pre_g"], w1t, w1o, small["ffn1_post_g"], sv1, *ffn_starts("ffn1"))

    vecs = sorted((n for n in small if P[n].ndim == 2), key=lambda n: -P[n].shape[1])
    packed = jnp.concatenate([G[n] for n in vecs] + [loss], axis=1)
    place, o = {}, 0
    for n in vecs:
        place[n] = o // P[n].shape[1]
        o += P[n].shape[1]
    scat["small"] = ["packed", "w_pool"], _exchange_start("gather_small_grads", [packed, G["w_pool"]], _GATHER)
    recv, outs = {}, {}
    done = [dx]
    for grp in ["ffn2_out", "ffn2_in", "mix_up", "mix", "ffn1_out", "ffn1_in", "small"]:
        names, (state, _) = scat[grp]
        recv.update(zip(names, _exchange_wait(("gather_" if grp == "small" else "scatter_") + grp + "_wait", state,
                                              done)))
        if grp == "small":
            loss = _sum_parts("loss_sum", recv["packed"][:, :, o:])[0, 0]
            names = [n for n in small if n not in place]
            res = _adam_rows("adam_rows", recv["packed"], [P[n] for n in vecs], [Mo[n] for n in vecs],
                             [Vo[n] for n in vecs], [place[n] * P[n].shape[1] for n in vecs])
            for a, n in enumerate(vecs):
                outs[n] = list(res[4 * a:4 * a + 4])
            done.append(res[-1])
        for n in names:
            shp = P[n].shape
            shp2 = shp if len(shp) == 2 else (shp[0] * shp[1], shp[2])
            r = recv["packed"] if n in place else recv[n].reshape((N_DEV,) + shp2)
            res = _adam("adam_" + n, r, P[n].reshape(shp2), Mo[n].reshape(shp2), Vo[n].reshape(shp2), place.get(n, 0))
            outs[n] = [native(n, t.reshape(shp)) for t in res]
            done.append(res[-1])
    return loss, dx, outs


def kernel(x, mem, ffn1_pre_g, ffn1_w_in, ffn1_w_out, ffn1_post_g, mix_pre_g, w_in, w_fu, b_f, gla_norm_g, w_pool, pool_scale, mem_norm_g, w_mem_kv, w_up_gla, w_up_pool, w_up_xattn, w_o, mix_post_g, ffn2_pre_g, ffn2_w_in, ffn2_w_out, ffn2_post_g, final_g, loss_target, m_ffn1_pre_g, m_ffn1_w_in, m_ffn1_w_out, m_ffn1_post_g, m_mix_pre_g, m_w_in, m_w_fu, m_b_f, m_gla_norm_g, m_w_pool, m_pool_scale, m_mem_norm_g, m_w_mem_kv, m_w_up_gla, m_w_up_pool, m_w_up_xattn, m_w_o, m_mix_post_g, m_ffn2_pre_g, m_ffn2_w_in, m_ffn2_w_out, m_ffn2_post_g, m_final_g, v_ffn1_pre_g, v_ffn1_w_in, v_ffn1_w_out, v_ffn1_post_g, v_mix_pre_g, v_w_in, v_w_fu, v_b_f, v_gla_norm_g, v_w_pool, v_pool_scale, v_mem_norm_g, v_w_mem_kv, v_w_up_gla, v_w_up_pool, v_w_up_xattn, v_w_o, v_mix_post_g, v_ffn2_pre_g, v_ffn2_w_in, v_ffn2_w_out, v_ffn2_post_g, v_final_g):
    params = [ffn1_pre_g, ffn1_w_in, ffn1_w_out, ffn1_post_g, mix_pre_g, w_in, w_fu, b_f, gla_norm_g, w_pool, pool_scale, mem_norm_g, w_mem_kv, w_up_gla, w_up_pool, w_up_xattn, w_o, mix_post_g, ffn2_pre_g, ffn2_w_in, ffn2_w_out, ffn2_post_g, final_g]
    moms = [m_ffn1_pre_g, m_ffn1_w_in, m_ffn1_w_out, m_ffn1_post_g, m_mix_pre_g, m_w_in, m_w_fu, m_b_f, m_gla_norm_g, m_w_pool, m_pool_scale, m_mem_norm_g, m_w_mem_kv, m_w_up_gla, m_w_up_pool, m_w_up_xattn, m_w_o, m_mix_post_g, m_ffn2_pre_g, m_ffn2_w_in, m_ffn2_w_out, m_ffn2_post_g, m_final_g]
    vars_ = [v_ffn1_pre_g, v_ffn1_w_in, v_ffn1_w_out, v_ffn1_post_g, v_mix_pre_g, v_w_in, v_w_fu, v_b_f, v_gla_norm_g, v_w_pool, v_pool_scale, v_mem_norm_g, v_w_mem_kv, v_w_up_gla, v_w_up_pool, v_w_up_xattn, v_w_o, v_mix_post_g, v_ffn2_pre_g, v_ffn2_w_in, v_ffn2_w_out, v_ffn2_post_g, v_final_g]
    P = {n: a[0] if a.ndim > 2 else a for n, a in zip(_NAMES, params)}
    Mo = {n: a[0] if a.ndim > 2 else a for n, a in zip(_NAMES, moms)}
    Vo = {n: a[0] if a.ndim > 2 else a for n, a in zip(_NAMES, vars_)}
    loss, dx, outs = _step(x[0], mem[0], loss_target[0], P, Mo, Vo)
    out = [loss, dx[None]]
    for kind in range(4):
        for n, p in zip(_NAMES, params):
            out.append(outs[n][kind].reshape(p.shape))
    return tuple(out)
```

```python
import functools

import jax
import jax.numpy as jnp
from jax import lax
from jax.experimental import pallas as pl
from jax.experimental.pallas import tpu as pltpu

F32 = jnp.float32
BF16 = jnp.bfloat16

N_DEV = 8
D_MODEL = 1024
D_FF = 2816
FF_BLK = 2 * D_FF // N_DEV
N_FF_BLK = D_FF // FF_BLK
FF_PAD = 768
CHUNK = 64
GLA_HEADS = 4
GLA_DK = 512
GLA_DV = 1024
GLA_HDK = 128
GLA_HDV = 256
GATE_RANK = 16
GATE_TEMP = 16.0
POOL_WINDOWS = (2, 4, 8, 16)
POOL_W = 512
POOL_GD = 128
POOL_HALO = 16
XA_HEADS = 4
XA_HD = 128
XA_W = 512
EPS = 1e-6
IN_SPLITS = (GLA_DK, GLA_DK, GLA_DV, GLA_DV, GATE_RANK, POOL_W, XA_W, 3 * D_MODEL)
IN_WIDTH = sum(IN_SPLITS)
IN_SHARD = IN_WIDTH // N_DEV
INT_W = 3072 + 3072 + 1024 + 128
INT_NBLK = 3
INT_BLK = INT_W // INT_NBLK
FLOW_W = 128
INT_MAIN = INT_W - FLOW_W
_POOL_COL = (3072 + 3072) // POOL_W
_XQ_COL = (3072 + 3072 + POOL_W) // XA_W

ADAM_LR = 0.001
ADAM_B1 = 0.9
ADAM_B2 = 0.999
ADAM_EPS = 1e-08
ADAM_WD = 0.01
ADAM_STEP = 10

VMEM_LIMIT = 56 * 1024 * 1024

_NN = (((1,), (0,)), ((), ()))
_NT = (((1,), (1,)), ((), ()))
_TN = (((0,), (0,)), ((), ()))


def _pcall(body, *, name, grid, in_specs, out_specs, out_shape, scratch=(), aliases=None):
    return pl.pallas_call(
        body, name=name, grid=grid, in_specs=in_specs, out_specs=out_specs, out_shape=out_shape,
        scratch_shapes=list(scratch), input_output_aliases=aliases or {},
        compiler_params=pltpu.CompilerParams(dimension_semantics=("arbitrary",) * len(grid),
                                             vmem_limit_bytes=VMEM_LIMIT))


def _dot(a, b, dims=_NN):
    return lax.dot_general(a.astype(BF16), b.astype(BF16), dims, preferred_element_type=F32)


def _mm(name, a, b, *, grid, a_blk, a_map, b_blk, b_map, o_shape, o_blk, o_map, dims, out_dtype=F32, deps=()):
    nk = grid[2]

    def body(a_ref, b_ref, *rest):
        o_ref, scr = rest[len(deps)], rest[len(deps) + 1:]
        p = _dot(a_ref[...], b_ref[...], dims)
        if nk == 1:
            o_ref[...] = p.astype(o_ref.dtype)
        else:
            acc = scr[0]
            k = pl.program_id(2)

            @pl.when(k == 0)
            def _():
                acc[...] = p

            @pl.when(k > 0)
            def _():
                acc[...] += p

            @pl.when(k == nk - 1)
            def _():
                o_ref[...] = acc[...].astype(o_ref.dtype)

    acc_shape = tuple(d for d in o_blk if d is not None)
    return _pcall(body, name=name, grid=grid,
                  in_specs=[pl.BlockSpec(a_blk, a_map), pl.BlockSpec(b_blk, b_map)]
                  + [pl.BlockSpec(memory_space=pl.ANY)] * len(deps),
                  out_specs=pl.BlockSpec(o_blk, o_map),
                  out_shape=jax.ShapeDtypeStruct(o_shape, out_dtype),
                  scratch=[pltpu.VMEM(acc_shape, F32)] if nk > 1 else [])(a, b, *deps)


def _mm_nn(name, a, b, out_dtype=F32, tm=1024, tn=None):
    M, K = a.shape
    N = b.shape[1]
    tm, tn = min(tm, M), (tn or N)
    return _mm(name, a, b, grid=(N // tn, M // tm, 1), a_blk=(tm, K), a_map=lambda j, i, k: (i, 0),
               b_blk=(K, tn), b_map=lambda j, i, k: (0, j), o_shape=(M, N), o_blk=(tm, tn),
               o_map=lambda j, i, k: (i, j), dims=_NN, out_dtype=out_dtype)


def _mm_nt(name, a, b, out_dtype=F32, tm=1024):
    M, K = a.shape
    N = b.shape[0]
    tm = min(tm, M)
    return _mm(name, a, b, grid=(1, M // tm, 1), a_blk=(tm, K), a_map=lambda j, i, k: (i, 0),
               b_blk=(N, K), b_map=lambda j, i, k: (0, 0), o_shape=(M, N), o_blk=(tm, N),
               o_map=lambda j, i, k: (i, 0), dims=_NT, out_dtype=out_dtype)


def _mm_tn(name, a, b, out_dtype=BF16, ts=512, tn=None, tn_a=None, deps=()):
    S, M = a.shape
    N = b.shape[1]
    ts, tn, tn_a = min(ts, S), (tn or N), (tn_a or M)
    return _mm(name, a, b, grid=((N // tn) * (M // tn_a), 1, S // ts), a_blk=(ts, tn_a),
               a_map=lambda j, i, k: (k, j if tn_a < M else 0), b_blk=(ts, tn),
               b_map=lambda j, i, k: (k, j if tn < N else 0), o_shape=(M, N), o_blk=(tn_a, tn),
               o_map=lambda j, i, k: (j, 0) if tn_a < M else (0, j), dims=_TN, out_dtype=out_dtype, deps=deps)


class _Win:
    def __init__(self, arr, w, c):
        self.arr, self.w, self.c = arr, w, c


def _row_spec(x, tm):
    if isinstance(x, _Win):
        return x.arr, pl.BlockSpec((tm, x.w), functools.partial(lambda i, c: (i, c), c=x.c))
    if x.ndim == 3:
        return x, pl.BlockSpec((x.shape[0], tm, x.shape[2]), lambda i: (0, i, 0))
    return x, pl.BlockSpec((tm, x.shape[1]), lambda i: (i, 0))


def _rowwise(name, fn, rows, consts, outs, accs=(), tm=512):
    first = rows[0].arr if isinstance(rows[0], _Win) else rows[0]
    S = first.shape[1] if first.ndim == 3 else first.shape[0]
    tm = min(tm, S)
    n_in, n_out = len(rows) + len(consts), len(outs)
    arrays, in_specs = [], []
    for r in rows:
        arr, spec = _row_spec(r, tm)
        arrays.append(arr)
        in_specs.append(spec)
    for c in consts:
        arrays.append(c)
        in_specs.append(pl.BlockSpec(c.shape, functools.partial(lambda i, n: (0,) * n, n=c.ndim)))
    aliases = {}
    for k, o in enumerate(outs):
        if isinstance(o, _Win):
            aliases[len(arrays)] = k
            arrays.append(o.arr)
            in_specs.append(pl.BlockSpec(memory_space=pl.ANY))
    n_thru = len(aliases)
    out_specs = [_row_spec(o, tm)[1] for o in outs]
    out_specs += [pl.BlockSpec(a.shape, functools.partial(lambda i, n: (0,) * n, n=len(a.shape))) for a in accs]
    out_shape = [_sds(o.arr.shape, o.arr.dtype) if isinstance(o, _Win) else _sds(o.shape, o.dtype) for o in outs]

    def body(*refs):
        res = fn(*[r[...] for r in refs[:n_in]])
        if not isinstance(res, (tuple, list)):
            res = (res,)
        orefs = refs[n_in + n_thru:]
        for r, v in zip(orefs[:n_out], res[:n_out]):
            r[...] = v.astype(r.dtype)
        i = pl.program_id(0)
        for r, v in zip(orefs[n_out:], res[n_out:]):
            @pl.when(i == 0)
            def _(r=r, v=v):
                r[...] = v.astype(r.dtype)

            @pl.when(i > 0)
            def _(r=r, v=v):
                r[...] += v.astype(r.dtype)

    return _pcall(body, name=name, grid=(S // tm,), in_specs=in_specs, out_specs=out_specs,
                  out_shape=out_shape + [_sds(a.shape, a.dtype) for a in accs], aliases=aliases)(*arrays)


def _sds(shape, dtype=F32):
    return jax.ShapeDtypeStruct(shape, dtype)


def _rms(x, g):
    return x * lax.rsqrt(jnp.mean(x * x, axis=-1, keepdims=True) + EPS) * g


def _rms_bwd(x, g, dy):
    _, vjp = jax.vjp(_rms, x, g)
    return vjp(dy)


def _sigmoid(x):
    return 0.5 * jnp.tanh(0.5 * x) + 0.5


def _silu(x):
    return x * _sigmoid(x)


def _log_sigmoid(f):
    return jnp.minimum(f, 0.0) - jnp.log(1.0 + jnp.exp(-jnp.abs(f)))


def _head_rms_gate(o, g_out, gn):
    parts = [_rms(o[:, h * GLA_HDV:(h + 1) * GLA_HDV], gn[:, h * GLA_HDV:(h + 1) * GLA_HDV]) for h in range(GLA_HEADS)]
    return jnp.concatenate(parts, axis=-1) * _silu(g_out.astype(F32))


def _merge(gates, ya, yb, yc):
    gates, ya, yb, yc = (t.astype(F32) for t in (gates, ya, yb, yc))
    return (_sigmoid(gates[:, :D_MODEL]) * ya + _sigmoid(gates[:, D_MODEL:2 * D_MODEL]) * yb
            + _sigmoid(gates[:, 2 * D_MODEL:]) * yc)


def _tri_dot(t, x):
    hi = x.astype(BF16)
    r1 = x - hi.astype(F32)
    mid = r1.astype(BF16)
    lo = (r1 - mid.astype(F32)).astype(BF16)
    d = functools.partial(lax.dot_general, dimension_numbers=_NN, preferred_element_type=F32)
    return d(t, hi) + d(t, mid) + d(t, lo)


def _pre_norm(name, x, g, after):
    S = x.shape[0]
    tm = min(1024, S)

    def body(x_ref, g_ref, *rest):
        rest[-1][...] = _rms(x_ref[...], g_ref[...]).astype(BF16)

    row = pl.BlockSpec((tm, D_MODEL), lambda i: (i, 0))
    return _pcall(body, name=name, grid=(S // tm,),
                  in_specs=[row, pl.BlockSpec((1, D_MODEL), lambda i: (0, 0))] + [pl.BlockSpec(memory_space=pl.ANY)] * len(after),
                  out_specs=row, out_shape=_sds(x.shape, BF16))(x, g, *after)


def _ffn_fwd(tag, x, h, wt, gpost, get_wo, g_next, tgt=None):
    S = x.shape[0]
    tm = min(512, S)

    def up_body(h_ref, w_ref, u_ref, act_ref):
        hh = h_ref[...]
        for j in range(N_FF_BLK):
            ab = _dot(hh, w_ref[2 * FF_PAD * j:2 * FF_PAD * (j + 1), :], _NT)
            u_ref[:, 2 * FF_PAD * j:2 * FF_PAD * (j + 1)] = ab.astype(BF16)
            act_ref[:, FF_PAD * j:FF_PAD * (j + 1)] = (_silu(ab[:, :FF_PAD]) * ab[:, FF_PAD:]).astype(BF16)

    def down_head_body(act_ref, w_ref, x_ref, g_ref, gn_ref, t_ref, f_ref, dxo_ref, loss_ref, dgn_ref):
        i = pl.program_id(0)
        f = _dot(act_ref[...], w_ref[...])
        f_ref[...] = f.astype(BF16)
        xo = x_ref[...] + 0.5 * _rms(f, g_ref[...])
        out, vjp = jax.vjp(_rms, xo, gn_ref[...])
        e = out - t_ref[...]
        loss = 0.5 * jnp.sum(jnp.mean(e * e, axis=-1, keepdims=True), axis=0, keepdims=True)
        dxo_ref[...], dgn = vjp(e * (1.0 / D_MODEL))

        @pl.when(i == 0)
        def _():
            loss_ref[...] = jnp.broadcast_to(loss, loss_ref.shape)
            dgn_ref[...] = dgn

        @pl.when(i > 0)
        def _():
            loss_ref[...] += jnp.broadcast_to(loss, loss_ref.shape)
            dgn_ref[...] += dgn

    if tgt is not None:
        def last_body(h_ref, wt_ref, wo_ref, x_ref, g_ref, gn_ref, t_ref, u_ref, act_ref, f_ref, dxo_ref, loss_ref,
                      dgn_ref):
            up_body(h_ref, wt_ref, u_ref, act_ref)
            down_head_body(act_ref, wo_ref, x_ref, g_ref, gn_ref, t_ref, f_ref, dxo_ref, loss_ref, dgn_ref)

        wo = get_wo(None)
        tl = min(256, S)
        row = pl.BlockSpec((tl, D_MODEL), lambda i: (i, 0))
        vec = pl.BlockSpec((1, D_MODEL), lambda i: (0, 0))
        u, act, f, dxo, loss, dg_next = _pcall(
            last_body, name=tag + "_fwd", grid=(S // tl,),
            in_specs=[row, pl.BlockSpec(wt.shape, lambda i: (0, 0), pipeline_mode=pl.Buffered(1)),
                      pl.BlockSpec(wo.shape, lambda i: (0, 0), pipeline_mode=pl.Buffered(1)), row, vec, vec, row],
            out_specs=[pl.BlockSpec((tl, N_DEV * FF_PAD), lambda i: (i, 0)),
                       pl.BlockSpec((tl, N_FF_BLK * FF_PAD), lambda i: (i, 0)), row, row,
                       pl.BlockSpec((1, 128), lambda i: (0, 0)), vec],
            out_shape=[_sds((S, N_DEV * FF_PAD), BF16), _sds((S, N_FF_BLK * FF_PAD), BF16), _sds((S, D_MODEL), BF16),
                       _sds((S, D_MODEL)), _sds((1, 128)), _sds((1, D_MODEL))],
        )(h, wt, wo, x, gpost, g_next, tgt)
        return (loss, dxo, dg_next), wo, (h, u, act, f)

    u, act = _pcall(
        up_body, name=tag + "_up", grid=(S // tm,),
        in_specs=[pl.BlockSpec((tm, D_MODEL), lambda i: (i, 0)),
                  pl.BlockSpec(wt.shape, lambda i: (0, 0), pipeline_mode=pl.Buffered(1))],
        out_specs=[pl.BlockSpec((tm, N_DEV * FF_PAD), lambda i: (i, 0)),
                   pl.BlockSpec((tm, N_FF_BLK * FF_PAD), lambda i: (i, 0))],
        out_shape=[_sds((S, N_DEV * FF_PAD), BF16), _sds((S, N_FF_BLK * FF_PAD), BF16)],
    )(h, wt)

    def down_body(act_ref, w_ref, x_ref, g_ref, gn_ref, f_ref, xo_ref, hn_ref):
        f = _dot(act_ref[...], w_ref[...])
        f_ref[...] = f.astype(BF16)
        xo = x_ref[...] + 0.5 * _rms(f, g_ref[...])
        xo_ref[...] = xo
        hn_ref[...] = _rms(xo, gn_ref[...]).astype(BF16)

    wo = get_wo(act)
    row = pl.BlockSpec((tm, D_MODEL), lambda i: (i, 0))
    vec = pl.BlockSpec((1, D_MODEL), lambda i: (0, 0))
    f, xo, h_next = _pcall(
        down_body, name=tag + "_down", grid=(S // tm,),
        in_specs=[pl.BlockSpec((tm, N_FF_BLK * FF_PAD), lambda i: (i, 0)),
                  pl.BlockSpec(wo.shape, lambda i: (0, 0), pipeline_mode=pl.Buffered(1)), row, vec, vec],
        out_specs=[row, row, row], out_shape=[_sds((S, D_MODEL), BF16), _sds((S, D_MODEL)), _sds((S, D_MODEL), BF16)],
    )(act, wo, x, gpost, g_next)
    return (xo, h_next), wo, (h, u, act, f)


def _ffn_bwd(tag, dxo, x, gpre, wt, wo, gpost, saved, on_dwo=None, on_dwt=None):
    h, u, act, f = saved
    S = x.shape[0]
    tm = min(512, S)

    def dact_body(f_ref, dxo_ref, g_ref, w_ref, u_ref, df_ref, du_ref, dg_ref):
        i = pl.program_id(0)
        df, dg = _rms_bwd(f_ref[...].astype(F32), g_ref[...], 0.5 * dxo_ref[...])
        df = df.astype(BF16)
        df_ref[...] = df

        @pl.when(i == 0)
        def _():
            dg_ref[...] = dg

        @pl.when(i > 0)
        def _():
            dg_ref[...] += dg

        for j in range(N_FF_BLK):
            dact = _dot(df, w_ref[FF_PAD * j:FF_PAD * (j + 1), :], _NT)
            ab = u_ref[:, 2 * FF_PAD * j:2 * FF_PAD * (j + 1)].astype(F32)
            a, b = ab[:, :FF_PAD], ab[:, FF_PAD:]
            sg = _sigmoid(a)
            du_ref[:, 2 * FF_PAD * j:2 * FF_PAD * j + FF_PAD] = (dact * b * (sg * (1.0 + a * (1.0 - sg)))).astype(BF16)
            du_ref[:, 2 * FF_PAD * j + FF_PAD:2 * FF_PAD * (j + 1)] = (dact * (a * sg)).astype(BF16)

    row = pl.BlockSpec((tm, D_MODEL), lambda i: (i, 0))
    vec = pl.BlockSpec((1, D_MODEL), lambda i: (0, 0))
    u_spec = pl.BlockSpec((tm, N_DEV * FF_PAD), lambda i: (i, 0))
    df, du, dgpost = _pcall(
        dact_body, name=tag + "_dact", grid=(S // tm,),
        in_specs=[row, row, vec, pl.BlockSpec(wo.shape, lambda i: (0, 0), pipeline_mode=pl.Buffered(1)), u_spec],
        out_specs=[row, u_spec, vec],
        out_shape=[_sds((S, D_MODEL), BF16), _sds((S, N_DEV * FF_PAD), BF16), _sds((1, D_MODEL))])(f, dxo, gpost, wo, u)
    dwo = _mm_tn(tag + "_dwo", act, df, ts=S, tn_a=FF_PAD)
    dwt = _mm_tn(tag + "_dwt", du, h, ts=S, tn_a=FF_PAD, deps=[on_dwo(dwo)] if on_dwo is not None else [])
    dx, dgpre = _dh_pre_norm_bwd(tag + "_dh", [du], wt, x, dxo, gpre, [on_dwt(dwt)] if on_dwt is not None else [])
    return dx, dgpre, dgpost


def _dh_pre_norm_bwd(name, dzs, wt, x, dres, g, deps):
    S = x.shape[0]
    ks = [dz.shape[1] for dz in dzs]

    def body(*refs):
        dz_refs = refs[:len(dzs)]
        w_ref, x_ref, dres_ref, g_ref = refs[len(dzs):len(dzs) + 4]
        dx_ref, dg_ref = refs[len(dzs) + 4 + len(deps):]
        i = pl.program_id(0)
        dh, o = None, 0
        for dz_ref, k in zip(dz_refs, ks):
            p = _dot(dz_ref[...], w_ref[o:o + k, :])
            dh, o = (p if dh is None else dh + p), o + k
        dx, dg = _rms_bwd(x_ref[...], g_ref[...], dh)
        dx_ref[...] = dx + dres_ref[...]

        @pl.when(i == 0)
        def _():
            dg_ref[...] = dg

        @pl.when(i > 0)
        def _():
            dg_ref[...] += dg

    th = min(512, S)
    row = pl.BlockSpec((th, D_MODEL), lambda i: (i, 0))
    vec = pl.BlockSpec((1, D_MODEL), lambda i: (0, 0))
    return _pcall(
        body, name=name, grid=(S // th,),
        in_specs=[pl.BlockSpec((th, k), lambda i: (i, 0)) for k in ks]
        + [pl.BlockSpec(wt.shape, lambda i: (0, 0), pipeline_mode=pl.Buffered(1)), row, row, vec]
        + [pl.BlockSpec(memory_space=pl.ANY)] * len(deps),
        out_specs=[row, vec], out_shape=[_sds((S, D_MODEL)), _sds((1, D_MODEL))])(*dzs, wt, x, dres, g, *deps)


GLA_G = 8


def _gla_tile_common(k, flow, wfu, bf):
    f = _dot(flow, wfu) + bf
    la = _log_sigmoid(f) * (1.0 / GATE_TEMP)
    tri = _tri_matrix(True)
    ws, ds = [], []
    for g in range(k.shape[0] // CHUNK):
        b = _tri_dot(tri, la[g * CHUNK:(g + 1) * CHUNK])
        b_end = b[CHUNK - 1:CHUNK, :]
        ws.append(jnp.exp(b_end - b))
        ds.append(jnp.exp(b_end))
    w = jnp.concatenate(ws, axis=0)
    return f, w, k * w, ds


def _tri_matrix(lower):
    r = lax.broadcasted_iota(jnp.int32, (CHUNK, CHUNK), 0)
    c = lax.broadcasted_iota(jnp.int32, (CHUNK, CHUNK), 1)
    return jnp.where((r >= c) if lower else (r <= c), 1.0, 0.0).astype(BF16)


def _heads():
    return [(slice(h * GLA_HDK, (h + 1) * GLA_HDK), slice(h * GLA_HDV, (h + 1) * GLA_HDV)) for h in range(GLA_HEADS)]


def _gla_fwd(proj, wfu, bf, gn, deps=()):
    S = proj.shape[0]
    G = min(GLA_G, S // CHUNK)
    T = G * CHUNK
    nc = S // CHUNK

    def body(q_ref, k_ref, v_ref, go_ref, flow_ref, wfu_ref, bf_ref, gn_ref, *rest):
        ya_ref, st_ref, state = rest[len(deps):]
        @pl.when(pl.program_id(0) == 0)
        def _():
            state[...] = jnp.zeros_like(state)

        _, _, kt, ds = _gla_tile_common(k_ref[...], flow_ref[...], wfu_ref[...], bf_ref[...])
        q = q_ref[...].astype(F32) * (GLA_HDK ** -0.5)
        v = v_ref[...]
        rows = [slice(g * CHUNK, (g + 1) * CHUNK) for g in range(G)]
        kv = [[_dot(v[r, vs], kt[r, ks], _TN) for ks, vs in _heads()] for r in rows]
        st = [state[vs, :] for _, vs in _heads()]
        o = []
        for g, r in enumerate(rows):
            outs = []
            for h, (ks, vs) in enumerate(_heads()):
                st[h] = st[h] * ds[g][:, ks] + kv[g][h]
                st_ref[g, vs, :] = st[h]
                outs.append(_dot(q[r, ks], st[h], _NT))
            o.append(jnp.concatenate(outs, axis=-1))
        for h, (_, vs) in enumerate(_heads()):
            state[vs, :] = st[h]
        ya_ref[...] = _head_rms_gate(jnp.concatenate(o, axis=0), go_ref[...], gn_ref[...]).astype(BF16)

    return _pcall(
        body, name="gla_fwd", grid=(S // T,),
        in_specs=[pl.BlockSpec((T, GLA_DK), lambda c: (c, 0)), pl.BlockSpec((T, GLA_DK), lambda c: (c, 1)),
                  pl.BlockSpec((T, GLA_DV), lambda c: (c, 1)), pl.BlockSpec((T, GLA_DV), lambda c: (c, 2)),
                  pl.BlockSpec((T, FLOW_W), lambda c: (c, (INT_W - FLOW_W) // FLOW_W)),
                  pl.BlockSpec(wfu.shape, lambda c: (0, 0)), pl.BlockSpec(bf.shape, lambda c: (0, 0)),
                  pl.BlockSpec(gn.shape, lambda c: (0, 0))]
        + [pl.BlockSpec(memory_space=pl.ANY)] * len(deps),
        out_specs=[pl.BlockSpec((T, GLA_DV), lambda c: (c, 0)),
                   pl.BlockSpec((G, GLA_DV, GLA_HDK), lambda c: (c, 0, 0))],
        out_shape=[_sds((S, GLA_DV), BF16), _sds((nc, GLA_DV, GLA_HDK))],
        scratch=[pltpu.VMEM((GLA_DV, GLA_HDK), F32)])(proj, proj, proj, proj, proj, wfu, bf, gn, *deps)


def _gla_bwd(proj, wfu, bf, gn, states, d_ya_in, dmain):
    S = proj.shape[0]
    G = min(GLA_G, S // CHUNK)
    T = G * CHUNK
    nt = S // T

    def body(q_ref, k_ref, v_ref, go_ref, flow_ref, wfu_ref, bf_ref, gn_ref, st_ref, stp_ref, dya_ref, dmain_in,
             dqkvg_ref, dflow_ref, dwfu_ref, dbf_ref, dgn_ref, dstate):
        dq_ref = dqkvg_ref.at[:, 0:GLA_DK]
        dk_ref = dqkvg_ref.at[:, GLA_DK:2 * GLA_DK]
        dv_ref = dqkvg_ref.at[:, 2 * GLA_DK:2 * GLA_DK + GLA_DV]
        dgo_ref = dqkvg_ref.at[:, 2 * GLA_DK + GLA_DV:]
        step = pl.program_id(0)

        @pl.when(step == 0)
        def _():
            dstate[...] = jnp.zeros_like(dstate)
            dwfu_ref[...] = jnp.zeros_like(dwfu_ref)
            dbf_ref[...] = jnp.zeros_like(dbf_ref)
            dgn_ref[...] = jnp.zeros_like(dgn_ref)

        flow, wfu_v = flow_ref[...], wfu_ref[...]
        f, w, kt, ds = _gla_tile_common(k_ref[...], flow, wfu_v, bf_ref[...])
        q = q_ref[...].astype(F32) * (GLA_HDK ** -0.5)
        v = v_ref[...]
        rows = [slice(g * CHUNK, (g + 1) * CHUNK) for g in range(G)]
        o = jnp.concatenate([jnp.concatenate([_dot(q[r, ks], st_ref[g, vs, :], _NT) for ks, vs in _heads()], axis=-1)
                             for g, r in enumerate(rows)], axis=0)
        _, vjp = jax.vjp(_head_rms_gate, o, go_ref[...], gn_ref[...])
        dout, dgo, dgn = vjp(dya_ref[...].astype(F32))
        dgo_ref[...] = dgo.astype(dgo_ref.dtype)
        dgn_ref[...] += dgn
        dq = [jnp.concatenate([_dot(dout[r, vs], st_ref[g, vs, :]) for _, vs in _heads()], axis=-1)
              for g, r in enumerate(rows)]
        qdo = [[_dot(dout[r, vs], q[r, ks], _TN) for ks, vs in _heads()] for r in rows]
        dq_ref[...] = (jnp.concatenate(dq, axis=0) * (GLA_HDK ** -0.5)).astype(dq_ref.dtype)
        has_prev = (step < nt - 1).astype(F32)
        carry = [dstate[vs, :] for _, vs in _heads()]
        dkt, dv, dd = [None] * G, [None] * G, [None] * G
        for g in reversed(range(G)):
            r = rows[g]
            dkts, dvs, dds = [], [], []
            for h, (ks, vs) in enumerate(_heads()):
                dst = carry[h] + qdo[g][h]
                dkts.append(_dot(v[r, vs], dst))
                dvs.append(_dot(kt[r, ks], dst, _NT))
                st_prev = st_ref[g - 1, vs, :] if g > 0 else stp_ref[vs, :] * has_prev
                dds.append(jnp.sum(dst * st_prev, axis=0, keepdims=True))
                carry[h] = dst * ds[g][:, ks]
            dkt[g], dv[g], dd[g] = (jnp.concatenate(t, axis=-1) for t in (dkts, dvs, dds))
        for h, (_, vs) in enumerate(_heads()):
            dstate[vs, :] = carry[h]
        dkt = jnp.concatenate(dkt, axis=0)
        dv_ref[...] = jnp.concatenate(dv, axis=0).astype(dv_ref.dtype)
        dk_ref[...] = (dkt * w).astype(dk_ref.dtype)
        de = dkt * kt
        tri = _tri_matrix(False)
        dla = []
        for g, r in enumerate(rows):
            db_end = jnp.sum(de[r], axis=0, keepdims=True) + dd[g] * ds[g]
            dla.append(db_end - _tri_dot(tri, de[r]))
        df = jnp.concatenate(dla, axis=0) * (1.0 - _sigmoid(f)) * (1.0 / GATE_TEMP)
        dflow_ref[...] = _dot(df, wfu_v, _NT).astype(dflow_ref.dtype)
        dwfu_ref[...] += _dot(flow, df, _TN)
        dbf_ref[...] += jnp.sum(df, axis=0, keepdims=True)

    rt = lambda s: nt - 1 - s
    return _pcall(
        body, name="gla_bwd", grid=(nt,),
        in_specs=[pl.BlockSpec((T, GLA_DK), lambda s: (rt(s), 0)), pl.BlockSpec((T, GLA_DK), lambda s: (rt(s), 1)),
                  pl.BlockSpec((T, GLA_DV), lambda s: (rt(s), 1)), pl.BlockSpec((T, GLA_DV), lambda s: (rt(s), 2)),
                  pl.BlockSpec((T, FLOW_W), lambda s: (rt(s), (INT_W - FLOW_W) // FLOW_W)),
                  pl.BlockSpec(wfu.shape, lambda s: (0, 0)), pl.BlockSpec(bf.shape, lambda s: (0, 0)),
                  pl.BlockSpec(gn.shape, lambda s: (0, 0)),
                  pl.BlockSpec((G, GLA_DV, GLA_HDK), lambda s: (rt(s), 0, 0)),
                  pl.BlockSpec((None, GLA_DV, GLA_HDK), lambda s: (jnp.maximum(rt(s) * G - 1, 0), 0, 0)),
                  pl.BlockSpec((T, GLA_DV), lambda s: (rt(s), 0)), pl.BlockSpec(memory_space=pl.ANY)],
        out_specs=[pl.BlockSpec((T, 2 * GLA_DK + 2 * GLA_DV), lambda s: (rt(s), 0)),
                   pl.BlockSpec((T, FLOW_W), lambda s: (rt(s), 0)),
                   pl.BlockSpec(wfu.shape, lambda s: (0, 0)), pl.BlockSpec(bf.shape, lambda s: (0, 0)),
                   pl.BlockSpec(gn.shape, lambda s: (0, 0))],
        out_shape=[_sds(dmain.shape, dmain.dtype), _sds((S, FLOW_W), BF16), _sds(wfu.shape), _sds(bf.shape),
                   _sds(gn.shape)],
        scratch=[pltpu.VMEM((GLA_DV, GLA_HDK), F32)], aliases={11: 0},
    )(proj, proj, proj, proj, proj, wfu, bf, gn, states, states, d_ya_in, dmain)


def _pool_counts(tm, i):
    t = (lax.broadcasted_iota(jnp.int32, (tm, POOL_GD), 0) + i * tm + 1).astype(F32)
    return [jnp.minimum(t, float(w)) for w in POOL_WINDOWS]


def _pool_fwd(proj, w_pool, pool_scale, tm=1024):
    S = proj.shape[0]
    tm = min(tm, S // 2)
    col = _POOL_COL
    hb = tm // POOL_HALO

    def body(p_ref, halo_ref, wp_ref, sc_ref, mixed_ref, out_ref):
        i = pl.program_id(0)
        p = p_ref[...].astype(F32)
        halo = halo_ref[...].astype(F32) * (i > 0).astype(F32)
        ext = jnp.concatenate([halo, p], axis=0)
        n = tm + POOL_HALO
        sums, acc, k = {}, ext, 1
        while k < POOL_WINDOWS[-1]:
            acc = acc + pltpu.roll(acc, k, axis=0)
            k *= 2
            sums[k] = acc
        cnts = _pool_counts(tm, i)
        mixed, lin = [], []
        for g, w in enumerate(POOL_WINDOWS):
            ls = slice(g * POOL_GD, (g + 1) * POOL_GD)
            m = sums[w][POOL_HALO:n, ls] / cnts[g] - p[:, ls]
            mixed.append(m)
            lin.append(_dot(m, wp_ref[g]))
        mixed_ref[...] = jnp.concatenate(mixed, axis=-1)
        out_ref[...] = (jnp.concatenate(lin, axis=-1) * sc_ref[...]).astype(out_ref.dtype)

    return _pcall(
        body, name="pool_fwd", grid=(S // tm,),
        in_specs=[pl.BlockSpec((tm, POOL_W), lambda i: (i, col)),
                  pl.BlockSpec((POOL_HALO, POOL_W), lambda i: (jnp.maximum(i * hb - 1, 0), col)),
                  pl.BlockSpec(w_pool.shape, lambda i: (0, 0, 0)), pl.BlockSpec(pool_scale.shape, lambda i: (0, 0))],
        out_specs=[pl.BlockSpec((tm, POOL_W), lambda i: (i, 0)), pl.BlockSpec((tm, POOL_W), lambda i: (i, 0))],
        out_shape=[_sds((S, POOL_W)), _sds((S, POOL_W), BF16)])(proj, proj, w_pool, pool_scale)


def _pool_lin_bwd(dout, mixed, w_pool, pool_scale):
    S = dout.shape[0]

    def fn(dout, mixed, wp, sc):
        dlin = dout * sc
        dm, dwp, lin = [], [], []
        for g in range(len(POOL_WINDOWS)):
            ls = slice(g * POOL_GD, (g + 1) * POOL_GD)
            lin.append(_dot(mixed[:, ls], wp[g]))
            dm.append(_dot(dlin[:, ls], wp[g], _NT))
            dwp.append(_dot(mixed[:, ls], dlin[:, ls], _TN))
        dsc = jnp.sum(dout * jnp.concatenate(lin, axis=-1), axis=0, keepdims=True)
        return jnp.concatenate(dm, axis=-1), jnp.concatenate(dwp, axis=0), dsc

    return _rowwise("pool_lin_bwd", fn, [dout, mixed], [w_pool, pool_scale], [_sds((S, POOL_W))],
                    [_sds((len(POOL_WINDOWS) * POOL_GD, POOL_GD)), _sds((1, POOL_W))], tm=1024)


def _pool_win_bwd(dmixed, dmain, tm=1024):
    S = dmixed.shape[0]
    tm = min(tm, S // 2)
    nt = S // tm
    hb = tm // POOL_HALO

    def body(dm_ref, halo_ref, dmain_in, dp_ref):
        i = pl.program_id(0)
        dm = dm_ref[...]
        halo = halo_ref[...] * (i < nt - 1).astype(F32)
        cnts = _pool_counts(tm, i)
        cnts_h = [c[:POOL_HALO] for c in _pool_counts(tm, i + 1)]
        r = jnp.concatenate([jnp.concatenate([dm[:, g * POOL_GD:(g + 1) * POOL_GD] / cnts[g] for g in range(4)], axis=-1),
                             jnp.concatenate([halo[:, g * POOL_GD:(g + 1) * POOL_GD] / cnts_h[g] for g in range(4)], axis=-1)],
                            axis=0)
        n = tm + POOL_HALO
        sums, acc, k = {}, r, 1
        while k < POOL_WINDOWS[-1]:
            acc = acc + pltpu.roll(acc, n - k, axis=0)
            k *= 2
            sums[k] = acc
        dp = [sums[w][:tm, g * POOL_GD:(g + 1) * POOL_GD] for g, w in enumerate(POOL_WINDOWS)]
        dp_ref[...] = (jnp.concatenate(dp, axis=-1) - dm).astype(dp_ref.dtype)

    return _pcall(
        body, name="pool_win_bwd", grid=(nt,),
        in_specs=[pl.BlockSpec((tm, POOL_W), lambda i: (i, 0)),
                  pl.BlockSpec((POOL_HALO, POOL_W), lambda i: (jnp.minimum((i + 1) * hb, S // POOL_HALO - 1), 0)),
                  pl.BlockSpec(memory_space=pl.ANY)],
        out_specs=pl.BlockSpec((tm, POOL_W), lambda i: (i, _POOL_COL)),
        out_shape=_sds(dmain.shape, dmain.dtype), aliases={2: 0})(dmixed, dmixed, dmain)


def _xattn_probs(q, kv, h):
    hs = slice(h * XA_HD, (h + 1) * XA_HD)
    s = _dot(q[:, hs], kv[:, hs], _NT) * (XA_HD ** -0.5)
    s = s - jnp.max(s, axis=-1, keepdims=True)
    e = jnp.exp(s)
    return e / jnp.sum(e, axis=-1, keepdims=True)


def _xattn_fwd(proj, kv):
    S = proj.shape[0]

    def fn(q, kv):
        outs = []
        for h in range(XA_HEADS):
            p = _xattn_probs(q, kv, h)
            outs.append(_dot(p, kv[:, XA_W + h * XA_HD:XA_W + (h + 1) * XA_HD]))
        return jnp.concatenate(outs, axis=-1)

    return _rowwise("xattn_fwd", fn, [_Win(proj, XA_W, _XQ_COL)], [kv], [_sds((S, XA_W), BF16)], tm=1024)[0]


def _xattn_bwd(proj, kv, dxa, dmain):
    def fn(q, dxa, kv):
        dqs, dks, dvs = [], [], []
        for h in range(XA_HEADS):
            hs = slice(h * XA_HD, (h + 1) * XA_HD)
            vh = kv[:, XA_W + h * XA_HD:XA_W + (h + 1) * XA_HD]
            p = _xattn_probs(q, kv, h)
            dp = _dot(dxa[:, hs], vh, _NT)
            ds = p * (dp - jnp.sum(p * dp, axis=-1, keepdims=True)) * (XA_HD ** -0.5)
            dqs.append(_dot(ds, kv[:, hs]))
            dks.append(_dot(ds, q[:, hs], _TN))
            dvs.append(_dot(p, dxa[:, hs], _TN))
        return jnp.concatenate(dqs, axis=-1), jnp.concatenate(dks + dvs, axis=-1)

    return _rowwise("xattn_bwd", fn, [_Win(proj, XA_W, _XQ_COL), dxa], [kv], [_Win(dmain, XA_W, _XQ_COL)],
                    [_sds(kv.shape)], tm=1024)


def _mix_fwd(x1, h, mem, W, g_next, on_proj=None, late_weights=None):
    S = x1.shape[0]
    M = mem.shape[0]
    tm = min(512, S)
    def proj_body(h_ref, w_ref, o_ref):
        hh = h_ref[...]
        for j in range(INT_NBLK):
            o_ref[:, INT_BLK * j:INT_BLK * (j + 1)] = _dot(hh, w_ref[INT_BLK * j:INT_BLK * (j + 1), :], _NT).astype(BF16)

    proj = _pcall(
        proj_body, name="mix_proj", grid=(S // tm,),
        in_specs=[pl.BlockSpec((tm, D_MODEL), lambda i: (i, 0)),
                  pl.BlockSpec((INT_W, D_MODEL), lambda i: (0, 0), pipeline_mode=pl.Buffered(1))],
        out_specs=pl.BlockSpec((tm, INT_W), lambda i: (i, 0)), out_shape=_sds((S, INT_W), BF16))(h, W["w_int_t"])
    ya_in, states = _gla_fwd(proj, W["w_fu_pad"], W["b_f"], W["gla_norm_g"],
                             on_proj(proj) if on_proj is not None else [])
    mixed, pool_out = _pool_fwd(proj, W["w_pool"], W["pool_scale"])
    mem_n = _rowwise("mem_norm", lambda m, g: _rms(m, g), [mem], [W["mem_norm_g"]], [_sds((M, D_MODEL), BF16)])[0]
    if late_weights is not None:
        W = {**W, **late_weights([pool_out])}
    kv = _mm_nn("mem_kv", mem_n, W["w_mem_kv"])
    xa = _xattn_fwd(proj, kv)
    def out_fn(gates, ya_in, pool_out, xa, x1, wa, wb, wc, wo, g, gn):
        ya, yb, yc = (_dot(t, w).astype(BF16) for t, w in ((ya_in, wa), (pool_out, wb), (xa, wc)))
        merged = _merge(gates, ya, yb, yc).astype(BF16)
        y = _dot(merged, wo)
        x2 = x1 + _rms(y, g)
        return ya, yb, yc, merged, y, x2, _rms(x2, gn)

    ya, yb, yc, merged, y, x2, h_next = _rowwise(
        "mix_out", out_fn, [_Win(proj, 3 * D_MODEL, 1), ya_in, pool_out, xa, x1],
        [W["w_up_gla"], W["w_up_pool"], W["w_up_xattn"], W["w_o"], W["mix_post_g"], g_next],
        [_sds((S, D_MODEL), BF16)] * 5 + [_sds((S, D_MODEL)), _sds((S, D_MODEL), BF16)])
    return x2, h_next, W, (h, proj, states, ya_in, mixed, pool_out, mem_n, kv, xa, ya, yb, yc, merged, y)


def _mix_bwd(dx2, x1, mem, W, saved, on_grads=None, on_up_grads=None):
    h, proj, states, ya_in, mixed, pool_out, mem_n, kv, xa, ya, yb, yc, merged, y = saved
    S = x1.shape[0]

    def out_bwd(y, dx2, gates, ya, yb, yc, g, wo, wa, wb, wc):
        dy, dg = _rms_bwd(y.astype(F32), g, dx2)
        dy = dy.astype(BF16)
        _, vjp = jax.vjp(_merge, gates, ya, yb, yc)
        dgates, dya, dyb, dyc = vjp(_dot(dy, wo, _NT))
        return (dy, dgates, dya, dyb, dyc, _dot(dya, wa, _NT), _dot(dyb, wb, _NT), _dot(dyc, wc, _NT), dg)

    dmain = lax.empty((S, INT_MAIN), BF16)
    dy, dmain, dya, dyb, dyc, d_ya_in, d_pool_out, d_xa, d_mix_post_g = _rowwise(
        "mix_out_bwd", out_bwd, [y, dx2, _Win(proj, 3 * D_MODEL, 1), ya, yb, yc],
        [W["mix_post_g"], W["w_o"], W["w_up_gla"], W["w_up_pool"], W["w_up_xattn"]],
        [_sds((S, D_MODEL), BF16), _Win(dmain, 3 * D_MODEL, 1)] + [_sds((S, D_MODEL), BF16)] * 4
        + [_sds((S, POOL_W), BF16), _sds((S, XA_W), BF16)], [_sds((1, D_MODEL))])
    d_w_o = _mm_tn("d_w_o", merged, dy, ts=S, tn_a=256)
    d_w_up_gla = _mm_tn("d_w_up_gla", ya_in, dya, ts=S, tn_a=256)
    d_w_up_pool = _mm_tn("d_w_up_pool", pool_out, dyb, ts=S, tn_a=128)
    d_w_up_xattn = _mm_tn("d_w_up_xattn", xa, dyc, ts=S, tn_a=128)
    up_grads = dict(w_up_gla=d_w_up_gla, w_up_pool=d_w_up_pool, w_up_xattn=d_w_up_xattn, w_o=d_w_o)
    up_deps = [on_up_grads(up_grads)] if on_up_grads is not None else []

    dmain, dflow, d_wfu_pad, d_b_f, d_gla_norm_g = _gla_bwd(proj, W["w_fu_pad"], W["b_f"], W["gla_norm_g"], states,
                                                            d_ya_in, dmain)
    dmixed, d_w_pool, d_pool_scale = _pool_lin_bwd(d_pool_out, mixed, W["w_pool"], W["pool_scale"])
    dmain = _pool_win_bwd(dmixed, dmain)
    dmain, dkv = _xattn_bwd(proj, kv, d_xa, dmain)
    d_w_mem_kv = _mm_tn("d_w_mem_kv", mem_n, dkv)
    dmem_n = _mm_nt("d_mem_n", dkv, W["w_mem_kv"])
    d_mem_norm_g = _rowwise("mem_norm_bwd", lambda m, d, g: _rms_bwd(m, g, d)[1], [mem, dmem_n], [W["mem_norm_g"]], [],
                            [_sds((1, D_MODEL))])[0]
    d_w_int_t = (_mm_tn("d_w_int", dmain, h, ts=S, tn_a=INT_MAIN // 8, deps=up_deps),
                 _mm_tn("d_w_flow", dflow, h, ts=S))
    grads = dict(
        w_int_t=d_w_int_t, w_fu_pad=d_wfu_pad, b_f=d_b_f, gla_norm_g=d_gla_norm_g, w_pool=d_w_pool,
        pool_scale=d_pool_scale, mem_norm_g=d_mem_norm_g, w_mem_kv=d_w_mem_kv, w_up_gla=d_w_up_gla,
        w_up_pool=d_w_up_pool, w_up_xattn=d_w_up_xattn, w_o=d_w_o, mix_post_g=d_mix_post_g)
    deps = [on_grads(grads)] if on_grads is not None else []
    dx1, grads["mix_pre_g"] = _dh_pre_norm_bwd("d_mix_h", [dmain, dflow], W["w_int_t"], x1, dx2, W["mix_pre_g"], deps)
    return dx1, grads


def _mesh_pos():
    x, y, c = lax.axis_index("x"), lax.axis_index("y"), lax.axis_index("c")
    return x, y, c, 4 * x + 2 * y + c


def _peer(x, y, c, r):
    px = 1 - x if r & 4 else x
    py = 1 - y if r & 2 else y
    pc = 1 - c if r & 1 else c
    return (px, py, pc), 4 * px + 2 * py + pc


_ALL_PEERS = tuple(range(1, N_DEV))
_SIBLING = 1
_SAME_CORE = (2, 4, 6)


def _dev_slot(ref, dev):
    return ref.at[dev]


def _zero_pad_rows(land):
    pad = FF_PAD - FF_BLK

    def body(land_in, o_ref):
        o_ref[...] = jnp.zeros_like(o_ref)

    return _pcall(body, name="zero_pad_rows", grid=(land.shape[0],), in_specs=[pl.BlockSpec(memory_space=pl.ANY)],
                  out_specs=pl.BlockSpec((None, pad, D_MODEL), lambda j: (j, FF_BLK // pad, 0)),
                  out_shape=_sds(land.shape, land.dtype), aliases={0: 0})(land)


class _Plan:
    def __init__(self, scatter, slots=None, shapes=None):
        self.scatter, self.slots, self.shapes = scatter, slots or {}, shapes or {}

    def src(self, srcs, a, dev):
        return self.slots.get(a, _dev_slot)(srcs[a], dev) if self.scatter else srcs[a]

    def dst(self, lands, a, dev):
        return lands[a].at[dev] if self.scatter else self.slots.get(a, _dev_slot)(lands[a], dev)

    def landing_zones(self, arrays):
        lands = []
        for a, arr in enumerate(arrays):
            if self.scatter:
                lands.append(lax.empty((N_DEV,) + tuple(self.shapes.get(a, arr.shape[1:])), arr.dtype))
            elif a in self.shapes:
                lands.append(_zero_pad_rows(lax.empty(self.shapes[a], arr.dtype)))
            else:
                lands.append(lax.empty((N_DEV,) + arr.shape, arr.dtype))
        return lands


_GATHER = _Plan(False)


def _peer_copies(srcs, lands, send_sems, recv_sems, plan, peers=_ALL_PEERS):
    x, y, c, me = _mesh_pos()
    cps = []
    for r in peers:
        pos, peer = _peer(x, y, c, r)
        for a in range(len(srcs)):
            k = a * (N_DEV - 1) + r - 1
            cps.append(pltpu.make_async_remote_copy(
                src_ref=plan.src(srcs, a, peer), dst_ref=plan.dst(lands, a, me),
                send_sem=send_sems.at[k], recv_sem=recv_sems.at[k], device_id=pos,
                device_id_type=pl.DeviceIdType.MESH))
    return cps


_HBM = pl.BlockSpec(memory_space=pltpu.HBM)
_SEM = pl.BlockSpec(memory_space=pltpu.SEMAPHORE)
_EFFECT = pltpu.SideEffectType.DATAFLOW_SIDE_EFFECTING


def _own_copies(srcs, lands, own_sems, plan):
    me = _mesh_pos()[3]
    return [pltpu.make_async_copy(plan.src(srcs, a, me), plan.dst(lands, a, me), own_sems.at[a])
            for a in range(len(srcs))]


def _exchange_start(name, arrays, plan):
    n = len(arrays)
    lands = plan.landing_zones(arrays)
    n_sem = n * (N_DEV - 1)

    def body(*refs):
        srcs, lands_ = refs[:n], refs[n:2 * n]
        send_sems, recv_sems, own_sems = refs[2 * n:2 * n + 3]
        token = refs[-1]
        for cp in _peer_copies(srcs, lands_, send_sems, recv_sems, plan) + _own_copies(srcs, lands_, own_sems, plan):
            cp.start()
        token[...] = jnp.zeros_like(token)

    hbm = lambda a: pltpu.HBM(a.shape, a.dtype)
    res = pl.pallas_call(
        body, name=name,
        out_shape=(pltpu.SemaphoreType.DMA((n_sem,)), pltpu.SemaphoreType.DMA((n_sem,)), pltpu.SemaphoreType.DMA((n,)),
                   *[hbm(a) for a in arrays], *[hbm(a) for a in lands], _sds((8, 128))),
        in_specs=[_HBM] * (2 * n),
        out_specs=(_SEM, _SEM, _SEM, *[_HBM] * (2 * n), pl.BlockSpec(memory_space=pltpu.VMEM)),
        input_output_aliases={i: 3 + i for i in range(2 * n)},
        compiler_params=pltpu.CompilerParams(has_side_effects=_EFFECT),
    )(*[pltpu.with_memory_space_constraint(a, pltpu.HBM) for a in list(arrays) + lands])
    return (res[:3], res[3:3 + n], res[3 + n:3 + 2 * n], plan), res[-1]


def _exchange_wait(name, state, after):
    sems, srcs, lands, plan = state
    n = len(srcs)

    def body(*refs):
        srcs_, lands_ = refs[:n], refs[n:2 * n]
        send_sems, recv_sems, own_sems = refs[2 * n:2 * n + 3]
        for cp in _peer_copies(srcs_, lands_, send_sems, recv_sems, plan):
            cp.wait_send()
            cp.wait_recv()
        for cp in _own_copies(srcs_, lands_, own_sems, plan):
            cp.wait()

    hbm = lambda a: pltpu.HBM(a.shape, a.dtype)
    res = pl.pallas_call(
        body, name=name, out_shape=tuple(hbm(a) for a in list(srcs) + list(lands)),
        in_specs=[_HBM] * (2 * n) + [_SEM] * 3 + [pl.BlockSpec(memory_space=pl.ANY)] * len(after),
        out_specs=tuple([_HBM] * (2 * n)), input_output_aliases={i: i for i in range(2 * n)},
        compiler_params=pltpu.CompilerParams(has_side_effects=_EFFECT),
    )(*srcs, *lands, *sems, *after)
    return res[n:]


def _gather_start(name, arrays, plan, after):
    n = len(arrays)
    lands = plan.landing_zones(arrays)
    n_sem = n * (N_DEV - 1)

    def body(*refs):
        srcs, lands_ = refs[:n], refs[n:2 * n]
        send_sems, recv_sems, own_sems = refs[2 * n + len(after):2 * n + len(after) + 3]
        token = refs[-1]
        for cp in (_peer_copies(srcs, lands_, send_sems, recv_sems, plan, (_SIBLING,) + _SAME_CORE)
                   + _own_copies(srcs, lands_, own_sems, plan)):
            cp.start()
        token[...] = jnp.zeros_like(token)

    hbm = lambda a: pltpu.HBM(a.shape, a.dtype)
    res = pl.pallas_call(
        body, name=name,
        out_shape=(pltpu.SemaphoreType.DMA((n_sem,)), pltpu.SemaphoreType.DMA((n_sem,)), pltpu.SemaphoreType.DMA((n,)),
                   *[hbm(a) for a in arrays], *[hbm(a) for a in lands], _sds((8, 128))),
        in_specs=[_HBM] * (2 * n) + [pl.BlockSpec(memory_space=pl.ANY)] * len(after),
        out_specs=(_SEM, _SEM, _SEM, *[_HBM] * (2 * n), pl.BlockSpec(memory_space=pltpu.VMEM)),
        input_output_aliases={i: 3 + i for i in range(2 * n)},
        compiler_params=pltpu.CompilerParams(has_side_effects=_EFFECT),
    )(*[pltpu.with_memory_space_constraint(a, pltpu.HBM) for a in list(arrays) + lands], *after)
    return (res[:3], res[3:3 + n], res[3 + n:3 + 2 * n], plan), res[-1]


def _pass_on_copies(lands, send_sems, recv_sems, plan):
    x, y, c, _ = _mesh_pos()
    sibling = _peer(x, y, c, _SIBLING)[0]
    cps = []
    for i, r in enumerate(_SAME_CORE):
        owner = _peer(x, y, c, r)[1]
        for a in range(len(lands)):
            k = a * len(_SAME_CORE) + i
            cps.append(pltpu.make_async_remote_copy(
                src_ref=plan.dst(lands, a, owner), dst_ref=plan.dst(lands, a, owner), send_sem=send_sems.at[k],
                recv_sem=recv_sems.at[k], device_id=sibling, device_id_type=pl.DeviceIdType.MESH))
    return cps


def _gather_pass_on(name, state, after):
    sems, srcs, lands, plan = state
    n = len(srcs)
    n_sem = n * len(_SAME_CORE)

    def body(*refs):
        srcs_, lands_ = refs[:n], refs[n:2 * n]
        send_sems, recv_sems = refs[2 * n], refs[2 * n + 1]
        on_send, on_recv = refs[2 * n + 3 + len(after)], refs[2 * n + 4 + len(after)]
        token = refs[-1]
        arrivals = _peer_copies(srcs_, lands_, send_sems, recv_sems, plan, _SAME_CORE)
        for arrived, on in zip(arrivals, _pass_on_copies(lands_, on_send, on_recv, plan)):
            arrived.wait_recv()
            on.start()
        token[...] = jnp.zeros_like(token)

    hbm = lambda a: pltpu.HBM(a.shape, a.dtype)
    res = pl.pallas_call(
        body, name=name,
        out_shape=(pltpu.SemaphoreType.DMA((n_sem,)), pltpu.SemaphoreType.DMA((n_sem,)),
                   *[hbm(a) for a in list(srcs) + list(lands)], _sds((8, 128))),
        in_specs=[_HBM] * (2 * n) + [_SEM] * 3 + [pl.BlockSpec(memory_space=pl.ANY)] * len(after),
        out_specs=(_SEM, _SEM, *[_HBM] * (2 * n), pl.BlockSpec(memory_space=pltpu.VMEM)),
        input_output_aliases={i: 2 + i for i in range(2 * n)},
        compiler_params=pltpu.CompilerParams(has_side_effects=_EFFECT),
    )(*srcs, *lands, *sems, *after)
    return (sems, res[:2], res[2:2 + n], res[2 + n:2 + 2 * n], plan), res[-1]


def _gather_wait(name, state, after):
    sems, on_sems, srcs, lands, plan = state
    n = len(srcs)

    def body(*refs):
        srcs_, lands_ = refs[:n], refs[n:2 * n]
        send_sems, recv_sems, own_sems, on_send, on_recv = refs[2 * n:2 * n + 5]
        for cp in _peer_copies(srcs_, lands_, send_sems, recv_sems, plan, (_SIBLING,)):
            cp.wait_recv()
        for cp in _peer_copies(srcs_, lands_, send_sems, recv_sems, plan, (_SIBLING,) + _SAME_CORE):
            cp.wait_send()
        for cp in _own_copies(srcs_, lands_, own_sems, plan):
            cp.wait()
        for cp in _pass_on_copies(lands_, on_send, on_recv, plan):
            cp.wait_send()
            cp.wait_recv()

    hbm = lambda a: pltpu.HBM(a.shape, a.dtype)
    res = pl.pallas_call(
        body, name=name, out_shape=tuple(hbm(a) for a in list(srcs) + list(lands)),
        in_specs=[_HBM] * (2 * n) + [_SEM] * 5 + [pl.BlockSpec(memory_space=pl.ANY)] * len(after),
        out_specs=tuple([_HBM] * (2 * n)), input_output_aliases={i: i for i in range(2 * n)},
        compiler_params=pltpu.CompilerParams(has_side_effects=_EFFECT),
    )(*srcs, *lands, *sems, *on_sems, *after)
    return res[n:]


def _sum_parts(name, recv):
    def body(r_ref, o_ref):
        s = r_ref[0]
        for j in range(1, N_DEV):
            s = s + r_ref[j]
        o_ref[...] = s

    return pl.pallas_call(body, name=name, out_shape=_sds(recv.shape[1:], recv.dtype))(recv)


def _adam(name, recv, w, m, v, col=0):
    shape = w.shape
    R, C = shape
    tr, tc = R, C
    while N_DEV * tr * C * 4 > 3 * 512 * 1024 and tr % 32 == 0:
        tr //= 2
    if N_DEV * tr * C * 4 > 6 * 1024 * 1024 and C % 256 == 0:
        tc = 256
    nc = C // tc

    def body(recv_ref, w_ref, m_ref, v_ref, g_ref, d_ref, m2_ref, v2_ref):
        g = recv_ref[0].astype(F32)
        for j in range(1, N_DEV):
            g = g + recv_ref[j].astype(F32)
        g_ref[...], d_ref[...], m2_ref[...], v2_ref[...] = _adam_math(g, w_ref[...], m_ref[...], v_ref[...])

    blk = pl.BlockSpec((tr, tc), lambda i, j: (i, j))
    return _pcall(body, name=name, grid=(R // tr, nc),
                  in_specs=[pl.BlockSpec((N_DEV, tr, tc), lambda i, j: (0, i, col * nc + j)), blk, blk, blk],
                  out_specs=[blk] * 4, out_shape=[_sds(shape)] * 4)(recv, w, m, v)


def _adam_math(g, w_, m_, v_):
    m2 = ADAM_B1 * m_ + (1.0 - ADAM_B1) * g
    v2 = ADAM_B2 * v_ + (1.0 - ADAM_B2) * (g * g)
    m_hat = m2 / (1.0 - ADAM_B1 ** ADAM_STEP)
    v_hat = v2 / (1.0 - ADAM_B2 ** ADAM_STEP)
    return g, -ADAM_LR * (m_hat / (jnp.sqrt(v_hat) + ADAM_EPS) + ADAM_WD * w_), m2, v2


def _adam_rows(name, recv, ws, ms, vs, offs):
    k = len(ws)

    def body(recv_ref, *refs):
        outs = refs[3 * k:]
        for a in range(k):
            cols = slice(offs[a], offs[a] + ws[a].shape[1])
            g = recv_ref[0, :, cols].astype(F32)
            for j in range(1, N_DEV):
                g = g + recv_ref[j, :, cols].astype(F32)
            res = _adam_math(g, refs[a][...], refs[k + a][...], refs[2 * k + a][...])
            for t in range(4):
                outs[4 * a + t][...] = res[t]

    def whole(a):
        return pl.BlockSpec(a.shape, lambda i: (0,) * a.ndim)

    rows = list(ws) + list(ms) + list(vs)
    return _pcall(body, name=name, grid=(1,), in_specs=[whole(recv)] + [whole(a) for a in rows],
                  out_specs=[whole(w) for w in ws for _ in range(4)],
                  out_shape=[_sds(w.shape) for w in ws for _ in range(4)])(recv, *rows)


_NAMES = ['ffn1_pre_g', 'ffn1_w_in', 'ffn1_w_out', 'ffn1_post_g', 'mix_pre_g', 'w_in', 'w_fu', 'b_f', 'gla_norm_g',
          'w_pool', 'pool_scale', 'mem_norm_g', 'w_mem_kv', 'w_up_gla', 'w_up_pool', 'w_up_xattn', 'w_o', 'mix_post_g',
          'ffn2_pre_g', 'ffn2_w_in', 'ffn2_w_out', 'ffn2_post_g', 'final_g']
_SHARDED = ['ffn1_w_in', 'ffn1_w_out', 'w_in', 'w_fu', 'w_mem_kv', 'w_up_gla', 'w_up_pool', 'w_up_xattn', 'w_o',
            'ffn2_w_in', 'ffn2_w_out']
_COL_SHARDED = ['w_up_pool', 'w_up_xattn']


def _cols_to_full(g):
    return jnp.transpose(g, (1, 0, 2)).reshape(g.shape[1], N_DEV * g.shape[2])


def _full_to_cols(f):
    R, C = f.shape
    return jnp.transpose(f.reshape(R, N_DEV, C // N_DEV), (1, 0, 2))


def _to_internal(w_in_t):
    o = 0
    parts = []
    for s in IN_SPLITS:
        parts.append(w_in_t[o:o + s])
        o += s
    q, k, v, g_out, f_low, p_in, xq, gates = parts
    f_low = jnp.pad(f_low, ((0, FLOW_W - GATE_RANK), (0, 0)))
    return jnp.concatenate([q, k, v, g_out, gates, p_in, xq, f_low], axis=0)


def _from_internal(d, d_flow):
    q, k, v, g_out = d[0:512], d[512:1024], d[1024:2048], d[2048:3072]
    gates, p_in, xq = d[3072:6144], d[6144:6656], d[6656:7168]
    return jnp.concatenate([q, k, v, g_out, d_flow[:GATE_RANK], p_in, xq, gates], axis=0)


_FFN_IN = ('ffn1_w_in', 'ffn2_w_in')
_FFN_OUT = ('ffn1_w_out', 'ffn2_w_out')
_GATHERS = {"ffn1_in": ['ffn1_w_in'], "ffn1_out": ['ffn1_w_out'],
            "mix_in": ['w_in', 'w_fu'], "mix_rest": ['w_mem_kv', 'w_up_gla', 'w_up_pool', 'w_up_xattn', 'w_o'],
            "ffn2": ['ffn2_w_in', 'ffn2_w_out']}
_MIX = _GATHERS["mix_in"] + _GATHERS["mix_rest"]
_MIX_UP = ['w_up_gla', 'w_up_pool', 'w_up_xattn', 'w_o']
_MIX_LATE = [n for n in _MIX if n not in _MIX_UP]


def _ffn_in_slot(ref, d):
    return ref.at[2 * (d % N_FF_BLK) + d // N_FF_BLK, pl.ds(0, FF_BLK)]


def _ffn_out_slot(ref, d):
    rows = FF_BLK // 2
    return ref.at[d // 2, pl.ds(pl.multiple_of((d % 2) * rows, rows), rows)]


def _ffn_plan(names, scatter):
    slots, shapes = {}, {}
    for a, n in enumerate(names):
        if n in _FFN_IN:
            slots[a] = _ffn_in_slot
            shapes[a] = (FF_BLK, D_MODEL) if scatter else (N_DEV, FF_PAD, D_MODEL)
        elif n in _FFN_OUT:
            slots[a] = _ffn_out_slot
            shapes[a] = (FF_BLK // 2, D_MODEL) if scatter else (N_FF_BLK, FF_PAD, D_MODEL)
    return _Plan(scatter, slots, shapes)


def _rows_to_full(g):
    return g.reshape(N_DEV * g.shape[1], g.shape[2])


def _mix_weights(gathered):
    W = {}
    for n, g in gathered.items():
        if n == "w_in":
            W["w_int_t"] = _to_internal(g.reshape(IN_WIDTH, D_MODEL))
        elif n == "w_fu":
            W["w_fu_pad"] = jnp.pad(_cols_to_full(g), ((0, FLOW_W - GATE_RANK), (0, 0)))
        else:
            W[n] = _cols_to_full(g) if n in _COL_SHARDED else _rows_to_full(g)
    return W


def _mix_chunks(G, n):
    if n == "w_in":
        return _from_internal(*G["w_int_t"]).reshape(N_DEV, IN_SHARD, D_MODEL)
    if n == "w_fu":
        return _full_to_cols(G["w_fu_pad"][:GATE_RANK].astype(BF16))
    if n in _COL_SHARDED:
        return _full_to_cols(G[n])
    return G[n].reshape(N_DEV, G[n].shape[0] // N_DEV, G[n].shape[1])


def _step(x, mem, tgt, P, Mo, Vo):
    def native(n, a):
        return jnp.swapaxes(a, 0, 1) if n in _FFN_IN + ("w_in",) else a

    P, Mo, Vo = ({n: native(n, a) for n, a in d.items()} for d in (P, Mo, Vo))
    small = {n: P[n] for n in _NAMES if n not in _SHARDED}

    gather, tokens = {}, []
    for grp, names in _GATHERS.items():
        gather[grp], tok = _gather_start("gather_" + grp, [P[n].astype(BF16) for n in names], _ffn_plan(names, False),
                                         tokens[-1:])
        tokens.append(tok)

    def pass_on(grp, after):
        return _gather_pass_on("gather_" + grp + "_on", gather[grp], after)

    def gathered(grp, after):
        return _gather_wait("gather_" + grp + "_wait", pass_on(grp, after)[0], [])

    def ffn_wo(lands):
        return lambda act: lands[0].reshape(N_FF_BLK * FF_PAD, D_MODEL)

    h1 = _pre_norm("ffn1_pre", x, small["ffn1_pre_g"], tokens[:1])
    w1t = gathered("ffn1_in", tokens[-1:] + [h1])[0].reshape(N_DEV * FF_PAD, D_MODEL)
    (x1, hm), w1o, sv1 = _ffn_fwd("ffn1", x, h1, w1t, small["ffn1_post_g"],
                                  lambda act: ffn_wo(gathered("ffn1_out", [act]))(act), small["mix_pre_g"])

    Wm = {**small, **_mix_weights(dict(zip(_GATHERS["mix_in"], gathered("mix_in", [x1]))))}
    passed = {}

    def pass_on_later(proj):
        tokens = []
        for grp in ("mix_rest", "ffn2"):
            passed[grp], tok = pass_on(grp, [proj])
            tokens.append(tok)
        return tokens

    def mix_rest(after):
        names = _GATHERS["mix_rest"]
        return _mix_weights(dict(zip(names, _gather_wait("gather_mix_rest_wait", passed["mix_rest"], after))))

    x2, h2, Wm, svm = _mix_fwd(x1, hm, mem, Wm, small["ffn2_pre_g"], pass_on_later, mix_rest)
    w2t, w2o = _gather_wait("gather_ffn2_wait", passed["ffn2"], [x2])
    w2t = w2t.reshape(N_DEV * FF_PAD, D_MODEL)
    (loss, dx3, d_final_g), w2o, sv2 = _ffn_fwd("ffn2", x2, h2, w2t, small["ffn2_post_g"], ffn_wo([w2o]),
                                                small["final_g"], tgt)

    G = dict(final_g=d_final_g)
    scat = {}

    def start(grp, names, arrays):
        scat[grp] = names, _exchange_start("scatter_" + grp, arrays, _ffn_plan(names, True))
        return scat[grp][1][1]

    def ffn_starts(tag):
        return (lambda dwo: start(tag + "_out", [tag + "_w_out"], [dwo.reshape(N_FF_BLK, FF_PAD, D_MODEL)]),
                lambda dwt: start(tag + "_in", [tag + "_w_in"], [dwt.reshape(N_DEV, FF_PAD, D_MODEL)]))

    dx2, G["ffn2_pre_g"], G["ffn2_post_g"] = _ffn_bwd(
        "ffn2", dx3, x2, small["ffn2_pre_g"], w2t, w2o, small["ffn2_post_g"], sv2, *ffn_starts("ffn2"))
    dx1, Gm = _mix_bwd(dx2, x1, mem, Wm, svm,
                       lambda Gm: start("mix", _MIX_LATE, [_mix_chunks(Gm, n) for n in _MIX_LATE]),
                       lambda Gm: start("mix_up", _MIX_UP, [_mix_chunks(Gm, n) for n in _MIX_UP]))
    G.update(Gm)
    dx, G["ffn1_pre_g"], G["ffn1_post_g"] = _ffn_bwd(
        "ffn1", dx1, x, small["ffn1_pre_g"], w1t, w1o, small["ffn1_post_g"], sv1, *ffn_starts("ffn1"))

    vecs = sorted((n for n in small if P[n].ndim == 2), key=lambda n: -P[n].shape[1])
    packed = jnp.concatenate([G[n] for n in vecs] + [loss], axis=1)
    place, o = {}, 0
    for n in vecs:
        place[n] = o // P[n].shape[1]
        o += P[n].shape[1]
    scat["small"] = ["packed", "w_pool"], _exchange_start("gather_small_grads", [packed, G["w_pool"]], _GATHER)
    recv, outs = {}, {}
    done = [dx]
    for grp in ["ffn2_out", "ffn2_in", "mix_up", "mix", "ffn1_out", "ffn1_in", "small"]:
        names, (state, _) = scat[grp]
        recv.update(zip(names, _exchange_wait(("gather_" if grp == "small" else "scatter_") + grp + "_wait", state,
                                              done)))
        if grp == "small":
            loss = _sum_parts("loss_sum", recv["packed"][:, :, o:])[0, 0]
            names = [n for n in small if n not in place]
            res = _adam_rows("adam_rows", recv["packed"], [P[n] for n in vecs], [Mo[n] for n in vecs],
                             [Vo[n] for n in vecs], [place[n] * P[n].shape[1] for n in vecs])
            for a, n in enumerate(vecs):
                outs[n] = list(res[4 * a:4 * a + 4])
            done.append(res[-1])
        for n in names:
            shp = P[n].shape
            shp2 = shp if len(shp) == 2 else (shp[0] * shp[1], shp[2])
            r = recv["packed"] if n in place else recv[n].reshape((N_DEV,) + shp2)
            res = _adam("adam_" + n, r, P[n].reshape(shp2), Mo[n].reshape(shp2), Vo[n].reshape(shp2), place.get(n, 0))
            outs[n] = [native(n, t.reshape(shp)) for t in res]
            done.append(res[-1])
    return loss, dx, outs


def kernel(x, mem, ffn1_pre_g, ffn1_w_in, ffn1_w_out, ffn1_post_g, mix_pre_g, w_in, w_fu, b_f, gla_norm_g, w_pool, pool_scale, mem_norm_g, w_mem_kv, w_up_gla, w_up_pool, w_up_xattn, w_o, mix_post_g, ffn2_pre_g, ffn2_w_in, ffn2_w_out, ffn2_post_g, final_g, loss_target, m_ffn1_pre_g, m_ffn1_w_in, m_ffn1_w_out, m_ffn1_post_g, m_mix_pre_g, m_w_in, m_w_fu, m_b_f, m_gla_norm_g, m_w_pool, m_pool_scale, m_mem_norm_g, m_w_mem_kv, m_w_up_gla, m_w_up_pool, m_w_up_xattn, m_w_o, m_mix_post_g, m_ffn2_pre_g, m_ffn2_w_in, m_ffn2_w_out, m_ffn2_post_g, m_final_g, v_ffn1_pre_g, v_ffn1_w_in, v_ffn1_w_out, v_ffn1_post_g, v_mix_pre_g, v_w_in, v_w_fu, v_b_f, v_gla_norm_g, v_w_pool, v_pool_scale, v_mem_norm_g, v_w_mem_kv, v_w_up_gla, v_w_up_pool, v_w_up_xattn, v_w_o, v_mix_post_g, v_ffn2_pre_g, v_ffn2_w_in, v_ffn2_w_out, v_ffn2_post_g, v_final_g):
    params = [ffn1_pre_g, ffn1_w_in, ffn1_w_out, ffn1_post_g, mix_pre_g, w_in, w_fu, b_f, gla_norm_g, w_pool, pool_scale, mem_norm_g, w_mem_kv, w_up_gla, w_up_pool, w_up_xattn, w_o, mix_post_g, ffn2_pre_g, ffn2_w_in, ffn2_w_out, ffn2_post_g, final_g]
    moms = [m_ffn1_pre_g, m_ffn1_w_in, m_ffn1_w_out, m_ffn1_post_g, m_mix_pre_g, m_w_in, m_w_fu, m_b_f, m_gla_norm_g, m_w_pool, m_pool_scale, m_mem_norm_g, m_w_mem_kv, m_w_up_gla, m_w_up_pool, m_w_up_xattn, m_w_o, m_mix_post_g, m_ffn2_pre_g, m_ffn2_w_in, m_ffn2_w_out, m_ffn2_post_g, m_final_g]
    vars_ = [v_ffn1_pre_g, v_ffn1_w_in, v_ffn1_w_out, v_ffn1_post_g, v_mix_pre_g, v_w_in, v_w_fu, v_b_f, v_gla_norm_g, v_w_pool, v_pool_scale, v_mem_norm_g, v_w_mem_kv, v_w_up_gla, v_w_up_pool, v_w_up_xattn, v_w_o, v_mix_post_g, v_ffn2_pre_g, v_ffn2_w_in, v_ffn2_w_out, v_ffn2_post_g, v_final_g]
    P = {n: a[0] if a.ndim > 2 else a for n, a in zip(_NAMES, params)}
    Mo = {n: a[0] if a.ndim > 2 else a for n, a in zip(_NAMES, moms)}
    Vo = {n: a[0] if a.ndim > 2 else a for n, a in zip(_NAMES, vars_)}
    loss, dx, outs = _step(x[0], mem[0], loss_target[0], P, Mo, Vo)
    out = [loss, dx[None]]
    for kind in range(4):
        for n, p in zip(_NAMES, params):
            out.append(outs[n][kind].reshape(p.shape))
    return tuple(out)
```

```python
import functools

import jax
import jax.numpy as jnp
from jax import lax
from jax.experimental import pallas as pl
from jax.experimental.pallas import tpu as pltpu

F32 = jnp.float32
BF16 = jnp.bfloat16

N_DEV = 8
D_MODEL = 1024
D_FF = 2816
FF_BLK = 2 * D_FF // N_DEV
N_FF_BLK = D_FF // FF_BLK
FF_PAD = 768
CHUNK = 64
GLA_HEADS = 4
GLA_DK = 512
GLA_DV = 1024
GLA_HDK = 128
GLA_HDV = 256
GATE_RANK = 16
GATE_TEMP = 16.0
POOL_WINDOWS = (2, 4, 8, 16)
POOL_W = 512
POOL_GD = 128
POOL_HALO = 16
XA_HEADS = 4
XA_HD = 128
XA_W = 512
EPS = 1e-6
IN_SPLITS = (GLA_DK, GLA_DK, GLA_DV, GLA_DV, GATE_RANK, POOL_W, XA_W, 3 * D_MODEL)
IN_WIDTH = sum(IN_SPLITS)
IN_SHARD = IN_WIDTH // N_DEV
INT_W = 3072 + 3072 + 1024 + 128
INT_NBLK = 3
INT_BLK = INT_W // INT_NBLK
FLOW_W = 128
INT_MAIN = INT_W - FLOW_W
_POOL_COL = (3072 + 3072) // POOL_W
_XQ_COL = (3072 + 3072 + POOL_W) // XA_W

ADAM_LR = 0.001
ADAM_B1 = 0.9
ADAM_B2 = 0.999
ADAM_EPS = 1e-08
ADAM_WD = 0.01
ADAM_STEP = 10

VMEM_LIMIT = 56 * 1024 * 1024

_NN = (((1,), (0,)), ((), ()))
_NT = (((1,), (1,)), ((), ()))
_TN = (((0,), (0,)), ((), ()))


def _pcall(body, *, name, grid, in_specs, out_specs, out_shape, scratch=(), aliases=None):
    return pl.pallas_call(
        body, name=name, grid=grid, in_specs=in_specs, out_specs=out_specs, out_shape=out_shape,
        scratch_shapes=list(scratch), input_output_aliases=aliases or {},
        compiler_params=pltpu.CompilerParams(dimension_semantics=("arbitrary",) * len(grid),
                                             vmem_limit_bytes=VMEM_LIMIT))


def _dot(a, b, dims=_NN):
    return lax.dot_general(a.astype(BF16), b.astype(BF16), dims, preferred_element_type=F32)


def _mm(name, a, b, *, grid, a_blk, a_map, b_blk, b_map, o_shape, o_blk, o_map, dims, out_dtype=F32, deps=()):
    nk = grid[2]

    def body(a_ref, b_ref, *rest):
        o_ref, scr = rest[len(deps)], rest[len(deps) + 1:]
        p = _dot(a_ref[...], b_ref[...], dims)
        if nk == 1:
            o_ref[...] = p.astype(o_ref.dtype)
        else:
            acc = scr[0]
            k = pl.program_id(2)

            @pl.when(k == 0)
            def _():
                acc[...] = p

            @pl.when(k > 0)
            def _():
                acc[...] += p

            @pl.when(k == nk - 1)
            def _():
                o_ref[...] = acc[...].astype(o_ref.dtype)

    acc_shape = tuple(d for d in o_blk if d is not None)
    return _pcall(body, name=name, grid=grid,
                  in_specs=[pl.BlockSpec(a_blk, a_map), pl.BlockSpec(b_blk, b_map)]
                  + [pl.BlockSpec(memory_space=pl.ANY)] * len(deps),
                  out_specs=pl.BlockSpec(o_blk, o_map),
                  out_shape=jax.ShapeDtypeStruct(o_shape, out_dtype),
                  scratch=[pltpu.VMEM(acc_shape, F32)] if nk > 1 else [])(a, b, *deps)


def _mm_nn(name, a, b, out_dtype=F32, tm=1024, tn=None):
    M, K = a.shape
    N = b.shape[1]
    tm, tn = min(tm, M), (tn or N)
    return _mm(name, a, b, grid=(N // tn, M // tm, 1), a_blk=(tm, K), a_map=lambda j, i, k: (i, 0),
               b_blk=(K, tn), b_map=lambda j, i, k: (0, j), o_shape=(M, N), o_blk=(tm, tn),
               o_map=lambda j, i, k: (i, j), dims=_NN, out_dtype=out_dtype)


def _mm_nt(name, a, b, out_dtype=F32, tm=1024):
    M, K = a.shape
    N = b.shape[0]
    tm = min(tm, M)
    return _mm(name, a, b, grid=(1, M // tm, 1), a_blk=(tm, K), a_map=lambda j, i, k: (i, 0),
               b_blk=(N, K), b_map=lambda j, i, k: (0, 0), o_shape=(M, N), o_blk=(tm, N),
               o_map=lambda j, i, k: (i, 0), dims=_NT, out_dtype=out_dtype)


def _mm_tn(name, a, b, out_dtype=BF16, ts=512, tn=None, tn_a=None, deps=()):
    S, M = a.shape
    N = b.shape[1]
    ts, tn, tn_a = min(ts, S), (tn or N), (tn_a or M)
    return _mm(name, a, b, grid=((N // tn) * (M // tn_a), 1, S // ts), a_blk=(ts, tn_a),
               a_map=lambda j, i, k: (k, j if tn_a < M else 0), b_blk=(ts, tn),
               b_map=lambda j, i, k: (k, j if tn < N else 0), o_shape=(M, N), o_blk=(tn_a, tn),
               o_map=lambda j, i, k: (j, 0) if tn_a < M else (0, j), dims=_TN, out_dtype=out_dtype, deps=deps)


class _Win:
    def __init__(self, arr, w, c):
        self.arr, self.w, self.c = arr, w, c


def _row_spec(x, tm):
    if isinstance(x, _Win):
        return x.arr, pl.BlockSpec((tm, x.w), functools.partial(lambda i, c: (i, c), c=x.c))
    if x.ndim == 3:
        return x, pl.BlockSpec((x.shape[0], tm, x.shape[2]), lambda i: (0, i, 0))
    return x, pl.BlockSpec((tm, x.shape[1]), lambda i: (i, 0))


def _rowwise(name, fn, rows, consts, outs, accs=(), tm=512):
    first = rows[0].arr if isinstance(rows[0], _Win) else rows[0]
    S = first.shape[1] if first.ndim == 3 else first.shape[0]
    tm = min(tm, S)
    n_in, n_out = len(rows) + len(consts), len(outs)
    arrays, in_specs = [], []
    for r in rows:
        arr, spec = _row_spec(r, tm)
        arrays.append(arr)
        in_specs.append(spec)
    for c in consts:
        arrays.append(c)
        in_specs.append(pl.BlockSpec(c.shape, functools.partial(lambda i, n: (0,) * n, n=c.ndim)))
    aliases = {}
    for k, o in enumerate(outs):
        if isinstance(o, _Win):
            aliases[len(arrays)] = k
            arrays.append(o.arr)
            in_specs.append(pl.BlockSpec(memory_space=pl.ANY))
    n_thru = len(aliases)
    out_specs = [_row_spec(o, tm)[1] for o in outs]
    out_specs += [pl.BlockSpec(a.shape, functools.partial(lambda i, n: (0,) * n, n=len(a.shape))) for a in accs]
    out_shape = [_sds(o.arr.shape, o.arr.dtype) if isinstance(o, _Win) else _sds(o.shape, o.dtype) for o in outs]

    def body(*refs):
        res = fn(*[r[...] for r in refs[:n_in]])
        if not isinstance(res, (tuple, list)):
            res = (res,)
        orefs = refs[n_in + n_thru:]
        for r, v in zip(orefs[:n_out], res[:n_out]):
            r[...] = v.astype(r.dtype)
        i = pl.program_id(0)
        for r, v in zip(orefs[n_out:], res[n_out:]):
            @pl.when(i == 0)
            def _(r=r, v=v):
                r[...] = v.astype(r.dtype)

            @pl.when(i > 0)
            def _(r=r, v=v):
                r[...] += v.astype(r.dtype)

    return _pcall(body, name=name, grid=(S // tm,), in_specs=in_specs, out_specs=out_specs,
                  out_shape=out_shape + [_sds(a.shape, a.dtype) for a in accs], aliases=aliases)(*arrays)


def _sds(shape, dtype=F32):
    return jax.ShapeDtypeStruct(shape, dtype)


def _rms(x, g):
    return x * lax.rsqrt(jnp.mean(x * x, axis=-1, keepdims=True) + EPS) * g


def _rms_bwd(x, g, dy):
    _, vjp = jax.vjp(_rms, x, g)
    return vjp(dy)


def _sigmoid(x):
    return 0.5 * jnp.tanh(0.5 * x) + 0.5


def _silu(x):
    return x * _sigmoid(x)


def _log_sigmoid(f):
    return jnp.minimum(f, 0.0) - jnp.log(1.0 + jnp.exp(-jnp.abs(f)))


def _head_rms_gate(o, g_out, gn):
    parts = [_rms(o[:, h * GLA_HDV:(h + 1) * GLA_HDV], gn[:, h * GLA_HDV:(h + 1) * GLA_HDV]) for h in range(GLA_HEADS)]
    return jnp.concatenate(parts, axis=-1) * _silu(g_out.astype(F32))


def _merge(gates, ya, yb, yc):
    gates, ya, yb, yc = (t.astype(F32) for t in (gates, ya, yb, yc))
    return (_sigmoid(gates[:, :D_MODEL]) * ya + _sigmoid(gates[:, D_MODEL:2 * D_MODEL]) * yb
            + _sigmoid(gates[:, 2 * D_MODEL:]) * yc)


def _tri_dot(t, x):
    hi = x.astype(BF16)
    r1 = x - hi.astype(F32)
    mid = r1.astype(BF16)
    lo = (r1 - mid.astype(F32)).astype(BF16)
    d = functools.partial(lax.dot_general, dimension_numbers=_NN, preferred_element_type=F32)
    return d(t, hi) + d(t, mid) + d(t, lo)


def _pre_norm(name, x, g, after):
    S = x.shape[0]
    tm = min(1024, S)

    def body(x_ref, g_ref, *rest):
        rest[-1][...] = _rms(x_ref[...], g_ref[...]).astype(BF16)

    row = pl.BlockSpec((tm, D_MODEL), lambda i: (i, 0))
    return _pcall(body, name=name, grid=(S // tm,),
                  in_specs=[row, pl.BlockSpec((1, D_MODEL), lambda i: (0, 0))] + [pl.BlockSpec(memory_space=pl.ANY)] * len(after),
                  out_specs=row, out_shape=_sds(x.shape, BF16))(x, g, *after)


def _ffn_fwd(tag, x, h, wt, gpost, get_wo, g_next, tgt=None):
    S = x.shape[0]
    tm = min(512, S)

    def up_body(h_ref, w_ref, u_ref, act_ref):
        hh = h_ref[...]
        for j in range(N_FF_BLK):
            ab = _dot(hh, w_ref[2 * FF_PAD * j:2 * FF_PAD * (j + 1), :], _NT)
            u_ref[:, 2 * FF_PAD * j:2 * FF_PAD * (j + 1)] = ab.astype(BF16)
            act_ref[:, FF_PAD * j:FF_PAD * (j + 1)] = (_silu(ab[:, :FF_PAD]) * ab[:, FF_PAD:]).astype(BF16)

    def down_head_body(act_ref, w_ref, x_ref, g_ref, gn_ref, t_ref, f_ref, dxo_ref, loss_ref, dgn_ref):
        i = pl.program_id(0)
        f = _dot(act_ref[...], w_ref[...])
        f_ref[...] = f.astype(BF16)
        xo = x_ref[...] + 0.5 * _rms(f, g_ref[...])
        out, vjp = jax.vjp(_rms, xo, gn_ref[...])
        e = out - t_ref[...]
        loss = 0.5 * jnp.sum(jnp.mean(e * e, axis=-1, keepdims=True), axis=0, keepdims=True)
        dxo_ref[...], dgn = vjp(e * (1.0 / D_MODEL))

        @pl.when(i == 0)
        def _():
            loss_ref[...] = jnp.broadcast_to(loss, loss_ref.shape)
            dgn_ref[...] = dgn

        @pl.when(i > 0)
        def _():
            loss_ref[...] += jnp.broadcast_to(loss, loss_ref.shape)
            dgn_ref[...] += dgn

    if tgt is not None:
        def last_body(h_ref, wt_ref, wo_ref, x_ref, g_ref, gn_ref, t_ref, u_ref, act_ref, f_ref, dxo_ref, loss_ref,
                      dgn_ref):
            up_body(h_ref, wt_ref, u_ref, act_ref)
            down_head_body(act_ref, wo_ref, x_ref, g_ref, gn_ref, t_ref, f_ref, dxo_ref, loss_ref, dgn_ref)

        wo = get_wo(None)
        tl = min(256, S)
        row = pl.BlockSpec((tl, D_MODEL), lambda i: (i, 0))
        vec = pl.BlockSpec((1, D_MODEL), lambda i: (0, 0))
        u, act, f, dxo, loss, dg_next = _pcall(
            last_body, name=tag + "_fwd", grid=(S // tl,),
            in_specs=[row, pl.BlockSpec(wt.shape, lambda i: (0, 0), pipeline_mode=pl.Buffered(1)),
                      pl.BlockSpec(wo.shape, lambda i: (0, 0), pipeline_mode=pl.Buffered(1)), row, vec, vec, row],
            out_specs=[pl.BlockSpec((tl, N_DEV * FF_PAD), lambda i: (i, 0)),
                       pl.BlockSpec((tl, N_FF_BLK * FF_PAD), lambda i: (i, 0)), row, row,
                       pl.BlockSpec((1, 128), lambda i: (0, 0)), vec],
            out_shape=[_sds((S, N_DEV * FF_PAD), BF16), _sds((S, N_FF_BLK * FF_PAD), BF16), _sds((S, D_MODEL), BF16),
                       _sds((S, D_MODEL)), _sds((1, 128)), _sds((1, D_MODEL))],
        )(h, wt, wo, x, gpost, g_next, tgt)
        return (loss, dxo, dg_next), wo, (h, u, act, f)

    u, act = _pcall(
        up_body, name=tag + "_up", grid=(S // tm,),
        in_specs=[pl.BlockSpec((tm, D_MODEL), lambda i: (i, 0)),
                  pl.BlockSpec(wt.shape, lambda i: (0, 0), pipeline_mode=pl.Buffered(1))],
        out_specs=[pl.BlockSpec((tm, N_DEV * FF_PAD), lambda i: (i, 0)),
                   pl.BlockSpec((tm, N_FF_BLK * FF_PAD), lambda i: (i, 0))],
        out_shape=[_sds((S, N_DEV * FF_PAD), BF16), _sds((S, N_FF_BLK * FF_PAD), BF16)],
    )(h, wt)

    def down_body(act_ref, w_ref, x_ref, g_ref, gn_ref, f_ref, xo_ref, hn_ref):
        f = _dot(act_ref[...], w_ref[...])
        f_ref[...] = f.astype(BF16)
        xo = x_ref[...] + 0.5 * _rms(f, g_ref[...])
        xo_ref[...] = xo
        hn_ref[...] = _rms(xo, gn_ref[...]).astype(BF16)

    wo = get_wo(act)
    row = pl.BlockSpec((tm, D_MODEL), lambda i: (i, 0))
    vec = pl.BlockSpec((1, D_MODEL), lambda i: (0, 0))
    f, xo, h_next = _pcall(
        down_body, name=tag + "_down", grid=(S // tm,),
        in_specs=[pl.BlockSpec((tm, N_FF_BLK * FF_PAD), lambda i: (i, 0)),
                  pl.BlockSpec(wo.shape, lambda i: (0, 0), pipeline_mode=pl.Buffered(1)), row, vec, vec],
        out_specs=[row, row, row], out_shape=[_sds((S, D_MODEL), BF16), _sds((S, D_MODEL)), _sds((S, D_MODEL), BF16)],
    )(act, wo, x, gpost, g_next)
    return (xo, h_next), wo, (h, u, act, f)


def _ffn_bwd(tag, dxo, x, gpre, wt, wo, gpost, saved, on_dwo=None, on_dwt=None):
    h, u, act, f = saved
    S = x.shape[0]
    tm = min(512, S)

    def dact_body(f_ref, dxo_ref, g_ref, w_ref, u_ref, df_ref, du_ref, dg_ref):
        i = pl.program_id(0)
        df, dg = _rms_bwd(f_ref[...].astype(F32), g_ref[...], 0.5 * dxo_ref[...])
        df = df.astype(BF16)
        df_ref[...] = df

        @pl.when(i == 0)
        def _():
            dg_ref[...] = dg

        @pl.when(i > 0)
        def _():
            dg_ref[...] += dg

        for j in range(N_FF_BLK):
            dact = _dot(df, w_ref[FF_PAD * j:FF_PAD * (j + 1), :], _NT)
            ab = u_ref[:, 2 * FF_PAD * j:2 * FF_PAD * (j + 1)].astype(F32)
            a, b = ab[:, :FF_PAD], ab[:, FF_PAD:]
            sg = _sigmoid(a)
            du_ref[:, 2 * FF_PAD * j:2 * FF_PAD * j + FF_PAD] = (dact * b * (sg * (1.0 + a * (1.0 - sg)))).astype(BF16)
            du_ref[:, 2 * FF_PAD * j + FF_PAD:2 * FF_PAD * (j + 1)] = (dact * (a * sg)).astype(BF16)

    row = pl.BlockSpec((tm, D_MODEL), lambda i: (i, 0))
    vec = pl.BlockSpec((1, D_MODEL), lambda i: (0, 0))
    u_spec = pl.BlockSpec((tm, N_DEV * FF_PAD), lambda i: (i, 0))
    df, du, dgpost = _pcall(
        dact_body, name=tag + "_dact", grid=(S // tm,),
        in_specs=[row, row, vec, pl.BlockSpec(wo.shape, lambda i: (0, 0), pipeline_mode=pl.Buffered(1)), u_spec],
        out_specs=[row, u_spec, vec],
        out_shape=[_sds((S, D_MODEL), BF16), _sds((S, N_DEV * FF_PAD), BF16), _sds((1, D_MODEL))])(f, dxo, gpost, wo, u)
    dwo = _mm_tn(tag + "_dwo", act, df, ts=S, tn_a=FF_PAD)
    dwt = _mm_tn(tag + "_dwt", du, h, ts=S, tn_a=FF_PAD, deps=[on_dwo(dwo)] if on_dwo is not None else [])
    dx, dgpre = _dh_pre_norm_bwd(tag + "_dh", [du], wt, x, dxo, gpre, [on_dwt(dwt)] if on_dwt is not None else [])
    return dx, dgpre, dgpost


def _dh_pre_norm_bwd(name, dzs, wt, x, dres, g, deps):
    S = x.shape[0]
    ks = [dz.shape[1] for dz in dzs]

    def body(*refs):
        dz_refs = refs[:len(dzs)]
        w_ref, x_ref, dres_ref, g_ref = refs[len(dzs):len(dzs) + 4]
        dx_ref, dg_ref = refs[len(dzs) + 4 + len(deps):]
        i = pl.program_id(0)
        dh, o = None, 0
        for dz_ref, k in zip(dz_refs, ks):
            p = _dot(dz_ref[...], w_ref[o:o + k, :])
            dh, o = (p if dh is None else dh + p), o + k
        dx, dg = _rms_bwd(x_ref[...], g_ref[...], dh)
        dx_ref[...] = dx + dres_ref[...]

        @pl.when(i == 0)
        def _():
            dg_ref[...] = dg

        @pl.when(i > 0)
        def _():
            dg_ref[...] += dg

    th = min(512, S)
    row = pl.BlockSpec((th, D_MODEL), lambda i: (i, 0))
    vec = pl.BlockSpec((1, D_MODEL), lambda i: (0, 0))
    return _pcall(
        body, name=name, grid=(S // th,),
        in_specs=[pl.BlockSpec((th, k), lambda i: (i, 0)) for k in ks]
        + [pl.BlockSpec(wt.shape, lambda i: (0, 0), pipeline_mode=pl.Buffered(1)), row, row, vec]
        + [pl.BlockSpec(memory_space=pl.ANY)] * len(deps),
        out_specs=[row, vec], out_shape=[_sds((S, D_MODEL)), _sds((1, D_MODEL))])(*dzs, wt, x, dres, g, *deps)


GLA_G = 8


def _gla_tile_common(k, flow, wfu, bf):
    f = _dot(flow, wfu) + bf
    la = _log_sigmoid(f) * (1.0 / GATE_TEMP)
    tri = _tri_matrix(True)
    ws, ds = [], []
    for g in range(k.shape[0] // CHUNK):
        b = _tri_dot(tri, la[g * CHUNK:(g + 1) * CHUNK])
        b_end = b[CHUNK - 1:CHUNK, :]
        ws.append(jnp.exp(b_end - b))
        ds.append(jnp.exp(b_end))
    w = jnp.concatenate(ws, axis=0)
    return f, w, k * w, ds


def _tri_matrix(lower):
    r = lax.broadcasted_iota(jnp.int32, (CHUNK, CHUNK), 0)
    c = lax.broadcasted_iota(jnp.int32, (CHUNK, CHUNK), 1)
    return jnp.where((r >= c) if lower else (r <= c), 1.0, 0.0).astype(BF16)


def _heads():
    return [(slice(h * GLA_HDK, (h + 1) * GLA_HDK), slice(h * GLA_HDV, (h + 1) * GLA_HDV)) for h in range(GLA_HEADS)]


def _gla_fwd(proj, wfu, bf, gn, deps=()):
    S = proj.shape[0]
    G = min(GLA_G, S // CHUNK)
    T = G * CHUNK
    nc = S // CHUNK

    def body(q_ref, k_ref, v_ref, go_ref, flow_ref, wfu_ref, bf_ref, gn_ref, *rest):
        ya_ref, st_ref, state = rest[len(deps):]
        @pl.when(pl.program_id(0) == 0)
        def _():
            state[...] = jnp.zeros_like(state)

        _, _, kt, ds = _gla_tile_common(k_ref[...], flow_ref[...], wfu_ref[...], bf_ref[...])
        q = q_ref[...].astype(F32) * (GLA_HDK ** -0.5)
        v = v_ref[...]
        rows = [slice(g * CHUNK, (g + 1) * CHUNK) for g in range(G)]
        kv = [[_dot(v[r, vs], kt[r, ks], _TN) for ks, vs in _heads()] for r in rows]
        st = [state[vs, :] for _, vs in _heads()]
        o = []
        for g, r in enumerate(rows):
            outs = []
            for h, (ks, vs) in enumerate(_heads()):
                st[h] = st[h] * ds[g][:, ks] + kv[g][h]
                st_ref[g, vs, :] = st[h]
                outs.append(_dot(q[r, ks], st[h], _NT))
            o.append(jnp.concatenate(outs, axis=-1))
        for h, (_, vs) in enumerate(_heads()):
            state[vs, :] = st[h]
        ya_ref[...] = _head_rms_gate(jnp.concatenate(o, axis=0), go_ref[...], gn_ref[...]).astype(BF16)

    return _pcall(
        body, name="gla_fwd", grid=(S // T,),
        in_specs=[pl.BlockSpec((T, GLA_DK), lambda c: (c, 0)), pl.BlockSpec((T, GLA_DK), lambda c: (c, 1)),
                  pl.BlockSpec((T, GLA_DV), lambda c: (c, 1)), pl.BlockSpec((T, GLA_DV), lambda c: (c, 2)),
                  pl.BlockSpec((T, FLOW_W), lambda c: (c, (INT_W - FLOW_W) // FLOW_W)),
                  pl.BlockSpec(wfu.shape, lambda c: (0, 0)), pl.BlockSpec(bf.shape, lambda c: (0, 0)),
                  pl.BlockSpec(gn.shape, lambda c: (0, 0))]
        + [pl.BlockSpec(memory_space=pl.ANY)] * len(deps),
        out_specs=[pl.BlockSpec((T, GLA_DV), lambda c: (c, 0)),
                   pl.BlockSpec((G, GLA_DV, GLA_HDK), lambda c: (c, 0, 0))],
        out_shape=[_sds((S, GLA_DV), BF16), _sds((nc, GLA_DV, GLA_HDK))],
        scratch=[pltpu.VMEM((GLA_DV, GLA_HDK), F32)])(proj, proj, proj, proj, proj, wfu, bf, gn, *deps)


def _gla_bwd(proj, wfu, bf, gn, states, d_ya_in, dmain):
    S = proj.shape[0]
    G = min(GLA_G, S // CHUNK)
    T = G * CHUNK
    nt = S // T

    def body(q_ref, k_ref, v_ref, go_ref, flow_ref, wfu_ref, bf_ref, gn_ref, st_ref, stp_ref, dya_ref, dmain_in,
             dqkvg_ref, dflow_ref, dwfu_ref, dbf_ref, dgn_ref, dstate):
        dq_ref = dqkvg_ref.at[:, 0:GLA_DK]
        dk_ref = dqkvg_ref.at[:, GLA_DK:2 * GLA_DK]
        dv_ref = dqkvg_ref.at[:, 2 * GLA_DK:2 * GLA_DK + GLA_DV]
        dgo_ref = dqkvg_ref.at[:, 2 * GLA_DK + GLA_DV:]
        step = pl.program_id(0)

        @pl.when(step == 0)
        def _():
            dstate[...] = jnp.zeros_like(dstate)
            dwfu_ref[...] = jnp.zeros_like(dwfu_ref)
            dbf_ref[...] = jnp.zeros_like(dbf_ref)
            dgn_ref[...] = jnp.zeros_like(dgn_ref)

        flow, wfu_v = flow_ref[...], wfu_ref[...]
        f, w, kt, ds = _gla_tile_common(k_ref[...], flow, wfu_v, bf_ref[...])
        q = q_ref[...].astype(F32) * (GLA_HDK ** -0.5)
        v = v_ref[...]
        rows = [slice(g * CHUNK, (g + 1) * CHUNK) for g in range(G)]
        o = jnp.concatenate([jnp.concatenate([_dot(q[r, ks], st_ref[g, vs, :], _NT) for ks, vs in _heads()], axis=-1)
                             for g, r in enumerate(rows)], axis=0)
        _, vjp = jax.vjp(_head_rms_gate, o, go_ref[...], gn_ref[...])
        dout, dgo, dgn = vjp(dya_ref[...].astype(F32))
        dgo_ref[...] = dgo.astype(dgo_ref.dtype)
        dgn_ref[...] += dgn
        dq = [jnp.concatenate([_dot(dout[r, vs], st_ref[g, vs, :]) for _, vs in _heads()], axis=-1)
              for g, r in enumerate(rows)]
        qdo = [[_dot(dout[r, vs], q[r, ks], _TN) for ks, vs in _heads()] for r in rows]
        dq_ref[...] = (jnp.concatenate(dq, axis=0) * (GLA_HDK ** -0.5)).astype(dq_ref.dtype)
        has_prev = (step < nt - 1).astype(F32)
        carry = [dstate[vs, :] for _, vs in _heads()]
        dkt, dv, dd = [None] * G, [None] * G, [None] * G
        for g in reversed(range(G)):
            r = rows[g]
            dkts, dvs, dds = [], [], []
            for h, (ks, vs) in enumerate(_heads()):
                dst = carry[h] + qdo[g][h]
                dkts.append(_dot(v[r, vs], dst))
                dvs.append(_dot(kt[r, ks], dst, _NT))
                st_prev = st_ref[g - 1, vs, :] if g > 0 else stp_ref[vs, :] * has_prev
                dds.append(jnp.sum(dst * st_prev, axis=0, keepdims=True))
                carry[h] = dst * ds[g][:, ks]
            dkt[g], dv[g], dd[g] = (jnp.concatenate(t, axis=-1) for t in (dkts, dvs, dds))
        for h, (_, vs) in enumerate(_heads()):
            dstate[vs, :] = carry[h]
        dkt = jnp.concatenate(dkt, axis=0)
        dv_ref[...] = jnp.concatenate(dv, axis=0).astype(dv_ref.dtype)
        dk_ref[...] = (dkt * w).astype(dk_ref.dtype)
        de = dkt * kt
        tri = _tri_matrix(False)
        dla = []
        for g, r in enumerate(rows):
            db_end = jnp.sum(de[r], axis=0, keepdims=True) + dd[g] * ds[g]
            dla.append(db_end - _tri_dot(tri, de[r]))
        df = jnp.concatenate(dla, axis=0) * (1.0 - _sigmoid(f)) * (1.0 / GATE_TEMP)
        dflow_ref[...] = _dot(df, wfu_v, _NT).astype(dflow_ref.dtype)
        dwfu_ref[...] += _dot(flow, df, _TN)
        dbf_ref[...] += jnp.sum(df, axis=0, keepdims=True)

    rt = lambda s: nt - 1 - s
    return _pcall(
        body, name="gla_bwd", grid=(nt,),
        in_specs=[pl.BlockSpec((T, GLA_DK), lambda s: (rt(s), 0)), pl.BlockSpec((T, GLA_DK), lambda s: (rt(s), 1)),
                  pl.BlockSpec((T, GLA_DV), lambda s: (rt(s), 1)), pl.BlockSpec((T, GLA_DV), lambda s: (rt(s), 2)),
                  pl.BlockSpec((T, FLOW_W), lambda s: (rt(s), (INT_W - FLOW_W) // FLOW_W)),
                  pl.BlockSpec(wfu.shape, lambda s: (0, 0)), pl.BlockSpec(bf.shape, lambda s: (0, 0)),
                  pl.BlockSpec(gn.shape, lambda s: (0, 0)),
                  pl.BlockSpec((G, GLA_DV, GLA_HDK), lambda s: (rt(s), 0, 0)),
                  pl.BlockSpec((None, GLA_DV, GLA_HDK), lambda s: (jnp.maximum(rt(s) * G - 1, 0), 0, 0)),
                  pl.BlockSpec((T, GLA_DV), lambda s: (rt(s), 0)), pl.BlockSpec(memory_space=pl.ANY)],
        out_specs=[pl.BlockSpec((T, 2 * GLA_DK + 2 * GLA_DV), lambda s: (rt(s), 0)),
                   pl.BlockSpec((T, FLOW_W), lambda s: (rt(s), 0)),
                   pl.BlockSpec(wfu.shape, lambda s: (0, 0)), pl.BlockSpec(bf.shape, lambda s: (0, 0)),
                   pl.BlockSpec(gn.shape, lambda s: (0, 0))],
        out_shape=[_sds(dmain.shape, dmain.dtype), _sds((S, FLOW_W), BF16), _sds(wfu.shape), _sds(bf.shape),
                   _sds(gn.shape)],
        scratch=[pltpu.VMEM((GLA_DV, GLA_HDK), F32)], aliases={11: 0},
    )(proj, proj, proj, proj, proj, wfu, bf, gn, states, states, d_ya_in, dmain)


def _pool_counts(tm, i):
    t = (lax.broadcasted_iota(jnp.int32, (tm, POOL_GD), 0) + i * tm + 1).astype(F32)
    return [jnp.minimum(t, float(w)) for w in POOL_WINDOWS]


def _pool_fwd(proj, w_pool, pool_scale, tm=1024):
    S = proj.shape[0]
    tm = min(tm, S // 2)
    col = _POOL_COL
    hb = tm // POOL_HALO

    def body(p_ref, halo_ref, wp_ref, sc_ref, mixed_ref, out_ref):
        i = pl.program_id(0)
        p = p_ref[...].astype(F32)
        halo = halo_ref[...].astype(F32) * (i > 0).astype(F32)
        ext = jnp.concatenate([halo, p], axis=0)
        n = tm + POOL_HALO
        sums, acc, k = {}, ext, 1
        while k < POOL_WINDOWS[-1]:
            acc = acc + pltpu.roll(acc, k, axis=0)
            k *= 2
            sums[k] = acc
        cnts = _pool_counts(tm, i)
        mixed, lin = [], []
        for g, w in enumerate(POOL_WINDOWS):
            ls = slice(g * POOL_GD, (g + 1) * POOL_GD)
            m = sums[w][POOL_HALO:n, ls] / cnts[g] - p[:, ls]
            mixed.append(m)
            lin.append(_dot(m, wp_ref[g]))
        mixed_ref[...] = jnp.concatenate(mixed, axis=-1)
        out_ref[...] = (jnp.concatenate(lin, axis=-1) * sc_ref[...]).astype(out_ref.dtype)

    return _pcall(
        body, name="pool_fwd", grid=(S // tm,),
        in_specs=[pl.BlockSpec((tm, POOL_W), lambda i: (i, col)),
                  pl.BlockSpec((POOL_HALO, POOL_W), lambda i: (jnp.maximum(i * hb - 1, 0), col)),
                  pl.BlockSpec(w_pool.shape, lambda i: (0, 0, 0)), pl.BlockSpec(pool_scale.shape, lambda i: (0, 0))],
        out_specs=[pl.BlockSpec((tm, POOL_W), lambda i: (i, 0)), pl.BlockSpec((tm, POOL_W), lambda i: (i, 0))],
        out_shape=[_sds((S, POOL_W)), _sds((S, POOL_W), BF16)])(proj, proj, w_pool, pool_scale)


def _pool_lin_bwd(dout, mixed, w_pool, pool_scale):
    S = dout.shape[0]

    def fn(dout, mixed, wp, sc):
        dlin = dout * sc
        dm, dwp, lin = [], [], []
        for g in range(len(POOL_WINDOWS)):
            ls = slice(g * POOL_GD, (g + 1) * POOL_GD)
            lin.append(_dot(mixed[:, ls], wp[g]))
            dm.append(_dot(dlin[:, ls], wp[g], _NT))
            dwp.append(_dot(mixed[:, ls], dlin[:, ls], _TN))
        dsc = jnp.sum(dout * jnp.concatenate(lin, axis=-1), axis=0, keepdims=True)
        return jnp.concatenate(dm, axis=-1), jnp.concatenate(dwp, axis=0), dsc

    return _rowwise("pool_lin_bwd", fn, [dout, mixed], [w_pool, pool_scale], [_sds((S, POOL_W))],
                    [_sds((len(POOL_WINDOWS) * POOL_GD, POOL_GD)), _sds((1, POOL_W))], tm=1024)


def _pool_win_bwd(dmixed, dmain, tm=1024):
    S = dmixed.shape[0]
    tm = min(tm, S // 2)
    nt = S // tm
    hb = tm // POOL_HALO

    def body(dm_ref, halo_ref, dmain_in, dp_ref):
        i = pl.program_id(0)
        dm = dm_ref[...]
        halo = halo_ref[...] * (i < nt - 1).astype(F32)
        cnts = _pool_counts(tm, i)
        cnts_h = [c[:POOL_HALO] for c in _pool_counts(tm, i + 1)]
        r = jnp.concatenate([jnp.concatenate([dm[:, g * POOL_GD:(g + 1) * POOL_GD] / cnts[g] for g in range(4)], axis=-1),
                             jnp.concatenate([halo[:, g * POOL_GD:(g + 1) * POOL_GD] / cnts_h[g] for g in range(4)], axis=-1)],
                            axis=0)
        n = tm + POOL_HALO
        sums, acc, k = {}, r, 1
        while k < POOL_WINDOWS[-1]:
            acc = acc + pltpu.roll(acc, n - k, axis=0)
            k *= 2
            sums[k] = acc
        dp = [sums[w][:tm, g * POOL_GD:(g + 1) * POOL_GD] for g, w in enumerate(POOL_WINDOWS)]
        dp_ref[...] = (jnp.concatenate(dp, axis=-1) - dm).astype(dp_ref.dtype)

    return _pcall(
        body, name="pool_win_bwd", grid=(nt,),
        in_specs=[pl.BlockSpec((tm, POOL_W), lambda i: (i, 0)),
                  pl.BlockSpec((POOL_HALO, POOL_W), lambda i: (jnp.minimum((i + 1) * hb, S // POOL_HALO - 1), 0)),
                  pl.BlockSpec(memory_space=pl.ANY)],
        out_specs=pl.BlockSpec((tm, POOL_W), lambda i: (i, _POOL_COL)),
        out_shape=_sds(dmain.shape, dmain.dtype), aliases={2: 0})(dmixed, dmixed, dmain)


def _xattn_probs(q, kv, h):
    hs = slice(h * XA_HD, (h + 1) * XA_HD)
    s = _dot(q[:, hs], kv[:, hs], _NT) * (XA_HD ** -0.5)
    s = s - jnp.max(s, axis=-1, keepdims=True)
    e = jnp.exp(s)
    return e / jnp.sum(e, axis=-1, keepdims=True)


def _xattn_fwd(proj, kv):
    S = proj.shape[0]

    def fn(q, kv):
        outs = []
        for h in range(XA_HEADS):
            p = _xattn_probs(q, kv, h)
            outs.append(_dot(p, kv[:, XA_W + h * XA_HD:XA_W + (h + 1) * XA_HD]))
        return jnp.concatenate(outs, axis=-1)

    return _rowwise("xattn_fwd", fn, [_Win(proj, XA_W, _XQ_COL)], [kv], [_sds((S, XA_W), BF16)], tm=1024)[0]


def _xattn_bwd(proj, kv, dxa, dmain):
    def fn(q, dxa, kv):
        dqs, dks, dvs = [], [], []
        for h in range(XA_HEADS):
            hs = slice(h * XA_HD, (h + 1) * XA_HD)
            vh = kv[:, XA_W + h * XA_HD:XA_W + (h + 1) * XA_HD]
            p = _xattn_probs(q, kv, h)
            dp = _dot(dxa[:, hs], vh, _NT)
            ds = p * (dp - jnp.sum(p * dp, axis=-1, keepdims=True)) * (XA_HD ** -0.5)
            dqs.append(_dot(ds, kv[:, hs]))
            dks.append(_dot(ds, q[:, hs], _TN))
            dvs.append(_dot(p, dxa[:, hs], _TN))
        return jnp.concatenate(dqs, axis=-1), jnp.concatenate(dks + dvs, axis=-1)

    return _rowwise("xattn_bwd", fn, [_Win(proj, XA_W, _XQ_COL), dxa], [kv], [_Win(dmain, XA_W, _XQ_COL)],
                    [_sds(kv.shape)], tm=1024)


def _mix_fwd(x1, h, mem, W, g_next, on_proj=None, late_weights=None):
    S = x1.shape[0]
    M = mem.shape[0]
    tm = min(512, S)
    def proj_body(h_ref, w_ref, o_ref):
        hh = h_ref[...]
        for j in range(INT_NBLK):
            o_ref[:, INT_BLK * j:INT_BLK * (j + 1)] = _dot(hh, w_ref[INT_BLK * j:INT_BLK * (j + 1), :], _NT).astype(BF16)

    proj = _pcall(
        proj_body, name="mix_proj", grid=(S // tm,),
        in_specs=[pl.BlockSpec((tm, D_MODEL), lambda i: (i, 0)),
                  pl.BlockSpec((INT_W, D_MODEL), lambda i: (0, 0), pipeline_mode=pl.Buffered(1))],
        out_specs=pl.BlockSpec((tm, INT_W), lambda i: (i, 0)), out_shape=_sds((S, INT_W), BF16))(h, W["w_int_t"])
    ya_in, states = _gla_fwd(proj, W["w_fu_pad"], W["b_f"], W["gla_norm_g"],
                             on_proj(proj) if on_proj is not None else [])
    mixed, pool_out = _pool_fwd(proj, W["w_pool"], W["pool_scale"])
    mem_n = _rowwise("mem_norm", lambda m, g: _rms(m, g), [mem], [W["mem_norm_g"]], [_sds((M, D_MODEL), BF16)])[0]
    if late_weights is not None:
        W = {**W, **late_weights([pool_out])}
    kv = _mm_nn("mem_kv", mem_n, W["w_mem_kv"])
    xa = _xattn_fwd(proj, kv)
    def out_fn(gates, ya_in, pool_out, xa, x1, wa, wb, wc, wo, g, gn):
        ya, yb, yc = (_dot(t, w).astype(BF16) for t, w in ((ya_in, wa), (pool_out, wb), (xa, wc)))
        merged = _merge(gates, ya, yb, yc).astype(BF16)
        y = _dot(merged, wo)
        x2 = x1 + _rms(y, g)
        return ya, yb, yc, merged, y, x2, _rms(x2, gn)

    ya, yb, yc, merged, y, x2, h_next = _rowwise(
        "mix_out", out_fn, [_Win(proj, 3 * D_MODEL, 1), ya_in, pool_out, xa, x1],
        [W["w_up_gla"], W["w_up_pool"], W["w_up_xattn"], W["w_o"], W["mix_post_g"], g_next],
        [_sds((S, D_MODEL), BF16)] * 5 + [_sds((S, D_MODEL)), _sds((S, D_MODEL), BF16)])
    return x2, h_next, W, (h, proj, states, ya_in, mixed, pool_out, mem_n, kv, xa, ya, yb, yc, merged, y)


def _mix_bwd(dx2, x1, mem, W, saved, on_grads=None, on_up_grads=None):
    h, proj, states, ya_in, mixed, pool_out, mem_n, kv, xa, ya, yb, yc, merged, y = saved
    S = x1.shape[0]

    def out_bwd(y, dx2, gates, ya, yb, yc, g, wo, wa, wb, wc):
        dy, dg = _rms_bwd(y.astype(F32), g, dx2)
        dy = dy.astype(BF16)
        _, vjp = jax.vjp(_merge, gates, ya, yb, yc)
        dgates, dya, dyb, dyc = vjp(_dot(dy, wo, _NT))
        return (dy, dgates, dya, dyb, dyc, _dot(dya, wa, _NT), _dot(dyb, wb, _NT), _dot(dyc, wc, _NT), dg)

    dmain = lax.empty((S, INT_MAIN), BF16)
    dy, dmain, dya, dyb, dyc, d_ya_in, d_pool_out, d_xa, d_mix_post_g = _rowwise(
        "mix_out_bwd", out_bwd, [y, dx2, _Win(proj, 3 * D_MODEL, 1), ya, yb, yc],
        [W["mix_post_g"], W["w_o"], W["w_up_gla"], W["w_up_pool"], W["w_up_xattn"]],
        [_sds((S, D_MODEL), BF16), _Win(dmain, 3 * D_MODEL, 1)] + [_sds((S, D_MODEL), BF16)] * 4
        + [_sds((S, POOL_W), BF16), _sds((S, XA_W), BF16)], [_sds((1, D_MODEL))])
    d_w_o = _mm_tn("d_w_o", merged, dy, ts=S, tn_a=256)
    d_w_up_gla = _mm_tn("d_w_up_gla", ya_in, dya, ts=S, tn_a=256)
    d_w_up_pool = _mm_tn("d_w_up_pool", pool_out, dyb, ts=S, tn_a=128)
    d_w_up_xattn = _mm_tn("d_w_up_xattn", xa, dyc, ts=S, tn_a=128)
    up_grads = dict(w_up_gla=d_w_up_gla, w_up_pool=d_w_up_pool, w_up_xattn=d_w_up_xattn, w_o=d_w_o)
    up_deps = [on_up_grads(up_grads)] if on_up_grads is not None else []

    dmain, dflow, d_wfu_pad, d_b_f, d_gla_norm_g = _gla_bwd(proj, W["w_fu_pad"], W["b_f"], W["gla_norm_g"], states,
                                                            d_ya_in, dmain)
    dmixed, d_w_pool, d_pool_scale = _pool_lin_bwd(d_pool_out, mixed, W["w_pool"], W["pool_scale"])
    dmain = _pool_win_bwd(dmixed, dmain)
    dmain, dkv = _xattn_bwd(proj, kv, d_xa, dmain)
    d_w_mem_kv = _mm_tn("d_w_mem_kv", mem_n, dkv)
    dmem_n = _mm_nt("d_mem_n", dkv, W["w_mem_kv"])
    d_mem_norm_g = _rowwise("mem_norm_bwd", lambda m, d, g: _rms_bwd(m, g, d)[1], [mem, dmem_n], [W["mem_norm_g"]], [],
                            [_sds((1, D_MODEL))])[0]
    d_w_int_t = (_mm_tn("d_w_int", dmain, h, ts=S, tn_a=INT_MAIN // 8, deps=up_deps),
                 _mm_tn("d_w_flow", dflow, h, ts=S))
    grads = dict(
        w_int_t=d_w_int_t, w_fu_pad=d_wfu_pad, b_f=d_b_f, gla_norm_g=d_gla_norm_g, w_pool=d_w_pool,
        pool_scale=d_pool_scale, mem_norm_g=d_mem_norm_g, w_mem_kv=d_w_mem_kv, w_up_gla=d_w_up_gla,
        w_up_pool=d_w_up_pool, w_up_xattn=d_w_up_xattn, w_o=d_w_o, mix_post_g=d_mix_post_g)
    deps = [on_grads(grads)] if on_grads is not None else []
    dx1, grads["mix_pre_g"] = _dh_pre_norm_bwd("d_mix_h", [dmain, dflow], W["w_int_t"], x1, dx2, W["mix_pre_g"], deps)
    return dx1, grads


def _mesh_pos():
    x, y, c = lax.axis_index("x"), lax.axis_index("y"), lax.axis_index("c")
    return x, y, c, 4 * x + 2 * y + c


def _peer(x, y, c, r):
    px = 1 - x if r & 4 else x
    py = 1 - y if r & 2 else y
    pc = 1 - c if r & 1 else c
    return (px, py, pc), 4 * px + 2 * py + pc


_ALL_PEERS = tuple(range(1, N_DEV))
_SIBLING = 1
_SAME_CORE = (2, 4, 6)


def _dev_slot(ref, dev):
    return ref.at[dev]


def _zero_pad_rows(land):
    pad = FF_PAD - FF_BLK

    def body(land_in, o_ref):
        o_ref[...] = jnp.zeros_like(o_ref)

    return _pcall(body, name="zero_pad_rows", grid=(land.shape[0],), in_specs=[pl.BlockSpec(memory_space=pl.ANY)],
                  out_specs=pl.BlockSpec((None, pad, D_MODEL), lambda j: (j, FF_BLK // pad, 0)),
                  out_shape=_sds(land.shape, land.dtype), aliases={0: 0})(land)


class _Plan:
    def __init__(self, scatter, slots=None, shapes=None):
        self.scatter, self.slots, self.shapes = scatter, slots or {}, shapes or {}

    def src(self, srcs, a, dev):
        return self.slots.get(a, _dev_slot)(srcs[a], dev) if self.scatter else srcs[a]

    def dst(self, lands, a, dev):
        return lands[a].at[dev] if self.scatter else self.slots.get(a, _dev_slot)(lands[a], dev)

    def landing_zones(self, arrays):
        lands = []
        for a, arr in enumerate(arrays):
            if self.scatter:
                lands.append(lax.empty((N_DEV,) + tuple(self.shapes.get(a, arr.shape[1:])), arr.dtype))
            elif a in self.shapes:
                lands.append(_zero_pad_rows(lax.empty(self.shapes[a], arr.dtype)))
            else:
                lands.append(lax.empty((N_DEV,) + arr.shape, arr.dtype))
        return lands


_GATHER = _Plan(False)


def _peer_copies(srcs, lands, send_sems, recv_sems, plan, peers=_ALL_PEERS):
    x, y, c, me = _mesh_pos()
    cps = []
    for r in peers:
        pos, peer = _peer(x, y, c, r)
        for a in range(len(srcs)):
            k = a * (N_DEV - 1) + r - 1
            cps.append(pltpu.make_async_remote_copy(
                src_ref=plan.src(srcs, a, peer), dst_ref=plan.dst(lands, a, me),
                send_sem=send_sems.at[k], recv_sem=recv_sems.at[k], device_id=pos,
                device_id_type=pl.DeviceIdType.MESH))
    return cps


_HBM = pl.BlockSpec(memory_space=pltpu.HBM)
_SEM = pl.BlockSpec(memory_space=pltpu.SEMAPHORE)
_EFFECT = pltpu.SideEffectType.DATAFLOW_SIDE_EFFECTING


def _own_copies(srcs, lands, own_sems, plan):
    me = _mesh_pos()[3]
    return [pltpu.make_async_copy(plan.src(srcs, a, me), plan.dst(lands, a, me), own_sems.at[a])
            for a in range(len(srcs))]


def _exchange_start(name, arrays, plan):
    n = len(arrays)
    lands = plan.landing_zones(arrays)
    n_sem = n * (N_DEV - 1)

    def body(*refs):
        srcs, lands_ = refs[:n], refs[n:2 * n]
        send_sems, recv_sems, own_sems = refs[2 * n:2 * n + 3]
        token = refs[-1]
        for cp in _peer_copies(srcs, lands_, send_sems, recv_sems, plan) + _own_copies(srcs, lands_, own_sems, plan):
            cp.start()
        token[...] = jnp.zeros_like(token)

    hbm = lambda a: pltpu.HBM(a.shape, a.dtype)
    res = pl.pallas_call(
        body, name=name,
        out_shape=(pltpu.SemaphoreType.DMA((n_sem,)), pltpu.SemaphoreType.DMA((n_sem,)), pltpu.SemaphoreType.DMA((n,)),
                   *[hbm(a) for a in arrays], *[hbm(a) for a in lands], _sds((8, 128))),
        in_specs=[_HBM] * (2 * n),
        out_specs=(_SEM, _SEM, _SEM, *[_HBM] * (2 * n), pl.BlockSpec(memory_space=pltpu.VMEM)),
        input_output_aliases={i: 3 + i for i in range(2 * n)},
        compiler_params=pltpu.CompilerParams(has_side_effects=_EFFECT),
    )(*[pltpu.with_memory_space_constraint(a, pltpu.HBM) for a in list(arrays) + lands])
    return (res[:3], res[3:3 + n], res[3 + n:3 + 2 * n], plan), res[-1]


def _exchange_wait(name, state, after):
    sems, srcs, lands, plan = state
    n = len(srcs)

    def body(*refs):
        srcs_, lands_ = refs[:n], refs[n:2 * n]
        send_sems, recv_sems, own_sems = refs[2 * n:2 * n + 3]
        for cp in _peer_copies(srcs_, lands_, send_sems, recv_sems, plan):
            cp.wait_send()
            cp.wait_recv()
        for cp in _own_copies(srcs_, lands_, own_sems, plan):
            cp.wait()

    hbm = lambda a: pltpu.HBM(a.shape, a.dtype)
    res = pl.pallas_call(
        body, name=name, out_shape=tuple(hbm(a) for a in list(srcs) + list(lands)),
        in_specs=[_HBM] * (2 * n) + [_SEM] * 3 + [pl.BlockSpec(memory_space=pl.ANY)] * len(after),
        out_specs=tuple([_HBM] * (2 * n)), input_output_aliases={i: i for i in range(2 * n)},
        compiler_params=pltpu.CompilerParams(has_side_effects=_EFFECT),
    )(*srcs, *lands, *sems, *after)
    return res[n:]


def _gather_start(name, arrays, plan, after):
    n = len(arrays)
    lands = plan.landing_zones(arrays)
    n_sem = n * (N_DEV - 1)

    def body(*refs):
        srcs, lands_ = refs[:n], refs[n:2 * n]
        send_sems, recv_sems, own_sems = refs[2 * n + len(after):2 * n + len(after) + 3]
        token = refs[-1]
        for cp in (_peer_copies(srcs, lands_, send_sems, recv_sems, plan, (_SIBLING,) + _SAME_CORE)
                   + _own_copies(srcs, lands_, own_sems, plan)):
            cp.start()
        token[...] = jnp.zeros_like(token)

    hbm = lambda a: pltpu.HBM(a.shape, a.dtype)
    res = pl.pallas_call(
        body, name=name,
        out_shape=(pltpu.SemaphoreType.DMA((n_sem,)), pltpu.SemaphoreType.DMA((n_sem,)), pltpu.SemaphoreType.DMA((n,)),
                   *[hbm(a) for a in arrays], *[hbm(a) for a in lands], _sds((8, 128))),
        in_specs=[_HBM] * (2 * n) + [pl.BlockSpec(memory_space=pl.ANY)] * len(after),
        out_specs=(_SEM, _SEM, _SEM, *[_HBM] * (2 * n), pl.BlockSpec(memory_space=pltpu.VMEM)),
        input_output_aliases={i: 3 + i for i in range(2 * n)},
        compiler_params=pltpu.CompilerParams(has_side_effects=_EFFECT),
    )(*[pltpu.with_memory_space_constraint(a, pltpu.HBM) for a in list(arrays) + lands], *after)
    return (res[:3], res[3:3 + n], res[3 + n:3 + 2 * n], plan), res[-1]


def _pass_on_copies(lands, send_sems, recv_sems, plan):
    x, y, c, _ = _mesh_pos()
    sibling = _peer(x, y, c, _SIBLING)[0]
    cps = []
    for i, r in enumerate(_SAME_CORE):
        owner = _peer(x, y, c, r)[1]
        for a in range(len(lands)):
            k = a * len(_SAME_CORE) + i
            cps.append(pltpu.make_async_remote_copy(
                src_ref=plan.dst(lands, a, owner), dst_ref=plan.dst(lands, a, owner), send_sem=send_sems.at[k],
                recv_sem=recv_sems.at[k], device_id=sibling, device_id_type=pl.DeviceIdType.MESH))
    return cps


def _gather_pass_on(name, state, after):
    sems, srcs, lands, plan = state
    n = len(srcs)
    n_sem = n * len(_SAME_CORE)

    def body(*refs):
        srcs_, lands_ = refs[:n], refs[n:2 * n]
        send_sems, recv_sems = refs[2 * n], refs[2 * n + 1]
        on_send, on_recv = refs[2 * n + 3 + len(after)], refs[2 * n + 4 + len(after)]
        token = refs[-1]
        arrivals = _peer_copies(srcs_, lands_, send_sems, recv_sems, plan, _SAME_CORE)
        for arrived, on in zip(arrivals, _pass_on_copies(lands_, on_send, on_recv, plan)):
            arrived.wait_recv()
            on.start()
        token[...] = jnp.zeros_like(token)

    hbm = lambda a: pltpu.HBM(a.shape, a.dtype)
    res = pl.pallas_call(
        body, name=name,
        out_shape=(pltpu.SemaphoreType.DMA((n_sem,)), pltpu.SemaphoreType.DMA((n_sem,)),
                   *[hbm(a) for a in list(srcs) + list(lands)], _sds((8, 128))),
        in_specs=[_HBM] * (2 * n) + [_SEM] * 3 + [pl.BlockSpec(memory_space=pl.ANY)] * len(after),
        out_specs=(_SEM, _SEM, *[_HBM] * (2 * n), pl.BlockSpec(memory_space=pltpu.VMEM)),
        input_output_aliases={i: 2 + i for i in range(2 * n)},
        compiler_params=pltpu.CompilerParams(has_side_effects=_EFFECT),
    )(*srcs, *lands, *sems, *after)
    return (sems, res[:2], res[2:2 + n], res[2 + n:2 + 2 * n], plan), res[-1]


def _gather_wait(name, state, after):
    sems, on_sems, srcs, lands, plan = state
    n = len(srcs)

    def body(*refs):
        srcs_, lands_ = refs[:n], refs[n:2 * n]
        send_sems, recv_sems, own_sems, on_send, on_recv = refs[2 * n:2 * n + 5]
        for cp in _peer_copies(srcs_, lands_, send_sems, recv_sems, plan, (_SIBLING,)):
            cp.wait_recv()
        for cp in _peer_copies(srcs_, lands_, send_sems, recv_sems, plan, (_SIBLING,) + _SAME_CORE):
            cp.wait_send()
        for cp in _own_copies(srcs_, lands_, own_sems, plan):
            cp.wait()
        for cp in _pass_on_copies(lands_, on_send, on_recv, plan):
            cp.wait_send()
            cp.wait_recv()

    hbm = lambda a: pltpu.HBM(a.shape, a.dtype)
    res = pl.pallas_call(
        body, name=name, out_shape=tuple(hbm(a) for a in list(srcs) + list(lands)),
        in_specs=[_HBM] * (2 * n) + [_SEM] * 5 + [pl.BlockSpec(memory_space=pl.ANY)] * len(after),
        out_specs=tuple([_HBM] * (2 * n)), input_output_aliases={i: i for i in range(2 * n)},
        compiler_params=pltpu.CompilerParams(has_side_effects=_EFFECT),
    )(*srcs, *lands, *sems, *on_sems, *after)
    return res[n:]


def _sum_parts(name, recv):
    def body(r_ref, o_ref):
        s = r_ref[0]
        for j in range(1, N_DEV):
            s = s + r_ref[j]
        o_ref[...] = s

    return pl.pallas_call(body, name=name, out_shape=_sds(recv.shape[1:], recv.dtype))(recv)


def _adam(name, recv, w, m, v, col=0):
    shape = w.shape
    R, C = shape
    tr, tc = R, C
    while N_DEV * tr * C * 4 > 6 * 1024 * 1024 and tr % 32 == 0:
        tr //= 2
    if N_DEV * tr * C * 4 > 6 * 1024 * 1024 and C % 256 == 0:
        tc = 256
    nc = C // tc

    def body(recv_ref, w_ref, m_ref, v_ref, g_ref, d_ref, m2_ref, v2_ref):
        g = recv_ref[0].astype(F32)
        for j in range(1, N_DEV):
            g = g + recv_ref[j].astype(F32)
        g_ref[...], d_ref[...], m2_ref[...], v2_ref[...] = _adam_math(g, w_ref[...], m_ref[...], v_ref[...])

    blk = pl.BlockSpec((tr, tc), lambda i, j: (i, j))
    return _pcall(body, name=name, grid=(R // tr, nc),
                  in_specs=[pl.BlockSpec((N_DEV, tr, tc), lambda i, j: (0, i, col * nc + j)), blk, blk, blk],
                  out_specs=[blk] * 4, out_shape=[_sds(shape)] * 4)(recv, w, m, v)


def _adam_math(g, w_, m_, v_):
    m2 = ADAM_B1 * m_ + (1.0 - ADAM_B1) * g
    v2 = ADAM_B2 * v_ + (1.0 - ADAM_B2) * (g * g)
    m_hat = m2 / (1.0 - ADAM_B1 ** ADAM_STEP)
    v_hat = v2 / (1.0 - ADAM_B2 ** ADAM_STEP)
    return g, -ADAM_LR * (m_hat / (jnp.sqrt(v_hat) + ADAM_EPS) + ADAM_WD * w_), m2, v2


def _adam_rows(name, recv, ws, ms, vs, offs):
    k = len(ws)

    def body(recv_ref, *refs):
        outs = refs[3 * k:]
        for a in range(k):
            cols = slice(offs[a], offs[a] + ws[a].shape[1])
            g = recv_ref[0, :, cols].astype(F32)
            for j in range(1, N_DEV):
                g = g + recv_ref[j, :, cols].astype(F32)
            res = _adam_math(g, refs[a][...], refs[k + a][...], refs[2 * k + a][...])
            for t in range(4):
                outs[4 * a + t][...] = res[t]

    def whole(a):
        return pl.BlockSpec(a.shape, lambda i: (0,) * a.ndim)

    rows = list(ws) + list(ms) + list(vs)
    return _pcall(body, name=name, grid=(1,), in_specs=[whole(recv)] + [whole(a) for a in rows],
                  out_specs=[whole(w) for w in ws for _ in range(4)],
                  out_shape=[_sds(w.shape) for w in ws for _ in range(4)])(recv, *rows)


_NAMES = ['ffn1_pre_g', 'ffn1_w_in', 'ffn1_w_out', 'ffn1_post_g', 'mix_pre_g', 'w_in', 'w_fu', 'b_f', 'gla_norm_g',
          'w_pool', 'pool_scale', 'mem_norm_g', 'w_mem_kv', 'w_up_gla', 'w_up_pool', 'w_up_xattn', 'w_o', 'mix_post_g',
          'ffn2_pre_g', 'ffn2_w_in', 'ffn2_w_out', 'ffn2_post_g', 'final_g']
_SHARDED = ['ffn1_w_in', 'ffn1_w_out', 'w_in', 'w_fu', 'w_mem_kv', 'w_up_gla', 'w_up_pool', 'w_up_xattn', 'w_o',
            'ffn2_w_in', 'ffn2_w_out']
_COL_SHARDED = ['w_up_pool', 'w_up_xattn']


def _cols_to_full(g):
    return jnp.transpose(g, (1, 0, 2)).reshape(g.shape[1], N_DEV * g.shape[2])


def _full_to_cols(f):
    R, C = f.shape
    return jnp.transpose(f.reshape(R, N_DEV, C // N_DEV), (1, 0, 2))


def _to_internal(w_in_t):
    o = 0
    parts = []
    for s in IN_SPLITS:
        parts.append(w_in_t[o:o + s])
        o += s
    q, k, v, g_out, f_low, p_in, xq, gates = parts
    f_low = jnp.pad(f_low, ((0, FLOW_W - GATE_RANK), (0, 0)))
    return jnp.concatenate([q, k, v, g_out, gates, p_in, xq, f_low], axis=0)


def _from_internal(d, d_flow):
    q, k, v, g_out = d[0:512], d[512:1024], d[1024:2048], d[2048:3072]
    gates, p_in, xq = d[3072:6144], d[6144:6656], d[6656:7168]
    return jnp.concatenate([q, k, v, g_out, d_flow[:GATE_RANK], p_in, xq, gates], axis=0)


_FFN_IN = ('ffn1_w_in', 'ffn2_w_in')
_FFN_OUT = ('ffn1_w_out', 'ffn2_w_out')
_GATHERS = {"ffn1_in": ['ffn1_w_in'], "ffn1_out": ['ffn1_w_out'],
            "mix_in": ['w_in', 'w_fu'], "mix_rest": ['w_mem_kv', 'w_up_gla', 'w_up_pool', 'w_up_xattn', 'w_o'],
            "ffn2": ['ffn2_w_in', 'ffn2_w_out']}
_MIX = _GATHERS["mix_in"] + _GATHERS["mix_rest"]
_MIX_UP = ['w_up_gla', 'w_up_pool', 'w_up_xattn', 'w_o']
_MIX_LATE = [n for n in _MIX if n not in _MIX_UP]


def _ffn_in_slot(ref, d):
    return ref.at[2 * (d % N_FF_BLK) + d // N_FF_BLK, pl.ds(0, FF_BLK)]


def _ffn_out_slot(ref, d):
    rows = FF_BLK // 2
    return ref.at[d // 2, pl.ds(pl.multiple_of((d % 2) * rows, rows), rows)]


def _ffn_plan(names, scatter):
    slots, shapes = {}, {}
    for a, n in enumerate(names):
        if n in _FFN_IN:
            slots[a] = _ffn_in_slot
            shapes[a] = (FF_BLK, D_MODEL) if scatter else (N_DEV, FF_PAD, D_MODEL)
        elif n in _FFN_OUT:
            slots[a] = _ffn_out_slot
            shapes[a] = (FF_BLK // 2, D_MODEL) if scatter else (N_FF_BLK, FF_PAD, D_MODEL)
    return _Plan(scatter, slots, shapes)


def _rows_to_full(g):
    return g.reshape(N_DEV * g.shape[1], g.shape[2])


def _mix_weights(gathered):
    W = {}
    for n, g in gathered.items():
        if n == "w_in":
            W["w_int_t"] = _to_internal(g.reshape(IN_WIDTH, D_MODEL))
        elif n == "w_fu":
            W["w_fu_pad"] = jnp.pad(_cols_to_full(g), ((0, FLOW_W - GATE_RANK), (0, 0)))
        else:
            W[n] = _cols_to_full(g) if n in _COL_SHARDED else _rows_to_full(g)
    return W


def _mix_chunks(G, n):
    if n == "w_in":
        return _from_internal(*G["w_int_t"]).reshape(N_DEV, IN_SHARD, D_MODEL)
    if n == "w_fu":
        return _full_to_cols(G["w_fu_pad"][:GATE_RANK].astype(BF16))
    if n in _COL_SHARDED:
        return _full_to_cols(G[n])
    return G[n].reshape(N_DEV, G[n].shape[0] // N_DEV, G[n].shape[1])


def _step(x, mem, tgt, P, Mo, Vo):
    def native(n, a):
        return jnp.swapaxes(a, 0, 1) if n in _FFN_IN + ("w_in",) else a

    P, Mo, Vo = ({n: native(n, a) for n, a in d.items()} for d in (P, Mo, Vo))
    small = {n: P[n] for n in _NAMES if n not in _SHARDED}

    gather, tokens = {}, []
    for grp, names in _GATHERS.items():
        gather[grp], tok = _gather_start("gather_" + grp, [P[n].astype(BF16) for n in names], _ffn_plan(names, False),
                                         tokens[-1:])
        tokens.append(tok)

    def pass_on(grp, after):
        return _gather_pass_on("gather_" + grp + "_on", gather[grp], after)

    def gathered(grp, after):
        return _gather_wait("gather_" + grp + "_wait", pass_on(grp, after)[0], [])

    def ffn_wo(lands):
        return lambda act: lands[0].reshape(N_FF_BLK * FF_PAD, D_MODEL)

    h1 = _pre_norm("ffn1_pre", x, small["ffn1_pre_g"], tokens[:1])
    w1t = gathered("ffn1_in", tokens[-1:] + [h1])[0].reshape(N_DEV * FF_PAD, D_MODEL)
    (x1, hm), w1o, sv1 = _ffn_fwd("ffn1", x, h1, w1t, small["ffn1_post_g"],
                                  lambda act: ffn_wo(gathered("ffn1_out", [act]))(act), small["mix_pre_g"])

    Wm = {**small, **_mix_weights(dict(zip(_GATHERS["mix_in"], gathered("mix_in", [x1]))))}
    passed = {}

    def pass_on_later(proj):
        tokens = []
        for grp in ("mix_rest", "ffn2"):
            passed[grp], tok = pass_on(grp, [proj])
            tokens.append(tok)
        return tokens

    def mix_rest(after):
        names = _GATHERS["mix_rest"]
        return _mix_weights(dict(zip(names, _gather_wait("gather_mix_rest_wait", passed["mix_rest"], after))))

    x2, h2, Wm, svm = _mix_fwd(x1, hm, mem, Wm, small["ffn2_pre_g"], pass_on_later, mix_rest)
    w2t, w2o = _gather_wait("gather_ffn2_wait", passed["ffn2"], [x2])
    w2t = w2t.reshape(N_DEV * FF_PAD, D_MODEL)
    (loss, dx3, d_final_g), w2o, sv2 = _ffn_fwd("ffn2", x2, h2, w2t, small["ffn2_post_g"], ffn_wo([w2o]),
                                                small["final_g"], tgt)

    G = dict(final_g=d_final_g)
    scat = {}

    def start(grp, names, arrays):
        scat[grp] = names, _exchange_start("scatter_" + grp, arrays, _ffn_plan(names, True))
        return scat[grp][1][1]

    def ffn_starts(tag):
        return (lambda dwo: start(tag + "_out", [tag + "_w_out"], [dwo.reshape(N_FF_BLK, FF_PAD, D_MODEL)]),
                lambda dwt: start(tag + "_in", [tag + "_w_in"], [dwt.reshape(N_DEV, FF_PAD, D_MODEL)]))

    dx2, G["ffn2_pre_g"], G["ffn2_post_g"] = _ffn_bwd(
        "ffn2", dx3, x2, small["ffn2_pre_g"], w2t, w2o, small["ffn2_post_g"], sv2, *ffn_starts("ffn2"))
    dx1, Gm = _mix_bwd(dx2, x1, mem, Wm, svm,
                       lambda Gm: start("mix", _MIX_LATE, [_mix_chunks(Gm, n) for n in _MIX_LATE]),
                       lambda Gm: start("mix_up", _MIX_UP, [_mix_chunks(Gm, n) for n in _MIX_UP]))
    G.update(Gm)
    dx, G["ffn1_pre_g"], G["ffn1_post_g"] = _ffn_bwd(
        "ffn1", dx1, x, small["ffn1_pre_g"], w1t, w1o, small["ffn1_post_g"], sv1, *ffn_starts("ffn1"))

    vecs = sorted((n for n in small if P[n].ndim == 2), key=lambda n: -P[n].shape[1])
    packed = jnp.concatenate([G[n] for n in vecs] + [loss], axis=1)
    place, o = {}, 0
    for n in vecs:
        place[n] = o // P[n].shape[1]
        o += P[n].shape[1]
    scat["small"] = ["packed", "w_pool"], _exchange_start("gather_small_grads", [packed, G["w_pool"]], _GATHER)
    recv, outs = {}, {}
    done = [dx]
    for grp in ["ffn2_out", "ffn2_in", "mix_up", "mix", "ffn1_out", "ffn1_in", "small"]:
        names, (state, _) = scat[grp]
        recv.update(zip(names, _exchange_wait(("gather_" if grp == "small" else "scatter_") + grp + "_wait", state,
                                              done)))
        if grp == "small":
            loss = _sum_parts("loss_sum", recv["packed"][:, :, o:])[0, 0]
            names = [n for n in small if n not in place]
            res = _adam_rows("adam_rows", recv["packed"], [P[n] for n in vecs], [Mo[n] for n in vecs],
                             [Vo[n] for n in vecs], [place[n] * P[n].shape[1] for n in vecs])
            for a, n in enumerate(vecs):
                outs[n] = list(res[4 * a:4 * a + 4])
            done.append(res[-1])
        for n in names:
            shp = P[n].shape
            shp2 = shp if len(shp) == 2 else (shp[0] * shp[1], shp[2])
            r = recv["packed"] if n in place else recv[n].reshape((N_DEV,) + shp2)
            res = _adam("adam_" + n, r, P[n].reshape(shp2), Mo[n].reshape(shp2), Vo[n].reshape(shp2), place.get(n, 0))
            outs[n] = [native(n, t.reshape(shp)) for t in res]
            done.append(res[-1])
    return loss, dx, outs


def kernel(x, mem, ffn1_pre_g, ffn1_w_in, ffn1_w_out, ffn1_post_g, mix_pre_g, w_in, w_fu, b_f, gla_norm_g, w_pool, pool_scale, mem_norm_g, w_mem_kv, w_up_gla, w_up_pool, w_up_xattn, w_o, mix_post_g, ffn2_pre_g, ffn2_w_in, ffn2_w_out, ffn2_post_g, final_g, loss_target, m_ffn1_pre_g, m_ffn1_w_in, m_ffn1_w_out, m_ffn1_post_g, m_mix_pre_g, m_w_in, m_w_fu, m_b_f, m_gla_norm_g, m_w_pool, m_pool_scale, m_mem_norm_g, m_w_mem_kv, m_w_up_gla, m_w_up_pool, m_w_up_xattn, m_w_o, m_mix_post_g, m_ffn2_pre_g, m_ffn2_w_in, m_ffn2_w_out, m_ffn2_post_g, m_final_g, v_ffn1_pre_g, v_ffn1_w_in, v_ffn1_w_out, v_ffn1_post_g, v_mix_pre_g, v_w_in, v_w_fu, v_b_f, v_gla_norm_g, v_w_pool, v_pool_scale, v_mem_norm_g, v_w_mem_kv, v_w_up_gla, v_w_up_pool, v_w_up_xattn, v_w_o, v_mix_post_g, v_ffn2_pre_g, v_ffn2_w_in, v_ffn2_w_out, v_ffn2_post_g, v_final_g):
    params = [ffn1_pre_g, ffn1_w_in, ffn1_w_out, ffn1_post_g, mix_pre_g, w_in, w_fu, b_f, gla_norm_g, w_pool, pool_scale, mem_norm_g, w_mem_kv, w_up_gla, w_up_pool, w_up_xattn, w_o, mix_post_g, ffn2_pre_g, ffn2_w_in, ffn2_w_out, ffn2_post_g, final_g]
    moms = [m_ffn1_pre_g, m_ffn1_w_in, m_ffn1_w_out, m_ffn1_post_g, m_mix_pre_g, m_w_in, m_w_fu, m_b_f, m_gla_norm_g, m_w_pool, m_pool_scale, m_mem_norm_g, m_w_mem_kv, m_w_up_gla, m_w_up_pool, m_w_up_xattn, m_w_o, m_mix_post_g, m_ffn2_pre_g, m_ffn2_w_in, m_ffn2_w_out, m_ffn2_post_g, m_final_g]
    vars_ = [v_ffn1_pre_g, v_ffn1_w_in, v_ffn1_w_out, v_ffn1_post_g, v_mix_pre_g, v_w_in, v_w_fu, v_b_f, v_gla_norm_g, v_w_pool, v_pool_scale, v_mem_norm_g, v_w_mem_kv, v_w_up_gla, v_w_up_pool, v_w_up_xattn, v_w_o, v_mix_post_g, v_ffn2_pre_g, v_ffn2_w_in, v_ffn2_w_out, v_ffn2_post_g, v_final_g]
    P = {n: a[0] if a.ndim > 2 else a for n, a in zip(_NAMES, params)}
    Mo = {n: a[0] if a.ndim > 2 else a for n, a in zip(_NAMES, moms)}
    Vo = {n: a[0] if a.ndim > 2 else a for n, a in zip(_NAMES, vars_)}
    loss, dx, outs = _step(x[0], mem[0], loss_target[0], P, Mo, Vo)
    out = [loss, dx[None]]
    for kind in range(4):
        for n, p in zip(_NAMES, params):
            out.append(outs[n][kind].reshape(p.shape))
    return tuple(out)
```
